```python
import math
import jax, jax.numpy as jnp
from jax import lax
import numpy as np

D_MODEL = 1024
BATCH = 8
SEQ = 8192
DEPTH = 1

D_MIX = D_MODEL
DIL_HEAD_DIM = 64
DIL_HEADS = (D_MIX // 2) // DIL_HEAD_DIM
DIL_WIDTH = DIL_HEADS * DIL_HEAD_DIM
DIL_BRANCHES = ((128, 1), (512, 4), (2048, 16))
MLA_NOPE = 128
MLA_ROPE = 64
MLA_QK_DIM = MLA_NOPE + MLA_ROPE
MLA_V_DIM = 128
MLA_HEADS = (D_MIX - DIL_WIDTH) // MLA_V_DIM
MLA_WIDTH = MLA_HEADS * MLA_V_DIM
MLA_Q_RANK = 256
MLA_KV_RANK = 128
ROPE_BASE = 10000.0
REL_BUCKETS = 32
REL_MAX_DIST = 2048
D_FF = 2816
FFN_RESID = 0.5
Q_BLOCK = 128
EPS = 1e-6
IN_COLS = 3 * DIL_WIDTH + MLA_Q_RANK + MLA_KV_RANK + MLA_ROPE

kernel_name = "hymba_dilated_mla_macaron"


def _rms(x, g):
    xf = x.astype(jnp.float32)
    y = xf * lax.rsqrt(jnp.mean(xf * xf, axis=-1, keepdims=True) + EPS)
    return (y * g.astype(jnp.float32)).astype(x.dtype)


def _swiglu(x, g, w_gate, w_up, w_down):
    h = _rms(x, g)
    return (jax.nn.silu(h @ w_gate) * (h @ w_up)) @ w_down


def _t5_bucket(dist):
    max_exact = REL_BUCKETS // 2
    d = np.maximum(dist, 1).astype(np.float32)
    large = max_exact + (np.log(d / max_exact) / np.log(REL_MAX_DIST / max_exact)
                         * (REL_BUCKETS - max_exact)).astype(np.int32)
    large = np.minimum(large, REL_BUCKETS - 1)
    return np.where(dist < max_exact, dist, large).astype(np.int32)


def _dilated_branch(q, k, v, rel_bias, window, dilation):
    B, S, H, hd = q.shape
    L = S // dilation
    W = window // dilation
    Bq = math.gcd(L, Q_BLOCK)
    nb = L // Bq

    def to_sub(t):
        return t.reshape(B, L, dilation, H, hd).transpose(0, 2, 3, 1, 4)

    qs = to_sub(q).reshape(B, dilation, H, nb, Bq, hd)
    pad = ((0, 0), (0, 0), (0, 0), (W, 0), (0, 0))
    ks = jnp.pad(to_sub(k), pad)
    vs = jnp.pad(to_sub(v), pad)
    idx = np.arange(nb)[:, None] * Bq + np.arange(Bq + W)[None, :]
    kb = ks[:, :, :, idx]
    vb = vs[:, :, :, idx]
    logits = jnp.einsum('brhnqc,brhnkc->brhnqk', qs, kb) * (hd ** -0.5)

    i = np.arange(Bq)[:, None]
    j = np.arange(Bq + W)[None, :]
    delta = i + W - j
    key_sub = idx[:, None, :] - W
    valid = (delta >= 0) & (delta <= W) & (key_sub >= 0)
    bucket = _t5_bucket(np.clip(delta, 0, None) * dilation)
    bias = jnp.take(rel_bias.astype(jnp.float32), jnp.asarray(bucket), axis=1)
    logits = logits + bias[None, None, :, None]
    logits = jnp.where(jnp.asarray(valid)[None, None, None], logits, -jnp.inf)
    lse = jax.nn.logsumexp(logits, axis=-1)
    p = jnp.exp(logits - lse[..., None])
    o = jnp.einsum('brhnqk,brhnkc->brhnqc', p, vb)
    o = o.reshape(B, dilation, H, L, hd).transpose(0, 3, 1, 2, 4).reshape(B, S, H, hd)
    lse = lse.reshape(B, dilation, H, L).transpose(0, 3, 1, 2).reshape(B, S, H)
    return o, lse


def _dilated_attention(q, k, v, q_g, k_g, rel_bias):
    B, S = q.shape[:2]
    sh = (B, S, DIL_HEADS, DIL_HEAD_DIM)
    q = _rms(q.reshape(sh).astype(jnp.float32), q_g)
    k = _rms(k.reshape(sh).astype(jnp.float32), k_g)
    v = v.reshape(sh).astype(jnp.float32)
    outs, lses = [], []
    for window, dilation in DIL_BRANCHES:
        o, lse = _dilated_branch(q, k, v, rel_bias, window, dilation)
        outs.append(o)
        lses.append(lse)
    alpha = jax.nn.softmax(jnp.stack(lses, 0), axis=0)
    o = jnp.sum(alpha[..., None] * jnp.stack(outs, 0), axis=0)
    return o.reshape(B, S, DIL_WIDTH)


def _rope(x, pos):
    dim = x.shape[-1]
    inv_freq = ROPE_BASE ** (-jnp.arange(0, dim, 2, dtype=jnp.float32) / dim)
    ang = pos[:, None] * inv_freq[None, :]
    cos = jnp.cos(ang)[None, :, None, :]
    sin = jnp.sin(ang)[None, :, None, :]
    x1, x2 = x[..., : dim // 2], x[..., dim // 2:]
    return jnp.concatenate([x1 * cos - x2 * sin, x2 * cos + x1 * sin], axis=-1)


def _mla(cq, ckv, k_pe, q_a_norm, w_q_b, kv_a_norm, w_kv_b, q_g, k_g):
    B, S = cq.shape[:2]
    H = MLA_HEADS
    pos = jnp.arange(S, dtype=jnp.float32)
    q = (_rms(cq, q_a_norm) @ w_q_b).reshape(B, S, H, MLA_QK_DIM).astype(jnp.float32)
    kv = (_rms(ckv, kv_a_norm) @ w_kv_b).reshape(B, S, H, MLA_NOPE + MLA_V_DIM).astype(jnp.float32)
    k_nope, v = kv[..., :MLA_NOPE], kv[..., MLA_NOPE:]
    k_pe = jnp.broadcast_to(k_pe.astype(jnp.float32)[:, :, None, :], (B, S, H, MLA_ROPE))
    k = jnp.concatenate([k_nope, k_pe], axis=-1)
    q = _rms(q, q_g)
    k = _rms(k, k_g)
    q = jnp.concatenate([q[..., :MLA_NOPE], _rope(q[..., MLA_NOPE:], pos)], axis=-1)
    k = jnp.concatenate([k[..., :MLA_NOPE], _rope(k[..., MLA_NOPE:], pos)], axis=-1)

    nb = S // Q_BLOCK
    qb = (q * MLA_QK_DIM ** -0.5).transpose(0, 2, 1, 3).reshape(B, H, nb, Q_BLOCK, MLA_QK_DIM)
    qb = qb.transpose(2, 0, 1, 3, 4)
    kt = k.transpose(0, 2, 1, 3)
    vt = v.transpose(0, 2, 1, 3)
    key_pos = jnp.arange(S)

    def block(args):
        q_blk, n = args
        logits = jnp.einsum('bhqc,bhkc->bhqk', q_blk, kt)
        q_pos = n * Q_BLOCK + jnp.arange(Q_BLOCK)
        mask = key_pos[None, :] <= q_pos[:, None]
        p = jax.nn.softmax(jnp.where(mask, logits, -jnp.inf), axis=-1)
        return jnp.einsum('bhqk,bhkc->bhqc', p, vt)

    o = lax.map(block, (qb, jnp.arange(nb)))
    return o.transpose(1, 0, 3, 2, 4).reshape(B, S, MLA_WIDTH)


def _fwd_setup_inputs(seed: int = 0) -> dict:
    key = jax.random.key(seed)
    ks = jax.random.split(key, 24)
    f32 = jnp.float32

    def w(k, shape, fan_in):
        return jax.random.normal(k, (DEPTH,) + shape, f32) * fan_in ** -0.5

    def g(k, dim):
        return 1.0 + 0.02 * jax.random.normal(k, (DEPTH, dim), f32)

    return {
        "x": jax.random.normal(ks[0], (BATCH, SEQ, D_MODEL), f32),
        "ffn1_norm": g(ks[1], D_MODEL),
        "ffn1_w_gate": w(ks[2], (D_MODEL, D_FF), D_MODEL),
        "ffn1_w_up": w(ks[3], (D_MODEL, D_FF), D_MODEL),
        "ffn1_w_down": w(ks[4], (D_FF, D_MODEL), D_FF),
        "mix_norm": g(ks[5], D_MODEL),
        "w_in": w(ks[6], (D_MODEL, IN_COLS), D_MODEL),
        "dil_q_norm": g(ks[7], DIL_HEAD_DIM),
        "dil_k_norm": g(ks[8], DIL_HEAD_DIM),
        "rel_bias": 0.2 * jax.random.normal(ks[9], (DIL_HEADS, REL_BUCKETS), f32),
        "mla_q_a_norm": g(ks[10], MLA_Q_RANK),
        "mla_w_q_b": w(ks[11], (MLA_Q_RANK, MLA_HEADS * MLA_QK_DIM), MLA_Q_RANK),
        "mla_kv_a_norm": g(ks[12], MLA_KV_RANK),
        "mla_w_kv_b": w(ks[13], (MLA_KV_RANK, MLA_HEADS * (MLA_NOPE + MLA_V_DIM)), MLA_KV_RANK),
        "mla_q_norm": g(ks[14], MLA_QK_DIM),
        "mla_k_norm": g(ks[15], MLA_QK_DIM),
        "out_norm_dil": g(ks[16], DIL_WIDTH),
        "out_norm_mla": g(ks[17], MLA_WIDTH),
        "w_out": w(ks[18], (D_MIX, D_MODEL), D_MIX),
        "ffn2_norm": g(ks[19], D_MODEL),
        "ffn2_w_gate": w(ks[20], (D_MODEL, D_FF), D_MODEL),
        "ffn2_w_up": w(ks[21], (D_MODEL, D_FF), D_MODEL),
        "ffn2_w_down": w(ks[22], (D_FF, D_MODEL), D_FF),
    }


def _fwd_reference(x, ffn1_norm, ffn1_w_gate, ffn1_w_up, ffn1_w_down, mix_norm, w_in,
              dil_q_norm, dil_k_norm, rel_bias, mla_q_a_norm, mla_w_q_b, mla_kv_a_norm,
              mla_w_kv_b, mla_q_norm, mla_k_norm, out_norm_dil, out_norm_mla, w_out,
              ffn2_norm, ffn2_w_gate, ffn2_w_up, ffn2_w_down):
    splits = np.cumsum([DIL_WIDTH, DIL_WIDTH, DIL_WIDTH, MLA_Q_RANK, MLA_KV_RANK])
    for l in range(DEPTH):
        x = x + FFN_RESID * _swiglu(x, ffn1_norm[l], ffn1_w_gate[l], ffn1_w_up[l], ffn1_w_down[l])
        h = _rms(x, mix_norm[l])
        proj = h @ w_in[l]
        q_a, k_a, v_a, cq, ckv, k_pe = jnp.split(proj, splits, axis=-1)
        o_dil = _dilated_attention(q_a, k_a, v_a, dil_q_norm[l], dil_k_norm[l], rel_bias)
        o_mla = _mla(cq, ckv, k_pe, mla_q_a_norm[l], mla_w_q_b[l], mla_kv_a_norm[l],
                     mla_w_kv_b[l], mla_q_norm[l], mla_k_norm[l])
        o = jnp.concatenate([_rms(o_dil, out_norm_dil[l]), _rms(o_mla, out_norm_mla[l])], axis=-1)
        x = x + o.astype(x.dtype) @ w_out[l]
        x = x + FFN_RESID * _swiglu(x, ffn2_norm[l], ffn2_w_gate[l], ffn2_w_up[l], ffn2_w_down[l])
    return x


import jax as _jax
import jax.numpy as _jnp

TWIN_FORMAT = 'train_step'
FWD_PARAMS = ['x', 'ffn1_norm', 'ffn1_w_gate', 'ffn1_w_up', 'ffn1_w_down', 'mix_norm', 'w_in', 'dil_q_norm', 'dil_k_norm', 'rel_bias', 'mla_q_a_norm', 'mla_w_q_b', 'mla_kv_a_norm', 'mla_w_kv_b', 'mla_q_norm', 'mla_k_norm', 'out_norm_dil', 'out_norm_mla', 'w_out', 'ffn2_norm', 'ffn2_w_gate', 'ffn2_w_up', 'ffn2_w_down']
TWIN_WEIGHTS = ['ffn1_norm', 'ffn1_w_gate', 'ffn1_w_up', 'ffn1_w_down', 'mix_norm', 'w_in', 'dil_q_norm', 'dil_k_norm', 'rel_bias', 'mla_q_a_norm', 'mla_w_q_b', 'mla_kv_a_norm', 'mla_w_kv_b', 'mla_q_norm', 'mla_k_norm', 'out_norm_dil', 'out_norm_mla', 'w_out', 'ffn2_norm', 'ffn2_w_gate', 'ffn2_w_up', 'ffn2_w_down']
TWIN_DIFF_INPUT = 'x'
TWIN_INPUTS = ['x', 'ffn1_norm', 'ffn1_w_gate', 'ffn1_w_up', 'ffn1_w_down', 'mix_norm', 'w_in', 'dil_q_norm', 'dil_k_norm', 'rel_bias', 'mla_q_a_norm', 'mla_w_q_b', 'mla_kv_a_norm', 'mla_w_kv_b', 'mla_q_norm', 'mla_k_norm', 'out_norm_dil', 'out_norm_mla', 'w_out', 'ffn2_norm', 'ffn2_w_gate', 'ffn2_w_up', 'ffn2_w_down', 'loss_target', 'm_ffn1_norm', 'm_ffn1_w_gate', 'm_ffn1_w_up', 'm_ffn1_w_down', 'm_mix_norm', 'm_w_in', 'm_dil_q_norm', 'm_dil_k_norm', 'm_rel_bias', 'm_mla_q_a_norm', 'm_mla_w_q_b', 'm_mla_kv_a_norm', 'm_mla_w_kv_b', 'm_mla_q_norm', 'm_mla_k_norm', 'm_out_norm_dil', 'm_out_norm_mla', 'm_w_out', 'm_ffn2_norm', 'm_ffn2_w_gate', 'm_ffn2_w_up', 'm_ffn2_w_down', 'v_ffn1_norm', 'v_ffn1_w_gate', 'v_ffn1_w_up', 'v_ffn1_w_down', 'v_mix_norm', 'v_w_in', 'v_dil_q_norm', 'v_dil_k_norm', 'v_rel_bias', 'v_mla_q_a_norm', 'v_mla_w_q_b', 'v_mla_kv_a_norm', 'v_mla_w_kv_b', 'v_mla_q_norm', 'v_mla_k_norm', 'v_out_norm_dil', 'v_out_norm_mla', 'v_w_out', 'v_ffn2_norm', 'v_ffn2_w_gate', 'v_ffn2_w_up', 'v_ffn2_w_down']
TWIN_OUTPUTS = ['loss', 'grad_x', 'grad_ffn1_norm', 'grad_ffn1_w_gate', 'grad_ffn1_w_up', 'grad_ffn1_w_down', 'grad_mix_norm', 'grad_w_in', 'grad_dil_q_norm', 'grad_dil_k_norm', 'grad_rel_bias', 'grad_mla_q_a_norm', 'grad_mla_w_q_b', 'grad_mla_kv_a_norm', 'grad_mla_w_kv_b', 'grad_mla_q_norm', 'grad_mla_k_norm', 'grad_out_norm_dil', 'grad_out_norm_mla', 'grad_w_out', 'grad_ffn2_norm', 'grad_ffn2_w_gate', 'grad_ffn2_w_up', 'grad_ffn2_w_down', 'delta_ffn1_norm', 'delta_ffn1_w_gate', 'delta_ffn1_w_up', 'delta_ffn1_w_down', 'delta_mix_norm', 'delta_w_in', 'delta_dil_q_norm', 'delta_dil_k_norm', 'delta_rel_bias', 'delta_mla_q_a_norm', 'delta_mla_w_q_b', 'delta_mla_kv_a_norm', 'delta_mla_w_kv_b', 'delta_mla_q_norm', 'delta_mla_k_norm', 'delta_out_norm_dil', 'delta_out_norm_mla', 'delta_w_out', 'delta_ffn2_norm', 'delta_ffn2_w_gate', 'delta_ffn2_w_up', 'delta_ffn2_w_down', 'new_m_ffn1_norm', 'new_m_ffn1_w_gate', 'new_m_ffn1_w_up', 'new_m_ffn1_w_down', 'new_m_mix_norm', 'new_m_w_in', 'new_m_dil_q_norm', 'new_m_dil_k_norm', 'new_m_rel_bias', 'new_m_mla_q_a_norm', 'new_m_mla_w_q_b', 'new_m_mla_kv_a_norm', 'new_m_mla_w_kv_b', 'new_m_mla_q_norm', 'new_m_mla_k_norm', 'new_m_out_norm_dil', 'new_m_out_norm_mla', 'new_m_w_out', 'new_m_ffn2_norm', 'new_m_ffn2_w_gate', 'new_m_ffn2_w_up', 'new_m_ffn2_w_down', 'new_v_ffn1_norm', 'new_v_ffn1_w_gate', 'new_v_ffn1_w_up', 'new_v_ffn1_w_down', 'new_v_mix_norm', 'new_v_w_in', 'new_v_dil_q_norm', 'new_v_dil_k_norm', 'new_v_rel_bias', 'new_v_mla_q_a_norm', 'new_v_mla_w_q_b', 'new_v_mla_kv_a_norm', 'new_v_mla_w_kv_b', 'new_v_mla_q_norm', 'new_v_mla_k_norm', 'new_v_out_norm_dil', 'new_v_out_norm_mla', 'new_v_w_out', 'new_v_ffn2_norm', 'new_v_ffn2_w_gate', 'new_v_ffn2_w_up', 'new_v_ffn2_w_down']
TWIN_LEAF_KINDS = {'loss': 'loss', 'grad_x': 'grad_x', 'grad_ffn1_norm': 'grad_w', 'grad_ffn1_w_gate': 'grad_w', 'grad_ffn1_w_up': 'grad_w', 'grad_ffn1_w_down': 'grad_w', 'grad_mix_norm': 'grad_w', 'grad_w_in': 'grad_w', 'grad_dil_q_norm': 'grad_w', 'grad_dil_k_norm': 'grad_w', 'grad_rel_bias': 'grad_w', 'grad_mla_q_a_norm': 'grad_w', 'grad_mla_w_q_b': 'grad_w', 'grad_mla_kv_a_norm': 'grad_w', 'grad_mla_w_kv_b': 'grad_w', 'grad_mla_q_norm': 'grad_w', 'grad_mla_k_norm': 'grad_w', 'grad_out_norm_dil': 'grad_w', 'grad_out_norm_mla': 'grad_w', 'grad_w_out': 'grad_w', 'grad_ffn2_norm': 'grad_w', 'grad_ffn2_w_gate': 'grad_w', 'grad_ffn2_w_up': 'grad_w', 'grad_ffn2_w_down': 'grad_w', 'delta_ffn1_norm': 'delta_w', 'delta_ffn1_w_gate': 'delta_w', 'delta_ffn1_w_up': 'delta_w', 'delta_ffn1_w_down': 'delta_w', 'delta_mix_norm': 'delta_w', 'delta_w_in': 'delta_w', 'delta_dil_q_norm': 'delta_w', 'delta_dil_k_norm': 'delta_w', 'delta_rel_bias': 'delta_w', 'delta_mla_q_a_norm': 'delta_w', 'delta_mla_w_q_b': 'delta_w', 'delta_mla_kv_a_norm': 'delta_w', 'delta_mla_w_kv_b': 'delta_w', 'delta_mla_q_norm': 'delta_w', 'delta_mla_k_norm': 'delta_w', 'delta_out_norm_dil': 'delta_w', 'delta_out_norm_mla': 'delta_w', 'delta_w_out': 'delta_w', 'delta_ffn2_norm': 'delta_w', 'delta_ffn2_w_gate': 'delta_w', 'delta_ffn2_w_up': 'delta_w', 'delta_ffn2_w_down': 'delta_w', 'new_m_ffn1_norm': 'new_m', 'new_m_ffn1_w_gate': 'new_m', 'new_m_ffn1_w_up': 'new_m', 'new_m_ffn1_w_down': 'new_m', 'new_m_mix_norm': 'new_m', 'new_m_w_in': 'new_m', 'new_m_dil_q_norm': 'new_m', 'new_m_dil_k_norm': 'new_m', 'new_m_rel_bias': 'new_m', 'new_m_mla_q_a_norm': 'new_m', 'new_m_mla_w_q_b': 'new_m', 'new_m_mla_kv_a_norm': 'new_m', 'new_m_mla_w_kv_b': 'new_m', 'new_m_mla_q_norm': 'new_m', 'new_m_mla_k_norm': 'new_m', 'new_m_out_norm_dil': 'new_m', 'new_m_out_norm_mla': 'new_m', 'new_m_w_out': 'new_m', 'new_m_ffn2_norm': 'new_m', 'new_m_ffn2_w_gate': 'new_m', 'new_m_ffn2_w_up': 'new_m', 'new_m_ffn2_w_down': 'new_m', 'new_v_ffn1_norm': 'new_v', 'new_v_ffn1_w_gate': 'new_v', 'new_v_ffn1_w_up': 'new_v', 'new_v_ffn1_w_down': 'new_v', 'new_v_mix_norm': 'new_v', 'new_v_w_in': 'new_v', 'new_v_dil_q_norm': 'new_v', 'new_v_dil_k_norm': 'new_v', 'new_v_rel_bias': 'new_v', 'new_v_mla_q_a_norm': 'new_v', 'new_v_mla_w_q_b': 'new_v', 'new_v_mla_kv_a_norm': 'new_v', 'new_v_mla_w_kv_b': 'new_v', 'new_v_mla_q_norm': 'new_v', 'new_v_mla_k_norm': 'new_v', 'new_v_out_norm_dil': 'new_v', 'new_v_out_norm_mla': 'new_v', 'new_v_w_out': 'new_v', 'new_v_ffn2_norm': 'new_v', 'new_v_ffn2_w_gate': 'new_v', 'new_v_ffn2_w_up': 'new_v', 'new_v_ffn2_w_down': 'new_v'}


def _forward(args):
    return _fwd_reference(*[args[k] for k in FWD_PARAMS])


def _output_shape():
    def fwd():
        inp = _fwd_setup_inputs(0)
        return _fwd_reference(*[inp[k] for k in FWD_PARAMS])
    out = _jax.eval_shape(fwd)
    return out.shape, out.dtype

N_MICROBATCH = 1
ADAM_LR = 0.001
ADAM_B1 = 0.9
ADAM_B2 = 0.999
ADAM_EPS = 1e-08
ADAM_WD = 0.01
ADAM_STEP = 10
PER_EXAMPLE_BATCH_AXIS = {'x': 0, 'loss_target': 0}
SHARED_INPUTS = []
_WEIGHT_DTYPES = {'ffn1_norm': _jnp.float32, 'ffn1_w_gate': _jnp.float32, 'ffn1_w_up': _jnp.float32, 'ffn1_w_down': _jnp.float32, 'mix_norm': _jnp.float32, 'w_in': _jnp.float32, 'dil_q_norm': _jnp.float32, 'dil_k_norm': _jnp.float32, 'rel_bias': _jnp.float32, 'mla_q_a_norm': _jnp.float32, 'mla_w_q_b': _jnp.float32, 'mla_kv_a_norm': _jnp.float32, 'mla_w_kv_b': _jnp.float32, 'mla_q_norm': _jnp.float32, 'mla_k_norm': _jnp.float32, 'out_norm_dil': _jnp.float32, 'out_norm_mla': _jnp.float32, 'w_out': _jnp.float32, 'ffn2_norm': _jnp.float32, 'ffn2_w_gate': _jnp.float32, 'ffn2_w_up': _jnp.float32, 'ffn2_w_down': _jnp.float32}
MOMENT_SCALE = {'ffn1_norm': 1.194096e+01, 'ffn1_w_gate': 3.076626e-01, 'ffn1_w_up': 3.216342e-01, 'ffn1_w_down': 5.393547e-01, 'mix_norm': 2.174245e+00, 'w_in': 1.200745e+00, 'dil_q_norm': 1.314561e+00, 'dil_k_norm': 1.324048e+00, 'rel_bias': 5.817679e-01, 'mla_q_a_norm': 3.651625e+00, 'mla_w_q_b': 1.731065e+00, 'mla_kv_a_norm': 1.054678e+01, 'mla_w_kv_b': 2.410910e+00, 'mla_q_norm': 3.760533e+00, 'mla_k_norm': 3.798581e+00, 'out_norm_dil': 6.338797e+01, 'out_norm_mla': 6.507770e+01, 'w_out': 2.395477e+00, 'ffn2_norm': 1.208792e+01, 'ffn2_w_gate': 1.309764e-01, 'ffn2_w_up': 1.843364e-01, 'ffn2_w_down': 2.945457e-01}


def _to_microbatches(a, axis):
    t = _jnp.moveaxis(a, axis, 0)
    t = t.reshape((N_MICROBATCH, t.shape[0] // N_MICROBATCH) + t.shape[1:])
    return _jnp.moveaxis(t, 1, axis + 1)


def setup_inputs(seed: int = 0) -> dict:
    inp = _fwd_setup_inputs(seed)
    key = _jax.random.fold_in(_jax.random.key(seed), 7919)
    shape, _ = _output_shape()
    out = dict(inp)
    out["loss_target"] = _jax.random.normal(_jax.random.fold_in(key, 0), shape, _jnp.float32)
    for i, name in enumerate(TWIN_WEIGHTS):
        w = inp[name].astype(_jnp.float32)
        if MOMENT_SCALE is None:
            s = _jnp.sqrt(_jnp.mean(_jnp.square(w)) + 1e-30)
        else:
            s = MOMENT_SCALE[name]
        km, kv = _jax.random.split(_jax.random.fold_in(key, i + 1))
        out[name] = w
        out["m_" + name] = s * _jax.random.normal(km, w.shape, _jnp.float32)
        out["v_" + name] = (s * s) * _jax.random.uniform(kv, w.shape, _jnp.float32, 0.5, 1.5)
    if N_MICROBATCH > 1:
        for name, axis in PER_EXAMPLE_BATCH_AXIS.items():
            out[name] = _to_microbatches(out[name], axis)
    return {'x': out['x'], 'ffn1_norm': out['ffn1_norm'], 'ffn1_w_gate': out['ffn1_w_gate'], 'ffn1_w_up': out['ffn1_w_up'], 'ffn1_w_down': out['ffn1_w_down'], 'mix_norm': out['mix_norm'], 'w_in': out['w_in'], 'dil_q_norm': out['dil_q_norm'], 'dil_k_norm': out['dil_k_norm'], 'rel_bias': out['rel_bias'], 'mla_q_a_norm': out['mla_q_a_norm'], 'mla_w_q_b': out['mla_w_q_b'], 'mla_kv_a_norm': out['mla_kv_a_norm'], 'mla_w_kv_b': out['mla_w_kv_b'], 'mla_q_norm': out['mla_q_norm'], 'mla_k_norm': out['mla_k_norm'], 'out_norm_dil': out['out_norm_dil'], 'out_norm_mla': out['out_norm_mla'], 'w_out': out['w_out'], 'ffn2_norm': out['ffn2_norm'], 'ffn2_w_gate': out['ffn2_w_gate'], 'ffn2_w_up': out['ffn2_w_up'], 'ffn2_w_down': out['ffn2_w_down'], 'loss_target': out['loss_target'], 'm_ffn1_norm': out['m_ffn1_norm'], 'm_ffn1_w_gate': out['m_ffn1_w_gate'], 'm_ffn1_w_up': out['m_ffn1_w_up'], 'm_ffn1_w_down': out['m_ffn1_w_down'], 'm_mix_norm': out['m_mix_norm'], 'm_w_in': out['m_w_in'], 'm_dil_q_norm': out['m_dil_q_norm'], 'm_dil_k_norm': out['m_dil_k_norm'], 'm_rel_bias': out['m_rel_bias'], 'm_mla_q_a_norm': out['m_mla_q_a_norm'], 'm_mla_w_q_b': out['m_mla_w_q_b'], 'm_mla_kv_a_norm': out['m_mla_kv_a_norm'], 'm_mla_w_kv_b': out['m_mla_w_kv_b'], 'm_mla_q_norm': out['m_mla_q_norm'], 'm_mla_k_norm': out['m_mla_k_norm'], 'm_out_norm_dil': out['m_out_norm_dil'], 'm_out_norm_mla': out['m_out_norm_mla'], 'm_w_out': out['m_w_out'], 'm_ffn2_norm': out['m_ffn2_norm'], 'm_ffn2_w_gate': out['m_ffn2_w_gate'], 'm_ffn2_w_up': out['m_ffn2_w_up'], 'm_ffn2_w_down': out['m_ffn2_w_down'], 'v_ffn1_norm': out['v_ffn1_norm'], 'v_ffn1_w_gate': out['v_ffn1_w_gate'], 'v_ffn1_w_up': out['v_ffn1_w_up'], 'v_ffn1_w_down': out['v_ffn1_w_down'], 'v_mix_norm': out['v_mix_norm'], 'v_w_in': out['v_w_in'], 'v_dil_q_norm': out['v_dil_q_norm'], 'v_dil_k_norm': out['v_dil_k_norm'], 'v_rel_bias': out['v_rel_bias'], 'v_mla_q_a_norm': out['v_mla_q_a_norm'], 'v_mla_w_q_b': out['v_mla_w_q_b'], 'v_mla_kv_a_norm': out['v_mla_kv_a_norm'], 'v_mla_w_kv_b': out['v_mla_w_kv_b'], 'v_mla_q_norm': out['v_mla_q_norm'], 'v_mla_k_norm': out['v_mla_k_norm'], 'v_out_norm_dil': out['v_out_norm_dil'], 'v_out_norm_mla': out['v_out_norm_mla'], 'v_w_out': out['v_w_out'], 'v_ffn2_norm': out['v_ffn2_norm'], 'v_ffn2_w_gate': out['v_ffn2_w_gate'], 'v_ffn2_w_up': out['v_ffn2_w_up'], 'v_ffn2_w_down': out['v_ffn2_w_down']}


def _loss(weights, diff, rest, loss_target):
    with _jax.named_scope("forward"):
        args = {**rest, TWIN_DIFF_INPUT: diff, **{k: w.astype(_WEIGHT_DTYPES[k]) for k, w in weights.items()}}
        y = _forward(args)
    with _jax.named_scope("loss_head"):
        err = _jnp.square(y.astype(_jnp.float32) - loss_target)
        return 0.5 * _jnp.sum(_jnp.mean(err, axis=-1)) if err.ndim else 0.5 * err


def _adamw(w, g, m, v):
    m = ADAM_B1 * m + (1.0 - ADAM_B1) * g
    v = ADAM_B2 * v + (1.0 - ADAM_B2) * _jnp.square(g)
    m_hat = m / (1.0 - ADAM_B1 ** ADAM_STEP)
    v_hat = v / (1.0 - ADAM_B2 ** ADAM_STEP)
    delta = -ADAM_LR * (m_hat / (_jnp.sqrt(v_hat) + ADAM_EPS) + ADAM_WD * w)
    return delta, m, v


def reference(x, ffn1_norm, ffn1_w_gate, ffn1_w_up, ffn1_w_down, mix_norm, w_in, dil_q_norm, dil_k_norm, rel_bias, mla_q_a_norm, mla_w_q_b, mla_kv_a_norm, mla_w_kv_b, mla_q_norm, mla_k_norm, out_norm_dil, out_norm_mla, w_out, ffn2_norm, ffn2_w_gate, ffn2_w_up, ffn2_w_down, loss_target, m_ffn1_norm, m_ffn1_w_gate, m_ffn1_w_up, m_ffn1_w_down, m_mix_norm, m_w_in, m_dil_q_norm, m_dil_k_norm, m_rel_bias, m_mla_q_a_norm, m_mla_w_q_b, m_mla_kv_a_norm, m_mla_w_kv_b, m_mla_q_norm, m_mla_k_norm, m_out_norm_dil, m_out_norm_mla, m_w_out, m_ffn2_norm, m_ffn2_w_gate, m_ffn2_w_up, m_ffn2_w_down, v_ffn1_norm, v_ffn1_w_gate, v_ffn1_w_up, v_ffn1_w_down, v_mix_norm, v_w_in, v_dil_q_norm, v_dil_k_norm, v_rel_bias, v_mla_q_a_norm, v_mla_w_q_b, v_mla_kv_a_norm, v_mla_w_kv_b, v_mla_q_norm, v_mla_k_norm, v_out_norm_dil, v_out_norm_mla, v_w_out, v_ffn2_norm, v_ffn2_w_gate, v_ffn2_w_up, v_ffn2_w_down):
    given = dict(x=x, ffn1_norm=ffn1_norm, ffn1_w_gate=ffn1_w_gate, ffn1_w_up=ffn1_w_up, ffn1_w_down=ffn1_w_down, mix_norm=mix_norm, w_in=w_in, dil_q_norm=dil_q_norm, dil_k_norm=dil_k_norm, rel_bias=rel_bias, mla_q_a_norm=mla_q_a_norm, mla_w_q_b=mla_w_q_b, mla_kv_a_norm=mla_kv_a_norm, mla_w_kv_b=mla_w_kv_b, mla_q_norm=mla_q_norm, mla_k_norm=mla_k_norm, out_norm_dil=out_norm_dil, out_norm_mla=out_norm_mla, w_out=w_out, ffn2_norm=ffn2_norm, ffn2_w_gate=ffn2_w_gate, ffn2_w_up=ffn2_w_up, ffn2_w_down=ffn2_w_down, loss_target=loss_target, m_ffn1_norm=m_ffn1_norm, m_ffn1_w_gate=m_ffn1_w_gate, m_ffn1_w_up=m_ffn1_w_up, m_ffn1_w_down=m_ffn1_w_down, m_mix_norm=m_mix_norm, m_w_in=m_w_in, m_dil_q_norm=m_dil_q_norm, m_dil_k_norm=m_dil_k_norm, m_rel_bias=m_rel_bias, m_mla_q_a_norm=m_mla_q_a_norm, m_mla_w_q_b=m_mla_w_q_b, m_mla_kv_a_norm=m_mla_kv_a_norm, m_mla_w_kv_b=m_mla_w_kv_b, m_mla_q_norm=m_mla_q_norm, m_mla_k_norm=m_mla_k_norm, m_out_norm_dil=m_out_norm_dil, m_out_norm_mla=m_out_norm_mla, m_w_out=m_w_out, m_ffn2_norm=m_ffn2_norm, m_ffn2_w_gate=m_ffn2_w_gate, m_ffn2_w_up=m_ffn2_w_up, m_ffn2_w_down=m_ffn2_w_down, v_ffn1_norm=v_ffn1_norm, v_ffn1_w_gate=v_ffn1_w_gate, v_ffn1_w_up=v_ffn1_w_up, v_ffn1_w_down=v_ffn1_w_down, v_mix_norm=v_mix_norm, v_w_in=v_w_in, v_dil_q_norm=v_dil_q_norm, v_dil_k_norm=v_dil_k_norm, v_rel_bias=v_rel_bias, v_mla_q_a_norm=v_mla_q_a_norm, v_mla_w_q_b=v_mla_w_q_b, v_mla_kv_a_norm=v_mla_kv_a_norm, v_mla_w_kv_b=v_mla_w_kv_b, v_mla_q_norm=v_mla_q_norm, v_mla_k_norm=v_mla_k_norm, v_out_norm_dil=v_out_norm_dil, v_out_norm_mla=v_out_norm_mla, v_w_out=v_w_out, v_ffn2_norm=v_ffn2_norm, v_ffn2_w_gate=v_ffn2_w_gate, v_ffn2_w_up=v_ffn2_w_up, v_ffn2_w_down=v_ffn2_w_down)
    weights = {n: given[n] for n in TWIN_WEIGHTS}
    shared = {n: given[n] for n in SHARED_INPUTS}
    per_example = {n: given[n] for n in ['x']}
    grad_fn = _jax.value_and_grad(_loss, argnums=(0, 1))

    def one_microbatch(ex, loss_target):
        ex = dict(ex)
        diff = ex.pop(TWIN_DIFF_INPUT)
        return grad_fn(weights, diff, {**shared, **ex}, loss_target)

    if N_MICROBATCH == 1:
        loss, (grad_w, grad_x) = one_microbatch(per_example, given["loss_target"])
    else:
        def body(carry, xs):
            loss_sum, grad_sum = carry
            l_k, (gw_k, gx_k) = one_microbatch(xs[0], xs[1])
            with _jax.named_scope("update"):
                return (loss_sum + l_k, _jax.tree.map(_jnp.add, grad_sum, gw_k)), gx_k

        init = (_jnp.zeros((), _jnp.float32), _jax.tree.map(_jnp.zeros_like, weights))
        (loss, grad_w), grad_x = _jax.lax.scan(body, init, (per_example, given["loss_target"]))
    with _jax.named_scope("update"):
        delta_w, new_m, new_v = {}, {}, {}
        for n in TWIN_WEIGHTS:
            delta_w[n], new_m[n], new_v[n] = _adamw(weights[n], grad_w[n], given["m_" + n], given["v_" + n])
    return (loss, grad_x, *[grad_w[n] for n in TWIN_WEIGHTS], *[delta_w[n] for n in TWIN_WEIGHTS],
            *[new_m[n] for n in TWIN_WEIGHTS], *[new_v[n] for n in TWIN_WEIGHTS])
```

```python
import functools

import numpy as np
import jax
import jax.numpy as jnp
from jax import lax
from jax.experimental import pallas as pl
from jax.experimental.pallas import tpu as pltpu

F32 = jnp.float32
BF16 = jnp.bfloat16

D_MODEL = 1024
D_FF = 2816
N_CHIPS = 4
DIL_HEADS = 8
DIL_HD = 64
DIL_WIDTH = 512
DIL_DILATIONS = (1, 4, 16)
DIL_W = 128
QB = 128
MLA_HEADS = 4
MLA_NOPE = 128
MLA_ROPE = 64
MLA_QK = 192
MLA_V = 128
MLA_Q_RANK = 256
MLA_KV_RANK = 128
ROPE_BASE = 10000.0
REL_BUCKETS = 32
REL_MAX_DIST = 2048
FFN_RESID = 0.5
EPS = 1e-6
NEG = -1e30
LANES = 128

ADAM_LR = 0.001
ADAM_B1 = 0.9
ADAM_B2 = 0.999
ADAM_EPS = 1e-08
ADAM_WD = 0.01
ADAM_STEP = 10

NT = (((1,), (1,)), ((), ()))
NN = (((1,), (0,)), ((), ()))
TN = (((0,), (0,)), ((), ()))

BIG = (
    ("ffn1_w_gate", (D_MODEL, D_FF // N_CHIPS)),
    ("ffn1_w_up", (D_MODEL, D_FF // N_CHIPS)),
    ("ffn1_w_down", (D_FF // N_CHIPS, D_MODEL)),
    ("w_in", (D_MODEL, 1984 // N_CHIPS)),
    ("mla_w_q_b", (MLA_Q_RANK, MLA_QK)),
    ("mla_w_kv_b", (MLA_KV_RANK, MLA_NOPE + MLA_V)),
    ("w_out", (D_MODEL // N_CHIPS, D_MODEL)),
    ("ffn2_w_gate", (D_MODEL, D_FF // N_CHIPS)),
    ("ffn2_w_up", (D_MODEL, D_FF // N_CHIPS)),
    ("ffn2_w_down", (D_FF // N_CHIPS, D_MODEL)),
)
SMALL = (
    ("ffn1_norm", (1, 1024), 8), ("mix_norm", (1, 1024), 8), ("dil_q_norm", (1, 64), 1),
    ("dil_k_norm", (1, 64), 1), ("rel_bias", (8, 32), 2), ("mla_q_a_norm", (1, 256), 2),
    ("mla_kv_a_norm", (1, 128), 1), ("mla_q_norm", (1, 192), 2), ("mla_k_norm", (1, 192), 2),
    ("out_norm_dil", (1, 512), 4), ("out_norm_mla", (1, 512), 4), ("ffn2_norm", (1, 1024), 8),
)
SMALL_ROWS = 48
WEIGHTS = ("ffn1_norm", "ffn1_w_gate", "ffn1_w_up", "ffn1_w_down", "mix_norm", "w_in", "dil_q_norm",
           "dil_k_norm", "rel_bias", "mla_q_a_norm", "mla_w_q_b", "mla_kv_a_norm", "mla_w_kv_b",
           "mla_q_norm", "mla_k_norm", "out_norm_dil", "out_norm_mla", "w_out", "ffn2_norm",
           "ffn2_w_gate", "ffn2_w_up", "ffn2_w_down")


def _pcall(body, **kw):
    return pl.pallas_call(body, **kw)


def _cparams(*sem):
    return pltpu.CompilerParams(dimension_semantics=sem)


def _sds(shape, dtype):
    return jax.ShapeDtypeStruct(shape, dtype)


def _dot(a, b, dn):
    return lax.dot_general(a, b, dn, preferred_element_type=F32)


def _rms_fwd(x, g, out_dtype, name, tm):
    n, d = x.shape
    tm = min(tm, n)

    def body(x_ref, g_ref, o_ref):
        xf = x_ref[...].astype(F32)
        r = lax.rsqrt(jnp.mean(xf * xf, axis=-1, keepdims=True) + EPS)
        o_ref[...] = (xf * r * g_ref[...]).astype(o_ref.dtype)

    return _pcall(
        body, name=name, grid=(n // tm,),
        in_specs=[pl.BlockSpec((tm, d), lambda i: (i, 0)), pl.BlockSpec((1, d), lambda i: (0, 0))],
        out_specs=pl.BlockSpec((tm, d), lambda i: (i, 0)),
        out_shape=_sds((n, d), out_dtype), compiler_params=_cparams("parallel"))(x, g)


def _rms_bwd(dys, x, g, res, name, tm):
    n, d = x.shape
    tm = min(tm, n)
    nd = len(dys)
    has_res = res is not None

    def body(*refs):
        dy_refs = refs[:nd]
        x_ref, g_ref = refs[nd], refs[nd + 1]
        res_ref = refs[nd + 2] if has_res else None
        dx_ref, dg_ref = refs[-2], refs[-1]
        dy = dy_refs[0][...].astype(F32)
        for r_ in dy_refs[1:]:
            dy = dy + r_[...].astype(F32)
        xf = x_ref[...].astype(F32)
        r = lax.rsqrt(jnp.mean(xf * xf, axis=-1, keepdims=True) + EPS)
        xh = xf * r
        dxh = dy * g_ref[...]
        dx = r * (dxh - xh * jnp.mean(dxh * xh, axis=-1, keepdims=True))
        if has_res:
            dx = dx + res_ref[...]
        dx_ref[...] = dx

        @pl.when(pl.program_id(0) == 0)
        def _():
            dg_ref[...] = jnp.zeros_like(dg_ref)

        dg_ref[...] += jnp.sum(dy * xh, axis=0, keepdims=True)

    row = pl.BlockSpec((tm, d), lambda i: (i, 0))
    vec = pl.BlockSpec((1, d), lambda i: (0, 0))
    ins = list(dys) + [x, g] + ([res] if has_res else [])
    return _pcall(
        body, name=name, grid=(n // tm,),
        in_specs=[row] * nd + [row, vec] + ([row] if has_res else []),
        out_specs=(row, vec),
        out_shape=(_sds((n, d), F32), _sds((1, d), F32)),
        compiler_params=_cparams("arbitrary"))(*ins)


def _mm(name, grid, pairs, dn, out_shape, out_spec, acc_shape, res=None, scale=1.0):
    npairs = len(pairs)
    nred = grid[2]
    has_res = res is not None

    def body(*refs):
        ab = refs[:2 * npairs]
        res_ref = refs[2 * npairs] if has_res else None
        o_ref = refs[2 * npairs + int(has_res)]
        acc_ref = refs[-1] if nred > 1 else None
        tot = None
        for p in range(npairs):
            d = _dot(ab[2 * p][...].astype(BF16), ab[2 * p + 1][...].astype(BF16), dn)
            tot = d if tot is None else tot + d

        def finish(v):
            if scale != 1.0:
                v = v * scale
            if has_res:
                v = res_ref[...] + v
            o_ref[...] = v.astype(o_ref.dtype)

        if nred == 1:
            finish(tot)
        else:
            r = pl.program_id(2)

            @pl.when(r == 0)
            def _():
                acc_ref[...] = tot

            @pl.when(r > 0)
            def _():
                acc_ref[...] += tot

            @pl.when(r == nred - 1)
            def _():
                finish(acc_ref[...])

    ins, specs = [], []
    for a, a_spec, b, b_spec in pairs:
        ins += [a, b]
        specs += [a_spec, b_spec]
    if has_res:
        ins.append(res[0])
        specs.append(res[1])
    return _pcall(
        body, name=name, grid=grid, in_specs=specs, out_specs=out_spec, out_shape=out_shape,
        scratch_shapes=[pltpu.VMEM(acc_shape, F32)] if nred > 1 else [],
        compiler_params=_cparams("parallel", "parallel", "arbitrary"))(*ins)


def _ffn_up(h, wg, wu, name, tm):
    t, d = h.shape
    nc, _, fs = wg.shape
    tm = min(tm, t)

    def body(h_ref, wg_ref, wu_ref, g_ref, u_ref, a_ref):
        hh = h_ref[...]
        gate = _dot(hh, wg_ref[...], NN)
        up = _dot(hh, wu_ref[...], NN)
        g_ref[...] = gate.astype(BF16)
        u_ref[...] = up.astype(BF16)
        a_ref[...] = (gate * jax.nn.sigmoid(gate) * up).astype(BF16)

    wspec = pl.BlockSpec((None, d, fs), lambda c, i: (c, 0, 0))
    ospec = pl.BlockSpec((None, tm, fs), lambda c, i: (c, i, 0))
    osd = _sds((nc, t, fs), BF16)
    return _pcall(
        body, name=name, grid=(nc, t // tm),
        in_specs=[pl.BlockSpec((tm, d), lambda c, i: (i, 0)), wspec, wspec],
        out_specs=(ospec, ospec, ospec), out_shape=(osd, osd, osd),
        compiler_params=_cparams("parallel", "parallel"))(h, wg, wu)


def _ffn_dact(dy, wd, gate, up, name, tm):
    t, d = dy.shape
    nc, fs, _ = wd.shape
    tm = min(tm, t)

    def body(dy_ref, wd_ref, g_ref, u_ref, dg_ref, du_ref):
        da = _dot(dy_ref[...].astype(BF16), wd_ref[...], NT) * FFN_RESID
        gate = g_ref[...].astype(F32)
        up = u_ref[...].astype(F32)
        sig = jax.nn.sigmoid(gate)
        dg_ref[...] = (da * up * (sig * (1.0 + gate * (1.0 - sig)))).astype(BF16)
        du_ref[...] = (da * (gate * sig)).astype(BF16)

    cspec = pl.BlockSpec((None, tm, fs), lambda c, i: (c, i, 0))
    osd = _sds((nc, t, fs), BF16)
    return _pcall(
        body, name=name, grid=(nc, t // tm),
        in_specs=[pl.BlockSpec((tm, d), lambda c, i: (i, 0)),
                  pl.BlockSpec((None, fs, d), lambda c, i: (c, 0, 0)), cspec, cspec],
        out_specs=(cspec, cspec), out_shape=(osd, osd),
        compiler_params=_cparams("parallel", "parallel"))(dy, wd, gate, up)


def _ffn_fwd(x, g, wg, wu, wd, tag):
    t = x.shape[0]
    nc, _, fs = wg.shape
    tm = min(512, t)
    h = _rms_fwd(x, g, BF16, f"{tag}_norm", 512)
    gate, up, act = _ffn_up(h, wg, wu, f"{tag}_up", 512)
    y = _mm(f"{tag}_down", (t // tm, 1, nc),
            [(act, pl.BlockSpec((None, tm, fs), lambda i, j, r: (r, i, 0)),
              wd, pl.BlockSpec((None, fs, D_MODEL), lambda i, j, r: (r, 0, 0)))],
            NN, _sds((t, D_MODEL), F32), pl.BlockSpec((tm, D_MODEL), lambda i, j, r: (i, 0)),
            (tm, D_MODEL), res=(x, pl.BlockSpec((tm, D_MODEL), lambda i, j, r: (i, 0))), scale=FFN_RESID)
    return y, (h, gate, up, act)


def _ffn_bwd(dy, x, g, wg, wu, wd, saved, tag):
    h, gate, up, act = saved
    t = x.shape[0]
    nc, _, fs = wg.shape
    tm = min(512, t)
    tk = min(512, t)
    dgate, dup = _ffn_dact(dy, wd, gate, up, f"{tag}_dact", 512)
    tok_c = pl.BlockSpec((None, tk, fs), lambda c, j, r: (c, r, 0))
    tok_d = pl.BlockSpec((tk, D_MODEL), lambda c, j, r: (r, 0))
    dwd = _mm(f"{tag}_dwd", (nc, 1, t // tk), [(act, tok_c, dy, tok_d)], TN,
              _sds((nc, fs, D_MODEL), F32), pl.BlockSpec((None, fs, D_MODEL), lambda c, j, r: (c, 0, 0)),
              (fs, D_MODEL), scale=FFN_RESID)
    wout = pl.BlockSpec((None, D_MODEL, fs), lambda c, j, r: (c, 0, 0))
    dwg = _mm(f"{tag}_dwg", (nc, 1, t // tk), [(h, tok_d, dgate, tok_c)], TN,
              _sds((nc, D_MODEL, fs), F32), wout, (D_MODEL, fs))
    dwu = _mm(f"{tag}_dwu", (nc, 1, t // tk), [(h, tok_d, dup, tok_c)], TN,
              _sds((nc, D_MODEL, fs), F32), wout, (D_MODEL, fs))
    a_spec = pl.BlockSpec((None, tm, fs), lambda i, j, r: (r, i, 0))
    w_spec = pl.BlockSpec((None, D_MODEL, fs), lambda i, j, r: (r, 0, 0))
    dh = _mm(f"{tag}_dh", (t // tm, 1, nc), [(dgate, a_spec, wg, w_spec), (dup, a_spec, wu, w_spec)], NT,
             _sds((t, D_MODEL), F32), pl.BlockSpec((tm, D_MODEL), lambda i, j, r: (i, 0)), (tm, D_MODEL))
    dx, dg = _rms_bwd([dh], x, g, dy, f"{tag}_dnorm", 512)
    return dx, dg, dwg, dwu, dwd


def _mm_simple(name, a, b, dn, out_dtype, tm=512, tk=512, res=None, scale=1.0):
    if dn == TN:
        k, m = a.shape
        n = b.shape[1]
        tk = min(tk, k)
        return _mm(name, (1, 1, k // tk),
                   [(a, pl.BlockSpec((tk, m), lambda i, j, r: (r, 0)), b, pl.BlockSpec((tk, n), lambda i, j, r: (r, 0)))],
                   TN, _sds((m, n), out_dtype), pl.BlockSpec((m, n), lambda i, j, r: (0, 0)), (m, n), scale=scale)
    m, k = a.shape
    n = b.shape[1] if dn == NN else b.shape[0]
    tm = min(tm, m)
    row = pl.BlockSpec((tm, n), lambda i, j, r: (i, 0))
    return _mm(name, (m // tm, 1, 1),
               [(a, pl.BlockSpec((tm, k), lambda i, j, r: (i, 0)), b, pl.BlockSpec(b.shape, lambda i, j, r: (0, 0)))],
               dn, _sds((m, n), out_dtype), row, (tm, n), res=None if res is None else (res, row), scale=scale)


def _t5_bucket(dist):
    max_exact = REL_BUCKETS // 2
    d = np.maximum(dist, 1).astype(np.float32)
    large = max_exact + (np.log(d / max_exact) / np.log(REL_MAX_DIST / max_exact)
                         * (REL_BUCKETS - max_exact)).astype(np.int32)
    large = np.minimum(large, REL_BUCKETS - 1)
    return np.where(dist < max_exact, dist, large).astype(np.int32)


def _bucket_tiles():
    i = np.arange(QB)[:, None]
    j = np.arange(QB + DIL_W)[None, :]
    delta = np.clip(i + DIL_W - j, 0, None)
    return np.stack([_t5_bucket(delta * dil) for dil in DIL_DILATIONS]).astype(np.int32)


def _bias_tiles(rel_bias):
    buckets = jnp.asarray(_bucket_tiles())

    def body(rb_ref, bk_ref, o_ref):
        bk = bk_ref[...]
        for h in range(DIL_HEADS):
            def pick(b, tile):
                return jnp.where(bk == b, rb_ref[h, b], tile)

            o_ref[h] = lax.fori_loop(0, REL_BUCKETS, pick, jnp.zeros((QB, QB + DIL_W), F32))

    return _pcall(
        body, name="dil_bias_tiles", grid=(3,),
        in_specs=[pl.BlockSpec(memory_space=pltpu.SMEM),
                  pl.BlockSpec((None, QB, QB + DIL_W), lambda b: (b, 0, 0))],
        out_specs=pl.BlockSpec((None, DIL_HEADS, QB, QB + DIL_W), lambda b: (b, 0, 0, 0)),
        out_shape=_sds((3, DIL_HEADS, QB, QB + DIL_W), F32),
        compiler_params=_cparams("parallel"))(rel_bias, buckets)


def _bias_grad(dtiles):
    buckets = jnp.asarray(_bucket_tiles())

    def body(dt_ref, bk_ref, o_ref):
        for h in range(DIL_HEADS):
            def one(b, carry):
                tot = jnp.zeros((), F32)
                for br in range(3):
                    tot = tot + jnp.sum(jnp.where(bk_ref[br] == b, dt_ref[br, h], 0.0))
                o_ref[h, b] = tot
                return carry

            lax.fori_loop(0, REL_BUCKETS, one, 0)

    return _pcall(
        body, name="dil_bias_grad",
        in_specs=[pl.BlockSpec(memory_space=pltpu.VMEM), pl.BlockSpec(memory_space=pltpu.VMEM)],
        out_specs=pl.BlockSpec(memory_space=pltpu.SMEM),
        out_shape=_sds((DIL_HEADS, REL_BUCKETS), F32))(dtiles, buckets)


def _dil_masks(has_prev):
    ii = lax.broadcasted_iota(jnp.int32, (QB, QB), 0)
    jj = lax.broadcasted_iota(jnp.int32, (QB, QB), 1)
    return jj <= ii, jj >= ii + jnp.where(has_prev, 0, QB)


def _dil_fwd(q, k, v, bias, cls_len, name):
    nh, t, hd = q.shape
    nb = t // QB
    per = cls_len // QB
    scale = hd ** -0.5

    def body(q_ref, kc_ref, kp_ref, vc_ref, vp_ref, b_ref, o_ref, lse_ref):
        n = pl.program_id(0)
        cur_ok, prev_ok = _dil_masks((n % per) != 0)
        for h in range(nh):
            qh = q_ref[h]
            sc = _dot(qh, kc_ref[h], NT) * scale + b_ref[h, :, QB:]
            sp = _dot(qh, kp_ref[h], NT) * scale + b_ref[h, :, :QB]
            sc = jnp.where(cur_ok, sc, NEG)
            sp = jnp.where(prev_ok, sp, NEG)
            m = jnp.maximum(jnp.max(sc, axis=-1, keepdims=True), jnp.max(sp, axis=-1, keepdims=True))
            pc = jnp.exp(sc - m)
            pp = jnp.exp(sp - m)
            den = jnp.sum(pc, axis=-1, keepdims=True) + jnp.sum(pp, axis=-1, keepdims=True)
            o = _dot(pc.astype(BF16), vc_ref[h], NN) + _dot(pp.astype(BF16), vp_ref[h], NN)
            o_ref[h] = o / den
            lse_ref[h] = m + jnp.log(den)

    cur = pl.BlockSpec((nh, QB, hd), lambda n: (0, n, 0))
    prev = pl.BlockSpec((nh, QB, hd), lambda n: (0, jnp.maximum(n - 1, 0), 0))
    return _pcall(
        body, name=name, grid=(nb,),
        in_specs=[cur, cur, prev, cur, prev, pl.BlockSpec((nh, QB, QB + DIL_W), lambda n: (0, 0, 0))],
        out_specs=(cur, pl.BlockSpec((nh, QB, 1), lambda n: (0, n, 0))),
        out_shape=(_sds((nh, t, hd), F32), _sds((nh, t, 1), F32)),
        compiler_params=_cparams("parallel"))(q, k, k, v, v, bias)


def _dil_bwd(q, k, v, do, lse, dl, bias, cls_len, name):
    nh, t, hd = q.shape
    nb = t // QB
    per = cls_len // QB
    scale = hd ** -0.5

    def body(qc_ref, qn_ref, doc_ref, don_ref, lc_ref, ln_ref, dc_ref, dn_ref, k_ref, v_ref, b_ref,
             dq_ref, dk_ref, dv_ref, db_ref, carry):
        n = pl.program_id(0)
        nxt = n + 1
        cur_ok, prev_ok = _dil_masks((nxt < nb) & ((nxt % per) != 0))

        @pl.when(n == 0)
        def _():
            db_ref[...] = jnp.zeros_like(db_ref)
            carry[...] = jnp.zeros_like(carry)

        for h in range(nh):
            kh = k_ref[h]
            vh = v_ref[h]
            q1, q2 = qc_ref[h], qn_ref[h]
            do1, do2 = doc_ref[h], don_ref[h]
            s1 = jnp.where(cur_ok, _dot(q1, kh, NT) * scale + b_ref[h, :, QB:], NEG)
            s2 = jnp.where(prev_ok, _dot(q2, kh, NT) * scale + b_ref[h, :, :QB], NEG)
            p1 = jnp.exp(s1 - lc_ref[h])
            p2 = jnp.exp(s2 - ln_ref[h])
            ds1 = p1 * (_dot(do1, vh, NT) - dc_ref[h])
            ds2 = p2 * (_dot(do2, vh, NT) - dn_ref[h])
            ds1b = ds1.astype(BF16)
            ds2b = ds2.astype(BF16)
            dq_ref[h] = carry[h] + _dot(ds1b, kh, NN) * scale
            carry[h] = _dot(ds2b, kh, NN) * scale
            dk_ref[h] = (_dot(ds1b, q1, TN) + _dot(ds2b, q2, TN)) * scale
            dv_ref[h] = _dot(p1.astype(BF16), do1, TN) + _dot(p2.astype(BF16), do2, TN)
            db_ref[h, :, QB:] += ds1
            db_ref[h, :, :QB] += ds2

    def cur(w):
        return pl.BlockSpec((nh, QB, w), lambda n: (0, n, 0))

    def nxt(w):
        return pl.BlockSpec((nh, QB, w), lambda n: (0, jnp.minimum(n + 1, nb - 1), 0))

    tile = pl.BlockSpec((nh, QB, QB + DIL_W), lambda n: (0, 0, 0))
    o3 = _sds((nh, t, hd), F32)
    return _pcall(
        body, name=name, grid=(nb,),
        in_specs=[cur(hd), nxt(hd), cur(hd), nxt(hd), cur(1), nxt(1), cur(1), nxt(1), cur(hd), cur(hd), tile],
        out_specs=(cur(hd), cur(hd), cur(hd), tile),
        out_shape=(o3, o3, o3, _sds((nh, QB, QB + DIL_W), F32)),
        scratch_shapes=[pltpu.VMEM((nh, QB, hd), F32)],
        compiler_params=_cparams("arbitrary"))(q, q, do, do, lse, lse, dl, dl, k, v, bias)


def _dil_merge(outs, lses, tm):
    n, hd = outs[0].shape
    tm = min(tm, n)

    def body(o0, o1, o2, l0, l1, l2, o_ref, l_ref):
        a0, a1, a2 = l0[...], l1[...], l2[...]
        m = jnp.maximum(jnp.maximum(a0, a1), a2)
        e0, e1, e2 = jnp.exp(a0 - m), jnp.exp(a1 - m), jnp.exp(a2 - m)
        den = e0 + e1 + e2
        o_ref[...] = (e0 * o0[...] + e1 * o1[...] + e2 * o2[...]) / den
        l_ref[...] = m + jnp.log(den)

    ospec = pl.BlockSpec((tm, hd), lambda i: (i, 0))
    lspec = pl.BlockSpec((tm, 1), lambda i: (i, 0))
    return _pcall(
        body, name="dil_merge", grid=(n // tm,),
        in_specs=[ospec] * 3 + [lspec] * 3, out_specs=(ospec, lspec),
        out_shape=(_sds((n, hd), F32), _sds((n, 1), F32)),
        compiler_params=_cparams("parallel"))(*outs, *lses)


def _rowdot(a, b, name, tm):
    n, d = a.shape
    tm = min(tm, n)

    def body(a_ref, b_ref, o_ref):
        o_ref[...] = jnp.sum(a_ref[...].astype(F32) * b_ref[...].astype(F32), axis=-1, keepdims=True)

    spec = pl.BlockSpec((tm, d), lambda i: (i, 0))
    return _pcall(body, name=name, grid=(n // tm,), in_specs=[spec, spec],
                  out_specs=pl.BlockSpec((tm, 1), lambda i: (i, 0)), out_shape=_sds((n, 1), F32),
                  compiler_params=_cparams("parallel"))(a, b)


def _add3(a, b, c, name, tm):
    n, d = a.shape
    tm = min(tm, n)

    def body(a_ref, b_ref, c_ref, o_ref):
        o_ref[...] = a_ref[...] + b_ref[...] + c_ref[...]

    spec = pl.BlockSpec((tm, d), lambda i: (i, 0))
    return _pcall(body, name=name, grid=(n // tm,), in_specs=[spec] * 3, out_specs=spec,
                  out_shape=_sds((n, d), F32), compiler_params=_cparams("parallel"))(a, b, c)


def _to_sub(a, dil):
    nh, t, w = a.shape
    if dil == 1:
        return a
    return a.reshape(nh, t // dil, dil, w).transpose(0, 2, 1, 3).reshape(nh, t, w)


def _from_sub(a, dil):
    nh, t, w = a.shape
    if dil == 1:
        return a
    return a.reshape(nh, dil, t // dil, w).transpose(0, 2, 1, 3).reshape(nh, t, w)


def _rope_tables(t):
    inv = ROPE_BASE ** (-np.arange(0, MLA_ROPE, 2, dtype=np.float64) / MLA_ROPE)
    ang = np.arange(t, dtype=np.float64)[:, None] * inv[None, :]
    cos, sin = np.cos(ang), np.sin(ang)
    return (jnp.asarray(np.concatenate([cos, cos], 1), F32), jnp.asarray(np.concatenate([-sin, sin], 1), F32))


def _half_swap():
    p = np.zeros((MLA_ROPE, MLA_ROPE), np.float32)
    half = MLA_ROPE // 2
    for i in range(MLA_ROPE):
        p[(i + half) % MLA_ROPE, i] = 1.0
    return jnp.asarray(p)


def _mla_qk_fwd(x, g, cos_t, sin_t, scale, name, tm):
    n, d = x.shape
    t = cos_t.shape[0]
    tm = min(tm, t)
    nt = t // tm
    swap = _half_swap()

    def body(x_ref, g_ref, c_ref, s_ref, p_ref, o_ref):
        xf = x_ref[...]
        r = lax.rsqrt(jnp.mean(xf * xf, axis=-1, keepdims=True) + EPS)
        y = xf * r * g_ref[...]
        yr = y[:, MLA_NOPE:]
        sw = lax.dot_general(yr, p_ref[...], NN, precision=lax.Precision.HIGHEST, preferred_element_type=F32)
        o_ref[:, :MLA_NOPE] = (y[:, :MLA_NOPE] * scale).astype(o_ref.dtype)
        o_ref[:, MLA_NOPE:] = ((yr * c_ref[...] + sw * s_ref[...]) * scale).astype(o_ref.dtype)

    row = pl.BlockSpec((tm, d), lambda i: (i, 0))
    tab = pl.BlockSpec((tm, MLA_ROPE), lambda i: (i % nt, 0))
    return _pcall(
        body, name=name, grid=(n // tm,),
        in_specs=[row, pl.BlockSpec((1, d), lambda i: (0, 0)), tab, tab,
                  pl.BlockSpec((MLA_ROPE, MLA_ROPE), lambda i: (0, 0))],
        out_specs=row, out_shape=_sds((n, d), BF16),
        compiler_params=_cparams("parallel"))(x, g, cos_t, sin_t, swap)


def _mla_qk_bwd(dy, x, g, cos_t, sin_t, scale, name, tm):
    n, d = x.shape
    t = cos_t.shape[0]
    tm = min(tm, t)
    nt = t // tm
    swap_t = _half_swap().T

    def body(dy_ref, x_ref, g_ref, c_ref, s_ref, p_ref, dx_ref, dg_ref):
        xf = x_ref[...]
        gg = g_ref[...]
        r = lax.rsqrt(jnp.mean(xf * xf, axis=-1, keepdims=True) + EPS)
        xh = xf * r
        dyf = dy_ref[...] * scale
        dyr = dyf[:, MLA_NOPE:]
        back = lax.dot_general(dyr * s_ref[...], p_ref[...], NN, precision=lax.Precision.HIGHEST,
                               preferred_element_type=F32)
        dn_n = dyf[:, :MLA_NOPE]
        dn_r = dyr * c_ref[...] + back
        xh_n, xh_r = xh[:, :MLA_NOPE], xh[:, MLA_NOPE:]
        dxh_n = dn_n * gg[:, :MLA_NOPE]
        dxh_r = dn_r * gg[:, MLA_NOPE:]
        mean = (jnp.sum(dxh_n * xh_n, axis=-1, keepdims=True)
                + jnp.sum(dxh_r * xh_r, axis=-1, keepdims=True)) * (1.0 / d)
        dx_ref[:, :MLA_NOPE] = r * (dxh_n - xh_n * mean)
        dx_ref[:, MLA_NOPE:] = r * (dxh_r - xh_r * mean)

        @pl.when(pl.program_id(0) == 0)
        def _():
            dg_ref[...] = jnp.zeros_like(dg_ref)

        dg_ref[:, :MLA_NOPE] += jnp.sum(dn_n * xh_n, axis=0, keepdims=True)
        dg_ref[:, MLA_NOPE:] += jnp.sum(dn_r * xh_r, axis=0, keepdims=True)

    row = pl.BlockSpec((tm, d), lambda i: (i, 0))
    vec = pl.BlockSpec((1, d), lambda i: (0, 0))
    tab = pl.BlockSpec((tm, MLA_ROPE), lambda i: (i % nt, 0))
    return _pcall(
        body, name=name, grid=(n // tm,),
        in_specs=[row, row, vec, tab, tab, pl.BlockSpec((MLA_ROPE, MLA_ROPE), lambda i: (0, 0))],
        out_specs=(row, vec), out_shape=(_sds((n, d), F32), _sds((1, d), F32)),
        compiler_params=_cparams("arbitrary"))(dy, x, g, cos_t, sin_t, swap_t)


def _causal_mask(i, j, tq, tk):
    row = i * tq + lax.broadcasted_iota(jnp.int32, (tq, tk), 0)
    col = j * tk + lax.broadcasted_iota(jnp.int32, (tq, tk), 1)
    return col <= row


def _mla_fwd(q, k, v, tq, tk):
    nh, t, dq = q.shape
    dv = v.shape[2]
    tq, tk = min(tq, t), min(tk, t)
    nq, nk = t // tq, t // tk

    def last_kv(i):
        return (i * tq + tq - 1) // tk

    def body(q_ref, k_ref, v_ref, o_ref, lse_ref, m_sc, l_sc, acc_sc):
        i, j = pl.program_id(1), pl.program_id(2)

        @pl.when(j == 0)
        def _():
            m_sc[...] = jnp.full_like(m_sc, NEG)
            l_sc[...] = jnp.zeros_like(l_sc)
            acc_sc[...] = jnp.zeros_like(acc_sc)

        @pl.when(j <= last_kv(i))
        def _():
            s = jnp.where(_causal_mask(i, j, tq, tk), _dot(q_ref[...], k_ref[...], NT), NEG)
            m_prev = m_sc[...]
            m_new = jnp.maximum(m_prev, jnp.max(s, axis=-1, keepdims=True))
            alpha = jnp.exp(m_prev - m_new)
            p = jnp.exp(s - m_new)
            l_sc[...] = alpha * l_sc[...] + jnp.sum(p, axis=-1, keepdims=True)
            acc_sc[...] = alpha * acc_sc[...] + _dot(p.astype(BF16), v_ref[...], NN)
            m_sc[...] = m_new

        @pl.when(j == nk - 1)
        def _():
            o_ref[...] = acc_sc[...] / l_sc[...]
            lse_ref[...] = m_sc[...] + jnp.log(l_sc[...])

    def kv(w):
        return pl.BlockSpec((None, tk, w), lambda h, i, j: (h, jnp.minimum(j, last_kv(i)), 0))

    return _pcall(
        body, name="mla_attn_fwd", grid=(nh, nq, nk),
        in_specs=[pl.BlockSpec((None, tq, dq), lambda h, i, j: (h, i, 0)), kv(dq), kv(dv)],
        out_specs=(pl.BlockSpec((None, tq, dv), lambda h, i, j: (h, i, 0)),
                   pl.BlockSpec((None, tq, 1), lambda h, i, j: (h, i, 0))),
        out_shape=(_sds((nh, t, dv), F32), _sds((nh, t, 1), F32)),
        scratch_shapes=[pltpu.VMEM((tq, 1), F32), pltpu.VMEM((tq, 1), F32), pltpu.VMEM((tq, dv), F32)],
        compiler_params=_cparams("parallel", "parallel", "arbitrary"))(q, k, v)


def _mla_bwd_dq(q, k, v, do, lse, dl, tq, tk):
    nh, t, dq = q.shape
    dv = v.shape[2]
    tq, tk = min(tq, t), min(tk, t)
    nq, nk = t // tq, t // tk

    def last_kv(i):
        return (i * tq + tq - 1) // tk

    def body(q_ref, k_ref, v_ref, do_ref, lse_ref, dl_ref, dq_ref, acc_sc):
        i, j = pl.program_id(1), pl.program_id(2)

        @pl.when(j == 0)
        def _():
            acc_sc[...] = jnp.zeros_like(acc_sc)

        @pl.when(j <= last_kv(i))
        def _():
            s = jnp.where(_causal_mask(i, j, tq, tk), _dot(q_ref[...], k_ref[...], NT), NEG)
            p = jnp.exp(s - lse_ref[...])
            dp = _dot(do_ref[...].astype(BF16), v_ref[...], NT)
            ds = p * (dp - dl_ref[...])
            acc_sc[...] += _dot(ds.astype(BF16), k_ref[...], NN)

        @pl.when(j == nk - 1)
        def _():
            dq_ref[...] = acc_sc[...]

    def qs(w):
        return pl.BlockSpec((None, tq, w), lambda h, i, j: (h, i, 0))

    def kv(w):
        return pl.BlockSpec((None, tk, w), lambda h, i, j: (h, jnp.minimum(j, last_kv(i)), 0))

    return _pcall(
        body, name="mla_attn_dq", grid=(nh, nq, nk),
        in_specs=[qs(dq), kv(dq), kv(dv), qs(dv), qs(1), qs(1)],
        out_specs=qs(dq), out_shape=_sds((nh, t, dq), F32),
        scratch_shapes=[pltpu.VMEM((tq, dq), F32)],
        compiler_params=_cparams("parallel", "parallel", "arbitrary"))(q, k, v, do, lse, dl)


def _mla_bwd_dkv(q, k, v, do, lse_row, dl_row, tq, tk):
    nh, t, dq = q.shape
    dv = v.shape[2]
    tq, tk = min(tq, t), min(tk, t)
    nq, nk = t // tq, t // tk

    def first_q(j):
        return (j * tk) // tq

    def body(q_ref, k_ref, v_ref, do_ref, lse_ref, dl_ref, dk_ref, dv_ref, dk_sc, dv_sc):
        j, i = pl.program_id(1), pl.program_id(2)

        @pl.when(i == 0)
        def _():
            dk_sc[...] = jnp.zeros_like(dk_sc)
            dv_sc[...] = jnp.zeros_like(dv_sc)

        @pl.when(i >= first_q(j))
        def _():
            key = j * tk + lax.broadcasted_iota(jnp.int32, (tk, tq), 0)
            qry = i * tq + lax.broadcasted_iota(jnp.int32, (tk, tq), 1)
            st = jnp.where(key <= qry, _dot(k_ref[...], q_ref[...], NT), NEG)
            pt = jnp.exp(st - lse_ref[...])
            dob = do_ref[...].astype(BF16)
            dpt = _dot(v_ref[...], dob, NT)
            dst = pt * (dpt - dl_ref[...])
            dv_sc[...] += _dot(pt.astype(BF16), dob, NN)
            dk_sc[...] += _dot(dst.astype(BF16), q_ref[...], NN)

        @pl.when(i == nq - 1)
        def _():
            dk_ref[...] = dk_sc[...]
            dv_ref[...] = dv_sc[...]

    def qs(w):
        return pl.BlockSpec((None, tq, w), lambda h, j, i: (h, jnp.maximum(i, first_q(j)), 0))

    def kv(w):
        return pl.BlockSpec((None, tk, w), lambda h, j, i: (h, j, 0))

    rowv = pl.BlockSpec((None, 1, tq), lambda h, j, i: (h, 0, jnp.maximum(i, first_q(j))))
    return _pcall(
        body, name="mla_attn_dkv", grid=(nh, nk, nq),
        in_specs=[qs(dq), kv(dq), kv(dv), qs(dv), rowv, rowv],
        out_specs=(kv(dq), kv(dv)), out_shape=(_sds((nh, t, dq), F32), _sds((nh, t, dv), F32)),
        scratch_shapes=[pltpu.VMEM((tk, dq), F32), pltpu.VMEM((tk, dv), F32)],
        compiler_params=_cparams("parallel", "parallel", "arbitrary"))(q, k, v, do, lse_row, dl_row)


def _loss_head(y, target, tm):
    t, d = y.shape
    tm = min(tm, t)
    nt = t // tm

    def body(y_ref, t_ref, dy_ref, loss_ref, acc):
        i = pl.program_id(0)
        err = y_ref[...] - t_ref[...]
        dy_ref[...] = err * (1.0 / d)

        @pl.when(i == 0)
        def _():
            acc[...] = jnp.zeros_like(acc)

        acc[...] += jnp.sum(err * err, axis=0, keepdims=True)

        @pl.when(i == nt - 1)
        def _():
            loss_ref[0, 0] = jnp.sum(acc[...]) * (0.5 / d)

    spec = pl.BlockSpec((tm, d), lambda i: (i, 0))
    return _pcall(
        body, name="loss_head", grid=(nt,), in_specs=[spec, spec],
        out_specs=(spec, pl.BlockSpec(memory_space=pltpu.SMEM)),
        out_shape=(_sds((t, d), F32), _sds((1, 1), F32)),
        scratch_shapes=[pltpu.VMEM((1, d), F32)],
        compiler_params=_cparams("arbitrary"))(y, target)


def _adamw(w, g, m, v, name):
    r, c = w.shape
    tr = r
    for cand in (256, 128, 64, 32, 16, 8):
        if r % cand == 0:
            tr = cand
            break

    def body(w_ref, g_ref, m_ref, v_ref, d_ref, nm_ref, nv_ref):
        gg = g_ref[...]
        nm = ADAM_B1 * m_ref[...] + (1.0 - ADAM_B1) * gg
        nv = ADAM_B2 * v_ref[...] + (1.0 - ADAM_B2) * (gg * gg)
        m_hat = nm / (1.0 - ADAM_B1 ** ADAM_STEP)
        v_hat = nv / (1.0 - ADAM_B2 ** ADAM_STEP)
        d_ref[...] = -ADAM_LR * (m_hat / (jnp.sqrt(v_hat) + ADAM_EPS) + ADAM_WD * w_ref[...])
        nm_ref[...] = nm
        nv_ref[...] = nv

    spec = pl.BlockSpec((tr, c), lambda i: (i, 0))
    sd = _sds((r, c), F32)
    return _pcall(body, name=name, grid=(r // tr,), in_specs=[spec] * 4, out_specs=(spec,) * 3,
                  out_shape=(sd, sd, sd), compiler_params=_cparams("parallel"))(w, g, m, v)


MESH_ID = pl.DeviceIdType.MESH
HBM_SPEC = pl.BlockSpec(memory_space=pltpu.HBM)


def _place():
    return lax.axis_index("x"), lax.axis_index("y"), lax.axis_index("c")


def _other_chips(x, y):
    return [(1 - x, y), (x, 1 - y), (1 - x, 1 - y)]


def _remote(src, dst, send_sems, recv_sems, k, to):
    return pltpu.make_async_remote_copy(src_ref=src, dst_ref=dst, send_sem=send_sems.at[k], recv_sem=recv_sems.at[k],
                                        device_id=to, device_id_type=MESH_ID)


def _gather_weights(packed):
    rows, lanes = packed.shape
    half = rows // 2

    def body(src, out, send_sems, recv_sems, local_sem):
        x, y, c = _place()
        me = 2 * x + y
        sibling = (x, y, 1 - c)
        chips = _other_chips(x, y)

        def part(chip, core):
            return out.at[chip, pl.ds(core * half, half), :]

        local = pltpu.make_async_copy(src, out.at[me], local_sem)
        local.start()
        mine = src.at[pl.ds(c * half, half), :]
        first = [_remote(mine, part(me, c), send_sems, recv_sems, k, (cx, cy, c)) for k, (cx, cy) in enumerate(chips)]
        for cp in first:
            cp.start()
        passed = []
        for k, (cx, cy) in enumerate(chips):
            got = part(2 * cx + cy, c)
            _remote(got, got, send_sems, recv_sems, k, (x, y, c)).wait_recv()
            fwd = _remote(got, got, send_sems, recv_sems, 3 + k, sibling)
            fwd.start()
            passed.append(fwd)
        for k, (cx, cy) in enumerate(chips):
            got = part(2 * cx + cy, 1 - c)
            _remote(got, got, send_sems, recv_sems, 3 + k, (x, y, c)).wait_recv()
        for cp in first + passed:
            cp.wait_send()
        local.wait()

    return _pcall(
        body, name="gather_weights", in_specs=[HBM_SPEC], out_specs=HBM_SPEC,
        out_shape=_sds((N_CHIPS, rows, lanes), packed.dtype),
        scratch_shapes=[pltpu.SemaphoreType.DMA((6,)), pltpu.SemaphoreType.DMA((6,)), pltpu.SemaphoreType.DMA(())],
    )(packed)


def _reduce_cores(grads):
    nchip, rows, lanes = grads.shape
    half = rows // 2

    def body(g, mine, theirs, send_sems, recv_sems, local_sem):
        x, y, c = _place()
        local = pltpu.make_async_copy(g.at[:, pl.ds(c * half, half), :], mine, local_sem)
        local.start()
        cp = _remote(g.at[:, pl.ds((1 - c) * half, half), :], theirs, send_sems, recv_sems, 0, (x, y, 1 - c))
        cp.start()
        cp.wait()
        local.wait()

    sd = _sds((nchip, half, lanes), grads.dtype)
    return _pcall(
        body, name="reduce_cores", in_specs=[HBM_SPEC], out_specs=(HBM_SPEC, HBM_SPEC), out_shape=(sd, sd),
        scratch_shapes=[pltpu.SemaphoreType.DMA((1,)), pltpu.SemaphoreType.DMA((1,)), pltpu.SemaphoreType.DMA(())],
    )(grads)


def _scatter_chips(part):
    nchip, half, lanes = part.shape

    def body(p, out, send_sems, recv_sems, local_sem):
        x, y, c = _place()
        me = 2 * x + y
        chips = _other_chips(x, y)
        local = pltpu.make_async_copy(p.at[me], out.at[3], local_sem)
        local.start()
        sends = [_remote(p.at[2 * cx + cy], out.at[k], send_sems, recv_sems, k, (cx, cy, c))
                 for k, (cx, cy) in enumerate(chips)]
        for cp in sends:
            cp.start()
        for cp in sends:
            cp.wait()
        local.wait()

    return _pcall(
        body, name="scatter_chips", in_specs=[HBM_SPEC], out_specs=HBM_SPEC,
        out_shape=_sds((nchip, half, lanes), part.dtype),
        scratch_shapes=[pltpu.SemaphoreType.DMA((3,)), pltpu.SemaphoreType.DMA((3,)), pltpu.SemaphoreType.DMA(())],
    )(part)


def _share_cores(mine):
    half, lanes = mine.shape

    def body(src, out, send_sems, recv_sems, local_sem):
        x, y, c = _place()
        dst = out.at[pl.ds(c * half, half), :]
        local = pltpu.make_async_copy(src, dst, local_sem)
        local.start()
        cp = _remote(src, dst, send_sems, recv_sems, 0, (x, y, 1 - c))
        cp.start()
        cp.wait_send()
        theirs = out.at[pl.ds((1 - c) * half, half), :]
        _remote(theirs, theirs, send_sems, recv_sems, 0, (x, y, c)).wait_recv()
        local.wait()

    return _pcall(
        body, name="share_cores", in_specs=[HBM_SPEC], out_specs=HBM_SPEC,
        out_shape=_sds((2 * half, lanes), mine.dtype),
        scratch_shapes=[pltpu.SemaphoreType.DMA((1,)), pltpu.SemaphoreType.DMA((1,)), pltpu.SemaphoreType.DMA(())],
    )(mine)


def _sum_blocks(stacked, name, tm):
    n, rows, lanes = stacked.shape
    tm = min(tm, rows)

    def body(s_ref, o_ref):
        tot = s_ref[n - 1].astype(F32)
        for k in range(n - 1):
            tot = tot + s_ref[k].astype(F32)
        o_ref[...] = tot

    return _pcall(body, name=name, grid=(rows // tm,),
                  in_specs=[pl.BlockSpec((n, tm, lanes), lambda i: (0, i, 0))],
                  out_specs=pl.BlockSpec((tm, lanes), lambda i: (i, 0)), out_shape=_sds((rows, lanes), F32),
                  compiler_params=_cparams("parallel"))(stacked)


def _add_halves(a, b, tm):
    n, rows, lanes = a.shape
    tm = min(tm, rows)

    def body(a_ref, b_ref, o_ref):
        o_ref[...] = a_ref[...] + b_ref[...]

    spec = pl.BlockSpec((None, tm, lanes), lambda k, i: (k, i, 0))
    return _pcall(body, name="add_core_halves", grid=(n, rows // tm), in_specs=[spec, spec], out_specs=spec,
                  out_shape=_sds((n, rows, lanes), F32), compiler_params=_cparams("parallel", "parallel"))(a, b)


def _allreduce_small(part):
    rows, lanes = part.shape
    ndev = 8

    def body(src, tot, buf, send_sems, recv_sems):
        x, y, c = _place()
        me = 4 * x + 2 * y + c
        buf[me] = src[...]
        sends = []
        for k in range(1, ndev):
            peer = (x ^ (k >> 2), y ^ ((k >> 1) & 1), c ^ (k & 1))
            cp = _remote(src, buf.at[me], send_sems, recv_sems, k - 1, peer)
            cp.start()
            sends.append(cp)
        for k in range(1, ndev):
            theirs = buf.at[me ^ k]
            _remote(theirs, theirs, send_sems, recv_sems, k - 1, (x, y, c)).wait_recv()
        for cp in sends:
            cp.wait_send()
        acc = buf[0]
        for d in range(1, ndev):
            acc = acc + buf[d]
        tot[...] = acc

    vm = pl.BlockSpec(memory_space=pltpu.VMEM)
    return _pcall(
        body, name="allreduce_small", in_specs=[vm], out_specs=vm, out_shape=_sds((rows, lanes), F32),
        scratch_shapes=[pltpu.VMEM((ndev, rows, lanes), F32), pltpu.SemaphoreType.DMA((ndev - 1,)),
                        pltpu.SemaphoreType.DMA((ndev - 1,))],
    )(part)


def _big_rows():
    return [int(np.prod(shape)) // LANES for _, shape in BIG]


def _pack_big(blocks, dtype):
    parts = [blocks[name].reshape(blocks[name].shape[0], -1, LANES).astype(dtype) for name, _ in BIG]
    return jnp.concatenate(parts, axis=1)


def _unpack_big(packed):
    out, off = {}, 0
    for (name, shape), r in zip(BIG, _big_rows()):
        out[name] = packed[:, off:off + r].reshape((packed.shape[0],) + shape)
        off += r
    return out


def _pack_small(vals):
    parts = []
    for name, shape, r in SMALL:
        flat = vals[name].reshape(-1).astype(F32)
        parts.append(jnp.pad(flat, (0, r * LANES - flat.shape[0])).reshape(r, LANES))
    used = sum(r for _, _, r in SMALL)
    parts.append(jnp.zeros((SMALL_ROWS - used, LANES), F32))
    return jnp.concatenate(parts, axis=0)


def _unpack_small(packed):
    out, off = {}, 0
    for name, shape, r in SMALL:
        n = int(np.prod(shape))
        out[name] = packed[off:off + r].reshape(-1)[:n].reshape(shape)
        off += r
    return out


def _heads_major(a, nh):
    t = a.shape[0]
    return a.reshape(t, nh, a.shape[1] // nh).transpose(1, 0, 2)


def _tokens_major(a):
    nh, t, w = a.shape
    return a.transpose(1, 0, 2).reshape(t, nh * w)


def _local_step(x, target, small, wfull):
    t = x.shape[0]
    nh, hd = DIL_HEADS, DIL_HD
    w_in = wfull["w_in"].transpose(1, 0, 2).reshape(D_MODEL, -1)
    w_out = wfull["w_out"].reshape(D_MODEL, D_MODEL)
    w_qb, w_kvb = wfull["mla_w_q_b"], wfull["mla_w_kv_b"]
    grads_s, grads_b = {}, {}

    x1, ffn1_saved = _ffn_fwd(x, small["ffn1_norm"], wfull["ffn1_w_gate"], wfull["ffn1_w_up"],
                              wfull["ffn1_w_down"], "ffn1")
    hm = _rms_fwd(x1, small["mix_norm"], BF16, "mix_norm", 512)
    proj = _mm_simple("in_proj", hm, w_in, NN, F32)
    q_a, k_a, v_a = proj[:, :512], proj[:, 512:1024], proj[:, 1024:1536]
    cq, ckv, k_pe = proj[:, 1536:1792], proj[:, 1792:1920], proj[:, 1920:1984]

    q_h = _heads_major(q_a, nh).reshape(nh * t, hd)
    k_h = _heads_major(k_a, nh).reshape(nh * t, hd)
    v_h = _heads_major(v_a, nh).astype(BF16)
    qn = _rms_fwd(q_h, small["dil_q_norm"], BF16, "dil_q_norm", 2048).reshape(nh, t, hd)
    kn = _rms_fwd(k_h, small["dil_k_norm"], BF16, "dil_k_norm", 2048).reshape(nh, t, hd)
    bias = _bias_tiles(small["rel_bias"])
    branch_in, outs, lses = [], [], []
    for b, dil in enumerate(DIL_DILATIONS):
        qs, ks, vs = _to_sub(qn, dil), _to_sub(kn, dil), _to_sub(v_h, dil)
        o_b, lse_b = _dil_fwd(qs, ks, vs, bias[b], t // dil, f"dil_fwd_{dil}")
        branch_in.append((qs, ks, vs))
        outs.append(_from_sub(o_b, dil).reshape(nh * t, hd))
        lses.append(_from_sub(lse_b, dil).reshape(nh * t, 1))
    o_dil_h, lse_tot = _dil_merge(outs, lses, 2048)
    o_dil = _tokens_major(o_dil_h.reshape(nh, t, hd))

    mh = MLA_HEADS
    cos_t, sin_t = _rope_tables(t)
    cqn = _rms_fwd(cq, small["mla_q_a_norm"], BF16, "mla_q_a_norm", 512)
    ckvn = _rms_fwd(ckv, small["mla_kv_a_norm"], BF16, "mla_kv_a_norm", 512)
    tm = min(512, t)

    def head_proj(name, a, w, width):
        k = a.shape[1]
        return _mm(name, (mh, t // tm, 1),
                   [(a, pl.BlockSpec((tm, k), lambda h, i, r: (i, 0)), w, pl.BlockSpec((None, k, width), lambda h, i, r: (h, 0, 0)))],
                   NN, _sds((mh, t, width), F32), pl.BlockSpec((None, tm, width), lambda h, i, r: (h, i, 0)), (tm, width))

    q_raw = head_proj("mla_q_proj", cqn, w_qb, MLA_QK)
    kv_raw = head_proj("mla_kv_proj", ckvn, w_kvb, MLA_NOPE + MLA_V)
    k_raw = jnp.concatenate([kv_raw[:, :, :MLA_NOPE], jnp.broadcast_to(k_pe[None], (mh, t, MLA_ROPE))], axis=2)
    v_m = kv_raw[:, :, MLA_NOPE:].astype(BF16)
    q_raw2, k_raw2 = q_raw.reshape(mh * t, MLA_QK), k_raw.reshape(mh * t, MLA_QK)
    q_scale = MLA_QK ** -0.5
    q_m = _mla_qk_fwd(q_raw2, small["mla_q_norm"], cos_t, sin_t, q_scale, "mla_q_rope", 512).reshape(mh, t, MLA_QK)
    k_m = _mla_qk_fwd(k_raw2, small["mla_k_norm"], cos_t, sin_t, 1.0, "mla_k_rope", 512).reshape(mh, t, MLA_QK)
    o_mla_h, lse_m = _mla_fwd(q_m, k_m, v_m, 512, 512)
    o_mla = _tokens_major(o_mla_h)

    od = _rms_fwd(o_dil, small["out_norm_dil"], BF16, "out_norm_dil", 512)
    om = _rms_fwd(o_mla, small["out_norm_mla"], BF16, "out_norm_mla", 512)
    half_w = DIL_WIDTH
    row = pl.BlockSpec((tm, D_MODEL), lambda i, j, r: (i, 0))
    act_spec = pl.BlockSpec((tm, half_w), lambda i, j, r: (i, 0))
    x2 = _mm("out_proj", (t // tm, 1, 1),
             [(od, act_spec, w_out, pl.BlockSpec((half_w, D_MODEL), lambda i, j, r: (0, 0))),
              (om, act_spec, w_out, pl.BlockSpec((half_w, D_MODEL), lambda i, j, r: (1, 0)))],
             NN, _sds((t, D_MODEL), F32), row, (tm, D_MODEL), res=(x1, row))
    x3, ffn2_saved = _ffn_fwd(x2, small["ffn2_norm"], wfull["ffn2_w_gate"], wfull["ffn2_w_up"],
                              wfull["ffn2_w_down"], "ffn2")
    dy, loss = _loss_head(x3, target, 512)

    dx2, grads_s["ffn2_norm"], grads_b["ffn2_w_gate"], grads_b["ffn2_w_up"], grads_b["ffn2_w_down"] = _ffn_bwd(
        dy, x2, small["ffn2_norm"], wfull["ffn2_w_gate"], wfull["ffn2_w_up"], wfull["ffn2_w_down"], ffn2_saved, "ffn2")

    d_ocat = _mm_simple("out_proj_dx", dx2, w_out, NT, F32)
    tk = min(512, t)
    tok = pl.BlockSpec((tk, half_w), lambda c, j, r: (r, 0))
    dw_out_d = _mm_simple("out_proj_dw_dil", od, dx2, TN, F32)
    dw_out_m = _mm_simple("out_proj_dw_mla", om, dx2, TN, F32)
    grads_b["w_out"] = jnp.concatenate([dw_out_d, dw_out_m], axis=0).reshape(N_CHIPS, D_MODEL // N_CHIPS, D_MODEL)
    do_dil, grads_s["out_norm_dil"] = _rms_bwd([d_ocat[:, :half_w]], o_dil, small["out_norm_dil"], None, "out_norm_dil_bwd", 512)
    do_mla, grads_s["out_norm_mla"] = _rms_bwd([d_ocat[:, half_w:]], o_mla, small["out_norm_mla"], None, "out_norm_mla_bwd", 512)

    do_m = _heads_major(do_mla, mh)
    dl_m = _rowdot(do_m.reshape(mh * t, MLA_V), o_mla_h.reshape(mh * t, MLA_V), "mla_delta", 2048).reshape(mh, t, 1)
    dq_m = _mla_bwd_dq(q_m, k_m, v_m, do_m, lse_m, dl_m, 512, 512)
    dk_m, dv_m = _mla_bwd_dkv(q_m, k_m, v_m, do_m, lse_m.reshape(mh, 1, t), dl_m.reshape(mh, 1, t), 512, 512)
    dq_raw, grads_s["mla_q_norm"] = _mla_qk_bwd(dq_m.reshape(mh * t, MLA_QK), q_raw2, small["mla_q_norm"],
                                                 cos_t, sin_t, q_scale, "mla_q_rope_bwd", 512)
    dk_raw, grads_s["mla_k_norm"] = _mla_qk_bwd(dk_m.reshape(mh * t, MLA_QK), k_raw2, small["mla_k_norm"],
                                                 cos_t, sin_t, 1.0, "mla_k_rope_bwd", 512)
    dq_raw = dq_raw.reshape(mh, t, MLA_QK)
    dk_raw = dk_raw.reshape(mh, t, MLA_QK)
    dkv_raw = jnp.concatenate([dk_raw[:, :, :MLA_NOPE], dv_m], axis=2)
    dk_pe_h = dk_raw[:, :, MLA_NOPE:]

    def head_proj_dx(name, d, w):
        width, k = d.shape[2], w.shape[1]
        return _mm(name, (t // tm, 1, mh),
                   [(d, pl.BlockSpec((None, tm, width), lambda i, j, r: (r, i, 0)), w, pl.BlockSpec((None, k, width), lambda i, j, r: (r, 0, 0)))],
                   NT, _sds((t, k), F32), pl.BlockSpec((tm, k), lambda i, j, r: (i, 0)), (tm, k))

    def head_proj_dw(name, a, d):
        width, k = d.shape[2], a.shape[1]
        return _mm(name, (mh, 1, t // tk),
                   [(a, pl.BlockSpec((tk, k), lambda h, j, r: (r, 0)), d, pl.BlockSpec((None, tk, width), lambda h, j, r: (h, r, 0)))],
                   TN, _sds((mh, k, width), F32), pl.BlockSpec((None, k, width), lambda h, j, r: (h, 0, 0)), (k, width))

    d_cqn = head_proj_dx("mla_q_proj_dx", dq_raw, w_qb)
    d_ckvn = head_proj_dx("mla_kv_proj_dx", dkv_raw, w_kvb)
    grads_b["mla_w_q_b"] = head_proj_dw("mla_q_proj_dw", cqn, dq_raw)
    grads_b["mla_w_kv_b"] = head_proj_dw("mla_kv_proj_dw", ckvn, dkv_raw)
    d_cq, grads_s["mla_q_a_norm"] = _rms_bwd([d_cqn], cq, small["mla_q_a_norm"], None, "mla_q_a_norm_bwd", 512)
    d_ckv, grads_s["mla_kv_a_norm"] = _rms_bwd([d_ckvn], ckv, small["mla_kv_a_norm"], None, "mla_kv_a_norm_bwd", 512)
    d_kpe = _sum_blocks(dk_pe_h.reshape(mh, t * MLA_ROPE // LANES, LANES), "mla_kpe_sum", 1024).reshape(t, MLA_ROPE)

    do_h = _heads_major(do_dil, nh)
    dl_d = _rowdot(do_h.reshape(nh * t, hd), o_dil_h, "dil_delta", 2048).reshape(nh, t, 1)
    lse_t = lse_tot.reshape(nh, t, 1)
    do_hb = do_h.astype(BF16)
    dqs, dks, dvs, dtiles = [], [], [], []
    for b, dil in enumerate(DIL_DILATIONS):
        qs, ks, vs = branch_in[b]
        dq_b, dk_b, dv_b, db_b = _dil_bwd(qs, ks, vs, _to_sub(do_hb, dil), _to_sub(lse_t, dil), _to_sub(dl_d, dil),
                                          bias[b], t // dil, f"dil_bwd_{dil}")
        dqs.append(_from_sub(dq_b, dil).reshape(nh * t, hd))
        dks.append(_from_sub(dk_b, dil).reshape(nh * t, hd))
        dvs.append(_from_sub(dv_b, dil).reshape(nh * t, hd))
        dtiles.append(db_b)
    grads_s["rel_bias"] = _bias_grad(jnp.stack(dtiles))
    dq_a_h, grads_s["dil_q_norm"] = _rms_bwd(dqs, q_h, small["dil_q_norm"], None, "dil_q_norm_bwd", 2048)
    dk_a_h, grads_s["dil_k_norm"] = _rms_bwd(dks, k_h, small["dil_k_norm"], None, "dil_k_norm_bwd", 2048)
    dv_a_h = _add3(dvs[0], dvs[1], dvs[2], "dil_dv_sum", 2048)
    dproj = jnp.concatenate([_tokens_major(dq_a_h.reshape(nh, t, hd)), _tokens_major(dk_a_h.reshape(nh, t, hd)),
                             _tokens_major(dv_a_h.reshape(nh, t, hd)), d_cq, d_ckv, d_kpe], axis=1)

    d_hm = _mm_simple("in_proj_dx", dproj, w_in, NT, F32)
    dw_in = _mm_simple("in_proj_dw", hm, dproj, TN, F32)
    grads_b["w_in"] = dw_in.reshape(D_MODEL, N_CHIPS, -1).transpose(1, 0, 2)
    dx1, grads_s["mix_norm"] = _rms_bwd([d_hm], x1, small["mix_norm"], dx2, "mix_norm_bwd", 512)
    dx, grads_s["ffn1_norm"], grads_b["ffn1_w_gate"], grads_b["ffn1_w_up"], grads_b["ffn1_w_down"] = _ffn_bwd(
        dx1, x, small["ffn1_norm"], wfull["ffn1_w_gate"], wfull["ffn1_w_up"], wfull["ffn1_w_down"], ffn1_saved, "ffn1")
    return loss, dx, grads_s, grads_b


def kernel(x, ffn1_norm, ffn1_w_gate, ffn1_w_up, ffn1_w_down, mix_norm, w_in, dil_q_norm, dil_k_norm, rel_bias, mla_q_a_norm, mla_w_q_b, mla_kv_a_norm, mla_w_kv_b, mla_q_norm, mla_k_norm, out_norm_dil, out_norm_mla, w_out, ffn2_norm, ffn2_w_gate, ffn2_w_up, ffn2_w_down, loss_target, m_ffn1_norm, m_ffn1_w_gate, m_ffn1_w_up, m_ffn1_w_down, m_mix_norm, m_w_in, m_dil_q_norm, m_dil_k_norm, m_rel_bias, m_mla_q_a_norm, m_mla_w_q_b, m_mla_kv_a_norm, m_mla_w_kv_b, m_mla_q_norm, m_mla_k_norm, m_out_norm_dil, m_out_norm_mla, m_w_out, m_ffn2_norm, m_ffn2_w_gate, m_ffn2_w_up, m_ffn2_w_down, v_ffn1_norm, v_ffn1_w_gate, v_ffn1_w_up, v_ffn1_w_down, v_mix_norm, v_w_in, v_dil_q_norm, v_dil_k_norm, v_rel_bias, v_mla_q_a_norm, v_mla_w_q_b, v_mla_kv_a_norm, v_mla_w_kv_b, v_mla_q_norm, v_mla_k_norm, v_out_norm_dil, v_out_norm_mla, v_w_out, v_ffn2_norm, v_ffn2_w_gate, v_ffn2_w_up, v_ffn2_w_down):
    given = dict(locals())
    big_names = [name for name, _ in BIG]
    small_names = [name for name, _, _ in SMALL]

    mine = _pack_big({n: given[n] for n in big_names}, BF16)[0]
    wfull = _unpack_big(_gather_weights(mine))
    small = {n: given[n] for n in small_names}

    loss, dx, grads_s, grads_b = _local_step(x[0], loss_target[0], small, wfull)
    loss = lax.psum(loss[0, 0], ("x", "y", "c"))

    packed = _pack_big(grads_b, F32).reshape(N_CHIPS, -1, LANES)
    own, theirs = _reduce_cores(packed)
    chip_part = _add_halves(own, theirs, 1264)
    half_sum = _sum_blocks(_scatter_chips(chip_part), "sum_chip_partials", 1264)
    g_big = _unpack_big(_share_cores(half_sum)[None])
    g_small = _unpack_small(_allreduce_small(_pack_small(grads_s)))

    grad, delta, new_m, new_v = {}, {}, {}, {}
    for name, shape in BIG:
        g2 = g_big[name].reshape(shape)
        d_, m_, v_ = _adamw(given[name].reshape(shape), g2, given["m_" + name].reshape(shape),
                            given["v_" + name].reshape(shape), f"adamw_{name}")
        full = given[name].shape
        grad[name], delta[name], new_m[name], new_v[name] = (a.reshape(full) for a in (g2, d_, m_, v_))
    ps = {k: _pack_small({n: given[pre + n] for n in small_names}) for k, pre in (("w", ""), ("m", "m_"), ("v", "v_"))}
    gs_packed = _pack_small(g_small)
    d_s, m_s, v_s = (_unpack_small(a) for a in _adamw(ps["w"], gs_packed, ps["m"], ps["v"], "adamw_small"))
    for name in small_names:
        grad[name], delta[name], new_m[name], new_v[name] = g_small[name], d_s[name], m_s[name], v_s[name]

    return (loss, dx[None], *[grad[n] for n in WEIGHTS], *[delta[n] for n in WEIGHTS],
            *[new_m[n] for n in WEIGHTS], *[new_v[n] for n in WEIGHTS])
```

```python
import functools

import numpy as np
import jax
import jax.numpy as jnp
from jax import lax
from jax.experimental import pallas as pl
from jax.experimental.pallas import tpu as pltpu

F32 = jnp.float32
BF16 = jnp.bfloat16

D_MODEL = 1024
D_FF = 2816
N_CHIPS = 4
DIL_HEADS = 8
DIL_HD = 64
DIL_WIDTH = 512
DIL_DILATIONS = (1, 4, 16)
DIL_W = 128
QB = 128
MLA_HEADS = 4
MLA_NOPE = 128
MLA_ROPE = 64
MLA_QK = 192
MLA_V = 128
MLA_Q_RANK = 256
MLA_KV_RANK = 128
ROPE_BASE = 10000.0
REL_BUCKETS = 32
REL_MAX_DIST = 2048
FFN_RESID = 0.5
EPS = 1e-6
NEG = -1e30
LANES = 128

ADAM_LR = 0.001
ADAM_B1 = 0.9
ADAM_B2 = 0.999
ADAM_EPS = 1e-08
ADAM_WD = 0.01
ADAM_STEP = 10

NT = (((1,), (1,)), ((), ()))
NN = (((1,), (0,)), ((), ()))
TN = (((0,), (0,)), ((), ()))

BIG = (
    ("ffn1_w_gate", (D_MODEL, D_FF // N_CHIPS)),
    ("ffn1_w_up", (D_MODEL, D_FF // N_CHIPS)),
    ("ffn1_w_down", (D_FF // N_CHIPS, D_MODEL)),
    ("w_in", (D_MODEL, 1984 // N_CHIPS)),
    ("mla_w_q_b", (MLA_Q_RANK, MLA_QK)),
    ("mla_w_kv_b", (MLA_KV_RANK, MLA_NOPE + MLA_V)),
    ("w_out", (D_MODEL // N_CHIPS, D_MODEL)),
    ("ffn2_w_gate", (D_MODEL, D_FF // N_CHIPS)),
    ("ffn2_w_up", (D_MODEL, D_FF // N_CHIPS)),
    ("ffn2_w_down", (D_FF // N_CHIPS, D_MODEL)),
)
SMALL = (
    ("ffn1_norm", (1, 1024), 8), ("mix_norm", (1, 1024), 8), ("dil_q_norm", (1, 64), 1),
    ("dil_k_norm", (1, 64), 1), ("rel_bias", (8, 32), 2), ("mla_q_a_norm", (1, 256), 2),
    ("mla_kv_a_norm", (1, 128), 1), ("mla_q_norm", (1, 192), 2), ("mla_k_norm", (1, 192), 2),
    ("out_norm_dil", (1, 512), 4), ("out_norm_mla", (1, 512), 4), ("ffn2_norm", (1, 1024), 8),
)
SMALL_ROWS = 48
WEIGHTS = ("ffn1_norm", "ffn1_w_gate", "ffn1_w_up", "ffn1_w_down", "mix_norm", "w_in", "dil_q_norm",
           "dil_k_norm", "rel_bias", "mla_q_a_norm", "mla_w_q_b", "mla_kv_a_norm", "mla_w_kv_b",
           "mla_q_norm", "mla_k_norm", "out_norm_dil", "out_norm_mla", "w_out", "ffn2_norm",
           "ffn2_w_gate", "ffn2_w_up", "ffn2_w_down")


def _pcall(body, **kw):
    return pl.pallas_call(body, **kw)


def _cparams(*sem):
    return pltpu.CompilerParams(dimension_semantics=sem)


def _sds(shape, dtype):
    return jax.ShapeDtypeStruct(shape, dtype)


def _dot(a, b, dn):
    return lax.dot_general(a, b, dn, preferred_element_type=F32)


def _rms_fwd(x, g, out_dtype, name, tm):
    n, d = x.shape
    tm = min(tm, n)

    def body(x_ref, g_ref, o_ref):
        xf = x_ref[...].astype(F32)
        r = lax.rsqrt(jnp.mean(xf * xf, axis=-1, keepdims=True) + EPS)
        o_ref[...] = (xf * r * g_ref[...]).astype(o_ref.dtype)

    return _pcall(
        body, name=name, grid=(n // tm,),
        in_specs=[pl.BlockSpec((tm, d), lambda i: (i, 0)), pl.BlockSpec((1, d), lambda i: (0, 0))],
        out_specs=pl.BlockSpec((tm, d), lambda i: (i, 0)),
        out_shape=_sds((n, d), out_dtype), compiler_params=_cparams("parallel"))(x, g)


def _rms_bwd(dys, x, g, res, name, tm):
    n, d = x.shape
    tm = min(tm, n)
    nd = len(dys)
    has_res = res is not None

    def body(*refs):
        dy_refs = refs[:nd]
        x_ref, g_ref = refs[nd], refs[nd + 1]
        res_ref = refs[nd + 2] if has_res else None
        dx_ref, dg_ref = refs[-2], refs[-1]
        dy = dy_refs[0][...].astype(F32)
        for r_ in dy_refs[1:]:
            dy = dy + r_[...].astype(F32)
        xf = x_ref[...].astype(F32)
        r = lax.rsqrt(jnp.mean(xf * xf, axis=-1, keepdims=True) + EPS)
        xh = xf * r
        dxh = dy * g_ref[...]
        dx = r * (dxh - xh * jnp.mean(dxh * xh, axis=-1, keepdims=True))
        if has_res:
            dx = dx + res_ref[...]
        dx_ref[...] = dx

        @pl.when(pl.program_id(0) == 0)
        def _():
            dg_ref[...] = jnp.zeros_like(dg_ref)

        dg_ref[...] += jnp.sum(dy * xh, axis=0, keepdims=True)

    row = pl.BlockSpec((tm, d), lambda i: (i, 0))
    vec = pl.BlockSpec((1, d), lambda i: (0, 0))
    ins = list(dys) + [x, g] + ([res] if has_res else [])
    return _pcall(
        body, name=name, grid=(n // tm,),
        in_specs=[row] * nd + [row, vec] + ([row] if has_res else []),
        out_specs=(row, vec),
        out_shape=(_sds((n, d), F32), _sds((1, d), F32)),
        compiler_params=_cparams("arbitrary"))(*ins)


def _mm(name, grid, pairs, dn, out_shape, out_spec, acc_shape, res=None, scale=1.0):
    npairs = len(pairs)
    nred = grid[2]
    has_res = res is not None

    def body(*refs):
        ab = refs[:2 * npairs]
        res_ref = refs[2 * npairs] if has_res else None
        o_ref = refs[2 * npairs + int(has_res)]
        acc_ref = refs[-1] if nred > 1 else None
        tot = None
        for p in range(npairs):
            d = _dot(ab[2 * p][...].astype(BF16), ab[2 * p + 1][...].astype(BF16), dn)
            tot = d if tot is None else tot + d

        def finish(v):
            if scale != 1.0:
                v = v * scale
            if has_res:
                v = res_ref[...] + v
            o_ref[...] = v.astype(o_ref.dtype)

        if nred == 1:
            finish(tot)
        else:
            r = pl.program_id(2)

            @pl.when(r == 0)
            def _():
                acc_ref[...] = tot

            @pl.when(r > 0)
            def _():
                acc_ref[...] += tot

            @pl.when(r == nred - 1)
            def _():
                finish(acc_ref[...])

    ins, specs = [], []
    for a, a_spec, b, b_spec in pairs:
        ins += [a, b]
        specs += [a_spec, b_spec]
    if has_res:
        ins.append(res[0])
        specs.append(res[1])
    return _pcall(
        body, name=name, grid=grid, in_specs=specs, out_specs=out_spec, out_shape=out_shape,
        scratch_shapes=[pltpu.VMEM(acc_shape, F32)] if nred > 1 else [],
        compiler_params=_cparams("parallel", "parallel", "arbitrary"))(*ins)


def _ffn_up(h, wg, wu, name, tm):
    t, d = h.shape
    nc, _, fs = wg.shape
    tm = min(tm, t)

    def body(h_ref, wg_ref, wu_ref, g_ref, u_ref, a_ref):
        hh = h_ref[...]
        gate = _dot(hh, wg_ref[...], NN)
        up = _dot(hh, wu_ref[...], NN)
        g_ref[...] = gate.astype(BF16)
        u_ref[...] = up.astype(BF16)
        a_ref[...] = (gate * jax.nn.sigmoid(gate) * up).astype(BF16)

    wspec = pl.BlockSpec((None, d, fs), lambda c, i: (c, 0, 0))
    ospec = pl.BlockSpec((None, tm, fs), lambda c, i: (c, i, 0))
    osd = _sds((nc, t, fs), BF16)
    return _pcall(
        body, name=name, grid=(nc, t // tm),
        in_specs=[pl.BlockSpec((tm, d), lambda c, i: (i, 0)), wspec, wspec],
        out_specs=(ospec, ospec, ospec), out_shape=(osd, osd, osd),
        compiler_params=_cparams("parallel", "parallel"))(h, wg, wu)


def _ffn_dact(dy, wd, gate, up, name, tm):
    t, d = dy.shape
    nc, fs, _ = wd.shape
    tm = min(tm, t)

    def body(dy_ref, wd_ref, g_ref, u_ref, dg_ref, du_ref):
        da = _dot(dy_ref[...].astype(BF16), wd_ref[...], NT) * FFN_RESID
        gate = g_ref[...].astype(F32)
        up = u_ref[...].astype(F32)
        sig = jax.nn.sigmoid(gate)
        dg_ref[...] = (da * up * (sig * (1.0 + gate * (1.0 - sig)))).astype(BF16)
        du_ref[...] = (da * (gate * sig)).astype(BF16)

    cspec = pl.BlockSpec((None, tm, fs), lambda c, i: (c, i, 0))
    osd = _sds((nc, t, fs), BF16)
    return _pcall(
        body, name=name, grid=(nc, t // tm),
        in_specs=[pl.BlockSpec((tm, d), lambda c, i: (i, 0)),
                  pl.BlockSpec((None, fs, d), lambda c, i: (c, 0, 0)), cspec, cspec],
        out_specs=(cspec, cspec), out_shape=(osd, osd),
        compiler_params=_cparams("parallel", "parallel"))(dy, wd, gate, up)


def _ffn_fwd(x, g, wg, wu, wd, tag):
    t = x.shape[0]
    nc, _, fs = wg.shape
    tm = min(512, t)
    h = _rms_fwd(x, g, BF16, f"{tag}_norm", 512)
    gate, up, act = _ffn_up(h, wg, wu, f"{tag}_up", 512)
    y = _mm(f"{tag}_down", (t // tm, 1, nc),
            [(act, pl.BlockSpec((None, tm, fs), lambda i, j, r: (r, i, 0)),
              wd, pl.BlockSpec((None, fs, D_MODEL), lambda i, j, r: (r, 0, 0)))],
            NN, _sds((t, D_MODEL), F32), pl.BlockSpec((tm, D_MODEL), lambda i, j, r: (i, 0)),
            (tm, D_MODEL), res=(x, pl.BlockSpec((tm, D_MODEL), lambda i, j, r: (i, 0))), scale=FFN_RESID)
    return y, (h, gate, up, act)


def _ffn_bwd(dy, x, g, wg, wu, wd, saved, tag):
    h, gate, up, act = saved
    t = x.shape[0]
    nc, _, fs = wg.shape
    tm = min(512, t)
    tk = min(512, t)
    dgate, dup = _ffn_dact(dy, wd, gate, up, f"{tag}_dact", 512)
    tok_c = pl.BlockSpec((None, tk, fs), lambda c, j, r: (c, r, 0))
    tok_d = pl.BlockSpec((tk, D_MODEL), lambda c, j, r: (r, 0))
    dwd = _mm(f"{tag}_dwd", (nc, 1, t // tk), [(act, tok_c, dy, tok_d)], TN,
              _sds((nc, fs, D_MODEL), F32), pl.BlockSpec((None, fs, D_MODEL), lambda c, j, r: (c, 0, 0)),
              (fs, D_MODEL), scale=FFN_RESID)
    wout = pl.BlockSpec((None, D_MODEL, fs), lambda c, j, r: (c, 0, 0))
    dwg = _mm(f"{tag}_dwg", (nc, 1, t // tk), [(h, tok_d, dgate, tok_c)], TN,
              _sds((nc, D_MODEL, fs), F32), wout, (D_MODEL, fs))
    dwu = _mm(f"{tag}_dwu", (nc, 1, t // tk), [(h, tok_d, dup, tok_c)], TN,
              _sds((nc, D_MODEL, fs), F32), wout, (D_MODEL, fs))
    a_spec = pl.BlockSpec((None, tm, fs), lambda i, j, r: (r, i, 0))
    w_spec = pl.BlockSpec((None, D_MODEL, fs), lambda i, j, r: (r, 0, 0))
    dh = _mm(f"{tag}_dh", (t // tm, 1, nc), [(dgate, a_spec, wg, w_spec), (dup, a_spec, wu, w_spec)], NT,
             _sds((t, D_MODEL), F32), pl.BlockSpec((tm, D_MODEL), lambda i, j, r: (i, 0)), (tm, D_MODEL))
    dx, dg = _rms_bwd([dh], x, g, dy, f"{tag}_dnorm", 512)
    return dx, dg, dwg, dwu, dwd


def _mm_simple(name, a, b, dn, out_dtype, tm=512, tk=512, res=None, scale=1.0):
    if dn == TN:
        k, m = a.shape
        n = b.shape[1]
        tk = min(tk, k)
        return _mm(name, (1, 1, k // tk),
                   [(a, pl.BlockSpec((tk, m), lambda i, j, r: (r, 0)), b, pl.BlockSpec((tk, n), lambda i, j, r: (r, 0)))],
                   TN, _sds((m, n), out_dtype), pl.BlockSpec((m, n), lambda i, j, r: (0, 0)), (m, n), scale=scale)
    m, k = a.shape
    n = b.shape[1] if dn == NN else b.shape[0]
    tm = min(tm, m)
    row = pl.BlockSpec((tm, n), lambda i, j, r: (i, 0))
    return _mm(name, (m // tm, 1, 1),
               [(a, pl.BlockSpec((tm, k), lambda i, j, r: (i, 0)), b, pl.BlockSpec(b.shape, lambda i, j, r: (0, 0)))],
               dn, _sds((m, n), out_dtype), row, (tm, n), res=None if res is None else (res, row), scale=scale)


def _t5_bucket(dist):
    max_exact = REL_BUCKETS // 2
    d = np.maximum(dist, 1).astype(np.float32)
    large = max_exact + (np.log(d / max_exact) / np.log(REL_MAX_DIST / max_exact)
                         * (REL_BUCKETS - max_exact)).astype(np.int32)
    large = np.minimum(large, REL_BUCKETS - 1)
    return np.where(dist < max_exact, dist, large).astype(np.int32)


def _bucket_tiles():
    i = np.arange(QB)[:, None]
    j = np.arange(QB + DIL_W)[None, :]
    delta = np.clip(i + DIL_W - j, 0, None)
    return np.stack([_t5_bucket(delta * dil) for dil in DIL_DILATIONS]).astype(np.int32)


def _bias_tiles(rel_bias):
    buckets = jnp.asarray(_bucket_tiles())

    def body(rb_ref, bk_ref, o_ref):
        bk = bk_ref[...]
        for h in range(DIL_HEADS):
            def pick(b, tile):
                return jnp.where(bk == b, rb_ref[h, b], tile)

            o_ref[h] = lax.fori_loop(0, REL_BUCKETS, pick, jnp.zeros((QB, QB + DIL_W), F32))

    return _pcall(
        body, name="dil_bias_tiles", grid=(3,),
        in_specs=[pl.BlockSpec(memory_space=pltpu.SMEM),
                  pl.BlockSpec((None, QB, QB + DIL_W), lambda b: (b, 0, 0))],
        out_specs=pl.BlockSpec((None, DIL_HEADS, QB, QB + DIL_W), lambda b: (b, 0, 0, 0)),
        out_shape=_sds((3, DIL_HEADS, QB, QB + DIL_W), F32),
        compiler_params=_cparams("parallel"))(rel_bias, buckets)


def _bias_grad(dtiles):
    buckets = jnp.asarray(_bucket_tiles())

    def body(dt_ref, bk_ref, o_ref):
        for h in range(DIL_HEADS):
            def one(b, carry):
                tot = jnp.zeros((), F32)
                for br in range(3):
                    tot = tot + jnp.sum(jnp.where(bk_ref[br] == b, dt_ref[br, h], 0.0))
                o_ref[h, b] = tot
                return carry

            lax.fori_loop(0, REL_BUCKETS, one, 0)

    return _pcall(
        body, name="dil_bias_grad",
        in_specs=[pl.BlockSpec(memory_space=pltpu.VMEM), pl.BlockSpec(memory_space=pltpu.VMEM)],
        out_specs=pl.BlockSpec(memory_space=pltpu.SMEM),
        out_shape=_sds((DIL_HEADS, REL_BUCKETS), F32))(dtiles, buckets)


def _dil_masks(has_prev):
    ii = lax.broadcasted_iota(jnp.int32, (QB, QB), 0)
    jj = lax.broadcasted_iota(jnp.int32, (QB, QB), 1)
    return jj <= ii, jj >= ii + jnp.where(has_prev, 0, QB)


def _dil_fwd(q, k, v, bias, cls_len, name):
    nh, t, hd = q.shape
    nb = t // QB
    per = cls_len // QB
    scale = hd ** -0.5

    def body(q_ref, kc_ref, kp_ref, vc_ref, vp_ref, b_ref, o_ref, lse_ref):
        n = pl.program_id(0)
        cur_ok, prev_ok = _dil_masks((n % per) != 0)
        for h in range(nh):
            qh = q_ref[h]
            sc = _dot(qh, kc_ref[h], NT) * scale + b_ref[h, :, QB:]
            sp = _dot(qh, kp_ref[h], NT) * scale + b_ref[h, :, :QB]
            sc = jnp.where(cur_ok, sc, NEG)
            sp = jnp.where(prev_ok, sp, NEG)
            m = jnp.maximum(jnp.max(sc, axis=-1, keepdims=True), jnp.max(sp, axis=-1, keepdims=True))
            pc = jnp.exp(sc - m)
            pp = jnp.exp(sp - m)
            den = jnp.sum(pc, axis=-1, keepdims=True) + jnp.sum(pp, axis=-1, keepdims=True)
            o = _dot(pc.astype(BF16), vc_ref[h], NN) + _dot(pp.astype(BF16), vp_ref[h], NN)
            o_ref[h] = o / den
            lse_ref[h] = m + jnp.log(den)

    cur = pl.BlockSpec((nh, QB, hd), lambda n: (0, n, 0))
    prev = pl.BlockSpec((nh, QB, hd), lambda n: (0, jnp.maximum(n - 1, 0), 0))
    return _pcall(
        body, name=name, grid=(nb,),
        in_specs=[cur, cur, prev, cur, prev, pl.BlockSpec((nh, QB, QB + DIL_W), lambda n: (0, 0, 0))],
        out_specs=(cur, pl.BlockSpec((nh, QB, 1), lambda n: (0, n, 0))),
        out_shape=(_sds((nh, t, hd), F32), _sds((nh, t, 1), F32)),
        compiler_params=_cparams("parallel"))(q, k, k, v, v, bias)


def _dil_bwd(q, k, v, do, lse, dl, bias, cls_len, name):
    nh, t, hd = q.shape
    nb = t // QB
    per = cls_len // QB
    scale = hd ** -0.5

    def body(qc_ref, qn_ref, doc_ref, don_ref, lc_ref, ln_ref, dc_ref, dn_ref, k_ref, v_ref, b_ref,
             dq_ref, dk_ref, dv_ref, db_ref, carry):
        n = pl.program_id(0)
        nxt = n + 1
        cur_ok, prev_ok = _dil_masks((nxt < nb) & ((nxt % per) != 0))

        @pl.when(n == 0)
        def _():
            db_ref[...] = jnp.zeros_like(db_ref)
            carry[...] = jnp.zeros_like(carry)

        for h in range(nh):
            kh = k_ref[h]
            vh = v_ref[h]
            q1, q2 = qc_ref[h], qn_ref[h]
            do1, do2 = doc_ref[h], don_ref[h]
            s1 = jnp.where(cur_ok, _dot(q1, kh, NT) * scale + b_ref[h, :, QB:], NEG)
            s2 = jnp.where(prev_ok, _dot(q2, kh, NT) * scale + b_ref[h, :, :QB], NEG)
            p1 = jnp.exp(s1 - lc_ref[h])
            p2 = jnp.exp(s2 - ln_ref[h])
            ds1 = p1 * (_dot(do1, vh, NT) - dc_ref[h])
            ds2 = p2 * (_dot(do2, vh, NT) - dn_ref[h])
            ds1b = ds1.astype(BF16)
            ds2b = ds2.astype(BF16)
            dq_ref[h] = carry[h] + _dot(ds1b, kh, NN) * scale
            carry[h] = _dot(ds2b, kh, NN) * scale
            dk_ref[h] = (_dot(ds1b, q1, TN) + _dot(ds2b, q2, TN)) * scale
            dv_ref[h] = _dot(p1.astype(BF16), do1, TN) + _dot(p2.astype(BF16), do2, TN)
            db_ref[h, :, QB:] += ds1
            db_ref[h, :, :QB] += ds2

    def cur(w):
        return pl.BlockSpec((nh, QB, w), lambda n: (0, n, 0))

    def nxt(w):
        return pl.BlockSpec((nh, QB, w), lambda n: (0, jnp.minimum(n + 1, nb - 1), 0))

    tile = pl.BlockSpec((nh, QB, QB + DIL_W), lambda n: (0, 0, 0))
    o3 = _sds((nh, t, hd), F32)
    return _pcall(
        body, name=name, grid=(nb,),
        in_specs=[cur(hd), nxt(hd), cur(hd), nxt(hd), cur(1), nxt(1), cur(1), nxt(1), cur(hd), cur(hd), tile],
        out_specs=(cur(hd), cur(hd), cur(hd), tile),
        out_shape=(o3, o3, o3, _sds((nh, QB, QB + DIL_W), F32)),
        scratch_shapes=[pltpu.VMEM((nh, QB, hd), F32)],
        compiler_params=_cparams("arbitrary"))(q, q, do, do, lse, lse, dl, dl, k, v, bias)


def _dil_merge(outs, lses, tm):
    n, hd = outs[0].shape
    tm = min(tm, n)

    def body(o0, o1, o2, l0, l1, l2, o_ref, l_ref):
        a0, a1, a2 = l0[...], l1[...], l2[...]
        m = jnp.maximum(jnp.maximum(a0, a1), a2)
        e0, e1, e2 = jnp.exp(a0 - m), jnp.exp(a1 - m), jnp.exp(a2 - m)
        den = e0 + e1 + e2
        o_ref[...] = (e0 * o0[...] + e1 * o1[...] + e2 * o2[...]) / den
        l_ref[...] = m + jnp.log(den)

    ospec = pl.BlockSpec((tm, hd), lambda i: (i, 0))
    lspec = pl.BlockSpec((tm, 1), lambda i: (i, 0))
    return _pcall(
        body, name="dil_merge", grid=(n // tm,),
        in_specs=[ospec] * 3 + [lspec] * 3, out_specs=(ospec, lspec),
        out_shape=(_sds((n, hd), F32), _sds((n, 1), F32)),
        compiler_params=_cparams("parallel"))(*outs, *lses)


def _rowdot(a, b, name, tm):
    n, d = a.shape
    tm = min(tm, n)

    def body(a_ref, b_ref, o_ref):
        o_ref[...] = jnp.sum(a_ref[...].astype(F32) * b_ref[...].astype(F32), axis=-1, keepdims=True)

    spec = pl.BlockSpec((tm, d), lambda i: (i, 0))
    return _pcall(body, name=name, grid=(n // tm,), in_specs=[spec, spec],
                  out_specs=pl.BlockSpec((tm, 1), lambda i: (i, 0)), out_shape=_sds((n, 1), F32),
                  compiler_params=_cparams("parallel"))(a, b)


def _add3(a, b, c, name, tm):
    n, d = a.shape
    tm = min(tm, n)

    def body(a_ref, b_ref, c_ref, o_ref):
        o_ref[...] = a_ref[...] + b_ref[...] + c_ref[...]

    spec = pl.BlockSpec((tm, d), lambda i: (i, 0))
    return _pcall(body, name=name, grid=(n // tm,), in_specs=[spec] * 3, out_specs=spec,
                  out_shape=_sds((n, d), F32), compiler_params=_cparams("parallel"))(a, b, c)


def _to_sub(a, dil):
    nh, t, w = a.shape
    if dil == 1:
        return a
    return a.reshape(nh, t // dil, dil, w).transpose(0, 2, 1, 3).reshape(nh, t, w)


def _from_sub(a, dil):
    nh, t, w = a.shape
    if dil == 1:
        return a
    return a.reshape(nh, dil, t // dil, w).transpose(0, 2, 1, 3).reshape(nh, t, w)


def _rope_tables(t):
    inv = ROPE_BASE ** (-np.arange(0, MLA_ROPE, 2, dtype=np.float64) / MLA_ROPE)
    ang = np.arange(t, dtype=np.float64)[:, None] * inv[None, :]
    cos, sin = np.cos(ang), np.sin(ang)
    return (jnp.asarray(np.concatenate([cos, cos], 1), F32), jnp.asarray(np.concatenate([-sin, sin], 1), F32))


def _half_swap():
    p = np.zeros((MLA_ROPE, MLA_ROPE), np.float32)
    half = MLA_ROPE // 2
    for i in range(MLA_ROPE):
        p[(i + half) % MLA_ROPE, i] = 1.0
    return jnp.asarray(p)


def _mla_qk_fwd(x, g, cos_t, sin_t, scale, name, tm):
    n, d = x.shape
    t = cos_t.shape[0]
    tm = min(tm, t)
    nt = t // tm
    swap = _half_swap()

    def body(x_ref, g_ref, c_ref, s_ref, p_ref, o_ref):
        xf = x_ref[...]
        r = lax.rsqrt(jnp.mean(xf * xf, axis=-1, keepdims=True) + EPS)
        y = xf * r * g_ref[...]
        yr = y[:, MLA_NOPE:]
        sw = lax.dot_general(yr, p_ref[...], NN, precision=lax.Precision.HIGHEST, preferred_element_type=F32)
        o_ref[:, :MLA_NOPE] = (y[:, :MLA_NOPE] * scale).astype(o_ref.dtype)
        o_ref[:, MLA_NOPE:] = ((yr * c_ref[...] + sw * s_ref[...]) * scale).astype(o_ref.dtype)

    row = pl.BlockSpec((tm, d), lambda i: (i, 0))
    tab = pl.BlockSpec((tm, MLA_ROPE), lambda i: (i % nt, 0))
    return _pcall(
        body, name=name, grid=(n // tm,),
        in_specs=[row, pl.BlockSpec((1, d), lambda i: (0, 0)), tab, tab,
                  pl.BlockSpec((MLA_ROPE, MLA_ROPE), lambda i: (0, 0))],
        out_specs=row, out_shape=_sds((n, d), BF16),
        compiler_params=_cparams("parallel"))(x, g, cos_t, sin_t, swap)


def _mla_qk_bwd(dy, x, g, cos_t, sin_t, scale, name, tm):
    n, d = x.shape
    t = cos_t.shape[0]
    tm = min(tm, t)
    nt = t // tm
    swap_t = _half_swap().T

    def body(dy_ref, x_ref, g_ref, c_ref, s_ref, p_ref, dx_ref, dg_ref):
        xf = x_ref[...]
        gg = g_ref[...]
        r = lax.rsqrt(jnp.mean(xf * xf, axis=-1, keepdims=True) + EPS)
        xh = xf * r
        dyf = dy_ref[...] * scale
        dyr = dyf[:, MLA_NOPE:]
        back = lax.dot_general(dyr * s_ref[...], p_ref[...], NN, precision=lax.Precision.HIGHEST,
                               preferred_element_type=F32)
        dn_n = dyf[:, :MLA_NOPE]
        dn_r = dyr * c_ref[...] + back
        xh_n, xh_r = xh[:, :MLA_NOPE], xh[:, MLA_NOPE:]
        dxh_n = dn_n * gg[:, :MLA_NOPE]
        dxh_r = dn_r * gg[:, MLA_NOPE:]
        mean = (jnp.sum(dxh_n * xh_n, axis=-1, keepdims=True)
                + jnp.sum(dxh_r * xh_r, axis=-1, keepdims=True)) * (1.0 / d)
        dx_ref[:, :MLA_NOPE] = r * (dxh_n - xh_n * mean)
        dx_ref[:, MLA_NOPE:] = r * (dxh_r - xh_r * mean)

        @pl.when(pl.program_id(0) == 0)
        def _():
            dg_ref[...] = jnp.zeros_like(dg_ref)

        dg_ref[:, :MLA_NOPE] += jnp.sum(dn_n * xh_n, axis=0, keepdims=True)
        dg_ref[:, MLA_NOPE:] += jnp.sum(dn_r * xh_r, axis=0, keepdims=True)

    row = pl.BlockSpec((tm, d), lambda i: (i, 0))
    vec = pl.BlockSpec((1, d), lambda i: (0, 0))
    tab = pl.BlockSpec((tm, MLA_ROPE), lambda i: (i % nt, 0))
    return _pcall(
        body, name=name, grid=(n // tm,),
        in_specs=[row, row, vec, tab, tab, pl.BlockSpec((MLA_ROPE, MLA_ROPE), lambda i: (0, 0))],
        out_specs=(row, vec), out_shape=(_sds((n, d), F32), _sds((1, d), F32)),
        compiler_params=_cparams("arbitrary"))(dy, x, g, cos_t, sin_t, swap_t)


def _causal_mask(i, j, tq, tk):
    row = i * tq + lax.broadcasted_iota(jnp.int32, (tq, tk), 0)
    col = j * tk + lax.broadcasted_iota(jnp.int32, (tq, tk), 1)
    return col <= row


def _causal_steps(nq, nk, tq, tk, q_major):
    if q_major:
        groups = [[(i, j) for j in range((i * tq + tq - 1) // tk + 1)] for i in range(nq)]
    else:
        groups = [[(i, j) for i in range((j * tk) // tq, nq)] for j in range(nk)]
    it, jt, fl = [], [], []
    for g in groups:
        for n, (i, j) in enumerate(g):
            it.append(i)
            jt.append(j)
            fl.append((n == 0) + 2 * (n == len(g) - 1) + 4 * (j * tk + tk - 1 > i * tq))
    return tuple(jnp.asarray(np.array(a, np.int32)) for a in (it, jt, fl))


def _causal_specs(tq, tk):
    def qs(w):
        return pl.BlockSpec((None, tq, w), lambda h, s, it, jt, fl: (h, it[s], 0))

    def kv(w):
        return pl.BlockSpec((None, tk, w), lambda h, s, it, jt, fl: (h, jt[s], 0))

    return qs, kv


def _mla_fwd(q, k, v, tq, tk):
    nh, t, dq = q.shape
    dv = v.shape[2]
    tq, tk = min(tq, t), min(tk, t)
    tables = _causal_steps(t // tq, t // tk, tq, tk, True)

    def body(it, jt, fl, q_ref, k_ref, v_ref, o_ref, lse_ref, m_sc, l_sc, acc_sc):
        step = pl.program_id(1)
        i, j, flags = it[step], jt[step], fl[step]

        @pl.when((flags & 1) != 0)
        def _():
            m_sc[...] = jnp.full_like(m_sc, NEG)
            l_sc[...] = jnp.zeros_like(l_sc)
            acc_sc[...] = jnp.zeros_like(acc_sc)

        def update(masked):
            s = _dot(q_ref[...], k_ref[...], NT)
            if masked:
                s = jnp.where(_causal_mask(i, j, tq, tk), s, NEG)
            m_prev = m_sc[...]
            m_new = jnp.maximum(m_prev, jnp.max(s, axis=-1, keepdims=True))
            alpha = jnp.exp(m_prev - m_new)
            p = jnp.exp(s - m_new)
            l_sc[...] = alpha * l_sc[...] + jnp.sum(p, axis=-1, keepdims=True)
            acc_sc[...] = alpha * acc_sc[...] + _dot(p.astype(BF16), v_ref[...], NN)
            m_sc[...] = m_new

        pl.when((flags & 4) != 0)(functools.partial(update, True))
        pl.when((flags & 4) == 0)(functools.partial(update, False))

        @pl.when((flags & 2) != 0)
        def _():
            o_ref[...] = acc_sc[...] / l_sc[...]
            lse_ref[...] = m_sc[...] + jnp.log(l_sc[...])

    qs, kv = _causal_specs(tq, tk)
    return _pcall(
        body, name="mla_attn_fwd",
        grid_spec=pltpu.PrefetchScalarGridSpec(
            num_scalar_prefetch=3, grid=(nh, tables[0].shape[0]),
            in_specs=[qs(dq), kv(dq), kv(dv)], out_specs=(qs(dv), qs(1)),
            scratch_shapes=[pltpu.VMEM((tq, 1), F32), pltpu.VMEM((tq, 1), F32), pltpu.VMEM((tq, dv), F32)]),
        out_shape=(_sds((nh, t, dv), F32), _sds((nh, t, 1), F32)),
        compiler_params=_cparams("parallel", "arbitrary"))(*tables, q, k, v)


def _mla_bwd_dq(q, k, v, do, lse, dl, tq, tk):
    nh, t, dq = q.shape
    dv = v.shape[2]
    tq, tk = min(tq, t), min(tk, t)
    tables = _causal_steps(t // tq, t // tk, tq, tk, True)

    def body(it, jt, fl, q_ref, k_ref, v_ref, do_ref, lse_ref, dl_ref, dq_ref, acc_sc):
        step = pl.program_id(1)
        i, j, flags = it[step], jt[step], fl[step]

        def update(masked):
            s = _dot(q_ref[...], k_ref[...], NT)
            if masked:
                s = jnp.where(_causal_mask(i, j, tq, tk), s, NEG)
            p = jnp.exp(s - lse_ref[...])
            dp = _dot(do_ref[...].astype(BF16), v_ref[...], NT)
            ds = p * (dp - dl_ref[...])
            part = _dot(ds.astype(BF16), k_ref[...], NN)

            @pl.when((flags & 1) != 0)
            def _():
                acc_sc[...] = part

            @pl.when((flags & 1) == 0)
            def _():
                acc_sc[...] += part

        pl.when((flags & 4) != 0)(functools.partial(update, True))
        pl.when((flags & 4) == 0)(functools.partial(update, False))

        @pl.when((flags & 2) != 0)
        def _():
            dq_ref[...] = acc_sc[...]

    qs, kv = _causal_specs(tq, tk)
    return _pcall(
        body, name="mla_attn_dq",
        grid_spec=pltpu.PrefetchScalarGridSpec(
            num_scalar_prefetch=3, grid=(nh, tables[0].shape[0]),
            in_specs=[qs(dq), kv(dq), kv(dv), qs(dv), qs(1), qs(1)], out_specs=qs(dq),
            scratch_shapes=[pltpu.VMEM((tq, dq), F32)]),
        out_shape=_sds((nh, t, dq), F32),
        compiler_params=_cparams("parallel", "arbitrary"))(*tables, q, k, v, do, lse, dl)


def _mla_bwd_dkv(q, k, v, do, lse_row, dl_row, tq, tk):
    nh, t, dq = q.shape
    dv = v.shape[2]
    tq, tk = min(tq, t), min(tk, t)
    tables = _causal_steps(t // tq, t // tk, tq, tk, False)

    def body(it, jt, fl, q_ref, k_ref, v_ref, do_ref, lse_ref, dl_ref, dk_ref, dv_ref, dk_sc, dv_sc):
        step = pl.program_id(1)
        i, j, flags = it[step], jt[step], fl[step]

        def update(masked):
            st = _dot(k_ref[...], q_ref[...], NT)
            if masked:
                key = j * tk + lax.broadcasted_iota(jnp.int32, (tk, tq), 0)
                qry = i * tq + lax.broadcasted_iota(jnp.int32, (tk, tq), 1)
                st = jnp.where(key <= qry, st, NEG)
            pt = jnp.exp(st - lse_ref[...])
            dob = do_ref[...].astype(BF16)
            dpt = _dot(v_ref[...], dob, NT)
            dst = pt * (dpt - dl_ref[...])
            dv_part = _dot(pt.astype(BF16), dob, NN)
            dk_part = _dot(dst.astype(BF16), q_ref[...], NN)

            @pl.when((flags & 1) != 0)
            def _():
                dv_sc[...] = dv_part
                dk_sc[...] = dk_part

            @pl.when((flags & 1) == 0)
            def _():
                dv_sc[...] += dv_part
                dk_sc[...] += dk_part

        pl.when((flags & 4) != 0)(functools.partial(update, True))
        pl.when((flags & 4) == 0)(functools.partial(update, False))

        @pl.when((flags & 2) != 0)
        def _():
            dk_ref[...] = dk_sc[...]
            dv_ref[...] = dv_sc[...]

    qs, kv = _causal_specs(tq, tk)
    rowv = pl.BlockSpec((None, 1, tq), lambda h, s, it, jt, fl: (h, 0, it[s]))
    return _pcall(
        body, name="mla_attn_dkv",
        grid_spec=pltpu.PrefetchScalarGridSpec(
            num_scalar_prefetch=3, grid=(nh, tables[0].shape[0]),
            in_specs=[qs(dq), kv(dq), kv(dv), qs(dv), rowv, rowv], out_specs=(kv(dq), kv(dv)),
            scratch_shapes=[pltpu.VMEM((tk, dq), F32), pltpu.VMEM((tk, dv), F32)]),
        out_shape=(_sds((nh, t, dq), F32), _sds((nh, t, dv), F32)),
        compiler_params=_cparams("parallel", "arbitrary"))(*tables, q, k, v, do, lse_row, dl_row)


def _loss_head(y, target, tm):
    t, d = y.shape
    tm = min(tm, t)
    nt = t // tm

    def body(y_ref, t_ref, dy_ref, loss_ref, acc):
        i = pl.program_id(0)
        err = y_ref[...] - t_ref[...]
        dy_ref[...] = err * (1.0 / d)

        @pl.when(i == 0)
        def _():
            acc[...] = jnp.zeros_like(acc)

        acc[...] += jnp.sum(err * err, axis=0, keepdims=True)

        @pl.when(i == nt - 1)
        def _():
            loss_ref[0, 0] = jnp.sum(acc[...]) * (0.5 / d)

    spec = pl.BlockSpec((tm, d), lambda i: (i, 0))
    return _pcall(
        body, name="loss_head", grid=(nt,), in_specs=[spec, spec],
        out_specs=(spec, pl.BlockSpec(memory_space=pltpu.SMEM)),
        out_shape=(_sds((t, d), F32), _sds((1, 1), F32)),
        scratch_shapes=[pltpu.VMEM((1, d), F32)],
        compiler_params=_cparams("arbitrary"))(y, target)


def _adamw(w, g, m, v, name):
    r, c = w.shape
    tr = r
    for cand in (256, 128, 64, 32, 16, 8):
        if r % cand == 0:
            tr = cand
            break

    def body(w_ref, g_ref, m_ref, v_ref, d_ref, nm_ref, nv_ref):
        gg = g_ref[...]
        nm = ADAM_B1 * m_ref[...] + (1.0 - ADAM_B1) * gg
        nv = ADAM_B2 * v_ref[...] + (1.0 - ADAM_B2) * (gg * gg)
        m_hat = nm / (1.0 - ADAM_B1 ** ADAM_STEP)
        v_hat = nv / (1.0 - ADAM_B2 ** ADAM_STEP)
        d_ref[...] = -ADAM_LR * (m_hat / (jnp.sqrt(v_hat) + ADAM_EPS) + ADAM_WD * w_ref[...])
        nm_ref[...] = nm
        nv_ref[...] = nv

    spec = pl.BlockSpec((tr, c), lambda i: (i, 0))
    sd = _sds((r, c), F32)
    return _pcall(body, name=name, grid=(r // tr,), in_specs=[spec] * 4, out_specs=(spec,) * 3,
                  out_shape=(sd, sd, sd), compiler_params=_cparams("parallel"))(w, g, m, v)


MESH_ID = pl.DeviceIdType.MESH
HBM_SPEC = pl.BlockSpec(memory_space=pltpu.HBM)


def _place():
    return lax.axis_index("x"), lax.axis_index("y"), lax.axis_index("c")


def _other_chips(x, y):
    return [(1 - x, y), (x, 1 - y), (1 - x, 1 - y)]


def _remote(src, dst, send_sems, recv_sems, k, to):
    return pltpu.make_async_remote_copy(src_ref=src, dst_ref=dst, send_sem=send_sems.at[k], recv_sem=recv_sems.at[k],
                                        device_id=to, device_id_type=MESH_ID)


D2D_SPLIT = 16
ICI_SPLIT = 4


def _chunks(rows, n):
    assert rows % n == 0
    return [(i * (rows // n), rows // n) for i in range(n)]


def _gather_weights(packed):
    rows, lanes = packed.shape
    half = rows // 2

    def body(src, out, send_sems, recv_sems, local_sem):
        x, y, c = _place()
        me = 2 * x + y
        sibling = (x, y, 1 - c)
        chips = _other_chips(x, y)

        def part(chip, core, lo=0, n=half):
            return out.at[chip, pl.ds(core * half + lo, n), :]

        for lo, n in _chunks(rows, D2D_SPLIT):
            pltpu.make_async_copy(src.at[pl.ds(lo, n), :], out.at[me, pl.ds(lo, n), :], local_sem).start()
        for k, (cx, cy) in enumerate(chips):
            for lo, n in _chunks(half, ICI_SPLIT):
                _remote(src.at[pl.ds(c * half + lo, n), :], part(me, c, lo, n), send_sems, recv_sems, k,
                        (cx, cy, c)).start()
        for k, (cx, cy) in enumerate(chips):
            got = part(2 * cx + cy, c)
            _remote(got, got, send_sems, recv_sems, k, (x, y, c)).wait_recv()
            for lo, n in _chunks(half, D2D_SPLIT):
                piece = part(2 * cx + cy, c, lo, n)
                _remote(piece, piece, send_sems, recv_sems, 3 + k, sibling).start()
        for k, (cx, cy) in enumerate(chips):
            got = part(2 * cx + cy, 1 - c)
            _remote(got, got, send_sems, recv_sems, 3 + k, (x, y, c)).wait_recv()
        for k in range(6):
            sent = part(me, c)
            _remote(sent, sent, send_sems, recv_sems, k, (x, y, c)).wait_send()
        pltpu.make_async_copy(src, out.at[me], local_sem).wait()

    return _pcall(
        body, name="gather_weights", in_specs=[HBM_SPEC], out_specs=HBM_SPEC,
        out_shape=_sds((N_CHIPS, rows, lanes), packed.dtype),
        scratch_shapes=[pltpu.SemaphoreType.DMA((6,)), pltpu.SemaphoreType.DMA((6,)), pltpu.SemaphoreType.DMA(())],
    )(packed)


def _reduce_cores(grads):
    nchip, rows, lanes = grads.shape
    half = rows // 2

    def body(g, theirs, send_sems, recv_sems):
        x, y, c = _place()
        for j in range(nchip):
            for lo, n in _chunks(half, D2D_SPLIT):
                _remote(g.at[j, pl.ds((1 - c) * half + lo, n), :], theirs.at[j, pl.ds(lo, n), :],
                        send_sems, recv_sems, 0, (x, y, 1 - c)).start()
        _remote(g.at[:, pl.ds((1 - c) * half, half), :], theirs, send_sems, recv_sems, 0, (x, y, c)).wait()

    return _pcall(
        body, name="reduce_cores", in_specs=[HBM_SPEC], out_specs=HBM_SPEC,
        out_shape=_sds((nchip, half, lanes), grads.dtype),
        scratch_shapes=[pltpu.SemaphoreType.DMA((1,)), pltpu.SemaphoreType.DMA((1,))],
    )(grads)


def _scatter_chips(part):
    nchip, half, lanes = part.shape

    def body(p, out, send_sems, recv_sems, local_sem):
        x, y, c = _place()
        me = 2 * x + y
        chips = _other_chips(x, y)
        for lo, n in _chunks(half, ICI_SPLIT):
            pltpu.make_async_copy(p.at[me, pl.ds(lo, n), :], out.at[3, pl.ds(lo, n), :], local_sem).start()
        for k, (cx, cy) in enumerate(chips):
            for lo, n in _chunks(half, ICI_SPLIT):
                _remote(p.at[2 * cx + cy, pl.ds(lo, n), :], out.at[k, pl.ds(lo, n), :],
                        send_sems, recv_sems, k, (cx, cy, c)).start()
        for k in range(3):
            _remote(p.at[k], out.at[k], send_sems, recv_sems, k, (x, y, c)).wait()
        pltpu.make_async_copy(p.at[me], out.at[3], local_sem).wait()

    return _pcall(
        body, name="scatter_chips", in_specs=[HBM_SPEC], out_specs=HBM_SPEC,
        out_shape=_sds((nchip, half, lanes), part.dtype),
        scratch_shapes=[pltpu.SemaphoreType.DMA((3,)), pltpu.SemaphoreType.DMA((3,)), pltpu.SemaphoreType.DMA(())],
    )(part)


def _share_cores(mine):
    half, lanes = mine.shape

    def body(src, out, send_sems, recv_sems, local_sem):
        x, y, c = _place()
        for lo, n in _chunks(half, D2D_SPLIT):
            piece = src.at[pl.ds(lo, n), :]
            dst = out.at[pl.ds(c * half + lo, n), :]
            pltpu.make_async_copy(piece, dst, local_sem).start()
            _remote(piece, dst, send_sems, recv_sems, 0, (x, y, 1 - c)).start()
        theirs = out.at[pl.ds((1 - c) * half, half), :]
        _remote(src, theirs, send_sems, recv_sems, 0, (x, y, c)).wait()
        pltpu.make_async_copy(src, out.at[pl.ds(c * half, half), :], local_sem).wait()

    return _pcall(
        body, name="share_cores", in_specs=[HBM_SPEC], out_specs=HBM_SPEC,
        out_shape=_sds((2 * half, lanes), mine.dtype),
        scratch_shapes=[pltpu.SemaphoreType.DMA((1,)), pltpu.SemaphoreType.DMA((1,)), pltpu.SemaphoreType.DMA(())],
    )(mine)


def _sum_blocks(stacked, name, tm):
    n, rows, lanes = stacked.shape
    tm = min(tm, rows)

    def body(s_ref, o_ref):
        tot = s_ref[n - 1].astype(F32)
        for k in range(n - 1):
            tot = tot + s_ref[k].astype(F32)
        o_ref[...] = tot

    return _pcall(body, name=name, grid=(rows // tm,),
                  in_specs=[pl.BlockSpec((n, tm, lanes), lambda i: (0, i, 0))],
                  out_specs=pl.BlockSpec((tm, lanes), lambda i: (i, 0)), out_shape=_sds((rows, lanes), F32),
                  compiler_params=_cparams("parallel"))(stacked)


def _add_halves(grads, theirs, core, tm):
    n, half, lanes = theirs.shape
    tm = min(tm, half)
    nblk = half // tm

    def body(c_ref, g_ref, t_ref, o_ref):
        o_ref[...] = (g_ref[...] + t_ref[...]).astype(o_ref.dtype)

    spec = pl.BlockSpec((None, tm, lanes), lambda k, i, c: (k, i, 0))
    return _pcall(
        body, name="add_core_halves",
        grid_spec=pltpu.PrefetchScalarGridSpec(
            num_scalar_prefetch=1, grid=(n, nblk),
            in_specs=[pl.BlockSpec((None, tm, lanes), lambda k, i, c: (k, c[0] * nblk + i, 0)), spec], out_specs=spec),
        out_shape=_sds((n, half, lanes), BF16),
        compiler_params=_cparams("parallel", "parallel"))(core, grads, theirs)


def _allreduce_small(part):
    rows, lanes = part.shape
    ndev = 8

    def body(src, tot, buf, send_sems, recv_sems):
        x, y, c = _place()
        me = 4 * x + 2 * y + c
        buf[me] = src[...]
        sends = []
        for k in range(1, ndev):
            peer = (x ^ (k >> 2), y ^ ((k >> 1) & 1), c ^ (k & 1))
            cp = _remote(src, buf.at[me], send_sems, recv_sems, k - 1, peer)
            cp.start()
            sends.append(cp)
        for k in range(1, ndev):
            theirs = buf.at[me ^ k]
            _remote(theirs, theirs, send_sems, recv_sems, k - 1, (x, y, c)).wait_recv()
        for cp in sends:
            cp.wait_send()
        acc = buf[0]
        for d in range(1, ndev):
            acc = acc + buf[d]
        tot[...] = acc

    vm = pl.BlockSpec(memory_space=pltpu.VMEM)
    return _pcall(
        body, name="allreduce_small", in_specs=[vm], out_specs=vm, out_shape=_sds((rows, lanes), F32),
        scratch_shapes=[pltpu.VMEM((ndev, rows, lanes), F32), pltpu.SemaphoreType.DMA((ndev - 1,)),
                        pltpu.SemaphoreType.DMA((ndev - 1,))],
    )(part)


def _big_rows():
    return [int(np.prod(shape)) // LANES for _, shape in BIG]


def _pack_big(blocks, dtype):
    parts = [blocks[name].reshape(blocks[name].shape[0], -1, LANES).astype(dtype) for name, _ in BIG]
    return jnp.concatenate(parts, axis=1)


def _unpack_big(packed):
    out, off = {}, 0
    for (name, shape), r in zip(BIG, _big_rows()):
        out[name] = packed[:, off:off + r].reshape((packed.shape[0],) + shape)
        off += r
    return out


def _pack_small(vals):
    parts = []
    for name, shape, r in SMALL:
        flat = vals[name].reshape(-1).astype(F32)
        parts.append(jnp.pad(flat, (0, r * LANES - flat.shape[0])).reshape(r, LANES))
    used = sum(r for _, _, r in SMALL)
    parts.append(jnp.zeros((SMALL_ROWS - used, LANES), F32))
    return jnp.concatenate(parts, axis=0)


def _unpack_small(packed):
    out, off = {}, 0
    for name, shape, r in SMALL:
        n = int(np.prod(shape))
        out[name] = packed[off:off + r].reshape(-1)[:n].reshape(shape)
        off += r
    return out


def _heads_major(a, nh):
    t = a.shape[0]
    return a.reshape(t, nh, a.shape[1] // nh).transpose(1, 0, 2)


def _tokens_major(a):
    nh, t, w = a.shape
    return a.transpose(1, 0, 2).reshape(t, nh * w)


def _local_step(x, target, small, wfull):
    t = x.shape[0]
    nh, hd = DIL_HEADS, DIL_HD
    w_in = wfull["w_in"].transpose(1, 0, 2).reshape(D_MODEL, -1)
    w_out = wfull["w_out"].reshape(D_MODEL, D_MODEL)
    w_qb, w_kvb = wfull["mla_w_q_b"], wfull["mla_w_kv_b"]
    grads_s, grads_b = {}, {}

    x1, ffn1_saved = _ffn_fwd(x, small["ffn1_norm"], wfull["ffn1_w_gate"], wfull["ffn1_w_up"],
                              wfull["ffn1_w_down"], "ffn1")
    hm = _rms_fwd(x1, small["mix_norm"], BF16, "mix_norm", 512)
    proj = _mm_simple("in_proj", hm, w_in, NN, F32)
    q_a, k_a, v_a = proj[:, :512], proj[:, 512:1024], proj[:, 1024:1536]
    cq, ckv, k_pe = proj[:, 1536:1792], proj[:, 1792:1920], proj[:, 1920:1984]

    q_h = _heads_major(q_a, nh).reshape(nh * t, hd)
    k_h = _heads_major(k_a, nh).reshape(nh * t, hd)
    v_h = _heads_major(v_a, nh).astype(BF16)
    qn = _rms_fwd(q_h, small["dil_q_norm"], BF16, "dil_q_norm", 2048).reshape(nh, t, hd)
    kn = _rms_fwd(k_h, small["dil_k_norm"], BF16, "dil_k_norm", 2048).reshape(nh, t, hd)
    bias = _bias_tiles(small["rel_bias"])
    branch_in, outs, lses = [], [], []
    for b, dil in enumerate(DIL_DILATIONS):
        qs, ks, vs = _to_sub(qn, dil), _to_sub(kn, dil), _to_sub(v_h, dil)
        o_b, lse_b = _dil_fwd(qs, ks, vs, bias[b], t // dil, f"dil_fwd_{dil}")
        branch_in.append((qs, ks, vs))
        outs.append(_from_sub(o_b, dil).reshape(nh * t, hd))
        lses.append(_from_sub(lse_b, dil).reshape(nh * t, 1))
    o_dil_h, lse_tot = _dil_merge(outs, lses, 2048)
    o_dil = _tokens_major(o_dil_h.reshape(nh, t, hd))

    mh = MLA_HEADS
    cos_t, sin_t = _rope_tables(t)
    cqn = _rms_fwd(cq, small["mla_q_a_norm"], BF16, "mla_q_a_norm", 512)
    ckvn = _rms_fwd(ckv, small["mla_kv_a_norm"], BF16, "mla_kv_a_norm", 512)
    tm = min(512, t)

    def head_proj(name, a, w, width):
        k = a.shape[1]
        return _mm(name, (mh, t // tm, 1),
                   [(a, pl.BlockSpec((tm, k), lambda h, i, r: (i, 0)), w, pl.BlockSpec((None, k, width), lambda h, i, r: (h, 0, 0)))],
                   NN, _sds((mh, t, width), F32), pl.BlockSpec((None, tm, width), lambda h, i, r: (h, i, 0)), (tm, width))

    q_raw = head_proj("mla_q_proj", cqn, w_qb, MLA_QK)
    kv_raw = head_proj("mla_kv_proj", ckvn, w_kvb, MLA_NOPE + MLA_V)
    k_raw = jnp.concatenate([kv_raw[:, :, :MLA_NOPE], jnp.broadcast_to(k_pe[None], (mh, t, MLA_ROPE))], axis=2)
    v_m = kv_raw[:, :, MLA_NOPE:].astype(BF16)
    q_raw2, k_raw2 = q_raw.reshape(mh * t, MLA_QK), k_raw.reshape(mh * t, MLA_QK)
    q_scale = MLA_QK ** -0.5
    q_m = _mla_qk_fwd(q_raw2, small["mla_q_norm"], cos_t, sin_t, q_scale, "mla_q_rope", 512).reshape(mh, t, MLA_QK)
    k_m = _mla_qk_fwd(k_raw2, small["mla_k_norm"], cos_t, sin_t, 1.0, "mla_k_rope", 512).reshape(mh, t, MLA_QK)
    o_mla_h, lse_m = _mla_fwd(q_m, k_m, v_m, 512, 512)
    o_mla = _tokens_major(o_mla_h)

    od = _rms_fwd(o_dil, small["out_norm_dil"], BF16, "out_norm_dil", 512)
    om = _rms_fwd(o_mla, small["out_norm_mla"], BF16, "out_norm_mla", 512)
    half_w = DIL_WIDTH
    row = pl.BlockSpec((tm, D_MODEL), lambda i, j, r: (i, 0))
    act_spec = pl.BlockSpec((tm, half_w), lambda i, j, r: (i, 0))
    x2 = _mm("out_proj", (t // tm, 1, 1),
             [(od, act_spec, w_out, pl.BlockSpec((half_w, D_MODEL), lambda i, j, r: (0, 0))),
              (om, act_spec, w_out, pl.BlockSpec((half_w, D_MODEL), lambda i, j, r: (1, 0)))],
             NN, _sds((t, D_MODEL), F32), row, (tm, D_MODEL), res=(x1, row))
    x3, ffn2_saved = _ffn_fwd(x2, small["ffn2_norm"], wfull["ffn2_w_gate"], wfull["ffn2_w_up"],
                              wfull["ffn2_w_down"], "ffn2")
    dy, loss = _loss_head(x3, target, 512)

    dx2, grads_s["ffn2_norm"], grads_b["ffn2_w_gate"], grads_b["ffn2_w_up"], grads_b["ffn2_w_down"] = _ffn_bwd(
        dy, x2, small["ffn2_norm"], wfull["ffn2_w_gate"], wfull["ffn2_w_up"], wfull["ffn2_w_down"], ffn2_saved, "ffn2")

    d_ocat = _mm_simple("out_proj_dx", dx2, w_out, NT, F32)
    tk = min(512, t)
    tok = pl.BlockSpec((tk, half_w), lambda c, j, r: (r, 0))
    dw_out_d = _mm_simple("out_proj_dw_dil", od, dx2, TN, F32)
    dw_out_m = _mm_simple("out_proj_dw_mla", om, dx2, TN, F32)
    grads_b["w_out"] = jnp.concatenate([dw_out_d, dw_out_m], axis=0).reshape(N_CHIPS, D_MODEL // N_CHIPS, D_MODEL)
    do_dil, grads_s["out_norm_dil"] = _rms_bwd([d_ocat[:, :half_w]], o_dil, small["out_norm_dil"], None, "out_norm_dil_bwd", 512)
    do_mla, grads_s["out_norm_mla"] = _rms_bwd([d_ocat[:, half_w:]], o_mla, small["out_norm_mla"], None, "out_norm_mla_bwd", 512)

    do_m = _heads_major(do_mla, mh)
    dl_m = _rowdot(do_m.reshape(mh * t, MLA_V), o_mla_h.reshape(mh * t, MLA_V), "mla_delta", 2048).reshape(mh, t, 1)
    dq_m = _mla_bwd_dq(q_m, k_m, v_m, do_m, lse_m, dl_m, 512, 512)
    dk_m, dv_m = _mla_bwd_dkv(q_m, k_m, v_m, do_m, lse_m.reshape(mh, 1, t), dl_m.reshape(mh, 1, t), 512, 512)
    dq_raw, grads_s["mla_q_norm"] = _mla_qk_bwd(dq_m.reshape(mh * t, MLA_QK), q_raw2, small["mla_q_norm"],
                                                 cos_t, sin_t, q_scale, "mla_q_rope_bwd", 512)
    dk_raw, grads_s["mla_k_norm"] = _mla_qk_bwd(dk_m.reshape(mh * t, MLA_QK), k_raw2, small["mla_k_norm"],
                                                 cos_t, sin_t, 1.0, "mla_k_rope_bwd", 512)
    dq_raw = dq_raw.reshape(mh, t, MLA_QK)
    dk_raw = dk_raw.reshape(mh, t, MLA_QK)
    dkv_raw = jnp.concatenate([dk_raw[:, :, :MLA_NOPE], dv_m], axis=2)
    dk_pe_h = dk_raw[:, :, MLA_NOPE:]

    def head_proj_dx(name, d, w):
        width, k = d.shape[2], w.shape[1]
        return _mm(name, (t // tm, 1, mh),
                   [(d, pl.BlockSpec((None, tm, width), lambda i, j, r: (r, i, 0)), w, pl.BlockSpec((None, k, width), lambda i, j, r: (r, 0, 0)))],
                   NT, _sds((t, k), F32), pl.BlockSpec((tm, k), lambda i, j, r: (i, 0)), (tm, k))

    def head_proj_dw(name, a, d):
        width, k = d.shape[2], a.shape[1]
        return _mm(name, (mh, 1, t // tk),
                   [(a, pl.BlockSpec((tk, k), lambda h, j, r: (r, 0)), d, pl.BlockSpec((None, tk, width), lambda h, j, r: (h, r, 0)))],
                   TN, _sds((mh, k, width), F32), pl.BlockSpec((None, k, width), lambda h, j, r: (h, 0, 0)), (k, width))

    d_cqn = head_proj_dx("mla_q_proj_dx", dq_raw, w_qb)
    d_ckvn = head_proj_dx("mla_kv_proj_dx", dkv_raw, w_kvb)
    grads_b["mla_w_q_b"] = head_proj_dw("mla_q_proj_dw", cqn, dq_raw)
    grads_b["mla_w_kv_b"] = head_proj_dw("mla_kv_proj_dw", ckvn, dkv_raw)
    d_cq, grads_s["mla_q_a_norm"] = _rms_bwd([d_cqn], cq, small["mla_q_a_norm"], None, "mla_q_a_norm_bwd", 512)
    d_ckv, grads_s["mla_kv_a_norm"] = _rms_bwd([d_ckvn], ckv, small["mla_kv_a_norm"], None, "mla_kv_a_norm_bwd", 512)
    d_kpe = _sum_blocks(dk_pe_h.reshape(mh, t * MLA_ROPE // LANES, LANES), "mla_kpe_sum", 1024).reshape(t, MLA_ROPE)

    do_h = _heads_major(do_dil, nh)
    dl_d = _rowdot(do_h.reshape(nh * t, hd), o_dil_h, "dil_delta", 2048).reshape(nh, t, 1)
    lse_t = lse_tot.reshape(nh, t, 1)
    do_hb = do_h.astype(BF16)
    dqs, dks, dvs, dtiles = [], [], [], []
    for b, dil in enumerate(DIL_DILATIONS):
        qs, ks, vs = branch_in[b]
        dq_b, dk_b, dv_b, db_b = _dil_bwd(qs, ks, vs, _to_sub(do_hb, dil), _to_sub(lse_t, dil), _to_sub(dl_d, dil),
                                          bias[b], t // dil, f"dil_bwd_{dil}")
        dqs.append(_from_sub(dq_b, dil).reshape(nh * t, hd))
        dks.append(_from_sub(dk_b, dil).reshape(nh * t, hd))
        dvs.append(_from_sub(dv_b, dil).reshape(nh * t, hd))
        dtiles.append(db_b)
    grads_s["rel_bias"] = _bias_grad(jnp.stack(dtiles))
    dq_a_h, grads_s["dil_q_norm"] = _rms_bwd(dqs, q_h, small["dil_q_norm"], None, "dil_q_norm_bwd", 2048)
    dk_a_h, grads_s["dil_k_norm"] = _rms_bwd(dks, k_h, small["dil_k_norm"], None, "dil_k_norm_bwd", 2048)
    dv_a_h = _add3(dvs[0], dvs[1], dvs[2], "dil_dv_sum", 2048)
    dproj = jnp.concatenate([_tokens_major(dq_a_h.reshape(nh, t, hd)), _tokens_major(dk_a_h.reshape(nh, t, hd)),
                             _tokens_major(dv_a_h.reshape(nh, t, hd)), d_cq, d_ckv, d_kpe], axis=1)

    d_hm = _mm_simple("in_proj_dx", dproj, w_in, NT, F32)
    dw_in = _mm_simple("in_proj_dw", hm, dproj, TN, F32)
    grads_b["w_in"] = dw_in.reshape(D_MODEL, N_CHIPS, -1).transpose(1, 0, 2)
    dx1, grads_s["mix_norm"] = _rms_bwd([d_hm], x1, small["mix_norm"], dx2, "mix_norm_bwd", 512)
    dx, grads_s["ffn1_norm"], grads_b["ffn1_w_gate"], grads_b["ffn1_w_up"], grads_b["ffn1_w_down"] = _ffn_bwd(
        dx1, x, small["ffn1_norm"], wfull["ffn1_w_gate"], wfull["ffn1_w_up"], wfull["ffn1_w_down"], ffn1_saved, "ffn1")
    return loss, dx, grads_s, grads_b


def kernel(x, ffn1_norm, ffn1_w_gate, ffn1_w_up, ffn1_w_down, mix_norm, w_in, dil_q_norm, dil_k_norm, rel_bias, mla_q_a_norm, mla_w_q_b, mla_kv_a_norm, mla_w_kv_b, mla_q_norm, mla_k_norm, out_norm_dil, out_norm_mla, w_out, ffn2_norm, ffn2_w_gate, ffn2_w_up, ffn2_w_down, loss_target, m_ffn1_norm, m_ffn1_w_gate, m_ffn1_w_up, m_ffn1_w_down, m_mix_norm, m_w_in, m_dil_q_norm, m_dil_k_norm, m_rel_bias, m_mla_q_a_norm, m_mla_w_q_b, m_mla_kv_a_norm, m_mla_w_kv_b, m_mla_q_norm, m_mla_k_norm, m_out_norm_dil, m_out_norm_mla, m_w_out, m_ffn2_norm, m_ffn2_w_gate, m_ffn2_w_up, m_ffn2_w_down, v_ffn1_norm, v_ffn1_w_gate, v_ffn1_w_up, v_ffn1_w_down, v_mix_norm, v_w_in, v_dil_q_norm, v_dil_k_norm, v_rel_bias, v_mla_q_a_norm, v_mla_w_q_b, v_mla_kv_a_norm, v_mla_w_kv_b, v_mla_q_norm, v_mla_k_norm, v_out_norm_dil, v_out_norm_mla, v_w_out, v_ffn2_norm, v_ffn2_w_gate, v_ffn2_w_up, v_ffn2_w_down):
    given = dict(locals())
    big_names = [name for name, _ in BIG]
    small_names = [name for name, _, _ in SMALL]

    mine = _pack_big({n: given[n] for n in big_names}, BF16)[0]
    wfull = _unpack_big(_gather_weights(mine))
    small = {n: given[n] for n in small_names}

    loss, dx, grads_s, grads_b = _local_step(x[0], loss_target[0], small, wfull)
    loss = lax.psum(loss[0, 0], ("x", "y", "c"))

    packed = _pack_big(grads_b, F32).reshape(N_CHIPS, -1, LANES)
    core = lax.axis_index("c").astype(jnp.int32).reshape(1)
    chip_part = _add_halves(packed, _reduce_cores(packed), core, 1264)
    half_sum = _sum_blocks(_scatter_chips(chip_part), "sum_chip_partials", 1264)
    g_big = _unpack_big(_share_cores(half_sum)[None])
    g_small = _unpack_small(_allreduce_small(_pack_small(grads_s)))

    grad, delta, new_m, new_v = {}, {}, {}, {}
    for name, shape in BIG:
        g2 = g_big[name].reshape(shape)
        d_, m_, v_ = _adamw(given[name].reshape(shape), g2, given["m_" + name].reshape(shape),
                            given["v_" + name].reshape(shape), f"adamw_{name}")
        full = given[name].shape
        grad[name], delta[name], new_m[name], new_v[name] = (a.reshape(full) for a in (g2, d_, m_, v_))
    ps = {k: _pack_small({n: given[pre + n] for n in small_names}) for k, pre in (("w", ""), ("m", "m_"), ("v", "v_"))}
    gs_packed = _pack_small(g_small)
    d_s, m_s, v_s = (_unpack_small(a) for a in _adamw(ps["w"], gs_packed, ps["m"], ps["v"], "adamw_small"))
    for name in small_names:
        grad[name], delta[name], new_m[name], new_v[name] = g_small[name], d_s[name], m_s[name], v_s[name]

    return (loss, dx[None], *[grad[n] for n in WEIGHTS], *[delta[n] for n in WEIGHTS],
            *[new_m[n] for n in WEIGHTS], *[new_v[n] for n in WEIGHTS])
```

```python
import functools

import numpy as np
import jax
import jax.numpy as jnp
from jax import lax
from jax.experimental import pallas as pl
from jax.experimental.pallas import tpu as pltpu

F32 = jnp.float32
BF16 = jnp.bfloat16

D_MODEL = 1024
D_FF = 2816
N_CHIPS = 4
DIL_HEADS = 8
DIL_HD = 64
DIL_WIDTH = 512
DIL_DILATIONS = (1, 4, 16)
DIL_W = 128
QB = 128
MLA_HEADS = 4
MLA_NOPE = 128
MLA_ROPE = 64
MLA_QK = 192
MLA_V = 128
MLA_Q_RANK = 256
MLA_KV_RANK = 128
ROPE_BASE = 10000.0
REL_BUCKETS = 32
REL_MAX_DIST = 2048
FFN_RESID = 0.5
EPS = 1e-6
NEG = -1e30
LANES = 128

ADAM_LR = 0.001
ADAM_B1 = 0.9
ADAM_B2 = 0.999
ADAM_EPS = 1e-08
ADAM_WD = 0.01
ADAM_STEP = 10

NT = (((1,), (1,)), ((), ()))
NN = (((1,), (0,)), ((), ()))
TN = (((0,), (0,)), ((), ()))

BIG = (
    ("ffn1_w_gate", (D_MODEL, D_FF // N_CHIPS)),
    ("ffn1_w_up", (D_MODEL, D_FF // N_CHIPS)),
    ("ffn1_w_down", (D_FF // N_CHIPS, D_MODEL)),
    ("w_in", (D_MODEL, 1984 // N_CHIPS)),
    ("mla_w_q_b", (MLA_Q_RANK, MLA_QK)),
    ("mla_w_kv_b", (MLA_KV_RANK, MLA_NOPE + MLA_V)),
    ("w_out", (D_MODEL // N_CHIPS, D_MODEL)),
    ("ffn2_w_gate", (D_MODEL, D_FF // N_CHIPS)),
    ("ffn2_w_up", (D_MODEL, D_FF // N_CHIPS)),
    ("ffn2_w_down", (D_FF // N_CHIPS, D_MODEL)),
)
SMALL = (
    ("ffn1_norm", (1, 1024), 8), ("mix_norm", (1, 1024), 8), ("dil_q_norm", (1, 64), 1),
    ("dil_k_norm", (1, 64), 1), ("rel_bias", (8, 32), 2), ("mla_q_a_norm", (1, 256), 2),
    ("mla_kv_a_norm", (1, 128), 1), ("mla_q_norm", (1, 192), 2), ("mla_k_norm", (1, 192), 2),
    ("out_norm_dil", (1, 512), 4), ("out_norm_mla", (1, 512), 4), ("ffn2_norm", (1, 1024), 8),
)
SMALL_ROWS = 48
WEIGHTS = ("ffn1_norm", "ffn1_w_gate", "ffn1_w_up", "ffn1_w_down", "mix_norm", "w_in", "dil_q_norm",
           "dil_k_norm", "rel_bias", "mla_q_a_norm", "mla_w_q_b", "mla_kv_a_norm", "mla_w_kv_b",
           "mla_q_norm", "mla_k_norm", "out_norm_dil", "out_norm_mla", "w_out", "ffn2_norm",
           "ffn2_w_gate", "ffn2_w_up", "ffn2_w_down")


def _pcall(body, **kw):
    return pl.pallas_call(body, **kw)


def _cparams(*sem):
    return pltpu.CompilerParams(dimension_semantics=sem)


def _sds(shape, dtype):
    return jax.ShapeDtypeStruct(shape, dtype)


def _dot(a, b, dn):
    return lax.dot_general(a, b, dn, preferred_element_type=F32)


def _rms_fwd(x, g, out_dtype, name, tm):
    n, d = x.shape
    tm = min(tm, n)

    def body(x_ref, g_ref, o_ref):
        xf = x_ref[...].astype(F32)
        r = lax.rsqrt(jnp.mean(xf * xf, axis=-1, keepdims=True) + EPS)
        o_ref[...] = (xf * r * g_ref[...]).astype(o_ref.dtype)

    return _pcall(
        body, name=name, grid=(n // tm,),
        in_specs=[pl.BlockSpec((tm, d), lambda i: (i, 0)), pl.BlockSpec((1, d), lambda i: (0, 0))],
        out_specs=pl.BlockSpec((tm, d), lambda i: (i, 0)),
        out_shape=_sds((n, d), out_dtype), compiler_params=_cparams("parallel"))(x, g)


def _rms_bwd(dys, x, g, res, name, tm):
    n, d = x.shape
    tm = min(tm, n)
    nd = len(dys)
    has_res = res is not None

    def body(*refs):
        dy_refs = refs[:nd]
        x_ref, g_ref = refs[nd], refs[nd + 1]
        res_ref = refs[nd + 2] if has_res else None
        dx_ref, dg_ref = refs[-2], refs[-1]
        dy = dy_refs[0][...].astype(F32)
        for r_ in dy_refs[1:]:
            dy = dy + r_[...].astype(F32)
        xf = x_ref[...].astype(F32)
        r = lax.rsqrt(jnp.mean(xf * xf, axis=-1, keepdims=True) + EPS)
        xh = xf * r
        dxh = dy * g_ref[...]
        dx = r * (dxh - xh * jnp.mean(dxh * xh, axis=-1, keepdims=True))
        if has_res:
            dx = dx + res_ref[...]
        dx_ref[...] = dx

        @pl.when(pl.program_id(0) == 0)
        def _():
            dg_ref[...] = jnp.zeros_like(dg_ref)

        dg_ref[...] += jnp.sum(dy * xh, axis=0, keepdims=True)

    row = pl.BlockSpec((tm, d), lambda i: (i, 0))
    vec = pl.BlockSpec((1, d), lambda i: (0, 0))
    ins = list(dys) + [x, g] + ([res] if has_res else [])
    return _pcall(
        body, name=name, grid=(n // tm,),
        in_specs=[row] * nd + [row, vec] + ([row] if has_res else []),
        out_specs=(row, vec),
        out_shape=(_sds((n, d), F32), _sds((1, d), F32)),
        compiler_params=_cparams("arbitrary"))(*ins)


def _mm(name, grid, pairs, dn, out_shape, out_spec, acc_shape, res=None, scale=1.0):
    npairs = len(pairs)
    nred = grid[2]
    has_res = res is not None

    def body(*refs):
        ab = refs[:2 * npairs]
        res_ref = refs[2 * npairs] if has_res else None
        o_ref = refs[2 * npairs + int(has_res)]
        acc_ref = refs[-1] if nred > 1 else None
        tot = None
        for p in range(npairs):
            d = _dot(ab[2 * p][...].astype(BF16), ab[2 * p + 1][...].astype(BF16), dn)
            tot = d if tot is None else tot + d

        def finish(v):
            if scale != 1.0:
                v = v * scale
            if has_res:
                v = res_ref[...] + v
            o_ref[...] = v.astype(o_ref.dtype)

        if nred == 1:
            finish(tot)
        else:
            r = pl.program_id(2)

            @pl.when(r == 0)
            def _():
                acc_ref[...] = tot

            @pl.when(r > 0)
            def _():
                acc_ref[...] += tot

            @pl.when(r == nred - 1)
            def _():
                finish(acc_ref[...])

    ins, specs = [], []
    for a, a_spec, b, b_spec in pairs:
        ins += [a, b]
        specs += [a_spec, b_spec]
    if has_res:
        ins.append(res[0])
        specs.append(res[1])
    return _pcall(
        body, name=name, grid=grid, in_specs=specs, out_specs=out_spec, out_shape=out_shape,
        scratch_shapes=[pltpu.VMEM(acc_shape, F32)] if nred > 1 else [],
        compiler_params=_cparams("parallel", "parallel", "arbitrary"))(*ins)


def _ffn_up(h, wg, wu, name, tm):
    t, d = h.shape
    nc, _, fs = wg.shape
    tm = min(tm, t)

    def body(h_ref, wg_ref, wu_ref, g_ref, u_ref, a_ref):
        hh = h_ref[...]
        gate = _dot(hh, wg_ref[...], NN)
        up = _dot(hh, wu_ref[...], NN)
        g_ref[...] = gate.astype(BF16)
        u_ref[...] = up.astype(BF16)
        a_ref[...] = (gate * jax.nn.sigmoid(gate) * up).astype(BF16)

    wspec = pl.BlockSpec((None, d, fs), lambda c, i: (c, 0, 0))
    ospec = pl.BlockSpec((None, tm, fs), lambda c, i: (c, i, 0))
    osd = _sds((nc, t, fs), BF16)
    return _pcall(
        body, name=name, grid=(nc, t // tm),
        in_specs=[pl.BlockSpec((tm, d), lambda c, i: (i, 0)), wspec, wspec],
        out_specs=(ospec, ospec, ospec), out_shape=(osd, osd, osd),
        compiler_params=_cparams("parallel", "parallel"))(h, wg, wu)


def _ffn_dact(dy, wd, gate, up, name, tm):
    t, d = dy.shape
    nc, fs, _ = wd.shape
    tm = min(tm, t)

    def body(dy_ref, wd_ref, g_ref, u_ref, dg_ref, du_ref):
        da = _dot(dy_ref[...].astype(BF16), wd_ref[...], NT) * FFN_RESID
        gate = g_ref[...].astype(F32)
        up = u_ref[...].astype(F32)
        sig = jax.nn.sigmoid(gate)
        dg_ref[...] = (da * up * (sig * (1.0 + gate * (1.0 - sig)))).astype(BF16)
        du_ref[...] = (da * (gate * sig)).astype(BF16)

    cspec = pl.BlockSpec((None, tm, fs), lambda c, i: (c, i, 0))
    osd = _sds((nc, t, fs), BF16)
    return _pcall(
        body, name=name, grid=(nc, t // tm),
        in_specs=[pl.BlockSpec((tm, d), lambda c, i: (i, 0)),
                  pl.BlockSpec((None, fs, d), lambda c, i: (c, 0, 0)), cspec, cspec],
        out_specs=(cspec, cspec), out_shape=(osd, osd),
        compiler_params=_cparams("parallel", "parallel"))(dy, wd, gate, up)


def _ffn_fwd(x, g, wg, wu, wd, tag):
    t = x.shape[0]
    nc, _, fs = wg.shape
    tm = min(512, t)
    h = _rms_fwd(x, g, BF16, f"{tag}_norm", 512)
    gate, up, act = _ffn_up(h, wg, wu, f"{tag}_up", 512)
    y = _mm(f"{tag}_down", (t // tm, 1, nc),
            [(act, pl.BlockSpec((None, tm, fs), lambda i, j, r: (r, i, 0)),
              wd, pl.BlockSpec((None, fs, D_MODEL), lambda i, j, r: (r, 0, 0)))],
            NN, _sds((t, D_MODEL), F32), pl.BlockSpec((tm, D_MODEL), lambda i, j, r: (i, 0)),
            (tm, D_MODEL), res=(x, pl.BlockSpec((tm, D_MODEL), lambda i, j, r: (i, 0))), scale=FFN_RESID)
    return y, (h, gate, up, act)


def _ffn_bwd(dy, x, g, wg, wu, wd, saved, tag):
    h, gate, up, act = saved
    t = x.shape[0]
    nc, _, fs = wg.shape
    tm = min(512, t)
    tk = min(512, t)
    dgate, dup = _ffn_dact(dy, wd, gate, up, f"{tag}_dact", 512)
    tok_c = pl.BlockSpec((None, tk, fs), lambda c, j, r: (c, r, 0))
    tok_d = pl.BlockSpec((tk, D_MODEL), lambda c, j, r: (r, 0))
    dwd = _mm(f"{tag}_dwd", (nc, 1, t // tk), [(act, tok_c, dy, tok_d)], TN,
              _sds((nc, fs, D_MODEL), F32), pl.BlockSpec((None, fs, D_MODEL), lambda c, j, r: (c, 0, 0)),
              (fs, D_MODEL), scale=FFN_RESID)
    wout = pl.BlockSpec((None, D_MODEL, fs), lambda c, j, r: (c, 0, 0))
    dwg = _mm(f"{tag}_dwg", (nc, 1, t // tk), [(h, tok_d, dgate, tok_c)], TN,
              _sds((nc, D_MODEL, fs), F32), wout, (D_MODEL, fs))
    dwu = _mm(f"{tag}_dwu", (nc, 1, t // tk), [(h, tok_d, dup, tok_c)], TN,
              _sds((nc, D_MODEL, fs), F32), wout, (D_MODEL, fs))
    a_spec = pl.BlockSpec((None, tm, fs), lambda i, j, r: (r, i, 0))
    w_spec = pl.BlockSpec((None, D_MODEL, fs), lambda i, j, r: (r, 0, 0))
    dh = _mm(f"{tag}_dh", (t // tm, 1, nc), [(dgate, a_spec, wg, w_spec), (dup, a_spec, wu, w_spec)], NT,
             _sds((t, D_MODEL), F32), pl.BlockSpec((tm, D_MODEL), lambda i, j, r: (i, 0)), (tm, D_MODEL))
    dx, dg = _rms_bwd([dh], x, g, dy, f"{tag}_dnorm", 512)
    return dx, dg, dwg, dwu, dwd


def _mm_simple(name, a, b, dn, out_dtype, tm=512, tk=512, res=None, scale=1.0):
    if dn == TN:
        k, m = a.shape
        n = b.shape[1]
        tk = min(tk, k)
        return _mm(name, (1, 1, k // tk),
                   [(a, pl.BlockSpec((tk, m), lambda i, j, r: (r, 0)), b, pl.BlockSpec((tk, n), lambda i, j, r: (r, 0)))],
                   TN, _sds((m, n), out_dtype), pl.BlockSpec((m, n), lambda i, j, r: (0, 0)), (m, n), scale=scale)
    m, k = a.shape
    n = b.shape[1] if dn == NN else b.shape[0]
    tm = min(tm, m)
    row = pl.BlockSpec((tm, n), lambda i, j, r: (i, 0))
    return _mm(name, (m // tm, 1, 1),
               [(a, pl.BlockSpec((tm, k), lambda i, j, r: (i, 0)), b, pl.BlockSpec(b.shape, lambda i, j, r: (0, 0)))],
               dn, _sds((m, n), out_dtype), row, (tm, n), res=None if res is None else (res, row), scale=scale)


def _t5_bucket(dist):
    max_exact = REL_BUCKETS // 2
    d = np.maximum(dist, 1).astype(np.float32)
    large = max_exact + (np.log(d / max_exact) / np.log(REL_MAX_DIST / max_exact)
                         * (REL_BUCKETS - max_exact)).astype(np.int32)
    large = np.minimum(large, REL_BUCKETS - 1)
    return np.where(dist < max_exact, dist, large).astype(np.int32)


def _bucket_tiles():
    i = np.arange(QB)[:, None]
    j = np.arange(QB + DIL_W)[None, :]
    delta = np.clip(i + DIL_W - j, 0, None)
    return np.stack([_t5_bucket(delta * dil) for dil in DIL_DILATIONS]).astype(np.int32)


def _bias_tiles(rel_bias):
    buckets = jnp.asarray(_bucket_tiles())

    def body(rb_ref, bk_ref, o_ref):
        bk = bk_ref[...]
        for h in range(DIL_HEADS):
            def pick(b, tile):
                return jnp.where(bk == b, rb_ref[h, b], tile)

            o_ref[h] = lax.fori_loop(0, REL_BUCKETS, pick, jnp.zeros((QB, QB + DIL_W), F32))

    return _pcall(
        body, name="dil_bias_tiles", grid=(3,),
        in_specs=[pl.BlockSpec(memory_space=pltpu.SMEM),
                  pl.BlockSpec((None, QB, QB + DIL_W), lambda b: (b, 0, 0))],
        out_specs=pl.BlockSpec((None, DIL_HEADS, QB, QB + DIL_W), lambda b: (b, 0, 0, 0)),
        out_shape=_sds((3, DIL_HEADS, QB, QB + DIL_W), F32),
        compiler_params=_cparams("parallel"))(rel_bias, buckets)


def _bias_grad(dtiles):
    buckets = jnp.asarray(_bucket_tiles())

    def body(dt_ref, bk_ref, o_ref):
        for h in range(DIL_HEADS):
            def one(b, carry):
                tot = jnp.zeros((), F32)
                for br in range(3):
                    tot = tot + jnp.sum(jnp.where(bk_ref[br] == b, dt_ref[br, h], 0.0))
                o_ref[h, b] = tot
                return carry

            lax.fori_loop(0, REL_BUCKETS, one, 0)

    return _pcall(
        body, name="dil_bias_grad",
        in_specs=[pl.BlockSpec(memory_space=pltpu.VMEM), pl.BlockSpec(memory_space=pltpu.VMEM)],
        out_specs=pl.BlockSpec(memory_space=pltpu.SMEM),
        out_shape=_sds((DIL_HEADS, REL_BUCKETS), F32))(dtiles, buckets)


def _dil_masks(has_prev):
    ii = lax.broadcasted_iota(jnp.int32, (QB, QB), 0)
    jj = lax.broadcasted_iota(jnp.int32, (QB, QB), 1)
    return jj <= ii, jj >= ii + jnp.where(has_prev, 0, QB)


def _dil_fwd(q, k, v, bias, cls_len, name):
    nh, t, hd = q.shape
    nb = t // QB
    per = cls_len // QB
    scale = hd ** -0.5

    def body(q_ref, kc_ref, kp_ref, vc_ref, vp_ref, b_ref, o_ref, lse_ref):
        n = pl.program_id(0)
        cur_ok, prev_ok = _dil_masks((n % per) != 0)
        for h in range(nh):
            qh = q_ref[h]
            sc = _dot(qh, kc_ref[h], NT) * scale + b_ref[h, :, QB:]
            sp = _dot(qh, kp_ref[h], NT) * scale + b_ref[h, :, :QB]
            sc = jnp.where(cur_ok, sc, NEG)
            sp = jnp.where(prev_ok, sp, NEG)
            m = jnp.maximum(jnp.max(sc, axis=-1, keepdims=True), jnp.max(sp, axis=-1, keepdims=True))
            pc = jnp.exp(sc - m)
            pp = jnp.exp(sp - m)
            den = jnp.sum(pc, axis=-1, keepdims=True) + jnp.sum(pp, axis=-1, keepdims=True)
            o = _dot(pc.astype(BF16), vc_ref[h], NN) + _dot(pp.astype(BF16), vp_ref[h], NN)
            o_ref[h] = o / den
            lse_ref[h] = m + jnp.log(den)

    cur = pl.BlockSpec((nh, QB, hd), lambda n: (0, n, 0))
    prev = pl.BlockSpec((nh, QB, hd), lambda n: (0, jnp.maximum(n - 1, 0), 0))
    return _pcall(
        body, name=name, grid=(nb,),
        in_specs=[cur, cur, prev, cur, prev, pl.BlockSpec((nh, QB, QB + DIL_W), lambda n: (0, 0, 0))],
        out_specs=(cur, pl.BlockSpec((nh, QB, 1), lambda n: (0, n, 0))),
        out_shape=(_sds((nh, t, hd), F32), _sds((nh, t, 1), F32)),
        compiler_params=_cparams("parallel"))(q, k, k, v, v, bias)


def _dil_bwd(q, k, v, do, lse, dl, bias, cls_len, name):
    nh, t, hd = q.shape
    nb = t // QB
    per = cls_len // QB
    scale = hd ** -0.5

    def body(qc_ref, qn_ref, doc_ref, don_ref, lc_ref, ln_ref, dc_ref, dn_ref, k_ref, v_ref, b_ref,
             dq_ref, dk_ref, dv_ref, db_ref, carry):
        n = pl.program_id(0)
        nxt = n + 1
        cur_ok, prev_ok = _dil_masks((nxt < nb) & ((nxt % per) != 0))

        @pl.when(n == 0)
        def _():
            db_ref[...] = jnp.zeros_like(db_ref)
            carry[...] = jnp.zeros_like(carry)

        for h in range(nh):
            kh = k_ref[h]
            vh = v_ref[h]
            q1, q2 = qc_ref[h], qn_ref[h]
            do1, do2 = doc_ref[h], don_ref[h]
            s1 = jnp.where(cur_ok, _dot(q1, kh, NT) * scale + b_ref[h, :, QB:], NEG)
            s2 = jnp.where(prev_ok, _dot(q2, kh, NT) * scale + b_ref[h, :, :QB], NEG)
            p1 = jnp.exp(s1 - lc_ref[h])
            p2 = jnp.exp(s2 - ln_ref[h])
            ds1 = p1 * (_dot(do1, vh, NT) - dc_ref[h])
            ds2 = p2 * (_dot(do2, vh, NT) - dn_ref[h])
            ds1b = ds1.astype(BF16)
            ds2b = ds2.astype(BF16)
            dq_ref[h] = carry[h] + _dot(ds1b, kh, NN) * scale
            carry[h] = _dot(ds2b, kh, NN) * scale
            dk_ref[h] = (_dot(ds1b, q1, TN) + _dot(ds2b, q2, TN)) * scale
            dv_ref[h] = _dot(p1.astype(BF16), do1, TN) + _dot(p2.astype(BF16), do2, TN)
            db_ref[h, :, QB:] += ds1
            db_ref[h, :, :QB] += ds2

    def cur(w):
        return pl.BlockSpec((nh, QB, w), lambda n: (0, n, 0))

    def nxt(w):
        return pl.BlockSpec((nh, QB, w), lambda n: (0, jnp.minimum(n + 1, nb - 1), 0))

    tile = pl.BlockSpec((nh, QB, QB + DIL_W), lambda n: (0, 0, 0))
    o3 = _sds((nh, t, hd), F32)
    return _pcall(
        body, name=name, grid=(nb,),
        in_specs=[cur(hd), nxt(hd), cur(hd), nxt(hd), cur(1), nxt(1), cur(1), nxt(1), cur(hd), cur(hd), tile],
        out_specs=(cur(hd), cur(hd), cur(hd), tile),
        out_shape=(o3, o3, o3, _sds((nh, QB, QB + DIL_W), F32)),
        scratch_shapes=[pltpu.VMEM((nh, QB, hd), F32)],
        compiler_params=_cparams("arbitrary"))(q, q, do, do, lse, lse, dl, dl, k, v, bias)


def _dil_merge(outs, lses, tm):
    n, hd = outs[0].shape
    tm = min(tm, n)

    def body(o0, o1, o2, l0, l1, l2, o_ref, l_ref):
        a0, a1, a2 = l0[...], l1[...], l2[...]
        m = jnp.maximum(jnp.maximum(a0, a1), a2)
        e0, e1, e2 = jnp.exp(a0 - m), jnp.exp(a1 - m), jnp.exp(a2 - m)
        den = e0 + e1 + e2
        o_ref[...] = (e0 * o0[...] + e1 * o1[...] + e2 * o2[...]) / den
        l_ref[...] = m + jnp.log(den)

    ospec = pl.BlockSpec((tm, hd), lambda i: (i, 0))
    lspec = pl.BlockSpec((tm, 1), lambda i: (i, 0))
    return _pcall(
        body, name="dil_merge", grid=(n // tm,),
        in_specs=[ospec] * 3 + [lspec] * 3, out_specs=(ospec, lspec),
        out_shape=(_sds((n, hd), F32), _sds((n, 1), F32)),
        compiler_params=_cparams("parallel"))(*outs, *lses)


def _rowdot(a, b, name, tm):
    n, d = a.shape
    tm = min(tm, n)

    def body(a_ref, b_ref, o_ref):
        o_ref[...] = jnp.sum(a_ref[...].astype(F32) * b_ref[...].astype(F32), axis=-1, keepdims=True)

    spec = pl.BlockSpec((tm, d), lambda i: (i, 0))
    return _pcall(body, name=name, grid=(n // tm,), in_specs=[spec, spec],
                  out_specs=pl.BlockSpec((tm, 1), lambda i: (i, 0)), out_shape=_sds((n, 1), F32),
                  compiler_params=_cparams("parallel"))(a, b)


def _add3(a, b, c, name, tm):
    n, d = a.shape
    tm = min(tm, n)

    def body(a_ref, b_ref, c_ref, o_ref):
        o_ref[...] = a_ref[...] + b_ref[...] + c_ref[...]

    spec = pl.BlockSpec((tm, d), lambda i: (i, 0))
    return _pcall(body, name=name, grid=(n // tm,), in_specs=[spec] * 3, out_specs=spec,
                  out_shape=_sds((n, d), F32), compiler_params=_cparams("parallel"))(a, b, c)


def _to_sub(a, dil):
    nh, t, w = a.shape
    if dil == 1:
        return a
    return a.reshape(nh, t // dil, dil, w).transpose(0, 2, 1, 3).reshape(nh, t, w)


def _from_sub(a, dil):
    nh, t, w = a.shape
    if dil == 1:
        return a
    return a.reshape(nh, dil, t // dil, w).transpose(0, 2, 1, 3).reshape(nh, t, w)


def _rope_tables(t):
    inv = ROPE_BASE ** (-np.arange(0, MLA_ROPE, 2, dtype=np.float64) / MLA_ROPE)
    ang = np.arange(t, dtype=np.float64)[:, None] * inv[None, :]
    cos, sin = np.cos(ang), np.sin(ang)
    return (jnp.asarray(np.concatenate([cos, cos], 1), F32), jnp.asarray(np.concatenate([-sin, sin], 1), F32))


def _half_swap():
    p = np.zeros((MLA_ROPE, MLA_ROPE), np.float32)
    half = MLA_ROPE // 2
    for i in range(MLA_ROPE):
        p[(i + half) % MLA_ROPE, i] = 1.0
    return jnp.asarray(p)


def _mla_qk_fwd(x, g, cos_t, sin_t, scale, name, tm):
    n, d = x.shape
    t = cos_t.shape[0]
    tm = min(tm, t)
    nt = t // tm
    swap = _half_swap()

    def body(x_ref, g_ref, c_ref, s_ref, p_ref, o_ref):
        xf = x_ref[...]
        r = lax.rsqrt(jnp.mean(xf * xf, axis=-1, keepdims=True) + EPS)
        y = xf * r * g_ref[...]
        yr = y[:, MLA_NOPE:]
        sw = lax.dot_general(yr, p_ref[...], NN, precision=lax.Precision.HIGHEST, preferred_element_type=F32)
        o_ref[:, :MLA_NOPE] = (y[:, :MLA_NOPE] * scale).astype(o_ref.dtype)
        o_ref[:, MLA_NOPE:] = ((yr * c_ref[...] + sw * s_ref[...]) * scale).astype(o_ref.dtype)

    row = pl.BlockSpec((tm, d), lambda i: (i, 0))
    tab = pl.BlockSpec((tm, MLA_ROPE), lambda i: (i % nt, 0))
    return _pcall(
        body, name=name, grid=(n // tm,),
        in_specs=[row, pl.BlockSpec((1, d), lambda i: (0, 0)), tab, tab,
                  pl.BlockSpec((MLA_ROPE, MLA_ROPE), lambda i: (0, 0))],
        out_specs=row, out_shape=_sds((n, d), BF16),
        compiler_params=_cparams("parallel"))(x, g, cos_t, sin_t, swap)


def _mla_qk_bwd(dy, x, g, cos_t, sin_t, scale, name, tm):
    n, d = x.shape
    t = cos_t.shape[0]
    tm = min(tm, t)
    nt = t // tm
    swap_t = _half_swap().T

    def body(dy_ref, x_ref, g_ref, c_ref, s_ref, p_ref, dx_ref, dg_ref):
        xf = x_ref[...]
        gg = g_ref[...]
        r = lax.rsqrt(jnp.mean(xf * xf, axis=-1, keepdims=True) + EPS)
        xh = xf * r
        dyf = dy_ref[...] * scale
        dyr = dyf[:, MLA_NOPE:]
        back = lax.dot_general(dyr * s_ref[...], p_ref[...], NN, precision=lax.Precision.HIGHEST,
                               preferred_element_type=F32)
        dn_n = dyf[:, :MLA_NOPE]
        dn_r = dyr * c_ref[...] + back
        xh_n, xh_r = xh[:, :MLA_NOPE], xh[:, MLA_NOPE:]
        dxh_n = dn_n * gg[:, :MLA_NOPE]
        dxh_r = dn_r * gg[:, MLA_NOPE:]
        mean = (jnp.sum(dxh_n * xh_n, axis=-1, keepdims=True)
                + jnp.sum(dxh_r * xh_r, axis=-1, keepdims=True)) * (1.0 / d)
        dx_ref[:, :MLA_NOPE] = r * (dxh_n - xh_n * mean)
        dx_ref[:, MLA_NOPE:] = r * (dxh_r - xh_r * mean)

        @pl.when(pl.program_id(0) == 0)
        def _():
            dg_ref[...] = jnp.zeros_like(dg_ref)

        dg_ref[:, :MLA_NOPE] += jnp.sum(dn_n * xh_n, axis=0, keepdims=True)
        dg_ref[:, MLA_NOPE:] += jnp.sum(dn_r * xh_r, axis=0, keepdims=True)

    row = pl.BlockSpec((tm, d), lambda i: (i, 0))
    vec = pl.BlockSpec((1, d), lambda i: (0, 0))
    tab = pl.BlockSpec((tm, MLA_ROPE), lambda i: (i % nt, 0))
    return _pcall(
        body, name=name, grid=(n // tm,),
        in_specs=[row, row, vec, tab, tab, pl.BlockSpec((MLA_ROPE, MLA_ROPE), lambda i: (0, 0))],
        out_specs=(row, vec), out_shape=(_sds((n, d), F32), _sds((1, d), F32)),
        compiler_params=_cparams("arbitrary"))(dy, x, g, cos_t, sin_t, swap_t)


def _causal_mask(i, j, tq, tk):
    row = i * tq + lax.broadcasted_iota(jnp.int32, (tq, tk), 0)
    col = j * tk + lax.broadcasted_iota(jnp.int32, (tq, tk), 1)
    return col <= row


def _causal_steps(nq, nk, tq, tk, q_major):
    if q_major:
        groups = [[(i, j) for j in range((i * tq + tq - 1) // tk + 1)] for i in range(nq)]
    else:
        groups = [[(i, j) for i in range((j * tk) // tq, nq)] for j in range(nk)]
    it, jt, fl = [], [], []
    for g in groups:
        for n, (i, j) in enumerate(g):
            it.append(i)
            jt.append(j)
            fl.append((n == 0) + 2 * (n == len(g) - 1) + 4 * (j * tk + tk - 1 > i * tq))
    return tuple(jnp.asarray(np.array(a, np.int32)) for a in (it, jt, fl))


def _causal_specs(tq, tk):
    def qs(w):
        return pl.BlockSpec((None, tq, w), lambda h, s, it, jt, fl: (h, it[s], 0))

    def kv(w):
        return pl.BlockSpec((None, tk, w), lambda h, s, it, jt, fl: (h, jt[s], 0))

    return qs, kv


def _mla_fwd(q, k, v, tq, tk):
    nh, t, dq = q.shape
    dv = v.shape[2]
    tq, tk = min(tq, t), min(tk, t)
    tables = _causal_steps(t // tq, t // tk, tq, tk, True)

    def body(it, jt, fl, q_ref, k_ref, v_ref, o_ref, lse_ref, m_sc, l_sc, acc_sc):
        step = pl.program_id(1)
        i, j, flags = it[step], jt[step], fl[step]

        @pl.when((flags & 1) != 0)
        def _():
            m_sc[...] = jnp.full_like(m_sc, NEG)
            l_sc[...] = jnp.zeros_like(l_sc)
            acc_sc[...] = jnp.zeros_like(acc_sc)

        def update(masked):
            s = _dot(q_ref[...], k_ref[...], NT)
            if masked:
                s = jnp.where(_causal_mask(i, j, tq, tk), s, NEG)
            m_prev = m_sc[...]
            m_new = jnp.maximum(m_prev, jnp.max(s, axis=-1, keepdims=True))
            alpha = jnp.exp(m_prev - m_new)
            p = jnp.exp(s - m_new)
            l_sc[...] = alpha * l_sc[...] + jnp.sum(p, axis=-1, keepdims=True)
            acc_sc[...] = alpha * acc_sc[...] + _dot(p.astype(BF16), v_ref[...], NN)
            m_sc[...] = m_new

        pl.when((flags & 4) != 0)(functools.partial(update, True))
        pl.when((flags & 4) == 0)(functools.partial(update, False))

        @pl.when((flags & 2) != 0)
        def _():
            o_ref[...] = acc_sc[...] / l_sc[...]
            lse_ref[...] = m_sc[...] + jnp.log(l_sc[...])

    qs, kv = _causal_specs(tq, tk)
    return _pcall(
        body, name="mla_attn_fwd",
        grid_spec=pltpu.PrefetchScalarGridSpec(
            num_scalar_prefetch=3, grid=(nh, tables[0].shape[0]),
            in_specs=[qs(dq), kv(dq), kv(dv)], out_specs=(qs(dv), qs(1)),
            scratch_shapes=[pltpu.VMEM((tq, 1), F32), pltpu.VMEM((tq, 1), F32), pltpu.VMEM((tq, dv), F32)]),
        out_shape=(_sds((nh, t, dv), F32), _sds((nh, t, 1), F32)),
        compiler_params=_cparams("parallel", "arbitrary"))(*tables, q, k, v)


def _mla_bwd_dq(q, k, v, do, lse, dl, tq, tk):
    nh, t, dq = q.shape
    dv = v.shape[2]
    tq, tk = min(tq, t), min(tk, t)
    tables = _causal_steps(t // tq, t // tk, tq, tk, True)

    def body(it, jt, fl, q_ref, k_ref, v_ref, do_ref, lse_ref, dl_ref, dq_ref, acc_sc):
        step = pl.program_id(1)
        i, j, flags = it[step], jt[step], fl[step]

        def update(masked):
            s = _dot(q_ref[...], k_ref[...], NT)
            if masked:
                s = jnp.where(_causal_mask(i, j, tq, tk), s, NEG)
            p = jnp.exp(s - lse_ref[...])
            dp = _dot(do_ref[...].astype(BF16), v_ref[...], NT)
            ds = p * (dp - dl_ref[...])
            part = _dot(ds.astype(BF16), k_ref[...], NN)

            @pl.when((flags & 1) != 0)
            def _():
                acc_sc[...] = part

            @pl.when((flags & 1) == 0)
            def _():
                acc_sc[...] += part

        pl.when((flags & 4) != 0)(functools.partial(update, True))
        pl.when((flags & 4) == 0)(functools.partial(update, False))

        @pl.when((flags & 2) != 0)
        def _():
            dq_ref[...] = acc_sc[...]

    qs, kv = _causal_specs(tq, tk)
    return _pcall(
        body, name="mla_attn_dq",
        grid_spec=pltpu.PrefetchScalarGridSpec(
            num_scalar_prefetch=3, grid=(nh, tables[0].shape[0]),
            in_specs=[qs(dq), kv(dq), kv(dv), qs(dv), qs(1), qs(1)], out_specs=qs(dq),
            scratch_shapes=[pltpu.VMEM((tq, dq), F32)]),
        out_shape=_sds((nh, t, dq), F32),
        compiler_params=_cparams("parallel", "arbitrary"))(*tables, q, k, v, do, lse, dl)


def _mla_bwd_dkv(q, k, v, do, lse_row, dl_row, tq, tk):
    nh, t, dq = q.shape
    dv = v.shape[2]
    tq, tk = min(tq, t), min(tk, t)
    tables = _causal_steps(t // tq, t // tk, tq, tk, False)

    def body(it, jt, fl, q_ref, k_ref, v_ref, do_ref, lse_ref, dl_ref, dk_ref, dv_ref, dk_sc, dv_sc):
        step = pl.program_id(1)
        i, j, flags = it[step], jt[step], fl[step]

        def update(masked):
            st = _dot(k_ref[...], q_ref[...], NT)
            if masked:
                key = j * tk + lax.broadcasted_iota(jnp.int32, (tk, tq), 0)
                qry = i * tq + lax.broadcasted_iota(jnp.int32, (tk, tq), 1)
                st = jnp.where(key <= qry, st, NEG)
            pt = jnp.exp(st - lse_ref[...])
            dob = do_ref[...].astype(BF16)
            dpt = _dot(v_ref[...], dob, NT)
            dst = pt * (dpt - dl_ref[...])
            dv_part = _dot(pt.astype(BF16), dob, NN)
            dk_part = _dot(dst.astype(BF16), q_ref[...], NN)

            @pl.when((flags & 1) != 0)
            def _():
                dv_sc[...] = dv_part
                dk_sc[...] = dk_part

            @pl.when((flags & 1) == 0)
            def _():
                dv_sc[...] += dv_part
                dk_sc[...] += dk_part

        pl.when((flags & 4) != 0)(functools.partial(update, True))
        pl.when((flags & 4) == 0)(functools.partial(update, False))

        @pl.when((flags & 2) != 0)
        def _():
            dk_ref[...] = dk_sc[...]
            dv_ref[...] = dv_sc[...]

    qs, kv = _causal_specs(tq, tk)
    rowv = pl.BlockSpec((None, 1, tq), lambda h, s, it, jt, fl: (h, 0, it[s]))
    return _pcall(
        body, name="mla_attn_dkv",
        grid_spec=pltpu.PrefetchScalarGridSpec(
            num_scalar_prefetch=3, grid=(nh, tables[0].shape[0]),
            in_specs=[qs(dq), kv(dq), kv(dv), qs(dv), rowv, rowv], out_specs=(kv(dq), kv(dv)),
            scratch_shapes=[pltpu.VMEM((tk, dq), F32), pltpu.VMEM((tk, dv), F32)]),
        out_shape=(_sds((nh, t, dq), F32), _sds((nh, t, dv), F32)),
        compiler_params=_cparams("parallel", "arbitrary"))(*tables, q, k, v, do, lse_row, dl_row)


def _loss_head(y, target, tm):
    t, d = y.shape
    tm = min(tm, t)
    nt = t // tm

    def body(y_ref, t_ref, dy_ref, loss_ref, acc):
        i = pl.program_id(0)
        err = y_ref[...] - t_ref[...]
        dy_ref[...] = err * (1.0 / d)

        @pl.when(i == 0)
        def _():
            acc[...] = jnp.zeros_like(acc)

        acc[...] += jnp.sum(err * err, axis=0, keepdims=True)

        @pl.when(i == nt - 1)
        def _():
            loss_ref[0, 0] = jnp.sum(acc[...]) * (0.5 / d)

    spec = pl.BlockSpec((tm, d), lambda i: (i, 0))
    return _pcall(
        body, name="loss_head", grid=(nt,), in_specs=[spec, spec],
        out_specs=(spec, pl.BlockSpec(memory_space=pltpu.SMEM)),
        out_shape=(_sds((t, d), F32), _sds((1, 1), F32)),
        scratch_shapes=[pltpu.VMEM((1, d), F32)],
        compiler_params=_cparams("arbitrary"))(y, target)


def _adamw(w, g, m, v, name):
    r, c = w.shape
    tr = r
    for cand in (256, 128, 64, 32, 16, 8):
        if r % cand == 0:
            tr = cand
            break

    def body(w_ref, g_ref, m_ref, v_ref, d_ref, nm_ref, nv_ref):
        gg = g_ref[...]
        nm = ADAM_B1 * m_ref[...] + (1.0 - ADAM_B1) * gg
        nv = ADAM_B2 * v_ref[...] + (1.0 - ADAM_B2) * (gg * gg)
        m_hat = nm / (1.0 - ADAM_B1 ** ADAM_STEP)
        v_hat = nv / (1.0 - ADAM_B2 ** ADAM_STEP)
        d_ref[...] = -ADAM_LR * (m_hat / (jnp.sqrt(v_hat) + ADAM_EPS) + ADAM_WD * w_ref[...])
        nm_ref[...] = nm
        nv_ref[...] = nv

    spec = pl.BlockSpec((tr, c), lambda i: (i, 0))
    sd = _sds((r, c), F32)
    return _pcall(body, name=name, grid=(r // tr,), in_specs=[spec] * 4, out_specs=(spec,) * 3,
                  out_shape=(sd, sd, sd), compiler_params=_cparams("parallel"))(w, g, m, v)


MESH_ID = pl.DeviceIdType.MESH
HBM_SPEC = pl.BlockSpec(memory_space=pltpu.HBM)


def _place():
    return lax.axis_index("x"), lax.axis_index("y"), lax.axis_index("c")


def _other_chips(x, y):
    return [(1 - x, y), (x, 1 - y), (1 - x, 1 - y)]


def _remote(src, dst, send_sems, recv_sems, k, to):
    return pltpu.make_async_remote_copy(src_ref=src, dst_ref=dst, send_sem=send_sems.at[k], recv_sem=recv_sems.at[k],
                                        device_id=to, device_id_type=MESH_ID)


D2D_SPLIT = 16
ICI_SPLIT = 4


def _chunks(rows, n):
    assert rows % n == 0
    return [(i * (rows // n), rows // n) for i in range(n)]


def _gather_weights(packed):
    rows, lanes = packed.shape
    half = rows // 2

    def body(src, out, send_sems, recv_sems):
        x, y, c = _place()
        me = 2 * x + y
        sibling = (x, y, 1 - c)
        chips = _other_chips(x, y)

        def part(chip, core, lo=0, n=half):
            return out.at[chip, pl.ds(core * half + lo, n), :]

        for k, (cx, cy) in enumerate(chips):
            for lo, n in _chunks(half, ICI_SPLIT):
                _remote(src.at[pl.ds(c * half + lo, n), :], part(me, c, lo, n), send_sems, recv_sems, k,
                        (cx, cy, c)).start()
        for k, (cx, cy) in enumerate(chips):
            got = part(2 * cx + cy, c)
            _remote(got, got, send_sems, recv_sems, k, (x, y, c)).wait_recv()
            for lo, n in _chunks(half, D2D_SPLIT):
                piece = part(2 * cx + cy, c, lo, n)
                _remote(piece, piece, send_sems, recv_sems, 3 + k, sibling).start()
        for k, (cx, cy) in enumerate(chips):
            got = part(2 * cx + cy, 1 - c)
            _remote(got, got, send_sems, recv_sems, 3 + k, (x, y, c)).wait_recv()
        for k in range(6):
            sent = part(me, c)
            _remote(sent, sent, send_sems, recv_sems, k, (x, y, c)).wait_send()

    return _pcall(
        body, name="gather_weights", in_specs=[HBM_SPEC], out_specs=HBM_SPEC,
        out_shape=_sds((N_CHIPS, rows, lanes), packed.dtype),
        scratch_shapes=[pltpu.SemaphoreType.DMA((6,)), pltpu.SemaphoreType.DMA((6,))],
    )(packed)


def _reduce_cores(grads):
    nchip, rows, lanes = grads.shape
    half = rows // 2

    def body(g, theirs, send_sems, recv_sems):
        x, y, c = _place()
        for j in range(nchip):
            for lo, n in _chunks(half, D2D_SPLIT):
                _remote(g.at[j, pl.ds((1 - c) * half + lo, n), :], theirs.at[j, pl.ds(lo, n), :],
                        send_sems, recv_sems, 0, (x, y, 1 - c)).start()
        _remote(g.at[:, pl.ds((1 - c) * half, half), :], theirs, send_sems, recv_sems, 0, (x, y, c)).wait()

    return _pcall(
        body, name="reduce_cores", in_specs=[HBM_SPEC], out_specs=HBM_SPEC,
        out_shape=_sds((nchip, half, lanes), grads.dtype),
        scratch_shapes=[pltpu.SemaphoreType.DMA((1,)), pltpu.SemaphoreType.DMA((1,))],
    )(grads)


def _scatter_chips(part):
    nchip, half, lanes = part.shape

    def body(p, out, send_sems, recv_sems):
        x, y, c = _place()
        for k, (cx, cy) in enumerate(_other_chips(x, y)):
            for lo, n in _chunks(half, ICI_SPLIT):
                _remote(p.at[2 * cx + cy, pl.ds(lo, n), :], out.at[k, pl.ds(lo, n), :],
                        send_sems, recv_sems, k, (cx, cy, c)).start()
        for k in range(3):
            _remote(p.at[k], out.at[k], send_sems, recv_sems, k, (x, y, c)).wait()

    return _pcall(
        body, name="scatter_chips", in_specs=[HBM_SPEC], out_specs=HBM_SPEC,
        out_shape=_sds((3, half, lanes), part.dtype),
        scratch_shapes=[pltpu.SemaphoreType.DMA((3,)), pltpu.SemaphoreType.DMA((3,))],
    )(part)


def _sum_partials(received, part, place, tm):
    _, half, lanes = received.shape
    tm = min(tm, half)
    nblk = half // tm

    def body(place_ref, r_ref, p_ref, o_ref):
        tot = p_ref[...].astype(F32)
        for k in range(3):
            tot = tot + r_ref[k].astype(F32)
        o_ref[...] = tot

    return _pcall(
        body, name="sum_chip_partials",
        grid_spec=pltpu.PrefetchScalarGridSpec(
            num_scalar_prefetch=1, grid=(nblk,),
            in_specs=[pl.BlockSpec((3, tm, lanes), lambda i, pc: (0, i, 0)),
                      pl.BlockSpec((None, tm, lanes), lambda i, pc: (pc[0], i, 0))],
            out_specs=pl.BlockSpec((tm, lanes), lambda i, pc: (pc[1] * nblk + i, 0))),
        out_shape=_sds((2 * half, lanes), F32),
        compiler_params=_cparams("parallel"))(place, received, part)


def _share_cores(block):
    rows, lanes = block.shape
    half = rows // 2

    def body(src, out, send_sems, recv_sems):
        x, y, c = _place()
        for lo, n in _chunks(half, D2D_SPLIT):
            piece = pl.ds(c * half + lo, n)
            _remote(src.at[piece, :], out.at[piece, :], send_sems, recv_sems, 0, (x, y, 1 - c)).start()
        mine = out.at[pl.ds(c * half, half), :]
        theirs = out.at[pl.ds((1 - c) * half, half), :]
        _remote(mine, theirs, send_sems, recv_sems, 0, (x, y, c)).wait()

    return _pcall(
        body, name="share_cores", in_specs=[HBM_SPEC], out_specs=HBM_SPEC,
        out_shape=_sds((rows, lanes), block.dtype), input_output_aliases={0: 0},
        scratch_shapes=[pltpu.SemaphoreType.DMA((1,)), pltpu.SemaphoreType.DMA((1,))],
    )(block)


def _sum_blocks(stacked, name, tm):
    n, rows, lanes = stacked.shape
    tm = min(tm, rows)

    def body(s_ref, o_ref):
        tot = s_ref[n - 1].astype(F32)
        for k in range(n - 1):
            tot = tot + s_ref[k].astype(F32)
        o_ref[...] = tot

    return _pcall(body, name=name, grid=(rows // tm,),
                  in_specs=[pl.BlockSpec((n, tm, lanes), lambda i: (0, i, 0))],
                  out_specs=pl.BlockSpec((tm, lanes), lambda i: (i, 0)), out_shape=_sds((rows, lanes), F32),
                  compiler_params=_cparams("parallel"))(stacked)


def _add_halves(grads, theirs, core, tm):
    n, half, lanes = theirs.shape
    tm = min(tm, half)
    nblk = half // tm

    def body(c_ref, g_ref, t_ref, o_ref):
        o_ref[...] = (g_ref[...] + t_ref[...]).astype(o_ref.dtype)

    spec = pl.BlockSpec((None, tm, lanes), lambda k, i, c: (k, i, 0))
    return _pcall(
        body, name="add_core_halves",
        grid_spec=pltpu.PrefetchScalarGridSpec(
            num_scalar_prefetch=1, grid=(n, nblk),
            in_specs=[pl.BlockSpec((None, tm, lanes), lambda k, i, c: (k, c[0] * nblk + i, 0)), spec], out_specs=spec),
        out_shape=_sds((n, half, lanes), BF16),
        compiler_params=_cparams("parallel", "parallel"))(core, grads, theirs)


def _allreduce_small(part):
    rows, lanes = part.shape
    ndev = 8

    def body(src, tot, buf, send_sems, recv_sems):
        x, y, c = _place()
        me = 4 * x + 2 * y + c
        buf[me] = src[...]
        sends = []
        for k in range(1, ndev):
            peer = (x ^ (k >> 2), y ^ ((k >> 1) & 1), c ^ (k & 1))
            cp = _remote(src, buf.at[me], send_sems, recv_sems, k - 1, peer)
            cp.start()
            sends.append(cp)
        for k in range(1, ndev):
            theirs = buf.at[me ^ k]
            _remote(theirs, theirs, send_sems, recv_sems, k - 1, (x, y, c)).wait_recv()
        for cp in sends:
            cp.wait_send()
        acc = buf[0]
        for d in range(1, ndev):
            acc = acc + buf[d]
        tot[...] = acc

    vm = pl.BlockSpec(memory_space=pltpu.VMEM)
    return _pcall(
        body, name="allreduce_small", in_specs=[vm], out_specs=vm, out_shape=_sds((rows, lanes), F32),
        scratch_shapes=[pltpu.VMEM((ndev, rows, lanes), F32), pltpu.SemaphoreType.DMA((ndev - 1,)),
                        pltpu.SemaphoreType.DMA((ndev - 1,))],
    )(part)


def _big_rows():
    return [int(np.prod(shape)) // LANES for _, shape in BIG]


def _pack_big(blocks, dtype):
    parts = [blocks[name].reshape(blocks[name].shape[0], -1, LANES).astype(dtype) for name, _ in BIG]
    return jnp.concatenate(parts, axis=1)


def _unpack_big(packed):
    out, off = {}, 0
    for (name, shape), r in zip(BIG, _big_rows()):
        out[name] = packed[:, off:off + r].reshape((packed.shape[0],) + shape)
        off += r
    return out


def _pack_small(vals):
    parts = []
    for name, shape, r in SMALL:
        flat = vals[name].reshape(-1).astype(F32)
        parts.append(jnp.pad(flat, (0, r * LANES - flat.shape[0])).reshape(r, LANES))
    used = sum(r for _, _, r in SMALL)
    parts.append(jnp.zeros((SMALL_ROWS - used, LANES), F32))
    return jnp.concatenate(parts, axis=0)


def _unpack_small(packed):
    out, off = {}, 0
    for name, shape, r in SMALL:
        n = int(np.prod(shape))
        out[name] = packed[off:off + r].reshape(-1)[:n].reshape(shape)
        off += r
    return out


def _heads_major(a, nh):
    t = a.shape[0]
    return a.reshape(t, nh, a.shape[1] // nh).transpose(1, 0, 2)


def _tokens_major(a):
    nh, t, w = a.shape
    return a.transpose(1, 0, 2).reshape(t, nh * w)


def _local_step(x, target, small, wfull):
    t = x.shape[0]
    nh, hd = DIL_HEADS, DIL_HD
    w_in = wfull["w_in"].transpose(1, 0, 2).reshape(D_MODEL, -1)
    w_out = wfull["w_out"].reshape(D_MODEL, D_MODEL)
    w_qb, w_kvb = wfull["mla_w_q_b"], wfull["mla_w_kv_b"]
    grads_s, grads_b = {}, {}

    x1, ffn1_saved = _ffn_fwd(x, small["ffn1_norm"], wfull["ffn1_w_gate"], wfull["ffn1_w_up"],
                              wfull["ffn1_w_down"], "ffn1")
    hm = _rms_fwd(x1, small["mix_norm"], BF16, "mix_norm", 512)
    proj = _mm_simple("in_proj", hm, w_in, NN, F32)
    q_a, k_a, v_a = proj[:, :512], proj[:, 512:1024], proj[:, 1024:1536]
    cq, ckv, k_pe = proj[:, 1536:1792], proj[:, 1792:1920], proj[:, 1920:1984]

    q_h = _heads_major(q_a, nh).reshape(nh * t, hd)
    k_h = _heads_major(k_a, nh).reshape(nh * t, hd)
    v_h = _heads_major(v_a, nh).astype(BF16)
    qn = _rms_fwd(q_h, small["dil_q_norm"], BF16, "dil_q_norm", 2048).reshape(nh, t, hd)
    kn = _rms_fwd(k_h, small["dil_k_norm"], BF16, "dil_k_norm", 2048).reshape(nh, t, hd)
    bias = _bias_tiles(small["rel_bias"])
    branch_in, outs, lses = [], [], []
    for b, dil in enumerate(DIL_DILATIONS):
        qs, ks, vs = _to_sub(qn, dil), _to_sub(kn, dil), _to_sub(v_h, dil)
        o_b, lse_b = _dil_fwd(qs, ks, vs, bias[b], t // dil, f"dil_fwd_{dil}")
        branch_in.append((qs, ks, vs))
        outs.append(_from_sub(o_b, dil).reshape(nh * t, hd))
        lses.append(_from_sub(lse_b, dil).reshape(nh * t, 1))
    o_dil_h, lse_tot = _dil_merge(outs, lses, 2048)
    o_dil = _tokens_major(o_dil_h.reshape(nh, t, hd))

    mh = MLA_HEADS
    cos_t, sin_t = _rope_tables(t)
    cqn = _rms_fwd(cq, small["mla_q_a_norm"], BF16, "mla_q_a_norm", 512)
    ckvn = _rms_fwd(ckv, small["mla_kv_a_norm"], BF16, "mla_kv_a_norm", 512)
    tm = min(512, t)

    def head_proj(name, a, w, width):
        k = a.shape[1]
        return _mm(name, (mh, t // tm, 1),
                   [(a, pl.BlockSpec((tm, k), lambda h, i, r: (i, 0)), w, pl.BlockSpec((None, k, width), lambda h, i, r: (h, 0, 0)))],
                   NN, _sds((mh, t, width), F32), pl.BlockSpec((None, tm, width), lambda h, i, r: (h, i, 0)), (tm, width))

    q_raw = head_proj("mla_q_proj", cqn, w_qb, MLA_QK)
    kv_raw = head_proj("mla_kv_proj", ckvn, w_kvb, MLA_NOPE + MLA_V)
    k_raw = jnp.concatenate([kv_raw[:, :, :MLA_NOPE], jnp.broadcast_to(k_pe[None], (mh, t, MLA_ROPE))], axis=2)
    v_m = kv_raw[:, :, MLA_NOPE:].astype(BF16)
    q_raw2, k_raw2 = q_raw.reshape(mh * t, MLA_QK), k_raw.reshape(mh * t, MLA_QK)
    q_scale = MLA_QK ** -0.5
    q_m = _mla_qk_fwd(q_raw2, small["mla_q_norm"], cos_t, sin_t, q_scale, "mla_q_rope", 512).reshape(mh, t, MLA_QK)
    k_m = _mla_qk_fwd(k_raw2, small["mla_k_norm"], cos_t, sin_t, 1.0, "mla_k_rope", 512).reshape(mh, t, MLA_QK)
    o_mla_h, lse_m = _mla_fwd(q_m, k_m, v_m, 512, 2048)
    o_mla = _tokens_major(o_mla_h)

    od = _rms_fwd(o_dil, small["out_norm_dil"], BF16, "out_norm_dil", 512)
    om = _rms_fwd(o_mla, small["out_norm_mla"], BF16, "out_norm_mla", 512)
    half_w = DIL_WIDTH
    row = pl.BlockSpec((tm, D_MODEL), lambda i, j, r: (i, 0))
    act_spec = pl.BlockSpec((tm, half_w), lambda i, j, r: (i, 0))
    x2 = _mm("out_proj", (t // tm, 1, 1),
             [(od, act_spec, w_out, pl.BlockSpec((half_w, D_MODEL), lambda i, j, r: (0, 0))),
              (om, act_spec, w_out, pl.BlockSpec((half_w, D_MODEL), lambda i, j, r: (1, 0)))],
             NN, _sds((t, D_MODEL), F32), row, (tm, D_MODEL), res=(x1, row))
    x3, ffn2_saved = _ffn_fwd(x2, small["ffn2_norm"], wfull["ffn2_w_gate"], wfull["ffn2_w_up"],
                              wfull["ffn2_w_down"], "ffn2")
    dy, loss = _loss_head(x3, target, 512)

    dx2, grads_s["ffn2_norm"], grads_b["ffn2_w_gate"], grads_b["ffn2_w_up"], grads_b["ffn2_w_down"] = _ffn_bwd(
        dy, x2, small["ffn2_norm"], wfull["ffn2_w_gate"], wfull["ffn2_w_up"], wfull["ffn2_w_down"], ffn2_saved, "ffn2")

    d_ocat = _mm_simple("out_proj_dx", dx2, w_out, NT, F32)
    tk = min(512, t)
    tok = pl.BlockSpec((tk, half_w), lambda c, j, r: (r, 0))
    dw_out_d = _mm_simple("out_proj_dw_dil", od, dx2, TN, F32)
    dw_out_m = _mm_simple("out_proj_dw_mla", om, dx2, TN, F32)
    grads_b["w_out"] = jnp.concatenate([dw_out_d, dw_out_m], axis=0).reshape(N_CHIPS, D_MODEL // N_CHIPS, D_MODEL)
    do_dil, grads_s["out_norm_dil"] = _rms_bwd([d_ocat[:, :half_w]], o_dil, small["out_norm_dil"], None, "out_norm_dil_bwd", 512)
    do_mla, grads_s["out_norm_mla"] = _rms_bwd([d_ocat[:, half_w:]], o_mla, small["out_norm_mla"], None, "out_norm_mla_bwd", 512)

    do_m = _heads_major(do_mla, mh)
    dl_m = _rowdot(do_m.reshape(mh * t, MLA_V), o_mla_h.reshape(mh * t, MLA_V), "mla_delta", 2048).reshape(mh, t, 1)
    dq_m = _mla_bwd_dq(q_m, k_m, v_m, do_m, lse_m, dl_m, 512, 2048)
    dk_m, dv_m = _mla_bwd_dkv(q_m, k_m, v_m, do_m, lse_m.reshape(mh, 1, t), dl_m.reshape(mh, 1, t), 2048, 512)
    dq_raw, grads_s["mla_q_norm"] = _mla_qk_bwd(dq_m.reshape(mh * t, MLA_QK), q_raw2, small["mla_q_norm"],
                                                 cos_t, sin_t, q_scale, "mla_q_rope_bwd", 512)
    dk_raw, grads_s["mla_k_norm"] = _mla_qk_bwd(dk_m.reshape(mh * t, MLA_QK), k_raw2, small["mla_k_norm"],
                                                 cos_t, sin_t, 1.0, "mla_k_rope_bwd", 512)
    dq_raw = dq_raw.reshape(mh, t, MLA_QK)
    dk_raw = dk_raw.reshape(mh, t, MLA_QK)
    dkv_raw = jnp.concatenate([dk_raw[:, :, :MLA_NOPE], dv_m], axis=2)
    dk_pe_h = dk_raw[:, :, MLA_NOPE:]

    def head_proj_dx(name, d, w):
        width, k = d.shape[2], w.shape[1]
        return _mm(name, (t // tm, 1, mh),
                   [(d, pl.BlockSpec((None, tm, width), lambda i, j, r: (r, i, 0)), w, pl.BlockSpec((None, k, width), lambda i, j, r: (r, 0, 0)))],
                   NT, _sds((t, k), F32), pl.BlockSpec((tm, k), lambda i, j, r: (i, 0)), (tm, k))

    def head_proj_dw(name, a, d):
        width, k = d.shape[2], a.shape[1]
        return _mm(name, (mh, 1, t // tk),
                   [(a, pl.BlockSpec((tk, k), lambda h, j, r: (r, 0)), d, pl.BlockSpec((None, tk, width), lambda h, j, r: (h, r, 0)))],
                   TN, _sds((mh, k, width), F32), pl.BlockSpec((None, k, width), lambda h, j, r: (h, 0, 0)), (k, width))

    d_cqn = head_proj_dx("mla_q_proj_dx", dq_raw, w_qb)
    d_ckvn = head_proj_dx("mla_kv_proj_dx", dkv_raw, w_kvb)
    grads_b["mla_w_q_b"] = head_proj_dw("mla_q_proj_dw", cqn, dq_raw)
    grads_b["mla_w_kv_b"] = head_proj_dw("mla_kv_proj_dw", ckvn, dkv_raw)
    d_cq, grads_s["mla_q_a_norm"] = _rms_bwd([d_cqn], cq, small["mla_q_a_norm"], None, "mla_q_a_norm_bwd", 512)
    d_ckv, grads_s["mla_kv_a_norm"] = _rms_bwd([d_ckvn], ckv, small["mla_kv_a_norm"], None, "mla_kv_a_norm_bwd", 512)
    d_kpe = _sum_blocks(dk_pe_h.reshape(mh, t * MLA_ROPE // LANES, LANES), "mla_kpe_sum", 1024).reshape(t, MLA_ROPE)

    do_h = _heads_major(do_dil, nh)
    dl_d = _rowdot(do_h.reshape(nh * t, hd), o_dil_h, "dil_delta", 2048).reshape(nh, t, 1)
    lse_t = lse_tot.reshape(nh, t, 1)
    do_hb = do_h.astype(BF16)
    dqs, dks, dvs, dtiles = [], [], [], []
    for b, dil in enumerate(DIL_DILATIONS):
        qs, ks, vs = branch_in[b]
        dq_b, dk_b, dv_b, db_b = _dil_bwd(qs, ks, vs, _to_sub(do_hb, dil), _to_sub(lse_t, dil), _to_sub(dl_d, dil),
                                          bias[b], t // dil, f"dil_bwd_{dil}")
        dqs.append(_from_sub(dq_b, dil).reshape(nh * t, hd))
        dks.append(_from_sub(dk_b, dil).reshape(nh * t, hd))
        dvs.append(_from_sub(dv_b, dil).reshape(nh * t, hd))
        dtiles.append(db_b)
    grads_s["rel_bias"] = _bias_grad(jnp.stack(dtiles))
    dq_a_h, grads_s["dil_q_norm"] = _rms_bwd(dqs, q_h, small["dil_q_norm"], None, "dil_q_norm_bwd", 2048)
    dk_a_h, grads_s["dil_k_norm"] = _rms_bwd(dks, k_h, small["dil_k_norm"], None, "dil_k_norm_bwd", 2048)
    dv_a_h = _add3(dvs[0], dvs[1], dvs[2], "dil_dv_sum", 2048)
    dproj = jnp.concatenate([_tokens_major(dq_a_h.reshape(nh, t, hd)), _tokens_major(dk_a_h.reshape(nh, t, hd)),
                             _tokens_major(dv_a_h.reshape(nh, t, hd)), d_cq, d_ckv, d_kpe], axis=1)

    d_hm = _mm_simple("in_proj_dx", dproj, w_in, NT, F32)
    dw_in = _mm_simple("in_proj_dw", hm, dproj, TN, F32)
    grads_b["w_in"] = dw_in.reshape(D_MODEL, N_CHIPS, -1).transpose(1, 0, 2)
    dx1, grads_s["mix_norm"] = _rms_bwd([d_hm], x1, small["mix_norm"], dx2, "mix_norm_bwd", 512)
    dx, grads_s["ffn1_norm"], grads_b["ffn1_w_gate"], grads_b["ffn1_w_up"], grads_b["ffn1_w_down"] = _ffn_bwd(
        dx1, x, small["ffn1_norm"], wfull["ffn1_w_gate"], wfull["ffn1_w_up"], wfull["ffn1_w_down"], ffn1_saved, "ffn1")
    return loss, dx, grads_s, grads_b


def kernel(x, ffn1_norm, ffn1_w_gate, ffn1_w_up, ffn1_w_down, mix_norm, w_in, dil_q_norm, dil_k_norm, rel_bias, mla_q_a_norm, mla_w_q_b, mla_kv_a_norm, mla_w_kv_b, mla_q_norm, mla_k_norm, out_norm_dil, out_norm_mla, w_out, ffn2_norm, ffn2_w_gate, ffn2_w_up, ffn2_w_down, loss_target, m_ffn1_norm, m_ffn1_w_gate, m_ffn1_w_up, m_ffn1_w_down, m_mix_norm, m_w_in, m_dil_q_norm, m_dil_k_norm, m_rel_bias, m_mla_q_a_norm, m_mla_w_q_b, m_mla_kv_a_norm, m_mla_w_kv_b, m_mla_q_norm, m_mla_k_norm, m_out_norm_dil, m_out_norm_mla, m_w_out, m_ffn2_norm, m_ffn2_w_gate, m_ffn2_w_up, m_ffn2_w_down, v_ffn1_norm, v_ffn1_w_gate, v_ffn1_w_up, v_ffn1_w_down, v_mix_norm, v_w_in, v_dil_q_norm, v_dil_k_norm, v_rel_bias, v_mla_q_a_norm, v_mla_w_q_b, v_mla_kv_a_norm, v_mla_w_kv_b, v_mla_q_norm, v_mla_k_norm, v_out_norm_dil, v_out_norm_mla, v_w_out, v_ffn2_norm, v_ffn2_w_gate, v_ffn2_w_up, v_ffn2_w_down):
    given = dict(locals())
    big_names = [name for name, _ in BIG]
    small_names = [name for name, _, _ in SMALL]

    chip = (2 * lax.axis_index("x") + lax.axis_index("y")).astype(jnp.int32)
    core = lax.axis_index("c").astype(jnp.int32)
    mine = _pack_big({n: given[n] for n in big_names}, BF16)
    gathered = lax.dynamic_update_slice(_gather_weights(mine[0]), mine, (chip, 0, 0))
    wfull = _unpack_big(gathered)
    small = {n: given[n] for n in small_names}

    loss, dx, grads_s, grads_b = _local_step(x[0], loss_target[0], small, wfull)
    loss = lax.psum(loss[0, 0], ("x", "y", "c"))

    packed = _pack_big(grads_b, F32).reshape(N_CHIPS, -1, LANES)
    chip_part = _add_halves(packed, _reduce_cores(packed), core.reshape(1), 1264)
    reduced = _sum_partials(_scatter_chips(chip_part), chip_part, jnp.stack([chip, core]), 1264)
    g_big = _unpack_big(_share_cores(reduced)[None])
    g_small = _unpack_small(_allreduce_small(_pack_small(grads_s)))

    grad, delta, new_m, new_v = {}, {}, {}, {}
    for name, shape in BIG:
        g2 = g_big[name].reshape(shape)
        d_, m_, v_ = _adamw(given[name].reshape(shape), g2, given["m_" + name].reshape(shape),
                            given["v_" + name].reshape(shape), f"adamw_{name}")
        full = given[name].shape
        grad[name], delta[name], new_m[name], new_v[name] = (a.reshape(full) for a in (g2, d_, m_, v_))
    ps = {k: _pack_small({n: given[pre + n] for n in small_names}) for k, pre in (("w", ""), ("m", "m_"), ("v", "v_"))}
    gs_packed = _pack_small(g_small)
    d_s, m_s, v_s = (_unpack_small(a) for a in _adamw(ps["w"], gs_packed, ps["m"], ps["v"], "adamw_small"))
    for name in small_names:
        grad[name], delta[name], new_m[name], new_v[name] = g_small[name], d_s[name], m_s[name], v_s[name]

    return (loss, dx[None], *[grad[n] for n in WEIGHTS], *[delta[n] for n in WEIGHTS],
            *[new_m[n] for n in WEIGHTS], *[new_v[n] for n in WEIGHTS])
```

```python
import functools

import numpy as np
import jax
import jax.numpy as jnp
from jax import lax
from jax.experimental import pallas as pl
from jax.experimental.pallas import tpu as pltpu

F32 = jnp.float32
BF16 = jnp.bfloat16

D_MODEL = 1024
D_FF = 2816
N_CHIPS = 4
DIL_HEADS = 8
DIL_HD = 64
DIL_WIDTH = 512
DIL_DILATIONS = (1, 4, 16)
DIL_W = 128
QB = 128
MLA_HEADS = 4
MLA_NOPE = 128
MLA_ROPE = 64
MLA_QK = 192
MLA_V = 128
MLA_Q_RANK = 256
MLA_KV_RANK = 128
ROPE_BASE = 10000.0
REL_BUCKETS = 32
REL_MAX_DIST = 2048
FFN_RESID = 0.5
EPS = 1e-6
NEG = -1e30
LANES = 128

ADAM_LR = 0.001
ADAM_B1 = 0.9
ADAM_B2 = 0.999
ADAM_EPS = 1e-08
ADAM_WD = 0.01
ADAM_STEP = 10

NT = (((1,), (1,)), ((), ()))
NN = (((1,), (0,)), ((), ()))
TN = (((0,), (0,)), ((), ()))

BIG = (
    ("ffn1_w_gate", (D_MODEL, D_FF // N_CHIPS)),
    ("ffn1_w_up", (D_MODEL, D_FF // N_CHIPS)),
    ("ffn1_w_down", (D_FF // N_CHIPS, D_MODEL)),
    ("w_in", (D_MODEL, 1984 // N_CHIPS)),
    ("mla_w_q_b", (MLA_Q_RANK, MLA_QK)),
    ("mla_w_kv_b", (MLA_KV_RANK, MLA_NOPE + MLA_V)),
    ("w_out", (D_MODEL // N_CHIPS, D_MODEL)),
    ("ffn2_w_gate", (D_MODEL, D_FF // N_CHIPS)),
    ("ffn2_w_up", (D_MODEL, D_FF // N_CHIPS)),
    ("ffn2_w_down", (D_FF // N_CHIPS, D_MODEL)),
)
SMALL = (
    ("ffn1_norm", (1, 1024), 8), ("mix_norm", (1, 1024), 8), ("dil_q_norm", (1, 64), 1),
    ("dil_k_norm", (1, 64), 1), ("rel_bias", (8, 32), 2), ("mla_q_a_norm", (1, 256), 2),
    ("mla_kv_a_norm", (1, 128), 1), ("mla_q_norm", (1, 192), 2), ("mla_k_norm", (1, 192), 2),
    ("out_norm_dil", (1, 512), 4), ("out_norm_mla", (1, 512), 4), ("ffn2_norm", (1, 1024), 8),
)
SMALL_ROWS = 48
WEIGHTS = ("ffn1_norm", "ffn1_w_gate", "ffn1_w_up", "ffn1_w_down", "mix_norm", "w_in", "dil_q_norm",
           "dil_k_norm", "rel_bias", "mla_q_a_norm", "mla_w_q_b", "mla_kv_a_norm", "mla_w_kv_b",
           "mla_q_norm", "mla_k_norm", "out_norm_dil", "out_norm_mla", "w_out", "ffn2_norm",
           "ffn2_w_gate", "ffn2_w_up", "ffn2_w_down")


def _pcall(body, **kw):
    return pl.pallas_call(body, **kw)


def _cparams(*sem):
    return pltpu.CompilerParams(dimension_semantics=sem)


def _sds(shape, dtype):
    return jax.ShapeDtypeStruct(shape, dtype)


def _dot(a, b, dn):
    return lax.dot_general(a, b, dn, preferred_element_type=F32)


def _rms_fwd(x, g, out_dtype, name, tm):
    n, d = x.shape
    tm = min(tm, n)

    def body(x_ref, g_ref, o_ref):
        xf = x_ref[...].astype(F32)
        r = lax.rsqrt(jnp.mean(xf * xf, axis=-1, keepdims=True) + EPS)
        o_ref[...] = (xf * r * g_ref[...]).astype(o_ref.dtype)

    return _pcall(
        body, name=name, grid=(n // tm,),
        in_specs=[pl.BlockSpec((tm, d), lambda i: (i, 0)), pl.BlockSpec((1, d), lambda i: (0, 0))],
        out_specs=pl.BlockSpec((tm, d), lambda i: (i, 0)),
        out_shape=_sds((n, d), out_dtype), compiler_params=_cparams("parallel"))(x, g)


def _rms_bwd(dys, x, g, res, name, tm):
    n, d = x.shape
    tm = min(tm, n)
    nd = len(dys)
    has_res = res is not None

    def body(*refs):
        dy_refs = refs[:nd]
        x_ref, g_ref = refs[nd], refs[nd + 1]
        res_ref = refs[nd + 2] if has_res else None
        dx_ref, dg_ref = refs[-2], refs[-1]
        dy = dy_refs[0][...].astype(F32)
        for r_ in dy_refs[1:]:
            dy = dy + r_[...].astype(F32)
        xf = x_ref[...].astype(F32)
        r = lax.rsqrt(jnp.mean(xf * xf, axis=-1, keepdims=True) + EPS)
        xh = xf * r
        dxh = dy * g_ref[...]
        dx = r * (dxh - xh * jnp.mean(dxh * xh, axis=-1, keepdims=True))
        if has_res:
            dx = dx + res_ref[...]
        dx_ref[...] = dx

        @pl.when(pl.program_id(0) == 0)
        def _():
            dg_ref[...] = jnp.zeros_like(dg_ref)

        dg_ref[...] += jnp.sum(dy * xh, axis=0, keepdims=True)

    row = pl.BlockSpec((tm, d), lambda i: (i, 0))
    vec = pl.BlockSpec((1, d), lambda i: (0, 0))
    ins = list(dys) + [x, g] + ([res] if has_res else [])
    return _pcall(
        body, name=name, grid=(n // tm,),
        in_specs=[row] * nd + [row, vec] + ([row] if has_res else []),
        out_specs=(row, vec),
        out_shape=(_sds((n, d), F32), _sds((1, d), F32)),
        compiler_params=_cparams("arbitrary"))(*ins)


def _mm(name, grid, pairs, dn, out_shape, out_spec, acc_shape, res=None, scale=1.0):
    npairs = len(pairs)
    nred = grid[2]
    has_res = res is not None

    def body(*refs):
        ab = refs[:2 * npairs]
        res_ref = refs[2 * npairs] if has_res else None
        o_ref = refs[2 * npairs + int(has_res)]
        acc_ref = refs[-1] if nred > 1 else None
        tot = None
        for p in range(npairs):
            d = _dot(ab[2 * p][...].astype(BF16), ab[2 * p + 1][...].astype(BF16), dn)
            tot = d if tot is None else tot + d

        def finish(v):
            if scale != 1.0:
                v = v * scale
            if has_res:
                v = res_ref[...] + v
            o_ref[...] = v.astype(o_ref.dtype)

        if nred == 1:
            finish(tot)
        else:
            r = pl.program_id(2)

            @pl.when(r == 0)
            def _():
                acc_ref[...] = tot

            @pl.when(r > 0)
            def _():
                acc_ref[...] += tot

            @pl.when(r == nred - 1)
            def _():
                finish(acc_ref[...])

    ins, specs = [], []
    for a, a_spec, b, b_spec in pairs:
        ins += [a, b]
        specs += [a_spec, b_spec]
    if has_res:
        ins.append(res[0])
        specs.append(res[1])
    return _pcall(
        body, name=name, grid=grid, in_specs=specs, out_specs=out_spec, out_shape=out_shape,
        scratch_shapes=[pltpu.VMEM(acc_shape, F32)] if nred > 1 else [],
        compiler_params=_cparams("parallel", "parallel", "arbitrary"))(*ins)


def _ffn_up(h, wg, wu, name, tm):
    t, d = h.shape
    nc, _, fs = wg.shape
    tm = min(tm, t)

    def body(h_ref, wg_ref, wu_ref, g_ref, u_ref, a_ref):
        hh = h_ref[...]
        gate = _dot(hh, wg_ref[...], NN)
        up = _dot(hh, wu_ref[...], NN)
        g_ref[...] = gate.astype(BF16)
        u_ref[...] = up.astype(BF16)
        a_ref[...] = (gate * jax.nn.sigmoid(gate) * up).astype(BF16)

    wspec = pl.BlockSpec((None, d, fs), lambda c, i: (c, 0, 0))
    ospec = pl.BlockSpec((None, tm, fs), lambda c, i: (c, i, 0))
    osd = _sds((nc, t, fs), BF16)
    return _pcall(
        body, name=name, grid=(nc, t // tm),
        in_specs=[pl.BlockSpec((tm, d), lambda c, i: (i, 0)), wspec, wspec],
        out_specs=(ospec, ospec, ospec), out_shape=(osd, osd, osd),
        compiler_params=_cparams("parallel", "parallel"))(h, wg, wu)


def _ffn_dact(dy, wd, gate, up, name, tm):
    t, d = dy.shape
    nc, fs, _ = wd.shape
    tm = min(tm, t)

    def body(dy_ref, wd_ref, g_ref, u_ref, dg_ref, du_ref):
        da = _dot(dy_ref[...].astype(BF16), wd_ref[...], NT) * FFN_RESID
        gate = g_ref[...].astype(F32)
        up = u_ref[...].astype(F32)
        sig = jax.nn.sigmoid(gate)
        dg_ref[...] = (da * up * (sig * (1.0 + gate * (1.0 - sig)))).astype(BF16)
        du_ref[...] = (da * (gate * sig)).astype(BF16)

    cspec = pl.BlockSpec((None, tm, fs), lambda c, i: (c, i, 0))
    osd = _sds((nc, t, fs), BF16)
    return _pcall(
        body, name=name, grid=(nc, t // tm),
        in_specs=[pl.BlockSpec((tm, d), lambda c, i: (i, 0)),
                  pl.BlockSpec((None, fs, d), lambda c, i: (c, 0, 0)), cspec, cspec],
        out_specs=(cspec, cspec), out_shape=(osd, osd),
        compiler_params=_cparams("parallel", "parallel"))(dy, wd, gate, up)


def _ffn_fwd(x, g, wg, wu, wd, tag):
    t = x.shape[0]
    nc, _, fs = wg.shape
    tm = min(512, t)
    h = _rms_fwd(x, g, BF16, f"{tag}_norm", 512)
    gate, up, act = _ffn_up(h, wg, wu, f"{tag}_up", 1024)
    pairs = [(act, pl.BlockSpec((None, tm, fs), lambda i, j, r, c=c: (c, i, 0)),
              wd, pl.BlockSpec((None, fs, D_MODEL), lambda i, j, r, c=c: (c, 0, 0))) for c in range(nc)]
    row = pl.BlockSpec((tm, D_MODEL), lambda i, j, r: (i, 0))
    y = _mm(f"{tag}_down", (t // tm, 1, 1), pairs, NN, _sds((t, D_MODEL), F32), row, (tm, D_MODEL),
            res=(x, row), scale=FFN_RESID)
    return y, (h, gate, up, act)


def _ffn_bwd(dy, x, g, wg, wu, wd, saved, tag):
    h, gate, up, act = saved
    t = x.shape[0]
    nc, _, fs = wg.shape
    tm = min(512, t)
    tk = min(2048, t)
    dgate, dup = _ffn_dact(dy, wd, gate, up, f"{tag}_dact", 1024)
    tok_c = pl.BlockSpec((None, tk, fs), lambda c, j, r: (c, r, 0))
    tok_d = pl.BlockSpec((tk, D_MODEL), lambda c, j, r: (r, 0))
    dwd = _mm(f"{tag}_dwd", (nc, 1, t // tk), [(act, tok_c, dy, tok_d)], TN,
              _sds((nc, fs, D_MODEL), F32), pl.BlockSpec((None, fs, D_MODEL), lambda c, j, r: (c, 0, 0)),
              (fs, D_MODEL), scale=FFN_RESID)
    wout = pl.BlockSpec((None, D_MODEL, fs), lambda c, j, r: (c, 0, 0))
    dwg = _mm(f"{tag}_dwg", (nc, 1, t // tk), [(h, tok_d, dgate, tok_c)], TN,
              _sds((nc, D_MODEL, fs), F32), wout, (D_MODEL, fs))
    dwu = _mm(f"{tag}_dwu", (nc, 1, t // tk), [(h, tok_d, dup, tok_c)], TN,
              _sds((nc, D_MODEL, fs), F32), wout, (D_MODEL, fs))
    pairs = []
    for c in range(nc):
        a_spec = pl.BlockSpec((None, tm, fs), lambda i, j, r, c=c: (c, i, 0))
        w_spec = pl.BlockSpec((None, D_MODEL, fs), lambda i, j, r, c=c: (c, 0, 0))
        pairs += [(dgate, a_spec, wg, w_spec), (dup, a_spec, wu, w_spec)]
    dh = _mm(f"{tag}_dh", (t // tm, 1, 1), pairs, NT,
             _sds((t, D_MODEL), F32), pl.BlockSpec((tm, D_MODEL), lambda i, j, r: (i, 0)), (tm, D_MODEL))
    dx, dg = _rms_bwd([dh], x, g, dy, f"{tag}_dnorm", 512)
    return dx, dg, dwg, dwu, dwd


def _mm_simple(name, a, b, dn, out_dtype, tm=512, tk=512, res=None, scale=1.0):
    if dn == TN:
        k, m = a.shape
        n = b.shape[1]
        tk = min(tk, k)
        return _mm(name, (1, 1, k // tk),
                   [(a, pl.BlockSpec((tk, m), lambda i, j, r: (r, 0)), b, pl.BlockSpec((tk, n), lambda i, j, r: (r, 0)))],
                   TN, _sds((m, n), out_dtype), pl.BlockSpec((m, n), lambda i, j, r: (0, 0)), (m, n), scale=scale)
    m, k = a.shape
    n = b.shape[1] if dn == NN else b.shape[0]
    tm = min(tm, m)
    row = pl.BlockSpec((tm, n), lambda i, j, r: (i, 0))
    return _mm(name, (m // tm, 1, 1),
               [(a, pl.BlockSpec((tm, k), lambda i, j, r: (i, 0)), b, pl.BlockSpec(b.shape, lambda i, j, r: (0, 0)))],
               dn, _sds((m, n), out_dtype), row, (tm, n), res=None if res is None else (res, row), scale=scale)


def _t5_bucket(dist):
    max_exact = REL_BUCKETS // 2
    d = np.maximum(dist, 1).astype(np.float32)
    large = max_exact + (np.log(d / max_exact) / np.log(REL_MAX_DIST / max_exact)
                         * (REL_BUCKETS - max_exact)).astype(np.int32)
    large = np.minimum(large, REL_BUCKETS - 1)
    return np.where(dist < max_exact, dist, large).astype(np.int32)


def _bucket_tiles():
    i = np.arange(QB)[:, None]
    j = np.arange(QB + DIL_W)[None, :]
    delta = np.clip(i + DIL_W - j, 0, None)
    return np.stack([_t5_bucket(delta * dil) for dil in DIL_DILATIONS]).astype(np.int32)


def _bias_tiles(rel_bias):
    buckets = jnp.asarray(_bucket_tiles())

    def body(rb_ref, bk_ref, o_ref):
        bk = bk_ref[...]
        for h in range(DIL_HEADS):
            def pick(b, tile):
                return jnp.where(bk == b, rb_ref[h, b], tile)

            o_ref[h] = lax.fori_loop(0, REL_BUCKETS, pick, jnp.zeros((QB, QB + DIL_W), F32))

    return _pcall(
        body, name="dil_bias_tiles", grid=(3,),
        in_specs=[pl.BlockSpec(memory_space=pltpu.SMEM),
                  pl.BlockSpec((None, QB, QB + DIL_W), lambda b: (b, 0, 0))],
        out_specs=pl.BlockSpec((None, DIL_HEADS, QB, QB + DIL_W), lambda b: (b, 0, 0, 0)),
        out_shape=_sds((3, DIL_HEADS, QB, QB + DIL_W), F32),
        compiler_params=_cparams("parallel"))(rel_bias, buckets)


def _bias_grad(dtiles):
    buckets = jnp.asarray(_bucket_tiles())

    def body(dt_ref, bk_ref, o_ref):
        for h in range(DIL_HEADS):
            def one(b, carry):
                tot = jnp.zeros((), F32)
                for br in range(3):
                    tot = tot + jnp.sum(jnp.where(bk_ref[br] == b, dt_ref[br, h], 0.0))
                o_ref[h, b] = tot
                return carry

            lax.fori_loop(0, REL_BUCKETS, one, 0)

    return _pcall(
        body, name="dil_bias_grad",
        in_specs=[pl.BlockSpec(memory_space=pltpu.VMEM), pl.BlockSpec(memory_space=pltpu.VMEM)],
        out_specs=pl.BlockSpec(memory_space=pltpu.SMEM),
        out_shape=_sds((DIL_HEADS, REL_BUCKETS), F32))(dtiles, buckets)


def _dil_masks(has_prev):
    ii = lax.broadcasted_iota(jnp.int32, (QB, QB), 0)
    jj = lax.broadcasted_iota(jnp.int32, (QB, QB), 1)
    return jj <= ii, jj >= ii + jnp.where(has_prev, 0, QB)


def _dil_fwd(q, k, v, bias, cls_len, name):
    nh, t, hd = q.shape
    nb = t // QB
    per = cls_len // QB
    scale = hd ** -0.5

    def body(q_ref, kc_ref, kp_ref, vc_ref, vp_ref, b_ref, o_ref, lse_ref):
        n = pl.program_id(0)
        cur_ok, prev_ok = _dil_masks((n % per) != 0)
        for h in range(nh):
            qh = q_ref[h]
            sc = _dot(qh, kc_ref[h], NT) * scale + b_ref[h, :, QB:]
            sp = _dot(qh, kp_ref[h], NT) * scale + b_ref[h, :, :QB]
            sc = jnp.where(cur_ok, sc, NEG)
            sp = jnp.where(prev_ok, sp, NEG)
            m = jnp.maximum(jnp.max(sc, axis=-1, keepdims=True), jnp.max(sp, axis=-1, keepdims=True))
            pc = jnp.exp(sc - m)
            pp = jnp.exp(sp - m)
            den = jnp.sum(pc, axis=-1, keepdims=True) + jnp.sum(pp, axis=-1, keepdims=True)
            o = _dot(pc.astype(BF16), vc_ref[h], NN) + _dot(pp.astype(BF16), vp_ref[h], NN)
            o_ref[h] = o / den
            lse_ref[h] = m + jnp.log(den)

    cur = pl.BlockSpec((nh, QB, hd), lambda n: (0, n, 0))
    prev = pl.BlockSpec((nh, QB, hd), lambda n: (0, jnp.maximum(n - 1, 0), 0))
    return _pcall(
        body, name=name, grid=(nb,),
        in_specs=[cur, cur, prev, cur, prev, pl.BlockSpec((nh, QB, QB + DIL_W), lambda n: (0, 0, 0))],
        out_specs=(cur, pl.BlockSpec((nh, QB, 1), lambda n: (0, n, 0))),
        out_shape=(_sds((nh, t, hd), F32), _sds((nh, t, 1), F32)),
        compiler_params=_cparams("parallel"))(q, k, k, v, v, bias)


def _dil_bwd(q, k, v, do, lse, dl, bias, cls_len, name):
    nh, t, hd = q.shape
    nb = t // QB
    per = cls_len // QB
    scale = hd ** -0.5

    def body(qc_ref, qn_ref, doc_ref, don_ref, lc_ref, ln_ref, dc_ref, dn_ref, k_ref, v_ref, b_ref,
             dq_ref, dk_ref, dv_ref, db_ref, carry):
        n = pl.program_id(0)
        nxt = n + 1
        cur_ok, prev_ok = _dil_masks((nxt < nb) & ((nxt % per) != 0))

        @pl.when(n == 0)
        def _():
            db_ref[...] = jnp.zeros_like(db_ref)
            carry[...] = jnp.zeros_like(carry)

        for h in range(nh):
            kh = k_ref[h]
            vh = v_ref[h]
            q1, q2 = qc_ref[h], qn_ref[h]
            do1, do2 = doc_ref[h], don_ref[h]
            s1 = jnp.where(cur_ok, _dot(q1, kh, NT) * scale + b_ref[h, :, QB:], NEG)
            s2 = jnp.where(prev_ok, _dot(q2, kh, NT) * scale + b_ref[h, :, :QB], NEG)
            p1 = jnp.exp(s1 - lc_ref[h])
            p2 = jnp.exp(s2 - ln_ref[h])
            ds1 = p1 * (_dot(do1, vh, NT) - dc_ref[h])
            ds2 = p2 * (_dot(do2, vh, NT) - dn_ref[h])
            ds1b = ds1.astype(BF16)
            ds2b = ds2.astype(BF16)
            dq_ref[h] = carry[h] + _dot(ds1b, kh, NN) * scale
            carry[h] = _dot(ds2b, kh, NN) * scale
            dk_ref[h] = (_dot(ds1b, q1, TN) + _dot(ds2b, q2, TN)) * scale
            dv_ref[h] = _dot(p1.astype(BF16), do1, TN) + _dot(p2.astype(BF16), do2, TN)
            db_ref[h, :, QB:] += ds1
            db_ref[h, :, :QB] += ds2

    def cur(w):
        return pl.BlockSpec((nh, QB, w), lambda n: (0, n, 0))

    def nxt(w):
        return pl.BlockSpec((nh, QB, w), lambda n: (0, jnp.minimum(n + 1, nb - 1), 0))

    tile = pl.BlockSpec((nh, QB, QB + DIL_W), lambda n: (0, 0, 0))
    o3 = _sds((nh, t, hd), F32)
    return _pcall(
        body, name=name, grid=(nb,),
        in_specs=[cur(hd), nxt(hd), cur(hd), nxt(hd), cur(1), nxt(1), cur(1), nxt(1), cur(hd), cur(hd), tile],
        out_specs=(cur(hd), cur(hd), cur(hd), tile),
        out_shape=(o3, o3, o3, _sds((nh, QB, QB + DIL_W), F32)),
        scratch_shapes=[pltpu.VMEM((nh, QB, hd), F32)],
        compiler_params=_cparams("arbitrary"))(q, q, do, do, lse, lse, dl, dl, k, v, bias)


def _dil_merge(outs, lses, tm):
    n, hd = outs[0].shape
    tm = min(tm, n)

    def body(o0, o1, o2, l0, l1, l2, o_ref, l_ref):
        a0, a1, a2 = l0[...], l1[...], l2[...]
        m = jnp.maximum(jnp.maximum(a0, a1), a2)
        e0, e1, e2 = jnp.exp(a0 - m), jnp.exp(a1 - m), jnp.exp(a2 - m)
        den = e0 + e1 + e2
        o_ref[...] = (e0 * o0[...] + e1 * o1[...] + e2 * o2[...]) / den
        l_ref[...] = m + jnp.log(den)

    ospec = pl.BlockSpec((tm, hd), lambda i: (i, 0))
    lspec = pl.BlockSpec((tm, 1), lambda i: (i, 0))
    return _pcall(
        body, name="dil_merge", grid=(n // tm,),
        in_specs=[ospec] * 3 + [lspec] * 3, out_specs=(ospec, lspec),
        out_shape=(_sds((n, hd), F32), _sds((n, 1), F32)),
        compiler_params=_cparams("parallel"))(*outs, *lses)


def _rowdot(a, b, name, tm):
    n, d = a.shape
    tm = min(tm, n)

    def body(a_ref, b_ref, o_ref):
        o_ref[...] = jnp.sum(a_ref[...].astype(F32) * b_ref[...].astype(F32), axis=-1, keepdims=True)

    spec = pl.BlockSpec((tm, d), lambda i: (i, 0))
    return _pcall(body, name=name, grid=(n // tm,), in_specs=[spec, spec],
                  out_specs=pl.BlockSpec((tm, 1), lambda i: (i, 0)), out_shape=_sds((n, 1), F32),
                  compiler_params=_cparams("parallel"))(a, b)


def _add3(a, b, c, name, tm):
    n, d = a.shape
    tm = min(tm, n)

    def body(a_ref, b_ref, c_ref, o_ref):
        o_ref[...] = a_ref[...] + b_ref[...] + c_ref[...]

    spec = pl.BlockSpec((tm, d), lambda i: (i, 0))
    return _pcall(body, name=name, grid=(n // tm,), in_specs=[spec] * 3, out_specs=spec,
                  out_shape=_sds((n, d), F32), compiler_params=_cparams("parallel"))(a, b, c)


def _to_sub(a, dil):
    nh, t, w = a.shape
    if dil == 1:
        return a
    return a.reshape(nh, t // dil, dil, w).transpose(0, 2, 1, 3).reshape(nh, t, w)


def _from_sub(a, dil):
    nh, t, w = a.shape
    if dil == 1:
        return a
    return a.reshape(nh, dil, t // dil, w).transpose(0, 2, 1, 3).reshape(nh, t, w)


def _rope_tables(t):
    inv = ROPE_BASE ** (-np.arange(0, MLA_ROPE, 2, dtype=np.float64) / MLA_ROPE)
    ang = np.arange(t, dtype=np.float64)[:, None] * inv[None, :]
    cos, sin = np.cos(ang), np.sin(ang)
    return (jnp.asarray(np.concatenate([cos, cos], 1), F32), jnp.asarray(np.concatenate([-sin, sin], 1), F32))


def _half_swap():
    p = np.zeros((MLA_ROPE, MLA_ROPE), np.float32)
    half = MLA_ROPE // 2
    for i in range(MLA_ROPE):
        p[(i + half) % MLA_ROPE, i] = 1.0
    return jnp.asarray(p)


def _mla_qk_fwd(x, g, cos_t, sin_t, scale, name, tm):
    n, d = x.shape
    t = cos_t.shape[0]
    tm = min(tm, t)
    nt = t // tm
    swap = _half_swap()

    def body(x_ref, g_ref, c_ref, s_ref, p_ref, o_ref):
        xf = x_ref[...]
        r = lax.rsqrt(jnp.mean(xf * xf, axis=-1, keepdims=True) + EPS)
        y = xf * r * g_ref[...]
        yr = y[:, MLA_NOPE:]
        sw = lax.dot_general(yr, p_ref[...], NN, precision=lax.Precision.HIGHEST, preferred_element_type=F32)
        o_ref[:, :MLA_NOPE] = (y[:, :MLA_NOPE] * scale).astype(o_ref.dtype)
        o_ref[:, MLA_NOPE:] = ((yr * c_ref[...] + sw * s_ref[...]) * scale).astype(o_ref.dtype)

    row = pl.BlockSpec((tm, d), lambda i: (i, 0))
    tab = pl.BlockSpec((tm, MLA_ROPE), lambda i: (i % nt, 0))
    return _pcall(
        body, name=name, grid=(n // tm,),
        in_specs=[row, pl.BlockSpec((1, d), lambda i: (0, 0)), tab, tab,
                  pl.BlockSpec((MLA_ROPE, MLA_ROPE), lambda i: (0, 0))],
        out_specs=row, out_shape=_sds((n, d), BF16),
        compiler_params=_cparams("parallel"))(x, g, cos_t, sin_t, swap)


def _mla_qk_bwd(dy, x, g, cos_t, sin_t, scale, name, tm):
    n, d = x.shape
    t = cos_t.shape[0]
    tm = min(tm, t)
    nt = t // tm
    swap_t = _half_swap().T

    def body(dy_ref, x_ref, g_ref, c_ref, s_ref, p_ref, dx_ref, dg_ref):
        xf = x_ref[...]
        gg = g_ref[...]
        r = lax.rsqrt(jnp.mean(xf * xf, axis=-1, keepdims=True) + EPS)
        xh = xf * r
        dyf = dy_ref[...] * scale
        dyr = dyf[:, MLA_NOPE:]
        back = lax.dot_general(dyr * s_ref[...], p_ref[...], NN, precision=lax.Precision.HIGHEST,
                               preferred_element_type=F32)
        dn_n = dyf[:, :MLA_NOPE]
        dn_r = dyr * c_ref[...] + back
        xh_n, xh_r = xh[:, :MLA_NOPE], xh[:, MLA_NOPE:]
        dxh_n = dn_n * gg[:, :MLA_NOPE]
        dxh_r = dn_r * gg[:, MLA_NOPE:]
        mean = (jnp.sum(dxh_n * xh_n, axis=-1, keepdims=True)
                + jnp.sum(dxh_r * xh_r, axis=-1, keepdims=True)) * (1.0 / d)
        dx_ref[:, :MLA_NOPE] = r * (dxh_n - xh_n * mean)
        dx_ref[:, MLA_NOPE:] = r * (dxh_r - xh_r * mean)

        @pl.when(pl.program_id(0) == 0)
        def _():
            dg_ref[...] = jnp.zeros_like(dg_ref)

        dg_ref[:, :MLA_NOPE] += jnp.sum(dn_n * xh_n, axis=0, keepdims=True)
        dg_ref[:, MLA_NOPE:] += jnp.sum(dn_r * xh_r, axis=0, keepdims=True)

    row = pl.BlockSpec((tm, d), lambda i: (i, 0))
    vec = pl.BlockSpec((1, d), lambda i: (0, 0))
    tab = pl.BlockSpec((tm, MLA_ROPE), lambda i: (i % nt, 0))
    return _pcall(
        body, name=name, grid=(n // tm,),
        in_specs=[row, row, vec, tab, tab, pl.BlockSpec((MLA_ROPE, MLA_ROPE), lambda i: (0, 0))],
        out_specs=(row, vec), out_shape=(_sds((n, d), F32), _sds((1, d), F32)),
        compiler_params=_cparams("arbitrary"))(dy, x, g, cos_t, sin_t, swap_t)


def _causal_mask(i, j, tq, tk):
    row = i * tq + lax.broadcasted_iota(jnp.int32, (tq, tk), 0)
    col = j * tk + lax.broadcasted_iota(jnp.int32, (tq, tk), 1)
    return col <= row


def _causal_steps(nq, nk, tq, tk, q_major):
    if q_major:
        groups = [[(i, j) for j in range((i * tq + tq - 1) // tk + 1)] for i in range(nq)]
    else:
        groups = [[(i, j) for i in range((j * tk) // tq, nq)] for j in range(nk)]
    it, jt, fl = [], [], []
    for g in groups:
        for n, (i, j) in enumerate(g):
            it.append(i)
            jt.append(j)
            fl.append((n == 0) + 2 * (n == len(g) - 1) + 4 * (j * tk + tk - 1 > i * tq))
    return tuple(jnp.asarray(np.array(a, np.int32)) for a in (it, jt, fl))


def _causal_specs(tq, tk):
    def qs(w):
        return pl.BlockSpec((None, tq, w), lambda h, s, it, jt, fl: (h, it[s], 0))

    def kv(w):
        return pl.BlockSpec((None, tk, w), lambda h, s, it, jt, fl: (h, jt[s], 0))

    return qs, kv


def _mla_fwd(q, k, v, tq, tk):
    nh, t, dq = q.shape
    dv = v.shape[2]
    tq, tk = min(tq, t), min(tk, t)
    tables = _causal_steps(t // tq, t // tk, tq, tk, True)

    def body(it, jt, fl, q_ref, k_ref, v_ref, o_ref, lse_ref, m_sc, l_sc, acc_sc):
        step = pl.program_id(1)
        i, j, flags = it[step], jt[step], fl[step]

        @pl.when((flags & 1) != 0)
        def _():
            m_sc[...] = jnp.full_like(m_sc, NEG)
            l_sc[...] = jnp.zeros_like(l_sc)
            acc_sc[...] = jnp.zeros_like(acc_sc)

        def update(masked):
            s = _dot(q_ref[...], k_ref[...], NT)
            if masked:
                s = jnp.where(_causal_mask(i, j, tq, tk), s, NEG)
            m_prev = m_sc[...]
            m_new = jnp.maximum(m_prev, jnp.max(s, axis=-1, keepdims=True))
            alpha = jnp.exp(m_prev - m_new)
            p = jnp.exp(s - m_new)
            l_sc[...] = alpha * l_sc[...] + jnp.sum(p, axis=-1, keepdims=True)
            acc_sc[...] = alpha * acc_sc[...] + _dot(p.astype(BF16), v_ref[...], NN)
            m_sc[...] = m_new

        pl.when((flags & 4) != 0)(functools.partial(update, True))
        pl.when((flags & 4) == 0)(functools.partial(update, False))

        @pl.when((flags & 2) != 0)
        def _():
            o_ref[...] = acc_sc[...] / l_sc[...]
            lse_ref[...] = m_sc[...] + jnp.log(l_sc[...])

    qs, kv = _causal_specs(tq, tk)
    return _pcall(
        body, name="mla_attn_fwd",
        grid_spec=pltpu.PrefetchScalarGridSpec(
            num_scalar_prefetch=3, grid=(nh, tables[0].shape[0]),
            in_specs=[qs(dq), kv(dq), kv(dv)], out_specs=(qs(dv), qs(1)),
            scratch_shapes=[pltpu.VMEM((tq, 1), F32), pltpu.VMEM((tq, 1), F32), pltpu.VMEM((tq, dv), F32)]),
        out_shape=(_sds((nh, t, dv), F32), _sds((nh, t, 1), F32)),
        compiler_params=_cparams("parallel", "arbitrary"))(*tables, q, k, v)


def _mla_bwd_dq(q, k, v, do, lse, dl, tq, tk):
    nh, t, dq = q.shape
    dv = v.shape[2]
    tq, tk = min(tq, t), min(tk, t)
    tables = _causal_steps(t // tq, t // tk, tq, tk, True)

    def body(it, jt, fl, q_ref, k_ref, v_ref, do_ref, lse_ref, dl_ref, dq_ref, acc_sc):
        step = pl.program_id(1)
        i, j, flags = it[step], jt[step], fl[step]

        def update(masked):
            s = _dot(q_ref[...], k_ref[...], NT)
            if masked:
                s = jnp.where(_causal_mask(i, j, tq, tk), s, NEG)
            p = jnp.exp(s - lse_ref[...])
            dp = _dot(do_ref[...].astype(BF16), v_ref[...], NT)
            ds = p * (dp - dl_ref[...])
            part = _dot(ds.astype(BF16), k_ref[...], NN)

            @pl.when((flags & 1) != 0)
            def _():
                acc_sc[...] = part

            @pl.when((flags & 1) == 0)
            def _():
                acc_sc[...] += part

        pl.when((flags & 4) != 0)(functools.partial(update, True))
        pl.when((flags & 4) == 0)(functools.partial(update, False))

        @pl.when((flags & 2) != 0)
        def _():
            dq_ref[...] = acc_sc[...]

    qs, kv = _causal_specs(tq, tk)
    return _pcall(
        body, name="mla_attn_dq",
        grid_spec=pltpu.PrefetchScalarGridSpec(
            num_scalar_prefetch=3, grid=(nh, tables[0].shape[0]),
            in_specs=[qs(dq), kv(dq), kv(dv), qs(dv), qs(1), qs(1)], out_specs=qs(dq),
            scratch_shapes=[pltpu.VMEM((tq, dq), F32)]),
        out_shape=_sds((nh, t, dq), F32),
        compiler_params=_cparams("parallel", "arbitrary"))(*tables, q, k, v, do, lse, dl)


def _mla_bwd_dkv(q, k, v, do, lse_row, dl_row, tq, tk):
    nh, t, dq = q.shape
    dv = v.shape[2]
    tq, tk = min(tq, t), min(tk, t)
    tables = _causal_steps(t // tq, t // tk, tq, tk, False)

    def body(it, jt, fl, q_ref, k_ref, v_ref, do_ref, lse_ref, dl_ref, dk_ref, dv_ref, dk_sc, dv_sc):
        step = pl.program_id(1)
        i, j, flags = it[step], jt[step], fl[step]

        def update(masked):
            st = _dot(k_ref[...], q_ref[...], NT)
            if masked:
                key = j * tk + lax.broadcasted_iota(jnp.int32, (tk, tq), 0)
                qry = i * tq + lax.broadcasted_iota(jnp.int32, (tk, tq), 1)
                st = jnp.where(key <= qry, st, NEG)
            pt = jnp.exp(st - lse_ref[...])
            dob = do_ref[...].astype(BF16)
            dpt = _dot(v_ref[...], dob, NT)
            dst = pt * (dpt - dl_ref[...])
            dv_part = _dot(pt.astype(BF16), dob, NN)
            dk_part = _dot(dst.astype(BF16), q_ref[...], NN)

            @pl.when((flags & 1) != 0)
            def _():
                dv_sc[...] = dv_part
                dk_sc[...] = dk_part

            @pl.when((flags & 1) == 0)
            def _():
                dv_sc[...] += dv_part
                dk_sc[...] += dk_part

        pl.when((flags & 4) != 0)(functools.partial(update, True))
        pl.when((flags & 4) == 0)(functools.partial(update, False))

        @pl.when((flags & 2) != 0)
        def _():
            dk_ref[...] = dk_sc[...]
            dv_ref[...] = dv_sc[...]

    qs, kv = _causal_specs(tq, tk)
    rowv = pl.BlockSpec((None, 1, tq), lambda h, s, it, jt, fl: (h, 0, it[s]))
    return _pcall(
        body, name="mla_attn_dkv",
        grid_spec=pltpu.PrefetchScalarGridSpec(
            num_scalar_prefetch=3, grid=(nh, tables[0].shape[0]),
            in_specs=[qs(dq), kv(dq), kv(dv), qs(dv), rowv, rowv], out_specs=(kv(dq), kv(dv)),
            scratch_shapes=[pltpu.VMEM((tk, dq), F32), pltpu.VMEM((tk, dv), F32)]),
        out_shape=(_sds((nh, t, dq), F32), _sds((nh, t, dv), F32)),
        compiler_params=_cparams("parallel", "arbitrary"))(*tables, q, k, v, do, lse_row, dl_row)


def _loss_head(y, target, tm):
    t, d = y.shape
    tm = min(tm, t)
    nt = t // tm

    def body(y_ref, t_ref, dy_ref, loss_ref, acc):
        i = pl.program_id(0)
        err = y_ref[...] - t_ref[...]
        dy_ref[...] = err * (1.0 / d)

        @pl.when(i == 0)
        def _():
            acc[...] = jnp.zeros_like(acc)

        acc[...] += jnp.sum(err * err, axis=0, keepdims=True)

        @pl.when(i == nt - 1)
        def _():
            loss_ref[0, 0] = jnp.sum(acc[...]) * (0.5 / d)

    spec = pl.BlockSpec((tm, d), lambda i: (i, 0))
    return _pcall(
        body, name="loss_head", grid=(nt,), in_specs=[spec, spec],
        out_specs=(spec, pl.BlockSpec(memory_space=pltpu.SMEM)),
        out_shape=(_sds((t, d), F32), _sds((1, 1), F32)),
        scratch_shapes=[pltpu.VMEM((1, d), F32)],
        compiler_params=_cparams("arbitrary"))(y, target)


def _adamw(w, g, m, v, name):
    r, c = w.shape
    tr = r
    for cand in (256, 128, 64, 32, 16, 8):
        if r % cand == 0:
            tr = cand
            break

    def body(w_ref, g_ref, m_ref, v_ref, d_ref, nm_ref, nv_ref):
        gg = g_ref[...]
        nm = ADAM_B1 * m_ref[...] + (1.0 - ADAM_B1) * gg
        nv = ADAM_B2 * v_ref[...] + (1.0 - ADAM_B2) * (gg * gg)
        m_hat = nm / (1.0 - ADAM_B1 ** ADAM_STEP)
        v_hat = nv / (1.0 - ADAM_B2 ** ADAM_STEP)
        d_ref[...] = -ADAM_LR * (m_hat / (jnp.sqrt(v_hat) + ADAM_EPS) + ADAM_WD * w_ref[...])
        nm_ref[...] = nm
        nv_ref[...] = nv

    spec = pl.BlockSpec((tr, c), lambda i: (i, 0))
    sd = _sds((r, c), F32)
    return _pcall(body, name=name, grid=(r // tr,), in_specs=[spec] * 4, out_specs=(spec,) * 3,
                  out_shape=(sd, sd, sd), compiler_params=_cparams("parallel"))(w, g, m, v)


MESH_ID = pl.DeviceIdType.MESH
HBM_SPEC = pl.BlockSpec(memory_space=pltpu.HBM)


def _place():
    return lax.axis_index("x"), lax.axis_index("y"), lax.axis_index("c")


def _other_chips(x, y):
    return [(1 - x, y), (x, 1 - y), (1 - x, 1 - y)]


def _remote(src, dst, send_sems, recv_sems, k, to):
    return pltpu.make_async_remote_copy(src_ref=src, dst_ref=dst, send_sem=send_sems.at[k], recv_sem=recv_sems.at[k],
                                        device_id=to, device_id_type=MESH_ID)


D2D_SPLIT = 16
ICI_SPLIT = 4


def _chunks(rows, n):
    assert rows % n == 0
    return [(i * (rows // n), rows // n) for i in range(n)]


def _gather_weights(packed):
    rows, lanes = packed.shape
    half = rows // 2

    def body(src, out, send_sems, recv_sems):
        x, y, c = _place()
        me = 2 * x + y
        sibling = (x, y, 1 - c)
        chips = _other_chips(x, y)

        def part(chip, core, lo=0, n=half):
            return out.at[chip, pl.ds(core * half + lo, n), :]

        for k, (cx, cy) in enumerate(chips):
            for lo, n in _chunks(half, ICI_SPLIT):
                _remote(src.at[pl.ds(c * half + lo, n), :], part(me, c, lo, n), send_sems, recv_sems, k,
                        (cx, cy, c)).start()
        for k, (cx, cy) in enumerate(chips):
            got = part(2 * cx + cy, c)
            _remote(got, got, send_sems, recv_sems, k, (x, y, c)).wait_recv()
            for lo, n in _chunks(half, D2D_SPLIT):
                piece = part(2 * cx + cy, c, lo, n)
                _remote(piece, piece, send_sems, recv_sems, 3 + k, sibling).start()
        for k, (cx, cy) in enumerate(chips):
            got = part(2 * cx + cy, 1 - c)
            _remote(got, got, send_sems, recv_sems, 3 + k, (x, y, c)).wait_recv()
        for k in range(6):
            sent = part(me, c)
            _remote(sent, sent, send_sems, recv_sems, k, (x, y, c)).wait_send()

    return _pcall(
        body, name="gather_weights", in_specs=[HBM_SPEC], out_specs=HBM_SPEC,
        out_shape=_sds((N_CHIPS, rows, lanes), packed.dtype),
        scratch_shapes=[pltpu.SemaphoreType.DMA((6,)), pltpu.SemaphoreType.DMA((6,))],
    )(packed)


def _reduce_cores(grads):
    nchip, rows, lanes = grads.shape
    half = rows // 2

    def body(g, theirs, send_sems, recv_sems):
        x, y, c = _place()
        for j in range(nchip):
            for lo, n in _chunks(half, D2D_SPLIT):
                _remote(g.at[j, pl.ds((1 - c) * half + lo, n), :], theirs.at[j, pl.ds(lo, n), :],
                        send_sems, recv_sems, 0, (x, y, 1 - c)).start()
        _remote(g.at[:, pl.ds((1 - c) * half, half), :], theirs, send_sems, recv_sems, 0, (x, y, c)).wait()

    return _pcall(
        body, name="reduce_cores", in_specs=[HBM_SPEC], out_specs=HBM_SPEC,
        out_shape=_sds((nchip, half, lanes), grads.dtype),
        scratch_shapes=[pltpu.SemaphoreType.DMA((1,)), pltpu.SemaphoreType.DMA((1,))],
    )(grads)


def _scatter_chips(part):
    nchip, half, lanes = part.shape

    def body(p, out, send_sems, recv_sems):
        x, y, c = _place()
        for k, (cx, cy) in enumerate(_other_chips(x, y)):
            for lo, n in _chunks(half, ICI_SPLIT):
                _remote(p.at[2 * cx + cy, pl.ds(lo, n), :], out.at[k, pl.ds(lo, n), :],
                        send_sems, recv_sems, k, (cx, cy, c)).start()
        for k in range(3):
            _remote(p.at[k], out.at[k], send_sems, recv_sems, k, (x, y, c)).wait()

    return _pcall(
        body, name="scatter_chips", in_specs=[HBM_SPEC], out_specs=HBM_SPEC,
        out_shape=_sds((3, half, lanes), part.dtype),
        scratch_shapes=[pltpu.SemaphoreType.DMA((3,)), pltpu.SemaphoreType.DMA((3,))],
    )(part)


def _sum_partials(received, part, place, tm):
    _, half, lanes = received.shape
    tm = min(tm, half)
    nblk = half // tm

    def body(place_ref, r_ref, p_ref, o_ref):
        tot = p_ref[...].astype(F32)
        for k in range(3):
            tot = tot + r_ref[k].astype(F32)
        o_ref[...] = tot

    return _pcall(
        body, name="sum_chip_partials",
        grid_spec=pltpu.PrefetchScalarGridSpec(
            num_scalar_prefetch=1, grid=(nblk,),
            in_specs=[pl.BlockSpec((3, tm, lanes), lambda i, pc: (0, i, 0)),
                      pl.BlockSpec((None, tm, lanes), lambda i, pc: (pc[0], i, 0))],
            out_specs=pl.BlockSpec((tm, lanes), lambda i, pc: (pc[1] * nblk + i, 0))),
        out_shape=_sds((2 * half, lanes), F32),
        compiler_params=_cparams("parallel"))(place, received, part)


def _share_cores(block):
    rows, lanes = block.shape
    half = rows // 2

    def body(src, out, send_sems, recv_sems):
        x, y, c = _place()
        for lo, n in _chunks(half, D2D_SPLIT):
            piece = pl.ds(c * half + lo, n)
            _remote(src.at[piece, :], out.at[piece, :], send_sems, recv_sems, 0, (x, y, 1 - c)).start()
        mine = out.at[pl.ds(c * half, half), :]
        theirs = out.at[pl.ds((1 - c) * half, half), :]
        _remote(mine, theirs, send_sems, recv_sems, 0, (x, y, c)).wait()

    return _pcall(
        body, name="share_cores", in_specs=[HBM_SPEC], out_specs=HBM_SPEC,
        out_shape=_sds((rows, lanes), block.dtype), input_output_aliases={0: 0},
        scratch_shapes=[pltpu.SemaphoreType.DMA((1,)), pltpu.SemaphoreType.DMA((1,))],
    )(block)


def _sum_blocks(stacked, name, tm):
    n, rows, lanes = stacked.shape
    tm = min(tm, rows)

    def body(s_ref, o_ref):
        tot = s_ref[n - 1].astype(F32)
        for k in range(n - 1):
            tot = tot + s_ref[k].astype(F32)
        o_ref[...] = tot

    return _pcall(body, name=name, grid=(rows // tm,),
                  in_specs=[pl.BlockSpec((n, tm, lanes), lambda i: (0, i, 0))],
                  out_specs=pl.BlockSpec((tm, lanes), lambda i: (i, 0)), out_shape=_sds((rows, lanes), F32),
                  compiler_params=_cparams("parallel"))(stacked)


def _add_halves(grads, theirs, core, tm):
    n, half, lanes = theirs.shape
    tm = min(tm, half)
    nblk = half // tm

    def body(c_ref, g_ref, t_ref, o_ref):
        o_ref[...] = (g_ref[...] + t_ref[...]).astype(o_ref.dtype)

    spec = pl.BlockSpec((None, tm, lanes), lambda k, i, c: (k, i, 0))
    return _pcall(
        body, name="add_core_halves",
        grid_spec=pltpu.PrefetchScalarGridSpec(
            num_scalar_prefetch=1, grid=(n, nblk),
            in_specs=[pl.BlockSpec((None, tm, lanes), lambda k, i, c: (k, c[0] * nblk + i, 0)), spec], out_specs=spec),
        out_shape=_sds((n, half, lanes), BF16),
        compiler_params=_cparams("parallel", "parallel"))(core, grads, theirs)


def _allreduce_small(part):
    rows, lanes = part.shape
    ndev = 8

    def body(src, tot, buf, send_sems, recv_sems):
        x, y, c = _place()
        me = 4 * x + 2 * y + c
        buf[me] = src[...]
        sends = []
        for k in range(1, ndev):
            peer = (x ^ (k >> 2), y ^ ((k >> 1) & 1), c ^ (k & 1))
            cp = _remote(src, buf.at[me], send_sems, recv_sems, k - 1, peer)
            cp.start()
            sends.append(cp)
        for k in range(1, ndev):
            theirs = buf.at[me ^ k]
            _remote(theirs, theirs, send_sems, recv_sems, k - 1, (x, y, c)).wait_recv()
        for cp in sends:
            cp.wait_send()
        acc = buf[0]
        for d in range(1, ndev):
            acc = acc + buf[d]
        tot[...] = acc

    vm = pl.BlockSpec(memory_space=pltpu.VMEM)
    return _pcall(
        body, name="allreduce_small", in_specs=[vm], out_specs=vm, out_shape=_sds((rows, lanes), F32),
        scratch_shapes=[pltpu.VMEM((ndev, rows, lanes), F32), pltpu.SemaphoreType.DMA((ndev - 1,)),
                        pltpu.SemaphoreType.DMA((ndev - 1,))],
    )(part)


def _big_rows():
    return [int(np.prod(shape)) // LANES for _, shape in BIG]


def _pack_big(blocks, dtype):
    parts = [blocks[name].reshape(blocks[name].shape[0], -1, LANES).astype(dtype) for name, _ in BIG]
    return jnp.concatenate(parts, axis=1)


def _unpack_big(packed):
    out, off = {}, 0
    for (name, shape), r in zip(BIG, _big_rows()):
        out[name] = packed[:, off:off + r].reshape((packed.shape[0],) + shape)
        off += r
    return out


def _pack_small(vals):
    parts = []
    for name, shape, r in SMALL:
        flat = vals[name].reshape(-1).astype(F32)
        parts.append(jnp.pad(flat, (0, r * LANES - flat.shape[0])).reshape(r, LANES))
    used = sum(r for _, _, r in SMALL)
    parts.append(jnp.zeros((SMALL_ROWS - used, LANES), F32))
    return jnp.concatenate(parts, axis=0)


def _unpack_small(packed):
    out, off = {}, 0
    for name, shape, r in SMALL:
        n = int(np.prod(shape))
        out[name] = packed[off:off + r].reshape(-1)[:n].reshape(shape)
        off += r
    return out


def _heads_major(a, nh):
    t = a.shape[0]
    return a.reshape(t, nh, a.shape[1] // nh).transpose(1, 0, 2)


def _tokens_major(a):
    nh, t, w = a.shape
    return a.transpose(1, 0, 2).reshape(t, nh * w)


def _local_step(x, target, small, wfull):
    t = x.shape[0]
    nh, hd = DIL_HEADS, DIL_HD
    w_in = wfull["w_in"].transpose(1, 0, 2).reshape(D_MODEL, -1)
    w_out = wfull["w_out"].reshape(D_MODEL, D_MODEL)
    w_qb, w_kvb = wfull["mla_w_q_b"], wfull["mla_w_kv_b"]
    grads_s, grads_b = {}, {}

    x1, ffn1_saved = _ffn_fwd(x, small["ffn1_norm"], wfull["ffn1_w_gate"], wfull["ffn1_w_up"],
                              wfull["ffn1_w_down"], "ffn1")
    hm = _rms_fwd(x1, small["mix_norm"], BF16, "mix_norm", 512)
    proj = _mm_simple("in_proj", hm, w_in, NN, F32)
    q_a, k_a, v_a = proj[:, :512], proj[:, 512:1024], proj[:, 1024:1536]
    cq, ckv, k_pe = proj[:, 1536:1792], proj[:, 1792:1920], proj[:, 1920:1984]

    q_h = _heads_major(q_a, nh).reshape(nh * t, hd)
    k_h = _heads_major(k_a, nh).reshape(nh * t, hd)
    v_h = _heads_major(v_a, nh).astype(BF16)
    qn = _rms_fwd(q_h, small["dil_q_norm"], BF16, "dil_q_norm", 2048).reshape(nh, t, hd)
    kn = _rms_fwd(k_h, small["dil_k_norm"], BF16, "dil_k_norm", 2048).reshape(nh, t, hd)
    bias = _bias_tiles(small["rel_bias"])
    branch_in, outs, lses = [], [], []
    for b, dil in enumerate(DIL_DILATIONS):
        qs, ks, vs = _to_sub(qn, dil), _to_sub(kn, dil), _to_sub(v_h, dil)
        o_b, lse_b = _dil_fwd(qs, ks, vs, bias[b], t // dil, f"dil_fwd_{dil}")
        branch_in.append((qs, ks, vs))
        outs.append(_from_sub(o_b, dil).reshape(nh * t, hd))
        lses.append(_from_sub(lse_b, dil).reshape(nh * t, 1))
    o_dil_h, lse_tot = _dil_merge(outs, lses, 2048)
    o_dil = _tokens_major(o_dil_h.reshape(nh, t, hd))

    mh = MLA_HEADS
    cos_t, sin_t = _rope_tables(t)
    cqn = _rms_fwd(cq, small["mla_q_a_norm"], BF16, "mla_q_a_norm", 512)
    ckvn = _rms_fwd(ckv, small["mla_kv_a_norm"], BF16, "mla_kv_a_norm", 512)
    tm = min(512, t)

    def head_proj(name, a, w, width):
        k = a.shape[1]
        return _mm(name, (mh, t // tm, 1),
                   [(a, pl.BlockSpec((tm, k), lambda h, i, r: (i, 0)), w, pl.BlockSpec((None, k, width), lambda h, i, r: (h, 0, 0)))],
                   NN, _sds((mh, t, width), F32), pl.BlockSpec((None, tm, width), lambda h, i, r: (h, i, 0)), (tm, width))

    q_raw = head_proj("mla_q_proj", cqn, w_qb, MLA_QK)
    kv_raw = head_proj("mla_kv_proj", ckvn, w_kvb, MLA_NOPE + MLA_V)
    k_raw = jnp.concatenate([kv_raw[:, :, :MLA_NOPE], jnp.broadcast_to(k_pe[None], (mh, t, MLA_ROPE))], axis=2)
    v_m = kv_raw[:, :, MLA_NOPE:].astype(BF16)
    q_raw2, k_raw2 = q_raw.reshape(mh * t, MLA_QK), k_raw.reshape(mh * t, MLA_QK)
    q_scale = MLA_QK ** -0.5
    q_m = _mla_qk_fwd(q_raw2, small["mla_q_norm"], cos_t, sin_t, q_scale, "mla_q_rope", 512).reshape(mh, t, MLA_QK)
    k_m = _mla_qk_fwd(k_raw2, small["mla_k_norm"], cos_t, sin_t, 1.0, "mla_k_rope", 512).reshape(mh, t, MLA_QK)
    o_mla_h, lse_m = _mla_fwd(q_m, k_m, v_m, 512, 2048)
    o_mla = _tokens_major(o_mla_h)

    od = _rms_fwd(o_dil, small["out_norm_dil"], BF16, "out_norm_dil", 512)
    om = _rms_fwd(o_mla, small["out_norm_mla"], BF16, "out_norm_mla", 512)
    half_w = DIL_WIDTH
    row = pl.BlockSpec((tm, D_MODEL), lambda i, j, r: (i, 0))
    act_spec = pl.BlockSpec((tm, half_w), lambda i, j, r: (i, 0))
    x2 = _mm("out_proj", (t // tm, 1, 1),
             [(od, act_spec, w_out, pl.BlockSpec((half_w, D_MODEL), lambda i, j, r: (0, 0))),
              (om, act_spec, w_out, pl.BlockSpec((half_w, D_MODEL), lambda i, j, r: (1, 0)))],
             NN, _sds((t, D_MODEL), F32), row, (tm, D_MODEL), res=(x1, row))
    x3, ffn2_saved = _ffn_fwd(x2, small["ffn2_norm"], wfull["ffn2_w_gate"], wfull["ffn2_w_up"],
                              wfull["ffn2_w_down"], "ffn2")
    dy, loss = _loss_head(x3, target, 512)

    dx2, grads_s["ffn2_norm"], grads_b["ffn2_w_gate"], grads_b["ffn2_w_up"], grads_b["ffn2_w_down"] = _ffn_bwd(
        dy, x2, small["ffn2_norm"], wfull["ffn2_w_gate"], wfull["ffn2_w_up"], wfull["ffn2_w_down"], ffn2_saved, "ffn2")

    d_ocat = _mm_simple("out_proj_dx", dx2, w_out, NT, F32)
    tk = min(512, t)
    tok = pl.BlockSpec((tk, half_w), lambda c, j, r: (r, 0))
    dw_out_d = _mm_simple("out_proj_dw_dil", od, dx2, TN, F32)
    dw_out_m = _mm_simple("out_proj_dw_mla", om, dx2, TN, F32)
    grads_b["w_out"] = jnp.concatenate([dw_out_d, dw_out_m], axis=0).reshape(N_CHIPS, D_MODEL // N_CHIPS, D_MODEL)
    do_dil, grads_s["out_norm_dil"] = _rms_bwd([d_ocat[:, :half_w]], o_dil, small["out_norm_dil"], None, "out_norm_dil_bwd", 512)
    do_mla, grads_s["out_norm_mla"] = _rms_bwd([d_ocat[:, half_w:]], o_mla, small["out_norm_mla"], None, "out_norm_mla_bwd", 512)

    do_m = _heads_major(do_mla, mh)
    dl_m = _rowdot(do_m.reshape(mh * t, MLA_V), o_mla_h.reshape(mh * t, MLA_V), "mla_delta", 2048).reshape(mh, t, 1)
    dq_m = _mla_bwd_dq(q_m, k_m, v_m, do_m, lse_m, dl_m, 512, 2048)
    dk_m, dv_m = _mla_bwd_dkv(q_m, k_m, v_m, do_m, lse_m.reshape(mh, 1, t), dl_m.reshape(mh, 1, t), 2048, 512)
    dq_raw, grads_s["mla_q_norm"] = _mla_qk_bwd(dq_m.reshape(mh * t, MLA_QK), q_raw2, small["mla_q_norm"],
                                                 cos_t, sin_t, q_scale, "mla_q_rope_bwd", 512)
    dk_raw, grads_s["mla_k_norm"] = _mla_qk_bwd(dk_m.reshape(mh * t, MLA_QK), k_raw2, small["mla_k_norm"],
                                                 cos_t, sin_t, 1.0, "mla_k_rope_bwd", 512)
    dq_raw = dq_raw.reshape(mh, t, MLA_QK)
    dk_raw = dk_raw.reshape(mh, t, MLA_QK)
    dkv_raw = jnp.concatenate([dk_raw[:, :, :MLA_NOPE], dv_m], axis=2)
    dk_pe_h = dk_raw[:, :, MLA_NOPE:]

    def head_proj_dx(name, d, w):
        width, k = d.shape[2], w.shape[1]
        return _mm(name, (t // tm, 1, mh),
                   [(d, pl.BlockSpec((None, tm, width), lambda i, j, r: (r, i, 0)), w, pl.BlockSpec((None, k, width), lambda i, j, r: (r, 0, 0)))],
                   NT, _sds((t, k), F32), pl.BlockSpec((tm, k), lambda i, j, r: (i, 0)), (tm, k))

    def head_proj_dw(name, a, d):
        width, k = d.shape[2], a.shape[1]
        return _mm(name, (mh, 1, t // tk),
                   [(a, pl.BlockSpec((tk, k), lambda h, j, r: (r, 0)), d, pl.BlockSpec((None, tk, width), lambda h, j, r: (h, r, 0)))],
                   TN, _sds((mh, k, width), F32), pl.BlockSpec((None, k, width), lambda h, j, r: (h, 0, 0)), (k, width))

    d_cqn = head_proj_dx("mla_q_proj_dx", dq_raw, w_qb)
    d_ckvn = head_proj_dx("mla_kv_proj_dx", dkv_raw, w_kvb)
    grads_b["mla_w_q_b"] = head_proj_dw("mla_q_proj_dw", cqn, dq_raw)
    grads_b["mla_w_kv_b"] = head_proj_dw("mla_kv_proj_dw", ckvn, dkv_raw)
    d_cq, grads_s["mla_q_a_norm"] = _rms_bwd([d_cqn], cq, small["mla_q_a_norm"], None, "mla_q_a_norm_bwd", 512)
    d_ckv, grads_s["mla_kv_a_norm"] = _rms_bwd([d_ckvn], ckv, small["mla_kv_a_norm"], None, "mla_kv_a_norm_bwd", 512)
    d_kpe = _sum_blocks(dk_pe_h.reshape(mh, t * MLA_ROPE // LANES, LANES), "mla_kpe_sum", 1024).reshape(t, MLA_ROPE)

    do_h = _heads_major(do_dil, nh)
    dl_d = _rowdot(do_h.reshape(nh * t, hd), o_dil_h, "dil_delta", 2048).reshape(nh, t, 1)
    lse_t = lse_tot.reshape(nh, t, 1)
    do_hb = do_h.astype(BF16)
    dqs, dks, dvs, dtiles = [], [], [], []
    for b, dil in enumerate(DIL_DILATIONS):
        qs, ks, vs = branch_in[b]
        dq_b, dk_b, dv_b, db_b = _dil_bwd(qs, ks, vs, _to_sub(do_hb, dil), _to_sub(lse_t, dil), _to_sub(dl_d, dil),
                                          bias[b], t // dil, f"dil_bwd_{dil}")
        dqs.append(_from_sub(dq_b, dil).reshape(nh * t, hd))
        dks.append(_from_sub(dk_b, dil).reshape(nh * t, hd))
        dvs.append(_from_sub(dv_b, dil).reshape(nh * t, hd))
        dtiles.append(db_b)
    grads_s["rel_bias"] = _bias_grad(jnp.stack(dtiles))
    dq_a_h, grads_s["dil_q_norm"] = _rms_bwd(dqs, q_h, small["dil_q_norm"], None, "dil_q_norm_bwd", 2048)
    dk_a_h, grads_s["dil_k_norm"] = _rms_bwd(dks, k_h, small["dil_k_norm"], None, "dil_k_norm_bwd", 2048)
    dv_a_h = _add3(dvs[0], dvs[1], dvs[2], "dil_dv_sum", 2048)
    dproj = jnp.concatenate([_tokens_major(dq_a_h.reshape(nh, t, hd)), _tokens_major(dk_a_h.reshape(nh, t, hd)),
                             _tokens_major(dv_a_h.reshape(nh, t, hd)), d_cq, d_ckv, d_kpe], axis=1)

    d_hm = _mm_simple("in_proj_dx", dproj, w_in, NT, F32)
    dw_in = _mm_simple("in_proj_dw", hm, dproj, TN, F32)
    grads_b["w_in"] = dw_in.reshape(D_MODEL, N_CHIPS, -1).transpose(1, 0, 2)
    dx1, grads_s["mix_norm"] = _rms_bwd([d_hm], x1, small["mix_norm"], dx2, "mix_norm_bwd", 512)
    dx, grads_s["ffn1_norm"], grads_b["ffn1_w_gate"], grads_b["ffn1_w_up"], grads_b["ffn1_w_down"] = _ffn_bwd(
        dx1, x, small["ffn1_norm"], wfull["ffn1_w_gate"], wfull["ffn1_w_up"], wfull["ffn1_w_down"], ffn1_saved, "ffn1")
    return loss, dx, grads_s, grads_b


def kernel(x, ffn1_norm, ffn1_w_gate, ffn1_w_up, ffn1_w_down, mix_norm, w_in, dil_q_norm, dil_k_norm, rel_bias, mla_q_a_norm, mla_w_q_b, mla_kv_a_norm, mla_w_kv_b, mla_q_norm, mla_k_norm, out_norm_dil, out_norm_mla, w_out, ffn2_norm, ffn2_w_gate, ffn2_w_up, ffn2_w_down, loss_target, m_ffn1_norm, m_ffn1_w_gate, m_ffn1_w_up, m_ffn1_w_down, m_mix_norm, m_w_in, m_dil_q_norm, m_dil_k_norm, m_rel_bias, m_mla_q_a_norm, m_mla_w_q_b, m_mla_kv_a_norm, m_mla_w_kv_b, m_mla_q_norm, m_mla_k_norm, m_out_norm_dil, m_out_norm_mla, m_w_out, m_ffn2_norm, m_ffn2_w_gate, m_ffn2_w_up, m_ffn2_w_down, v_ffn1_norm, v_ffn1_w_gate, v_ffn1_w_up, v_ffn1_w_down, v_mix_norm, v_w_in, v_dil_q_norm, v_dil_k_norm, v_rel_bias, v_mla_q_a_norm, v_mla_w_q_b, v_mla_kv_a_norm, v_mla_w_kv_b, v_mla_q_norm, v_mla_k_norm, v_out_norm_dil, v_out_norm_mla, v_w_out, v_ffn2_norm, v_ffn2_w_gate, v_ffn2_w_up, v_ffn2_w_down):
    given = dict(locals())
    big_names = [name for name, _ in BIG]
    small_names = [name for name, _, _ in SMALL]

    chip = (2 * lax.axis_index("x") + lax.axis_index("y")).astype(jnp.int32)
    core = lax.axis_index("c").astype(jnp.int32)
    mine = _pack_big({n: given[n] for n in big_names}, BF16)
    gathered = lax.dynamic_update_slice(_gather_weights(mine[0]), mine, (chip, 0, 0))
    wfull = _unpack_big(gathered)
    small = {n: given[n] for n in small_names}

    loss, dx, grads_s, grads_b = _local_step(x[0], loss_target[0], small, wfull)
    loss = lax.psum(loss[0, 0], ("x", "y", "c"))

    packed = _pack_big(grads_b, F32).reshape(N_CHIPS, -1, LANES)
    chip_part = _add_halves(packed, _reduce_cores(packed), core.reshape(1), 1264)
    reduced = _sum_partials(_scatter_chips(chip_part), chip_part, jnp.stack([chip, core]), 1264)
    g_big = _unpack_big(_share_cores(reduced)[None])
    g_small = _unpack_small(_allreduce_small(_pack_small(grads_s)))

    grad, delta, new_m, new_v = {}, {}, {}, {}
    for name, shape in BIG:
        g2 = g_big[name].reshape(shape)
        d_, m_, v_ = _adamw(given[name].reshape(shape), g2, given["m_" + name].reshape(shape),
                            given["v_" + name].reshape(shape), f"adamw_{name}")
        full = given[name].shape
        grad[name], delta[name], new_m[name], new_v[name] = (a.reshape(full) for a in (g2, d_, m_, v_))
    ps = {k: _pack_small({n: given[pre + n] for n in small_names}) for k, pre in (("w", ""), ("m", "m_"), ("v", "v_"))}
    gs_packed = _pack_small(g_small)
    d_s, m_s, v_s = (_unpack_small(a) for a in _adamw(ps["w"], gs_packed, ps["m"], ps["v"], "adamw_small"))
    for name in small_names:
        grad[name], delta[name], new_m[name], new_v[name] = g_small[name], d_s[name], m_s[name], v_s[name]

    return (loss, dx[None], *[grad[n] for n in WEIGHTS], *[delta[n] for n in WEIGHTS],
            *[new_m[n] for n in WEIGHTS], *[new_v[n] for n in WEIGHTS])
```

```python
import functools

import numpy as np
import jax
import jax.numpy as jnp
from jax import lax
from jax.experimental import pallas as pl
from jax.experimental.pallas import tpu as pltpu

F32 = jnp.float32
BF16 = jnp.bfloat16

D_MODEL = 1024
D_FF = 2816
N_CHIPS = 4
DIL_HEADS = 8
DIL_HD = 64
DIL_WIDTH = 512
DIL_DILATIONS = (1, 4, 16)
DIL_W = 128
QB = 128
MLA_HEADS = 4
MLA_NOPE = 128
MLA_ROPE = 64
MLA_QK = 192
MLA_V = 128
MLA_Q_RANK = 256
MLA_KV_RANK = 128
ROPE_BASE = 10000.0
REL_BUCKETS = 32
REL_MAX_DIST = 2048
FFN_RESID = 0.5
EPS = 1e-6
NEG = -1e30
LANES = 128

ADAM_LR = 0.001
ADAM_B1 = 0.9
ADAM_B2 = 0.999
ADAM_EPS = 1e-08
ADAM_WD = 0.01
ADAM_STEP = 10

NT = (((1,), (1,)), ((), ()))
NN = (((1,), (0,)), ((), ()))
TN = (((0,), (0,)), ((), ()))

BIG = (
    ("ffn1_w_gate", (D_MODEL, D_FF // N_CHIPS)),
    ("ffn1_w_up", (D_MODEL, D_FF // N_CHIPS)),
    ("ffn1_w_down", (D_FF // N_CHIPS, D_MODEL)),
    ("w_in", (D_MODEL, 1984 // N_CHIPS)),
    ("mla_w_q_b", (MLA_Q_RANK, MLA_QK)),
    ("mla_w_kv_b", (MLA_KV_RANK, MLA_NOPE + MLA_V)),
    ("w_out", (D_MODEL // N_CHIPS, D_MODEL)),
    ("ffn2_w_gate", (D_MODEL, D_FF // N_CHIPS)),
    ("ffn2_w_up", (D_MODEL, D_FF // N_CHIPS)),
    ("ffn2_w_down", (D_FF // N_CHIPS, D_MODEL)),
)
SMALL = (
    ("ffn1_norm", (1, 1024), 8), ("mix_norm", (1, 1024), 8), ("dil_q_norm", (1, 64), 1),
    ("dil_k_norm", (1, 64), 1), ("rel_bias", (8, 32), 2), ("mla_q_a_norm", (1, 256), 2),
    ("mla_kv_a_norm", (1, 128), 1), ("mla_q_norm", (1, 192), 2), ("mla_k_norm", (1, 192), 2),
    ("out_norm_dil", (1, 512), 4), ("out_norm_mla", (1, 512), 4), ("ffn2_norm", (1, 1024), 8),
)
SMALL_ROWS = 48
WEIGHTS = ("ffn1_norm", "ffn1_w_gate", "ffn1_w_up", "ffn1_w_down", "mix_norm", "w_in", "dil_q_norm",
           "dil_k_norm", "rel_bias", "mla_q_a_norm", "mla_w_q_b", "mla_kv_a_norm", "mla_w_kv_b",
           "mla_q_norm", "mla_k_norm", "out_norm_dil", "out_norm_mla", "w_out", "ffn2_norm",
           "ffn2_w_gate", "ffn2_w_up", "ffn2_w_down")


def _pcall(body, **kw):
    return pl.pallas_call(body, **kw)


def _cparams(*sem):
    return pltpu.CompilerParams(dimension_semantics=sem)


def _sds(shape, dtype):
    return jax.ShapeDtypeStruct(shape, dtype)


def _dot(a, b, dn):
    return lax.dot_general(a, b, dn, preferred_element_type=F32)


def _rms_fwd(x, g, out_dtype, name, tm):
    n, d = x.shape
    tm = min(tm, n)

    def body(x_ref, g_ref, o_ref):
        xf = x_ref[...].astype(F32)
        r = lax.rsqrt(jnp.mean(xf * xf, axis=-1, keepdims=True) + EPS)
        o_ref[...] = (xf * r * g_ref[...]).astype(o_ref.dtype)

    return _pcall(
        body, name=name, grid=(n // tm,),
        in_specs=[pl.BlockSpec((tm, d), lambda i: (i, 0)), pl.BlockSpec((1, d), lambda i: (0, 0))],
        out_specs=pl.BlockSpec((tm, d), lambda i: (i, 0)),
        out_shape=_sds((n, d), out_dtype), compiler_params=_cparams("parallel"))(x, g)


def _rms_bwd(dys, x, g, res, name, tm):
    n, d = x.shape
    tm = min(tm, n)
    nd = len(dys)
    has_res = res is not None

    def body(*refs):
        dy_refs = refs[:nd]
        x_ref, g_ref = refs[nd], refs[nd + 1]
        res_ref = refs[nd + 2] if has_res else None
        dx_ref, dg_ref = refs[-2], refs[-1]
        dy = dy_refs[0][...].astype(F32)
        for r_ in dy_refs[1:]:
            dy = dy + r_[...].astype(F32)
        xf = x_ref[...].astype(F32)
        r = lax.rsqrt(jnp.mean(xf * xf, axis=-1, keepdims=True) + EPS)
        xh = xf * r
        dxh = dy * g_ref[...]
        dx = r * (dxh - xh * jnp.mean(dxh * xh, axis=-1, keepdims=True))
        if has_res:
            dx = dx + res_ref[...]
        dx_ref[...] = dx

        @pl.when(pl.program_id(0) == 0)
        def _():
            dg_ref[...] = jnp.zeros_like(dg_ref)

        dg_ref[...] += jnp.sum(dy * xh, axis=0, keepdims=True)

    row = pl.BlockSpec((tm, d), lambda i: (i, 0))
    vec = pl.BlockSpec((1, d), lambda i: (0, 0))
    ins = list(dys) + [x, g] + ([res] if has_res else [])
    return _pcall(
        body, name=name, grid=(n // tm,),
        in_specs=[row] * nd + [row, vec] + ([row] if has_res else []),
        out_specs=(row, vec),
        out_shape=(_sds((n, d), F32), _sds((1, d), F32)),
        compiler_params=_cparams("arbitrary"))(*ins)


def _mm(name, grid, pairs, dn, out_shape, out_spec, acc_shape, res=None, scale=1.0):
    npairs = len(pairs)
    nred = grid[2]
    has_res = res is not None

    def body(*refs):
        ab = refs[:2 * npairs]
        res_ref = refs[2 * npairs] if has_res else None
        o_ref = refs[2 * npairs + int(has_res)]
        acc_ref = refs[-1] if nred > 1 else None
        tot = None
        for p in range(npairs):
            d = _dot(ab[2 * p][...].astype(BF16), ab[2 * p + 1][...].astype(BF16), dn)
            tot = d if tot is None else tot + d

        def finish(v):
            if scale != 1.0:
                v = v * scale
            if has_res:
                v = res_ref[...] + v
            o_ref[...] = v.astype(o_ref.dtype)

        if nred == 1:
            finish(tot)
        else:
            r = pl.program_id(2)

            @pl.when(r == 0)
            def _():
                acc_ref[...] = tot

            @pl.when(r > 0)
            def _():
                acc_ref[...] += tot

            @pl.when(r == nred - 1)
            def _():
                finish(acc_ref[...])

    ins, specs = [], []
    for a, a_spec, b, b_spec in pairs:
        ins += [a, b]
        specs += [a_spec, b_spec]
    if has_res:
        ins.append(res[0])
        specs.append(res[1])
    return _pcall(
        body, name=name, grid=grid, in_specs=specs, out_specs=out_spec, out_shape=out_shape,
        scratch_shapes=[pltpu.VMEM(acc_shape, F32)] if nred > 1 else [],
        compiler_params=_cparams("parallel", "parallel", "arbitrary"))(*ins)


def _ffn_up(h, wg, wu, name, tm):
    t, d = h.shape
    nc, _, fs = wg.shape
    tm = min(tm, t)

    def body(h_ref, wg_ref, wu_ref, g_ref, u_ref, a_ref):
        hh = h_ref[...]
        gate = _dot(hh, wg_ref[...], NN)
        up = _dot(hh, wu_ref[...], NN)
        g_ref[...] = gate.astype(BF16)
        u_ref[...] = up.astype(BF16)
        a_ref[...] = (gate * jax.nn.sigmoid(gate) * up).astype(BF16)

    wspec = pl.BlockSpec((None, d, fs), lambda c, i: (c, 0, 0))
    ospec = pl.BlockSpec((None, tm, fs), lambda c, i: (c, i, 0))
    osd = _sds((nc, t, fs), BF16)
    return _pcall(
        body, name=name, grid=(nc, t // tm),
        in_specs=[pl.BlockSpec((tm, d), lambda c, i: (i, 0)), wspec, wspec],
        out_specs=(ospec, ospec, ospec), out_shape=(osd, osd, osd),
        compiler_params=_cparams("parallel", "parallel"))(h, wg, wu)


def _ffn_dact(dy, wd, gate, up, name, tm):
    t, d = dy.shape
    nc, fs, _ = wd.shape
    tm = min(tm, t)

    def body(dy_ref, wd_ref, g_ref, u_ref, dg_ref, du_ref):
        da = _dot(dy_ref[...].astype(BF16), wd_ref[...], NT) * FFN_RESID
        gate = g_ref[...].astype(F32)
        up = u_ref[...].astype(F32)
        sig = jax.nn.sigmoid(gate)
        dg_ref[...] = (da * up * (sig * (1.0 + gate * (1.0 - sig)))).astype(BF16)
        du_ref[...] = (da * (gate * sig)).astype(BF16)

    cspec = pl.BlockSpec((None, tm, fs), lambda c, i: (c, i, 0))
    osd = _sds((nc, t, fs), BF16)
    return _pcall(
        body, name=name, grid=(nc, t // tm),
        in_specs=[pl.BlockSpec((tm, d), lambda c, i: (i, 0)),
                  pl.BlockSpec((None, fs, d), lambda c, i: (c, 0, 0)), cspec, cspec],
        out_specs=(cspec, cspec), out_shape=(osd, osd),
        compiler_params=_cparams("parallel", "parallel"))(dy, wd, gate, up)


def _ffn_fwd(x, g, wg, wu, wd, tag):
    t = x.shape[0]
    nc, _, fs = wg.shape
    tm = min(512, t)
    h = _rms_fwd(x, g, BF16, f"{tag}_norm", 512)
    gate, up, act = _ffn_up(h, wg, wu, f"{tag}_up", 1024)
    pairs = [(act, pl.BlockSpec((None, tm, fs), lambda i, j, r, c=c: (c, i, 0)),
              wd, pl.BlockSpec((None, fs, D_MODEL), lambda i, j, r, c=c: (c, 0, 0))) for c in range(nc)]
    row = pl.BlockSpec((tm, D_MODEL), lambda i, j, r: (i, 0))
    y = _mm(f"{tag}_down", (t // tm, 1, 1), pairs, NN, _sds((t, D_MODEL), F32), row, (tm, D_MODEL),
            res=(x, row), scale=FFN_RESID)
    return y, (h, gate, up, act)


def _ffn_bwd(dy, x, g, wg, wu, wd, saved, tag):
    h, gate, up, act = saved
    t = x.shape[0]
    nc, _, fs = wg.shape
    tm = min(512, t)
    tk = min(2048, t)
    dgate, dup = _ffn_dact(dy, wd, gate, up, f"{tag}_dact", 1024)
    tok_c = pl.BlockSpec((None, tk, fs), lambda c, j, r: (c, r, 0))
    tok_d = pl.BlockSpec((tk, D_MODEL), lambda c, j, r: (r, 0))
    dwd = _mm(f"{tag}_dwd", (nc, 1, t // tk), [(act, tok_c, dy, tok_d)], TN,
              _sds((nc, fs, D_MODEL), F32), pl.BlockSpec((None, fs, D_MODEL), lambda c, j, r: (c, 0, 0)),
              (fs, D_MODEL), scale=FFN_RESID)
    wout = pl.BlockSpec((None, D_MODEL, fs), lambda c, j, r: (c, 0, 0))
    dwg = _mm(f"{tag}_dwg", (nc, 1, t // tk), [(h, tok_d, dgate, tok_c)], TN,
              _sds((nc, D_MODEL, fs), F32), wout, (D_MODEL, fs))
    dwu = _mm(f"{tag}_dwu", (nc, 1, t // tk), [(h, tok_d, dup, tok_c)], TN,
              _sds((nc, D_MODEL, fs), F32), wout, (D_MODEL, fs))
    pairs = []
    for c in range(nc):
        a_spec = pl.BlockSpec((None, tm, fs), lambda i, j, r, c=c: (c, i, 0))
        w_spec = pl.BlockSpec((None, D_MODEL, fs), lambda i, j, r, c=c: (c, 0, 0))
        pairs += [(dgate, a_spec, wg, w_spec), (dup, a_spec, wu, w_spec)]
    dh = _mm(f"{tag}_dh", (t // tm, 1, 1), pairs, NT,
             _sds((t, D_MODEL), F32), pl.BlockSpec((tm, D_MODEL), lambda i, j, r: (i, 0)), (tm, D_MODEL))
    dx, dg = _rms_bwd([dh], x, g, dy, f"{tag}_dnorm", 512)
    return dx, dg, dwg, dwu, dwd


def _mm_simple(name, a, b, dn, out_dtype, tm=512, tk=512, res=None, scale=1.0):
    if dn == TN:
        k, m = a.shape
        n = b.shape[1]
        tk = min(tk, k)
        return _mm(name, (1, 1, k // tk),
                   [(a, pl.BlockSpec((tk, m), lambda i, j, r: (r, 0)), b, pl.BlockSpec((tk, n), lambda i, j, r: (r, 0)))],
                   TN, _sds((m, n), out_dtype), pl.BlockSpec((m, n), lambda i, j, r: (0, 0)), (m, n), scale=scale)
    m, k = a.shape
    n = b.shape[1] if dn == NN else b.shape[0]
    tm = min(tm, m)
    row = pl.BlockSpec((tm, n), lambda i, j, r: (i, 0))
    return _mm(name, (m // tm, 1, 1),
               [(a, pl.BlockSpec((tm, k), lambda i, j, r: (i, 0)), b, pl.BlockSpec(b.shape, lambda i, j, r: (0, 0)))],
               dn, _sds((m, n), out_dtype), row, (tm, n), res=None if res is None else (res, row), scale=scale)


def _t5_bucket(dist):
    max_exact = REL_BUCKETS // 2
    d = np.maximum(dist, 1).astype(np.float32)
    large = max_exact + (np.log(d / max_exact) / np.log(REL_MAX_DIST / max_exact)
                         * (REL_BUCKETS - max_exact)).astype(np.int32)
    large = np.minimum(large, REL_BUCKETS - 1)
    return np.where(dist < max_exact, dist, large).astype(np.int32)


def _bucket_tiles():
    i = np.arange(QB)[:, None]
    j = np.arange(QB + DIL_W)[None, :]
    delta = np.clip(i + DIL_W - j, 0, None)
    return np.stack([_t5_bucket(delta * dil) for dil in DIL_DILATIONS]).astype(np.int32)


def _bias_tiles(rel_bias):
    buckets = jnp.asarray(_bucket_tiles())

    def body(rb_ref, bk_ref, o_ref):
        bk = bk_ref[...]
        for h in range(DIL_HEADS):
            def pick(b, tile):
                return jnp.where(bk == b, rb_ref[h, b], tile)

            o_ref[h] = lax.fori_loop(0, REL_BUCKETS, pick, jnp.zeros((QB, QB + DIL_W), F32))

    return _pcall(
        body, name="dil_bias_tiles", grid=(3,),
        in_specs=[pl.BlockSpec(memory_space=pltpu.SMEM),
                  pl.BlockSpec((None, QB, QB + DIL_W), lambda b: (b, 0, 0))],
        out_specs=pl.BlockSpec((None, DIL_HEADS, QB, QB + DIL_W), lambda b: (b, 0, 0, 0)),
        out_shape=_sds((3, DIL_HEADS, QB, QB + DIL_W), F32),
        compiler_params=_cparams("parallel"))(rel_bias, buckets)


def _bias_grad(dtiles):
    buckets = jnp.asarray(_bucket_tiles())

    def body(dt_ref, bk_ref, o_ref):
        for h in range(DIL_HEADS):
            def one(b, carry):
                tot = jnp.zeros((), F32)
                for br in range(3):
                    tot = tot + jnp.sum(jnp.where(bk_ref[br] == b, dt_ref[br, h], 0.0))
                o_ref[h, b] = tot
                return carry

            lax.fori_loop(0, REL_BUCKETS, one, 0)

    return _pcall(
        body, name="dil_bias_grad",
        in_specs=[pl.BlockSpec(memory_space=pltpu.VMEM), pl.BlockSpec(memory_space=pltpu.VMEM)],
        out_specs=pl.BlockSpec(memory_space=pltpu.SMEM),
        out_shape=_sds((DIL_HEADS, REL_BUCKETS), F32))(dtiles, buckets)


def _split_heads(a, lo):
    zero = jnp.zeros_like(a)
    return jnp.concatenate([jnp.where(lo, a, zero), jnp.where(lo, zero, a)], axis=0)


def _side_by_side(a):
    n = a.shape[0] // 2
    return jnp.concatenate([a[:n], a[n:]], axis=1)


def _band_masks(prev_ok):
    ii = lax.broadcasted_iota(jnp.int32, (2 * QB, QB), 0) & (QB - 1)
    jj = lax.broadcasted_iota(jnp.int32, (2 * QB, QB), 1)
    return jj <= ii, jj >= ii + jnp.where(prev_ok, 0, QB)


def _dil_view(a, dil):
    t, w = a.shape
    return a.reshape(t // dil, dil * w)


def _dil_fwd(q, k, v, bias, dil, name):
    t, w = q.shape
    npair = w // LANES
    nl = t // dil // QB
    scale = DIL_HD ** -0.5

    def body(q_ref, kc_ref, kp_ref, vc_ref, vp_ref, b_ref, o_ref, lse_ref):
        nn = pl.program_id(1)
        lo = lax.broadcasted_iota(jnp.int32, (QB, LANES), 1) < DIL_HD
        lo2 = lax.broadcasted_iota(jnp.int32, (2 * QB, LANES), 1) < DIL_HD
        ii = lax.broadcasted_iota(jnp.int32, (2 * QB, 2 * QB), 0) & (QB - 1)
        jj = lax.broadcasted_iota(jnp.int32, (2 * QB, 2 * QB), 1)
        first_key = jnp.maximum(ii, jnp.where(nn != 0, 0, QB))
        valid = (jj >= first_key) & (jj <= ii + QB)
        for p in range(npair):
            cols = slice(p * LANES, (p + 1) * LANES)
            qq = _split_heads(q_ref[:, cols], lo)
            kk = jnp.concatenate([kp_ref[:, cols], kc_ref[:, cols]], axis=0)
            vv = jnp.concatenate([vp_ref[:, cols], vc_ref[:, cols]], axis=0)
            s = jnp.where(valid, _dot(qq, kk, NT) * scale + b_ref[p], NEG)
            m = jnp.max(s, axis=-1, keepdims=True)
            e = jnp.exp(s - m)
            den = jnp.sum(e, axis=-1, keepdims=True)
            pn = (e * (1.0 / den)).astype(BF16)
            o_ref[:, cols] = _dot(_side_by_side(pn), _split_heads(vv, lo2), NN)
            lse = m + jnp.log(den)
            lse_ref[:, cols] = jnp.where(lo, lse[:QB], lse[QB:])

    cur = pl.BlockSpec((QB, w), lambda r, n: (n, r))
    prev = pl.BlockSpec((QB, w), lambda r, n: (jnp.maximum(n - 1, 0), r))
    sd = _sds((t // dil, dil * w), F32)
    o, lse = _pcall(
        body, name=name, grid=(dil, nl),
        in_specs=[cur, cur, prev, cur, prev, pl.BlockSpec((npair, 2 * QB, 2 * QB), lambda r, n: (0, 0, 0))],
        out_specs=(cur, cur), out_shape=(sd, sd),
        compiler_params=_cparams("parallel", "parallel"))(*[_dil_view(a, dil) for a in (q, k, k, v, v)], bias)
    return o.reshape(t, w), lse.reshape(t, w)


def _dil_bwd(q, k, v, do, lse, dl, bias, dil, name):
    t, w = q.shape
    npair = w // LANES
    nl = t // dil // QB
    scale = DIL_HD ** -0.5

    def body(qc_ref, qn_ref, doc_ref, don_ref, lc_ref, ln_ref, dc_ref, dn_ref, k_ref, v_ref, b_ref,
             dq_ref, dk_ref, dv_ref, db_ref, carry):
        r, nn = pl.program_id(0), pl.program_id(1)
        lo = lax.broadcasted_iota(jnp.int32, (QB, LANES), 1) < DIL_HD
        cur_ok, prev_ok = _band_masks(nn + 1 < nl)

        @pl.when((r == 0) & (nn == 0))
        def _():
            db_ref[...] = jnp.zeros_like(db_ref)
            carry[...] = jnp.zeros_like(carry)

        for p in range(npair):
            cols = slice(p * LANES, (p + 1) * LANES)
            head0, head1 = slice(p * LANES, p * LANES + 1), slice(p * LANES + DIL_HD, p * LANES + DIL_HD + 1)
            kp, vp = k_ref[:, cols], v_ref[:, cols]
            k2 = _split_heads(kp, lo)

            def side(q_ref, do_ref, l_ref, d_ref, bias, ok):
                qq = _split_heads(q_ref[:, cols], lo)
                dd = _split_heads(do_ref[:, cols].astype(BF16), lo)
                lse = jnp.concatenate([l_ref[:, head0], l_ref[:, head1]], axis=0)
                dl_ = jnp.concatenate([d_ref[:, head0], d_ref[:, head1]], axis=0)
                s = jnp.where(ok, _dot(qq, kp, NT) * scale + bias, NEG)
                prob = jnp.exp(s - lse)
                ds = prob * (_dot(dd, vp, NT) - dl_)
                return qq, dd, prob.astype(BF16), ds

            q1, d1, p1, ds1 = side(qc_ref, doc_ref, lc_ref, dc_ref, b_ref[p, :, QB:], cur_ok)
            q2, d2, p2, ds2 = side(qn_ref, don_ref, ln_ref, dn_ref, b_ref[p, :, :QB], prev_ok)
            ds1b, ds2b = ds1.astype(BF16), ds2.astype(BF16)
            dq_ref[:, cols] = carry[:, cols] + _dot(_side_by_side(ds1b), k2, NN) * scale
            carry[:, cols] = _dot(_side_by_side(ds2b), k2, NN) * scale
            dk_ref[:, cols] = _dot(jnp.concatenate([ds1b, ds2b], axis=0), jnp.concatenate([q1, q2], axis=0), TN) * scale
            dv_ref[:, cols] = _dot(jnp.concatenate([p1, p2], axis=0), jnp.concatenate([d1, d2], axis=0), TN)
            db_ref[p, :, QB:] += ds1
            db_ref[p, :, :QB] += ds2

    cur = pl.BlockSpec((QB, w), lambda r, n: (n, r))
    nxt = pl.BlockSpec((QB, w), lambda r, n: (jnp.minimum(n + 1, nl - 1), r))
    tile = pl.BlockSpec((npair, 2 * QB, 2 * QB), lambda r, n: (0, 0, 0))
    sd = _sds((t // dil, dil * w), F32)
    views = [_dil_view(a, dil) for a in (q, q, do, do, lse, lse, dl, dl, k, v)]
    dq, dk, dv, db = _pcall(
        body, name=name, grid=(dil, nl),
        in_specs=[cur, nxt, cur, nxt, cur, nxt, cur, nxt, cur, cur, tile],
        out_specs=(cur, cur, cur, tile),
        out_shape=(sd, sd, sd, _sds((npair, 2 * QB, 2 * QB), F32)),
        scratch_shapes=[pltpu.VMEM((QB, w), F32)],
        compiler_params=_cparams("arbitrary", "arbitrary"))(*views, bias)
    return dq.reshape(t, w), dk.reshape(t, w), dv.reshape(t, w), db


def _head_sum_matrix(scale):
    idx = np.arange(DIL_WIDTH) // DIL_HD
    return jnp.asarray((idx[:, None] == idx[None, :]).astype(np.float32) * scale, BF16)


def _head_sum(x, mat):
    hi = x.astype(BF16)
    lo = (x - hi.astype(F32)).astype(BF16)
    return _dot(hi, mat, NN) + _dot(lo, mat, NN)


def _dil_merge(outs, lses, g, tm):
    t, w = outs[0].shape
    tm = min(tm, t)

    def body(o0, o1, o2, l0, l1, l2, g_ref, o_ref, l_ref, n_ref):
        a0, a1, a2 = l0[...], l1[...], l2[...]
        m = jnp.maximum(jnp.maximum(a0, a1), a2)
        e0, e1, e2 = jnp.exp(a0 - m), jnp.exp(a1 - m), jnp.exp(a2 - m)
        den = e0 + e1 + e2
        o = (e0 * o0[...] + e1 * o1[...] + e2 * o2[...]) / den
        o_ref[...] = o
        l_ref[...] = m + jnp.log(den)
        r = lax.rsqrt(jnp.mean(o * o, axis=-1, keepdims=True) + EPS)
        n_ref[...] = (o * r * g_ref[...]).astype(n_ref.dtype)

    spec = pl.BlockSpec((tm, w), lambda i: (i, 0))
    return _pcall(
        body, name="dil_merge", grid=(t // tm,),
        in_specs=[spec] * 6 + [pl.BlockSpec((1, w), lambda i: (0, 0))], out_specs=(spec, spec, spec),
        out_shape=(_sds((t, w), F32), _sds((t, w), F32), _sds((t, w), BF16)),
        compiler_params=_cparams("parallel"))(*outs, *lses, g)


def _head_rowdot(a, b, tm):
    t, w = a.shape
    tm = min(tm, t)

    def body(a_ref, b_ref, m_ref, o_ref):
        o_ref[...] = _head_sum(a_ref[...] * b_ref[...], m_ref[...])

    spec = pl.BlockSpec((tm, w), lambda i: (i, 0))
    return _pcall(body, name="dil_delta", grid=(t // tm,),
                  in_specs=[spec, spec, pl.BlockSpec((w, w), lambda i: (0, 0))], out_specs=spec,
                  out_shape=_sds((t, w), F32), compiler_params=_cparams("parallel"))(a, b, _head_sum_matrix(1.0))


def _head_norm_fwd(x, col, g, name, tm):
    t = x.shape[0]
    w = DIL_WIDTH
    tm = min(tm, t)

    def body(x_ref, g_ref, m_ref, o_ref):
        xf = x_ref[...]
        r = lax.rsqrt(_head_sum(xf * xf, m_ref[...]) + EPS)
        o_ref[...] = (xf * r * g_ref[...]).astype(o_ref.dtype)

    return _pcall(
        body, name=name, grid=(t // tm,),
        in_specs=[pl.BlockSpec((tm, w), lambda i: (i, col)), pl.BlockSpec((1, w), lambda i: (0, 0)),
                  pl.BlockSpec((w, w), lambda i: (0, 0))],
        out_specs=pl.BlockSpec((tm, w), lambda i: (i, 0)), out_shape=_sds((t, w), BF16),
        compiler_params=_cparams("parallel"))(x, g, _head_sum_matrix(1.0 / DIL_HD))


def _head_norm_bwd(dys, x, col, g, name, tm):
    t = x.shape[0]
    w = DIL_WIDTH
    tm = min(tm, t)
    nd = len(dys)
    nt = t // tm
    lane = np.arange(w) % DIL_HD
    fold = jnp.asarray((lane[:, None] == lane[None, :]).astype(np.float32))

    def body(*refs):
        x_ref, g_ref, m_ref, f_ref = refs[nd:nd + 4]
        dx_ref, dg_ref = refs[-2], refs[-1]
        dy = refs[0][...]
        for r_ in refs[1:nd]:
            dy = dy + r_[...]
        xf = x_ref[...]
        mat = m_ref[...]
        r = lax.rsqrt(_head_sum(xf * xf, mat) + EPS)
        xh = xf * r
        dxh = dy * g_ref[...]
        dx_ref[...] = r * (dxh - xh * _head_sum(dxh * xh, mat))

        @pl.when(pl.program_id(0) == 0)
        def _():
            dg_ref[...] = jnp.zeros_like(dg_ref)

        dg_ref[...] += jnp.sum(dy * xh, axis=0, keepdims=True)

        @pl.when(pl.program_id(0) == nt - 1)
        def _():
            per_lane = jnp.broadcast_to(dg_ref[...], (8, w))
            dg_ref[...] = lax.dot_general(per_lane, f_ref[...], NN, precision=lax.Precision.HIGHEST,
                                          preferred_element_type=F32)[0:1]

    row = pl.BlockSpec((tm, w), lambda i: (i, 0))
    vec = pl.BlockSpec((1, w), lambda i: (0, 0))
    sq = pl.BlockSpec((w, w), lambda i: (0, 0))
    return _pcall(
        body, name=name, grid=(nt,),
        in_specs=[row] * nd + [pl.BlockSpec((tm, w), lambda i: (i, col)), vec, sq, sq],
        out_specs=(row, vec), out_shape=(_sds((t, w), F32), _sds((1, w), F32)),
        compiler_params=_cparams("arbitrary"))(*dys, x, g, _head_sum_matrix(1.0 / DIL_HD), fold)


def _rowdot(a, b, name, tm):
    n, d = a.shape
    tm = min(tm, n)

    def body(a_ref, b_ref, o_ref):
        o_ref[...] = jnp.sum(a_ref[...].astype(F32) * b_ref[...].astype(F32), axis=-1, keepdims=True)

    spec = pl.BlockSpec((tm, d), lambda i: (i, 0))
    return _pcall(body, name=name, grid=(n // tm,), in_specs=[spec, spec],
                  out_specs=pl.BlockSpec((tm, 1), lambda i: (i, 0)), out_shape=_sds((n, 1), F32),
                  compiler_params=_cparams("parallel"))(a, b)


def _add3(a, b, c, name, tm):
    n, d = a.shape
    tm = min(tm, n)

    def body(a_ref, b_ref, c_ref, o_ref):
        o_ref[...] = a_ref[...] + b_ref[...] + c_ref[...]

    spec = pl.BlockSpec((tm, d), lambda i: (i, 0))
    return _pcall(body, name=name, grid=(n // tm,), in_specs=[spec] * 3, out_specs=spec,
                  out_shape=_sds((n, d), F32), compiler_params=_cparams("parallel"))(a, b, c)


def _rope_tables(t):
    inv = ROPE_BASE ** (-np.arange(0, MLA_ROPE, 2, dtype=np.float64) / MLA_ROPE)
    ang = np.arange(t, dtype=np.float64)[:, None] * inv[None, :]
    cos, sin = np.cos(ang), np.sin(ang)
    return (jnp.asarray(np.concatenate([cos, cos], 1), F32), jnp.asarray(np.concatenate([-sin, sin], 1), F32))


def _half_swap():
    p = np.zeros((MLA_ROPE, MLA_ROPE), np.float32)
    half = MLA_ROPE // 2
    for i in range(MLA_ROPE):
        p[(i + half) % MLA_ROPE, i] = 1.0
    return jnp.asarray(p)


def _mla_qk_fwd(x, g, cos_t, sin_t, scale, name, tm):
    n, d = x.shape
    t = cos_t.shape[0]
    tm = min(tm, t)
    nt = t // tm
    swap = _half_swap()

    def body(x_ref, g_ref, c_ref, s_ref, p_ref, o_ref):
        xf = x_ref[...]
        r = lax.rsqrt(jnp.mean(xf * xf, axis=-1, keepdims=True) + EPS)
        y = xf * r * g_ref[...]
        yr = y[:, MLA_NOPE:]
        sw = lax.dot_general(yr, p_ref[...], NN, precision=lax.Precision.HIGHEST, preferred_element_type=F32)
        o_ref[:, :MLA_NOPE] = (y[:, :MLA_NOPE] * scale).astype(o_ref.dtype)
        o_ref[:, MLA_NOPE:] = ((yr * c_ref[...] + sw * s_ref[...]) * scale).astype(o_ref.dtype)

    row = pl.BlockSpec((tm, d), lambda i: (i, 0))
    tab = pl.BlockSpec((tm, MLA_ROPE), lambda i: (i % nt, 0))
    return _pcall(
        body, name=name, grid=(n // tm,),
        in_specs=[row, pl.BlockSpec((1, d), lambda i: (0, 0)), tab, tab,
                  pl.BlockSpec((MLA_ROPE, MLA_ROPE), lambda i: (0, 0))],
        out_specs=row, out_shape=_sds((n, d), BF16),
        compiler_params=_cparams("parallel"))(x, g, cos_t, sin_t, swap)


def _mla_qk_bwd(dy, x, g, cos_t, sin_t, scale, name, tm):
    n, d = x.shape
    t = cos_t.shape[0]
    tm = min(tm, t)
    nt = t // tm
    swap_t = _half_swap().T

    def body(dy_ref, x_ref, g_ref, c_ref, s_ref, p_ref, dx_ref, dg_ref):
        xf = x_ref[...]
        gg = g_ref[...]
        r = lax.rsqrt(jnp.mean(xf * xf, axis=-1, keepdims=True) + EPS)
        xh = xf * r
        dyf = dy_ref[...] * scale
        dyr = dyf[:, MLA_NOPE:]
        back = lax.dot_general(dyr * s_ref[...], p_ref[...], NN, precision=lax.Precision.HIGHEST,
                               preferred_element_type=F32)
        dn_n = dyf[:, :MLA_NOPE]
        dn_r = dyr * c_ref[...] + back
        xh_n, xh_r = xh[:, :MLA_NOPE], xh[:, MLA_NOPE:]
        dxh_n = dn_n * gg[:, :MLA_NOPE]
        dxh_r = dn_r * gg[:, MLA_NOPE:]
        mean = (jnp.sum(dxh_n * xh_n, axis=-1, keepdims=True)
                + jnp.sum(dxh_r * xh_r, axis=-1, keepdims=True)) * (1.0 / d)
        dx_ref[:, :MLA_NOPE] = r * (dxh_n - xh_n * mean)
        dx_ref[:, MLA_NOPE:] = r * (dxh_r - xh_r * mean)

        @pl.when(pl.program_id(0) == 0)
        def _():
            dg_ref[...] = jnp.zeros_like(dg_ref)

        dg_ref[:, :MLA_NOPE] += jnp.sum(dn_n * xh_n, axis=0, keepdims=True)
        dg_ref[:, MLA_NOPE:] += jnp.sum(dn_r * xh_r, axis=0, keepdims=True)

    row = pl.BlockSpec((tm, d), lambda i: (i, 0))
    vec = pl.BlockSpec((1, d), lambda i: (0, 0))
    tab = pl.BlockSpec((tm, MLA_ROPE), lambda i: (i % nt, 0))
    return _pcall(
        body, name=name, grid=(n // tm,),
        in_specs=[row, row, vec, tab, tab, pl.BlockSpec((MLA_ROPE, MLA_ROPE), lambda i: (0, 0))],
        out_specs=(row, vec), out_shape=(_sds((n, d), F32), _sds((1, d), F32)),
        compiler_params=_cparams("arbitrary"))(dy, x, g, cos_t, sin_t, swap_t)


def _causal_mask(i, j, tq, tk):
    row = i * tq + lax.broadcasted_iota(jnp.int32, (tq, tk), 0)
    col = j * tk + lax.broadcasted_iota(jnp.int32, (tq, tk), 1)
    return col <= row


def _causal_steps(nq, nk, tq, tk, q_major):
    if q_major:
        groups = [[(i, j) for j in range((i * tq + tq - 1) // tk + 1)] for i in range(nq)]
    else:
        groups = [[(i, j) for i in range((j * tk) // tq, nq)] for j in range(nk)]
    it, jt, fl = [], [], []
    for g in groups:
        for n, (i, j) in enumerate(g):
            it.append(i)
            jt.append(j)
            fl.append((n == 0) + 2 * (n == len(g) - 1) + 4 * (j * tk + tk - 1 > i * tq))
    return tuple(jnp.asarray(np.array(a, np.int32)) for a in (it, jt, fl))


def _causal_specs(tq, tk):
    def qs(w):
        return pl.BlockSpec((None, tq, w), lambda h, s, it, jt, fl: (h, it[s], 0))

    def kv(w):
        return pl.BlockSpec((None, tk, w), lambda h, s, it, jt, fl: (h, jt[s], 0))

    return qs, kv


def _mla_fwd(q, k, v, tq, tk):
    nh, t, dq = q.shape
    dv = v.shape[2]
    tq, tk = min(tq, t), min(tk, t)
    tables = _causal_steps(t // tq, t // tk, tq, tk, True)

    def body(it, jt, fl, q_ref, k_ref, v_ref, o_ref, lse_ref, m_sc, l_sc, acc_sc):
        step = pl.program_id(1)
        i, j, flags = it[step], jt[step], fl[step]

        @pl.when((flags & 1) != 0)
        def _():
            m_sc[...] = jnp.full_like(m_sc, NEG)
            l_sc[...] = jnp.zeros_like(l_sc)
            acc_sc[...] = jnp.zeros_like(acc_sc)

        def update(masked):
            s = _dot(q_ref[...], k_ref[...], NT)
            if masked:
                s = jnp.where(_causal_mask(i, j, tq, tk), s, NEG)
            m_prev = m_sc[...]
            m_new = jnp.maximum(m_prev, jnp.max(s, axis=-1, keepdims=True))
            alpha = jnp.exp(m_prev - m_new)
            p = jnp.exp(s - m_new)
            l_sc[...] = alpha * l_sc[...] + jnp.sum(p, axis=-1, keepdims=True)
            acc_sc[...] = alpha * acc_sc[...] + _dot(p.astype(BF16), v_ref[...], NN)
            m_sc[...] = m_new

        pl.when((flags & 4) != 0)(functools.partial(update, True))
        pl.when((flags & 4) == 0)(functools.partial(update, False))

        @pl.when((flags & 2) != 0)
        def _():
            o_ref[...] = acc_sc[...] / l_sc[...]
            lse_ref[...] = m_sc[...] + jnp.log(l_sc[...])

    qs, kv = _causal_specs(tq, tk)
    return _pcall(
        body, name="mla_attn_fwd",
        grid_spec=pltpu.PrefetchScalarGridSpec(
            num_scalar_prefetch=3, grid=(nh, tables[0].shape[0]),
            in_specs=[qs(dq), kv(dq), kv(dv)], out_specs=(qs(dv), qs(1)),
            scratch_shapes=[pltpu.VMEM((tq, 1), F32), pltpu.VMEM((tq, 1), F32), pltpu.VMEM((tq, dv), F32)]),
        out_shape=(_sds((nh, t, dv), F32), _sds((nh, t, 1), F32)),
        compiler_params=_cparams("parallel", "arbitrary"))(*tables, q, k, v)


def _mla_bwd_dq(q, k, v, do, lse, dl, tq, tk):
    nh, t, dq = q.shape
    dv = v.shape[2]
    tq, tk = min(tq, t), min(tk, t)
    tables = _causal_steps(t // tq, t // tk, tq, tk, True)

    def body(it, jt, fl, q_ref, k_ref, v_ref, do_ref, lse_ref, dl_ref, dq_ref, acc_sc):
        step = pl.program_id(1)
        i, j, flags = it[step], jt[step], fl[step]

        def update(masked):
            s = _dot(q_ref[...], k_ref[...], NT)
            if masked:
                s = jnp.where(_causal_mask(i, j, tq, tk), s, NEG)
            p = jnp.exp(s - lse_ref[...])
            dp = _dot(do_ref[...].astype(BF16), v_ref[...], NT)
            ds = p * (dp - dl_ref[...])
            part = _dot(ds.astype(BF16), k_ref[...], NN)

            @pl.when((flags & 1) != 0)
            def _():
                acc_sc[...] = part

            @pl.when((flags & 1) == 0)
            def _():
                acc_sc[...] += part

        pl.when((flags & 4) != 0)(functools.partial(update, True))
        pl.when((flags & 4) == 0)(functools.partial(update, False))

        @pl.when((flags & 2) != 0)
        def _():
            dq_ref[...] = acc_sc[...]

    qs, kv = _causal_specs(tq, tk)
    return _pcall(
        body, name="mla_attn_dq",
        grid_spec=pltpu.PrefetchScalarGridSpec(
            num_scalar_prefetch=3, grid=(nh, tables[0].shape[0]),
            in_specs=[qs(dq), kv(dq), kv(dv), qs(dv), qs(1), qs(1)], out_specs=qs(dq),
            scratch_shapes=[pltpu.VMEM((tq, dq), F32)]),
        out_shape=_sds((nh, t, dq), F32),
        compiler_params=_cparams("parallel", "arbitrary"))(*tables, q, k, v, do, lse, dl)


def _mla_bwd_dkv(q, k, v, do, lse_row, dl_row, tq, tk):
    nh, t, dq = q.shape
    dv = v.shape[2]
    tq, tk = min(tq, t), min(tk, t)
    tables = _causal_steps(t // tq, t // tk, tq, tk, False)

    def body(it, jt, fl, q_ref, k_ref, v_ref, do_ref, lse_ref, dl_ref, dk_ref, dv_ref, dk_sc, dv_sc):
        step = pl.program_id(1)
        i, j, flags = it[step], jt[step], fl[step]

        def update(masked):
            st = _dot(k_ref[...], q_ref[...], NT)
            if masked:
                key = j * tk + lax.broadcasted_iota(jnp.int32, (tk, tq), 0)
                qry = i * tq + lax.broadcasted_iota(jnp.int32, (tk, tq), 1)
                st = jnp.where(key <= qry, st, NEG)
            pt = jnp.exp(st - lse_ref[...])
            dob = do_ref[...].astype(BF16)
            dpt = _dot(v_ref[...], dob, NT)
            dst = pt * (dpt - dl_ref[...])
            dv_part = _dot(pt.astype(BF16), dob, NN)
            dk_part = _dot(dst.astype(BF16), q_ref[...], NN)

            @pl.when((flags & 1) != 0)
            def _():
                dv_sc[...] = dv_part
                dk_sc[...] = dk_part

            @pl.when((flags & 1) == 0)
            def _():
                dv_sc[...] += dv_part
                dk_sc[...] += dk_part

        pl.when((flags & 4) != 0)(functools.partial(update, True))
        pl.when((flags & 4) == 0)(functools.partial(update, False))

        @pl.when((flags & 2) != 0)
        def _():
            dk_ref[...] = dk_sc[...]
            dv_ref[...] = dv_sc[...]

    qs, kv = _causal_specs(tq, tk)
    rowv = pl.BlockSpec((None, 1, tq), lambda h, s, it, jt, fl: (h, 0, it[s]))
    return _pcall(
        body, name="mla_attn_dkv",
        grid_spec=pltpu.PrefetchScalarGridSpec(
            num_scalar_prefetch=3, grid=(nh, tables[0].shape[0]),
            in_specs=[qs(dq), kv(dq), kv(dv), qs(dv), rowv, rowv], out_specs=(kv(dq), kv(dv)),
            scratch_shapes=[pltpu.VMEM((tk, dq), F32), pltpu.VMEM((tk, dv), F32)]),
        out_shape=(_sds((nh, t, dq), F32), _sds((nh, t, dv), F32)),
        compiler_params=_cparams("parallel", "arbitrary"))(*tables, q, k, v, do, lse_row, dl_row)


def _loss_head(y, target, tm):
    t, d = y.shape
    tm = min(tm, t)
    nt = t // tm

    def body(y_ref, t_ref, dy_ref, loss_ref, acc):
        i = pl.program_id(0)
        err = y_ref[...] - t_ref[...]
        dy_ref[...] = err * (1.0 / d)

        @pl.when(i == 0)
        def _():
            acc[...] = jnp.zeros_like(acc)

        acc[...] += jnp.sum(err * err, axis=0, keepdims=True)

        @pl.when(i == nt - 1)
        def _():
            loss_ref[0, 0] = jnp.sum(acc[...]) * (0.5 / d)

    spec = pl.BlockSpec((tm, d), lambda i: (i, 0))
    return _pcall(
        body, name="loss_head", grid=(nt,), in_specs=[spec, spec],
        out_specs=(spec, pl.BlockSpec(memory_space=pltpu.SMEM)),
        out_shape=(_sds((t, d), F32), _sds((1, 1), F32)),
        scratch_shapes=[pltpu.VMEM((1, d), F32)],
        compiler_params=_cparams("arbitrary"))(y, target)


def _adamw(w, g, m, v, name):
    r, c = w.shape
    tr = r
    for cand in (256, 128, 64, 32, 16, 8):
        if r % cand == 0:
            tr = cand
            break

    def body(w_ref, g_ref, m_ref, v_ref, d_ref, nm_ref, nv_ref):
        gg = g_ref[...]
        nm = ADAM_B1 * m_ref[...] + (1.0 - ADAM_B1) * gg
        nv = ADAM_B2 * v_ref[...] + (1.0 - ADAM_B2) * (gg * gg)
        m_hat = nm / (1.0 - ADAM_B1 ** ADAM_STEP)
        v_hat = nv / (1.0 - ADAM_B2 ** ADAM_STEP)
        d_ref[...] = -ADAM_LR * (m_hat / (jnp.sqrt(v_hat) + ADAM_EPS) + ADAM_WD * w_ref[...])
        nm_ref[...] = nm
        nv_ref[...] = nv

    spec = pl.BlockSpec((tr, c), lambda i: (i, 0))
    sd = _sds((r, c), F32)
    return _pcall(body, name=name, grid=(r // tr,), in_specs=[spec] * 4, out_specs=(spec,) * 3,
                  out_shape=(sd, sd, sd), compiler_params=_cparams("parallel"))(w, g, m, v)


MESH_ID = pl.DeviceIdType.MESH
HBM_SPEC = pl.BlockSpec(memory_space=pltpu.HBM)


def _place():
    return lax.axis_index("x"), lax.axis_index("y"), lax.axis_index("c")


def _other_chips(x, y):
    return [(1 - x, y), (x, 1 - y), (1 - x, 1 - y)]


def _remote(src, dst, send_sems, recv_sems, k, to):
    return pltpu.make_async_remote_copy(src_ref=src, dst_ref=dst, send_sem=send_sems.at[k], recv_sem=recv_sems.at[k],
                                        device_id=to, device_id_type=MESH_ID)


D2D_SPLIT = 16
ICI_SPLIT = 4


def _chunks(rows, n):
    assert rows % n == 0
    return [(i * (rows // n), rows // n) for i in range(n)]


def _gather_weights(packed):
    rows, lanes = packed.shape
    half = rows // 2

    def body(src, out, send_sems, recv_sems):
        x, y, c = _place()
        me = 2 * x + y
        sibling = (x, y, 1 - c)
        chips = _other_chips(x, y)

        def part(chip, core, lo=0, n=half):
            return out.at[chip, pl.ds(core * half + lo, n), :]

        for k, (cx, cy) in enumerate(chips):
            for lo, n in _chunks(half, ICI_SPLIT):
                _remote(src.at[pl.ds(c * half + lo, n), :], part(me, c, lo, n), send_sems, recv_sems, k,
                        (cx, cy, c)).start()
        for k, (cx, cy) in enumerate(chips):
            got = part(2 * cx + cy, c)
            _remote(got, got, send_sems, recv_sems, k, (x, y, c)).wait_recv()
            for lo, n in _chunks(half, D2D_SPLIT):
                piece = part(2 * cx + cy, c, lo, n)
                _remote(piece, piece, send_sems, recv_sems, 3 + k, sibling).start()
        for k, (cx, cy) in enumerate(chips):
            got = part(2 * cx + cy, 1 - c)
            _remote(got, got, send_sems, recv_sems, 3 + k, (x, y, c)).wait_recv()
        for k in range(6):
            sent = part(me, c)
            _remote(sent, sent, send_sems, recv_sems, k, (x, y, c)).wait_send()

    return _pcall(
        body, name="gather_weights", in_specs=[HBM_SPEC], out_specs=HBM_SPEC,
        out_shape=_sds((N_CHIPS, rows, lanes), packed.dtype),
        scratch_shapes=[pltpu.SemaphoreType.DMA((6,)), pltpu.SemaphoreType.DMA((6,))],
    )(packed)


def _reduce_cores(grads):
    nchip, rows, lanes = grads.shape
    half = rows // 2

    def body(g, theirs, send_sems, recv_sems):
        x, y, c = _place()
        for j in range(nchip):
            for lo, n in _chunks(half, D2D_SPLIT):
                _remote(g.at[j, pl.ds((1 - c) * half + lo, n), :], theirs.at[j, pl.ds(lo, n), :],
                        send_sems, recv_sems, 0, (x, y, 1 - c)).start()
        _remote(g.at[:, pl.ds((1 - c) * half, half), :], theirs, send_sems, recv_sems, 0, (x, y, c)).wait()

    return _pcall(
        body, name="reduce_cores", in_specs=[HBM_SPEC], out_specs=HBM_SPEC,
        out_shape=_sds((nchip, half, lanes), grads.dtype),
        scratch_shapes=[pltpu.SemaphoreType.DMA((1,)), pltpu.SemaphoreType.DMA((1,))],
    )(grads)


def _scatter_chips(part):
    nchip, half, lanes = part.shape

    def body(p, out, send_sems, recv_sems):
        x, y, c = _place()
        for k, (cx, cy) in enumerate(_other_chips(x, y)):
            for lo, n in _chunks(half, ICI_SPLIT):
                _remote(p.at[2 * cx + cy, pl.ds(lo, n), :], out.at[k, pl.ds(lo, n), :],
                        send_sems, recv_sems, k, (cx, cy, c)).start()
        for k in range(3):
            _remote(p.at[k], out.at[k], send_sems, recv_sems, k, (x, y, c)).wait()

    return _pcall(
        body, name="scatter_chips", in_specs=[HBM_SPEC], out_specs=HBM_SPEC,
        out_shape=_sds((3, half, lanes), part.dtype),
        scratch_shapes=[pltpu.SemaphoreType.DMA((3,)), pltpu.SemaphoreType.DMA((3,))],
    )(part)


def _sum_partials(received, part, place, tm):
    _, half, lanes = received.shape
    tm = min(tm, half)
    nblk = half // tm

    def body(place_ref, r_ref, p_ref, o_ref):
        tot = p_ref[...].astype(F32)
        for k in range(3):
            tot = tot + r_ref[k].astype(F32)
        o_ref[...] = tot

    return _pcall(
        body, name="sum_chip_partials",
        grid_spec=pltpu.PrefetchScalarGridSpec(
            num_scalar_prefetch=1, grid=(nblk,),
            in_specs=[pl.BlockSpec((3, tm, lanes), lambda i, pc: (0, i, 0)),
                      pl.BlockSpec((None, tm, lanes), lambda i, pc: (pc[0], i, 0))],
            out_specs=pl.BlockSpec((tm, lanes), lambda i, pc: (pc[1] * nblk + i, 0))),
        out_shape=_sds((2 * half, lanes), F32),
        compiler_params=_cparams("parallel"))(place, received, part)


def _share_cores(block):
    rows, lanes = block.shape
    half = rows // 2

    def body(src, out, send_sems, recv_sems):
        x, y, c = _place()
        for lo, n in _chunks(half, D2D_SPLIT):
            piece = pl.ds(c * half + lo, n)
            _remote(src.at[piece, :], out.at[piece, :], send_sems, recv_sems, 0, (x, y, 1 - c)).start()
        mine = out.at[pl.ds(c * half, half), :]
        theirs = out.at[pl.ds((1 - c) * half, half), :]
        _remote(mine, theirs, send_sems, recv_sems, 0, (x, y, c)).wait()

    return _pcall(
        body, name="share_cores", in_specs=[HBM_SPEC], out_specs=HBM_SPEC,
        out_shape=_sds((rows, lanes), block.dtype), input_output_aliases={0: 0},
        scratch_shapes=[pltpu.SemaphoreType.DMA((1,)), pltpu.SemaphoreType.DMA((1,))],
    )(block)


def _sum_blocks(stacked, name, tm):
    n, rows, lanes = stacked.shape
    tm = min(tm, rows)

    def body(s_ref, o_ref):
        tot = s_ref[n - 1].astype(F32)
        for k in range(n - 1):
            tot = tot + s_ref[k].astype(F32)
        o_ref[...] = tot

    return _pcall(body, name=name, grid=(rows // tm,),
                  in_specs=[pl.BlockSpec((n, tm, lanes), lambda i: (0, i, 0))],
                  out_specs=pl.BlockSpec((tm, lanes), lambda i: (i, 0)), out_shape=_sds((rows, lanes), F32),
                  compiler_params=_cparams("parallel"))(stacked)


def _add_halves(grads, theirs, core, tm):
    n, half, lanes = theirs.shape
    tm = min(tm, half)
    nblk = half // tm

    def body(c_ref, g_ref, t_ref, o_ref):
        o_ref[...] = (g_ref[...] + t_ref[...]).astype(o_ref.dtype)

    spec = pl.BlockSpec((None, tm, lanes), lambda k, i, c: (k, i, 0))
    return _pcall(
        body, name="add_core_halves",
        grid_spec=pltpu.PrefetchScalarGridSpec(
            num_scalar_prefetch=1, grid=(n, nblk),
            in_specs=[pl.BlockSpec((None, tm, lanes), lambda k, i, c: (k, c[0] * nblk + i, 0)), spec], out_specs=spec),
        out_shape=_sds((n, half, lanes), BF16),
        compiler_params=_cparams("parallel", "parallel"))(core, grads, theirs)


def _allreduce_small(part):
    rows, lanes = part.shape
    ndev = 8

    def body(src, tot, buf, send_sems, recv_sems):
        x, y, c = _place()
        me = 4 * x + 2 * y + c
        buf[me] = src[...]
        sends = []
        for k in range(1, ndev):
            peer = (x ^ (k >> 2), y ^ ((k >> 1) & 1), c ^ (k & 1))
            cp = _remote(src, buf.at[me], send_sems, recv_sems, k - 1, peer)
            cp.start()
            sends.append(cp)
        for k in range(1, ndev):
            theirs = buf.at[me ^ k]
            _remote(theirs, theirs, send_sems, recv_sems, k - 1, (x, y, c)).wait_recv()
        for cp in sends:
            cp.wait_send()
        acc = buf[0]
        for d in range(1, ndev):
            acc = acc + buf[d]
        tot[...] = acc

    vm = pl.BlockSpec(memory_space=pltpu.VMEM)
    return _pcall(
        body, name="allreduce_small", in_specs=[vm], out_specs=vm, out_shape=_sds((rows, lanes), F32),
        scratch_shapes=[pltpu.VMEM((ndev, rows, lanes), F32), pltpu.SemaphoreType.DMA((ndev - 1,)),
                        pltpu.SemaphoreType.DMA((ndev - 1,))],
    )(part)


def _big_rows():
    return [int(np.prod(shape)) // LANES for _, shape in BIG]


def _pack_big(blocks, dtype):
    parts = [blocks[name].reshape(blocks[name].shape[0], -1, LANES).astype(dtype) for name, _ in BIG]
    return jnp.concatenate(parts, axis=1)


def _unpack_big(packed):
    out, off = {}, 0
    for (name, shape), r in zip(BIG, _big_rows()):
        out[name] = packed[:, off:off + r].reshape((packed.shape[0],) + shape)
        off += r
    return out


def _pack_small(vals):
    parts = []
    for name, shape, r in SMALL:
        flat = vals[name].reshape(-1).astype(F32)
        parts.append(jnp.pad(flat, (0, r * LANES - flat.shape[0])).reshape(r, LANES))
    used = sum(r for _, _, r in SMALL)
    parts.append(jnp.zeros((SMALL_ROWS - used, LANES), F32))
    return jnp.concatenate(parts, axis=0)


def _unpack_small(packed):
    out, off = {}, 0
    for name, shape, r in SMALL:
        n = int(np.prod(shape))
        out[name] = packed[off:off + r].reshape(-1)[:n].reshape(shape)
        off += r
    return out


def _heads_major(a, nh):
    t = a.shape[0]
    return a.reshape(t, nh, a.shape[1] // nh).transpose(1, 0, 2)


def _tokens_major(a):
    nh, t, w = a.shape
    return a.transpose(1, 0, 2).reshape(t, nh * w)


def _local_step(x, target, small, wfull):
    t = x.shape[0]
    nh, hd = DIL_HEADS, DIL_HD
    w_in = wfull["w_in"].transpose(1, 0, 2).reshape(D_MODEL, -1)
    w_out = wfull["w_out"].reshape(D_MODEL, D_MODEL)
    w_qb, w_kvb = wfull["mla_w_q_b"], wfull["mla_w_kv_b"]
    grads_s, grads_b = {}, {}

    x1, ffn1_saved = _ffn_fwd(x, small["ffn1_norm"], wfull["ffn1_w_gate"], wfull["ffn1_w_up"],
                              wfull["ffn1_w_down"], "ffn1")
    hm = _rms_fwd(x1, small["mix_norm"], BF16, "mix_norm", 512)
    proj = _mm_simple("in_proj", hm, w_in, NN, F32)
    cq, ckv, k_pe = proj[:, 1536:1792], proj[:, 1792:1920], proj[:, 1920:1984]

    gq, gk = jnp.tile(small["dil_q_norm"], (1, nh)), jnp.tile(small["dil_k_norm"], (1, nh))
    qn = _head_norm_fwd(proj, 0, gq, "dil_q_norm", 512)
    kn = _head_norm_fwd(proj, 1, gk, "dil_k_norm", 512)
    v_d = proj[:, 2 * DIL_WIDTH:3 * DIL_WIDTH].astype(BF16)
    bias = _bias_tiles(small["rel_bias"]).reshape(3, nh // 2, 2 * QB, QB + DIL_W)
    outs, lses = [], []
    for b, dil in enumerate(DIL_DILATIONS):
        o_b, lse_b = _dil_fwd(qn, kn, v_d, bias[b], dil, f"dil_fwd_{dil}")
        outs.append(o_b)
        lses.append(lse_b)
    o_dil, lse_tot, od = _dil_merge(outs, lses, small["out_norm_dil"], 512)

    mh = MLA_HEADS
    cos_t, sin_t = _rope_tables(t)
    cqn = _rms_fwd(cq, small["mla_q_a_norm"], BF16, "mla_q_a_norm", 512)
    ckvn = _rms_fwd(ckv, small["mla_kv_a_norm"], BF16, "mla_kv_a_norm", 512)
    tm = min(512, t)

    def head_proj(name, a, w, width):
        k = a.shape[1]
        return _mm(name, (mh, t // tm, 1),
                   [(a, pl.BlockSpec((tm, k), lambda h, i, r: (i, 0)), w, pl.BlockSpec((None, k, width), lambda h, i, r: (h, 0, 0)))],
                   NN, _sds((mh, t, width), F32), pl.BlockSpec((None, tm, width), lambda h, i, r: (h, i, 0)), (tm, width))

    q_raw = head_proj("mla_q_proj", cqn, w_qb, MLA_QK)
    kv_raw = head_proj("mla_kv_proj", ckvn, w_kvb, MLA_NOPE + MLA_V)
    k_raw = jnp.concatenate([kv_raw[:, :, :MLA_NOPE], jnp.broadcast_to(k_pe[None], (mh, t, MLA_ROPE))], axis=2)
    v_m = kv_raw[:, :, MLA_NOPE:].astype(BF16)
    q_raw2, k_raw2 = q_raw.reshape(mh * t, MLA_QK), k_raw.reshape(mh * t, MLA_QK)
    q_scale = MLA_QK ** -0.5
    q_m = _mla_qk_fwd(q_raw2, small["mla_q_norm"], cos_t, sin_t, q_scale, "mla_q_rope", 512).reshape(mh, t, MLA_QK)
    k_m = _mla_qk_fwd(k_raw2, small["mla_k_norm"], cos_t, sin_t, 1.0, "mla_k_rope", 512).reshape(mh, t, MLA_QK)
    o_mla_h, lse_m = _mla_fwd(q_m, k_m, v_m, 512, 2048)
    o_mla = _tokens_major(o_mla_h)

    om = _rms_fwd(o_mla, small["out_norm_mla"], BF16, "out_norm_mla", 512)
    half_w = DIL_WIDTH
    row = pl.BlockSpec((tm, D_MODEL), lambda i, j, r: (i, 0))
    act_spec = pl.BlockSpec((tm, half_w), lambda i, j, r: (i, 0))
    x2 = _mm("out_proj", (t // tm, 1, 1),
             [(od, act_spec, w_out, pl.BlockSpec((half_w, D_MODEL), lambda i, j, r: (0, 0))),
              (om, act_spec, w_out, pl.BlockSpec((half_w, D_MODEL), lambda i, j, r: (1, 0)))],
             NN, _sds((t, D_MODEL), F32), row, (tm, D_MODEL), res=(x1, row))
    x3, ffn2_saved = _ffn_fwd(x2, small["ffn2_norm"], wfull["ffn2_w_gate"], wfull["ffn2_w_up"],
                              wfull["ffn2_w_down"], "ffn2")
    dy, loss = _loss_head(x3, target, 512)

    dx2, grads_s["ffn2_norm"], grads_b["ffn2_w_gate"], grads_b["ffn2_w_up"], grads_b["ffn2_w_down"] = _ffn_bwd(
        dy, x2, small["ffn2_norm"], wfull["ffn2_w_gate"], wfull["ffn2_w_up"], wfull["ffn2_w_down"], ffn2_saved, "ffn2")

    d_ocat = _mm_simple("out_proj_dx", dx2, w_out, NT, F32)
    tk = min(512, t)
    tok = pl.BlockSpec((tk, half_w), lambda c, j, r: (r, 0))
    dw_out_d = _mm_simple("out_proj_dw_dil", od, dx2, TN, F32)
    dw_out_m = _mm_simple("out_proj_dw_mla", om, dx2, TN, F32)
    grads_b["w_out"] = jnp.concatenate([dw_out_d, dw_out_m], axis=0).reshape(N_CHIPS, D_MODEL // N_CHIPS, D_MODEL)
    do_dil, grads_s["out_norm_dil"] = _rms_bwd([d_ocat[:, :half_w]], o_dil, small["out_norm_dil"], None, "out_norm_dil_bwd", 512)
    do_mla, grads_s["out_norm_mla"] = _rms_bwd([d_ocat[:, half_w:]], o_mla, small["out_norm_mla"], None, "out_norm_mla_bwd", 512)

    do_m = _heads_major(do_mla, mh)
    dl_m = _rowdot(do_m.reshape(mh * t, MLA_V), o_mla_h.reshape(mh * t, MLA_V), "mla_delta", 2048).reshape(mh, t, 1)
    dq_m = _mla_bwd_dq(q_m, k_m, v_m, do_m, lse_m, dl_m, 512, 2048)
    dk_m, dv_m = _mla_bwd_dkv(q_m, k_m, v_m, do_m, lse_m.reshape(mh, 1, t), dl_m.reshape(mh, 1, t), 2048, 512)
    dq_raw, grads_s["mla_q_norm"] = _mla_qk_bwd(dq_m.reshape(mh * t, MLA_QK), q_raw2, small["mla_q_norm"],
                                                 cos_t, sin_t, q_scale, "mla_q_rope_bwd", 512)
    dk_raw, grads_s["mla_k_norm"] = _mla_qk_bwd(dk_m.reshape(mh * t, MLA_QK), k_raw2, small["mla_k_norm"],
                                                 cos_t, sin_t, 1.0, "mla_k_rope_bwd", 512)
    dq_raw = dq_raw.reshape(mh, t, MLA_QK)
    dk_raw = dk_raw.reshape(mh, t, MLA_QK)
    dkv_raw = jnp.concatenate([dk_raw[:, :, :MLA_NOPE], dv_m], axis=2)
    dk_pe_h = dk_raw[:, :, MLA_NOPE:]

    def head_proj_dx(name, d, w):
        width, k = d.shape[2], w.shape[1]
        return _mm(name, (t // tm, 1, mh),
                   [(d, pl.BlockSpec((None, tm, width), lambda i, j, r: (r, i, 0)), w, pl.BlockSpec((None, k, width), lambda i, j, r: (r, 0, 0)))],
                   NT, _sds((t, k), F32), pl.BlockSpec((tm, k), lambda i, j, r: (i, 0)), (tm, k))

    def head_proj_dw(name, a, d):
        width, k = d.shape[2], a.shape[1]
        return _mm(name, (mh, 1, t // tk),
                   [(a, pl.BlockSpec((tk, k), lambda h, j, r: (r, 0)), d, pl.BlockSpec((None, tk, width), lambda h, j, r: (h, r, 0)))],
                   TN, _sds((mh, k, width), F32), pl.BlockSpec((None, k, width), lambda h, j, r: (h, 0, 0)), (k, width))

    d_cqn = head_proj_dx("mla_q_proj_dx", dq_raw, w_qb)
    d_ckvn = head_proj_dx("mla_kv_proj_dx", dkv_raw, w_kvb)
    grads_b["mla_w_q_b"] = head_proj_dw("mla_q_proj_dw", cqn, dq_raw)
    grads_b["mla_w_kv_b"] = head_proj_dw("mla_kv_proj_dw", ckvn, dkv_raw)
    d_cq, grads_s["mla_q_a_norm"] = _rms_bwd([d_cqn], cq, small["mla_q_a_norm"], None, "mla_q_a_norm_bwd", 512)
    d_ckv, grads_s["mla_kv_a_norm"] = _rms_bwd([d_ckvn], ckv, small["mla_kv_a_norm"], None, "mla_kv_a_norm_bwd", 512)
    d_kpe = _sum_blocks(dk_pe_h.reshape(mh, t * MLA_ROPE // LANES, LANES), "mla_kpe_sum", 1024).reshape(t, MLA_ROPE)

    dl_d = _head_rowdot(do_dil, o_dil, 512)
    dqs, dks, dvs, dtiles = [], [], [], []
    for b, dil in enumerate(DIL_DILATIONS):
        dq_b, dk_b, dv_b, db_b = _dil_bwd(qn, kn, v_d, do_dil, lse_tot, dl_d, bias[b], dil, f"dil_bwd_{dil}")
        dqs.append(dq_b)
        dks.append(dk_b)
        dvs.append(dv_b)
        dtiles.append(db_b)
    grads_s["rel_bias"] = _bias_grad(jnp.stack(dtiles).reshape(3, nh, QB, QB + DIL_W))
    dq_a, dgq = _head_norm_bwd(dqs, proj, 0, gq, "dil_q_norm_bwd", 512)
    dk_a, dgk = _head_norm_bwd(dks, proj, 1, gk, "dil_k_norm_bwd", 512)
    grads_s["dil_q_norm"], grads_s["dil_k_norm"] = dgq[:, :hd], dgk[:, :hd]
    dv_a = _add3(dvs[0], dvs[1], dvs[2], "dil_dv_sum", 512)
    dproj = jnp.concatenate([dq_a, dk_a, dv_a, d_cq, d_ckv, d_kpe], axis=1)

    d_hm = _mm_simple("in_proj_dx", dproj, w_in, NT, F32)
    dw_in = _mm_simple("in_proj_dw", hm, dproj, TN, F32)
    grads_b["w_in"] = dw_in.reshape(D_MODEL, N_CHIPS, -1).transpose(1, 0, 2)
    dx1, grads_s["mix_norm"] = _rms_bwd([d_hm], x1, small["mix_norm"], dx2, "mix_norm_bwd", 512)
    dx, grads_s["ffn1_norm"], grads_b["ffn1_w_gate"], grads_b["ffn1_w_up"], grads_b["ffn1_w_down"] = _ffn_bwd(
        dx1, x, small["ffn1_norm"], wfull["ffn1_w_gate"], wfull["ffn1_w_up"], wfull["ffn1_w_down"], ffn1_saved, "ffn1")
    return loss, dx, grads_s, grads_b


def kernel(x, ffn1_norm, ffn1_w_gate, ffn1_w_up, ffn1_w_down, mix_norm, w_in, dil_q_norm, dil_k_norm, rel_bias, mla_q_a_norm, mla_w_q_b, mla_kv_a_norm, mla_w_kv_b, mla_q_norm, mla_k_norm, out_norm_dil, out_norm_mla, w_out, ffn2_norm, ffn2_w_gate, ffn2_w_up, ffn2_w_down, loss_target, m_ffn1_norm, m_ffn1_w_gate, m_ffn1_w_up, m_ffn1_w_down, m_mix_norm, m_w_in, m_dil_q_norm, m_dil_k_norm, m_rel_bias, m_mla_q_a_norm, m_mla_w_q_b, m_mla_kv_a_norm, m_mla_w_kv_b, m_mla_q_norm, m_mla_k_norm, m_out_norm_dil, m_out_norm_mla, m_w_out, m_ffn2_norm, m_ffn2_w_gate, m_ffn2_w_up, m_ffn2_w_down, v_ffn1_norm, v_ffn1_w_gate, v_ffn1_w_up, v_ffn1_w_down, v_mix_norm, v_w_in, v_dil_q_norm, v_dil_k_norm, v_rel_bias, v_mla_q_a_norm, v_mla_w_q_b, v_mla_kv_a_norm, v_mla_w_kv_b, v_mla_q_norm, v_mla_k_norm, v_out_norm_dil, v_out_norm_mla, v_w_out, v_ffn2_norm, v_ffn2_w_gate, v_ffn2_w_up, v_ffn2_w_down):
    given = dict(locals())
    big_names = [name for name, _ in BIG]
    small_names = [name for name, _, _ in SMALL]

    chip = (2 * lax.axis_index("x") + lax.axis_index("y")).astype(jnp.int32)
    core = lax.axis_index("c").astype(jnp.int32)
    mine = _pack_big({n: given[n] for n in big_names}, BF16)
    gathered = lax.dynamic_update_slice(_gather_weights(mine[0]), mine, (chip, 0, 0))
    wfull = _unpack_big(gathered)
    small = {n: given[n] for n in small_names}

    loss, dx, grads_s, grads_b = _local_step(x[0], loss_target[0], small, wfull)
    loss = lax.psum(loss[0, 0], ("x", "y", "c"))

    packed = _pack_big(grads_b, F32).reshape(N_CHIPS, -1, LANES)
    chip_part = _add_halves(packed, _reduce_cores(packed), core.reshape(1), 1264)
    reduced = _sum_partials(_scatter_chips(chip_part), chip_part, jnp.stack([chip, core]), 1264)
    g_big = _unpack_big(_share_cores(reduced)[None])
    g_small = _unpack_small(_allreduce_small(_pack_small(grads_s)))

    grad, delta, new_m, new_v = {}, {}, {}, {}
    for name, shape in BIG:
        g2 = g_big[name].reshape(shape)
        d_, m_, v_ = _adamw(given[name].reshape(shape), g2, given["m_" + name].reshape(shape),
                            given["v_" + name].reshape(shape), f"adamw_{name}")
        full = given[name].shape
        grad[name], delta[name], new_m[name], new_v[name] = (a.reshape(full) for a in (g2, d_, m_, v_))
    ps = {k: _pack_small({n: given[pre + n] for n in small_names}) for k, pre in (("w", ""), ("m", "m_"), ("v", "v_"))}
    gs_packed = _pack_small(g_small)
    d_s, m_s, v_s = (_unpack_small(a) for a in _adamw(ps["w"], gs_packed, ps["m"], ps["v"], "adamw_small"))
    for name in small_names:
        grad[name], delta[name], new_m[name], new_v[name] = g_small[name], d_s[name], m_s[name], v_s[name]

    return (loss, dx[None], *[grad[n] for n in WEIGHTS], *[delta[n] for n in WEIGHTS],
            *[new_m[n] for n in WEIGHTS], *[new_v[n] for n in WEIGHTS])
```

```python
import functools

import numpy as np
import jax
import jax.numpy as jnp
from jax import lax
from jax.experimental import pallas as pl
from jax.experimental.pallas import tpu as pltpu

F32 = jnp.float32
BF16 = jnp.bfloat16

D_MODEL = 1024
D_FF = 2816
N_CHIPS = 4
DIL_HEADS = 8
DIL_HD = 64
DIL_WIDTH = 512
DIL_DILATIONS = (1, 4, 16)
DIL_W = 128
QB = 128
MLA_HEADS = 4
MLA_NOPE = 128
MLA_ROPE = 64
MLA_QK = 192
MLA_V = 128
MLA_Q_RANK = 256
MLA_KV_RANK = 128
ROPE_BASE = 10000.0
REL_BUCKETS = 32
REL_MAX_DIST = 2048
FFN_RESID = 0.5
EPS = 1e-6
NEG = -1e30
LANES = 128

ADAM_LR = 0.001
ADAM_B1 = 0.9
ADAM_B2 = 0.999
ADAM_EPS = 1e-08
ADAM_WD = 0.01
ADAM_STEP = 10

NT = (((1,), (1,)), ((), ()))
NN = (((1,), (0,)), ((), ()))
TN = (((0,), (0,)), ((), ()))

BIG = (
    ("ffn1_w_gate", (D_MODEL, D_FF // N_CHIPS)),
    ("ffn1_w_up", (D_MODEL, D_FF // N_CHIPS)),
    ("ffn1_w_down", (D_FF // N_CHIPS, D_MODEL)),
    ("w_in", (D_MODEL, 1984 // N_CHIPS)),
    ("mla_w_q_b", (MLA_Q_RANK, MLA_QK)),
    ("mla_w_kv_b", (MLA_KV_RANK, MLA_NOPE + MLA_V)),
    ("w_out", (D_MODEL // N_CHIPS, D_MODEL)),
    ("ffn2_w_gate", (D_MODEL, D_FF // N_CHIPS)),
    ("ffn2_w_up", (D_MODEL, D_FF // N_CHIPS)),
    ("ffn2_w_down", (D_FF // N_CHIPS, D_MODEL)),
)
SMALL = (
    ("ffn1_norm", (1, 1024), 8), ("mix_norm", (1, 1024), 8), ("dil_q_norm", (1, 64), 1),
    ("dil_k_norm", (1, 64), 1), ("rel_bias", (8, 32), 2), ("mla_q_a_norm", (1, 256), 2),
    ("mla_kv_a_norm", (1, 128), 1), ("mla_q_norm", (1, 192), 2), ("mla_k_norm", (1, 192), 2),
    ("out_norm_dil", (1, 512), 4), ("out_norm_mla", (1, 512), 4), ("ffn2_norm", (1, 1024), 8),
)
SMALL_ROWS = 48
WEIGHTS = ("ffn1_norm", "ffn1_w_gate", "ffn1_w_up", "ffn1_w_down", "mix_norm", "w_in", "dil_q_norm",
           "dil_k_norm", "rel_bias", "mla_q_a_norm", "mla_w_q_b", "mla_kv_a_norm", "mla_w_kv_b",
           "mla_q_norm", "mla_k_norm", "out_norm_dil", "out_norm_mla", "w_out", "ffn2_norm",
           "ffn2_w_gate", "ffn2_w_up", "ffn2_w_down")


def _pcall(body, **kw):
    return pl.pallas_call(body, **kw)


def _cparams(*sem):
    return pltpu.CompilerParams(dimension_semantics=sem)


def _sds(shape, dtype):
    return jax.ShapeDtypeStruct(shape, dtype)


def _dot(a, b, dn):
    return lax.dot_general(a, b, dn, preferred_element_type=F32)


def _rms_fwd(x, g, out_dtype, name, tm):
    n, d = x.shape
    tm = min(tm, n)

    def body(x_ref, g_ref, o_ref):
        xf = x_ref[...].astype(F32)
        r = lax.rsqrt(jnp.mean(xf * xf, axis=-1, keepdims=True) + EPS)
        o_ref[...] = (xf * r * g_ref[...]).astype(o_ref.dtype)

    return _pcall(
        body, name=name, grid=(n // tm,),
        in_specs=[pl.BlockSpec((tm, d), lambda i: (i, 0)), pl.BlockSpec((1, d), lambda i: (0, 0))],
        out_specs=pl.BlockSpec((tm, d), lambda i: (i, 0)),
        out_shape=_sds((n, d), out_dtype), compiler_params=_cparams("parallel"))(x, g)


def _rms_bwd(dys, x, g, res, name, tm):
    n, d = x.shape
    tm = min(tm, n)
    nd = len(dys)
    has_res = res is not None

    def body(*refs):
        dy_refs = refs[:nd]
        x_ref, g_ref = refs[nd], refs[nd + 1]
        res_ref = refs[nd + 2] if has_res else None
        dx_ref, dg_ref = refs[-2], refs[-1]
        dy = dy_refs[0][...].astype(F32)
        for r_ in dy_refs[1:]:
            dy = dy + r_[...].astype(F32)
        xf = x_ref[...].astype(F32)
        r = lax.rsqrt(jnp.mean(xf * xf, axis=-1, keepdims=True) + EPS)
        xh = xf * r
        dxh = dy * g_ref[...]
        dx = r * (dxh - xh * jnp.mean(dxh * xh, axis=-1, keepdims=True))
        if has_res:
            dx = dx + res_ref[...]
        dx_ref[...] = dx

        @pl.when(pl.program_id(0) == 0)
        def _():
            dg_ref[...] = jnp.zeros_like(dg_ref)

        dg_ref[...] += jnp.sum(dy * xh, axis=0, keepdims=True)

    row = pl.BlockSpec((tm, d), lambda i: (i, 0))
    vec = pl.BlockSpec((1, d), lambda i: (0, 0))
    ins = list(dys) + [x, g] + ([res] if has_res else [])
    return _pcall(
        body, name=name, grid=(n // tm,),
        in_specs=[row] * nd + [row, vec] + ([row] if has_res else []),
        out_specs=(row, vec),
        out_shape=(_sds((n, d), F32), _sds((1, d), F32)),
        compiler_params=_cparams("arbitrary"))(*ins)


def _mm(name, grid, pairs, dn, out_shape, out_spec, acc_shape, res=None, scale=1.0):
    npairs = len(pairs)
    nred = grid[2]
    has_res = res is not None

    def body(*refs):
        ab = refs[:2 * npairs]
        res_ref = refs[2 * npairs] if has_res else None
        o_ref = refs[2 * npairs + int(has_res)]
        acc_ref = refs[-1] if nred > 1 else None
        tot = None
        for p in range(npairs):
            d = _dot(ab[2 * p][...].astype(BF16), ab[2 * p + 1][...].astype(BF16), dn)
            tot = d if tot is None else tot + d

        def finish(v):
            if scale != 1.0:
                v = v * scale
            if has_res:
                v = res_ref[...] + v
            o_ref[...] = v.astype(o_ref.dtype)

        if nred == 1:
            finish(tot)
        else:
            r = pl.program_id(2)

            @pl.when(r == 0)
            def _():
                acc_ref[...] = tot

            @pl.when(r > 0)
            def _():
                acc_ref[...] += tot

            @pl.when(r == nred - 1)
            def _():
                finish(acc_ref[...])

    ins, specs = [], []
    for a, a_spec, b, b_spec in pairs:
        ins += [a, b]
        specs += [a_spec, b_spec]
    if has_res:
        ins.append(res[0])
        specs.append(res[1])
    return _pcall(
        body, name=name, grid=grid, in_specs=specs, out_specs=out_spec, out_shape=out_shape,
        scratch_shapes=[pltpu.VMEM(acc_shape, F32)] if nred > 1 else [],
        compiler_params=_cparams("parallel", "parallel", "arbitrary"))(*ins)


def _ffn_up(h, wg, wu, name, tm):
    t, d = h.shape
    nc, _, fs = wg.shape
    tm = min(tm, t)

    def body(h_ref, wg_ref, wu_ref, g_ref, u_ref, a_ref):
        hh = h_ref[...]
        gate = _dot(hh, wg_ref[...], NN)
        up = _dot(hh, wu_ref[...], NN)
        g_ref[...] = gate.astype(BF16)
        u_ref[...] = up.astype(BF16)
        a_ref[...] = (gate * jax.nn.sigmoid(gate) * up).astype(BF16)

    wspec = pl.BlockSpec((None, d, fs), lambda c, i: (c, 0, 0))
    ospec = pl.BlockSpec((None, tm, fs), lambda c, i: (c, i, 0))
    osd = _sds((nc, t, fs), BF16)
    return _pcall(
        body, name=name, grid=(nc, t // tm),
        in_specs=[pl.BlockSpec((tm, d), lambda c, i: (i, 0)), wspec, wspec],
        out_specs=(ospec, ospec, ospec), out_shape=(osd, osd, osd),
        compiler_params=_cparams("parallel", "parallel"))(h, wg, wu)


def _ffn_dact(dy, wd, gate, up, name, tm):
    t, d = dy.shape
    nc, fs, _ = wd.shape
    tm = min(tm, t)

    def body(dy_ref, wd_ref, g_ref, u_ref, dg_ref, du_ref):
        da = _dot(dy_ref[...].astype(BF16), wd_ref[...], NT) * FFN_RESID
        gate = g_ref[...].astype(F32)
        up = u_ref[...].astype(F32)
        sig = jax.nn.sigmoid(gate)
        dg_ref[...] = (da * up * (sig * (1.0 + gate * (1.0 - sig)))).astype(BF16)
        du_ref[...] = (da * (gate * sig)).astype(BF16)

    cspec = pl.BlockSpec((None, tm, fs), lambda c, i: (c, i, 0))
    osd = _sds((nc, t, fs), BF16)
    return _pcall(
        body, name=name, grid=(nc, t // tm),
        in_specs=[pl.BlockSpec((tm, d), lambda c, i: (i, 0)),
                  pl.BlockSpec((None, fs, d), lambda c, i: (c, 0, 0)), cspec, cspec],
        out_specs=(cspec, cspec), out_shape=(osd, osd),
        compiler_params=_cparams("parallel", "parallel"))(dy, wd, gate, up)


def _ffn_fwd(x, g, wg, wu, wd, tag):
    t = x.shape[0]
    nc, _, fs = wg.shape
    tm = min(512, t)
    h = _rms_fwd(x, g, BF16, f"{tag}_norm", 512)
    gate, up, act = _ffn_up(h, wg, wu, f"{tag}_up", 1024)
    pairs = [(act, pl.BlockSpec((None, tm, fs), lambda i, j, r, c=c: (c, i, 0)),
              wd, pl.BlockSpec((None, fs, D_MODEL), lambda i, j, r, c=c: (c, 0, 0))) for c in range(nc)]
    row = pl.BlockSpec((tm, D_MODEL), lambda i, j, r: (i, 0))
    y = _mm(f"{tag}_down", (t // tm, 1, 1), pairs, NN, _sds((t, D_MODEL), F32), row, (tm, D_MODEL),
            res=(x, row), scale=FFN_RESID)
    return y, (h, gate, up, act)


def _ffn_bwd(dy, x, g, wg, wu, wd, saved, tag):
    h, gate, up, act = saved
    t = x.shape[0]
    nc, _, fs = wg.shape
    tm = min(512, t)
    tk = min(2048, t)
    dgate, dup = _ffn_dact(dy, wd, gate, up, f"{tag}_dact", 1024)
    tok_c = pl.BlockSpec((None, tk, fs), lambda c, j, r: (c, r, 0))
    tok_d = pl.BlockSpec((tk, D_MODEL), lambda c, j, r: (r, 0))
    dwd = _mm(f"{tag}_dwd", (nc, 1, t // tk), [(act, tok_c, dy, tok_d)], TN,
              _sds((nc, fs, D_MODEL), F32), pl.BlockSpec((None, fs, D_MODEL), lambda c, j, r: (c, 0, 0)),
              (fs, D_MODEL), scale=FFN_RESID)
    wout = pl.BlockSpec((None, D_MODEL, fs), lambda c, j, r: (c, 0, 0))
    dwg = _mm(f"{tag}_dwg", (nc, 1, t // tk), [(h, tok_d, dgate, tok_c)], TN,
              _sds((nc, D_MODEL, fs), F32), wout, (D_MODEL, fs))
    dwu = _mm(f"{tag}_dwu", (nc, 1, t // tk), [(h, tok_d, dup, tok_c)], TN,
              _sds((nc, D_MODEL, fs), F32), wout, (D_MODEL, fs))
    pairs = []
    for c in range(nc):
        a_spec = pl.BlockSpec((None, tm, fs), lambda i, j, r, c=c: (c, i, 0))
        w_spec = pl.BlockSpec((None, D_MODEL, fs), lambda i, j, r, c=c: (c, 0, 0))
        pairs += [(dgate, a_spec, wg, w_spec), (dup, a_spec, wu, w_spec)]
    dh = _mm(f"{tag}_dh", (t // tm, 1, 1), pairs, NT,
             _sds((t, D_MODEL), F32), pl.BlockSpec((tm, D_MODEL), lambda i, j, r: (i, 0)), (tm, D_MODEL))
    dx, dg = _rms_bwd([dh], x, g, dy, f"{tag}_dnorm", 512)
    return dx, dg, dwg, dwu, dwd


def _mm_simple(name, a, b, dn, out_dtype, tm=512, tk=512, res=None, scale=1.0):
    if dn == TN:
        k, m = a.shape
        n = b.shape[1]
        tk = min(tk, k)
        return _mm(name, (1, 1, k // tk),
                   [(a, pl.BlockSpec((tk, m), lambda i, j, r: (r, 0)), b, pl.BlockSpec((tk, n), lambda i, j, r: (r, 0)))],
                   TN, _sds((m, n), out_dtype), pl.BlockSpec((m, n), lambda i, j, r: (0, 0)), (m, n), scale=scale)
    m, k = a.shape
    n = b.shape[1] if dn == NN else b.shape[0]
    tm = min(tm, m)
    row = pl.BlockSpec((tm, n), lambda i, j, r: (i, 0))
    return _mm(name, (m // tm, 1, 1),
               [(a, pl.BlockSpec((tm, k), lambda i, j, r: (i, 0)), b, pl.BlockSpec(b.shape, lambda i, j, r: (0, 0)))],
               dn, _sds((m, n), out_dtype), row, (tm, n), res=None if res is None else (res, row), scale=scale)


def _t5_bucket(dist):
    max_exact = REL_BUCKETS // 2
    d = np.maximum(dist, 1).astype(np.float32)
    large = max_exact + (np.log(d / max_exact) / np.log(REL_MAX_DIST / max_exact)
                         * (REL_BUCKETS - max_exact)).astype(np.int32)
    large = np.minimum(large, REL_BUCKETS - 1)
    return np.where(dist < max_exact, dist, large).astype(np.int32)


def _bucket_tiles():
    i = np.arange(QB)[:, None]
    j = np.arange(QB + DIL_W)[None, :]
    delta = np.clip(i + DIL_W - j, 0, None)
    return np.stack([_t5_bucket(delta * dil) for dil in DIL_DILATIONS]).astype(np.int32)


def _bias_tiles(rel_bias):
    buckets = jnp.asarray(_bucket_tiles())

    def body(rb_ref, bk_ref, o_ref):
        bk = bk_ref[...]
        for h in range(DIL_HEADS):
            def pick(b, tile):
                return jnp.where(bk == b, rb_ref[h, b], tile)

            o_ref[h] = lax.fori_loop(0, REL_BUCKETS, pick, jnp.zeros((QB, QB + DIL_W), F32))

    return _pcall(
        body, name="dil_bias_tiles", grid=(3,),
        in_specs=[pl.BlockSpec(memory_space=pltpu.SMEM),
                  pl.BlockSpec((None, QB, QB + DIL_W), lambda b: (b, 0, 0))],
        out_specs=pl.BlockSpec((None, DIL_HEADS, QB, QB + DIL_W), lambda b: (b, 0, 0, 0)),
        out_shape=_sds((3, DIL_HEADS, QB, QB + DIL_W), F32),
        compiler_params=_cparams("parallel"))(rel_bias, buckets)


def _bias_grad(dtiles):
    buckets = jnp.asarray(_bucket_tiles())

    def body(dt_ref, bk_ref, o_ref):
        def one(b, carry):
            hit = [bk_ref[br] == b for br in range(3)]
            for h in range(DIL_HEADS):
                tot = jnp.zeros((), F32)
                for br in range(3):
                    tot = tot + jnp.sum(jnp.where(hit[br], dt_ref[br, h], 0.0))
                o_ref[h, b] = tot
            return carry

        lax.fori_loop(0, REL_BUCKETS, one, 0)

    return _pcall(
        body, name="dil_bias_grad",
        in_specs=[pl.BlockSpec(memory_space=pltpu.VMEM), pl.BlockSpec(memory_space=pltpu.VMEM)],
        out_specs=pl.BlockSpec(memory_space=pltpu.SMEM),
        out_shape=_sds((DIL_HEADS, REL_BUCKETS), F32))(dtiles, buckets)


def _split_heads(a, lo):
    zero = jnp.zeros_like(a)
    return jnp.concatenate([jnp.where(lo, a, zero), jnp.where(lo, zero, a)], axis=0)


def _side_by_side(a):
    n = a.shape[0] // 2
    return jnp.concatenate([a[:n], a[n:]], axis=1)


def _band_masks(prev_ok):
    ii = lax.broadcasted_iota(jnp.int32, (2 * QB, QB), 0) & (QB - 1)
    jj = lax.broadcasted_iota(jnp.int32, (2 * QB, QB), 1)
    return jj <= ii, jj >= ii + jnp.where(prev_ok, 0, QB)


def _dil_view(a, dil):
    t, w = a.shape
    return a.reshape(t // dil, dil * w)


def _dil_fwd(q, k, v, bias, dil, name):
    t, w = q.shape
    npair = w // LANES
    nl = t // dil // QB
    scale = DIL_HD ** -0.5

    def body(q_ref, kc_ref, kp_ref, vc_ref, vp_ref, b_ref, o_ref, lse_ref):
        nn = pl.program_id(1)
        lo = lax.broadcasted_iota(jnp.int32, (QB, LANES), 1) < DIL_HD
        lo2 = lax.broadcasted_iota(jnp.int32, (2 * QB, LANES), 1) < DIL_HD
        ii = lax.broadcasted_iota(jnp.int32, (2 * QB, 2 * QB), 0) & (QB - 1)
        jj = lax.broadcasted_iota(jnp.int32, (2 * QB, 2 * QB), 1)
        first_key = jnp.maximum(ii, jnp.where(nn != 0, 0, QB))
        valid = (jj >= first_key) & (jj <= ii + QB)
        for p in range(npair):
            cols = slice(p * LANES, (p + 1) * LANES)
            qq = _split_heads(q_ref[:, cols], lo)
            kk = jnp.concatenate([kp_ref[:, cols], kc_ref[:, cols]], axis=0)
            vv = jnp.concatenate([vp_ref[:, cols], vc_ref[:, cols]], axis=0)
            s = jnp.where(valid, _dot(qq, kk, NT) * scale + b_ref[p], NEG)
            m = jnp.max(s, axis=-1, keepdims=True)
            e = jnp.exp(s - m)
            den = jnp.sum(e, axis=-1, keepdims=True)
            pn = (e * (1.0 / den)).astype(BF16)
            o_ref[:, cols] = _dot(_side_by_side(pn), _split_heads(vv, lo2), NN)
            lse = m + jnp.log(den)
            lse_ref[:, cols] = jnp.where(lo, lse[:QB], lse[QB:])

    cur = pl.BlockSpec((QB, w), lambda r, n: (n, r))
    prev = pl.BlockSpec((QB, w), lambda r, n: (jnp.maximum(n - 1, 0), r))
    sd = _sds((t // dil, dil * w), F32)
    o, lse = _pcall(
        body, name=name, grid=(dil, nl),
        in_specs=[cur, cur, prev, cur, prev, pl.BlockSpec((npair, 2 * QB, 2 * QB), lambda r, n: (0, 0, 0))],
        out_specs=(cur, cur), out_shape=(sd, sd),
        compiler_params=_cparams("parallel", "parallel"))(*[_dil_view(a, dil) for a in (q, k, k, v, v)], bias)
    return o.reshape(t, w), lse.reshape(t, w)


def _dil_bwd(q, k, v, do, stats, bias, dil, name):
    t, w = q.shape
    npair = w // LANES
    nl = t // dil // QB
    scale = DIL_HD ** -0.5

    def body(qc_ref, qn_ref, doc_ref, don_ref, sc_ref, sn_ref, k_ref, v_ref, b_ref,
             dq_ref, dk_ref, dv_ref, db_ref, carry):
        r, nn = pl.program_id(0), pl.program_id(1)
        lo = lax.broadcasted_iota(jnp.int32, (QB, LANES), 1) < DIL_HD
        cur_ok, prev_ok = _band_masks(nn + 1 < nl)

        @pl.when((r == 0) & (nn == 0))
        def _():
            db_ref[...] = jnp.zeros_like(db_ref)
            carry[...] = jnp.zeros_like(carry)

        for p in range(npair):
            cols = slice(p * LANES, (p + 1) * LANES)
            kp, vp = k_ref[:, cols], v_ref[:, cols]
            k2 = _split_heads(kp, lo)

            def column(ref, lane):
                first = p * LANES + lane
                return jnp.concatenate([ref[:, first:first + 1], ref[:, first + DIL_HD:first + DIL_HD + 1]], axis=0)

            def side(q_ref, do_ref, s_ref, bias, ok):
                qq = _split_heads(q_ref[:, cols], lo)
                dd = _split_heads(do_ref[:, cols], lo)
                s = jnp.where(ok, _dot(qq, kp, NT) * scale + bias, NEG)
                prob = jnp.exp(s - column(s_ref, 0))
                ds = prob * (_dot(dd, vp, NT) - column(s_ref, DIL_HD // 2))
                return qq, dd, prob.astype(BF16), ds

            q1, d1, p1, ds1 = side(qc_ref, doc_ref, sc_ref, b_ref[p, :, QB:], cur_ok)
            q2, d2, p2, ds2 = side(qn_ref, don_ref, sn_ref, b_ref[p, :, :QB], prev_ok)
            ds1b, ds2b = ds1.astype(BF16), ds2.astype(BF16)
            dq_ref[:, cols] = carry[:, cols] + _dot(_side_by_side(ds1b), k2, NN) * scale
            carry[:, cols] = _dot(_side_by_side(ds2b), k2, NN) * scale
            dk_ref[:, cols] = _dot(jnp.concatenate([ds1b, ds2b], axis=0), jnp.concatenate([q1, q2], axis=0), TN) * scale
            dv_ref[:, cols] = _dot(jnp.concatenate([p1, p2], axis=0), jnp.concatenate([d1, d2], axis=0), TN)
            db_ref[p, :, QB:] += ds1
            db_ref[p, :, :QB] += ds2

    cur = pl.BlockSpec((QB, w), lambda r, n: (n, r))
    nxt = pl.BlockSpec((QB, w), lambda r, n: (jnp.minimum(n + 1, nl - 1), r))
    tile = pl.BlockSpec((npair, 2 * QB, 2 * QB), lambda r, n: (0, 0, 0))
    sd = _sds((t // dil, dil * w), F32)
    views = [_dil_view(a, dil) for a in (q, q, do, do, stats, stats, k, v)]
    dq, dk, dv, db = _pcall(
        body, name=name, grid=(dil, nl),
        in_specs=[cur, nxt, cur, nxt, cur, nxt, cur, cur, tile],
        out_specs=(cur, cur, cur, tile),
        out_shape=(sd, sd, sd, _sds((npair, 2 * QB, 2 * QB), F32)),
        scratch_shapes=[pltpu.VMEM((QB, w), F32)],
        compiler_params=_cparams("arbitrary", "arbitrary"))(*views, bias)
    return dq.reshape(t, w), dk.reshape(t, w), dv.reshape(t, w), db


def _head_sum_matrix(scale):
    idx = np.arange(DIL_WIDTH) // DIL_HD
    return jnp.asarray((idx[:, None] == idx[None, :]).astype(np.float32) * scale, BF16)


def _head_sum(x, mat):
    hi = x.astype(BF16)
    lo = (x - hi.astype(F32)).astype(BF16)
    return _dot(hi, mat, NN) + _dot(lo, mat, NN)


def _dil_merge(outs, lses, g, tm):
    t, w = outs[0].shape
    tm = min(tm, t)

    def body(o0, o1, o2, l0, l1, l2, g_ref, o_ref, l_ref, n_ref):
        a0, a1, a2 = l0[...], l1[...], l2[...]
        m = jnp.maximum(jnp.maximum(a0, a1), a2)
        e0, e1, e2 = jnp.exp(a0 - m), jnp.exp(a1 - m), jnp.exp(a2 - m)
        den = e0 + e1 + e2
        o = (e0 * o0[...] + e1 * o1[...] + e2 * o2[...]) / den
        o_ref[...] = o
        l_ref[...] = m + jnp.log(den)
        r = lax.rsqrt(jnp.mean(o * o, axis=-1, keepdims=True) + EPS)
        n_ref[...] = (o * r * g_ref[...]).astype(n_ref.dtype)

    spec = pl.BlockSpec((tm, w), lambda i: (i, 0))
    return _pcall(
        body, name="dil_merge", grid=(t // tm,),
        in_specs=[spec] * 6 + [pl.BlockSpec((1, w), lambda i: (0, 0))], out_specs=(spec, spec, spec),
        out_shape=(_sds((t, w), F32), _sds((t, w), F32), _sds((t, w), BF16)),
        compiler_params=_cparams("parallel"))(*outs, *lses, g)


def _dil_stats(do, o, lse, tm):
    t, w = do.shape
    tm = min(tm, t)

    def body(a_ref, b_ref, l_ref, m_ref, o_ref):
        first = (lax.broadcasted_iota(jnp.int32, (tm, w), 1) & (DIL_HD - 1)) < DIL_HD // 2
        o_ref[...] = jnp.where(first, l_ref[...], _head_sum(a_ref[...] * b_ref[...], m_ref[...]))

    spec = pl.BlockSpec((tm, w), lambda i: (i, 0))
    return _pcall(body, name="dil_stats", grid=(t // tm,),
                  in_specs=[spec, spec, spec, pl.BlockSpec((w, w), lambda i: (0, 0))], out_specs=spec,
                  out_shape=_sds((t, w), F32), compiler_params=_cparams("parallel"))(do, o, lse, _head_sum_matrix(1.0))


def _head_norm_fwd(x, col, g, name, tm):
    t = x.shape[0]
    w = DIL_WIDTH
    tm = min(tm, t)

    def body(x_ref, g_ref, m_ref, o_ref):
        xf = x_ref[...]
        r = lax.rsqrt(_head_sum(xf * xf, m_ref[...]) + EPS)
        o_ref[...] = (xf * r * g_ref[...]).astype(o_ref.dtype)

    return _pcall(
        body, name=name, grid=(t // tm,),
        in_specs=[pl.BlockSpec((tm, w), lambda i: (i, col)), pl.BlockSpec((1, w), lambda i: (0, 0)),
                  pl.BlockSpec((w, w), lambda i: (0, 0))],
        out_specs=pl.BlockSpec((tm, w), lambda i: (i, 0)), out_shape=_sds((t, w), BF16),
        compiler_params=_cparams("parallel"))(x, g, _head_sum_matrix(1.0 / DIL_HD))


def _head_norm_bwd(dys, x, col, g, name, tm):
    t = x.shape[0]
    w = DIL_WIDTH
    tm = min(tm, t)
    nd = len(dys)
    nt = t // tm
    lane = np.arange(w) % DIL_HD
    fold = jnp.asarray((lane[:, None] == lane[None, :]).astype(np.float32))

    def body(*refs):
        x_ref, g_ref, m_ref, f_ref = refs[nd:nd + 4]
        dx_ref, dg_ref = refs[-2], refs[-1]
        dy = refs[0][...]
        for r_ in refs[1:nd]:
            dy = dy + r_[...]
        xf = x_ref[...]
        mat = m_ref[...]
        r = lax.rsqrt(_head_sum(xf * xf, mat) + EPS)
        xh = xf * r
        dxh = dy * g_ref[...]
        dx_ref[...] = r * (dxh - xh * _head_sum(dxh * xh, mat))

        @pl.when(pl.program_id(0) == 0)
        def _():
            dg_ref[...] = jnp.zeros_like(dg_ref)

        dg_ref[...] += jnp.sum(dy * xh, axis=0, keepdims=True)

        @pl.when(pl.program_id(0) == nt - 1)
        def _():
            per_lane = jnp.broadcast_to(dg_ref[...], (8, w))
            dg_ref[...] = lax.dot_general(per_lane, f_ref[...], NN, precision=lax.Precision.HIGHEST,
                                          preferred_element_type=F32)[0:1]

    row = pl.BlockSpec((tm, w), lambda i: (i, 0))
    vec = pl.BlockSpec((1, w), lambda i: (0, 0))
    sq = pl.BlockSpec((w, w), lambda i: (0, 0))
    return _pcall(
        body, name=name, grid=(nt,),
        in_specs=[row] * nd + [pl.BlockSpec((tm, w), lambda i: (i, col)), vec, sq, sq],
        out_specs=(row, vec), out_shape=(_sds((t, w), F32), _sds((1, w), F32)),
        compiler_params=_cparams("arbitrary"))(*dys, x, g, _head_sum_matrix(1.0 / DIL_HD), fold)


def _rowdot(a, b, name, tm):
    n, d = a.shape
    tm = min(tm, n)

    def body(a_ref, b_ref, o_ref):
        o_ref[...] = jnp.sum(a_ref[...].astype(F32) * b_ref[...].astype(F32), axis=-1, keepdims=True)

    spec = pl.BlockSpec((tm, d), lambda i: (i, 0))
    return _pcall(body, name=name, grid=(n // tm,), in_specs=[spec, spec],
                  out_specs=pl.BlockSpec((tm, 1), lambda i: (i, 0)), out_shape=_sds((n, 1), F32),
                  compiler_params=_cparams("parallel"))(a, b)


def _add3(a, b, c, name, tm):
    n, d = a.shape
    tm = min(tm, n)

    def body(a_ref, b_ref, c_ref, o_ref):
        o_ref[...] = a_ref[...] + b_ref[...] + c_ref[...]

    spec = pl.BlockSpec((tm, d), lambda i: (i, 0))
    return _pcall(body, name=name, grid=(n // tm,), in_specs=[spec] * 3, out_specs=spec,
                  out_shape=_sds((n, d), F32), compiler_params=_cparams("parallel"))(a, b, c)


def _rope_tables(t):
    inv = ROPE_BASE ** (-np.arange(0, MLA_ROPE, 2, dtype=np.float64) / MLA_ROPE)
    ang = np.arange(t, dtype=np.float64)[:, None] * inv[None, :]
    cos, sin = np.cos(ang), np.sin(ang)
    return (jnp.asarray(np.concatenate([cos, cos], 1), F32), jnp.asarray(np.concatenate([-sin, sin], 1), F32))


def _half_swap():
    p = np.zeros((MLA_ROPE, MLA_ROPE), np.float32)
    half = MLA_ROPE // 2
    for i in range(MLA_ROPE):
        p[(i + half) % MLA_ROPE, i] = 1.0
    return jnp.asarray(p)


def _mla_qk_fwd(x, g, cos_t, sin_t, scale, name, tm):
    n, d = x.shape
    t = cos_t.shape[0]
    tm = min(tm, t)
    nt = t // tm
    swap = _half_swap()

    def body(x_ref, g_ref, c_ref, s_ref, p_ref, o_ref):
        xf = x_ref[...]
        r = lax.rsqrt(jnp.mean(xf * xf, axis=-1, keepdims=True) + EPS)
        y = xf * r * g_ref[...]
        yr = y[:, MLA_NOPE:]
        sw = lax.dot_general(yr, p_ref[...], NN, precision=lax.Precision.HIGHEST, preferred_element_type=F32)
        o_ref[:, :MLA_NOPE] = (y[:, :MLA_NOPE] * scale).astype(o_ref.dtype)
        o_ref[:, MLA_NOPE:] = ((yr * c_ref[...] + sw * s_ref[...]) * scale).astype(o_ref.dtype)

    row = pl.BlockSpec((tm, d), lambda i: (i, 0))
    tab = pl.BlockSpec((tm, MLA_ROPE), lambda i: (i % nt, 0))
    return _pcall(
        body, name=name, grid=(n // tm,),
        in_specs=[row, pl.BlockSpec((1, d), lambda i: (0, 0)), tab, tab,
                  pl.BlockSpec((MLA_ROPE, MLA_ROPE), lambda i: (0, 0))],
        out_specs=row, out_shape=_sds((n, d), BF16),
        compiler_params=_cparams("parallel"))(x, g, cos_t, sin_t, swap)


def _mla_qk_bwd(dy, x, g, cos_t, sin_t, scale, name, tm):
    n, d = x.shape
    t = cos_t.shape[0]
    tm = min(tm, t)
    nt = t // tm
    swap_t = _half_swap().T

    def body(dy_ref, x_ref, g_ref, c_ref, s_ref, p_ref, dx_ref, dg_ref):
        xf = x_ref[...]
        gg = g_ref[...]
        r = lax.rsqrt(jnp.mean(xf * xf, axis=-1, keepdims=True) + EPS)
        xh = xf * r
        dyf = dy_ref[...] * scale
        dyr = dyf[:, MLA_NOPE:]
        back = lax.dot_general(dyr * s_ref[...], p_ref[...], NN, precision=lax.Precision.HIGHEST,
                               preferred_element_type=F32)
        dn_n = dyf[:, :MLA_NOPE]
        dn_r = dyr * c_ref[...] + back
        xh_n, xh_r = xh[:, :MLA_NOPE], xh[:, MLA_NOPE:]
        dxh_n = dn_n * gg[:, :MLA_NOPE]
        dxh_r = dn_r * gg[:, MLA_NOPE:]
        mean = (jnp.sum(dxh_n * xh_n, axis=-1, keepdims=True)
                + jnp.sum(dxh_r * xh_r, axis=-1, keepdims=True)) * (1.0 / d)
        dx_ref[:, :MLA_NOPE] = r * (dxh_n - xh_n * mean)
        dx_ref[:, MLA_NOPE:] = r * (dxh_r - xh_r * mean)

        @pl.when(pl.program_id(0) == 0)
        def _():
            dg_ref[...] = jnp.zeros_like(dg_ref)

        dg_ref[:, :MLA_NOPE] += jnp.sum(dn_n * xh_n, axis=0, keepdims=True)
        dg_ref[:, MLA_NOPE:] += jnp.sum(dn_r * xh_r, axis=0, keepdims=True)

    row = pl.BlockSpec((tm, d), lambda i: (i, 0))
    vec = pl.BlockSpec((1, d), lambda i: (0, 0))
    tab = pl.BlockSpec((tm, MLA_ROPE), lambda i: (i % nt, 0))
    return _pcall(
        body, name=name, grid=(n // tm,),
        in_specs=[row, row, vec, tab, tab, pl.BlockSpec((MLA_ROPE, MLA_ROPE), lambda i: (0, 0))],
        out_specs=(row, vec), out_shape=(_sds((n, d), F32), _sds((1, d), F32)),
        compiler_params=_cparams("arbitrary"))(dy, x, g, cos_t, sin_t, swap_t)


def _causal_mask(i, j, tq, tk, width):
    row = i * tq + lax.broadcasted_iota(jnp.int32, (tq, width), 0)
    col = j * tk + lax.broadcasted_iota(jnp.int32, (tq, width), 1)
    return col <= row


def _causal_steps(nq, nk, tq, tk, q_major):
    if q_major:
        groups = [[(i, j) for j in range((i * tq + tq - 1) // tk + 1)] for i in range(nq)]
        nunit = tk // tq if tk % tq == 0 else 1
    else:
        groups = [[(i, j) for i in range((j * tk) // tq, nq)] for j in range(nk)]
        nunit = tq // tk if tq % tk == 0 else 1
    it, jt, fl = [], [], []
    for g in groups:
        for n, (i, j) in enumerate(g):
            crossing = j * tk + tk - 1 > i * tq
            if q_major:
                unit = tk // nunit
                u = min(nunit, -(-(i * tq + tq - j * tk) // unit)) - 1
            else:
                unit = tq // nunit
                u = max(0, j * tk - i * tq) // unit
            it.append(i)
            jt.append(j)
            fl.append((n == 0) + 2 * (n == len(g) - 1) + 4 * crossing + 8 * (u if crossing else 0))
    return tuple(jnp.asarray(np.array(a, np.int32)) for a in (it, jt, fl)), nunit


def _by_crossing(flags, nunit, update):
    pl.when((flags & 4) == 0)(functools.partial(update, None))
    for u in range(nunit):
        pl.when(((flags & 4) != 0) & ((flags >> 3) == u))(functools.partial(update, u))


def _causal_specs(tq, tk):
    def qs(w):
        return pl.BlockSpec((None, tq, w), lambda h, s, it, jt, fl: (h, it[s], 0))

    def kv(w):
        return pl.BlockSpec((None, tk, w), lambda h, s, it, jt, fl: (h, jt[s], 0))

    return qs, kv


def _mla_fwd(q, k, v, tq, tk):
    nh, t, dq = q.shape
    dv = v.shape[2]
    tq, tk = min(tq, t), min(tk, t)
    tables, nunit = _causal_steps(t // tq, t // tk, tq, tk, True)

    def body(it, jt, fl, q_ref, k_ref, v_ref, o_ref, lse_ref, m_sc, l_sc, acc_sc):
        step = pl.program_id(1)
        i, j, flags = it[step], jt[step], fl[step]

        @pl.when((flags & 1) != 0)
        def _():
            m_sc[...] = jnp.full_like(m_sc, NEG)
            l_sc[...] = jnp.zeros_like(l_sc)
            acc_sc[...] = jnp.zeros_like(acc_sc)

        def update(units):
            wk = tk if units is None else (units + 1) * (tk // nunit)
            s = _dot(q_ref[...], k_ref[:wk, :], NT)
            if units is not None:
                s = jnp.where(_causal_mask(i, j, tq, tk, wk), s, NEG)
            m_prev = m_sc[...]
            m_new = jnp.maximum(m_prev, jnp.max(s, axis=-1, keepdims=True))
            alpha = jnp.exp(m_prev - m_new)
            p = jnp.exp(s - m_new)
            l_sc[...] = alpha * l_sc[...] + jnp.sum(p, axis=-1, keepdims=True)
            acc_sc[...] = alpha * acc_sc[...] + _dot(p.astype(BF16), v_ref[:wk, :], NN)
            m_sc[...] = m_new

        _by_crossing(flags, nunit, update)

        @pl.when((flags & 2) != 0)
        def _():
            o_ref[...] = acc_sc[...] / l_sc[...]
            lse_ref[...] = m_sc[...] + jnp.log(l_sc[...])

    qs, kv = _causal_specs(tq, tk)
    return _pcall(
        body, name="mla_attn_fwd",
        grid_spec=pltpu.PrefetchScalarGridSpec(
            num_scalar_prefetch=3, grid=(nh, tables[0].shape[0]),
            in_specs=[qs(dq), kv(dq), kv(dv)], out_specs=(qs(dv), qs(1)),
            scratch_shapes=[pltpu.VMEM((tq, 1), F32), pltpu.VMEM((tq, 1), F32), pltpu.VMEM((tq, dv), F32)]),
        out_shape=(_sds((nh, t, dv), F32), _sds((nh, t, 1), F32)),
        compiler_params=_cparams("parallel", "arbitrary"))(*tables, q, k, v)


def _mla_bwd_dq(q, k, v, do, lse, dl, tq, tk):
    nh, t, dq = q.shape
    dv = v.shape[2]
    tq, tk = min(tq, t), min(tk, t)
    tables, nunit = _causal_steps(t // tq, t // tk, tq, tk, True)

    def body(it, jt, fl, q_ref, k_ref, v_ref, do_ref, lse_ref, dl_ref, dq_ref, acc_sc):
        step = pl.program_id(1)
        i, j, flags = it[step], jt[step], fl[step]

        def update(units):
            wk = tk if units is None else (units + 1) * (tk // nunit)
            s = _dot(q_ref[...], k_ref[:wk, :], NT)
            if units is not None:
                s = jnp.where(_causal_mask(i, j, tq, tk, wk), s, NEG)
            p = jnp.exp(s - lse_ref[...])
            dp = _dot(do_ref[...].astype(BF16), v_ref[:wk, :], NT)
            ds = p * (dp - dl_ref[...])
            part = _dot(ds.astype(BF16), k_ref[:wk, :], NN)

            @pl.when((flags & 1) != 0)
            def _():
                acc_sc[...] = part

            @pl.when((flags & 1) == 0)
            def _():
                acc_sc[...] += part

        _by_crossing(flags, nunit, update)

        @pl.when((flags & 2) != 0)
        def _():
            dq_ref[...] = acc_sc[...]

    qs, kv = _causal_specs(tq, tk)
    return _pcall(
        body, name="mla_attn_dq",
        grid_spec=pltpu.PrefetchScalarGridSpec(
            num_scalar_prefetch=3, grid=(nh, tables[0].shape[0]),
            in_specs=[qs(dq), kv(dq), kv(dv), qs(dv), qs(1), qs(1)], out_specs=qs(dq),
            scratch_shapes=[pltpu.VMEM((tq, dq), F32)]),
        out_shape=_sds((nh, t, dq), F32),
        compiler_params=_cparams("parallel", "arbitrary"))(*tables, q, k, v, do, lse, dl)


def _mla_bwd_dkv(q, k, v, do, lse_row, dl_row, tq, tk):
    nh, t, dq = q.shape
    dv = v.shape[2]
    tq, tk = min(tq, t), min(tk, t)
    tables, nunit = _causal_steps(t // tq, t // tk, tq, tk, False)

    def body(it, jt, fl, q_ref, k_ref, v_ref, do_ref, lse_ref, dl_ref, dk_ref, dv_ref, dk_sc, dv_sc):
        step = pl.program_id(1)
        i, j, flags = it[step], jt[step], fl[step]

        def update(units):
            off = 0 if units is None else units * (tq // nunit)
            qq = q_ref[off:, :]
            st = _dot(k_ref[...], qq, NT)
            if units is not None:
                key = j * tk + lax.broadcasted_iota(jnp.int32, (tk, tq - off), 0)
                qry = i * tq + off + lax.broadcasted_iota(jnp.int32, (tk, tq - off), 1)
                st = jnp.where(key <= qry, st, NEG)
            pt = jnp.exp(st - lse_ref[:, off:])
            dob = do_ref[off:, :].astype(BF16)
            dpt = _dot(v_ref[...], dob, NT)
            dst = pt * (dpt - dl_ref[:, off:])
            dv_part = _dot(pt.astype(BF16), dob, NN)
            dk_part = _dot(dst.astype(BF16), qq, NN)

            @pl.when((flags & 1) != 0)
            def _():
                dv_sc[...] = dv_part
                dk_sc[...] = dk_part

            @pl.when((flags & 1) == 0)
            def _():
                dv_sc[...] += dv_part
                dk_sc[...] += dk_part

        _by_crossing(flags, nunit, update)

        @pl.when((flags & 2) != 0)
        def _():
            dk_ref[...] = dk_sc[...]
            dv_ref[...] = dv_sc[...]

    qs, kv = _causal_specs(tq, tk)
    rowv = pl.BlockSpec((None, 1, tq), lambda h, s, it, jt, fl: (h, 0, it[s]))
    return _pcall(
        body, name="mla_attn_dkv",
        grid_spec=pltpu.PrefetchScalarGridSpec(
            num_scalar_prefetch=3, grid=(nh, tables[0].shape[0]),
            in_specs=[qs(dq), kv(dq), kv(dv), qs(dv), rowv, rowv], out_specs=(kv(dq), kv(dv)),
            scratch_shapes=[pltpu.VMEM((tk, dq), F32), pltpu.VMEM((tk, dv), F32)]),
        out_shape=(_sds((nh, t, dq), F32), _sds((nh, t, dv), F32)),
        compiler_params=_cparams("parallel", "arbitrary"))(*tables, q, k, v, do, lse_row, dl_row)


def _loss_head(y, target, tm):
    t, d = y.shape
    tm = min(tm, t)
    nt = t // tm

    def body(y_ref, t_ref, dy_ref, loss_ref, acc):
        i = pl.program_id(0)
        err = y_ref[...] - t_ref[...]
        dy_ref[...] = err * (1.0 / d)

        @pl.when(i == 0)
        def _():
            acc[...] = jnp.zeros_like(acc)

        acc[...] += jnp.sum(err * err, axis=0, keepdims=True)

        @pl.when(i == nt - 1)
        def _():
            loss_ref[0, 0] = jnp.sum(acc[...]) * (0.5 / d)

    spec = pl.BlockSpec((tm, d), lambda i: (i, 0))
    return _pcall(
        body, name="loss_head", grid=(nt,), in_specs=[spec, spec],
        out_specs=(spec, pl.BlockSpec(memory_space=pltpu.SMEM)),
        out_shape=(_sds((t, d), F32), _sds((1, 1), F32)),
        scratch_shapes=[pltpu.VMEM((1, d), F32)],
        compiler_params=_cparams("arbitrary"))(y, target)


def _adamw(w, g, m, v, name):
    r, c = w.shape
    tr = r
    for cand in (256, 128, 64, 32, 16, 8):
        if r % cand == 0:
            tr = cand
            break

    def body(w_ref, g_ref, m_ref, v_ref, d_ref, nm_ref, nv_ref):
        gg = g_ref[...]
        nm = ADAM_B1 * m_ref[...] + (1.0 - ADAM_B1) * gg
        nv = ADAM_B2 * v_ref[...] + (1.0 - ADAM_B2) * (gg * gg)
        m_hat = nm / (1.0 - ADAM_B1 ** ADAM_STEP)
        v_hat = nv / (1.0 - ADAM_B2 ** ADAM_STEP)
        d_ref[...] = -ADAM_LR * (m_hat / (jnp.sqrt(v_hat) + ADAM_EPS) + ADAM_WD * w_ref[...])
        nm_ref[...] = nm
        nv_ref[...] = nv

    spec = pl.BlockSpec((tr, c), lambda i: (i, 0))
    sd = _sds((r, c), F32)
    return _pcall(body, name=name, grid=(r // tr,), in_specs=[spec] * 4, out_specs=(spec,) * 3,
                  out_shape=(sd, sd, sd), compiler_params=_cparams("parallel"))(w, g, m, v)


MESH_ID = pl.DeviceIdType.MESH
HBM_SPEC = pl.BlockSpec(memory_space=pltpu.HBM)


def _place():
    return lax.axis_index("x"), lax.axis_index("y"), lax.axis_index("c")


def _other_chips(x, y):
    return [(1 - x, y), (x, 1 - y), (1 - x, 1 - y)]


def _remote(src, dst, send_sems, recv_sems, k, to):
    return pltpu.make_async_remote_copy(src_ref=src, dst_ref=dst, send_sem=send_sems.at[k], recv_sem=recv_sems.at[k],
                                        device_id=to, device_id_type=MESH_ID)


D2D_SPLIT = 16
ICI_SPLIT = 4


def _chunks(rows, n):
    assert rows % n == 0
    return [(i * (rows // n), rows // n) for i in range(n)]


def _gather_weights(packed):
    rows, lanes = packed.shape
    half = rows // 2

    def body(src, out, send_sems, recv_sems):
        x, y, c = _place()
        me = 2 * x + y
        sibling = (x, y, 1 - c)
        chips = _other_chips(x, y)

        def part(chip, core, lo=0, n=half):
            return out.at[chip, pl.ds(core * half + lo, n), :]

        for k, (cx, cy) in enumerate(chips):
            for lo, n in _chunks(half, ICI_SPLIT):
                _remote(src.at[pl.ds(c * half + lo, n), :], part(me, c, lo, n), send_sems, recv_sems, k,
                        (cx, cy, c)).start()
        for k, (cx, cy) in enumerate(chips):
            got = part(2 * cx + cy, c)
            _remote(got, got, send_sems, recv_sems, k, (x, y, c)).wait_recv()
            for lo, n in _chunks(half, D2D_SPLIT):
                piece = part(2 * cx + cy, c, lo, n)
                _remote(piece, piece, send_sems, recv_sems, 3 + k, sibling).start()
        for k, (cx, cy) in enumerate(chips):
            got = part(2 * cx + cy, 1 - c)
            _remote(got, got, send_sems, recv_sems, 3 + k, (x, y, c)).wait_recv()
        for k in range(6):
            sent = part(me, c)
            _remote(sent, sent, send_sems, recv_sems, k, (x, y, c)).wait_send()

    return _pcall(
        body, name="gather_weights", in_specs=[HBM_SPEC], out_specs=HBM_SPEC,
        out_shape=_sds((N_CHIPS, rows, lanes), packed.dtype),
        scratch_shapes=[pltpu.SemaphoreType.DMA((6,)), pltpu.SemaphoreType.DMA((6,))],
    )(packed)


def _reduce_cores(grads):
    nchip, rows, lanes = grads.shape
    half = rows // 2

    def body(g, theirs, send_sems, recv_sems):
        x, y, c = _place()
        for j in range(nchip):
            for lo, n in _chunks(half, D2D_SPLIT):
                _remote(g.at[j, pl.ds((1 - c) * half + lo, n), :], theirs.at[j, pl.ds(lo, n), :],
                        send_sems, recv_sems, 0, (x, y, 1 - c)).start()
        _remote(g.at[:, pl.ds((1 - c) * half, half), :], theirs, send_sems, recv_sems, 0, (x, y, c)).wait()

    return _pcall(
        body, name="reduce_cores", in_specs=[HBM_SPEC], out_specs=HBM_SPEC,
        out_shape=_sds((nchip, half, lanes), grads.dtype),
        scratch_shapes=[pltpu.SemaphoreType.DMA((1,)), pltpu.SemaphoreType.DMA((1,))],
    )(grads)


def _scatter_chips(part):
    nchip, half, lanes = part.shape

    def body(p, out, send_sems, recv_sems):
        x, y, c = _place()
        for k, (cx, cy) in enumerate(_other_chips(x, y)):
            for lo, n in _chunks(half, ICI_SPLIT):
                _remote(p.at[2 * cx + cy, pl.ds(lo, n), :], out.at[k, pl.ds(lo, n), :],
                        send_sems, recv_sems, k, (cx, cy, c)).start()
        for k in range(3):
            _remote(p.at[k], out.at[k], send_sems, recv_sems, k, (x, y, c)).wait()

    return _pcall(
        body, name="scatter_chips", in_specs=[HBM_SPEC], out_specs=HBM_SPEC,
        out_shape=_sds((3, half, lanes), part.dtype),
        scratch_shapes=[pltpu.SemaphoreType.DMA((3,)), pltpu.SemaphoreType.DMA((3,))],
    )(part)


def _sum_partials(received, part, place, tm):
    _, half, lanes = received.shape
    tm = min(tm, half)
    nblk = half // tm

    def body(place_ref, r_ref, p_ref, o_ref):
        tot = p_ref[...].astype(F32)
        for k in range(3):
            tot = tot + r_ref[k].astype(F32)
        o_ref[...] = tot

    return _pcall(
        body, name="sum_chip_partials",
        grid_spec=pltpu.PrefetchScalarGridSpec(
            num_scalar_prefetch=1, grid=(nblk,),
            in_specs=[pl.BlockSpec((3, tm, lanes), lambda i, pc: (0, i, 0)),
                      pl.BlockSpec((None, tm, lanes), lambda i, pc: (pc[0], i, 0))],
            out_specs=pl.BlockSpec((tm, lanes), lambda i, pc: (pc[1] * nblk + i, 0))),
        out_shape=_sds((2 * half, lanes), F32),
        compiler_params=_cparams("parallel"))(place, received, part)


def _share_cores(block):
    rows, lanes = block.shape
    half = rows // 2

    def body(src, out, send_sems, recv_sems):
        x, y, c = _place()
        for lo, n in _chunks(half, D2D_SPLIT):
            piece = pl.ds(c * half + lo, n)
            _remote(src.at[piece, :], out.at[piece, :], send_sems, recv_sems, 0, (x, y, 1 - c)).start()
        mine = out.at[pl.ds(c * half, half), :]
        theirs = out.at[pl.ds((1 - c) * half, half), :]
        _remote(mine, theirs, send_sems, recv_sems, 0, (x, y, c)).wait()

    return _pcall(
        body, name="share_cores", in_specs=[HBM_SPEC], out_specs=HBM_SPEC,
        out_shape=_sds((rows, lanes), block.dtype), input_output_aliases={0: 0},
        scratch_shapes=[pltpu.SemaphoreType.DMA((1,)), pltpu.SemaphoreType.DMA((1,))],
    )(block)


def _sum_blocks(stacked, name, tm):
    n, rows, lanes = stacked.shape
    tm = min(tm, rows)

    def body(s_ref, o_ref):
        tot = s_ref[n - 1].astype(F32)
        for k in range(n - 1):
            tot = tot + s_ref[k].astype(F32)
        o_ref[...] = tot

    return _pcall(body, name=name, grid=(rows // tm,),
                  in_specs=[pl.BlockSpec((n, tm, lanes), lambda i: (0, i, 0))],
                  out_specs=pl.BlockSpec((tm, lanes), lambda i: (i, 0)), out_shape=_sds((rows, lanes), F32),
                  compiler_params=_cparams("parallel"))(stacked)


def _add_halves(grads, theirs, core, tm):
    n, half, lanes = theirs.shape
    tm = min(tm, half)
    nblk = half // tm

    def body(c_ref, g_ref, t_ref, o_ref):
        o_ref[...] = (g_ref[...] + t_ref[...]).astype(o_ref.dtype)

    spec = pl.BlockSpec((None, tm, lanes), lambda k, i, c: (k, i, 0))
    return _pcall(
        body, name="add_core_halves",
        grid_spec=pltpu.PrefetchScalarGridSpec(
            num_scalar_prefetch=1, grid=(n, nblk),
            in_specs=[pl.BlockSpec((None, tm, lanes), lambda k, i, c: (k, c[0] * nblk + i, 0)), spec], out_specs=spec),
        out_shape=_sds((n, half, lanes), BF16),
        compiler_params=_cparams("parallel", "parallel"))(core, grads, theirs)


def _allreduce_small(part):
    rows, lanes = part.shape
    ndev = 8

    def body(src, tot, buf, send_sems, recv_sems):
        x, y, c = _place()
        me = 4 * x + 2 * y + c
        buf[me] = src[...]
        sends = []
        for k in range(1, ndev):
            peer = (x ^ (k >> 2), y ^ ((k >> 1) & 1), c ^ (k & 1))
            cp = _remote(src, buf.at[me], send_sems, recv_sems, k - 1, peer)
            cp.start()
            sends.append(cp)
        for k in range(1, ndev):
            theirs = buf.at[me ^ k]
            _remote(theirs, theirs, send_sems, recv_sems, k - 1, (x, y, c)).wait_recv()
        for cp in sends:
            cp.wait_send()
        acc = buf[0]
        for d in range(1, ndev):
            acc = acc + buf[d]
        tot[...] = acc

    vm = pl.BlockSpec(memory_space=pltpu.VMEM)
    return _pcall(
        body, name="allreduce_small", in_specs=[vm], out_specs=vm, out_shape=_sds((rows, lanes), F32),
        scratch_shapes=[pltpu.VMEM((ndev, rows, lanes), F32), pltpu.SemaphoreType.DMA((ndev - 1,)),
                        pltpu.SemaphoreType.DMA((ndev - 1,))],
    )(part)


def _big_rows():
    return [int(np.prod(shape)) // LANES for _, shape in BIG]


def _pack_big(blocks, dtype):
    parts = [blocks[name].reshape(blocks[name].shape[0], -1, LANES).astype(dtype) for name, _ in BIG]
    return jnp.concatenate(parts, axis=1)


def _unpack_big(packed):
    out, off = {}, 0
    for (name, shape), r in zip(BIG, _big_rows()):
        out[name] = packed[:, off:off + r].reshape((packed.shape[0],) + shape)
        off += r
    return out


def _pack_small(vals):
    parts = []
    for name, shape, r in SMALL:
        flat = vals[name].reshape(-1).astype(F32)
        parts.append(jnp.pad(flat, (0, r * LANES - flat.shape[0])).reshape(r, LANES))
    used = sum(r for _, _, r in SMALL)
    parts.append(jnp.zeros((SMALL_ROWS - used, LANES), F32))
    return jnp.concatenate(parts, axis=0)


def _unpack_small(packed):
    out, off = {}, 0
    for name, shape, r in SMALL:
        n = int(np.prod(shape))
        out[name] = packed[off:off + r].reshape(-1)[:n].reshape(shape)
        off += r
    return out


def _heads_major(a, nh):
    t = a.shape[0]
    return a.reshape(t, nh, a.shape[1] // nh).transpose(1, 0, 2)


def _tokens_major(a):
    nh, t, w = a.shape
    return a.transpose(1, 0, 2).reshape(t, nh * w)


def _local_step(x, target, small, wfull):
    t = x.shape[0]
    nh, hd = DIL_HEADS, DIL_HD
    w_in = wfull["w_in"].transpose(1, 0, 2).reshape(D_MODEL, -1)
    w_out = wfull["w_out"].reshape(D_MODEL, D_MODEL)
    w_qb, w_kvb = wfull["mla_w_q_b"], wfull["mla_w_kv_b"]
    grads_s, grads_b = {}, {}

    x1, ffn1_saved = _ffn_fwd(x, small["ffn1_norm"], wfull["ffn1_w_gate"], wfull["ffn1_w_up"],
                              wfull["ffn1_w_down"], "ffn1")
    hm = _rms_fwd(x1, small["mix_norm"], BF16, "mix_norm", 512)
    proj = _mm_simple("in_proj", hm, w_in, NN, F32, tm=1024)
    cq, ckv, k_pe = proj[:, 1536:1792], proj[:, 1792:1920], proj[:, 1920:1984]

    gq, gk = jnp.tile(small["dil_q_norm"], (1, nh)), jnp.tile(small["dil_k_norm"], (1, nh))
    qn = _head_norm_fwd(proj, 0, gq, "dil_q_norm", 512)
    kn = _head_norm_fwd(proj, 1, gk, "dil_k_norm", 512)
    v_d = proj[:, 2 * DIL_WIDTH:3 * DIL_WIDTH].astype(BF16)
    bias = _bias_tiles(small["rel_bias"]).reshape(3, nh // 2, 2 * QB, QB + DIL_W)
    outs, lses = [], []
    for b, dil in enumerate(DIL_DILATIONS):
        o_b, lse_b = _dil_fwd(qn, kn, v_d, bias[b], dil, f"dil_fwd_{dil}")
        outs.append(o_b)
        lses.append(lse_b)
    o_dil, lse_tot, od = _dil_merge(outs, lses, small["out_norm_dil"], 512)

    mh = MLA_HEADS
    cos_t, sin_t = _rope_tables(t)
    cqn = _rms_fwd(cq, small["mla_q_a_norm"], BF16, "mla_q_a_norm", 512)
    ckvn = _rms_fwd(ckv, small["mla_kv_a_norm"], BF16, "mla_kv_a_norm", 512)
    tm = min(512, t)

    th = min(2048, t)

    def head_proj(name, a, w, width):
        k = a.shape[1]
        return _mm(name, (mh, t // th, 1),
                   [(a, pl.BlockSpec((th, k), lambda h, i, r: (i, 0)), w, pl.BlockSpec((None, k, width), lambda h, i, r: (h, 0, 0)))],
                   NN, _sds((mh, t, width), F32), pl.BlockSpec((None, th, width), lambda h, i, r: (h, i, 0)), (th, width))

    q_raw = head_proj("mla_q_proj", cqn, w_qb, MLA_QK)
    kv_raw = head_proj("mla_kv_proj", ckvn, w_kvb, MLA_NOPE + MLA_V)
    k_raw = jnp.concatenate([kv_raw[:, :, :MLA_NOPE], jnp.broadcast_to(k_pe[None], (mh, t, MLA_ROPE))], axis=2)
    v_m = kv_raw[:, :, MLA_NOPE:].astype(BF16)
    q_raw2, k_raw2 = q_raw.reshape(mh * t, MLA_QK), k_raw.reshape(mh * t, MLA_QK)
    q_scale = MLA_QK ** -0.5
    q_m = _mla_qk_fwd(q_raw2, small["mla_q_norm"], cos_t, sin_t, q_scale, "mla_q_rope", 2048).reshape(mh, t, MLA_QK)
    k_m = _mla_qk_fwd(k_raw2, small["mla_k_norm"], cos_t, sin_t, 1.0, "mla_k_rope", 2048).reshape(mh, t, MLA_QK)
    o_mla_h, lse_m = _mla_fwd(q_m, k_m, v_m, 512, 2048)
    o_mla = _tokens_major(o_mla_h)

    om = _rms_fwd(o_mla, small["out_norm_mla"], BF16, "out_norm_mla", 512)
    half_w = DIL_WIDTH
    row = pl.BlockSpec((tm, D_MODEL), lambda i, j, r: (i, 0))
    act_spec = pl.BlockSpec((tm, half_w), lambda i, j, r: (i, 0))
    x2 = _mm("out_proj", (t // tm, 1, 1),
             [(od, act_spec, w_out, pl.BlockSpec((half_w, D_MODEL), lambda i, j, r: (0, 0))),
              (om, act_spec, w_out, pl.BlockSpec((half_w, D_MODEL), lambda i, j, r: (1, 0)))],
             NN, _sds((t, D_MODEL), F32), row, (tm, D_MODEL), res=(x1, row))
    x3, ffn2_saved = _ffn_fwd(x2, small["ffn2_norm"], wfull["ffn2_w_gate"], wfull["ffn2_w_up"],
                              wfull["ffn2_w_down"], "ffn2")
    dy, loss = _loss_head(x3, target, 512)

    dx2, grads_s["ffn2_norm"], grads_b["ffn2_w_gate"], grads_b["ffn2_w_up"], grads_b["ffn2_w_down"] = _ffn_bwd(
        dy, x2, small["ffn2_norm"], wfull["ffn2_w_gate"], wfull["ffn2_w_up"], wfull["ffn2_w_down"], ffn2_saved, "ffn2")

    d_ocat = _mm_simple("out_proj_dx", dx2, w_out, NT, F32, tm=1024)
    dw_out_d = _mm_simple("out_proj_dw_dil", od, dx2, TN, F32, tk=2048)
    dw_out_m = _mm_simple("out_proj_dw_mla", om, dx2, TN, F32, tk=2048)
    grads_b["w_out"] = jnp.concatenate([dw_out_d, dw_out_m], axis=0).reshape(N_CHIPS, D_MODEL // N_CHIPS, D_MODEL)
    do_dil, grads_s["out_norm_dil"] = _rms_bwd([d_ocat[:, :half_w]], o_dil, small["out_norm_dil"], None, "out_norm_dil_bwd", 512)
    do_mla, grads_s["out_norm_mla"] = _rms_bwd([d_ocat[:, half_w:]], o_mla, small["out_norm_mla"], None, "out_norm_mla_bwd", 512)

    do_m = _heads_major(do_mla, mh)
    dl_m = _rowdot(do_m.reshape(mh * t, MLA_V), o_mla_h.reshape(mh * t, MLA_V), "mla_delta", 2048).reshape(mh, t, 1)
    dq_m = _mla_bwd_dq(q_m, k_m, v_m, do_m, lse_m, dl_m, 512, 2048)
    dk_m, dv_m = _mla_bwd_dkv(q_m, k_m, v_m, do_m, lse_m.reshape(mh, 1, t), dl_m.reshape(mh, 1, t), 2048, 512)
    dq_raw, grads_s["mla_q_norm"] = _mla_qk_bwd(dq_m.reshape(mh * t, MLA_QK), q_raw2, small["mla_q_norm"],
                                                 cos_t, sin_t, q_scale, "mla_q_rope_bwd", 2048)
    dk_raw, grads_s["mla_k_norm"] = _mla_qk_bwd(dk_m.reshape(mh * t, MLA_QK), k_raw2, small["mla_k_norm"],
                                                 cos_t, sin_t, 1.0, "mla_k_rope_bwd", 2048)
    dq_raw = dq_raw.reshape(mh, t, MLA_QK)
    dk_raw = dk_raw.reshape(mh, t, MLA_QK)
    dkv_raw = jnp.concatenate([dk_raw[:, :, :MLA_NOPE], dv_m], axis=2)
    dk_pe_h = dk_raw[:, :, MLA_NOPE:]

    def head_proj_dx(name, d, w):
        width, k = d.shape[2], w.shape[1]
        pairs = [(d, pl.BlockSpec((None, th, width), lambda i, j, r, h=h: (h, i, 0)),
                  w, pl.BlockSpec((None, k, width), lambda i, j, r, h=h: (h, 0, 0))) for h in range(mh)]
        return _mm(name, (t // th, 1, 1), pairs, NT, _sds((t, k), F32),
                   pl.BlockSpec((th, k), lambda i, j, r: (i, 0)), (th, k))

    def head_proj_dw(name, a, d):
        width, k = d.shape[2], a.shape[1]
        return _mm(name, (mh, 1, t // th),
                   [(a, pl.BlockSpec((th, k), lambda h, j, r: (r, 0)), d, pl.BlockSpec((None, th, width), lambda h, j, r: (h, r, 0)))],
                   TN, _sds((mh, k, width), F32), pl.BlockSpec((None, k, width), lambda h, j, r: (h, 0, 0)), (k, width))

    d_cqn = head_proj_dx("mla_q_proj_dx", dq_raw, w_qb)
    d_ckvn = head_proj_dx("mla_kv_proj_dx", dkv_raw, w_kvb)
    grads_b["mla_w_q_b"] = head_proj_dw("mla_q_proj_dw", cqn, dq_raw)
    grads_b["mla_w_kv_b"] = head_proj_dw("mla_kv_proj_dw", ckvn, dkv_raw)
    d_cq, grads_s["mla_q_a_norm"] = _rms_bwd([d_cqn], cq, small["mla_q_a_norm"], None, "mla_q_a_norm_bwd", 512)
    d_ckv, grads_s["mla_kv_a_norm"] = _rms_bwd([d_ckvn], ckv, small["mla_kv_a_norm"], None, "mla_kv_a_norm_bwd", 512)
    d_kpe = _sum_blocks(dk_pe_h.reshape(mh, t * MLA_ROPE // LANES, LANES), "mla_kpe_sum", 1024).reshape(t, MLA_ROPE)

    stats = _dil_stats(do_dil, o_dil, lse_tot, 512)
    do_db = do_dil.astype(BF16)
    dqs, dks, dvs, dtiles = [], [], [], []
    for b, dil in enumerate(DIL_DILATIONS):
        dq_b, dk_b, dv_b, db_b = _dil_bwd(qn, kn, v_d, do_db, stats, bias[b], dil, f"dil_bwd_{dil}")
        dqs.append(dq_b)
        dks.append(dk_b)
        dvs.append(dv_b)
        dtiles.append(db_b)
    grads_s["rel_bias"] = _bias_grad(jnp.stack(dtiles).reshape(3, nh, QB, QB + DIL_W))
    dq_a, dgq = _head_norm_bwd(dqs, proj, 0, gq, "dil_q_norm_bwd", 512)
    dk_a, dgk = _head_norm_bwd(dks, proj, 1, gk, "dil_k_norm_bwd", 512)
    grads_s["dil_q_norm"], grads_s["dil_k_norm"] = dgq[:, :hd], dgk[:, :hd]
    dv_a = _add3(dvs[0], dvs[1], dvs[2], "dil_dv_sum", 512)
    dproj = jnp.concatenate([dq_a, dk_a, dv_a, d_cq, d_ckv, d_kpe], axis=1)

    d_hm = _mm_simple("in_proj_dx", dproj, w_in, NT, F32, tm=1024)
    dw_in = _mm_simple("in_proj_dw", hm, dproj, TN, F32)
    grads_b["w_in"] = dw_in.reshape(D_MODEL, N_CHIPS, -1).transpose(1, 0, 2)
    dx1, grads_s["mix_norm"] = _rms_bwd([d_hm], x1, small["mix_norm"], dx2, "mix_norm_bwd", 512)
    dx, grads_s["ffn1_norm"], grads_b["ffn1_w_gate"], grads_b["ffn1_w_up"], grads_b["ffn1_w_down"] = _ffn_bwd(
        dx1, x, small["ffn1_norm"], wfull["ffn1_w_gate"], wfull["ffn1_w_up"], wfull["ffn1_w_down"], ffn1_saved, "ffn1")
    return loss, dx, grads_s, grads_b


def kernel(x, ffn1_norm, ffn1_w_gate, ffn1_w_up, ffn1_w_down, mix_norm, w_in, dil_q_norm, dil_k_norm, rel_bias, mla_q_a_norm, mla_w_q_b, mla_kv_a_norm, mla_w_kv_b, mla_q_norm, mla_k_norm, out_norm_dil, out_norm_mla, w_out, ffn2_norm, ffn2_w_gate, ffn2_w_up, ffn2_w_down, loss_target, m_ffn1_norm, m_ffn1_w_gate, m_ffn1_w_up, m_ffn1_w_down, m_mix_norm, m_w_in, m_dil_q_norm, m_dil_k_norm, m_rel_bias, m_mla_q_a_norm, m_mla_w_q_b, m_mla_kv_a_norm, m_mla_w_kv_b, m_mla_q_norm, m_mla_k_norm, m_out_norm_dil, m_out_norm_mla, m_w_out, m_ffn2_norm, m_ffn2_w_gate, m_ffn2_w_up, m_ffn2_w_down, v_ffn1_norm, v_ffn1_w_gate, v_ffn1_w_up, v_ffn1_w_down, v_mix_norm, v_w_in, v_dil_q_norm, v_dil_k_norm, v_rel_bias, v_mla_q_a_norm, v_mla_w_q_b, v_mla_kv_a_norm, v_mla_w_kv_b, v_mla_q_norm, v_mla_k_norm, v_out_norm_dil, v_out_norm_mla, v_w_out, v_ffn2_norm, v_ffn2_w_gate, v_ffn2_w_up, v_ffn2_w_down):
    given = dict(locals())
    big_names = [name for name, _ in BIG]
    small_names = [name for name, _, _ in SMALL]

    chip = (2 * lax.axis_index("x") + lax.axis_index("y")).astype(jnp.int32)
    core = lax.axis_index("c").astype(jnp.int32)
    mine = _pack_big({n: given[n] for n in big_names}, BF16)
    gathered = lax.dynamic_update_slice(_gather_weights(mine[0]), mine, (chip, 0, 0))
    wfull = _unpack_big(gathered)
    small = {n: given[n] for n in small_names}

    loss, dx, grads_s, grads_b = _local_step(x[0], loss_target[0], small, wfull)
    loss = lax.psum(loss[0, 0], ("x", "y", "c"))

    packed = _pack_big(grads_b, F32).reshape(N_CHIPS, -1, LANES)
    chip_part = _add_halves(packed, _reduce_cores(packed), core.reshape(1), 1264)
    reduced = _sum_partials(_scatter_chips(chip_part), chip_part, jnp.stack([chip, core]), 1264)
    g_big = _unpack_big(_share_cores(reduced)[None])
    g_small = _unpack_small(_allreduce_small(_pack_small(grads_s)))

    grad, delta, new_m, new_v = {}, {}, {}, {}
    for name, shape in BIG:
        g2 = g_big[name].reshape(shape)
        d_, m_, v_ = _adamw(given[name].reshape(shape), g2, given["m_" + name].reshape(shape),
                            given["v_" + name].reshape(shape), f"adamw_{name}")
        full = given[name].shape
        grad[name], delta[name], new_m[name], new_v[name] = (a.reshape(full) for a in (g2, d_, m_, v_))
    ps = {k: _pack_small({n: given[pre + n] for n in small_names}) for k, pre in (("w", ""), ("m", "m_"), ("v", "v_"))}
    gs_packed = _pack_small(g_small)
    d_s, m_s, v_s = (_unpack_small(a) for a in _adamw(ps["w"], gs_packed, ps["m"], ps["v"], "adamw_small"))
    for name in small_names:
        grad[name], delta[name], new_m[name], new_v[name] = g_small[name], d_s[name], m_s[name], v_s[name]

    return (loss, dx[None], *[grad[n] for n in WEIGHTS], *[delta[n] for n in WEIGHTS],
            *[new_m[n] for n in WEIGHTS], *[new_v[n] for n in WEIGHTS])
```

```python
import functools

import numpy as np
import jax
import jax.numpy as jnp
from jax import lax
from jax.experimental import pallas as pl
from jax.experimental.pallas import tpu as pltpu

F32 = jnp.float32
BF16 = jnp.bfloat16

D_MODEL = 1024
D_FF = 2816
N_CHIPS = 4
DIL_HEADS = 8
DIL_HD = 64
DIL_WIDTH = 512
DIL_DILATIONS = (1, 4, 16)
DIL_W = 128
QB = 128
MLA_HEADS = 4
MLA_NOPE = 128
MLA_ROPE = 64
MLA_QK = 192
MLA_V = 128
MLA_Q_RANK = 256
MLA_KV_RANK = 128
ROPE_BASE = 10000.0
REL_BUCKETS = 32
REL_MAX_DIST = 2048
FFN_RESID = 0.5
EPS = 1e-6
NEG = -1e30
LANES = 128

ADAM_LR = 0.001
ADAM_B1 = 0.9
ADAM_B2 = 0.999
ADAM_EPS = 1e-08
ADAM_WD = 0.01
ADAM_STEP = 10

NT = (((1,), (1,)), ((), ()))
NN = (((1,), (0,)), ((), ()))
TN = (((0,), (0,)), ((), ()))

BIG = (
    ("ffn1_w_gate", (D_MODEL, D_FF // N_CHIPS)),
    ("ffn1_w_up", (D_MODEL, D_FF // N_CHIPS)),
    ("ffn1_w_down", (D_FF // N_CHIPS, D_MODEL)),
    ("w_in", (D_MODEL, 1984 // N_CHIPS)),
    ("mla_w_q_b", (MLA_Q_RANK, MLA_QK)),
    ("mla_w_kv_b", (MLA_KV_RANK, MLA_NOPE + MLA_V)),
    ("w_out", (D_MODEL // N_CHIPS, D_MODEL)),
    ("ffn2_w_gate", (D_MODEL, D_FF // N_CHIPS)),
    ("ffn2_w_up", (D_MODEL, D_FF // N_CHIPS)),
    ("ffn2_w_down", (D_FF // N_CHIPS, D_MODEL)),
)
SMALL = (
    ("ffn1_norm", (1, 1024), 8), ("mix_norm", (1, 1024), 8), ("dil_q_norm", (1, 64), 1),
    ("dil_k_norm", (1, 64), 1), ("rel_bias", (8, 32), 2), ("mla_q_a_norm", (1, 256), 2),
    ("mla_kv_a_norm", (1, 128), 1), ("mla_q_norm", (1, 192), 2), ("mla_k_norm", (1, 192), 2),
    ("out_norm_dil", (1, 512), 4), ("out_norm_mla", (1, 512), 4), ("ffn2_norm", (1, 1024), 8),
)
SMALL_ROWS = 48
WEIGHTS = ("ffn1_norm", "ffn1_w_gate", "ffn1_w_up", "ffn1_w_down", "mix_norm", "w_in", "dil_q_norm",
           "dil_k_norm", "rel_bias", "mla_q_a_norm", "mla_w_q_b", "mla_kv_a_norm", "mla_w_kv_b",
           "mla_q_norm", "mla_k_norm", "out_norm_dil", "out_norm_mla", "w_out", "ffn2_norm",
           "ffn2_w_gate", "ffn2_w_up", "ffn2_w_down")


def _pcall(body, **kw):
    return pl.pallas_call(body, **kw)


def _cparams(*sem):
    return pltpu.CompilerParams(dimension_semantics=sem)


def _sds(shape, dtype):
    return jax.ShapeDtypeStruct(shape, dtype)


def _dot(a, b, dn):
    return lax.dot_general(a, b, dn, preferred_element_type=F32)


def _rms_fwd(x, g, out_dtype, name, tm):
    n, d = x.shape
    tm = min(tm, n)

    def body(x_ref, g_ref, o_ref):
        xf = x_ref[...].astype(F32)
        r = lax.rsqrt(jnp.mean(xf * xf, axis=-1, keepdims=True) + EPS)
        o_ref[...] = (xf * r * g_ref[...]).astype(o_ref.dtype)

    return _pcall(
        body, name=name, grid=(n // tm,),
        in_specs=[pl.BlockSpec((tm, d), lambda i: (i, 0)), pl.BlockSpec((1, d), lambda i: (0, 0))],
        out_specs=pl.BlockSpec((tm, d), lambda i: (i, 0)),
        out_shape=_sds((n, d), out_dtype), compiler_params=_cparams("parallel"))(x, g)


def _rms_bwd(dys, x, g, res, name, tm):
    n, d = x.shape
    tm = min(tm, n)
    nd = len(dys)
    has_res = res is not None

    def body(*refs):
        dy_refs = refs[:nd]
        x_ref, g_ref = refs[nd], refs[nd + 1]
        res_ref = refs[nd + 2] if has_res else None
        dx_ref, dg_ref = refs[-2], refs[-1]
        dy = dy_refs[0][...].astype(F32)
        for r_ in dy_refs[1:]:
            dy = dy + r_[...].astype(F32)
        xf = x_ref[...].astype(F32)
        r = lax.rsqrt(jnp.mean(xf * xf, axis=-1, keepdims=True) + EPS)
        xh = xf * r
        dxh = dy * g_ref[...]
        dx = r * (dxh - xh * jnp.mean(dxh * xh, axis=-1, keepdims=True))
        if has_res:
            dx = dx + res_ref[...]
        dx_ref[...] = dx

        @pl.when(pl.program_id(0) == 0)
        def _():
            dg_ref[...] = jnp.zeros_like(dg_ref)

        dg_ref[...] += jnp.sum(dy * xh, axis=0, keepdims=True)

    row = pl.BlockSpec((tm, d), lambda i: (i, 0))
    vec = pl.BlockSpec((1, d), lambda i: (0, 0))
    ins = list(dys) + [x, g] + ([res] if has_res else [])
    return _pcall(
        body, name=name, grid=(n // tm,),
        in_specs=[row] * nd + [row, vec] + ([row] if has_res else []),
        out_specs=(row, vec),
        out_shape=(_sds((n, d), F32), _sds((1, d), F32)),
        compiler_params=_cparams("arbitrary"))(*ins)


def _mm(name, grid, pairs, dn, out_shape, out_spec, acc_shape, res=None, scale=1.0):
    npairs = len(pairs)
    nred = grid[2]
    has_res = res is not None

    def body(*refs):
        ab = refs[:2 * npairs]
        res_ref = refs[2 * npairs] if has_res else None
        o_ref = refs[2 * npairs + int(has_res)]
        acc_ref = refs[-1] if nred > 1 else None
        tot = None
        for p in range(npairs):
            d = _dot(ab[2 * p][...].astype(BF16), ab[2 * p + 1][...].astype(BF16), dn)
            tot = d if tot is None else tot + d

        def finish(v):
            if scale != 1.0:
                v = v * scale
            if has_res:
                v = res_ref[...] + v
            o_ref[...] = v.astype(o_ref.dtype)

        if nred == 1:
            finish(tot)
        else:
            r = pl.program_id(2)

            @pl.when(r == 0)
            def _():
                acc_ref[...] = tot

            @pl.when(r > 0)
            def _():
                acc_ref[...] += tot

            @pl.when(r == nred - 1)
            def _():
                finish(acc_ref[...])

    ins, specs = [], []
    for a, a_spec, b, b_spec in pairs:
        ins += [a, b]
        specs += [a_spec, b_spec]
    if has_res:
        ins.append(res[0])
        specs.append(res[1])
    return _pcall(
        body, name=name, grid=grid, in_specs=specs, out_specs=out_spec, out_shape=out_shape,
        scratch_shapes=[pltpu.VMEM(acc_shape, F32)] if nred > 1 else [],
        compiler_params=_cparams("parallel", "parallel", "arbitrary"))(*ins)


def _ffn_up(h, wg, wu, name, tm):
    t, d = h.shape
    nc, _, fs = wg.shape
    tm = min(tm, t)

    def body(h_ref, wg_ref, wu_ref, g_ref, u_ref, a_ref):
        hh = h_ref[...]
        gate = _dot(hh, wg_ref[...], NN)
        up = _dot(hh, wu_ref[...], NN)
        g_ref[...] = gate.astype(BF16)
        u_ref[...] = up.astype(BF16)
        a_ref[...] = (gate * jax.nn.sigmoid(gate) * up).astype(BF16)

    wspec = pl.BlockSpec((None, d, fs), lambda c, i: (c, 0, 0))
    ospec = pl.BlockSpec((None, tm, fs), lambda c, i: (c, i, 0))
    osd = _sds((nc, t, fs), BF16)
    return _pcall(
        body, name=name, grid=(nc, t // tm),
        in_specs=[pl.BlockSpec((tm, d), lambda c, i: (i, 0)), wspec, wspec],
        out_specs=(ospec, ospec, ospec), out_shape=(osd, osd, osd),
        compiler_params=_cparams("parallel", "parallel"))(h, wg, wu)


def _ffn_dact(dy, wd, gate, up, name, tm):
    t, d = dy.shape
    nc, fs, _ = wd.shape
    tm = min(tm, t)

    def body(dy_ref, wd_ref, g_ref, u_ref, dg_ref, du_ref):
        da = _dot(dy_ref[...].astype(BF16), wd_ref[...], NT) * FFN_RESID
        gate = g_ref[...].astype(F32)
        up = u_ref[...].astype(F32)
        sig = jax.nn.sigmoid(gate)
        dg_ref[...] = (da * up * (sig * (1.0 + gate * (1.0 - sig)))).astype(BF16)
        du_ref[...] = (da * (gate * sig)).astype(BF16)

    cspec = pl.BlockSpec((None, tm, fs), lambda c, i: (c, i, 0))
    osd = _sds((nc, t, fs), BF16)
    return _pcall(
        body, name=name, grid=(nc, t // tm),
        in_specs=[pl.BlockSpec((tm, d), lambda c, i: (i, 0)),
                  pl.BlockSpec((None, fs, d), lambda c, i: (c, 0, 0)), cspec, cspec],
        out_specs=(cspec, cspec), out_shape=(osd, osd),
        compiler_params=_cparams("parallel", "parallel"))(dy, wd, gate, up)


def _ffn_fwd(x, g, wg, wu, wd, tag):
    t = x.shape[0]
    nc, _, fs = wg.shape
    tm = min(512, t)
    h = _rms_fwd(x, g, BF16, f"{tag}_norm", 512)
    gate, up, act = _ffn_up(h, wg, wu, f"{tag}_up", 1024)
    pairs = [(act, pl.BlockSpec((None, tm, fs), lambda i, j, r, c=c: (c, i, 0)),
              wd, pl.BlockSpec((None, fs, D_MODEL), lambda i, j, r, c=c: (c, 0, 0))) for c in range(nc)]
    row = pl.BlockSpec((tm, D_MODEL), lambda i, j, r: (i, 0))
    y = _mm(f"{tag}_down", (t // tm, 1, 1), pairs, NN, _sds((t, D_MODEL), F32), row, (tm, D_MODEL),
            res=(x, row), scale=FFN_RESID)
    return y, (h, gate, up, act)


def _ffn_bwd(dy, x, g, wg, wu, wd, saved, tag):
    h, gate, up, act = saved
    t = x.shape[0]
    nc, _, fs = wg.shape
    tm = min(512, t)
    tk = min(2048, t)
    dgate, dup = _ffn_dact(dy, wd, gate, up, f"{tag}_dact", 1024)
    tok_c = pl.BlockSpec((None, tk, fs), lambda c, j, r: (c, r, 0))
    tok_d = pl.BlockSpec((tk, D_MODEL), lambda c, j, r: (r, 0))
    dwd = _mm(f"{tag}_dwd", (nc, 1, t // tk), [(act, tok_c, dy, tok_d)], TN,
              _sds((nc, fs, D_MODEL), F32), pl.BlockSpec((None, fs, D_MODEL), lambda c, j, r: (c, 0, 0)),
              (fs, D_MODEL), scale=FFN_RESID)
    wout = pl.BlockSpec((None, D_MODEL, fs), lambda c, j, r: (c, 0, 0))
    dwg = _mm(f"{tag}_dwg", (nc, 1, t // tk), [(h, tok_d, dgate, tok_c)], TN,
              _sds((nc, D_MODEL, fs), F32), wout, (D_MODEL, fs))
    dwu = _mm(f"{tag}_dwu", (nc, 1, t // tk), [(h, tok_d, dup, tok_c)], TN,
              _sds((nc, D_MODEL, fs), F32), wout, (D_MODEL, fs))
    pairs = []
    for c in range(nc):
        a_spec = pl.BlockSpec((None, tm, fs), lambda i, j, r, c=c: (c, i, 0))
        w_spec = pl.BlockSpec((None, D_MODEL, fs), lambda i, j, r, c=c: (c, 0, 0))
        pairs += [(dgate, a_spec, wg, w_spec), (dup, a_spec, wu, w_spec)]
    dh = _mm(f"{tag}_dh", (t // tm, 1, 1), pairs, NT,
             _sds((t, D_MODEL), F32), pl.BlockSpec((tm, D_MODEL), lambda i, j, r: (i, 0)), (tm, D_MODEL))
    dx, dg = _rms_bwd([dh], x, g, dy, f"{tag}_dnorm", 512)
    return dx, dg, dwg, dwu, dwd


def _mm_simple(name, a, b, dn, out_dtype, tm=512, tk=512, res=None, scale=1.0):
    if dn == TN:
        k, m = a.shape
        n = b.shape[1]
        tk = min(tk, k)
        return _mm(name, (1, 1, k // tk),
                   [(a, pl.BlockSpec((tk, m), lambda i, j, r: (r, 0)), b, pl.BlockSpec((tk, n), lambda i, j, r: (r, 0)))],
                   TN, _sds((m, n), out_dtype), pl.BlockSpec((m, n), lambda i, j, r: (0, 0)), (m, n), scale=scale)
    m, k = a.shape
    n = b.shape[1] if dn == NN else b.shape[0]
    tm = min(tm, m)
    row = pl.BlockSpec((tm, n), lambda i, j, r: (i, 0))
    return _mm(name, (m // tm, 1, 1),
               [(a, pl.BlockSpec((tm, k), lambda i, j, r: (i, 0)), b, pl.BlockSpec(b.shape, lambda i, j, r: (0, 0)))],
               dn, _sds((m, n), out_dtype), row, (tm, n), res=None if res is None else (res, row), scale=scale)


def _t5_bucket(dist):
    max_exact = REL_BUCKETS // 2
    d = np.maximum(dist, 1).astype(np.float32)
    large = max_exact + (np.log(d / max_exact) / np.log(REL_MAX_DIST / max_exact)
                         * (REL_BUCKETS - max_exact)).astype(np.int32)
    large = np.minimum(large, REL_BUCKETS - 1)
    return np.where(dist < max_exact, dist, large).astype(np.int32)


def _bucket_tiles():
    i = np.arange(QB)[:, None]
    j = np.arange(QB + DIL_W)[None, :]
    delta = np.clip(i + DIL_W - j, 0, None)
    return np.stack([_t5_bucket(delta * dil) for dil in DIL_DILATIONS]).astype(np.int32)


def _bias_tiles(rel_bias):
    buckets = jnp.asarray(_bucket_tiles())

    def body(rb_ref, bk_ref, o_ref):
        bk = bk_ref[...]
        for h in range(DIL_HEADS):
            def pick(b, tile):
                return jnp.where(bk == b, rb_ref[h, b], tile)

            o_ref[h] = lax.fori_loop(0, REL_BUCKETS, pick, jnp.zeros((QB, QB + DIL_W), F32))

    return _pcall(
        body, name="dil_bias_tiles", grid=(3,),
        in_specs=[pl.BlockSpec(memory_space=pltpu.SMEM),
                  pl.BlockSpec((None, QB, QB + DIL_W), lambda b: (b, 0, 0))],
        out_specs=pl.BlockSpec((None, DIL_HEADS, QB, QB + DIL_W), lambda b: (b, 0, 0, 0)),
        out_shape=_sds((3, DIL_HEADS, QB, QB + DIL_W), F32),
        compiler_params=_cparams("parallel"))(rel_bias, buckets)


def _bias_grad(dtiles):
    buckets = jnp.asarray(_bucket_tiles())

    def body(dt_ref, bk_ref, o_ref):
        def one(b, carry):
            hit = [bk_ref[br] == b for br in range(3)]
            for h in range(DIL_HEADS):
                tot = jnp.zeros((), F32)
                for br in range(3):
                    tot = tot + jnp.sum(jnp.where(hit[br], dt_ref[br, h], 0.0))
                o_ref[h, b] = tot
            return carry

        lax.fori_loop(0, REL_BUCKETS, one, 0)

    return _pcall(
        body, name="dil_bias_grad",
        in_specs=[pl.BlockSpec(memory_space=pltpu.VMEM), pl.BlockSpec(memory_space=pltpu.VMEM)],
        out_specs=pl.BlockSpec(memory_space=pltpu.SMEM),
        out_shape=_sds((DIL_HEADS, REL_BUCKETS), F32))(dtiles, buckets)


def _split_heads(a, lo):
    zero = jnp.zeros_like(a)
    return jnp.concatenate([jnp.where(lo, a, zero), jnp.where(lo, zero, a)], axis=0)


def _side_by_side(a):
    n = a.shape[0] // 2
    return jnp.concatenate([a[:n], a[n:]], axis=1)


def _band_masks(prev_ok):
    ii = lax.broadcasted_iota(jnp.int32, (2 * QB, QB), 0) & (QB - 1)
    jj = lax.broadcasted_iota(jnp.int32, (2 * QB, QB), 1)
    return jj <= ii, jj >= ii + jnp.where(prev_ok, 0, QB)


def _dil_view(a, dil):
    t, w = a.shape
    return a.reshape(t // dil, dil * w)


def _dil_fwd(q, k, v, bias, dil, name):
    t, w = q.shape
    npair = w // LANES
    nl = t // dil // QB
    scale = DIL_HD ** -0.5

    def body(q_ref, kc_ref, kp_ref, vc_ref, vp_ref, b_ref, o_ref, lse_ref):
        nn = pl.program_id(1)
        lo = lax.broadcasted_iota(jnp.int32, (QB, LANES), 1) < DIL_HD
        lo2 = lax.broadcasted_iota(jnp.int32, (2 * QB, LANES), 1) < DIL_HD
        ii = lax.broadcasted_iota(jnp.int32, (2 * QB, 2 * QB), 0) & (QB - 1)
        jj = lax.broadcasted_iota(jnp.int32, (2 * QB, 2 * QB), 1)
        first_key = jnp.maximum(ii, jnp.where(nn != 0, 0, QB))
        valid = (jj >= first_key) & (jj <= ii + QB)
        for p in range(npair):
            cols = slice(p * LANES, (p + 1) * LANES)
            qq = _split_heads(q_ref[:, cols], lo)
            kk = jnp.concatenate([kp_ref[:, cols], kc_ref[:, cols]], axis=0)
            vv = jnp.concatenate([vp_ref[:, cols], vc_ref[:, cols]], axis=0)
            s = jnp.where(valid, _dot(qq, kk, NT) * scale + b_ref[p], NEG)
            m = jnp.max(s, axis=-1, keepdims=True)
            e = jnp.exp(s - m)
            den = jnp.sum(e, axis=-1, keepdims=True)
            pn = (e * (1.0 / den)).astype(BF16)
            o_ref[:, cols] = _dot(_side_by_side(pn), _split_heads(vv, lo2), NN)
            lse = m + jnp.log(den)
            lse_ref[:, cols] = jnp.where(lo, lse[:QB], lse[QB:])

    cur = pl.BlockSpec((QB, w), lambda r, n: (n, r))
    prev = pl.BlockSpec((QB, w), lambda r, n: (jnp.maximum(n - 1, 0), r))
    sd = _sds((t // dil, dil * w), F32)
    o, lse = _pcall(
        body, name=name, grid=(dil, nl),
        in_specs=[cur, cur, prev, cur, prev, pl.BlockSpec((npair, 2 * QB, 2 * QB), lambda r, n: (0, 0, 0))],
        out_specs=(cur, cur), out_shape=(sd, sd),
        compiler_params=_cparams("parallel", "parallel"))(*[_dil_view(a, dil) for a in (q, k, k, v, v)], bias)
    return o.reshape(t, w), lse.reshape(t, w)


def _dil_bwd(q, k, v, do, stats, bias, dil, name):
    t, w = q.shape
    npair = w // LANES
    nl = t // dil // QB
    scale = DIL_HD ** -0.5

    def body(qc_ref, qn_ref, doc_ref, don_ref, sc_ref, sn_ref, k_ref, v_ref, b_ref,
             dq_ref, dk_ref, dv_ref, db_ref, carry):
        r, nn = pl.program_id(0), pl.program_id(1)
        lo = lax.broadcasted_iota(jnp.int32, (QB, LANES), 1) < DIL_HD
        cur_ok, prev_ok = _band_masks(nn + 1 < nl)

        @pl.when((r == 0) & (nn == 0))
        def _():
            db_ref[...] = jnp.zeros_like(db_ref)
            carry[...] = jnp.zeros_like(carry)

        for p in range(npair):
            cols = slice(p * LANES, (p + 1) * LANES)
            kp, vp = k_ref[:, cols], v_ref[:, cols]
            k2 = _split_heads(kp, lo)

            def column(ref, lane):
                first = p * LANES + lane
                return jnp.concatenate([ref[:, first:first + 1], ref[:, first + DIL_HD:first + DIL_HD + 1]], axis=0)

            def side(q_ref, do_ref, s_ref, bias, ok):
                qq = _split_heads(q_ref[:, cols], lo)
                dd = _split_heads(do_ref[:, cols], lo)
                s = jnp.where(ok, _dot(qq, kp, NT) * scale + bias, NEG)
                prob = jnp.exp(s - column(s_ref, 0))
                ds = prob * (_dot(dd, vp, NT) - column(s_ref, DIL_HD // 2))
                return qq, dd, prob.astype(BF16), ds

            q1, d1, p1, ds1 = side(qc_ref, doc_ref, sc_ref, b_ref[p, :, QB:], cur_ok)
            q2, d2, p2, ds2 = side(qn_ref, don_ref, sn_ref, b_ref[p, :, :QB], prev_ok)
            ds1b, ds2b = ds1.astype(BF16), ds2.astype(BF16)
            dq_ref[:, cols] = carry[:, cols] + _dot(_side_by_side(ds1b), k2, NN) * scale
            carry[:, cols] = _dot(_side_by_side(ds2b), k2, NN) * scale
            dk_ref[:, cols] = _dot(jnp.concatenate([ds1b, ds2b], axis=0), jnp.concatenate([q1, q2], axis=0), TN) * scale
            dv_ref[:, cols] = _dot(jnp.concatenate([p1, p2], axis=0), jnp.concatenate([d1, d2], axis=0), TN)
            db_ref[p, :, QB:] += ds1
            db_ref[p, :, :QB] += ds2

    cur = pl.BlockSpec((QB, w), lambda r, n: (n, r))
    nxt = pl.BlockSpec((QB, w), lambda r, n: (jnp.minimum(n + 1, nl - 1), r))
    tile = pl.BlockSpec((npair, 2 * QB, 2 * QB), lambda r, n: (0, 0, 0))
    sd = _sds((t // dil, dil * w), F32)
    views = [_dil_view(a, dil) for a in (q, q, do, do, stats, stats, k, v)]
    dq, dk, dv, db = _pcall(
        body, name=name, grid=(dil, nl),
        in_specs=[cur, nxt, cur, nxt, cur, nxt, cur, cur, tile],
        out_specs=(cur, cur, cur, tile),
        out_shape=(sd, sd, sd, _sds((npair, 2 * QB, 2 * QB), F32)),
        scratch_shapes=[pltpu.VMEM((QB, w), F32)],
        compiler_params=_cparams("arbitrary", "arbitrary"))(*views, bias)
    return dq.reshape(t, w), dk.reshape(t, w), dv.reshape(t, w), db


def _head_sum_matrix(scale):
    idx = np.arange(DIL_WIDTH) // DIL_HD
    return jnp.asarray((idx[:, None] == idx[None, :]).astype(np.float32) * scale, BF16)


def _head_sum(x, mat):
    hi = x.astype(BF16)
    lo = (x - hi.astype(F32)).astype(BF16)
    return _dot(hi, mat, NN) + _dot(lo, mat, NN)


def _dil_merge(outs, lses, g, tm):
    t, w = outs[0].shape
    tm = min(tm, t)

    def body(o0, o1, o2, l0, l1, l2, g_ref, o_ref, l_ref, n_ref):
        a0, a1, a2 = l0[...], l1[...], l2[...]
        m = jnp.maximum(jnp.maximum(a0, a1), a2)
        e0, e1, e2 = jnp.exp(a0 - m), jnp.exp(a1 - m), jnp.exp(a2 - m)
        den = e0 + e1 + e2
        o = (e0 * o0[...] + e1 * o1[...] + e2 * o2[...]) / den
        o_ref[...] = o
        l_ref[...] = m + jnp.log(den)
        r = lax.rsqrt(jnp.mean(o * o, axis=-1, keepdims=True) + EPS)
        n_ref[...] = (o * r * g_ref[...]).astype(n_ref.dtype)

    spec = pl.BlockSpec((tm, w), lambda i: (i, 0))
    return _pcall(
        body, name="dil_merge", grid=(t // tm,),
        in_specs=[spec] * 6 + [pl.BlockSpec((1, w), lambda i: (0, 0))], out_specs=(spec, spec, spec),
        out_shape=(_sds((t, w), F32), _sds((t, w), F32), _sds((t, w), BF16)),
        compiler_params=_cparams("parallel"))(*outs, *lses, g)


def _dil_stats(do, o, lse, tm):
    t, w = do.shape
    tm = min(tm, t)

    def body(a_ref, b_ref, l_ref, m_ref, o_ref):
        first = (lax.broadcasted_iota(jnp.int32, (tm, w), 1) & (DIL_HD - 1)) < DIL_HD // 2
        o_ref[...] = jnp.where(first, l_ref[...], _head_sum(a_ref[...] * b_ref[...], m_ref[...]))

    spec = pl.BlockSpec((tm, w), lambda i: (i, 0))
    return _pcall(body, name="dil_stats", grid=(t // tm,),
                  in_specs=[spec, spec, spec, pl.BlockSpec((w, w), lambda i: (0, 0))], out_specs=spec,
                  out_shape=_sds((t, w), F32), compiler_params=_cparams("parallel"))(do, o, lse, _head_sum_matrix(1.0))


def _head_norm_fwd(x, col, g, name, tm):
    t = x.shape[0]
    w = DIL_WIDTH
    tm = min(tm, t)

    def body(x_ref, g_ref, m_ref, o_ref):
        xf = x_ref[...]
        r = lax.rsqrt(_head_sum(xf * xf, m_ref[...]) + EPS)
        o_ref[...] = (xf * r * g_ref[...]).astype(o_ref.dtype)

    return _pcall(
        body, name=name, grid=(t // tm,),
        in_specs=[pl.BlockSpec((tm, w), lambda i: (i, col)), pl.BlockSpec((1, w), lambda i: (0, 0)),
                  pl.BlockSpec((w, w), lambda i: (0, 0))],
        out_specs=pl.BlockSpec((tm, w), lambda i: (i, 0)), out_shape=_sds((t, w), BF16),
        compiler_params=_cparams("parallel"))(x, g, _head_sum_matrix(1.0 / DIL_HD))


def _head_norm_bwd(dys, x, col, g, name, tm):
    t = x.shape[0]
    w = DIL_WIDTH
    tm = min(tm, t)
    nd = len(dys)
    nt = t // tm
    lane = np.arange(w) % DIL_HD
    fold = jnp.asarray((lane[:, None] == lane[None, :]).astype(np.float32))

    def body(*refs):
        x_ref, g_ref, m_ref, f_ref = refs[nd:nd + 4]
        dx_ref, dg_ref = refs[-2], refs[-1]
        dy = refs[0][...]
        for r_ in refs[1:nd]:
            dy = dy + r_[...]
        xf = x_ref[...]
        mat = m_ref[...]
        r = lax.rsqrt(_head_sum(xf * xf, mat) + EPS)
        xh = xf * r
        dxh = dy * g_ref[...]
        dx_ref[...] = r * (dxh - xh * _head_sum(dxh * xh, mat))

        @pl.when(pl.program_id(0) == 0)
        def _():
            dg_ref[...] = jnp.zeros_like(dg_ref)

        dg_ref[...] += jnp.sum(dy * xh, axis=0, keepdims=True)

        @pl.when(pl.program_id(0) == nt - 1)
        def _():
            per_lane = jnp.broadcast_to(dg_ref[...], (8, w))
            dg_ref[...] = lax.dot_general(per_lane, f_ref[...], NN, precision=lax.Precision.HIGHEST,
                                          preferred_element_type=F32)[0:1]

    row = pl.BlockSpec((tm, w), lambda i: (i, 0))
    vec = pl.BlockSpec((1, w), lambda i: (0, 0))
    sq = pl.BlockSpec((w, w), lambda i: (0, 0))
    return _pcall(
        body, name=name, grid=(nt,),
        in_specs=[row] * nd + [pl.BlockSpec((tm, w), lambda i: (i, col)), vec, sq, sq],
        out_specs=(row, vec), out_shape=(_sds((t, w), F32), _sds((1, w), F32)),
        compiler_params=_cparams("arbitrary"))(*dys, x, g, _head_sum_matrix(1.0 / DIL_HD), fold)


def _rowdot(a, b, name, tm):
    n, d = a.shape
    tm = min(tm, n)

    def body(a_ref, b_ref, o_ref):
        o_ref[...] = jnp.sum(a_ref[...].astype(F32) * b_ref[...].astype(F32), axis=-1, keepdims=True)

    spec = pl.BlockSpec((tm, d), lambda i: (i, 0))
    return _pcall(body, name=name, grid=(n // tm,), in_specs=[spec, spec],
                  out_specs=pl.BlockSpec((tm, 1), lambda i: (i, 0)), out_shape=_sds((n, 1), F32),
                  compiler_params=_cparams("parallel"))(a, b)


def _add3(a, b, c, name, tm):
    n, d = a.shape
    tm = min(tm, n)

    def body(a_ref, b_ref, c_ref, o_ref):
        o_ref[...] = a_ref[...] + b_ref[...] + c_ref[...]

    spec = pl.BlockSpec((tm, d), lambda i: (i, 0))
    return _pcall(body, name=name, grid=(n // tm,), in_specs=[spec] * 3, out_specs=spec,
                  out_shape=_sds((n, d), F32), compiler_params=_cparams("parallel"))(a, b, c)


def _rope_tables(t):
    inv = ROPE_BASE ** (-np.arange(0, MLA_ROPE, 2, dtype=np.float64) / MLA_ROPE)
    ang = np.arange(t, dtype=np.float64)[:, None] * inv[None, :]
    cos, sin = np.cos(ang), np.sin(ang)
    return (jnp.asarray(np.concatenate([cos, cos], 1), F32), jnp.asarray(np.concatenate([-sin, sin], 1), F32))


def _half_swap():
    p = np.zeros((MLA_ROPE, MLA_ROPE), np.float32)
    half = MLA_ROPE // 2
    for i in range(MLA_ROPE):
        p[(i + half) % MLA_ROPE, i] = 1.0
    return jnp.asarray(p)


def _mla_qk_fwd(x, g, cos_t, sin_t, scale, name, tm):
    n, d = x.shape
    t = cos_t.shape[0]
    tm = min(tm, t)
    nt = t // tm
    swap = _half_swap()

    def body(x_ref, g_ref, c_ref, s_ref, p_ref, o_ref):
        xf = x_ref[...]
        r = lax.rsqrt(jnp.mean(xf * xf, axis=-1, keepdims=True) + EPS)
        y = xf * r * g_ref[...]
        yr = y[:, MLA_NOPE:]
        sw = lax.dot_general(yr, p_ref[...], NN, precision=lax.Precision.HIGHEST, preferred_element_type=F32)
        o_ref[:, :MLA_NOPE] = (y[:, :MLA_NOPE] * scale).astype(o_ref.dtype)
        o_ref[:, MLA_NOPE:] = ((yr * c_ref[...] + sw * s_ref[...]) * scale).astype(o_ref.dtype)

    row = pl.BlockSpec((tm, d), lambda i: (i, 0))
    tab = pl.BlockSpec((tm, MLA_ROPE), lambda i: (i % nt, 0))
    return _pcall(
        body, name=name, grid=(n // tm,),
        in_specs=[row, pl.BlockSpec((1, d), lambda i: (0, 0)), tab, tab,
                  pl.BlockSpec((MLA_ROPE, MLA_ROPE), lambda i: (0, 0))],
        out_specs=row, out_shape=_sds((n, d), BF16),
        compiler_params=_cparams("parallel"))(x, g, cos_t, sin_t, swap)


def _mla_qk_bwd(dy, x, g, cos_t, sin_t, scale, name, tm):
    n, d = x.shape
    t = cos_t.shape[0]
    tm = min(tm, t)
    nt = t // tm
    swap_t = _half_swap().T

    def body(dy_ref, x_ref, g_ref, c_ref, s_ref, p_ref, dx_ref, dg_ref):
        xf = x_ref[...]
        gg = g_ref[...]
        r = lax.rsqrt(jnp.mean(xf * xf, axis=-1, keepdims=True) + EPS)
        xh = xf * r
        dyf = dy_ref[...] * scale
        dyr = dyf[:, MLA_NOPE:]
        back = lax.dot_general(dyr * s_ref[...], p_ref[...], NN, precision=lax.Precision.HIGHEST,
                               preferred_element_type=F32)
        dn_n = dyf[:, :MLA_NOPE]
        dn_r = dyr * c_ref[...] + back
        xh_n, xh_r = xh[:, :MLA_NOPE], xh[:, MLA_NOPE:]
        dxh_n = dn_n * gg[:, :MLA_NOPE]
        dxh_r = dn_r * gg[:, MLA_NOPE:]
        mean = (jnp.sum(dxh_n * xh_n, axis=-1, keepdims=True)
                + jnp.sum(dxh_r * xh_r, axis=-1, keepdims=True)) * (1.0 / d)
        dx_ref[:, :MLA_NOPE] = r * (dxh_n - xh_n * mean)
        dx_ref[:, MLA_NOPE:] = r * (dxh_r - xh_r * mean)

        @pl.when(pl.program_id(0) == 0)
        def _():
            dg_ref[...] = jnp.zeros_like(dg_ref)

        dg_ref[:, :MLA_NOPE] += jnp.sum(dn_n * xh_n, axis=0, keepdims=True)
        dg_ref[:, MLA_NOPE:] += jnp.sum(dn_r * xh_r, axis=0, keepdims=True)

    row = pl.BlockSpec((tm, d), lambda i: (i, 0))
    vec = pl.BlockSpec((1, d), lambda i: (0, 0))
    tab = pl.BlockSpec((tm, MLA_ROPE), lambda i: (i % nt, 0))
    return _pcall(
        body, name=name, grid=(n // tm,),
        in_specs=[row, row, vec, tab, tab, pl.BlockSpec((MLA_ROPE, MLA_ROPE), lambda i: (0, 0))],
        out_specs=(row, vec), out_shape=(_sds((n, d), F32), _sds((1, d), F32)),
        compiler_params=_cparams("arbitrary"))(dy, x, g, cos_t, sin_t, swap_t)


def _causal_mask(i, j, tq, tk, width):
    row = i * tq + lax.broadcasted_iota(jnp.int32, (tq, width), 0)
    col = j * tk + lax.broadcasted_iota(jnp.int32, (tq, width), 1)
    return col <= row


def _causal_steps(nq, nk, tq, tk, q_major):
    if q_major:
        groups = [[(i, j) for j in range((i * tq + tq - 1) // tk + 1)] for i in range(nq)]
        nunit = tk // tq if tk % tq == 0 else 1
    else:
        groups = [[(i, j) for i in range((j * tk) // tq, nq)] for j in range(nk)]
        nunit = tq // tk if tq % tk == 0 else 1
    it, jt, fl = [], [], []
    for g in groups:
        for n, (i, j) in enumerate(g):
            crossing = j * tk + tk - 1 > i * tq
            if q_major:
                unit = tk // nunit
                u = min(nunit, -(-(i * tq + tq - j * tk) // unit)) - 1
            else:
                unit = tq // nunit
                u = max(0, j * tk - i * tq) // unit
            it.append(i)
            jt.append(j)
            fl.append((n == 0) + 2 * (n == len(g) - 1) + 4 * crossing + 8 * (u if crossing else 0))
    return tuple(jnp.asarray(np.array(a, np.int32)) for a in (it, jt, fl)), nunit


def _by_crossing(flags, nunit, update):
    pl.when((flags & 4) == 0)(functools.partial(update, None))
    for u in range(nunit):
        pl.when(((flags & 4) != 0) & ((flags >> 3) == u))(functools.partial(update, u))


def _causal_specs(tq, tk):
    def qs(w):
        return pl.BlockSpec((None, tq, w), lambda h, s, it, jt, fl: (h, it[s], 0))

    def kv(w):
        return pl.BlockSpec((None, tk, w), lambda h, s, it, jt, fl: (h, jt[s], 0))

    return qs, kv


def _mla_fwd(q, k, v, tq, tk):
    nh, t, dq = q.shape
    dv = v.shape[2]
    tq, tk = min(tq, t), min(tk, t)
    tables, nunit = _causal_steps(t // tq, t // tk, tq, tk, True)

    def body(it, jt, fl, q_ref, k_ref, v_ref, o_ref, lse_ref, m_sc, l_sc, acc_sc):
        step = pl.program_id(1)
        i, j, flags = it[step], jt[step], fl[step]

        @pl.when((flags & 1) != 0)
        def _():
            m_sc[...] = jnp.full_like(m_sc, NEG)
            l_sc[...] = jnp.zeros_like(l_sc)
            acc_sc[...] = jnp.zeros_like(acc_sc)

        def update(units):
            wk = tk if units is None else (units + 1) * (tk // nunit)
            s = _dot(q_ref[...], k_ref[:wk, :], NT)
            if units is not None:
                s = jnp.where(_causal_mask(i, j, tq, tk, wk), s, NEG)
            m_prev = m_sc[...]
            m_new = jnp.maximum(m_prev, jnp.max(s, axis=-1, keepdims=True))
            alpha = jnp.exp(m_prev - m_new)
            p = jnp.exp(s - m_new)
            l_sc[...] = alpha * l_sc[...] + jnp.sum(p, axis=-1, keepdims=True)
            acc_sc[...] = alpha * acc_sc[...] + _dot(p.astype(BF16), v_ref[:wk, :], NN)
            m_sc[...] = m_new

        _by_crossing(flags, nunit, update)

        @pl.when((flags & 2) != 0)
        def _():
            o_ref[...] = acc_sc[...] / l_sc[...]
            lse_ref[...] = m_sc[...] + jnp.log(l_sc[...])

    qs, kv = _causal_specs(tq, tk)
    return _pcall(
        body, name="mla_attn_fwd",
        grid_spec=pltpu.PrefetchScalarGridSpec(
            num_scalar_prefetch=3, grid=(nh, tables[0].shape[0]),
            in_specs=[qs(dq), kv(dq), kv(dv)], out_specs=(qs(dv), qs(1)),
            scratch_shapes=[pltpu.VMEM((tq, 1), F32), pltpu.VMEM((tq, 1), F32), pltpu.VMEM((tq, dv), F32)]),
        out_shape=(_sds((nh, t, dv), F32), _sds((nh, t, 1), F32)),
        compiler_params=_cparams("parallel", "arbitrary"))(*tables, q, k, v)


def _mla_bwd(q, k, k_t, v, do, lse_row, dl_row, tq, tk):
    nh, t, dq = q.shape
    dv = v.shape[2]
    tq, tk = min(tq, t), min(tk, t)
    nq = t // tq
    tables, nunit = _causal_steps(nq, t // tk, tq, tk, False)

    def body(it, jt, fl, q_ref, k_ref, kt_ref, v_ref, do_ref, lse_ref, dl_ref, dk_ref, dv_ref, dq_ref, dk_sc, dv_sc):
        step = pl.program_id(1)
        i, j, flags = it[step], jt[step], fl[step]

        def update(units):
            off = 0 if units is None else units * (tq // nunit)
            qq = q_ref[off:, :]
            st = _dot(k_ref[...], qq, NT)
            if units is not None:
                key = j * tk + lax.broadcasted_iota(jnp.int32, (tk, tq - off), 0)
                qry = i * tq + off + lax.broadcasted_iota(jnp.int32, (tk, tq - off), 1)
                st = jnp.where(key <= qry, st, NEG)
            pt = jnp.exp(st - lse_ref[:, off:])
            dob = do_ref[off:, :].astype(BF16)
            dpt = _dot(v_ref[...], dob, NT)
            dst = pt * (dpt - dl_ref[:, off:])
            dsb = dst.astype(BF16)
            dv_part = _dot(pt.astype(BF16), dob, NN)
            dk_part = _dot(dsb, qq, NN)
            dq_part = _dot(kt_ref[...], dsb, NN)

            @pl.when((flags & 1) != 0)
            def _():
                dv_sc[...] = dv_part
                dk_sc[...] = dk_part

            @pl.when((flags & 1) == 0)
            def _():
                dv_sc[...] += dv_part
                dk_sc[...] += dk_part

            if off == 0:
                @pl.when(j == 0)
                def _():
                    dq_ref[i] = dq_part

                @pl.when(j != 0)
                def _():
                    dq_ref[i] += dq_part
            else:
                dq_ref[i, :, off:] += dq_part

        _by_crossing(flags, nunit, update)

        @pl.when((flags & 2) != 0)
        def _():
            dk_ref[...] = dk_sc[...]
            dv_ref[...] = dv_sc[...]

    qs, kv = _causal_specs(tq, tk)
    rowv = pl.BlockSpec((None, 1, tq), lambda h, s, it, jt, fl: (h, 0, it[s]))
    ktv = pl.BlockSpec((None, dq, tk), lambda h, s, it, jt, fl: (h, 0, jt[s]))
    whole = pl.BlockSpec((None, nq, dq, tq), lambda h, s, it, jt, fl: (h, 0, 0, 0))
    return _pcall(
        body, name="mla_attn_bwd",
        grid_spec=pltpu.PrefetchScalarGridSpec(
            num_scalar_prefetch=3, grid=(nh, tables[0].shape[0]),
            in_specs=[qs(dq), kv(dq), ktv, kv(dv), qs(dv), rowv, rowv], out_specs=(kv(dq), kv(dv), whole),
            scratch_shapes=[pltpu.VMEM((tk, dq), F32), pltpu.VMEM((tk, dv), F32)]),
        out_shape=(_sds((nh, t, dq), F32), _sds((nh, t, dv), F32), _sds((nh, nq, dq, tq), F32)),
        compiler_params=_cparams("parallel", "arbitrary"))(*tables, q, k, k_t, v, do, lse_row, dl_row)


def _loss_head(y, target, tm):
    t, d = y.shape
    tm = min(tm, t)
    nt = t // tm

    def body(y_ref, t_ref, dy_ref, loss_ref, acc):
        i = pl.program_id(0)
        err = y_ref[...] - t_ref[...]
        dy_ref[...] = err * (1.0 / d)

        @pl.when(i == 0)
        def _():
            acc[...] = jnp.zeros_like(acc)

        acc[...] += jnp.sum(err * err, axis=0, keepdims=True)

        @pl.when(i == nt - 1)
        def _():
            loss_ref[0, 0] = jnp.sum(acc[...]) * (0.5 / d)

    spec = pl.BlockSpec((tm, d), lambda i: (i, 0))
    return _pcall(
        body, name="loss_head", grid=(nt,), in_specs=[spec, spec],
        out_specs=(spec, pl.BlockSpec(memory_space=pltpu.SMEM)),
        out_shape=(_sds((t, d), F32), _sds((1, 1), F32)),
        scratch_shapes=[pltpu.VMEM((1, d), F32)],
        compiler_params=_cparams("arbitrary"))(y, target)


def _adamw(w, g, m, v, name):
    r, c = w.shape
    tr = r
    for cand in (256, 128, 64, 32, 16, 8):
        if r % cand == 0:
            tr = cand
            break

    def body(w_ref, g_ref, m_ref, v_ref, d_ref, nm_ref, nv_ref):
        gg = g_ref[...]
        nm = ADAM_B1 * m_ref[...] + (1.0 - ADAM_B1) * gg
        nv = ADAM_B2 * v_ref[...] + (1.0 - ADAM_B2) * (gg * gg)
        m_hat = nm / (1.0 - ADAM_B1 ** ADAM_STEP)
        v_hat = nv / (1.0 - ADAM_B2 ** ADAM_STEP)
        d_ref[...] = -ADAM_LR * (m_hat / (jnp.sqrt(v_hat) + ADAM_EPS) + ADAM_WD * w_ref[...])
        nm_ref[...] = nm
        nv_ref[...] = nv

    spec = pl.BlockSpec((tr, c), lambda i: (i, 0))
    sd = _sds((r, c), F32)
    return _pcall(body, name=name, grid=(r // tr,), in_specs=[spec] * 4, out_specs=(spec,) * 3,
                  out_shape=(sd, sd, sd), compiler_params=_cparams("parallel"))(w, g, m, v)


MESH_ID = pl.DeviceIdType.MESH
HBM_SPEC = pl.BlockSpec(memory_space=pltpu.HBM)


def _place():
    return lax.axis_index("x"), lax.axis_index("y"), lax.axis_index("c")


def _other_chips(x, y):
    return [(1 - x, y), (x, 1 - y), (1 - x, 1 - y)]


def _remote(src, dst, send_sems, recv_sems, k, to):
    return pltpu.make_async_remote_copy(src_ref=src, dst_ref=dst, send_sem=send_sems.at[k], recv_sem=recv_sems.at[k],
                                        device_id=to, device_id_type=MESH_ID)


D2D_SPLIT = 16
ICI_SPLIT = 4


def _chunks(rows, n):
    assert rows % n == 0
    return [(i * (rows // n), rows // n) for i in range(n)]


def _gather_weights(packed):
    rows, lanes = packed.shape
    half = rows // 2

    def body(src, out, send_sems, recv_sems):
        x, y, c = _place()
        me = 2 * x + y
        sibling = (x, y, 1 - c)
        chips = _other_chips(x, y)

        def part(chip, core, lo=0, n=half):
            return out.at[chip, pl.ds(core * half + lo, n), :]

        for k, (cx, cy) in enumerate(chips):
            for lo, n in _chunks(half, ICI_SPLIT):
                _remote(src.at[pl.ds(c * half + lo, n), :], part(me, c, lo, n), send_sems, recv_sems, k,
                        (cx, cy, c)).start()
        for k, (cx, cy) in enumerate(chips):
            got = part(2 * cx + cy, c)
            _remote(got, got, send_sems, recv_sems, k, (x, y, c)).wait_recv()
            for lo, n in _chunks(half, D2D_SPLIT):
                piece = part(2 * cx + cy, c, lo, n)
                _remote(piece, piece, send_sems, recv_sems, 3 + k, sibling).start()
        for k, (cx, cy) in enumerate(chips):
            got = part(2 * cx + cy, 1 - c)
            _remote(got, got, send_sems, recv_sems, 3 + k, (x, y, c)).wait_recv()
        for k in range(6):
            sent = part(me, c)
            _remote(sent, sent, send_sems, recv_sems, k, (x, y, c)).wait_send()

    return _pcall(
        body, name="gather_weights", in_specs=[HBM_SPEC], out_specs=HBM_SPEC,
        out_shape=_sds((N_CHIPS, rows, lanes), packed.dtype),
        scratch_shapes=[pltpu.SemaphoreType.DMA((6,)), pltpu.SemaphoreType.DMA((6,))],
    )(packed)


def _reduce_cores(grads):
    nchip, rows, lanes = grads.shape
    half = rows // 2

    def body(g, theirs, send_sems, recv_sems):
        x, y, c = _place()
        for j in range(nchip):
            for lo, n in _chunks(half, D2D_SPLIT):
                _remote(g.at[j, pl.ds((1 - c) * half + lo, n), :], theirs.at[j, pl.ds(lo, n), :],
                        send_sems, recv_sems, 0, (x, y, 1 - c)).start()
        _remote(g.at[:, pl.ds((1 - c) * half, half), :], theirs, send_sems, recv_sems, 0, (x, y, c)).wait()

    return _pcall(
        body, name="reduce_cores", in_specs=[HBM_SPEC], out_specs=HBM_SPEC,
        out_shape=_sds((nchip, half, lanes), grads.dtype),
        scratch_shapes=[pltpu.SemaphoreType.DMA((1,)), pltpu.SemaphoreType.DMA((1,))],
    )(grads)


def _scatter_chips(part):
    nchip, half, lanes = part.shape

    def body(p, out, send_sems, recv_sems):
        x, y, c = _place()
        for k, (cx, cy) in enumerate(_other_chips(x, y)):
            for lo, n in _chunks(half, ICI_SPLIT):
                _remote(p.at[2 * cx + cy, pl.ds(lo, n), :], out.at[k, pl.ds(lo, n), :],
                        send_sems, recv_sems, k, (cx, cy, c)).start()
        for k in range(3):
            _remote(p.at[k], out.at[k], send_sems, recv_sems, k, (x, y, c)).wait()

    return _pcall(
        body, name="scatter_chips", in_specs=[HBM_SPEC], out_specs=HBM_SPEC,
        out_shape=_sds((3, half, lanes), part.dtype),
        scratch_shapes=[pltpu.SemaphoreType.DMA((3,)), pltpu.SemaphoreType.DMA((3,))],
    )(part)


def _sum_partials(received, part, place, tm):
    _, half, lanes = received.shape
    tm = min(tm, half)
    nblk = half // tm

    def body(place_ref, r_ref, p_ref, o_ref):
        tot = p_ref[...].astype(F32)
        for k in range(3):
            tot = tot + r_ref[k].astype(F32)
        o_ref[...] = tot

    return _pcall(
        body, name="sum_chip_partials",
        grid_spec=pltpu.PrefetchScalarGridSpec(
            num_scalar_prefetch=1, grid=(nblk,),
            in_specs=[pl.BlockSpec((3, tm, lanes), lambda i, pc: (0, i, 0)),
                      pl.BlockSpec((None, tm, lanes), lambda i, pc: (pc[0], i, 0))],
            out_specs=pl.BlockSpec((tm, lanes), lambda i, pc: (pc[1] * nblk + i, 0))),
        out_shape=_sds((2 * half, lanes), F32),
        compiler_params=_cparams("parallel"))(place, received, part)


def _share_cores(block):
    rows, lanes = block.shape
    half = rows // 2

    def body(src, out, send_sems, recv_sems):
        x, y, c = _place()
        for lo, n in _chunks(half, D2D_SPLIT):
            piece = pl.ds(c * half + lo, n)
            _remote(src.at[piece, :], out.at[piece, :], send_sems, recv_sems, 0, (x, y, 1 - c)).start()
        mine = out.at[pl.ds(c * half, half), :]
        theirs = out.at[pl.ds((1 - c) * half, half), :]
        _remote(mine, theirs, send_sems, recv_sems, 0, (x, y, c)).wait()

    return _pcall(
        body, name="share_cores", in_specs=[HBM_SPEC], out_specs=HBM_SPEC,
        out_shape=_sds((rows, lanes), block.dtype), input_output_aliases={0: 0},
        scratch_shapes=[pltpu.SemaphoreType.DMA((1,)), pltpu.SemaphoreType.DMA((1,))],
    )(block)


def _sum_blocks(stacked, name, tm):
    n, rows, lanes = stacked.shape
    tm = min(tm, rows)

    def body(s_ref, o_ref):
        tot = s_ref[n - 1].astype(F32)
        for k in range(n - 1):
            tot = tot + s_ref[k].astype(F32)
        o_ref[...] = tot

    return _pcall(body, name=name, grid=(rows // tm,),
                  in_specs=[pl.BlockSpec((n, tm, lanes), lambda i: (0, i, 0))],
                  out_specs=pl.BlockSpec((tm, lanes), lambda i: (i, 0)), out_shape=_sds((rows, lanes), F32),
                  compiler_params=_cparams("parallel"))(stacked)


def _add_halves(grads, theirs, core, tm):
    n, half, lanes = theirs.shape
    tm = min(tm, half)
    nblk = half // tm

    def body(c_ref, g_ref, t_ref, o_ref):
        o_ref[...] = (g_ref[...] + t_ref[...]).astype(o_ref.dtype)

    spec = pl.BlockSpec((None, tm, lanes), lambda k, i, c: (k, i, 0))
    return _pcall(
        body, name="add_core_halves",
        grid_spec=pltpu.PrefetchScalarGridSpec(
            num_scalar_prefetch=1, grid=(n, nblk),
            in_specs=[pl.BlockSpec((None, tm, lanes), lambda k, i, c: (k, c[0] * nblk + i, 0)), spec], out_specs=spec),
        out_shape=_sds((n, half, lanes), BF16),
        compiler_params=_cparams("parallel", "parallel"))(core, grads, theirs)


def _allreduce_small(part):
    rows, lanes = part.shape
    ndev = 8

    def body(src, tot, buf, send_sems, recv_sems):
        x, y, c = _place()
        me = 4 * x + 2 * y + c
        buf[me] = src[...]
        sends = []
        for k in range(1, ndev):
            peer = (x ^ (k >> 2), y ^ ((k >> 1) & 1), c ^ (k & 1))
            cp = _remote(src, buf.at[me], send_sems, recv_sems, k - 1, peer)
            cp.start()
            sends.append(cp)
        for k in range(1, ndev):
            theirs = buf.at[me ^ k]
            _remote(theirs, theirs, send_sems, recv_sems, k - 1, (x, y, c)).wait_recv()
        for cp in sends:
            cp.wait_send()
        acc = buf[0]
        for d in range(1, ndev):
            acc = acc + buf[d]
        tot[...] = acc

    vm = pl.BlockSpec(memory_space=pltpu.VMEM)
    return _pcall(
        body, name="allreduce_small", in_specs=[vm], out_specs=vm, out_shape=_sds((rows, lanes), F32),
        scratch_shapes=[pltpu.VMEM((ndev, rows, lanes), F32), pltpu.SemaphoreType.DMA((ndev - 1,)),
                        pltpu.SemaphoreType.DMA((ndev - 1,))],
    )(part)


def _big_rows():
    return [int(np.prod(shape)) // LANES for _, shape in BIG]


def _pack_big(blocks, dtype):
    parts = [blocks[name].reshape(blocks[name].shape[0], -1, LANES).astype(dtype) for name, _ in BIG]
    return jnp.concatenate(parts, axis=1)


def _unpack_big(packed):
    out, off = {}, 0
    for (name, shape), r in zip(BIG, _big_rows()):
        out[name] = packed[:, off:off + r].reshape((packed.shape[0],) + shape)
        off += r
    return out


def _pack_small(vals):
    parts = []
    for name, shape, r in SMALL:
        flat = vals[name].reshape(-1).astype(F32)
        parts.append(jnp.pad(flat, (0, r * LANES - flat.shape[0])).reshape(r, LANES))
    used = sum(r for _, _, r in SMALL)
    parts.append(jnp.zeros((SMALL_ROWS - used, LANES), F32))
    return jnp.concatenate(parts, axis=0)


def _unpack_small(packed):
    out, off = {}, 0
    for name, shape, r in SMALL:
        n = int(np.prod(shape))
        out[name] = packed[off:off + r].reshape(-1)[:n].reshape(shape)
        off += r
    return out


def _heads_major(a, nh):
    t = a.shape[0]
    return a.reshape(t, nh, a.shape[1] // nh).transpose(1, 0, 2)


def _tokens_major(a):
    nh, t, w = a.shape
    return a.transpose(1, 0, 2).reshape(t, nh * w)


def _local_step(x, target, small, wfull):
    t = x.shape[0]
    nh, hd = DIL_HEADS, DIL_HD
    w_in = wfull["w_in"].transpose(1, 0, 2).reshape(D_MODEL, -1)
    w_out = wfull["w_out"].reshape(D_MODEL, D_MODEL)
    w_qb, w_kvb = wfull["mla_w_q_b"], wfull["mla_w_kv_b"]
    grads_s, grads_b = {}, {}

    x1, ffn1_saved = _ffn_fwd(x, small["ffn1_norm"], wfull["ffn1_w_gate"], wfull["ffn1_w_up"],
                              wfull["ffn1_w_down"], "ffn1")
    hm = _rms_fwd(x1, small["mix_norm"], BF16, "mix_norm", 512)
    proj = _mm_simple("in_proj", hm, w_in, NN, F32, tm=1024)
    cq, ckv, k_pe = proj[:, 1536:1792], proj[:, 1792:1920], proj[:, 1920:1984]

    gq, gk = jnp.tile(small["dil_q_norm"], (1, nh)), jnp.tile(small["dil_k_norm"], (1, nh))
    qn = _head_norm_fwd(proj, 0, gq, "dil_q_norm", 512)
    kn = _head_norm_fwd(proj, 1, gk, "dil_k_norm", 512)
    v_d = proj[:, 2 * DIL_WIDTH:3 * DIL_WIDTH].astype(BF16)
    bias = _bias_tiles(small["rel_bias"]).reshape(3, nh // 2, 2 * QB, QB + DIL_W)
    outs, lses = [], []
    for b, dil in enumerate(DIL_DILATIONS):
        o_b, lse_b = _dil_fwd(qn, kn, v_d, bias[b], dil, f"dil_fwd_{dil}")
        outs.append(o_b)
        lses.append(lse_b)
    o_dil, lse_tot, od = _dil_merge(outs, lses, small["out_norm_dil"], 512)

    mh = MLA_HEADS
    cos_t, sin_t = _rope_tables(t)
    cqn = _rms_fwd(cq, small["mla_q_a_norm"], BF16, "mla_q_a_norm", 512)
    ckvn = _rms_fwd(ckv, small["mla_kv_a_norm"], BF16, "mla_kv_a_norm", 512)
    tm = min(512, t)

    th = min(2048, t)

    def head_proj(name, a, w, width):
        k = a.shape[1]
        return _mm(name, (mh, t // th, 1),
                   [(a, pl.BlockSpec((th, k), lambda h, i, r: (i, 0)), w, pl.BlockSpec((None, k, width), lambda h, i, r: (h, 0, 0)))],
                   NN, _sds((mh, t, width), F32), pl.BlockSpec((None, th, width), lambda h, i, r: (h, i, 0)), (th, width))

    q_raw = head_proj("mla_q_proj", cqn, w_qb, MLA_QK)
    kv_raw = head_proj("mla_kv_proj", ckvn, w_kvb, MLA_NOPE + MLA_V)
    k_raw = jnp.concatenate([kv_raw[:, :, :MLA_NOPE], jnp.broadcast_to(k_pe[None], (mh, t, MLA_ROPE))], axis=2)
    v_m = kv_raw[:, :, MLA_NOPE:].astype(BF16)
    q_raw2, k_raw2 = q_raw.reshape(mh * t, MLA_QK), k_raw.reshape(mh * t, MLA_QK)
    q_scale = MLA_QK ** -0.5
    q_m = _mla_qk_fwd(q_raw2, small["mla_q_norm"], cos_t, sin_t, q_scale, "mla_q_rope", 2048).reshape(mh, t, MLA_QK)
    k_m = _mla_qk_fwd(k_raw2, small["mla_k_norm"], cos_t, sin_t, 1.0, "mla_k_rope", 2048).reshape(mh, t, MLA_QK)
    o_mla_h, lse_m = _mla_fwd(q_m, k_m, v_m, 512, 2048)
    o_mla = _tokens_major(o_mla_h)

    om = _rms_fwd(o_mla, small["out_norm_mla"], BF16, "out_norm_mla", 512)
    half_w = DIL_WIDTH
    row = pl.BlockSpec((tm, D_MODEL), lambda i, j, r: (i, 0))
    act_spec = pl.BlockSpec((tm, half_w), lambda i, j, r: (i, 0))
    x2 = _mm("out_proj", (t // tm, 1, 1),
             [(od, act_spec, w_out, pl.BlockSpec((half_w, D_MODEL), lambda i, j, r: (0, 0))),
              (om, act_spec, w_out, pl.BlockSpec((half_w, D_MODEL), lambda i, j, r: (1, 0)))],
             NN, _sds((t, D_MODEL), F32), row, (tm, D_MODEL), res=(x1, row))
    x3, ffn2_saved = _ffn_fwd(x2, small["ffn2_norm"], wfull["ffn2_w_gate"], wfull["ffn2_w_up"],
                              wfull["ffn2_w_down"], "ffn2")
    dy, loss = _loss_head(x3, target, 512)

    dx2, grads_s["ffn2_norm"], grads_b["ffn2_w_gate"], grads_b["ffn2_w_up"], grads_b["ffn2_w_down"] = _ffn_bwd(
        dy, x2, small["ffn2_norm"], wfull["ffn2_w_gate"], wfull["ffn2_w_up"], wfull["ffn2_w_down"], ffn2_saved, "ffn2")

    d_ocat = _mm_simple("out_proj_dx", dx2, w_out, NT, F32, tm=1024)
    dw_out_d = _mm_simple("out_proj_dw_dil", od, dx2, TN, F32, tk=2048)
    dw_out_m = _mm_simple("out_proj_dw_mla", om, dx2, TN, F32, tk=2048)
    grads_b["w_out"] = jnp.concatenate([dw_out_d, dw_out_m], axis=0).reshape(N_CHIPS, D_MODEL // N_CHIPS, D_MODEL)
    do_dil, grads_s["out_norm_dil"] = _rms_bwd([d_ocat[:, :half_w]], o_dil, small["out_norm_dil"], None, "out_norm_dil_bwd", 512)
    do_mla, grads_s["out_norm_mla"] = _rms_bwd([d_ocat[:, half_w:]], o_mla, small["out_norm_mla"], None, "out_norm_mla_bwd", 512)

    do_m = _heads_major(do_mla, mh)
    dl_m = _rowdot(do_m.reshape(mh * t, MLA_V), o_mla_h.reshape(mh * t, MLA_V), "mla_delta", 2048).reshape(mh, t, 1)
    dk_m, dv_m, dq_t = _mla_bwd(q_m, k_m, k_m.transpose(0, 2, 1), v_m, do_m, lse_m.reshape(mh, 1, t),
                                dl_m.reshape(mh, 1, t), 2048, 512)
    dq_m = dq_t.transpose(0, 1, 3, 2).reshape(mh, t, MLA_QK)
    dq_raw, grads_s["mla_q_norm"] = _mla_qk_bwd(dq_m.reshape(mh * t, MLA_QK), q_raw2, small["mla_q_norm"],
                                                 cos_t, sin_t, q_scale, "mla_q_rope_bwd", 2048)
    dk_raw, grads_s["mla_k_norm"] = _mla_qk_bwd(dk_m.reshape(mh * t, MLA_QK), k_raw2, small["mla_k_norm"],
                                                 cos_t, sin_t, 1.0, "mla_k_rope_bwd", 2048)
    dq_raw = dq_raw.reshape(mh, t, MLA_QK)
    dk_raw = dk_raw.reshape(mh, t, MLA_QK)
    dkv_raw = jnp.concatenate([dk_raw[:, :, :MLA_NOPE], dv_m], axis=2)
    dk_pe_h = dk_raw[:, :, MLA_NOPE:]

    def head_proj_dx(name, d, w):
        width, k = d.shape[2], w.shape[1]
        pairs = [(d, pl.BlockSpec((None, th, width), lambda i, j, r, h=h: (h, i, 0)),
                  w, pl.BlockSpec((None, k, width), lambda i, j, r, h=h: (h, 0, 0))) for h in range(mh)]
        return _mm(name, (t // th, 1, 1), pairs, NT, _sds((t, k), F32),
                   pl.BlockSpec((th, k), lambda i, j, r: (i, 0)), (th, k))

    def head_proj_dw(name, a, d):
        width, k = d.shape[2], a.shape[1]
        return _mm(name, (mh, 1, t // th),
                   [(a, pl.BlockSpec((th, k), lambda h, j, r: (r, 0)), d, pl.BlockSpec((None, th, width), lambda h, j, r: (h, r, 0)))],
                   TN, _sds((mh, k, width), F32), pl.BlockSpec((None, k, width), lambda h, j, r: (h, 0, 0)), (k, width))

    d_cqn = head_proj_dx("mla_q_proj_dx", dq_raw, w_qb)
    d_ckvn = head_proj_dx("mla_kv_proj_dx", dkv_raw, w_kvb)
    grads_b["mla_w_q_b"] = head_proj_dw("mla_q_proj_dw", cqn, dq_raw)
    grads_b["mla_w_kv_b"] = head_proj_dw("mla_kv_proj_dw", ckvn, dkv_raw)
    d_cq, grads_s["mla_q_a_norm"] = _rms_bwd([d_cqn], cq, small["mla_q_a_norm"], None, "mla_q_a_norm_bwd", 512)
    d_ckv, grads_s["mla_kv_a_norm"] = _rms_bwd([d_ckvn], ckv, small["mla_kv_a_norm"], None, "mla_kv_a_norm_bwd", 512)
    d_kpe = _sum_blocks(dk_pe_h.reshape(mh, t * MLA_ROPE // LANES, LANES), "mla_kpe_sum", 1024).reshape(t, MLA_ROPE)

    stats = _dil_stats(do_dil, o_dil, lse_tot, 512)
    do_db = do_dil.astype(BF16)
    dqs, dks, dvs, dtiles = [], [], [], []
    for b, dil in enumerate(DIL_DILATIONS):
        dq_b, dk_b, dv_b, db_b = _dil_bwd(qn, kn, v_d, do_db, stats, bias[b], dil, f"dil_bwd_{dil}")
        dqs.append(dq_b)
        dks.append(dk_b)
        dvs.append(dv_b)
        dtiles.append(db_b)
    grads_s["rel_bias"] = _bias_grad(jnp.stack(dtiles).reshape(3, nh, QB, QB + DIL_W))
    dq_a, dgq = _head_norm_bwd(dqs, proj, 0, gq, "dil_q_norm_bwd", 512)
    dk_a, dgk = _head_norm_bwd(dks, proj, 1, gk, "dil_k_norm_bwd", 512)
    grads_s["dil_q_norm"], grads_s["dil_k_norm"] = dgq[:, :hd], dgk[:, :hd]
    dv_a = _add3(dvs[0], dvs[1], dvs[2], "dil_dv_sum", 512)
    dproj = jnp.concatenate([dq_a, dk_a, dv_a, d_cq, d_ckv, d_kpe], axis=1)

    d_hm = _mm_simple("in_proj_dx", dproj, w_in, NT, F32, tm=1024)
    dw_in = _mm_simple("in_proj_dw", hm, dproj, TN, F32)
    grads_b["w_in"] = dw_in.reshape(D_MODEL, N_CHIPS, -1).transpose(1, 0, 2)
    dx1, grads_s["mix_norm"] = _rms_bwd([d_hm], x1, small["mix_norm"], dx2, "mix_norm_bwd", 512)
    dx, grads_s["ffn1_norm"], grads_b["ffn1_w_gate"], grads_b["ffn1_w_up"], grads_b["ffn1_w_down"] = _ffn_bwd(
        dx1, x, small["ffn1_norm"], wfull["ffn1_w_gate"], wfull["ffn1_w_up"], wfull["ffn1_w_down"], ffn1_saved, "ffn1")
    return loss, dx, grads_s, grads_b


def kernel(x, ffn1_norm, ffn1_w_gate, ffn1_w_up, ffn1_w_down, mix_norm, w_in, dil_q_norm, dil_k_norm, rel_bias, mla_q_a_norm, mla_w_q_b, mla_kv_a_norm, mla_w_kv_b, mla_q_norm, mla_k_norm, out_norm_dil, out_norm_mla, w_out, ffn2_norm, ffn2_w_gate, ffn2_w_up, ffn2_w_down, loss_target, m_ffn1_norm, m_ffn1_w_gate, m_ffn1_w_up, m_ffn1_w_down, m_mix_norm, m_w_in, m_dil_q_norm, m_dil_k_norm, m_rel_bias, m_mla_q_a_norm, m_mla_w_q_b, m_mla_kv_a_norm, m_mla_w_kv_b, m_mla_q_norm, m_mla_k_norm, m_out_norm_dil, m_out_norm_mla, m_w_out, m_ffn2_norm, m_ffn2_w_gate, m_ffn2_w_up, m_ffn2_w_down, v_ffn1_norm, v_ffn1_w_gate, v_ffn1_w_up, v_ffn1_w_down, v_mix_norm, v_w_in, v_dil_q_norm, v_dil_k_norm, v_rel_bias, v_mla_q_a_norm, v_mla_w_q_b, v_mla_kv_a_norm, v_mla_w_kv_b, v_mla_q_norm, v_mla_k_norm, v_out_norm_dil, v_out_norm_mla, v_w_out, v_ffn2_norm, v_ffn2_w_gate, v_ffn2_w_up, v_ffn2_w_down):
    given = dict(locals())
    big_names = [name for name, _ in BIG]
    small_names = [name for name, _, _ in SMALL]

    chip = (2 * lax.axis_index("x") + lax.axis_index("y")).astype(jnp.int32)
    core = lax.axis_index("c").astype(jnp.int32)
    mine = _pack_big({n: given[n] for n in big_names}, BF16)
    gathered = lax.dynamic_update_slice(_gather_weights(mine[0]), mine, (chip, 0, 0))
    wfull = _unpack_big(gathered)
    small = {n: given[n] for n in small_names}

    loss, dx, grads_s, grads_b = _local_step(x[0], loss_target[0], small, wfull)
    loss = lax.psum(loss[0, 0], ("x", "y", "c"))

    packed = _pack_big(grads_b, F32).reshape(N_CHIPS, -1, LANES)
    chip_part = _add_halves(packed, _reduce_cores(packed), core.reshape(1), 1264)
    reduced = _sum_partials(_scatter_chips(chip_part), chip_part, jnp.stack([chip, core]), 1264)
    g_big = _unpack_big(_share_cores(reduced)[None])
    g_small = _unpack_small(_allreduce_small(_pack_small(grads_s)))

    grad, delta, new_m, new_v = {}, {}, {}, {}
    for name, shape in BIG:
        g2 = g_big[name].reshape(shape)
        d_, m_, v_ = _adamw(given[name].reshape(shape), g2, given["m_" + name].reshape(shape),
                            given["v_" + name].reshape(shape), f"adamw_{name}")
        full = given[name].shape
        grad[name], delta[name], new_m[name], new_v[name] = (a.reshape(full) for a in (g2, d_, m_, v_))
    ps = {k: _pack_small({n: given[pre + n] for n in small_names}) for k, pre in (("w", ""), ("m", "m_"), ("v", "v_"))}
    gs_packed = _pack_small(g_small)
    d_s, m_s, v_s = (_unpack_small(a) for a in _adamw(ps["w"], gs_packed, ps["m"], ps["v"], "adamw_small"))
    for name in small_names:
        grad[name], delta[name], new_m[name], new_v[name] = g_small[name], d_s[name], m_s[name], v_s[name]

    return (loss, dx[None], *[grad[n] for n in WEIGHTS], *[delta[n] for n in WEIGHTS],
            *[new_m[n] for n in WEIGHTS], *[new_v[n] for n in WEIGHTS])
```

```python
import functools

import numpy as np
import jax
import jax.numpy as jnp
from jax import lax
from jax.experimental import pallas as pl
from jax.experimental.pallas import tpu as pltpu

F32 = jnp.float32
BF16 = jnp.bfloat16

D_MODEL = 1024
D_FF = 2816
N_CHIPS = 4
DIL_HEADS = 8
DIL_HD = 64
DIL_WIDTH = 512
DIL_DILATIONS = (1, 4, 16)
DIL_W = 128
QB = 128
MLA_HEADS = 4
MLA_NOPE = 128
MLA_ROPE = 64
MLA_QK = 192
MLA_V = 128
MLA_Q_RANK = 256
MLA_KV_RANK = 128
ROPE_BASE = 10000.0
REL_BUCKETS = 32
REL_MAX_DIST = 2048
FFN_RESID = 0.5
EPS = 1e-6
NEG = -1e30
LANES = 128

ADAM_LR = 0.001
ADAM_B1 = 0.9
ADAM_B2 = 0.999
ADAM_EPS = 1e-08
ADAM_WD = 0.01
ADAM_STEP = 10

NT = (((1,), (1,)), ((), ()))
NN = (((1,), (0,)), ((), ()))
TN = (((0,), (0,)), ((), ()))

BIG = (
    ("ffn1_w_gate", (D_MODEL, D_FF // N_CHIPS)),
    ("ffn1_w_up", (D_MODEL, D_FF // N_CHIPS)),
    ("ffn1_w_down", (D_FF // N_CHIPS, D_MODEL)),
    ("w_in", (D_MODEL, 1984 // N_CHIPS)),
    ("mla_w_q_b", (MLA_Q_RANK, MLA_QK)),
    ("mla_w_kv_b", (MLA_KV_RANK, MLA_NOPE + MLA_V)),
    ("w_out", (D_MODEL // N_CHIPS, D_MODEL)),
    ("ffn2_w_gate", (D_MODEL, D_FF // N_CHIPS)),
    ("ffn2_w_up", (D_MODEL, D_FF // N_CHIPS)),
    ("ffn2_w_down", (D_FF // N_CHIPS, D_MODEL)),
)
SMALL = (
    ("ffn1_norm", (1, 1024), 8), ("mix_norm", (1, 1024), 8), ("dil_q_norm", (1, 64), 1),
    ("dil_k_norm", (1, 64), 1), ("rel_bias", (8, 32), 2), ("mla_q_a_norm", (1, 256), 2),
    ("mla_kv_a_norm", (1, 128), 1), ("mla_q_norm", (1, 192), 2), ("mla_k_norm", (1, 192), 2),
    ("out_norm_dil", (1, 512), 4), ("out_norm_mla", (1, 512), 4), ("ffn2_norm", (1, 1024), 8),
)
SMALL_ROWS = 48
WEIGHTS = ("ffn1_norm", "ffn1_w_gate", "ffn1_w_up", "ffn1_w_down", "mix_norm", "w_in", "dil_q_norm",
           "dil_k_norm", "rel_bias", "mla_q_a_norm", "mla_w_q_b", "mla_kv_a_norm", "mla_w_kv_b",
           "mla_q_norm", "mla_k_norm", "out_norm_dil", "out_norm_mla", "w_out", "ffn2_norm",
           "ffn2_w_gate", "ffn2_w_up", "ffn2_w_down")


def _pcall(body, **kw):
    return pl.pallas_call(body, **kw)


def _cparams(*sem):
    return pltpu.CompilerParams(dimension_semantics=sem)


def _sds(shape, dtype):
    return jax.ShapeDtypeStruct(shape, dtype)


def _dot(a, b, dn):
    return lax.dot_general(a, b, dn, preferred_element_type=F32)


def _rms_fwd(x, g, out_dtype, name, tm):
    n, d = x.shape
    tm = min(tm, n)

    def body(x_ref, g_ref, o_ref):
        xf = x_ref[...].astype(F32)
        r = lax.rsqrt(jnp.mean(xf * xf, axis=-1, keepdims=True) + EPS)
        o_ref[...] = (xf * r * g_ref[...]).astype(o_ref.dtype)

    return _pcall(
        body, name=name, grid=(n // tm,),
        in_specs=[pl.BlockSpec((tm, d), lambda i: (i, 0)), pl.BlockSpec((1, d), lambda i: (0, 0))],
        out_specs=pl.BlockSpec((tm, d), lambda i: (i, 0)),
        out_shape=_sds((n, d), out_dtype), compiler_params=_cparams("parallel"))(x, g)


def _rms_bwd(dys, x, g, res, name, tm):
    n, d = x.shape
    tm = min(tm, n)
    nd = len(dys)
    has_res = res is not None

    def body(*refs):
        dy_refs = refs[:nd]
        x_ref, g_ref = refs[nd], refs[nd + 1]
        res_ref = refs[nd + 2] if has_res else None
        dx_ref, dg_ref = refs[-2], refs[-1]
        dy = dy_refs[0][...].astype(F32)
        for r_ in dy_refs[1:]:
            dy = dy + r_[...].astype(F32)
        xf = x_ref[...].astype(F32)
        r = lax.rsqrt(jnp.mean(xf * xf, axis=-1, keepdims=True) + EPS)
        xh = xf * r
        dxh = dy * g_ref[...]
        dx = r * (dxh - xh * jnp.mean(dxh * xh, axis=-1, keepdims=True))
        if has_res:
            dx = dx + res_ref[...]
        dx_ref[...] = dx

        @pl.when(pl.program_id(0) == 0)
        def _():
            dg_ref[...] = jnp.zeros_like(dg_ref)

        dg_ref[...] += jnp.sum(dy * xh, axis=0, keepdims=True)

    row = pl.BlockSpec((tm, d), lambda i: (i, 0))
    vec = pl.BlockSpec((1, d), lambda i: (0, 0))
    ins = list(dys) + [x, g] + ([res] if has_res else [])
    return _pcall(
        body, name=name, grid=(n // tm,),
        in_specs=[row] * nd + [row, vec] + ([row] if has_res else []),
        out_specs=(row, vec),
        out_shape=(_sds((n, d), F32), _sds((1, d), F32)),
        compiler_params=_cparams("arbitrary"))(*ins)


def _mm(name, grid, pairs, dn, out_shape, out_spec, acc_shape, res=None, scale=1.0):
    npairs = len(pairs)
    nred = grid[2]
    has_res = res is not None

    def body(*refs):
        ab = refs[:2 * npairs]
        res_ref = refs[2 * npairs] if has_res else None
        o_ref = refs[2 * npairs + int(has_res)]
        acc_ref = refs[-1] if nred > 1 else None
        tot = None
        for p in range(npairs):
            d = _dot(ab[2 * p][...].astype(BF16), ab[2 * p + 1][...].astype(BF16), dn)
            tot = d if tot is None else tot + d

        def finish(v):
            if scale != 1.0:
                v = v * scale
            if has_res:
                v = res_ref[...] + v
            o_ref[...] = v.astype(o_ref.dtype)

        if nred == 1:
            finish(tot)
        else:
            r = pl.program_id(2)

            @pl.when(r == 0)
            def _():
                acc_ref[...] = tot

            @pl.when(r > 0)
            def _():
                acc_ref[...] += tot

            @pl.when(r == nred - 1)
            def _():
                finish(acc_ref[...])

    ins, specs = [], []
    for a, a_spec, b, b_spec in pairs:
        ins += [a, b]
        specs += [a_spec, b_spec]
    if has_res:
        ins.append(res[0])
        specs.append(res[1])
    return _pcall(
        body, name=name, grid=grid, in_specs=specs, out_specs=out_spec, out_shape=out_shape,
        scratch_shapes=[pltpu.VMEM(acc_shape, F32)] if nred > 1 else [],
        compiler_params=_cparams("parallel", "parallel", "arbitrary"))(*ins)


def _ffn_up(h, wg, wu, name, tm):
    t, d = h.shape
    nc, _, fs = wg.shape
    tm = min(tm, t)

    def body(h_ref, wg_ref, wu_ref, g_ref, u_ref, a_ref):
        hh = h_ref[...]
        gate = _dot(hh, wg_ref[...], NN)
        up = _dot(hh, wu_ref[...], NN)
        sig = jax.nn.sigmoid(gate)
        silu = gate * sig
        g_ref[...] = (up * (sig + silu * (1.0 - sig))).astype(BF16)
        u_ref[...] = silu.astype(BF16)
        a_ref[...] = (silu * up).astype(BF16)

    wspec = pl.BlockSpec((None, d, fs), lambda c, i: (c, 0, 0))
    ospec = pl.BlockSpec((None, tm, fs), lambda c, i: (c, i, 0))
    osd = _sds((nc, t, fs), BF16)
    return _pcall(
        body, name=name, grid=(nc, t // tm),
        in_specs=[pl.BlockSpec((tm, d), lambda c, i: (i, 0)), wspec, wspec],
        out_specs=(ospec, ospec, ospec), out_shape=(osd, osd, osd),
        compiler_params=_cparams("parallel", "parallel"))(h, wg, wu)


def _ffn_dact(dy, wd, dact_dgate, dact_dup, name, tm):
    t, d = dy.shape
    nc, fs, _ = wd.shape
    tm = min(tm, t)

    def body(dy_ref, wd_ref, g_ref, u_ref, dg_ref, du_ref):
        da = _dot(dy_ref[...].astype(BF16), wd_ref[...], NT) * FFN_RESID
        dg_ref[...] = (da * g_ref[...].astype(F32)).astype(BF16)
        du_ref[...] = (da * u_ref[...].astype(F32)).astype(BF16)

    cspec = pl.BlockSpec((None, tm, fs), lambda c, i: (c, i, 0))
    osd = _sds((nc, t, fs), BF16)
    return _pcall(
        body, name=name, grid=(nc, t // tm),
        in_specs=[pl.BlockSpec((tm, d), lambda c, i: (i, 0)),
                  pl.BlockSpec((None, fs, d), lambda c, i: (c, 0, 0)), cspec, cspec],
        out_specs=(cspec, cspec), out_shape=(osd, osd),
        compiler_params=_cparams("parallel", "parallel"))(dy, wd, dact_dgate, dact_dup)


def _ffn_fwd(x, g, wg, wu, wd, tag):
    t = x.shape[0]
    nc, _, fs = wg.shape
    tm = min(512, t)
    h = _rms_fwd(x, g, BF16, f"{tag}_norm", 512)
    dact_dgate, dact_dup, act = _ffn_up(h, wg, wu, f"{tag}_up", 1024)
    pairs = [(act, pl.BlockSpec((None, tm, fs), lambda i, j, r, c=c: (c, i, 0)),
              wd, pl.BlockSpec((None, fs, D_MODEL), lambda i, j, r, c=c: (c, 0, 0))) for c in range(nc)]
    row = pl.BlockSpec((tm, D_MODEL), lambda i, j, r: (i, 0))
    y = _mm(f"{tag}_down", (t // tm, 1, 1), pairs, NN, _sds((t, D_MODEL), F32), row, (tm, D_MODEL),
            res=(x, row), scale=FFN_RESID)
    return y, (h, dact_dgate, dact_dup, act)


def _ffn_bwd(dy, x, g, wg, wu, wd, saved, tag):
    h, dact_dgate, dact_dup, act = saved
    t = x.shape[0]
    nc, _, fs = wg.shape
    tm = min(512, t)
    tk = min(2048, t)
    dgate, dup = _ffn_dact(dy, wd, dact_dgate, dact_dup, f"{tag}_dact", 1024)
    tok_c = pl.BlockSpec((None, tk, fs), lambda c, j, r: (c, r, 0))
    tok_d = pl.BlockSpec((tk, D_MODEL), lambda c, j, r: (r, 0))
    dwd = _mm(f"{tag}_dwd", (nc, 1, t // tk), [(act, tok_c, dy, tok_d)], TN,
              _sds((nc, fs, D_MODEL), F32), pl.BlockSpec((None, fs, D_MODEL), lambda c, j, r: (c, 0, 0)),
              (fs, D_MODEL), scale=FFN_RESID)
    wout = pl.BlockSpec((None, D_MODEL, fs), lambda c, j, r: (c, 0, 0))
    dwg = _mm(f"{tag}_dwg", (nc, 1, t // tk), [(h, tok_d, dgate, tok_c)], TN,
              _sds((nc, D_MODEL, fs), F32), wout, (D_MODEL, fs))
    dwu = _mm(f"{tag}_dwu", (nc, 1, t // tk), [(h, tok_d, dup, tok_c)], TN,
              _sds((nc, D_MODEL, fs), F32), wout, (D_MODEL, fs))
    pairs = []
    for c in range(nc):
        a_spec = pl.BlockSpec((None, tm, fs), lambda i, j, r, c=c: (c, i, 0))
        w_spec = pl.BlockSpec((None, D_MODEL, fs), lambda i, j, r, c=c: (c, 0, 0))
        pairs += [(dgate, a_spec, wg, w_spec), (dup, a_spec, wu, w_spec)]
    dh = _mm(f"{tag}_dh", (t // tm, 1, 1), pairs, NT,
             _sds((t, D_MODEL), F32), pl.BlockSpec((tm, D_MODEL), lambda i, j, r: (i, 0)), (tm, D_MODEL))
    dx, dg = _rms_bwd([dh], x, g, dy, f"{tag}_dnorm", 512)
    return dx, dg, dwg, dwu, dwd


def _mm_simple(name, a, b, dn, out_dtype, tm=512, tk=512, res=None, scale=1.0):
    if dn == TN:
        k, m = a.shape
        n = b.shape[1]
        tk = min(tk, k)
        return _mm(name, (1, 1, k // tk),
                   [(a, pl.BlockSpec((tk, m), lambda i, j, r: (r, 0)), b, pl.BlockSpec((tk, n), lambda i, j, r: (r, 0)))],
                   TN, _sds((m, n), out_dtype), pl.BlockSpec((m, n), lambda i, j, r: (0, 0)), (m, n), scale=scale)
    m, k = a.shape
    n = b.shape[1] if dn == NN else b.shape[0]
    tm = min(tm, m)
    row = pl.BlockSpec((tm, n), lambda i, j, r: (i, 0))
    return _mm(name, (m // tm, 1, 1),
               [(a, pl.BlockSpec((tm, k), lambda i, j, r: (i, 0)), b, pl.BlockSpec(b.shape, lambda i, j, r: (0, 0)))],
               dn, _sds((m, n), out_dtype), row, (tm, n), res=None if res is None else (res, row), scale=scale)


def _t5_bucket(dist):
    max_exact = REL_BUCKETS // 2
    d = np.maximum(dist, 1).astype(np.float32)
    large = max_exact + (np.log(d / max_exact) / np.log(REL_MAX_DIST / max_exact)
                         * (REL_BUCKETS - max_exact)).astype(np.int32)
    large = np.minimum(large, REL_BUCKETS - 1)
    return np.where(dist < max_exact, dist, large).astype(np.int32)


def _bucket_tiles():
    i = np.arange(QB)[:, None]
    j = np.arange(QB + DIL_W)[None, :]
    delta = np.clip(i + DIL_W - j, 0, None)
    return np.stack([_t5_bucket(delta * dil) for dil in DIL_DILATIONS]).astype(np.int32)


def _bias_tiles(rel_bias):
    buckets = jnp.asarray(_bucket_tiles())

    def body(rb_ref, bk_ref, o_ref):
        bk = bk_ref[...]
        for h in range(DIL_HEADS):
            def pick(b, tile):
                return jnp.where(bk == b, rb_ref[h, b], tile)

            o_ref[h] = lax.fori_loop(0, REL_BUCKETS, pick, jnp.zeros((QB, QB + DIL_W), F32))

    return _pcall(
        body, name="dil_bias_tiles", grid=(3,),
        in_specs=[pl.BlockSpec(memory_space=pltpu.SMEM),
                  pl.BlockSpec((None, QB, QB + DIL_W), lambda b: (b, 0, 0))],
        out_specs=pl.BlockSpec((None, DIL_HEADS, QB, QB + DIL_W), lambda b: (b, 0, 0, 0)),
        out_shape=_sds((3, DIL_HEADS, QB, QB + DIL_W), F32),
        compiler_params=_cparams("parallel"))(rel_bias, buckets)


def _bias_grad(dtiles):
    buckets = jnp.asarray(_bucket_tiles())

    def body(dt_ref, bk_ref, o_ref):
        def one(b, carry):
            hit = [bk_ref[br] == b for br in range(3)]
            for h in range(DIL_HEADS):
                tot = jnp.zeros((), F32)
                for br in range(3):
                    tot = tot + jnp.sum(jnp.where(hit[br], dt_ref[br, h], 0.0))
                o_ref[h, b] = tot
            return carry

        lax.fori_loop(0, REL_BUCKETS, one, 0)

    return _pcall(
        body, name="dil_bias_grad",
        in_specs=[pl.BlockSpec(memory_space=pltpu.VMEM), pl.BlockSpec(memory_space=pltpu.VMEM)],
        out_specs=pl.BlockSpec(memory_space=pltpu.SMEM),
        out_shape=_sds((DIL_HEADS, REL_BUCKETS), F32))(dtiles, buckets)


def _split_heads(a, lo):
    zero = jnp.zeros_like(a)
    return jnp.concatenate([jnp.where(lo, a, zero), jnp.where(lo, zero, a)], axis=0)


def _side_by_side(a):
    n = a.shape[0] // 2
    return jnp.concatenate([a[:n], a[n:]], axis=1)


def _band_masks(prev_ok):
    ii = lax.broadcasted_iota(jnp.int32, (2 * QB, QB), 0) & (QB - 1)
    jj = lax.broadcasted_iota(jnp.int32, (2 * QB, QB), 1)
    return jj <= ii, jj >= ii + jnp.where(prev_ok, 0, QB)


def _dil_view(a, dil):
    t, w = a.shape
    return a.reshape(t // dil, dil * w)


def _dil_fwd(q, k, v, bias, dil, name):
    t, w = q.shape
    npair = w // LANES
    nl = t // dil // QB
    scale = DIL_HD ** -0.5

    def body(q_ref, kc_ref, kp_ref, vc_ref, vp_ref, b_ref, o_ref, lse_ref):
        nn = pl.program_id(1)
        lo = lax.broadcasted_iota(jnp.int32, (QB, LANES), 1) < DIL_HD
        lo2 = lax.broadcasted_iota(jnp.int32, (2 * QB, LANES), 1) < DIL_HD
        ii = lax.broadcasted_iota(jnp.int32, (2 * QB, 2 * QB), 0) & (QB - 1)
        jj = lax.broadcasted_iota(jnp.int32, (2 * QB, 2 * QB), 1)
        first_key = jnp.maximum(ii, jnp.where(nn != 0, 0, QB))
        valid = (jj >= first_key) & (jj <= ii + QB)
        for p in range(npair):
            cols = slice(p * LANES, (p + 1) * LANES)
            qq = _split_heads(q_ref[:, cols], lo)
            kk = jnp.concatenate([kp_ref[:, cols], kc_ref[:, cols]], axis=0)
            vv = jnp.concatenate([vp_ref[:, cols], vc_ref[:, cols]], axis=0)
            s = jnp.where(valid, _dot(qq, kk, NT) * scale + b_ref[p], NEG)
            m = jnp.max(s, axis=-1, keepdims=True)
            e = jnp.exp(s - m)
            den = jnp.sum(e, axis=-1, keepdims=True)
            pn = (e * (1.0 / den)).astype(BF16)
            o_ref[:, cols] = _dot(_side_by_side(pn), _split_heads(vv, lo2), NN)
            lse = m + jnp.log(den)
            lse_ref[:, cols] = jnp.where(lo, lse[:QB], lse[QB:])

    cur = pl.BlockSpec((QB, w), lambda r, n: (n, r))
    prev = pl.BlockSpec((QB, w), lambda r, n: (jnp.maximum(n - 1, 0), r))
    sd = _sds((t // dil, dil * w), F32)
    o, lse = _pcall(
        body, name=name, grid=(dil, nl),
        in_specs=[cur, cur, prev, cur, prev, pl.BlockSpec((npair, 2 * QB, 2 * QB), lambda r, n: (0, 0, 0))],
        out_specs=(cur, cur), out_shape=(sd, sd),
        compiler_params=_cparams("parallel", "parallel"))(*[_dil_view(a, dil) for a in (q, k, k, v, v)], bias)
    return o.reshape(t, w), lse.reshape(t, w)


def _dil_bwd(q, k, v, do, stats, bias, dil, name):
    t, w = q.shape
    npair = w // LANES
    nl = t // dil // QB
    scale = DIL_HD ** -0.5

    def body(qc_ref, qn_ref, doc_ref, don_ref, sc_ref, sn_ref, k_ref, v_ref, b_ref,
             dq_ref, dk_ref, dv_ref, db_ref, carry):
        r, nn = pl.program_id(0), pl.program_id(1)
        lo = lax.broadcasted_iota(jnp.int32, (QB, LANES), 1) < DIL_HD
        cur_ok, prev_ok = _band_masks(nn + 1 < nl)

        @pl.when((r == 0) & (nn == 0))
        def _():
            db_ref[...] = jnp.zeros_like(db_ref)
            carry[...] = jnp.zeros_like(carry)

        for p in range(npair):
            cols = slice(p * LANES, (p + 1) * LANES)
            kp, vp = k_ref[:, cols], v_ref[:, cols]
            k2 = _split_heads(kp, lo)

            def column(ref, lane):
                first = p * LANES + lane
                return jnp.concatenate([ref[:, first:first + 1], ref[:, first + DIL_HD:first + DIL_HD + 1]], axis=0)

            def side(q_ref, do_ref, s_ref, bias, ok):
                qq = _split_heads(q_ref[:, cols], lo)
                dd = _split_heads(do_ref[:, cols], lo)
                s = jnp.where(ok, _dot(qq, kp, NT) * scale + bias, NEG)
                prob = jnp.exp(s - column(s_ref, 0))
                ds = prob * (_dot(dd, vp, NT) - column(s_ref, DIL_HD // 2))
                return qq, dd, prob.astype(BF16), ds

            q1, d1, p1, ds1 = side(qc_ref, doc_ref, sc_ref, b_ref[p, :, QB:], cur_ok)
            q2, d2, p2, ds2 = side(qn_ref, don_ref, sn_ref, b_ref[p, :, :QB], prev_ok)
            ds1b, ds2b = ds1.astype(BF16), ds2.astype(BF16)
            dq_ref[:, cols] = carry[:, cols] + _dot(_side_by_side(ds1b), k2, NN) * scale
            carry[:, cols] = _dot(_side_by_side(ds2b), k2, NN) * scale
            dk_ref[:, cols] = _dot(jnp.concatenate([ds1b, ds2b], axis=0), jnp.concatenate([q1, q2], axis=0), TN) * scale
            dv_ref[:, cols] = _dot(jnp.concatenate([p1, p2], axis=0), jnp.concatenate([d1, d2], axis=0), TN)
            db_ref[p, :, QB:] += ds1
            db_ref[p, :, :QB] += ds2

    cur = pl.BlockSpec((QB, w), lambda r, n: (n, r))
    nxt = pl.BlockSpec((QB, w), lambda r, n: (jnp.minimum(n + 1, nl - 1), r))
    tile = pl.BlockSpec((npair, 2 * QB, 2 * QB), lambda r, n: (0, 0, 0))
    sd = _sds((t // dil, dil * w), F32)
    views = [_dil_view(a, dil) for a in (q, q, do, do, stats, stats, k, v)]
    dq, dk, dv, db = _pcall(
        body, name=name, grid=(dil, nl),
        in_specs=[cur, nxt, cur, nxt, cur, nxt, cur, cur, tile],
        out_specs=(cur, cur, cur, tile),
        out_shape=(sd, sd, sd, _sds((npair, 2 * QB, 2 * QB), F32)),
        scratch_shapes=[pltpu.VMEM((QB, w), F32)],
        compiler_params=_cparams("arbitrary", "arbitrary"))(*views, bias)
    return dq.reshape(t, w), dk.reshape(t, w), dv.reshape(t, w), db


def _head_sum_matrix(scale):
    idx = np.arange(DIL_WIDTH) // DIL_HD
    return jnp.asarray((idx[:, None] == idx[None, :]).astype(np.float32) * scale, BF16)


def _head_sum(x, mat):
    hi = x.astype(BF16)
    lo = (x - hi.astype(F32)).astype(BF16)
    return _dot(hi, mat, NN) + _dot(lo, mat, NN)


def _dil_merge(outs, lses, g, tm):
    t, w = outs[0].shape
    tm = min(tm, t)

    def body(o0, o1, o2, l0, l1, l2, g_ref, o_ref, l_ref, n_ref):
        a0, a1, a2 = l0[...], l1[...], l2[...]
        m = jnp.maximum(jnp.maximum(a0, a1), a2)
        e0, e1, e2 = jnp.exp(a0 - m), jnp.exp(a1 - m), jnp.exp(a2 - m)
        den = e0 + e1 + e2
        o = (e0 * o0[...] + e1 * o1[...] + e2 * o2[...]) / den
        o_ref[...] = o
        l_ref[...] = m + jnp.log(den)
        r = lax.rsqrt(jnp.mean(o * o, axis=-1, keepdims=True) + EPS)
        n_ref[...] = (o * r * g_ref[...]).astype(n_ref.dtype)

    spec = pl.BlockSpec((tm, w), lambda i: (i, 0))
    return _pcall(
        body, name="dil_merge", grid=(t // tm,),
        in_specs=[spec] * 6 + [pl.BlockSpec((1, w), lambda i: (0, 0))], out_specs=(spec, spec, spec),
        out_shape=(_sds((t, w), F32), _sds((t, w), F32), _sds((t, w), BF16)),
        compiler_params=_cparams("parallel"))(*outs, *lses, g)


def _dil_stats(do, o, lse, tm):
    t, w = do.shape
    tm = min(tm, t)

    def body(a_ref, b_ref, l_ref, m_ref, o_ref):
        first = (lax.broadcasted_iota(jnp.int32, (tm, w), 1) & (DIL_HD - 1)) < DIL_HD // 2
        o_ref[...] = jnp.where(first, l_ref[...], _head_sum(a_ref[...] * b_ref[...], m_ref[...]))

    spec = pl.BlockSpec((tm, w), lambda i: (i, 0))
    return _pcall(body, name="dil_stats", grid=(t // tm,),
                  in_specs=[spec, spec, spec, pl.BlockSpec((w, w), lambda i: (0, 0))], out_specs=spec,
                  out_shape=_sds((t, w), F32), compiler_params=_cparams("parallel"))(do, o, lse, _head_sum_matrix(1.0))


def _head_norm_fwd(x, col, g, name, tm):
    t = x.shape[0]
    w = DIL_WIDTH
    tm = min(tm, t)

    def body(x_ref, g_ref, m_ref, o_ref):
        xf = x_ref[...]
        r = lax.rsqrt(_head_sum(xf * xf, m_ref[...]) + EPS)
        o_ref[...] = (xf * r * g_ref[...]).astype(o_ref.dtype)

    return _pcall(
        body, name=name, grid=(t // tm,),
        in_specs=[pl.BlockSpec((tm, w), lambda i: (i, col)), pl.BlockSpec((1, w), lambda i: (0, 0)),
                  pl.BlockSpec((w, w), lambda i: (0, 0))],
        out_specs=pl.BlockSpec((tm, w), lambda i: (i, 0)), out_shape=_sds((t, w), BF16),
        compiler_params=_cparams("parallel"))(x, g, _head_sum_matrix(1.0 / DIL_HD))


def _head_norm_bwd(dys, x, col, g, name, tm):
    t = x.shape[0]
    w = DIL_WIDTH
    tm = min(tm, t)
    nd = len(dys)
    nt = t // tm
    lane = np.arange(w) % DIL_HD
    fold = jnp.asarray((lane[:, None] == lane[None, :]).astype(np.float32))

    def body(*refs):
        x_ref, g_ref, m_ref, f_ref = refs[nd:nd + 4]
        dx_ref, dg_ref = refs[-2], refs[-1]
        dy = refs[0][...]
        for r_ in refs[1:nd]:
            dy = dy + r_[...]
        xf = x_ref[...]
        mat = m_ref[...]
        r = lax.rsqrt(_head_sum(xf * xf, mat) + EPS)
        xh = xf * r
        dxh = dy * g_ref[...]
        dx_ref[...] = r * (dxh - xh * _head_sum(dxh * xh, mat))

        @pl.when(pl.program_id(0) == 0)
        def _():
            dg_ref[...] = jnp.zeros_like(dg_ref)

        dg_ref[...] += jnp.sum(dy * xh, axis=0, keepdims=True)

        @pl.when(pl.program_id(0) == nt - 1)
        def _():
            per_lane = jnp.broadcast_to(dg_ref[...], (8, w))
            dg_ref[...] = lax.dot_general(per_lane, f_ref[...], NN, precision=lax.Precision.HIGHEST,
                                          preferred_element_type=F32)[0:1]

    row = pl.BlockSpec((tm, w), lambda i: (i, 0))
    vec = pl.BlockSpec((1, w), lambda i: (0, 0))
    sq = pl.BlockSpec((w, w), lambda i: (0, 0))
    return _pcall(
        body, name=name, grid=(nt,),
        in_specs=[row] * nd + [pl.BlockSpec((tm, w), lambda i: (i, col)), vec, sq, sq],
        out_specs=(row, vec), out_shape=(_sds((t, w), F32), _sds((1, w), F32)),
        compiler_params=_cparams("arbitrary"))(*dys, x, g, _head_sum_matrix(1.0 / DIL_HD), fold)


def _rowdot(a, b, name, tm):
    n, d = a.shape
    tm = min(tm, n)

    def body(a_ref, b_ref, o_ref):
        o_ref[...] = jnp.sum(a_ref[...].astype(F32) * b_ref[...].astype(F32), axis=-1, keepdims=True)

    spec = pl.BlockSpec((tm, d), lambda i: (i, 0))
    return _pcall(body, name=name, grid=(n // tm,), in_specs=[spec, spec],
                  out_specs=pl.BlockSpec((tm, 1), lambda i: (i, 0)), out_shape=_sds((n, 1), F32),
                  compiler_params=_cparams("parallel"))(a, b)


def _add3(a, b, c, name, tm):
    n, d = a.shape
    tm = min(tm, n)

    def body(a_ref, b_ref, c_ref, o_ref):
        o_ref[...] = a_ref[...] + b_ref[...] + c_ref[...]

    spec = pl.BlockSpec((tm, d), lambda i: (i, 0))
    return _pcall(body, name=name, grid=(n // tm,), in_specs=[spec] * 3, out_specs=spec,
                  out_shape=_sds((n, d), F32), compiler_params=_cparams("parallel"))(a, b, c)


def _rope_tables(t):
    inv = ROPE_BASE ** (-np.arange(0, MLA_ROPE, 2, dtype=np.float64) / MLA_ROPE)
    ang = np.arange(t, dtype=np.float64)[:, None] * inv[None, :]
    cos, sin = np.cos(ang), np.sin(ang)
    return (jnp.asarray(np.concatenate([cos, cos], 1), F32), jnp.asarray(np.concatenate([-sin, sin], 1), F32))


def _half_swap():
    p = np.zeros((MLA_ROPE, MLA_ROPE), np.float32)
    half = MLA_ROPE // 2
    for i in range(MLA_ROPE):
        p[(i + half) % MLA_ROPE, i] = 1.0
    return jnp.asarray(p)


def _mla_qk_fwd(x, g, cos_t, sin_t, scale, name, tm):
    n, d = x.shape
    t = cos_t.shape[0]
    tm = min(tm, t)
    nt = t // tm
    swap = _half_swap()

    def body(x_ref, g_ref, c_ref, s_ref, p_ref, o_ref):
        xf = x_ref[...]
        r = lax.rsqrt(jnp.mean(xf * xf, axis=-1, keepdims=True) + EPS)
        y = xf * r * g_ref[...]
        yr = y[:, MLA_NOPE:]
        sw = lax.dot_general(yr, p_ref[...], NN, precision=lax.Precision.HIGHEST, preferred_element_type=F32)
        o_ref[:, :MLA_NOPE] = (y[:, :MLA_NOPE] * scale).astype(o_ref.dtype)
        o_ref[:, MLA_NOPE:] = ((yr * c_ref[...] + sw * s_ref[...]) * scale).astype(o_ref.dtype)

    row = pl.BlockSpec((tm, d), lambda i: (i, 0))
    tab = pl.BlockSpec((tm, MLA_ROPE), lambda i: (i % nt, 0))
    return _pcall(
        body, name=name, grid=(n // tm,),
        in_specs=[row, pl.BlockSpec((1, d), lambda i: (0, 0)), tab, tab,
                  pl.BlockSpec((MLA_ROPE, MLA_ROPE), lambda i: (0, 0))],
        out_specs=row, out_shape=_sds((n, d), BF16),
        compiler_params=_cparams("parallel"))(x, g, cos_t, sin_t, swap)


def _mla_qk_bwd(dy, x, g, cos_t, sin_t, scale, name, tm):
    n, d = x.shape
    t = cos_t.shape[0]
    tm = min(tm, t)
    nt = t // tm
    swap_t = _half_swap().T

    def body(dy_ref, x_ref, g_ref, c_ref, s_ref, p_ref, dx_ref, dg_ref):
        xf = x_ref[...]
        gg = g_ref[...]
        r = lax.rsqrt(jnp.mean(xf * xf, axis=-1, keepdims=True) + EPS)
        xh = xf * r
        dyf = dy_ref[...] * scale
        dyr = dyf[:, MLA_NOPE:]
        back = lax.dot_general(dyr * s_ref[...], p_ref[...], NN, precision=lax.Precision.HIGHEST,
                               preferred_element_type=F32)
        dn_n = dyf[:, :MLA_NOPE]
        dn_r = dyr * c_ref[...] + back
        xh_n, xh_r = xh[:, :MLA_NOPE], xh[:, MLA_NOPE:]
        dxh_n = dn_n * gg[:, :MLA_NOPE]
        dxh_r = dn_r * gg[:, MLA_NOPE:]
        mean = (jnp.sum(dxh_n * xh_n, axis=-1, keepdims=True)
                + jnp.sum(dxh_r * xh_r, axis=-1, keepdims=True)) * (1.0 / d)
        dx_ref[:, :MLA_NOPE] = r * (dxh_n - xh_n * mean)
        dx_ref[:, MLA_NOPE:] = r * (dxh_r - xh_r * mean)

        @pl.when(pl.program_id(0) == 0)
        def _():
            dg_ref[...] = jnp.zeros_like(dg_ref)

        dg_ref[:, :MLA_NOPE] += jnp.sum(dn_n * xh_n, axis=0, keepdims=True)
        dg_ref[:, MLA_NOPE:] += jnp.sum(dn_r * xh_r, axis=0, keepdims=True)

    row = pl.BlockSpec((tm, d), lambda i: (i, 0))
    vec = pl.BlockSpec((1, d), lambda i: (0, 0))
    tab = pl.BlockSpec((tm, MLA_ROPE), lambda i: (i % nt, 0))
    return _pcall(
        body, name=name, grid=(n // tm,),
        in_specs=[row, row, vec, tab, tab, pl.BlockSpec((MLA_ROPE, MLA_ROPE), lambda i: (0, 0))],
        out_specs=(row, vec), out_shape=(_sds((n, d), F32), _sds((1, d), F32)),
        compiler_params=_cparams("arbitrary"))(dy, x, g, cos_t, sin_t, swap_t)


def _causal_mask(i, j, tq, tk, width):
    row = i * tq + lax.broadcasted_iota(jnp.int32, (tq, width), 0)
    col = j * tk + lax.broadcasted_iota(jnp.int32, (tq, width), 1)
    return col <= row


def _causal_steps(nq, nk, tq, tk, q_major):
    if q_major:
        groups = [[(i, j) for j in range((i * tq + tq - 1) // tk + 1)] for i in range(nq)]
        nunit = tk // tq if tk % tq == 0 else 1
    else:
        groups = [[(i, j) for i in range((j * tk) // tq, nq)] for j in range(nk)]
        nunit = tq // tk if tq % tk == 0 else 1
    it, jt, fl = [], [], []
    for g in groups:
        for n, (i, j) in enumerate(g):
            crossing = j * tk + tk - 1 > i * tq
            if q_major:
                unit = tk // nunit
                u = min(nunit, -(-(i * tq + tq - j * tk) // unit)) - 1
            else:
                unit = tq // nunit
                u = max(0, j * tk - i * tq) // unit
            it.append(i)
            jt.append(j)
            fl.append((n == 0) + 2 * (n == len(g) - 1) + 4 * crossing + 8 * (u if crossing else 0))
    return tuple(jnp.asarray(np.array(a, np.int32)) for a in (it, jt, fl)), nunit


def _by_crossing(flags, nunit, update):
    pl.when((flags & 4) == 0)(functools.partial(update, None))
    for u in range(nunit):
        pl.when(((flags & 4) != 0) & ((flags >> 3) == u))(functools.partial(update, u))


def _causal_specs(tq, tk):
    def qs(w):
        return pl.BlockSpec((None, tq, w), lambda h, s, it, jt, fl: (h, it[s], 0))

    def kv(w):
        return pl.BlockSpec((None, tk, w), lambda h, s, it, jt, fl: (h, jt[s], 0))

    return qs, kv


def _mla_fwd(q, k, v, tq, tk):
    nh, t, dq = q.shape
    dv = v.shape[2]
    tq, tk = min(tq, t), min(tk, t)
    tables, nunit = _causal_steps(t // tq, t // tk, tq, tk, True)

    def body(it, jt, fl, q_ref, k_ref, v_ref, o_ref, lse_ref, m_sc, l_sc, acc_sc):
        step = pl.program_id(1)
        i, j, flags = it[step], jt[step], fl[step]

        @pl.when((flags & 1) != 0)
        def _():
            m_sc[...] = jnp.full_like(m_sc, NEG)
            l_sc[...] = jnp.zeros_like(l_sc)
            acc_sc[...] = jnp.zeros_like(acc_sc)

        def update(units):
            wk = tk if units is None else (units + 1) * (tk // nunit)
            s = _dot(q_ref[...], k_ref[:wk, :], NT)
            if units is not None:
                s = jnp.where(_causal_mask(i, j, tq, tk, wk), s, NEG)
            m_prev = m_sc[...]
            m_new = jnp.maximum(m_prev, jnp.max(s, axis=-1, keepdims=True))
            alpha = jnp.exp(m_prev - m_new)
            p = jnp.exp(s - m_new)
            l_sc[...] = alpha * l_sc[...] + jnp.sum(p, axis=-1, keepdims=True)
            acc_sc[...] = alpha * acc_sc[...] + _dot(p.astype(BF16), v_ref[:wk, :], NN)
            m_sc[...] = m_new

        _by_crossing(flags, nunit, update)

        @pl.when((flags & 2) != 0)
        def _():
            o_ref[...] = acc_sc[...] / l_sc[...]
            lse_ref[...] = m_sc[...] + jnp.log(l_sc[...])

    qs, kv = _causal_specs(tq, tk)
    return _pcall(
        body, name="mla_attn_fwd",
        grid_spec=pltpu.PrefetchScalarGridSpec(
            num_scalar_prefetch=3, grid=(nh, tables[0].shape[0]),
            in_specs=[qs(dq), kv(dq), kv(dv)], out_specs=(qs(dv), qs(1)),
            scratch_shapes=[pltpu.VMEM((tq, 1), F32), pltpu.VMEM((tq, 1), F32), pltpu.VMEM((tq, dv), F32)]),
        out_shape=(_sds((nh, t, dv), F32), _sds((nh, t, 1), F32)),
        compiler_params=_cparams("parallel", "arbitrary"))(*tables, q, k, v)


def _mla_bwd(q, k, k_t, v, do, lse_row, dl_row, tq, tk):
    nh, t, dq = q.shape
    dv = v.shape[2]
    tq, tk = min(tq, t), min(tk, t)
    nq = t // tq
    tables, nunit = _causal_steps(nq, t // tk, tq, tk, False)

    def body(it, jt, fl, q_ref, k_ref, kt_ref, v_ref, do_ref, lse_ref, dl_ref, dk_ref, dv_ref, dq_ref, dk_sc, dv_sc):
        step = pl.program_id(1)
        i, j, flags = it[step], jt[step], fl[step]

        def update(units):
            off = 0 if units is None else units * (tq // nunit)
            qq = q_ref[off:, :]
            st = _dot(k_ref[...], qq, NT)
            if units is not None:
                key = j * tk + lax.broadcasted_iota(jnp.int32, (tk, tq - off), 0)
                qry = i * tq + off + lax.broadcasted_iota(jnp.int32, (tk, tq - off), 1)
                st = jnp.where(key <= qry, st, NEG)
            pt = jnp.exp(st - lse_ref[:, off:])
            dob = do_ref[off:, :].astype(BF16)
            dpt = _dot(v_ref[...], dob, NT)
            dst = pt * (dpt - dl_ref[:, off:])
            dsb = dst.astype(BF16)
            dv_part = _dot(pt.astype(BF16), dob, NN)
            dk_part = _dot(dsb, qq, NN)
            dq_part = _dot(kt_ref[...], dsb, NN)

            @pl.when((flags & 1) != 0)
            def _():
                dv_sc[...] = dv_part
                dk_sc[...] = dk_part

            @pl.when((flags & 1) == 0)
            def _():
                dv_sc[...] += dv_part
                dk_sc[...] += dk_part

            if off == 0:
                @pl.when(j == 0)
                def _():
                    dq_ref[i] = dq_part

                @pl.when(j != 0)
                def _():
                    dq_ref[i] += dq_part
            else:
                dq_ref[i, :, off:] += dq_part

        _by_crossing(flags, nunit, update)

        @pl.when((flags & 2) != 0)
        def _():
            dk_ref[...] = dk_sc[...]
            dv_ref[...] = dv_sc[...]

    qs, kv = _causal_specs(tq, tk)
    rowv = pl.BlockSpec((None, 1, tq), lambda h, s, it, jt, fl: (h, 0, it[s]))
    ktv = pl.BlockSpec((None, dq, tk), lambda h, s, it, jt, fl: (h, 0, jt[s]))
    whole = pl.BlockSpec((None, nq, dq, tq), lambda h, s, it, jt, fl: (h, 0, 0, 0))
    return _pcall(
        body, name="mla_attn_bwd",
        grid_spec=pltpu.PrefetchScalarGridSpec(
            num_scalar_prefetch=3, grid=(nh, tables[0].shape[0]),
            in_specs=[qs(dq), kv(dq), ktv, kv(dv), qs(dv), rowv, rowv], out_specs=(kv(dq), kv(dv), whole),
            scratch_shapes=[pltpu.VMEM((tk, dq), F32), pltpu.VMEM((tk, dv), F32)]),
        out_shape=(_sds((nh, t, dq), F32), _sds((nh, t, dv), F32), _sds((nh, nq, dq, tq), F32)),
        compiler_params=_cparams("parallel", "arbitrary"))(*tables, q, k, k_t, v, do, lse_row, dl_row)


def _loss_head(y, target, tm):
    t, d = y.shape
    tm = min(tm, t)
    nt = t // tm

    def body(y_ref, t_ref, dy_ref, loss_ref, acc):
        i = pl.program_id(0)
        err = y_ref[...] - t_ref[...]
        dy_ref[...] = err * (1.0 / d)

        @pl.when(i == 0)
        def _():
            acc[...] = jnp.zeros_like(acc)

        acc[...] += jnp.sum(err * err, axis=0, keepdims=True)

        @pl.when(i == nt - 1)
        def _():
            loss_ref[0, 0] = jnp.sum(acc[...]) * (0.5 / d)

    spec = pl.BlockSpec((tm, d), lambda i: (i, 0))
    return _pcall(
        body, name="loss_head", grid=(nt,), in_specs=[spec, spec],
        out_specs=(spec, pl.BlockSpec(memory_space=pltpu.SMEM)),
        out_shape=(_sds((t, d), F32), _sds((1, 1), F32)),
        scratch_shapes=[pltpu.VMEM((1, d), F32)],
        compiler_params=_cparams("arbitrary"))(y, target)


def _adamw(w, g, m, v, name):
    r, c = w.shape
    tr = r
    for cand in (256, 128, 64, 32, 16, 8):
        if r % cand == 0:
            tr = cand
            break

    def body(w_ref, g_ref, m_ref, v_ref, d_ref, nm_ref, nv_ref):
        gg = g_ref[...]
        nm = ADAM_B1 * m_ref[...] + (1.0 - ADAM_B1) * gg
        nv = ADAM_B2 * v_ref[...] + (1.0 - ADAM_B2) * (gg * gg)
        m_hat = nm / (1.0 - ADAM_B1 ** ADAM_STEP)
        v_hat = nv / (1.0 - ADAM_B2 ** ADAM_STEP)
        d_ref[...] = -ADAM_LR * (m_hat / (jnp.sqrt(v_hat) + ADAM_EPS) + ADAM_WD * w_ref[...])
        nm_ref[...] = nm
        nv_ref[...] = nv

    spec = pl.BlockSpec((tr, c), lambda i: (i, 0))
    sd = _sds((r, c), F32)
    return _pcall(body, name=name, grid=(r // tr,), in_specs=[spec] * 4, out_specs=(spec,) * 3,
                  out_shape=(sd, sd, sd), compiler_params=_cparams("parallel"))(w, g, m, v)


MESH_ID = pl.DeviceIdType.MESH
HBM_SPEC = pl.BlockSpec(memory_space=pltpu.HBM)


def _place():
    return lax.axis_index("x"), lax.axis_index("y"), lax.axis_index("c")


def _other_chips(x, y):
    return [(1 - x, y), (x, 1 - y), (1 - x, 1 - y)]


def _remote(src, dst, send_sems, recv_sems, k, to):
    return pltpu.make_async_remote_copy(src_ref=src, dst_ref=dst, send_sem=send_sems.at[k], recv_sem=recv_sems.at[k],
                                        device_id=to, device_id_type=MESH_ID)


def _halves(arrays):
    for a in arrays:
        assert a.shape[-2] % 32 == 0
    return [a.shape[-2] // 2 for a in arrays]


def _gather_weights(blocks):
    n = len(blocks)
    halves = _halves(blocks)

    def body(*refs):
        srcs, outs, send_sems, recv_sems = refs[:n], refs[n:2 * n], refs[2 * n], refs[2 * n + 1]
        x, y, c = _place()
        me = 2 * x + y
        sibling = (x, y, 1 - c)
        chips = _other_chips(x, y)

        def part(a, chip, core):
            return outs[a].at[chip, pl.ds(core * halves[a], halves[a]), :]

        for a in range(n):
            mine = srcs[a].at[pl.ds(c * halves[a], halves[a]), :]
            for k, (cx, cy) in enumerate(chips):
                _remote(mine, part(a, me, c), send_sems, recv_sems, 6 * a + k, (cx, cy, c)).start()
        for k, (cx, cy) in enumerate(chips):
            for a in range(n):
                got = part(a, 2 * cx + cy, c)
                _remote(got, got, send_sems, recv_sems, 6 * a + k, (x, y, c)).wait_recv()
                _remote(got, got, send_sems, recv_sems, 6 * a + 3 + k, sibling).start()
        for k, (cx, cy) in enumerate(chips):
            for a in range(n):
                got = part(a, 2 * cx + cy, 1 - c)
                _remote(got, got, send_sems, recv_sems, 6 * a + 3 + k, (x, y, c)).wait_recv()
        for a in range(n):
            sent = part(a, me, c)
            for k in range(6):
                _remote(sent, sent, send_sems, recv_sems, 6 * a + k, (x, y, c)).wait_send()

    return _pcall(
        body, name="gather_weights", in_specs=[HBM_SPEC] * n, out_specs=tuple([HBM_SPEC] * n),
        out_shape=tuple(_sds((N_CHIPS,) + b.shape, b.dtype) for b in blocks),
        scratch_shapes=[pltpu.SemaphoreType.DMA((6 * n,)), pltpu.SemaphoreType.DMA((6 * n,))],
    )(*blocks)


def _reduce_cores(grads):
    n = len(grads)
    halves = _halves(grads)

    def body(*refs):
        gs, outs, send_sems, recv_sems = refs[:n], refs[n:2 * n], refs[2 * n], refs[2 * n + 1]
        x, y, c = _place()
        for a in range(n):
            for j in range(N_CHIPS):
                _remote(gs[a].at[j, pl.ds((1 - c) * halves[a], halves[a]), :], outs[a].at[j],
                        send_sems, recv_sems, a, (x, y, 1 - c)).start()
        for a in range(n):
            _remote(gs[a].at[:, pl.ds((1 - c) * halves[a], halves[a]), :], outs[a],
                    send_sems, recv_sems, a, (x, y, c)).wait()

    return _pcall(
        body, name="reduce_cores", in_specs=[HBM_SPEC] * n, out_specs=tuple([HBM_SPEC] * n),
        out_shape=tuple(_sds((N_CHIPS, h, g.shape[2]), g.dtype) for g, h in zip(grads, halves)),
        scratch_shapes=[pltpu.SemaphoreType.DMA((n,)), pltpu.SemaphoreType.DMA((n,))],
    )(*grads)


def _scatter_chips(parts):
    n = len(parts)

    def body(*refs):
        ps, outs, send_sems, recv_sems = refs[:n], refs[n:2 * n], refs[2 * n], refs[2 * n + 1]
        x, y, c = _place()
        for a in range(n):
            for k, (cx, cy) in enumerate(_other_chips(x, y)):
                _remote(ps[a].at[2 * cx + cy], outs[a].at[k], send_sems, recv_sems, 3 * a + k, (cx, cy, c)).start()
        for a in range(n):
            for k in range(3):
                _remote(ps[a].at[k], outs[a].at[k], send_sems, recv_sems, 3 * a + k, (x, y, c)).wait()

    return _pcall(
        body, name="scatter_chips", in_specs=[HBM_SPEC] * n, out_specs=tuple([HBM_SPEC] * n),
        out_shape=tuple(_sds((3,) + p.shape[1:], p.dtype) for p in parts),
        scratch_shapes=[pltpu.SemaphoreType.DMA((3 * n,)), pltpu.SemaphoreType.DMA((3 * n,))],
    )(*parts)


def _sum_partials(received, parts, place):
    n = len(parts)
    steps = 2
    tiles = [p.shape[1] // steps for p in parts]

    def body(place_ref, *refs):
        rs, ps, outs = refs[:n], refs[n:2 * n], refs[2 * n:]
        for a in range(n):
            tot = ps[a][...].astype(F32)
            for k in range(3):
                tot = tot + rs[a][k].astype(F32)
            outs[a][...] = tot

    cols = [p.shape[2] for p in parts]
    return _pcall(
        body, name="sum_chip_partials",
        grid_spec=pltpu.PrefetchScalarGridSpec(
            num_scalar_prefetch=1, grid=(steps,),
            in_specs=[pl.BlockSpec((3, tm, w), lambda i, pc: (0, i, 0)) for tm, w in zip(tiles, cols)]
            + [pl.BlockSpec((None, tm, w), lambda i, pc: (pc[0], i, 0)) for tm, w in zip(tiles, cols)],
            out_specs=tuple(pl.BlockSpec((tm, w), lambda i, pc: (pc[1] * steps + i, 0)) for tm, w in zip(tiles, cols))),
        out_shape=tuple(_sds((2 * p.shape[1], p.shape[2]), F32) for p in parts),
        compiler_params=_cparams("parallel"))(place, *received, *parts)


def _share_cores(blocks):
    n = len(blocks)
    halves = _halves(blocks)

    def body(*refs):
        srcs, outs, send_sems, recv_sems = refs[:n], refs[n:2 * n], refs[2 * n], refs[2 * n + 1]
        x, y, c = _place()
        for a in range(n):
            piece = pl.ds(c * halves[a], halves[a])
            _remote(srcs[a].at[piece, :], outs[a].at[piece, :], send_sems, recv_sems, a, (x, y, 1 - c)).start()
        for a in range(n):
            mine = outs[a].at[pl.ds(c * halves[a], halves[a]), :]
            theirs = outs[a].at[pl.ds((1 - c) * halves[a], halves[a]), :]
            _remote(mine, theirs, send_sems, recv_sems, a, (x, y, c)).wait()

    return _pcall(
        body, name="share_cores", in_specs=[HBM_SPEC] * n, out_specs=tuple([HBM_SPEC] * n),
        out_shape=tuple(_sds(b.shape, b.dtype) for b in blocks), input_output_aliases={a: a for a in range(n)},
        scratch_shapes=[pltpu.SemaphoreType.DMA((n,)), pltpu.SemaphoreType.DMA((n,))],
    )(*blocks)


def _sum_blocks(stacked, name, tm):
    n, rows, lanes = stacked.shape
    tm = min(tm, rows)

    def body(s_ref, o_ref):
        tot = s_ref[n - 1].astype(F32)
        for k in range(n - 1):
            tot = tot + s_ref[k].astype(F32)
        o_ref[...] = tot

    return _pcall(body, name=name, grid=(rows // tm,),
                  in_specs=[pl.BlockSpec((n, tm, lanes), lambda i: (0, i, 0))],
                  out_specs=pl.BlockSpec((tm, lanes), lambda i: (i, 0)), out_shape=_sds((rows, lanes), F32),
                  compiler_params=_cparams("parallel"))(stacked)


def _add_halves(grads, theirs, core):
    n = len(grads)
    steps = 2
    tiles = [t.shape[1] // steps for t in theirs]
    cols = [t.shape[2] for t in theirs]

    def body(c_ref, *refs):
        gs, ts, outs = refs[:n], refs[n:2 * n], refs[2 * n:]
        for a in range(n):
            outs[a][...] = (gs[a][...] + ts[a][...]).astype(BF16)

    own = [pl.BlockSpec((None, tm, w), lambda k, i, c: (k, c[0] * steps + i, 0)) for tm, w in zip(tiles, cols)]
    same = [pl.BlockSpec((None, tm, w), lambda k, i, c: (k, i, 0)) for tm, w in zip(tiles, cols)]
    return _pcall(
        body, name="add_core_halves",
        grid_spec=pltpu.PrefetchScalarGridSpec(
            num_scalar_prefetch=1, grid=(N_CHIPS, steps), in_specs=own + same, out_specs=tuple(same)),
        out_shape=tuple(_sds(t.shape, BF16) for t in theirs),
        compiler_params=_cparams("parallel", "parallel"))(core, *grads, *theirs)


def _allreduce_small(part):
    rows, lanes = part.shape
    ndev = 8

    def body(src, tot, buf, send_sems, recv_sems):
        x, y, c = _place()
        me = 4 * x + 2 * y + c
        buf[me] = src[...]
        sends = []
        for k in range(1, ndev):
            peer = (x ^ (k >> 2), y ^ ((k >> 1) & 1), c ^ (k & 1))
            cp = _remote(src, buf.at[me], send_sems, recv_sems, k - 1, peer)
            cp.start()
            sends.append(cp)
        for k in range(1, ndev):
            theirs = buf.at[me ^ k]
            _remote(theirs, theirs, send_sems, recv_sems, k - 1, (x, y, c)).wait_recv()
        for cp in sends:
            cp.wait_send()
        acc = buf[0]
        for d in range(1, ndev):
            acc = acc + buf[d]
        tot[...] = acc

    vm = pl.BlockSpec(memory_space=pltpu.VMEM)
    return _pcall(
        body, name="allreduce_small", in_specs=[vm], out_specs=vm, out_shape=_sds((rows, lanes), F32),
        scratch_shapes=[pltpu.VMEM((ndev, rows, lanes), F32), pltpu.SemaphoreType.DMA((ndev - 1,)),
                        pltpu.SemaphoreType.DMA((ndev - 1,))],
    )(part)


def _pack_small(vals):
    parts = []
    for name, shape, r in SMALL:
        flat = vals[name].reshape(-1).astype(F32)
        parts.append(jnp.pad(flat, (0, r * LANES - flat.shape[0])).reshape(r, LANES))
    used = sum(r for _, _, r in SMALL)
    parts.append(jnp.zeros((SMALL_ROWS - used, LANES), F32))
    return jnp.concatenate(parts, axis=0)


def _unpack_small(packed):
    out, off = {}, 0
    for name, shape, r in SMALL:
        n = int(np.prod(shape))
        out[name] = packed[off:off + r].reshape(-1)[:n].reshape(shape)
        off += r
    return out


def _heads_major(a, nh):
    t = a.shape[0]
    return a.reshape(t, nh, a.shape[1] // nh).transpose(1, 0, 2)


def _tokens_major(a):
    nh, t, w = a.shape
    return a.transpose(1, 0, 2).reshape(t, nh * w)


def _local_step(x, target, small, wfull):
    t = x.shape[0]
    nh, hd = DIL_HEADS, DIL_HD
    w_in = wfull["w_in"].transpose(1, 0, 2).reshape(D_MODEL, -1)
    w_out = wfull["w_out"].reshape(D_MODEL, D_MODEL)
    w_qb, w_kvb = wfull["mla_w_q_b"], wfull["mla_w_kv_b"]
    grads_s, grads_b = {}, {}

    x1, ffn1_saved = _ffn_fwd(x, small["ffn1_norm"], wfull["ffn1_w_gate"], wfull["ffn1_w_up"],
                              wfull["ffn1_w_down"], "ffn1")
    hm = _rms_fwd(x1, small["mix_norm"], BF16, "mix_norm", 512)
    proj = _mm_simple("in_proj", hm, w_in, NN, F32, tm=1024)
    cq, ckv, k_pe = proj[:, 1536:1792], proj[:, 1792:1920], proj[:, 1920:1984]

    gq, gk = jnp.tile(small["dil_q_norm"], (1, nh)), jnp.tile(small["dil_k_norm"], (1, nh))
    qn = _head_norm_fwd(proj, 0, gq, "dil_q_norm", 512)
    kn = _head_norm_fwd(proj, 1, gk, "dil_k_norm", 512)
    v_d = proj[:, 2 * DIL_WIDTH:3 * DIL_WIDTH].astype(BF16)
    bias = _bias_tiles(small["rel_bias"]).reshape(3, nh // 2, 2 * QB, QB + DIL_W)
    outs, lses = [], []
    for b, dil in enumerate(DIL_DILATIONS):
        o_b, lse_b = _dil_fwd(qn, kn, v_d, bias[b], dil, f"dil_fwd_{dil}")
        outs.append(o_b)
        lses.append(lse_b)
    o_dil, lse_tot, od = _dil_merge(outs, lses, small["out_norm_dil"], 512)

    mh = MLA_HEADS
    cos_t, sin_t = _rope_tables(t)
    cqn = _rms_fwd(cq, small["mla_q_a_norm"], BF16, "mla_q_a_norm", 512)
    ckvn = _rms_fwd(ckv, small["mla_kv_a_norm"], BF16, "mla_kv_a_norm", 512)
    tm = min(512, t)

    th = min(2048, t)

    def head_proj(name, a, w, width):
        k = a.shape[1]
        return _mm(name, (mh, t // th, 1),
                   [(a, pl.BlockSpec((th, k), lambda h, i, r: (i, 0)), w, pl.BlockSpec((None, k, width), lambda h, i, r: (h, 0, 0)))],
                   NN, _sds((mh, t, width), F32), pl.BlockSpec((None, th, width), lambda h, i, r: (h, i, 0)), (th, width))

    q_raw = head_proj("mla_q_proj", cqn, w_qb, MLA_QK)
    kv_raw = head_proj("mla_kv_proj", ckvn, w_kvb, MLA_NOPE + MLA_V)
    k_raw = jnp.concatenate([kv_raw[:, :, :MLA_NOPE], jnp.broadcast_to(k_pe[None], (mh, t, MLA_ROPE))], axis=2)
    v_m = kv_raw[:, :, MLA_NOPE:].astype(BF16)
    q_raw2, k_raw2 = q_raw.reshape(mh * t, MLA_QK), k_raw.reshape(mh * t, MLA_QK)
    q_scale = MLA_QK ** -0.5
    q_m = _mla_qk_fwd(q_raw2, small["mla_q_norm"], cos_t, sin_t, q_scale, "mla_q_rope", 2048).reshape(mh, t, MLA_QK)
    k_m = _mla_qk_fwd(k_raw2, small["mla_k_norm"], cos_t, sin_t, 1.0, "mla_k_rope", 2048).reshape(mh, t, MLA_QK)
    o_mla_h, lse_m = _mla_fwd(q_m, k_m, v_m, 512, 2048)
    o_mla = _tokens_major(o_mla_h)

    om = _rms_fwd(o_mla, small["out_norm_mla"], BF16, "out_norm_mla", 512)
    half_w = DIL_WIDTH
    row = pl.BlockSpec((tm, D_MODEL), lambda i, j, r: (i, 0))
    act_spec = pl.BlockSpec((tm, half_w), lambda i, j, r: (i, 0))
    x2 = _mm("out_proj", (t // tm, 1, 1),
             [(od, act_spec, w_out, pl.BlockSpec((half_w, D_MODEL), lambda i, j, r: (0, 0))),
              (om, act_spec, w_out, pl.BlockSpec((half_w, D_MODEL), lambda i, j, r: (1, 0)))],
             NN, _sds((t, D_MODEL), F32), row, (tm, D_MODEL), res=(x1, row))
    x3, ffn2_saved = _ffn_fwd(x2, small["ffn2_norm"], wfull["ffn2_w_gate"], wfull["ffn2_w_up"],
                              wfull["ffn2_w_down"], "ffn2")
    dy, loss = _loss_head(x3, target, 512)

    dx2, grads_s["ffn2_norm"], grads_b["ffn2_w_gate"], grads_b["ffn2_w_up"], grads_b["ffn2_w_down"] = _ffn_bwd(
        dy, x2, small["ffn2_norm"], wfull["ffn2_w_gate"], wfull["ffn2_w_up"], wfull["ffn2_w_down"], ffn2_saved, "ffn2")

    d_ocat = _mm_simple("out_proj_dx", dx2, w_out, NT, F32, tm=1024)
    dw_out_d = _mm_simple("out_proj_dw_dil", od, dx2, TN, F32, tk=2048)
    dw_out_m = _mm_simple("out_proj_dw_mla", om, dx2, TN, F32, tk=2048)
    grads_b["w_out"] = jnp.concatenate([dw_out_d, dw_out_m], axis=0).reshape(N_CHIPS, D_MODEL // N_CHIPS, D_MODEL)
    do_dil, grads_s["out_norm_dil"] = _rms_bwd([d_ocat[:, :half_w]], o_dil, small["out_norm_dil"], None, "out_norm_dil_bwd", 512)
    do_mla, grads_s["out_norm_mla"] = _rms_bwd([d_ocat[:, half_w:]], o_mla, small["out_norm_mla"], None, "out_norm_mla_bwd", 512)

    do_m = _heads_major(do_mla, mh)
    dl_m = _rowdot(do_m.reshape(mh * t, MLA_V), o_mla_h.reshape(mh * t, MLA_V), "mla_delta", 2048).reshape(mh, t, 1)
    dk_m, dv_m, dq_t = _mla_bwd(q_m, k_m, k_m.transpose(0, 2, 1), v_m, do_m, lse_m.reshape(mh, 1, t),
                                dl_m.reshape(mh, 1, t), 2048, 512)
    dq_m = dq_t.transpose(0, 1, 3, 2).reshape(mh, t, MLA_QK)
    dq_raw, grads_s["mla_q_norm"] = _mla_qk_bwd(dq_m.reshape(mh * t, MLA_QK), q_raw2, small["mla_q_norm"],
                                                 cos_t, sin_t, q_scale, "mla_q_rope_bwd", 2048)
    dk_raw, grads_s["mla_k_norm"] = _mla_qk_bwd(dk_m.reshape(mh * t, MLA_QK), k_raw2, small["mla_k_norm"],
                                                 cos_t, sin_t, 1.0, "mla_k_rope_bwd", 2048)
    dq_raw = dq_raw.reshape(mh, t, MLA_QK)
    dk_raw = dk_raw.reshape(mh, t, MLA_QK)
    dkv_raw = jnp.concatenate([dk_raw[:, :, :MLA_NOPE], dv_m], axis=2)
    dk_pe_h = dk_raw[:, :, MLA_NOPE:]

    def head_proj_dx(name, d, w):
        width, k = d.shape[2], w.shape[1]
        pairs = [(d, pl.BlockSpec((None, th, width), lambda i, j, r, h=h: (h, i, 0)),
                  w, pl.BlockSpec((None, k, width), lambda i, j, r, h=h: (h, 0, 0))) for h in range(mh)]
        return _mm(name, (t // th, 1, 1), pairs, NT, _sds((t, k), F32),
                   pl.BlockSpec((th, k), lambda i, j, r: (i, 0)), (th, k))

    def head_proj_dw(name, a, d):
        width, k = d.shape[2], a.shape[1]
        return _mm(name, (mh, 1, t // th),
                   [(a, pl.BlockSpec((th, k), lambda h, j, r: (r, 0)), d, pl.BlockSpec((None, th, width), lambda h, j, r: (h, r, 0)))],
                   TN, _sds((mh, k, width), F32), pl.BlockSpec((None, k, width), lambda h, j, r: (h, 0, 0)), (k, width))

    d_cqn = head_proj_dx("mla_q_proj_dx", dq_raw, w_qb)
    d_ckvn = head_proj_dx("mla_kv_proj_dx", dkv_raw, w_kvb)
    grads_b["mla_w_q_b"] = head_proj_dw("mla_q_proj_dw", cqn, dq_raw)
    grads_b["mla_w_kv_b"] = head_proj_dw("mla_kv_proj_dw", ckvn, dkv_raw)
    d_cq, grads_s["mla_q_a_norm"] = _rms_bwd([d_cqn], cq, small["mla_q_a_norm"], None, "mla_q_a_norm_bwd", 512)
    d_ckv, grads_s["mla_kv_a_norm"] = _rms_bwd([d_ckvn], ckv, small["mla_kv_a_norm"], None, "mla_kv_a_norm_bwd", 512)
    d_kpe = _sum_blocks(dk_pe_h.reshape(mh, t * MLA_ROPE // LANES, LANES), "mla_kpe_sum", 1024).reshape(t, MLA_ROPE)

    stats = _dil_stats(do_dil, o_dil, lse_tot, 512)
    do_db = do_dil.astype(BF16)
    dqs, dks, dvs, dtiles = [], [], [], []
    for b, dil in enumerate(DIL_DILATIONS):
        dq_b, dk_b, dv_b, db_b = _dil_bwd(qn, kn, v_d, do_db, stats, bias[b], dil, f"dil_bwd_{dil}")
        dqs.append(dq_b)
        dks.append(dk_b)
        dvs.append(dv_b)
        dtiles.append(db_b)
    grads_s["rel_bias"] = _bias_grad(jnp.stack(dtiles).reshape(3, nh, QB, QB + DIL_W))
    dq_a, dgq = _head_norm_bwd(dqs, proj, 0, gq, "dil_q_norm_bwd", 512)
    dk_a, dgk = _head_norm_bwd(dks, proj, 1, gk, "dil_k_norm_bwd", 512)
    grads_s["dil_q_norm"], grads_s["dil_k_norm"] = dgq[:, :hd], dgk[:, :hd]
    dv_a = _add3(dvs[0], dvs[1], dvs[2], "dil_dv_sum", 512)

    dparts = [dq_a, dk_a, dv_a, d_cq, d_ckv, d_kpe]
    t2 = min(1024, t)
    pairs, dw_parts, lo = [], [], 0
    for n, dpart in enumerate(dparts):
        width = dpart.shape[1]
        w_part = w_in[:, lo:lo + width]
        pairs.append((dpart, pl.BlockSpec((t2, width), lambda i, j, r: (i, 0)),
                      w_part, pl.BlockSpec((D_MODEL, width), lambda i, j, r: (0, 0))))
        dw_parts.append(_mm_simple(f"in_proj_dw_{n}", hm, dpart, TN, F32, tk=2048))
        lo += width
    d_hm = _mm("in_proj_dx", (t // t2, 1, 1), pairs, NT, _sds((t, D_MODEL), F32),
               pl.BlockSpec((t2, D_MODEL), lambda i, j, r: (i, 0)), (t2, D_MODEL))
    dw_in = jnp.concatenate(dw_parts, axis=1)
    grads_b["w_in"] = dw_in.reshape(D_MODEL, N_CHIPS, -1).transpose(1, 0, 2)
    dx1, grads_s["mix_norm"] = _rms_bwd([d_hm], x1, small["mix_norm"], dx2, "mix_norm_bwd", 512)
    dx, grads_s["ffn1_norm"], grads_b["ffn1_w_gate"], grads_b["ffn1_w_up"], grads_b["ffn1_w_down"] = _ffn_bwd(
        dx1, x, small["ffn1_norm"], wfull["ffn1_w_gate"], wfull["ffn1_w_up"], wfull["ffn1_w_down"], ffn1_saved, "ffn1")
    return loss, dx, grads_s, grads_b


def kernel(x, ffn1_norm, ffn1_w_gate, ffn1_w_up, ffn1_w_down, mix_norm, w_in, dil_q_norm, dil_k_norm, rel_bias, mla_q_a_norm, mla_w_q_b, mla_kv_a_norm, mla_w_kv_b, mla_q_norm, mla_k_norm, out_norm_dil, out_norm_mla, w_out, ffn2_norm, ffn2_w_gate, ffn2_w_up, ffn2_w_down, loss_target, m_ffn1_norm, m_ffn1_w_gate, m_ffn1_w_up, m_ffn1_w_down, m_mix_norm, m_w_in, m_dil_q_norm, m_dil_k_norm, m_rel_bias, m_mla_q_a_norm, m_mla_w_q_b, m_mla_kv_a_norm, m_mla_w_kv_b, m_mla_q_norm, m_mla_k_norm, m_out_norm_dil, m_out_norm_mla, m_w_out, m_ffn2_norm, m_ffn2_w_gate, m_ffn2_w_up, m_ffn2_w_down, v_ffn1_norm, v_ffn1_w_gate, v_ffn1_w_up, v_ffn1_w_down, v_mix_norm, v_w_in, v_dil_q_norm, v_dil_k_norm, v_rel_bias, v_mla_q_a_norm, v_mla_w_q_b, v_mla_kv_a_norm, v_mla_w_kv_b, v_mla_q_norm, v_mla_k_norm, v_out_norm_dil, v_out_norm_mla, v_w_out, v_ffn2_norm, v_ffn2_w_gate, v_ffn2_w_up, v_ffn2_w_down):
    given = dict(locals())
    big_names = [name for name, _ in BIG]
    small_names = [name for name, _, _ in SMALL]

    chip = (2 * lax.axis_index("x") + lax.axis_index("y")).astype(jnp.int32)
    core = lax.axis_index("c").astype(jnp.int32)
    mine = [given[n].astype(BF16) for n in big_names]
    gathered = _gather_weights([m[0] for m in mine])
    wfull = {n: lax.dynamic_update_slice(g, m, (chip, 0, 0)) for n, g, m in zip(big_names, gathered, mine)}
    small = {n: given[n] for n in small_names}

    loss, dx, grads_s, grads_b = _local_step(x[0], loss_target[0], small, wfull)
    loss = lax.psum(loss[0, 0], ("x", "y", "c"))

    partial = [grads_b[n] for n in big_names]
    chip_part = _add_halves(partial, _reduce_cores(partial), core.reshape(1))
    reduced = _sum_partials(_scatter_chips(chip_part), chip_part, jnp.stack([chip, core]))
    g_big = dict(zip(big_names, _share_cores(reduced)))
    g_small = _unpack_small(_allreduce_small(_pack_small(grads_s)))

    grad, delta, new_m, new_v = {}, {}, {}, {}
    for name, shape in BIG:
        g2 = g_big[name]
        d_, m_, v_ = _adamw(given[name].reshape(shape), g2, given["m_" + name].reshape(shape),
                            given["v_" + name].reshape(shape), f"adamw_{name}")
        full = given[name].shape
        grad[name], delta[name], new_m[name], new_v[name] = (a.reshape(full) for a in (g2, d_, m_, v_))
    ps = {k: _pack_small({n: given[pre + n] for n in small_names}) for k, pre in (("w", ""), ("m", "m_"), ("v", "v_"))}
    gs_packed = _pack_small(g_small)
    d_s, m_s, v_s = (_unpack_small(a) for a in _adamw(ps["w"], gs_packed, ps["m"], ps["v"], "adamw_small"))
    for name in small_names:
        grad[name], delta[name], new_m[name], new_v[name] = g_small[name], d_s[name], m_s[name], v_s[name]

    return (loss, dx[None], *[grad[n] for n in WEIGHTS], *[delta[n] for n in WEIGHTS],
            *[new_m[n] for n in WEIGHTS], *[new_v[n] for n in WEIGHTS])
```

```python
import functools

import numpy as np
import jax
import jax.numpy as jnp
from jax import lax
from jax.experimental import pallas as pl
from jax.experimental.pallas import tpu as pltpu

F32 = jnp.float32
BF16 = jnp.bfloat16

D_MODEL = 1024
D_FF = 2816
N_CHIPS = 4
DIL_HEADS = 8
DIL_HD = 64
DIL_WIDTH = 512
DIL_DILATIONS = (1, 4, 16)
DIL_W = 128
QB = 128
MLA_HEADS = 4
MLA_NOPE = 128
MLA_ROPE = 64
MLA_QK = 192
MLA_V = 128
MLA_Q_RANK = 256
MLA_KV_RANK = 128
ROPE_BASE = 10000.0
REL_BUCKETS = 32
REL_MAX_DIST = 2048
FFN_RESID = 0.5
EPS = 1e-6
NEG = -1e30
LANES = 128

ADAM_LR = 0.001
ADAM_B1 = 0.9
ADAM_B2 = 0.999
ADAM_EPS = 1e-08
ADAM_WD = 0.01
ADAM_STEP = 10

NT = (((1,), (1,)), ((), ()))
NN = (((1,), (0,)), ((), ()))
TN = (((0,), (0,)), ((), ()))

BIG = (
    ("ffn1_w_gate", (D_MODEL, D_FF // N_CHIPS)),
    ("ffn1_w_up", (D_MODEL, D_FF // N_CHIPS)),
    ("ffn1_w_down", (D_FF // N_CHIPS, D_MODEL)),
    ("w_in", (D_MODEL, 1984 // N_CHIPS)),
    ("mla_w_q_b", (MLA_Q_RANK, MLA_QK)),
    ("mla_w_kv_b", (MLA_KV_RANK, MLA_NOPE + MLA_V)),
    ("w_out", (D_MODEL // N_CHIPS, D_MODEL)),
    ("ffn2_w_gate", (D_MODEL, D_FF // N_CHIPS)),
    ("ffn2_w_up", (D_MODEL, D_FF // N_CHIPS)),
    ("ffn2_w_down", (D_FF // N_CHIPS, D_MODEL)),
)
SMALL = (
    ("ffn1_norm", (1, 1024), 8), ("mix_norm", (1, 1024), 8), ("dil_q_norm", (1, 64), 1),
    ("dil_k_norm", (1, 64), 1), ("rel_bias", (8, 32), 2), ("mla_q_a_norm", (1, 256), 2),
    ("mla_kv_a_norm", (1, 128), 1), ("mla_q_norm", (1, 192), 2), ("mla_k_norm", (1, 192), 2),
    ("out_norm_dil", (1, 512), 4), ("out_norm_mla", (1, 512), 4), ("ffn2_norm", (1, 1024), 8),
)
SMALL_ROWS = 48
WEIGHTS = ("ffn1_norm", "ffn1_w_gate", "ffn1_w_up", "ffn1_w_down", "mix_norm", "w_in", "dil_q_norm",
           "dil_k_norm", "rel_bias", "mla_q_a_norm", "mla_w_q_b", "mla_kv_a_norm", "mla_w_kv_b",
           "mla_q_norm", "mla_k_norm", "out_norm_dil", "out_norm_mla", "w_out", "ffn2_norm",
           "ffn2_w_gate", "ffn2_w_up", "ffn2_w_down")


def _pcall(body, **kw):
    return pl.pallas_call(body, **kw)


def _cparams(*sem):
    return pltpu.CompilerParams(dimension_semantics=sem)


def _sds(shape, dtype):
    return jax.ShapeDtypeStruct(shape, dtype)


def _dot(a, b, dn):
    return lax.dot_general(a, b, dn, preferred_element_type=F32)


def _rms_fwd(x, g, out_dtype, name, tm):
    n, d = x.shape
    tm = min(tm, n)

    def body(x_ref, g_ref, o_ref):
        xf = x_ref[...].astype(F32)
        r = lax.rsqrt(jnp.mean(xf * xf, axis=-1, keepdims=True) + EPS)
        o_ref[...] = (xf * r * g_ref[...]).astype(o_ref.dtype)

    return _pcall(
        body, name=name, grid=(n // tm,),
        in_specs=[pl.BlockSpec((tm, d), lambda i: (i, 0)), pl.BlockSpec((1, d), lambda i: (0, 0))],
        out_specs=pl.BlockSpec((tm, d), lambda i: (i, 0)),
        out_shape=_sds((n, d), out_dtype), compiler_params=_cparams("parallel"))(x, g)


def _rms_bwd(dys, x, g, res, name, tm):
    n, d = x.shape
    tm = min(tm, n)
    nd = len(dys)
    has_res = res is not None

    def body(*refs):
        dy_refs = refs[:nd]
        x_ref, g_ref = refs[nd], refs[nd + 1]
        res_ref = refs[nd + 2] if has_res else None
        dx_ref, dg_ref = refs[-2], refs[-1]
        dy = dy_refs[0][...].astype(F32)
        for r_ in dy_refs[1:]:
            dy = dy + r_[...].astype(F32)
        xf = x_ref[...].astype(F32)
        r = lax.rsqrt(jnp.mean(xf * xf, axis=-1, keepdims=True) + EPS)
        xh = xf * r
        dxh = dy * g_ref[...]
        dx = r * (dxh - xh * jnp.mean(dxh * xh, axis=-1, keepdims=True))
        if has_res:
            dx = dx + res_ref[...]
        dx_ref[...] = dx

        @pl.when(pl.program_id(0) == 0)
        def _():
            dg_ref[...] = jnp.zeros_like(dg_ref)

        dg_ref[...] += jnp.sum(dy * xh, axis=0, keepdims=True)

    row = pl.BlockSpec((tm, d), lambda i: (i, 0))
    vec = pl.BlockSpec((1, d), lambda i: (0, 0))
    ins = list(dys) + [x, g] + ([res] if has_res else [])
    return _pcall(
        body, name=name, grid=(n // tm,),
        in_specs=[row] * nd + [row, vec] + ([row] if has_res else []),
        out_specs=(row, vec),
        out_shape=(_sds((n, d), F32), _sds((1, d), F32)),
        compiler_params=_cparams("arbitrary"))(*ins)


def _mm(name, grid, pairs, dn, out_shape, out_spec, acc_shape, res=None, scale=1.0):
    npairs = len(pairs)
    nred = grid[2]
    has_res = res is not None

    def body(*refs):
        ab = refs[:2 * npairs]
        res_ref = refs[2 * npairs] if has_res else None
        o_ref = refs[2 * npairs + int(has_res)]
        acc_ref = refs[-1] if nred > 1 else None
        tot = None
        for p in range(npairs):
            d = _dot(ab[2 * p][...].astype(BF16), ab[2 * p + 1][...].astype(BF16), dn)
            tot = d if tot is None else tot + d

        def finish(v):
            if scale != 1.0:
                v = v * scale
            if has_res:
                v = res_ref[...] + v
            o_ref[...] = v.astype(o_ref.dtype)

        if nred == 1:
            finish(tot)
        else:
            r = pl.program_id(2)

            @pl.when(r == 0)
            def _():
                acc_ref[...] = tot

            @pl.when(r > 0)
            def _():
                acc_ref[...] += tot

            @pl.when(r == nred - 1)
            def _():
                finish(acc_ref[...])

    ins, specs = [], []
    for a, a_spec, b, b_spec in pairs:
        ins += [a, b]
        specs += [a_spec, b_spec]
    if has_res:
        ins.append(res[0])
        specs.append(res[1])
    return _pcall(
        body, name=name, grid=grid, in_specs=specs, out_specs=out_spec, out_shape=out_shape,
        scratch_shapes=[pltpu.VMEM(acc_shape, F32)] if nred > 1 else [],
        compiler_params=_cparams("parallel", "parallel", "arbitrary"))(*ins)


def _ffn_up(h, wg, wu, name, tm):
    t, d = h.shape
    nc, _, fs = wg.shape
    tm = min(tm, t)

    def body(h_ref, wg_ref, wu_ref, g_ref, u_ref, a_ref):
        hh = h_ref[...]
        gate = _dot(hh, wg_ref[...], NN)
        up = _dot(hh, wu_ref[...], NN)
        sig = jax.nn.sigmoid(gate)
        silu = gate * sig
        g_ref[...] = (up * (sig + silu * (1.0 - sig))).astype(BF16)
        u_ref[...] = silu.astype(BF16)
        a_ref[...] = (silu * up).astype(BF16)

    wspec = pl.BlockSpec((None, d, fs), lambda c, i: (c, 0, 0))
    ospec = pl.BlockSpec((None, tm, fs), lambda c, i: (c, i, 0))
    osd = _sds((nc, t, fs), BF16)
    return _pcall(
        body, name=name, grid=(nc, t // tm),
        in_specs=[pl.BlockSpec((tm, d), lambda c, i: (i, 0)), wspec, wspec],
        out_specs=(ospec, ospec, ospec), out_shape=(osd, osd, osd),
        compiler_params=_cparams("parallel", "parallel"))(h, wg, wu)


def _ffn_hidden_bwd(dy, h, wd, dact_dgate, dact_dup, act, name, tm):
    t, d = dy.shape
    nc, fs, _ = wd.shape
    tm = min(tm, t)
    nt = t // tm

    def body(dy_ref, h_ref, wd_ref, g_ref, u_ref, a_ref, dg_ref, du_ref, dwg_hbm, dwu_hbm, dwd_hbm,
             wg_acc, wu_acc, wd_acc, sem):
        c, i = pl.program_id(0), pl.program_id(1)
        dyb = dy_ref[...].astype(BF16)
        da = _dot(dyb, wd_ref[...], NT) * FFN_RESID
        dgate = (da * g_ref[...].astype(F32)).astype(BF16)
        dup = (da * u_ref[...].astype(F32)).astype(BF16)
        dg_ref[...] = dgate
        du_ref[...] = dup
        hh = h_ref[...]
        parts = (_dot(hh, dgate, TN), _dot(hh, dup, TN), _dot(a_ref[...], dyb, TN) * FFN_RESID)
        accs = (wg_acc, wu_acc, wd_acc)

        @pl.when(i == 0)
        def _():
            for acc, part in zip(accs, parts):
                acc[...] = part

        @pl.when(i > 0)
        def _():
            for acc, part in zip(accs, parts):
                acc[...] += part

        @pl.when(i == nt - 1)
        def _():
            copies = [pltpu.make_async_copy(acc, out.at[c], sem.at[n])
                      for n, (acc, out) in enumerate(zip(accs, (dwg_hbm, dwu_hbm, dwd_hbm)))]
            for cp in copies:
                cp.start()
            for cp in copies:
                cp.wait()

    tok = pl.BlockSpec((tm, d), lambda c, i: (i, 0))
    cspec = pl.BlockSpec((None, tm, fs), lambda c, i: (c, i, 0))
    hbm = pl.BlockSpec(memory_space=pltpu.HBM)
    osd = _sds((nc, t, fs), BF16)
    return _pcall(
        body, name=name, grid=(nc, nt),
        in_specs=[tok, tok, pl.BlockSpec((None, fs, d), lambda c, i: (c, 0, 0)), cspec, cspec, cspec],
        out_specs=(cspec, cspec, hbm, hbm, hbm),
        out_shape=(osd, osd, _sds((nc, d, fs), F32), _sds((nc, d, fs), F32), _sds((nc, fs, d), F32)),
        scratch_shapes=[pltpu.VMEM((d, fs), F32), pltpu.VMEM((d, fs), F32), pltpu.VMEM((fs, d), F32),
                        pltpu.SemaphoreType.DMA((3,))],
        compiler_params=_cparams("arbitrary", "arbitrary"))(dy, h, wd, dact_dgate, dact_dup, act)


def _ffn_fwd(x, g, wg, wu, wd, tag):
    t = x.shape[0]
    nc, _, fs = wg.shape
    tm = min(512, t)
    h = _rms_fwd(x, g, BF16, f"{tag}_norm", 512)
    dact_dgate, dact_dup, act = _ffn_up(h, wg, wu, f"{tag}_up", 1024)
    pairs = [(act, pl.BlockSpec((None, tm, fs), lambda i, j, r, c=c: (c, i, 0)),
              wd, pl.BlockSpec((None, fs, D_MODEL), lambda i, j, r, c=c: (c, 0, 0))) for c in range(nc)]
    row = pl.BlockSpec((tm, D_MODEL), lambda i, j, r: (i, 0))
    y = _mm(f"{tag}_down", (t // tm, 1, 1), pairs, NN, _sds((t, D_MODEL), F32), row, (tm, D_MODEL),
            res=(x, row), scale=FFN_RESID)
    return y, (h, dact_dgate, dact_dup, act)


def _ffn_bwd(dy, x, g, wg, wu, wd, saved, tag):
    h, dact_dgate, dact_dup, act = saved
    t = x.shape[0]
    nc, _, fs = wg.shape
    tm = min(512, t)
    dgate, dup, dwg, dwu, dwd = _ffn_hidden_bwd(dy, h, wd, dact_dgate, dact_dup, act, f"{tag}_hidden_bwd", 1024)
    pairs = []
    for c in range(nc):
        a_spec = pl.BlockSpec((None, tm, fs), lambda i, j, r, c=c: (c, i, 0))
        w_spec = pl.BlockSpec((None, D_MODEL, fs), lambda i, j, r, c=c: (c, 0, 0))
        pairs += [(dgate, a_spec, wg, w_spec), (dup, a_spec, wu, w_spec)]
    dh = _mm(f"{tag}_dh", (t // tm, 1, 1), pairs, NT,
             _sds((t, D_MODEL), F32), pl.BlockSpec((tm, D_MODEL), lambda i, j, r: (i, 0)), (tm, D_MODEL))
    dx, dg = _rms_bwd([dh], x, g, dy, f"{tag}_dnorm", 512)
    return dx, dg, dwg, dwu, dwd


def _mm_simple(name, a, b, dn, out_dtype, tm=512, tk=512, res=None, scale=1.0):
    if dn == TN:
        k, m = a.shape
        n = b.shape[1]
        tk = min(tk, k)
        return _mm(name, (1, 1, k // tk),
                   [(a, pl.BlockSpec((tk, m), lambda i, j, r: (r, 0)), b, pl.BlockSpec((tk, n), lambda i, j, r: (r, 0)))],
                   TN, _sds((m, n), out_dtype), pl.BlockSpec((m, n), lambda i, j, r: (0, 0)), (m, n), scale=scale)
    m, k = a.shape
    n = b.shape[1] if dn == NN else b.shape[0]
    tm = min(tm, m)
    row = pl.BlockSpec((tm, n), lambda i, j, r: (i, 0))
    return _mm(name, (m // tm, 1, 1),
               [(a, pl.BlockSpec((tm, k), lambda i, j, r: (i, 0)), b, pl.BlockSpec(b.shape, lambda i, j, r: (0, 0)))],
               dn, _sds((m, n), out_dtype), row, (tm, n), res=None if res is None else (res, row), scale=scale)


def _t5_bucket(dist):
    max_exact = REL_BUCKETS // 2
    d = np.maximum(dist, 1).astype(np.float32)
    large = max_exact + (np.log(d / max_exact) / np.log(REL_MAX_DIST / max_exact)
                         * (REL_BUCKETS - max_exact)).astype(np.int32)
    large = np.minimum(large, REL_BUCKETS - 1)
    return np.where(dist < max_exact, dist, large).astype(np.int32)


def _bucket_tiles():
    i = np.arange(QB)[:, None]
    j = np.arange(QB + DIL_W)[None, :]
    delta = np.clip(i + DIL_W - j, 0, None)
    return np.stack([_t5_bucket(delta * dil) for dil in DIL_DILATIONS]).astype(np.int32)


def _bias_tiles(rel_bias):
    buckets = jnp.asarray(_bucket_tiles())

    def body(rb_ref, bk_ref, o_ref):
        bk = bk_ref[...]
        for h in range(DIL_HEADS):
            def pick(b, tile):
                return jnp.where(bk == b, rb_ref[h, b], tile)

            o_ref[h] = lax.fori_loop(0, REL_BUCKETS, pick, jnp.zeros((QB, QB + DIL_W), F32))

    return _pcall(
        body, name="dil_bias_tiles", grid=(3,),
        in_specs=[pl.BlockSpec(memory_space=pltpu.SMEM),
                  pl.BlockSpec((None, QB, QB + DIL_W), lambda b: (b, 0, 0))],
        out_specs=pl.BlockSpec((None, DIL_HEADS, QB, QB + DIL_W), lambda b: (b, 0, 0, 0)),
        out_shape=_sds((3, DIL_HEADS, QB, QB + DIL_W), F32),
        compiler_params=_cparams("parallel"))(rel_bias, buckets)


def _bias_grad(dtiles):
    buckets = jnp.asarray(_bucket_tiles())

    def body(dt_ref, bk_ref, o_ref):
        def one(b, carry):
            hit = [bk_ref[br] == b for br in range(3)]
            for h in range(DIL_HEADS):
                tot = jnp.zeros((), F32)
                for br in range(3):
                    tot = tot + jnp.sum(jnp.where(hit[br], dt_ref[br, h], 0.0))
                o_ref[h, b] = tot
            return carry

        lax.fori_loop(0, REL_BUCKETS, one, 0)

    return _pcall(
        body, name="dil_bias_grad",
        in_specs=[pl.BlockSpec(memory_space=pltpu.VMEM), pl.BlockSpec(memory_space=pltpu.VMEM)],
        out_specs=pl.BlockSpec(memory_space=pltpu.SMEM),
        out_shape=_sds((DIL_HEADS, REL_BUCKETS), F32))(dtiles, buckets)


def _split_heads(a, lo):
    zero = jnp.zeros_like(a)
    return jnp.concatenate([jnp.where(lo, a, zero), jnp.where(lo, zero, a)], axis=0)


def _side_by_side(a):
    n = a.shape[0] // 2
    return jnp.concatenate([a[:n], a[n:]], axis=1)


def _band_masks(prev_ok):
    ii = lax.broadcasted_iota(jnp.int32, (2 * QB, QB), 0) & (QB - 1)
    jj = lax.broadcasted_iota(jnp.int32, (2 * QB, QB), 1)
    return jj <= ii, jj >= ii + jnp.where(prev_ok, 0, QB)


def _dil_view(a, dil):
    t, w = a.shape
    return a.reshape(t // dil, dil * w)


def _dil_fwd(q, k, v, bias, dil, name):
    t, w = q.shape
    npair = w // LANES
    nl = t // dil // QB
    scale = DIL_HD ** -0.5

    def body(q_ref, kc_ref, kp_ref, vc_ref, vp_ref, b_ref, o_ref, lse_ref):
        nn = pl.program_id(1)
        lo = lax.broadcasted_iota(jnp.int32, (QB, LANES), 1) < DIL_HD
        lo2 = lax.broadcasted_iota(jnp.int32, (2 * QB, LANES), 1) < DIL_HD
        ii = lax.broadcasted_iota(jnp.int32, (2 * QB, 2 * QB), 0) & (QB - 1)
        jj = lax.broadcasted_iota(jnp.int32, (2 * QB, 2 * QB), 1)
        first_key = jnp.maximum(ii, jnp.where(nn != 0, 0, QB))
        valid = (jj >= first_key) & (jj <= ii + QB)
        for p in range(npair):
            cols = slice(p * LANES, (p + 1) * LANES)
            qq = _split_heads(q_ref[:, cols], lo)
            kk = jnp.concatenate([kp_ref[:, cols], kc_ref[:, cols]], axis=0)
            vv = jnp.concatenate([vp_ref[:, cols], vc_ref[:, cols]], axis=0)
            s = jnp.where(valid, _dot(qq, kk, NT) * scale + b_ref[p], NEG)
            m = jnp.max(s, axis=-1, keepdims=True)
            e = jnp.exp(s - m)
            den = jnp.sum(e, axis=-1, keepdims=True)
            pn = (e * (1.0 / den)).astype(BF16)
            o_ref[:, cols] = _dot(_side_by_side(pn), _split_heads(vv, lo2), NN)
            lse = m + jnp.log(den)
            lse_ref[:, cols] = jnp.where(lo, lse[:QB], lse[QB:])

    cur = pl.BlockSpec((QB, w), lambda r, n: (n, r))
    prev = pl.BlockSpec((QB, w), lambda r, n: (jnp.maximum(n - 1, 0), r))
    sd = _sds((t // dil, dil * w), F32)
    o, lse = _pcall(
        body, name=name, grid=(dil, nl),
        in_specs=[cur, cur, prev, cur, prev, pl.BlockSpec((npair, 2 * QB, 2 * QB), lambda r, n: (0, 0, 0))],
        out_specs=(cur, cur), out_shape=(sd, sd),
        compiler_params=_cparams("parallel", "parallel"))(*[_dil_view(a, dil) for a in (q, k, k, v, v)], bias)
    return o.reshape(t, w), lse.reshape(t, w)


def _dil_bwd(q, k, v, do, stats, bias, dil, name):
    t, w = q.shape
    npair = w // LANES
    nl = t // dil // QB
    scale = DIL_HD ** -0.5

    def body(qc_ref, qn_ref, doc_ref, don_ref, sc_ref, sn_ref, k_ref, v_ref, b_ref,
             dq_ref, dk_ref, dv_ref, db_ref, carry):
        r, nn = pl.program_id(0), pl.program_id(1)
        lo = lax.broadcasted_iota(jnp.int32, (QB, LANES), 1) < DIL_HD
        cur_ok, prev_ok = _band_masks(nn + 1 < nl)

        @pl.when((r == 0) & (nn == 0))
        def _():
            db_ref[...] = jnp.zeros_like(db_ref)
            carry[...] = jnp.zeros_like(carry)

        for p in range(npair):
            cols = slice(p * LANES, (p + 1) * LANES)
            kp, vp = k_ref[:, cols], v_ref[:, cols]
            k2 = _split_heads(kp, lo)

            def column(ref, lane):
                first = p * LANES + lane
                return jnp.concatenate([ref[:, first:first + 1], ref[:, first + DIL_HD:first + DIL_HD + 1]], axis=0)

            def side(q_ref, do_ref, s_ref, bias, ok):
                qq = _split_heads(q_ref[:, cols], lo)
                dd = _split_heads(do_ref[:, cols], lo)
                s = jnp.where(ok, _dot(qq, kp, NT) * scale + bias, NEG)
                prob = jnp.exp(s - column(s_ref, 0))
                ds = prob * (_dot(dd, vp, NT) - column(s_ref, DIL_HD // 2))
                return qq, dd, prob.astype(BF16), ds

            q1, d1, p1, ds1 = side(qc_ref, doc_ref, sc_ref, b_ref[p, :, QB:], cur_ok)
            q2, d2, p2, ds2 = side(qn_ref, don_ref, sn_ref, b_ref[p, :, :QB], prev_ok)
            ds1b, ds2b = ds1.astype(BF16), ds2.astype(BF16)
            dq_ref[:, cols] = carry[:, cols] + _dot(_side_by_side(ds1b), k2, NN) * scale
            carry[:, cols] = _dot(_side_by_side(ds2b), k2, NN) * scale
            dk_ref[:, cols] = _dot(jnp.concatenate([ds1b, ds2b], axis=0), jnp.concatenate([q1, q2], axis=0), TN) * scale
            dv_ref[:, cols] = _dot(jnp.concatenate([p1, p2], axis=0), jnp.concatenate([d1, d2], axis=0), TN)
            db_ref[p, :, QB:] += ds1
            db_ref[p, :, :QB] += ds2

    cur = pl.BlockSpec((QB, w), lambda r, n: (n, r))
    nxt = pl.BlockSpec((QB, w), lambda r, n: (jnp.minimum(n + 1, nl - 1), r))
    tile = pl.BlockSpec((npair, 2 * QB, 2 * QB), lambda r, n: (0, 0, 0))
    sd = _sds((t // dil, dil * w), F32)
    views = [_dil_view(a, dil) for a in (q, q, do, do, stats, stats, k, v)]
    dq, dk, dv, db = _pcall(
        body, name=name, grid=(dil, nl),
        in_specs=[cur, nxt, cur, nxt, cur, nxt, cur, cur, tile],
        out_specs=(cur, cur, cur, tile),
        out_shape=(sd, sd, sd, _sds((npair, 2 * QB, 2 * QB), F32)),
        scratch_shapes=[pltpu.VMEM((QB, w), F32)],
        compiler_params=_cparams("arbitrary", "arbitrary"))(*views, bias)
    return dq.reshape(t, w), dk.reshape(t, w), dv.reshape(t, w), db


def _head_sum_matrix(scale):
    idx = np.arange(DIL_WIDTH) // DIL_HD
    return jnp.asarray((idx[:, None] == idx[None, :]).astype(np.float32) * scale, BF16)


def _head_sum(x, mat):
    hi = x.astype(BF16)
    lo = (x - hi.astype(F32)).astype(BF16)
    return _dot(hi, mat, NN) + _dot(lo, mat, NN)


def _dil_merge(outs, lses, g, tm):
    t, w = outs[0].shape
    tm = min(tm, t)

    def body(o0, o1, o2, l0, l1, l2, g_ref, o_ref, l_ref, n_ref):
        a0, a1, a2 = l0[...], l1[...], l2[...]
        m = jnp.maximum(jnp.maximum(a0, a1), a2)
        e0, e1, e2 = jnp.exp(a0 - m), jnp.exp(a1 - m), jnp.exp(a2 - m)
        den = e0 + e1 + e2
        o = (e0 * o0[...] + e1 * o1[...] + e2 * o2[...]) / den
        o_ref[...] = o
        l_ref[...] = m + jnp.log(den)
        r = lax.rsqrt(jnp.mean(o * o, axis=-1, keepdims=True) + EPS)
        n_ref[...] = (o * r * g_ref[...]).astype(n_ref.dtype)

    spec = pl.BlockSpec((tm, w), lambda i: (i, 0))
    return _pcall(
        body, name="dil_merge", grid=(t // tm,),
        in_specs=[spec] * 6 + [pl.BlockSpec((1, w), lambda i: (0, 0))], out_specs=(spec, spec, spec),
        out_shape=(_sds((t, w), F32), _sds((t, w), F32), _sds((t, w), BF16)),
        compiler_params=_cparams("parallel"))(*outs, *lses, g)


def _dil_stats(do, o, lse, tm):
    t, w = do.shape
    tm = min(tm, t)

    def body(a_ref, b_ref, l_ref, m_ref, o_ref):
        first = (lax.broadcasted_iota(jnp.int32, (tm, w), 1) & (DIL_HD - 1)) < DIL_HD // 2
        o_ref[...] = jnp.where(first, l_ref[...], _head_sum(a_ref[...] * b_ref[...], m_ref[...]))

    spec = pl.BlockSpec((tm, w), lambda i: (i, 0))
    return _pcall(body, name="dil_stats", grid=(t // tm,),
                  in_specs=[spec, spec, spec, pl.BlockSpec((w, w), lambda i: (0, 0))], out_specs=spec,
                  out_shape=_sds((t, w), F32), compiler_params=_cparams("parallel"))(do, o, lse, _head_sum_matrix(1.0))


def _head_norm_fwd(x, col, g, name, tm):
    t = x.shape[0]
    w = DIL_WIDTH
    tm = min(tm, t)

    def body(x_ref, g_ref, m_ref, o_ref):
        xf = x_ref[...]
        r = lax.rsqrt(_head_sum(xf * xf, m_ref[...]) + EPS)
        o_ref[...] = (xf * r * g_ref[...]).astype(o_ref.dtype)

    return _pcall(
        body, name=name, grid=(t // tm,),
        in_specs=[pl.BlockSpec((tm, w), lambda i: (i, col)), pl.BlockSpec((1, w), lambda i: (0, 0)),
                  pl.BlockSpec((w, w), lambda i: (0, 0))],
        out_specs=pl.BlockSpec((tm, w), lambda i: (i, 0)), out_shape=_sds((t, w), BF16),
        compiler_params=_cparams("parallel"))(x, g, _head_sum_matrix(1.0 / DIL_HD))


def _head_norm_bwd(dys, x, col, g, name, tm):
    t = x.shape[0]
    w = DIL_WIDTH
    tm = min(tm, t)
    nd = len(dys)
    nt = t // tm
    lane = np.arange(w) % DIL_HD
    fold = jnp.asarray((lane[:, None] == lane[None, :]).astype(np.float32))

    def body(*refs):
        x_ref, g_ref, m_ref, f_ref = refs[nd:nd + 4]
        dx_ref, dg_ref = refs[-2], refs[-1]
        dy = refs[0][...]
        for r_ in refs[1:nd]:
            dy = dy + r_[...]
        xf = x_ref[...]
        mat = m_ref[...]
        r = lax.rsqrt(_head_sum(xf * xf, mat) + EPS)
        xh = xf * r
        dxh = dy * g_ref[...]
        dx_ref[...] = r * (dxh - xh * _head_sum(dxh * xh, mat))

        @pl.when(pl.program_id(0) == 0)
        def _():
            dg_ref[...] = jnp.zeros_like(dg_ref)

        dg_ref[...] += jnp.sum(dy * xh, axis=0, keepdims=True)

        @pl.when(pl.program_id(0) == nt - 1)
        def _():
            per_lane = jnp.broadcast_to(dg_ref[...], (8, w))
            dg_ref[...] = lax.dot_general(per_lane, f_ref[...], NN, precision=lax.Precision.HIGHEST,
                                          preferred_element_type=F32)[0:1]

    row = pl.BlockSpec((tm, w), lambda i: (i, 0))
    vec = pl.BlockSpec((1, w), lambda i: (0, 0))
    sq = pl.BlockSpec((w, w), lambda i: (0, 0))
    return _pcall(
        body, name=name, grid=(nt,),
        in_specs=[row] * nd + [pl.BlockSpec((tm, w), lambda i: (i, col)), vec, sq, sq],
        out_specs=(row, vec), out_shape=(_sds((t, w), F32), _sds((1, w), F32)),
        compiler_params=_cparams("arbitrary"))(*dys, x, g, _head_sum_matrix(1.0 / DIL_HD), fold)


def _rowdot(a, b, name, tm):
    n, d = a.shape
    tm = min(tm, n)

    def body(a_ref, b_ref, o_ref):
        o_ref[...] = jnp.sum(a_ref[...].astype(F32) * b_ref[...].astype(F32), axis=-1, keepdims=True)

    spec = pl.BlockSpec((tm, d), lambda i: (i, 0))
    return _pcall(body, name=name, grid=(n // tm,), in_specs=[spec, spec],
                  out_specs=pl.BlockSpec((tm, 1), lambda i: (i, 0)), out_shape=_sds((n, 1), F32),
                  compiler_params=_cparams("parallel"))(a, b)


def _add3(a, b, c, name, tm):
    n, d = a.shape
    tm = min(tm, n)

    def body(a_ref, b_ref, c_ref, o_ref):
        o_ref[...] = a_ref[...] + b_ref[...] + c_ref[...]

    spec = pl.BlockSpec((tm, d), lambda i: (i, 0))
    return _pcall(body, name=name, grid=(n // tm,), in_specs=[spec] * 3, out_specs=spec,
                  out_shape=_sds((n, d), F32), compiler_params=_cparams("parallel"))(a, b, c)


def _rope_tables(t):
    inv = ROPE_BASE ** (-np.arange(0, MLA_ROPE, 2, dtype=np.float64) / MLA_ROPE)
    ang = np.arange(t, dtype=np.float64)[:, None] * inv[None, :]
    cos, sin = np.cos(ang), np.sin(ang)
    return (jnp.asarray(np.concatenate([cos, cos], 1), F32), jnp.asarray(np.concatenate([-sin, sin], 1), F32))


def _half_swap():
    p = np.zeros((MLA_ROPE, MLA_ROPE), np.float32)
    half = MLA_ROPE // 2
    for i in range(MLA_ROPE):
        p[(i + half) % MLA_ROPE, i] = 1.0
    return jnp.asarray(p)


def _mla_qk_fwd(x, g, cos_t, sin_t, scale, name, tm):
    n, d = x.shape
    t = cos_t.shape[0]
    tm = min(tm, t)
    nt = t // tm
    swap = _half_swap()

    def body(x_ref, g_ref, c_ref, s_ref, p_ref, o_ref):
        xf = x_ref[...]
        r = lax.rsqrt(jnp.mean(xf * xf, axis=-1, keepdims=True) + EPS)
        y = xf * r * g_ref[...]
        yr = y[:, MLA_NOPE:]
        sw = lax.dot_general(yr, p_ref[...], NN, precision=lax.Precision.HIGHEST, preferred_element_type=F32)
        o_ref[:, :MLA_NOPE] = (y[:, :MLA_NOPE] * scale).astype(o_ref.dtype)
        o_ref[:, MLA_NOPE:] = ((yr * c_ref[...] + sw * s_ref[...]) * scale).astype(o_ref.dtype)

    row = pl.BlockSpec((tm, d), lambda i: (i, 0))
    tab = pl.BlockSpec((tm, MLA_ROPE), lambda i: (i % nt, 0))
    return _pcall(
        body, name=name, grid=(n // tm,),
        in_specs=[row, pl.BlockSpec((1, d), lambda i: (0, 0)), tab, tab,
                  pl.BlockSpec((MLA_ROPE, MLA_ROPE), lambda i: (0, 0))],
        out_specs=row, out_shape=_sds((n, d), BF16),
        compiler_params=_cparams("parallel"))(x, g, cos_t, sin_t, swap)


def _mla_qk_bwd(dy, x, g, cos_t, sin_t, scale, name, tm):
    n, d = x.shape
    t = cos_t.shape[0]
    tm = min(tm, t)
    nt = t // tm
    swap_t = _half_swap().T

    def body(dy_ref, x_ref, g_ref, c_ref, s_ref, p_ref, dx_ref, dg_ref):
        xf = x_ref[...]
        gg = g_ref[...]
        r = lax.rsqrt(jnp.mean(xf * xf, axis=-1, keepdims=True) + EPS)
        xh = xf * r
        dyf = dy_ref[...] * scale
        dyr = dyf[:, MLA_NOPE:]
        back = lax.dot_general(dyr * s_ref[...], p_ref[...], NN, precision=lax.Precision.HIGHEST,
                               preferred_element_type=F32)
        dn_n = dyf[:, :MLA_NOPE]
        dn_r = dyr * c_ref[...] + back
        xh_n, xh_r = xh[:, :MLA_NOPE], xh[:, MLA_NOPE:]
        dxh_n = dn_n * gg[:, :MLA_NOPE]
        dxh_r = dn_r * gg[:, MLA_NOPE:]
        mean = (jnp.sum(dxh_n * xh_n, axis=-1, keepdims=True)
                + jnp.sum(dxh_r * xh_r, axis=-1, keepdims=True)) * (1.0 / d)
        dx_ref[:, :MLA_NOPE] = r * (dxh_n - xh_n * mean)
        dx_ref[:, MLA_NOPE:] = r * (dxh_r - xh_r * mean)

        @pl.when(pl.program_id(0) == 0)
        def _():
            dg_ref[...] = jnp.zeros_like(dg_ref)

        dg_ref[:, :MLA_NOPE] += jnp.sum(dn_n * xh_n, axis=0, keepdims=True)
        dg_ref[:, MLA_NOPE:] += jnp.sum(dn_r * xh_r, axis=0, keepdims=True)

    row = pl.BlockSpec((tm, d), lambda i: (i, 0))
    vec = pl.BlockSpec((1, d), lambda i: (0, 0))
    tab = pl.BlockSpec((tm, MLA_ROPE), lambda i: (i % nt, 0))
    return _pcall(
        body, name=name, grid=(n // tm,),
        in_specs=[row, row, vec, tab, tab, pl.BlockSpec((MLA_ROPE, MLA_ROPE), lambda i: (0, 0))],
        out_specs=(row, vec), out_shape=(_sds((n, d), F32), _sds((1, d), F32)),
        compiler_params=_cparams("arbitrary"))(dy, x, g, cos_t, sin_t, swap_t)


def _causal_mask(i, j, tq, tk, width):
    row = i * tq + lax.broadcasted_iota(jnp.int32, (tq, width), 0)
    col = j * tk + lax.broadcasted_iota(jnp.int32, (tq, width), 1)
    return col <= row


def _causal_steps(nq, nk, tq, tk, q_major):
    if q_major:
        groups = [[(i, j) for j in range((i * tq + tq - 1) // tk + 1)] for i in range(nq)]
        nunit = tk // tq if tk % tq == 0 else 1
    else:
        groups = [[(i, j) for i in range((j * tk) // tq, nq)] for j in range(nk)]
        nunit = tq // tk if tq % tk == 0 else 1
    it, jt, fl = [], [], []
    for g in groups:
        for n, (i, j) in enumerate(g):
            crossing = j * tk + tk - 1 > i * tq
            if q_major:
                unit = tk // nunit
                u = min(nunit, -(-(i * tq + tq - j * tk) // unit)) - 1
            else:
                unit = tq // nunit
                u = max(0, j * tk - i * tq) // unit
            it.append(i)
            jt.append(j)
            fl.append((n == 0) + 2 * (n == len(g) - 1) + 4 * crossing + 8 * (u if crossing else 0))
    return tuple(jnp.asarray(np.array(a, np.int32)) for a in (it, jt, fl)), nunit


def _by_crossing(flags, nunit, update):
    pl.when((flags & 4) == 0)(functools.partial(update, None))
    for u in range(nunit):
        pl.when(((flags & 4) != 0) & ((flags >> 3) == u))(functools.partial(update, u))


def _causal_specs(tq, tk):
    def qs(w):
        return pl.BlockSpec((None, tq, w), lambda h, s, it, jt, fl: (h, it[s], 0))

    def kv(w):
        return pl.BlockSpec((None, tk, w), lambda h, s, it, jt, fl: (h, jt[s], 0))

    return qs, kv


def _mla_fwd(q, k, v, tq, tk):
    nh, t, dq = q.shape
    dv = v.shape[2]
    tq, tk = min(tq, t), min(tk, t)
    tables, nunit = _causal_steps(t // tq, t // tk, tq, tk, True)

    def body(it, jt, fl, q_ref, k_ref, v_ref, o_ref, lse_ref, m_sc, l_sc, acc_sc):
        step = pl.program_id(1)
        i, j, flags = it[step], jt[step], fl[step]

        @pl.when((flags & 1) != 0)
        def _():
            m_sc[...] = jnp.full_like(m_sc, NEG)
            l_sc[...] = jnp.zeros_like(l_sc)
            acc_sc[...] = jnp.zeros_like(acc_sc)

        def update(units):
            wk = tk if units is None else (units + 1) * (tk // nunit)
            s = _dot(q_ref[...], k_ref[:wk, :], NT)
            if units is not None:
                s = jnp.where(_causal_mask(i, j, tq, tk, wk), s, NEG)
            m_prev = m_sc[...]
            m_new = jnp.maximum(m_prev, jnp.max(s, axis=-1, keepdims=True))
            alpha = jnp.exp(m_prev - m_new)
            p = jnp.exp(s - m_new)
            l_sc[...] = alpha * l_sc[...] + jnp.sum(p, axis=-1, keepdims=True)
            acc_sc[...] = alpha * acc_sc[...] + _dot(p.astype(BF16), v_ref[:wk, :], NN)
            m_sc[...] = m_new

        _by_crossing(flags, nunit, update)

        @pl.when((flags & 2) != 0)
        def _():
            o_ref[...] = acc_sc[...] / l_sc[...]
            lse_ref[...] = m_sc[...] + jnp.log(l_sc[...])

    qs, kv = _causal_specs(tq, tk)
    return _pcall(
        body, name="mla_attn_fwd",
        grid_spec=pltpu.PrefetchScalarGridSpec(
            num_scalar_prefetch=3, grid=(nh, tables[0].shape[0]),
            in_specs=[qs(dq), kv(dq), kv(dv)], out_specs=(qs(dv), qs(1)),
            scratch_shapes=[pltpu.VMEM((tq, 1), F32), pltpu.VMEM((tq, 1), F32), pltpu.VMEM((tq, dv), F32)]),
        out_shape=(_sds((nh, t, dv), F32), _sds((nh, t, 1), F32)),
        compiler_params=_cparams("parallel", "arbitrary"))(*tables, q, k, v)


def _mla_bwd(q, k, k_t, v, do, lse_row, dl_row, tq, tk):
    nh, t, dq = q.shape
    dv = v.shape[2]
    tq, tk = min(tq, t), min(tk, t)
    nq = t // tq
    tables, nunit = _causal_steps(nq, t // tk, tq, tk, False)

    def body(it, jt, fl, q_ref, k_ref, kt_ref, v_ref, do_ref, lse_ref, dl_ref, dk_ref, dv_ref, dq_ref, dk_sc, dv_sc):
        step = pl.program_id(1)
        i, j, flags = it[step], jt[step], fl[step]

        def update(units):
            off = 0 if units is None else units * (tq // nunit)
            qq = q_ref[off:, :]
            st = _dot(k_ref[...], qq, NT)
            if units is not None:
                key = j * tk + lax.broadcasted_iota(jnp.int32, (tk, tq - off), 0)
                qry = i * tq + off + lax.broadcasted_iota(jnp.int32, (tk, tq - off), 1)
                st = jnp.where(key <= qry, st, NEG)
            pt = jnp.exp(st - lse_ref[:, off:])
            dob = do_ref[off:, :].astype(BF16)
            dpt = _dot(v_ref[...], dob, NT)
            dst = pt * (dpt - dl_ref[:, off:])
            dsb = dst.astype(BF16)
            dv_part = _dot(pt.astype(BF16), dob, NN)
            dk_part = _dot(dsb, qq, NN)
            dq_part = _dot(kt_ref[...], dsb, NN)

            @pl.when((flags & 1) != 0)
            def _():
                dv_sc[...] = dv_part
                dk_sc[...] = dk_part

            @pl.when((flags & 1) == 0)
            def _():
                dv_sc[...] += dv_part
                dk_sc[...] += dk_part

            if off == 0:
                @pl.when(j == 0)
                def _():
                    dq_ref[i] = dq_part

                @pl.when(j != 0)
                def _():
                    dq_ref[i] += dq_part
            else:
                dq_ref[i, :, off:] += dq_part

        _by_crossing(flags, nunit, update)

        @pl.when((flags & 2) != 0)
        def _():
            dk_ref[...] = dk_sc[...]
            dv_ref[...] = dv_sc[...]

    qs, kv = _causal_specs(tq, tk)
    rowv = pl.BlockSpec((None, 1, tq), lambda h, s, it, jt, fl: (h, 0, it[s]))
    ktv = pl.BlockSpec((None, dq, tk), lambda h, s, it, jt, fl: (h, 0, jt[s]))
    whole = pl.BlockSpec((None, nq, dq, tq), lambda h, s, it, jt, fl: (h, 0, 0, 0))
    return _pcall(
        body, name="mla_attn_bwd",
        grid_spec=pltpu.PrefetchScalarGridSpec(
            num_scalar_prefetch=3, grid=(nh, tables[0].shape[0]),
            in_specs=[qs(dq), kv(dq), ktv, kv(dv), qs(dv), rowv, rowv], out_specs=(kv(dq), kv(dv), whole),
            scratch_shapes=[pltpu.VMEM((tk, dq), F32), pltpu.VMEM((tk, dv), F32)]),
        out_shape=(_sds((nh, t, dq), F32), _sds((nh, t, dv), F32), _sds((nh, nq, dq, tq), F32)),
        compiler_params=_cparams("parallel", "arbitrary"))(*tables, q, k, k_t, v, do, lse_row, dl_row)


def _loss_head(y, target, tm):
    t, d = y.shape
    tm = min(tm, t)
    nt = t // tm

    def body(y_ref, t_ref, dy_ref, loss_ref, acc):
        i = pl.program_id(0)
        err = y_ref[...] - t_ref[...]
        dy_ref[...] = err * (1.0 / d)

        @pl.when(i == 0)
        def _():
            acc[...] = jnp.zeros_like(acc)

        acc[...] += jnp.sum(err * err, axis=0, keepdims=True)

        @pl.when(i == nt - 1)
        def _():
            loss_ref[0, 0] = jnp.sum(acc[...]) * (0.5 / d)

    spec = pl.BlockSpec((tm, d), lambda i: (i, 0))
    return _pcall(
        body, name="loss_head", grid=(nt,), in_specs=[spec, spec],
        out_specs=(spec, pl.BlockSpec(memory_space=pltpu.SMEM)),
        out_shape=(_sds((t, d), F32), _sds((1, 1), F32)),
        scratch_shapes=[pltpu.VMEM((1, d), F32)],
        compiler_params=_cparams("arbitrary"))(y, target)


def _adamw(w, g, m, v, name):
    r, c = w.shape
    tr = r
    for cand in (256, 128, 64, 32, 16, 8):
        if r % cand == 0:
            tr = cand
            break

    def body(w_ref, g_ref, m_ref, v_ref, d_ref, nm_ref, nv_ref):
        gg = g_ref[...]
        nm = ADAM_B1 * m_ref[...] + (1.0 - ADAM_B1) * gg
        nv = ADAM_B2 * v_ref[...] + (1.0 - ADAM_B2) * (gg * gg)
        m_hat = nm / (1.0 - ADAM_B1 ** ADAM_STEP)
        v_hat = nv / (1.0 - ADAM_B2 ** ADAM_STEP)
        d_ref[...] = -ADAM_LR * (m_hat / (jnp.sqrt(v_hat) + ADAM_EPS) + ADAM_WD * w_ref[...])
        nm_ref[...] = nm
        nv_ref[...] = nv

    spec = pl.BlockSpec((tr, c), lambda i: (i, 0))
    sd = _sds((r, c), F32)
    return _pcall(body, name=name, grid=(r // tr,), in_specs=[spec] * 4, out_specs=(spec,) * 3,
                  out_shape=(sd, sd, sd), compiler_params=_cparams("parallel"))(w, g, m, v)


MESH_ID = pl.DeviceIdType.MESH
HBM_SPEC = pl.BlockSpec(memory_space=pltpu.HBM)


def _place():
    return lax.axis_index("x"), lax.axis_index("y"), lax.axis_index("c")


def _other_chips(x, y):
    return [(1 - x, y), (x, 1 - y), (1 - x, 1 - y)]


def _remote(src, dst, send_sems, recv_sems, k, to):
    return pltpu.make_async_remote_copy(src_ref=src, dst_ref=dst, send_sem=send_sems.at[k], recv_sem=recv_sems.at[k],
                                        device_id=to, device_id_type=MESH_ID)


def _halves(arrays):
    for a in arrays:
        assert a.shape[-2] % 32 == 0
    return [a.shape[-2] // 2 for a in arrays]


def _gather_weights(blocks):
    n = len(blocks)
    halves = _halves(blocks)

    def body(*refs):
        srcs, outs, send_sems, recv_sems = refs[:n], refs[n:2 * n], refs[2 * n], refs[2 * n + 1]
        x, y, c = _place()
        me = 2 * x + y
        sibling = (x, y, 1 - c)
        chips = _other_chips(x, y)

        def part(a, chip, core):
            return outs[a].at[chip, pl.ds(core * halves[a], halves[a]), :]

        for a in range(n):
            mine = srcs[a].at[pl.ds(c * halves[a], halves[a]), :]
            for k, (cx, cy) in enumerate(chips):
                _remote(mine, part(a, me, c), send_sems, recv_sems, 6 * a + k, (cx, cy, c)).start()
        for k, (cx, cy) in enumerate(chips):
            for a in range(n):
                got = part(a, 2 * cx + cy, c)
                _remote(got, got, send_sems, recv_sems, 6 * a + k, (x, y, c)).wait_recv()
                _remote(got, got, send_sems, recv_sems, 6 * a + 3 + k, sibling).start()
        for k, (cx, cy) in enumerate(chips):
            for a in range(n):
                got = part(a, 2 * cx + cy, 1 - c)
                _remote(got, got, send_sems, recv_sems, 6 * a + 3 + k, (x, y, c)).wait_recv()
        for a in range(n):
            sent = part(a, me, c)
            for k in range(6):
                _remote(sent, sent, send_sems, recv_sems, 6 * a + k, (x, y, c)).wait_send()

    return _pcall(
        body, name="gather_weights", in_specs=[HBM_SPEC] * n, out_specs=tuple([HBM_SPEC] * n),
        out_shape=tuple(_sds((N_CHIPS,) + b.shape, b.dtype) for b in blocks),
        scratch_shapes=[pltpu.SemaphoreType.DMA((6 * n,)), pltpu.SemaphoreType.DMA((6 * n,))],
    )(*blocks)


def _reduce_cores(grads):
    n = len(grads)
    halves = _halves(grads)

    def body(*refs):
        gs, outs, send_sems, recv_sems = refs[:n], refs[n:2 * n], refs[2 * n], refs[2 * n + 1]
        x, y, c = _place()
        for a in range(n):
            for j in range(N_CHIPS):
                _remote(gs[a].at[j, pl.ds((1 - c) * halves[a], halves[a]), :], outs[a].at[j],
                        send_sems, recv_sems, a, (x, y, 1 - c)).start()
        for a in range(n):
            _remote(gs[a].at[:, pl.ds((1 - c) * halves[a], halves[a]), :], outs[a],
                    send_sems, recv_sems, a, (x, y, c)).wait()

    return _pcall(
        body, name="reduce_cores", in_specs=[HBM_SPEC] * n, out_specs=tuple([HBM_SPEC] * n),
        out_shape=tuple(_sds((N_CHIPS, h, g.shape[2]), g.dtype) for g, h in zip(grads, halves)),
        scratch_shapes=[pltpu.SemaphoreType.DMA((n,)), pltpu.SemaphoreType.DMA((n,))],
    )(*grads)


def _scatter_chips(parts):
    n = len(parts)

    def body(*refs):
        ps, outs, send_sems, recv_sems = refs[:n], refs[n:2 * n], refs[2 * n], refs[2 * n + 1]
        x, y, c = _place()
        for a in range(n):
            for k, (cx, cy) in enumerate(_other_chips(x, y)):
                _remote(ps[a].at[2 * cx + cy], outs[a].at[k], send_sems, recv_sems, 3 * a + k, (cx, cy, c)).start()
        for a in range(n):
            for k in range(3):
                _remote(ps[a].at[k], outs[a].at[k], send_sems, recv_sems, 3 * a + k, (x, y, c)).wait()

    return _pcall(
        body, name="scatter_chips", in_specs=[HBM_SPEC] * n, out_specs=tuple([HBM_SPEC] * n),
        out_shape=tuple(_sds((3,) + p.shape[1:], p.dtype) for p in parts),
        scratch_shapes=[pltpu.SemaphoreType.DMA((3 * n,)), pltpu.SemaphoreType.DMA((3 * n,))],
    )(*parts)


def _sum_partials(received, parts, place):
    n = len(parts)
    steps = 2
    tiles = [p.shape[1] // steps for p in parts]

    def body(place_ref, *refs):
        rs, ps, outs = refs[:n], refs[n:2 * n], refs[2 * n:]
        for a in range(n):
            tot = ps[a][...].astype(F32)
            for k in range(3):
                tot = tot + rs[a][k].astype(F32)
            outs[a][...] = tot

    cols = [p.shape[2] for p in parts]
    return _pcall(
        body, name="sum_chip_partials",
        grid_spec=pltpu.PrefetchScalarGridSpec(
            num_scalar_prefetch=1, grid=(steps,),
            in_specs=[pl.BlockSpec((3, tm, w), lambda i, pc: (0, i, 0)) for tm, w in zip(tiles, cols)]
            + [pl.BlockSpec((None, tm, w), lambda i, pc: (pc[0], i, 0)) for tm, w in zip(tiles, cols)],
            out_specs=tuple(pl.BlockSpec((tm, w), lambda i, pc: (pc[1] * steps + i, 0)) for tm, w in zip(tiles, cols))),
        out_shape=tuple(_sds((2 * p.shape[1], p.shape[2]), F32) for p in parts),
        compiler_params=_cparams("parallel"))(place, *received, *parts)


def _share_cores(blocks):
    n = len(blocks)
    halves = _halves(blocks)

    def body(*refs):
        srcs, outs, send_sems, recv_sems = refs[:n], refs[n:2 * n], refs[2 * n], refs[2 * n + 1]
        x, y, c = _place()
        for a in range(n):
            piece = pl.ds(c * halves[a], halves[a])
            _remote(srcs[a].at[piece, :], outs[a].at[piece, :], send_sems, recv_sems, a, (x, y, 1 - c)).start()
        for a in range(n):
            mine = outs[a].at[pl.ds(c * halves[a], halves[a]), :]
            theirs = outs[a].at[pl.ds((1 - c) * halves[a], halves[a]), :]
            _remote(mine, theirs, send_sems, recv_sems, a, (x, y, c)).wait()

    return _pcall(
        body, name="share_cores", in_specs=[HBM_SPEC] * n, out_specs=tuple([HBM_SPEC] * n),
        out_shape=tuple(_sds(b.shape, b.dtype) for b in blocks), input_output_aliases={a: a for a in range(n)},
        scratch_shapes=[pltpu.SemaphoreType.DMA((n,)), pltpu.SemaphoreType.DMA((n,))],
    )(*blocks)


def _sum_blocks(stacked, name, tm):
    n, rows, lanes = stacked.shape
    tm = min(tm, rows)

    def body(s_ref, o_ref):
        tot = s_ref[n - 1].astype(F32)
        for k in range(n - 1):
            tot = tot + s_ref[k].astype(F32)
        o_ref[...] = tot

    return _pcall(body, name=name, grid=(rows // tm,),
                  in_specs=[pl.BlockSpec((n, tm, lanes), lambda i: (0, i, 0))],
                  out_specs=pl.BlockSpec((tm, lanes), lambda i: (i, 0)), out_shape=_sds((rows, lanes), F32),
                  compiler_params=_cparams("parallel"))(stacked)


def _add_halves(grads, theirs, core):
    n = len(grads)
    steps = 2
    tiles = [t.shape[1] // steps for t in theirs]
    cols = [t.shape[2] for t in theirs]

    def body(c_ref, *refs):
        gs, ts, outs = refs[:n], refs[n:2 * n], refs[2 * n:]
        for a in range(n):
            outs[a][...] = (gs[a][...] + ts[a][...]).astype(BF16)

    own = [pl.BlockSpec((None, tm, w), lambda k, i, c: (k, c[0] * steps + i, 0)) for tm, w in zip(tiles, cols)]
    same = [pl.BlockSpec((None, tm, w), lambda k, i, c: (k, i, 0)) for tm, w in zip(tiles, cols)]
    return _pcall(
        body, name="add_core_halves",
        grid_spec=pltpu.PrefetchScalarGridSpec(
            num_scalar_prefetch=1, grid=(N_CHIPS, steps), in_specs=own + same, out_specs=tuple(same)),
        out_shape=tuple(_sds(t.shape, BF16) for t in theirs),
        compiler_params=_cparams("parallel", "parallel"))(core, *grads, *theirs)


def _allreduce_small(part):
    rows, lanes = part.shape
    ndev = 8

    def body(src, tot, buf, send_sems, recv_sems):
        x, y, c = _place()
        me = 4 * x + 2 * y + c
        buf[me] = src[...]
        sends = []
        for k in range(1, ndev):
            peer = (x ^ (k >> 2), y ^ ((k >> 1) & 1), c ^ (k & 1))
            cp = _remote(src, buf.at[me], send_sems, recv_sems, k - 1, peer)
            cp.start()
            sends.append(cp)
        for k in range(1, ndev):
            theirs = buf.at[me ^ k]
            _remote(theirs, theirs, send_sems, recv_sems, k - 1, (x, y, c)).wait_recv()
        for cp in sends:
            cp.wait_send()
        acc = buf[0]
        for d in range(1, ndev):
            acc = acc + buf[d]
        tot[...] = acc

    vm = pl.BlockSpec(memory_space=pltpu.VMEM)
    return _pcall(
        body, name="allreduce_small", in_specs=[vm], out_specs=vm, out_shape=_sds((rows, lanes), F32),
        scratch_shapes=[pltpu.VMEM((ndev, rows, lanes), F32), pltpu.SemaphoreType.DMA((ndev - 1,)),
                        pltpu.SemaphoreType.DMA((ndev - 1,))],
    )(part)


def _pack_small(vals):
    parts = []
    for name, shape, r in SMALL:
        flat = vals[name].reshape(-1).astype(F32)
        parts.append(jnp.pad(flat, (0, r * LANES - flat.shape[0])).reshape(r, LANES))
    used = sum(r for _, _, r in SMALL)
    parts.append(jnp.zeros((SMALL_ROWS - used, LANES), F32))
    return jnp.concatenate(parts, axis=0)


def _unpack_small(packed):
    out, off = {}, 0
    for name, shape, r in SMALL:
        n = int(np.prod(shape))
        out[name] = packed[off:off + r].reshape(-1)[:n].reshape(shape)
        off += r
    return out


def _heads_major(a, nh):
    t = a.shape[0]
    return a.reshape(t, nh, a.shape[1] // nh).transpose(1, 0, 2)


def _tokens_major(a):
    nh, t, w = a.shape
    return a.transpose(1, 0, 2).reshape(t, nh * w)


def _local_step(x, target, small, wfull):
    t = x.shape[0]
    nh, hd = DIL_HEADS, DIL_HD
    w_in = wfull["w_in"].transpose(1, 0, 2).reshape(D_MODEL, -1)
    w_out = wfull["w_out"].reshape(D_MODEL, D_MODEL)
    w_qb, w_kvb = wfull["mla_w_q_b"], wfull["mla_w_kv_b"]
    grads_s, grads_b = {}, {}

    x1, ffn1_saved = _ffn_fwd(x, small["ffn1_norm"], wfull["ffn1_w_gate"], wfull["ffn1_w_up"],
                              wfull["ffn1_w_down"], "ffn1")
    hm = _rms_fwd(x1, small["mix_norm"], BF16, "mix_norm", 512)
    proj = _mm_simple("in_proj", hm, w_in, NN, F32, tm=1024)
    cq, ckv, k_pe = proj[:, 1536:1792], proj[:, 1792:1920], proj[:, 1920:1984]

    gq, gk = jnp.tile(small["dil_q_norm"], (1, nh)), jnp.tile(small["dil_k_norm"], (1, nh))
    qn = _head_norm_fwd(proj, 0, gq, "dil_q_norm", 512)
    kn = _head_norm_fwd(proj, 1, gk, "dil_k_norm", 512)
    v_d = proj[:, 2 * DIL_WIDTH:3 * DIL_WIDTH].astype(BF16)
    bias = _bias_tiles(small["rel_bias"]).reshape(3, nh // 2, 2 * QB, QB + DIL_W)
    outs, lses = [], []
    for b, dil in enumerate(DIL_DILATIONS):
        o_b, lse_b = _dil_fwd(qn, kn, v_d, bias[b], dil, f"dil_fwd_{dil}")
        outs.append(o_b)
        lses.append(lse_b)
    o_dil, lse_tot, od = _dil_merge(outs, lses, small["out_norm_dil"], 512)

    mh = MLA_HEADS
    cos_t, sin_t = _rope_tables(t)
    cqn = _rms_fwd(cq, small["mla_q_a_norm"], BF16, "mla_q_a_norm", 512)
    ckvn = _rms_fwd(ckv, small["mla_kv_a_norm"], BF16, "mla_kv_a_norm", 512)
    tm = min(512, t)

    th = min(2048, t)

    def head_proj(name, a, w, width):
        k = a.shape[1]
        return _mm(name, (mh, t // th, 1),
                   [(a, pl.BlockSpec((th, k), lambda h, i, r: (i, 0)), w, pl.BlockSpec((None, k, width), lambda h, i, r: (h, 0, 0)))],
                   NN, _sds((mh, t, width), F32), pl.BlockSpec((None, th, width), lambda h, i, r: (h, i, 0)), (th, width))

    q_raw = head_proj("mla_q_proj", cqn, w_qb, MLA_QK)
    kv_raw = head_proj("mla_kv_proj", ckvn, w_kvb, MLA_NOPE + MLA_V)
    k_raw = jnp.concatenate([kv_raw[:, :, :MLA_NOPE], jnp.broadcast_to(k_pe[None], (mh, t, MLA_ROPE))], axis=2)
    v_m = kv_raw[:, :, MLA_NOPE:].astype(BF16)
    q_raw2, k_raw2 = q_raw.reshape(mh * t, MLA_QK), k_raw.reshape(mh * t, MLA_QK)
    q_scale = MLA_QK ** -0.5
    q_m = _mla_qk_fwd(q_raw2, small["mla_q_norm"], cos_t, sin_t, q_scale, "mla_q_rope", 2048).reshape(mh, t, MLA_QK)
    k_m = _mla_qk_fwd(k_raw2, small["mla_k_norm"], cos_t, sin_t, 1.0, "mla_k_rope", 2048).reshape(mh, t, MLA_QK)
    o_mla_h, lse_m = _mla_fwd(q_m, k_m, v_m, 512, 2048)
    o_mla = _tokens_major(o_mla_h)

    om = _rms_fwd(o_mla, small["out_norm_mla"], BF16, "out_norm_mla", 512)
    half_w = DIL_WIDTH
    row = pl.BlockSpec((tm, D_MODEL), lambda i, j, r: (i, 0))
    act_spec = pl.BlockSpec((tm, half_w), lambda i, j, r: (i, 0))
    x2 = _mm("out_proj", (t // tm, 1, 1),
             [(od, act_spec, w_out, pl.BlockSpec((half_w, D_MODEL), lambda i, j, r: (0, 0))),
              (om, act_spec, w_out, pl.BlockSpec((half_w, D_MODEL), lambda i, j, r: (1, 0)))],
             NN, _sds((t, D_MODEL), F32), row, (tm, D_MODEL), res=(x1, row))
    x3, ffn2_saved = _ffn_fwd(x2, small["ffn2_norm"], wfull["ffn2_w_gate"], wfull["ffn2_w_up"],
                              wfull["ffn2_w_down"], "ffn2")
    dy, loss = _loss_head(x3, target, 512)

    dx2, grads_s["ffn2_norm"], grads_b["ffn2_w_gate"], grads_b["ffn2_w_up"], grads_b["ffn2_w_down"] = _ffn_bwd(
        dy, x2, small["ffn2_norm"], wfull["ffn2_w_gate"], wfull["ffn2_w_up"], wfull["ffn2_w_down"], ffn2_saved, "ffn2")

    d_ocat = _mm_simple("out_proj_dx", dx2, w_out, NT, F32, tm=1024)
    dw_out_d = _mm_simple("out_proj_dw_dil", od, dx2, TN, F32, tk=2048)
    dw_out_m = _mm_simple("out_proj_dw_mla", om, dx2, TN, F32, tk=2048)
    grads_b["w_out"] = jnp.concatenate([dw_out_d, dw_out_m], axis=0).reshape(N_CHIPS, D_MODEL // N_CHIPS, D_MODEL)
    do_dil, grads_s["out_norm_dil"] = _rms_bwd([d_ocat[:, :half_w]], o_dil, small["out_norm_dil"], None, "out_norm_dil_bwd", 512)
    do_mla, grads_s["out_norm_mla"] = _rms_bwd([d_ocat[:, half_w:]], o_mla, small["out_norm_mla"], None, "out_norm_mla_bwd", 512)

    do_m = _heads_major(do_mla, mh)
    dl_m = _rowdot(do_m.reshape(mh * t, MLA_V), o_mla_h.reshape(mh * t, MLA_V), "mla_delta", 2048).reshape(mh, t, 1)
    dk_m, dv_m, dq_t = _mla_bwd(q_m, k_m, k_m.transpose(0, 2, 1), v_m, do_m, lse_m.reshape(mh, 1, t),
                                dl_m.reshape(mh, 1, t), 2048, 512)
    dq_m = dq_t.transpose(0, 1, 3, 2).reshape(mh, t, MLA_QK)
    dq_raw, grads_s["mla_q_norm"] = _mla_qk_bwd(dq_m.reshape(mh * t, MLA_QK), q_raw2, small["mla_q_norm"],
                                                 cos_t, sin_t, q_scale, "mla_q_rope_bwd", 2048)
    dk_raw, grads_s["mla_k_norm"] = _mla_qk_bwd(dk_m.reshape(mh * t, MLA_QK), k_raw2, small["mla_k_norm"],
                                                 cos_t, sin_t, 1.0, "mla_k_rope_bwd", 2048)
    dq_raw = dq_raw.reshape(mh, t, MLA_QK)
    dk_raw = dk_raw.reshape(mh, t, MLA_QK)
    dkv_raw = jnp.concatenate([dk_raw[:, :, :MLA_NOPE], dv_m], axis=2)
    dk_pe_h = dk_raw[:, :, MLA_NOPE:]

    def head_proj_dx(name, d, w):
        width, k = d.shape[2], w.shape[1]
        pairs = [(d, pl.BlockSpec((None, th, width), lambda i, j, r, h=h: (h, i, 0)),
                  w, pl.BlockSpec((None, k, width), lambda i, j, r, h=h: (h, 0, 0))) for h in range(mh)]
        return _mm(name, (t // th, 1, 1), pairs, NT, _sds((t, k), F32),
                   pl.BlockSpec((th, k), lambda i, j, r: (i, 0)), (th, k))

    def head_proj_dw(name, a, d):
        width, k = d.shape[2], a.shape[1]
        return _mm(name, (mh, 1, t // th),
                   [(a, pl.BlockSpec((th, k), lambda h, j, r: (r, 0)), d, pl.BlockSpec((None, th, width), lambda h, j, r: (h, r, 0)))],
                   TN, _sds((mh, k, width), F32), pl.BlockSpec((None, k, width), lambda h, j, r: (h, 0, 0)), (k, width))

    d_cqn = head_proj_dx("mla_q_proj_dx", dq_raw, w_qb)
    d_ckvn = head_proj_dx("mla_kv_proj_dx", dkv_raw, w_kvb)
    grads_b["mla_w_q_b"] = head_proj_dw("mla_q_proj_dw", cqn, dq_raw)
    grads_b["mla_w_kv_b"] = head_proj_dw("mla_kv_proj_dw", ckvn, dkv_raw)
    d_cq, grads_s["mla_q_a_norm"] = _rms_bwd([d_cqn], cq, small["mla_q_a_norm"], None, "mla_q_a_norm_bwd", 512)
    d_ckv, grads_s["mla_kv_a_norm"] = _rms_bwd([d_ckvn], ckv, small["mla_kv_a_norm"], None, "mla_kv_a_norm_bwd", 512)
    d_kpe = _sum_blocks(dk_pe_h.reshape(mh, t * MLA_ROPE // LANES, LANES), "mla_kpe_sum", 1024).reshape(t, MLA_ROPE)

    stats = _dil_stats(do_dil, o_dil, lse_tot, 512)
    do_db = do_dil.astype(BF16)
    dqs, dks, dvs, dtiles = [], [], [], []
    for b, dil in enumerate(DIL_DILATIONS):
        dq_b, dk_b, dv_b, db_b = _dil_bwd(qn, kn, v_d, do_db, stats, bias[b], dil, f"dil_bwd_{dil}")
        dqs.append(dq_b)
        dks.append(dk_b)
        dvs.append(dv_b)
        dtiles.append(db_b)
    grads_s["rel_bias"] = _bias_grad(jnp.stack(dtiles).reshape(3, nh, QB, QB + DIL_W))
    dq_a, dgq = _head_norm_bwd(dqs, proj, 0, gq, "dil_q_norm_bwd", 512)
    dk_a, dgk = _head_norm_bwd(dks, proj, 1, gk, "dil_k_norm_bwd", 512)
    grads_s["dil_q_norm"], grads_s["dil_k_norm"] = dgq[:, :hd], dgk[:, :hd]
    dv_a = _add3(dvs[0], dvs[1], dvs[2], "dil_dv_sum", 512)

    dparts = [dq_a, dk_a, dv_a, d_cq, d_ckv, d_kpe]
    t2 = min(1024, t)
    pairs, dw_parts, lo = [], [], 0
    for n, dpart in enumerate(dparts):
        width = dpart.shape[1]
        w_part = w_in[:, lo:lo + width]
        pairs.append((dpart, pl.BlockSpec((t2, width), lambda i, j, r: (i, 0)),
                      w_part, pl.BlockSpec((D_MODEL, width), lambda i, j, r: (0, 0))))
        dw_parts.append(_mm_simple(f"in_proj_dw_{n}", hm, dpart, TN, F32, tk=2048))
        lo += width
    d_hm = _mm("in_proj_dx", (t // t2, 1, 1), pairs, NT, _sds((t, D_MODEL), F32),
               pl.BlockSpec((t2, D_MODEL), lambda i, j, r: (i, 0)), (t2, D_MODEL))
    dw_in = jnp.concatenate(dw_parts, axis=1)
    grads_b["w_in"] = dw_in.reshape(D_MODEL, N_CHIPS, -1).transpose(1, 0, 2)
    dx1, grads_s["mix_norm"] = _rms_bwd([d_hm], x1, small["mix_norm"], dx2, "mix_norm_bwd", 512)
    dx, grads_s["ffn1_norm"], grads_b["ffn1_w_gate"], grads_b["ffn1_w_up"], grads_b["ffn1_w_down"] = _ffn_bwd(
        dx1, x, small["ffn1_norm"], wfull["ffn1_w_gate"], wfull["ffn1_w_up"], wfull["ffn1_w_down"], ffn1_saved, "ffn1")
    return loss, dx, grads_s, grads_b


def kernel(x, ffn1_norm, ffn1_w_gate, ffn1_w_up, ffn1_w_down, mix_norm, w_in, dil_q_norm, dil_k_norm, rel_bias, mla_q_a_norm, mla_w_q_b, mla_kv_a_norm, mla_w_kv_b, mla_q_norm, mla_k_norm, out_norm_dil, out_norm_mla, w_out, ffn2_norm, ffn2_w_gate, ffn2_w_up, ffn2_w_down, loss_target, m_ffn1_norm, m_ffn1_w_gate, m_ffn1_w_up, m_ffn1_w_down, m_mix_norm, m_w_in, m_dil_q_norm, m_dil_k_norm, m_rel_bias, m_mla_q_a_norm, m_mla_w_q_b, m_mla_kv_a_norm, m_mla_w_kv_b, m_mla_q_norm, m_mla_k_norm, m_out_norm_dil, m_out_norm_mla, m_w_out, m_ffn2_norm, m_ffn2_w_gate, m_ffn2_w_up, m_ffn2_w_down, v_ffn1_norm, v_ffn1_w_gate, v_ffn1_w_up, v_ffn1_w_down, v_mix_norm, v_w_in, v_dil_q_norm, v_dil_k_norm, v_rel_bias, v_mla_q_a_norm, v_mla_w_q_b, v_mla_kv_a_norm, v_mla_w_kv_b, v_mla_q_norm, v_mla_k_norm, v_out_norm_dil, v_out_norm_mla, v_w_out, v_ffn2_norm, v_ffn2_w_gate, v_ffn2_w_up, v_ffn2_w_down):
    given = dict(locals())
    big_names = [name for name, _ in BIG]
    small_names = [name for name, _, _ in SMALL]

    chip = (2 * lax.axis_index("x") + lax.axis_index("y")).astype(jnp.int32)
    core = lax.axis_index("c").astype(jnp.int32)
    mine = [given[n].astype(BF16) for n in big_names]
    gathered = _gather_weights([m[0] for m in mine])
    wfull = {n: lax.dynamic_update_slice(g, m, (chip, 0, 0)) for n, g, m in zip(big_names, gathered, mine)}
    small = {n: given[n] for n in small_names}

    loss, dx, grads_s, grads_b = _local_step(x[0], loss_target[0], small, wfull)
    loss = lax.psum(loss[0, 0], ("x", "y", "c"))

    partial = [grads_b[n] for n in big_names]
    chip_part = _add_halves(partial, _reduce_cores(partial), core.reshape(1))
    reduced = _sum_partials(_scatter_chips(chip_part), chip_part, jnp.stack([chip, core]))
    g_big = dict(zip(big_names, _share_cores(reduced)))
    g_small = _unpack_small(_allreduce_small(_pack_small(grads_s)))

    grad, delta, new_m, new_v = {}, {}, {}, {}
    for name, shape in BIG:
        g2 = g_big[name]
        d_, m_, v_ = _adamw(given[name].reshape(shape), g2, given["m_" + name].reshape(shape),
                            given["v_" + name].reshape(shape), f"adamw_{name}")
        full = given[name].shape
        grad[name], delta[name], new_m[name], new_v[name] = (a.reshape(full) for a in (g2, d_, m_, v_))
    ps = {k: _pack_small({n: given[pre + n] for n in small_names}) for k, pre in (("w", ""), ("m", "m_"), ("v", "v_"))}
    gs_packed = _pack_small(g_small)
    d_s, m_s, v_s = (_unpack_small(a) for a in _adamw(ps["w"], gs_packed, ps["m"], ps["v"], "adamw_small"))
    for name in small_names:
        grad[name], delta[name], new_m[name], new_v[name] = g_small[name], d_s[name], m_s[name], v_s[name]

    return (loss, dx[None], *[grad[n] for n in WEIGHTS], *[delta[n] for n in WEIGHTS],
            *[new_m[n] for n in WEIGHTS], *[new_v[n] for n in WEIGHTS])
```

```python
import functools

import numpy as np
import jax
import jax.numpy as jnp
from jax import lax
from jax.experimental import pallas as pl
from jax.experimental.pallas import tpu as pltpu

F32 = jnp.float32
BF16 = jnp.bfloat16

D_MODEL = 1024
D_FF = 2816
N_CHIPS = 4
DIL_HEADS = 8
DIL_HD = 64
DIL_WIDTH = 512
DIL_DILATIONS = (1, 4, 16)
DIL_W = 128
QB = 128
MLA_HEADS = 4
MLA_NOPE = 128
MLA_ROPE = 64
MLA_QK = 192
MLA_V = 128
MLA_Q_RANK = 256
MLA_KV_RANK = 128
ROPE_BASE = 10000.0
REL_BUCKETS = 32
REL_MAX_DIST = 2048
FFN_RESID = 0.5
EPS = 1e-6
NEG = -1e30
LANES = 128

ADAM_LR = 0.001
ADAM_B1 = 0.9
ADAM_B2 = 0.999
ADAM_EPS = 1e-08
ADAM_WD = 0.01
ADAM_STEP = 10

NT = (((1,), (1,)), ((), ()))
NN = (((1,), (0,)), ((), ()))
TN = (((0,), (0,)), ((), ()))

BIG = (
    ("ffn1_w_gate", (D_MODEL, D_FF // N_CHIPS)),
    ("ffn1_w_up", (D_MODEL, D_FF // N_CHIPS)),
    ("ffn1_w_down", (D_FF // N_CHIPS, D_MODEL)),
    ("w_in", (D_MODEL, 1984 // N_CHIPS)),
    ("mla_w_q_b", (MLA_Q_RANK, MLA_QK)),
    ("mla_w_kv_b", (MLA_KV_RANK, MLA_NOPE + MLA_V)),
    ("w_out", (D_MODEL // N_CHIPS, D_MODEL)),
    ("ffn2_w_gate", (D_MODEL, D_FF // N_CHIPS)),
    ("ffn2_w_up", (D_MODEL, D_FF // N_CHIPS)),
    ("ffn2_w_down", (D_FF // N_CHIPS, D_MODEL)),
)
SMALL = (
    ("ffn1_norm", (1, 1024), 8), ("mix_norm", (1, 1024), 8), ("dil_q_norm", (1, 64), 1),
    ("dil_k_norm", (1, 64), 1), ("rel_bias", (8, 32), 2), ("mla_q_a_norm", (1, 256), 2),
    ("mla_kv_a_norm", (1, 128), 1), ("mla_q_norm", (1, 192), 2), ("mla_k_norm", (1, 192), 2),
    ("out_norm_dil", (1, 512), 4), ("out_norm_mla", (1, 512), 4), ("ffn2_norm", (1, 1024), 8),
)
SMALL_ROWS = 48
WEIGHTS = ("ffn1_norm", "ffn1_w_gate", "ffn1_w_up", "ffn1_w_down", "mix_norm", "w_in", "dil_q_norm",
           "dil_k_norm", "rel_bias", "mla_q_a_norm", "mla_w_q_b", "mla_kv_a_norm", "mla_w_kv_b",
           "mla_q_norm", "mla_k_norm", "out_norm_dil", "out_norm_mla", "w_out", "ffn2_norm",
           "ffn2_w_gate", "ffn2_w_up", "ffn2_w_down")


def _pcall(body, **kw):
    return pl.pallas_call(body, **kw)


def _cparams(*sem):
    return pltpu.CompilerParams(dimension_semantics=sem)


def _sds(shape, dtype):
    return jax.ShapeDtypeStruct(shape, dtype)


def _dot(a, b, dn):
    return lax.dot_general(a, b, dn, preferred_element_type=F32)


def _rms_fwd(x, g, out_dtype, name, tm):
    n, d = x.shape
    tm = min(tm, n)

    def body(x_ref, g_ref, o_ref):
        xf = x_ref[...].astype(F32)
        r = lax.rsqrt(jnp.mean(xf * xf, axis=-1, keepdims=True) + EPS)
        o_ref[...] = (xf * r * g_ref[...]).astype(o_ref.dtype)

    return _pcall(
        body, name=name, grid=(n // tm,),
        in_specs=[pl.BlockSpec((tm, d), lambda i: (i, 0)), pl.BlockSpec((1, d), lambda i: (0, 0))],
        out_specs=pl.BlockSpec((tm, d), lambda i: (i, 0)),
        out_shape=_sds((n, d), out_dtype), compiler_params=_cparams("parallel"))(x, g)


def _rms_bwd(dys, x, g, res, name, tm):
    n, d = x.shape
    tm = min(tm, n)
    nd = len(dys)
    has_res = res is not None

    def body(*refs):
        dy_refs = refs[:nd]
        x_ref, g_ref = refs[nd], refs[nd + 1]
        res_ref = refs[nd + 2] if has_res else None
        dx_ref, dg_ref = refs[-2], refs[-1]
        dy = dy_refs[0][...].astype(F32)
        for r_ in dy_refs[1:]:
            dy = dy + r_[...].astype(F32)
        xf = x_ref[...].astype(F32)
        r = lax.rsqrt(jnp.mean(xf * xf, axis=-1, keepdims=True) + EPS)
        xh = xf * r
        dxh = dy * g_ref[...]
        dx = r * (dxh - xh * jnp.mean(dxh * xh, axis=-1, keepdims=True))
        if has_res:
            dx = dx + res_ref[...]
        dx_ref[...] = dx

        @pl.when(pl.program_id(0) == 0)
        def _():
            dg_ref[...] = jnp.zeros_like(dg_ref)

        dg_ref[...] += jnp.sum(dy * xh, axis=0, keepdims=True)

    row = pl.BlockSpec((tm, d), lambda i: (i, 0))
    vec = pl.BlockSpec((1, d), lambda i: (0, 0))
    ins = list(dys) + [x, g] + ([res] if has_res else [])
    return _pcall(
        body, name=name, grid=(n // tm,),
        in_specs=[row] * nd + [row, vec] + ([row] if has_res else []),
        out_specs=(row, vec),
        out_shape=(_sds((n, d), F32), _sds((1, d), F32)),
        compiler_params=_cparams("arbitrary"))(*ins)


def _mm(name, grid, pairs, dn, out_shape, out_spec, acc_shape, res=None, scale=1.0):
    npairs = len(pairs)
    nred = grid[2]
    has_res = res is not None

    def body(*refs):
        ab = refs[:2 * npairs]
        res_ref = refs[2 * npairs] if has_res else None
        o_ref = refs[2 * npairs + int(has_res)]
        acc_ref = refs[-1] if nred > 1 else None
        tot = None
        for p in range(npairs):
            d = _dot(ab[2 * p][...].astype(BF16), ab[2 * p + 1][...].astype(BF16), dn)
            tot = d if tot is None else tot + d

        def finish(v):
            if scale != 1.0:
                v = v * scale
            if has_res:
                v = res_ref[...] + v
            o_ref[...] = v.astype(o_ref.dtype)

        if nred == 1:
            finish(tot)
        else:
            r = pl.program_id(2)

            @pl.when(r == 0)
            def _():
                acc_ref[...] = tot

            @pl.when(r > 0)
            def _():
                acc_ref[...] += tot

            @pl.when(r == nred - 1)
            def _():
                finish(acc_ref[...])

    ins, specs = [], []
    for a, a_spec, b, b_spec in pairs:
        ins += [a, b]
        specs += [a_spec, b_spec]
    if has_res:
        ins.append(res[0])
        specs.append(res[1])
    return _pcall(
        body, name=name, grid=grid, in_specs=specs, out_specs=out_spec, out_shape=out_shape,
        scratch_shapes=[pltpu.VMEM(acc_shape, F32)] if nred > 1 else [],
        compiler_params=_cparams("parallel", "parallel", "arbitrary"))(*ins)


def _ffn_up(h, wg, wu, name, tm):
    t, d = h.shape
    nc, _, fs = wg.shape
    tm = min(tm, t)

    def body(h_ref, wg_ref, wu_ref, g_ref, u_ref, a_ref):
        hh = h_ref[...]
        gate = _dot(hh, wg_ref[...], NN)
        up = _dot(hh, wu_ref[...], NN)
        sig = jax.nn.sigmoid(gate)
        silu = gate * sig
        g_ref[...] = (up * (sig + silu * (1.0 - sig))).astype(BF16)
        u_ref[...] = silu.astype(BF16)
        a_ref[...] = (silu * up).astype(BF16)

    wspec = pl.BlockSpec((None, d, fs), lambda c, i: (c, 0, 0))
    ospec = pl.BlockSpec((None, tm, fs), lambda c, i: (c, i, 0))
    osd = _sds((nc, t, fs), BF16)
    return _pcall(
        body, name=name, grid=(nc, t // tm),
        in_specs=[pl.BlockSpec((tm, d), lambda c, i: (i, 0)), wspec, wspec],
        out_specs=(ospec, ospec, ospec), out_shape=(osd, osd, osd),
        compiler_params=_cparams("parallel", "parallel"))(h, wg, wu)


def _ffn_hidden_bwd(dy, h, wd, dact_dgate, dact_dup, act, name, tm):
    t, d = dy.shape
    nc, fs, _ = wd.shape
    tm = min(tm, t)
    nt = t // tm

    def body(dy_ref, h_ref, wd_ref, g_ref, u_ref, a_ref, dg_ref, du_ref, dwg_hbm, dwu_hbm, dwd_hbm,
             wg_acc, wu_acc, wd_acc, sem):
        c, i = pl.program_id(0), pl.program_id(1)
        dyb = dy_ref[...].astype(BF16)
        da = _dot(dyb, wd_ref[...], NT) * FFN_RESID
        dgate = (da * g_ref[...].astype(F32)).astype(BF16)
        dup = (da * u_ref[...].astype(F32)).astype(BF16)
        dg_ref[...] = dgate
        du_ref[...] = dup
        hh = h_ref[...]
        parts = (_dot(hh, dgate, TN), _dot(hh, dup, TN), _dot(a_ref[...], dyb, TN) * FFN_RESID)
        accs = (wg_acc, wu_acc, wd_acc)

        @pl.when(i == 0)
        def _():
            for acc, part in zip(accs, parts):
                acc[...] = part

        @pl.when(i > 0)
        def _():
            for acc, part in zip(accs, parts):
                acc[...] += part

        @pl.when(i == nt - 1)
        def _():
            copies = [pltpu.make_async_copy(acc, out.at[c], sem.at[n])
                      for n, (acc, out) in enumerate(zip(accs, (dwg_hbm, dwu_hbm, dwd_hbm)))]
            for cp in copies:
                cp.start()
            for cp in copies:
                cp.wait()

    tok = pl.BlockSpec((tm, d), lambda c, i: (i, 0))
    cspec = pl.BlockSpec((None, tm, fs), lambda c, i: (c, i, 0))
    hbm = pl.BlockSpec(memory_space=pltpu.HBM)
    osd = _sds((nc, t, fs), BF16)
    return _pcall(
        body, name=name, grid=(nc, nt),
        in_specs=[tok, tok, pl.BlockSpec((None, fs, d), lambda c, i: (c, 0, 0)), cspec, cspec, cspec],
        out_specs=(cspec, cspec, hbm, hbm, hbm),
        out_shape=(osd, osd, _sds((nc, d, fs), F32), _sds((nc, d, fs), F32), _sds((nc, fs, d), F32)),
        scratch_shapes=[pltpu.VMEM((d, fs), F32), pltpu.VMEM((d, fs), F32), pltpu.VMEM((fs, d), F32),
                        pltpu.SemaphoreType.DMA((3,))],
        compiler_params=_cparams("arbitrary", "arbitrary"))(dy, h, wd, dact_dgate, dact_dup, act)


def _ffn_fwd(x, g, wg, wu, wd, tag):
    t = x.shape[0]
    nc, _, fs = wg.shape
    tm = min(512, t)
    h = _rms_fwd(x, g, BF16, f"{tag}_norm", 512)
    dact_dgate, dact_dup, act = _ffn_up(h, wg, wu, f"{tag}_up", 1024)
    pairs = [(act, pl.BlockSpec((None, tm, fs), lambda i, j, r, c=c: (c, i, 0)),
              wd, pl.BlockSpec((None, fs, D_MODEL), lambda i, j, r, c=c: (c, 0, 0))) for c in range(nc)]
    row = pl.BlockSpec((tm, D_MODEL), lambda i, j, r: (i, 0))
    y = _mm(f"{tag}_down", (t // tm, 1, 1), pairs, NN, _sds((t, D_MODEL), F32), row, (tm, D_MODEL),
            res=(x, row), scale=FFN_RESID)
    return y, (h, dact_dgate, dact_dup, act)


def _ffn_bwd(dy, x, g, wg, wu, wd, saved, tag):
    h, dact_dgate, dact_dup, act = saved
    t = x.shape[0]
    nc, _, fs = wg.shape
    tm = min(512, t)
    dgate, dup, dwg, dwu, dwd = _ffn_hidden_bwd(dy, h, wd, dact_dgate, dact_dup, act, f"{tag}_hidden_bwd", 1024)
    pairs = []
    for c in range(nc):
        a_spec = pl.BlockSpec((None, tm, fs), lambda i, j, r, c=c: (c, i, 0))
        w_spec = pl.BlockSpec((None, D_MODEL, fs), lambda i, j, r, c=c: (c, 0, 0))
        pairs += [(dgate, a_spec, wg, w_spec), (dup, a_spec, wu, w_spec)]
    dh = _mm(f"{tag}_dh", (t // tm, 1, 1), pairs, NT,
             _sds((t, D_MODEL), F32), pl.BlockSpec((tm, D_MODEL), lambda i, j, r: (i, 0)), (tm, D_MODEL))
    dx, dg = _rms_bwd([dh], x, g, dy, f"{tag}_dnorm", 512)
    return dx, dg, dwg, dwu, dwd


def _mm_simple(name, a, b, dn, out_dtype, tm=512, tk=512, res=None, scale=1.0):
    if dn == TN:
        k, m = a.shape
        n = b.shape[1]
        tk = min(tk, k)
        return _mm(name, (1, 1, k // tk),
                   [(a, pl.BlockSpec((tk, m), lambda i, j, r: (r, 0)), b, pl.BlockSpec((tk, n), lambda i, j, r: (r, 0)))],
                   TN, _sds((m, n), out_dtype), pl.BlockSpec((m, n), lambda i, j, r: (0, 0)), (m, n), scale=scale)
    m, k = a.shape
    n = b.shape[1] if dn == NN else b.shape[0]
    tm = min(tm, m)
    row = pl.BlockSpec((tm, n), lambda i, j, r: (i, 0))
    return _mm(name, (m // tm, 1, 1),
               [(a, pl.BlockSpec((tm, k), lambda i, j, r: (i, 0)), b, pl.BlockSpec(b.shape, lambda i, j, r: (0, 0)))],
               dn, _sds((m, n), out_dtype), row, (tm, n), res=None if res is None else (res, row), scale=scale)


def _t5_bucket(dist):
    max_exact = REL_BUCKETS // 2
    d = np.maximum(dist, 1).astype(np.float32)
    large = max_exact + (np.log(d / max_exact) / np.log(REL_MAX_DIST / max_exact)
                         * (REL_BUCKETS - max_exact)).astype(np.int32)
    large = np.minimum(large, REL_BUCKETS - 1)
    return np.where(dist < max_exact, dist, large).astype(np.int32)


def _bucket_tiles():
    i = np.arange(QB)[:, None]
    j = np.arange(QB + DIL_W)[None, :]
    delta = np.clip(i + DIL_W - j, 0, None)
    return np.stack([_t5_bucket(delta * dil) for dil in DIL_DILATIONS]).astype(np.int32)


def _bias_tiles(rel_bias):
    buckets = jnp.asarray(_bucket_tiles())

    def body(rb_ref, bk_ref, o_ref):
        bk = bk_ref[...]
        for h in range(DIL_HEADS):
            def pick(b, tile):
                return jnp.where(bk == b, rb_ref[h, b], tile)

            o_ref[h] = lax.fori_loop(0, REL_BUCKETS, pick, jnp.zeros((QB, QB + DIL_W), F32))

    return _pcall(
        body, name="dil_bias_tiles", grid=(3,),
        in_specs=[pl.BlockSpec(memory_space=pltpu.SMEM),
                  pl.BlockSpec((None, QB, QB + DIL_W), lambda b: (b, 0, 0))],
        out_specs=pl.BlockSpec((None, DIL_HEADS, QB, QB + DIL_W), lambda b: (b, 0, 0, 0)),
        out_shape=_sds((3, DIL_HEADS, QB, QB + DIL_W), F32),
        compiler_params=_cparams("parallel"))(rel_bias, buckets)


def _bias_grad(dtiles):
    buckets = jnp.asarray(_bucket_tiles())

    def body(dt_ref, bk_ref, o_ref):
        def one(b, carry):
            hit = [bk_ref[br] == b for br in range(3)]
            for h in range(DIL_HEADS):
                tot = jnp.zeros((), F32)
                for br in range(3):
                    tot = tot + jnp.sum(jnp.where(hit[br], dt_ref[br, h], 0.0))
                o_ref[h, b] = tot
            return carry

        lax.fori_loop(0, REL_BUCKETS, one, 0)

    return _pcall(
        body, name="dil_bias_grad",
        in_specs=[pl.BlockSpec(memory_space=pltpu.VMEM), pl.BlockSpec(memory_space=pltpu.VMEM)],
        out_specs=pl.BlockSpec(memory_space=pltpu.SMEM),
        out_shape=_sds((DIL_HEADS, REL_BUCKETS), F32))(dtiles, buckets)


def _split_heads(a, lo):
    zero = jnp.zeros_like(a)
    return jnp.concatenate([jnp.where(lo, a, zero), jnp.where(lo, zero, a)], axis=0)


def _side_by_side(a):
    n = a.shape[0] // 2
    return jnp.concatenate([a[:n], a[n:]], axis=1)


def _band_masks(prev_ok):
    ii = lax.broadcasted_iota(jnp.int32, (2 * QB, QB), 0) & (QB - 1)
    jj = lax.broadcasted_iota(jnp.int32, (2 * QB, QB), 1)
    return jj <= ii, jj >= ii + jnp.where(prev_ok, 0, QB)


def _dil_fwd(q, k, v, bias, dil, name):
    w = DIL_WIDTH
    t = q.shape[0] * dil
    npair = w // LANES
    nl = t // dil // QB
    scale = DIL_HD ** -0.5

    def body(q_ref, kc_ref, kp_ref, vc_ref, vp_ref, b_ref, o_ref, lse_ref):
        nn = pl.program_id(1)
        lo = lax.broadcasted_iota(jnp.int32, (QB, LANES), 1) < DIL_HD
        lo2 = lax.broadcasted_iota(jnp.int32, (2 * QB, LANES), 1) < DIL_HD
        ii = lax.broadcasted_iota(jnp.int32, (2 * QB, 2 * QB), 0) & (QB - 1)
        jj = lax.broadcasted_iota(jnp.int32, (2 * QB, 2 * QB), 1)
        first_key = jnp.maximum(ii, jnp.where(nn != 0, 0, QB))
        valid = (jj >= first_key) & (jj <= ii + QB)
        for p in range(npair):
            cols = slice(p * LANES, (p + 1) * LANES)
            qq = _split_heads(q_ref[:, cols], lo)
            kk = jnp.concatenate([kp_ref[:, cols], kc_ref[:, cols]], axis=0)
            vv = jnp.concatenate([vp_ref[:, cols], vc_ref[:, cols]], axis=0)
            s = jnp.where(valid, _dot(qq, kk, NT) * scale + b_ref[p], NEG)
            m = jnp.max(s, axis=-1, keepdims=True)
            e = jnp.exp(s - m)
            den = jnp.sum(e, axis=-1, keepdims=True)
            pn = (e * (1.0 / den)).astype(BF16)
            o_ref[:, cols] = _dot(_side_by_side(pn), _split_heads(vv, lo2), NN)
            lse = m + jnp.log(den)
            lse_ref[:, cols] = jnp.where(lo, lse[:QB], lse[QB:])

    cur = pl.BlockSpec((QB, w), lambda r, n: (n, r))
    prev = pl.BlockSpec((QB, w), lambda r, n: (jnp.maximum(n - 1, 0), r))
    sd = _sds((t // dil, dil * w), F32)
    return _pcall(
        body, name=name, grid=(dil, nl),
        in_specs=[cur, cur, prev, cur, prev, pl.BlockSpec((npair, 2 * QB, 2 * QB), lambda r, n: (0, 0, 0))],
        out_specs=(cur, cur), out_shape=(sd, sd),
        compiler_params=_cparams("parallel", "parallel"))(q, k, k, v, v, bias)


def _dil_bwd(q, k, v, do, stats, bias, dil, name):
    w = DIL_WIDTH
    t = q.shape[0] * dil
    npair = w // LANES
    nl = t // dil // QB
    scale = DIL_HD ** -0.5

    def body(qc_ref, qn_ref, doc_ref, don_ref, sc_ref, sn_ref, k_ref, v_ref, b_ref,
             dq_ref, dk_ref, dv_ref, db_ref, carry):
        r, nn = pl.program_id(0), pl.program_id(1)
        lo = lax.broadcasted_iota(jnp.int32, (QB, LANES), 1) < DIL_HD
        cur_ok, prev_ok = _band_masks(nn + 1 < nl)

        @pl.when((r == 0) & (nn == 0))
        def _():
            db_ref[...] = jnp.zeros_like(db_ref)
            carry[...] = jnp.zeros_like(carry)

        for p in range(npair):
            cols = slice(p * LANES, (p + 1) * LANES)
            kp, vp = k_ref[:, cols], v_ref[:, cols]
            k2 = _split_heads(kp, lo)

            def column(ref, lane):
                first = p * LANES + lane
                return jnp.concatenate([ref[:, first:first + 1], ref[:, first + DIL_HD:first + DIL_HD + 1]], axis=0)

            def side(q_ref, do_ref, s_ref, bias, ok):
                qq = _split_heads(q_ref[:, cols], lo)
                dd = _split_heads(do_ref[:, cols], lo)
                s = jnp.where(ok, _dot(qq, kp, NT) * scale + bias, NEG)
                prob = jnp.exp(s - column(s_ref, 0))
                ds = prob * (_dot(dd, vp, NT) - column(s_ref, DIL_HD // 2))
                return qq, dd, prob.astype(BF16), ds

            q1, d1, p1, ds1 = side(qc_ref, doc_ref, sc_ref, b_ref[p, :, QB:], cur_ok)
            q2, d2, p2, ds2 = side(qn_ref, don_ref, sn_ref, b_ref[p, :, :QB], prev_ok)
            ds1b, ds2b = ds1.astype(BF16), ds2.astype(BF16)
            dq_ref[:, cols] = carry[:, cols] + _dot(_side_by_side(ds1b), k2, NN) * scale
            carry[:, cols] = _dot(_side_by_side(ds2b), k2, NN) * scale
            dk_ref[:, cols] = _dot(jnp.concatenate([ds1b, ds2b], axis=0), jnp.concatenate([q1, q2], axis=0), TN) * scale
            dv_ref[:, cols] = _dot(jnp.concatenate([p1, p2], axis=0), jnp.concatenate([d1, d2], axis=0), TN)
            db_ref[p, :, QB:] += ds1
            db_ref[p, :, :QB] += ds2

    cur = pl.BlockSpec((QB, w), lambda r, n: (n, r))
    nxt = pl.BlockSpec((QB, w), lambda r, n: (jnp.minimum(n + 1, nl - 1), r))
    tile = pl.BlockSpec((npair, 2 * QB, 2 * QB), lambda r, n: (0, 0, 0))
    sd = _sds((t // dil, dil * w), F32)
    return _pcall(
        body, name=name, grid=(dil, nl),
        in_specs=[cur, nxt, cur, nxt, cur, nxt, cur, cur, tile],
        out_specs=(cur, cur, cur, tile),
        out_shape=(sd, sd, sd, _sds((npair, 2 * QB, 2 * QB), F32)),
        scratch_shapes=[pltpu.VMEM((QB, w), F32)],
        compiler_params=_cparams("arbitrary", "arbitrary"))(q, q, do, do, stats, stats, k, v, bias)


def _head_sum_matrix(scale):
    idx = np.arange(DIL_WIDTH) // DIL_HD
    return jnp.asarray((idx[:, None] == idx[None, :]).astype(np.float32) * scale, BF16)


def _head_sum(x, mat):
    hi = x.astype(BF16)
    lo = (x - hi.astype(F32)).astype(BF16)
    return _dot(hi, mat, NN) + _dot(lo, mat, NN)


def _to_views(src, tmp, out_refs):
    tm, w = src.shape
    for j in range(w // LANES):
        tmp[j] = src[:, j * LANES:(j + 1) * LANES]
    for d, o_ref in zip(DIL_DILATIONS, out_refs):
        if d == 1:
            o_ref[...] = src.astype(o_ref.dtype)
            continue
        for r in range(d):
            for j in range(w // LANES):
                lo = r * w + j * LANES
                o_ref[:, lo:lo + LANES] = tmp[j, pl.ds(r, tm // d, stride=d), :].astype(o_ref.dtype)


def _from_view(v_ref, tmp, d):
    tm = tmp.shape[1]
    w = v_ref.shape[1] // d
    for r in range(d):
        for j in range(w // LANES):
            lo = r * w + j * LANES
            tmp[j, pl.ds(r, tm // d, stride=d), :] = v_ref[:, lo:lo + LANES]
    return jnp.concatenate([tmp[j] for j in range(w // LANES)], axis=1)


def _view_specs(tm, t, dtype):
    specs = tuple(pl.BlockSpec((tm // d, d * DIL_WIDTH), lambda i: (i, 0)) for d in DIL_DILATIONS)
    shapes = tuple(_sds((t // d, d * DIL_WIDTH), dtype) for d in DIL_DILATIONS)
    return specs, shapes


def _view_scratch(tm):
    return pltpu.VMEM((DIL_WIDTH // LANES, tm, LANES), F32)


def _dil_merge(outs, lses, g, tm):
    w = DIL_WIDTH
    t = outs[0].shape[0]
    tm = min(tm, t)

    def body(o0, o1, o2, l0, l1, l2, g_ref, o_ref, l_ref, n_ref, so1, so2, sl1, sl2):
        d1, d2 = DIL_DILATIONS[1], DIL_DILATIONS[2]
        a0, a1, a2 = l0[...], _from_view(l1, sl1, d1), _from_view(l2, sl2, d2)
        m = jnp.maximum(jnp.maximum(a0, a1), a2)
        e0, e1, e2 = jnp.exp(a0 - m), jnp.exp(a1 - m), jnp.exp(a2 - m)
        den = e0 + e1 + e2
        o = (e0 * o0[...] + e1 * _from_view(o1, so1, d1) + e2 * _from_view(o2, so2, d2)) / den
        o_ref[...] = o
        l_ref[...] = m + jnp.log(den)
        r = lax.rsqrt(jnp.mean(o * o, axis=-1, keepdims=True) + EPS)
        n_ref[...] = (o * r * g_ref[...]).astype(n_ref.dtype)

    specs, _ = _view_specs(tm, t, F32)
    spec = pl.BlockSpec((tm, w), lambda i: (i, 0))
    return _pcall(
        body, name="dil_merge", grid=(t // tm,),
        in_specs=list(specs) * 2 + [pl.BlockSpec((1, w), lambda i: (0, 0))], out_specs=(spec, spec, spec),
        out_shape=(_sds((t, w), F32), _sds((t, w), F32), _sds((t, w), BF16)),
        scratch_shapes=[_view_scratch(tm)] * 4,
        compiler_params=_cparams("parallel"))(*outs, *lses, g)


def _dil_stats(do, o, lse, tm):
    t, w = do.shape
    tm = min(tm, t)

    def body(a_ref, b_ref, l_ref, m_ref, s1, s4, s16, d1, d4, d16, tmp):
        first = (lax.broadcasted_iota(jnp.int32, (tm, w), 1) & (DIL_HD - 1)) < DIL_HD // 2
        do_ = a_ref[...]
        _to_views(jnp.where(first, l_ref[...], _head_sum(do_ * b_ref[...], m_ref[...])), tmp, (s1, s4, s16))
        _to_views(do_, tmp, (d1, d4, d16))

    spec = pl.BlockSpec((tm, w), lambda i: (i, 0))
    f_specs, f_shapes = _view_specs(tm, t, F32)
    b_specs, b_shapes = _view_specs(tm, t, BF16)
    res = _pcall(body, name="dil_stats", grid=(t // tm,),
                 in_specs=[spec, spec, spec, pl.BlockSpec((w, w), lambda i: (0, 0))],
                 out_specs=f_specs + b_specs, out_shape=f_shapes + b_shapes,
                 scratch_shapes=[_view_scratch(tm)],
                 compiler_params=_cparams("parallel"))(do, o, lse, _head_sum_matrix(1.0))
    return res[:3], res[3:]


def _head_norm_fwd(x, col, g, name, tm):
    t = x.shape[0]
    w = DIL_WIDTH
    tm = min(tm, t)
    normed = g is not None

    def body(*refs):
        outs, tmp = refs[-4:-1], refs[-1]
        xf = refs[0][...]
        if normed:
            g_ref, m_ref = refs[1], refs[2]
            xf = xf * lax.rsqrt(_head_sum(xf * xf, m_ref[...]) + EPS) * g_ref[...]
        _to_views(xf, tmp, outs)

    specs, shapes = _view_specs(tm, t, BF16)
    extra = [g, _head_sum_matrix(1.0 / DIL_HD)] if normed else []
    extra_specs = [pl.BlockSpec((1, w), lambda i: (0, 0)), pl.BlockSpec((w, w), lambda i: (0, 0))] if normed else []
    return _pcall(
        body, name=name, grid=(t // tm,),
        in_specs=[pl.BlockSpec((tm, w), lambda i: (i, col))] + extra_specs,
        out_specs=specs, out_shape=shapes, scratch_shapes=[_view_scratch(tm)],
        compiler_params=_cparams("parallel"))(x, *extra)


def _head_norm_bwd(dys, x, col, g, name, tm):
    t = x.shape[0]
    w = DIL_WIDTH
    tm = min(tm, t)
    nd = len(dys)
    nt = t // tm
    lane = np.arange(w) % DIL_HD
    fold = jnp.asarray((lane[:, None] == lane[None, :]).astype(np.float32))

    def body(*refs):
        x_ref, g_ref, m_ref, f_ref = refs[nd:nd + 4]
        dx_ref, dg_ref, s1, s2 = refs[-4:]
        dy = refs[0][...] + _from_view(refs[1], s1, DIL_DILATIONS[1]) + _from_view(refs[2], s2, DIL_DILATIONS[2])
        xf = x_ref[...]
        mat = m_ref[...]
        r = lax.rsqrt(_head_sum(xf * xf, mat) + EPS)
        xh = xf * r
        dxh = dy * g_ref[...]
        dx_ref[...] = r * (dxh - xh * _head_sum(dxh * xh, mat))

        @pl.when(pl.program_id(0) == 0)
        def _():
            dg_ref[...] = jnp.zeros_like(dg_ref)

        dg_ref[...] += jnp.sum(dy * xh, axis=0, keepdims=True)

        @pl.when(pl.program_id(0) == nt - 1)
        def _():
            per_lane = jnp.broadcast_to(dg_ref[...], (8, w))
            dg_ref[...] = lax.dot_general(per_lane, f_ref[...], NN, precision=lax.Precision.HIGHEST,
                                          preferred_element_type=F32)[0:1]

    row = pl.BlockSpec((tm, w), lambda i: (i, 0))
    vec = pl.BlockSpec((1, w), lambda i: (0, 0))
    sq = pl.BlockSpec((w, w), lambda i: (0, 0))
    views, _ = _view_specs(tm, t, F32)
    return _pcall(
        body, name=name, grid=(nt,),
        in_specs=list(views) + [pl.BlockSpec((tm, w), lambda i: (i, col)), vec, sq, sq],
        out_specs=(row, vec), out_shape=(_sds((t, w), F32), _sds((1, w), F32)),
        scratch_shapes=[_view_scratch(tm)] * 2,
        compiler_params=_cparams("arbitrary"))(*dys, x, g, _head_sum_matrix(1.0 / DIL_HD), fold)


def _rowdot(a, b, name, tm):
    n, d = a.shape
    tm = min(tm, n)

    def body(a_ref, b_ref, o_ref):
        o_ref[...] = jnp.sum(a_ref[...].astype(F32) * b_ref[...].astype(F32), axis=-1, keepdims=True)

    spec = pl.BlockSpec((tm, d), lambda i: (i, 0))
    return _pcall(body, name=name, grid=(n // tm,), in_specs=[spec, spec],
                  out_specs=pl.BlockSpec((tm, 1), lambda i: (i, 0)), out_shape=_sds((n, 1), F32),
                  compiler_params=_cparams("parallel"))(a, b)


def _sum_branches(parts, name, tm):
    t = parts[0].shape[0]
    w = DIL_WIDTH
    tm = min(tm, t)

    def body(a_ref, b_ref, c_ref, o_ref, s1, s2):
        o_ref[...] = a_ref[...] + _from_view(b_ref, s1, DIL_DILATIONS[1]) + _from_view(c_ref, s2, DIL_DILATIONS[2])

    views, _ = _view_specs(tm, t, F32)
    return _pcall(body, name=name, grid=(t // tm,), in_specs=list(views),
                  out_specs=pl.BlockSpec((tm, w), lambda i: (i, 0)), out_shape=_sds((t, w), F32),
                  scratch_shapes=[_view_scratch(tm)] * 2,
                  compiler_params=_cparams("parallel"))(*parts)


def _rope_tables(t):
    inv = ROPE_BASE ** (-np.arange(0, MLA_ROPE, 2, dtype=np.float64) / MLA_ROPE)
    ang = np.arange(t, dtype=np.float64)[:, None] * inv[None, :]
    cos, sin = np.cos(ang), np.sin(ang)
    return (jnp.asarray(np.concatenate([cos, cos], 1), F32), jnp.asarray(np.concatenate([-sin, sin], 1), F32))


def _half_swap():
    p = np.zeros((MLA_ROPE, MLA_ROPE), np.float32)
    half = MLA_ROPE // 2
    for i in range(MLA_ROPE):
        p[(i + half) % MLA_ROPE, i] = 1.0
    return jnp.asarray(p)


def _mla_qk_fwd(x, g, cos_t, sin_t, scale, name, tm):
    n, d = x.shape
    t = cos_t.shape[0]
    tm = min(tm, t)
    nt = t // tm
    swap = _half_swap()

    def body(x_ref, g_ref, c_ref, s_ref, p_ref, o_ref):
        xf = x_ref[...]
        r = lax.rsqrt(jnp.mean(xf * xf, axis=-1, keepdims=True) + EPS)
        y = xf * r * g_ref[...]
        yr = y[:, MLA_NOPE:]
        sw = lax.dot_general(yr, p_ref[...], NN, precision=lax.Precision.HIGHEST, preferred_element_type=F32)
        o_ref[:, :MLA_NOPE] = (y[:, :MLA_NOPE] * scale).astype(o_ref.dtype)
        o_ref[:, MLA_NOPE:] = ((yr * c_ref[...] + sw * s_ref[...]) * scale).astype(o_ref.dtype)

    row = pl.BlockSpec((tm, d), lambda i: (i, 0))
    tab = pl.BlockSpec((tm, MLA_ROPE), lambda i: (i % nt, 0))
    return _pcall(
        body, name=name, grid=(n // tm,),
        in_specs=[row, pl.BlockSpec((1, d), lambda i: (0, 0)), tab, tab,
                  pl.BlockSpec((MLA_ROPE, MLA_ROPE), lambda i: (0, 0))],
        out_specs=row, out_shape=_sds((n, d), BF16),
        compiler_params=_cparams("parallel"))(x, g, cos_t, sin_t, swap)


def _mla_qk_bwd(dy, x, g, cos_t, sin_t, scale, name, tm):
    n, d = x.shape
    t = cos_t.shape[0]
    tm = min(tm, t)
    nt = t // tm
    swap_t = _half_swap().T

    def body(dy_ref, x_ref, g_ref, c_ref, s_ref, p_ref, dx_ref, dg_ref):
        xf = x_ref[...]
        gg = g_ref[...]
        r = lax.rsqrt(jnp.mean(xf * xf, axis=-1, keepdims=True) + EPS)
        xh = xf * r
        dyf = dy_ref[...] * scale
        dyr = dyf[:, MLA_NOPE:]
        back = lax.dot_general(dyr * s_ref[...], p_ref[...], NN, precision=lax.Precision.HIGHEST,
                               preferred_element_type=F32)
        dn_n = dyf[:, :MLA_NOPE]
        dn_r = dyr * c_ref[...] + back
        xh_n, xh_r = xh[:, :MLA_NOPE], xh[:, MLA_NOPE:]
        dxh_n = dn_n * gg[:, :MLA_NOPE]
        dxh_r = dn_r * gg[:, MLA_NOPE:]
        mean = (jnp.sum(dxh_n * xh_n, axis=-1, keepdims=True)
                + jnp.sum(dxh_r * xh_r, axis=-1, keepdims=True)) * (1.0 / d)
        dx_ref[:, :MLA_NOPE] = r * (dxh_n - xh_n * mean)
        dx_ref[:, MLA_NOPE:] = r * (dxh_r - xh_r * mean)

        @pl.when(pl.program_id(0) == 0)
        def _():
            dg_ref[...] = jnp.zeros_like(dg_ref)

        dg_ref[:, :MLA_NOPE] += jnp.sum(dn_n * xh_n, axis=0, keepdims=True)
        dg_ref[:, MLA_NOPE:] += jnp.sum(dn_r * xh_r, axis=0, keepdims=True)

    row = pl.BlockSpec((tm, d), lambda i: (i, 0))
    vec = pl.BlockSpec((1, d), lambda i: (0, 0))
    tab = pl.BlockSpec((tm, MLA_ROPE), lambda i: (i % nt, 0))
    return _pcall(
        body, name=name, grid=(n // tm,),
        in_specs=[row, row, vec, tab, tab, pl.BlockSpec((MLA_ROPE, MLA_ROPE), lambda i: (0, 0))],
        out_specs=(row, vec), out_shape=(_sds((n, d), F32), _sds((1, d), F32)),
        compiler_params=_cparams("arbitrary"))(dy, x, g, cos_t, sin_t, swap_t)


def _causal_mask(i, j, tq, tk, width):
    row = i * tq + lax.broadcasted_iota(jnp.int32, (tq, width), 0)
    col = j * tk + lax.broadcasted_iota(jnp.int32, (tq, width), 1)
    return col <= row


def _causal_steps(nq, nk, tq, tk, q_major):
    if q_major:
        groups = [[(i, j) for j in range((i * tq + tq - 1) // tk + 1)] for i in range(nq)]
        nunit = tk // tq if tk % tq == 0 else 1
    else:
        groups = [[(i, j) for i in range((j * tk) // tq, nq)] for j in range(nk)]
        nunit = tq // tk if tq % tk == 0 else 1
    it, jt, fl = [], [], []
    for g in groups:
        for n, (i, j) in enumerate(g):
            crossing = j * tk + tk - 1 > i * tq
            if q_major:
                unit = tk // nunit
                u = min(nunit, -(-(i * tq + tq - j * tk) // unit)) - 1
            else:
                unit = tq // nunit
                u = max(0, j * tk - i * tq) // unit
            it.append(i)
            jt.append(j)
            fl.append((n == 0) + 2 * (n == len(g) - 1) + 4 * crossing + 8 * (u if crossing else 0))
    return tuple(jnp.asarray(np.array(a, np.int32)) for a in (it, jt, fl)), nunit


def _by_crossing(flags, nunit, update):
    pl.when((flags & 4) == 0)(functools.partial(update, None))
    for u in range(nunit):
        pl.when(((flags & 4) != 0) & ((flags >> 3) == u))(functools.partial(update, u))


def _causal_specs(tq, tk):
    def qs(w):
        return pl.BlockSpec((None, tq, w), lambda h, s, it, jt, fl: (h, it[s], 0))

    def kv(w):
        return pl.BlockSpec((None, tk, w), lambda h, s, it, jt, fl: (h, jt[s], 0))

    return qs, kv


def _mla_fwd(q, k, v, tq, tk):
    nh, t, dq = q.shape
    dv = v.shape[2]
    tq, tk = min(tq, t), min(tk, t)
    tables, nunit = _causal_steps(t // tq, t // tk, tq, tk, True)

    def body(it, jt, fl, q_ref, k_ref, v_ref, o_ref, lse_ref, m_sc, l_sc, acc_sc):
        step = pl.program_id(1)
        i, j, flags = it[step], jt[step], fl[step]

        @pl.when((flags & 1) != 0)
        def _():
            m_sc[...] = jnp.full_like(m_sc, NEG)
            l_sc[...] = jnp.zeros_like(l_sc)
            acc_sc[...] = jnp.zeros_like(acc_sc)

        def update(units):
            wk = tk if units is None else (units + 1) * (tk // nunit)
            s = _dot(q_ref[...], k_ref[:wk, :], NT)
            if units is not None:
                s = jnp.where(_causal_mask(i, j, tq, tk, wk), s, NEG)
            m_prev = m_sc[...]
            m_new = jnp.maximum(m_prev, jnp.max(s, axis=-1, keepdims=True))
            alpha = jnp.exp(m_prev - m_new)
            p = jnp.exp(s - m_new)
            l_sc[...] = alpha * l_sc[...] + jnp.sum(p, axis=-1, keepdims=True)
            acc_sc[...] = alpha * acc_sc[...] + _dot(p.astype(BF16), v_ref[:wk, :], NN)
            m_sc[...] = m_new

        _by_crossing(flags, nunit, update)

        @pl.when((flags & 2) != 0)
        def _():
            o_ref[...] = acc_sc[...] / l_sc[...]
            lse_ref[...] = m_sc[...] + jnp.log(l_sc[...])

    qs, kv = _causal_specs(tq, tk)
    return _pcall(
        body, name="mla_attn_fwd",
        grid_spec=pltpu.PrefetchScalarGridSpec(
            num_scalar_prefetch=3, grid=(nh, tables[0].shape[0]),
            in_specs=[qs(dq), kv(dq), kv(dv)], out_specs=(qs(dv), qs(1)),
            scratch_shapes=[pltpu.VMEM((tq, 1), F32), pltpu.VMEM((tq, 1), F32), pltpu.VMEM((tq, dv), F32)]),
        out_shape=(_sds((nh, t, dv), F32), _sds((nh, t, 1), F32)),
        compiler_params=_cparams("parallel", "arbitrary"))(*tables, q, k, v)


def _mla_bwd(q, k, k_t, v, do, lse_row, dl_row, tq, tk):
    nh, t, dq = q.shape
    dv = v.shape[2]
    tq, tk = min(tq, t), min(tk, t)
    nq = t // tq
    tables, nunit = _causal_steps(nq, t // tk, tq, tk, False)

    def body(it, jt, fl, q_ref, k_ref, kt_ref, v_ref, do_ref, lse_ref, dl_ref, dk_ref, dv_ref, dq_ref, dk_sc, dv_sc):
        step = pl.program_id(1)
        i, j, flags = it[step], jt[step], fl[step]

        def update(units):
            off = 0 if units is None else units * (tq // nunit)
            qq = q_ref[off:, :]
            st = _dot(k_ref[...], qq, NT)
            if units is not None:
                key = j * tk + lax.broadcasted_iota(jnp.int32, (tk, tq - off), 0)
                qry = i * tq + off + lax.broadcasted_iota(jnp.int32, (tk, tq - off), 1)
                st = jnp.where(key <= qry, st, NEG)
            pt = jnp.exp(st - lse_ref[:, off:])
            dob = do_ref[off:, :].astype(BF16)
            dpt = _dot(v_ref[...], dob, NT)
            dst = pt * (dpt - dl_ref[:, off:])
            dsb = dst.astype(BF16)
            dv_part = _dot(pt.astype(BF16), dob, NN)
            dk_part = _dot(dsb, qq, NN)
            dq_part = _dot(kt_ref[...], dsb, NN)

            @pl.when((flags & 1) != 0)
            def _():
                dv_sc[...] = dv_part
                dk_sc[...] = dk_part

            @pl.when((flags & 1) == 0)
            def _():
                dv_sc[...] += dv_part
                dk_sc[...] += dk_part

            if off == 0:
                @pl.when(j == 0)
                def _():
                    dq_ref[i] = dq_part

                @pl.when(j != 0)
                def _():
                    dq_ref[i] += dq_part
            else:
                dq_ref[i, :, off:] += dq_part

        _by_crossing(flags, nunit, update)

        @pl.when((flags & 2) != 0)
        def _():
            dk_ref[...] = dk_sc[...]
            dv_ref[...] = dv_sc[...]

    qs, kv = _causal_specs(tq, tk)
    rowv = pl.BlockSpec((None, 1, tq), lambda h, s, it, jt, fl: (h, 0, it[s]))
    ktv = pl.BlockSpec((None, dq, tk), lambda h, s, it, jt, fl: (h, 0, jt[s]))
    whole = pl.BlockSpec((None, nq, dq, tq), lambda h, s, it, jt, fl: (h, 0, 0, 0))
    return _pcall(
        body, name="mla_attn_bwd",
        grid_spec=pltpu.PrefetchScalarGridSpec(
            num_scalar_prefetch=3, grid=(nh, tables[0].shape[0]),
            in_specs=[qs(dq), kv(dq), ktv, kv(dv), qs(dv), rowv, rowv], out_specs=(kv(dq), kv(dv), whole),
            scratch_shapes=[pltpu.VMEM((tk, dq), F32), pltpu.VMEM((tk, dv), F32)]),
        out_shape=(_sds((nh, t, dq), F32), _sds((nh, t, dv), F32), _sds((nh, nq, dq, tq), F32)),
        compiler_params=_cparams("parallel", "arbitrary"))(*tables, q, k, k_t, v, do, lse_row, dl_row)


def _loss_head(y, target, tm):
    t, d = y.shape
    tm = min(tm, t)
    nt = t // tm

    def body(y_ref, t_ref, dy_ref, loss_ref, acc):
        i = pl.program_id(0)
        err = y_ref[...] - t_ref[...]
        dy_ref[...] = err * (1.0 / d)

        @pl.when(i == 0)
        def _():
            acc[...] = jnp.zeros_like(acc)

        acc[...] += jnp.sum(err * err, axis=0, keepdims=True)

        @pl.when(i == nt - 1)
        def _():
            loss_ref[0, 0] = jnp.sum(acc[...]) * (0.5 / d)

    spec = pl.BlockSpec((tm, d), lambda i: (i, 0))
    return _pcall(
        body, name="loss_head", grid=(nt,), in_specs=[spec, spec],
        out_specs=(spec, pl.BlockSpec(memory_space=pltpu.SMEM)),
        out_shape=(_sds((t, d), F32), _sds((1, 1), F32)),
        scratch_shapes=[pltpu.VMEM((1, d), F32)],
        compiler_params=_cparams("arbitrary"))(y, target)


def _adamw(w, g, m, v, name):
    r, c = w.shape
    tr = r
    for cand in (256, 128, 64, 32, 16, 8):
        if r % cand == 0:
            tr = cand
            break

    def body(w_ref, g_ref, m_ref, v_ref, d_ref, nm_ref, nv_ref):
        gg = g_ref[...]
        nm = ADAM_B1 * m_ref[...] + (1.0 - ADAM_B1) * gg
        nv = ADAM_B2 * v_ref[...] + (1.0 - ADAM_B2) * (gg * gg)
        m_hat = nm / (1.0 - ADAM_B1 ** ADAM_STEP)
        v_hat = nv / (1.0 - ADAM_B2 ** ADAM_STEP)
        d_ref[...] = -ADAM_LR * (m_hat / (jnp.sqrt(v_hat) + ADAM_EPS) + ADAM_WD * w_ref[...])
        nm_ref[...] = nm
        nv_ref[...] = nv

    spec = pl.BlockSpec((tr, c), lambda i: (i, 0))
    sd = _sds((r, c), F32)
    return _pcall(body, name=name, grid=(r // tr,), in_specs=[spec] * 4, out_specs=(spec,) * 3,
                  out_shape=(sd, sd, sd), compiler_params=_cparams("parallel"))(w, g, m, v)


MESH_ID = pl.DeviceIdType.MESH
HBM_SPEC = pl.BlockSpec(memory_space=pltpu.HBM)


def _place():
    return lax.axis_index("x"), lax.axis_index("y"), lax.axis_index("c")


def _other_chips(x, y):
    return [(1 - x, y), (x, 1 - y), (1 - x, 1 - y)]


def _remote(src, dst, send_sems, recv_sems, k, to):
    return pltpu.make_async_remote_copy(src_ref=src, dst_ref=dst, send_sem=send_sems.at[k], recv_sem=recv_sems.at[k],
                                        device_id=to, device_id_type=MESH_ID)


def _halves(arrays):
    for a in arrays:
        assert a.shape[-2] % 32 == 0
    return [a.shape[-2] // 2 for a in arrays]


def _gather_weights(blocks):
    n = len(blocks)
    halves = _halves(blocks)

    def body(*refs):
        srcs, outs, send_sems, recv_sems = refs[:n], refs[n:2 * n], refs[2 * n], refs[2 * n + 1]
        x, y, c = _place()
        me = 2 * x + y
        sibling = (x, y, 1 - c)
        chips = _other_chips(x, y)

        def part(a, chip, core):
            return outs[a].at[chip, pl.ds(core * halves[a], halves[a]), :]

        for a in range(n):
            mine = srcs[a].at[pl.ds(c * halves[a], halves[a]), :]
            for k, (cx, cy) in enumerate(chips):
                _remote(mine, part(a, me, c), send_sems, recv_sems, 6 * a + k, (cx, cy, c)).start()
        for k, (cx, cy) in enumerate(chips):
            for a in range(n):
                got = part(a, 2 * cx + cy, c)
                _remote(got, got, send_sems, recv_sems, 6 * a + k, (x, y, c)).wait_recv()
                _remote(got, got, send_sems, recv_sems, 6 * a + 3 + k, sibling).start()
        for k, (cx, cy) in enumerate(chips):
            for a in range(n):
                got = part(a, 2 * cx + cy, 1 - c)
                _remote(got, got, send_sems, recv_sems, 6 * a + 3 + k, (x, y, c)).wait_recv()
        for a in range(n):
            sent = part(a, me, c)
            for k in range(6):
                _remote(sent, sent, send_sems, recv_sems, 6 * a + k, (x, y, c)).wait_send()

    return _pcall(
        body, name="gather_weights", in_specs=[HBM_SPEC] * n, out_specs=tuple([HBM_SPEC] * n),
        out_shape=tuple(_sds((N_CHIPS,) + b.shape, b.dtype) for b in blocks),
        scratch_shapes=[pltpu.SemaphoreType.DMA((6 * n,)), pltpu.SemaphoreType.DMA((6 * n,))],
    )(*blocks)


def _reduce_cores(grads):
    n = len(grads)
    halves = _halves(grads)

    def body(*refs):
        gs, outs, send_sems, recv_sems = refs[:n], refs[n:2 * n], refs[2 * n], refs[2 * n + 1]
        x, y, c = _place()
        for a in range(n):
            for j in range(N_CHIPS):
                _remote(gs[a].at[j, pl.ds((1 - c) * halves[a], halves[a]), :], outs[a].at[j],
                        send_sems, recv_sems, a, (x, y, 1 - c)).start()
        for a in range(n):
            _remote(gs[a].at[:, pl.ds((1 - c) * halves[a], halves[a]), :], outs[a],
                    send_sems, recv_sems, a, (x, y, c)).wait()

    return _pcall(
        body, name="reduce_cores", in_specs=[HBM_SPEC] * n, out_specs=tuple([HBM_SPEC] * n),
        out_shape=tuple(_sds((N_CHIPS, h, g.shape[2]), g.dtype) for g, h in zip(grads, halves)),
        scratch_shapes=[pltpu.SemaphoreType.DMA((n,)), pltpu.SemaphoreType.DMA((n,))],
    )(*grads)


def _scatter_chips(parts):
    n = len(parts)

    def body(*refs):
        ps, outs, send_sems, recv_sems = refs[:n], refs[n:2 * n], refs[2 * n], refs[2 * n + 1]
        x, y, c = _place()
        for a in range(n):
            for k, (cx, cy) in enumerate(_other_chips(x, y)):
                _remote(ps[a].at[2 * cx + cy], outs[a].at[k], send_sems, recv_sems, 3 * a + k, (cx, cy, c)).start()
        for a in range(n):
            for k in range(3):
                _remote(ps[a].at[k], outs[a].at[k], send_sems, recv_sems, 3 * a + k, (x, y, c)).wait()

    return _pcall(
        body, name="scatter_chips", in_specs=[HBM_SPEC] * n, out_specs=tuple([HBM_SPEC] * n),
        out_shape=tuple(_sds((3,) + p.shape[1:], p.dtype) for p in parts),
        scratch_shapes=[pltpu.SemaphoreType.DMA((3 * n,)), pltpu.SemaphoreType.DMA((3 * n,))],
    )(*parts)


def _sum_partials(received, parts, place):
    n = len(parts)
    steps = 2
    tiles = [p.shape[1] // steps for p in parts]

    def body(place_ref, *refs):
        rs, ps, outs = refs[:n], refs[n:2 * n], refs[2 * n:]
        for a in range(n):
            tot = ps[a][...].astype(F32)
            for k in range(3):
                tot = tot + rs[a][k].astype(F32)
            outs[a][...] = tot

    cols = [p.shape[2] for p in parts]
    return _pcall(
        body, name="sum_chip_partials",
        grid_spec=pltpu.PrefetchScalarGridSpec(
            num_scalar_prefetch=1, grid=(steps,),
            in_specs=[pl.BlockSpec((3, tm, w), lambda i, pc: (0, i, 0)) for tm, w in zip(tiles, cols)]
            + [pl.BlockSpec((None, tm, w), lambda i, pc: (pc[0], i, 0)) for tm, w in zip(tiles, cols)],
            out_specs=tuple(pl.BlockSpec((tm, w), lambda i, pc: (pc[1] * steps + i, 0)) for tm, w in zip(tiles, cols))),
        out_shape=tuple(_sds((2 * p.shape[1], p.shape[2]), F32) for p in parts),
        compiler_params=_cparams("parallel"))(place, *received, *parts)


def _share_cores(blocks):
    n = len(blocks)
    halves = _halves(blocks)

    def body(*refs):
        srcs, outs, send_sems, recv_sems = refs[:n], refs[n:2 * n], refs[2 * n], refs[2 * n + 1]
        x, y, c = _place()
        for a in range(n):
            piece = pl.ds(c * halves[a], halves[a])
            _remote(srcs[a].at[piece, :], outs[a].at[piece, :], send_sems, recv_sems, a, (x, y, 1 - c)).start()
        for a in range(n):
            mine = outs[a].at[pl.ds(c * halves[a], halves[a]), :]
            theirs = outs[a].at[pl.ds((1 - c) * halves[a], halves[a]), :]
            _remote(mine, theirs, send_sems, recv_sems, a, (x, y, c)).wait()

    return _pcall(
        body, name="share_cores", in_specs=[HBM_SPEC] * n, out_specs=tuple([HBM_SPEC] * n),
        out_shape=tuple(_sds(b.shape, b.dtype) for b in blocks), input_output_aliases={a: a for a in range(n)},
        scratch_shapes=[pltpu.SemaphoreType.DMA((n,)), pltpu.SemaphoreType.DMA((n,))],
    )(*blocks)


def _sum_blocks(stacked, name, tm):
    n, rows, lanes = stacked.shape
    tm = min(tm, rows)

    def body(s_ref, o_ref):
        tot = s_ref[n - 1].astype(F32)
        for k in range(n - 1):
            tot = tot + s_ref[k].astype(F32)
        o_ref[...] = tot

    return _pcall(body, name=name, grid=(rows // tm,),
                  in_specs=[pl.BlockSpec((n, tm, lanes), lambda i: (0, i, 0))],
                  out_specs=pl.BlockSpec((tm, lanes), lambda i: (i, 0)), out_shape=_sds((rows, lanes), F32),
                  compiler_params=_cparams("parallel"))(stacked)


def _add_halves(grads, theirs, core):
    n = len(grads)
    steps = 2
    tiles = [t.shape[1] // steps for t in theirs]
    cols = [t.shape[2] for t in theirs]

    def body(c_ref, *refs):
        gs, ts, outs = refs[:n], refs[n:2 * n], refs[2 * n:]
        for a in range(n):
            outs[a][...] = (gs[a][...] + ts[a][...]).astype(BF16)

    own = [pl.BlockSpec((None, tm, w), lambda k, i, c: (k, c[0] * steps + i, 0)) for tm, w in zip(tiles, cols)]
    same = [pl.BlockSpec((None, tm, w), lambda k, i, c: (k, i, 0)) for tm, w in zip(tiles, cols)]
    return _pcall(
        body, name="add_core_halves",
        grid_spec=pltpu.PrefetchScalarGridSpec(
            num_scalar_prefetch=1, grid=(N_CHIPS, steps), in_specs=own + same, out_specs=tuple(same)),
        out_shape=tuple(_sds(t.shape, BF16) for t in theirs),
        compiler_params=_cparams("parallel", "parallel"))(core, *grads, *theirs)


def _allreduce_small(part):
    rows, lanes = part.shape
    ndev = 8

    def body(src, tot, buf, send_sems, recv_sems):
        x, y, c = _place()
        me = 4 * x + 2 * y + c
        buf[me] = src[...]
        sends = []
        for k in range(1, ndev):
            peer = (x ^ (k >> 2), y ^ ((k >> 1) & 1), c ^ (k & 1))
            cp = _remote(src, buf.at[me], send_sems, recv_sems, k - 1, peer)
            cp.start()
            sends.append(cp)
        for k in range(1, ndev):
            theirs = buf.at[me ^ k]
            _remote(theirs, theirs, send_sems, recv_sems, k - 1, (x, y, c)).wait_recv()
        for cp in sends:
            cp.wait_send()
        acc = buf[0]
        for d in range(1, ndev):
            acc = acc + buf[d]
        tot[...] = acc

    vm = pl.BlockSpec(memory_space=pltpu.VMEM)
    return _pcall(
        body, name="allreduce_small", in_specs=[vm], out_specs=vm, out_shape=_sds((rows, lanes), F32),
        scratch_shapes=[pltpu.VMEM((ndev, rows, lanes), F32), pltpu.SemaphoreType.DMA((ndev - 1,)),
                        pltpu.SemaphoreType.DMA((ndev - 1,))],
    )(part)


def _pack_small(vals):
    parts = []
    for name, shape, r in SMALL:
        flat = vals[name].reshape(-1).astype(F32)
        parts.append(jnp.pad(flat, (0, r * LANES - flat.shape[0])).reshape(r, LANES))
    used = sum(r for _, _, r in SMALL)
    parts.append(jnp.zeros((SMALL_ROWS - used, LANES), F32))
    return jnp.concatenate(parts, axis=0)


def _unpack_small(packed):
    out, off = {}, 0
    for name, shape, r in SMALL:
        n = int(np.prod(shape))
        out[name] = packed[off:off + r].reshape(-1)[:n].reshape(shape)
        off += r
    return out


def _heads_major(a, nh):
    t = a.shape[0]
    return a.reshape(t, nh, a.shape[1] // nh).transpose(1, 0, 2)


def _tokens_major(a):
    nh, t, w = a.shape
    return a.transpose(1, 0, 2).reshape(t, nh * w)


def _local_step(x, target, small, wfull):
    t = x.shape[0]
    nh, hd = DIL_HEADS, DIL_HD
    w_in = wfull["w_in"].transpose(1, 0, 2).reshape(D_MODEL, -1)
    w_out = wfull["w_out"].reshape(D_MODEL, D_MODEL)
    w_qb, w_kvb = wfull["mla_w_q_b"], wfull["mla_w_kv_b"]
    grads_s, grads_b = {}, {}

    x1, ffn1_saved = _ffn_fwd(x, small["ffn1_norm"], wfull["ffn1_w_gate"], wfull["ffn1_w_up"],
                              wfull["ffn1_w_down"], "ffn1")
    hm = _rms_fwd(x1, small["mix_norm"], BF16, "mix_norm", 512)
    proj = _mm_simple("in_proj", hm, w_in, NN, F32, tm=1024)
    cq, ckv, k_pe = proj[:, 1536:1792], proj[:, 1792:1920], proj[:, 1920:1984]

    gq, gk = jnp.tile(small["dil_q_norm"], (1, nh)), jnp.tile(small["dil_k_norm"], (1, nh))
    qn = _head_norm_fwd(proj, 0, gq, "dil_q_norm", 512)
    kn = _head_norm_fwd(proj, 1, gk, "dil_k_norm", 512)
    v_d = _head_norm_fwd(proj, 2, None, "dil_v_views", 512)
    bias = _bias_tiles(small["rel_bias"]).reshape(3, nh // 2, 2 * QB, QB + DIL_W)
    outs, lses = [], []
    for b, dil in enumerate(DIL_DILATIONS):
        o_b, lse_b = _dil_fwd(qn[b], kn[b], v_d[b], bias[b], dil, f"dil_fwd_{dil}")
        outs.append(o_b)
        lses.append(lse_b)
    o_dil, lse_tot, od = _dil_merge(outs, lses, small["out_norm_dil"], 512)

    mh = MLA_HEADS
    cos_t, sin_t = _rope_tables(t)
    cqn = _rms_fwd(cq, small["mla_q_a_norm"], BF16, "mla_q_a_norm", 512)
    ckvn = _rms_fwd(ckv, small["mla_kv_a_norm"], BF16, "mla_kv_a_norm", 512)
    tm = min(512, t)

    th = min(2048, t)

    def head_proj(name, a, w, width):
        k = a.shape[1]
        return _mm(name, (mh, t // th, 1),
                   [(a, pl.BlockSpec((th, k), lambda h, i, r: (i, 0)), w, pl.BlockSpec((None, k, width), lambda h, i, r: (h, 0, 0)))],
                   NN, _sds((mh, t, width), F32), pl.BlockSpec((None, th, width), lambda h, i, r: (h, i, 0)), (th, width))

    q_raw = head_proj("mla_q_proj", cqn, w_qb, MLA_QK)
    kv_raw = head_proj("mla_kv_proj", ckvn, w_kvb, MLA_NOPE + MLA_V)
    k_raw = jnp.concatenate([kv_raw[:, :, :MLA_NOPE], jnp.broadcast_to(k_pe[None], (mh, t, MLA_ROPE))], axis=2)
    v_m = kv_raw[:, :, MLA_NOPE:].astype(BF16)
    q_raw2, k_raw2 = q_raw.reshape(mh * t, MLA_QK), k_raw.reshape(mh * t, MLA_QK)
    q_scale = MLA_QK ** -0.5
    q_m = _mla_qk_fwd(q_raw2, small["mla_q_norm"], cos_t, sin_t, q_scale, "mla_q_rope", 2048).reshape(mh, t, MLA_QK)
    k_m = _mla_qk_fwd(k_raw2, small["mla_k_norm"], cos_t, sin_t, 1.0, "mla_k_rope", 2048).reshape(mh, t, MLA_QK)
    o_mla_h, lse_m = _mla_fwd(q_m, k_m, v_m, 512, 2048)
    o_mla = _tokens_major(o_mla_h)

    om = _rms_fwd(o_mla, small["out_norm_mla"], BF16, "out_norm_mla", 512)
    half_w = DIL_WIDTH
    row = pl.BlockSpec((tm, D_MODEL), lambda i, j, r: (i, 0))
    act_spec = pl.BlockSpec((tm, half_w), lambda i, j, r: (i, 0))
    x2 = _mm("out_proj", (t // tm, 1, 1),
             [(od, act_spec, w_out, pl.BlockSpec((half_w, D_MODEL), lambda i, j, r: (0, 0))),
              (om, act_spec, w_out, pl.BlockSpec((half_w, D_MODEL), lambda i, j, r: (1, 0)))],
             NN, _sds((t, D_MODEL), F32), row, (tm, D_MODEL), res=(x1, row))
    x3, ffn2_saved = _ffn_fwd(x2, small["ffn2_norm"], wfull["ffn2_w_gate"], wfull["ffn2_w_up"],
                              wfull["ffn2_w_down"], "ffn2")
    dy, loss = _loss_head(x3, target, 512)

    dx2, grads_s["ffn2_norm"], grads_b["ffn2_w_gate"], grads_b["ffn2_w_up"], grads_b["ffn2_w_down"] = _ffn_bwd(
        dy, x2, small["ffn2_norm"], wfull["ffn2_w_gate"], wfull["ffn2_w_up"], wfull["ffn2_w_down"], ffn2_saved, "ffn2")

    d_ocat = _mm_simple("out_proj_dx", dx2, w_out, NT, F32, tm=1024)
    dw_out_d = _mm_simple("out_proj_dw_dil", od, dx2, TN, F32, tk=2048)
    dw_out_m = _mm_simple("out_proj_dw_mla", om, dx2, TN, F32, tk=2048)
    grads_b["w_out"] = jnp.concatenate([dw_out_d, dw_out_m], axis=0).reshape(N_CHIPS, D_MODEL // N_CHIPS, D_MODEL)
    do_dil, grads_s["out_norm_dil"] = _rms_bwd([d_ocat[:, :half_w]], o_dil, small["out_norm_dil"], None, "out_norm_dil_bwd", 512)
    do_mla, grads_s["out_norm_mla"] = _rms_bwd([d_ocat[:, half_w:]], o_mla, small["out_norm_mla"], None, "out_norm_mla_bwd", 512)

    do_m = _heads_major(do_mla, mh)
    dl_m = _rowdot(do_m.reshape(mh * t, MLA_V), o_mla_h.reshape(mh * t, MLA_V), "mla_delta", 2048).reshape(mh, t, 1)
    dk_m, dv_m, dq_t = _mla_bwd(q_m, k_m, k_m.transpose(0, 2, 1), v_m, do_m, lse_m.reshape(mh, 1, t),
                                dl_m.reshape(mh, 1, t), 2048, 512)
    dq_m = dq_t.transpose(0, 1, 3, 2).reshape(mh, t, MLA_QK)
    dq_raw, grads_s["mla_q_norm"] = _mla_qk_bwd(dq_m.reshape(mh * t, MLA_QK), q_raw2, small["mla_q_norm"],
                                                 cos_t, sin_t, q_scale, "mla_q_rope_bwd", 2048)
    dk_raw, grads_s["mla_k_norm"] = _mla_qk_bwd(dk_m.reshape(mh * t, MLA_QK), k_raw2, small["mla_k_norm"],
                                                 cos_t, sin_t, 1.0, "mla_k_rope_bwd", 2048)
    dq_raw = dq_raw.reshape(mh, t, MLA_QK)
    dk_raw = dk_raw.reshape(mh, t, MLA_QK)
    dkv_raw = jnp.concatenate([dk_raw[:, :, :MLA_NOPE], dv_m], axis=2)
    dk_pe_h = dk_raw[:, :, MLA_NOPE:]

    def head_proj_dx(name, d, w):
        width, k = d.shape[2], w.shape[1]
        pairs = [(d, pl.BlockSpec((None, th, width), lambda i, j, r, h=h: (h, i, 0)),
                  w, pl.BlockSpec((None, k, width), lambda i, j, r, h=h: (h, 0, 0))) for h in range(mh)]
        return _mm(name, (t // th, 1, 1), pairs, NT, _sds((t, k), F32),
                   pl.BlockSpec((th, k), lambda i, j, r: (i, 0)), (th, k))

    def head_proj_dw(name, a, d):
        width, k = d.shape[2], a.shape[1]
        return _mm(name, (mh, 1, t // th),
                   [(a, pl.BlockSpec((th, k), lambda h, j, r: (r, 0)), d, pl.BlockSpec((None, th, width), lambda h, j, r: (h, r, 0)))],
                   TN, _sds((mh, k, width), F32), pl.BlockSpec((None, k, width), lambda h, j, r: (h, 0, 0)), (k, width))

    d_cqn = head_proj_dx("mla_q_proj_dx", dq_raw, w_qb)
    d_ckvn = head_proj_dx("mla_kv_proj_dx", dkv_raw, w_kvb)
    grads_b["mla_w_q_b"] = head_proj_dw("mla_q_proj_dw", cqn, dq_raw)
    grads_b["mla_w_kv_b"] = head_proj_dw("mla_kv_proj_dw", ckvn, dkv_raw)
    d_cq, grads_s["mla_q_a_norm"] = _rms_bwd([d_cqn], cq, small["mla_q_a_norm"], None, "mla_q_a_norm_bwd", 512)
    d_ckv, grads_s["mla_kv_a_norm"] = _rms_bwd([d_ckvn], ckv, small["mla_kv_a_norm"], None, "mla_kv_a_norm_bwd", 512)
    d_kpe = _sum_blocks(dk_pe_h.reshape(mh, t * MLA_ROPE // LANES, LANES), "mla_kpe_sum", 1024).reshape(t, MLA_ROPE)

    stats, do_db = _dil_stats(do_dil, o_dil, lse_tot, 512)
    dqs, dks, dvs, dtiles = [], [], [], []
    for b, dil in enumerate(DIL_DILATIONS):
        dq_b, dk_b, dv_b, db_b = _dil_bwd(qn[b], kn[b], v_d[b], do_db[b], stats[b], bias[b], dil, f"dil_bwd_{dil}")
        dqs.append(dq_b)
        dks.append(dk_b)
        dvs.append(dv_b)
        dtiles.append(db_b)
    grads_s["rel_bias"] = _bias_grad(jnp.stack(dtiles).reshape(3, nh, QB, QB + DIL_W))
    dq_a, dgq = _head_norm_bwd(dqs, proj, 0, gq, "dil_q_norm_bwd", 512)
    dk_a, dgk = _head_norm_bwd(dks, proj, 1, gk, "dil_k_norm_bwd", 512)
    grads_s["dil_q_norm"], grads_s["dil_k_norm"] = dgq[:, :hd], dgk[:, :hd]
    dv_a = _sum_branches(dvs, "dil_dv_sum", 512)

    dparts = [dq_a, dk_a, dv_a, d_cq, d_ckv, d_kpe]
    t2 = min(1024, t)
    pairs, dw_parts, lo = [], [], 0
    for n, dpart in enumerate(dparts):
        width = dpart.shape[1]
        w_part = w_in[:, lo:lo + width]
        pairs.append((dpart, pl.BlockSpec((t2, width), lambda i, j, r: (i, 0)),
                      w_part, pl.BlockSpec((D_MODEL, width), lambda i, j, r: (0, 0))))
        dw_parts.append(_mm_simple(f"in_proj_dw_{n}", hm, dpart, TN, F32, tk=2048))
        lo += width
    d_hm = _mm("in_proj_dx", (t // t2, 1, 1), pairs, NT, _sds((t, D_MODEL), F32),
               pl.BlockSpec((t2, D_MODEL), lambda i, j, r: (i, 0)), (t2, D_MODEL))
    dw_in = jnp.concatenate(dw_parts, axis=1)
    grads_b["w_in"] = dw_in.reshape(D_MODEL, N_CHIPS, -1).transpose(1, 0, 2)
    dx1, grads_s["mix_norm"] = _rms_bwd([d_hm], x1, small["mix_norm"], dx2, "mix_norm_bwd", 512)
    dx, grads_s["ffn1_norm"], grads_b["ffn1_w_gate"], grads_b["ffn1_w_up"], grads_b["ffn1_w_down"] = _ffn_bwd(
        dx1, x, small["ffn1_norm"], wfull["ffn1_w_gate"], wfull["ffn1_w_up"], wfull["ffn1_w_down"], ffn1_saved, "ffn1")
    return loss, dx, grads_s, grads_b


def kernel(x, ffn1_norm, ffn1_w_gate, ffn1_w_up, ffn1_w_down, mix_norm, w_in, dil_q_norm, dil_k_norm, rel_bias, mla_q_a_norm, mla_w_q_b, mla_kv_a_norm, mla_w_kv_b, mla_q_norm, mla_k_norm, out_norm_dil, out_norm_mla, w_out, ffn2_norm, ffn2_w_gate, ffn2_w_up, ffn2_w_down, loss_target, m_ffn1_norm, m_ffn1_w_gate, m_ffn1_w_up, m_ffn1_w_down, m_mix_norm, m_w_in, m_dil_q_norm, m_dil_k_norm, m_rel_bias, m_mla_q_a_norm, m_mla_w_q_b, m_mla_kv_a_norm, m_mla_w_kv_b, m_mla_q_norm, m_mla_k_norm, m_out_norm_dil, m_out_norm_mla, m_w_out, m_ffn2_norm, m_ffn2_w_gate, m_ffn2_w_up, m_ffn2_w_down, v_ffn1_norm, v_ffn1_w_gate, v_ffn1_w_up, v_ffn1_w_down, v_mix_norm, v_w_in, v_dil_q_norm, v_dil_k_norm, v_rel_bias, v_mla_q_a_norm, v_mla_w_q_b, v_mla_kv_a_norm, v_mla_w_kv_b, v_mla_q_norm, v_mla_k_norm, v_out_norm_dil, v_out_norm_mla, v_w_out, v_ffn2_norm, v_ffn2_w_gate, v_ffn2_w_up, v_ffn2_w_down):
    given = dict(locals())
    big_names = [name for name, _ in BIG]
    small_names = [name for name, _, _ in SMALL]

    chip = (2 * lax.axis_index("x") + lax.axis_index("y")).astype(jnp.int32)
    core = lax.axis_index("c").astype(jnp.int32)
    mine = [given[n].astype(BF16) for n in big_names]
    gathered = _gather_weights([m[0] for m in mine])
    wfull = {n: lax.dynamic_update_slice(g, m, (chip, 0, 0)) for n, g, m in zip(big_names, gathered, mine)}
    small = {n: given[n] for n in small_names}

    loss, dx, grads_s, grads_b = _local_step(x[0], loss_target[0], small, wfull)
    loss = lax.psum(loss[0, 0], ("x", "y", "c"))

    partial = [grads_b[n] for n in big_names]
    chip_part = _add_halves(partial, _reduce_cores(partial), core.reshape(1))
    reduced = _sum_partials(_scatter_chips(chip_part), chip_part, jnp.stack([chip, core]))
    g_big = dict(zip(big_names, _share_cores(reduced)))
    g_small = _unpack_small(_allreduce_small(_pack_small(grads_s)))

    grad, delta, new_m, new_v = {}, {}, {}, {}
    for name, shape in BIG:
        g2 = g_big[name]
        d_, m_, v_ = _adamw(given[name].reshape(shape), g2, given["m_" + name].reshape(shape),
                            given["v_" + name].reshape(shape), f"adamw_{name}")
        full = given[name].shape
        grad[name], delta[name], new_m[name], new_v[name] = (a.reshape(full) for a in (g2, d_, m_, v_))
    ps = {k: _pack_small({n: given[pre + n] for n in small_names}) for k, pre in (("w", ""), ("m", "m_"), ("v", "v_"))}
    gs_packed = _pack_small(g_small)
    d_s, m_s, v_s = (_unpack_small(a) for a in _adamw(ps["w"], gs_packed, ps["m"], ps["v"], "adamw_small"))
    for name in small_names:
        grad[name], delta[name], new_m[name], new_v[name] = g_small[name], d_s[name], m_s[name], v_s[name]

    return (loss, dx[None], *[grad[n] for n in WEIGHTS], *[delta[n] for n in WEIGHTS],
            *[new_m[n] for n in WEIGHTS], *[new_v[n] for n in WEIGHTS])
```

```python
import functools

import numpy as np
import jax
import jax.numpy as jnp
from jax import lax
from jax.experimental import pallas as pl
from jax.experimental.pallas import tpu as pltpu

F32 = jnp.float32
BF16 = jnp.bfloat16

D_MODEL = 1024
D_FF = 2816
N_CHIPS = 4
DIL_HEADS = 8
DIL_HD = 64
DIL_WIDTH = 512
DIL_DILATIONS = (1, 4, 16)
DIL_W = 128
QB = 128
MLA_HEADS = 4
MLA_NOPE = 128
MLA_ROPE = 64
MLA_QK = 192
MLA_V = 128
MLA_Q_RANK = 256
MLA_KV_RANK = 128
ROPE_BASE = 10000.0
REL_BUCKETS = 32
REL_MAX_DIST = 2048
FFN_RESID = 0.5
EPS = 1e-6
NEG = -1e30
LANES = 128

ADAM_LR = 0.001
ADAM_B1 = 0.9
ADAM_B2 = 0.999
ADAM_EPS = 1e-08
ADAM_WD = 0.01
ADAM_STEP = 10

NT = (((1,), (1,)), ((), ()))
NN = (((1,), (0,)), ((), ()))
TN = (((0,), (0,)), ((), ()))

BIG = (
    ("ffn1_w_gate", (D_MODEL, D_FF // N_CHIPS)),
    ("ffn1_w_up", (D_MODEL, D_FF // N_CHIPS)),
    ("ffn1_w_down", (D_FF // N_CHIPS, D_MODEL)),
    ("w_in", (D_MODEL, 1984 // N_CHIPS)),
    ("mla_w_q_b", (MLA_Q_RANK, MLA_QK)),
    ("mla_w_kv_b", (MLA_KV_RANK, MLA_NOPE + MLA_V)),
    ("w_out", (D_MODEL // N_CHIPS, D_MODEL)),
    ("ffn2_w_gate", (D_MODEL, D_FF // N_CHIPS)),
    ("ffn2_w_up", (D_MODEL, D_FF // N_CHIPS)),
    ("ffn2_w_down", (D_FF // N_CHIPS, D_MODEL)),
)
SMALL = (
    ("ffn1_norm", (1, 1024), 8), ("mix_norm", (1, 1024), 8), ("dil_q_norm", (1, 64), 1),
    ("dil_k_norm", (1, 64), 1), ("rel_bias", (8, 32), 2), ("mla_q_a_norm", (1, 256), 2),
    ("mla_kv_a_norm", (1, 128), 1), ("mla_q_norm", (1, 192), 2), ("mla_k_norm", (1, 192), 2),
    ("out_norm_dil", (1, 512), 4), ("out_norm_mla", (1, 512), 4), ("ffn2_norm", (1, 1024), 8),
)
SMALL_ROWS = 48
WEIGHTS = ("ffn1_norm", "ffn1_w_gate", "ffn1_w_up", "ffn1_w_down", "mix_norm", "w_in", "dil_q_norm",
           "dil_k_norm", "rel_bias", "mla_q_a_norm", "mla_w_q_b", "mla_kv_a_norm", "mla_w_kv_b",
           "mla_q_norm", "mla_k_norm", "out_norm_dil", "out_norm_mla", "w_out", "ffn2_norm",
           "ffn2_w_gate", "ffn2_w_up", "ffn2_w_down")


def _pcall(body, **kw):
    return pl.pallas_call(body, **kw)


def _cparams(*sem):
    return pltpu.CompilerParams(dimension_semantics=sem)


def _sds(shape, dtype):
    return jax.ShapeDtypeStruct(shape, dtype)


def _dot(a, b, dn):
    return lax.dot_general(a, b, dn, preferred_element_type=F32)


def _rms_fwd(x, g, out_dtype, name, tm):
    n, d = x.shape
    tm = min(tm, n)

    def body(x_ref, g_ref, o_ref):
        xf = x_ref[...].astype(F32)
        r = lax.rsqrt(jnp.mean(xf * xf, axis=-1, keepdims=True) + EPS)
        o_ref[...] = (xf * r * g_ref[...]).astype(o_ref.dtype)

    return _pcall(
        body, name=name, grid=(n // tm,),
        in_specs=[pl.BlockSpec((tm, d), lambda i: (i, 0)), pl.BlockSpec((1, d), lambda i: (0, 0))],
        out_specs=pl.BlockSpec((tm, d), lambda i: (i, 0)),
        out_shape=_sds((n, d), out_dtype), compiler_params=_cparams("parallel"))(x, g)


def _rms_bwd(dys, x, g, res, name, tm):
    n, d = x.shape
    tm = min(tm, n)
    nd = len(dys)
    has_res = res is not None

    def body(*refs):
        dy_refs = refs[:nd]
        x_ref, g_ref = refs[nd], refs[nd + 1]
        res_ref = refs[nd + 2] if has_res else None
        dx_ref, dg_ref = refs[-2], refs[-1]
        dy = dy_refs[0][...].astype(F32)
        for r_ in dy_refs[1:]:
            dy = dy + r_[...].astype(F32)
        xf = x_ref[...].astype(F32)
        r = lax.rsqrt(jnp.mean(xf * xf, axis=-1, keepdims=True) + EPS)
        xh = xf * r
        dxh = dy * g_ref[...]
        dx = r * (dxh - xh * jnp.mean(dxh * xh, axis=-1, keepdims=True))
        if has_res:
            dx = dx + res_ref[...]
        dx_ref[...] = dx

        @pl.when(pl.program_id(0) == 0)
        def _():
            dg_ref[...] = jnp.zeros_like(dg_ref)

        dg_ref[...] += jnp.sum(dy * xh, axis=0, keepdims=True)

    row = pl.BlockSpec((tm, d), lambda i: (i, 0))
    vec = pl.BlockSpec((1, d), lambda i: (0, 0))
    ins = list(dys) + [x, g] + ([res] if has_res else [])
    return _pcall(
        body, name=name, grid=(n // tm,),
        in_specs=[row] * nd + [row, vec] + ([row] if has_res else []),
        out_specs=(row, vec),
        out_shape=(_sds((n, d), F32), _sds((1, d), F32)),
        compiler_params=_cparams("arbitrary"))(*ins)


def _mm(name, grid, pairs, dn, out_shape, out_spec, acc_shape, res=None, scale=1.0):
    npairs = len(pairs)
    nred = grid[2]
    has_res = res is not None

    def body(*refs):
        ab = refs[:2 * npairs]
        res_ref = refs[2 * npairs] if has_res else None
        o_ref = refs[2 * npairs + int(has_res)]
        acc_ref = refs[-1] if nred > 1 else None
        tot = None
        for p in range(npairs):
            d = _dot(ab[2 * p][...].astype(BF16), ab[2 * p + 1][...].astype(BF16), dn)
            tot = d if tot is None else tot + d

        def finish(v):
            if scale != 1.0:
                v = v * scale
            if has_res:
                v = res_ref[...] + v
            o_ref[...] = v.astype(o_ref.dtype)

        if nred == 1:
            finish(tot)
        else:
            r = pl.program_id(2)

            @pl.when(r == 0)
            def _():
                acc_ref[...] = tot

            @pl.when(r > 0)
            def _():
                acc_ref[...] += tot

            @pl.when(r == nred - 1)
            def _():
                finish(acc_ref[...])

    ins, specs = [], []
    for a, a_spec, b, b_spec in pairs:
        ins += [a, b]
        specs += [a_spec, b_spec]
    if has_res:
        ins.append(res[0])
        specs.append(res[1])
    return _pcall(
        body, name=name, grid=grid, in_specs=specs, out_specs=out_spec, out_shape=out_shape,
        scratch_shapes=[pltpu.VMEM(acc_shape, F32)] if nred > 1 else [],
        compiler_params=_cparams("parallel", "parallel", "arbitrary"))(*ins)


def _ffn_up(h, wg, wu, name, tm):
    t, d = h.shape
    nc, _, fs = wg.shape
    tm = min(tm, t)

    def body(h_ref, wg_ref, wu_ref, g_ref, u_ref, a_ref):
        hh = h_ref[...]
        gate = _dot(hh, wg_ref[...], NN)
        up = _dot(hh, wu_ref[...], NN)
        sig = jax.nn.sigmoid(gate)
        silu = gate * sig
        g_ref[...] = (up * (sig + silu * (1.0 - sig))).astype(BF16)
        u_ref[...] = silu.astype(BF16)
        a_ref[...] = (silu * up).astype(BF16)

    wspec = pl.BlockSpec((None, d, fs), lambda c, i: (c, 0, 0))
    ospec = pl.BlockSpec((None, tm, fs), lambda c, i: (c, i, 0))
    osd = _sds((nc, t, fs), BF16)
    return _pcall(
        body, name=name, grid=(nc, t // tm),
        in_specs=[pl.BlockSpec((tm, d), lambda c, i: (i, 0)), wspec, wspec],
        out_specs=(ospec, ospec, ospec), out_shape=(osd, osd, osd),
        compiler_params=_cparams("parallel", "parallel"))(h, wg, wu)


def _ffn_hidden_bwd(dy, h, wd, dact_dgate, dact_dup, act, name, tm, outgoing=()):
    t, d = dy.shape
    nc, fs, _ = wd.shape
    tm = min(tm, t)
    nt = t // tm
    no = len(outgoing)

    def body(*refs):
        dy_ref, h_ref, wd_ref, g_ref, u_ref, a_ref = refs[:6]
        sent = refs[6:6 + no]
        dg_ref, du_ref, dwg_hbm, dwu_hbm, dwd_hbm = refs[6 + no:11 + no]
        arrived = refs[11 + no:11 + 2 * no]
        wg_acc, wu_acc, wd_acc, sem = refs[11 + 2 * no:15 + 2 * no]
        c, i = pl.program_id(0), pl.program_id(1)
        if no:
            send_sems, recv_sems = refs[15 + 2 * no:]

            @pl.when((c == 0) & (i == 0))
            def _():
                _scatter_start(sent, arrived, send_sems, recv_sems)

        dyb = dy_ref[...].astype(BF16)
        da = _dot(dyb, wd_ref[...], NT) * FFN_RESID
        dgate = (da * g_ref[...].astype(F32)).astype(BF16)
        dup = (da * u_ref[...].astype(F32)).astype(BF16)
        dg_ref[...] = dgate
        du_ref[...] = dup
        hh = h_ref[...]
        parts = (_dot(hh, dgate, TN), _dot(hh, dup, TN), _dot(a_ref[...], dyb, TN) * FFN_RESID)
        accs = (wg_acc, wu_acc, wd_acc)

        @pl.when(i == 0)
        def _():
            for acc, part in zip(accs, parts):
                acc[...] = part

        @pl.when(i > 0)
        def _():
            for acc, part in zip(accs, parts):
                acc[...] += part

        @pl.when(i == nt - 1)
        def _():
            copies = [pltpu.make_async_copy(acc, out.at[c], sem.at[n])
                      for n, (acc, out) in enumerate(zip(accs, (dwg_hbm, dwu_hbm, dwd_hbm)))]
            for cp in copies:
                cp.start()
            for cp in copies:
                cp.wait()

        if no:
            @pl.when((c == nc - 1) & (i == nt - 1))
            def _():
                _scatter_wait(sent, arrived, send_sems, recv_sems)

    tok = pl.BlockSpec((tm, d), lambda c, i: (i, 0))
    cspec = pl.BlockSpec((None, tm, fs), lambda c, i: (c, i, 0))
    hbm = pl.BlockSpec(memory_space=pltpu.HBM)
    osd = _sds((nc, t, fs), BF16)
    res = _pcall(
        body, name=name, grid=(nc, nt),
        in_specs=[tok, tok, pl.BlockSpec((None, fs, d), lambda c, i: (c, 0, 0)), cspec, cspec, cspec] + [hbm] * no,
        out_specs=(cspec, cspec, hbm, hbm, hbm) + (hbm,) * no,
        out_shape=(osd, osd, _sds((nc, d, fs), F32), _sds((nc, d, fs), F32), _sds((nc, fs, d), F32))
        + _scatter_shapes(outgoing),
        scratch_shapes=[pltpu.VMEM((d, fs), F32), pltpu.VMEM((d, fs), F32), pltpu.VMEM((fs, d), F32),
                        pltpu.SemaphoreType.DMA((3,))] + (_scatter_sems(no) if no else []),
        compiler_params=_cparams("arbitrary", "arbitrary"))(dy, h, wd, dact_dgate, dact_dup, act, *outgoing)
    res = tuple(res)
    return res[:5] + (res[5:],)


def _ffn_fwd(x, g, wg, wu, wd, tag):
    t = x.shape[0]
    nc, _, fs = wg.shape
    tm = min(512, t)
    h = _rms_fwd(x, g, BF16, f"{tag}_norm", 512)
    dact_dgate, dact_dup, act = _ffn_up(h, wg, wu, f"{tag}_up", 1024)
    pairs = [(act, pl.BlockSpec((None, tm, fs), lambda i, j, r, c=c: (c, i, 0)),
              wd, pl.BlockSpec((None, fs, D_MODEL), lambda i, j, r, c=c: (c, 0, 0))) for c in range(nc)]
    row = pl.BlockSpec((tm, D_MODEL), lambda i, j, r: (i, 0))
    y = _mm(f"{tag}_down", (t // tm, 1, 1), pairs, NN, _sds((t, D_MODEL), F32), row, (tm, D_MODEL),
            res=(x, row), scale=FFN_RESID)
    return y, (h, dact_dgate, dact_dup, act)


def _ffn_bwd(dy, x, g, wg, wu, wd, saved, tag, outgoing=()):
    h, dact_dgate, dact_dup, act = saved
    t = x.shape[0]
    nc, _, fs = wg.shape
    tm = min(512, t)
    dgate, dup, dwg, dwu, dwd, arrived = _ffn_hidden_bwd(dy, h, wd, dact_dgate, dact_dup, act,
                                                         f"{tag}_hidden_bwd", 1024, outgoing)
    pairs = []
    for c in range(nc):
        a_spec = pl.BlockSpec((None, tm, fs), lambda i, j, r, c=c: (c, i, 0))
        w_spec = pl.BlockSpec((None, D_MODEL, fs), lambda i, j, r, c=c: (c, 0, 0))
        pairs += [(dgate, a_spec, wg, w_spec), (dup, a_spec, wu, w_spec)]
    dh = _mm(f"{tag}_dh", (t // tm, 1, 1), pairs, NT,
             _sds((t, D_MODEL), F32), pl.BlockSpec((tm, D_MODEL), lambda i, j, r: (i, 0)), (tm, D_MODEL))
    dx, dg = _rms_bwd([dh], x, g, dy, f"{tag}_dnorm", 512)
    return dx, dg, dwg, dwu, dwd, arrived


def _mm_simple(name, a, b, dn, out_dtype, tm=512, tk=512, res=None, scale=1.0):
    if dn == TN:
        k, m = a.shape
        n = b.shape[1]
        tk = min(tk, k)
        return _mm(name, (1, 1, k // tk),
                   [(a, pl.BlockSpec((tk, m), lambda i, j, r: (r, 0)), b, pl.BlockSpec((tk, n), lambda i, j, r: (r, 0)))],
                   TN, _sds((m, n), out_dtype), pl.BlockSpec((m, n), lambda i, j, r: (0, 0)), (m, n), scale=scale)
    m, k = a.shape
    n = b.shape[1] if dn == NN else b.shape[0]
    tm = min(tm, m)
    row = pl.BlockSpec((tm, n), lambda i, j, r: (i, 0))
    return _mm(name, (m // tm, 1, 1),
               [(a, pl.BlockSpec((tm, k), lambda i, j, r: (i, 0)), b, pl.BlockSpec(b.shape, lambda i, j, r: (0, 0)))],
               dn, _sds((m, n), out_dtype), row, (tm, n), res=None if res is None else (res, row), scale=scale)


def _t5_bucket(dist):
    max_exact = REL_BUCKETS // 2
    d = np.maximum(dist, 1).astype(np.float32)
    large = max_exact + (np.log(d / max_exact) / np.log(REL_MAX_DIST / max_exact)
                         * (REL_BUCKETS - max_exact)).astype(np.int32)
    large = np.minimum(large, REL_BUCKETS - 1)
    return np.where(dist < max_exact, dist, large).astype(np.int32)


def _bucket_tiles():
    i = np.arange(QB)[:, None]
    j = np.arange(QB + DIL_W)[None, :]
    delta = np.clip(i + DIL_W - j, 0, None)
    return np.stack([_t5_bucket(delta * dil) for dil in DIL_DILATIONS]).astype(np.int32)


def _bias_tiles(rel_bias):
    buckets = jnp.asarray(_bucket_tiles())

    def body(rb_ref, bk_ref, o_ref):
        bk = bk_ref[...]
        for h in range(DIL_HEADS):
            def pick(b, tile):
                return jnp.where(bk == b, rb_ref[h, b], tile)

            o_ref[h] = lax.fori_loop(0, REL_BUCKETS, pick, jnp.zeros((QB, QB + DIL_W), F32))

    return _pcall(
        body, name="dil_bias_tiles", grid=(3,),
        in_specs=[pl.BlockSpec(memory_space=pltpu.SMEM),
                  pl.BlockSpec((None, QB, QB + DIL_W), lambda b: (b, 0, 0))],
        out_specs=pl.BlockSpec((None, DIL_HEADS, QB, QB + DIL_W), lambda b: (b, 0, 0, 0)),
        out_shape=_sds((3, DIL_HEADS, QB, QB + DIL_W), F32),
        compiler_params=_cparams("parallel"))(rel_bias, buckets)


def _bias_grad(dtiles):
    buckets = jnp.asarray(_bucket_tiles())

    def body(dt_ref, bk_ref, o_ref):
        def one(b, carry):
            hit = [bk_ref[br] == b for br in range(3)]
            for h in range(DIL_HEADS):
                tot = jnp.zeros((), F32)
                for br in range(3):
                    tot = tot + jnp.sum(jnp.where(hit[br], dt_ref[br, h], 0.0))
                o_ref[h, b] = tot
            return carry

        lax.fori_loop(0, REL_BUCKETS, one, 0)

    return _pcall(
        body, name="dil_bias_grad",
        in_specs=[pl.BlockSpec(memory_space=pltpu.VMEM), pl.BlockSpec(memory_space=pltpu.VMEM)],
        out_specs=pl.BlockSpec(memory_space=pltpu.SMEM),
        out_shape=_sds((DIL_HEADS, REL_BUCKETS), F32))(dtiles, buckets)


def _split_heads(a, lo):
    zero = jnp.zeros_like(a)
    return jnp.concatenate([jnp.where(lo, a, zero), jnp.where(lo, zero, a)], axis=0)


def _side_by_side(a):
    n = a.shape[0] // 2
    return jnp.concatenate([a[:n], a[n:]], axis=1)


def _band_masks(prev_ok):
    ii = lax.broadcasted_iota(jnp.int32, (2 * QB, QB), 0) & (QB - 1)
    jj = lax.broadcasted_iota(jnp.int32, (2 * QB, QB), 1)
    return jj <= ii, jj >= ii + jnp.where(prev_ok, 0, QB)


def _dil_fwd(q, k, v, bias, dil, name):
    w = DIL_WIDTH
    t = q.shape[0] * dil
    npair = w // LANES
    nl = t // dil // QB
    scale = DIL_HD ** -0.5

    def body(q_ref, kc_ref, kp_ref, vc_ref, vp_ref, b_ref, o_ref, lse_ref):
        nn = pl.program_id(1)
        lo = lax.broadcasted_iota(jnp.int32, (QB, LANES), 1) < DIL_HD
        lo2 = lax.broadcasted_iota(jnp.int32, (2 * QB, LANES), 1) < DIL_HD
        ii = lax.broadcasted_iota(jnp.int32, (2 * QB, 2 * QB), 0) & (QB - 1)
        jj = lax.broadcasted_iota(jnp.int32, (2 * QB, 2 * QB), 1)
        first_key = jnp.maximum(ii, jnp.where(nn != 0, 0, QB))
        valid = (jj >= first_key) & (jj <= ii + QB)
        for p in range(npair):
            cols = slice(p * LANES, (p + 1) * LANES)
            qq = _split_heads(q_ref[:, cols], lo)
            kk = jnp.concatenate([kp_ref[:, cols], kc_ref[:, cols]], axis=0)
            vv = jnp.concatenate([vp_ref[:, cols], vc_ref[:, cols]], axis=0)
            s = jnp.where(valid, _dot(qq, kk, NT) * scale + b_ref[p], NEG)
            m = jnp.max(s, axis=-1, keepdims=True)
            e = jnp.exp(s - m)
            den = jnp.sum(e, axis=-1, keepdims=True)
            pn = (e * (1.0 / den)).astype(BF16)
            o_ref[:, cols] = _dot(_side_by_side(pn), _split_heads(vv, lo2), NN)
            lse = m + jnp.log(den)
            lse_ref[:, cols] = jnp.where(lo, lse[:QB], lse[QB:])

    cur = pl.BlockSpec((QB, w), lambda r, n: (n, r))
    prev = pl.BlockSpec((QB, w), lambda r, n: (jnp.maximum(n - 1, 0), r))
    sd = _sds((t // dil, dil * w), F32)
    return _pcall(
        body, name=name, grid=(dil, nl),
        in_specs=[cur, cur, prev, cur, prev, pl.BlockSpec((npair, 2 * QB, 2 * QB), lambda r, n: (0, 0, 0))],
        out_specs=(cur, cur), out_shape=(sd, sd),
        compiler_params=_cparams("parallel", "parallel"))(q, k, k, v, v, bias)


def _dil_bwd(q, k, v, do, stats, bias, dil, name):
    w = DIL_WIDTH
    t = q.shape[0] * dil
    npair = w // LANES
    nl = t // dil // QB
    scale = DIL_HD ** -0.5

    def body(qc_ref, qn_ref, doc_ref, don_ref, sc_ref, sn_ref, k_ref, v_ref, b_ref,
             dq_ref, dk_ref, dv_ref, db_ref, carry):
        r, nn = pl.program_id(0), pl.program_id(1)
        lo = lax.broadcasted_iota(jnp.int32, (QB, LANES), 1) < DIL_HD
        cur_ok, prev_ok = _band_masks(nn + 1 < nl)

        @pl.when((r == 0) & (nn == 0))
        def _():
            db_ref[...] = jnp.zeros_like(db_ref)
            carry[...] = jnp.zeros_like(carry)

        for p in range(npair):
            cols = slice(p * LANES, (p + 1) * LANES)
            kp, vp = k_ref[:, cols], v_ref[:, cols]
            k2 = _split_heads(kp, lo)

            def column(ref, lane):
                first = p * LANES + lane
                return jnp.concatenate([ref[:, first:first + 1], ref[:, first + DIL_HD:first + DIL_HD + 1]], axis=0)

            def side(q_ref, do_ref, s_ref, bias, ok):
                qq = _split_heads(q_ref[:, cols], lo)
                dd = _split_heads(do_ref[:, cols], lo)
                s = jnp.where(ok, _dot(qq, kp, NT) * scale + bias, NEG)
                prob = jnp.exp(s - column(s_ref, 0))
                ds = prob * (_dot(dd, vp, NT) - column(s_ref, DIL_HD // 2))
                return qq, dd, prob.astype(BF16), ds

            q1, d1, p1, ds1 = side(qc_ref, doc_ref, sc_ref, b_ref[p, :, QB:], cur_ok)
            q2, d2, p2, ds2 = side(qn_ref, don_ref, sn_ref, b_ref[p, :, :QB], prev_ok)
            ds1b, ds2b = ds1.astype(BF16), ds2.astype(BF16)
            dq_ref[:, cols] = carry[:, cols] + _dot(_side_by_side(ds1b), k2, NN) * scale
            carry[:, cols] = _dot(_side_by_side(ds2b), k2, NN) * scale
            dk_ref[:, cols] = _dot(jnp.concatenate([ds1b, ds2b], axis=0), jnp.concatenate([q1, q2], axis=0), TN) * scale
            dv_ref[:, cols] = _dot(jnp.concatenate([p1, p2], axis=0), jnp.concatenate([d1, d2], axis=0), TN)
            db_ref[p, :, QB:] += ds1
            db_ref[p, :, :QB] += ds2

    cur = pl.BlockSpec((QB, w), lambda r, n: (n, r))
    nxt = pl.BlockSpec((QB, w), lambda r, n: (jnp.minimum(n + 1, nl - 1), r))
    tile = pl.BlockSpec((npair, 2 * QB, 2 * QB), lambda r, n: (0, 0, 0))
    sd = _sds((t // dil, dil * w), F32)
    return _pcall(
        body, name=name, grid=(dil, nl),
        in_specs=[cur, nxt, cur, nxt, cur, nxt, cur, cur, tile],
        out_specs=(cur, cur, cur, tile),
        out_shape=(sd, sd, sd, _sds((npair, 2 * QB, 2 * QB), F32)),
        scratch_shapes=[pltpu.VMEM((QB, w), F32)],
        compiler_params=_cparams("arbitrary", "arbitrary"))(q, q, do, do, stats, stats, k, v, bias)


def _head_sum_matrix(scale):
    idx = np.arange(DIL_WIDTH) // DIL_HD
    return jnp.asarray((idx[:, None] == idx[None, :]).astype(np.float32) * scale, BF16)


def _head_sum(x, mat):
    hi = x.astype(BF16)
    lo = (x - hi.astype(F32)).astype(BF16)
    return _dot(hi, mat, NN) + _dot(lo, mat, NN)


def _to_views(src, tmp, out_refs):
    tm, w = src.shape
    for j in range(w // LANES):
        tmp[j] = src[:, j * LANES:(j + 1) * LANES]
    for d, o_ref in zip(DIL_DILATIONS, out_refs):
        if d == 1:
            o_ref[...] = src.astype(o_ref.dtype)
            continue
        for r in range(d):
            for j in range(w // LANES):
                lo = r * w + j * LANES
                o_ref[:, lo:lo + LANES] = tmp[j, pl.ds(r, tm // d, stride=d), :].astype(o_ref.dtype)


def _from_view(v_ref, tmp, d):
    tm = tmp.shape[1]
    w = v_ref.shape[1] // d
    for r in range(d):
        for j in range(w // LANES):
            lo = r * w + j * LANES
            tmp[j, pl.ds(r, tm // d, stride=d), :] = v_ref[:, lo:lo + LANES]
    return jnp.concatenate([tmp[j] for j in range(w // LANES)], axis=1)


def _view_specs(tm, t, dtype):
    specs = tuple(pl.BlockSpec((tm // d, d * DIL_WIDTH), lambda i: (i, 0)) for d in DIL_DILATIONS)
    shapes = tuple(_sds((t // d, d * DIL_WIDTH), dtype) for d in DIL_DILATIONS)
    return specs, shapes


def _view_scratch(tm):
    return pltpu.VMEM((DIL_WIDTH // LANES, tm, LANES), F32)


def _dil_merge(outs, lses, g, tm):
    w = DIL_WIDTH
    t = outs[0].shape[0]
    tm = min(tm, t)

    def body(o0, o1, o2, l0, l1, l2, g_ref, o_ref, l_ref, n_ref, so1, so2, sl1, sl2):
        d1, d2 = DIL_DILATIONS[1], DIL_DILATIONS[2]
        a0, a1, a2 = l0[...], _from_view(l1, sl1, d1), _from_view(l2, sl2, d2)
        m = jnp.maximum(jnp.maximum(a0, a1), a2)
        e0, e1, e2 = jnp.exp(a0 - m), jnp.exp(a1 - m), jnp.exp(a2 - m)
        den = e0 + e1 + e2
        o = (e0 * o0[...] + e1 * _from_view(o1, so1, d1) + e2 * _from_view(o2, so2, d2)) / den
        o_ref[...] = o
        l_ref[...] = m + jnp.log(den)
        r = lax.rsqrt(jnp.mean(o * o, axis=-1, keepdims=True) + EPS)
        n_ref[...] = (o * r * g_ref[...]).astype(n_ref.dtype)

    specs, _ = _view_specs(tm, t, F32)
    spec = pl.BlockSpec((tm, w), lambda i: (i, 0))
    return _pcall(
        body, name="dil_merge", grid=(t // tm,),
        in_specs=list(specs) * 2 + [pl.BlockSpec((1, w), lambda i: (0, 0))], out_specs=(spec, spec, spec),
        out_shape=(_sds((t, w), F32), _sds((t, w), F32), _sds((t, w), BF16)),
        scratch_shapes=[_view_scratch(tm)] * 4,
        compiler_params=_cparams("parallel"))(*outs, *lses, g)


def _dil_stats(do, o, lse, tm):
    t, w = do.shape
    tm = min(tm, t)

    def body(a_ref, b_ref, l_ref, m_ref, s1, s4, s16, d1, d4, d16, tmp):
        first = (lax.broadcasted_iota(jnp.int32, (tm, w), 1) & (DIL_HD - 1)) < DIL_HD // 2
        do_ = a_ref[...]
        _to_views(jnp.where(first, l_ref[...], _head_sum(do_ * b_ref[...], m_ref[...])), tmp, (s1, s4, s16))
        _to_views(do_, tmp, (d1, d4, d16))

    spec = pl.BlockSpec((tm, w), lambda i: (i, 0))
    f_specs, f_shapes = _view_specs(tm, t, F32)
    b_specs, b_shapes = _view_specs(tm, t, BF16)
    res = _pcall(body, name="dil_stats", grid=(t // tm,),
                 in_specs=[spec, spec, spec, pl.BlockSpec((w, w), lambda i: (0, 0))],
                 out_specs=f_specs + b_specs, out_shape=f_shapes + b_shapes,
                 scratch_shapes=[_view_scratch(tm)],
                 compiler_params=_cparams("parallel"))(do, o, lse, _head_sum_matrix(1.0))
    return res[:3], res[3:]


def _head_norm_fwd(x, col, g, name, tm):
    t = x.shape[0]
    w = DIL_WIDTH
    tm = min(tm, t)
    normed = g is not None

    def body(*refs):
        outs, tmp = refs[-4:-1], refs[-1]
        xf = refs[0][...]
        if normed:
            g_ref, m_ref = refs[1], refs[2]
            xf = xf * lax.rsqrt(_head_sum(xf * xf, m_ref[...]) + EPS) * g_ref[...]
        _to_views(xf, tmp, outs)

    specs, shapes = _view_specs(tm, t, BF16)
    extra = [g, _head_sum_matrix(1.0 / DIL_HD)] if normed else []
    extra_specs = [pl.BlockSpec((1, w), lambda i: (0, 0)), pl.BlockSpec((w, w), lambda i: (0, 0))] if normed else []
    return _pcall(
        body, name=name, grid=(t // tm,),
        in_specs=[pl.BlockSpec((tm, w), lambda i: (i, col))] + extra_specs,
        out_specs=specs, out_shape=shapes, scratch_shapes=[_view_scratch(tm)],
        compiler_params=_cparams("parallel"))(x, *extra)


def _head_norm_bwd(dys, x, col, g, name, tm):
    t = x.shape[0]
    w = DIL_WIDTH
    tm = min(tm, t)
    nd = len(dys)
    nt = t // tm
    lane = np.arange(w) % DIL_HD
    fold = jnp.asarray((lane[:, None] == lane[None, :]).astype(np.float32))

    def body(*refs):
        x_ref, g_ref, m_ref, f_ref = refs[nd:nd + 4]
        dx_ref, dg_ref, s1, s2 = refs[-4:]
        dy = refs[0][...] + _from_view(refs[1], s1, DIL_DILATIONS[1]) + _from_view(refs[2], s2, DIL_DILATIONS[2])
        xf = x_ref[...]
        mat = m_ref[...]
        r = lax.rsqrt(_head_sum(xf * xf, mat) + EPS)
        xh = xf * r
        dxh = dy * g_ref[...]
        dx_ref[...] = r * (dxh - xh * _head_sum(dxh * xh, mat))

        @pl.when(pl.program_id(0) == 0)
        def _():
            dg_ref[...] = jnp.zeros_like(dg_ref)

        dg_ref[...] += jnp.sum(dy * xh, axis=0, keepdims=True)

        @pl.when(pl.program_id(0) == nt - 1)
        def _():
            per_lane = jnp.broadcast_to(dg_ref[...], (8, w))
            dg_ref[...] = lax.dot_general(per_lane, f_ref[...], NN, precision=lax.Precision.HIGHEST,
                                          preferred_element_type=F32)[0:1]

    row = pl.BlockSpec((tm, w), lambda i: (i, 0))
    vec = pl.BlockSpec((1, w), lambda i: (0, 0))
    sq = pl.BlockSpec((w, w), lambda i: (0, 0))
    views, _ = _view_specs(tm, t, F32)
    return _pcall(
        body, name=name, grid=(nt,),
        in_specs=list(views) + [pl.BlockSpec((tm, w), lambda i: (i, col)), vec, sq, sq],
        out_specs=(row, vec), out_shape=(_sds((t, w), F32), _sds((1, w), F32)),
        scratch_shapes=[_view_scratch(tm)] * 2,
        compiler_params=_cparams("arbitrary"))(*dys, x, g, _head_sum_matrix(1.0 / DIL_HD), fold)


def _rowdot(a, b, name, tm):
    n, d = a.shape
    tm = min(tm, n)

    def body(a_ref, b_ref, o_ref):
        o_ref[...] = jnp.sum(a_ref[...].astype(F32) * b_ref[...].astype(F32), axis=-1, keepdims=True)

    spec = pl.BlockSpec((tm, d), lambda i: (i, 0))
    return _pcall(body, name=name, grid=(n // tm,), in_specs=[spec, spec],
                  out_specs=pl.BlockSpec((tm, 1), lambda i: (i, 0)), out_shape=_sds((n, 1), F32),
                  compiler_params=_cparams("parallel"))(a, b)


def _sum_branches(parts, name, tm):
    t = parts[0].shape[0]
    w = DIL_WIDTH
    tm = min(tm, t)

    def body(a_ref, b_ref, c_ref, o_ref, s1, s2):
        o_ref[...] = a_ref[...] + _from_view(b_ref, s1, DIL_DILATIONS[1]) + _from_view(c_ref, s2, DIL_DILATIONS[2])

    views, _ = _view_specs(tm, t, F32)
    return _pcall(body, name=name, grid=(t // tm,), in_specs=list(views),
                  out_specs=pl.BlockSpec((tm, w), lambda i: (i, 0)), out_shape=_sds((t, w), F32),
                  scratch_shapes=[_view_scratch(tm)] * 2,
                  compiler_params=_cparams("parallel"))(*parts)


def _rope_tables(t):
    inv = ROPE_BASE ** (-np.arange(0, MLA_ROPE, 2, dtype=np.float64) / MLA_ROPE)
    ang = np.arange(t, dtype=np.float64)[:, None] * inv[None, :]
    cos, sin = np.cos(ang), np.sin(ang)
    return (jnp.asarray(np.concatenate([cos, cos], 1), F32), jnp.asarray(np.concatenate([-sin, sin], 1), F32))


def _half_swap():
    p = np.zeros((MLA_ROPE, MLA_ROPE), np.float32)
    half = MLA_ROPE // 2
    for i in range(MLA_ROPE):
        p[(i + half) % MLA_ROPE, i] = 1.0
    return jnp.asarray(p)


def _mla_qk_fwd(x, g, cos_t, sin_t, scale, name, tm):
    n, d = x.shape
    t = cos_t.shape[0]
    tm = min(tm, t)
    nt = t // tm
    swap = _half_swap()

    def body(x_ref, g_ref, c_ref, s_ref, p_ref, o_ref):
        xf = x_ref[...]
        r = lax.rsqrt(jnp.mean(xf * xf, axis=-1, keepdims=True) + EPS)
        y = xf * r * g_ref[...]
        yr = y[:, MLA_NOPE:]
        sw = lax.dot_general(yr, p_ref[...], NN, precision=lax.Precision.HIGHEST, preferred_element_type=F32)
        o_ref[:, :MLA_NOPE] = (y[:, :MLA_NOPE] * scale).astype(o_ref.dtype)
        o_ref[:, MLA_NOPE:] = ((yr * c_ref[...] + sw * s_ref[...]) * scale).astype(o_ref.dtype)

    row = pl.BlockSpec((tm, d), lambda i: (i, 0))
    tab = pl.BlockSpec((tm, MLA_ROPE), lambda i: (i % nt, 0))
    return _pcall(
        body, name=name, grid=(n // tm,),
        in_specs=[row, pl.BlockSpec((1, d), lambda i: (0, 0)), tab, tab,
                  pl.BlockSpec((MLA_ROPE, MLA_ROPE), lambda i: (0, 0))],
        out_specs=row, out_shape=_sds((n, d), BF16),
        compiler_params=_cparams("parallel"))(x, g, cos_t, sin_t, swap)


def _mla_qk_bwd(dy, x, g, cos_t, sin_t, scale, name, tm):
    n, d = x.shape
    t = cos_t.shape[0]
    tm = min(tm, t)
    nt = t // tm
    swap_t = _half_swap().T

    def body(dy_ref, x_ref, g_ref, c_ref, s_ref, p_ref, dx_ref, dg_ref):
        xf = x_ref[...]
        gg = g_ref[...]
        r = lax.rsqrt(jnp.mean(xf * xf, axis=-1, keepdims=True) + EPS)
        xh = xf * r
        dyf = dy_ref[...] * scale
        dyr = dyf[:, MLA_NOPE:]
        back = lax.dot_general(dyr * s_ref[...], p_ref[...], NN, precision=lax.Precision.HIGHEST,
                               preferred_element_type=F32)
        dn_n = dyf[:, :MLA_NOPE]
        dn_r = dyr * c_ref[...] + back
        xh_n, xh_r = xh[:, :MLA_NOPE], xh[:, MLA_NOPE:]
        dxh_n = dn_n * gg[:, :MLA_NOPE]
        dxh_r = dn_r * gg[:, MLA_NOPE:]
        mean = (jnp.sum(dxh_n * xh_n, axis=-1, keepdims=True)
                + jnp.sum(dxh_r * xh_r, axis=-1, keepdims=True)) * (1.0 / d)
        dx_ref[:, :MLA_NOPE] = r * (dxh_n - xh_n * mean)
        dx_ref[:, MLA_NOPE:] = r * (dxh_r - xh_r * mean)

        @pl.when(pl.program_id(0) == 0)
        def _():
            dg_ref[...] = jnp.zeros_like(dg_ref)

        dg_ref[:, :MLA_NOPE] += jnp.sum(dn_n * xh_n, axis=0, keepdims=True)
        dg_ref[:, MLA_NOPE:] += jnp.sum(dn_r * xh_r, axis=0, keepdims=True)

    row = pl.BlockSpec((tm, d), lambda i: (i, 0))
    vec = pl.BlockSpec((1, d), lambda i: (0, 0))
    tab = pl.BlockSpec((tm, MLA_ROPE), lambda i: (i % nt, 0))
    return _pcall(
        body, name=name, grid=(n // tm,),
        in_specs=[row, row, vec, tab, tab, pl.BlockSpec((MLA_ROPE, MLA_ROPE), lambda i: (0, 0))],
        out_specs=(row, vec), out_shape=(_sds((n, d), F32), _sds((1, d), F32)),
        compiler_params=_cparams("arbitrary"))(dy, x, g, cos_t, sin_t, swap_t)


def _causal_mask(i, j, tq, tk, width):
    row = i * tq + lax.broadcasted_iota(jnp.int32, (tq, width), 0)
    col = j * tk + lax.broadcasted_iota(jnp.int32, (tq, width), 1)
    return col <= row


def _causal_steps(nq, nk, tq, tk, q_major):
    if q_major:
        groups = [[(i, j) for j in range((i * tq + tq - 1) // tk + 1)] for i in range(nq)]
        nunit = tk // tq if tk % tq == 0 else 1
    else:
        groups = [[(i, j) for i in range((j * tk) // tq, nq)] for j in range(nk)]
        nunit = tq // tk if tq % tk == 0 else 1
    it, jt, fl = [], [], []
    for g in groups:
        for n, (i, j) in enumerate(g):
            crossing = j * tk + tk - 1 > i * tq
            if q_major:
                unit = tk // nunit
                u = min(nunit, -(-(i * tq + tq - j * tk) // unit)) - 1
            else:
                unit = tq // nunit
                u = max(0, j * tk - i * tq) // unit
            it.append(i)
            jt.append(j)
            fl.append((n == 0) + 2 * (n == len(g) - 1) + 4 * crossing + 8 * (u if crossing else 0))
    return tuple(jnp.asarray(np.array(a, np.int32)) for a in (it, jt, fl)), nunit


def _by_crossing(flags, nunit, update):
    pl.when((flags & 4) == 0)(functools.partial(update, None))
    for u in range(nunit):
        pl.when(((flags & 4) != 0) & ((flags >> 3) == u))(functools.partial(update, u))


def _causal_specs(tq, tk):
    def qs(w):
        return pl.BlockSpec((None, tq, w), lambda h, s, it, jt, fl: (h, it[s], 0))

    def kv(w):
        return pl.BlockSpec((None, tk, w), lambda h, s, it, jt, fl: (h, jt[s], 0))

    return qs, kv


def _mla_fwd(q, k, v, tq, tk):
    nh, t, dq = q.shape
    dv = v.shape[2]
    tq, tk = min(tq, t), min(tk, t)
    tables, nunit = _causal_steps(t // tq, t // tk, tq, tk, True)

    def body(it, jt, fl, q_ref, k_ref, v_ref, o_ref, lse_ref, m_sc, l_sc, acc_sc):
        step = pl.program_id(1)
        i, j, flags = it[step], jt[step], fl[step]

        @pl.when((flags & 1) != 0)
        def _():
            m_sc[...] = jnp.full_like(m_sc, NEG)
            l_sc[...] = jnp.zeros_like(l_sc)
            acc_sc[...] = jnp.zeros_like(acc_sc)

        def update(units):
            wk = tk if units is None else (units + 1) * (tk // nunit)
            s = _dot(q_ref[...], k_ref[:wk, :], NT)
            if units is not None:
                s = jnp.where(_causal_mask(i, j, tq, tk, wk), s, NEG)
            m_prev = m_sc[...]
            m_new = jnp.maximum(m_prev, jnp.max(s, axis=-1, keepdims=True))
            alpha = jnp.exp(m_prev - m_new)
            p = jnp.exp(s - m_new)
            l_sc[...] = alpha * l_sc[...] + jnp.sum(p, axis=-1, keepdims=True)
            acc_sc[...] = alpha * acc_sc[...] + _dot(p.astype(BF16), v_ref[:wk, :], NN)
            m_sc[...] = m_new

        _by_crossing(flags, nunit, update)

        @pl.when((flags & 2) != 0)
        def _():
            o_ref[...] = acc_sc[...] / l_sc[...]
            lse_ref[...] = m_sc[...] + jnp.log(l_sc[...])

    qs, kv = _causal_specs(tq, tk)
    return _pcall(
        body, name="mla_attn_fwd",
        grid_spec=pltpu.PrefetchScalarGridSpec(
            num_scalar_prefetch=3, grid=(nh, tables[0].shape[0]),
            in_specs=[qs(dq), kv(dq), kv(dv)], out_specs=(qs(dv), qs(1)),
            scratch_shapes=[pltpu.VMEM((tq, 1), F32), pltpu.VMEM((tq, 1), F32), pltpu.VMEM((tq, dv), F32)]),
        out_shape=(_sds((nh, t, dv), F32), _sds((nh, t, 1), F32)),
        compiler_params=_cparams("parallel", "arbitrary"))(*tables, q, k, v)


def _mla_bwd(q, k, k_t, v, do, lse_row, dl_row, tq, tk):
    nh, t, dq = q.shape
    dv = v.shape[2]
    tq, tk = min(tq, t), min(tk, t)
    nq = t // tq
    tables, nunit = _causal_steps(nq, t // tk, tq, tk, False)

    def body(it, jt, fl, q_ref, k_ref, kt_ref, v_ref, do_ref, lse_ref, dl_ref, dk_ref, dv_ref, dq_ref, dk_sc, dv_sc):
        step = pl.program_id(1)
        i, j, flags = it[step], jt[step], fl[step]

        def update(units):
            off = 0 if units is None else units * (tq // nunit)
            qq = q_ref[off:, :]
            st = _dot(k_ref[...], qq, NT)
            if units is not None:
                key = j * tk + lax.broadcasted_iota(jnp.int32, (tk, tq - off), 0)
                qry = i * tq + off + lax.broadcasted_iota(jnp.int32, (tk, tq - off), 1)
                st = jnp.where(key <= qry, st, NEG)
            pt = jnp.exp(st - lse_ref[:, off:])
            dob = do_ref[off:, :].astype(BF16)
            dpt = _dot(v_ref[...], dob, NT)
            dst = pt * (dpt - dl_ref[:, off:])
            dsb = dst.astype(BF16)
            dv_part = _dot(pt.astype(BF16), dob, NN)
            dk_part = _dot(dsb, qq, NN)
            dq_part = _dot(kt_ref[...], dsb, NN)

            @pl.when((flags & 1) != 0)
            def _():
                dv_sc[...] = dv_part
                dk_sc[...] = dk_part

            @pl.when((flags & 1) == 0)
            def _():
                dv_sc[...] += dv_part
                dk_sc[...] += dk_part

            if off == 0:
                @pl.when(j == 0)
                def _():
                    dq_ref[i] = dq_part

                @pl.when(j != 0)
                def _():
                    dq_ref[i] += dq_part
            else:
                dq_ref[i, :, off:] += dq_part

        _by_crossing(flags, nunit, update)

        @pl.when((flags & 2) != 0)
        def _():
            dk_ref[...] = dk_sc[...]
            dv_ref[...] = dv_sc[...]

    qs, kv = _causal_specs(tq, tk)
    rowv = pl.BlockSpec((None, 1, tq), lambda h, s, it, jt, fl: (h, 0, it[s]))
    ktv = pl.BlockSpec((None, dq, tk), lambda h, s, it, jt, fl: (h, 0, jt[s]))
    whole = pl.BlockSpec((None, nq, dq, tq), lambda h, s, it, jt, fl: (h, 0, 0, 0))
    return _pcall(
        body, name="mla_attn_bwd",
        grid_spec=pltpu.PrefetchScalarGridSpec(
            num_scalar_prefetch=3, grid=(nh, tables[0].shape[0]),
            in_specs=[qs(dq), kv(dq), ktv, kv(dv), qs(dv), rowv, rowv], out_specs=(kv(dq), kv(dv), whole),
            scratch_shapes=[pltpu.VMEM((tk, dq), F32), pltpu.VMEM((tk, dv), F32)]),
        out_shape=(_sds((nh, t, dq), F32), _sds((nh, t, dv), F32), _sds((nh, nq, dq, tq), F32)),
        compiler_params=_cparams("parallel", "arbitrary"))(*tables, q, k, k_t, v, do, lse_row, dl_row)


def _loss_head(y, target, tm):
    t, d = y.shape
    tm = min(tm, t)
    nt = t // tm

    def body(y_ref, t_ref, dy_ref, loss_ref, acc):
        i = pl.program_id(0)
        err = y_ref[...] - t_ref[...]
        dy_ref[...] = err * (1.0 / d)

        @pl.when(i == 0)
        def _():
            acc[...] = jnp.zeros_like(acc)

        acc[...] += jnp.sum(err * err, axis=0, keepdims=True)

        @pl.when(i == nt - 1)
        def _():
            loss_ref[0, 0] = jnp.sum(acc[...]) * (0.5 / d)

    spec = pl.BlockSpec((tm, d), lambda i: (i, 0))
    return _pcall(
        body, name="loss_head", grid=(nt,), in_specs=[spec, spec],
        out_specs=(spec, pl.BlockSpec(memory_space=pltpu.SMEM)),
        out_shape=(_sds((t, d), F32), _sds((1, 1), F32)),
        scratch_shapes=[pltpu.VMEM((1, d), F32)],
        compiler_params=_cparams("arbitrary"))(y, target)


def _adamw(w, g, m, v, name):
    r, c = w.shape
    tr = r
    for cand in (256, 128, 64, 32, 16, 8):
        if r % cand == 0:
            tr = cand
            break

    def body(w_ref, g_ref, m_ref, v_ref, d_ref, nm_ref, nv_ref):
        gg = g_ref[...]
        nm = ADAM_B1 * m_ref[...] + (1.0 - ADAM_B1) * gg
        nv = ADAM_B2 * v_ref[...] + (1.0 - ADAM_B2) * (gg * gg)
        m_hat = nm / (1.0 - ADAM_B1 ** ADAM_STEP)
        v_hat = nv / (1.0 - ADAM_B2 ** ADAM_STEP)
        d_ref[...] = -ADAM_LR * (m_hat / (jnp.sqrt(v_hat) + ADAM_EPS) + ADAM_WD * w_ref[...])
        nm_ref[...] = nm
        nv_ref[...] = nv

    spec = pl.BlockSpec((tr, c), lambda i: (i, 0))
    sd = _sds((r, c), F32)
    return _pcall(body, name=name, grid=(r // tr,), in_specs=[spec] * 4, out_specs=(spec,) * 3,
                  out_shape=(sd, sd, sd), compiler_params=_cparams("parallel"))(w, g, m, v)


MESH_ID = pl.DeviceIdType.MESH
HBM_SPEC = pl.BlockSpec(memory_space=pltpu.HBM)


def _place():
    return lax.axis_index("x"), lax.axis_index("y"), lax.axis_index("c")


def _other_chips(x, y):
    return [(1 - x, y), (x, 1 - y), (1 - x, 1 - y)]


def _remote(src, dst, send_sems, recv_sems, k, to):
    return pltpu.make_async_remote_copy(src_ref=src, dst_ref=dst, send_sem=send_sems.at[k], recv_sem=recv_sems.at[k],
                                        device_id=to, device_id_type=MESH_ID)


def _halves(arrays):
    for a in arrays:
        assert a.shape[-2] % 32 == 0
    return [a.shape[-2] // 2 for a in arrays]


def _gather_weights(blocks):
    n = len(blocks)
    halves = _halves(blocks)

    def body(*refs):
        srcs, outs, send_sems, recv_sems = refs[:n], refs[n:2 * n], refs[2 * n], refs[2 * n + 1]
        x, y, c = _place()
        me = 2 * x + y
        sibling = (x, y, 1 - c)
        chips = _other_chips(x, y)

        def part(a, chip, core):
            return outs[a].at[chip, pl.ds(core * halves[a], halves[a]), :]

        for a in range(n):
            mine = srcs[a].at[pl.ds(c * halves[a], halves[a]), :]
            for k, (cx, cy) in enumerate(chips):
                _remote(mine, part(a, me, c), send_sems, recv_sems, 6 * a + k, (cx, cy, c)).start()
        for k, (cx, cy) in enumerate(chips):
            for a in range(n):
                got = part(a, 2 * cx + cy, c)
                _remote(got, got, send_sems, recv_sems, 6 * a + k, (x, y, c)).wait_recv()
                _remote(got, got, send_sems, recv_sems, 6 * a + 3 + k, sibling).start()
        for k, (cx, cy) in enumerate(chips):
            for a in range(n):
                got = part(a, 2 * cx + cy, 1 - c)
                _remote(got, got, send_sems, recv_sems, 6 * a + 3 + k, (x, y, c)).wait_recv()
        for a in range(n):
            sent = part(a, me, c)
            for k in range(6):
                _remote(sent, sent, send_sems, recv_sems, 6 * a + k, (x, y, c)).wait_send()

    return _pcall(
        body, name="gather_weights", in_specs=[HBM_SPEC] * n, out_specs=tuple([HBM_SPEC] * n),
        out_shape=tuple(_sds((N_CHIPS,) + b.shape, b.dtype) for b in blocks),
        scratch_shapes=[pltpu.SemaphoreType.DMA((6 * n,)), pltpu.SemaphoreType.DMA((6 * n,))],
    )(*blocks)


def _reduce_cores(grads, tag):
    n = len(grads)
    halves = _halves(grads)

    def body(*refs):
        gs, outs, send_sems, recv_sems = refs[:n], refs[n:2 * n], refs[2 * n], refs[2 * n + 1]
        x, y, c = _place()
        for a in range(n):
            for j in range(N_CHIPS):
                _remote(gs[a].at[j, pl.ds((1 - c) * halves[a], halves[a]), :], outs[a].at[j],
                        send_sems, recv_sems, a, (x, y, 1 - c)).start()
        for a in range(n):
            _remote(gs[a].at[:, pl.ds((1 - c) * halves[a], halves[a]), :], outs[a],
                    send_sems, recv_sems, a, (x, y, c)).wait()

    return _pcall(
        body, name=f"reduce_cores_{tag}", in_specs=[HBM_SPEC] * n, out_specs=tuple([HBM_SPEC] * n),
        out_shape=tuple(_sds((N_CHIPS, h, g.shape[2]), g.dtype) for g, h in zip(grads, halves)),
        scratch_shapes=[pltpu.SemaphoreType.DMA((n,)), pltpu.SemaphoreType.DMA((n,))],
    )(*grads)


def _scatter_chips(parts):
    n = len(parts)

    def body(*refs):
        ps, outs, send_sems, recv_sems = refs[:n], refs[n:2 * n], refs[2 * n], refs[2 * n + 1]
        _scatter_start(ps, outs, send_sems, recv_sems)
        _scatter_wait(ps, outs, send_sems, recv_sems)

    return _pcall(
        body, name="scatter_chips", in_specs=[HBM_SPEC] * n, out_specs=tuple([HBM_SPEC] * n),
        out_shape=_scatter_shapes(parts), scratch_shapes=_scatter_sems(n),
    )(*parts)


def _scatter_shapes(parts):
    return tuple(_sds((3,) + p.shape[1:], p.dtype) for p in parts)


def _scatter_sems(n):
    return [pltpu.SemaphoreType.DMA((3 * n,)), pltpu.SemaphoreType.DMA((3 * n,))]


def _scatter_start(ps, outs, send_sems, recv_sems):
    x, y, c = _place()
    for a in range(len(ps)):
        for k, (cx, cy) in enumerate(_other_chips(x, y)):
            _remote(ps[a].at[2 * cx + cy], outs[a].at[k], send_sems, recv_sems, 3 * a + k, (cx, cy, c)).start()


def _scatter_wait(ps, outs, send_sems, recv_sems):
    x, y, c = _place()
    for a in range(len(ps)):
        for k in range(3):
            _remote(ps[a].at[k], outs[a].at[k], send_sems, recv_sems, 3 * a + k, (x, y, c)).wait()


def _sum_partials(received, parts, place):
    n = len(parts)
    steps = 2
    tiles = [p.shape[1] // steps for p in parts]

    def body(place_ref, *refs):
        rs, ps, outs = refs[:n], refs[n:2 * n], refs[2 * n:]
        for a in range(n):
            tot = ps[a][...].astype(F32)
            for k in range(3):
                tot = tot + rs[a][k].astype(F32)
            outs[a][...] = tot

    cols = [p.shape[2] for p in parts]
    return _pcall(
        body, name="sum_chip_partials",
        grid_spec=pltpu.PrefetchScalarGridSpec(
            num_scalar_prefetch=1, grid=(steps,),
            in_specs=[pl.BlockSpec((3, tm, w), lambda i, pc: (0, i, 0)) for tm, w in zip(tiles, cols)]
            + [pl.BlockSpec((None, tm, w), lambda i, pc: (pc[0], i, 0)) for tm, w in zip(tiles, cols)],
            out_specs=tuple(pl.BlockSpec((tm, w), lambda i, pc: (pc[1] * steps + i, 0)) for tm, w in zip(tiles, cols))),
        out_shape=tuple(_sds((2 * p.shape[1], p.shape[2]), F32) for p in parts),
        compiler_params=_cparams("parallel"))(place, *received, *parts)


def _share_cores(blocks):
    n = len(blocks)
    halves = _halves(blocks)

    def body(*refs):
        srcs, outs, send_sems, recv_sems = refs[:n], refs[n:2 * n], refs[2 * n], refs[2 * n + 1]
        x, y, c = _place()
        for a in range(n):
            piece = pl.ds(c * halves[a], halves[a])
            _remote(srcs[a].at[piece, :], outs[a].at[piece, :], send_sems, recv_sems, a, (x, y, 1 - c)).start()
        for a in range(n):
            mine = outs[a].at[pl.ds(c * halves[a], halves[a]), :]
            theirs = outs[a].at[pl.ds((1 - c) * halves[a], halves[a]), :]
            _remote(mine, theirs, send_sems, recv_sems, a, (x, y, c)).wait()

    return _pcall(
        body, name="share_cores", in_specs=[HBM_SPEC] * n, out_specs=tuple([HBM_SPEC] * n),
        out_shape=tuple(_sds(b.shape, b.dtype) for b in blocks), input_output_aliases={a: a for a in range(n)},
        scratch_shapes=[pltpu.SemaphoreType.DMA((n,)), pltpu.SemaphoreType.DMA((n,))],
    )(*blocks)


def _sum_blocks(stacked, name, tm):
    n, rows, lanes = stacked.shape
    tm = min(tm, rows)

    def body(s_ref, o_ref):
        tot = s_ref[n - 1].astype(F32)
        for k in range(n - 1):
            tot = tot + s_ref[k].astype(F32)
        o_ref[...] = tot

    return _pcall(body, name=name, grid=(rows // tm,),
                  in_specs=[pl.BlockSpec((n, tm, lanes), lambda i: (0, i, 0))],
                  out_specs=pl.BlockSpec((tm, lanes), lambda i: (i, 0)), out_shape=_sds((rows, lanes), F32),
                  compiler_params=_cparams("parallel"))(stacked)


def _add_halves(grads, theirs, core, tag):
    n = len(grads)
    steps = 2
    tiles = [t.shape[1] // steps for t in theirs]
    cols = [t.shape[2] for t in theirs]

    def body(c_ref, *refs):
        gs, ts, outs = refs[:n], refs[n:2 * n], refs[2 * n:]
        for a in range(n):
            outs[a][...] = (gs[a][...] + ts[a][...]).astype(BF16)

    own = [pl.BlockSpec((None, tm, w), lambda k, i, c: (k, c[0] * steps + i, 0)) for tm, w in zip(tiles, cols)]
    same = [pl.BlockSpec((None, tm, w), lambda k, i, c: (k, i, 0)) for tm, w in zip(tiles, cols)]
    return _pcall(
        body, name=f"add_core_halves_{tag}",
        grid_spec=pltpu.PrefetchScalarGridSpec(
            num_scalar_prefetch=1, grid=(N_CHIPS, steps), in_specs=own + same, out_specs=tuple(same)),
        out_shape=tuple(_sds(t.shape, BF16) for t in theirs),
        compiler_params=_cparams("parallel", "parallel"))(core, *grads, *theirs)


def _allreduce_small(part):
    rows, lanes = part.shape
    ndev = 8

    def body(src, tot, buf, send_sems, recv_sems):
        x, y, c = _place()
        me = 4 * x + 2 * y + c
        buf[me] = src[...]
        sends = []
        for k in range(1, ndev):
            peer = (x ^ (k >> 2), y ^ ((k >> 1) & 1), c ^ (k & 1))
            cp = _remote(src, buf.at[me], send_sems, recv_sems, k - 1, peer)
            cp.start()
            sends.append(cp)
        for k in range(1, ndev):
            theirs = buf.at[me ^ k]
            _remote(theirs, theirs, send_sems, recv_sems, k - 1, (x, y, c)).wait_recv()
        for cp in sends:
            cp.wait_send()
        acc = buf[0]
        for d in range(1, ndev):
            acc = acc + buf[d]
        tot[...] = acc

    vm = pl.BlockSpec(memory_space=pltpu.VMEM)
    return _pcall(
        body, name="allreduce_small", in_specs=[vm], out_specs=vm, out_shape=_sds((rows, lanes), F32),
        scratch_shapes=[pltpu.VMEM((ndev, rows, lanes), F32), pltpu.SemaphoreType.DMA((ndev - 1,)),
                        pltpu.SemaphoreType.DMA((ndev - 1,))],
    )(part)


def _pack_small(vals):
    parts = []
    for name, shape, r in SMALL:
        flat = vals[name].reshape(-1).astype(F32)
        parts.append(jnp.pad(flat, (0, r * LANES - flat.shape[0])).reshape(r, LANES))
    used = sum(r for _, _, r in SMALL)
    parts.append(jnp.zeros((SMALL_ROWS - used, LANES), F32))
    return jnp.concatenate(parts, axis=0)


def _unpack_small(packed):
    out, off = {}, 0
    for name, shape, r in SMALL:
        n = int(np.prod(shape))
        out[name] = packed[off:off + r].reshape(-1)[:n].reshape(shape)
        off += r
    return out


def _heads_major(a, nh):
    t = a.shape[0]
    return a.reshape(t, nh, a.shape[1] // nh).transpose(1, 0, 2)


def _tokens_major(a):
    nh, t, w = a.shape
    return a.transpose(1, 0, 2).reshape(t, nh * w)


LATE = ("ffn1_w_gate", "ffn1_w_up", "ffn1_w_down")
EARLY = tuple(name for name, _ in BIG if name not in LATE)


def _local_step(x, target, small, wfull, early_exchange=None):
    t = x.shape[0]
    nh, hd = DIL_HEADS, DIL_HD
    w_in = wfull["w_in"].transpose(1, 0, 2).reshape(D_MODEL, -1)
    w_out = wfull["w_out"].reshape(D_MODEL, D_MODEL)
    w_qb, w_kvb = wfull["mla_w_q_b"], wfull["mla_w_kv_b"]
    grads_s, grads_b = {}, {}

    x1, ffn1_saved = _ffn_fwd(x, small["ffn1_norm"], wfull["ffn1_w_gate"], wfull["ffn1_w_up"],
                              wfull["ffn1_w_down"], "ffn1")
    hm = _rms_fwd(x1, small["mix_norm"], BF16, "mix_norm", 512)
    proj = _mm_simple("in_proj", hm, w_in, NN, F32, tm=1024)
    cq, ckv, k_pe = proj[:, 1536:1792], proj[:, 1792:1920], proj[:, 1920:1984]

    gq, gk = jnp.tile(small["dil_q_norm"], (1, nh)), jnp.tile(small["dil_k_norm"], (1, nh))
    qn = _head_norm_fwd(proj, 0, gq, "dil_q_norm", 512)
    kn = _head_norm_fwd(proj, 1, gk, "dil_k_norm", 512)
    v_d = _head_norm_fwd(proj, 2, None, "dil_v_views", 512)
    bias = _bias_tiles(small["rel_bias"]).reshape(3, nh // 2, 2 * QB, QB + DIL_W)
    outs, lses = [], []
    for b, dil in enumerate(DIL_DILATIONS):
        o_b, lse_b = _dil_fwd(qn[b], kn[b], v_d[b], bias[b], dil, f"dil_fwd_{dil}")
        outs.append(o_b)
        lses.append(lse_b)
    o_dil, lse_tot, od = _dil_merge(outs, lses, small["out_norm_dil"], 512)

    mh = MLA_HEADS
    cos_t, sin_t = _rope_tables(t)
    cqn = _rms_fwd(cq, small["mla_q_a_norm"], BF16, "mla_q_a_norm", 512)
    ckvn = _rms_fwd(ckv, small["mla_kv_a_norm"], BF16, "mla_kv_a_norm", 512)
    tm = min(512, t)

    th = min(2048, t)

    def head_proj(name, a, w, width):
        k = a.shape[1]
        return _mm(name, (mh, t // th, 1),
                   [(a, pl.BlockSpec((th, k), lambda h, i, r: (i, 0)), w, pl.BlockSpec((None, k, width), lambda h, i, r: (h, 0, 0)))],
                   NN, _sds((mh, t, width), F32), pl.BlockSpec((None, th, width), lambda h, i, r: (h, i, 0)), (th, width))

    q_raw = head_proj("mla_q_proj", cqn, w_qb, MLA_QK)
    kv_raw = head_proj("mla_kv_proj", ckvn, w_kvb, MLA_NOPE + MLA_V)
    k_raw = jnp.concatenate([kv_raw[:, :, :MLA_NOPE], jnp.broadcast_to(k_pe[None], (mh, t, MLA_ROPE))], axis=2)
    v_m = kv_raw[:, :, MLA_NOPE:].astype(BF16)
    q_raw2, k_raw2 = q_raw.reshape(mh * t, MLA_QK), k_raw.reshape(mh * t, MLA_QK)
    q_scale = MLA_QK ** -0.5
    q_m = _mla_qk_fwd(q_raw2, small["mla_q_norm"], cos_t, sin_t, q_scale, "mla_q_rope", 2048).reshape(mh, t, MLA_QK)
    k_m = _mla_qk_fwd(k_raw2, small["mla_k_norm"], cos_t, sin_t, 1.0, "mla_k_rope", 2048).reshape(mh, t, MLA_QK)
    o_mla_h, lse_m = _mla_fwd(q_m, k_m, v_m, 512, 4096)
    o_mla = _tokens_major(o_mla_h)

    om = _rms_fwd(o_mla, small["out_norm_mla"], BF16, "out_norm_mla", 512)
    half_w = DIL_WIDTH
    row = pl.BlockSpec((tm, D_MODEL), lambda i, j, r: (i, 0))
    act_spec = pl.BlockSpec((tm, half_w), lambda i, j, r: (i, 0))
    x2 = _mm("out_proj", (t // tm, 1, 1),
             [(od, act_spec, w_out, pl.BlockSpec((half_w, D_MODEL), lambda i, j, r: (0, 0))),
              (om, act_spec, w_out, pl.BlockSpec((half_w, D_MODEL), lambda i, j, r: (1, 0)))],
             NN, _sds((t, D_MODEL), F32), row, (tm, D_MODEL), res=(x1, row))
    x3, ffn2_saved = _ffn_fwd(x2, small["ffn2_norm"], wfull["ffn2_w_gate"], wfull["ffn2_w_up"],
                              wfull["ffn2_w_down"], "ffn2")
    dy, loss = _loss_head(x3, target, 512)

    dx2, grads_s["ffn2_norm"], grads_b["ffn2_w_gate"], grads_b["ffn2_w_up"], grads_b["ffn2_w_down"], _ = _ffn_bwd(
        dy, x2, small["ffn2_norm"], wfull["ffn2_w_gate"], wfull["ffn2_w_up"], wfull["ffn2_w_down"], ffn2_saved, "ffn2")

    d_ocat = _mm_simple("out_proj_dx", dx2, w_out, NT, F32, tm=1024)
    dw_out_d = _mm_simple("out_proj_dw_dil", od, dx2, TN, F32, tk=2048)
    dw_out_m = _mm_simple("out_proj_dw_mla", om, dx2, TN, F32, tk=2048)
    grads_b["w_out"] = jnp.concatenate([dw_out_d, dw_out_m], axis=0).reshape(N_CHIPS, D_MODEL // N_CHIPS, D_MODEL)
    do_dil, grads_s["out_norm_dil"] = _rms_bwd([d_ocat[:, :half_w]], o_dil, small["out_norm_dil"], None, "out_norm_dil_bwd", 512)
    do_mla, grads_s["out_norm_mla"] = _rms_bwd([d_ocat[:, half_w:]], o_mla, small["out_norm_mla"], None, "out_norm_mla_bwd", 512)

    do_m = _heads_major(do_mla, mh)
    dl_m = _rowdot(do_m.reshape(mh * t, MLA_V), o_mla_h.reshape(mh * t, MLA_V), "mla_delta", 2048).reshape(mh, t, 1)
    dk_m, dv_m, dq_t = _mla_bwd(q_m, k_m, k_m.transpose(0, 2, 1), v_m, do_m, lse_m.reshape(mh, 1, t),
                                dl_m.reshape(mh, 1, t), 2048, 512)
    dq_m = dq_t.transpose(0, 1, 3, 2).reshape(mh, t, MLA_QK)
    dq_raw, grads_s["mla_q_norm"] = _mla_qk_bwd(dq_m.reshape(mh * t, MLA_QK), q_raw2, small["mla_q_norm"],
                                                 cos_t, sin_t, q_scale, "mla_q_rope_bwd", 2048)
    dk_raw, grads_s["mla_k_norm"] = _mla_qk_bwd(dk_m.reshape(mh * t, MLA_QK), k_raw2, small["mla_k_norm"],
                                                 cos_t, sin_t, 1.0, "mla_k_rope_bwd", 2048)
    dq_raw = dq_raw.reshape(mh, t, MLA_QK)
    dk_raw = dk_raw.reshape(mh, t, MLA_QK)
    dkv_raw = jnp.concatenate([dk_raw[:, :, :MLA_NOPE], dv_m], axis=2)
    dk_pe_h = dk_raw[:, :, MLA_NOPE:]

    def head_proj_dx(name, d, w):
        width, k = d.shape[2], w.shape[1]
        pairs = [(d, pl.BlockSpec((None, th, width), lambda i, j, r, h=h: (h, i, 0)),
                  w, pl.BlockSpec((None, k, width), lambda i, j, r, h=h: (h, 0, 0))) for h in range(mh)]
        return _mm(name, (t // th, 1, 1), pairs, NT, _sds((t, k), F32),
                   pl.BlockSpec((th, k), lambda i, j, r: (i, 0)), (th, k))

    def head_proj_dw(name, a, d):
        width, k = d.shape[2], a.shape[1]
        return _mm(name, (mh, 1, t // th),
                   [(a, pl.BlockSpec((th, k), lambda h, j, r: (r, 0)), d, pl.BlockSpec((None, th, width), lambda h, j, r: (h, r, 0)))],
                   TN, _sds((mh, k, width), F32), pl.BlockSpec((None, k, width), lambda h, j, r: (h, 0, 0)), (k, width))

    d_cqn = head_proj_dx("mla_q_proj_dx", dq_raw, w_qb)
    d_ckvn = head_proj_dx("mla_kv_proj_dx", dkv_raw, w_kvb)
    grads_b["mla_w_q_b"] = head_proj_dw("mla_q_proj_dw", cqn, dq_raw)
    grads_b["mla_w_kv_b"] = head_proj_dw("mla_kv_proj_dw", ckvn, dkv_raw)
    d_cq, grads_s["mla_q_a_norm"] = _rms_bwd([d_cqn], cq, small["mla_q_a_norm"], None, "mla_q_a_norm_bwd", 512)
    d_ckv, grads_s["mla_kv_a_norm"] = _rms_bwd([d_ckvn], ckv, small["mla_kv_a_norm"], None, "mla_kv_a_norm_bwd", 512)
    d_kpe = _sum_blocks(dk_pe_h.reshape(mh, t * MLA_ROPE // LANES, LANES), "mla_kpe_sum", 1024).reshape(t, MLA_ROPE)

    stats, do_db = _dil_stats(do_dil, o_dil, lse_tot, 512)
    dqs, dks, dvs, dtiles = [], [], [], []
    for b, dil in enumerate(DIL_DILATIONS):
        dq_b, dk_b, dv_b, db_b = _dil_bwd(qn[b], kn[b], v_d[b], do_db[b], stats[b], bias[b], dil, f"dil_bwd_{dil}")
        dqs.append(dq_b)
        dks.append(dk_b)
        dvs.append(dv_b)
        dtiles.append(db_b)
    grads_s["rel_bias"] = _bias_grad(jnp.stack(dtiles).reshape(3, nh, QB, QB + DIL_W))
    dq_a, dgq = _head_norm_bwd(dqs, proj, 0, gq, "dil_q_norm_bwd", 512)
    dk_a, dgk = _head_norm_bwd(dks, proj, 1, gk, "dil_k_norm_bwd", 512)
    grads_s["dil_q_norm"], grads_s["dil_k_norm"] = dgq[:, :hd], dgk[:, :hd]
    dv_a = _sum_branches(dvs, "dil_dv_sum", 512)

    dparts = [dq_a, dk_a, dv_a, d_cq, d_ckv, d_kpe]
    t2 = min(1024, t)
    pairs, dw_parts, lo = [], [], 0
    for n, dpart in enumerate(dparts):
        width = dpart.shape[1]
        w_part = w_in[:, lo:lo + width]
        pairs.append((dpart, pl.BlockSpec((t2, width), lambda i, j, r: (i, 0)),
                      w_part, pl.BlockSpec((D_MODEL, width), lambda i, j, r: (0, 0))))
        dw_parts.append(_mm_simple(f"in_proj_dw_{n}", hm, dpart, TN, F32, tk=2048))
        lo += width
    d_hm = _mm("in_proj_dx", (t // t2, 1, 1), pairs, NT, _sds((t, D_MODEL), F32),
               pl.BlockSpec((t2, D_MODEL), lambda i, j, r: (i, 0)), (t2, D_MODEL))
    dw_in = jnp.concatenate(dw_parts, axis=1)
    grads_b["w_in"] = dw_in.reshape(D_MODEL, N_CHIPS, -1).transpose(1, 0, 2)
    dx1, grads_s["mix_norm"] = _rms_bwd([d_hm], x1, small["mix_norm"], dx2, "mix_norm_bwd", 512)
    outgoing = early_exchange([grads_b[n] for n in EARLY]) if early_exchange else ()
    dx, grads_s["ffn1_norm"], grads_b["ffn1_w_gate"], grads_b["ffn1_w_up"], grads_b["ffn1_w_down"], arrived = _ffn_bwd(
        dx1, x, small["ffn1_norm"], wfull["ffn1_w_gate"], wfull["ffn1_w_up"], wfull["ffn1_w_down"], ffn1_saved, "ffn1",
        outgoing)
    return loss, dx, grads_s, grads_b, (tuple(outgoing), arrived)


def kernel(x, ffn1_norm, ffn1_w_gate, ffn1_w_up, ffn1_w_down, mix_norm, w_in, dil_q_norm, dil_k_norm, rel_bias, mla_q_a_norm, mla_w_q_b, mla_kv_a_norm, mla_w_kv_b, mla_q_norm, mla_k_norm, out_norm_dil, out_norm_mla, w_out, ffn2_norm, ffn2_w_gate, ffn2_w_up, ffn2_w_down, loss_target, m_ffn1_norm, m_ffn1_w_gate, m_ffn1_w_up, m_ffn1_w_down, m_mix_norm, m_w_in, m_dil_q_norm, m_dil_k_norm, m_rel_bias, m_mla_q_a_norm, m_mla_w_q_b, m_mla_kv_a_norm, m_mla_w_kv_b, m_mla_q_norm, m_mla_k_norm, m_out_norm_dil, m_out_norm_mla, m_w_out, m_ffn2_norm, m_ffn2_w_gate, m_ffn2_w_up, m_ffn2_w_down, v_ffn1_norm, v_ffn1_w_gate, v_ffn1_w_up, v_ffn1_w_down, v_mix_norm, v_w_in, v_dil_q_norm, v_dil_k_norm, v_rel_bias, v_mla_q_a_norm, v_mla_w_q_b, v_mla_kv_a_norm, v_mla_w_kv_b, v_mla_q_norm, v_mla_k_norm, v_out_norm_dil, v_out_norm_mla, v_w_out, v_ffn2_norm, v_ffn2_w_gate, v_ffn2_w_up, v_ffn2_w_down):
    given = dict(locals())
    big_names = [name for name, _ in BIG]
    small_names = [name for name, _, _ in SMALL]

    chip = (2 * lax.axis_index("x") + lax.axis_index("y")).astype(jnp.int32)
    core = lax.axis_index("c").astype(jnp.int32)
    mine = [given[n].astype(BF16) for n in big_names]
    gathered = _gather_weights([m[0] for m in mine])
    wfull = {n: lax.dynamic_update_slice(g, m, (chip, 0, 0)) for n, g, m in zip(big_names, gathered, mine)}
    small = {n: given[n] for n in small_names}

    def chip_partials(partial, tag):
        return _add_halves(partial, _reduce_cores(partial, tag), core.reshape(1), tag)

    loss, dx, grads_s, grads_b, (early_part, early_got) = _local_step(
        x[0], loss_target[0], small, wfull, functools.partial(chip_partials, tag="early"))
    loss = lax.psum(loss[0, 0], ("x", "y", "c"))
    late_part = chip_partials([grads_b[n] for n in LATE], "late")
    reduced = _sum_partials(tuple(_scatter_chips(late_part)) + tuple(early_got), tuple(late_part) + tuple(early_part),
                            jnp.stack([chip, core]))
    g_big = dict(zip(LATE + EARLY, _share_cores(reduced)))
    g_small = _unpack_small(_allreduce_small(_pack_small(grads_s)))

    grad, delta, new_m, new_v = {}, {}, {}, {}
    for name, shape in BIG:
        g2 = g_big[name]
        d_, m_, v_ = _adamw(given[name].reshape(shape), g2, given["m_" + name].reshape(shape),
                            given["v_" + name].reshape(shape), f"adamw_{name}")
        full = given[name].shape
        grad[name], delta[name], new_m[name], new_v[name] = (a.reshape(full) for a in (g2, d_, m_, v_))
    ps = {k: _pack_small({n: given[pre + n] for n in small_names}) for k, pre in (("w", ""), ("m", "m_"), ("v", "v_"))}
    gs_packed = _pack_small(g_small)
    d_s, m_s, v_s = (_unpack_small(a) for a in _adamw(ps["w"], gs_packed, ps["m"], ps["v"], "adamw_small"))
    for name in small_names:
        grad[name], delta[name], new_m[name], new_v[name] = g_small[name], d_s[name], m_s[name], v_s[name]

    return (loss, dx[None], *[grad[n] for n in WEIGHTS], *[delta[n] for n in WEIGHTS],
            *[new_m[n] for n in WEIGHTS], *[new_v[n] for n in WEIGHTS])
```

```python
import functools

import numpy as np
import jax
import jax.numpy as jnp
from jax import lax
from jax.experimental import pallas as pl
from jax.experimental.pallas import tpu as pltpu

F32 = jnp.float32
BF16 = jnp.bfloat16

D_MODEL = 1024
D_FF = 2816
N_CHIPS = 4
DIL_HEADS = 8
DIL_HD = 64
DIL_WIDTH = 512
DIL_DILATIONS = (1, 4, 16)
DIL_W = 128
QB = 128
MLA_HEADS = 4
MLA_NOPE = 128
MLA_ROPE = 64
MLA_QK = 192
MLA_V = 128
MLA_Q_RANK = 256
MLA_KV_RANK = 128
ROPE_BASE = 10000.0
REL_BUCKETS = 32
REL_MAX_DIST = 2048
FFN_RESID = 0.5
EPS = 1e-6
NEG = -1e30
LANES = 128

ADAM_LR = 0.001
ADAM_B1 = 0.9
ADAM_B2 = 0.999
ADAM_EPS = 1e-08
ADAM_WD = 0.01
ADAM_STEP = 10

NT = (((1,), (1,)), ((), ()))
NN = (((1,), (0,)), ((), ()))
TN = (((0,), (0,)), ((), ()))

BIG = (
    ("ffn1_w_gate", (D_MODEL, D_FF // N_CHIPS)),
    ("ffn1_w_up", (D_MODEL, D_FF // N_CHIPS)),
    ("ffn1_w_down", (D_FF // N_CHIPS, D_MODEL)),
    ("w_in", (D_MODEL, 1984 // N_CHIPS)),
    ("mla_w_q_b", (MLA_Q_RANK, MLA_QK)),
    ("mla_w_kv_b", (MLA_KV_RANK, MLA_NOPE + MLA_V)),
    ("w_out", (D_MODEL // N_CHIPS, D_MODEL)),
    ("ffn2_w_gate", (D_MODEL, D_FF // N_CHIPS)),
    ("ffn2_w_up", (D_MODEL, D_FF // N_CHIPS)),
    ("ffn2_w_down", (D_FF // N_CHIPS, D_MODEL)),
)
SMALL = (
    ("ffn1_norm", (1, 1024), 8), ("mix_norm", (1, 1024), 8), ("dil_q_norm", (1, 64), 1),
    ("dil_k_norm", (1, 64), 1), ("rel_bias", (8, 32), 2), ("mla_q_a_norm", (1, 256), 2),
    ("mla_kv_a_norm", (1, 128), 1), ("mla_q_norm", (1, 192), 2), ("mla_k_norm", (1, 192), 2),
    ("out_norm_dil", (1, 512), 4), ("out_norm_mla", (1, 512), 4), ("ffn2_norm", (1, 1024), 8),
)
SMALL_ROWS = 48
WEIGHTS = ("ffn1_norm", "ffn1_w_gate", "ffn1_w_up", "ffn1_w_down", "mix_norm", "w_in", "dil_q_norm",
           "dil_k_norm", "rel_bias", "mla_q_a_norm", "mla_w_q_b", "mla_kv_a_norm", "mla_w_kv_b",
           "mla_q_norm", "mla_k_norm", "out_norm_dil", "out_norm_mla", "w_out", "ffn2_norm",
           "ffn2_w_gate", "ffn2_w_up", "ffn2_w_down")


def _pcall(body, **kw):
    return pl.pallas_call(body, **kw)


def _cparams(*sem):
    return pltpu.CompilerParams(dimension_semantics=sem)


def _sds(shape, dtype):
    return jax.ShapeDtypeStruct(shape, dtype)


def _dot(a, b, dn):
    return lax.dot_general(a, b, dn, preferred_element_type=F32)


def _rms_fwd(x, g, out_dtype, name, tm):
    n, d = x.shape
    tm = min(tm, n)

    def body(x_ref, g_ref, o_ref):
        xf = x_ref[...].astype(F32)
        r = lax.rsqrt(jnp.mean(xf * xf, axis=-1, keepdims=True) + EPS)
        o_ref[...] = (xf * r * g_ref[...]).astype(o_ref.dtype)

    return _pcall(
        body, name=name, grid=(n // tm,),
        in_specs=[pl.BlockSpec((tm, d), lambda i: (i, 0)), pl.BlockSpec((1, d), lambda i: (0, 0))],
        out_specs=pl.BlockSpec((tm, d), lambda i: (i, 0)),
        out_shape=_sds((n, d), out_dtype), compiler_params=_cparams("parallel"))(x, g)


def _rms_bwd(dys, x, g, res, name, tm):
    n, d = x.shape
    tm = min(tm, n)
    nd = len(dys)
    has_res = res is not None

    def body(*refs):
        dy_refs = refs[:nd]
        x_ref, g_ref = refs[nd], refs[nd + 1]
        res_ref = refs[nd + 2] if has_res else None
        dx_ref, dg_ref = refs[-2], refs[-1]
        dy = dy_refs[0][...].astype(F32)
        for r_ in dy_refs[1:]:
            dy = dy + r_[...].astype(F32)
        xf = x_ref[...].astype(F32)
        r = lax.rsqrt(jnp.mean(xf * xf, axis=-1, keepdims=True) + EPS)
        xh = xf * r
        dxh = dy * g_ref[...]
        dx = r * (dxh - xh * jnp.mean(dxh * xh, axis=-1, keepdims=True))
        if has_res:
            dx = dx + res_ref[...]
        dx_ref[...] = dx

        @pl.when(pl.program_id(0) == 0)
        def _():
            dg_ref[...] = jnp.zeros_like(dg_ref)

        dg_ref[...] += jnp.sum(dy * xh, axis=0, keepdims=True)

    row = pl.BlockSpec((tm, d), lambda i: (i, 0))
    vec = pl.BlockSpec((1, d), lambda i: (0, 0))
    ins = list(dys) + [x, g] + ([res] if has_res else [])
    return _pcall(
        body, name=name, grid=(n // tm,),
        in_specs=[row] * nd + [row, vec] + ([row] if has_res else []),
        out_specs=(row, vec),
        out_shape=(_sds((n, d), F32), _sds((1, d), F32)),
        compiler_params=_cparams("arbitrary"))(*ins)


def _mm(name, grid, pairs, dn, out_shape, out_spec, acc_shape, res=None, scale=1.0):
    npairs = len(pairs)
    nred = grid[2]
    has_res = res is not None

    def body(*refs):
        ab = refs[:2 * npairs]
        res_ref = refs[2 * npairs] if has_res else None
        o_ref = refs[2 * npairs + int(has_res)]
        acc_ref = refs[-1] if nred > 1 else None
        tot = None
        for p in range(npairs):
            d = _dot(ab[2 * p][...].astype(BF16), ab[2 * p + 1][...].astype(BF16), dn)
            tot = d if tot is None else tot + d

        def finish(v):
            if scale != 1.0:
                v = v * scale
            if has_res:
                v = res_ref[...] + v
            o_ref[...] = v.astype(o_ref.dtype)

        if nred == 1:
            finish(tot)
        else:
            r = pl.program_id(2)

            @pl.when(r == 0)
            def _():
                acc_ref[...] = tot

            @pl.when(r > 0)
            def _():
                acc_ref[...] += tot

            @pl.when(r == nred - 1)
            def _():
                finish(acc_ref[...])

    ins, specs = [], []
    for a, a_spec, b, b_spec in pairs:
        ins += [a, b]
        specs += [a_spec, b_spec]
    if has_res:
        ins.append(res[0])
        specs.append(res[1])
    return _pcall(
        body, name=name, grid=grid, in_specs=specs, out_specs=out_spec, out_shape=out_shape,
        scratch_shapes=[pltpu.VMEM(acc_shape, F32)] if nred > 1 else [],
        compiler_params=_cparams("parallel", "parallel", "arbitrary"))(*ins)


def _ffn_up(h, wg, wu, name, tm, incoming=()):
    t, d = h.shape
    nc, _, fs = wg.shape
    tm = min(tm, t)
    nt = t // tm
    ni = len(incoming)
    halves = _halves(incoming)

    def body(*refs):
        h_ref, wg_ref, wu_ref = refs[:3]
        srcs = refs[3:3 + ni]
        g_ref, u_ref, a_ref = refs[3 + ni:6 + ni]
        outs = refs[6 + ni:6 + 2 * ni]
        if ni:
            send_sems, recv_sems = refs[6 + 2 * ni:]
            c, i = pl.program_id(0), pl.program_id(1)

            @pl.when((c == 0) & (i == 0))
            def _():
                _gather_start(srcs, outs, halves, send_sems, recv_sems)

        hh = h_ref[...]
        gate = _dot(hh, wg_ref[...], NN)
        up = _dot(hh, wu_ref[...], NN)
        sig = jax.nn.sigmoid(gate)
        silu = gate * sig
        g_ref[...] = (up * (sig + silu * (1.0 - sig))).astype(BF16)
        u_ref[...] = silu.astype(BF16)
        a_ref[...] = (silu * up).astype(BF16)

        if ni:
            @pl.when((c == nc - 1) & (i == nt - 1))
            def _():
                _gather_wait(outs, halves, send_sems, recv_sems)

    wspec = pl.BlockSpec((None, d, fs), lambda c, i: (c, 0, 0))
    ospec = pl.BlockSpec((None, tm, fs), lambda c, i: (c, i, 0))
    hbm = pl.BlockSpec(memory_space=pltpu.HBM)
    osd = _sds((nc, t, fs), BF16)
    res = tuple(_pcall(
        body, name=name, grid=(nc, nt),
        in_specs=[pl.BlockSpec((tm, d), lambda c, i: (i, 0)), wspec, wspec] + [hbm] * ni,
        out_specs=(ospec, ospec, ospec) + (hbm,) * ni,
        out_shape=(osd, osd, osd) + tuple(_sds((N_CHIPS,) + b.shape, b.dtype) for b in incoming),
        scratch_shapes=[pltpu.SemaphoreType.DMA((3 * ni,)), pltpu.SemaphoreType.DMA((3 * ni,))] if ni else [],
        compiler_params=_cparams("arbitrary", "arbitrary"))(h, wg, wu, *incoming))
    return res[:3] + (res[3:],)


def _ffn_hidden_bwd(dy, h, wd, dact_dgate, dact_dup, act, name, tm, outgoing=()):
    t, d = dy.shape
    nc, fs, _ = wd.shape
    tm = min(tm, t)
    nt = t // tm
    no = len(outgoing)

    def body(*refs):
        dy_ref, h_ref, wd_ref, g_ref, u_ref, a_ref = refs[:6]
        sent = refs[6:6 + no]
        dg_ref, du_ref, dwg_hbm, dwu_hbm, dwd_hbm = refs[6 + no:11 + no]
        arrived = refs[11 + no:11 + 2 * no]
        wg_acc, wu_acc, wd_acc, sem = refs[11 + 2 * no:15 + 2 * no]
        c, i = pl.program_id(0), pl.program_id(1)
        if no:
            send_sems, recv_sems = refs[15 + 2 * no:]

            @pl.when((c == 0) & (i == 0))
            def _():
                _scatter_start(sent, arrived, send_sems, recv_sems)

        dyb = dy_ref[...].astype(BF16)
        da = _dot(dyb, wd_ref[...], NT) * FFN_RESID
        dgate = (da * g_ref[...].astype(F32)).astype(BF16)
        dup = (da * u_ref[...].astype(F32)).astype(BF16)
        dg_ref[...] = dgate
        du_ref[...] = dup
        hh = h_ref[...]
        parts = (_dot(hh, dgate, TN), _dot(hh, dup, TN), _dot(a_ref[...], dyb, TN) * FFN_RESID)
        accs = (wg_acc, wu_acc, wd_acc)

        @pl.when(i == 0)
        def _():
            for acc, part in zip(accs, parts):
                acc[...] = part

        @pl.when(i > 0)
        def _():
            for acc, part in zip(accs, parts):
                acc[...] += part

        @pl.when(i == nt - 1)
        def _():
            copies = [pltpu.make_async_copy(acc, out.at[c], sem.at[n])
                      for n, (acc, out) in enumerate(zip(accs, (dwg_hbm, dwu_hbm, dwd_hbm)))]
            for cp in copies:
                cp.start()
            for cp in copies:
                cp.wait()

        if no:
            @pl.when((c == nc - 1) & (i == nt - 1))
            def _():
                _scatter_wait(sent, arrived, send_sems, recv_sems)

    tok = pl.BlockSpec((tm, d), lambda c, i: (i, 0))
    cspec = pl.BlockSpec((None, tm, fs), lambda c, i: (c, i, 0))
    hbm = pl.BlockSpec(memory_space=pltpu.HBM)
    osd = _sds((nc, t, fs), BF16)
    res = _pcall(
        body, name=name, grid=(nc, nt),
        in_specs=[tok, tok, pl.BlockSpec((None, fs, d), lambda c, i: (c, 0, 0)), cspec, cspec, cspec] + [hbm] * no,
        out_specs=(cspec, cspec, hbm, hbm, hbm) + (hbm,) * no,
        out_shape=(osd, osd, _sds((nc, d, fs), F32), _sds((nc, d, fs), F32), _sds((nc, fs, d), F32))
        + _scatter_shapes(outgoing),
        scratch_shapes=[pltpu.VMEM((d, fs), F32), pltpu.VMEM((d, fs), F32), pltpu.VMEM((fs, d), F32),
                        pltpu.SemaphoreType.DMA((3,))] + (_scatter_sems(no) if no else []),
        compiler_params=_cparams("arbitrary", "arbitrary"))(dy, h, wd, dact_dgate, dact_dup, act, *outgoing)
    res = tuple(res)
    return res[:5] + (res[5:],)


def _ffn_fwd(x, g, wg, wu, wd, tag, incoming=()):
    t = x.shape[0]
    nc, _, fs = wg.shape
    tm = min(512, t)
    h = _rms_fwd(x, g, BF16, f"{tag}_norm", 512)
    dact_dgate, dact_dup, act, partly = _ffn_up(h, wg, wu, f"{tag}_up", 1024, incoming)
    pairs = [(act, pl.BlockSpec((None, tm, fs), lambda i, j, r, c=c: (c, i, 0)),
              wd, pl.BlockSpec((None, fs, D_MODEL), lambda i, j, r, c=c: (c, 0, 0))) for c in range(nc)]
    row = pl.BlockSpec((tm, D_MODEL), lambda i, j, r: (i, 0))
    y = _mm(f"{tag}_down", (t // tm, 1, 1), pairs, NN, _sds((t, D_MODEL), F32), row, (tm, D_MODEL),
            res=(x, row), scale=FFN_RESID)
    return y, (h, dact_dgate, dact_dup, act), partly


def _ffn_bwd(dy, x, g, wg, wu, wd, saved, tag, outgoing=()):
    h, dact_dgate, dact_dup, act = saved
    t = x.shape[0]
    nc, _, fs = wg.shape
    tm = min(512, t)
    dgate, dup, dwg, dwu, dwd, arrived = _ffn_hidden_bwd(dy, h, wd, dact_dgate, dact_dup, act,
                                                         f"{tag}_hidden_bwd", 1024, outgoing)
    pairs = []
    for c in range(nc):
        a_spec = pl.BlockSpec((None, tm, fs), lambda i, j, r, c=c: (c, i, 0))
        w_spec = pl.BlockSpec((None, D_MODEL, fs), lambda i, j, r, c=c: (c, 0, 0))
        pairs += [(dgate, a_spec, wg, w_spec), (dup, a_spec, wu, w_spec)]
    dh = _mm(f"{tag}_dh", (t // tm, 1, 1), pairs, NT,
             _sds((t, D_MODEL), F32), pl.BlockSpec((tm, D_MODEL), lambda i, j, r: (i, 0)), (tm, D_MODEL))
    dx, dg = _rms_bwd([dh], x, g, dy, f"{tag}_dnorm", 512)
    return dx, dg, dwg, dwu, dwd, arrived


def _mm_simple(name, a, b, dn, out_dtype, tm=512, tk=512, res=None, scale=1.0):
    if dn == TN:
        k, m = a.shape
        n = b.shape[1]
        tk = min(tk, k)
        return _mm(name, (1, 1, k // tk),
                   [(a, pl.BlockSpec((tk, m), lambda i, j, r: (r, 0)), b, pl.BlockSpec((tk, n), lambda i, j, r: (r, 0)))],
                   TN, _sds((m, n), out_dtype), pl.BlockSpec((m, n), lambda i, j, r: (0, 0)), (m, n), scale=scale)
    m, k = a.shape
    n = b.shape[1] if dn == NN else b.shape[0]
    tm = min(tm, m)
    row = pl.BlockSpec((tm, n), lambda i, j, r: (i, 0))
    return _mm(name, (m // tm, 1, 1),
               [(a, pl.BlockSpec((tm, k), lambda i, j, r: (i, 0)), b, pl.BlockSpec(b.shape, lambda i, j, r: (0, 0)))],
               dn, _sds((m, n), out_dtype), row, (tm, n), res=None if res is None else (res, row), scale=scale)


def _t5_bucket(dist):
    max_exact = REL_BUCKETS // 2
    d = np.maximum(dist, 1).astype(np.float32)
    large = max_exact + (np.log(d / max_exact) / np.log(REL_MAX_DIST / max_exact)
                         * (REL_BUCKETS - max_exact)).astype(np.int32)
    large = np.minimum(large, REL_BUCKETS - 1)
    return np.where(dist < max_exact, dist, large).astype(np.int32)


def _bucket_tiles():
    i = np.arange(QB)[:, None]
    j = np.arange(QB + DIL_W)[None, :]
    delta = np.clip(i + DIL_W - j, 0, None)
    return np.stack([_t5_bucket(delta * dil) for dil in DIL_DILATIONS]).astype(np.int32)


def _bias_tiles(rel_bias):
    buckets = jnp.asarray(_bucket_tiles())

    def body(rb_ref, bk_ref, o_ref):
        bk = bk_ref[...]
        for h in range(DIL_HEADS):
            def pick(b, tile):
                return jnp.where(bk == b, rb_ref[h, b], tile)

            o_ref[h] = lax.fori_loop(0, REL_BUCKETS, pick, jnp.zeros((QB, QB + DIL_W), F32))

    return _pcall(
        body, name="dil_bias_tiles", grid=(3,),
        in_specs=[pl.BlockSpec(memory_space=pltpu.SMEM),
                  pl.BlockSpec((None, QB, QB + DIL_W), lambda b: (b, 0, 0))],
        out_specs=pl.BlockSpec((None, DIL_HEADS, QB, QB + DIL_W), lambda b: (b, 0, 0, 0)),
        out_shape=_sds((3, DIL_HEADS, QB, QB + DIL_W), F32),
        compiler_params=_cparams("parallel"))(rel_bias, buckets)


def _bias_grad(dtiles):
    buckets = jnp.asarray(_bucket_tiles())

    def body(dt_ref, bk_ref, o_ref):
        def one(b, carry):
            hit = [bk_ref[br] == b for br in range(3)]
            for h in range(DIL_HEADS):
                tot = jnp.zeros((), F32)
                for br in range(3):
                    tot = tot + jnp.sum(jnp.where(hit[br], dt_ref[br, h], 0.0))
                o_ref[h, b] = tot
            return carry

        lax.fori_loop(0, REL_BUCKETS, one, 0)

    return _pcall(
        body, name="dil_bias_grad",
        in_specs=[pl.BlockSpec(memory_space=pltpu.VMEM), pl.BlockSpec(memory_space=pltpu.VMEM)],
        out_specs=pl.BlockSpec(memory_space=pltpu.SMEM),
        out_shape=_sds((DIL_HEADS, REL_BUCKETS), F32))(dtiles, buckets)


def _split_heads(a, lo):
    zero = jnp.zeros_like(a)
    return jnp.concatenate([jnp.where(lo, a, zero), jnp.where(lo, zero, a)], axis=0)


def _side_by_side(a):
    n = a.shape[0] // 2
    return jnp.concatenate([a[:n], a[n:]], axis=1)


def _band_masks(prev_ok):
    ii = lax.broadcasted_iota(jnp.int32, (2 * QB, QB), 0) & (QB - 1)
    jj = lax.broadcasted_iota(jnp.int32, (2 * QB, QB), 1)
    return jj <= ii, jj >= ii + jnp.where(prev_ok, 0, QB)


def _dil_fwd(q, k, v, bias, dil, name):
    w = DIL_WIDTH
    t = q.shape[0] * dil
    npair = w // LANES
    nl = t // dil // QB
    scale = DIL_HD ** -0.5

    def body(q_ref, kc_ref, kp_ref, vc_ref, vp_ref, b_ref, o_ref, lse_ref):
        nn = pl.program_id(1)
        lo = lax.broadcasted_iota(jnp.int32, (QB, LANES), 1) < DIL_HD
        lo2 = lax.broadcasted_iota(jnp.int32, (2 * QB, LANES), 1) < DIL_HD
        ii = lax.broadcasted_iota(jnp.int32, (2 * QB, 2 * QB), 0) & (QB - 1)
        jj = lax.broadcasted_iota(jnp.int32, (2 * QB, 2 * QB), 1)
        first_key = jnp.maximum(ii, jnp.where(nn != 0, 0, QB))
        valid = (jj >= first_key) & (jj <= ii + QB)
        for p in range(npair):
            cols = slice(p * LANES, (p + 1) * LANES)
            qq = _split_heads(q_ref[:, cols], lo)
            kk = jnp.concatenate([kp_ref[:, cols], kc_ref[:, cols]], axis=0)
            vv = jnp.concatenate([vp_ref[:, cols], vc_ref[:, cols]], axis=0)
            s = jnp.where(valid, _dot(qq, kk, NT) * scale + b_ref[p], NEG)
            m = jnp.max(s, axis=-1, keepdims=True)
            e = jnp.exp(s - m)
            den = jnp.sum(e, axis=-1, keepdims=True)
            pn = (e * (1.0 / den)).astype(BF16)
            o_ref[:, cols] = _dot(_side_by_side(pn), _split_heads(vv, lo2), NN)
            lse = m + jnp.log(den)
            lse_ref[:, cols] = jnp.where(lo, lse[:QB], lse[QB:])

    cur = pl.BlockSpec((QB, w), lambda r, n: (n, r))
    prev = pl.BlockSpec((QB, w), lambda r, n: (jnp.maximum(n - 1, 0), r))
    sd = _sds((t // dil, dil * w), F32)
    return _pcall(
        body, name=name, grid=(dil, nl),
        in_specs=[cur, cur, prev, cur, prev, pl.BlockSpec((npair, 2 * QB, 2 * QB), lambda r, n: (0, 0, 0))],
        out_specs=(cur, cur), out_shape=(sd, sd),
        compiler_params=_cparams("parallel", "parallel"))(q, k, k, v, v, bias)


def _dil_bwd(q, k, v, do, stats, bias, dil, name):
    w = DIL_WIDTH
    t = q.shape[0] * dil
    npair = w // LANES
    nl = t // dil // QB
    scale = DIL_HD ** -0.5

    def body(qc_ref, qn_ref, doc_ref, don_ref, sc_ref, sn_ref, k_ref, v_ref, b_ref,
             dq_ref, dk_ref, dv_ref, db_ref, carry):
        r, nn = pl.program_id(0), pl.program_id(1)
        lo = lax.broadcasted_iota(jnp.int32, (QB, LANES), 1) < DIL_HD
        cur_ok, prev_ok = _band_masks(nn + 1 < nl)

        @pl.when((r == 0) & (nn == 0))
        def _():
            db_ref[...] = jnp.zeros_like(db_ref)
            carry[...] = jnp.zeros_like(carry)

        for p in range(npair):
            cols = slice(p * LANES, (p + 1) * LANES)
            kp, vp = k_ref[:, cols], v_ref[:, cols]
            k2 = _split_heads(kp, lo)

            def column(ref, lane):
                first = p * LANES + lane
                return jnp.concatenate([ref[:, first:first + 1], ref[:, first + DIL_HD:first + DIL_HD + 1]], axis=0)

            def side(q_ref, do_ref, s_ref, bias, ok):
                qq = _split_heads(q_ref[:, cols], lo)
                dd = _split_heads(do_ref[:, cols], lo)
                s = jnp.where(ok, _dot(qq, kp, NT) * scale + bias, NEG)
                prob = jnp.exp(s - column(s_ref, 0))
                ds = prob * (_dot(dd, vp, NT) - column(s_ref, DIL_HD // 2))
                return qq, dd, prob.astype(BF16), ds

            q1, d1, p1, ds1 = side(qc_ref, doc_ref, sc_ref, b_ref[p, :, QB:], cur_ok)
            q2, d2, p2, ds2 = side(qn_ref, don_ref, sn_ref, b_ref[p, :, :QB], prev_ok)
            ds1b, ds2b = ds1.astype(BF16), ds2.astype(BF16)
            dq_ref[:, cols] = carry[:, cols] + _dot(_side_by_side(ds1b), k2, NN) * scale
            carry[:, cols] = _dot(_side_by_side(ds2b), k2, NN) * scale
            dk_ref[:, cols] = _dot(jnp.concatenate([ds1b, ds2b], axis=0), jnp.concatenate([q1, q2], axis=0), TN) * scale
            dv_ref[:, cols] = _dot(jnp.concatenate([p1, p2], axis=0), jnp.concatenate([d1, d2], axis=0), TN)
            db_ref[p, :, QB:] += ds1
            db_ref[p, :, :QB] += ds2

    cur = pl.BlockSpec((QB, w), lambda r, n: (n, r))
    nxt = pl.BlockSpec((QB, w), lambda r, n: (jnp.minimum(n + 1, nl - 1), r))
    tile = pl.BlockSpec((npair, 2 * QB, 2 * QB), lambda r, n: (0, 0, 0))
    sd = _sds((t // dil, dil * w), F32)
    return _pcall(
        body, name=name, grid=(dil, nl),
        in_specs=[cur, nxt, cur, nxt, cur, nxt, cur, cur, tile],
        out_specs=(cur, cur, cur, tile),
        out_shape=(sd, sd, sd, _sds((npair, 2 * QB, 2 * QB), F32)),
        scratch_shapes=[pltpu.VMEM((QB, w), F32)],
        compiler_params=_cparams("arbitrary", "arbitrary"))(q, q, do, do, stats, stats, k, v, bias)


def _head_sum_matrix(scale):
    idx = np.arange(DIL_WIDTH) // DIL_HD
    return jnp.asarray((idx[:, None] == idx[None, :]).astype(np.float32) * scale, BF16)


def _head_sum(x, mat):
    hi = x.astype(BF16)
    lo = (x - hi.astype(F32)).astype(BF16)
    return _dot(hi, mat, NN) + _dot(lo, mat, NN)


def _to_views(src, tmp, out_refs):
    tm, w = src.shape
    for j in range(w // LANES):
        tmp[j] = src[:, j * LANES:(j + 1) * LANES]
    for d, o_ref in zip(DIL_DILATIONS, out_refs):
        if d == 1:
            o_ref[...] = src.astype(o_ref.dtype)
            continue
        for r in range(d):
            for j in range(w // LANES):
                lo = r * w + j * LANES
                o_ref[:, lo:lo + LANES] = tmp[j, pl.ds(r, tm // d, stride=d), :].astype(o_ref.dtype)


def _from_view(v_ref, tmp, d):
    tm = tmp.shape[1]
    w = v_ref.shape[1] // d
    for r in range(d):
        for j in range(w // LANES):
            lo = r * w + j * LANES
            tmp[j, pl.ds(r, tm // d, stride=d), :] = v_ref[:, lo:lo + LANES]
    return jnp.concatenate([tmp[j] for j in range(w // LANES)], axis=1)


def _view_specs(tm, t, dtype):
    specs = tuple(pl.BlockSpec((tm // d, d * DIL_WIDTH), lambda i: (i, 0)) for d in DIL_DILATIONS)
    shapes = tuple(_sds((t // d, d * DIL_WIDTH), dtype) for d in DIL_DILATIONS)
    return specs, shapes


def _view_scratch(tm):
    return pltpu.VMEM((DIL_WIDTH // LANES, tm, LANES), F32)


def _dil_merge(outs, lses, g, tm):
    w = DIL_WIDTH
    t = outs[0].shape[0]
    tm = min(tm, t)

    def body(o0, o1, o2, l0, l1, l2, g_ref, o_ref, l_ref, n_ref, so1, so2, sl1, sl2):
        d1, d2 = DIL_DILATIONS[1], DIL_DILATIONS[2]
        a0, a1, a2 = l0[...], _from_view(l1, sl1, d1), _from_view(l2, sl2, d2)
        m = jnp.maximum(jnp.maximum(a0, a1), a2)
        e0, e1, e2 = jnp.exp(a0 - m), jnp.exp(a1 - m), jnp.exp(a2 - m)
        den = e0 + e1 + e2
        o = (e0 * o0[...] + e1 * _from_view(o1, so1, d1) + e2 * _from_view(o2, so2, d2)) / den
        o_ref[...] = o
        l_ref[...] = m + jnp.log(den)
        r = lax.rsqrt(jnp.mean(o * o, axis=-1, keepdims=True) + EPS)
        n_ref[...] = (o * r * g_ref[...]).astype(n_ref.dtype)

    specs, _ = _view_specs(tm, t, F32)
    spec = pl.BlockSpec((tm, w), lambda i: (i, 0))
    return _pcall(
        body, name="dil_merge", grid=(t // tm,),
        in_specs=list(specs) * 2 + [pl.BlockSpec((1, w), lambda i: (0, 0))], out_specs=(spec, spec, spec),
        out_shape=(_sds((t, w), F32), _sds((t, w), F32), _sds((t, w), BF16)),
        scratch_shapes=[_view_scratch(tm)] * 4,
        compiler_params=_cparams("parallel"))(*outs, *lses, g)


def _dil_stats(do, o, lse, tm):
    t, w = do.shape
    tm = min(tm, t)

    def body(a_ref, b_ref, l_ref, m_ref, s1, s4, s16, d1, d4, d16, tmp):
        first = (lax.broadcasted_iota(jnp.int32, (tm, w), 1) & (DIL_HD - 1)) < DIL_HD // 2
        do_ = a_ref[...]
        _to_views(jnp.where(first, l_ref[...], _head_sum(do_ * b_ref[...], m_ref[...])), tmp, (s1, s4, s16))
        _to_views(do_, tmp, (d1, d4, d16))

    spec = pl.BlockSpec((tm, w), lambda i: (i, 0))
    f_specs, f_shapes = _view_specs(tm, t, F32)
    b_specs, b_shapes = _view_specs(tm, t, BF16)
    res = _pcall(body, name="dil_stats", grid=(t // tm,),
                 in_specs=[spec, spec, spec, pl.BlockSpec((w, w), lambda i: (0, 0))],
                 out_specs=f_specs + b_specs, out_shape=f_shapes + b_shapes,
                 scratch_shapes=[_view_scratch(tm)],
                 compiler_params=_cparams("parallel"))(do, o, lse, _head_sum_matrix(1.0))
    return res[:3], res[3:]


def _head_norm_fwd(x, col, g, name, tm):
    t = x.shape[0]
    w = DIL_WIDTH
    tm = min(tm, t)
    normed = g is not None

    def body(*refs):
        outs, tmp = refs[-4:-1], refs[-1]
        xf = refs[0][...]
        if normed:
            g_ref, m_ref = refs[1], refs[2]
            xf = xf * lax.rsqrt(_head_sum(xf * xf, m_ref[...]) + EPS) * g_ref[...]
        _to_views(xf, tmp, outs)

    specs, shapes = _view_specs(tm, t, BF16)
    extra = [g, _head_sum_matrix(1.0 / DIL_HD)] if normed else []
    extra_specs = [pl.BlockSpec((1, w), lambda i: (0, 0)), pl.BlockSpec((w, w), lambda i: (0, 0))] if normed else []
    return _pcall(
        body, name=name, grid=(t // tm,),
        in_specs=[pl.BlockSpec((tm, w), lambda i: (i, col))] + extra_specs,
        out_specs=specs, out_shape=shapes, scratch_shapes=[_view_scratch(tm)],
        compiler_params=_cparams("parallel"))(x, *extra)


def _head_norm_bwd(dys, x, col, g, name, tm):
    t = x.shape[0]
    w = DIL_WIDTH
    tm = min(tm, t)
    nd = len(dys)
    nt = t // tm
    lane = np.arange(w) % DIL_HD
    fold = jnp.asarray((lane[:, None] == lane[None, :]).astype(np.float32))

    def body(*refs):
        x_ref, g_ref, m_ref, f_ref = refs[nd:nd + 4]
        dx_ref, dg_ref, s1, s2 = refs[-4:]
        dy = refs[0][...] + _from_view(refs[1], s1, DIL_DILATIONS[1]) + _from_view(refs[2], s2, DIL_DILATIONS[2])
        xf = x_ref[...]
        mat = m_ref[...]
        r = lax.rsqrt(_head_sum(xf * xf, mat) + EPS)
        xh = xf * r
        dxh = dy * g_ref[...]
        dx_ref[...] = r * (dxh - xh * _head_sum(dxh * xh, mat))

        @pl.when(pl.program_id(0) == 0)
        def _():
            dg_ref[...] = jnp.zeros_like(dg_ref)

        dg_ref[...] += jnp.sum(dy * xh, axis=0, keepdims=True)

        @pl.when(pl.program_id(0) == nt - 1)
        def _():
            per_lane = jnp.broadcast_to(dg_ref[...], (8, w))
            dg_ref[...] = lax.dot_general(per_lane, f_ref[...], NN, precision=lax.Precision.HIGHEST,
                                          preferred_element_type=F32)[0:1]

    row = pl.BlockSpec((tm, w), lambda i: (i, 0))
    vec = pl.BlockSpec((1, w), lambda i: (0, 0))
    sq = pl.BlockSpec((w, w), lambda i: (0, 0))
    views, _ = _view_specs(tm, t, F32)
    return _pcall(
        body, name=name, grid=(nt,),
        in_specs=list(views) + [pl.BlockSpec((tm, w), lambda i: (i, col)), vec, sq, sq],
        out_specs=(row, vec), out_shape=(_sds((t, w), F32), _sds((1, w), F32)),
        scratch_shapes=[_view_scratch(tm)] * 2,
        compiler_params=_cparams("arbitrary"))(*dys, x, g, _head_sum_matrix(1.0 / DIL_HD), fold)


def _rowdot(a, b, name, tm):
    n, d = a.shape
    tm = min(tm, n)

    def body(a_ref, b_ref, o_ref):
        o_ref[...] = jnp.sum(a_ref[...].astype(F32) * b_ref[...].astype(F32), axis=-1, keepdims=True)

    spec = pl.BlockSpec((tm, d), lambda i: (i, 0))
    return _pcall(body, name=name, grid=(n // tm,), in_specs=[spec, spec],
                  out_specs=pl.BlockSpec((tm, 1), lambda i: (i, 0)), out_shape=_sds((n, 1), F32),
                  compiler_params=_cparams("parallel"))(a, b)


def _sum_branches(parts, name, tm):
    t = parts[0].shape[0]
    w = DIL_WIDTH
    tm = min(tm, t)

    def body(a_ref, b_ref, c_ref, o_ref, s1, s2):
        o_ref[...] = a_ref[...] + _from_view(b_ref, s1, DIL_DILATIONS[1]) + _from_view(c_ref, s2, DIL_DILATIONS[2])

    views, _ = _view_specs(tm, t, F32)
    return _pcall(body, name=name, grid=(t // tm,), in_specs=list(views),
                  out_specs=pl.BlockSpec((tm, w), lambda i: (i, 0)), out_shape=_sds((t, w), F32),
                  scratch_shapes=[_view_scratch(tm)] * 2,
                  compiler_params=_cparams("parallel"))(*parts)


def _rope_tables(t):
    inv = ROPE_BASE ** (-np.arange(0, MLA_ROPE, 2, dtype=np.float64) / MLA_ROPE)
    ang = np.arange(t, dtype=np.float64)[:, None] * inv[None, :]
    cos, sin = np.cos(ang), np.sin(ang)
    return (jnp.asarray(np.concatenate([cos, cos], 1), F32), jnp.asarray(np.concatenate([-sin, sin], 1), F32))


def _half_swap():
    p = np.zeros((MLA_ROPE, MLA_ROPE), np.float32)
    half = MLA_ROPE // 2
    for i in range(MLA_ROPE):
        p[(i + half) % MLA_ROPE, i] = 1.0
    return jnp.asarray(p)


def _mla_qk_fwd(x, g, cos_t, sin_t, scale, name, tm):
    n, d = x.shape
    t = cos_t.shape[0]
    tm = min(tm, t)
    nt = t // tm
    swap = _half_swap()

    def body(x_ref, g_ref, c_ref, s_ref, p_ref, o_ref):
        xf = x_ref[...]
        r = lax.rsqrt(jnp.mean(xf * xf, axis=-1, keepdims=True) + EPS)
        y = xf * r * g_ref[...]
        yr = y[:, MLA_NOPE:]
        sw = lax.dot_general(yr, p_ref[...], NN, precision=lax.Precision.HIGHEST, preferred_element_type=F32)
        o_ref[:, :MLA_NOPE] = (y[:, :MLA_NOPE] * scale).astype(o_ref.dtype)
        o_ref[:, MLA_NOPE:] = ((yr * c_ref[...] + sw * s_ref[...]) * scale).astype(o_ref.dtype)

    row = pl.BlockSpec((tm, d), lambda i: (i, 0))
    tab = pl.BlockSpec((tm, MLA_ROPE), lambda i: (i % nt, 0))
    return _pcall(
        body, name=name, grid=(n // tm,),
        in_specs=[row, pl.BlockSpec((1, d), lambda i: (0, 0)), tab, tab,
                  pl.BlockSpec((MLA_ROPE, MLA_ROPE), lambda i: (0, 0))],
        out_specs=row, out_shape=_sds((n, d), BF16),
        compiler_params=_cparams("parallel"))(x, g, cos_t, sin_t, swap)


def _mla_qk_bwd(dy, x, g, cos_t, sin_t, scale, name, tm):
    n, d = x.shape
    t = cos_t.shape[0]
    tm = min(tm, t)
    nt = t // tm
    swap_t = _half_swap().T

    def body(dy_ref, x_ref, g_ref, c_ref, s_ref, p_ref, dx_ref, dg_ref):
        xf = x_ref[...]
        gg = g_ref[...]
        r = lax.rsqrt(jnp.mean(xf * xf, axis=-1, keepdims=True) + EPS)
        xh = xf * r
        dyf = dy_ref[...] * scale
        dyr = dyf[:, MLA_NOPE:]
        back = lax.dot_general(dyr * s_ref[...], p_ref[...], NN, precision=lax.Precision.HIGHEST,
                               preferred_element_type=F32)
        dn_n = dyf[:, :MLA_NOPE]
        dn_r = dyr * c_ref[...] + back
        xh_n, xh_r = xh[:, :MLA_NOPE], xh[:, MLA_NOPE:]
        dxh_n = dn_n * gg[:, :MLA_NOPE]
        dxh_r = dn_r * gg[:, MLA_NOPE:]
        mean = (jnp.sum(dxh_n * xh_n, axis=-1, keepdims=True)
                + jnp.sum(dxh_r * xh_r, axis=-1, keepdims=True)) * (1.0 / d)
        dx_ref[:, :MLA_NOPE] = r * (dxh_n - xh_n * mean)
        dx_ref[:, MLA_NOPE:] = r * (dxh_r - xh_r * mean)

        @pl.when(pl.program_id(0) == 0)
        def _():
            dg_ref[...] = jnp.zeros_like(dg_ref)

        dg_ref[:, :MLA_NOPE] += jnp.sum(dn_n * xh_n, axis=0, keepdims=True)
        dg_ref[:, MLA_NOPE:] += jnp.sum(dn_r * xh_r, axis=0, keepdims=True)

    row = pl.BlockSpec((tm, d), lambda i: (i, 0))
    vec = pl.BlockSpec((1, d), lambda i: (0, 0))
    tab = pl.BlockSpec((tm, MLA_ROPE), lambda i: (i % nt, 0))
    return _pcall(
        body, name=name, grid=(n // tm,),
        in_specs=[row, row, vec, tab, tab, pl.BlockSpec((MLA_ROPE, MLA_ROPE), lambda i: (0, 0))],
        out_specs=(row, vec), out_shape=(_sds((n, d), F32), _sds((1, d), F32)),
        compiler_params=_cparams("arbitrary"))(dy, x, g, cos_t, sin_t, swap_t)


def _causal_mask(i, j, tq, tk, width):
    row = i * tq + lax.broadcasted_iota(jnp.int32, (tq, width), 0)
    col = j * tk + lax.broadcasted_iota(jnp.int32, (tq, width), 1)
    return col <= row


def _causal_steps(nq, nk, tq, tk, q_major):
    if q_major:
        groups = [[(i, j) for j in range((i * tq + tq - 1) // tk + 1)] for i in range(nq)]
        nunit = tk // tq if tk % tq == 0 else 1
    else:
        groups = [[(i, j) for i in range((j * tk) // tq, nq)] for j in range(nk)]
        nunit = tq // tk if tq % tk == 0 else 1
    it, jt, fl = [], [], []
    for g in groups:
        for n, (i, j) in enumerate(g):
            crossing = j * tk + tk - 1 > i * tq
            if q_major:
                unit = tk // nunit
                u = min(nunit, -(-(i * tq + tq - j * tk) // unit)) - 1
            else:
                unit = tq // nunit
                u = max(0, j * tk - i * tq) // unit
            it.append(i)
            jt.append(j)
            fl.append((n == 0) + 2 * (n == len(g) - 1) + 4 * crossing + 8 * (u if crossing else 0))
    return tuple(jnp.asarray(np.array(a, np.int32)) for a in (it, jt, fl)), nunit


def _by_crossing(flags, nunit, update):
    pl.when((flags & 4) == 0)(functools.partial(update, None))
    for u in range(nunit):
        pl.when(((flags & 4) != 0) & ((flags >> 3) == u))(functools.partial(update, u))


def _causal_specs(tq, tk):
    def qs(w):
        return pl.BlockSpec((None, tq, w), lambda h, s, it, jt, fl: (h, it[s], 0))

    def kv(w):
        return pl.BlockSpec((None, tk, w), lambda h, s, it, jt, fl: (h, jt[s], 0))

    return qs, kv


def _mla_fwd(q, k, v, tq, tk):
    nh, t, dq = q.shape
    dv = v.shape[2]
    tq, tk = min(tq, t), min(tk, t)
    tables, nunit = _causal_steps(t // tq, t // tk, tq, tk, True)

    def body(it, jt, fl, q_ref, k_ref, v_ref, o_ref, lse_ref, m_sc, l_sc, acc_sc):
        step = pl.program_id(1)
        i, j, flags = it[step], jt[step], fl[step]

        @pl.when((flags & 1) != 0)
        def _():
            m_sc[...] = jnp.full_like(m_sc, NEG)
            l_sc[...] = jnp.zeros_like(l_sc)
            acc_sc[...] = jnp.zeros_like(acc_sc)

        def update(units):
            wk = tk if units is None else (units + 1) * (tk // nunit)
            s = _dot(q_ref[...], k_ref[:wk, :], NT)
            if units is not None:
                s = jnp.where(_causal_mask(i, j, tq, tk, wk), s, NEG)
            m_prev = m_sc[...]
            m_new = jnp.maximum(m_prev, jnp.max(s, axis=-1, keepdims=True))
            alpha = jnp.exp(m_prev - m_new)
            p = jnp.exp(s - m_new)
            l_sc[...] = alpha * l_sc[...] + jnp.sum(p, axis=-1, keepdims=True)
            acc_sc[...] = alpha * acc_sc[...] + _dot(p.astype(BF16), v_ref[:wk, :], NN)
            m_sc[...] = m_new

        _by_crossing(flags, nunit, update)

        @pl.when((flags & 2) != 0)
        def _():
            o_ref[...] = acc_sc[...] / l_sc[...]
            lse_ref[...] = m_sc[...] + jnp.log(l_sc[...])

    qs, kv = _causal_specs(tq, tk)
    return _pcall(
        body, name="mla_attn_fwd",
        grid_spec=pltpu.PrefetchScalarGridSpec(
            num_scalar_prefetch=3, grid=(nh, tables[0].shape[0]),
            in_specs=[qs(dq), kv(dq), kv(dv)], out_specs=(qs(dv), qs(1)),
            scratch_shapes=[pltpu.VMEM((tq, 1), F32), pltpu.VMEM((tq, 1), F32), pltpu.VMEM((tq, dv), F32)]),
        out_shape=(_sds((nh, t, dv), F32), _sds((nh, t, 1), F32)),
        compiler_params=_cparams("parallel", "arbitrary"))(*tables, q, k, v)


def _mla_bwd(q, k, k_t, v, do, lse_row, dl_row, tq, tk):
    nh, t, dq = q.shape
    dv = v.shape[2]
    tq, tk = min(tq, t), min(tk, t)
    nq = t // tq
    tables, nunit = _causal_steps(nq, t // tk, tq, tk, False)

    def body(it, jt, fl, q_ref, k_ref, kt_ref, v_ref, do_ref, lse_ref, dl_ref, dk_ref, dv_ref, dq_ref, dk_sc, dv_sc):
        step = pl.program_id(1)
        i, j, flags = it[step], jt[step], fl[step]

        def update(units):
            off = 0 if units is None else units * (tq // nunit)
            qq = q_ref[off:, :]
            st = _dot(k_ref[...], qq, NT)
            if units is not None:
                key = j * tk + lax.broadcasted_iota(jnp.int32, (tk, tq - off), 0)
                qry = i * tq + off + lax.broadcasted_iota(jnp.int32, (tk, tq - off), 1)
                st = jnp.where(key <= qry, st, NEG)
            pt = jnp.exp(st - lse_ref[:, off:])
            dob = do_ref[off:, :].astype(BF16)
            dpt = _dot(v_ref[...], dob, NT)
            dst = pt * (dpt - dl_ref[:, off:])
            dsb = dst.astype(BF16)
            dv_part = _dot(pt.astype(BF16), dob, NN)
            dk_part = _dot(dsb, qq, NN)
            dq_part = _dot(kt_ref[...], dsb, NN)

            @pl.when((flags & 1) != 0)
            def _():
                dv_sc[...] = dv_part
                dk_sc[...] = dk_part

            @pl.when((flags & 1) == 0)
            def _():
                dv_sc[...] += dv_part
                dk_sc[...] += dk_part

            if off == 0:
                @pl.when(j == 0)
                def _():
                    dq_ref[i] = dq_part

                @pl.when(j != 0)
                def _():
                    dq_ref[i] += dq_part
            else:
                dq_ref[i, :, off:] += dq_part

        _by_crossing(flags, nunit, update)

        @pl.when((flags & 2) != 0)
        def _():
            dk_ref[...] = dk_sc[...]
            dv_ref[...] = dv_sc[...]

    qs, kv = _causal_specs(tq, tk)
    rowv = pl.BlockSpec((None, 1, tq), lambda h, s, it, jt, fl: (h, 0, it[s]))
    ktv = pl.BlockSpec((None, dq, tk), lambda h, s, it, jt, fl: (h, 0, jt[s]))
    whole = pl.BlockSpec((None, nq, dq, tq), lambda h, s, it, jt, fl: (h, 0, 0, 0))
    return _pcall(
        body, name="mla_attn_bwd",
        grid_spec=pltpu.PrefetchScalarGridSpec(
            num_scalar_prefetch=3, grid=(nh, tables[0].shape[0]),
            in_specs=[qs(dq), kv(dq), ktv, kv(dv), qs(dv), rowv, rowv], out_specs=(kv(dq), kv(dv), whole),
            scratch_shapes=[pltpu.VMEM((tk, dq), F32), pltpu.VMEM((tk, dv), F32)]),
        out_shape=(_sds((nh, t, dq), F32), _sds((nh, t, dv), F32), _sds((nh, nq, dq, tq), F32)),
        compiler_params=_cparams("parallel", "arbitrary"))(*tables, q, k, k_t, v, do, lse_row, dl_row)


def _loss_head(y, target, tm):
    t, d = y.shape
    tm = min(tm, t)
    nt = t // tm

    def body(y_ref, t_ref, dy_ref, loss_ref, acc):
        i = pl.program_id(0)
        err = y_ref[...] - t_ref[...]
        dy_ref[...] = err * (1.0 / d)

        @pl.when(i == 0)
        def _():
            acc[...] = jnp.zeros_like(acc)

        acc[...] += jnp.sum(err * err, axis=0, keepdims=True)

        @pl.when(i == nt - 1)
        def _():
            loss_ref[0, 0] = jnp.sum(acc[...]) * (0.5 / d)

    spec = pl.BlockSpec((tm, d), lambda i: (i, 0))
    return _pcall(
        body, name="loss_head", grid=(nt,), in_specs=[spec, spec],
        out_specs=(spec, pl.BlockSpec(memory_space=pltpu.SMEM)),
        out_shape=(_sds((t, d), F32), _sds((1, 1), F32)),
        scratch_shapes=[pltpu.VMEM((1, d), F32)],
        compiler_params=_cparams("arbitrary"))(y, target)


def _adamw(w, g, m, v, name):
    r, c = w.shape
    tr = r
    for cand in (256, 128, 64, 32, 16, 8):
        if r % cand == 0:
            tr = cand
            break

    def body(w_ref, g_ref, m_ref, v_ref, d_ref, nm_ref, nv_ref):
        gg = g_ref[...]
        nm = ADAM_B1 * m_ref[...] + (1.0 - ADAM_B1) * gg
        nv = ADAM_B2 * v_ref[...] + (1.0 - ADAM_B2) * (gg * gg)
        m_hat = nm / (1.0 - ADAM_B1 ** ADAM_STEP)
        v_hat = nv / (1.0 - ADAM_B2 ** ADAM_STEP)
        d_ref[...] = -ADAM_LR * (m_hat / (jnp.sqrt(v_hat) + ADAM_EPS) + ADAM_WD * w_ref[...])
        nm_ref[...] = nm
        nv_ref[...] = nv

    spec = pl.BlockSpec((tr, c), lambda i: (i, 0))
    sd = _sds((r, c), F32)
    return _pcall(body, name=name, grid=(r // tr,), in_specs=[spec] * 4, out_specs=(spec,) * 3,
                  out_shape=(sd, sd, sd), compiler_params=_cparams("parallel"))(w, g, m, v)


MESH_ID = pl.DeviceIdType.MESH
HBM_SPEC = pl.BlockSpec(memory_space=pltpu.HBM)


def _place():
    return lax.axis_index("x"), lax.axis_index("y"), lax.axis_index("c")


def _other_chips(x, y):
    return [(1 - x, y), (x, 1 - y), (1 - x, 1 - y)]


def _remote(src, dst, send_sems, recv_sems, k, to):
    return pltpu.make_async_remote_copy(src_ref=src, dst_ref=dst, send_sem=send_sems.at[k], recv_sem=recv_sems.at[k],
                                        device_id=to, device_id_type=MESH_ID)


def _halves(arrays):
    for a in arrays:
        assert a.shape[-2] % 32 == 0
    return [a.shape[-2] // 2 for a in arrays]


def _gather_start(srcs, outs, halves, send_sems, recv_sems):
    x, y, c = _place()
    for a, half in enumerate(halves):
        rows = pl.ds(c * half, half)
        for k, (cx, cy) in enumerate(_other_chips(x, y)):
            _remote(srcs[a].at[rows, :], outs[a].at[2 * x + y, rows, :], send_sems, recv_sems, 3 * a + k,
                    (cx, cy, c)).start()


def _gather_wait(outs, halves, send_sems, recv_sems):
    x, y, c = _place()
    for a, half in enumerate(halves):
        for k, (cx, cy) in enumerate(_other_chips(x, y)):
            got = outs[a].at[2 * cx + cy, pl.ds(c * half, half), :]
            _remote(got, got, send_sems, recv_sems, 3 * a + k, (x, y, c)).wait()


def _forward_cores(partly):
    n = len(partly)
    halves = _halves(partly)

    def body(*refs):
        srcs, outs, send_sems, recv_sems = refs[:n], refs[n:2 * n], refs[2 * n], refs[2 * n + 1]
        x, y, c = _place()
        for a, half in enumerate(halves):
            for k, (cx, cy) in enumerate(_other_chips(x, y)):
                rows = pl.ds(c * half, half)
                _remote(srcs[a].at[2 * cx + cy, rows, :], outs[a].at[2 * cx + cy, rows, :], send_sems, recv_sems,
                        3 * a + k, (x, y, 1 - c)).start()
        for a, half in enumerate(halves):
            for k, (cx, cy) in enumerate(_other_chips(x, y)):
                mine = outs[a].at[2 * cx + cy, pl.ds(c * half, half), :]
                theirs = outs[a].at[2 * cx + cy, pl.ds((1 - c) * half, half), :]
                _remote(mine, theirs, send_sems, recv_sems, 3 * a + k, (x, y, c)).wait()

    return _pcall(
        body, name="forward_cores", in_specs=[HBM_SPEC] * n, out_specs=tuple([HBM_SPEC] * n),
        out_shape=tuple(_sds(p.shape, p.dtype) for p in partly), input_output_aliases={a: a for a in range(n)},
        scratch_shapes=[pltpu.SemaphoreType.DMA((3 * n,)), pltpu.SemaphoreType.DMA((3 * n,))],
    )(*partly)


def _gather_weights(blocks):
    n = len(blocks)
    halves = _halves(blocks)

    def body(*refs):
        srcs, outs, send_sems, recv_sems = refs[:n], refs[n:2 * n], refs[2 * n], refs[2 * n + 1]
        x, y, c = _place()
        me = 2 * x + y
        sibling = (x, y, 1 - c)
        chips = _other_chips(x, y)

        def part(a, chip, core):
            return outs[a].at[chip, pl.ds(core * halves[a], halves[a]), :]

        for a in range(n):
            mine = srcs[a].at[pl.ds(c * halves[a], halves[a]), :]
            for k, (cx, cy) in enumerate(chips):
                _remote(mine, part(a, me, c), send_sems, recv_sems, 6 * a + k, (cx, cy, c)).start()
        for k, (cx, cy) in enumerate(chips):
            for a in range(n):
                got = part(a, 2 * cx + cy, c)
                _remote(got, got, send_sems, recv_sems, 6 * a + k, (x, y, c)).wait_recv()
                _remote(got, got, send_sems, recv_sems, 6 * a + 3 + k, sibling).start()
        for k, (cx, cy) in enumerate(chips):
            for a in range(n):
                got = part(a, 2 * cx + cy, 1 - c)
                _remote(got, got, send_sems, recv_sems, 6 * a + 3 + k, (x, y, c)).wait_recv()
        for a in range(n):
            sent = part(a, me, c)
            for k in range(6):
                _remote(sent, sent, send_sems, recv_sems, 6 * a + k, (x, y, c)).wait_send()

    return _pcall(
        body, name="gather_weights", in_specs=[HBM_SPEC] * n, out_specs=tuple([HBM_SPEC] * n),
        out_shape=tuple(_sds((N_CHIPS,) + b.shape, b.dtype) for b in blocks),
        scratch_shapes=[pltpu.SemaphoreType.DMA((6 * n,)), pltpu.SemaphoreType.DMA((6 * n,))],
    )(*blocks)


def _reduce_cores(grads, tag):
    n = len(grads)
    halves = _halves(grads)

    def body(*refs):
        gs, outs, send_sems, recv_sems = refs[:n], refs[n:2 * n], refs[2 * n], refs[2 * n + 1]
        x, y, c = _place()
        for a in range(n):
            for j in range(N_CHIPS):
                _remote(gs[a].at[j, pl.ds((1 - c) * halves[a], halves[a]), :], outs[a].at[j],
                        send_sems, recv_sems, a, (x, y, 1 - c)).start()
        for a in range(n):
            _remote(gs[a].at[:, pl.ds((1 - c) * halves[a], halves[a]), :], outs[a],
                    send_sems, recv_sems, a, (x, y, c)).wait()

    return _pcall(
        body, name=f"reduce_cores_{tag}", in_specs=[HBM_SPEC] * n, out_specs=tuple([HBM_SPEC] * n),
        out_shape=tuple(_sds((N_CHIPS, h, g.shape[2]), g.dtype) for g, h in zip(grads, halves)),
        scratch_shapes=[pltpu.SemaphoreType.DMA((n,)), pltpu.SemaphoreType.DMA((n,))],
    )(*grads)


def _scatter_chips(parts):
    n = len(parts)

    def body(*refs):
        ps, outs, send_sems, recv_sems = refs[:n], refs[n:2 * n], refs[2 * n], refs[2 * n + 1]
        _scatter_start(ps, outs, send_sems, recv_sems)
        _scatter_wait(ps, outs, send_sems, recv_sems)

    return _pcall(
        body, name="scatter_chips", in_specs=[HBM_SPEC] * n, out_specs=tuple([HBM_SPEC] * n),
        out_shape=_scatter_shapes(parts), scratch_shapes=_scatter_sems(n),
    )(*parts)


def _scatter_shapes(parts):
    return tuple(_sds((3,) + p.shape[1:], p.dtype) for p in parts)


def _scatter_sems(n):
    return [pltpu.SemaphoreType.DMA((3 * n,)), pltpu.SemaphoreType.DMA((3 * n,))]


def _scatter_start(ps, outs, send_sems, recv_sems):
    x, y, c = _place()
    for a in range(len(ps)):
        for k, (cx, cy) in enumerate(_other_chips(x, y)):
            _remote(ps[a].at[2 * cx + cy], outs[a].at[k], send_sems, recv_sems, 3 * a + k, (cx, cy, c)).start()


def _scatter_wait(ps, outs, send_sems, recv_sems):
    x, y, c = _place()
    for a in range(len(ps)):
        for k in range(3):
            _remote(ps[a].at[k], outs[a].at[k], send_sems, recv_sems, 3 * a + k, (x, y, c)).wait()


def _sum_partials(received, parts, place):
    n = len(parts)
    steps = 2
    tiles = [p.shape[1] // steps for p in parts]

    def body(place_ref, *refs):
        rs, ps, outs = refs[:n], refs[n:2 * n], refs[2 * n:]
        for a in range(n):
            tot = ps[a][...].astype(F32)
            for k in range(3):
                tot = tot + rs[a][k].astype(F32)
            outs[a][...] = tot

    cols = [p.shape[2] for p in parts]
    return _pcall(
        body, name="sum_chip_partials",
        grid_spec=pltpu.PrefetchScalarGridSpec(
            num_scalar_prefetch=1, grid=(steps,),
            in_specs=[pl.BlockSpec((3, tm, w), lambda i, pc: (0, i, 0)) for tm, w in zip(tiles, cols)]
            + [pl.BlockSpec((None, tm, w), lambda i, pc: (pc[0], i, 0)) for tm, w in zip(tiles, cols)],
            out_specs=tuple(pl.BlockSpec((tm, w), lambda i, pc: (pc[1] * steps + i, 0)) for tm, w in zip(tiles, cols))),
        out_shape=tuple(_sds((2 * p.shape[1], p.shape[2]), F32) for p in parts),
        compiler_params=_cparams("parallel"))(place, *received, *parts)


def _share_cores(blocks):
    n = len(blocks)
    halves = _halves(blocks)

    def body(*refs):
        srcs, outs, send_sems, recv_sems = refs[:n], refs[n:2 * n], refs[2 * n], refs[2 * n + 1]
        x, y, c = _place()
        for a in range(n):
            piece = pl.ds(c * halves[a], halves[a])
            _remote(srcs[a].at[piece, :], outs[a].at[piece, :], send_sems, recv_sems, a, (x, y, 1 - c)).start()
        for a in range(n):
            mine = outs[a].at[pl.ds(c * halves[a], halves[a]), :]
            theirs = outs[a].at[pl.ds((1 - c) * halves[a], halves[a]), :]
            _remote(mine, theirs, send_sems, recv_sems, a, (x, y, c)).wait()

    return _pcall(
        body, name="share_cores", in_specs=[HBM_SPEC] * n, out_specs=tuple([HBM_SPEC] * n),
        out_shape=tuple(_sds(b.shape, b.dtype) for b in blocks), input_output_aliases={a: a for a in range(n)},
        scratch_shapes=[pltpu.SemaphoreType.DMA((n,)), pltpu.SemaphoreType.DMA((n,))],
    )(*blocks)


def _sum_blocks(stacked, name, tm):
    n, rows, lanes = stacked.shape
    tm = min(tm, rows)

    def body(s_ref, o_ref):
        tot = s_ref[n - 1].astype(F32)
        for k in range(n - 1):
            tot = tot + s_ref[k].astype(F32)
        o_ref[...] = tot

    return _pcall(body, name=name, grid=(rows // tm,),
                  in_specs=[pl.BlockSpec((n, tm, lanes), lambda i: (0, i, 0))],
                  out_specs=pl.BlockSpec((tm, lanes), lambda i: (i, 0)), out_shape=_sds((rows, lanes), F32),
                  compiler_params=_cparams("parallel"))(stacked)


def _add_halves(grads, theirs, core, tag):
    n = len(grads)
    steps = 2
    tiles = [t.shape[1] // steps for t in theirs]
    cols = [t.shape[2] for t in theirs]

    def body(c_ref, *refs):
        gs, ts, outs = refs[:n], refs[n:2 * n], refs[2 * n:]
        for a in range(n):
            outs[a][...] = (gs[a][...] + ts[a][...]).astype(BF16)

    own = [pl.BlockSpec((None, tm, w), lambda k, i, c: (k, c[0] * steps + i, 0)) for tm, w in zip(tiles, cols)]
    same = [pl.BlockSpec((None, tm, w), lambda k, i, c: (k, i, 0)) for tm, w in zip(tiles, cols)]
    return _pcall(
        body, name=f"add_core_halves_{tag}",
        grid_spec=pltpu.PrefetchScalarGridSpec(
            num_scalar_prefetch=1, grid=(N_CHIPS, steps), in_specs=own + same, out_specs=tuple(same)),
        out_shape=tuple(_sds(t.shape, BF16) for t in theirs),
        compiler_params=_cparams("parallel", "parallel"))(core, *grads, *theirs)


def _allreduce_small(part):
    rows, lanes = part.shape
    ndev = 8

    def body(src, tot, buf, send_sems, recv_sems):
        x, y, c = _place()
        me = 4 * x + 2 * y + c
        buf[me] = src[...]
        sends = []
        for k in range(1, ndev):
            peer = (x ^ (k >> 2), y ^ ((k >> 1) & 1), c ^ (k & 1))
            cp = _remote(src, buf.at[me], send_sems, recv_sems, k - 1, peer)
            cp.start()
            sends.append(cp)
        for k in range(1, ndev):
            theirs = buf.at[me ^ k]
            _remote(theirs, theirs, send_sems, recv_sems, k - 1, (x, y, c)).wait_recv()
        for cp in sends:
            cp.wait_send()
        acc = buf[0]
        for d in range(1, ndev):
            acc = acc + buf[d]
        tot[...] = acc

    vm = pl.BlockSpec(memory_space=pltpu.VMEM)
    return _pcall(
        body, name="allreduce_small", in_specs=[vm], out_specs=vm, out_shape=_sds((rows, lanes), F32),
        scratch_shapes=[pltpu.VMEM((ndev, rows, lanes), F32), pltpu.SemaphoreType.DMA((ndev - 1,)),
                        pltpu.SemaphoreType.DMA((ndev - 1,))],
    )(part)


def _pack_small(vals):
    parts = []
    for name, shape, r in SMALL:
        flat = vals[name].reshape(-1).astype(F32)
        parts.append(jnp.pad(flat, (0, r * LANES - flat.shape[0])).reshape(r, LANES))
    used = sum(r for _, _, r in SMALL)
    parts.append(jnp.zeros((SMALL_ROWS - used, LANES), F32))
    return jnp.concatenate(parts, axis=0)


def _unpack_small(packed):
    out, off = {}, 0
    for name, shape, r in SMALL:
        n = int(np.prod(shape))
        out[name] = packed[off:off + r].reshape(-1)[:n].reshape(shape)
        off += r
    return out


def _heads_major(a, nh):
    t = a.shape[0]
    return a.reshape(t, nh, a.shape[1] // nh).transpose(1, 0, 2)


def _tokens_major(a):
    nh, t, w = a.shape
    return a.transpose(1, 0, 2).reshape(t, nh * w)


LATE = ("ffn1_w_gate", "ffn1_w_up", "ffn1_w_down")
EARLY = tuple(name for name, _ in BIG if name not in LATE)


def _local_step(x, target, small, wfull, early_exchange=None, later_weights=None):
    t = x.shape[0]
    nh, hd = DIL_HEADS, DIL_HD
    grads_s, grads_b = {}, {}

    x1, ffn1_saved, partly = _ffn_fwd(x, small["ffn1_norm"], wfull["ffn1_w_gate"], wfull["ffn1_w_up"],
                                      wfull["ffn1_w_down"], "ffn1", later_weights[0] if later_weights else ())
    if later_weights:
        wfull = {**wfull, **later_weights[1](partly)}
    w_in = wfull["w_in"].transpose(1, 0, 2).reshape(D_MODEL, -1)
    w_out = wfull["w_out"].reshape(D_MODEL, D_MODEL)
    w_qb, w_kvb = wfull["mla_w_q_b"], wfull["mla_w_kv_b"]
    hm = _rms_fwd(x1, small["mix_norm"], BF16, "mix_norm", 512)
    proj = _mm_simple("in_proj", hm, w_in, NN, F32, tm=1024)
    cq, ckv, k_pe = proj[:, 1536:1792], proj[:, 1792:1920], proj[:, 1920:1984]

    gq, gk = jnp.tile(small["dil_q_norm"], (1, nh)), jnp.tile(small["dil_k_norm"], (1, nh))
    qn = _head_norm_fwd(proj, 0, gq, "dil_q_norm", 512)
    kn = _head_norm_fwd(proj, 1, gk, "dil_k_norm", 512)
    v_d = _head_norm_fwd(proj, 2, None, "dil_v_views", 512)
    bias = _bias_tiles(small["rel_bias"]).reshape(3, nh // 2, 2 * QB, QB + DIL_W)
    outs, lses = [], []
    for b, dil in enumerate(DIL_DILATIONS):
        o_b, lse_b = _dil_fwd(qn[b], kn[b], v_d[b], bias[b], dil, f"dil_fwd_{dil}")
        outs.append(o_b)
        lses.append(lse_b)
    o_dil, lse_tot, od = _dil_merge(outs, lses, small["out_norm_dil"], 512)

    mh = MLA_HEADS
    cos_t, sin_t = _rope_tables(t)
    cqn = _rms_fwd(cq, small["mla_q_a_norm"], BF16, "mla_q_a_norm", 512)
    ckvn = _rms_fwd(ckv, small["mla_kv_a_norm"], BF16, "mla_kv_a_norm", 512)
    tm = min(512, t)

    th = min(2048, t)

    def head_proj(name, a, w, width):
        k = a.shape[1]
        return _mm(name, (mh, t // th, 1),
                   [(a, pl.BlockSpec((th, k), lambda h, i, r: (i, 0)), w, pl.BlockSpec((None, k, width), lambda h, i, r: (h, 0, 0)))],
                   NN, _sds((mh, t, width), F32), pl.BlockSpec((None, th, width), lambda h, i, r: (h, i, 0)), (th, width))

    q_raw = head_proj("mla_q_proj", cqn, w_qb, MLA_QK)
    kv_raw = head_proj("mla_kv_proj", ckvn, w_kvb, MLA_NOPE + MLA_V)
    k_raw = jnp.concatenate([kv_raw[:, :, :MLA_NOPE], jnp.broadcast_to(k_pe[None], (mh, t, MLA_ROPE))], axis=2)
    v_m = kv_raw[:, :, MLA_NOPE:].astype(BF16)
    q_raw2, k_raw2 = q_raw.reshape(mh * t, MLA_QK), k_raw.reshape(mh * t, MLA_QK)
    q_scale = MLA_QK ** -0.5
    q_m = _mla_qk_fwd(q_raw2, small["mla_q_norm"], cos_t, sin_t, q_scale, "mla_q_rope", 2048).reshape(mh, t, MLA_QK)
    k_m = _mla_qk_fwd(k_raw2, small["mla_k_norm"], cos_t, sin_t, 1.0, "mla_k_rope", 2048).reshape(mh, t, MLA_QK)
    o_mla_h, lse_m = _mla_fwd(q_m, k_m, v_m, 512, 4096)
    o_mla = _tokens_major(o_mla_h)

    om = _rms_fwd(o_mla, small["out_norm_mla"], BF16, "out_norm_mla", 512)
    half_w = DIL_WIDTH
    row = pl.BlockSpec((tm, D_MODEL), lambda i, j, r: (i, 0))
    act_spec = pl.BlockSpec((tm, half_w), lambda i, j, r: (i, 0))
    x2 = _mm("out_proj", (t // tm, 1, 1),
             [(od, act_spec, w_out, pl.BlockSpec((half_w, D_MODEL), lambda i, j, r: (0, 0))),
              (om, act_spec, w_out, pl.BlockSpec((half_w, D_MODEL), lambda i, j, r: (1, 0)))],
             NN, _sds((t, D_MODEL), F32), row, (tm, D_MODEL), res=(x1, row))
    x3, ffn2_saved, _ = _ffn_fwd(x2, small["ffn2_norm"], wfull["ffn2_w_gate"], wfull["ffn2_w_up"],
                                 wfull["ffn2_w_down"], "ffn2")
    dy, loss = _loss_head(x3, target, 512)

    dx2, grads_s["ffn2_norm"], grads_b["ffn2_w_gate"], grads_b["ffn2_w_up"], grads_b["ffn2_w_down"], _ = _ffn_bwd(
        dy, x2, small["ffn2_norm"], wfull["ffn2_w_gate"], wfull["ffn2_w_up"], wfull["ffn2_w_down"], ffn2_saved, "ffn2")

    d_ocat = _mm_simple("out_proj_dx", dx2, w_out, NT, F32, tm=1024)
    dw_out_d = _mm_simple("out_proj_dw_dil", od, dx2, TN, F32, tk=2048)
    dw_out_m = _mm_simple("out_proj_dw_mla", om, dx2, TN, F32, tk=2048)
    grads_b["w_out"] = jnp.concatenate([dw_out_d, dw_out_m], axis=0).reshape(N_CHIPS, D_MODEL // N_CHIPS, D_MODEL)
    do_dil, grads_s["out_norm_dil"] = _rms_bwd([d_ocat[:, :half_w]], o_dil, small["out_norm_dil"], None, "out_norm_dil_bwd", 512)
    do_mla, grads_s["out_norm_mla"] = _rms_bwd([d_ocat[:, half_w:]], o_mla, small["out_norm_mla"], None, "out_norm_mla_bwd", 512)

    do_m = _heads_major(do_mla, mh)
    dl_m = _rowdot(do_m.reshape(mh * t, MLA_V), o_mla_h.reshape(mh * t, MLA_V), "mla_delta", 2048).reshape(mh, t, 1)
    dk_m, dv_m, dq_t = _mla_bwd(q_m, k_m, k_m.transpose(0, 2, 1), v_m, do_m, lse_m.reshape(mh, 1, t),
                                dl_m.reshape(mh, 1, t), 2048, 512)
    dq_m = dq_t.transpose(0, 1, 3, 2).reshape(mh, t, MLA_QK)
    dq_raw, grads_s["mla_q_norm"] = _mla_qk_bwd(dq_m.reshape(mh * t, MLA_QK), q_raw2, small["mla_q_norm"],
                                                 cos_t, sin_t, q_scale, "mla_q_rope_bwd", 2048)
    dk_raw, grads_s["mla_k_norm"] = _mla_qk_bwd(dk_m.reshape(mh * t, MLA_QK), k_raw2, small["mla_k_norm"],
                                                 cos_t, sin_t, 1.0, "mla_k_rope_bwd", 2048)
    dq_raw = dq_raw.reshape(mh, t, MLA_QK)
    dk_raw = dk_raw.reshape(mh, t, MLA_QK)
    dkv_raw = jnp.concatenate([dk_raw[:, :, :MLA_NOPE], dv_m], axis=2)
    dk_pe_h = dk_raw[:, :, MLA_NOPE:]

    def head_proj_dx(name, d, w):
        width, k = d.shape[2], w.shape[1]
        pairs = [(d, pl.BlockSpec((None, th, width), lambda i, j, r, h=h: (h, i, 0)),
                  w, pl.BlockSpec((None, k, width), lambda i, j, r, h=h: (h, 0, 0))) for h in range(mh)]
        return _mm(name, (t // th, 1, 1), pairs, NT, _sds((t, k), F32),
                   pl.BlockSpec((th, k), lambda i, j, r: (i, 0)), (th, k))

    def head_proj_dw(name, a, d):
        width, k = d.shape[2], a.shape[1]
        return _mm(name, (mh, 1, t // th),
                   [(a, pl.BlockSpec((th, k), lambda h, j, r: (r, 0)), d, pl.BlockSpec((None, th, width), lambda h, j, r: (h, r, 0)))],
                   TN, _sds((mh, k, width), F32), pl.BlockSpec((None, k, width), lambda h, j, r: (h, 0, 0)), (k, width))

    d_cqn = head_proj_dx("mla_q_proj_dx", dq_raw, w_qb)
    d_ckvn = head_proj_dx("mla_kv_proj_dx", dkv_raw, w_kvb)
    grads_b["mla_w_q_b"] = head_proj_dw("mla_q_proj_dw", cqn, dq_raw)
    grads_b["mla_w_kv_b"] = head_proj_dw("mla_kv_proj_dw", ckvn, dkv_raw)
    d_cq, grads_s["mla_q_a_norm"] = _rms_bwd([d_cqn], cq, small["mla_q_a_norm"], None, "mla_q_a_norm_bwd", 512)
    d_ckv, grads_s["mla_kv_a_norm"] = _rms_bwd([d_ckvn], ckv, small["mla_kv_a_norm"], None, "mla_kv_a_norm_bwd", 512)
    d_kpe = _sum_blocks(dk_pe_h.reshape(mh, t * MLA_ROPE // LANES, LANES), "mla_kpe_sum", 1024).reshape(t, MLA_ROPE)

    stats, do_db = _dil_stats(do_dil, o_dil, lse_tot, 512)
    dqs, dks, dvs, dtiles = [], [], [], []
    for b, dil in enumerate(DIL_DILATIONS):
        dq_b, dk_b, dv_b, db_b = _dil_bwd(qn[b], kn[b], v_d[b], do_db[b], stats[b], bias[b], dil, f"dil_bwd_{dil}")
        dqs.append(dq_b)
        dks.append(dk_b)
        dvs.append(dv_b)
        dtiles.append(db_b)
    grads_s["rel_bias"] = _bias_grad(jnp.stack(dtiles).reshape(3, nh, QB, QB + DIL_W))
    dq_a, dgq = _head_norm_bwd(dqs, proj, 0, gq, "dil_q_norm_bwd", 512)
    dk_a, dgk = _head_norm_bwd(dks, proj, 1, gk, "dil_k_norm_bwd", 512)
    grads_s["dil_q_norm"], grads_s["dil_k_norm"] = dgq[:, :hd], dgk[:, :hd]
    dv_a = _sum_branches(dvs, "dil_dv_sum", 512)

    dparts = [dq_a, dk_a, dv_a, d_cq, d_ckv, d_kpe]
    t2 = min(1024, t)
    pairs, dw_parts, lo = [], [], 0
    for n, dpart in enumerate(dparts):
        width = dpart.shape[1]
        w_part = w_in[:, lo:lo + width]
        pairs.append((dpart, pl.BlockSpec((t2, width), lambda i, j, r: (i, 0)),
                      w_part, pl.BlockSpec((D_MODEL, width), lambda i, j, r: (0, 0))))
        dw_parts.append(_mm_simple(f"in_proj_dw_{n}", hm, dpart, TN, F32, tk=2048))
        lo += width
    d_hm = _mm("in_proj_dx", (t // t2, 1, 1), pairs, NT, _sds((t, D_MODEL), F32),
               pl.BlockSpec((t2, D_MODEL), lambda i, j, r: (i, 0)), (t2, D_MODEL))
    dw_in = jnp.concatenate(dw_parts, axis=1)
    grads_b["w_in"] = dw_in.reshape(D_MODEL, N_CHIPS, -1).transpose(1, 0, 2)
    dx1, grads_s["mix_norm"] = _rms_bwd([d_hm], x1, small["mix_norm"], dx2, "mix_norm_bwd", 512)
    outgoing = early_exchange([grads_b[n] for n in EARLY]) if early_exchange else ()
    dx, grads_s["ffn1_norm"], grads_b["ffn1_w_gate"], grads_b["ffn1_w_up"], grads_b["ffn1_w_down"], arrived = _ffn_bwd(
        dx1, x, small["ffn1_norm"], wfull["ffn1_w_gate"], wfull["ffn1_w_up"], wfull["ffn1_w_down"], ffn1_saved, "ffn1",
        outgoing)
    return loss, dx, grads_s, grads_b, (tuple(outgoing), arrived)


def kernel(x, ffn1_norm, ffn1_w_gate, ffn1_w_up, ffn1_w_down, mix_norm, w_in, dil_q_norm, dil_k_norm, rel_bias, mla_q_a_norm, mla_w_q_b, mla_kv_a_norm, mla_w_kv_b, mla_q_norm, mla_k_norm, out_norm_dil, out_norm_mla, w_out, ffn2_norm, ffn2_w_gate, ffn2_w_up, ffn2_w_down, loss_target, m_ffn1_norm, m_ffn1_w_gate, m_ffn1_w_up, m_ffn1_w_down, m_mix_norm, m_w_in, m_dil_q_norm, m_dil_k_norm, m_rel_bias, m_mla_q_a_norm, m_mla_w_q_b, m_mla_kv_a_norm, m_mla_w_kv_b, m_mla_q_norm, m_mla_k_norm, m_out_norm_dil, m_out_norm_mla, m_w_out, m_ffn2_norm, m_ffn2_w_gate, m_ffn2_w_up, m_ffn2_w_down, v_ffn1_norm, v_ffn1_w_gate, v_ffn1_w_up, v_ffn1_w_down, v_mix_norm, v_w_in, v_dil_q_norm, v_dil_k_norm, v_rel_bias, v_mla_q_a_norm, v_mla_w_q_b, v_mla_kv_a_norm, v_mla_w_kv_b, v_mla_q_norm, v_mla_k_norm, v_out_norm_dil, v_out_norm_mla, v_w_out, v_ffn2_norm, v_ffn2_w_gate, v_ffn2_w_up, v_ffn2_w_down):
    given = dict(locals())
    big_names = [name for name, _ in BIG]
    small_names = [name for name, _, _ in SMALL]

    chip = (2 * lax.axis_index("x") + lax.axis_index("y")).astype(jnp.int32)
    core = lax.axis_index("c").astype(jnp.int32)
    mine = {n: given[n].astype(BF16) for n in big_names}

    def with_own(names, arrays):
        return {n: lax.dynamic_update_slice(a, mine[n], (chip, 0, 0)) for n, a in zip(names, arrays)}

    wfirst = with_own(LATE, _gather_weights([mine[n][0] for n in LATE]))
    later_weights = ([mine[n][0] for n in EARLY], lambda partly: with_own(EARLY, _forward_cores(partly)))
    small = {n: given[n] for n in small_names}

    def chip_partials(partial, tag):
        return _add_halves(partial, _reduce_cores(partial, tag), core.reshape(1), tag)

    loss, dx, grads_s, grads_b, (early_part, early_got) = _local_step(
        x[0], loss_target[0], small, wfirst, functools.partial(chip_partials, tag="early"), later_weights)
    loss = lax.psum(loss[0, 0], ("x", "y", "c"))
    late_part = chip_partials([grads_b[n] for n in LATE], "late")
    reduced = _sum_partials(tuple(_scatter_chips(late_part)) + tuple(early_got), tuple(late_part) + tuple(early_part),
                            jnp.stack([chip, core]))
    g_big = dict(zip(LATE + EARLY, _share_cores(reduced)))
    g_small = _unpack_small(_allreduce_small(_pack_small(grads_s)))

    grad, delta, new_m, new_v = {}, {}, {}, {}
    for name, shape in BIG:
        g2 = g_big[name]
        d_, m_, v_ = _adamw(given[name].reshape(shape), g2, given["m_" + name].reshape(shape),
                            given["v_" + name].reshape(shape), f"adamw_{name}")
        full = given[name].shape
        grad[name], delta[name], new_m[name], new_v[name] = (a.reshape(full) for a in (g2, d_, m_, v_))
    ps = {k: _pack_small({n: given[pre + n] for n in small_names}) for k, pre in (("w", ""), ("m", "m_"), ("v", "v_"))}
    gs_packed = _pack_small(g_small)
    d_s, m_s, v_s = (_unpack_small(a) for a in _adamw(ps["w"], gs_packed, ps["m"], ps["v"], "adamw_small"))
    for name in small_names:
        grad[name], delta[name], new_m[name], new_v[name] = g_small[name], d_s[name], m_s[name], v_s[name]

    return (loss, dx[None], *[grad[n] for n in WEIGHTS], *[delta[n] for n in WEIGHTS],
            *[new_m[n] for n in WEIGHTS], *[new_v[n] for n in WEIGHTS])
```

```python
import functools

import numpy as np
import jax
import jax.numpy as jnp
from jax import lax
from jax.experimental import pallas as pl
from jax.experimental.pallas import tpu as pltpu

F32 = jnp.float32
BF16 = jnp.bfloat16

D_MODEL = 1024
D_FF = 2816
N_CHIPS = 4
DIL_HEADS = 8
DIL_HD = 64
DIL_WIDTH = 512
DIL_DILATIONS = (1, 4, 16)
DIL_W = 128
QB = 128
MLA_HEADS = 4
MLA_NOPE = 128
MLA_ROPE = 64
MLA_QK = 192
MLA_V = 128
MLA_Q_RANK = 256
MLA_KV_RANK = 128
ROPE_BASE = 10000.0
REL_BUCKETS = 32
REL_MAX_DIST = 2048
FFN_RESID = 0.5
EPS = 1e-6
NEG = -1e30
LANES = 128

ADAM_LR = 0.001
ADAM_B1 = 0.9
ADAM_B2 = 0.999
ADAM_EPS = 1e-08
ADAM_WD = 0.01
ADAM_STEP = 10

NT = (((1,), (1,)), ((), ()))
NN = (((1,), (0,)), ((), ()))
TN = (((0,), (0,)), ((), ()))

BIG = (
    ("ffn1_w_gate", (D_MODEL, D_FF // N_CHIPS)),
    ("ffn1_w_up", (D_MODEL, D_FF // N_CHIPS)),
    ("ffn1_w_down", (D_FF // N_CHIPS, D_MODEL)),
    ("w_in", (D_MODEL, 1984 // N_CHIPS)),
    ("mla_w_q_b", (MLA_Q_RANK, MLA_QK)),
    ("mla_w_kv_b", (MLA_KV_RANK, MLA_NOPE + MLA_V)),
    ("w_out", (D_MODEL // N_CHIPS, D_MODEL)),
    ("ffn2_w_gate", (D_MODEL, D_FF // N_CHIPS)),
    ("ffn2_w_up", (D_MODEL, D_FF // N_CHIPS)),
    ("ffn2_w_down", (D_FF // N_CHIPS, D_MODEL)),
)
SMALL = (
    ("ffn1_norm", (1, 1024), 8), ("mix_norm", (1, 1024), 8), ("dil_q_norm", (1, 64), 1),
    ("dil_k_norm", (1, 64), 1), ("rel_bias", (8, 32), 2), ("mla_q_a_norm", (1, 256), 2),
    ("mla_kv_a_norm", (1, 128), 1), ("mla_q_norm", (1, 192), 2), ("mla_k_norm", (1, 192), 2),
    ("out_norm_dil", (1, 512), 4), ("out_norm_mla", (1, 512), 4), ("ffn2_norm", (1, 1024), 8),
)
SMALL_ROWS = 48
WEIGHTS = ("ffn1_norm", "ffn1_w_gate", "ffn1_w_up", "ffn1_w_down", "mix_norm", "w_in", "dil_q_norm",
           "dil_k_norm", "rel_bias", "mla_q_a_norm", "mla_w_q_b", "mla_kv_a_norm", "mla_w_kv_b",
           "mla_q_norm", "mla_k_norm", "out_norm_dil", "out_norm_mla", "w_out", "ffn2_norm",
           "ffn2_w_gate", "ffn2_w_up", "ffn2_w_down")


def _pcall(body, **kw):
    return pl.pallas_call(body, **kw)


def _cparams(*sem):
    return pltpu.CompilerParams(dimension_semantics=sem)


def _sds(shape, dtype):
    return jax.ShapeDtypeStruct(shape, dtype)


def _dot(a, b, dn):
    return lax.dot_general(a, b, dn, preferred_element_type=F32)


def _rms_fwd(x, g, out_dtype, name, tm):
    n, d = x.shape
    tm = min(tm, n)

    def body(x_ref, g_ref, o_ref):
        xf = x_ref[...].astype(F32)
        r = lax.rsqrt(jnp.mean(xf * xf, axis=-1, keepdims=True) + EPS)
        o_ref[...] = (xf * r * g_ref[...]).astype(o_ref.dtype)

    return _pcall(
        body, name=name, grid=(n // tm,),
        in_specs=[pl.BlockSpec((tm, d), lambda i: (i, 0)), pl.BlockSpec((1, d), lambda i: (0, 0))],
        out_specs=pl.BlockSpec((tm, d), lambda i: (i, 0)),
        out_shape=_sds((n, d), out_dtype), compiler_params=_cparams("parallel"))(x, g)


def _rms_bwd(dys, x, g, res, name, tm):
    n, d = x.shape
    tm = min(tm, n)
    nd = len(dys)
    has_res = res is not None

    def body(*refs):
        dy_refs = refs[:nd]
        x_ref, g_ref = refs[nd], refs[nd + 1]
        res_ref = refs[nd + 2] if has_res else None
        dx_ref, dg_ref = refs[-2], refs[-1]
        dy = dy_refs[0][...].astype(F32)
        for r_ in dy_refs[1:]:
            dy = dy + r_[...].astype(F32)
        xf = x_ref[...].astype(F32)
        r = lax.rsqrt(jnp.mean(xf * xf, axis=-1, keepdims=True) + EPS)
        xh = xf * r
        dxh = dy * g_ref[...]
        dx = r * (dxh - xh * jnp.mean(dxh * xh, axis=-1, keepdims=True))
        if has_res:
            dx = dx + res_ref[...]
        dx_ref[...] = dx

        @pl.when(pl.program_id(0) == 0)
        def _():
            dg_ref[...] = jnp.zeros_like(dg_ref)

        dg_ref[...] += jnp.sum(dy * xh, axis=0, keepdims=True)

    row = pl.BlockSpec((tm, d), lambda i: (i, 0))
    vec = pl.BlockSpec((1, d), lambda i: (0, 0))
    ins = list(dys) + [x, g] + ([res] if has_res else [])
    return _pcall(
        body, name=name, grid=(n // tm,),
        in_specs=[row] * nd + [row, vec] + ([row] if has_res else []),
        out_specs=(row, vec),
        out_shape=(_sds((n, d), F32), _sds((1, d), F32)),
        compiler_params=_cparams("arbitrary"))(*ins)


def _mm(name, grid, pairs, dn, out_shape, out_spec, acc_shape, res=None, scale=1.0, outgoing=()):
    npairs = len(pairs)
    nred = grid[2]
    has_res = res is not None
    no = len(outgoing)

    def body(*refs):
        ab = refs[:2 * npairs]
        res_ref = refs[2 * npairs] if has_res else None
        first_out = 2 * npairs + int(has_res) + no
        sent = refs[first_out - no:first_out]
        o_ref = refs[first_out]
        arrived = refs[first_out + 1:first_out + 1 + no]
        acc_ref = refs[first_out + 1 + no] if nred > 1 else None
        if no:
            send_sems, recv_sems = refs[-2:]
            ids = [pl.program_id(n) for n in range(3)]

            @pl.when((ids[0] == 0) & (ids[1] == 0) & (ids[2] == 0))
            def _():
                _scatter_start(sent, arrived, send_sems, recv_sems)

        tot = None
        for p in range(npairs):
            d = _dot(ab[2 * p][...].astype(BF16), ab[2 * p + 1][...].astype(BF16), dn)
            tot = d if tot is None else tot + d

        def finish(v):
            if scale != 1.0:
                v = v * scale
            if has_res:
                v = res_ref[...] + v
            o_ref[...] = v.astype(o_ref.dtype)

        if nred == 1:
            finish(tot)
        else:
            r = pl.program_id(2)

            @pl.when(r == 0)
            def _():
                acc_ref[...] = tot

            @pl.when(r > 0)
            def _():
                acc_ref[...] += tot

            @pl.when(r == nred - 1)
            def _():
                finish(acc_ref[...])

        if no:
            @pl.when((ids[0] == grid[0] - 1) & (ids[1] == grid[1] - 1) & (ids[2] == nred - 1))
            def _():
                _scatter_wait(sent, arrived, send_sems, recv_sems)

    ins, specs = [], []
    for a, a_spec, b, b_spec in pairs:
        ins += [a, b]
        specs += [a_spec, b_spec]
    if has_res:
        ins.append(res[0])
        specs.append(res[1])
    scratch = [pltpu.VMEM(acc_shape, F32)] if nred > 1 else []
    if not no:
        return _pcall(
            body, name=name, grid=grid, in_specs=specs, out_specs=out_spec, out_shape=out_shape,
            scratch_shapes=scratch, compiler_params=_cparams("parallel", "parallel", "arbitrary"))(*ins)
    hbm = pl.BlockSpec(memory_space=pltpu.HBM)
    res_ = tuple(_pcall(
        body, name=name, grid=grid, in_specs=specs + [hbm] * no, out_specs=(out_spec,) + (hbm,) * no,
        out_shape=(out_shape,) + _scatter_shapes(outgoing), scratch_shapes=scratch + _scatter_sems(no),
        compiler_params=_cparams("arbitrary", "arbitrary", "arbitrary"))(*ins, *outgoing))
    return res_[0], res_[1:]


def _ffn_up(h, wg, wu, name, tm, incoming=()):
    t, d = h.shape
    nc, _, fs = wg.shape
    tm = min(tm, t)
    nt = t // tm
    ni = len(incoming)
    halves = _halves(incoming)

    def body(*refs):
        h_ref, wg_ref, wu_ref = refs[:3]
        srcs = refs[3:3 + ni]
        g_ref, u_ref, a_ref = refs[3 + ni:6 + ni]
        outs = refs[6 + ni:6 + 2 * ni]
        if ni:
            send_sems, recv_sems = refs[6 + 2 * ni:]
            c, i = pl.program_id(0), pl.program_id(1)

            @pl.when((c == 0) & (i == 0))
            def _():
                _gather_start(srcs, outs, halves, send_sems, recv_sems)

        hh = h_ref[...]
        gate = _dot(hh, wg_ref[...], NN)
        up = _dot(hh, wu_ref[...], NN)
        sig = jax.nn.sigmoid(gate)
        silu = gate * sig
        g_ref[...] = (up * (sig + silu * (1.0 - sig))).astype(BF16)
        u_ref[...] = silu.astype(BF16)
        a_ref[...] = (silu * up).astype(BF16)

        if ni:
            @pl.when((c == nc - 1) & (i == nt - 1))
            def _():
                _gather_wait(outs, halves, send_sems, recv_sems)

    wspec = pl.BlockSpec((None, d, fs), lambda c, i: (c, 0, 0))
    ospec = pl.BlockSpec((None, tm, fs), lambda c, i: (c, i, 0))
    hbm = pl.BlockSpec(memory_space=pltpu.HBM)
    osd = _sds((nc, t, fs), BF16)
    res = tuple(_pcall(
        body, name=name, grid=(nc, nt),
        in_specs=[pl.BlockSpec((tm, d), lambda c, i: (i, 0)), wspec, wspec] + [hbm] * ni,
        out_specs=(ospec, ospec, ospec) + (hbm,) * ni,
        out_shape=(osd, osd, osd) + tuple(_sds((N_CHIPS,) + b.shape, b.dtype) for b in incoming),
        scratch_shapes=[pltpu.SemaphoreType.DMA((3 * ni,)), pltpu.SemaphoreType.DMA((3 * ni,))] if ni else [],
        compiler_params=_cparams("arbitrary", "arbitrary"))(h, wg, wu, *incoming))
    return res[:3] + (res[3:],)


def _ffn_hidden_bwd(dy, h, wd, dact_dgate, dact_dup, act, name, tm, outgoing=()):
    t, d = dy.shape
    nc, fs, _ = wd.shape
    tm = min(tm, t)
    nt = t // tm
    no = len(outgoing)

    def body(*refs):
        dy_ref, h_ref, wd_ref, g_ref, u_ref, a_ref = refs[:6]
        sent = refs[6:6 + no]
        dg_ref, du_ref, dwg_hbm, dwu_hbm, dwd_hbm = refs[6 + no:11 + no]
        arrived = refs[11 + no:11 + 2 * no]
        wg_acc, wu_acc, wd_acc, sem = refs[11 + 2 * no:15 + 2 * no]
        c, i = pl.program_id(0), pl.program_id(1)
        if no:
            send_sems, recv_sems = refs[15 + 2 * no:]

            @pl.when((c == 0) & (i == 0))
            def _():
                _scatter_start(sent, arrived, send_sems, recv_sems)

        dyb = dy_ref[...].astype(BF16)
        da = _dot(dyb, wd_ref[...], NT) * FFN_RESID
        dgate = (da * g_ref[...].astype(F32)).astype(BF16)
        dup = (da * u_ref[...].astype(F32)).astype(BF16)
        dg_ref[...] = dgate
        du_ref[...] = dup
        hh = h_ref[...]
        parts = (_dot(hh, dgate, TN), _dot(hh, dup, TN), _dot(a_ref[...], dyb, TN) * FFN_RESID)
        accs = (wg_acc, wu_acc, wd_acc)

        @pl.when(i == 0)
        def _():
            for acc, part in zip(accs, parts):
                acc[...] = part

        @pl.when(i > 0)
        def _():
            for acc, part in zip(accs, parts):
                acc[...] += part

        @pl.when(i == nt - 1)
        def _():
            copies = [pltpu.make_async_copy(acc, out.at[c], sem.at[n])
                      for n, (acc, out) in enumerate(zip(accs, (dwg_hbm, dwu_hbm, dwd_hbm)))]
            for cp in copies:
                cp.start()
            for cp in copies:
                cp.wait()

        if no:
            @pl.when((c == nc - 1) & (i == nt - 1))
            def _():
                _scatter_wait(sent, arrived, send_sems, recv_sems)

    tok = pl.BlockSpec((tm, d), lambda c, i: (i, 0))
    cspec = pl.BlockSpec((None, tm, fs), lambda c, i: (c, i, 0))
    hbm = pl.BlockSpec(memory_space=pltpu.HBM)
    osd = _sds((nc, t, fs), BF16)
    res = _pcall(
        body, name=name, grid=(nc, nt),
        in_specs=[tok, tok, pl.BlockSpec((None, fs, d), lambda c, i: (c, 0, 0)), cspec, cspec, cspec] + [hbm] * no,
        out_specs=(cspec, cspec, hbm, hbm, hbm) + (hbm,) * no,
        out_shape=(osd, osd, _sds((nc, d, fs), F32), _sds((nc, d, fs), F32), _sds((nc, fs, d), F32))
        + _scatter_shapes(outgoing),
        scratch_shapes=[pltpu.VMEM((d, fs), F32), pltpu.VMEM((d, fs), F32), pltpu.VMEM((fs, d), F32),
                        pltpu.SemaphoreType.DMA((3,))] + (_scatter_sems(no) if no else []),
        compiler_params=_cparams("arbitrary", "arbitrary"))(dy, h, wd, dact_dgate, dact_dup, act, *outgoing)
    res = tuple(res)
    return res[:5] + (res[5:],)


def _ffn_fwd(x, g, wg, wu, wd, tag, incoming=()):
    t = x.shape[0]
    nc, _, fs = wg.shape
    tm = min(512, t)
    h = _rms_fwd(x, g, BF16, f"{tag}_norm", 512)
    dact_dgate, dact_dup, act, partly = _ffn_up(h, wg, wu, f"{tag}_up", 1024, incoming)
    pairs = [(act, pl.BlockSpec((None, tm, fs), lambda i, j, r, c=c: (c, i, 0)),
              wd, pl.BlockSpec((None, fs, D_MODEL), lambda i, j, r, c=c: (c, 0, 0))) for c in range(nc)]
    row = pl.BlockSpec((tm, D_MODEL), lambda i, j, r: (i, 0))
    y = _mm(f"{tag}_down", (t // tm, 1, 1), pairs, NN, _sds((t, D_MODEL), F32), row, (tm, D_MODEL),
            res=(x, row), scale=FFN_RESID)
    return y, (h, dact_dgate, dact_dup, act), partly


def _ffn_bwd(dy, x, g, wg, wu, wd, saved, tag, outgoing=(), own_exchange=None):
    h, dact_dgate, dact_dup, act = saved
    t = x.shape[0]
    nc, _, fs = wg.shape
    tm = min(512, t)
    dgate, dup, dwg, dwu, dwd, arrived = _ffn_hidden_bwd(dy, h, wd, dact_dgate, dact_dup, act,
                                                         f"{tag}_hidden_bwd", 1024, outgoing)
    pairs = []
    for c in range(nc):
        a_spec = pl.BlockSpec((None, tm, fs), lambda i, j, r, c=c: (c, i, 0))
        w_spec = pl.BlockSpec((None, D_MODEL, fs), lambda i, j, r, c=c: (c, 0, 0))
        pairs += [(dgate, a_spec, wg, w_spec), (dup, a_spec, wu, w_spec)]
    own_part = tuple(own_exchange([dwg, dwu, dwd])) if own_exchange else ()
    dh = _mm(f"{tag}_dh", (t // tm, 1, 1), pairs, NT,
             _sds((t, D_MODEL), F32), pl.BlockSpec((tm, D_MODEL), lambda i, j, r: (i, 0)), (tm, D_MODEL),
             outgoing=own_part)
    dh, own_got = dh if own_part else (dh, ())
    dx, dg = _rms_bwd([dh], x, g, dy, f"{tag}_dnorm", 512)
    return dx, dg, dwg, dwu, dwd, arrived, (own_part, own_got)


def _mm_simple(name, a, b, dn, out_dtype, tm=512, tk=512, res=None, scale=1.0):
    if dn == TN:
        k, m = a.shape
        n = b.shape[1]
        tk = min(tk, k)
        return _mm(name, (1, 1, k // tk),
                   [(a, pl.BlockSpec((tk, m), lambda i, j, r: (r, 0)), b, pl.BlockSpec((tk, n), lambda i, j, r: (r, 0)))],
                   TN, _sds((m, n), out_dtype), pl.BlockSpec((m, n), lambda i, j, r: (0, 0)), (m, n), scale=scale)
    m, k = a.shape
    n = b.shape[1] if dn == NN else b.shape[0]
    tm = min(tm, m)
    row = pl.BlockSpec((tm, n), lambda i, j, r: (i, 0))
    return _mm(name, (m // tm, 1, 1),
               [(a, pl.BlockSpec((tm, k), lambda i, j, r: (i, 0)), b, pl.BlockSpec(b.shape, lambda i, j, r: (0, 0)))],
               dn, _sds((m, n), out_dtype), row, (tm, n), res=None if res is None else (res, row), scale=scale)


def _t5_bucket(dist):
    max_exact = REL_BUCKETS // 2
    d = np.maximum(dist, 1).astype(np.float32)
    large = max_exact + (np.log(d / max_exact) / np.log(REL_MAX_DIST / max_exact)
                         * (REL_BUCKETS - max_exact)).astype(np.int32)
    large = np.minimum(large, REL_BUCKETS - 1)
    return np.where(dist < max_exact, dist, large).astype(np.int32)


def _bucket_tiles():
    i = np.arange(QB)[:, None]
    j = np.arange(QB + DIL_W)[None, :]
    delta = np.clip(i + DIL_W - j, 0, None)
    return np.stack([_t5_bucket(delta * dil) for dil in DIL_DILATIONS]).astype(np.int32)


def _bias_tiles(rel_bias):
    buckets = jnp.asarray(_bucket_tiles())

    def body(rb_ref, bk_ref, o_ref):
        bk = bk_ref[...]
        for h in range(DIL_HEADS):
            def pick(b, tile):
                return jnp.where(bk == b, rb_ref[h, b], tile)

            o_ref[h] = lax.fori_loop(0, REL_BUCKETS, pick, jnp.zeros((QB, QB + DIL_W), F32))

    return _pcall(
        body, name="dil_bias_tiles", grid=(3,),
        in_specs=[pl.BlockSpec(memory_space=pltpu.SMEM),
                  pl.BlockSpec((None, QB, QB + DIL_W), lambda b: (b, 0, 0))],
        out_specs=pl.BlockSpec((None, DIL_HEADS, QB, QB + DIL_W), lambda b: (b, 0, 0, 0)),
        out_shape=_sds((3, DIL_HEADS, QB, QB + DIL_W), F32),
        compiler_params=_cparams("parallel"))(rel_bias, buckets)


def _bias_grad(dtiles):
    buckets = jnp.asarray(_bucket_tiles())

    def body(dt_ref, bk_ref, o_ref):
        def one(b, carry):
            hit = [bk_ref[br] == b for br in range(3)]
            for h in range(DIL_HEADS):
                tot = jnp.zeros((), F32)
                for br in range(3):
                    tot = tot + jnp.sum(jnp.where(hit[br], dt_ref[br, h], 0.0))
                o_ref[h, b] = tot
            return carry

        lax.fori_loop(0, REL_BUCKETS, one, 0)

    return _pcall(
        body, name="dil_bias_grad",
        in_specs=[pl.BlockSpec(memory_space=pltpu.VMEM), pl.BlockSpec(memory_space=pltpu.VMEM)],
        out_specs=pl.BlockSpec(memory_space=pltpu.SMEM),
        out_shape=_sds((DIL_HEADS, REL_BUCKETS), F32))(dtiles, buckets)


def _split_heads(a, lo):
    zero = jnp.zeros_like(a)
    return jnp.concatenate([jnp.where(lo, a, zero), jnp.where(lo, zero, a)], axis=0)


def _side_by_side(a):
    n = a.shape[0] // 2
    return jnp.concatenate([a[:n], a[n:]], axis=1)


def _band_masks(prev_ok):
    ii = lax.broadcasted_iota(jnp.int32, (2 * QB, QB), 0) & (QB - 1)
    jj = lax.broadcasted_iota(jnp.int32, (2 * QB, QB), 1)
    return jj <= ii, jj >= ii + jnp.where(prev_ok, 0, QB)


def _dil_fwd(q, k, v, bias, dil, name):
    w = DIL_WIDTH
    t = q.shape[0] * dil
    npair = w // LANES
    nl = t // dil // QB
    scale = DIL_HD ** -0.5

    def body(q_ref, kc_ref, kp_ref, vc_ref, vp_ref, b_ref, o_ref, lse_ref):
        nn = pl.program_id(1)
        lo = lax.broadcasted_iota(jnp.int32, (QB, LANES), 1) < DIL_HD
        lo2 = lax.broadcasted_iota(jnp.int32, (2 * QB, LANES), 1) < DIL_HD
        ii = lax.broadcasted_iota(jnp.int32, (2 * QB, 2 * QB), 0) & (QB - 1)
        jj = lax.broadcasted_iota(jnp.int32, (2 * QB, 2 * QB), 1)
        first_key = jnp.maximum(ii, jnp.where(nn != 0, 0, QB))
        valid = (jj >= first_key) & (jj <= ii + QB)
        for p in range(npair):
            cols = slice(p * LANES, (p + 1) * LANES)
            qq = _split_heads(q_ref[:, cols], lo)
            kk = jnp.concatenate([kp_ref[:, cols], kc_ref[:, cols]], axis=0)
            vv = jnp.concatenate([vp_ref[:, cols], vc_ref[:, cols]], axis=0)
            s = jnp.where(valid, _dot(qq, kk, NT) * scale + b_ref[p], NEG)
            m = jnp.max(s, axis=-1, keepdims=True)
            e = jnp.exp(s - m)
            den = jnp.sum(e, axis=-1, keepdims=True)
            pn = (e * (1.0 / den)).astype(BF16)
            o_ref[:, cols] = _dot(_side_by_side(pn), _split_heads(vv, lo2), NN)
            lse = m + jnp.log(den)
            lse_ref[:, cols] = jnp.where(lo, lse[:QB], lse[QB:])

    cur = pl.BlockSpec((QB, w), lambda r, n: (n, r))
    prev = pl.BlockSpec((QB, w), lambda r, n: (jnp.maximum(n - 1, 0), r))
    sd = _sds((t // dil, dil * w), F32)
    return _pcall(
        body, name=name, grid=(dil, nl),
        in_specs=[cur, cur, prev, cur, prev, pl.BlockSpec((npair, 2 * QB, 2 * QB), lambda r, n: (0, 0, 0))],
        out_specs=(cur, cur), out_shape=(sd, sd),
        compiler_params=_cparams("parallel", "parallel"))(q, k, k, v, v, bias)


def _dil_bwd(q, k, v, do, stats, bias, dil, name):
    w = DIL_WIDTH
    t = q.shape[0] * dil
    npair = w // LANES
    nl = t // dil // QB
    scale = DIL_HD ** -0.5

    def body(qc_ref, qn_ref, doc_ref, don_ref, sc_ref, sn_ref, k_ref, v_ref, b_ref,
             dq_ref, dk_ref, dv_ref, db_ref, carry):
        r, nn = pl.program_id(0), pl.program_id(1)
        lo = lax.broadcasted_iota(jnp.int32, (QB, LANES), 1) < DIL_HD
        cur_ok, prev_ok = _band_masks(nn + 1 < nl)

        @pl.when((r == 0) & (nn == 0))
        def _():
            db_ref[...] = jnp.zeros_like(db_ref)
            carry[...] = jnp.zeros_like(carry)

        for p in range(npair):
            cols = slice(p * LANES, (p + 1) * LANES)
            kp, vp = k_ref[:, cols], v_ref[:, cols]
            k2 = _split_heads(kp, lo)

            def column(ref, lane):
                first = p * LANES + lane
                return jnp.concatenate([ref[:, first:first + 1], ref[:, first + DIL_HD:first + DIL_HD + 1]], axis=0)

            def side(q_ref, do_ref, s_ref, bias, ok):
                qq = _split_heads(q_ref[:, cols], lo)
                dd = _split_heads(do_ref[:, cols], lo)
                s = jnp.where(ok, _dot(qq, kp, NT) * scale + bias, NEG)
                prob = jnp.exp(s - column(s_ref, 0))
                ds = prob * (_dot(dd, vp, NT) - column(s_ref, DIL_HD // 2))
                return qq, dd, prob.astype(BF16), ds

            q1, d1, p1, ds1 = side(qc_ref, doc_ref, sc_ref, b_ref[p, :, QB:], cur_ok)
            q2, d2, p2, ds2 = side(qn_ref, don_ref, sn_ref, b_ref[p, :, :QB], prev_ok)
            ds1b, ds2b = ds1.astype(BF16), ds2.astype(BF16)
            dq_ref[:, cols] = carry[:, cols] + _dot(_side_by_side(ds1b), k2, NN) * scale
            carry[:, cols] = _dot(_side_by_side(ds2b), k2, NN) * scale
            dk_ref[:, cols] = _dot(jnp.concatenate([ds1b, ds2b], axis=0), jnp.concatenate([q1, q2], axis=0), TN) * scale
            dv_ref[:, cols] = _dot(jnp.concatenate([p1, p2], axis=0), jnp.concatenate([d1, d2], axis=0), TN)
            db_ref[p, :, QB:] += ds1
            db_ref[p, :, :QB] += ds2

    cur = pl.BlockSpec((QB, w), lambda r, n: (n, r))
    nxt = pl.BlockSpec((QB, w), lambda r, n: (jnp.minimum(n + 1, nl - 1), r))
    tile = pl.BlockSpec((npair, 2 * QB, 2 * QB), lambda r, n: (0, 0, 0))
    sd = _sds((t // dil, dil * w), F32)
    return _pcall(
        body, name=name, grid=(dil, nl),
        in_specs=[cur, nxt, cur, nxt, cur, nxt, cur, cur, tile],
        out_specs=(cur, cur, cur, tile),
        out_shape=(sd, sd, sd, _sds((npair, 2 * QB, 2 * QB), F32)),
        scratch_shapes=[pltpu.VMEM((QB, w), F32)],
        compiler_params=_cparams("arbitrary", "arbitrary"))(q, q, do, do, stats, stats, k, v, bias)


def _head_sum_matrix(scale):
    idx = np.arange(DIL_WIDTH) // DIL_HD
    return jnp.asarray((idx[:, None] == idx[None, :]).astype(np.float32) * scale, BF16)


def _head_sum(x, mat):
    hi = x.astype(BF16)
    lo = (x - hi.astype(F32)).astype(BF16)
    return _dot(hi, mat, NN) + _dot(lo, mat, NN)


def _to_views(src, tmp, out_refs):
    tm, w = src.shape
    for j in range(w // LANES):
        tmp[j] = src[:, j * LANES:(j + 1) * LANES]
    for d, o_ref in zip(DIL_DILATIONS, out_refs):
        if d == 1:
            o_ref[...] = src.astype(o_ref.dtype)
            continue
        for r in range(d):
            for j in range(w // LANES):
                lo = r * w + j * LANES
                o_ref[:, lo:lo + LANES] = tmp[j, pl.ds(r, tm // d, stride=d), :].astype(o_ref.dtype)


def _from_view(v_ref, tmp, d):
    tm = tmp.shape[1]
    w = v_ref.shape[1] // d
    for r in range(d):
        for j in range(w // LANES):
            lo = r * w + j * LANES
            tmp[j, pl.ds(r, tm // d, stride=d), :] = v_ref[:, lo:lo + LANES]
    return jnp.concatenate([tmp[j] for j in range(w // LANES)], axis=1)


def _view_specs(tm, t, dtype):
    specs = tuple(pl.BlockSpec((tm // d, d * DIL_WIDTH), lambda i: (i, 0)) for d in DIL_DILATIONS)
    shapes = tuple(_sds((t // d, d * DIL_WIDTH), dtype) for d in DIL_DILATIONS)
    return specs, shapes


def _view_scratch(tm):
    return pltpu.VMEM((DIL_WIDTH // LANES, tm, LANES), F32)


def _dil_merge(outs, lses, g, tm):
    w = DIL_WIDTH
    t = outs[0].shape[0]
    tm = min(tm, t)

    def body(o0, o1, o2, l0, l1, l2, g_ref, o_ref, l_ref, n_ref, so1, so2, sl1, sl2):
        d1, d2 = DIL_DILATIONS[1], DIL_DILATIONS[2]
        a0, a1, a2 = l0[...], _from_view(l1, sl1, d1), _from_view(l2, sl2, d2)
        m = jnp.maximum(jnp.maximum(a0, a1), a2)
        e0, e1, e2 = jnp.exp(a0 - m), jnp.exp(a1 - m), jnp.exp(a2 - m)
        den = e0 + e1 + e2
        o = (e0 * o0[...] + e1 * _from_view(o1, so1, d1) + e2 * _from_view(o2, so2, d2)) / den
        o_ref[...] = o
        l_ref[...] = m + jnp.log(den)
        r = lax.rsqrt(jnp.mean(o * o, axis=-1, keepdims=True) + EPS)
        n_ref[...] = (o * r * g_ref[...]).astype(n_ref.dtype)

    specs, _ = _view_specs(tm, t, F32)
    spec = pl.BlockSpec((tm, w), lambda i: (i, 0))
    return _pcall(
        body, name="dil_merge", grid=(t // tm,),
        in_specs=list(specs) * 2 + [pl.BlockSpec((1, w), lambda i: (0, 0))], out_specs=(spec, spec, spec),
        out_shape=(_sds((t, w), F32), _sds((t, w), F32), _sds((t, w), BF16)),
        scratch_shapes=[_view_scratch(tm)] * 4,
        compiler_params=_cparams("parallel"))(*outs, *lses, g)


def _dil_stats(do, o, lse, tm):
    t, w = do.shape
    tm = min(tm, t)

    def body(a_ref, b_ref, l_ref, m_ref, s1, s4, s16, d1, d4, d16, tmp):
        first = (lax.broadcasted_iota(jnp.int32, (tm, w), 1) & (DIL_HD - 1)) < DIL_HD // 2
        do_ = a_ref[...]
        _to_views(jnp.where(first, l_ref[...], _head_sum(do_ * b_ref[...], m_ref[...])), tmp, (s1, s4, s16))
        _to_views(do_, tmp, (d1, d4, d16))

    spec = pl.BlockSpec((tm, w), lambda i: (i, 0))
    f_specs, f_shapes = _view_specs(tm, t, F32)
    b_specs, b_shapes = _view_specs(tm, t, BF16)
    res = _pcall(body, name="dil_stats", grid=(t // tm,),
                 in_specs=[spec, spec, spec, pl.BlockSpec((w, w), lambda i: (0, 0))],
                 out_specs=f_specs + b_specs, out_shape=f_shapes + b_shapes,
                 scratch_shapes=[_view_scratch(tm)],
                 compiler_params=_cparams("parallel"))(do, o, lse, _head_sum_matrix(1.0))
    return res[:3], res[3:]


def _head_norm_fwd(x, col, g, name, tm):
    t = x.shape[0]
    w = DIL_WIDTH
    tm = min(tm, t)
    normed = g is not None

    def body(*refs):
        outs, tmp = refs[-4:-1], refs[-1]
        xf = refs[0][...]
        if normed:
            g_ref, m_ref = refs[1], refs[2]
            xf = xf * lax.rsqrt(_head_sum(xf * xf, m_ref[...]) + EPS) * g_ref[...]
        _to_views(xf, tmp, outs)

    specs, shapes = _view_specs(tm, t, BF16)
    extra = [g, _head_sum_matrix(1.0 / DIL_HD)] if normed else []
    extra_specs = [pl.BlockSpec((1, w), lambda i: (0, 0)), pl.BlockSpec((w, w), lambda i: (0, 0))] if normed else []
    return _pcall(
        body, name=name, grid=(t // tm,),
        in_specs=[pl.BlockSpec((tm, w), lambda i: (i, col))] + extra_specs,
        out_specs=specs, out_shape=shapes, scratch_shapes=[_view_scratch(tm)],
        compiler_params=_cparams("parallel"))(x, *extra)


def _head_norm_bwd(dys, x, col, g, name, tm):
    t = x.shape[0]
    w = DIL_WIDTH
    tm = min(tm, t)
    nd = len(dys)
    nt = t // tm
    lane = np.arange(w) % DIL_HD
    fold = jnp.asarray((lane[:, None] == lane[None, :]).astype(np.float32))

    def body(*refs):
        x_ref, g_ref, m_ref, f_ref = refs[nd:nd + 4]
        dx_ref, dg_ref, s1, s2 = refs[-4:]
        dy = refs[0][...] + _from_view(refs[1], s1, DIL_DILATIONS[1]) + _from_view(refs[2], s2, DIL_DILATIONS[2])
        xf = x_ref[...]
        mat = m_ref[...]
        r = lax.rsqrt(_head_sum(xf * xf, mat) + EPS)
        xh = xf * r
        dxh = dy * g_ref[...]
        dx_ref[...] = r * (dxh - xh * _head_sum(dxh * xh, mat))

        @pl.when(pl.program_id(0) == 0)
        def _():
            dg_ref[...] = jnp.zeros_like(dg_ref)

        dg_ref[...] += jnp.sum(dy * xh, axis=0, keepdims=True)

        @pl.when(pl.program_id(0) == nt - 1)
        def _():
            per_lane = jnp.broadcast_to(dg_ref[...], (8, w))
            dg_ref[...] = lax.dot_general(per_lane, f_ref[...], NN, precision=lax.Precision.HIGHEST,
                                          preferred_element_type=F32)[0:1]

    row = pl.BlockSpec((tm, w), lambda i: (i, 0))
    vec = pl.BlockSpec((1, w), lambda i: (0, 0))
    sq = pl.BlockSpec((w, w), lambda i: (0, 0))
    views, _ = _view_specs(tm, t, F32)
    return _pcall(
        body, name=name, grid=(nt,),
        in_specs=list(views) + [pl.BlockSpec((tm, w), lambda i: (i, col)), vec, sq, sq],
        out_specs=(row, vec), out_shape=(_sds((t, w), F32), _sds((1, w), F32)),
        scratch_shapes=[_view_scratch(tm)] * 2,
        compiler_params=_cparams("arbitrary"))(*dys, x, g, _head_sum_matrix(1.0 / DIL_HD), fold)


def _rowdot(a, b, name, tm):
    n, d = a.shape
    tm = min(tm, n)

    def body(a_ref, b_ref, o_ref):
        o_ref[...] = jnp.sum(a_ref[...].astype(F32) * b_ref[...].astype(F32), axis=-1, keepdims=True)

    spec = pl.BlockSpec((tm, d), lambda i: (i, 0))
    return _pcall(body, name=name, grid=(n // tm,), in_specs=[spec, spec],
                  out_specs=pl.BlockSpec((tm, 1), lambda i: (i, 0)), out_shape=_sds((n, 1), F32),
                  compiler_params=_cparams("parallel"))(a, b)


def _sum_branches(parts, name, tm):
    t = parts[0].shape[0]
    w = DIL_WIDTH
    tm = min(tm, t)

    def body(a_ref, b_ref, c_ref, o_ref, s1, s2):
        o_ref[...] = a_ref[...] + _from_view(b_ref, s1, DIL_DILATIONS[1]) + _from_view(c_ref, s2, DIL_DILATIONS[2])

    views, _ = _view_specs(tm, t, F32)
    return _pcall(body, name=name, grid=(t // tm,), in_specs=list(views),
                  out_specs=pl.BlockSpec((tm, w), lambda i: (i, 0)), out_shape=_sds((t, w), F32),
                  scratch_shapes=[_view_scratch(tm)] * 2,
                  compiler_params=_cparams("parallel"))(*parts)


def _rope_tables(t):
    inv = ROPE_BASE ** (-np.arange(0, MLA_ROPE, 2, dtype=np.float64) / MLA_ROPE)
    ang = np.arange(t, dtype=np.float64)[:, None] * inv[None, :]
    cos, sin = np.cos(ang), np.sin(ang)
    return (jnp.asarray(np.concatenate([cos, cos], 1), F32), jnp.asarray(np.concatenate([-sin, sin], 1), F32))


def _swap_halves(a):
    half = MLA_ROPE // 2
    return jnp.concatenate([a[:, half:], a[:, :half]], axis=1)


def _mla_qk_fwd(x, g, cos_t, sin_t, scale, name, tm):
    n, d = x.shape
    t = cos_t.shape[0]
    tm = min(tm, t)
    nt = t // tm

    def body(x_ref, g_ref, c_ref, s_ref, o_ref):
        xf = x_ref[...]
        r = lax.rsqrt(jnp.mean(xf * xf, axis=-1, keepdims=True) + EPS)
        y = xf * r * g_ref[...]
        yr = y[:, MLA_NOPE:]
        o_ref[:, :MLA_NOPE] = (y[:, :MLA_NOPE] * scale).astype(o_ref.dtype)
        o_ref[:, MLA_NOPE:] = ((yr * c_ref[...] + _swap_halves(yr) * s_ref[...]) * scale).astype(o_ref.dtype)

    row = pl.BlockSpec((tm, d), lambda i: (i, 0))
    tab = pl.BlockSpec((tm, MLA_ROPE), lambda i: (i % nt, 0))
    return _pcall(
        body, name=name, grid=(n // tm,),
        in_specs=[row, pl.BlockSpec((1, d), lambda i: (0, 0)), tab, tab],
        out_specs=row, out_shape=_sds((n, d), BF16),
        compiler_params=_cparams("parallel"))(x, g, cos_t, sin_t)


def _mla_qk_bwd(dy, x, g, cos_t, sin_t, scale, name, tm):
    n, d = x.shape
    t = cos_t.shape[0]
    tm = min(tm, t)
    nt = t // tm

    def body(dy_ref, x_ref, g_ref, c_ref, s_ref, dx_ref, dg_ref):
        xf = x_ref[...]
        gg = g_ref[...]
        r = lax.rsqrt(jnp.mean(xf * xf, axis=-1, keepdims=True) + EPS)
        xh = xf * r
        dyf = dy_ref[...] * scale
        dyr = dyf[:, MLA_NOPE:]
        dn_n = dyf[:, :MLA_NOPE]
        dn_r = dyr * c_ref[...] + _swap_halves(dyr * s_ref[...])
        xh_n, xh_r = xh[:, :MLA_NOPE], xh[:, MLA_NOPE:]
        dxh_n = dn_n * gg[:, :MLA_NOPE]
        dxh_r = dn_r * gg[:, MLA_NOPE:]
        mean = (jnp.sum(dxh_n * xh_n, axis=-1, keepdims=True)
                + jnp.sum(dxh_r * xh_r, axis=-1, keepdims=True)) * (1.0 / d)
        dx_ref[:, :MLA_NOPE] = r * (dxh_n - xh_n * mean)
        dx_ref[:, MLA_NOPE:] = r * (dxh_r - xh_r * mean)

        @pl.when(pl.program_id(0) == 0)
        def _():
            dg_ref[...] = jnp.zeros_like(dg_ref)

        dg_ref[:, :MLA_NOPE] += jnp.sum(dn_n * xh_n, axis=0, keepdims=True)
        dg_ref[:, MLA_NOPE:] += jnp.sum(dn_r * xh_r, axis=0, keepdims=True)

    row = pl.BlockSpec((tm, d), lambda i: (i, 0))
    vec = pl.BlockSpec((1, d), lambda i: (0, 0))
    tab = pl.BlockSpec((tm, MLA_ROPE), lambda i: (i % nt, 0))
    return _pcall(
        body, name=name, grid=(n // tm,),
        in_specs=[row, row, vec, tab, tab],
        out_specs=(row, vec), out_shape=(_sds((n, d), F32), _sds((1, d), F32)),
        compiler_params=_cparams("arbitrary"))(dy, x, g, cos_t, sin_t)


def _causal_mask(i, j, tq, tk, width):
    row = i * tq + lax.broadcasted_iota(jnp.int32, (tq, width), 0)
    col = j * tk + lax.broadcasted_iota(jnp.int32, (tq, width), 1)
    return col <= row


def _causal_steps(nq, nk, tq, tk, q_major):
    if q_major:
        groups = [[(i, j) for j in range((i * tq + tq - 1) // tk + 1)] for i in range(nq)]
        nunit = tk // tq if tk % tq == 0 else 1
    else:
        groups = [[(i, j) for i in range((j * tk) // tq, nq)] for j in range(nk)]
        nunit = tq // tk if tq % tk == 0 else 1
    it, jt, fl = [], [], []
    for g in groups:
        for n, (i, j) in enumerate(g):
            crossing = j * tk + tk - 1 > i * tq
            if q_major:
                unit = tk // nunit
                u = min(nunit, -(-(i * tq + tq - j * tk) // unit)) - 1
            else:
                unit = tq // nunit
                u = max(0, j * tk - i * tq) // unit
            it.append(i)
            jt.append(j)
            fl.append((n == 0) + 2 * (n == len(g) - 1) + 4 * crossing + 8 * (u if crossing else 0))
    return tuple(jnp.asarray(np.array(a, np.int32)) for a in (it, jt, fl)), nunit


def _by_crossing(flags, nunit, update):
    pl.when((flags & 4) == 0)(functools.partial(update, None))
    for u in range(nunit):
        pl.when(((flags & 4) != 0) & ((flags >> 3) == u))(functools.partial(update, u))


def _causal_specs(tq, tk):
    def qs(w):
        return pl.BlockSpec((None, tq, w), lambda h, s, it, jt, fl: (h, it[s], 0))

    def kv(w):
        return pl.BlockSpec((None, tk, w), lambda h, s, it, jt, fl: (h, jt[s], 0))

    return qs, kv


def _mla_fwd(q, k, v, tq, tk):
    nh, t, dq = q.shape
    dv = v.shape[2]
    tq, tk = min(tq, t), min(tk, t)
    tables, nunit = _causal_steps(t // tq, t // tk, tq, tk, True)

    def body(it, jt, fl, q_ref, k_ref, v_ref, o_ref, lse_ref, m_sc, l_sc, acc_sc):
        step = pl.program_id(1)
        i, j, flags = it[step], jt[step], fl[step]

        @pl.when((flags & 1) != 0)
        def _():
            m_sc[...] = jnp.full_like(m_sc, NEG)
            l_sc[...] = jnp.zeros_like(l_sc)
            acc_sc[...] = jnp.zeros_like(acc_sc)

        def update(units):
            wk = tk if units is None else (units + 1) * (tk // nunit)
            s = _dot(q_ref[...], k_ref[:wk, :], NT)
            if units is not None:
                s = jnp.where(_causal_mask(i, j, tq, tk, wk), s, NEG)
            m_prev = m_sc[...]
            m_new = jnp.maximum(m_prev, jnp.max(s, axis=-1, keepdims=True))
            alpha = jnp.exp(m_prev - m_new)
            p = jnp.exp(s - m_new)
            l_sc[...] = alpha * l_sc[...] + jnp.sum(p, axis=-1, keepdims=True)
            acc_sc[...] = alpha * acc_sc[...] + _dot(p.astype(BF16), v_ref[:wk, :], NN)
            m_sc[...] = m_new

        _by_crossing(flags, nunit, update)

        @pl.when((flags & 2) != 0)
        def _():
            o_ref[...] = acc_sc[...] / l_sc[...]
            lse_ref[...] = m_sc[...] + jnp.log(l_sc[...])

    qs, kv = _causal_specs(tq, tk)
    return _pcall(
        body, name="mla_attn_fwd",
        grid_spec=pltpu.PrefetchScalarGridSpec(
            num_scalar_prefetch=3, grid=(nh, tables[0].shape[0]),
            in_specs=[qs(dq), kv(dq), kv(dv)], out_specs=(qs(dv), qs(1)),
            scratch_shapes=[pltpu.VMEM((tq, 1), F32), pltpu.VMEM((tq, 1), F32), pltpu.VMEM((tq, dv), F32)]),
        out_shape=(_sds((nh, t, dv), F32), _sds((nh, t, 1), F32)),
        compiler_params=_cparams("parallel", "arbitrary"))(*tables, q, k, v)


def _mla_bwd(q, k, k_t, v, do, lse_row, dl_row, tq, tk):
    nh, t, dq = q.shape
    dv = v.shape[2]
    tq, tk = min(tq, t), min(tk, t)
    nq = t // tq
    tables, nunit = _causal_steps(nq, t // tk, tq, tk, False)

    def body(it, jt, fl, q_ref, k_ref, kt_ref, v_ref, do_ref, lse_ref, dl_ref, dk_ref, dv_ref, dq_ref, dk_sc, dv_sc):
        step = pl.program_id(1)
        i, j, flags = it[step], jt[step], fl[step]

        def update(units):
            off = 0 if units is None else units * (tq // nunit)
            qq = q_ref[off:, :]
            st = _dot(k_ref[...], qq, NT)
            if units is not None:
                key = j * tk + lax.broadcasted_iota(jnp.int32, (tk, tq - off), 0)
                qry = i * tq + off + lax.broadcasted_iota(jnp.int32, (tk, tq - off), 1)
                st = jnp.where(key <= qry, st, NEG)
            pt = jnp.exp(st - lse_ref[:, off:])
            dob = do_ref[off:, :].astype(BF16)
            dpt = _dot(v_ref[...], dob, NT)
            dst = pt * (dpt - dl_ref[:, off:])
            dsb = dst.astype(BF16)
            dv_part = _dot(pt.astype(BF16), dob, NN)
            dk_part = _dot(dsb, qq, NN)
            dq_part = _dot(kt_ref[...], dsb, NN)

            @pl.when((flags & 1) != 0)
            def _():
                dv_sc[...] = dv_part
                dk_sc[...] = dk_part

            @pl.when((flags & 1) == 0)
            def _():
                dv_sc[...] += dv_part
                dk_sc[...] += dk_part

            if off == 0:
                @pl.when(j == 0)
                def _():
                    dq_ref[i] = dq_part

                @pl.when(j != 0)
                def _():
                    dq_ref[i] += dq_part
            else:
                dq_ref[i, :, off:] += dq_part

        _by_crossing(flags, nunit, update)

        @pl.when((flags & 2) != 0)
        def _():
            dk_ref[...] = dk_sc[...]
            dv_ref[...] = dv_sc[...]

    qs, kv = _causal_specs(tq, tk)
    rowv = pl.BlockSpec((None, 1, tq), lambda h, s, it, jt, fl: (h, 0, it[s]))
    ktv = pl.BlockSpec((None, dq, tk), lambda h, s, it, jt, fl: (h, 0, jt[s]))
    whole = pl.BlockSpec((None, nq, dq, tq), lambda h, s, it, jt, fl: (h, 0, 0, 0))
    return _pcall(
        body, name="mla_attn_bwd",
        grid_spec=pltpu.PrefetchScalarGridSpec(
            num_scalar_prefetch=3, grid=(nh, tables[0].shape[0]),
            in_specs=[qs(dq), kv(dq), ktv, kv(dv), qs(dv), rowv, rowv], out_specs=(kv(dq), kv(dv), whole),
            scratch_shapes=[pltpu.VMEM((tk, dq), F32), pltpu.VMEM((tk, dv), F32)]),
        out_shape=(_sds((nh, t, dq), F32), _sds((nh, t, dv), F32), _sds((nh, nq, dq, tq), F32)),
        compiler_params=_cparams("parallel", "arbitrary"))(*tables, q, k, k_t, v, do, lse_row, dl_row)


def _loss_head(y, target, tm):
    t, d = y.shape
    tm = min(tm, t)
    nt = t // tm

    def body(y_ref, t_ref, dy_ref, loss_ref, acc):
        i = pl.program_id(0)
        err = y_ref[...] - t_ref[...]
        dy_ref[...] = err * (1.0 / d)

        @pl.when(i == 0)
        def _():
            acc[...] = jnp.zeros_like(acc)

        acc[...] += jnp.sum(err * err, axis=0, keepdims=True)

        @pl.when(i == nt - 1)
        def _():
            loss_ref[0, 0] = jnp.sum(acc[...]) * (0.5 / d)

    spec = pl.BlockSpec((tm, d), lambda i: (i, 0))
    return _pcall(
        body, name="loss_head", grid=(nt,), in_specs=[spec, spec],
        out_specs=(spec, pl.BlockSpec(memory_space=pltpu.SMEM)),
        out_shape=(_sds((t, d), F32), _sds((1, 1), F32)),
        scratch_shapes=[pltpu.VMEM((1, d), F32)],
        compiler_params=_cparams("arbitrary"))(y, target)


def _adamw(w, g, m, v, name):
    r, c = w.shape
    tr = r
    for cand in (256, 128, 64, 32, 16, 8):
        if r % cand == 0:
            tr = cand
            break

    def body(w_ref, g_ref, m_ref, v_ref, d_ref, nm_ref, nv_ref):
        gg = g_ref[...]
        nm = ADAM_B1 * m_ref[...] + (1.0 - ADAM_B1) * gg
        nv = ADAM_B2 * v_ref[...] + (1.0 - ADAM_B2) * (gg * gg)
        m_hat = nm / (1.0 - ADAM_B1 ** ADAM_STEP)
        v_hat = nv / (1.0 - ADAM_B2 ** ADAM_STEP)
        d_ref[...] = -ADAM_LR * (m_hat / (jnp.sqrt(v_hat) + ADAM_EPS) + ADAM_WD * w_ref[...])
        nm_ref[...] = nm
        nv_ref[...] = nv

    spec = pl.BlockSpec((tr, c), lambda i: (i, 0))
    sd = _sds((r, c), F32)
    return _pcall(body, name=name, grid=(r // tr,), in_specs=[spec] * 4, out_specs=(spec,) * 3,
                  out_shape=(sd, sd, sd), compiler_params=_cparams("parallel"))(w, g, m, v)


MESH_ID = pl.DeviceIdType.MESH
HBM_SPEC = pl.BlockSpec(memory_space=pltpu.HBM)


def _place():
    return lax.axis_index("x"), lax.axis_index("y"), lax.axis_index("c")


def _other_chips(x, y):
    return [(1 - x, y), (x, 1 - y), (1 - x, 1 - y)]


def _remote(src, dst, send_sems, recv_sems, k, to):
    return pltpu.make_async_remote_copy(src_ref=src, dst_ref=dst, send_sem=send_sems.at[k], recv_sem=recv_sems.at[k],
                                        device_id=to, device_id_type=MESH_ID)


def _halves(arrays):
    for a in arrays:
        assert a.shape[-2] % 32 == 0
    return [a.shape[-2] // 2 for a in arrays]


def _gather_start(srcs, outs, halves, send_sems, recv_sems):
    x, y, c = _place()
    for a, half in enumerate(halves):
        rows = pl.ds(c * half, half)
        for k, (cx, cy) in enumerate(_other_chips(x, y)):
            _remote(srcs[a].at[rows, :], outs[a].at[2 * x + y, rows, :], send_sems, recv_sems, 3 * a + k,
                    (cx, cy, c)).start()


def _gather_wait(outs, halves, send_sems, recv_sems):
    x, y, c = _place()
    for a, half in enumerate(halves):
        for k, (cx, cy) in enumerate(_other_chips(x, y)):
            got = outs[a].at[2 * cx + cy, pl.ds(c * half, half), :]
            _remote(got, got, send_sems, recv_sems, 3 * a + k, (x, y, c)).wait()


def _forward_cores(partly):
    n = len(partly)
    halves = _halves(partly)

    def body(*refs):
        srcs, outs, send_sems, recv_sems = refs[:n], refs[n:2 * n], refs[2 * n], refs[2 * n + 1]
        x, y, c = _place()
        for a, half in enumerate(halves):
            for k, (cx, cy) in enumerate(_other_chips(x, y)):
                rows = pl.ds(c * half, half)
                _remote(srcs[a].at[2 * cx + cy, rows, :], outs[a].at[2 * cx + cy, rows, :], send_sems, recv_sems,
                        3 * a + k, (x, y, 1 - c)).start()
        for a, half in enumerate(halves):
            for k, (cx, cy) in enumerate(_other_chips(x, y)):
                mine = outs[a].at[2 * cx + cy, pl.ds(c * half, half), :]
                theirs = outs[a].at[2 * cx + cy, pl.ds((1 - c) * half, half), :]
                _remote(mine, theirs, send_sems, recv_sems, 3 * a + k, (x, y, c)).wait()

    return _pcall(
        body, name="forward_cores", in_specs=[HBM_SPEC] * n, out_specs=tuple([HBM_SPEC] * n),
        out_shape=tuple(_sds(p.shape, p.dtype) for p in partly), input_output_aliases={a: a for a in range(n)},
        scratch_shapes=[pltpu.SemaphoreType.DMA((3 * n,)), pltpu.SemaphoreType.DMA((3 * n,))],
    )(*partly)


def _gather_weights(blocks):
    n = len(blocks)
    halves = _halves(blocks)

    def body(*refs):
        srcs, outs, send_sems, recv_sems = refs[:n], refs[n:2 * n], refs[2 * n], refs[2 * n + 1]
        x, y, c = _place()
        me = 2 * x + y
        sibling = (x, y, 1 - c)
        chips = _other_chips(x, y)

        def part(a, chip, core):
            return outs[a].at[chip, pl.ds(core * halves[a], halves[a]), :]

        for a in range(n):
            mine = srcs[a].at[pl.ds(c * halves[a], halves[a]), :]
            for k, (cx, cy) in enumerate(chips):
                _remote(mine, part(a, me, c), send_sems, recv_sems, 6 * a + k, (cx, cy, c)).start()
        for k, (cx, cy) in enumerate(chips):
            for a in range(n):
                got = part(a, 2 * cx + cy, c)
                _remote(got, got, send_sems, recv_sems, 6 * a + k, (x, y, c)).wait_recv()
                _remote(got, got, send_sems, recv_sems, 6 * a + 3 + k, sibling).start()
        for k, (cx, cy) in enumerate(chips):
            for a in range(n):
                got = part(a, 2 * cx + cy, 1 - c)
                _remote(got, got, send_sems, recv_sems, 6 * a + 3 + k, (x, y, c)).wait_recv()
        for a in range(n):
            sent = part(a, me, c)
            for k in range(6):
                _remote(sent, sent, send_sems, recv_sems, 6 * a + k, (x, y, c)).wait_send()

    return _pcall(
        body, name="gather_weights", in_specs=[HBM_SPEC] * n, out_specs=tuple([HBM_SPEC] * n),
        out_shape=tuple(_sds((N_CHIPS,) + b.shape, b.dtype) for b in blocks),
        scratch_shapes=[pltpu.SemaphoreType.DMA((6 * n,)), pltpu.SemaphoreType.DMA((6 * n,))],
    )(*blocks)


def _reduce_cores(grads, tag):
    n = len(grads)
    halves = _halves(grads)

    def body(*refs):
        gs, outs, send_sems, recv_sems = refs[:n], refs[n:2 * n], refs[2 * n], refs[2 * n + 1]
        x, y, c = _place()
        for a in range(n):
            for j in range(N_CHIPS):
                _remote(gs[a].at[j, pl.ds((1 - c) * halves[a], halves[a]), :], outs[a].at[j],
                        send_sems, recv_sems, a, (x, y, 1 - c)).start()
        for a in range(n):
            _remote(gs[a].at[:, pl.ds((1 - c) * halves[a], halves[a]), :], outs[a],
                    send_sems, recv_sems, a, (x, y, c)).wait()

    return _pcall(
        body, name=f"reduce_cores_{tag}", in_specs=[HBM_SPEC] * n, out_specs=tuple([HBM_SPEC] * n),
        out_shape=tuple(_sds((N_CHIPS, h, g.shape[2]), g.dtype) for g, h in zip(grads, halves)),
        scratch_shapes=[pltpu.SemaphoreType.DMA((n,)), pltpu.SemaphoreType.DMA((n,))],
    )(*grads)


def _scatter_shapes(parts):
    return tuple(_sds((3,) + p.shape[1:], p.dtype) for p in parts)


def _scatter_sems(n):
    return [pltpu.SemaphoreType.DMA((3 * n,)), pltpu.SemaphoreType.DMA((3 * n,))]


def _scatter_start(ps, outs, send_sems, recv_sems):
    x, y, c = _place()
    for a in range(len(ps)):
        for k, (cx, cy) in enumerate(_other_chips(x, y)):
            _remote(ps[a].at[2 * cx + cy], outs[a].at[k], send_sems, recv_sems, 3 * a + k, (cx, cy, c)).start()


def _scatter_wait(ps, outs, send_sems, recv_sems):
    x, y, c = _place()
    for a in range(len(ps)):
        for k in range(3):
            _remote(ps[a].at[k], outs[a].at[k], send_sems, recv_sems, 3 * a + k, (x, y, c)).wait()


def _sum_partials(received, parts, place):
    n = len(parts)
    steps = 2
    tiles = [p.shape[1] // steps for p in parts]

    def body(place_ref, *refs):
        rs, ps, outs = refs[:n], refs[n:2 * n], refs[2 * n:]
        for a in range(n):
            tot = ps[a][...].astype(F32)
            for k in range(3):
                tot = tot + rs[a][k].astype(F32)
            outs[a][...] = tot

    cols = [p.shape[2] for p in parts]
    return _pcall(
        body, name="sum_chip_partials",
        grid_spec=pltpu.PrefetchScalarGridSpec(
            num_scalar_prefetch=1, grid=(steps,),
            in_specs=[pl.BlockSpec((3, tm, w), lambda i, pc: (0, i, 0)) for tm, w in zip(tiles, cols)]
            + [pl.BlockSpec((None, tm, w), lambda i, pc: (pc[0], i, 0)) for tm, w in zip(tiles, cols)],
            out_specs=tuple(pl.BlockSpec((tm, w), lambda i, pc: (pc[1] * steps + i, 0)) for tm, w in zip(tiles, cols))),
        out_shape=tuple(_sds((2 * p.shape[1], p.shape[2]), F32) for p in parts),
        compiler_params=_cparams("parallel"))(place, *received, *parts)


def _share_cores(blocks):
    n = len(blocks)
    halves = _halves(blocks)

    def body(*refs):
        srcs, outs, send_sems, recv_sems = refs[:n], refs[n:2 * n], refs[2 * n], refs[2 * n + 1]
        x, y, c = _place()
        for a in range(n):
            piece = pl.ds(c * halves[a], halves[a])
            _remote(srcs[a].at[piece, :], outs[a].at[piece, :], send_sems, recv_sems, a, (x, y, 1 - c)).start()
        for a in range(n):
            mine = outs[a].at[pl.ds(c * halves[a], halves[a]), :]
            theirs = outs[a].at[pl.ds((1 - c) * halves[a], halves[a]), :]
            _remote(mine, theirs, send_sems, recv_sems, a, (x, y, c)).wait()

    return _pcall(
        body, name="share_cores", in_specs=[HBM_SPEC] * n, out_specs=tuple([HBM_SPEC] * n),
        out_shape=tuple(_sds(b.shape, b.dtype) for b in blocks), input_output_aliases={a: a for a in range(n)},
        scratch_shapes=[pltpu.SemaphoreType.DMA((n,)), pltpu.SemaphoreType.DMA((n,))],
    )(*blocks)


def _sum_blocks(stacked, name, tm):
    n, rows, lanes = stacked.shape
    tm = min(tm, rows)

    def body(s_ref, o_ref):
        tot = s_ref[n - 1].astype(F32)
        for k in range(n - 1):
            tot = tot + s_ref[k].astype(F32)
        o_ref[...] = tot

    return _pcall(body, name=name, grid=(rows // tm,),
                  in_specs=[pl.BlockSpec((n, tm, lanes), lambda i: (0, i, 0))],
                  out_specs=pl.BlockSpec((tm, lanes), lambda i: (i, 0)), out_shape=_sds((rows, lanes), F32),
                  compiler_params=_cparams("parallel"))(stacked)


def _add_halves(grads, theirs, core, tag):
    n = len(grads)
    steps = 2
    tiles = [t.shape[1] // steps for t in theirs]
    cols = [t.shape[2] for t in theirs]

    def body(c_ref, *refs):
        gs, ts, outs = refs[:n], refs[n:2 * n], refs[2 * n:]
        for a in range(n):
            outs[a][...] = (gs[a][...] + ts[a][...]).astype(BF16)

    own = [pl.BlockSpec((None, tm, w), lambda k, i, c: (k, c[0] * steps + i, 0)) for tm, w in zip(tiles, cols)]
    same = [pl.BlockSpec((None, tm, w), lambda k, i, c: (k, i, 0)) for tm, w in zip(tiles, cols)]
    return _pcall(
        body, name=f"add_core_halves_{tag}",
        grid_spec=pltpu.PrefetchScalarGridSpec(
            num_scalar_prefetch=1, grid=(N_CHIPS, steps), in_specs=own + same, out_specs=tuple(same)),
        out_shape=tuple(_sds(t.shape, BF16) for t in theirs),
        compiler_params=_cparams("parallel", "parallel"))(core, *grads, *theirs)


def _allreduce_small(part):
    rows, lanes = part.shape
    ndev = 8

    def body(src, tot, buf, send_sems, recv_sems):
        x, y, c = _place()
        me = 4 * x + 2 * y + c
        buf[me] = src[...]
        sends = []
        for k in range(1, ndev):
            peer = (x ^ (k >> 2), y ^ ((k >> 1) & 1), c ^ (k & 1))
            cp = _remote(src, buf.at[me], send_sems, recv_sems, k - 1, peer)
            cp.start()
            sends.append(cp)
        for k in range(1, ndev):
            theirs = buf.at[me ^ k]
            _remote(theirs, theirs, send_sems, recv_sems, k - 1, (x, y, c)).wait_recv()
        for cp in sends:
            cp.wait_send()
        acc = buf[0]
        for d in range(1, ndev):
            acc = acc + buf[d]
        tot[...] = acc

    vm = pl.BlockSpec(memory_space=pltpu.VMEM)
    return _pcall(
        body, name="allreduce_small", in_specs=[vm], out_specs=vm, out_shape=_sds((rows, lanes), F32),
        scratch_shapes=[pltpu.VMEM((ndev, rows, lanes), F32), pltpu.SemaphoreType.DMA((ndev - 1,)),
                        pltpu.SemaphoreType.DMA((ndev - 1,))],
    )(part)


def _pack_small(vals):
    parts = []
    for name, shape, r in SMALL:
        flat = vals[name].reshape(-1).astype(F32)
        parts.append(jnp.pad(flat, (0, r * LANES - flat.shape[0])).reshape(r, LANES))
    used = sum(r for _, _, r in SMALL)
    parts.append(jnp.zeros((SMALL_ROWS - used, LANES), F32))
    return jnp.concatenate(parts, axis=0)


def _unpack_small(packed):
    out, off = {}, 0
    for name, shape, r in SMALL:
        n = int(np.prod(shape))
        out[name] = packed[off:off + r].reshape(-1)[:n].reshape(shape)
        off += r
    return out


def _heads_major(a, nh):
    t = a.shape[0]
    return a.reshape(t, nh, a.shape[1] // nh).transpose(1, 0, 2)


def _tokens_major(a):
    nh, t, w = a.shape
    return a.transpose(1, 0, 2).reshape(t, nh * w)


LATE = ("ffn1_w_gate", "ffn1_w_up", "ffn1_w_down")
EARLY = tuple(name for name, _ in BIG if name not in LATE)


def _local_step(x, target, small, wfull, exchanges=None, later_weights=None):
    t = x.shape[0]
    nh, hd = DIL_HEADS, DIL_HD
    grads_s, grads_b = {}, {}

    x1, ffn1_saved, partly = _ffn_fwd(x, small["ffn1_norm"], wfull["ffn1_w_gate"], wfull["ffn1_w_up"],
                                      wfull["ffn1_w_down"], "ffn1", later_weights[0] if later_weights else ())
    if later_weights:
        wfull = {**wfull, **later_weights[1](partly)}
    w_in = wfull["w_in"].transpose(1, 0, 2).reshape(D_MODEL, -1)
    w_out = wfull["w_out"].reshape(D_MODEL, D_MODEL)
    w_qb, w_kvb = wfull["mla_w_q_b"], wfull["mla_w_kv_b"]
    hm = _rms_fwd(x1, small["mix_norm"], BF16, "mix_norm", 512)
    proj = _mm_simple("in_proj", hm, w_in, NN, F32, tm=1024)
    cq, ckv, k_pe = proj[:, 1536:1792], proj[:, 1792:1920], proj[:, 1920:1984]

    gq, gk = jnp.tile(small["dil_q_norm"], (1, nh)), jnp.tile(small["dil_k_norm"], (1, nh))
    qn = _head_norm_fwd(proj, 0, gq, "dil_q_norm", 512)
    kn = _head_norm_fwd(proj, 1, gk, "dil_k_norm", 512)
    v_d = _head_norm_fwd(proj, 2, None, "dil_v_views", 512)
    bias = _bias_tiles(small["rel_bias"]).reshape(3, nh // 2, 2 * QB, QB + DIL_W)
    outs, lses = [], []
    for b, dil in enumerate(DIL_DILATIONS):
        o_b, lse_b = _dil_fwd(qn[b], kn[b], v_d[b], bias[b], dil, f"dil_fwd_{dil}")
        outs.append(o_b)
        lses.append(lse_b)
    o_dil, lse_tot, od = _dil_merge(outs, lses, small["out_norm_dil"], 512)

    mh = MLA_HEADS
    cos_t, sin_t = _rope_tables(t)
    cqn = _rms_fwd(cq, small["mla_q_a_norm"], BF16, "mla_q_a_norm", 512)
    ckvn = _rms_fwd(ckv, small["mla_kv_a_norm"], BF16, "mla_kv_a_norm", 512)
    tm = min(512, t)

    th = min(2048, t)

    def head_proj(name, a, w, width):
        k = a.shape[1]
        return _mm(name, (mh, t // th, 1),
                   [(a, pl.BlockSpec((th, k), lambda h, i, r: (i, 0)), w, pl.BlockSpec((None, k, width), lambda h, i, r: (h, 0, 0)))],
                   NN, _sds((mh, t, width), F32), pl.BlockSpec((None, th, width), lambda h, i, r: (h, i, 0)), (th, width))

    q_raw = head_proj("mla_q_proj", cqn, w_qb, MLA_QK)
    kv_raw = head_proj("mla_kv_proj", ckvn, w_kvb, MLA_NOPE + MLA_V)
    k_raw = jnp.concatenate([kv_raw[:, :, :MLA_NOPE], jnp.broadcast_to(k_pe[None], (mh, t, MLA_ROPE))], axis=2)
    v_m = kv_raw[:, :, MLA_NOPE:].astype(BF16)
    q_raw2, k_raw2 = q_raw.reshape(mh * t, MLA_QK), k_raw.reshape(mh * t, MLA_QK)
    q_scale = MLA_QK ** -0.5
    q_m = _mla_qk_fwd(q_raw2, small["mla_q_norm"], cos_t, sin_t, q_scale, "mla_q_rope", 2048).reshape(mh, t, MLA_QK)
    k_m = _mla_qk_fwd(k_raw2, small["mla_k_norm"], cos_t, sin_t, 1.0, "mla_k_rope", 2048).reshape(mh, t, MLA_QK)
    o_mla_h, lse_m = _mla_fwd(q_m, k_m, v_m, 512, 4096)
    o_mla = _tokens_major(o_mla_h)

    om = _rms_fwd(o_mla, small["out_norm_mla"], BF16, "out_norm_mla", 512)
    half_w = DIL_WIDTH
    row = pl.BlockSpec((tm, D_MODEL), lambda i, j, r: (i, 0))
    act_spec = pl.BlockSpec((tm, half_w), lambda i, j, r: (i, 0))
    x2 = _mm("out_proj", (t // tm, 1, 1),
             [(od, act_spec, w_out, pl.BlockSpec((half_w, D_MODEL), lambda i, j, r: (0, 0))),
              (om, act_spec, w_out, pl.BlockSpec((half_w, D_MODEL), lambda i, j, r: (1, 0)))],
             NN, _sds((t, D_MODEL), F32), row, (tm, D_MODEL), res=(x1, row))
    x3, ffn2_saved, _ = _ffn_fwd(x2, small["ffn2_norm"], wfull["ffn2_w_gate"], wfull["ffn2_w_up"],
                                 wfull["ffn2_w_down"], "ffn2")
    dy, loss = _loss_head(x3, target, 512)

    dx2, grads_s["ffn2_norm"], grads_b["ffn2_w_gate"], grads_b["ffn2_w_up"], grads_b["ffn2_w_down"], _, _ = _ffn_bwd(
        dy, x2, small["ffn2_norm"], wfull["ffn2_w_gate"], wfull["ffn2_w_up"], wfull["ffn2_w_down"], ffn2_saved, "ffn2")

    d_ocat = _mm_simple("out_proj_dx", dx2, w_out, NT, F32, tm=1024)
    dw_out_d = _mm_simple("out_proj_dw_dil", od, dx2, TN, F32, tk=2048)
    dw_out_m = _mm_simple("out_proj_dw_mla", om, dx2, TN, F32, tk=2048)
    grads_b["w_out"] = jnp.concatenate([dw_out_d, dw_out_m], axis=0).reshape(N_CHIPS, D_MODEL // N_CHIPS, D_MODEL)
    do_dil, grads_s["out_norm_dil"] = _rms_bwd([d_ocat[:, :half_w]], o_dil, small["out_norm_dil"], None, "out_norm_dil_bwd", 512)
    do_mla, grads_s["out_norm_mla"] = _rms_bwd([d_ocat[:, half_w:]], o_mla, small["out_norm_mla"], None, "out_norm_mla_bwd", 512)

    do_m = _heads_major(do_mla, mh)
    dl_m = _rowdot(do_m.reshape(mh * t, MLA_V), o_mla_h.reshape(mh * t, MLA_V), "mla_delta", 2048).reshape(mh, t, 1)
    dk_m, dv_m, dq_t = _mla_bwd(q_m, k_m, k_m.transpose(0, 2, 1), v_m, do_m, lse_m.reshape(mh, 1, t),
                                dl_m.reshape(mh, 1, t), 2048, 512)
    dq_m = dq_t.transpose(0, 1, 3, 2).reshape(mh, t, MLA_QK)
    dq_raw, grads_s["mla_q_norm"] = _mla_qk_bwd(dq_m.reshape(mh * t, MLA_QK), q_raw2, small["mla_q_norm"],
                                                 cos_t, sin_t, q_scale, "mla_q_rope_bwd", 2048)
    dk_raw, grads_s["mla_k_norm"] = _mla_qk_bwd(dk_m.reshape(mh * t, MLA_QK), k_raw2, small["mla_k_norm"],
                                                 cos_t, sin_t, 1.0, "mla_k_rope_bwd", 2048)
    dq_raw = dq_raw.reshape(mh, t, MLA_QK)
    dk_raw = dk_raw.reshape(mh, t, MLA_QK)
    dkv_raw = jnp.concatenate([dk_raw[:, :, :MLA_NOPE], dv_m], axis=2)
    dk_pe_h = dk_raw[:, :, MLA_NOPE:]

    def head_proj_dx(name, d, w):
        width, k = d.shape[2], w.shape[1]
        pairs = [(d, pl.BlockSpec((None, th, width), lambda i, j, r, h=h: (h, i, 0)),
                  w, pl.BlockSpec((None, k, width), lambda i, j, r, h=h: (h, 0, 0))) for h in range(mh)]
        return _mm(name, (t // th, 1, 1), pairs, NT, _sds((t, k), F32),
                   pl.BlockSpec((th, k), lambda i, j, r: (i, 0)), (th, k))

    def head_proj_dw(name, a, d):
        width, k = d.shape[2], a.shape[1]
        return _mm(name, (mh, 1, t // th),
                   [(a, pl.BlockSpec((th, k), lambda h, j, r: (r, 0)), d, pl.BlockSpec((None, th, width), lambda h, j, r: (h, r, 0)))],
                   TN, _sds((mh, k, width), F32), pl.BlockSpec((None, k, width), lambda h, j, r: (h, 0, 0)), (k, width))

    d_cqn = head_proj_dx("mla_q_proj_dx", dq_raw, w_qb)
    d_ckvn = head_proj_dx("mla_kv_proj_dx", dkv_raw, w_kvb)
    grads_b["mla_w_q_b"] = head_proj_dw("mla_q_proj_dw", cqn, dq_raw)
    grads_b["mla_w_kv_b"] = head_proj_dw("mla_kv_proj_dw", ckvn, dkv_raw)
    d_cq, grads_s["mla_q_a_norm"] = _rms_bwd([d_cqn], cq, small["mla_q_a_norm"], None, "mla_q_a_norm_bwd", 512)
    d_ckv, grads_s["mla_kv_a_norm"] = _rms_bwd([d_ckvn], ckv, small["mla_kv_a_norm"], None, "mla_kv_a_norm_bwd", 512)
    d_kpe = _sum_blocks(dk_pe_h.reshape(mh, t * MLA_ROPE // LANES, LANES), "mla_kpe_sum", 1024).reshape(t, MLA_ROPE)

    stats, do_db = _dil_stats(do_dil, o_dil, lse_tot, 512)
    dqs, dks, dvs, dtiles = [], [], [], []
    for b, dil in enumerate(DIL_DILATIONS):
        dq_b, dk_b, dv_b, db_b = _dil_bwd(qn[b], kn[b], v_d[b], do_db[b], stats[b], bias[b], dil, f"dil_bwd_{dil}")
        dqs.append(dq_b)
        dks.append(dk_b)
        dvs.append(dv_b)
        dtiles.append(db_b)
    grads_s["rel_bias"] = _bias_grad(jnp.stack(dtiles).reshape(3, nh, QB, QB + DIL_W))
    dq_a, dgq = _head_norm_bwd(dqs, proj, 0, gq, "dil_q_norm_bwd", 512)
    dk_a, dgk = _head_norm_bwd(dks, proj, 1, gk, "dil_k_norm_bwd", 512)
    grads_s["dil_q_norm"], grads_s["dil_k_norm"] = dgq[:, :hd], dgk[:, :hd]
    dv_a = _sum_branches(dvs, "dil_dv_sum", 512)

    dparts = [dq_a, dk_a, dv_a, d_cq, d_ckv, d_kpe]
    t2 = min(1024, t)
    pairs, dw_parts, lo = [], [], 0
    for n, dpart in enumerate(dparts):
        width = dpart.shape[1]
        w_part = w_in[:, lo:lo + width]
        pairs.append((dpart, pl.BlockSpec((t2, width), lambda i, j, r: (i, 0)),
                      w_part, pl.BlockSpec((D_MODEL, width), lambda i, j, r: (0, 0))))
        dw_parts.append(_mm_simple(f"in_proj_dw_{n}", hm, dpart, TN, F32, tk=2048))
        lo += width
    d_hm = _mm("in_proj_dx", (t // t2, 1, 1), pairs, NT, _sds((t, D_MODEL), F32),
               pl.BlockSpec((t2, D_MODEL), lambda i, j, r: (i, 0)), (t2, D_MODEL))
    dw_in = jnp.concatenate(dw_parts, axis=1)
    grads_b["w_in"] = dw_in.reshape(D_MODEL, N_CHIPS, -1).transpose(1, 0, 2)
    dx1, grads_s["mix_norm"] = _rms_bwd([d_hm], x1, small["mix_norm"], dx2, "mix_norm_bwd", 512)
    outgoing = exchanges[0]([grads_b[n] for n in EARLY]) if exchanges else ()
    dx, grads_s["ffn1_norm"], grads_b["ffn1_w_gate"], grads_b["ffn1_w_up"], grads_b["ffn1_w_down"], arrived, late = _ffn_bwd(
        dx1, x, small["ffn1_norm"], wfull["ffn1_w_gate"], wfull["ffn1_w_up"], wfull["ffn1_w_down"], ffn1_saved, "ffn1",
        outgoing, exchanges[1] if exchanges else None)
    return loss, dx, grads_s, grads_b, (tuple(outgoing), arrived), late


def kernel(x, ffn1_norm, ffn1_w_gate, ffn1_w_up, ffn1_w_down, mix_norm, w_in, dil_q_norm, dil_k_norm, rel_bias, mla_q_a_norm, mla_w_q_b, mla_kv_a_norm, mla_w_kv_b, mla_q_norm, mla_k_norm, out_norm_dil, out_norm_mla, w_out, ffn2_norm, ffn2_w_gate, ffn2_w_up, ffn2_w_down, loss_target, m_ffn1_norm, m_ffn1_w_gate, m_ffn1_w_up, m_ffn1_w_down, m_mix_norm, m_w_in, m_dil_q_norm, m_dil_k_norm, m_rel_bias, m_mla_q_a_norm, m_mla_w_q_b, m_mla_kv_a_norm, m_mla_w_kv_b, m_mla_q_norm, m_mla_k_norm, m_out_norm_dil, m_out_norm_mla, m_w_out, m_ffn2_norm, m_ffn2_w_gate, m_ffn2_w_up, m_ffn2_w_down, v_ffn1_norm, v_ffn1_w_gate, v_ffn1_w_up, v_ffn1_w_down, v_mix_norm, v_w_in, v_dil_q_norm, v_dil_k_norm, v_rel_bias, v_mla_q_a_norm, v_mla_w_q_b, v_mla_kv_a_norm, v_mla_w_kv_b, v_mla_q_norm, v_mla_k_norm, v_out_norm_dil, v_out_norm_mla, v_w_out, v_ffn2_norm, v_ffn2_w_gate, v_ffn2_w_up, v_ffn2_w_down):
    given = dict(locals())
    big_names = [name for name, _ in BIG]
    small_names = [name for name, _, _ in SMALL]

    chip = (2 * lax.axis_index("x") + lax.axis_index("y")).astype(jnp.int32)
    core = lax.axis_index("c").astype(jnp.int32)
    mine = {n: given[n].astype(BF16) for n in big_names}

    def with_own(names, arrays):
        return {n: lax.dynamic_update_slice(a, mine[n], (chip, 0, 0)) for n, a in zip(names, arrays)}

    wfirst = with_own(LATE, _gather_weights([mine[n][0] for n in LATE]))
    later_weights = ([mine[n][0] for n in EARLY], lambda partly: with_own(EARLY, _forward_cores(partly)))
    small = {n: given[n] for n in small_names}

    def chip_partials(partial, tag):
        return _add_halves(partial, _reduce_cores(partial, tag), core.reshape(1), tag)

    exchanges = (functools.partial(chip_partials, tag="early"), functools.partial(chip_partials, tag="late"))
    loss, dx, grads_s, grads_b, (early_part, early_got), (late_part, late_got) = _local_step(
        x[0], loss_target[0], small, wfirst, exchanges, later_weights)
    loss = lax.psum(loss[0, 0], ("x", "y", "c"))
    reduced = _sum_partials(tuple(late_got) + tuple(early_got), tuple(late_part) + tuple(early_part),
                            jnp.stack([chip, core]))
    g_big = dict(zip(LATE + EARLY, _share_cores(reduced)))
    g_small = _unpack_small(_allreduce_small(_pack_small(grads_s)))

    grad, delta, new_m, new_v = {}, {}, {}, {}
    for name, shape in BIG:
        g2 = g_big[name]
        d_, m_, v_ = _adamw(given[name].reshape(shape), g2, given["m_" + name].reshape(shape),
                            given["v_" + name].reshape(shape), f"adamw_{name}")
        full = given[name].shape
        grad[name], delta[name], new_m[name], new_v[name] = (a.reshape(full) for a in (g2, d_, m_, v_))
    ps = {k: _pack_small({n: given[pre + n] for n in small_names}) for k, pre in (("w", ""), ("m", "m_"), ("v", "v_"))}
    gs_packed = _pack_small(g_small)
    d_s, m_s, v_s = (_unpack_small(a) for a in _adamw(ps["w"], gs_packed, ps["m"], ps["v"], "adamw_small"))
    for name in small_names:
        grad[name], delta[name], new_m[name], new_v[name] = g_small[name], d_s[name], m_s[name], v_s[name]

    return (loss, dx[None], *[grad[n] for n in WEIGHTS], *[delta[n] for n in WEIGHTS],
            *[new_m[n] for n in WEIGHTS], *[new_v[n] for n in WEIGHTS])
```

```python
import functools

import numpy as np
import jax
import jax.numpy as jnp
from jax import lax
from jax.experimental import pallas as pl
from jax.experimental.pallas import tpu as pltpu

F32 = jnp.float32
BF16 = jnp.bfloat16

D_MODEL = 1024
D_FF = 2816
N_CHIPS = 4
DIL_HEADS = 8
DIL_HD = 64
DIL_WIDTH = 512
DIL_DILATIONS = (1, 4, 16)
DIL_W = 128
QB = 128
MLA_HEADS = 4
MLA_NOPE = 128
MLA_ROPE = 64
MLA_QK = 192
MLA_V = 128
MLA_Q_RANK = 256
MLA_KV_RANK = 128
ROPE_BASE = 10000.0
REL_BUCKETS = 32
REL_MAX_DIST = 2048
FFN_RESID = 0.5
EPS = 1e-6
NEG = -1e30
LANES = 128

ADAM_LR = 0.001
ADAM_B1 = 0.9
ADAM_B2 = 0.999
ADAM_EPS = 1e-08
ADAM_WD = 0.01
ADAM_STEP = 10

NT = (((1,), (1,)), ((), ()))
NN = (((1,), (0,)), ((), ()))
TN = (((0,), (0,)), ((), ()))

BIG = (
    ("ffn1_w_gate", (D_MODEL, D_FF // N_CHIPS)),
    ("ffn1_w_up", (D_MODEL, D_FF // N_CHIPS)),
    ("ffn1_w_down", (D_FF // N_CHIPS, D_MODEL)),
    ("w_in", (D_MODEL, 1984 // N_CHIPS)),
    ("mla_w_q_b", (MLA_Q_RANK, MLA_QK)),
    ("mla_w_kv_b", (MLA_KV_RANK, MLA_NOPE + MLA_V)),
    ("w_out", (D_MODEL // N_CHIPS, D_MODEL)),
    ("ffn2_w_gate", (D_MODEL, D_FF // N_CHIPS)),
    ("ffn2_w_up", (D_MODEL, D_FF // N_CHIPS)),
    ("ffn2_w_down", (D_FF // N_CHIPS, D_MODEL)),
)
SMALL = (
    ("ffn1_norm", (1, 1024), 8), ("mix_norm", (1, 1024), 8), ("dil_q_norm", (1, 64), 1),
    ("dil_k_norm", (1, 64), 1), ("rel_bias", (8, 32), 2), ("mla_q_a_norm", (1, 256), 2),
    ("mla_kv_a_norm", (1, 128), 1), ("mla_q_norm", (1, 192), 2), ("mla_k_norm", (1, 192), 2),
    ("out_norm_dil", (1, 512), 4), ("out_norm_mla", (1, 512), 4), ("ffn2_norm", (1, 1024), 8),
)
SMALL_ROWS = 48
WEIGHTS = ("ffn1_norm", "ffn1_w_gate", "ffn1_w_up", "ffn1_w_down", "mix_norm", "w_in", "dil_q_norm",
           "dil_k_norm", "rel_bias", "mla_q_a_norm", "mla_w_q_b", "mla_kv_a_norm", "mla_w_kv_b",
           "mla_q_norm", "mla_k_norm", "out_norm_dil", "out_norm_mla", "w_out", "ffn2_norm",
           "ffn2_w_gate", "ffn2_w_up", "ffn2_w_down")


def _pcall(body, **kw):
    return pl.pallas_call(body, **kw)


def _cparams(*sem):
    return pltpu.CompilerParams(dimension_semantics=sem)


def _sds(shape, dtype):
    return jax.ShapeDtypeStruct(shape, dtype)


def _dot(a, b, dn):
    return lax.dot_general(a, b, dn, preferred_element_type=F32)


def _rms_fwd(x, g, out_dtype, name, tm):
    n, d = x.shape
    tm = min(tm, n)

    def body(x_ref, g_ref, o_ref):
        xf = x_ref[...].astype(F32)
        r = lax.rsqrt(jnp.mean(xf * xf, axis=-1, keepdims=True) + EPS)
        o_ref[...] = (xf * r * g_ref[...]).astype(o_ref.dtype)

    return _pcall(
        body, name=name, grid=(n // tm,),
        in_specs=[pl.BlockSpec((tm, d), lambda i: (i, 0)), pl.BlockSpec((1, d), lambda i: (0, 0))],
        out_specs=pl.BlockSpec((tm, d), lambda i: (i, 0)),
        out_shape=_sds((n, d), out_dtype), compiler_params=_cparams("parallel"))(x, g)


def _rms_bwd(dys, x, g, res, name, tm):
    n, d = x.shape
    tm = min(tm, n)
    nd = len(dys)
    has_res = res is not None

    def body(*refs):
        dy_refs = refs[:nd]
        x_ref, g_ref = refs[nd], refs[nd + 1]
        res_ref = refs[nd + 2] if has_res else None
        dx_ref, dg_ref = refs[-2], refs[-1]
        dy = dy_refs[0][...].astype(F32)
        for r_ in dy_refs[1:]:
            dy = dy + r_[...].astype(F32)
        xf = x_ref[...].astype(F32)
        r = lax.rsqrt(jnp.mean(xf * xf, axis=-1, keepdims=True) + EPS)
        xh = xf * r
        dxh = dy * g_ref[...]
        dx = r * (dxh - xh * jnp.mean(dxh * xh, axis=-1, keepdims=True))
        if has_res:
            dx = dx + res_ref[...]
        dx_ref[...] = dx

        @pl.when(pl.program_id(0) == 0)
        def _():
            dg_ref[...] = jnp.zeros_like(dg_ref)

        dg_ref[...] += jnp.sum(dy * xh, axis=0, keepdims=True)

    row = pl.BlockSpec((tm, d), lambda i: (i, 0))
    vec = pl.BlockSpec((1, d), lambda i: (0, 0))
    ins = list(dys) + [x, g] + ([res] if has_res else [])
    return _pcall(
        body, name=name, grid=(n // tm,),
        in_specs=[row] * nd + [row, vec] + ([row] if has_res else []),
        out_specs=(row, vec),
        out_shape=(_sds((n, d), F32), _sds((1, d), F32)),
        compiler_params=_cparams("arbitrary"))(*ins)


def _mm(name, grid, pairs, dn, out_shape, out_spec, acc_shape, res=None, scale=1.0, outgoing=(), norm=None):
    npairs = len(pairs)
    nred = grid[2]
    has_res = res is not None
    has_norm = norm is not None
    no = len(outgoing)

    def body(*refs):
        ab = refs[:2 * npairs]
        res_ref = refs[2 * npairs] if has_res else None
        nin = 2 * npairs + int(has_res) + 2 * int(has_norm)
        if has_norm:
            x_ref, g_ref = refs[nin - 2:nin]
        first_out = nin + no
        sent = refs[nin:first_out]
        o_ref = refs[first_out]
        nout = 1 + int(has_norm)
        dg_ref = refs[first_out + 1] if has_norm else None
        arrived = refs[first_out + nout:first_out + nout + no]
        acc_ref = refs[first_out + nout + no] if nred > 1 else None
        if no:
            send_sems, recv_sems = refs[-2:]
            ids = [pl.program_id(n) for n in range(3)]

            @pl.when((ids[0] == 0) & (ids[1] == 0) & (ids[2] == 0))
            def _():
                _scatter_start(sent, arrived, send_sems, recv_sems)

        tot = None
        for p in range(npairs):
            d = _dot(ab[2 * p][...].astype(BF16), ab[2 * p + 1][...].astype(BF16), dn)
            tot = d if tot is None else tot + d

        def finish(v):
            if scale != 1.0:
                v = v * scale
            if has_norm:
                xf = x_ref[...]
                r = lax.rsqrt(jnp.mean(xf * xf, axis=-1, keepdims=True) + EPS)
                xh = xf * r
                dxh = v * g_ref[...]

                @pl.when(pl.program_id(0) == 0)
                def _():
                    dg_ref[...] = jnp.zeros_like(dg_ref)

                dg_ref[...] += jnp.sum(v * xh, axis=0, keepdims=True)
                v = r * (dxh - xh * jnp.mean(dxh * xh, axis=-1, keepdims=True))
            if has_res:
                v = res_ref[...] + v
            o_ref[...] = v.astype(o_ref.dtype)

        if nred == 1:
            finish(tot)
        else:
            r = pl.program_id(2)

            @pl.when(r == 0)
            def _():
                acc_ref[...] = tot

            @pl.when(r > 0)
            def _():
                acc_ref[...] += tot

            @pl.when(r == nred - 1)
            def _():
                finish(acc_ref[...])

        if no:
            @pl.when((ids[0] == grid[0] - 1) & (ids[1] == grid[1] - 1) & (ids[2] == nred - 1))
            def _():
                _scatter_wait(sent, arrived, send_sems, recv_sems)

    ins, specs = [], []
    for a, a_spec, b, b_spec in pairs:
        ins += [a, b]
        specs += [a_spec, b_spec]
    if has_res:
        ins.append(res[0])
        specs.append(res[1])
    scratch = [pltpu.VMEM(acc_shape, F32)] if nred > 1 else []
    if not no and not has_norm:
        return _pcall(
            body, name=name, grid=grid, in_specs=specs, out_specs=out_spec, out_shape=out_shape,
            scratch_shapes=scratch, compiler_params=_cparams("parallel", "parallel", "arbitrary"))(*ins)
    out_specs, out_shapes = (out_spec,), (out_shape,)
    if has_norm:
        assert grid[1] == 1
        d = norm[1].shape[1]
        ins += [norm[0], norm[1]]
        specs += [out_spec, pl.BlockSpec((1, d), lambda i, j, r: (0, 0))]
        out_specs += (pl.BlockSpec((1, d), lambda i, j, r: (0, 0)),)
        out_shapes += (_sds((1, d), F32),)
    hbm = pl.BlockSpec(memory_space=pltpu.HBM)
    res_ = tuple(_pcall(
        body, name=name, grid=grid, in_specs=specs + [hbm] * no, out_specs=out_specs + (hbm,) * no,
        out_shape=out_shapes + _scatter_shapes(outgoing),
        scratch_shapes=scratch + (_scatter_sems(no) if no else []),
        compiler_params=_cparams("arbitrary", "arbitrary", "arbitrary"))(*ins, *outgoing))
    nout = len(out_shapes)
    return res_[:nout] + ((res_[nout:],) if no else ())


def _ffn_up(h, wg, wu, name, tm, incoming=()):
    t, d = h.shape
    nc, _, fs = wg.shape
    tm = min(tm, t)
    nt = t // tm
    ni = len(incoming)
    halves = _halves(incoming)

    def body(*refs):
        h_ref, wg_ref, wu_ref = refs[:3]
        srcs = refs[3:3 + ni]
        g_ref, u_ref, a_ref = refs[3 + ni:6 + ni]
        outs = refs[6 + ni:6 + 2 * ni]
        if ni:
            send_sems, recv_sems = refs[6 + 2 * ni:]
            c, i = pl.program_id(0), pl.program_id(1)

            @pl.when((c == 0) & (i == 0))
            def _():
                _gather_start(srcs, outs, halves, send_sems, recv_sems)

        hh = h_ref[...]
        gate = _dot(hh, wg_ref[...], NN)
        up = _dot(hh, wu_ref[...], NN)
        sig = jax.nn.sigmoid(gate)
        silu = gate * sig
        g_ref[...] = (up * (sig + silu * (1.0 - sig))).astype(BF16)
        u_ref[...] = silu.astype(BF16)
        a_ref[...] = (silu * up).astype(BF16)

        if ni:
            @pl.when((c == nc - 1) & (i == nt - 1))
            def _():
                _gather_wait(outs, halves, send_sems, recv_sems)

    wspec = pl.BlockSpec((None, d, fs), lambda c, i: (c, 0, 0))
    ospec = pl.BlockSpec((None, tm, fs), lambda c, i: (c, i, 0))
    hbm = pl.BlockSpec(memory_space=pltpu.HBM)
    osd = _sds((nc, t, fs), BF16)
    res = tuple(_pcall(
        body, name=name, grid=(nc, nt),
        in_specs=[pl.BlockSpec((tm, d), lambda c, i: (i, 0)), wspec, wspec] + [hbm] * ni,
        out_specs=(ospec, ospec, ospec) + (hbm,) * ni,
        out_shape=(osd, osd, osd) + tuple(_sds((N_CHIPS,) + b.shape, b.dtype) for b in incoming),
        scratch_shapes=[pltpu.SemaphoreType.DMA((3 * ni,)), pltpu.SemaphoreType.DMA((3 * ni,))] if ni else [],
        compiler_params=_cparams("arbitrary", "arbitrary"))(h, wg, wu, *incoming))
    return res[:3] + (res[3:],)


def _ffn_hidden_bwd(dy, h, wd, dact_dgate, dact_dup, act, name, tm, outgoing=()):
    t, d = dy.shape
    nc, fs, _ = wd.shape
    tm = min(tm, t)
    nt = t // tm
    no = len(outgoing)

    def body(*refs):
        dy_ref, h_ref, wd_ref, g_ref, u_ref, a_ref = refs[:6]
        sent = refs[6:6 + no]
        dg_ref, du_ref, dwg_hbm, dwu_hbm, dwd_hbm = refs[6 + no:11 + no]
        arrived = refs[11 + no:11 + 2 * no]
        wg_acc, wu_acc, wd_acc, sem = refs[11 + 2 * no:15 + 2 * no]
        c, i = pl.program_id(0), pl.program_id(1)
        if no:
            send_sems, recv_sems = refs[15 + 2 * no:]

            @pl.when((c == 0) & (i == 0))
            def _():
                _scatter_start(sent, arrived, send_sems, recv_sems)

        dyb = dy_ref[...].astype(BF16)
        da = _dot(dyb, wd_ref[...], NT) * FFN_RESID
        dgate = (da * g_ref[...].astype(F32)).astype(BF16)
        dup = (da * u_ref[...].astype(F32)).astype(BF16)
        dg_ref[...] = dgate
        du_ref[...] = dup
        hh = h_ref[...]
        parts = (_dot(hh, dgate, TN), _dot(hh, dup, TN), _dot(a_ref[...], dyb, TN) * FFN_RESID)
        accs = (wg_acc, wu_acc, wd_acc)

        @pl.when(i == 0)
        def _():
            for acc, part in zip(accs, parts):
                acc[...] = part

        @pl.when(i > 0)
        def _():
            for acc, part in zip(accs, parts):
                acc[...] += part

        @pl.when(i == nt - 1)
        def _():
            copies = [pltpu.make_async_copy(acc, out.at[c], sem.at[n])
                      for n, (acc, out) in enumerate(zip(accs, (dwg_hbm, dwu_hbm, dwd_hbm)))]
            for cp in copies:
                cp.start()
            for cp in copies:
                cp.wait()

        if no:
            @pl.when((c == nc - 1) & (i == nt - 1))
            def _():
                _scatter_wait(sent, arrived, send_sems, recv_sems)

    tok = pl.BlockSpec((tm, d), lambda c, i: (i, 0))
    cspec = pl.BlockSpec((None, tm, fs), lambda c, i: (c, i, 0))
    hbm = pl.BlockSpec(memory_space=pltpu.HBM)
    osd = _sds((nc, t, fs), BF16)
    res = _pcall(
        body, name=name, grid=(nc, nt),
        in_specs=[tok, tok, pl.BlockSpec((None, fs, d), lambda c, i: (c, 0, 0)), cspec, cspec, cspec] + [hbm] * no,
        out_specs=(cspec, cspec, hbm, hbm, hbm) + (hbm,) * no,
        out_shape=(osd, osd, _sds((nc, d, fs), F32), _sds((nc, d, fs), F32), _sds((nc, fs, d), F32))
        + _scatter_shapes(outgoing),
        scratch_shapes=[pltpu.VMEM((d, fs), F32), pltpu.VMEM((d, fs), F32), pltpu.VMEM((fs, d), F32),
                        pltpu.SemaphoreType.DMA((3,))] + (_scatter_sems(no) if no else []),
        compiler_params=_cparams("arbitrary", "arbitrary"))(dy, h, wd, dact_dgate, dact_dup, act, *outgoing)
    res = tuple(res)
    return res[:5] + (res[5:],)


def _ffn_fwd(x, g, wg, wu, wd, tag, incoming=()):
    t = x.shape[0]
    nc, _, fs = wg.shape
    tm = min(512, t)
    h = _rms_fwd(x, g, BF16, f"{tag}_norm", 512)
    dact_dgate, dact_dup, act, partly = _ffn_up(h, wg, wu, f"{tag}_up", 1024, incoming)
    pairs = [(act, pl.BlockSpec((None, tm, fs), lambda i, j, r, c=c: (c, i, 0)),
              wd, pl.BlockSpec((None, fs, D_MODEL), lambda i, j, r, c=c: (c, 0, 0))) for c in range(nc)]
    row = pl.BlockSpec((tm, D_MODEL), lambda i, j, r: (i, 0))
    y = _mm(f"{tag}_down", (t // tm, 1, 1), pairs, NN, _sds((t, D_MODEL), F32), row, (tm, D_MODEL),
            res=(x, row), scale=FFN_RESID)
    return y, (h, dact_dgate, dact_dup, act), partly


def _ffn_bwd(dy, x, g, wg, wu, wd, saved, tag, outgoing=(), own_exchange=None):
    h, dact_dgate, dact_dup, act = saved
    t = x.shape[0]
    nc, _, fs = wg.shape
    tm = min(512, t)
    dgate, dup, dwg, dwu, dwd, arrived = _ffn_hidden_bwd(dy, h, wd, dact_dgate, dact_dup, act,
                                                         f"{tag}_hidden_bwd", 1024, outgoing)
    pairs = []
    for c in range(nc):
        a_spec = pl.BlockSpec((None, tm, fs), lambda i, j, r, c=c: (c, i, 0))
        w_spec = pl.BlockSpec((None, D_MODEL, fs), lambda i, j, r, c=c: (c, 0, 0))
        pairs += [(dgate, a_spec, wg, w_spec), (dup, a_spec, wu, w_spec)]
    own_part = tuple(own_exchange([dwg, dwu, dwd])) if own_exchange else ()
    row = pl.BlockSpec((tm, D_MODEL), lambda i, j, r: (i, 0))
    res = _mm(f"{tag}_dh", (t // tm, 1, 1), pairs, NT, _sds((t, D_MODEL), F32), row, (tm, D_MODEL),
              res=(dy, row), norm=(x, g), outgoing=own_part)
    dx, dg = res[0], res[1]
    own_got = res[2] if own_part else ()
    return dx, dg, dwg, dwu, dwd, arrived, (own_part, own_got)


def _mm_simple(name, a, b, dn, out_dtype, tm=512, tk=512, res=None, scale=1.0):
    if dn == TN:
        k, m = a.shape
        n = b.shape[1]
        tk = min(tk, k)
        return _mm(name, (1, 1, k // tk),
                   [(a, pl.BlockSpec((tk, m), lambda i, j, r: (r, 0)), b, pl.BlockSpec((tk, n), lambda i, j, r: (r, 0)))],
                   TN, _sds((m, n), out_dtype), pl.BlockSpec((m, n), lambda i, j, r: (0, 0)), (m, n), scale=scale)
    m, k = a.shape
    n = b.shape[1] if dn == NN else b.shape[0]
    tm = min(tm, m)
    row = pl.BlockSpec((tm, n), lambda i, j, r: (i, 0))
    return _mm(name, (m // tm, 1, 1),
               [(a, pl.BlockSpec((tm, k), lambda i, j, r: (i, 0)), b, pl.BlockSpec(b.shape, lambda i, j, r: (0, 0)))],
               dn, _sds((m, n), out_dtype), row, (tm, n), res=None if res is None else (res, row), scale=scale)


def _t5_bucket(dist):
    max_exact = REL_BUCKETS // 2
    d = np.maximum(dist, 1).astype(np.float32)
    large = max_exact + (np.log(d / max_exact) / np.log(REL_MAX_DIST / max_exact)
                         * (REL_BUCKETS - max_exact)).astype(np.int32)
    large = np.minimum(large, REL_BUCKETS - 1)
    return np.where(dist < max_exact, dist, large).astype(np.int32)


def _bucket_tiles():
    i = np.arange(QB)[:, None]
    j = np.arange(QB + DIL_W)[None, :]
    delta = np.clip(i + DIL_W - j, 0, None)
    return np.stack([_t5_bucket(delta * dil) for dil in DIL_DILATIONS]).astype(np.int32)


def _bias_tiles(rel_bias):
    buckets = jnp.asarray(_bucket_tiles())

    def body(rb_ref, bk_ref, o_ref):
        bk = bk_ref[...]
        for h in range(DIL_HEADS):
            def pick(b, tile):
                return jnp.where(bk == b, rb_ref[h, b], tile)

            o_ref[h] = lax.fori_loop(0, REL_BUCKETS, pick, jnp.zeros((QB, QB + DIL_W), F32))

    return _pcall(
        body, name="dil_bias_tiles", grid=(3,),
        in_specs=[pl.BlockSpec(memory_space=pltpu.SMEM),
                  pl.BlockSpec((None, QB, QB + DIL_W), lambda b: (b, 0, 0))],
        out_specs=pl.BlockSpec((None, DIL_HEADS, QB, QB + DIL_W), lambda b: (b, 0, 0, 0)),
        out_shape=_sds((3, DIL_HEADS, QB, QB + DIL_W), F32),
        compiler_params=_cparams("parallel"))(rel_bias, buckets)


def _bias_grad(dtiles):
    buckets = jnp.asarray(_bucket_tiles())

    def body(dt_ref, bk_ref, o_ref):
        def one(b, carry):
            hit = [bk_ref[br] == b for br in range(3)]
            for h in range(DIL_HEADS):
                tot = jnp.zeros((), F32)
                for br in range(3):
                    tot = tot + jnp.sum(jnp.where(hit[br], dt_ref[br, h], 0.0))
                o_ref[h, b] = tot
            return carry

        lax.fori_loop(0, REL_BUCKETS, one, 0)

    return _pcall(
        body, name="dil_bias_grad",
        in_specs=[pl.BlockSpec(memory_space=pltpu.VMEM), pl.BlockSpec(memory_space=pltpu.VMEM)],
        out_specs=pl.BlockSpec(memory_space=pltpu.SMEM),
        out_shape=_sds((DIL_HEADS, REL_BUCKETS), F32))(dtiles, buckets)


def _split_heads(a, lo):
    zero = jnp.zeros_like(a)
    return jnp.concatenate([jnp.where(lo, a, zero), jnp.where(lo, zero, a)], axis=0)


def _side_by_side(a):
    n = a.shape[0] // 2
    return jnp.concatenate([a[:n], a[n:]], axis=1)


def _band_masks(prev_ok):
    ii = lax.broadcasted_iota(jnp.int32, (2 * QB, QB), 0) & (QB - 1)
    jj = lax.broadcasted_iota(jnp.int32, (2 * QB, QB), 1)
    return jj <= ii, jj >= ii + jnp.where(prev_ok, 0, QB)


def _dil_fwd(q, k, v, bias, dil, name):
    w = DIL_WIDTH
    t = q.shape[0] * dil
    npair = w // LANES
    nl = t // dil // QB
    scale = DIL_HD ** -0.5

    def body(q_ref, kc_ref, kp_ref, vc_ref, vp_ref, b_ref, o_ref, lse_ref):
        nn = pl.program_id(1)
        lo = lax.broadcasted_iota(jnp.int32, (QB, LANES), 1) < DIL_HD
        lo2 = lax.broadcasted_iota(jnp.int32, (2 * QB, LANES), 1) < DIL_HD
        ii = lax.broadcasted_iota(jnp.int32, (2 * QB, 2 * QB), 0) & (QB - 1)
        jj = lax.broadcasted_iota(jnp.int32, (2 * QB, 2 * QB), 1)
        first_key = jnp.maximum(ii, jnp.where(nn != 0, 0, QB))
        valid = (jj >= first_key) & (jj <= ii + QB)
        for p in range(npair):
            cols = slice(p * LANES, (p + 1) * LANES)
            qq = _split_heads(q_ref[:, cols], lo)
            kk = jnp.concatenate([kp_ref[:, cols], kc_ref[:, cols]], axis=0)
            vv = jnp.concatenate([vp_ref[:, cols], vc_ref[:, cols]], axis=0)
            s = jnp.where(valid, _dot(qq, kk, NT) * scale + b_ref[p], NEG)
            m = jnp.max(s, axis=-1, keepdims=True)
            e = jnp.exp(s - m)
            den = jnp.sum(e, axis=-1, keepdims=True)
            pn = (e * (1.0 / den)).astype(BF16)
            o_ref[:, cols] = _dot(_side_by_side(pn), _split_heads(vv, lo2), NN)
            lse = m + jnp.log(den)
            lse_ref[:, cols] = jnp.where(lo, lse[:QB], lse[QB:])

    cur = pl.BlockSpec((QB, w), lambda r, n: (n, r))
    prev = pl.BlockSpec((QB, w), lambda r, n: (jnp.maximum(n - 1, 0), r))
    sd = _sds((t // dil, dil * w), F32)
    return _pcall(
        body, name=name, grid=(dil, nl),
        in_specs=[cur, cur, prev, cur, prev, pl.BlockSpec((npair, 2 * QB, 2 * QB), lambda r, n: (0, 0, 0))],
        out_specs=(cur, cur), out_shape=(sd, sd),
        compiler_params=_cparams("parallel", "parallel"))(q, k, k, v, v, bias)


def _dil_bwd(q, k, v, do, stats, bias, dil, name):
    w = DIL_WIDTH
    t = q.shape[0] * dil
    npair = w // LANES
    nl = t // dil // QB
    scale = DIL_HD ** -0.5

    def body(qc_ref, qn_ref, doc_ref, don_ref, sc_ref, sn_ref, k_ref, v_ref, b_ref,
             dq_ref, dk_ref, dv_ref, db_ref, carry):
        r, nn = pl.program_id(0), pl.program_id(1)
        lo = lax.broadcasted_iota(jnp.int32, (QB, LANES), 1) < DIL_HD
        cur_ok, prev_ok = _band_masks(nn + 1 < nl)

        @pl.when((r == 0) & (nn == 0))
        def _():
            db_ref[...] = jnp.zeros_like(db_ref)
            carry[...] = jnp.zeros_like(carry)

        for p in range(npair):
            cols = slice(p * LANES, (p + 1) * LANES)
            kp, vp = k_ref[:, cols], v_ref[:, cols]
            k2 = _split_heads(kp, lo)

            def column(ref, lane):
                first = p * LANES + lane
                return jnp.concatenate([ref[:, first:first + 1], ref[:, first + DIL_HD:first + DIL_HD + 1]], axis=0)

            def side(q_ref, do_ref, s_ref, bias, ok):
                qq = _split_heads(q_ref[:, cols], lo)
                dd = _split_heads(do_ref[:, cols], lo)
                s = jnp.where(ok, _dot(qq, kp, NT) * scale + bias, NEG)
                prob = jnp.exp(s - column(s_ref, 0))
                ds = prob * (_dot(dd, vp, NT) - column(s_ref, DIL_HD // 2))
                return qq, dd, prob.astype(BF16), ds

            q1, d1, p1, ds1 = side(qc_ref, doc_ref, sc_ref, b_ref[p, :, QB:], cur_ok)
            q2, d2, p2, ds2 = side(qn_ref, don_ref, sn_ref, b_ref[p, :, :QB], prev_ok)
            ds1b, ds2b = ds1.astype(BF16), ds2.astype(BF16)
            dq_ref[:, cols] = carry[:, cols] + _dot(_side_by_side(ds1b), k2, NN) * scale
            carry[:, cols] = _dot(_side_by_side(ds2b), k2, NN) * scale
            dk_ref[:, cols] = _dot(jnp.concatenate([ds1b, ds2b], axis=0), jnp.concatenate([q1, q2], axis=0), TN) * scale
            dv_ref[:, cols] = _dot(jnp.concatenate([p1, p2], axis=0), jnp.concatenate([d1, d2], axis=0), TN)
            db_ref[p, :, QB:] += ds1
            db_ref[p, :, :QB] += ds2

    cur = pl.BlockSpec((QB, w), lambda r, n: (n, r))
    nxt = pl.BlockSpec((QB, w), lambda r, n: (jnp.minimum(n + 1, nl - 1), r))
    tile = pl.BlockSpec((npair, 2 * QB, 2 * QB), lambda r, n: (0, 0, 0))
    sd = _sds((t // dil, dil * w), F32)
    return _pcall(
        body, name=name, grid=(dil, nl),
        in_specs=[cur, nxt, cur, nxt, cur, nxt, cur, cur, tile],
        out_specs=(cur, cur, cur, tile),
        out_shape=(sd, sd, sd, _sds((npair, 2 * QB, 2 * QB), F32)),
        scratch_shapes=[pltpu.VMEM((QB, w), F32)],
        compiler_params=_cparams("arbitrary", "arbitrary"))(q, q, do, do, stats, stats, k, v, bias)


def _head_sum_matrix(scale):
    idx = np.arange(DIL_WIDTH) // DIL_HD
    return jnp.asarray((idx[:, None] == idx[None, :]).astype(np.float32) * scale, BF16)


def _head_sum(x, mat):
    hi = x.astype(BF16)
    lo = (x - hi.astype(F32)).astype(BF16)
    return _dot(hi, mat, NN) + _dot(lo, mat, NN)


def _to_views(src, tmp, out_refs):
    tm, w = src.shape
    for j in range(w // LANES):
        tmp[j] = src[:, j * LANES:(j + 1) * LANES]
    for d, o_ref in zip(DIL_DILATIONS, out_refs):
        if d == 1:
            o_ref[...] = src.astype(o_ref.dtype)
            continue
        for r in range(d):
            for j in range(w // LANES):
                lo = r * w + j * LANES
                o_ref[:, lo:lo + LANES] = tmp[j, pl.ds(r, tm // d, stride=d), :].astype(o_ref.dtype)


def _from_view(v_ref, tmp, d):
    tm = tmp.shape[1]
    w = v_ref.shape[1] // d
    for r in range(d):
        for j in range(w // LANES):
            lo = r * w + j * LANES
            tmp[j, pl.ds(r, tm // d, stride=d), :] = v_ref[:, lo:lo + LANES]
    return jnp.concatenate([tmp[j] for j in range(w // LANES)], axis=1)


def _view_specs(tm, t, dtype):
    specs = tuple(pl.BlockSpec((tm // d, d * DIL_WIDTH), lambda i: (i, 0)) for d in DIL_DILATIONS)
    shapes = tuple(_sds((t // d, d * DIL_WIDTH), dtype) for d in DIL_DILATIONS)
    return specs, shapes


def _view_scratch(tm):
    return pltpu.VMEM((DIL_WIDTH // LANES, tm, LANES), F32)


def _dil_merge(outs, lses, g, tm):
    w = DIL_WIDTH
    t = outs[0].shape[0]
    tm = min(tm, t)

    def body(o0, o1, o2, l0, l1, l2, g_ref, o_ref, l_ref, n_ref, so1, so2, sl1, sl2):
        d1, d2 = DIL_DILATIONS[1], DIL_DILATIONS[2]
        a0, a1, a2 = l0[...], _from_view(l1, sl1, d1), _from_view(l2, sl2, d2)
        m = jnp.maximum(jnp.maximum(a0, a1), a2)
        e0, e1, e2 = jnp.exp(a0 - m), jnp.exp(a1 - m), jnp.exp(a2 - m)
        den = e0 + e1 + e2
        o = (e0 * o0[...] + e1 * _from_view(o1, so1, d1) + e2 * _from_view(o2, so2, d2)) / den
        o_ref[...] = o
        l_ref[...] = m + jnp.log(den)
        r = lax.rsqrt(jnp.mean(o * o, axis=-1, keepdims=True) + EPS)
        n_ref[...] = (o * r * g_ref[...]).astype(n_ref.dtype)

    specs, _ = _view_specs(tm, t, F32)
    spec = pl.BlockSpec((tm, w), lambda i: (i, 0))
    return _pcall(
        body, name="dil_merge", grid=(t // tm,),
        in_specs=list(specs) * 2 + [pl.BlockSpec((1, w), lambda i: (0, 0))], out_specs=(spec, spec, spec),
        out_shape=(_sds((t, w), F32), _sds((t, w), F32), _sds((t, w), BF16)),
        scratch_shapes=[_view_scratch(tm)] * 4,
        compiler_params=_cparams("parallel"))(*outs, *lses, g)


def _dil_stats(do, o, lse, tm):
    t, w = do.shape
    tm = min(tm, t)

    def body(a_ref, b_ref, l_ref, m_ref, s1, s4, s16, d1, d4, d16, tmp):
        first = (lax.broadcasted_iota(jnp.int32, (tm, w), 1) & (DIL_HD - 1)) < DIL_HD // 2
        do_ = a_ref[...]
        _to_views(jnp.where(first, l_ref[...], _head_sum(do_ * b_ref[...], m_ref[...])), tmp, (s1, s4, s16))
        _to_views(do_, tmp, (d1, d4, d16))

    spec = pl.BlockSpec((tm, w), lambda i: (i, 0))
    f_specs, f_shapes = _view_specs(tm, t, F32)
    b_specs, b_shapes = _view_specs(tm, t, BF16)
    res = _pcall(body, name="dil_stats", grid=(t // tm,),
                 in_specs=[spec, spec, spec, pl.BlockSpec((w, w), lambda i: (0, 0))],
                 out_specs=f_specs + b_specs, out_shape=f_shapes + b_shapes,
                 scratch_shapes=[_view_scratch(tm)],
                 compiler_params=_cparams("parallel"))(do, o, lse, _head_sum_matrix(1.0))
    return res[:3], res[3:]


def _head_norm_fwd(x, col, g, name, tm):
    t = x.shape[0]
    w = DIL_WIDTH
    tm = min(tm, t)
    normed = g is not None

    def body(*refs):
        outs, tmp = refs[-4:-1], refs[-1]
        xf = refs[0][...]
        if normed:
            g_ref, m_ref = refs[1], refs[2]
            xf = xf * lax.rsqrt(_head_sum(xf * xf, m_ref[...]) + EPS) * g_ref[...]
        _to_views(xf, tmp, outs)

    specs, shapes = _view_specs(tm, t, BF16)
    extra = [g, _head_sum_matrix(1.0 / DIL_HD)] if normed else []
    extra_specs = [pl.BlockSpec((1, w), lambda i: (0, 0)), pl.BlockSpec((w, w), lambda i: (0, 0))] if normed else []
    return _pcall(
        body, name=name, grid=(t // tm,),
        in_specs=[pl.BlockSpec((tm, w), lambda i: (i, col))] + extra_specs,
        out_specs=specs, out_shape=shapes, scratch_shapes=[_view_scratch(tm)],
        compiler_params=_cparams("parallel"))(x, *extra)


def _head_norm_bwd(dys, x, col, g, name, tm):
    t = x.shape[0]
    w = DIL_WIDTH
    tm = min(tm, t)
    nd = len(dys)
    nt = t // tm
    lane = np.arange(w) % DIL_HD
    fold = jnp.asarray((lane[:, None] == lane[None, :]).astype(np.float32))

    def body(*refs):
        x_ref, g_ref, m_ref, f_ref = refs[nd:nd + 4]
        dx_ref, dg_ref, s1, s2 = refs[-4:]
        dy = refs[0][...] + _from_view(refs[1], s1, DIL_DILATIONS[1]) + _from_view(refs[2], s2, DIL_DILATIONS[2])
        xf = x_ref[...]
        mat = m_ref[...]
        r = lax.rsqrt(_head_sum(xf * xf, mat) + EPS)
        xh = xf * r
        dxh = dy * g_ref[...]
        dx_ref[...] = r * (dxh - xh * _head_sum(dxh * xh, mat))

        @pl.when(pl.program_id(0) == 0)
        def _():
            dg_ref[...] = jnp.zeros_like(dg_ref)

        dg_ref[...] += jnp.sum(dy * xh, axis=0, keepdims=True)

        @pl.when(pl.program_id(0) == nt - 1)
        def _():
            per_lane = jnp.broadcast_to(dg_ref[...], (8, w))
            dg_ref[...] = lax.dot_general(per_lane, f_ref[...], NN, precision=lax.Precision.HIGHEST,
                                          preferred_element_type=F32)[0:1]

    row = pl.BlockSpec((tm, w), lambda i: (i, 0))
    vec = pl.BlockSpec((1, w), lambda i: (0, 0))
    sq = pl.BlockSpec((w, w), lambda i: (0, 0))
    views, _ = _view_specs(tm, t, F32)
    return _pcall(
        body, name=name, grid=(nt,),
        in_specs=list(views) + [pl.BlockSpec((tm, w), lambda i: (i, col)), vec, sq, sq],
        out_specs=(row, vec), out_shape=(_sds((t, w), F32), _sds((1, w), F32)),
        scratch_shapes=[_view_scratch(tm)] * 2,
        compiler_params=_cparams("arbitrary"))(*dys, x, g, _head_sum_matrix(1.0 / DIL_HD), fold)


def _rowdot(a, b, name, tm):
    n, d = a.shape
    tm = min(tm, n)

    def body(a_ref, b_ref, o_ref):
        o_ref[...] = jnp.sum(a_ref[...].astype(F32) * b_ref[...].astype(F32), axis=-1, keepdims=True)

    spec = pl.BlockSpec((tm, d), lambda i: (i, 0))
    return _pcall(body, name=name, grid=(n // tm,), in_specs=[spec, spec],
                  out_specs=pl.BlockSpec((tm, 1), lambda i: (i, 0)), out_shape=_sds((n, 1), F32),
                  compiler_params=_cparams("parallel"))(a, b)


def _sum_branches(parts, name, tm):
    t = parts[0].shape[0]
    w = DIL_WIDTH
    tm = min(tm, t)

    def body(a_ref, b_ref, c_ref, o_ref, s1, s2):
        o_ref[...] = a_ref[...] + _from_view(b_ref, s1, DIL_DILATIONS[1]) + _from_view(c_ref, s2, DIL_DILATIONS[2])

    views, _ = _view_specs(tm, t, F32)
    return _pcall(body, name=name, grid=(t // tm,), in_specs=list(views),
                  out_specs=pl.BlockSpec((tm, w), lambda i: (i, 0)), out_shape=_sds((t, w), F32),
                  scratch_shapes=[_view_scratch(tm)] * 2,
                  compiler_params=_cparams("parallel"))(*parts)


def _rope_tables(t):
    inv = ROPE_BASE ** (-np.arange(0, MLA_ROPE, 2, dtype=np.float64) / MLA_ROPE)
    ang = np.arange(t, dtype=np.float64)[:, None] * inv[None, :]
    cos, sin = np.cos(ang), np.sin(ang)
    return (jnp.asarray(np.concatenate([cos, cos], 1), F32), jnp.asarray(np.concatenate([-sin, sin], 1), F32))


def _swap_halves(a):
    half = MLA_ROPE // 2
    return jnp.concatenate([a[:, half:], a[:, :half]], axis=1)


def _mla_qk_fwd(x, g, cos_t, sin_t, scale, name, tm):
    n, d = x.shape
    t = cos_t.shape[0]
    tm = min(tm, t)
    nt = t // tm

    def body(x_ref, g_ref, c_ref, s_ref, o_ref):
        xf = x_ref[...]
        r = lax.rsqrt(jnp.mean(xf * xf, axis=-1, keepdims=True) + EPS)
        y = xf * r * g_ref[...]
        yr = y[:, MLA_NOPE:]
        o_ref[:, :MLA_NOPE] = (y[:, :MLA_NOPE] * scale).astype(o_ref.dtype)
        o_ref[:, MLA_NOPE:] = ((yr * c_ref[...] + _swap_halves(yr) * s_ref[...]) * scale).astype(o_ref.dtype)

    row = pl.BlockSpec((tm, d), lambda i: (i, 0))
    tab = pl.BlockSpec((tm, MLA_ROPE), lambda i: (i % nt, 0))
    return _pcall(
        body, name=name, grid=(n // tm,),
        in_specs=[row, pl.BlockSpec((1, d), lambda i: (0, 0)), tab, tab],
        out_specs=row, out_shape=_sds((n, d), BF16),
        compiler_params=_cparams("parallel"))(x, g, cos_t, sin_t)


def _mla_qk_bwd(dy, x, g, cos_t, sin_t, scale, name, tm):
    n, d = x.shape
    t = cos_t.shape[0]
    tm = min(tm, t)
    nt = t // tm

    def body(dy_ref, x_ref, g_ref, c_ref, s_ref, dx_ref, dg_ref):
        xf = x_ref[...]
        gg = g_ref[...]
        r = lax.rsqrt(jnp.mean(xf * xf, axis=-1, keepdims=True) + EPS)
        xh = xf * r
        dyf = dy_ref[...] * scale
        dyr = dyf[:, MLA_NOPE:]
        dn_n = dyf[:, :MLA_NOPE]
        dn_r = dyr * c_ref[...] + _swap_halves(dyr * s_ref[...])
        xh_n, xh_r = xh[:, :MLA_NOPE], xh[:, MLA_NOPE:]
        dxh_n = dn_n * gg[:, :MLA_NOPE]
        dxh_r = dn_r * gg[:, MLA_NOPE:]
        mean = (jnp.sum(dxh_n * xh_n, axis=-1, keepdims=True)
                + jnp.sum(dxh_r * xh_r, axis=-1, keepdims=True)) * (1.0 / d)
        dx_ref[:, :MLA_NOPE] = r * (dxh_n - xh_n * mean)
        dx_ref[:, MLA_NOPE:] = r * (dxh_r - xh_r * mean)

        @pl.when(pl.program_id(0) == 0)
        def _():
            dg_ref[...] = jnp.zeros_like(dg_ref)

        dg_ref[:, :MLA_NOPE] += jnp.sum(dn_n * xh_n, axis=0, keepdims=True)
        dg_ref[:, MLA_NOPE:] += jnp.sum(dn_r * xh_r, axis=0, keepdims=True)

    row = pl.BlockSpec((tm, d), lambda i: (i, 0))
    vec = pl.BlockSpec((1, d), lambda i: (0, 0))
    tab = pl.BlockSpec((tm, MLA_ROPE), lambda i: (i % nt, 0))
    return _pcall(
        body, name=name, grid=(n // tm,),
        in_specs=[row, row, vec, tab, tab],
        out_specs=(row, vec), out_shape=(_sds((n, d), F32), _sds((1, d), F32)),
        compiler_params=_cparams("arbitrary"))(dy, x, g, cos_t, sin_t)


def _causal_mask(i, j, tq, tk, width):
    row = i * tq + lax.broadcasted_iota(jnp.int32, (tq, width), 0)
    col = j * tk + lax.broadcasted_iota(jnp.int32, (tq, width), 1)
    return col <= row


def _causal_steps(nq, nk, tq, tk, q_major):
    if q_major:
        groups = [[(i, j) for j in range((i * tq + tq - 1) // tk + 1)] for i in range(nq)]
        nunit = tk // tq if tk % tq == 0 else 1
    else:
        groups = [[(i, j) for i in range((j * tk) // tq, nq)] for j in range(nk)]
        nunit = tq // tk if tq % tk == 0 else 1
    it, jt, fl = [], [], []
    for g in groups:
        for n, (i, j) in enumerate(g):
            crossing = j * tk + tk - 1 > i * tq
            if q_major:
                unit = tk // nunit
                u = min(nunit, -(-(i * tq + tq - j * tk) // unit)) - 1
            else:
                unit = tq // nunit
                u = max(0, j * tk - i * tq) // unit
            it.append(i)
            jt.append(j)
            fl.append((n == 0) + 2 * (n == len(g) - 1) + 4 * crossing + 8 * (u if crossing else 0))
    return tuple(jnp.asarray(np.array(a, np.int32)) for a in (it, jt, fl)), nunit


def _by_crossing(flags, nunit, update):
    pl.when((flags & 4) == 0)(functools.partial(update, None))
    for u in range(nunit):
        pl.when(((flags & 4) != 0) & ((flags >> 3) == u))(functools.partial(update, u))


def _causal_specs(tq, tk):
    def qs(w):
        return pl.BlockSpec((None, tq, w), lambda h, s, it, jt, fl: (h, it[s], 0))

    def kv(w):
        return pl.BlockSpec((None, tk, w), lambda h, s, it, jt, fl: (h, jt[s], 0))

    return qs, kv


def _mla_fwd(q, k, v, tq, tk):
    nh, t, dq = q.shape
    dv = v.shape[2]
    tq, tk = min(tq, t), min(tk, t)
    tables, nunit = _causal_steps(t // tq, t // tk, tq, tk, True)

    def body(it, jt, fl, q_ref, k_ref, v_ref, o_ref, lse_ref, m_sc, l_sc, acc_sc):
        step = pl.program_id(1)
        i, j, flags = it[step], jt[step], fl[step]

        @pl.when((flags & 1) != 0)
        def _():
            m_sc[...] = jnp.full_like(m_sc, NEG)
            l_sc[...] = jnp.zeros_like(l_sc)
            acc_sc[...] = jnp.zeros_like(acc_sc)

        def update(units):
            wk = tk if units is None else (units + 1) * (tk // nunit)
            s = _dot(q_ref[...], k_ref[:wk, :], NT)
            if units is not None:
                s = jnp.where(_causal_mask(i, j, tq, tk, wk), s, NEG)
            m_prev = m_sc[...]
            m_new = jnp.maximum(m_prev, jnp.max(s, axis=-1, keepdims=True))
            alpha = jnp.exp(m_prev - m_new)
            p = jnp.exp(s - m_new)
            l_sc[...] = alpha * l_sc[...] + jnp.sum(p, axis=-1, keepdims=True)
            acc_sc[...] = alpha * acc_sc[...] + _dot(p.astype(BF16), v_ref[:wk, :], NN)
            m_sc[...] = m_new

        _by_crossing(flags, nunit, update)

        @pl.when((flags & 2) != 0)
        def _():
            o_ref[...] = acc_sc[...] / l_sc[...]
            lse_ref[...] = m_sc[...] + jnp.log(l_sc[...])

    qs, kv = _causal_specs(tq, tk)
    return _pcall(
        body, name="mla_attn_fwd",
        grid_spec=pltpu.PrefetchScalarGridSpec(
            num_scalar_prefetch=3, grid=(nh, tables[0].shape[0]),
            in_specs=[qs(dq), kv(dq), kv(dv)], out_specs=(qs(dv), qs(1)),
            scratch_shapes=[pltpu.VMEM((tq, 1), F32), pltpu.VMEM((tq, 1), F32), pltpu.VMEM((tq, dv), F32)]),
        out_shape=(_sds((nh, t, dv), F32), _sds((nh, t, 1), F32)),
        compiler_params=_cparams("parallel", "arbitrary"))(*tables, q, k, v)


def _mla_bwd(q, k, k_t, v, do, lse_row, dl_row, tq, tk):
    nh, t, dq = q.shape
    dv = v.shape[2]
    tq, tk = min(tq, t), min(tk, t)
    nq = t // tq
    tables, nunit = _causal_steps(nq, t // tk, tq, tk, False)

    def body(it, jt, fl, q_ref, k_ref, kt_ref, v_ref, do_ref, lse_ref, dl_ref, dk_ref, dv_ref, dq_ref, dk_sc, dv_sc):
        step = pl.program_id(1)
        i, j, flags = it[step], jt[step], fl[step]

        def update(units):
            off = 0 if units is None else units * (tq // nunit)
            qq = q_ref[off:, :]
            st = _dot(k_ref[...], qq, NT)
            if units is not None:
                key = j * tk + lax.broadcasted_iota(jnp.int32, (tk, tq - off), 0)
                qry = i * tq + off + lax.broadcasted_iota(jnp.int32, (tk, tq - off), 1)
                st = jnp.where(key <= qry, st, NEG)
            pt = jnp.exp(st - lse_ref[:, off:])
            dob = do_ref[off:, :].astype(BF16)
            dpt = _dot(v_ref[...], dob, NT)
            dst = pt * (dpt - dl_ref[:, off:])
            dsb = dst.astype(BF16)
            dv_part = _dot(pt.astype(BF16), dob, NN)
            dk_part = _dot(dsb, qq, NN)
            dq_part = _dot(kt_ref[...], dsb, NN)

            @pl.when((flags & 1) != 0)
            def _():
                dv_sc[...] = dv_part
                dk_sc[...] = dk_part

            @pl.when((flags & 1) == 0)
            def _():
                dv_sc[...] += dv_part
                dk_sc[...] += dk_part

            if off == 0:
                @pl.when(j == 0)
                def _():
                    dq_ref[i] = dq_part

                @pl.when(j != 0)
                def _():
                    dq_ref[i] += dq_part
            else:
                dq_ref[i, :, off:] += dq_part

        _by_crossing(flags, nunit, update)

        @pl.when((flags & 2) != 0)
        def _():
            dk_ref[...] = dk_sc[...]
            dv_ref[...] = dv_sc[...]

    qs, kv = _causal_specs(tq, tk)
    rowv = pl.BlockSpec((None, 1, tq), lambda h, s, it, jt, fl: (h, 0, it[s]))
    ktv = pl.BlockSpec((None, dq, tk), lambda h, s, it, jt, fl: (h, 0, jt[s]))
    whole = pl.BlockSpec((None, nq, dq, tq), lambda h, s, it, jt, fl: (h, 0, 0, 0))
    return _pcall(
        body, name="mla_attn_bwd",
        grid_spec=pltpu.PrefetchScalarGridSpec(
            num_scalar_prefetch=3, grid=(nh, tables[0].shape[0]),
            in_specs=[qs(dq), kv(dq), ktv, kv(dv), qs(dv), rowv, rowv], out_specs=(kv(dq), kv(dv), whole),
            scratch_shapes=[pltpu.VMEM((tk, dq), F32), pltpu.VMEM((tk, dv), F32)]),
        out_shape=(_sds((nh, t, dq), F32), _sds((nh, t, dv), F32), _sds((nh, nq, dq, tq), F32)),
        compiler_params=_cparams("parallel", "arbitrary"))(*tables, q, k, k_t, v, do, lse_row, dl_row)


def _loss_head(y, target, tm):
    t, d = y.shape
    tm = min(tm, t)
    nt = t // tm

    def body(y_ref, t_ref, dy_ref, loss_ref, acc):
        i = pl.program_id(0)
        err = y_ref[...] - t_ref[...]
        dy_ref[...] = err * (1.0 / d)

        @pl.when(i == 0)
        def _():
            acc[...] = jnp.zeros_like(acc)

        acc[...] += jnp.sum(err * err, axis=0, keepdims=True)

        @pl.when(i == nt - 1)
        def _():
            loss_ref[0, 0] = jnp.sum(acc[...]) * (0.5 / d)

    spec = pl.BlockSpec((tm, d), lambda i: (i, 0))
    return _pcall(
        body, name="loss_head", grid=(nt,), in_specs=[spec, spec],
        out_specs=(spec, pl.BlockSpec(memory_space=pltpu.SMEM)),
        out_shape=(_sds((t, d), F32), _sds((1, 1), F32)),
        scratch_shapes=[pltpu.VMEM((1, d), F32)],
        compiler_params=_cparams("arbitrary"))(y, target)


def _adamw(w, g, m, v, name):
    r, c = w.shape
    tr = r
    for cand in (256, 128, 64, 32, 16, 8):
        if r % cand == 0:
            tr = cand
            break

    def body(w_ref, g_ref, m_ref, v_ref, d_ref, nm_ref, nv_ref):
        gg = g_ref[...]
        nm = ADAM_B1 * m_ref[...] + (1.0 - ADAM_B1) * gg
        nv = ADAM_B2 * v_ref[...] + (1.0 - ADAM_B2) * (gg * gg)
        m_hat = nm / (1.0 - ADAM_B1 ** ADAM_STEP)
        v_hat = nv / (1.0 - ADAM_B2 ** ADAM_STEP)
        d_ref[...] = -ADAM_LR * (m_hat / (jnp.sqrt(v_hat) + ADAM_EPS) + ADAM_WD * w_ref[...])
        nm_ref[...] = nm
        nv_ref[...] = nv

    spec = pl.BlockSpec((tr, c), lambda i: (i, 0))
    sd = _sds((r, c), F32)
    return _pcall(body, name=name, grid=(r // tr,), in_specs=[spec] * 4, out_specs=(spec,) * 3,
                  out_shape=(sd, sd, sd), compiler_params=_cparams("parallel"))(w, g, m, v)


MESH_ID = pl.DeviceIdType.MESH
HBM_SPEC = pl.BlockSpec(memory_space=pltpu.HBM)


def _place():
    return lax.axis_index("x"), lax.axis_index("y"), lax.axis_index("c")


def _other_chips(x, y):
    return [(1 - x, y), (x, 1 - y), (1 - x, 1 - y)]


def _remote(src, dst, send_sems, recv_sems, k, to):
    return pltpu.make_async_remote_copy(src_ref=src, dst_ref=dst, send_sem=send_sems.at[k], recv_sem=recv_sems.at[k],
                                        device_id=to, device_id_type=MESH_ID)


def _halves(arrays):
    for a in arrays:
        assert a.shape[-2] % 32 == 0
    return [a.shape[-2] // 2 for a in arrays]


def _gather_start(srcs, outs, halves, send_sems, recv_sems):
    x, y, c = _place()
    for a, half in enumerate(halves):
        rows = pl.ds(c * half, half)
        for k, (cx, cy) in enumerate(_other_chips(x, y)):
            _remote(srcs[a].at[rows, :], outs[a].at[2 * x + y, rows, :], send_sems, recv_sems, 3 * a + k,
                    (cx, cy, c)).start()


def _gather_wait(outs, halves, send_sems, recv_sems):
    x, y, c = _place()
    for a, half in enumerate(halves):
        for k, (cx, cy) in enumerate(_other_chips(x, y)):
            got = outs[a].at[2 * cx + cy, pl.ds(c * half, half), :]
            _remote(got, got, send_sems, recv_sems, 3 * a + k, (x, y, c)).wait()


def _forward_cores(partly):
    n = len(partly)
    halves = _halves(partly)

    def body(*refs):
        srcs, outs, send_sems, recv_sems = refs[:n], refs[n:2 * n], refs[2 * n], refs[2 * n + 1]
        x, y, c = _place()
        for a, half in enumerate(halves):
            for k, (cx, cy) in enumerate(_other_chips(x, y)):
                rows = pl.ds(c * half, half)
                _remote(srcs[a].at[2 * cx + cy, rows, :], outs[a].at[2 * cx + cy, rows, :], send_sems, recv_sems,
                        3 * a + k, (x, y, 1 - c)).start()
        for a, half in enumerate(halves):
            for k, (cx, cy) in enumerate(_other_chips(x, y)):
                mine = outs[a].at[2 * cx + cy, pl.ds(c * half, half), :]
                theirs = outs[a].at[2 * cx + cy, pl.ds((1 - c) * half, half), :]
                _remote(mine, theirs, send_sems, recv_sems, 3 * a + k, (x, y, c)).wait()

    return _pcall(
        body, name="forward_cores", in_specs=[HBM_SPEC] * n, out_specs=tuple([HBM_SPEC] * n),
        out_shape=tuple(_sds(p.shape, p.dtype) for p in partly), input_output_aliases={a: a for a in range(n)},
        scratch_shapes=[pltpu.SemaphoreType.DMA((3 * n,)), pltpu.SemaphoreType.DMA((3 * n,))],
    )(*partly)


def _gather_weights(blocks):
    n = len(blocks)
    halves = _halves(blocks)

    def body(*refs):
        srcs, outs, send_sems, recv_sems = refs[:n], refs[n:2 * n], refs[2 * n], refs[2 * n + 1]
        x, y, c = _place()
        me = 2 * x + y
        sibling = (x, y, 1 - c)
        chips = _other_chips(x, y)

        def part(a, chip, core):
            return outs[a].at[chip, pl.ds(core * halves[a], halves[a]), :]

        for a in range(n):
            mine = srcs[a].at[pl.ds(c * halves[a], halves[a]), :]
            for k, (cx, cy) in enumerate(chips):
                _remote(mine, part(a, me, c), send_sems, recv_sems, 6 * a + k, (cx, cy, c)).start()
        for k, (cx, cy) in enumerate(chips):
            for a in range(n):
                got = part(a, 2 * cx + cy, c)
                _remote(got, got, send_sems, recv_sems, 6 * a + k, (x, y, c)).wait_recv()
                _remote(got, got, send_sems, recv_sems, 6 * a + 3 + k, sibling).start()
        for k, (cx, cy) in enumerate(chips):
            for a in range(n):
                got = part(a, 2 * cx + cy, 1 - c)
                _remote(got, got, send_sems, recv_sems, 6 * a + 3 + k, (x, y, c)).wait_recv()
        for a in range(n):
            sent = part(a, me, c)
            for k in range(6):
                _remote(sent, sent, send_sems, recv_sems, 6 * a + k, (x, y, c)).wait_send()

    return _pcall(
        body, name="gather_weights", in_specs=[HBM_SPEC] * n, out_specs=tuple([HBM_SPEC] * n),
        out_shape=tuple(_sds((N_CHIPS,) + b.shape, b.dtype) for b in blocks),
        scratch_shapes=[pltpu.SemaphoreType.DMA((6 * n,)), pltpu.SemaphoreType.DMA((6 * n,))],
    )(*blocks)


def _reduce_cores(grads, tag):
    n = len(grads)
    halves = _halves(grads)

    def body(*refs):
        gs, outs, send_sems, recv_sems = refs[:n], refs[n:2 * n], refs[2 * n], refs[2 * n + 1]
        x, y, c = _place()
        for a in range(n):
            for j in range(N_CHIPS):
                _remote(gs[a].at[j, pl.ds((1 - c) * halves[a], halves[a]), :], outs[a].at[j],
                        send_sems, recv_sems, a, (x, y, 1 - c)).start()
        for a in range(n):
            _remote(gs[a].at[:, pl.ds((1 - c) * halves[a], halves[a]), :], outs[a],
                    send_sems, recv_sems, a, (x, y, c)).wait()

    return _pcall(
        body, name=f"reduce_cores_{tag}", in_specs=[HBM_SPEC] * n, out_specs=tuple([HBM_SPEC] * n),
        out_shape=tuple(_sds((N_CHIPS, h, g.shape[2]), g.dtype) for g, h in zip(grads, halves)),
        scratch_shapes=[pltpu.SemaphoreType.DMA((n,)), pltpu.SemaphoreType.DMA((n,))],
    )(*grads)


def _scatter_shapes(parts):
    return tuple(_sds((3,) + p.shape[1:], p.dtype) for p in parts)


def _scatter_sems(n):
    return [pltpu.SemaphoreType.DMA((3 * n,)), pltpu.SemaphoreType.DMA((3 * n,))]


def _scatter_start(ps, outs, send_sems, recv_sems):
    x, y, c = _place()
    for a in range(len(ps)):
        for k, (cx, cy) in enumerate(_other_chips(x, y)):
            _remote(ps[a].at[2 * cx + cy], outs[a].at[k], send_sems, recv_sems, 3 * a + k, (cx, cy, c)).start()


def _scatter_wait(ps, outs, send_sems, recv_sems):
    x, y, c = _place()
    for a in range(len(ps)):
        for k in range(3):
            _remote(ps[a].at[k], outs[a].at[k], send_sems, recv_sems, 3 * a + k, (x, y, c)).wait()


def _sum_partials(received, parts, place):
    n = len(parts)
    steps = 2
    tiles = [p.shape[1] // steps for p in parts]

    def body(place_ref, *refs):
        rs, ps, outs = refs[:n], refs[n:2 * n], refs[2 * n:]
        for a in range(n):
            tot = ps[a][...].astype(F32)
            for k in range(3):
                tot = tot + rs[a][k].astype(F32)
            outs[a][...] = tot

    cols = [p.shape[2] for p in parts]
    return _pcall(
        body, name="sum_chip_partials",
        grid_spec=pltpu.PrefetchScalarGridSpec(
            num_scalar_prefetch=1, grid=(steps,),
            in_specs=[pl.BlockSpec((3, tm, w), lambda i, pc: (0, i, 0)) for tm, w in zip(tiles, cols)]
            + [pl.BlockSpec((None, tm, w), lambda i, pc: (pc[0], i, 0)) for tm, w in zip(tiles, cols)],
            out_specs=tuple(pl.BlockSpec((tm, w), lambda i, pc: (pc[1] * steps + i, 0)) for tm, w in zip(tiles, cols))),
        out_shape=tuple(_sds((2 * p.shape[1], p.shape[2]), F32) for p in parts),
        compiler_params=_cparams("parallel"))(place, *received, *parts)


def _share_cores(blocks):
    n = len(blocks)
    halves = _halves(blocks)

    def body(*refs):
        srcs, outs, send_sems, recv_sems = refs[:n], refs[n:2 * n], refs[2 * n], refs[2 * n + 1]
        x, y, c = _place()
        for a in range(n):
            piece = pl.ds(c * halves[a], halves[a])
            _remote(srcs[a].at[piece, :], outs[a].at[piece, :], send_sems, recv_sems, a, (x, y, 1 - c)).start()
        for a in range(n):
            mine = outs[a].at[pl.ds(c * halves[a], halves[a]), :]
            theirs = outs[a].at[pl.ds((1 - c) * halves[a], halves[a]), :]
            _remote(mine, theirs, send_sems, recv_sems, a, (x, y, c)).wait()

    return _pcall(
        body, name="share_cores", in_specs=[HBM_SPEC] * n, out_specs=tuple([HBM_SPEC] * n),
        out_shape=tuple(_sds(b.shape, b.dtype) for b in blocks), input_output_aliases={a: a for a in range(n)},
        scratch_shapes=[pltpu.SemaphoreType.DMA((n,)), pltpu.SemaphoreType.DMA((n,))],
    )(*blocks)


def _sum_blocks(stacked, name, tm):
    n, rows, lanes = stacked.shape
    tm = min(tm, rows)

    def body(s_ref, o_ref):
        tot = s_ref[n - 1].astype(F32)
        for k in range(n - 1):
            tot = tot + s_ref[k].astype(F32)
        o_ref[...] = tot

    return _pcall(body, name=name, grid=(rows // tm,),
                  in_specs=[pl.BlockSpec((n, tm, lanes), lambda i: (0, i, 0))],
                  out_specs=pl.BlockSpec((tm, lanes), lambda i: (i, 0)), out_shape=_sds((rows, lanes), F32),
                  compiler_params=_cparams("parallel"))(stacked)


def _add_halves(grads, theirs, core, tag):
    n = len(grads)
    steps = 2
    tiles = [t.shape[1] // steps for t in theirs]
    cols = [t.shape[2] for t in theirs]

    def body(c_ref, *refs):
        gs, ts, outs = refs[:n], refs[n:2 * n], refs[2 * n:]
        for a in range(n):
            outs[a][...] = (gs[a][...] + ts[a][...]).astype(BF16)

    own = [pl.BlockSpec((None, tm, w), lambda k, i, c: (k, c[0] * steps + i, 0)) for tm, w in zip(tiles, cols)]
    same = [pl.BlockSpec((None, tm, w), lambda k, i, c: (k, i, 0)) for tm, w in zip(tiles, cols)]
    return _pcall(
        body, name=f"add_core_halves_{tag}",
        grid_spec=pltpu.PrefetchScalarGridSpec(
            num_scalar_prefetch=1, grid=(N_CHIPS, steps), in_specs=own + same, out_specs=tuple(same)),
        out_shape=tuple(_sds(t.shape, BF16) for t in theirs),
        compiler_params=_cparams("parallel", "parallel"))(core, *grads, *theirs)


def _allreduce_small(part):
    rows, lanes = part.shape
    ndev = 8

    def body(src, tot, buf, send_sems, recv_sems):
        x, y, c = _place()
        me = 4 * x + 2 * y + c
        buf[me] = src[...]
        sends = []
        for k in range(1, ndev):
            peer = (x ^ (k >> 2), y ^ ((k >> 1) & 1), c ^ (k & 1))
            cp = _remote(src, buf.at[me], send_sems, recv_sems, k - 1, peer)
            cp.start()
            sends.append(cp)
        for k in range(1, ndev):
            theirs = buf.at[me ^ k]
            _remote(theirs, theirs, send_sems, recv_sems, k - 1, (x, y, c)).wait_recv()
        for cp in sends:
            cp.wait_send()
        acc = buf[0]
        for d in range(1, ndev):
            acc = acc + buf[d]
        tot[...] = acc

    vm = pl.BlockSpec(memory_space=pltpu.VMEM)
    return _pcall(
        body, name="allreduce_small", in_specs=[vm], out_specs=vm, out_shape=_sds((rows, lanes), F32),
        scratch_shapes=[pltpu.VMEM((ndev, rows, lanes), F32), pltpu.SemaphoreType.DMA((ndev - 1,)),
                        pltpu.SemaphoreType.DMA((ndev - 1,))],
    )(part)


def _pack_small(vals):
    parts = []
    for name, shape, r in SMALL:
        flat = vals[name].reshape(-1).astype(F32)
        parts.append(jnp.pad(flat, (0, r * LANES - flat.shape[0])).reshape(r, LANES))
    used = sum(r for _, _, r in SMALL)
    parts.append(jnp.zeros((SMALL_ROWS - used, LANES), F32))
    return jnp.concatenate(parts, axis=0)


def _unpack_small(packed):
    out, off = {}, 0
    for name, shape, r in SMALL:
        n = int(np.prod(shape))
        out[name] = packed[off:off + r].reshape(-1)[:n].reshape(shape)
        off += r
    return out


def _heads_major(a, nh):
    t = a.shape[0]
    return a.reshape(t, nh, a.shape[1] // nh).transpose(1, 0, 2)


def _tokens_major(a):
    nh, t, w = a.shape
    return a.transpose(1, 0, 2).reshape(t, nh * w)


LATE = ("ffn1_w_gate", "ffn1_w_up", "ffn1_w_down")
EARLY = tuple(name for name, _ in BIG if name not in LATE)


def _local_step(x, target, small, wfull, exchanges=None, later_weights=None):
    t = x.shape[0]
    nh, hd = DIL_HEADS, DIL_HD
    grads_s, grads_b = {}, {}

    x1, ffn1_saved, partly = _ffn_fwd(x, small["ffn1_norm"], wfull["ffn1_w_gate"], wfull["ffn1_w_up"],
                                      wfull["ffn1_w_down"], "ffn1", later_weights[0] if later_weights else ())
    if later_weights:
        wfull = {**wfull, **later_weights[1](partly)}
    w_in = wfull["w_in"].transpose(1, 0, 2).reshape(D_MODEL, -1)
    w_out = wfull["w_out"].reshape(D_MODEL, D_MODEL)
    w_qb, w_kvb = wfull["mla_w_q_b"], wfull["mla_w_kv_b"]
    hm = _rms_fwd(x1, small["mix_norm"], BF16, "mix_norm", 512)
    proj = _mm_simple("in_proj", hm, w_in, NN, F32, tm=1024)
    cq, ckv, k_pe = proj[:, 1536:1792], proj[:, 1792:1920], proj[:, 1920:1984]

    gq, gk = jnp.tile(small["dil_q_norm"], (1, nh)), jnp.tile(small["dil_k_norm"], (1, nh))
    qn = _head_norm_fwd(proj, 0, gq, "dil_q_norm", 512)
    kn = _head_norm_fwd(proj, 1, gk, "dil_k_norm", 512)
    v_d = _head_norm_fwd(proj, 2, None, "dil_v_views", 512)
    bias = _bias_tiles(small["rel_bias"]).reshape(3, nh // 2, 2 * QB, QB + DIL_W)
    outs, lses = [], []
    for b, dil in enumerate(DIL_DILATIONS):
        o_b, lse_b = _dil_fwd(qn[b], kn[b], v_d[b], bias[b], dil, f"dil_fwd_{dil}")
        outs.append(o_b)
        lses.append(lse_b)
    o_dil, lse_tot, od = _dil_merge(outs, lses, small["out_norm_dil"], 512)

    mh = MLA_HEADS
    cos_t, sin_t = _rope_tables(t)
    cqn = _rms_fwd(cq, small["mla_q_a_norm"], BF16, "mla_q_a_norm", 512)
    ckvn = _rms_fwd(ckv, small["mla_kv_a_norm"], BF16, "mla_kv_a_norm", 512)
    tm = min(512, t)

    th = min(2048, t)

    def head_proj(name, a, w, width):
        k = a.shape[1]
        return _mm(name, (mh, t // th, 1),
                   [(a, pl.BlockSpec((th, k), lambda h, i, r: (i, 0)), w, pl.BlockSpec((None, k, width), lambda h, i, r: (h, 0, 0)))],
                   NN, _sds((mh, t, width), F32), pl.BlockSpec((None, th, width), lambda h, i, r: (h, i, 0)), (th, width))

    q_raw = head_proj("mla_q_proj", cqn, w_qb, MLA_QK)
    kv_raw = head_proj("mla_kv_proj", ckvn, w_kvb, MLA_NOPE + MLA_V)
    k_raw = jnp.concatenate([kv_raw[:, :, :MLA_NOPE], jnp.broadcast_to(k_pe[None], (mh, t, MLA_ROPE))], axis=2)
    v_m = kv_raw[:, :, MLA_NOPE:].astype(BF16)
    q_raw2, k_raw2 = q_raw.reshape(mh * t, MLA_QK), k_raw.reshape(mh * t, MLA_QK)
    q_scale = MLA_QK ** -0.5
    q_m = _mla_qk_fwd(q_raw2, small["mla_q_norm"], cos_t, sin_t, q_scale, "mla_q_rope", 2048).reshape(mh, t, MLA_QK)
    k_m = _mla_qk_fwd(k_raw2, small["mla_k_norm"], cos_t, sin_t, 1.0, "mla_k_rope", 2048).reshape(mh, t, MLA_QK)
    o_mla_h, lse_m = _mla_fwd(q_m, k_m, v_m, 512, 4096)
    o_mla = _tokens_major(o_mla_h)

    om = _rms_fwd(o_mla, small["out_norm_mla"], BF16, "out_norm_mla", 512)
    half_w = DIL_WIDTH
    row = pl.BlockSpec((tm, D_MODEL), lambda i, j, r: (i, 0))
    act_spec = pl.BlockSpec((tm, half_w), lambda i, j, r: (i, 0))
    x2 = _mm("out_proj", (t // tm, 1, 1),
             [(od, act_spec, w_out, pl.BlockSpec((half_w, D_MODEL), lambda i, j, r: (0, 0))),
              (om, act_spec, w_out, pl.BlockSpec((half_w, D_MODEL), lambda i, j, r: (1, 0)))],
             NN, _sds((t, D_MODEL), F32), row, (tm, D_MODEL), res=(x1, row))
    x3, ffn2_saved, _ = _ffn_fwd(x2, small["ffn2_norm"], wfull["ffn2_w_gate"], wfull["ffn2_w_up"],
                                 wfull["ffn2_w_down"], "ffn2")
    dy, loss = _loss_head(x3, target, 512)

    dx2, grads_s["ffn2_norm"], grads_b["ffn2_w_gate"], grads_b["ffn2_w_up"], grads_b["ffn2_w_down"], _, _ = _ffn_bwd(
        dy, x2, small["ffn2_norm"], wfull["ffn2_w_gate"], wfull["ffn2_w_up"], wfull["ffn2_w_down"], ffn2_saved, "ffn2")

    d_ocat = _mm_simple("out_proj_dx", dx2, w_out, NT, F32, tm=1024)
    dw_out_d = _mm_simple("out_proj_dw_dil", od, dx2, TN, F32, tk=2048)
    dw_out_m = _mm_simple("out_proj_dw_mla", om, dx2, TN, F32, tk=2048)
    grads_b["w_out"] = jnp.concatenate([dw_out_d, dw_out_m], axis=0).reshape(N_CHIPS, D_MODEL // N_CHIPS, D_MODEL)
    do_dil, grads_s["out_norm_dil"] = _rms_bwd([d_ocat[:, :half_w]], o_dil, small["out_norm_dil"], None, "out_norm_dil_bwd", 512)
    do_mla, grads_s["out_norm_mla"] = _rms_bwd([d_ocat[:, half_w:]], o_mla, small["out_norm_mla"], None, "out_norm_mla_bwd", 512)

    do_m = _heads_major(do_mla, mh)
    dl_m = _rowdot(do_m.reshape(mh * t, MLA_V), o_mla_h.reshape(mh * t, MLA_V), "mla_delta", 2048).reshape(mh, t, 1)
    dk_m, dv_m, dq_t = _mla_bwd(q_m, k_m, k_m.transpose(0, 2, 1), v_m, do_m, lse_m.reshape(mh, 1, t),
                                dl_m.reshape(mh, 1, t), 2048, 512)
    dq_m = dq_t.transpose(0, 1, 3, 2).reshape(mh, t, MLA_QK)
    dq_raw, grads_s["mla_q_norm"] = _mla_qk_bwd(dq_m.reshape(mh * t, MLA_QK), q_raw2, small["mla_q_norm"],
                                                 cos_t, sin_t, q_scale, "mla_q_rope_bwd", 2048)
    dk_raw, grads_s["mla_k_norm"] = _mla_qk_bwd(dk_m.reshape(mh * t, MLA_QK), k_raw2, small["mla_k_norm"],
                                                 cos_t, sin_t, 1.0, "mla_k_rope_bwd", 2048)
    dq_raw = dq_raw.reshape(mh, t, MLA_QK)
    dk_raw = dk_raw.reshape(mh, t, MLA_QK)
    dkv_raw = jnp.concatenate([dk_raw[:, :, :MLA_NOPE], dv_m], axis=2)
    dk_pe_h = dk_raw[:, :, MLA_NOPE:]

    def head_proj_dx(name, d, w):
        width, k = d.shape[2], w.shape[1]
        pairs = [(d, pl.BlockSpec((None, th, width), lambda i, j, r, h=h: (h, i, 0)),
                  w, pl.BlockSpec((None, k, width), lambda i, j, r, h=h: (h, 0, 0))) for h in range(mh)]
        return _mm(name, (t // th, 1, 1), pairs, NT, _sds((t, k), F32),
                   pl.BlockSpec((th, k), lambda i, j, r: (i, 0)), (th, k))

    def head_proj_dw(name, a, d):
        width, k = d.shape[2], a.shape[1]
        return _mm(name, (mh, 1, t // th),
                   [(a, pl.BlockSpec((th, k), lambda h, j, r: (r, 0)), d, pl.BlockSpec((None, th, width), lambda h, j, r: (h, r, 0)))],
                   TN, _sds((mh, k, width), F32), pl.BlockSpec((None, k, width), lambda h, j, r: (h, 0, 0)), (k, width))

    d_cqn = head_proj_dx("mla_q_proj_dx", dq_raw, w_qb)
    d_ckvn = head_proj_dx("mla_kv_proj_dx", dkv_raw, w_kvb)
    grads_b["mla_w_q_b"] = head_proj_dw("mla_q_proj_dw", cqn, dq_raw)
    grads_b["mla_w_kv_b"] = head_proj_dw("mla_kv_proj_dw", ckvn, dkv_raw)
    d_cq, grads_s["mla_q_a_norm"] = _rms_bwd([d_cqn], cq, small["mla_q_a_norm"], None, "mla_q_a_norm_bwd", 512)
    d_ckv, grads_s["mla_kv_a_norm"] = _rms_bwd([d_ckvn], ckv, small["mla_kv_a_norm"], None, "mla_kv_a_norm_bwd", 512)
    d_kpe = _sum_blocks(dk_pe_h.reshape(mh, t * MLA_ROPE // LANES, LANES), "mla_kpe_sum", 1024).reshape(t, MLA_ROPE)

    stats, do_db = _dil_stats(do_dil, o_dil, lse_tot, 512)
    dqs, dks, dvs, dtiles = [], [], [], []
    for b, dil in enumerate(DIL_DILATIONS):
        dq_b, dk_b, dv_b, db_b = _dil_bwd(qn[b], kn[b], v_d[b], do_db[b], stats[b], bias[b], dil, f"dil_bwd_{dil}")
        dqs.append(dq_b)
        dks.append(dk_b)
        dvs.append(dv_b)
        dtiles.append(db_b)
    grads_s["rel_bias"] = _bias_grad(jnp.stack(dtiles).reshape(3, nh, QB, QB + DIL_W))
    dq_a, dgq = _head_norm_bwd(dqs, proj, 0, gq, "dil_q_norm_bwd", 512)
    dk_a, dgk = _head_norm_bwd(dks, proj, 1, gk, "dil_k_norm_bwd", 512)
    grads_s["dil_q_norm"], grads_s["dil_k_norm"] = dgq[:, :hd], dgk[:, :hd]
    dv_a = _sum_branches(dvs, "dil_dv_sum", 512)

    dparts = [dq_a, dk_a, dv_a, d_cq, d_ckv, d_kpe]
    t2 = min(512, t)
    pairs, dw_parts, lo = [], [], 0
    for n, dpart in enumerate(dparts):
        width = dpart.shape[1]
        w_part = w_in[:, lo:lo + width]
        pairs.append((dpart, pl.BlockSpec((t2, width), lambda i, j, r: (i, 0)),
                      w_part, pl.BlockSpec((D_MODEL, width), lambda i, j, r: (0, 0))))
        dw_parts.append(_mm_simple(f"in_proj_dw_{n}", hm, dpart, TN, F32, tk=2048))
        lo += width
    row2 = pl.BlockSpec((t2, D_MODEL), lambda i, j, r: (i, 0))
    dx1, grads_s["mix_norm"] = _mm("in_proj_dx", (t // t2, 1, 1), pairs, NT, _sds((t, D_MODEL), F32), row2,
                                   (t2, D_MODEL), res=(dx2, row2), norm=(x1, small["mix_norm"]))
    dw_in = jnp.concatenate(dw_parts, axis=1)
    grads_b["w_in"] = dw_in.reshape(D_MODEL, N_CHIPS, -1).transpose(1, 0, 2)
    outgoing = exchanges[0]([grads_b[n] for n in EARLY]) if exchanges else ()
    dx, grads_s["ffn1_norm"], grads_b["ffn1_w_gate"], grads_b["ffn1_w_up"], grads_b["ffn1_w_down"], arrived, late = _ffn_bwd(
        dx1, x, small["ffn1_norm"], wfull["ffn1_w_gate"], wfull["ffn1_w_up"], wfull["ffn1_w_down"], ffn1_saved, "ffn1",
        outgoing, exchanges[1] if exchanges else None)
    return loss, dx, grads_s, grads_b, (tuple(outgoing), arrived), late


def kernel(x, ffn1_norm, ffn1_w_gate, ffn1_w_up, ffn1_w_down, mix_norm, w_in, dil_q_norm, dil_k_norm, rel_bias, mla_q_a_norm, mla_w_q_b, mla_kv_a_norm, mla_w_kv_b, mla_q_norm, mla_k_norm, out_norm_dil, out_norm_mla, w_out, ffn2_norm, ffn2_w_gate, ffn2_w_up, ffn2_w_down, loss_target, m_ffn1_norm, m_ffn1_w_gate, m_ffn1_w_up, m_ffn1_w_down, m_mix_norm, m_w_in, m_dil_q_norm, m_dil_k_norm, m_rel_bias, m_mla_q_a_norm, m_mla_w_q_b, m_mla_kv_a_norm, m_mla_w_kv_b, m_mla_q_norm, m_mla_k_norm, m_out_norm_dil, m_out_norm_mla, m_w_out, m_ffn2_norm, m_ffn2_w_gate, m_ffn2_w_up, m_ffn2_w_down, v_ffn1_norm, v_ffn1_w_gate, v_ffn1_w_up, v_ffn1_w_down, v_mix_norm, v_w_in, v_dil_q_norm, v_dil_k_norm, v_rel_bias, v_mla_q_a_norm, v_mla_w_q_b, v_mla_kv_a_norm, v_mla_w_kv_b, v_mla_q_norm, v_mla_k_norm, v_out_norm_dil, v_out_norm_mla, v_w_out, v_ffn2_norm, v_ffn2_w_gate, v_ffn2_w_up, v_ffn2_w_down):
    given = dict(locals())
    big_names = [name for name, _ in BIG]
    small_names = [name for name, _, _ in SMALL]

    chip = (2 * lax.axis_index("x") + lax.axis_index("y")).astype(jnp.int32)
    core = lax.axis_index("c").astype(jnp.int32)
    mine = {n: given[n].astype(BF16) for n in big_names}

    def with_own(names, arrays):
        return {n: lax.dynamic_update_slice(a, mine[n], (chip, 0, 0)) for n, a in zip(names, arrays)}

    wfirst = with_own(LATE, _gather_weights([mine[n][0] for n in LATE]))
    later_weights = ([mine[n][0] for n in EARLY], lambda partly: with_own(EARLY, _forward_cores(partly)))
    small = {n: given[n] for n in small_names}

    def chip_partials(partial, tag):
        return _add_halves(partial, _reduce_cores(partial, tag), core.reshape(1), tag)

    exchanges = (functools.partial(chip_partials, tag="early"), functools.partial(chip_partials, tag="late"))
    loss, dx, grads_s, grads_b, (early_part, early_got), (late_part, late_got) = _local_step(
        x[0], loss_target[0], small, wfirst, exchanges, later_weights)
    loss = lax.psum(loss[0, 0], ("x", "y", "c"))
    reduced = _sum_partials(tuple(late_got) + tuple(early_got), tuple(late_part) + tuple(early_part),
                            jnp.stack([chip, core]))
    g_big = dict(zip(LATE + EARLY, _share_cores(reduced)))
    g_small = _unpack_small(_allreduce_small(_pack_small(grads_s)))

    grad, delta, new_m, new_v = {}, {}, {}, {}
    for name, shape in BIG:
        g2 = g_big[name]
        d_, m_, v_ = _adamw(given[name].reshape(shape), g2, given["m_" + name].reshape(shape),
                            given["v_" + name].reshape(shape), f"adamw_{name}")
        full = given[name].shape
        grad[name], delta[name], new_m[name], new_v[name] = (a.reshape(full) for a in (g2, d_, m_, v_))
    for name in small_names:
        grad[name] = g_small[name]
        delta[name], new_m[name], new_v[name] = _adamw(given[name], g_small[name], given["m_" + name],
                                                       given["v_" + name], f"adamw_{name}")

    return (loss, dx[None], *[grad[n] for n in WEIGHTS], *[delta[n] for n in WEIGHTS],
            *[new_m[n] for n in WEIGHTS], *[new_v[n] for n in WEIGHTS])
```

```python
import functools

import numpy as np
import jax
import jax.numpy as jnp
from jax import lax
from jax.experimental import pallas as pl
from jax.experimental.pallas import tpu as pltpu

F32 = jnp.float32
BF16 = jnp.bfloat16

D_MODEL = 1024
D_FF = 2816
N_CHIPS = 4
DIL_HEADS = 8
DIL_HD = 64
DIL_WIDTH = 512
DIL_DILATIONS = (1, 4, 16)
DIL_W = 128
QB = 128
MLA_HEADS = 4
MLA_NOPE = 128
MLA_ROPE = 64
MLA_QK = 192
MLA_V = 128
MLA_Q_RANK = 256
MLA_KV_RANK = 128
ROPE_BASE = 10000.0
REL_BUCKETS = 32
REL_MAX_DIST = 2048
FFN_RESID = 0.5
EPS = 1e-6
NEG = -1e30
LANES = 128

ADAM_LR = 0.001
ADAM_B1 = 0.9
ADAM_B2 = 0.999
ADAM_EPS = 1e-08
ADAM_WD = 0.01
ADAM_STEP = 10

NT = (((1,), (1,)), ((), ()))
NN = (((1,), (0,)), ((), ()))
TN = (((0,), (0,)), ((), ()))

BIG = (
    ("ffn1_w_gate", (D_MODEL, D_FF // N_CHIPS)),
    ("ffn1_w_up", (D_MODEL, D_FF // N_CHIPS)),
    ("ffn1_w_down", (D_FF // N_CHIPS, D_MODEL)),
    ("w_in", (D_MODEL, 1984 // N_CHIPS)),
    ("mla_w_q_b", (MLA_Q_RANK, MLA_QK)),
    ("mla_w_kv_b", (MLA_KV_RANK, MLA_NOPE + MLA_V)),
    ("w_out", (D_MODEL // N_CHIPS, D_MODEL)),
    ("ffn2_w_gate", (D_MODEL, D_FF // N_CHIPS)),
    ("ffn2_w_up", (D_MODEL, D_FF // N_CHIPS)),
    ("ffn2_w_down", (D_FF // N_CHIPS, D_MODEL)),
)
SMALL = (
    ("ffn1_norm", (1, 1024), 8), ("mix_norm", (1, 1024), 8), ("dil_q_norm", (1, 64), 1),
    ("dil_k_norm", (1, 64), 1), ("rel_bias", (8, 32), 2), ("mla_q_a_norm", (1, 256), 2),
    ("mla_kv_a_norm", (1, 128), 1), ("mla_q_norm", (1, 192), 2), ("mla_k_norm", (1, 192), 2),
    ("out_norm_dil", (1, 512), 4), ("out_norm_mla", (1, 512), 4), ("ffn2_norm", (1, 1024), 8),
)
SMALL_ROWS = 48
WEIGHTS = ("ffn1_norm", "ffn1_w_gate", "ffn1_w_up", "ffn1_w_down", "mix_norm", "w_in", "dil_q_norm",
           "dil_k_norm", "rel_bias", "mla_q_a_norm", "mla_w_q_b", "mla_kv_a_norm", "mla_w_kv_b",
           "mla_q_norm", "mla_k_norm", "out_norm_dil", "out_norm_mla", "w_out", "ffn2_norm",
           "ffn2_w_gate", "ffn2_w_up", "ffn2_w_down")


def _pcall(body, **kw):
    return pl.pallas_call(body, **kw)


def _cparams(*sem):
    return pltpu.CompilerParams(dimension_semantics=sem)


def _sds(shape, dtype):
    return jax.ShapeDtypeStruct(shape, dtype)


def _dot(a, b, dn):
    return lax.dot_general(a, b, dn, preferred_element_type=F32)


def _rms_fwd(x, g, out_dtype, name, tm):
    n, d = x.shape
    tm = min(tm, n)

    def body(x_ref, g_ref, o_ref):
        xf = x_ref[...].astype(F32)
        r = lax.rsqrt(jnp.mean(xf * xf, axis=-1, keepdims=True) + EPS)
        o_ref[...] = (xf * r * g_ref[...]).astype(o_ref.dtype)

    return _pcall(
        body, name=name, grid=(n // tm,),
        in_specs=[pl.BlockSpec((tm, d), lambda i: (i, 0)), pl.BlockSpec((1, d), lambda i: (0, 0))],
        out_specs=pl.BlockSpec((tm, d), lambda i: (i, 0)),
        out_shape=_sds((n, d), out_dtype), compiler_params=_cparams("parallel"))(x, g)


def _rms_bwd(dys, x, g, res, name, tm):
    n, d = x.shape
    tm = min(tm, n)
    nd = len(dys)
    has_res = res is not None

    def body(*refs):
        dy_refs = refs[:nd]
        x_ref, g_ref = refs[nd], refs[nd + 1]
        res_ref = refs[nd + 2] if has_res else None
        dx_ref, dg_ref = refs[-2], refs[-1]
        dy = dy_refs[0][...].astype(F32)
        for r_ in dy_refs[1:]:
            dy = dy + r_[...].astype(F32)
        xf = x_ref[...].astype(F32)
        r = lax.rsqrt(jnp.mean(xf * xf, axis=-1, keepdims=True) + EPS)
        xh = xf * r
        dxh = dy * g_ref[...]
        dx = r * (dxh - xh * jnp.mean(dxh * xh, axis=-1, keepdims=True))
        if has_res:
            dx = dx + res_ref[...]
        dx_ref[...] = dx

        @pl.when(pl.program_id(0) == 0)
        def _():
            dg_ref[...] = jnp.zeros_like(dg_ref)

        dg_ref[...] += jnp.sum(dy * xh, axis=0, keepdims=True)

    row = pl.BlockSpec((tm, d), lambda i: (i, 0))
    vec = pl.BlockSpec((1, d), lambda i: (0, 0))
    ins = list(dys) + [x, g] + ([res] if has_res else [])
    return _pcall(
        body, name=name, grid=(n // tm,),
        in_specs=[row] * nd + [row, vec] + ([row] if has_res else []),
        out_specs=(row, vec),
        out_shape=(_sds((n, d), F32), _sds((1, d), F32)),
        compiler_params=_cparams("arbitrary"))(*ins)


def _mm(name, grid, pairs, dn, out_shape, out_spec, acc_shape, res=None, scale=1.0, outgoing=(), norm=None):
    npairs = len(pairs)
    nred = grid[2]
    has_res = res is not None
    has_norm = norm is not None
    no = len(outgoing)

    def body(*refs):
        ab = refs[:2 * npairs]
        res_ref = refs[2 * npairs] if has_res else None
        nin = 2 * npairs + int(has_res) + 2 * int(has_norm)
        if has_norm:
            x_ref, g_ref = refs[nin - 2:nin]
        first_out = nin + no
        sent = refs[nin:first_out]
        o_ref = refs[first_out]
        nout = 1 + int(has_norm)
        dg_ref = refs[first_out + 1] if has_norm else None
        arrived = refs[first_out + nout:first_out + nout + no]
        acc_ref = refs[first_out + nout + no] if nred > 1 else None
        if no:
            send_sems, recv_sems = refs[-2:]
            ids = [pl.program_id(n) for n in range(3)]

            @pl.when((ids[0] == 0) & (ids[1] == 0) & (ids[2] == 0))
            def _():
                _scatter_start(sent, arrived, send_sems, recv_sems)

        tot = None
        for p in range(npairs):
            d = _dot(ab[2 * p][...].astype(BF16), ab[2 * p + 1][...].astype(BF16), dn)
            tot = d if tot is None else tot + d

        def finish(v):
            if scale != 1.0:
                v = v * scale
            if has_norm:
                xf = x_ref[...]
                r = lax.rsqrt(jnp.mean(xf * xf, axis=-1, keepdims=True) + EPS)
                xh = xf * r
                dxh = v * g_ref[...]

                @pl.when(pl.program_id(0) == 0)
                def _():
                    dg_ref[...] = jnp.zeros_like(dg_ref)

                dg_ref[...] += jnp.sum(v * xh, axis=0, keepdims=True)
                v = r * (dxh - xh * jnp.mean(dxh * xh, axis=-1, keepdims=True))
            if has_res:
                v = res_ref[...] + v
            o_ref[...] = v.astype(o_ref.dtype)

        if nred == 1:
            finish(tot)
        else:
            r = pl.program_id(2)

            @pl.when(r == 0)
            def _():
                acc_ref[...] = tot

            @pl.when(r > 0)
            def _():
                acc_ref[...] += tot

            @pl.when(r == nred - 1)
            def _():
                finish(acc_ref[...])

        if no:
            @pl.when((ids[0] == grid[0] - 1) & (ids[1] == grid[1] - 1) & (ids[2] == nred - 1))
            def _():
                _scatter_wait(sent, arrived, send_sems, recv_sems)

    ins, specs = [], []
    for a, a_spec, b, b_spec in pairs:
        ins += [a, b]
        specs += [a_spec, b_spec]
    if has_res:
        ins.append(res[0])
        specs.append(res[1])
    scratch = [pltpu.VMEM(acc_shape, F32)] if nred > 1 else []
    if not no and not has_norm:
        return _pcall(
            body, name=name, grid=grid, in_specs=specs, out_specs=out_spec, out_shape=out_shape,
            scratch_shapes=scratch, compiler_params=_cparams("parallel", "parallel", "arbitrary"))(*ins)
    out_specs, out_shapes = (out_spec,), (out_shape,)
    if has_norm:
        assert grid[1] == 1
        d = norm[1].shape[1]
        ins += [norm[0], norm[1]]
        specs += [out_spec, pl.BlockSpec((1, d), lambda i, j, r: (0, 0))]
        out_specs += (pl.BlockSpec((1, d), lambda i, j, r: (0, 0)),)
        out_shapes += (_sds((1, d), F32),)
    hbm = pl.BlockSpec(memory_space=pltpu.HBM)
    res_ = tuple(_pcall(
        body, name=name, grid=grid, in_specs=specs + [hbm] * no, out_specs=out_specs + (hbm,) * no,
        out_shape=out_shapes + _scatter_shapes(outgoing),
        scratch_shapes=scratch + (_scatter_sems(no) if no else []),
        compiler_params=_cparams("arbitrary", "arbitrary", "arbitrary"))(*ins, *outgoing))
    nout = len(out_shapes)
    return res_[:nout] + ((res_[nout:],) if no else ())


def _ffn_up(h, wg, wu, name, tm, incoming=()):
    t, d = h.shape
    nc, _, fs = wg.shape
    tm = min(tm, t)
    nt = t // tm
    ni = len(incoming)
    halves = _halves(incoming)

    def body(*refs):
        h_ref, wg_ref, wu_ref = refs[:3]
        srcs = refs[3:3 + ni]
        g_ref, u_ref, a_ref = refs[3 + ni:6 + ni]
        outs = refs[6 + ni:6 + 2 * ni]
        if ni:
            send_sems, recv_sems = refs[6 + 2 * ni:]
            c, i = pl.program_id(0), pl.program_id(1)

            @pl.when((c == 0) & (i == 0))
            def _():
                _gather_start(srcs, outs, halves, send_sems, recv_sems)

        hh = h_ref[...]
        gate = _dot(hh, wg_ref[...], NN)
        up = _dot(hh, wu_ref[...], NN)
        sig = jax.nn.sigmoid(gate)
        silu = gate * sig
        g_ref[...] = (up * (sig + silu * (1.0 - sig))).astype(BF16)
        u_ref[...] = silu.astype(BF16)
        a_ref[...] = (silu * up).astype(BF16)

        if ni:
            @pl.when((c == nc - 1) & (i == nt - 1))
            def _():
                _gather_wait(outs, halves, send_sems, recv_sems)

    wspec = pl.BlockSpec((None, d, fs), lambda c, i: (c, 0, 0))
    ospec = pl.BlockSpec((None, tm, fs), lambda c, i: (c, i, 0))
    hbm = pl.BlockSpec(memory_space=pltpu.HBM)
    osd = _sds((nc, t, fs), BF16)
    res = tuple(_pcall(
        body, name=name, grid=(nc, nt),
        in_specs=[pl.BlockSpec((tm, d), lambda c, i: (i, 0)), wspec, wspec] + [hbm] * ni,
        out_specs=(ospec, ospec, ospec) + (hbm,) * ni,
        out_shape=(osd, osd, osd) + tuple(_sds((N_CHIPS,) + b.shape, b.dtype) for b in incoming),
        scratch_shapes=[pltpu.SemaphoreType.DMA((3 * ni,)), pltpu.SemaphoreType.DMA((3 * ni,))] if ni else [],
        compiler_params=_cparams("arbitrary", "arbitrary"))(h, wg, wu, *incoming))
    return res[:3] + (res[3:],)


def _ffn_hidden_bwd(dy, h, wd, dact_dgate, dact_dup, act, name, tm, outgoing=()):
    t, d = dy.shape
    nc, fs, _ = wd.shape
    tm = min(tm, t)
    nt = t // tm
    no = len(outgoing)

    def body(*refs):
        dy_ref, h_ref, wd_ref, g_ref, u_ref, a_ref = refs[:6]
        sent = refs[6:6 + no]
        dg_ref, du_ref, dwg_hbm, dwu_hbm, dwd_hbm = refs[6 + no:11 + no]
        arrived = refs[11 + no:11 + 2 * no]
        wg_acc, wu_acc, wd_acc, sem = refs[11 + 2 * no:15 + 2 * no]
        c, i = pl.program_id(0), pl.program_id(1)
        if no:
            send_sems, recv_sems = refs[15 + 2 * no:]

            @pl.when((c == 0) & (i == 0))
            def _():
                _scatter_start(sent, arrived, send_sems, recv_sems)

        dyb = dy_ref[...].astype(BF16)
        da = _dot(dyb, wd_ref[...], NT) * FFN_RESID
        dgate = (da * g_ref[...].astype(F32)).astype(BF16)
        dup = (da * u_ref[...].astype(F32)).astype(BF16)
        dg_ref[...] = dgate
        du_ref[...] = dup
        hh = h_ref[...]
        parts = (_dot(hh, dgate, TN), _dot(hh, dup, TN), _dot(a_ref[...], dyb, TN) * FFN_RESID)
        accs = (wg_acc, wu_acc, wd_acc)

        @pl.when(i == 0)
        def _():
            for acc, part in zip(accs, parts):
                acc[...] = part

        @pl.when(i > 0)
        def _():
            for acc, part in zip(accs, parts):
                acc[...] += part

        @pl.when(i == nt - 1)
        def _():
            copies = [pltpu.make_async_copy(acc, out.at[c], sem.at[n])
                      for n, (acc, out) in enumerate(zip(accs, (dwg_hbm, dwu_hbm, dwd_hbm)))]
            for cp in copies:
                cp.start()
            for cp in copies:
                cp.wait()

        if no:
            @pl.when((c == nc - 1) & (i == nt - 1))
            def _():
                _scatter_wait(sent, arrived, send_sems, recv_sems)

    tok = pl.BlockSpec((tm, d), lambda c, i: (i, 0))
    cspec = pl.BlockSpec((None, tm, fs), lambda c, i: (c, i, 0))
    hbm = pl.BlockSpec(memory_space=pltpu.HBM)
    osd = _sds((nc, t, fs), BF16)
    res = _pcall(
        body, name=name, grid=(nc, nt),
        in_specs=[tok, tok, pl.BlockSpec((None, fs, d), lambda c, i: (c, 0, 0)), cspec, cspec, cspec] + [hbm] * no,
        out_specs=(cspec, cspec, hbm, hbm, hbm) + (hbm,) * no,
        out_shape=(osd, osd, _sds((nc, d, fs), F32), _sds((nc, d, fs), F32), _sds((nc, fs, d), F32))
        + _scatter_shapes(outgoing),
        scratch_shapes=[pltpu.VMEM((d, fs), F32), pltpu.VMEM((d, fs), F32), pltpu.VMEM((fs, d), F32),
                        pltpu.SemaphoreType.DMA((3,))] + (_scatter_sems(no) if no else []),
        compiler_params=_cparams("arbitrary", "arbitrary"))(dy, h, wd, dact_dgate, dact_dup, act, *outgoing)
    res = tuple(res)
    return res[:5] + (res[5:],)


def _ffn_fwd(x, g, wg, wu, wd, tag, incoming=()):
    t = x.shape[0]
    nc, _, fs = wg.shape
    tm = min(512, t)
    h = _rms_fwd(x, g, BF16, f"{tag}_norm", 512)
    dact_dgate, dact_dup, act, partly = _ffn_up(h, wg, wu, f"{tag}_up", 1024, incoming)
    pairs = [(act, pl.BlockSpec((None, tm, fs), lambda i, j, r, c=c: (c, i, 0)),
              wd, pl.BlockSpec((None, fs, D_MODEL), lambda i, j, r, c=c: (c, 0, 0))) for c in range(nc)]
    row = pl.BlockSpec((tm, D_MODEL), lambda i, j, r: (i, 0))
    y = _mm(f"{tag}_down", (t // tm, 1, 1), pairs, NN, _sds((t, D_MODEL), F32), row, (tm, D_MODEL),
            res=(x, row), scale=FFN_RESID)
    return y, (h, dact_dgate, dact_dup, act), partly


def _ffn_bwd(dy, x, g, wg, wu, wd, saved, tag, outgoing=(), own_exchange=None):
    h, dact_dgate, dact_dup, act = saved
    t = x.shape[0]
    nc, _, fs = wg.shape
    tm = min(512, t)
    dgate, dup, dwg, dwu, dwd, arrived = _ffn_hidden_bwd(dy, h, wd, dact_dgate, dact_dup, act,
                                                         f"{tag}_hidden_bwd", 1024, outgoing)
    pairs = []
    for c in range(nc):
        a_spec = pl.BlockSpec((None, tm, fs), lambda i, j, r, c=c: (c, i, 0))
        w_spec = pl.BlockSpec((None, D_MODEL, fs), lambda i, j, r, c=c: (c, 0, 0))
        pairs += [(dgate, a_spec, wg, w_spec), (dup, a_spec, wu, w_spec)]
    own_part = tuple(own_exchange([dwg, dwu, dwd])) if own_exchange else ()
    row = pl.BlockSpec((tm, D_MODEL), lambda i, j, r: (i, 0))
    res = _mm(f"{tag}_dh", (t // tm, 1, 1), pairs, NT, _sds((t, D_MODEL), F32), row, (tm, D_MODEL),
              res=(dy, row), norm=(x, g), outgoing=own_part)
    dx, dg = res[0], res[1]
    own_got = res[2] if own_part else ()
    return dx, dg, dwg, dwu, dwd, arrived, (own_part, own_got)


def _mm_simple(name, a, b, dn, out_dtype, tm=512, tk=512, res=None, scale=1.0):
    if dn == TN:
        k, m = a.shape
        n = b.shape[1]
        tk = min(tk, k)
        return _mm(name, (1, 1, k // tk),
                   [(a, pl.BlockSpec((tk, m), lambda i, j, r: (r, 0)), b, pl.BlockSpec((tk, n), lambda i, j, r: (r, 0)))],
                   TN, _sds((m, n), out_dtype), pl.BlockSpec((m, n), lambda i, j, r: (0, 0)), (m, n), scale=scale)
    m, k = a.shape
    n = b.shape[1] if dn == NN else b.shape[0]
    tm = min(tm, m)
    row = pl.BlockSpec((tm, n), lambda i, j, r: (i, 0))
    return _mm(name, (m // tm, 1, 1),
               [(a, pl.BlockSpec((tm, k), lambda i, j, r: (i, 0)), b, pl.BlockSpec(b.shape, lambda i, j, r: (0, 0)))],
               dn, _sds((m, n), out_dtype), row, (tm, n), res=None if res is None else (res, row), scale=scale)


def _t5_bucket(dist):
    max_exact = REL_BUCKETS // 2
    d = np.maximum(dist, 1).astype(np.float32)
    large = max_exact + (np.log(d / max_exact) / np.log(REL_MAX_DIST / max_exact)
                         * (REL_BUCKETS - max_exact)).astype(np.int32)
    large = np.minimum(large, REL_BUCKETS - 1)
    return np.where(dist < max_exact, dist, large).astype(np.int32)


def _bucket_tiles():
    i = np.arange(QB)[:, None]
    j = np.arange(QB + DIL_W)[None, :]
    delta = np.clip(i + DIL_W - j, 0, None)
    return np.stack([_t5_bucket(delta * dil) for dil in DIL_DILATIONS]).astype(np.int32)


def _bias_tiles(rel_bias):
    buckets = jnp.asarray(_bucket_tiles())

    def body(rb_ref, bk_ref, o_ref):
        bk = bk_ref[...]
        for h in range(DIL_HEADS):
            def pick(b, tile):
                return jnp.where(bk == b, rb_ref[h, b], tile)

            o_ref[h] = lax.fori_loop(0, REL_BUCKETS, pick, jnp.zeros((QB, QB + DIL_W), F32))

    return _pcall(
        body, name="dil_bias_tiles", grid=(3,),
        in_specs=[pl.BlockSpec(memory_space=pltpu.SMEM),
                  pl.BlockSpec((None, QB, QB + DIL_W), lambda b: (b, 0, 0))],
        out_specs=pl.BlockSpec((None, DIL_HEADS, QB, QB + DIL_W), lambda b: (b, 0, 0, 0)),
        out_shape=_sds((3, DIL_HEADS, QB, QB + DIL_W), F32),
        compiler_params=_cparams("parallel"))(rel_bias, buckets)


def _bias_grad(dtiles):
    buckets = jnp.asarray(_bucket_tiles())

    def body(dt_ref, bk_ref, o_ref):
        def one(b, carry):
            hit = [bk_ref[br] == b for br in range(3)]
            for h in range(DIL_HEADS):
                tot = jnp.zeros((), F32)
                for br in range(3):
                    tot = tot + jnp.sum(jnp.where(hit[br], dt_ref[br, h], 0.0))
                o_ref[h, b] = tot
            return carry

        lax.fori_loop(0, REL_BUCKETS, one, 0)

    return _pcall(
        body, name="dil_bias_grad",
        in_specs=[pl.BlockSpec(memory_space=pltpu.VMEM), pl.BlockSpec(memory_space=pltpu.VMEM)],
        out_specs=pl.BlockSpec(memory_space=pltpu.SMEM),
        out_shape=_sds((DIL_HEADS, REL_BUCKETS), F32))(dtiles, buckets)


def _split_heads(a, lo):
    zero = jnp.zeros_like(a)
    return jnp.concatenate([jnp.where(lo, a, zero), jnp.where(lo, zero, a)], axis=0)


def _side_by_side(a):
    n = a.shape[0] // 2
    return jnp.concatenate([a[:n], a[n:]], axis=1)


def _band_masks(prev_ok):
    ii = lax.broadcasted_iota(jnp.int32, (2 * QB, QB), 0) & (QB - 1)
    jj = lax.broadcasted_iota(jnp.int32, (2 * QB, QB), 1)
    return jj <= ii, jj >= ii + jnp.where(prev_ok, 0, QB)


def _dil_fwd(q, k, v, bias, dil, name):
    w = DIL_WIDTH
    t = q.shape[0] * dil
    npair = w // LANES
    nl = t // dil // QB
    scale = DIL_HD ** -0.5

    def body(q_ref, kc_ref, kp_ref, vc_ref, vp_ref, b_ref, o_ref, lse_ref):
        nn = pl.program_id(1)
        lo = lax.broadcasted_iota(jnp.int32, (QB, LANES), 1) < DIL_HD
        lo2 = lax.broadcasted_iota(jnp.int32, (2 * QB, LANES), 1) < DIL_HD
        ii = lax.broadcasted_iota(jnp.int32, (2 * QB, 2 * QB), 0) & (QB - 1)
        jj = lax.broadcasted_iota(jnp.int32, (2 * QB, 2 * QB), 1)
        first_key = jnp.maximum(ii, jnp.where(nn != 0, 0, QB))
        valid = (jj >= first_key) & (jj <= ii + QB)
        for p in range(npair):
            cols = slice(p * LANES, (p + 1) * LANES)
            qq = _split_heads(q_ref[:, cols], lo)
            kk = jnp.concatenate([kp_ref[:, cols], kc_ref[:, cols]], axis=0)
            vv = jnp.concatenate([vp_ref[:, cols], vc_ref[:, cols]], axis=0)
            s = jnp.where(valid, _dot(qq, kk, NT) * scale + b_ref[p], NEG)
            m = jnp.max(s, axis=-1, keepdims=True)
            e = jnp.exp(s - m)
            den = jnp.sum(e, axis=-1, keepdims=True)
            pn = (e * (1.0 / den)).astype(BF16)
            o_ref[:, cols] = _dot(_side_by_side(pn), _split_heads(vv, lo2), NN)
            lse = m + jnp.log(den)
            lse_ref[:, cols] = jnp.where(lo, lse[:QB], lse[QB:])

    cur = pl.BlockSpec((QB, w), lambda r, n: (n, r))
    prev = pl.BlockSpec((QB, w), lambda r, n: (jnp.maximum(n - 1, 0), r))
    sd = _sds((t // dil, dil * w), F32)
    return _pcall(
        body, name=name, grid=(dil, nl),
        in_specs=[cur, cur, prev, cur, prev, pl.BlockSpec((npair, 2 * QB, 2 * QB), lambda r, n: (0, 0, 0))],
        out_specs=(cur, cur), out_shape=(sd, sd),
        compiler_params=_cparams("parallel", "parallel"))(q, k, k, v, v, bias)


def _dil_bwd(q, k, v, do, stats, bias, dil, name):
    w = DIL_WIDTH
    t = q.shape[0] * dil
    npair = w // LANES
    nl = t // dil // QB
    scale = DIL_HD ** -0.5

    def body(qc_ref, qn_ref, doc_ref, don_ref, sc_ref, sn_ref, k_ref, v_ref, b_ref,
             dq_ref, dk_ref, dv_ref, db_ref, carry):
        r, nn = pl.program_id(0), pl.program_id(1)
        lo = lax.broadcasted_iota(jnp.int32, (QB, LANES), 1) < DIL_HD
        cur_ok, prev_ok = _band_masks(nn + 1 < nl)

        @pl.when((r == 0) & (nn == 0))
        def _():
            db_ref[...] = jnp.zeros_like(db_ref)
            carry[...] = jnp.zeros_like(carry)

        for p in range(npair):
            cols = slice(p * LANES, (p + 1) * LANES)
            kp, vp = k_ref[:, cols], v_ref[:, cols]
            k2 = _split_heads(kp, lo)

            def column(ref, lane):
                first = p * LANES + lane
                return jnp.concatenate([ref[:, first:first + 1], ref[:, first + DIL_HD:first + DIL_HD + 1]], axis=0)

            def side(q_ref, do_ref, s_ref, bias, ok):
                qq = _split_heads(q_ref[:, cols], lo)
                dd = _split_heads(do_ref[:, cols], lo)
                s = jnp.where(ok, _dot(qq, kp, NT) * scale + bias, NEG)
                prob = jnp.exp(s - column(s_ref, 0))
                ds = prob * (_dot(dd, vp, NT) - column(s_ref, DIL_HD // 2))
                return qq, dd, prob.astype(BF16), ds

            q1, d1, p1, ds1 = side(qc_ref, doc_ref, sc_ref, b_ref[p, :, QB:], cur_ok)
            q2, d2, p2, ds2 = side(qn_ref, don_ref, sn_ref, b_ref[p, :, :QB], prev_ok)
            ds1b, ds2b = ds1.astype(BF16), ds2.astype(BF16)
            dq_ref[:, cols] = carry[:, cols] + _dot(_side_by_side(ds1b), k2, NN) * scale
            carry[:, cols] = _dot(_side_by_side(ds2b), k2, NN) * scale
            dk_ref[:, cols] = _dot(jnp.concatenate([ds1b, ds2b], axis=0), jnp.concatenate([q1, q2], axis=0), TN) * scale
            dv_ref[:, cols] = _dot(jnp.concatenate([p1, p2], axis=0), jnp.concatenate([d1, d2], axis=0), TN)
            db_ref[p, :, QB:] += ds1
            db_ref[p, :, :QB] += ds2

    cur = pl.BlockSpec((QB, w), lambda r, n: (n, r))
    nxt = pl.BlockSpec((QB, w), lambda r, n: (jnp.minimum(n + 1, nl - 1), r))
    tile = pl.BlockSpec((npair, 2 * QB, 2 * QB), lambda r, n: (0, 0, 0))
    sd = _sds((t // dil, dil * w), F32)
    return _pcall(
        body, name=name, grid=(dil, nl),
        in_specs=[cur, nxt, cur, nxt, cur, nxt, cur, cur, tile],
        out_specs=(cur, cur, cur, tile),
        out_shape=(sd, sd, sd, _sds((npair, 2 * QB, 2 * QB), F32)),
        scratch_shapes=[pltpu.VMEM((QB, w), F32)],
        compiler_params=_cparams("arbitrary", "arbitrary"))(q, q, do, do, stats, stats, k, v, bias)


def _head_sum_matrix(scale):
    idx = np.arange(DIL_WIDTH) // DIL_HD
    return jnp.asarray((idx[:, None] == idx[None, :]).astype(np.float32) * scale, BF16)


def _head_sum(x, mat):
    hi = x.astype(BF16)
    lo = (x - hi.astype(F32)).astype(BF16)
    return _dot(hi, mat, NN) + _dot(lo, mat, NN)


def _to_views(src, tmp, out_refs):
    tm, w = src.shape
    for j in range(w // LANES):
        tmp[j] = src[:, j * LANES:(j + 1) * LANES]
    for d, o_ref in zip(DIL_DILATIONS, out_refs):
        if d == 1:
            o_ref[...] = src.astype(o_ref.dtype)
            continue
        for r in range(d):
            for j in range(w // LANES):
                lo = r * w + j * LANES
                o_ref[:, lo:lo + LANES] = tmp[j, pl.ds(r, tm // d, stride=d), :].astype(o_ref.dtype)


def _from_view(v_ref, tmp, d):
    tm = tmp.shape[1]
    w = v_ref.shape[1] // d
    for r in range(d):
        for j in range(w // LANES):
            lo = r * w + j * LANES
            tmp[j, pl.ds(r, tm // d, stride=d), :] = v_ref[:, lo:lo + LANES]
    return jnp.concatenate([tmp[j] for j in range(w // LANES)], axis=1)


def _view_specs(tm, t, dtype):
    specs = tuple(pl.BlockSpec((tm // d, d * DIL_WIDTH), lambda i: (i, 0)) for d in DIL_DILATIONS)
    shapes = tuple(_sds((t // d, d * DIL_WIDTH), dtype) for d in DIL_DILATIONS)
    return specs, shapes


def _view_scratch(tm):
    return pltpu.VMEM((DIL_WIDTH // LANES, tm, LANES), F32)


def _dil_merge(outs, lses, g, tm):
    w = DIL_WIDTH
    t = outs[0].shape[0]
    tm = min(tm, t)

    def body(o0, o1, o2, l0, l1, l2, g_ref, o_ref, l_ref, n_ref, so1, so2, sl1, sl2):
        d1, d2 = DIL_DILATIONS[1], DIL_DILATIONS[2]
        a0, a1, a2 = l0[...], _from_view(l1, sl1, d1), _from_view(l2, sl2, d2)
        m = jnp.maximum(jnp.maximum(a0, a1), a2)
        e0, e1, e2 = jnp.exp(a0 - m), jnp.exp(a1 - m), jnp.exp(a2 - m)
        den = e0 + e1 + e2
        o = (e0 * o0[...] + e1 * _from_view(o1, so1, d1) + e2 * _from_view(o2, so2, d2)) / den
        o_ref[...] = o
        l_ref[...] = m + jnp.log(den)
        r = lax.rsqrt(jnp.mean(o * o, axis=-1, keepdims=True) + EPS)
        n_ref[...] = (o * r * g_ref[...]).astype(n_ref.dtype)

    specs, _ = _view_specs(tm, t, F32)
    spec = pl.BlockSpec((tm, w), lambda i: (i, 0))
    return _pcall(
        body, name="dil_merge", grid=(t // tm,),
        in_specs=list(specs) * 2 + [pl.BlockSpec((1, w), lambda i: (0, 0))], out_specs=(spec, spec, spec),
        out_shape=(_sds((t, w), F32), _sds((t, w), F32), _sds((t, w), BF16)),
        scratch_shapes=[_view_scratch(tm)] * 4,
        compiler_params=_cparams("parallel"))(*outs, *lses, g)


def _dil_stats(do, o, lse, tm):
    t, w = do.shape
    tm = min(tm, t)

    def body(a_ref, b_ref, l_ref, m_ref, s1, s4, s16, d1, d4, d16, tmp):
        first = (lax.broadcasted_iota(jnp.int32, (tm, w), 1) & (DIL_HD - 1)) < DIL_HD // 2
        do_ = a_ref[...]
        _to_views(jnp.where(first, l_ref[...], _head_sum(do_ * b_ref[...], m_ref[...])), tmp, (s1, s4, s16))
        _to_views(do_, tmp, (d1, d4, d16))

    spec = pl.BlockSpec((tm, w), lambda i: (i, 0))
    f_specs, f_shapes = _view_specs(tm, t, F32)
    b_specs, b_shapes = _view_specs(tm, t, BF16)
    res = _pcall(body, name="dil_stats", grid=(t // tm,),
                 in_specs=[spec, spec, spec, pl.BlockSpec((w, w), lambda i: (0, 0))],
                 out_specs=f_specs + b_specs, out_shape=f_shapes + b_shapes,
                 scratch_shapes=[_view_scratch(tm)],
                 compiler_params=_cparams("parallel"))(do, o, lse, _head_sum_matrix(1.0))
    return res[:3], res[3:]


def _head_norm_fwd(x, col, g, name, tm):
    t = x.shape[0]
    w = DIL_WIDTH
    tm = min(tm, t)
    normed = g is not None

    def body(*refs):
        outs, tmp = refs[-4:-1], refs[-1]
        xf = refs[0][...]
        if normed:
            g_ref, m_ref = refs[1], refs[2]
            xf = xf * lax.rsqrt(_head_sum(xf * xf, m_ref[...]) + EPS) * g_ref[...]
        _to_views(xf, tmp, outs)

    specs, shapes = _view_specs(tm, t, BF16)
    extra = [g, _head_sum_matrix(1.0 / DIL_HD)] if normed else []
    extra_specs = [pl.BlockSpec((1, w), lambda i: (0, 0)), pl.BlockSpec((w, w), lambda i: (0, 0))] if normed else []
    return _pcall(
        body, name=name, grid=(t // tm,),
        in_specs=[pl.BlockSpec((tm, w), lambda i: (i, col))] + extra_specs,
        out_specs=specs, out_shape=shapes, scratch_shapes=[_view_scratch(tm)],
        compiler_params=_cparams("parallel"))(x, *extra)


def _head_norm_bwd(dys, x, col, g, name, tm):
    t = x.shape[0]
    w = DIL_WIDTH
    tm = min(tm, t)
    nd = len(dys)
    nt = t // tm
    lane = np.arange(w) % DIL_HD
    fold = jnp.asarray((lane[:, None] == lane[None, :]).astype(np.float32))

    def body(*refs):
        x_ref, g_ref, m_ref, f_ref = refs[nd:nd + 4]
        dx_ref, dg_ref, s1, s2 = refs[-4:]
        dy = refs[0][...] + _from_view(refs[1], s1, DIL_DILATIONS[1]) + _from_view(refs[2], s2, DIL_DILATIONS[2])
        xf = x_ref[...]
        mat = m_ref[...]
        r = lax.rsqrt(_head_sum(xf * xf, mat) + EPS)
        xh = xf * r
        dxh = dy * g_ref[...]
        dx_ref[...] = r * (dxh - xh * _head_sum(dxh * xh, mat))

        @pl.when(pl.program_id(0) == 0)
        def _():
            dg_ref[...] = jnp.zeros_like(dg_ref)

        dg_ref[...] += jnp.sum(dy * xh, axis=0, keepdims=True)

        @pl.when(pl.program_id(0) == nt - 1)
        def _():
            per_lane = jnp.broadcast_to(dg_ref[...], (8, w))
            dg_ref[...] = lax.dot_general(per_lane, f_ref[...], NN, precision=lax.Precision.HIGHEST,
                                          preferred_element_type=F32)[0:1]

    row = pl.BlockSpec((tm, w), lambda i: (i, 0))
    vec = pl.BlockSpec((1, w), lambda i: (0, 0))
    sq = pl.BlockSpec((w, w), lambda i: (0, 0))
    views, _ = _view_specs(tm, t, F32)
    return _pcall(
        body, name=name, grid=(nt,),
        in_specs=list(views) + [pl.BlockSpec((tm, w), lambda i: (i, col)), vec, sq, sq],
        out_specs=(row, vec), out_shape=(_sds((t, w), F32), _sds((1, w), F32)),
        scratch_shapes=[_view_scratch(tm)] * 2,
        compiler_params=_cparams("arbitrary"))(*dys, x, g, _head_sum_matrix(1.0 / DIL_HD), fold)


def _rowdot(a, b, name, tm):
    n, d = a.shape
    tm = min(tm, n)

    def body(a_ref, b_ref, o_ref):
        o_ref[...] = jnp.sum(a_ref[...].astype(F32) * b_ref[...].astype(F32), axis=-1, keepdims=True)

    spec = pl.BlockSpec((tm, d), lambda i: (i, 0))
    return _pcall(body, name=name, grid=(n // tm,), in_specs=[spec, spec],
                  out_specs=pl.BlockSpec((tm, 1), lambda i: (i, 0)), out_shape=_sds((n, 1), F32),
                  compiler_params=_cparams("parallel"))(a, b)


def _sum_branches(parts, name, tm):
    t = parts[0].shape[0]
    w = DIL_WIDTH
    tm = min(tm, t)

    def body(a_ref, b_ref, c_ref, o_ref, s1, s2):
        o_ref[...] = a_ref[...] + _from_view(b_ref, s1, DIL_DILATIONS[1]) + _from_view(c_ref, s2, DIL_DILATIONS[2])

    views, _ = _view_specs(tm, t, F32)
    return _pcall(body, name=name, grid=(t // tm,), in_specs=list(views),
                  out_specs=pl.BlockSpec((tm, w), lambda i: (i, 0)), out_shape=_sds((t, w), F32),
                  scratch_shapes=[_view_scratch(tm)] * 2,
                  compiler_params=_cparams("parallel"))(*parts)


def _rope_tables(t):
    inv = ROPE_BASE ** (-np.arange(0, MLA_ROPE, 2, dtype=np.float64) / MLA_ROPE)
    ang = np.arange(t, dtype=np.float64)[:, None] * inv[None, :]
    cos, sin = np.cos(ang), np.sin(ang)
    return (jnp.asarray(np.concatenate([cos, cos], 1), F32), jnp.asarray(np.concatenate([-sin, sin], 1), F32))


def _swap_halves(a):
    half = MLA_ROPE // 2
    return jnp.concatenate([a[:, half:], a[:, :half]], axis=1)


def _qk_parts(x, pe, tm, nt):
    if pe is None:
        return None
    return (pl.BlockSpec((tm, MLA_NOPE), lambda i: (i, 0)), pl.BlockSpec((tm, MLA_ROPE), lambda i: (i % nt, 0)))


def _mla_qk_fwd(x, g, cos_t, sin_t, scale, name, tm, pe=None):
    n = x.shape[0]
    d = MLA_QK
    t = cos_t.shape[0]
    tm = min(tm, t)
    nt = t // tm
    split = _qk_parts(x, pe, tm, nt)

    def body(*refs):
        if split:
            xn_ref, xr_ref, xv_ref, g_ref, c_ref, s_ref, o_ref, v_ref = refs
            xn, xr = xn_ref[...], xr_ref[...]
            v_ref[...] = xv_ref[...].astype(v_ref.dtype)
        else:
            x_ref, g_ref, c_ref, s_ref, o_ref = refs
            xf = x_ref[...]
            xn, xr = xf[:, :MLA_NOPE], xf[:, MLA_NOPE:]
        ms = (jnp.sum(xn * xn, axis=-1, keepdims=True) + jnp.sum(xr * xr, axis=-1, keepdims=True)) * (1.0 / d)
        r = lax.rsqrt(ms + EPS)
        gg = g_ref[...]
        yn = xn * r * gg[:, :MLA_NOPE]
        yr = xr * r * gg[:, MLA_NOPE:]
        o_ref[:, :MLA_NOPE] = (yn * scale).astype(o_ref.dtype)
        o_ref[:, MLA_NOPE:] = ((yr * c_ref[...] + _swap_halves(yr) * s_ref[...]) * scale).astype(o_ref.dtype)

    row = pl.BlockSpec((tm, d), lambda i: (i, 0))
    vec = pl.BlockSpec((1, d), lambda i: (0, 0))
    tab = pl.BlockSpec((tm, MLA_ROPE), lambda i: (i % nt, 0))
    if not split:
        return _pcall(body, name=name, grid=(n // tm,), in_specs=[row, vec, tab, tab],
                      out_specs=row, out_shape=_sds((n, d), BF16),
                      compiler_params=_cparams("parallel"))(x, g, cos_t, sin_t)
    vals = pl.BlockSpec((tm, MLA_V), lambda i: (i, 1))
    return _pcall(body, name=name, grid=(n // tm,), in_specs=[split[0], split[1], vals, vec, tab, tab],
                  out_specs=(row, pl.BlockSpec((tm, MLA_V), lambda i: (i, 0))),
                  out_shape=(_sds((n, d), BF16), _sds((n, MLA_V), BF16)),
                  compiler_params=_cparams("parallel"))(x, pe, x, g, cos_t, sin_t)


def _mla_qk_bwd(dy, x, g, cos_t, sin_t, scale, name, tm, pe=None):
    n = x.shape[0]
    d = MLA_QK
    t = cos_t.shape[0]
    tm = min(tm, t)
    nt = t // tm
    split = _qk_parts(x, pe, tm, nt)

    def body(*refs):
        if split:
            dy_ref, xn_ref, xr_ref, g_ref, c_ref, s_ref, dxn_ref, dxr_ref, dg_ref = refs
            xn, xr = xn_ref[...], xr_ref[...]
        else:
            dy_ref, x_ref, g_ref, c_ref, s_ref, dx_ref, dg_ref = refs
            xf = x_ref[...]
            xn, xr = xf[:, :MLA_NOPE], xf[:, MLA_NOPE:]
        gg = g_ref[...]
        ms = (jnp.sum(xn * xn, axis=-1, keepdims=True) + jnp.sum(xr * xr, axis=-1, keepdims=True)) * (1.0 / d)
        r = lax.rsqrt(ms + EPS)
        xh_n, xh_r = xn * r, xr * r
        dyf = dy_ref[...] * scale
        dyr = dyf[:, MLA_NOPE:]
        dn_n = dyf[:, :MLA_NOPE]
        dn_r = dyr * c_ref[...] + _swap_halves(dyr * s_ref[...])
        dxh_n = dn_n * gg[:, :MLA_NOPE]
        dxh_r = dn_r * gg[:, MLA_NOPE:]
        mean = (jnp.sum(dxh_n * xh_n, axis=-1, keepdims=True)
                + jnp.sum(dxh_r * xh_r, axis=-1, keepdims=True)) * (1.0 / d)
        dx_n = r * (dxh_n - xh_n * mean)
        dx_r = r * (dxh_r - xh_r * mean)
        if split:
            dxn_ref[...] = dx_n
            dxr_ref[...] = dx_r
        else:
            dx_ref[:, :MLA_NOPE] = dx_n
            dx_ref[:, MLA_NOPE:] = dx_r

        @pl.when(pl.program_id(0) == 0)
        def _():
            dg_ref[...] = jnp.zeros_like(dg_ref)

        dg_ref[:, :MLA_NOPE] += jnp.sum(dn_n * xh_n, axis=0, keepdims=True)
        dg_ref[:, MLA_NOPE:] += jnp.sum(dn_r * xh_r, axis=0, keepdims=True)

    row = pl.BlockSpec((tm, d), lambda i: (i, 0))
    vec = pl.BlockSpec((1, d), lambda i: (0, 0))
    tab = pl.BlockSpec((tm, MLA_ROPE), lambda i: (i % nt, 0))
    if not split:
        return _pcall(body, name=name, grid=(n // tm,), in_specs=[row, row, vec, tab, tab],
                      out_specs=(row, vec), out_shape=(_sds((n, d), F32), _sds((1, d), F32)),
                      compiler_params=_cparams("arbitrary"))(dy, x, g, cos_t, sin_t)
    outs = (pl.BlockSpec((tm, MLA_NOPE), lambda i: (i, 0)), pl.BlockSpec((tm, MLA_ROPE), lambda i: (i, 0)), vec)
    return _pcall(body, name=name, grid=(n // tm,), in_specs=[row, split[0], split[1], vec, tab, tab],
                  out_specs=outs, out_shape=(_sds((n, MLA_NOPE), F32), _sds((n, MLA_ROPE), F32), _sds((1, d), F32)),
                  compiler_params=_cparams("arbitrary"))(dy, x, pe, g, cos_t, sin_t)


def _causal_mask(i, j, tq, tk, width):
    row = i * tq + lax.broadcasted_iota(jnp.int32, (tq, width), 0)
    col = j * tk + lax.broadcasted_iota(jnp.int32, (tq, width), 1)
    return col <= row


def _causal_steps(nq, nk, tq, tk, q_major):
    if q_major:
        groups = [[(i, j) for j in range((i * tq + tq - 1) // tk + 1)] for i in range(nq)]
        nunit = tk // tq if tk % tq == 0 else 1
    else:
        groups = [[(i, j) for i in range((j * tk) // tq, nq)] for j in range(nk)]
        nunit = tq // tk if tq % tk == 0 else 1
    it, jt, fl = [], [], []
    for g in groups:
        for n, (i, j) in enumerate(g):
            crossing = j * tk + tk - 1 > i * tq
            if q_major:
                unit = tk // nunit
                u = min(nunit, -(-(i * tq + tq - j * tk) // unit)) - 1
            else:
                unit = tq // nunit
                u = max(0, j * tk - i * tq) // unit
            it.append(i)
            jt.append(j)
            fl.append((n == 0) + 2 * (n == len(g) - 1) + 4 * crossing + 8 * (u if crossing else 0))
    return tuple(jnp.asarray(np.array(a, np.int32)) for a in (it, jt, fl)), nunit


def _by_crossing(flags, nunit, update):
    pl.when((flags & 4) == 0)(functools.partial(update, None))
    for u in range(nunit):
        pl.when(((flags & 4) != 0) & ((flags >> 3) == u))(functools.partial(update, u))


def _causal_specs(tq, tk):
    def qs(w):
        return pl.BlockSpec((None, tq, w), lambda h, s, it, jt, fl: (h, it[s], 0))

    def kv(w):
        return pl.BlockSpec((None, tk, w), lambda h, s, it, jt, fl: (h, jt[s], 0))

    return qs, kv


def _mla_fwd(q, k, v, tq, tk):
    nh, t, dq = q.shape
    dv = v.shape[2]
    tq, tk = min(tq, t), min(tk, t)
    tables, nunit = _causal_steps(t // tq, t // tk, tq, tk, True)

    def body(it, jt, fl, q_ref, k_ref, v_ref, o_ref, lse_ref, m_sc, l_sc, acc_sc):
        step = pl.program_id(1)
        i, j, flags = it[step], jt[step], fl[step]

        @pl.when((flags & 1) != 0)
        def _():
            m_sc[...] = jnp.full_like(m_sc, NEG)
            l_sc[...] = jnp.zeros_like(l_sc)
            acc_sc[...] = jnp.zeros_like(acc_sc)

        def update(units):
            wk = tk if units is None else (units + 1) * (tk // nunit)
            s = _dot(q_ref[...], k_ref[:wk, :], NT)
            if units is not None:
                s = jnp.where(_causal_mask(i, j, tq, tk, wk), s, NEG)
            m_prev = m_sc[...]
            m_new = jnp.maximum(m_prev, jnp.max(s, axis=-1, keepdims=True))
            alpha = jnp.exp(m_prev - m_new)
            p = jnp.exp(s - m_new)
            l_sc[...] = alpha * l_sc[...] + jnp.sum(p, axis=-1, keepdims=True)
            acc_sc[...] = alpha * acc_sc[...] + _dot(p.astype(BF16), v_ref[:wk, :], NN)
            m_sc[...] = m_new

        _by_crossing(flags, nunit, update)

        @pl.when((flags & 2) != 0)
        def _():
            o_ref[...] = acc_sc[...] / l_sc[...]
            lse_ref[...] = m_sc[...] + jnp.log(l_sc[...])

    qs, kv = _causal_specs(tq, tk)
    return _pcall(
        body, name="mla_attn_fwd",
        grid_spec=pltpu.PrefetchScalarGridSpec(
            num_scalar_prefetch=3, grid=(nh, tables[0].shape[0]),
            in_specs=[qs(dq), kv(dq), kv(dv)], out_specs=(qs(dv), qs(1)),
            scratch_shapes=[pltpu.VMEM((tq, 1), F32), pltpu.VMEM((tq, 1), F32), pltpu.VMEM((tq, dv), F32)]),
        out_shape=(_sds((nh, t, dv), F32), _sds((nh, t, 1), F32)),
        compiler_params=_cparams("parallel", "arbitrary"))(*tables, q, k, v)


def _mla_bwd(q, k, k_t, v, do, lse_row, dl_row, tq, tk):
    nh, t, dq = q.shape
    dv = v.shape[2]
    tq, tk = min(tq, t), min(tk, t)
    nq = t // tq
    tables, nunit = _causal_steps(nq, t // tk, tq, tk, False)

    def body(it, jt, fl, q_ref, k_ref, kt_ref, v_ref, do_ref, lse_ref, dl_ref, dk_ref, dv_ref, dq_ref, dk_sc, dv_sc):
        step = pl.program_id(1)
        i, j, flags = it[step], jt[step], fl[step]

        def update(units):
            off = 0 if units is None else units * (tq // nunit)
            qq = q_ref[off:, :]
            st = _dot(k_ref[...], qq, NT)
            if units is not None:
                key = j * tk + lax.broadcasted_iota(jnp.int32, (tk, tq - off), 0)
                qry = i * tq + off + lax.broadcasted_iota(jnp.int32, (tk, tq - off), 1)
                st = jnp.where(key <= qry, st, NEG)
            pt = jnp.exp(st - lse_ref[:, off:])
            dob = do_ref[off:, :].astype(BF16)
            dpt = _dot(v_ref[...], dob, NT)
            dst = pt * (dpt - dl_ref[:, off:])
            dsb = dst.astype(BF16)
            dv_part = _dot(pt.astype(BF16), dob, NN)
            dk_part = _dot(dsb, qq, NN)
            dq_part = _dot(kt_ref[...], dsb, NN)

            @pl.when((flags & 1) != 0)
            def _():
                dv_sc[...] = dv_part
                dk_sc[...] = dk_part

            @pl.when((flags & 1) == 0)
            def _():
                dv_sc[...] += dv_part
                dk_sc[...] += dk_part

            if off == 0:
                @pl.when(j == 0)
                def _():
                    dq_ref[i] = dq_part

                @pl.when(j != 0)
                def _():
                    dq_ref[i] += dq_part
            else:
                dq_ref[i, :, off:] += dq_part

        _by_crossing(flags, nunit, update)

        @pl.when((flags & 2) != 0)
        def _():
            dk_ref[...] = dk_sc[...]
            dv_ref[...] = dv_sc[...]

    qs, kv = _causal_specs(tq, tk)
    rowv = pl.BlockSpec((None, 1, tq), lambda h, s, it, jt, fl: (h, 0, it[s]))
    ktv = pl.BlockSpec((None, dq, tk), lambda h, s, it, jt, fl: (h, 0, jt[s]))
    whole = pl.BlockSpec((None, nq, dq, tq), lambda h, s, it, jt, fl: (h, 0, 0, 0))
    return _pcall(
        body, name="mla_attn_bwd",
        grid_spec=pltpu.PrefetchScalarGridSpec(
            num_scalar_prefetch=3, grid=(nh, tables[0].shape[0]),
            in_specs=[qs(dq), kv(dq), ktv, kv(dv), qs(dv), rowv, rowv], out_specs=(kv(dq), kv(dv), whole),
            scratch_shapes=[pltpu.VMEM((tk, dq), F32), pltpu.VMEM((tk, dv), F32)]),
        out_shape=(_sds((nh, t, dq), F32), _sds((nh, t, dv), F32), _sds((nh, nq, dq, tq), F32)),
        compiler_params=_cparams("parallel", "arbitrary"))(*tables, q, k, k_t, v, do, lse_row, dl_row)


def _loss_head(y, target, tm):
    t, d = y.shape
    tm = min(tm, t)
    nt = t // tm

    def body(y_ref, t_ref, dy_ref, loss_ref, acc):
        i = pl.program_id(0)
        err = y_ref[...] - t_ref[...]
        dy_ref[...] = err * (1.0 / d)

        @pl.when(i == 0)
        def _():
            acc[...] = jnp.zeros_like(acc)

        acc[...] += jnp.sum(err * err, axis=0, keepdims=True)

        @pl.when(i == nt - 1)
        def _():
            loss_ref[0, 0] = jnp.sum(acc[...]) * (0.5 / d)

    spec = pl.BlockSpec((tm, d), lambda i: (i, 0))
    return _pcall(
        body, name="loss_head", grid=(nt,), in_specs=[spec, spec],
        out_specs=(spec, pl.BlockSpec(memory_space=pltpu.SMEM)),
        out_shape=(_sds((t, d), F32), _sds((1, 1), F32)),
        scratch_shapes=[pltpu.VMEM((1, d), F32)],
        compiler_params=_cparams("arbitrary"))(y, target)


def _adamw(w, g, m, v, name):
    r, c = w.shape
    tr = r
    for cand in (256, 128, 64, 32, 16, 8):
        if r % cand == 0:
            tr = cand
            break

    def body(w_ref, g_ref, m_ref, v_ref, d_ref, nm_ref, nv_ref):
        gg = g_ref[...]
        nm = ADAM_B1 * m_ref[...] + (1.0 - ADAM_B1) * gg
        nv = ADAM_B2 * v_ref[...] + (1.0 - ADAM_B2) * (gg * gg)
        m_hat = nm / (1.0 - ADAM_B1 ** ADAM_STEP)
        v_hat = nv / (1.0 - ADAM_B2 ** ADAM_STEP)
        d_ref[...] = -ADAM_LR * (m_hat / (jnp.sqrt(v_hat) + ADAM_EPS) + ADAM_WD * w_ref[...])
        nm_ref[...] = nm
        nv_ref[...] = nv

    spec = pl.BlockSpec((tr, c), lambda i: (i, 0))
    sd = _sds((r, c), F32)
    return _pcall(body, name=name, grid=(r // tr,), in_specs=[spec] * 4, out_specs=(spec,) * 3,
                  out_shape=(sd, sd, sd), compiler_params=_cparams("parallel"))(w, g, m, v)


MESH_ID = pl.DeviceIdType.MESH
HBM_SPEC = pl.BlockSpec(memory_space=pltpu.HBM)


def _place():
    return lax.axis_index("x"), lax.axis_index("y"), lax.axis_index("c")


def _other_chips(x, y):
    return [(1 - x, y), (x, 1 - y), (1 - x, 1 - y)]


def _remote(src, dst, send_sems, recv_sems, k, to):
    return pltpu.make_async_remote_copy(src_ref=src, dst_ref=dst, send_sem=send_sems.at[k], recv_sem=recv_sems.at[k],
                                        device_id=to, device_id_type=MESH_ID)


def _halves(arrays):
    for a in arrays:
        assert a.shape[-2] % 32 == 0
    return [a.shape[-2] // 2 for a in arrays]


def _gather_start(srcs, outs, halves, send_sems, recv_sems):
    x, y, c = _place()
    for a, half in enumerate(halves):
        rows = pl.ds(c * half, half)
        for k, (cx, cy) in enumerate(_other_chips(x, y)):
            _remote(srcs[a].at[rows, :], outs[a].at[2 * x + y, rows, :], send_sems, recv_sems, 3 * a + k,
                    (cx, cy, c)).start()


def _gather_wait(outs, halves, send_sems, recv_sems):
    x, y, c = _place()
    for a, half in enumerate(halves):
        for k, (cx, cy) in enumerate(_other_chips(x, y)):
            got = outs[a].at[2 * cx + cy, pl.ds(c * half, half), :]
            _remote(got, got, send_sems, recv_sems, 3 * a + k, (x, y, c)).wait()


def _forward_cores(partly):
    n = len(partly)
    halves = _halves(partly)

    def body(*refs):
        srcs, outs, send_sems, recv_sems = refs[:n], refs[n:2 * n], refs[2 * n], refs[2 * n + 1]
        x, y, c = _place()
        for a, half in enumerate(halves):
            for k, (cx, cy) in enumerate(_other_chips(x, y)):
                rows = pl.ds(c * half, half)
                _remote(srcs[a].at[2 * cx + cy, rows, :], outs[a].at[2 * cx + cy, rows, :], send_sems, recv_sems,
                        3 * a + k, (x, y, 1 - c)).start()
        for a, half in enumerate(halves):
            for k, (cx, cy) in enumerate(_other_chips(x, y)):
                mine = outs[a].at[2 * cx + cy, pl.ds(c * half, half), :]
                theirs = outs[a].at[2 * cx + cy, pl.ds((1 - c) * half, half), :]
                _remote(mine, theirs, send_sems, recv_sems, 3 * a + k, (x, y, c)).wait()

    return _pcall(
        body, name="forward_cores", in_specs=[HBM_SPEC] * n, out_specs=tuple([HBM_SPEC] * n),
        out_shape=tuple(_sds(p.shape, p.dtype) for p in partly), input_output_aliases={a: a for a in range(n)},
        scratch_shapes=[pltpu.SemaphoreType.DMA((3 * n,)), pltpu.SemaphoreType.DMA((3 * n,))],
    )(*partly)


def _gather_weights(blocks):
    n = len(blocks)
    halves = _halves(blocks)

    def body(*refs):
        srcs, outs, send_sems, recv_sems = refs[:n], refs[n:2 * n], refs[2 * n], refs[2 * n + 1]
        x, y, c = _place()
        me = 2 * x + y
        sibling = (x, y, 1 - c)
        chips = _other_chips(x, y)

        def part(a, chip, core):
            return outs[a].at[chip, pl.ds(core * halves[a], halves[a]), :]

        for a in range(n):
            mine = srcs[a].at[pl.ds(c * halves[a], halves[a]), :]
            for k, (cx, cy) in enumerate(chips):
                _remote(mine, part(a, me, c), send_sems, recv_sems, 6 * a + k, (cx, cy, c)).start()
        for k, (cx, cy) in enumerate(chips):
            for a in range(n):
                got = part(a, 2 * cx + cy, c)
                _remote(got, got, send_sems, recv_sems, 6 * a + k, (x, y, c)).wait_recv()
                _remote(got, got, send_sems, recv_sems, 6 * a + 3 + k, sibling).start()
        for k, (cx, cy) in enumerate(chips):
            for a in range(n):
                got = part(a, 2 * cx + cy, 1 - c)
                _remote(got, got, send_sems, recv_sems, 6 * a + 3 + k, (x, y, c)).wait_recv()
        for a in range(n):
            sent = part(a, me, c)
            for k in range(6):
                _remote(sent, sent, send_sems, recv_sems, 6 * a + k, (x, y, c)).wait_send()

    return _pcall(
        body, name="gather_weights", in_specs=[HBM_SPEC] * n, out_specs=tuple([HBM_SPEC] * n),
        out_shape=tuple(_sds((N_CHIPS,) + b.shape, b.dtype) for b in blocks),
        scratch_shapes=[pltpu.SemaphoreType.DMA((6 * n,)), pltpu.SemaphoreType.DMA((6 * n,))],
    )(*blocks)


def _reduce_cores(grads, tag):
    n = len(grads)
    halves = _halves(grads)

    def body(*refs):
        gs, outs, send_sems, recv_sems = refs[:n], refs[n:2 * n], refs[2 * n], refs[2 * n + 1]
        x, y, c = _place()
        for a in range(n):
            for j in range(N_CHIPS):
                _remote(gs[a].at[j, pl.ds((1 - c) * halves[a], halves[a]), :], outs[a].at[j],
                        send_sems, recv_sems, a, (x, y, 1 - c)).start()
        for a in range(n):
            _remote(gs[a].at[:, pl.ds((1 - c) * halves[a], halves[a]), :], outs[a],
                    send_sems, recv_sems, a, (x, y, c)).wait()

    return _pcall(
        body, name=f"reduce_cores_{tag}", in_specs=[HBM_SPEC] * n, out_specs=tuple([HBM_SPEC] * n),
        out_shape=tuple(_sds((N_CHIPS, h, g.shape[2]), g.dtype) for g, h in zip(grads, halves)),
        scratch_shapes=[pltpu.SemaphoreType.DMA((n,)), pltpu.SemaphoreType.DMA((n,))],
    )(*grads)


def _scatter_shapes(parts):
    return tuple(_sds((3,) + p.shape[1:], p.dtype) for p in parts)


def _scatter_sems(n):
    return [pltpu.SemaphoreType.DMA((3 * n,)), pltpu.SemaphoreType.DMA((3 * n,))]


def _scatter_start(ps, outs, send_sems, recv_sems):
    x, y, c = _place()
    for a in range(len(ps)):
        for k, (cx, cy) in enumerate(_other_chips(x, y)):
            _remote(ps[a].at[2 * cx + cy], outs[a].at[k], send_sems, recv_sems, 3 * a + k, (cx, cy, c)).start()


def _scatter_wait(ps, outs, send_sems, recv_sems):
    x, y, c = _place()
    for a in range(len(ps)):
        for k in range(3):
            _remote(ps[a].at[k], outs[a].at[k], send_sems, recv_sems, 3 * a + k, (x, y, c)).wait()


def _sum_partials(received, parts, place):
    n = len(parts)
    steps = 2
    tiles = [p.shape[1] // steps for p in parts]

    def body(place_ref, *refs):
        rs, ps, outs = refs[:n], refs[n:2 * n], refs[2 * n:]
        for a in range(n):
            tot = ps[a][...].astype(F32)
            for k in range(3):
                tot = tot + rs[a][k].astype(F32)
            outs[a][...] = tot

    cols = [p.shape[2] for p in parts]
    return _pcall(
        body, name="sum_chip_partials",
        grid_spec=pltpu.PrefetchScalarGridSpec(
            num_scalar_prefetch=1, grid=(steps,),
            in_specs=[pl.BlockSpec((3, tm, w), lambda i, pc: (0, i, 0)) for tm, w in zip(tiles, cols)]
            + [pl.BlockSpec((None, tm, w), lambda i, pc: (pc[0], i, 0)) for tm, w in zip(tiles, cols)],
            out_specs=tuple(pl.BlockSpec((tm, w), lambda i, pc: (pc[1] * steps + i, 0)) for tm, w in zip(tiles, cols))),
        out_shape=tuple(_sds((2 * p.shape[1], p.shape[2]), F32) for p in parts),
        compiler_params=_cparams("parallel"))(place, *received, *parts)


def _share_cores(blocks):
    n = len(blocks)
    halves = _halves(blocks)

    def body(*refs):
        srcs, outs, send_sems, recv_sems = refs[:n], refs[n:2 * n], refs[2 * n], refs[2 * n + 1]
        x, y, c = _place()
        for a in range(n):
            piece = pl.ds(c * halves[a], halves[a])
            _remote(srcs[a].at[piece, :], outs[a].at[piece, :], send_sems, recv_sems, a, (x, y, 1 - c)).start()
        for a in range(n):
            mine = outs[a].at[pl.ds(c * halves[a], halves[a]), :]
            theirs = outs[a].at[pl.ds((1 - c) * halves[a], halves[a]), :]
            _remote(mine, theirs, send_sems, recv_sems, a, (x, y, c)).wait()

    return _pcall(
        body, name="share_cores", in_specs=[HBM_SPEC] * n, out_specs=tuple([HBM_SPEC] * n),
        out_shape=tuple(_sds(b.shape, b.dtype) for b in blocks), input_output_aliases={a: a for a in range(n)},
        scratch_shapes=[pltpu.SemaphoreType.DMA((n,)), pltpu.SemaphoreType.DMA((n,))],
    )(*blocks)


def _sum_blocks(stacked, name, tm):
    n, rows, lanes = stacked.shape
    tm = min(tm, rows)

    def body(s_ref, o_ref):
        tot = s_ref[n - 1].astype(F32)
        for k in range(n - 1):
            tot = tot + s_ref[k].astype(F32)
        o_ref[...] = tot

    return _pcall(body, name=name, grid=(rows // tm,),
                  in_specs=[pl.BlockSpec((n, tm, lanes), lambda i: (0, i, 0))],
                  out_specs=pl.BlockSpec((tm, lanes), lambda i: (i, 0)), out_shape=_sds((rows, lanes), F32),
                  compiler_params=_cparams("parallel"))(stacked)


def _add_halves(grads, theirs, core, tag):
    n = len(grads)
    steps = 2
    tiles = [t.shape[1] // steps for t in theirs]
    cols = [t.shape[2] for t in theirs]

    def body(c_ref, *refs):
        gs, ts, outs = refs[:n], refs[n:2 * n], refs[2 * n:]
        for a in range(n):
            outs[a][...] = (gs[a][...] + ts[a][...]).astype(BF16)

    own = [pl.BlockSpec((None, tm, w), lambda k, i, c: (k, c[0] * steps + i, 0)) for tm, w in zip(tiles, cols)]
    same = [pl.BlockSpec((None, tm, w), lambda k, i, c: (k, i, 0)) for tm, w in zip(tiles, cols)]
    return _pcall(
        body, name=f"add_core_halves_{tag}",
        grid_spec=pltpu.PrefetchScalarGridSpec(
            num_scalar_prefetch=1, grid=(N_CHIPS, steps), in_specs=own + same, out_specs=tuple(same)),
        out_shape=tuple(_sds(t.shape, BF16) for t in theirs),
        compiler_params=_cparams("parallel", "parallel"))(core, *grads, *theirs)


def _allreduce_small(part):
    rows, lanes = part.shape
    ndev = 8

    def body(src, tot, buf, send_sems, recv_sems):
        x, y, c = _place()
        me = 4 * x + 2 * y + c
        buf[me] = src[...]
        sends = []
        for k in range(1, ndev):
            peer = (x ^ (k >> 2), y ^ ((k >> 1) & 1), c ^ (k & 1))
            cp = _remote(src, buf.at[me], send_sems, recv_sems, k - 1, peer)
            cp.start()
            sends.append(cp)
        for k in range(1, ndev):
            theirs = buf.at[me ^ k]
            _remote(theirs, theirs, send_sems, recv_sems, k - 1, (x, y, c)).wait_recv()
        for cp in sends:
            cp.wait_send()
        acc = buf[0]
        for d in range(1, ndev):
            acc = acc + buf[d]
        tot[...] = acc

    vm = pl.BlockSpec(memory_space=pltpu.VMEM)
    return _pcall(
        body, name="allreduce_small", in_specs=[vm], out_specs=vm, out_shape=_sds((rows, lanes), F32),
        scratch_shapes=[pltpu.VMEM((ndev, rows, lanes), F32), pltpu.SemaphoreType.DMA((ndev - 1,)),
                        pltpu.SemaphoreType.DMA((ndev - 1,))],
    )(part)


def _pack_small(vals):
    parts = []
    for name, shape, r in SMALL:
        flat = vals[name].reshape(-1).astype(F32)
        parts.append(jnp.pad(flat, (0, r * LANES - flat.shape[0])).reshape(r, LANES))
    used = sum(r for _, _, r in SMALL)
    parts.append(jnp.zeros((SMALL_ROWS - used, LANES), F32))
    return jnp.concatenate(parts, axis=0)


def _unpack_small(packed):
    out, off = {}, 0
    for name, shape, r in SMALL:
        n = int(np.prod(shape))
        out[name] = packed[off:off + r].reshape(-1)[:n].reshape(shape)
        off += r
    return out


def _heads_major(a, nh):
    t = a.shape[0]
    return a.reshape(t, nh, a.shape[1] // nh).transpose(1, 0, 2)


def _tokens_major(a):
    nh, t, w = a.shape
    return a.transpose(1, 0, 2).reshape(t, nh * w)


LATE = ("ffn1_w_gate", "ffn1_w_up", "ffn1_w_down")
EARLY = tuple(name for name, _ in BIG if name not in LATE)


def _local_step(x, target, small, wfull, exchanges=None, later_weights=None):
    t = x.shape[0]
    nh, hd = DIL_HEADS, DIL_HD
    grads_s, grads_b = {}, {}

    x1, ffn1_saved, partly = _ffn_fwd(x, small["ffn1_norm"], wfull["ffn1_w_gate"], wfull["ffn1_w_up"],
                                      wfull["ffn1_w_down"], "ffn1", later_weights[0] if later_weights else ())
    if later_weights:
        wfull = {**wfull, **later_weights[1](partly)}
    w_in = wfull["w_in"].transpose(1, 0, 2).reshape(D_MODEL, -1)
    w_out = wfull["w_out"].reshape(D_MODEL, D_MODEL)
    w_qb, w_kvb = wfull["mla_w_q_b"], wfull["mla_w_kv_b"]
    hm = _rms_fwd(x1, small["mix_norm"], BF16, "mix_norm", 512)
    proj = _mm_simple("in_proj", hm, w_in, NN, F32, tm=1024)
    cq, ckv, k_pe = proj[:, 1536:1792], proj[:, 1792:1920], proj[:, 1920:1984]

    gq, gk = jnp.tile(small["dil_q_norm"], (1, nh)), jnp.tile(small["dil_k_norm"], (1, nh))
    qn = _head_norm_fwd(proj, 0, gq, "dil_q_norm", 512)
    kn = _head_norm_fwd(proj, 1, gk, "dil_k_norm", 512)
    v_d = _head_norm_fwd(proj, 2, None, "dil_v_views", 512)
    bias = _bias_tiles(small["rel_bias"]).reshape(3, nh // 2, 2 * QB, QB + DIL_W)
    outs, lses = [], []
    for b, dil in enumerate(DIL_DILATIONS):
        o_b, lse_b = _dil_fwd(qn[b], kn[b], v_d[b], bias[b], dil, f"dil_fwd_{dil}")
        outs.append(o_b)
        lses.append(lse_b)
    o_dil, lse_tot, od = _dil_merge(outs, lses, small["out_norm_dil"], 512)

    mh = MLA_HEADS
    cos_t, sin_t = _rope_tables(t)
    cqn = _rms_fwd(cq, small["mla_q_a_norm"], BF16, "mla_q_a_norm", 512)
    ckvn = _rms_fwd(ckv, small["mla_kv_a_norm"], BF16, "mla_kv_a_norm", 512)
    tm = min(512, t)

    th = min(2048, t)

    def head_proj(name, a, w, width):
        k = a.shape[1]
        return _mm(name, (mh, t // th, 1),
                   [(a, pl.BlockSpec((th, k), lambda h, i, r: (i, 0)), w, pl.BlockSpec((None, k, width), lambda h, i, r: (h, 0, 0)))],
                   NN, _sds((mh, t, width), F32), pl.BlockSpec((None, th, width), lambda h, i, r: (h, i, 0)), (th, width))

    q_raw = head_proj("mla_q_proj", cqn, w_qb, MLA_QK)
    kv_raw = head_proj("mla_kv_proj", ckvn, w_kvb, MLA_NOPE + MLA_V)
    q_raw2, kv_raw2 = q_raw.reshape(mh * t, MLA_QK), kv_raw.reshape(mh * t, MLA_NOPE + MLA_V)
    q_scale = MLA_QK ** -0.5
    q_m = _mla_qk_fwd(q_raw2, small["mla_q_norm"], cos_t, sin_t, q_scale, "mla_q_rope", 2048).reshape(mh, t, MLA_QK)
    k_m, v_m = _mla_qk_fwd(kv_raw2, small["mla_k_norm"], cos_t, sin_t, 1.0, "mla_k_rope", 2048, pe=k_pe)
    k_m, v_m = k_m.reshape(mh, t, MLA_QK), v_m.reshape(mh, t, MLA_V)
    o_mla_h, lse_m = _mla_fwd(q_m, k_m, v_m, 512, 4096)
    o_mla = _tokens_major(o_mla_h)

    om = _rms_fwd(o_mla, small["out_norm_mla"], BF16, "out_norm_mla", 512)
    half_w = DIL_WIDTH
    row = pl.BlockSpec((tm, D_MODEL), lambda i, j, r: (i, 0))
    act_spec = pl.BlockSpec((tm, half_w), lambda i, j, r: (i, 0))
    x2 = _mm("out_proj", (t // tm, 1, 1),
             [(od, act_spec, w_out, pl.BlockSpec((half_w, D_MODEL), lambda i, j, r: (0, 0))),
              (om, act_spec, w_out, pl.BlockSpec((half_w, D_MODEL), lambda i, j, r: (1, 0)))],
             NN, _sds((t, D_MODEL), F32), row, (tm, D_MODEL), res=(x1, row))
    x3, ffn2_saved, _ = _ffn_fwd(x2, small["ffn2_norm"], wfull["ffn2_w_gate"], wfull["ffn2_w_up"],
                                 wfull["ffn2_w_down"], "ffn2")
    dy, loss = _loss_head(x3, target, 512)

    dx2, grads_s["ffn2_norm"], grads_b["ffn2_w_gate"], grads_b["ffn2_w_up"], grads_b["ffn2_w_down"], _, _ = _ffn_bwd(
        dy, x2, small["ffn2_norm"], wfull["ffn2_w_gate"], wfull["ffn2_w_up"], wfull["ffn2_w_down"], ffn2_saved, "ffn2")

    d_ocat = _mm_simple("out_proj_dx", dx2, w_out, NT, F32, tm=1024)
    dw_out_d = _mm_simple("out_proj_dw_dil", od, dx2, TN, F32, tk=2048)
    dw_out_m = _mm_simple("out_proj_dw_mla", om, dx2, TN, F32, tk=2048)
    grads_b["w_out"] = jnp.concatenate([dw_out_d, dw_out_m], axis=0).reshape(N_CHIPS, D_MODEL // N_CHIPS, D_MODEL)
    do_dil, grads_s["out_norm_dil"] = _rms_bwd([d_ocat[:, :half_w]], o_dil, small["out_norm_dil"], None, "out_norm_dil_bwd", 512)
    do_mla, grads_s["out_norm_mla"] = _rms_bwd([d_ocat[:, half_w:]], o_mla, small["out_norm_mla"], None, "out_norm_mla_bwd", 512)

    do_m = _heads_major(do_mla, mh)
    dl_m = _rowdot(do_m.reshape(mh * t, MLA_V), o_mla_h.reshape(mh * t, MLA_V), "mla_delta", 2048).reshape(mh, t, 1)
    dk_m, dv_m, dq_t = _mla_bwd(q_m, k_m, k_m.transpose(0, 2, 1), v_m, do_m, lse_m.reshape(mh, 1, t),
                                dl_m.reshape(mh, 1, t), 2048, 512)
    dq_m = dq_t.transpose(0, 1, 3, 2).reshape(mh, t, MLA_QK)
    dq_raw, grads_s["mla_q_norm"] = _mla_qk_bwd(dq_m.reshape(mh * t, MLA_QK), q_raw2, small["mla_q_norm"],
                                                 cos_t, sin_t, q_scale, "mla_q_rope_bwd", 2048)
    dk_nope, dk_pe_h, grads_s["mla_k_norm"] = _mla_qk_bwd(dk_m.reshape(mh * t, MLA_QK), kv_raw2, small["mla_k_norm"],
                                                          cos_t, sin_t, 1.0, "mla_k_rope_bwd", 2048, pe=k_pe)
    dq_raw = dq_raw.reshape(mh, t, MLA_QK)
    dk_nope = dk_nope.reshape(mh, t, MLA_NOPE)

    def head_proj_dx(name, d, w):
        width, k = d.shape[2], w.shape[1]
        pairs = [(d, pl.BlockSpec((None, th, width), lambda i, j, r, h=h: (h, i, 0)),
                  w, pl.BlockSpec((None, k, width), lambda i, j, r, h=h: (h, 0, 0))) for h in range(mh)]
        return _mm(name, (t // th, 1, 1), pairs, NT, _sds((t, k), F32),
                   pl.BlockSpec((th, k), lambda i, j, r: (i, 0)), (th, k))

    def head_proj_dw(name, a, d):
        width, k = d.shape[2], a.shape[1]
        return _mm(name, (mh, 1, t // th),
                   [(a, pl.BlockSpec((th, k), lambda h, j, r: (r, 0)), d, pl.BlockSpec((None, th, width), lambda h, j, r: (h, r, 0)))],
                   TN, _sds((mh, k, width), F32), pl.BlockSpec((None, k, width), lambda h, j, r: (h, 0, 0)), (k, width))

    d_cqn = head_proj_dx("mla_q_proj_dx", dq_raw, w_qb)
    kv_pairs = []
    for h in range(mh):
        for part, d_part in enumerate((dk_nope, dv_m)):
            kv_pairs.append((d_part, pl.BlockSpec((None, th, MLA_NOPE), lambda i, j, r, h=h: (h, i, 0)),
                             w_kvb, pl.BlockSpec((None, MLA_KV_RANK, MLA_NOPE), lambda i, j, r, h=h, part=part: (h, 0, part))))
    d_ckvn = _mm("mla_kv_proj_dx", (t // th, 1, 1), kv_pairs, NT, _sds((t, MLA_KV_RANK), F32),
                 pl.BlockSpec((th, MLA_KV_RANK), lambda i, j, r: (i, 0)), (th, MLA_KV_RANK))
    grads_b["mla_w_q_b"] = head_proj_dw("mla_q_proj_dw", cqn, dq_raw)
    grads_b["mla_w_kv_b"] = jnp.concatenate([head_proj_dw("mla_k_proj_dw", ckvn, dk_nope),
                                             head_proj_dw("mla_v_proj_dw", ckvn, dv_m)], axis=2)
    d_cq, grads_s["mla_q_a_norm"] = _rms_bwd([d_cqn], cq, small["mla_q_a_norm"], None, "mla_q_a_norm_bwd", 512)
    d_ckv, grads_s["mla_kv_a_norm"] = _rms_bwd([d_ckvn], ckv, small["mla_kv_a_norm"], None, "mla_kv_a_norm_bwd", 512)
    d_kpe = _sum_blocks(dk_pe_h.reshape(mh, t * MLA_ROPE // LANES, LANES), "mla_kpe_sum", 1024).reshape(t, MLA_ROPE)

    stats, do_db = _dil_stats(do_dil, o_dil, lse_tot, 512)
    dqs, dks, dvs, dtiles = [], [], [], []
    for b, dil in enumerate(DIL_DILATIONS):
        dq_b, dk_b, dv_b, db_b = _dil_bwd(qn[b], kn[b], v_d[b], do_db[b], stats[b], bias[b], dil, f"dil_bwd_{dil}")
        dqs.append(dq_b)
        dks.append(dk_b)
        dvs.append(dv_b)
        dtiles.append(db_b)
    grads_s["rel_bias"] = _bias_grad(jnp.stack(dtiles).reshape(3, nh, QB, QB + DIL_W))
    dq_a, dgq = _head_norm_bwd(dqs, proj, 0, gq, "dil_q_norm_bwd", 512)
    dk_a, dgk = _head_norm_bwd(dks, proj, 1, gk, "dil_k_norm_bwd", 512)
    grads_s["dil_q_norm"], grads_s["dil_k_norm"] = dgq[:, :hd], dgk[:, :hd]
    dv_a = _sum_branches(dvs, "dil_dv_sum", 512)

    dparts = [dq_a, dk_a, dv_a, d_cq, d_ckv, d_kpe]
    t2 = min(512, t)
    pairs, dw_parts, lo = [], [], 0
    for n, dpart in enumerate(dparts):
        width = dpart.shape[1]
        w_part = w_in[:, lo:lo + width]
        pairs.append((dpart, pl.BlockSpec((t2, width), lambda i, j, r: (i, 0)),
                      w_part, pl.BlockSpec((D_MODEL, width), lambda i, j, r: (0, 0))))
        dw_parts.append(_mm_simple(f"in_proj_dw_{n}", hm, dpart, TN, F32, tk=2048))
        lo += width
    row2 = pl.BlockSpec((t2, D_MODEL), lambda i, j, r: (i, 0))
    dx1, grads_s["mix_norm"] = _mm("in_proj_dx", (t // t2, 1, 1), pairs, NT, _sds((t, D_MODEL), F32), row2,
                                   (t2, D_MODEL), res=(dx2, row2), norm=(x1, small["mix_norm"]))
    dw_in = jnp.concatenate(dw_parts, axis=1)
    grads_b["w_in"] = dw_in.reshape(D_MODEL, N_CHIPS, -1).transpose(1, 0, 2)
    outgoing = exchanges[0]([grads_b[n] for n in EARLY]) if exchanges else ()
    dx, grads_s["ffn1_norm"], grads_b["ffn1_w_gate"], grads_b["ffn1_w_up"], grads_b["ffn1_w_down"], arrived, late = _ffn_bwd(
        dx1, x, small["ffn1_norm"], wfull["ffn1_w_gate"], wfull["ffn1_w_up"], wfull["ffn1_w_down"], ffn1_saved, "ffn1",
        outgoing, exchanges[1] if exchanges else None)
    return loss, dx, grads_s, grads_b, (tuple(outgoing), arrived), late


def kernel(x, ffn1_norm, ffn1_w_gate, ffn1_w_up, ffn1_w_down, mix_norm, w_in, dil_q_norm, dil_k_norm, rel_bias, mla_q_a_norm, mla_w_q_b, mla_kv_a_norm, mla_w_kv_b, mla_q_norm, mla_k_norm, out_norm_dil, out_norm_mla, w_out, ffn2_norm, ffn2_w_gate, ffn2_w_up, ffn2_w_down, loss_target, m_ffn1_norm, m_ffn1_w_gate, m_ffn1_w_up, m_ffn1_w_down, m_mix_norm, m_w_in, m_dil_q_norm, m_dil_k_norm, m_rel_bias, m_mla_q_a_norm, m_mla_w_q_b, m_mla_kv_a_norm, m_mla_w_kv_b, m_mla_q_norm, m_mla_k_norm, m_out_norm_dil, m_out_norm_mla, m_w_out, m_ffn2_norm, m_ffn2_w_gate, m_ffn2_w_up, m_ffn2_w_down, v_ffn1_norm, v_ffn1_w_gate, v_ffn1_w_up, v_ffn1_w_down, v_mix_norm, v_w_in, v_dil_q_norm, v_dil_k_norm, v_rel_bias, v_mla_q_a_norm, v_mla_w_q_b, v_mla_kv_a_norm, v_mla_w_kv_b, v_mla_q_norm, v_mla_k_norm, v_out_norm_dil, v_out_norm_mla, v_w_out, v_ffn2_norm, v_ffn2_w_gate, v_ffn2_w_up, v_ffn2_w_down):
    given = dict(locals())
    big_names = [name for name, _ in BIG]
    small_names = [name for name, _, _ in SMALL]

    chip = (2 * lax.axis_index("x") + lax.axis_index("y")).astype(jnp.int32)
    core = lax.axis_index("c").astype(jnp.int32)
    mine = {n: given[n].astype(BF16) for n in big_names}

    def with_own(names, arrays):
        return {n: lax.dynamic_update_slice(a, mine[n], (chip, 0, 0)) for n, a in zip(names, arrays)}

    wfirst = with_own(LATE, _gather_weights([mine[n][0] for n in LATE]))
    later_weights = ([mine[n][0] for n in EARLY], lambda partly: with_own(EARLY, _forward_cores(partly)))
    small = {n: given[n] for n in small_names}

    def chip_partials(partial, tag):
        return _add_halves(partial, _reduce_cores(partial, tag), core.reshape(1), tag)

    exchanges = (functools.partial(chip_partials, tag="early"), functools.partial(chip_partials, tag="late"))
    loss, dx, grads_s, grads_b, (early_part, early_got), (late_part, late_got) = _local_step(
        x[0], loss_target[0], small, wfirst, exchanges, later_weights)
    loss = lax.psum(loss[0, 0], ("x", "y", "c"))
    reduced = _sum_partials(tuple(late_got) + tuple(early_got), tuple(late_part) + tuple(early_part),
                            jnp.stack([chip, core]))
    g_big = dict(zip(LATE + EARLY, _share_cores(reduced)))
    g_small = _unpack_small(_allreduce_small(_pack_small(grads_s)))

    grad, delta, new_m, new_v = {}, {}, {}, {}
    for name, shape in BIG:
        g2 = g_big[name]
        d_, m_, v_ = _adamw(given[name].reshape(shape), g2, given["m_" + name].reshape(shape),
                            given["v_" + name].reshape(shape), f"adamw_{name}")
        full = given[name].shape
        grad[name], delta[name], new_m[name], new_v[name] = (a.reshape(full) for a in (g2, d_, m_, v_))
    for name in small_names:
        grad[name] = g_small[name]
        delta[name], new_m[name], new_v[name] = _adamw(given[name], g_small[name], given["m_" + name],
                                                       given["v_" + name], f"adamw_{name}")

    return (loss, dx[None], *[grad[n] for n in WEIGHTS], *[delta[n] for n in WEIGHTS],
            *[new_m[n] for n in WEIGHTS], *[new_v[n] for n in WEIGHTS])
```

```python
import functools

import numpy as np
import jax
import jax.numpy as jnp
from jax import lax
from jax.experimental import pallas as pl
from jax.experimental.pallas import tpu as pltpu

F32 = jnp.float32
BF16 = jnp.bfloat16

D_MODEL = 1024
D_FF = 2816
N_CHIPS = 4
DIL_HEADS = 8
DIL_HD = 64
DIL_WIDTH = 512
DIL_DILATIONS = (1, 4, 16)
DIL_W = 128
QB = 128
MLA_HEADS = 4
MLA_NOPE = 128
MLA_ROPE = 64
MLA_QK = 192
MLA_V = 128
MLA_Q_RANK = 256
MLA_KV_RANK = 128
ROPE_BASE = 10000.0
REL_BUCKETS = 32
REL_MAX_DIST = 2048
FFN_RESID = 0.5
EPS = 1e-6
NEG = -1e30
LANES = 128

ADAM_LR = 0.001
ADAM_B1 = 0.9
ADAM_B2 = 0.999
ADAM_EPS = 1e-08
ADAM_WD = 0.01
ADAM_STEP = 10

NT = (((1,), (1,)), ((), ()))
NN = (((1,), (0,)), ((), ()))
TN = (((0,), (0,)), ((), ()))

BIG = (
    ("ffn1_w_gate", (D_MODEL, D_FF // N_CHIPS)),
    ("ffn1_w_up", (D_MODEL, D_FF // N_CHIPS)),
    ("ffn1_w_down", (D_FF // N_CHIPS, D_MODEL)),
    ("w_in", (D_MODEL, 1984 // N_CHIPS)),
    ("mla_w_q_b", (MLA_Q_RANK, MLA_QK)),
    ("mla_w_kv_b", (MLA_KV_RANK, MLA_NOPE + MLA_V)),
    ("w_out", (D_MODEL // N_CHIPS, D_MODEL)),
    ("ffn2_w_gate", (D_MODEL, D_FF // N_CHIPS)),
    ("ffn2_w_up", (D_MODEL, D_FF // N_CHIPS)),
    ("ffn2_w_down", (D_FF // N_CHIPS, D_MODEL)),
)
SMALL = (
    ("ffn1_norm", (1, 1024), 8), ("mix_norm", (1, 1024), 8), ("dil_q_norm", (1, 64), 1),
    ("dil_k_norm", (1, 64), 1), ("rel_bias", (8, 32), 2), ("mla_q_a_norm", (1, 256), 2),
    ("mla_kv_a_norm", (1, 128), 1), ("mla_q_norm", (1, 192), 2), ("mla_k_norm", (1, 192), 2),
    ("out_norm_dil", (1, 512), 4), ("out_norm_mla", (1, 512), 4), ("ffn2_norm", (1, 1024), 8),
)
SMALL_ROWS = 48
WEIGHTS = ("ffn1_norm", "ffn1_w_gate", "ffn1_w_up", "ffn1_w_down", "mix_norm", "w_in", "dil_q_norm",
           "dil_k_norm", "rel_bias", "mla_q_a_norm", "mla_w_q_b", "mla_kv_a_norm", "mla_w_kv_b",
           "mla_q_norm", "mla_k_norm", "out_norm_dil", "out_norm_mla", "w_out", "ffn2_norm",
           "ffn2_w_gate", "ffn2_w_up", "ffn2_w_down")


def _pcall(body, **kw):
    return pl.pallas_call(body, **kw)


def _cparams(*sem):
    return pltpu.CompilerParams(dimension_semantics=sem)


def _sds(shape, dtype):
    return jax.ShapeDtypeStruct(shape, dtype)


def _dot(a, b, dn):
    return lax.dot_general(a, b, dn, preferred_element_type=F32)


def _rms_fwd(x, g, out_dtype, name, tm):
    n, d = x.shape
    tm = min(tm, n)

    def body(x_ref, g_ref, o_ref):
        xf = x_ref[...].astype(F32)
        r = lax.rsqrt(jnp.mean(xf * xf, axis=-1, keepdims=True) + EPS)
        o_ref[...] = (xf * r * g_ref[...]).astype(o_ref.dtype)

    return _pcall(
        body, name=name, grid=(n // tm,),
        in_specs=[pl.BlockSpec((tm, d), lambda i: (i, 0)), pl.BlockSpec((1, d), lambda i: (0, 0))],
        out_specs=pl.BlockSpec((tm, d), lambda i: (i, 0)),
        out_shape=_sds((n, d), out_dtype), compiler_params=_cparams("parallel"))(x, g)


def _rms_bwd(dys, x, g, res, name, tm):
    n, d = x.shape
    tm = min(tm, n)
    nd = len(dys)
    has_res = res is not None

    def body(*refs):
        dy_refs = refs[:nd]
        x_ref, g_ref = refs[nd], refs[nd + 1]
        res_ref = refs[nd + 2] if has_res else None
        dx_ref, dg_ref = refs[-2], refs[-1]
        dy = dy_refs[0][...].astype(F32)
        for r_ in dy_refs[1:]:
            dy = dy + r_[...].astype(F32)
        xf = x_ref[...].astype(F32)
        r = lax.rsqrt(jnp.mean(xf * xf, axis=-1, keepdims=True) + EPS)
        xh = xf * r
        dxh = dy * g_ref[...]
        dx = r * (dxh - xh * jnp.mean(dxh * xh, axis=-1, keepdims=True))
        if has_res:
            dx = dx + res_ref[...]
        dx_ref[...] = dx

        @pl.when(pl.program_id(0) == 0)
        def _():
            dg_ref[...] = jnp.zeros_like(dg_ref)

        dg_ref[...] += jnp.sum(dy * xh, axis=0, keepdims=True)

    row = pl.BlockSpec((tm, d), lambda i: (i, 0))
    vec = pl.BlockSpec((1, d), lambda i: (0, 0))
    ins = list(dys) + [x, g] + ([res] if has_res else [])
    return _pcall(
        body, name=name, grid=(n // tm,),
        in_specs=[row] * nd + [row, vec] + ([row] if has_res else []),
        out_specs=(row, vec),
        out_shape=(_sds((n, d), F32), _sds((1, d), F32)),
        compiler_params=_cparams("arbitrary"))(*ins)


def _mm(name, grid, pairs, dn, out_shape, out_spec, acc_shape, res=None, scale=1.0, outgoing=(), norm=None,
        exchange="chips"):
    npairs = len(pairs)
    nred = grid[2]
    has_res = res is not None
    has_norm = norm is not None
    no = len(outgoing)
    ex_start, ex_wait, ex_shapes, ex_sems = EXCHANGES[exchange]

    def body(*refs):
        ab = refs[:2 * npairs]
        res_ref = refs[2 * npairs] if has_res else None
        nin = 2 * npairs + int(has_res) + 2 * int(has_norm)
        if has_norm:
            x_ref, g_ref = refs[nin - 2:nin]
        first_out = nin + no
        sent = refs[nin:first_out]
        o_ref = refs[first_out]
        nout = 1 + int(has_norm)
        dg_ref = refs[first_out + 1] if has_norm else None
        arrived = refs[first_out + nout:first_out + nout + no]
        acc_ref = refs[first_out + nout + no] if nred > 1 else None
        if no:
            send_sems, recv_sems = refs[-2:]
            ids = [pl.program_id(n) for n in range(3)]

            @pl.when((ids[0] == 0) & (ids[1] == 0) & (ids[2] == 0))
            def _():
                ex_start(sent, arrived, send_sems, recv_sems)

        tot = None
        for p in range(npairs):
            d = _dot(ab[2 * p][...].astype(BF16), ab[2 * p + 1][...].astype(BF16), dn)
            tot = d if tot is None else tot + d

        def finish(v):
            if scale != 1.0:
                v = v * scale
            if has_norm:
                xf = x_ref[...]
                r = lax.rsqrt(jnp.mean(xf * xf, axis=-1, keepdims=True) + EPS)
                xh = xf * r
                dxh = v * g_ref[...]

                @pl.when(pl.program_id(0) == 0)
                def _():
                    dg_ref[...] = jnp.zeros_like(dg_ref)

                dg_ref[...] += jnp.sum(v * xh, axis=0, keepdims=True)
                v = r * (dxh - xh * jnp.mean(dxh * xh, axis=-1, keepdims=True))
            if has_res:
                v = res_ref[...] + v
            o_ref[...] = v.astype(o_ref.dtype)

        if nred == 1:
            finish(tot)
        else:
            r = pl.program_id(2)

            @pl.when(r == 0)
            def _():
                acc_ref[...] = tot

            @pl.when(r > 0)
            def _():
                acc_ref[...] += tot

            @pl.when(r == nred - 1)
            def _():
                finish(acc_ref[...])

        if no:
            @pl.when((ids[0] == grid[0] - 1) & (ids[1] == grid[1] - 1) & (ids[2] == nred - 1))
            def _():
                ex_wait(sent, arrived, send_sems, recv_sems)

    ins, specs = [], []
    for a, a_spec, b, b_spec in pairs:
        ins += [a, b]
        specs += [a_spec, b_spec]
    if has_res:
        ins.append(res[0])
        specs.append(res[1])
    scratch = [pltpu.VMEM(acc_shape, F32)] if nred > 1 else []
    if not no and not has_norm:
        return _pcall(
            body, name=name, grid=grid, in_specs=specs, out_specs=out_spec, out_shape=out_shape,
            scratch_shapes=scratch, compiler_params=_cparams("parallel", "parallel", "arbitrary"))(*ins)
    out_specs, out_shapes = (out_spec,), (out_shape,)
    if has_norm:
        assert grid[1] == 1
        d = norm[1].shape[1]
        ins += [norm[0], norm[1]]
        specs += [out_spec, pl.BlockSpec((1, d), lambda i, j, r: (0, 0))]
        out_specs += (pl.BlockSpec((1, d), lambda i, j, r: (0, 0)),)
        out_shapes += (_sds((1, d), F32),)
    hbm = pl.BlockSpec(memory_space=pltpu.HBM)
    res_ = tuple(_pcall(
        body, name=name, grid=grid, in_specs=specs + [hbm] * no, out_specs=out_specs + (hbm,) * no,
        out_shape=out_shapes + ex_shapes(outgoing),
        scratch_shapes=scratch + (ex_sems(no) if no else []),
        compiler_params=_cparams("arbitrary", "arbitrary", "arbitrary"))(*ins, *outgoing))
    nout = len(out_shapes)
    return res_[:nout] + ((res_[nout:],) if no else ())


def _ffn_up(h, wg, wu, name, tm, incoming=()):
    t, d = h.shape
    nc, _, fs = wg.shape
    tm = min(tm, t)
    nt = t // tm
    ni = len(incoming)
    halves = _halves(incoming)

    def body(*refs):
        h_ref, wg_ref, wu_ref = refs[:3]
        srcs = refs[3:3 + ni]
        g_ref, u_ref, a_ref = refs[3 + ni:6 + ni]
        outs = refs[6 + ni:6 + 2 * ni]
        if ni:
            send_sems, recv_sems = refs[6 + 2 * ni:]
            c, i = pl.program_id(0), pl.program_id(1)

            @pl.when((c == 0) & (i == 0))
            def _():
                _gather_start(srcs, outs, halves, send_sems, recv_sems)

        hh = h_ref[...]
        gate = _dot(hh, wg_ref[...], NN)
        up = _dot(hh, wu_ref[...], NN)
        sig = jax.nn.sigmoid(gate)
        silu = gate * sig
        g_ref[...] = (up * (sig + silu * (1.0 - sig))).astype(BF16)
        u_ref[...] = silu.astype(BF16)
        a_ref[...] = (silu * up).astype(BF16)

        if ni:
            @pl.when((c == nc - 1) & (i == nt - 1))
            def _():
                _gather_wait(outs, halves, send_sems, recv_sems)

    wspec = pl.BlockSpec((None, d, fs), lambda c, i: (c, 0, 0))
    ospec = pl.BlockSpec((None, tm, fs), lambda c, i: (c, i, 0))
    hbm = pl.BlockSpec(memory_space=pltpu.HBM)
    osd = _sds((nc, t, fs), BF16)
    res = tuple(_pcall(
        body, name=name, grid=(nc, nt),
        in_specs=[pl.BlockSpec((tm, d), lambda c, i: (i, 0)), wspec, wspec] + [hbm] * ni,
        out_specs=(ospec, ospec, ospec) + (hbm,) * ni,
        out_shape=(osd, osd, osd) + tuple(_sds((N_CHIPS,) + b.shape, b.dtype) for b in incoming),
        scratch_shapes=[pltpu.SemaphoreType.DMA((3 * ni,)), pltpu.SemaphoreType.DMA((3 * ni,))] if ni else [],
        compiler_params=_cparams("arbitrary", "arbitrary"))(h, wg, wu, *incoming))
    return res[:3] + (res[3:],)


def _ffn_hidden_bwd(dy, h, wd, dact_dgate, dact_dup, act, name, tm, outgoing=()):
    t, d = dy.shape
    nc, fs, _ = wd.shape
    tm = min(tm, t)
    nt = t // tm
    no = len(outgoing)

    def body(*refs):
        dy_ref, h_ref, wd_ref, g_ref, u_ref, a_ref = refs[:6]
        sent = refs[6:6 + no]
        dg_ref, du_ref, dwg_hbm, dwu_hbm, dwd_hbm = refs[6 + no:11 + no]
        arrived = refs[11 + no:11 + 2 * no]
        wg_acc, wu_acc, wd_acc, sem = refs[11 + 2 * no:15 + 2 * no]
        c, i = pl.program_id(0), pl.program_id(1)
        if no:
            send_sems, recv_sems = refs[15 + 2 * no:]

            @pl.when((c == 0) & (i == 0))
            def _():
                _scatter_start(sent, arrived, send_sems, recv_sems)

        dyb = dy_ref[...].astype(BF16)
        da = _dot(dyb, wd_ref[...], NT) * FFN_RESID
        dgate = (da * g_ref[...].astype(F32)).astype(BF16)
        dup = (da * u_ref[...].astype(F32)).astype(BF16)
        dg_ref[...] = dgate
        du_ref[...] = dup
        hh = h_ref[...]
        parts = (_dot(hh, dgate, TN), _dot(hh, dup, TN), _dot(a_ref[...], dyb, TN) * FFN_RESID)
        accs = (wg_acc, wu_acc, wd_acc)

        @pl.when(i == 0)
        def _():
            for acc, part in zip(accs, parts):
                acc[...] = part

        @pl.when(i > 0)
        def _():
            for acc, part in zip(accs, parts):
                acc[...] += part

        @pl.when(i == nt - 1)
        def _():
            copies = [pltpu.make_async_copy(acc, out.at[c], sem.at[n])
                      for n, (acc, out) in enumerate(zip(accs, (dwg_hbm, dwu_hbm, dwd_hbm)))]
            for cp in copies:
                cp.start()
            for cp in copies:
                cp.wait()

        if no:
            @pl.when((c == nc - 1) & (i == nt - 1))
            def _():
                _scatter_wait(sent, arrived, send_sems, recv_sems)

    tok = pl.BlockSpec((tm, d), lambda c, i: (i, 0))
    cspec = pl.BlockSpec((None, tm, fs), lambda c, i: (c, i, 0))
    hbm = pl.BlockSpec(memory_space=pltpu.HBM)
    osd = _sds((nc, t, fs), BF16)
    res = _pcall(
        body, name=name, grid=(nc, nt),
        in_specs=[tok, tok, pl.BlockSpec((None, fs, d), lambda c, i: (c, 0, 0)), cspec, cspec, cspec] + [hbm] * no,
        out_specs=(cspec, cspec, hbm, hbm, hbm) + (hbm,) * no,
        out_shape=(osd, osd, _sds((nc, d, fs), F32), _sds((nc, d, fs), F32), _sds((nc, fs, d), F32))
        + _scatter_shapes(outgoing),
        scratch_shapes=[pltpu.VMEM((d, fs), F32), pltpu.VMEM((d, fs), F32), pltpu.VMEM((fs, d), F32),
                        pltpu.SemaphoreType.DMA((3,))] + (_scatter_sems(no) if no else []),
        compiler_params=_cparams("arbitrary", "arbitrary"))(dy, h, wd, dact_dgate, dact_dup, act, *outgoing)
    res = tuple(res)
    return res[:5] + (res[5:],)


def _ffn_fwd(x, g, wg, wu, wd, tag, incoming=()):
    t = x.shape[0]
    nc, _, fs = wg.shape
    tm = min(512, t)
    h = _rms_fwd(x, g, BF16, f"{tag}_norm", 512)
    dact_dgate, dact_dup, act, partly = _ffn_up(h, wg, wu, f"{tag}_up", 1024, incoming)
    pairs = [(act, pl.BlockSpec((None, tm, fs), lambda i, j, r, c=c: (c, i, 0)),
              wd, pl.BlockSpec((None, fs, D_MODEL), lambda i, j, r, c=c: (c, 0, 0))) for c in range(nc)]
    row = pl.BlockSpec((tm, D_MODEL), lambda i, j, r: (i, 0))
    y = _mm(f"{tag}_down", (t // tm, 1, 1), pairs, NN, _sds((t, D_MODEL), F32), row, (tm, D_MODEL),
            res=(x, row), scale=FFN_RESID)
    return y, (h, dact_dgate, dact_dup, act), partly


def _ffn_bwd(dy, x, g, wg, wu, wd, saved, tag, outgoing=(), own_exchange=None):
    h, dact_dgate, dact_dup, act = saved
    t = x.shape[0]
    nc, _, fs = wg.shape
    tm = min(512, t)
    dgate, dup, dwg, dwu, dwd, arrived = _ffn_hidden_bwd(dy, h, wd, dact_dgate, dact_dup, act,
                                                         f"{tag}_hidden_bwd", 1024, outgoing)
    pairs = []
    for c in range(nc):
        a_spec = pl.BlockSpec((None, tm, fs), lambda i, j, r, c=c: (c, i, 0))
        w_spec = pl.BlockSpec((None, D_MODEL, fs), lambda i, j, r, c=c: (c, 0, 0))
        pairs += [(dgate, a_spec, wg, w_spec), (dup, a_spec, wu, w_spec)]
    own_part = tuple(own_exchange([dwg, dwu, dwd])) if own_exchange else ()
    row = pl.BlockSpec((tm, D_MODEL), lambda i, j, r: (i, 0))
    res = _mm(f"{tag}_dh", (t // tm, 1, 1), pairs, NT, _sds((t, D_MODEL), F32), row, (tm, D_MODEL),
              res=(dy, row), norm=(x, g), outgoing=own_part)
    dx, dg = res[0], res[1]
    own_got = res[2] if own_part else ()
    return dx, dg, dwg, dwu, dwd, arrived, (own_part, own_got)


def _mm_simple(name, a, b, dn, out_dtype, tm=512, tk=512, res=None, scale=1.0):
    if dn == TN:
        k, m = a.shape
        n = b.shape[1]
        tk = min(tk, k)
        return _mm(name, (1, 1, k // tk),
                   [(a, pl.BlockSpec((tk, m), lambda i, j, r: (r, 0)), b, pl.BlockSpec((tk, n), lambda i, j, r: (r, 0)))],
                   TN, _sds((m, n), out_dtype), pl.BlockSpec((m, n), lambda i, j, r: (0, 0)), (m, n), scale=scale)
    m, k = a.shape
    n = b.shape[1] if dn == NN else b.shape[0]
    tm = min(tm, m)
    row = pl.BlockSpec((tm, n), lambda i, j, r: (i, 0))
    return _mm(name, (m // tm, 1, 1),
               [(a, pl.BlockSpec((tm, k), lambda i, j, r: (i, 0)), b, pl.BlockSpec(b.shape, lambda i, j, r: (0, 0)))],
               dn, _sds((m, n), out_dtype), row, (tm, n), res=None if res is None else (res, row), scale=scale)


def _t5_bucket(dist):
    max_exact = REL_BUCKETS // 2
    d = np.maximum(dist, 1).astype(np.float32)
    large = max_exact + (np.log(d / max_exact) / np.log(REL_MAX_DIST / max_exact)
                         * (REL_BUCKETS - max_exact)).astype(np.int32)
    large = np.minimum(large, REL_BUCKETS - 1)
    return np.where(dist < max_exact, dist, large).astype(np.int32)


def _bucket_tiles():
    i = np.arange(QB)[:, None]
    j = np.arange(QB + DIL_W)[None, :]
    delta = np.clip(i + DIL_W - j, 0, None)
    return np.stack([_t5_bucket(delta * dil) for dil in DIL_DILATIONS]).astype(np.int32)


def _bias_tiles(rel_bias):
    buckets = jnp.asarray(_bucket_tiles())

    def body(rb_ref, bk_ref, o_ref):
        bk = bk_ref[...]
        for h in range(DIL_HEADS):
            def pick(b, tile):
                return jnp.where(bk == b, rb_ref[h, b], tile)

            o_ref[h] = lax.fori_loop(0, REL_BUCKETS, pick, jnp.zeros((QB, QB + DIL_W), F32))

    return _pcall(
        body, name="dil_bias_tiles", grid=(3,),
        in_specs=[pl.BlockSpec(memory_space=pltpu.SMEM),
                  pl.BlockSpec((None, QB, QB + DIL_W), lambda b: (b, 0, 0))],
        out_specs=pl.BlockSpec((None, DIL_HEADS, QB, QB + DIL_W), lambda b: (b, 0, 0, 0)),
        out_shape=_sds((3, DIL_HEADS, QB, QB + DIL_W), F32),
        compiler_params=_cparams("parallel"))(rel_bias, buckets)


def _bias_grad(dtiles):
    buckets = jnp.asarray(_bucket_tiles())

    def body(dt_ref, bk_ref, o_ref):
        def one(b, carry):
            hit = [bk_ref[br] == b for br in range(3)]
            for h in range(DIL_HEADS):
                tot = jnp.zeros((), F32)
                for br in range(3):
                    tot = tot + jnp.sum(jnp.where(hit[br], dt_ref[br, h], 0.0))
                o_ref[h, b] = tot
            return carry

        lax.fori_loop(0, REL_BUCKETS, one, 0)

    return _pcall(
        body, name="dil_bias_grad",
        in_specs=[pl.BlockSpec(memory_space=pltpu.VMEM), pl.BlockSpec(memory_space=pltpu.VMEM)],
        out_specs=pl.BlockSpec(memory_space=pltpu.SMEM),
        out_shape=_sds((DIL_HEADS, REL_BUCKETS), F32))(dtiles, buckets)


def _split_heads(a, lo):
    zero = jnp.zeros_like(a)
    return jnp.concatenate([jnp.where(lo, a, zero), jnp.where(lo, zero, a)], axis=0)


def _side_by_side(a):
    n = a.shape[0] // 2
    return jnp.concatenate([a[:n], a[n:]], axis=1)


def _band_masks(prev_ok):
    ii = lax.broadcasted_iota(jnp.int32, (2 * QB, QB), 0) & (QB - 1)
    jj = lax.broadcasted_iota(jnp.int32, (2 * QB, QB), 1)
    return jj <= ii, jj >= ii + jnp.where(prev_ok, 0, QB)


def _dil_fwd(q, k, v, bias, dil, name):
    w = DIL_WIDTH
    t = q.shape[0] * dil
    npair = w // LANES
    nl = t // dil // QB
    scale = DIL_HD ** -0.5

    def body(q_ref, kc_ref, kp_ref, vc_ref, vp_ref, b_ref, o_ref, lse_ref):
        nn = pl.program_id(1)
        lo = lax.broadcasted_iota(jnp.int32, (QB, LANES), 1) < DIL_HD
        lo2 = lax.broadcasted_iota(jnp.int32, (2 * QB, LANES), 1) < DIL_HD
        ii = lax.broadcasted_iota(jnp.int32, (2 * QB, 2 * QB), 0) & (QB - 1)
        jj = lax.broadcasted_iota(jnp.int32, (2 * QB, 2 * QB), 1)
        first_key = jnp.maximum(ii, jnp.where(nn != 0, 0, QB))
        valid = (jj >= first_key) & (jj <= ii + QB)
        for p in range(npair):
            cols = slice(p * LANES, (p + 1) * LANES)
            qq = _split_heads(q_ref[:, cols], lo)
            kk = jnp.concatenate([kp_ref[:, cols], kc_ref[:, cols]], axis=0)
            vv = jnp.concatenate([vp_ref[:, cols], vc_ref[:, cols]], axis=0)
            s = jnp.where(valid, _dot(qq, kk, NT) * scale + b_ref[p], NEG)
            m = jnp.max(s, axis=-1, keepdims=True)
            e = jnp.exp(s - m)
            den = jnp.sum(e, axis=-1, keepdims=True)
            pn = (e * (1.0 / den)).astype(BF16)
            o_ref[:, cols] = _dot(_side_by_side(pn), _split_heads(vv, lo2), NN)
            lse = m + jnp.log(den)
            lse_ref[:, cols] = jnp.where(lo, lse[:QB], lse[QB:])

    cur = pl.BlockSpec((QB, w), lambda r, n: (n, r))
    prev = pl.BlockSpec((QB, w), lambda r, n: (jnp.maximum(n - 1, 0), r))
    sd = _sds((t // dil, dil * w), F32)
    return _pcall(
        body, name=name, grid=(dil, nl),
        in_specs=[cur, cur, prev, cur, prev, pl.BlockSpec((npair, 2 * QB, 2 * QB), lambda r, n: (0, 0, 0))],
        out_specs=(cur, cur), out_shape=(sd, sd),
        compiler_params=_cparams("parallel", "parallel"))(q, k, k, v, v, bias)


def _dil_bwd(q, k, v, do, stats, bias, dil, name):
    w = DIL_WIDTH
    t = q.shape[0] * dil
    npair = w // LANES
    nl = t // dil // QB
    scale = DIL_HD ** -0.5

    def body(qc_ref, qn_ref, doc_ref, don_ref, sc_ref, sn_ref, k_ref, v_ref, b_ref,
             dq_ref, dk_ref, dv_ref, db_ref, carry):
        r, nn = pl.program_id(0), pl.program_id(1)
        lo = lax.broadcasted_iota(jnp.int32, (QB, LANES), 1) < DIL_HD
        cur_ok, prev_ok = _band_masks(nn + 1 < nl)

        @pl.when((r == 0) & (nn == 0))
        def _():
            db_ref[...] = jnp.zeros_like(db_ref)
            carry[...] = jnp.zeros_like(carry)

        for p in range(npair):
            cols = slice(p * LANES, (p + 1) * LANES)
            kp, vp = k_ref[:, cols], v_ref[:, cols]
            k2 = _split_heads(kp, lo)

            def column(ref, lane):
                first = p * LANES + lane
                return jnp.concatenate([ref[:, first:first + 1], ref[:, first + DIL_HD:first + DIL_HD + 1]], axis=0)

            def side(q_ref, do_ref, s_ref, bias, ok):
                qq = _split_heads(q_ref[:, cols], lo)
                dd = _split_heads(do_ref[:, cols], lo)
                s = jnp.where(ok, _dot(qq, kp, NT) * scale + bias, NEG)
                prob = jnp.exp(s - column(s_ref, 0))
                ds = prob * (_dot(dd, vp, NT) - column(s_ref, DIL_HD // 2))
                return qq, dd, prob.astype(BF16), ds

            q1, d1, p1, ds1 = side(qc_ref, doc_ref, sc_ref, b_ref[p, :, QB:], cur_ok)
            q2, d2, p2, ds2 = side(qn_ref, don_ref, sn_ref, b_ref[p, :, :QB], prev_ok)
            ds1b, ds2b = ds1.astype(BF16), ds2.astype(BF16)
            dq_ref[:, cols] = carry[:, cols] + _dot(_side_by_side(ds1b), k2, NN) * scale
            carry[:, cols] = _dot(_side_by_side(ds2b), k2, NN) * scale
            dk_ref[:, cols] = _dot(jnp.concatenate([ds1b, ds2b], axis=0), jnp.concatenate([q1, q2], axis=0), TN) * scale
            dv_ref[:, cols] = _dot(jnp.concatenate([p1, p2], axis=0), jnp.concatenate([d1, d2], axis=0), TN)
            db_ref[p, :, QB:] += ds1
            db_ref[p, :, :QB] += ds2

    cur = pl.BlockSpec((QB, w), lambda r, n: (n, r))
    nxt = pl.BlockSpec((QB, w), lambda r, n: (jnp.minimum(n + 1, nl - 1), r))
    tile = pl.BlockSpec((npair, 2 * QB, 2 * QB), lambda r, n: (0, 0, 0))
    sd = _sds((t // dil, dil * w), F32)
    return _pcall(
        body, name=name, grid=(dil, nl),
        in_specs=[cur, nxt, cur, nxt, cur, nxt, cur, cur, tile],
        out_specs=(cur, cur, cur, tile),
        out_shape=(sd, sd, sd, _sds((npair, 2 * QB, 2 * QB), F32)),
        scratch_shapes=[pltpu.VMEM((QB, w), F32)],
        compiler_params=_cparams("arbitrary", "arbitrary"))(q, q, do, do, stats, stats, k, v, bias)


def _head_sum_matrix(scale):
    idx = np.arange(DIL_WIDTH) // DIL_HD
    return jnp.asarray((idx[:, None] == idx[None, :]).astype(np.float32) * scale, BF16)


def _head_sum(x, mat):
    hi = x.astype(BF16)
    lo = (x - hi.astype(F32)).astype(BF16)
    return _dot(hi, mat, NN) + _dot(lo, mat, NN)


def _to_views(src, tmp, out_refs):
    tm, w = src.shape
    for j in range(w // LANES):
        tmp[j] = src[:, j * LANES:(j + 1) * LANES]
    for d, o_ref in zip(DIL_DILATIONS, out_refs):
        if d == 1:
            o_ref[...] = src.astype(o_ref.dtype)
            continue
        for r in range(d):
            for j in range(w // LANES):
                lo = r * w + j * LANES
                o_ref[:, lo:lo + LANES] = tmp[j, pl.ds(r, tm // d, stride=d), :].astype(o_ref.dtype)


def _from_view(v_ref, tmp, d):
    tm = tmp.shape[1]
    w = v_ref.shape[1] // d
    for r in range(d):
        for j in range(w // LANES):
            lo = r * w + j * LANES
            tmp[j, pl.ds(r, tm // d, stride=d), :] = v_ref[:, lo:lo + LANES]
    return jnp.concatenate([tmp[j] for j in range(w // LANES)], axis=1)


def _view_specs(tm, t, dtype):
    specs = tuple(pl.BlockSpec((tm // d, d * DIL_WIDTH), lambda i: (i, 0)) for d in DIL_DILATIONS)
    shapes = tuple(_sds((t // d, d * DIL_WIDTH), dtype) for d in DIL_DILATIONS)
    return specs, shapes


def _view_scratch(tm):
    return pltpu.VMEM((DIL_WIDTH // LANES, tm, LANES), F32)


def _dil_merge(outs, lses, g, tm):
    w = DIL_WIDTH
    t = outs[0].shape[0]
    tm = min(tm, t)

    def body(o0, o1, o2, l0, l1, l2, g_ref, o_ref, l_ref, n_ref, so1, so2, sl1, sl2):
        d1, d2 = DIL_DILATIONS[1], DIL_DILATIONS[2]
        a0, a1, a2 = l0[...], _from_view(l1, sl1, d1), _from_view(l2, sl2, d2)
        m = jnp.maximum(jnp.maximum(a0, a1), a2)
        e0, e1, e2 = jnp.exp(a0 - m), jnp.exp(a1 - m), jnp.exp(a2 - m)
        den = e0 + e1 + e2
        o = (e0 * o0[...] + e1 * _from_view(o1, so1, d1) + e2 * _from_view(o2, so2, d2)) / den
        o_ref[...] = o
        l_ref[...] = m + jnp.log(den)
        r = lax.rsqrt(jnp.mean(o * o, axis=-1, keepdims=True) + EPS)
        n_ref[...] = (o * r * g_ref[...]).astype(n_ref.dtype)

    specs, _ = _view_specs(tm, t, F32)
    spec = pl.BlockSpec((tm, w), lambda i: (i, 0))
    return _pcall(
        body, name="dil_merge", grid=(t // tm,),
        in_specs=list(specs) * 2 + [pl.BlockSpec((1, w), lambda i: (0, 0))], out_specs=(spec, spec, spec),
        out_shape=(_sds((t, w), F32), _sds((t, w), F32), _sds((t, w), BF16)),
        scratch_shapes=[_view_scratch(tm)] * 4,
        compiler_params=_cparams("parallel"))(*outs, *lses, g)


def _dil_stats(do, o, lse, tm):
    t, w = do.shape
    tm = min(tm, t)

    def body(a_ref, b_ref, l_ref, m_ref, s1, s4, s16, d1, d4, d16, tmp):
        first = (lax.broadcasted_iota(jnp.int32, (tm, w), 1) & (DIL_HD - 1)) < DIL_HD // 2
        do_ = a_ref[...]
        _to_views(jnp.where(first, l_ref[...], _head_sum(do_ * b_ref[...], m_ref[...])), tmp, (s1, s4, s16))
        _to_views(do_, tmp, (d1, d4, d16))

    spec = pl.BlockSpec((tm, w), lambda i: (i, 0))
    f_specs, f_shapes = _view_specs(tm, t, F32)
    b_specs, b_shapes = _view_specs(tm, t, BF16)
    res = _pcall(body, name="dil_stats", grid=(t // tm,),
                 in_specs=[spec, spec, spec, pl.BlockSpec((w, w), lambda i: (0, 0))],
                 out_specs=f_specs + b_specs, out_shape=f_shapes + b_shapes,
                 scratch_shapes=[_view_scratch(tm)],
                 compiler_params=_cparams("parallel"))(do, o, lse, _head_sum_matrix(1.0))
    return res[:3], res[3:]


def _head_norm_fwd(x, col, g, name, tm):
    t = x.shape[0]
    w = DIL_WIDTH
    tm = min(tm, t)
    normed = g is not None

    def body(*refs):
        outs, tmp = refs[-4:-1], refs[-1]
        xf = refs[0][...]
        if normed:
            g_ref, m_ref = refs[1], refs[2]
            xf = xf * lax.rsqrt(_head_sum(xf * xf, m_ref[...]) + EPS) * g_ref[...]
        _to_views(xf, tmp, outs)

    specs, shapes = _view_specs(tm, t, BF16)
    extra = [g, _head_sum_matrix(1.0 / DIL_HD)] if normed else []
    extra_specs = [pl.BlockSpec((1, w), lambda i: (0, 0)), pl.BlockSpec((w, w), lambda i: (0, 0))] if normed else []
    return _pcall(
        body, name=name, grid=(t // tm,),
        in_specs=[pl.BlockSpec((tm, w), lambda i: (i, col))] + extra_specs,
        out_specs=specs, out_shape=shapes, scratch_shapes=[_view_scratch(tm)],
        compiler_params=_cparams("parallel"))(x, *extra)


def _head_norm_bwd(dys, x, col, g, name, tm):
    t = x.shape[0]
    w = DIL_WIDTH
    tm = min(tm, t)
    nd = len(dys)
    nt = t // tm
    lane = np.arange(w) % DIL_HD
    fold = jnp.asarray((lane[:, None] == lane[None, :]).astype(np.float32))

    def body(*refs):
        x_ref, g_ref, m_ref, f_ref = refs[nd:nd + 4]
        dx_ref, dg_ref, s1, s2 = refs[-4:]
        dy = refs[0][...] + _from_view(refs[1], s1, DIL_DILATIONS[1]) + _from_view(refs[2], s2, DIL_DILATIONS[2])
        xf = x_ref[...]
        mat = m_ref[...]
        r = lax.rsqrt(_head_sum(xf * xf, mat) + EPS)
        xh = xf * r
        dxh = dy * g_ref[...]
        dx_ref[...] = r * (dxh - xh * _head_sum(dxh * xh, mat))

        @pl.when(pl.program_id(0) == 0)
        def _():
            dg_ref[...] = jnp.zeros_like(dg_ref)

        dg_ref[...] += jnp.sum(dy * xh, axis=0, keepdims=True)

        @pl.when(pl.program_id(0) == nt - 1)
        def _():
            per_lane = jnp.broadcast_to(dg_ref[...], (8, w))
            dg_ref[...] = lax.dot_general(per_lane, f_ref[...], NN, precision=lax.Precision.HIGHEST,
                                          preferred_element_type=F32)[0:1]

    row = pl.BlockSpec((tm, w), lambda i: (i, 0))
    vec = pl.BlockSpec((1, w), lambda i: (0, 0))
    sq = pl.BlockSpec((w, w), lambda i: (0, 0))
    views, _ = _view_specs(tm, t, F32)
    return _pcall(
        body, name=name, grid=(nt,),
        in_specs=list(views) + [pl.BlockSpec((tm, w), lambda i: (i, col)), vec, sq, sq],
        out_specs=(row, vec), out_shape=(_sds((t, w), F32), _sds((1, w), F32)),
        scratch_shapes=[_view_scratch(tm)] * 2,
        compiler_params=_cparams("arbitrary"))(*dys, x, g, _head_sum_matrix(1.0 / DIL_HD), fold)


def _rowdot(a, b, name, tm):
    n, d = a.shape
    tm = min(tm, n)

    def body(a_ref, b_ref, o_ref):
        o_ref[...] = jnp.sum(a_ref[...].astype(F32) * b_ref[...].astype(F32), axis=-1, keepdims=True)

    spec = pl.BlockSpec((tm, d), lambda i: (i, 0))
    return _pcall(body, name=name, grid=(n // tm,), in_specs=[spec, spec],
                  out_specs=pl.BlockSpec((tm, 1), lambda i: (i, 0)), out_shape=_sds((n, 1), F32),
                  compiler_params=_cparams("parallel"))(a, b)


def _sum_branches(parts, name, tm):
    t = parts[0].shape[0]
    w = DIL_WIDTH
    tm = min(tm, t)

    def body(a_ref, b_ref, c_ref, o_ref, s1, s2):
        o_ref[...] = a_ref[...] + _from_view(b_ref, s1, DIL_DILATIONS[1]) + _from_view(c_ref, s2, DIL_DILATIONS[2])

    views, _ = _view_specs(tm, t, F32)
    return _pcall(body, name=name, grid=(t // tm,), in_specs=list(views),
                  out_specs=pl.BlockSpec((tm, w), lambda i: (i, 0)), out_shape=_sds((t, w), F32),
                  scratch_shapes=[_view_scratch(tm)] * 2,
                  compiler_params=_cparams("parallel"))(*parts)


def _rope_tables(t):
    inv = ROPE_BASE ** (-np.arange(0, MLA_ROPE, 2, dtype=np.float64) / MLA_ROPE)
    ang = np.arange(t, dtype=np.float64)[:, None] * inv[None, :]
    cos, sin = np.cos(ang), np.sin(ang)
    return (jnp.asarray(np.concatenate([cos, cos], 1), F32), jnp.asarray(np.concatenate([-sin, sin], 1), F32))


def _swap_halves(a):
    half = MLA_ROPE // 2
    return jnp.concatenate([a[:, half:], a[:, :half]], axis=1)


def _qk_parts(x, pe, tm, nt):
    if pe is None:
        return None
    return (pl.BlockSpec((tm, MLA_NOPE), lambda i: (i, 0)), pl.BlockSpec((tm, MLA_ROPE), lambda i: (i % nt, 0)))


def _mla_qk_fwd(x, g, cos_t, sin_t, scale, name, tm, pe=None):
    n = x.shape[0]
    d = MLA_QK
    t = cos_t.shape[0]
    tm = min(tm, t)
    nt = t // tm
    split = _qk_parts(x, pe, tm, nt)

    def body(*refs):
        if split:
            xn_ref, xr_ref, xv_ref, g_ref, c_ref, s_ref, o_ref, v_ref = refs
            xn, xr = xn_ref[...], xr_ref[...]
            v_ref[...] = xv_ref[...].astype(v_ref.dtype)
        else:
            x_ref, g_ref, c_ref, s_ref, o_ref = refs
            xf = x_ref[...]
            xn, xr = xf[:, :MLA_NOPE], xf[:, MLA_NOPE:]
        ms = (jnp.sum(xn * xn, axis=-1, keepdims=True) + jnp.sum(xr * xr, axis=-1, keepdims=True)) * (1.0 / d)
        r = lax.rsqrt(ms + EPS)
        gg = g_ref[...]
        yn = xn * r * gg[:, :MLA_NOPE]
        yr = xr * r * gg[:, MLA_NOPE:]
        o_ref[:, :MLA_NOPE] = (yn * scale).astype(o_ref.dtype)
        o_ref[:, MLA_NOPE:] = ((yr * c_ref[...] + _swap_halves(yr) * s_ref[...]) * scale).astype(o_ref.dtype)

    row = pl.BlockSpec((tm, d), lambda i: (i, 0))
    vec = pl.BlockSpec((1, d), lambda i: (0, 0))
    tab = pl.BlockSpec((tm, MLA_ROPE), lambda i: (i % nt, 0))
    if not split:
        return _pcall(body, name=name, grid=(n // tm,), in_specs=[row, vec, tab, tab],
                      out_specs=row, out_shape=_sds((n, d), BF16),
                      compiler_params=_cparams("parallel"))(x, g, cos_t, sin_t)
    vals = pl.BlockSpec((tm, MLA_V), lambda i: (i, 1))
    return _pcall(body, name=name, grid=(n // tm,), in_specs=[split[0], split[1], vals, vec, tab, tab],
                  out_specs=(row, pl.BlockSpec((tm, MLA_V), lambda i: (i, 0))),
                  out_shape=(_sds((n, d), BF16), _sds((n, MLA_V), BF16)),
                  compiler_params=_cparams("parallel"))(x, pe, x, g, cos_t, sin_t)


def _mla_qk_bwd(dy, x, g, cos_t, sin_t, scale, name, tm, pe=None):
    n = x.shape[0]
    d = MLA_QK
    t = cos_t.shape[0]
    tm = min(tm, t)
    nt = t // tm
    split = _qk_parts(x, pe, tm, nt)

    def body(*refs):
        if split:
            dy_ref, xn_ref, xr_ref, g_ref, c_ref, s_ref, dxn_ref, dxr_ref, dg_ref = refs
            xn, xr = xn_ref[...], xr_ref[...]
        else:
            dy_ref, x_ref, g_ref, c_ref, s_ref, dx_ref, dg_ref = refs
            xf = x_ref[...]
            xn, xr = xf[:, :MLA_NOPE], xf[:, MLA_NOPE:]
        gg = g_ref[...]
        ms = (jnp.sum(xn * xn, axis=-1, keepdims=True) + jnp.sum(xr * xr, axis=-1, keepdims=True)) * (1.0 / d)
        r = lax.rsqrt(ms + EPS)
        xh_n, xh_r = xn * r, xr * r
        dyf = dy_ref[...] * scale
        dyr = dyf[:, MLA_NOPE:]
        dn_n = dyf[:, :MLA_NOPE]
        dn_r = dyr * c_ref[...] + _swap_halves(dyr * s_ref[...])
        dxh_n = dn_n * gg[:, :MLA_NOPE]
        dxh_r = dn_r * gg[:, MLA_NOPE:]
        mean = (jnp.sum(dxh_n * xh_n, axis=-1, keepdims=True)
                + jnp.sum(dxh_r * xh_r, axis=-1, keepdims=True)) * (1.0 / d)
        dx_n = r * (dxh_n - xh_n * mean)
        dx_r = r * (dxh_r - xh_r * mean)
        if split:
            dxn_ref[...] = dx_n
            dxr_ref[...] = dx_r
        else:
            dx_ref[:, :MLA_NOPE] = dx_n
            dx_ref[:, MLA_NOPE:] = dx_r

        @pl.when(pl.program_id(0) == 0)
        def _():
            dg_ref[...] = jnp.zeros_like(dg_ref)

        dg_ref[:, :MLA_NOPE] += jnp.sum(dn_n * xh_n, axis=0, keepdims=True)
        dg_ref[:, MLA_NOPE:] += jnp.sum(dn_r * xh_r, axis=0, keepdims=True)

    row = pl.BlockSpec((tm, d), lambda i: (i, 0))
    vec = pl.BlockSpec((1, d), lambda i: (0, 0))
    tab = pl.BlockSpec((tm, MLA_ROPE), lambda i: (i % nt, 0))
    if not split:
        return _pcall(body, name=name, grid=(n // tm,), in_specs=[row, row, vec, tab, tab],
                      out_specs=(row, vec), out_shape=(_sds((n, d), F32), _sds((1, d), F32)),
                      compiler_params=_cparams("arbitrary"))(dy, x, g, cos_t, sin_t)
    outs = (pl.BlockSpec((tm, MLA_NOPE), lambda i: (i, 0)), pl.BlockSpec((tm, MLA_ROPE), lambda i: (i, 0)), vec)
    return _pcall(body, name=name, grid=(n // tm,), in_specs=[row, split[0], split[1], vec, tab, tab],
                  out_specs=outs, out_shape=(_sds((n, MLA_NOPE), F32), _sds((n, MLA_ROPE), F32), _sds((1, d), F32)),
                  compiler_params=_cparams("arbitrary"))(dy, x, pe, g, cos_t, sin_t)


def _causal_mask(i, j, tq, tk, width):
    row = i * tq + lax.broadcasted_iota(jnp.int32, (tq, width), 0)
    col = j * tk + lax.broadcasted_iota(jnp.int32, (tq, width), 1)
    return col <= row


def _causal_steps(nq, nk, tq, tk, q_major):
    if q_major:
        groups = [[(i, j) for j in range((i * tq + tq - 1) // tk + 1)] for i in range(nq)]
        nunit = tk // tq if tk % tq == 0 else 1
    else:
        groups = [[(i, j) for i in range((j * tk) // tq, nq)] for j in range(nk)]
        nunit = tq // tk if tq % tk == 0 else 1
    it, jt, fl = [], [], []
    for g in groups:
        for n, (i, j) in enumerate(g):
            crossing = j * tk + tk - 1 > i * tq
            if q_major:
                unit = tk // nunit
                u = min(nunit, -(-(i * tq + tq - j * tk) // unit)) - 1
            else:
                unit = tq // nunit
                u = max(0, j * tk - i * tq) // unit
            it.append(i)
            jt.append(j)
            fl.append((n == 0) + 2 * (n == len(g) - 1) + 4 * crossing + 8 * (u if crossing else 0))
    return tuple(jnp.asarray(np.array(a, np.int32)) for a in (it, jt, fl)), nunit


def _by_crossing(flags, nunit, update):
    pl.when((flags & 4) == 0)(functools.partial(update, None))
    for u in range(nunit):
        pl.when(((flags & 4) != 0) & ((flags >> 3) == u))(functools.partial(update, u))


def _causal_specs(tq, tk):
    def qs(w):
        return pl.BlockSpec((None, tq, w), lambda h, s, it, jt, fl: (h, it[s], 0))

    def kv(w):
        return pl.BlockSpec((None, tk, w), lambda h, s, it, jt, fl: (h, jt[s], 0))

    return qs, kv


def _mla_fwd(q, k, v, tq, tk):
    nh, t, dq = q.shape
    dv = v.shape[2]
    tq, tk = min(tq, t), min(tk, t)
    tables, nunit = _causal_steps(t // tq, t // tk, tq, tk, True)

    def body(it, jt, fl, q_ref, k_ref, v_ref, o_ref, lse_ref, m_sc, l_sc, acc_sc):
        step = pl.program_id(1)
        i, j, flags = it[step], jt[step], fl[step]

        @pl.when((flags & 1) != 0)
        def _():
            m_sc[...] = jnp.full_like(m_sc, NEG)
            l_sc[...] = jnp.zeros_like(l_sc)
            acc_sc[...] = jnp.zeros_like(acc_sc)

        def update(units):
            wk = tk if units is None else (units + 1) * (tk // nunit)
            s = _dot(q_ref[...], k_ref[:wk, :], NT)
            if units is not None:
                s = jnp.where(_causal_mask(i, j, tq, tk, wk), s, NEG)
            m_prev = m_sc[...]
            m_new = jnp.maximum(m_prev, jnp.max(s, axis=-1, keepdims=True))
            alpha = jnp.exp(m_prev - m_new)
            p = jnp.exp(s - m_new)
            l_sc[...] = alpha * l_sc[...] + jnp.sum(p, axis=-1, keepdims=True)
            acc_sc[...] = alpha * acc_sc[...] + _dot(p.astype(BF16), v_ref[:wk, :], NN)
            m_sc[...] = m_new

        _by_crossing(flags, nunit, update)

        @pl.when((flags & 2) != 0)
        def _():
            o_ref[...] = acc_sc[...] / l_sc[...]
            lse_ref[...] = m_sc[...] + jnp.log(l_sc[...])

    qs, kv = _causal_specs(tq, tk)
    return _pcall(
        body, name="mla_attn_fwd",
        grid_spec=pltpu.PrefetchScalarGridSpec(
            num_scalar_prefetch=3, grid=(nh, tables[0].shape[0]),
            in_specs=[qs(dq), kv(dq), kv(dv)], out_specs=(qs(dv), qs(1)),
            scratch_shapes=[pltpu.VMEM((tq, 1), F32), pltpu.VMEM((tq, 1), F32), pltpu.VMEM((tq, dv), F32)]),
        out_shape=(_sds((nh, t, dv), F32), _sds((nh, t, 1), F32)),
        compiler_params=_cparams("parallel", "arbitrary"))(*tables, q, k, v)


def _mla_bwd(q, k, k_t, v, do, lse_row, dl_row, tq, tk):
    nh, t, dq = q.shape
    dv = v.shape[2]
    tq, tk = min(tq, t), min(tk, t)
    nq = t // tq
    tables, nunit = _causal_steps(nq, t // tk, tq, tk, False)

    def body(it, jt, fl, q_ref, k_ref, kt_ref, v_ref, do_ref, lse_ref, dl_ref, dk_ref, dv_ref, dq_ref, dk_sc, dv_sc):
        step = pl.program_id(1)
        i, j, flags = it[step], jt[step], fl[step]

        def update(units):
            off = 0 if units is None else units * (tq // nunit)
            qq = q_ref[off:, :]
            st = _dot(k_ref[...], qq, NT)
            if units is not None:
                key = j * tk + lax.broadcasted_iota(jnp.int32, (tk, tq - off), 0)
                qry = i * tq + off + lax.broadcasted_iota(jnp.int32, (tk, tq - off), 1)
                st = jnp.where(key <= qry, st, NEG)
            pt = jnp.exp(st - lse_ref[:, off:])
            dob = do_ref[off:, :].astype(BF16)
            dpt = _dot(v_ref[...], dob, NT)
            dst = pt * (dpt - dl_ref[:, off:])
            dsb = dst.astype(BF16)
            dv_part = _dot(pt.astype(BF16), dob, NN)
            dk_part = _dot(dsb, qq, NN)
            dq_part = _dot(kt_ref[...], dsb, NN)

            @pl.when((flags & 1) != 0)
            def _():
                dv_sc[...] = dv_part
                dk_sc[...] = dk_part

            @pl.when((flags & 1) == 0)
            def _():
                dv_sc[...] += dv_part
                dk_sc[...] += dk_part

            if off == 0:
                @pl.when(j == 0)
                def _():
                    dq_ref[i] = dq_part

                @pl.when(j != 0)
                def _():
                    dq_ref[i] += dq_part
            else:
                dq_ref[i, :, off:] += dq_part

        _by_crossing(flags, nunit, update)

        @pl.when((flags & 2) != 0)
        def _():
            dk_ref[...] = dk_sc[...]
            dv_ref[...] = dv_sc[...]

    qs, kv = _causal_specs(tq, tk)
    rowv = pl.BlockSpec((None, 1, tq), lambda h, s, it, jt, fl: (h, 0, it[s]))
    ktv = pl.BlockSpec((None, dq, tk), lambda h, s, it, jt, fl: (h, 0, jt[s]))
    whole = pl.BlockSpec((None, nq, dq, tq), lambda h, s, it, jt, fl: (h, 0, 0, 0))
    return _pcall(
        body, name="mla_attn_bwd",
        grid_spec=pltpu.PrefetchScalarGridSpec(
            num_scalar_prefetch=3, grid=(nh, tables[0].shape[0]),
            in_specs=[qs(dq), kv(dq), ktv, kv(dv), qs(dv), rowv, rowv], out_specs=(kv(dq), kv(dv), whole),
            scratch_shapes=[pltpu.VMEM((tk, dq), F32), pltpu.VMEM((tk, dv), F32)]),
        out_shape=(_sds((nh, t, dq), F32), _sds((nh, t, dv), F32), _sds((nh, nq, dq, tq), F32)),
        compiler_params=_cparams("parallel", "arbitrary"))(*tables, q, k, k_t, v, do, lse_row, dl_row)


def _loss_head(y, target, tm):
    t, d = y.shape
    tm = min(tm, t)
    nt = t // tm

    def body(y_ref, t_ref, dy_ref, loss_ref, acc):
        i = pl.program_id(0)
        err = y_ref[...] - t_ref[...]
        dy_ref[...] = err * (1.0 / d)

        @pl.when(i == 0)
        def _():
            acc[...] = jnp.zeros_like(acc)

        acc[...] += jnp.sum(err * err, axis=0, keepdims=True)

        @pl.when(i == nt - 1)
        def _():
            loss_ref[0, 0] = jnp.sum(acc[...]) * (0.5 / d)

    spec = pl.BlockSpec((tm, d), lambda i: (i, 0))
    return _pcall(
        body, name="loss_head", grid=(nt,), in_specs=[spec, spec],
        out_specs=(spec, pl.BlockSpec(memory_space=pltpu.SMEM)),
        out_shape=(_sds((t, d), F32), _sds((1, 1), F32)),
        scratch_shapes=[pltpu.VMEM((1, d), F32)],
        compiler_params=_cparams("arbitrary"))(y, target)


def _adamw(w, g, m, v, name):
    r, c = w.shape
    tr = r
    for cand in (256, 128, 64, 32, 16, 8):
        if r % cand == 0:
            tr = cand
            break

    def body(w_ref, g_ref, m_ref, v_ref, d_ref, nm_ref, nv_ref):
        gg = g_ref[...]
        nm = ADAM_B1 * m_ref[...] + (1.0 - ADAM_B1) * gg
        nv = ADAM_B2 * v_ref[...] + (1.0 - ADAM_B2) * (gg * gg)
        m_hat = nm / (1.0 - ADAM_B1 ** ADAM_STEP)
        v_hat = nv / (1.0 - ADAM_B2 ** ADAM_STEP)
        d_ref[...] = -ADAM_LR * (m_hat / (jnp.sqrt(v_hat) + ADAM_EPS) + ADAM_WD * w_ref[...])
        nm_ref[...] = nm
        nv_ref[...] = nv

    spec = pl.BlockSpec((tr, c), lambda i: (i, 0))
    sd = _sds((r, c), F32)
    return _pcall(body, name=name, grid=(r // tr,), in_specs=[spec] * 4, out_specs=(spec,) * 3,
                  out_shape=(sd, sd, sd), compiler_params=_cparams("parallel"))(w, g, m, v)


MESH_ID = pl.DeviceIdType.MESH
HBM_SPEC = pl.BlockSpec(memory_space=pltpu.HBM)


def _place():
    return lax.axis_index("x"), lax.axis_index("y"), lax.axis_index("c")


def _other_chips(x, y):
    return [(1 - x, y), (x, 1 - y), (1 - x, 1 - y)]


def _remote(src, dst, send_sems, recv_sems, k, to):
    return pltpu.make_async_remote_copy(src_ref=src, dst_ref=dst, send_sem=send_sems.at[k], recv_sem=recv_sems.at[k],
                                        device_id=to, device_id_type=MESH_ID)


def _halves(arrays):
    for a in arrays:
        assert a.shape[-2] % 32 == 0
    return [a.shape[-2] // 2 for a in arrays]


def _gather_start(srcs, outs, halves, send_sems, recv_sems):
    x, y, c = _place()
    for a, half in enumerate(halves):
        rows = pl.ds(c * half, half)
        for k, (cx, cy) in enumerate(_other_chips(x, y)):
            _remote(srcs[a].at[rows, :], outs[a].at[2 * x + y, rows, :], send_sems, recv_sems, 3 * a + k,
                    (cx, cy, c)).start()


def _gather_wait(outs, halves, send_sems, recv_sems):
    x, y, c = _place()
    for a, half in enumerate(halves):
        for k, (cx, cy) in enumerate(_other_chips(x, y)):
            got = outs[a].at[2 * cx + cy, pl.ds(c * half, half), :]
            _remote(got, got, send_sems, recv_sems, 3 * a + k, (x, y, c)).wait()


def _forward_cores(partly):
    n = len(partly)
    halves = _halves(partly)

    def body(*refs):
        srcs, outs, send_sems, recv_sems = refs[:n], refs[n:2 * n], refs[2 * n], refs[2 * n + 1]
        x, y, c = _place()
        for a, half in enumerate(halves):
            for k, (cx, cy) in enumerate(_other_chips(x, y)):
                rows = pl.ds(c * half, half)
                _remote(srcs[a].at[2 * cx + cy, rows, :], outs[a].at[2 * cx + cy, rows, :], send_sems, recv_sems,
                        3 * a + k, (x, y, 1 - c)).start()
        for a, half in enumerate(halves):
            for k, (cx, cy) in enumerate(_other_chips(x, y)):
                mine = outs[a].at[2 * cx + cy, pl.ds(c * half, half), :]
                theirs = outs[a].at[2 * cx + cy, pl.ds((1 - c) * half, half), :]
                _remote(mine, theirs, send_sems, recv_sems, 3 * a + k, (x, y, c)).wait()

    return _pcall(
        body, name="forward_cores", in_specs=[HBM_SPEC] * n, out_specs=tuple([HBM_SPEC] * n),
        out_shape=tuple(_sds(p.shape, p.dtype) for p in partly), input_output_aliases={a: a for a in range(n)},
        scratch_shapes=[pltpu.SemaphoreType.DMA((3 * n,)), pltpu.SemaphoreType.DMA((3 * n,))],
    )(*partly)


def _gather_weights(blocks):
    n = len(blocks)
    halves = _halves(blocks)

    def body(*refs):
        srcs, outs, send_sems, recv_sems = refs[:n], refs[n:2 * n], refs[2 * n], refs[2 * n + 1]
        x, y, c = _place()
        me = 2 * x + y
        sibling = (x, y, 1 - c)
        chips = _other_chips(x, y)

        def part(a, chip, core):
            return outs[a].at[chip, pl.ds(core * halves[a], halves[a]), :]

        for a in range(n):
            mine = srcs[a].at[pl.ds(c * halves[a], halves[a]), :]
            for k, (cx, cy) in enumerate(chips):
                _remote(mine, part(a, me, c), send_sems, recv_sems, 6 * a + k, (cx, cy, c)).start()
        for k, (cx, cy) in enumerate(chips):
            for a in range(n):
                got = part(a, 2 * cx + cy, c)
                _remote(got, got, send_sems, recv_sems, 6 * a + k, (x, y, c)).wait_recv()
                _remote(got, got, send_sems, recv_sems, 6 * a + 3 + k, sibling).start()
        for k, (cx, cy) in enumerate(chips):
            for a in range(n):
                got = part(a, 2 * cx + cy, 1 - c)
                _remote(got, got, send_sems, recv_sems, 6 * a + 3 + k, (x, y, c)).wait_recv()
        for a in range(n):
            sent = part(a, me, c)
            for k in range(6):
                _remote(sent, sent, send_sems, recv_sems, 6 * a + k, (x, y, c)).wait_send()

    return _pcall(
        body, name="gather_weights", in_specs=[HBM_SPEC] * n, out_specs=tuple([HBM_SPEC] * n),
        out_shape=tuple(_sds((N_CHIPS,) + b.shape, b.dtype) for b in blocks),
        scratch_shapes=[pltpu.SemaphoreType.DMA((6 * n,)), pltpu.SemaphoreType.DMA((6 * n,))],
    )(*blocks)


def _reduce_cores(grads, tag):
    n = len(grads)

    def body(*refs):
        gs, outs, send_sems, recv_sems = refs[:n], refs[n:2 * n], refs[2 * n], refs[2 * n + 1]
        _cores_start(gs, outs, send_sems, recv_sems)
        _cores_wait(gs, outs, send_sems, recv_sems)

    return _pcall(
        body, name=f"reduce_cores_{tag}", in_specs=[HBM_SPEC] * n, out_specs=tuple([HBM_SPEC] * n),
        out_shape=_cores_shapes(grads), scratch_shapes=_cores_sems(n),
    )(*grads)


def _cores_shapes(grads):
    return tuple(_sds((N_CHIPS, h, g.shape[2]), g.dtype) for g, h in zip(grads, _halves(grads)))


def _cores_sems(n):
    return [pltpu.SemaphoreType.DMA((n,)), pltpu.SemaphoreType.DMA((n,))]


def _cores_start(gs, outs, send_sems, recv_sems):
    x, y, c = _place()
    for a, g in enumerate(gs):
        half = g.shape[1] // 2
        for j in range(N_CHIPS):
            _remote(g.at[j, pl.ds((1 - c) * half, half), :], outs[a].at[j], send_sems, recv_sems, a,
                    (x, y, 1 - c)).start()


def _cores_wait(gs, outs, send_sems, recv_sems):
    x, y, c = _place()
    for a, g in enumerate(gs):
        half = g.shape[1] // 2
        _remote(g.at[:, pl.ds((1 - c) * half, half), :], outs[a], send_sems, recv_sems, a, (x, y, c)).wait()


def _scatter_shapes(parts):
    return tuple(_sds((3,) + p.shape[1:], p.dtype) for p in parts)


def _scatter_sems(n):
    return [pltpu.SemaphoreType.DMA((3 * n,)), pltpu.SemaphoreType.DMA((3 * n,))]


def _scatter_start(ps, outs, send_sems, recv_sems):
    x, y, c = _place()
    for a in range(len(ps)):
        for k, (cx, cy) in enumerate(_other_chips(x, y)):
            _remote(ps[a].at[2 * cx + cy], outs[a].at[k], send_sems, recv_sems, 3 * a + k, (cx, cy, c)).start()


def _scatter_wait(ps, outs, send_sems, recv_sems):
    x, y, c = _place()
    for a in range(len(ps)):
        for k in range(3):
            _remote(ps[a].at[k], outs[a].at[k], send_sems, recv_sems, 3 * a + k, (x, y, c)).wait()


EXCHANGES = {"chips": (_scatter_start, _scatter_wait, _scatter_shapes, _scatter_sems),
             "cores": (_cores_start, _cores_wait, _cores_shapes, _cores_sems)}


def _sum_partials(received, parts, place):
    n = len(parts)
    steps = 2
    tiles = [p.shape[1] // steps for p in parts]

    def body(place_ref, *refs):
        rs, ps, outs = refs[:n], refs[n:2 * n], refs[2 * n:]
        for a in range(n):
            tot = ps[a][...].astype(F32)
            for k in range(3):
                tot = tot + rs[a][k].astype(F32)
            outs[a][...] = tot

    cols = [p.shape[2] for p in parts]
    return _pcall(
        body, name="sum_chip_partials",
        grid_spec=pltpu.PrefetchScalarGridSpec(
            num_scalar_prefetch=1, grid=(steps,),
            in_specs=[pl.BlockSpec((3, tm, w), lambda i, pc: (0, i, 0)) for tm, w in zip(tiles, cols)]
            + [pl.BlockSpec((None, tm, w), lambda i, pc: (pc[0], i, 0)) for tm, w in zip(tiles, cols)],
            out_specs=tuple(pl.BlockSpec((tm, w), lambda i, pc: (pc[1] * steps + i, 0)) for tm, w in zip(tiles, cols))),
        out_shape=tuple(_sds((2 * p.shape[1], p.shape[2]), F32) for p in parts),
        compiler_params=_cparams("parallel"))(place, *received, *parts)


def _share_cores(blocks):
    n = len(blocks)
    halves = _halves(blocks)

    def body(*refs):
        srcs, outs, send_sems, recv_sems = refs[:n], refs[n:2 * n], refs[2 * n], refs[2 * n + 1]
        x, y, c = _place()
        for a in range(n):
            piece = pl.ds(c * halves[a], halves[a])
            _remote(srcs[a].at[piece, :], outs[a].at[piece, :], send_sems, recv_sems, a, (x, y, 1 - c)).start()
        for a in range(n):
            mine = outs[a].at[pl.ds(c * halves[a], halves[a]), :]
            theirs = outs[a].at[pl.ds((1 - c) * halves[a], halves[a]), :]
            _remote(mine, theirs, send_sems, recv_sems, a, (x, y, c)).wait()

    return _pcall(
        body, name="share_cores", in_specs=[HBM_SPEC] * n, out_specs=tuple([HBM_SPEC] * n),
        out_shape=tuple(_sds(b.shape, b.dtype) for b in blocks), input_output_aliases={a: a for a in range(n)},
        scratch_shapes=[pltpu.SemaphoreType.DMA((n,)), pltpu.SemaphoreType.DMA((n,))],
    )(*blocks)


def _sum_blocks(stacked, name, tm):
    n, rows, lanes = stacked.shape
    tm = min(tm, rows)

    def body(s_ref, o_ref):
        tot = s_ref[n - 1].astype(F32)
        for k in range(n - 1):
            tot = tot + s_ref[k].astype(F32)
        o_ref[...] = tot

    return _pcall(body, name=name, grid=(rows // tm,),
                  in_specs=[pl.BlockSpec((n, tm, lanes), lambda i: (0, i, 0))],
                  out_specs=pl.BlockSpec((tm, lanes), lambda i: (i, 0)), out_shape=_sds((rows, lanes), F32),
                  compiler_params=_cparams("parallel"))(stacked)


def _add_halves(grads, theirs, core, tag):
    n = len(grads)
    steps = 2
    tiles = [t.shape[1] // steps for t in theirs]
    cols = [t.shape[2] for t in theirs]

    def body(c_ref, *refs):
        gs, ts, outs = refs[:n], refs[n:2 * n], refs[2 * n:]
        for a in range(n):
            outs[a][...] = (gs[a][...] + ts[a][...]).astype(BF16)

    own = [pl.BlockSpec((None, tm, w), lambda k, i, c: (k, c[0] * steps + i, 0)) for tm, w in zip(tiles, cols)]
    same = [pl.BlockSpec((None, tm, w), lambda k, i, c: (k, i, 0)) for tm, w in zip(tiles, cols)]
    return _pcall(
        body, name=f"add_core_halves_{tag}",
        grid_spec=pltpu.PrefetchScalarGridSpec(
            num_scalar_prefetch=1, grid=(N_CHIPS, steps), in_specs=own + same, out_specs=tuple(same)),
        out_shape=tuple(_sds(t.shape, BF16) for t in theirs),
        compiler_params=_cparams("parallel", "parallel"))(core, *grads, *theirs)


def _allreduce_small(part):
    rows, lanes = part.shape
    ndev = 8

    def body(src, tot, buf, send_sems, recv_sems):
        x, y, c = _place()
        me = 4 * x + 2 * y + c
        buf[me] = src[...]
        sends = []
        for k in range(1, ndev):
            peer = (x ^ (k >> 2), y ^ ((k >> 1) & 1), c ^ (k & 1))
            cp = _remote(src, buf.at[me], send_sems, recv_sems, k - 1, peer)
            cp.start()
            sends.append(cp)
        for k in range(1, ndev):
            theirs = buf.at[me ^ k]
            _remote(theirs, theirs, send_sems, recv_sems, k - 1, (x, y, c)).wait_recv()
        for cp in sends:
            cp.wait_send()
        acc = buf[0]
        for d in range(1, ndev):
            acc = acc + buf[d]
        tot[...] = acc

    vm = pl.BlockSpec(memory_space=pltpu.VMEM)
    return _pcall(
        body, name="allreduce_small", in_specs=[vm], out_specs=vm, out_shape=_sds((rows, lanes), F32),
        scratch_shapes=[pltpu.VMEM((ndev, rows, lanes), F32), pltpu.SemaphoreType.DMA((ndev - 1,)),
                        pltpu.SemaphoreType.DMA((ndev - 1,))],
    )(part)


def _pack_small(vals):
    parts = []
    for name, shape, r in SMALL:
        flat = vals[name].reshape(-1).astype(F32)
        parts.append(jnp.pad(flat, (0, r * LANES - flat.shape[0])).reshape(r, LANES))
    used = sum(r for _, _, r in SMALL)
    parts.append(jnp.zeros((SMALL_ROWS - used, LANES), F32))
    return jnp.concatenate(parts, axis=0)


def _unpack_small(packed):
    out, off = {}, 0
    for name, shape, r in SMALL:
        n = int(np.prod(shape))
        out[name] = packed[off:off + r].reshape(-1)[:n].reshape(shape)
        off += r
    return out


def _heads_major(a, nh):
    t = a.shape[0]
    return a.reshape(t, nh, a.shape[1] // nh).transpose(1, 0, 2)


def _tokens_major(a):
    nh, t, w = a.shape
    return a.transpose(1, 0, 2).reshape(t, nh * w)


LATE = ("ffn1_w_gate", "ffn1_w_up", "ffn1_w_down")
EARLY = tuple(name for name, _ in BIG if name not in LATE)


def _local_step(x, target, small, wfull, exchanges=None, later_weights=None):
    t = x.shape[0]
    nh, hd = DIL_HEADS, DIL_HD
    grads_s, grads_b = {}, {}

    x1, ffn1_saved, partly = _ffn_fwd(x, small["ffn1_norm"], wfull["ffn1_w_gate"], wfull["ffn1_w_up"],
                                      wfull["ffn1_w_down"], "ffn1", later_weights[0] if later_weights else ())
    if later_weights:
        wfull = {**wfull, **later_weights[1](partly)}
    w_in = wfull["w_in"].transpose(1, 0, 2).reshape(D_MODEL, -1)
    w_out = wfull["w_out"].reshape(D_MODEL, D_MODEL)
    w_qb, w_kvb = wfull["mla_w_q_b"], wfull["mla_w_kv_b"]
    hm = _rms_fwd(x1, small["mix_norm"], BF16, "mix_norm", 512)
    proj = _mm_simple("in_proj", hm, w_in, NN, F32, tm=1024)
    cq, ckv, k_pe = proj[:, 1536:1792], proj[:, 1792:1920], proj[:, 1920:1984]

    gq, gk = jnp.tile(small["dil_q_norm"], (1, nh)), jnp.tile(small["dil_k_norm"], (1, nh))
    qn = _head_norm_fwd(proj, 0, gq, "dil_q_norm", 512)
    kn = _head_norm_fwd(proj, 1, gk, "dil_k_norm", 512)
    v_d = _head_norm_fwd(proj, 2, None, "dil_v_views", 512)
    bias = _bias_tiles(small["rel_bias"]).reshape(3, nh // 2, 2 * QB, QB + DIL_W)
    outs, lses = [], []
    for b, dil in enumerate(DIL_DILATIONS):
        o_b, lse_b = _dil_fwd(qn[b], kn[b], v_d[b], bias[b], dil, f"dil_fwd_{dil}")
        outs.append(o_b)
        lses.append(lse_b)
    o_dil, lse_tot, od = _dil_merge(outs, lses, small["out_norm_dil"], 512)

    mh = MLA_HEADS
    cos_t, sin_t = _rope_tables(t)
    cqn = _rms_fwd(cq, small["mla_q_a_norm"], BF16, "mla_q_a_norm", 512)
    ckvn = _rms_fwd(ckv, small["mla_kv_a_norm"], BF16, "mla_kv_a_norm", 512)
    tm = min(512, t)

    th = min(2048, t)

    def head_proj(name, a, w, width):
        k = a.shape[1]
        return _mm(name, (mh, t // th, 1),
                   [(a, pl.BlockSpec((th, k), lambda h, i, r: (i, 0)), w, pl.BlockSpec((None, k, width), lambda h, i, r: (h, 0, 0)))],
                   NN, _sds((mh, t, width), F32), pl.BlockSpec((None, th, width), lambda h, i, r: (h, i, 0)), (th, width))

    q_raw = head_proj("mla_q_proj", cqn, w_qb, MLA_QK)
    kv_raw = head_proj("mla_kv_proj", ckvn, w_kvb, MLA_NOPE + MLA_V)
    q_raw2, kv_raw2 = q_raw.reshape(mh * t, MLA_QK), kv_raw.reshape(mh * t, MLA_NOPE + MLA_V)
    q_scale = MLA_QK ** -0.5
    q_m = _mla_qk_fwd(q_raw2, small["mla_q_norm"], cos_t, sin_t, q_scale, "mla_q_rope", 2048).reshape(mh, t, MLA_QK)
    k_m, v_m = _mla_qk_fwd(kv_raw2, small["mla_k_norm"], cos_t, sin_t, 1.0, "mla_k_rope", 2048, pe=k_pe)
    k_m, v_m = k_m.reshape(mh, t, MLA_QK), v_m.reshape(mh, t, MLA_V)
    o_mla_h, lse_m = _mla_fwd(q_m, k_m, v_m, 512, 4096)
    o_mla = _tokens_major(o_mla_h)

    om = _rms_fwd(o_mla, small["out_norm_mla"], BF16, "out_norm_mla", 512)
    half_w = DIL_WIDTH
    row = pl.BlockSpec((tm, D_MODEL), lambda i, j, r: (i, 0))
    act_spec = pl.BlockSpec((tm, half_w), lambda i, j, r: (i, 0))
    x2 = _mm("out_proj", (t // tm, 1, 1),
             [(od, act_spec, w_out, pl.BlockSpec((half_w, D_MODEL), lambda i, j, r: (0, 0))),
              (om, act_spec, w_out, pl.BlockSpec((half_w, D_MODEL), lambda i, j, r: (1, 0)))],
             NN, _sds((t, D_MODEL), F32), row, (tm, D_MODEL), res=(x1, row))
    x3, ffn2_saved, _ = _ffn_fwd(x2, small["ffn2_norm"], wfull["ffn2_w_gate"], wfull["ffn2_w_up"],
                                 wfull["ffn2_w_down"], "ffn2")
    dy, loss = _loss_head(x3, target, 512)

    dx2, grads_s["ffn2_norm"], grads_b["ffn2_w_gate"], grads_b["ffn2_w_up"], grads_b["ffn2_w_down"], _, _ = _ffn_bwd(
        dy, x2, small["ffn2_norm"], wfull["ffn2_w_gate"], wfull["ffn2_w_up"], wfull["ffn2_w_down"], ffn2_saved, "ffn2")

    d_ocat = _mm_simple("out_proj_dx", dx2, w_out, NT, F32, tm=1024)
    dw_out_d = _mm_simple("out_proj_dw_dil", od, dx2, TN, F32, tk=2048)
    dw_out_m = _mm_simple("out_proj_dw_mla", om, dx2, TN, F32, tk=2048)
    grads_b["w_out"] = jnp.concatenate([dw_out_d, dw_out_m], axis=0).reshape(N_CHIPS, D_MODEL // N_CHIPS, D_MODEL)
    do_dil, grads_s["out_norm_dil"] = _rms_bwd([d_ocat[:, :half_w]], o_dil, small["out_norm_dil"], None, "out_norm_dil_bwd", 512)
    do_mla, grads_s["out_norm_mla"] = _rms_bwd([d_ocat[:, half_w:]], o_mla, small["out_norm_mla"], None, "out_norm_mla_bwd", 512)

    do_m = _heads_major(do_mla, mh)
    dl_m = _rowdot(do_m.reshape(mh * t, MLA_V), o_mla_h.reshape(mh * t, MLA_V), "mla_delta", 2048).reshape(mh, t, 1)
    dk_m, dv_m, dq_t = _mla_bwd(q_m, k_m, k_m.transpose(0, 2, 1), v_m, do_m, lse_m.reshape(mh, 1, t),
                                dl_m.reshape(mh, 1, t), 2048, 512)
    dq_m = dq_t.transpose(0, 1, 3, 2).reshape(mh, t, MLA_QK)
    dq_raw, grads_s["mla_q_norm"] = _mla_qk_bwd(dq_m.reshape(mh * t, MLA_QK), q_raw2, small["mla_q_norm"],
                                                 cos_t, sin_t, q_scale, "mla_q_rope_bwd", 2048)
    dk_nope, dk_pe_h, grads_s["mla_k_norm"] = _mla_qk_bwd(dk_m.reshape(mh * t, MLA_QK), kv_raw2, small["mla_k_norm"],
                                                          cos_t, sin_t, 1.0, "mla_k_rope_bwd", 2048, pe=k_pe)
    dq_raw = dq_raw.reshape(mh, t, MLA_QK)
    dk_nope = dk_nope.reshape(mh, t, MLA_NOPE)

    def head_proj_dx(name, d, w):
        width, k = d.shape[2], w.shape[1]
        pairs = [(d, pl.BlockSpec((None, th, width), lambda i, j, r, h=h: (h, i, 0)),
                  w, pl.BlockSpec((None, k, width), lambda i, j, r, h=h: (h, 0, 0))) for h in range(mh)]
        return _mm(name, (t // th, 1, 1), pairs, NT, _sds((t, k), F32),
                   pl.BlockSpec((th, k), lambda i, j, r: (i, 0)), (th, k))

    def head_proj_dw(name, a, d):
        width, k = d.shape[2], a.shape[1]
        return _mm(name, (mh, 1, t // th),
                   [(a, pl.BlockSpec((th, k), lambda h, j, r: (r, 0)), d, pl.BlockSpec((None, th, width), lambda h, j, r: (h, r, 0)))],
                   TN, _sds((mh, k, width), F32), pl.BlockSpec((None, k, width), lambda h, j, r: (h, 0, 0)), (k, width))

    d_cqn = head_proj_dx("mla_q_proj_dx", dq_raw, w_qb)
    kv_pairs = []
    for h in range(mh):
        for part, d_part in enumerate((dk_nope, dv_m)):
            kv_pairs.append((d_part, pl.BlockSpec((None, th, MLA_NOPE), lambda i, j, r, h=h: (h, i, 0)),
                             w_kvb, pl.BlockSpec((None, MLA_KV_RANK, MLA_NOPE), lambda i, j, r, h=h, part=part: (h, 0, part))))
    d_ckvn = _mm("mla_kv_proj_dx", (t // th, 1, 1), kv_pairs, NT, _sds((t, MLA_KV_RANK), F32),
                 pl.BlockSpec((th, MLA_KV_RANK), lambda i, j, r: (i, 0)), (th, MLA_KV_RANK))
    grads_b["mla_w_q_b"] = head_proj_dw("mla_q_proj_dw", cqn, dq_raw)
    grads_b["mla_w_kv_b"] = jnp.concatenate([head_proj_dw("mla_k_proj_dw", ckvn, dk_nope),
                                             head_proj_dw("mla_v_proj_dw", ckvn, dv_m)], axis=2)
    d_cq, grads_s["mla_q_a_norm"] = _rms_bwd([d_cqn], cq, small["mla_q_a_norm"], None, "mla_q_a_norm_bwd", 512)
    d_ckv, grads_s["mla_kv_a_norm"] = _rms_bwd([d_ckvn], ckv, small["mla_kv_a_norm"], None, "mla_kv_a_norm_bwd", 512)
    d_kpe = _sum_blocks(dk_pe_h.reshape(mh, t * MLA_ROPE // LANES, LANES), "mla_kpe_sum", 1024).reshape(t, MLA_ROPE)

    stats, do_db = _dil_stats(do_dil, o_dil, lse_tot, 512)
    dqs, dks, dvs, dtiles = [], [], [], []
    for b, dil in enumerate(DIL_DILATIONS):
        dq_b, dk_b, dv_b, db_b = _dil_bwd(qn[b], kn[b], v_d[b], do_db[b], stats[b], bias[b], dil, f"dil_bwd_{dil}")
        dqs.append(dq_b)
        dks.append(dk_b)
        dvs.append(dv_b)
        dtiles.append(db_b)
    grads_s["rel_bias"] = _bias_grad(jnp.stack(dtiles).reshape(3, nh, QB, QB + DIL_W))
    dq_a, dgq = _head_norm_bwd(dqs, proj, 0, gq, "dil_q_norm_bwd", 512)
    dk_a, dgk = _head_norm_bwd(dks, proj, 1, gk, "dil_k_norm_bwd", 512)
    grads_s["dil_q_norm"], grads_s["dil_k_norm"] = dgq[:, :hd], dgk[:, :hd]
    dv_a = _sum_branches(dvs, "dil_dv_sum", 512)

    dparts = [dq_a, dk_a, dv_a, d_cq, d_ckv, d_kpe]
    t2 = min(512, t)
    pairs, dw_parts, lo = [], [], 0
    for n, dpart in enumerate(dparts):
        width = dpart.shape[1]
        w_part = w_in[:, lo:lo + width]
        pairs.append((dpart, pl.BlockSpec((t2, width), lambda i, j, r: (i, 0)),
                      w_part, pl.BlockSpec((D_MODEL, width), lambda i, j, r: (0, 0))))
        dw_parts.append(_mm_simple(f"in_proj_dw_{n}", hm, dpart, TN, F32, tk=2048))
        lo += width
    dw_in = jnp.concatenate(dw_parts, axis=1)
    grads_b["w_in"] = dw_in.reshape(D_MODEL, N_CHIPS, -1).transpose(1, 0, 2)
    early = tuple(grads_b[n] for n in EARLY) if exchanges else ()
    row2 = pl.BlockSpec((t2, D_MODEL), lambda i, j, r: (i, 0))
    res = _mm("in_proj_dx", (t // t2, 1, 1), pairs, NT, _sds((t, D_MODEL), F32), row2, (t2, D_MODEL),
              res=(dx2, row2), norm=(x1, small["mix_norm"]), outgoing=early, exchange="cores")
    dx1, grads_s["mix_norm"] = res[0], res[1]
    outgoing = exchanges[0](early, res[2]) if exchanges else ()
    dx, grads_s["ffn1_norm"], grads_b["ffn1_w_gate"], grads_b["ffn1_w_up"], grads_b["ffn1_w_down"], arrived, late = _ffn_bwd(
        dx1, x, small["ffn1_norm"], wfull["ffn1_w_gate"], wfull["ffn1_w_up"], wfull["ffn1_w_down"], ffn1_saved, "ffn1",
        outgoing, exchanges[1] if exchanges else None)
    return loss, dx, grads_s, grads_b, (tuple(outgoing), arrived), late


def kernel(x, ffn1_norm, ffn1_w_gate, ffn1_w_up, ffn1_w_down, mix_norm, w_in, dil_q_norm, dil_k_norm, rel_bias, mla_q_a_norm, mla_w_q_b, mla_kv_a_norm, mla_w_kv_b, mla_q_norm, mla_k_norm, out_norm_dil, out_norm_mla, w_out, ffn2_norm, ffn2_w_gate, ffn2_w_up, ffn2_w_down, loss_target, m_ffn1_norm, m_ffn1_w_gate, m_ffn1_w_up, m_ffn1_w_down, m_mix_norm, m_w_in, m_dil_q_norm, m_dil_k_norm, m_rel_bias, m_mla_q_a_norm, m_mla_w_q_b, m_mla_kv_a_norm, m_mla_w_kv_b, m_mla_q_norm, m_mla_k_norm, m_out_norm_dil, m_out_norm_mla, m_w_out, m_ffn2_norm, m_ffn2_w_gate, m_ffn2_w_up, m_ffn2_w_down, v_ffn1_norm, v_ffn1_w_gate, v_ffn1_w_up, v_ffn1_w_down, v_mix_norm, v_w_in, v_dil_q_norm, v_dil_k_norm, v_rel_bias, v_mla_q_a_norm, v_mla_w_q_b, v_mla_kv_a_norm, v_mla_w_kv_b, v_mla_q_norm, v_mla_k_norm, v_out_norm_dil, v_out_norm_mla, v_w_out, v_ffn2_norm, v_ffn2_w_gate, v_ffn2_w_up, v_ffn2_w_down):
    given = dict(locals())
    big_names = [name for name, _ in BIG]
    small_names = [name for name, _, _ in SMALL]

    chip = (2 * lax.axis_index("x") + lax.axis_index("y")).astype(jnp.int32)
    core = lax.axis_index("c").astype(jnp.int32)
    mine = {n: given[n].astype(BF16) for n in big_names}

    def with_own(names, arrays):
        return {n: lax.dynamic_update_slice(a, mine[n], (chip, 0, 0)) for n, a in zip(names, arrays)}

    wfirst = with_own(LATE, _gather_weights([mine[n][0] for n in LATE]))
    later_weights = ([mine[n][0] for n in EARLY], lambda partly: with_own(EARLY, _forward_cores(partly)))
    small = {n: given[n] for n in small_names}

    def early_partials(partial, theirs):
        return _add_halves(partial, theirs, core.reshape(1), "early")

    def late_partials(partial):
        return _add_halves(partial, _reduce_cores(partial, "late"), core.reshape(1), "late")

    exchanges = (early_partials, late_partials)
    loss, dx, grads_s, grads_b, (early_part, early_got), (late_part, late_got) = _local_step(
        x[0], loss_target[0], small, wfirst, exchanges, later_weights)
    loss = lax.psum(loss[0, 0], ("x", "y", "c"))
    reduced = _sum_partials(tuple(late_got) + tuple(early_got), tuple(late_part) + tuple(early_part),
                            jnp.stack([chip, core]))
    g_big = dict(zip(LATE + EARLY, _share_cores(reduced)))
    g_small = _unpack_small(_allreduce_small(_pack_small(grads_s)))

    grad, delta, new_m, new_v = {}, {}, {}, {}
    for name, shape in BIG:
        g2 = g_big[name]
        d_, m_, v_ = _adamw(given[name].reshape(shape), g2, given["m_" + name].reshape(shape),
                            given["v_" + name].reshape(shape), f"adamw_{name}")
        full = given[name].shape
        grad[name], delta[name], new_m[name], new_v[name] = (a.reshape(full) for a in (g2, d_, m_, v_))
    for name in small_names:
        grad[name] = g_small[name]
        delta[name], new_m[name], new_v[name] = _adamw(given[name], g_small[name], given["m_" + name],
                                                       given["v_" + name], f"adamw_{name}")

    return (loss, dx[None], *[grad[n] for n in WEIGHTS], *[delta[n] for n in WEIGHTS],
            *[new_m[n] for n in WEIGHTS], *[new_v[n] for n in WEIGHTS])
```

```python
import functools

import numpy as np
import jax
import jax.numpy as jnp
from jax import lax
from jax.experimental import pallas as pl
from jax.experimental.pallas import tpu as pltpu

F32 = jnp.float32
BF16 = jnp.bfloat16

D_MODEL = 1024
D_FF = 2816
N_CHIPS = 4
DIL_HEADS = 8
DIL_HD = 64
DIL_WIDTH = 512
DIL_DILATIONS = (1, 4, 16)
DIL_W = 128
QB = 128
MLA_HEADS = 4
MLA_NOPE = 128
MLA_ROPE = 64
MLA_QK = 192
MLA_V = 128
MLA_Q_RANK = 256
MLA_KV_RANK = 128
ROPE_BASE = 10000.0
REL_BUCKETS = 32
REL_MAX_DIST = 2048
FFN_RESID = 0.5
EPS = 1e-6
NEG = -1e30
LANES = 128

ADAM_LR = 0.001
ADAM_B1 = 0.9
ADAM_B2 = 0.999
ADAM_EPS = 1e-08
ADAM_WD = 0.01
ADAM_STEP = 10

NT = (((1,), (1,)), ((), ()))
NN = (((1,), (0,)), ((), ()))
TN = (((0,), (0,)), ((), ()))

BIG = (
    ("ffn1_w_gate", (D_MODEL, D_FF // N_CHIPS)),
    ("ffn1_w_up", (D_MODEL, D_FF // N_CHIPS)),
    ("ffn1_w_down", (D_FF // N_CHIPS, D_MODEL)),
    ("w_in", (D_MODEL, 1984 // N_CHIPS)),
    ("mla_w_q_b", (MLA_Q_RANK, MLA_QK)),
    ("mla_w_kv_b", (MLA_KV_RANK, MLA_NOPE + MLA_V)),
    ("w_out", (D_MODEL // N_CHIPS, D_MODEL)),
    ("ffn2_w_gate", (D_MODEL, D_FF // N_CHIPS)),
    ("ffn2_w_up", (D_MODEL, D_FF // N_CHIPS)),
    ("ffn2_w_down", (D_FF // N_CHIPS, D_MODEL)),
)
SMALL = (
    ("ffn1_norm", (1, 1024), 8), ("mix_norm", (1, 1024), 8), ("dil_q_norm", (1, 64), 1),
    ("dil_k_norm", (1, 64), 1), ("rel_bias", (8, 32), 2), ("mla_q_a_norm", (1, 256), 2),
    ("mla_kv_a_norm", (1, 128), 1), ("mla_q_norm", (1, 192), 2), ("mla_k_norm", (1, 192), 2),
    ("out_norm_dil", (1, 512), 4), ("out_norm_mla", (1, 512), 4), ("ffn2_norm", (1, 1024), 8),
)
SMALL_ROWS = 48
WEIGHTS = ("ffn1_norm", "ffn1_w_gate", "ffn1_w_up", "ffn1_w_down", "mix_norm", "w_in", "dil_q_norm",
           "dil_k_norm", "rel_bias", "mla_q_a_norm", "mla_w_q_b", "mla_kv_a_norm", "mla_w_kv_b",
           "mla_q_norm", "mla_k_norm", "out_norm_dil", "out_norm_mla", "w_out", "ffn2_norm",
           "ffn2_w_gate", "ffn2_w_up", "ffn2_w_down")


def _pcall(body, **kw):
    return pl.pallas_call(body, **kw)


def _cparams(*sem):
    return pltpu.CompilerParams(dimension_semantics=sem)


def _sds(shape, dtype):
    return jax.ShapeDtypeStruct(shape, dtype)


def _dot(a, b, dn):
    return lax.dot_general(a, b, dn, preferred_element_type=F32)


def _rms_fwd(x, g, out_dtype, name, tm):
    n, d = x.shape
    tm = min(tm, n)

    def body(x_ref, g_ref, o_ref):
        xf = x_ref[...].astype(F32)
        r = lax.rsqrt(jnp.mean(xf * xf, axis=-1, keepdims=True) + EPS)
        o_ref[...] = (xf * r * g_ref[...]).astype(o_ref.dtype)

    return _pcall(
        body, name=name, grid=(n // tm,),
        in_specs=[pl.BlockSpec((tm, d), lambda i: (i, 0)), pl.BlockSpec((1, d), lambda i: (0, 0))],
        out_specs=pl.BlockSpec((tm, d), lambda i: (i, 0)),
        out_shape=_sds((n, d), out_dtype), compiler_params=_cparams("parallel"))(x, g)


def _rms_bwd(dys, x, g, res, name, tm):
    n, d = x.shape
    tm = min(tm, n)
    nd = len(dys)
    has_res = res is not None

    def body(*refs):
        dy_refs = refs[:nd]
        x_ref, g_ref = refs[nd], refs[nd + 1]
        res_ref = refs[nd + 2] if has_res else None
        dx_ref, dg_ref = refs[-2], refs[-1]
        dy = dy_refs[0][...].astype(F32)
        for r_ in dy_refs[1:]:
            dy = dy + r_[...].astype(F32)
        xf = x_ref[...].astype(F32)
        r = lax.rsqrt(jnp.mean(xf * xf, axis=-1, keepdims=True) + EPS)
        xh = xf * r
        dxh = dy * g_ref[...]
        dx = r * (dxh - xh * jnp.mean(dxh * xh, axis=-1, keepdims=True))
        if has_res:
            dx = dx + res_ref[...]
        dx_ref[...] = dx

        @pl.when(pl.program_id(0) == 0)
        def _():
            dg_ref[...] = jnp.zeros_like(dg_ref)

        dg_ref[...] += jnp.sum(dy * xh, axis=0, keepdims=True)

    row = pl.BlockSpec((tm, d), lambda i: (i, 0))
    vec = pl.BlockSpec((1, d), lambda i: (0, 0))
    ins = list(dys) + [x, g] + ([res] if has_res else [])
    return _pcall(
        body, name=name, grid=(n // tm,),
        in_specs=[row] * nd + [row, vec] + ([row] if has_res else []),
        out_specs=(row, vec),
        out_shape=(_sds((n, d), F32), _sds((1, d), F32)),
        compiler_params=_cparams("arbitrary"))(*ins)


def _mm(name, grid, pairs, dn, out_shape, out_spec, acc_shape, res=None, scale=1.0, outgoing=(), norm=None,
        exchange="chips"):
    npairs = len(pairs)
    nred = grid[2]
    has_res = res is not None
    has_norm = norm is not None
    no = len(outgoing)
    ex_start, ex_wait, ex_shapes, ex_sems = EXCHANGES[exchange]

    def body(*refs):
        ab = refs[:2 * npairs]
        res_ref = refs[2 * npairs] if has_res else None
        nin = 2 * npairs + int(has_res) + 2 * int(has_norm)
        if has_norm:
            x_ref, g_ref = refs[nin - 2:nin]
        first_out = nin + no
        sent = refs[nin:first_out]
        o_ref = refs[first_out]
        nout = 1 + int(has_norm)
        dg_ref = refs[first_out + 1] if has_norm else None
        arrived = refs[first_out + nout:first_out + nout + no]
        acc_ref = refs[first_out + nout + no] if nred > 1 else None
        if no:
            send_sems, recv_sems = refs[-2:]
            ids = [pl.program_id(n) for n in range(3)]

            @pl.when((ids[0] == 0) & (ids[1] == 0) & (ids[2] == 0))
            def _():
                ex_start(sent, arrived, send_sems, recv_sems)

        tot = None
        for p in range(npairs):
            d = _dot(ab[2 * p][...].astype(BF16), ab[2 * p + 1][...].astype(BF16), dn)
            tot = d if tot is None else tot + d

        def finish(v):
            if scale != 1.0:
                v = v * scale
            if has_norm:
                xf = x_ref[...]
                r = lax.rsqrt(jnp.mean(xf * xf, axis=-1, keepdims=True) + EPS)
                xh = xf * r
                dxh = v * g_ref[...]

                @pl.when(pl.program_id(0) == 0)
                def _():
                    dg_ref[...] = jnp.zeros_like(dg_ref)

                dg_ref[...] += jnp.sum(v * xh, axis=0, keepdims=True)
                v = r * (dxh - xh * jnp.mean(dxh * xh, axis=-1, keepdims=True))
            if has_res:
                v = res_ref[...] + v
            o_ref[...] = v.astype(o_ref.dtype)

        if nred == 1:
            finish(tot)
        else:
            r = pl.program_id(2)

            @pl.when(r == 0)
            def _():
                acc_ref[...] = tot

            @pl.when(r > 0)
            def _():
                acc_ref[...] += tot

            @pl.when(r == nred - 1)
            def _():
                finish(acc_ref[...])

        if no:
            @pl.when((ids[0] == grid[0] - 1) & (ids[1] == grid[1] - 1) & (ids[2] == nred - 1))
            def _():
                ex_wait(sent, arrived, send_sems, recv_sems)

    ins, specs = [], []
    for a, a_spec, b, b_spec in pairs:
        ins += [a, b]
        specs += [a_spec, b_spec]
    if has_res:
        ins.append(res[0])
        specs.append(res[1])
    scratch = [pltpu.VMEM(acc_shape, F32)] if nred > 1 else []
    if not no and not has_norm:
        return _pcall(
            body, name=name, grid=grid, in_specs=specs, out_specs=out_spec, out_shape=out_shape,
            scratch_shapes=scratch, compiler_params=_cparams("parallel", "parallel", "arbitrary"))(*ins)
    out_specs, out_shapes = (out_spec,), (out_shape,)
    if has_norm:
        assert grid[1] == 1
        d = norm[1].shape[1]
        ins += [norm[0], norm[1]]
        specs += [out_spec, pl.BlockSpec((1, d), lambda i, j, r: (0, 0))]
        out_specs += (pl.BlockSpec((1, d), lambda i, j, r: (0, 0)),)
        out_shapes += (_sds((1, d), F32),)
    hbm = pl.BlockSpec(memory_space=pltpu.HBM)
    res_ = tuple(_pcall(
        body, name=name, grid=grid, in_specs=specs + [hbm] * no, out_specs=out_specs + (hbm,) * no,
        out_shape=out_shapes + ex_shapes(outgoing),
        scratch_shapes=scratch + (ex_sems(no) if no else []),
        compiler_params=_cparams("arbitrary", "arbitrary", "arbitrary"))(*ins, *outgoing))
    nout = len(out_shapes)
    return res_[:nout] + ((res_[nout:],) if no else ())


def _ffn_up(h, wg, wu, name, tm, incoming=()):
    t, d = h.shape
    nc, _, fs = wg.shape
    tm = min(tm, t)
    nt = t // tm
    ni = len(incoming)
    halves = _halves(incoming)

    def body(*refs):
        h_ref, wg_ref, wu_ref = refs[:3]
        srcs = refs[3:3 + ni]
        g_ref, u_ref, a_ref = refs[3 + ni:6 + ni]
        outs = refs[6 + ni:6 + 2 * ni]
        if ni:
            send_sems, recv_sems = refs[6 + 2 * ni:]
            c, i = pl.program_id(0), pl.program_id(1)

            @pl.when((c == 0) & (i == 0))
            def _():
                _gather_start(srcs, outs, halves, send_sems, recv_sems)

        hh = h_ref[...]
        gate = _dot(hh, wg_ref[...], NN)
        up = _dot(hh, wu_ref[...], NN)
        sig = jax.nn.sigmoid(gate)
        silu = gate * sig
        g_ref[...] = (up * (sig + silu * (1.0 - sig))).astype(BF16)
        u_ref[...] = silu.astype(BF16)
        a_ref[...] = (silu * up).astype(BF16)

        if ni:
            @pl.when((c == nc - 1) & (i == nt - 1))
            def _():
                _gather_wait(outs, halves, send_sems, recv_sems)

    wspec = pl.BlockSpec((None, d, fs), lambda c, i: (c, 0, 0))
    ospec = pl.BlockSpec((None, tm, fs), lambda c, i: (c, i, 0))
    hbm = pl.BlockSpec(memory_space=pltpu.HBM)
    osd = _sds((nc, t, fs), BF16)
    res = tuple(_pcall(
        body, name=name, grid=(nc, nt),
        in_specs=[pl.BlockSpec((tm, d), lambda c, i: (i, 0)), wspec, wspec] + [hbm] * ni,
        out_specs=(ospec, ospec, ospec) + (hbm,) * ni,
        out_shape=(osd, osd, osd) + tuple(_sds((N_CHIPS,) + b.shape, b.dtype) for b in incoming),
        scratch_shapes=[pltpu.SemaphoreType.DMA((3 * ni,)), pltpu.SemaphoreType.DMA((3 * ni,))] if ni else [],
        compiler_params=_cparams("arbitrary", "arbitrary"))(h, wg, wu, *incoming))
    return res[:3] + (res[3:],)


def _ffn_hidden_bwd(dy, h, wd, dact_dgate, dact_dup, act, name, tm, outgoing=()):
    t, d = dy.shape
    nc, fs, _ = wd.shape
    tm = min(tm, t)
    nt = t // tm
    no = len(outgoing)

    def body(*refs):
        dy_ref, h_ref, wd_ref, g_ref, u_ref, a_ref = refs[:6]
        sent = refs[6:6 + no]
        dg_ref, du_ref, dwg_hbm, dwu_hbm, dwd_hbm = refs[6 + no:11 + no]
        arrived = refs[11 + no:11 + 2 * no]
        wg_acc, wu_acc, wd_acc, sem = refs[11 + 2 * no:15 + 2 * no]
        c, i = pl.program_id(0), pl.program_id(1)
        if no:
            send_sems, recv_sems = refs[15 + 2 * no:]

            @pl.when((c == 0) & (i == 0))
            def _():
                _scatter_start(sent, arrived, send_sems, recv_sems)

        dyb = dy_ref[...].astype(BF16)
        da = _dot(dyb, wd_ref[...], NT) * FFN_RESID
        dgate = (da * g_ref[...].astype(F32)).astype(BF16)
        dup = (da * u_ref[...].astype(F32)).astype(BF16)
        dg_ref[...] = dgate
        du_ref[...] = dup
        hh = h_ref[...]
        parts = (_dot(hh, dgate, TN), _dot(hh, dup, TN), _dot(a_ref[...], dyb, TN) * FFN_RESID)
        accs = (wg_acc, wu_acc, wd_acc)

        @pl.when(i == 0)
        def _():
            for acc, part in zip(accs, parts):
                acc[...] = part

        @pl.when(i > 0)
        def _():
            for acc, part in zip(accs, parts):
                acc[...] += part

        @pl.when(i == nt - 1)
        def _():
            copies = [pltpu.make_async_copy(acc, out.at[c], sem.at[n])
                      for n, (acc, out) in enumerate(zip(accs, (dwg_hbm, dwu_hbm, dwd_hbm)))]
            for cp in copies:
                cp.start()
            for cp in copies:
                cp.wait()

        if no:
            @pl.when((c == nc - 1) & (i == nt - 1))
            def _():
                _scatter_wait(sent, arrived, send_sems, recv_sems)

    tok = pl.BlockSpec((tm, d), lambda c, i: (i, 0))
    cspec = pl.BlockSpec((None, tm, fs), lambda c, i: (c, i, 0))
    hbm = pl.BlockSpec(memory_space=pltpu.HBM)
    osd = _sds((nc, t, fs), BF16)
    res = _pcall(
        body, name=name, grid=(nc, nt),
        in_specs=[tok, tok, pl.BlockSpec((None, fs, d), lambda c, i: (c, 0, 0)), cspec, cspec, cspec] + [hbm] * no,
        out_specs=(cspec, cspec, hbm, hbm, hbm) + (hbm,) * no,
        out_shape=(osd, osd, _sds((nc, d, fs), F32), _sds((nc, d, fs), F32), _sds((nc, fs, d), F32))
        + _scatter_shapes(outgoing),
        scratch_shapes=[pltpu.VMEM((d, fs), F32), pltpu.VMEM((d, fs), F32), pltpu.VMEM((fs, d), F32),
                        pltpu.SemaphoreType.DMA((3,))] + (_scatter_sems(no) if no else []),
        compiler_params=_cparams("arbitrary", "arbitrary"))(dy, h, wd, dact_dgate, dact_dup, act, *outgoing)
    res = tuple(res)
    return res[:5] + (res[5:],)


def _ffn_fwd(x, g, wg, wu, wd, tag, incoming=(), target=None):
    t = x.shape[0]
    nc, _, fs = wg.shape
    tm = min(512, t)
    h = _rms_fwd(x, g, BF16, f"{tag}_norm", 512)
    dact_dgate, dact_dup, act, partly = _ffn_up(h, wg, wu, f"{tag}_up", 1024, incoming)
    if target is not None:
        return _ffn_down_loss(act, wd, x, target, f"{tag}_down_loss", 512), (h, dact_dgate, dact_dup, act), partly
    pairs = [(act, pl.BlockSpec((None, tm, fs), lambda i, j, r, c=c: (c, i, 0)),
              wd, pl.BlockSpec((None, fs, D_MODEL), lambda i, j, r, c=c: (c, 0, 0))) for c in range(nc)]
    row = pl.BlockSpec((tm, D_MODEL), lambda i, j, r: (i, 0))
    y = _mm(f"{tag}_down", (t // tm, 1, 1), pairs, NN, _sds((t, D_MODEL), F32), row, (tm, D_MODEL),
            res=(x, row), scale=FFN_RESID)
    return y, (h, dact_dgate, dact_dup, act), partly


def _ffn_bwd(dy, x, g, wg, wu, wd, saved, tag, outgoing=(), own_exchange=None):
    h, dact_dgate, dact_dup, act = saved
    t = x.shape[0]
    nc, _, fs = wg.shape
    tm = min(512, t)
    dgate, dup, dwg, dwu, dwd, arrived = _ffn_hidden_bwd(dy, h, wd, dact_dgate, dact_dup, act,
                                                         f"{tag}_hidden_bwd", 1024, outgoing)
    pairs = []
    for c in range(nc):
        a_spec = pl.BlockSpec((None, tm, fs), lambda i, j, r, c=c: (c, i, 0))
        w_spec = pl.BlockSpec((None, D_MODEL, fs), lambda i, j, r, c=c: (c, 0, 0))
        pairs += [(dgate, a_spec, wg, w_spec), (dup, a_spec, wu, w_spec)]
    own_part = tuple(own_exchange([dwg, dwu, dwd])) if own_exchange else ()
    row = pl.BlockSpec((tm, D_MODEL), lambda i, j, r: (i, 0))
    res = _mm(f"{tag}_dh", (t // tm, 1, 1), pairs, NT, _sds((t, D_MODEL), F32), row, (tm, D_MODEL),
              res=(dy, row), norm=(x, g), outgoing=own_part)
    dx, dg = res[0], res[1]
    own_got = res[2] if own_part else ()
    return dx, dg, dwg, dwu, dwd, arrived, (own_part, own_got)


def _mm_simple(name, a, b, dn, out_dtype, tm=512, tk=512, res=None, scale=1.0):
    if dn == TN:
        k, m = a.shape
        n = b.shape[1]
        tk = min(tk, k)
        return _mm(name, (1, 1, k // tk),
                   [(a, pl.BlockSpec((tk, m), lambda i, j, r: (r, 0)), b, pl.BlockSpec((tk, n), lambda i, j, r: (r, 0)))],
                   TN, _sds((m, n), out_dtype), pl.BlockSpec((m, n), lambda i, j, r: (0, 0)), (m, n), scale=scale)
    m, k = a.shape
    n = b.shape[1] if dn == NN else b.shape[0]
    tm = min(tm, m)
    row = pl.BlockSpec((tm, n), lambda i, j, r: (i, 0))
    return _mm(name, (m // tm, 1, 1),
               [(a, pl.BlockSpec((tm, k), lambda i, j, r: (i, 0)), b, pl.BlockSpec(b.shape, lambda i, j, r: (0, 0)))],
               dn, _sds((m, n), out_dtype), row, (tm, n), res=None if res is None else (res, row), scale=scale)


def _t5_bucket(dist):
    max_exact = REL_BUCKETS // 2
    d = np.maximum(dist, 1).astype(np.float32)
    large = max_exact + (np.log(d / max_exact) / np.log(REL_MAX_DIST / max_exact)
                         * (REL_BUCKETS - max_exact)).astype(np.int32)
    large = np.minimum(large, REL_BUCKETS - 1)
    return np.where(dist < max_exact, dist, large).astype(np.int32)


def _bucket_tiles():
    i = np.arange(QB)[:, None]
    j = np.arange(QB + DIL_W)[None, :]
    delta = np.clip(i + DIL_W - j, 0, None)
    return np.stack([_t5_bucket(delta * dil) for dil in DIL_DILATIONS]).astype(np.int32)


def _bias_tiles(rel_bias):
    buckets = jnp.asarray(_bucket_tiles())

    def body(rb_ref, bk_ref, o_ref):
        bk = bk_ref[...]
        for h in range(DIL_HEADS):
            def pick(b, tile):
                return jnp.where(bk == b, rb_ref[h, b], tile)

            o_ref[h] = lax.fori_loop(0, REL_BUCKETS, pick, jnp.zeros((QB, QB + DIL_W), F32))

    return _pcall(
        body, name="dil_bias_tiles", grid=(3,),
        in_specs=[pl.BlockSpec(memory_space=pltpu.SMEM),
                  pl.BlockSpec((None, QB, QB + DIL_W), lambda b: (b, 0, 0))],
        out_specs=pl.BlockSpec((None, DIL_HEADS, QB, QB + DIL_W), lambda b: (b, 0, 0, 0)),
        out_shape=_sds((3, DIL_HEADS, QB, QB + DIL_W), F32),
        compiler_params=_cparams("parallel"))(rel_bias, buckets)


def _bias_grad(dtiles):
    buckets = jnp.asarray(_bucket_tiles())

    def body(dt_ref, bk_ref, o_ref):
        def one(b, carry):
            hit = [bk_ref[br] == b for br in range(3)]
            for h in range(DIL_HEADS):
                tot = jnp.zeros((), F32)
                for br in range(3):
                    tot = tot + jnp.sum(jnp.where(hit[br], dt_ref[br, h], 0.0))
                o_ref[h, b] = tot
            return carry

        lax.fori_loop(0, REL_BUCKETS, one, 0)

    return _pcall(
        body, name="dil_bias_grad",
        in_specs=[pl.BlockSpec(memory_space=pltpu.VMEM), pl.BlockSpec(memory_space=pltpu.VMEM)],
        out_specs=pl.BlockSpec(memory_space=pltpu.SMEM),
        out_shape=_sds((DIL_HEADS, REL_BUCKETS), F32))(dtiles, buckets)


def _split_heads(a, lo):
    zero = jnp.zeros_like(a)
    return jnp.concatenate([jnp.where(lo, a, zero), jnp.where(lo, zero, a)], axis=0)


def _side_by_side(a):
    n = a.shape[0] // 2
    return jnp.concatenate([a[:n], a[n:]], axis=1)


def _band_masks(prev_ok):
    ii = lax.broadcasted_iota(jnp.int32, (2 * QB, QB), 0) & (QB - 1)
    jj = lax.broadcasted_iota(jnp.int32, (2 * QB, QB), 1)
    return jj <= ii, jj >= ii + jnp.where(prev_ok, 0, QB)


def _dil_fwd(q, k, v, bias, dil, name):
    w = DIL_WIDTH
    t = q.shape[0] * dil
    npair = w // LANES
    nl = t // dil // QB
    scale = DIL_HD ** -0.5

    def body(q_ref, kc_ref, kp_ref, vc_ref, vp_ref, b_ref, o_ref, lse_ref):
        nn = pl.program_id(1)
        lo = lax.broadcasted_iota(jnp.int32, (QB, LANES), 1) < DIL_HD
        lo2 = lax.broadcasted_iota(jnp.int32, (2 * QB, LANES), 1) < DIL_HD
        ii = lax.broadcasted_iota(jnp.int32, (2 * QB, 2 * QB), 0) & (QB - 1)
        jj = lax.broadcasted_iota(jnp.int32, (2 * QB, 2 * QB), 1)
        first_key = jnp.maximum(ii, jnp.where(nn != 0, 0, QB))
        valid = (jj >= first_key) & (jj <= ii + QB)
        for p in range(npair):
            cols = slice(p * LANES, (p + 1) * LANES)
            qq = _split_heads(q_ref[:, cols], lo)
            kk = jnp.concatenate([kp_ref[:, cols], kc_ref[:, cols]], axis=0)
            vv = jnp.concatenate([vp_ref[:, cols], vc_ref[:, cols]], axis=0)
            s = jnp.where(valid, _dot(qq, kk, NT) * scale + b_ref[p], NEG)
            m = jnp.max(s, axis=-1, keepdims=True)
            e = jnp.exp(s - m)
            den = jnp.sum(e, axis=-1, keepdims=True)
            pn = (e * (1.0 / den)).astype(BF16)
            o_ref[:, cols] = _dot(_side_by_side(pn), _split_heads(vv, lo2), NN)
            lse = m + jnp.log(den)
            lse_ref[:, cols] = jnp.where(lo, lse[:QB], lse[QB:])

    cur = pl.BlockSpec((QB, w), lambda r, n: (n, r))
    prev = pl.BlockSpec((QB, w), lambda r, n: (jnp.maximum(n - 1, 0), r))
    sd = _sds((t // dil, dil * w), F32)
    return _pcall(
        body, name=name, grid=(dil, nl),
        in_specs=[cur, cur, prev, cur, prev, pl.BlockSpec((npair, 2 * QB, 2 * QB), lambda r, n: (0, 0, 0))],
        out_specs=(cur, cur), out_shape=(sd, sd),
        compiler_params=_cparams("parallel", "parallel"))(q, k, k, v, v, bias)


def _dil_bwd(q, k, v, do, stats, bias, dil, name):
    w = DIL_WIDTH
    t = q.shape[0] * dil
    npair = w // LANES
    nl = t // dil // QB
    scale = DIL_HD ** -0.5

    def body(qc_ref, qn_ref, doc_ref, don_ref, sc_ref, sn_ref, k_ref, v_ref, b_ref,
             dq_ref, dk_ref, dv_ref, db_ref, carry):
        r, nn = pl.program_id(0), pl.program_id(1)
        lo = lax.broadcasted_iota(jnp.int32, (QB, LANES), 1) < DIL_HD
        cur_ok, prev_ok = _band_masks(nn + 1 < nl)

        @pl.when((r == 0) & (nn == 0))
        def _():
            db_ref[...] = jnp.zeros_like(db_ref)
            carry[...] = jnp.zeros_like(carry)

        for p in range(npair):
            cols = slice(p * LANES, (p + 1) * LANES)
            kp, vp = k_ref[:, cols], v_ref[:, cols]
            k2 = _split_heads(kp, lo)

            def column(ref, lane):
                first = p * LANES + lane
                return jnp.concatenate([ref[:, first:first + 1], ref[:, first + DIL_HD:first + DIL_HD + 1]], axis=0)

            def side(q_ref, do_ref, s_ref, bias, ok):
                qq = _split_heads(q_ref[:, cols], lo)
                dd = _split_heads(do_ref[:, cols], lo)
                s = jnp.where(ok, _dot(qq, kp, NT) * scale + bias, NEG)
                prob = jnp.exp(s - column(s_ref, 0))
                ds = prob * (_dot(dd, vp, NT) - column(s_ref, DIL_HD // 2))
                return qq, dd, prob.astype(BF16), ds

            q1, d1, p1, ds1 = side(qc_ref, doc_ref, sc_ref, b_ref[p, :, QB:], cur_ok)
            q2, d2, p2, ds2 = side(qn_ref, don_ref, sn_ref, b_ref[p, :, :QB], prev_ok)
            ds1b, ds2b = ds1.astype(BF16), ds2.astype(BF16)
            dq_ref[:, cols] = carry[:, cols] + _dot(_side_by_side(ds1b), k2, NN) * scale
            carry[:, cols] = _dot(_side_by_side(ds2b), k2, NN) * scale
            dk_ref[:, cols] = _dot(jnp.concatenate([ds1b, ds2b], axis=0), jnp.concatenate([q1, q2], axis=0), TN) * scale
            dv_ref[:, cols] = _dot(jnp.concatenate([p1, p2], axis=0), jnp.concatenate([d1, d2], axis=0), TN)
            db_ref[p, :, QB:] += ds1
            db_ref[p, :, :QB] += ds2

    cur = pl.BlockSpec((QB, w), lambda r, n: (n, r))
    nxt = pl.BlockSpec((QB, w), lambda r, n: (jnp.minimum(n + 1, nl - 1), r))
    tile = pl.BlockSpec((npair, 2 * QB, 2 * QB), lambda r, n: (0, 0, 0))
    sd = _sds((t // dil, dil * w), F32)
    return _pcall(
        body, name=name, grid=(dil, nl),
        in_specs=[cur, nxt, cur, nxt, cur, nxt, cur, cur, tile],
        out_specs=(cur, cur, cur, tile),
        out_shape=(sd, sd, sd, _sds((npair, 2 * QB, 2 * QB), F32)),
        scratch_shapes=[pltpu.VMEM((QB, w), F32)],
        compiler_params=_cparams("arbitrary", "arbitrary"))(q, q, do, do, stats, stats, k, v, bias)


def _head_sum_matrix(scale):
    idx = np.arange(DIL_WIDTH) // DIL_HD
    return jnp.asarray((idx[:, None] == idx[None, :]).astype(np.float32) * scale, BF16)


def _head_sum(x, mat):
    hi = x.astype(BF16)
    lo = (x - hi.astype(F32)).astype(BF16)
    return _dot(hi, mat, NN) + _dot(lo, mat, NN)


def _to_views(src, tmp, out_refs):
    tm, w = src.shape
    for j in range(w // LANES):
        tmp[j] = src[:, j * LANES:(j + 1) * LANES]
    for d, o_ref in zip(DIL_DILATIONS, out_refs):
        if d == 1:
            o_ref[...] = src.astype(o_ref.dtype)
            continue
        for r in range(d):
            for j in range(w // LANES):
                lo = r * w + j * LANES
                o_ref[:, lo:lo + LANES] = tmp[j, pl.ds(r, tm // d, stride=d), :].astype(o_ref.dtype)


def _from_view(v_ref, tmp, d):
    tm = tmp.shape[1]
    w = v_ref.shape[1] // d
    for r in range(d):
        for j in range(w // LANES):
            lo = r * w + j * LANES
            tmp[j, pl.ds(r, tm // d, stride=d), :] = v_ref[:, lo:lo + LANES]
    return jnp.concatenate([tmp[j] for j in range(w // LANES)], axis=1)


def _view_specs(tm, t, dtype):
    specs = tuple(pl.BlockSpec((tm // d, d * DIL_WIDTH), lambda i: (i, 0)) for d in DIL_DILATIONS)
    shapes = tuple(_sds((t // d, d * DIL_WIDTH), dtype) for d in DIL_DILATIONS)
    return specs, shapes


def _view_scratch(tm):
    return pltpu.VMEM((DIL_WIDTH // LANES, tm, LANES), F32)


def _dil_merge(outs, lses, g, tm):
    w = DIL_WIDTH
    t = outs[0].shape[0]
    tm = min(tm, t)

    def body(o0, o1, o2, l0, l1, l2, g_ref, o_ref, l_ref, n_ref, so1, so2, sl1, sl2):
        d1, d2 = DIL_DILATIONS[1], DIL_DILATIONS[2]
        a0, a1, a2 = l0[...], _from_view(l1, sl1, d1), _from_view(l2, sl2, d2)
        m = jnp.maximum(jnp.maximum(a0, a1), a2)
        e0, e1, e2 = jnp.exp(a0 - m), jnp.exp(a1 - m), jnp.exp(a2 - m)
        den = e0 + e1 + e2
        o = (e0 * o0[...] + e1 * _from_view(o1, so1, d1) + e2 * _from_view(o2, so2, d2)) / den
        o_ref[...] = o
        l_ref[...] = m + jnp.log(den)
        r = lax.rsqrt(jnp.mean(o * o, axis=-1, keepdims=True) + EPS)
        n_ref[...] = (o * r * g_ref[...]).astype(n_ref.dtype)

    specs, _ = _view_specs(tm, t, F32)
    spec = pl.BlockSpec((tm, w), lambda i: (i, 0))
    return _pcall(
        body, name="dil_merge", grid=(t // tm,),
        in_specs=list(specs) * 2 + [pl.BlockSpec((1, w), lambda i: (0, 0))], out_specs=(spec, spec, spec),
        out_shape=(_sds((t, w), F32), _sds((t, w), F32), _sds((t, w), BF16)),
        scratch_shapes=[_view_scratch(tm)] * 4,
        compiler_params=_cparams("parallel"))(*outs, *lses, g)


def _dil_stats(do, o, lse, tm):
    t, w = do.shape
    tm = min(tm, t)

    def body(a_ref, b_ref, l_ref, m_ref, s1, s4, s16, d1, d4, d16, tmp):
        first = (lax.broadcasted_iota(jnp.int32, (tm, w), 1) & (DIL_HD - 1)) < DIL_HD // 2
        do_ = a_ref[...]
        _to_views(jnp.where(first, l_ref[...], _head_sum(do_ * b_ref[...], m_ref[...])), tmp, (s1, s4, s16))
        _to_views(do_, tmp, (d1, d4, d16))

    spec = pl.BlockSpec((tm, w), lambda i: (i, 0))
    f_specs, f_shapes = _view_specs(tm, t, F32)
    b_specs, b_shapes = _view_specs(tm, t, BF16)
    res = _pcall(body, name="dil_stats", grid=(t // tm,),
                 in_specs=[spec, spec, spec, pl.BlockSpec((w, w), lambda i: (0, 0))],
                 out_specs=f_specs + b_specs, out_shape=f_shapes + b_shapes,
                 scratch_shapes=[_view_scratch(tm)],
                 compiler_params=_cparams("parallel"))(do, o, lse, _head_sum_matrix(1.0))
    return res[:3], res[3:]


def _head_norm_fwd(x, col, g, name, tm):
    t = x.shape[0]
    w = DIL_WIDTH
    tm = min(tm, t)
    normed = g is not None

    def body(*refs):
        outs, tmp = refs[-4:-1], refs[-1]
        xf = refs[0][...]
        if normed:
            g_ref, m_ref = refs[1], refs[2]
            xf = xf * lax.rsqrt(_head_sum(xf * xf, m_ref[...]) + EPS) * g_ref[...]
        _to_views(xf, tmp, outs)

    specs, shapes = _view_specs(tm, t, BF16)
    extra = [g, _head_sum_matrix(1.0 / DIL_HD)] if normed else []
    extra_specs = [pl.BlockSpec((1, w), lambda i: (0, 0)), pl.BlockSpec((w, w), lambda i: (0, 0))] if normed else []
    return _pcall(
        body, name=name, grid=(t // tm,),
        in_specs=[pl.BlockSpec((tm, w), lambda i: (i, col))] + extra_specs,
        out_specs=specs, out_shape=shapes, scratch_shapes=[_view_scratch(tm)],
        compiler_params=_cparams("parallel"))(x, *extra)


def _head_norm_bwd(dys, x, col, g, name, tm):
    t = x.shape[0]
    w = DIL_WIDTH
    tm = min(tm, t)
    nd = len(dys)
    nt = t // tm
    lane = np.arange(w) % DIL_HD
    fold = jnp.asarray((lane[:, None] == lane[None, :]).astype(np.float32))

    def body(*refs):
        x_ref, g_ref, m_ref, f_ref = refs[nd:nd + 4]
        dx_ref, dg_ref, s1, s2 = refs[-4:]
        dy = refs[0][...] + _from_view(refs[1], s1, DIL_DILATIONS[1]) + _from_view(refs[2], s2, DIL_DILATIONS[2])
        xf = x_ref[...]
        mat = m_ref[...]
        r = lax.rsqrt(_head_sum(xf * xf, mat) + EPS)
        xh = xf * r
        dxh = dy * g_ref[...]
        dx_ref[...] = r * (dxh - xh * _head_sum(dxh * xh, mat))

        @pl.when(pl.program_id(0) == 0)
        def _():
            dg_ref[...] = jnp.zeros_like(dg_ref)

        dg_ref[...] += jnp.sum(dy * xh, axis=0, keepdims=True)

        @pl.when(pl.program_id(0) == nt - 1)
        def _():
            per_lane = jnp.broadcast_to(dg_ref[...], (8, w))
            dg_ref[...] = lax.dot_general(per_lane, f_ref[...], NN, precision=lax.Precision.HIGHEST,
                                          preferred_element_type=F32)[0:1]

    row = pl.BlockSpec((tm, w), lambda i: (i, 0))
    vec = pl.BlockSpec((1, w), lambda i: (0, 0))
    sq = pl.BlockSpec((w, w), lambda i: (0, 0))
    views, _ = _view_specs(tm, t, F32)
    return _pcall(
        body, name=name, grid=(nt,),
        in_specs=list(views) + [pl.BlockSpec((tm, w), lambda i: (i, col)), vec, sq, sq],
        out_specs=(row, vec), out_shape=(_sds((t, w), F32), _sds((1, w), F32)),
        scratch_shapes=[_view_scratch(tm)] * 2,
        compiler_params=_cparams("arbitrary"))(*dys, x, g, _head_sum_matrix(1.0 / DIL_HD), fold)


def _rowdot(a, b, name, tm):
    n, d = a.shape
    tm = min(tm, n)

    def body(a_ref, b_ref, o_ref):
        o_ref[...] = jnp.sum(a_ref[...].astype(F32) * b_ref[...].astype(F32), axis=-1, keepdims=True)

    spec = pl.BlockSpec((tm, d), lambda i: (i, 0))
    return _pcall(body, name=name, grid=(n // tm,), in_specs=[spec, spec],
                  out_specs=pl.BlockSpec((tm, 1), lambda i: (i, 0)), out_shape=_sds((n, 1), F32),
                  compiler_params=_cparams("parallel"))(a, b)


def _sum_branches(parts, name, tm):
    t = parts[0].shape[0]
    w = DIL_WIDTH
    tm = min(tm, t)

    def body(a_ref, b_ref, c_ref, o_ref, s1, s2):
        o_ref[...] = a_ref[...] + _from_view(b_ref, s1, DIL_DILATIONS[1]) + _from_view(c_ref, s2, DIL_DILATIONS[2])

    views, _ = _view_specs(tm, t, F32)
    return _pcall(body, name=name, grid=(t // tm,), in_specs=list(views),
                  out_specs=pl.BlockSpec((tm, w), lambda i: (i, 0)), out_shape=_sds((t, w), F32),
                  scratch_shapes=[_view_scratch(tm)] * 2,
                  compiler_params=_cparams("parallel"))(*parts)


def _rope_tables(t):
    inv = ROPE_BASE ** (-np.arange(0, MLA_ROPE, 2, dtype=np.float64) / MLA_ROPE)
    ang = np.arange(t, dtype=np.float64)[:, None] * inv[None, :]
    cos, sin = np.cos(ang), np.sin(ang)
    return (jnp.asarray(np.concatenate([cos, cos], 1), F32), jnp.asarray(np.concatenate([-sin, sin], 1), F32))


def _swap_halves(a):
    half = MLA_ROPE // 2
    return jnp.concatenate([a[:, half:], a[:, :half]], axis=1)


def _qk_parts(x, pe, tm, nt):
    if pe is None:
        return None
    return (pl.BlockSpec((tm, MLA_NOPE), lambda i: (i, 0)), pl.BlockSpec((tm, MLA_ROPE), lambda i: (i % nt, 0)))


def _mla_qk_fwd(x, g, cos_t, sin_t, scale, name, tm, pe=None):
    n = x.shape[0]
    d = MLA_QK
    t = cos_t.shape[0]
    tm = min(tm, t)
    nt = t // tm
    split = _qk_parts(x, pe, tm, nt)

    def body(*refs):
        if split:
            xn_ref, xr_ref, xv_ref, g_ref, c_ref, s_ref, o_ref, v_ref = refs
            xn, xr = xn_ref[...], xr_ref[...]
            v_ref[...] = xv_ref[...].astype(v_ref.dtype)
        else:
            x_ref, g_ref, c_ref, s_ref, o_ref = refs
            xf = x_ref[...]
            xn, xr = xf[:, :MLA_NOPE], xf[:, MLA_NOPE:]
        ms = (jnp.sum(xn * xn, axis=-1, keepdims=True) + jnp.sum(xr * xr, axis=-1, keepdims=True)) * (1.0 / d)
        r = lax.rsqrt(ms + EPS)
        gg = g_ref[...]
        yn = xn * r * gg[:, :MLA_NOPE]
        yr = xr * r * gg[:, MLA_NOPE:]
        o_ref[:, :MLA_NOPE] = (yn * scale).astype(o_ref.dtype)
        o_ref[:, MLA_NOPE:] = ((yr * c_ref[...] + _swap_halves(yr) * s_ref[...]) * scale).astype(o_ref.dtype)

    row = pl.BlockSpec((tm, d), lambda i: (i, 0))
    vec = pl.BlockSpec((1, d), lambda i: (0, 0))
    tab = pl.BlockSpec((tm, MLA_ROPE), lambda i: (i % nt, 0))
    if not split:
        return _pcall(body, name=name, grid=(n // tm,), in_specs=[row, vec, tab, tab],
                      out_specs=row, out_shape=_sds((n, d), BF16),
                      compiler_params=_cparams("parallel"))(x, g, cos_t, sin_t)
    vals = pl.BlockSpec((tm, MLA_V), lambda i: (i, 1))
    return _pcall(body, name=name, grid=(n // tm,), in_specs=[split[0], split[1], vals, vec, tab, tab],
                  out_specs=(row, pl.BlockSpec((tm, MLA_V), lambda i: (i, 0))),
                  out_shape=(_sds((n, d), BF16), _sds((n, MLA_V), BF16)),
                  compiler_params=_cparams("parallel"))(x, pe, x, g, cos_t, sin_t)


def _mla_qk_bwd(dy, x, g, cos_t, sin_t, scale, name, tm, pe=None):
    n = x.shape[0]
    d = MLA_QK
    t = cos_t.shape[0]
    tm = min(tm, t)
    nt = t // tm
    split = _qk_parts(x, pe, tm, nt)

    def body(*refs):
        if split:
            dy_ref, xn_ref, xr_ref, g_ref, c_ref, s_ref, dxn_ref, dxr_ref, dg_ref = refs
            xn, xr = xn_ref[...], xr_ref[...]
        else:
            dy_ref, x_ref, g_ref, c_ref, s_ref, dx_ref, dg_ref = refs
            xf = x_ref[...]
            xn, xr = xf[:, :MLA_NOPE], xf[:, MLA_NOPE:]
        gg = g_ref[...]
        ms = (jnp.sum(xn * xn, axis=-1, keepdims=True) + jnp.sum(xr * xr, axis=-1, keepdims=True)) * (1.0 / d)
        r = lax.rsqrt(ms + EPS)
        xh_n, xh_r = xn * r, xr * r
        dyf = dy_ref[...] * scale
        dyr = dyf[:, MLA_NOPE:]
        dn_n = dyf[:, :MLA_NOPE]
        dn_r = dyr * c_ref[...] + _swap_halves(dyr * s_ref[...])
        dxh_n = dn_n * gg[:, :MLA_NOPE]
        dxh_r = dn_r * gg[:, MLA_NOPE:]
        mean = (jnp.sum(dxh_n * xh_n, axis=-1, keepdims=True)
                + jnp.sum(dxh_r * xh_r, axis=-1, keepdims=True)) * (1.0 / d)
        dx_n = r * (dxh_n - xh_n * mean)
        dx_r = r * (dxh_r - xh_r * mean)
        if split:
            dxn_ref[...] = dx_n
            dxr_ref[...] = dx_r
        else:
            dx_ref[:, :MLA_NOPE] = dx_n
            dx_ref[:, MLA_NOPE:] = dx_r

        @pl.when(pl.program_id(0) == 0)
        def _():
            dg_ref[...] = jnp.zeros_like(dg_ref)

        dg_ref[:, :MLA_NOPE] += jnp.sum(dn_n * xh_n, axis=0, keepdims=True)
        dg_ref[:, MLA_NOPE:] += jnp.sum(dn_r * xh_r, axis=0, keepdims=True)

    row = pl.BlockSpec((tm, d), lambda i: (i, 0))
    vec = pl.BlockSpec((1, d), lambda i: (0, 0))
    tab = pl.BlockSpec((tm, MLA_ROPE), lambda i: (i % nt, 0))
    if not split:
        return _pcall(body, name=name, grid=(n // tm,), in_specs=[row, row, vec, tab, tab],
                      out_specs=(row, vec), out_shape=(_sds((n, d), F32), _sds((1, d), F32)),
                      compiler_params=_cparams("arbitrary"))(dy, x, g, cos_t, sin_t)
    outs = (pl.BlockSpec((tm, MLA_NOPE), lambda i: (i, 0)), pl.BlockSpec((tm, MLA_ROPE), lambda i: (i, 0)), vec)
    return _pcall(body, name=name, grid=(n // tm,), in_specs=[row, split[0], split[1], vec, tab, tab],
                  out_specs=outs, out_shape=(_sds((n, MLA_NOPE), F32), _sds((n, MLA_ROPE), F32), _sds((1, d), F32)),
                  compiler_params=_cparams("arbitrary"))(dy, x, pe, g, cos_t, sin_t)


def _causal_mask(i, j, tq, tk, width):
    row = i * tq + lax.broadcasted_iota(jnp.int32, (tq, width), 0)
    col = j * tk + lax.broadcasted_iota(jnp.int32, (tq, width), 1)
    return col <= row


def _causal_steps(nq, nk, tq, tk, q_major):
    if q_major:
        groups = [[(i, j) for j in range((i * tq + tq - 1) // tk + 1)] for i in range(nq)]
        nunit = tk // tq if tk % tq == 0 else 1
    else:
        groups = [[(i, j) for i in range((j * tk) // tq, nq)] for j in range(nk)]
        nunit = tq // tk if tq % tk == 0 else 1
    it, jt, fl = [], [], []
    for g in groups:
        for n, (i, j) in enumerate(g):
            crossing = j * tk + tk - 1 > i * tq
            if q_major:
                unit = tk // nunit
                u = min(nunit, -(-(i * tq + tq - j * tk) // unit)) - 1
            else:
                unit = tq // nunit
                u = max(0, j * tk - i * tq) // unit
            it.append(i)
            jt.append(j)
            fl.append((n == 0) + 2 * (n == len(g) - 1) + 4 * crossing + 8 * (u if crossing else 0))
    return tuple(jnp.asarray(np.array(a, np.int32)) for a in (it, jt, fl)), nunit


def _by_crossing(flags, nunit, update):
    pl.when((flags & 4) == 0)(functools.partial(update, None))
    for u in range(nunit):
        pl.when(((flags & 4) != 0) & ((flags >> 3) == u))(functools.partial(update, u))


def _causal_specs(tq, tk):
    def qs(w):
        return pl.BlockSpec((None, tq, w), lambda h, s, it, jt, fl: (h, it[s], 0))

    def kv(w):
        return pl.BlockSpec((None, tk, w), lambda h, s, it, jt, fl: (h, jt[s], 0))

    return qs, kv


def _mla_fwd(q, k, v, tq, tk):
    nh, t, dq = q.shape
    dv = v.shape[2]
    tq, tk = min(tq, t), min(tk, t)
    tables, nunit = _causal_steps(t // tq, t // tk, tq, tk, True)

    def body(it, jt, fl, q_ref, k_ref, v_ref, o_ref, lse_ref, m_sc, l_sc, acc_sc):
        step = pl.program_id(1)
        i, j, flags = it[step], jt[step], fl[step]

        @pl.when((flags & 1) != 0)
        def _():
            m_sc[...] = jnp.full_like(m_sc, NEG)
            l_sc[...] = jnp.zeros_like(l_sc)
            acc_sc[...] = jnp.zeros_like(acc_sc)

        def update(units):
            wk = tk if units is None else (units + 1) * (tk // nunit)
            s = _dot(q_ref[...], k_ref[:wk, :], NT)
            if units is not None:
                s = jnp.where(_causal_mask(i, j, tq, tk, wk), s, NEG)
            m_prev = m_sc[...]
            m_new = jnp.maximum(m_prev, jnp.max(s, axis=-1, keepdims=True))
            alpha = jnp.exp(m_prev - m_new)
            p = jnp.exp(s - m_new)
            l_sc[...] = alpha * l_sc[...] + jnp.sum(p, axis=-1, keepdims=True)
            acc_sc[...] = alpha * acc_sc[...] + _dot(p.astype(BF16), v_ref[:wk, :], NN)
            m_sc[...] = m_new

        _by_crossing(flags, nunit, update)

        @pl.when((flags & 2) != 0)
        def _():
            o_ref[...] = acc_sc[...] / l_sc[...]
            lse_ref[...] = m_sc[...] + jnp.log(l_sc[...])

    qs, kv = _causal_specs(tq, tk)
    return _pcall(
        body, name="mla_attn_fwd",
        grid_spec=pltpu.PrefetchScalarGridSpec(
            num_scalar_prefetch=3, grid=(nh, tables[0].shape[0]),
            in_specs=[qs(dq), kv(dq), kv(dv)], out_specs=(qs(dv), qs(1)),
            scratch_shapes=[pltpu.VMEM((tq, 1), F32), pltpu.VMEM((tq, 1), F32), pltpu.VMEM((tq, dv), F32)]),
        out_shape=(_sds((nh, t, dv), F32), _sds((nh, t, 1), F32)),
        compiler_params=_cparams("parallel", "arbitrary"))(*tables, q, k, v)


def _mla_bwd(q, k, k_t, v, do, lse_row, dl_row, tq, tk):
    nh, t, dq = q.shape
    dv = v.shape[2]
    tq, tk = min(tq, t), min(tk, t)
    nq = t // tq
    tables, nunit = _causal_steps(nq, t // tk, tq, tk, False)

    def body(it, jt, fl, q_ref, k_ref, kt_ref, v_ref, do_ref, lse_ref, dl_ref, dk_ref, dv_ref, dq_ref, dk_sc, dv_sc):
        step = pl.program_id(1)
        i, j, flags = it[step], jt[step], fl[step]

        def update(units):
            off = 0 if units is None else units * (tq // nunit)
            qq = q_ref[off:, :]
            st = _dot(k_ref[...], qq, NT)
            if units is not None:
                key = j * tk + lax.broadcasted_iota(jnp.int32, (tk, tq - off), 0)
                qry = i * tq + off + lax.broadcasted_iota(jnp.int32, (tk, tq - off), 1)
                st = jnp.where(key <= qry, st, NEG)
            pt = jnp.exp(st - lse_ref[:, off:])
            dob = do_ref[off:, :].astype(BF16)
            dpt = _dot(v_ref[...], dob, NT)
            dst = pt * (dpt - dl_ref[:, off:])
            dsb = dst.astype(BF16)
            dv_part = _dot(pt.astype(BF16), dob, NN)
            dk_part = _dot(dsb, qq, NN)
            dq_part = _dot(kt_ref[...], dsb, NN)

            @pl.when((flags & 1) != 0)
            def _():
                dv_sc[...] = dv_part
                dk_sc[...] = dk_part

            @pl.when((flags & 1) == 0)
            def _():
                dv_sc[...] += dv_part
                dk_sc[...] += dk_part

            if off == 0:
                @pl.when(j == 0)
                def _():
                    dq_ref[i] = dq_part

                @pl.when(j != 0)
                def _():
                    dq_ref[i] += dq_part
            else:
                dq_ref[i, :, off:] += dq_part

        _by_crossing(flags, nunit, update)

        @pl.when((flags & 2) != 0)
        def _():
            dk_ref[...] = dk_sc[...]
            dv_ref[...] = dv_sc[...]

    qs, kv = _causal_specs(tq, tk)
    rowv = pl.BlockSpec((None, 1, tq), lambda h, s, it, jt, fl: (h, 0, it[s]))
    ktv = pl.BlockSpec((None, dq, tk), lambda h, s, it, jt, fl: (h, 0, jt[s]))
    whole = pl.BlockSpec((None, nq, dq, tq), lambda h, s, it, jt, fl: (h, 0, 0, 0))
    return _pcall(
        body, name="mla_attn_bwd",
        grid_spec=pltpu.PrefetchScalarGridSpec(
            num_scalar_prefetch=3, grid=(nh, tables[0].shape[0]),
            in_specs=[qs(dq), kv(dq), ktv, kv(dv), qs(dv), rowv, rowv], out_specs=(kv(dq), kv(dv), whole),
            scratch_shapes=[pltpu.VMEM((tk, dq), F32), pltpu.VMEM((tk, dv), F32)]),
        out_shape=(_sds((nh, t, dq), F32), _sds((nh, t, dv), F32), _sds((nh, nq, dq, tq), F32)),
        compiler_params=_cparams("parallel", "arbitrary"))(*tables, q, k, k_t, v, do, lse_row, dl_row)


def _ffn_down_loss(act, wd, x, target, name, tm):
    nc, t, fs = act.shape
    d = x.shape[1]
    tm = min(tm, t)
    nt = t // tm

    def body(*refs):
        a_refs, w_refs = refs[:nc], refs[nc:2 * nc]
        x_ref, t_ref, dy_ref, loss_ref, acc = refs[2 * nc:]
        i = pl.program_id(0)
        tot = _dot(a_refs[0][...], w_refs[0][...], NN)
        for c in range(1, nc):
            tot = tot + _dot(a_refs[c][...], w_refs[c][...], NN)
        err = x_ref[...] + tot * FFN_RESID - t_ref[...]
        dy_ref[...] = err * (1.0 / d)

        @pl.when(i == 0)
        def _():
            acc[...] = jnp.zeros_like(acc)

        acc[...] += jnp.sum(err * err, axis=0, keepdims=True)

        @pl.when(i == nt - 1)
        def _():
            loss_ref[0, 0] = jnp.sum(acc[...]) * (0.5 / d)

    row = pl.BlockSpec((tm, d), lambda i: (i, 0))
    a_specs = [pl.BlockSpec((None, tm, fs), lambda i, c=c: (c, i, 0)) for c in range(nc)]
    w_specs = [pl.BlockSpec((None, fs, d), lambda i, c=c: (c, 0, 0)) for c in range(nc)]
    return _pcall(
        body, name=name, grid=(nt,), in_specs=a_specs + w_specs + [row, row],
        out_specs=(row, pl.BlockSpec(memory_space=pltpu.SMEM)),
        out_shape=(_sds((t, d), F32), _sds((1, 1), F32)),
        scratch_shapes=[pltpu.VMEM((1, d), F32)],
        compiler_params=_cparams("arbitrary"))(*[act] * nc, *[wd] * nc, x, target)


def _adamw(w, g, m, v, name):
    r, c = w.shape
    tr = r
    for cand in (256, 128, 64, 32, 16, 8):
        if r % cand == 0:
            tr = cand
            break

    def body(w_ref, g_ref, m_ref, v_ref, d_ref, nm_ref, nv_ref):
        gg = g_ref[...]
        nm = ADAM_B1 * m_ref[...] + (1.0 - ADAM_B1) * gg
        nv = ADAM_B2 * v_ref[...] + (1.0 - ADAM_B2) * (gg * gg)
        m_hat = nm / (1.0 - ADAM_B1 ** ADAM_STEP)
        v_hat = nv / (1.0 - ADAM_B2 ** ADAM_STEP)
        d_ref[...] = -ADAM_LR * (m_hat / (jnp.sqrt(v_hat) + ADAM_EPS) + ADAM_WD * w_ref[...])
        nm_ref[...] = nm
        nv_ref[...] = nv

    spec = pl.BlockSpec((tr, c), lambda i: (i, 0))
    sd = _sds((r, c), F32)
    return _pcall(body, name=name, grid=(r // tr,), in_specs=[spec] * 4, out_specs=(spec,) * 3,
                  out_shape=(sd, sd, sd), compiler_params=_cparams("parallel"))(w, g, m, v)


MESH_ID = pl.DeviceIdType.MESH
HBM_SPEC = pl.BlockSpec(memory_space=pltpu.HBM)


def _place():
    return lax.axis_index("x"), lax.axis_index("y"), lax.axis_index("c")


def _other_chips(x, y):
    return [(1 - x, y), (x, 1 - y), (1 - x, 1 - y)]


def _remote(src, dst, send_sems, recv_sems, k, to):
    return pltpu.make_async_remote_copy(src_ref=src, dst_ref=dst, send_sem=send_sems.at[k], recv_sem=recv_sems.at[k],
                                        device_id=to, device_id_type=MESH_ID)


def _halves(arrays):
    for a in arrays:
        assert a.shape[-2] % 32 == 0
    return [a.shape[-2] // 2 for a in arrays]


def _gather_start(srcs, outs, halves, send_sems, recv_sems):
    x, y, c = _place()
    for a, half in enumerate(halves):
        rows = pl.ds(c * half, half)
        for k, (cx, cy) in enumerate(_other_chips(x, y)):
            _remote(srcs[a].at[rows, :], outs[a].at[2 * x + y, rows, :], send_sems, recv_sems, 3 * a + k,
                    (cx, cy, c)).start()


def _gather_wait(outs, halves, send_sems, recv_sems):
    x, y, c = _place()
    for a, half in enumerate(halves):
        for k, (cx, cy) in enumerate(_other_chips(x, y)):
            got = outs[a].at[2 * cx + cy, pl.ds(c * half, half), :]
            _remote(got, got, send_sems, recv_sems, 3 * a + k, (x, y, c)).wait()


def _forward_cores(partly):
    n = len(partly)
    halves = _halves(partly)

    def body(*refs):
        srcs, outs, send_sems, recv_sems = refs[:n], refs[n:2 * n], refs[2 * n], refs[2 * n + 1]
        x, y, c = _place()
        for a, half in enumerate(halves):
            for k, (cx, cy) in enumerate(_other_chips(x, y)):
                rows = pl.ds(c * half, half)
                _remote(srcs[a].at[2 * cx + cy, rows, :], outs[a].at[2 * cx + cy, rows, :], send_sems, recv_sems,
                        3 * a + k, (x, y, 1 - c)).start()
        for a, half in enumerate(halves):
            for k, (cx, cy) in enumerate(_other_chips(x, y)):
                mine = outs[a].at[2 * cx + cy, pl.ds(c * half, half), :]
                theirs = outs[a].at[2 * cx + cy, pl.ds((1 - c) * half, half), :]
                _remote(mine, theirs, send_sems, recv_sems, 3 * a + k, (x, y, c)).wait()

    return _pcall(
        body, name="forward_cores", in_specs=[HBM_SPEC] * n, out_specs=tuple([HBM_SPEC] * n),
        out_shape=tuple(_sds(p.shape, p.dtype) for p in partly), input_output_aliases={a: a for a in range(n)},
        scratch_shapes=[pltpu.SemaphoreType.DMA((3 * n,)), pltpu.SemaphoreType.DMA((3 * n,))],
    )(*partly)


def _gather_weights(blocks):
    n = len(blocks)
    halves = _halves(blocks)

    def body(*refs):
        srcs, outs, send_sems, recv_sems = refs[:n], refs[n:2 * n], refs[2 * n], refs[2 * n + 1]
        x, y, c = _place()
        me = 2 * x + y
        sibling = (x, y, 1 - c)
        chips = _other_chips(x, y)

        def part(a, chip, core):
            return outs[a].at[chip, pl.ds(core * halves[a], halves[a]), :]

        for a in range(n):
            mine = srcs[a].at[pl.ds(c * halves[a], halves[a]), :]
            for k, (cx, cy) in enumerate(chips):
                _remote(mine, part(a, me, c), send_sems, recv_sems, 6 * a + k, (cx, cy, c)).start()
        for k, (cx, cy) in enumerate(chips):
            for a in range(n):
                got = part(a, 2 * cx + cy, c)
                _remote(got, got, send_sems, recv_sems, 6 * a + k, (x, y, c)).wait_recv()
                _remote(got, got, send_sems, recv_sems, 6 * a + 3 + k, sibling).start()
        for k, (cx, cy) in enumerate(chips):
            for a in range(n):
                got = part(a, 2 * cx + cy, 1 - c)
                _remote(got, got, send_sems, recv_sems, 6 * a + 3 + k, (x, y, c)).wait_recv()
        for a in range(n):
            sent = part(a, me, c)
            for k in range(6):
                _remote(sent, sent, send_sems, recv_sems, 6 * a + k, (x, y, c)).wait_send()

    return _pcall(
        body, name="gather_weights", in_specs=[HBM_SPEC] * n, out_specs=tuple([HBM_SPEC] * n),
        out_shape=tuple(_sds((N_CHIPS,) + b.shape, b.dtype) for b in blocks),
        scratch_shapes=[pltpu.SemaphoreType.DMA((6 * n,)), pltpu.SemaphoreType.DMA((6 * n,))],
    )(*blocks)


def _reduce_cores(grads, tag):
    n = len(grads)

    def body(*refs):
        gs, outs, send_sems, recv_sems = refs[:n], refs[n:2 * n], refs[2 * n], refs[2 * n + 1]
        _cores_start(gs, outs, send_sems, recv_sems)
        _cores_wait(gs, outs, send_sems, recv_sems)

    return _pcall(
        body, name=f"reduce_cores_{tag}", in_specs=[HBM_SPEC] * n, out_specs=tuple([HBM_SPEC] * n),
        out_shape=_cores_shapes(grads), scratch_shapes=_cores_sems(n),
    )(*grads)


def _cores_shapes(grads):
    return tuple(_sds((N_CHIPS, h, g.shape[2]), g.dtype) for g, h in zip(grads, _halves(grads)))


def _cores_sems(n):
    return [pltpu.SemaphoreType.DMA((n,)), pltpu.SemaphoreType.DMA((n,))]


def _cores_start(gs, outs, send_sems, recv_sems):
    x, y, c = _place()
    for a, g in enumerate(gs):
        half = g.shape[1] // 2
        for j in range(N_CHIPS):
            _remote(g.at[j, pl.ds((1 - c) * half, half), :], outs[a].at[j], send_sems, recv_sems, a,
                    (x, y, 1 - c)).start()


def _cores_wait(gs, outs, send_sems, recv_sems):
    x, y, c = _place()
    for a, g in enumerate(gs):
        half = g.shape[1] // 2
        _remote(g.at[:, pl.ds((1 - c) * half, half), :], outs[a], send_sems, recv_sems, a, (x, y, c)).wait()


def _scatter_shapes(parts):
    return tuple(_sds((3,) + p.shape[1:], p.dtype) for p in parts)


def _scatter_sems(n):
    return [pltpu.SemaphoreType.DMA((3 * n,)), pltpu.SemaphoreType.DMA((3 * n,))]


def _scatter_start(ps, outs, send_sems, recv_sems):
    x, y, c = _place()
    for a in range(len(ps)):
        for k, (cx, cy) in enumerate(_other_chips(x, y)):
            _remote(ps[a].at[2 * cx + cy], outs[a].at[k], send_sems, recv_sems, 3 * a + k, (cx, cy, c)).start()


def _scatter_wait(ps, outs, send_sems, recv_sems):
    x, y, c = _place()
    for a in range(len(ps)):
        for k in range(3):
            _remote(ps[a].at[k], outs[a].at[k], send_sems, recv_sems, 3 * a + k, (x, y, c)).wait()


EXCHANGES = {"chips": (_scatter_start, _scatter_wait, _scatter_shapes, _scatter_sems),
             "cores": (_cores_start, _cores_wait, _cores_shapes, _cores_sems)}


def _sum_partials(received, parts, place):
    n = len(parts)
    steps = 2
    tiles = [p.shape[1] // steps for p in parts]

    def body(place_ref, *refs):
        rs, ps, outs = refs[:n], refs[n:2 * n], refs[2 * n:]
        for a in range(n):
            tot = ps[a][...].astype(F32)
            for k in range(3):
                tot = tot + rs[a][k].astype(F32)
            outs[a][...] = tot

    cols = [p.shape[2] for p in parts]
    return _pcall(
        body, name="sum_chip_partials",
        grid_spec=pltpu.PrefetchScalarGridSpec(
            num_scalar_prefetch=1, grid=(steps,),
            in_specs=[pl.BlockSpec((3, tm, w), lambda i, pc: (0, i, 0)) for tm, w in zip(tiles, cols)]
            + [pl.BlockSpec((None, tm, w), lambda i, pc: (pc[0], i, 0)) for tm, w in zip(tiles, cols)],
            out_specs=tuple(pl.BlockSpec((tm, w), lambda i, pc: (pc[1] * steps + i, 0)) for tm, w in zip(tiles, cols))),
        out_shape=tuple(_sds((2 * p.shape[1], p.shape[2]), F32) for p in parts),
        compiler_params=_cparams("parallel"))(place, *received, *parts)


def _share_cores(blocks):
    n = len(blocks)
    halves = _halves(blocks)

    def body(*refs):
        srcs, outs, send_sems, recv_sems = refs[:n], refs[n:2 * n], refs[2 * n], refs[2 * n + 1]
        x, y, c = _place()
        for a in range(n):
            piece = pl.ds(c * halves[a], halves[a])
            _remote(srcs[a].at[piece, :], outs[a].at[piece, :], send_sems, recv_sems, a, (x, y, 1 - c)).start()
        for a in range(n):
            mine = outs[a].at[pl.ds(c * halves[a], halves[a]), :]
            theirs = outs[a].at[pl.ds((1 - c) * halves[a], halves[a]), :]
            _remote(mine, theirs, send_sems, recv_sems, a, (x, y, c)).wait()

    return _pcall(
        body, name="share_cores", in_specs=[HBM_SPEC] * n, out_specs=tuple([HBM_SPEC] * n),
        out_shape=tuple(_sds(b.shape, b.dtype) for b in blocks), input_output_aliases={a: a for a in range(n)},
        scratch_shapes=[pltpu.SemaphoreType.DMA((n,)), pltpu.SemaphoreType.DMA((n,))],
    )(*blocks)


def _sum_blocks(stacked, name, tm):
    n, rows, lanes = stacked.shape
    tm = min(tm, rows)

    def body(s_ref, o_ref):
        tot = s_ref[n - 1].astype(F32)
        for k in range(n - 1):
            tot = tot + s_ref[k].astype(F32)
        o_ref[...] = tot

    return _pcall(body, name=name, grid=(rows // tm,),
                  in_specs=[pl.BlockSpec((n, tm, lanes), lambda i: (0, i, 0))],
                  out_specs=pl.BlockSpec((tm, lanes), lambda i: (i, 0)), out_shape=_sds((rows, lanes), F32),
                  compiler_params=_cparams("parallel"))(stacked)


def _add_halves(grads, theirs, core, tag):
    n = len(grads)
    steps = 2
    tiles = [t.shape[1] // steps for t in theirs]
    cols = [t.shape[2] for t in theirs]

    def body(c_ref, *refs):
        gs, ts, outs = refs[:n], refs[n:2 * n], refs[2 * n:]
        for a in range(n):
            outs[a][...] = (gs[a][...] + ts[a][...]).astype(BF16)

    own = [pl.BlockSpec((None, tm, w), lambda k, i, c: (k, c[0] * steps + i, 0)) for tm, w in zip(tiles, cols)]
    same = [pl.BlockSpec((None, tm, w), lambda k, i, c: (k, i, 0)) for tm, w in zip(tiles, cols)]
    return _pcall(
        body, name=f"add_core_halves_{tag}",
        grid_spec=pltpu.PrefetchScalarGridSpec(
            num_scalar_prefetch=1, grid=(N_CHIPS, steps), in_specs=own + same, out_specs=tuple(same)),
        out_shape=tuple(_sds(t.shape, BF16) for t in theirs),
        compiler_params=_cparams("parallel", "parallel"))(core, *grads, *theirs)


def _allreduce_small(part):
    rows, lanes = part.shape
    ndev = 8

    def body(src, tot, buf, send_sems, recv_sems):
        x, y, c = _place()
        me = 4 * x + 2 * y + c
        buf[me] = src[...]
        sends = []
        for k in range(1, ndev):
            peer = (x ^ (k >> 2), y ^ ((k >> 1) & 1), c ^ (k & 1))
            cp = _remote(src, buf.at[me], send_sems, recv_sems, k - 1, peer)
            cp.start()
            sends.append(cp)
        for k in range(1, ndev):
            theirs = buf.at[me ^ k]
            _remote(theirs, theirs, send_sems, recv_sems, k - 1, (x, y, c)).wait_recv()
        for cp in sends:
            cp.wait_send()
        acc = buf[0]
        for d in range(1, ndev):
            acc = acc + buf[d]
        tot[...] = acc

    vm = pl.BlockSpec(memory_space=pltpu.VMEM)
    return _pcall(
        body, name="allreduce_small", in_specs=[vm], out_specs=vm, out_shape=_sds((rows, lanes), F32),
        scratch_shapes=[pltpu.VMEM((ndev, rows, lanes), F32), pltpu.SemaphoreType.DMA((ndev - 1,)),
                        pltpu.SemaphoreType.DMA((ndev - 1,))],
    )(part)


def _pack_small(vals):
    parts = []
    for name, shape, r in SMALL:
        flat = vals[name].reshape(-1).astype(F32)
        parts.append(jnp.pad(flat, (0, r * LANES - flat.shape[0])).reshape(r, LANES))
    used = sum(r for _, _, r in SMALL)
    parts.append(jnp.zeros((SMALL_ROWS - used, LANES), F32))
    return jnp.concatenate(parts, axis=0)


def _unpack_small(packed):
    out, off = {}, 0
    for name, shape, r in SMALL:
        n = int(np.prod(shape))
        out[name] = packed[off:off + r].reshape(-1)[:n].reshape(shape)
        off += r
    return out


def _heads_major(a, nh):
    t = a.shape[0]
    return a.reshape(t, nh, a.shape[1] // nh).transpose(1, 0, 2)


def _tokens_major(a):
    nh, t, w = a.shape
    return a.transpose(1, 0, 2).reshape(t, nh * w)


LATE = ("ffn1_w_gate", "ffn1_w_up", "ffn1_w_down")
EARLY = tuple(name for name, _ in BIG if name not in LATE)


def _local_step(x, target, small, wfull, exchanges=None, later_weights=None):
    t = x.shape[0]
    nh, hd = DIL_HEADS, DIL_HD
    grads_s, grads_b = {}, {}

    x1, ffn1_saved, partly = _ffn_fwd(x, small["ffn1_norm"], wfull["ffn1_w_gate"], wfull["ffn1_w_up"],
                                      wfull["ffn1_w_down"], "ffn1", later_weights[0] if later_weights else ())
    if later_weights:
        wfull = {**wfull, **later_weights[1](partly)}
    w_in = wfull["w_in"].transpose(1, 0, 2).reshape(D_MODEL, -1)
    w_out = wfull["w_out"].reshape(D_MODEL, D_MODEL)
    w_qb, w_kvb = wfull["mla_w_q_b"], wfull["mla_w_kv_b"]
    hm = _rms_fwd(x1, small["mix_norm"], BF16, "mix_norm", 512)
    proj = _mm_simple("in_proj", hm, w_in, NN, F32, tm=1024)
    cq, ckv, k_pe = proj[:, 1536:1792], proj[:, 1792:1920], proj[:, 1920:1984]

    gq, gk = jnp.tile(small["dil_q_norm"], (1, nh)), jnp.tile(small["dil_k_norm"], (1, nh))
    qn = _head_norm_fwd(proj, 0, gq, "dil_q_norm", 512)
    kn = _head_norm_fwd(proj, 1, gk, "dil_k_norm", 512)
    v_d = _head_norm_fwd(proj, 2, None, "dil_v_views", 512)
    bias = _bias_tiles(small["rel_bias"]).reshape(3, nh // 2, 2 * QB, QB + DIL_W)
    outs, lses = [], []
    for b, dil in enumerate(DIL_DILATIONS):
        o_b, lse_b = _dil_fwd(qn[b], kn[b], v_d[b], bias[b], dil, f"dil_fwd_{dil}")
        outs.append(o_b)
        lses.append(lse_b)
    o_dil, lse_tot, od = _dil_merge(outs, lses, small["out_norm_dil"], 512)

    mh = MLA_HEADS
    cos_t, sin_t = _rope_tables(t)
    cqn = _rms_fwd(cq, small["mla_q_a_norm"], BF16, "mla_q_a_norm", 512)
    ckvn = _rms_fwd(ckv, small["mla_kv_a_norm"], BF16, "mla_kv_a_norm", 512)
    tm = min(512, t)

    th = min(2048, t)

    def head_proj(name, a, w, width):
        k = a.shape[1]
        return _mm(name, (mh, t // th, 1),
                   [(a, pl.BlockSpec((th, k), lambda h, i, r: (i, 0)), w, pl.BlockSpec((None, k, width), lambda h, i, r: (h, 0, 0)))],
                   NN, _sds((mh, t, width), F32), pl.BlockSpec((None, th, width), lambda h, i, r: (h, i, 0)), (th, width))

    q_raw = head_proj("mla_q_proj", cqn, w_qb, MLA_QK)
    kv_raw = head_proj("mla_kv_proj", ckvn, w_kvb, MLA_NOPE + MLA_V)
    q_raw2, kv_raw2 = q_raw.reshape(mh * t, MLA_QK), kv_raw.reshape(mh * t, MLA_NOPE + MLA_V)
    q_scale = MLA_QK ** -0.5
    q_m = _mla_qk_fwd(q_raw2, small["mla_q_norm"], cos_t, sin_t, q_scale, "mla_q_rope", 2048).reshape(mh, t, MLA_QK)
    k_m, v_m = _mla_qk_fwd(kv_raw2, small["mla_k_norm"], cos_t, sin_t, 1.0, "mla_k_rope", 2048, pe=k_pe)
    k_m, v_m = k_m.reshape(mh, t, MLA_QK), v_m.reshape(mh, t, MLA_V)
    o_mla_h, lse_m = _mla_fwd(q_m, k_m, v_m, 512, 4096)
    o_mla = _tokens_major(o_mla_h)

    om = _rms_fwd(o_mla, small["out_norm_mla"], BF16, "out_norm_mla", 512)
    half_w = DIL_WIDTH
    row = pl.BlockSpec((tm, D_MODEL), lambda i, j, r: (i, 0))
    act_spec = pl.BlockSpec((tm, half_w), lambda i, j, r: (i, 0))
    x2 = _mm("out_proj", (t // tm, 1, 1),
             [(od, act_spec, w_out, pl.BlockSpec((half_w, D_MODEL), lambda i, j, r: (0, 0))),
              (om, act_spec, w_out, pl.BlockSpec((half_w, D_MODEL), lambda i, j, r: (1, 0)))],
             NN, _sds((t, D_MODEL), F32), row, (tm, D_MODEL), res=(x1, row))
    (dy, loss), ffn2_saved, _ = _ffn_fwd(x2, small["ffn2_norm"], wfull["ffn2_w_gate"], wfull["ffn2_w_up"],
                                         wfull["ffn2_w_down"], "ffn2", target=target)

    dx2, grads_s["ffn2_norm"], grads_b["ffn2_w_gate"], grads_b["ffn2_w_up"], grads_b["ffn2_w_down"], _, _ = _ffn_bwd(
        dy, x2, small["ffn2_norm"], wfull["ffn2_w_gate"], wfull["ffn2_w_up"], wfull["ffn2_w_down"], ffn2_saved, "ffn2")

    d_ocat = _mm_simple("out_proj_dx", dx2, w_out, NT, F32, tm=1024)
    dw_out_d = _mm_simple("out_proj_dw_dil", od, dx2, TN, F32, tk=2048)
    dw_out_m = _mm_simple("out_proj_dw_mla", om, dx2, TN, F32, tk=2048)
    grads_b["w_out"] = jnp.concatenate([dw_out_d, dw_out_m], axis=0).reshape(N_CHIPS, D_MODEL // N_CHIPS, D_MODEL)
    do_dil, grads_s["out_norm_dil"] = _rms_bwd([d_ocat[:, :half_w]], o_dil, small["out_norm_dil"], None, "out_norm_dil_bwd", 512)
    do_mla, grads_s["out_norm_mla"] = _rms_bwd([d_ocat[:, half_w:]], o_mla, small["out_norm_mla"], None, "out_norm_mla_bwd", 512)

    do_m = _heads_major(do_mla, mh)
    dl_m = _rowdot(do_m.reshape(mh * t, MLA_V), o_mla_h.reshape(mh * t, MLA_V), "mla_delta", 2048).reshape(mh, t, 1)
    dk_m, dv_m, dq_t = _mla_bwd(q_m, k_m, k_m.transpose(0, 2, 1), v_m, do_m, lse_m.reshape(mh, 1, t),
                                dl_m.reshape(mh, 1, t), 2048, 512)
    dq_m = dq_t.transpose(0, 1, 3, 2).reshape(mh, t, MLA_QK)
    dq_raw, grads_s["mla_q_norm"] = _mla_qk_bwd(dq_m.reshape(mh * t, MLA_QK), q_raw2, small["mla_q_norm"],
                                                 cos_t, sin_t, q_scale, "mla_q_rope_bwd", 2048)
    dk_nope, dk_pe_h, grads_s["mla_k_norm"] = _mla_qk_bwd(dk_m.reshape(mh * t, MLA_QK), kv_raw2, small["mla_k_norm"],
                                                          cos_t, sin_t, 1.0, "mla_k_rope_bwd", 2048, pe=k_pe)
    dq_raw = dq_raw.reshape(mh, t, MLA_QK)
    dk_nope = dk_nope.reshape(mh, t, MLA_NOPE)

    def head_proj_dx(name, d, w):
        width, k = d.shape[2], w.shape[1]
        pairs = [(d, pl.BlockSpec((None, th, width), lambda i, j, r, h=h: (h, i, 0)),
                  w, pl.BlockSpec((None, k, width), lambda i, j, r, h=h: (h, 0, 0))) for h in range(mh)]
        return _mm(name, (t // th, 1, 1), pairs, NT, _sds((t, k), F32),
                   pl.BlockSpec((th, k), lambda i, j, r: (i, 0)), (th, k))

    def head_proj_dw(name, a, d):
        width, k = d.shape[2], a.shape[1]
        return _mm(name, (mh, 1, t // th),
                   [(a, pl.BlockSpec((th, k), lambda h, j, r: (r, 0)), d, pl.BlockSpec((None, th, width), lambda h, j, r: (h, r, 0)))],
                   TN, _sds((mh, k, width), F32), pl.BlockSpec((None, k, width), lambda h, j, r: (h, 0, 0)), (k, width))

    d_cqn = head_proj_dx("mla_q_proj_dx", dq_raw, w_qb)
    kv_pairs = []
    for h in range(mh):
        for part, d_part in enumerate((dk_nope, dv_m)):
            kv_pairs.append((d_part, pl.BlockSpec((None, th, MLA_NOPE), lambda i, j, r, h=h: (h, i, 0)),
                             w_kvb, pl.BlockSpec((None, MLA_KV_RANK, MLA_NOPE), lambda i, j, r, h=h, part=part: (h, 0, part))))
    d_ckvn = _mm("mla_kv_proj_dx", (t // th, 1, 1), kv_pairs, NT, _sds((t, MLA_KV_RANK), F32),
                 pl.BlockSpec((th, MLA_KV_RANK), lambda i, j, r: (i, 0)), (th, MLA_KV_RANK))
    grads_b["mla_w_q_b"] = head_proj_dw("mla_q_proj_dw", cqn, dq_raw)
    grads_b["mla_w_kv_b"] = jnp.concatenate([head_proj_dw("mla_k_proj_dw", ckvn, dk_nope),
                                             head_proj_dw("mla_v_proj_dw", ckvn, dv_m)], axis=2)
    d_cq, grads_s["mla_q_a_norm"] = _rms_bwd([d_cqn], cq, small["mla_q_a_norm"], None, "mla_q_a_norm_bwd", 512)
    d_ckv, grads_s["mla_kv_a_norm"] = _rms_bwd([d_ckvn], ckv, small["mla_kv_a_norm"], None, "mla_kv_a_norm_bwd", 512)
    d_kpe = _sum_blocks(dk_pe_h.reshape(mh, t * MLA_ROPE // LANES, LANES), "mla_kpe_sum", 1024).reshape(t, MLA_ROPE)

    stats, do_db = _dil_stats(do_dil, o_dil, lse_tot, 512)
    dqs, dks, dvs, dtiles = [], [], [], []
    for b, dil in enumerate(DIL_DILATIONS):
        dq_b, dk_b, dv_b, db_b = _dil_bwd(qn[b], kn[b], v_d[b], do_db[b], stats[b], bias[b], dil, f"dil_bwd_{dil}")
        dqs.append(dq_b)
        dks.append(dk_b)
        dvs.append(dv_b)
        dtiles.append(db_b)
    grads_s["rel_bias"] = _bias_grad(jnp.stack(dtiles).reshape(3, nh, QB, QB + DIL_W))
    dq_a, dgq = _head_norm_bwd(dqs, proj, 0, gq, "dil_q_norm_bwd", 512)
    dk_a, dgk = _head_norm_bwd(dks, proj, 1, gk, "dil_k_norm_bwd", 512)
    grads_s["dil_q_norm"], grads_s["dil_k_norm"] = dgq[:, :hd], dgk[:, :hd]
    dv_a = _sum_branches(dvs, "dil_dv_sum", 512)

    dparts = [dq_a, dk_a, dv_a, d_cq, d_ckv, d_kpe]
    t2 = min(512, t)
    pairs, dw_parts, lo = [], [], 0
    for n, dpart in enumerate(dparts):
        width = dpart.shape[1]
        w_part = w_in[:, lo:lo + width]
        pairs.append((dpart, pl.BlockSpec((t2, width), lambda i, j, r: (i, 0)),
                      w_part, pl.BlockSpec((D_MODEL, width), lambda i, j, r: (0, 0))))
        dw_parts.append(_mm_simple(f"in_proj_dw_{n}", hm, dpart, TN, F32, tk=2048))
        lo += width
    dw_in = jnp.concatenate(dw_parts, axis=1)
    grads_b["w_in"] = dw_in.reshape(D_MODEL, N_CHIPS, -1).transpose(1, 0, 2)
    early = tuple(grads_b[n] for n in EARLY) if exchanges else ()
    row2 = pl.BlockSpec((t2, D_MODEL), lambda i, j, r: (i, 0))
    res = _mm("in_proj_dx", (t // t2, 1, 1), pairs, NT, _sds((t, D_MODEL), F32), row2, (t2, D_MODEL),
              res=(dx2, row2), norm=(x1, small["mix_norm"]), outgoing=early, exchange="cores")
    dx1, grads_s["mix_norm"] = res[0], res[1]
    outgoing = exchanges[0](early, res[2]) if exchanges else ()
    dx, grads_s["ffn1_norm"], grads_b["ffn1_w_gate"], grads_b["ffn1_w_up"], grads_b["ffn1_w_down"], arrived, late = _ffn_bwd(
        dx1, x, small["ffn1_norm"], wfull["ffn1_w_gate"], wfull["ffn1_w_up"], wfull["ffn1_w_down"], ffn1_saved, "ffn1",
        outgoing, exchanges[1] if exchanges else None)
    return loss, dx, grads_s, grads_b, (tuple(outgoing), arrived), late


def kernel(x, ffn1_norm, ffn1_w_gate, ffn1_w_up, ffn1_w_down, mix_norm, w_in, dil_q_norm, dil_k_norm, rel_bias, mla_q_a_norm, mla_w_q_b, mla_kv_a_norm, mla_w_kv_b, mla_q_norm, mla_k_norm, out_norm_dil, out_norm_mla, w_out, ffn2_norm, ffn2_w_gate, ffn2_w_up, ffn2_w_down, loss_target, m_ffn1_norm, m_ffn1_w_gate, m_ffn1_w_up, m_ffn1_w_down, m_mix_norm, m_w_in, m_dil_q_norm, m_dil_k_norm, m_rel_bias, m_mla_q_a_norm, m_mla_w_q_b, m_mla_kv_a_norm, m_mla_w_kv_b, m_mla_q_norm, m_mla_k_norm, m_out_norm_dil, m_out_norm_mla, m_w_out, m_ffn2_norm, m_ffn2_w_gate, m_ffn2_w_up, m_ffn2_w_down, v_ffn1_norm, v_ffn1_w_gate, v_ffn1_w_up, v_ffn1_w_down, v_mix_norm, v_w_in, v_dil_q_norm, v_dil_k_norm, v_rel_bias, v_mla_q_a_norm, v_mla_w_q_b, v_mla_kv_a_norm, v_mla_w_kv_b, v_mla_q_norm, v_mla_k_norm, v_out_norm_dil, v_out_norm_mla, v_w_out, v_ffn2_norm, v_ffn2_w_gate, v_ffn2_w_up, v_ffn2_w_down):
    given = dict(locals())
    big_names = [name for name, _ in BIG]
    small_names = [name for name, _, _ in SMALL]

    chip = (2 * lax.axis_index("x") + lax.axis_index("y")).astype(jnp.int32)
    core = lax.axis_index("c").astype(jnp.int32)
    mine = {n: given[n].astype(BF16) for n in big_names}

    def with_own(names, arrays):
        return {n: lax.dynamic_update_slice(a, mine[n], (chip, 0, 0)) for n, a in zip(names, arrays)}

    wfirst = with_own(LATE, _gather_weights([mine[n][0] for n in LATE]))
    later_weights = ([mine[n][0] for n in EARLY], lambda partly: with_own(EARLY, _forward_cores(partly)))
    small = {n: given[n] for n in small_names}

    def early_partials(partial, theirs):
        return _add_halves(partial, theirs, core.reshape(1), "early")

    def late_partials(partial):
        return _add_halves(partial, _reduce_cores(partial, "late"), core.reshape(1), "late")

    exchanges = (early_partials, late_partials)
    loss, dx, grads_s, grads_b, (early_part, early_got), (late_part, late_got) = _local_step(
        x[0], loss_target[0], small, wfirst, exchanges, later_weights)
    loss = lax.psum(loss[0, 0], ("x", "y", "c"))
    reduced = _sum_partials(tuple(late_got) + tuple(early_got), tuple(late_part) + tuple(early_part),
                            jnp.stack([chip, core]))
    g_big = dict(zip(LATE + EARLY, _share_cores(reduced)))
    g_small = _unpack_small(_allreduce_small(_pack_small(grads_s)))

    grad, delta, new_m, new_v = {}, {}, {}, {}
    for name, shape in BIG:
        g2 = g_big[name]
        d_, m_, v_ = _adamw(given[name].reshape(shape), g2, given["m_" + name].reshape(shape),
                            given["v_" + name].reshape(shape), f"adamw_{name}")
        full = given[name].shape
        grad[name], delta[name], new_m[name], new_v[name] = (a.reshape(full) for a in (g2, d_, m_, v_))
    for name in small_names:
        grad[name] = g_small[name]
        delta[name], new_m[name], new_v[name] = _adamw(given[name], g_small[name], given["m_" + name],
                                                       given["v_" + name], f"adamw_{name}")

    return (loss, dx[None], *[grad[n] for n in WEIGHTS], *[delta[n] for n in WEIGHTS],
            *[new_m[n] for n in WEIGHTS], *[new_v[n] for n in WEIGHTS])
```

```python
import functools

import numpy as np
import jax
import jax.numpy as jnp
from jax import lax
from jax.experimental import pallas as pl
from jax.experimental.pallas import tpu as pltpu

F32 = jnp.float32
BF16 = jnp.bfloat16

D_MODEL = 1024
D_FF = 2816
N_CHIPS = 4
DIL_HEADS = 8
DIL_HD = 64
DIL_WIDTH = 512
DIL_DILATIONS = (1, 4, 16)
DIL_W = 128
QB = 128
MLA_HEADS = 4
MLA_NOPE = 128
MLA_ROPE = 64
MLA_QK = 192
MLA_V = 128
MLA_Q_RANK = 256
MLA_KV_RANK = 128
ROPE_BASE = 10000.0
REL_BUCKETS = 32
REL_MAX_DIST = 2048
FFN_RESID = 0.5
EPS = 1e-6
NEG = -1e30
LANES = 128

ADAM_LR = 0.001
ADAM_B1 = 0.9
ADAM_B2 = 0.999
ADAM_EPS = 1e-08
ADAM_WD = 0.01
ADAM_STEP = 10

NT = (((1,), (1,)), ((), ()))
NN = (((1,), (0,)), ((), ()))
TN = (((0,), (0,)), ((), ()))

BIG = (
    ("ffn1_w_gate", (D_MODEL, D_FF // N_CHIPS)),
    ("ffn1_w_up", (D_MODEL, D_FF // N_CHIPS)),
    ("ffn1_w_down", (D_FF // N_CHIPS, D_MODEL)),
    ("w_in", (D_MODEL, 1984 // N_CHIPS)),
    ("mla_w_q_b", (MLA_Q_RANK, MLA_QK)),
    ("mla_w_kv_b", (MLA_KV_RANK, MLA_NOPE + MLA_V)),
    ("w_out", (D_MODEL // N_CHIPS, D_MODEL)),
    ("ffn2_w_gate", (D_MODEL, D_FF // N_CHIPS)),
    ("ffn2_w_up", (D_MODEL, D_FF // N_CHIPS)),
    ("ffn2_w_down", (D_FF // N_CHIPS, D_MODEL)),
)
SMALL = (
    ("ffn1_norm", (1, 1024), 8), ("mix_norm", (1, 1024), 8), ("dil_q_norm", (1, 64), 1),
    ("dil_k_norm", (1, 64), 1), ("rel_bias", (8, 32), 2), ("mla_q_a_norm", (1, 256), 2),
    ("mla_kv_a_norm", (1, 128), 1), ("mla_q_norm", (1, 192), 2), ("mla_k_norm", (1, 192), 2),
    ("out_norm_dil", (1, 512), 4), ("out_norm_mla", (1, 512), 4), ("ffn2_norm", (1, 1024), 8),
)
SMALL_ROWS = 48
WEIGHTS = ("ffn1_norm", "ffn1_w_gate", "ffn1_w_up", "ffn1_w_down", "mix_norm", "w_in", "dil_q_norm",
           "dil_k_norm", "rel_bias", "mla_q_a_norm", "mla_w_q_b", "mla_kv_a_norm", "mla_w_kv_b",
           "mla_q_norm", "mla_k_norm", "out_norm_dil", "out_norm_mla", "w_out", "ffn2_norm",
           "ffn2_w_gate", "ffn2_w_up", "ffn2_w_down")


def _pcall(body, **kw):
    return pl.pallas_call(body, **kw)


def _cparams(*sem):
    return pltpu.CompilerParams(dimension_semantics=sem)


def _sds(shape, dtype):
    return jax.ShapeDtypeStruct(shape, dtype)


def _dot(a, b, dn):
    return lax.dot_general(a, b, dn, preferred_element_type=F32)


def _rms_fwd(x, g, out_dtype, name, tm):
    n, d = x.shape
    tm = min(tm, n)

    def body(x_ref, g_ref, o_ref):
        xf = x_ref[...].astype(F32)
        r = lax.rsqrt(jnp.mean(xf * xf, axis=-1, keepdims=True) + EPS)
        o_ref[...] = (xf * r * g_ref[...]).astype(o_ref.dtype)

    return _pcall(
        body, name=name, grid=(n // tm,),
        in_specs=[pl.BlockSpec((tm, d), lambda i: (i, 0)), pl.BlockSpec((1, d), lambda i: (0, 0))],
        out_specs=pl.BlockSpec((tm, d), lambda i: (i, 0)),
        out_shape=_sds((n, d), out_dtype), compiler_params=_cparams("parallel"))(x, g)


def _rms_bwd(dys, x, g, res, name, tm):
    n, d = x.shape
    tm = min(tm, n)
    nd = len(dys)
    has_res = res is not None

    def body(*refs):
        dy_refs = refs[:nd]
        x_ref, g_ref = refs[nd], refs[nd + 1]
        res_ref = refs[nd + 2] if has_res else None
        dx_ref, dg_ref = refs[-2], refs[-1]
        dy = dy_refs[0][...].astype(F32)
        for r_ in dy_refs[1:]:
            dy = dy + r_[...].astype(F32)
        xf = x_ref[...].astype(F32)
        r = lax.rsqrt(jnp.mean(xf * xf, axis=-1, keepdims=True) + EPS)
        xh = xf * r
        dxh = dy * g_ref[...]
        dx = r * (dxh - xh * jnp.mean(dxh * xh, axis=-1, keepdims=True))
        if has_res:
            dx = dx + res_ref[...]
        dx_ref[...] = dx

        @pl.when(pl.program_id(0) == 0)
        def _():
            dg_ref[...] = jnp.zeros_like(dg_ref)

        dg_ref[...] += jnp.sum(dy * xh, axis=0, keepdims=True)

    row = pl.BlockSpec((tm, d), lambda i: (i, 0))
    vec = pl.BlockSpec((1, d), lambda i: (0, 0))
    ins = list(dys) + [x, g] + ([res] if has_res else [])
    return _pcall(
        body, name=name, grid=(n // tm,),
        in_specs=[row] * nd + [row, vec] + ([row] if has_res else []),
        out_specs=(row, vec),
        out_shape=(_sds((n, d), F32), _sds((1, d), F32)),
        compiler_params=_cparams("arbitrary"))(*ins)


def _mm(name, grid, pairs, dn, out_shape, out_spec, acc_shape, res=None, scale=1.0, outgoing=(), norm=None,
        exchange="chips"):
    npairs = len(pairs)
    nred = grid[2]
    has_res = res is not None
    has_norm = norm is not None
    no = len(outgoing)
    ex_start, ex_wait, ex_shapes, ex_sems = EXCHANGES[exchange]

    def body(*refs):
        ab = refs[:2 * npairs]
        res_ref = refs[2 * npairs] if has_res else None
        nin = 2 * npairs + int(has_res) + 2 * int(has_norm)
        if has_norm:
            x_ref, g_ref = refs[nin - 2:nin]
        first_out = nin + no
        sent = refs[nin:first_out]
        o_ref = refs[first_out]
        nout = 1 + int(has_norm)
        dg_ref = refs[first_out + 1] if has_norm else None
        arrived = refs[first_out + nout:first_out + nout + no]
        acc_ref = refs[first_out + nout + no] if nred > 1 else None
        if no:
            send_sems, recv_sems = refs[-2:]
            ids = [pl.program_id(n) for n in range(3)]

            @pl.when((ids[0] == 0) & (ids[1] == 0) & (ids[2] == 0))
            def _():
                ex_start(sent, arrived, send_sems, recv_sems)

        tot = None
        for p in range(npairs):
            d = _dot(ab[2 * p][...].astype(BF16), ab[2 * p + 1][...].astype(BF16), dn)
            tot = d if tot is None else tot + d

        def finish(v):
            if scale != 1.0:
                v = v * scale
            if has_norm:
                xf = x_ref[...]
                r = lax.rsqrt(jnp.mean(xf * xf, axis=-1, keepdims=True) + EPS)
                xh = xf * r
                dxh = v * g_ref[...]

                @pl.when(pl.program_id(0) == 0)
                def _():
                    dg_ref[...] = jnp.zeros_like(dg_ref)

                dg_ref[...] += jnp.sum(v * xh, axis=0, keepdims=True)
                v = r * (dxh - xh * jnp.mean(dxh * xh, axis=-1, keepdims=True))
            if has_res:
                v = res_ref[...] + v
            o_ref[...] = v.astype(o_ref.dtype)

        if nred == 1:
            finish(tot)
        else:
            r = pl.program_id(2)

            @pl.when(r == 0)
            def _():
                acc_ref[...] = tot

            @pl.when(r > 0)
            def _():
                acc_ref[...] += tot

            @pl.when(r == nred - 1)
            def _():
                finish(acc_ref[...])

        if no:
            @pl.when((ids[0] == grid[0] - 1) & (ids[1] == grid[1] - 1) & (ids[2] == nred - 1))
            def _():
                ex_wait(sent, arrived, send_sems, recv_sems)

    ins, specs = [], []
    for a, a_spec, b, b_spec in pairs:
        ins += [a, b]
        specs += [a_spec, b_spec]
    if has_res:
        ins.append(res[0])
        specs.append(res[1])
    scratch = [pltpu.VMEM(acc_shape, F32)] if nred > 1 else []
    if not no and not has_norm:
        return _pcall(
            body, name=name, grid=grid, in_specs=specs, out_specs=out_spec, out_shape=out_shape,
            scratch_shapes=scratch, compiler_params=_cparams("parallel", "parallel", "arbitrary"))(*ins)
    out_specs, out_shapes = (out_spec,), (out_shape,)
    if has_norm:
        assert grid[1] == 1
        d = norm[1].shape[1]
        ins += [norm[0], norm[1]]
        specs += [out_spec, pl.BlockSpec((1, d), lambda i, j, r: (0, 0))]
        out_specs += (pl.BlockSpec((1, d), lambda i, j, r: (0, 0)),)
        out_shapes += (_sds((1, d), F32),)
    hbm = pl.BlockSpec(memory_space=pltpu.HBM)
    res_ = tuple(_pcall(
        body, name=name, grid=grid, in_specs=specs + [hbm] * no, out_specs=out_specs + (hbm,) * no,
        out_shape=out_shapes + ex_shapes(outgoing),
        scratch_shapes=scratch + (ex_sems(no) if no else []),
        compiler_params=_cparams("arbitrary", "arbitrary", "arbitrary"))(*ins, *outgoing))
    nout = len(out_shapes)
    return res_[:nout] + ((res_[nout:],) if no else ())


def _ffn_up(h, wg, wu, name, tm, incoming=()):
    t, d = h.shape
    nc, _, fs = wg.shape
    tm = min(tm, t)
    nt = t // tm
    ni = len(incoming)
    halves = _halves(incoming)

    def body(*refs):
        h_ref, wg_ref, wu_ref = refs[:3]
        srcs = refs[3:3 + ni]
        g_ref, u_ref, a_ref = refs[3 + ni:6 + ni]
        outs = refs[6 + ni:6 + 2 * ni]
        if ni:
            send_sems, recv_sems = refs[6 + 2 * ni:]
            c, i = pl.program_id(0), pl.program_id(1)

            @pl.when((c == 0) & (i == 0))
            def _():
                _gather_start(srcs, outs, halves, send_sems, recv_sems)

        hh = h_ref[...]
        gate = _dot(hh, wg_ref[...], NN)
        up = _dot(hh, wu_ref[...], NN)
        sig = jax.nn.sigmoid(gate)
        silu = gate * sig
        g_ref[...] = (up * (sig + silu * (1.0 - sig))).astype(BF16)
        u_ref[...] = silu.astype(BF16)
        a_ref[...] = (silu * up).astype(BF16)

        if ni:
            @pl.when((c == nc - 1) & (i == nt - 1))
            def _():
                _gather_wait(outs, halves, send_sems, recv_sems)

    wspec = pl.BlockSpec((None, d, fs), lambda c, i: (c, 0, 0))
    ospec = pl.BlockSpec((None, tm, fs), lambda c, i: (c, i, 0))
    hbm = pl.BlockSpec(memory_space=pltpu.HBM)
    osd = _sds((nc, t, fs), BF16)
    res = tuple(_pcall(
        body, name=name, grid=(nc, nt),
        in_specs=[pl.BlockSpec((tm, d), lambda c, i: (i, 0)), wspec, wspec] + [hbm] * ni,
        out_specs=(ospec, ospec, ospec) + (hbm,) * ni,
        out_shape=(osd, osd, osd) + tuple(_sds((N_CHIPS,) + b.shape, b.dtype) for b in incoming),
        scratch_shapes=[pltpu.SemaphoreType.DMA((3 * ni,)), pltpu.SemaphoreType.DMA((3 * ni,))] if ni else [],
        compiler_params=_cparams("arbitrary", "arbitrary"))(h, wg, wu, *incoming))
    return res[:3] + (res[3:],)


def _ffn_hidden_bwd(dy, h, wd, dact_dgate, dact_dup, act, name, tm, outgoing=()):
    t, d = dy.shape
    nc, fs, _ = wd.shape
    tm = min(tm, t)
    nt = t // tm
    no = len(outgoing)

    def body(*refs):
        dy_ref, h_ref, wd_ref, g_ref, u_ref, a_ref = refs[:6]
        sent = refs[6:6 + no]
        dg_ref, du_ref, dwg_hbm, dwu_hbm, dwd_hbm = refs[6 + no:11 + no]
        arrived = refs[11 + no:11 + 2 * no]
        wg_acc, wu_acc, wd_acc, sem = refs[11 + 2 * no:15 + 2 * no]
        c, i = pl.program_id(0), pl.program_id(1)
        if no:
            send_sems, recv_sems = refs[15 + 2 * no:]

            @pl.when((c == 0) & (i == 0))
            def _():
                _scatter_start(sent, arrived, send_sems, recv_sems)

        dyb = dy_ref[...].astype(BF16)
        da = _dot(dyb, wd_ref[...], NT) * FFN_RESID
        dgate = (da * g_ref[...].astype(F32)).astype(BF16)
        dup = (da * u_ref[...].astype(F32)).astype(BF16)
        dg_ref[...] = dgate
        du_ref[...] = dup
        hh = h_ref[...]
        parts = (_dot(hh, dgate, TN), _dot(hh, dup, TN), _dot(a_ref[...], dyb, TN) * FFN_RESID)
        accs = (wg_acc, wu_acc, wd_acc)

        @pl.when(i == 0)
        def _():
            for acc, part in zip(accs, parts):
                acc[...] = part

        @pl.when(i > 0)
        def _():
            for acc, part in zip(accs, parts):
                acc[...] += part

        @pl.when(i == nt - 1)
        def _():
            copies = [pltpu.make_async_copy(acc, out.at[c], sem.at[n])
                      for n, (acc, out) in enumerate(zip(accs, (dwg_hbm, dwu_hbm, dwd_hbm)))]
            for cp in copies:
                cp.start()
            for cp in copies:
                cp.wait()

        if no:
            @pl.when((c == nc - 1) & (i == nt - 1))
            def _():
                _scatter_wait(sent, arrived, send_sems, recv_sems)

    tok = pl.BlockSpec((tm, d), lambda c, i: (i, 0))
    cspec = pl.BlockSpec((None, tm, fs), lambda c, i: (c, i, 0))
    hbm = pl.BlockSpec(memory_space=pltpu.HBM)
    osd = _sds((nc, t, fs), BF16)
    res = _pcall(
        body, name=name, grid=(nc, nt),
        in_specs=[tok, tok, pl.BlockSpec((None, fs, d), lambda c, i: (c, 0, 0)), cspec, cspec, cspec] + [hbm] * no,
        out_specs=(cspec, cspec, hbm, hbm, hbm) + (hbm,) * no,
        out_shape=(osd, osd, _sds((nc, d, fs), F32), _sds((nc, d, fs), F32), _sds((nc, fs, d), F32))
        + _scatter_shapes(outgoing),
        scratch_shapes=[pltpu.VMEM((d, fs), F32), pltpu.VMEM((d, fs), F32), pltpu.VMEM((fs, d), F32),
                        pltpu.SemaphoreType.DMA((3,))] + (_scatter_sems(no) if no else []),
        compiler_params=_cparams("arbitrary", "arbitrary"))(dy, h, wd, dact_dgate, dact_dup, act, *outgoing)
    res = tuple(res)
    return res[:5] + (res[5:],)


def _ffn_fwd(x, g, wg, wu, wd, tag, incoming=(), target=None):
    t = x.shape[0]
    nc, _, fs = wg.shape
    tm = min(512, t)
    h = _rms_fwd(x, g, BF16, f"{tag}_norm", 512)
    dact_dgate, dact_dup, act, partly = _ffn_up(h, wg, wu, f"{tag}_up", 1024, incoming)
    if target is not None:
        return _ffn_down_loss(act, wd, x, target, f"{tag}_down_loss", 512), (h, dact_dgate, dact_dup, act), partly
    pairs = [(act, pl.BlockSpec((None, tm, fs), lambda i, j, r, c=c: (c, i, 0)),
              wd, pl.BlockSpec((None, fs, D_MODEL), lambda i, j, r, c=c: (c, 0, 0))) for c in range(nc)]
    row = pl.BlockSpec((tm, D_MODEL), lambda i, j, r: (i, 0))
    y = _mm(f"{tag}_down", (t // tm, 1, 1), pairs, NN, _sds((t, D_MODEL), F32), row, (tm, D_MODEL),
            res=(x, row), scale=FFN_RESID)
    return y, (h, dact_dgate, dact_dup, act), partly


def _ffn_bwd(dy, x, g, wg, wu, wd, saved, tag, outgoing=(), own_exchange=None):
    h, dact_dgate, dact_dup, act = saved
    t = x.shape[0]
    nc, _, fs = wg.shape
    tm = min(512, t)
    dgate, dup, dwg, dwu, dwd, arrived = _ffn_hidden_bwd(dy, h, wd, dact_dgate, dact_dup, act,
                                                         f"{tag}_hidden_bwd", 1024, outgoing)
    pairs = []
    for c in range(nc):
        a_spec = pl.BlockSpec((None, tm, fs), lambda i, j, r, c=c: (c, i, 0))
        w_spec = pl.BlockSpec((None, D_MODEL, fs), lambda i, j, r, c=c: (c, 0, 0))
        pairs += [(dgate, a_spec, wg, w_spec), (dup, a_spec, wu, w_spec)]
    own_part = tuple(own_exchange([dwg, dwu, dwd])) if own_exchange else ()
    row = pl.BlockSpec((tm, D_MODEL), lambda i, j, r: (i, 0))
    res = _mm(f"{tag}_dh", (t // tm, 1, 1), pairs, NT, _sds((t, D_MODEL), F32), row, (tm, D_MODEL),
              res=(dy, row), norm=(x, g), outgoing=own_part)
    dx, dg = res[0], res[1]
    own_got = res[2] if own_part else ()
    return dx, dg, dwg, dwu, dwd, arrived, (own_part, own_got)


def _mm_tn_multi(name, a, bs, tk):
    k, m = a.shape
    tk = min(tk, k)
    nb = len(bs)

    def body(*refs):
        a_ref, b_refs, o_refs = refs[0], refs[1:1 + nb], refs[1 + nb:]
        aa = a_ref[...].astype(BF16)
        parts = [_dot(aa, b_ref[...].astype(BF16), TN) for b_ref in b_refs]

        @pl.when(pl.program_id(0) == 0)
        def _():
            for o_ref, part in zip(o_refs, parts):
                o_ref[...] = part

        @pl.when(pl.program_id(0) > 0)
        def _():
            for o_ref, part in zip(o_refs, parts):
                o_ref[...] += part

    return _pcall(
        body, name=name, grid=(k // tk,),
        in_specs=[pl.BlockSpec((tk, m), lambda r: (r, 0))] + [pl.BlockSpec((tk, b.shape[1]), lambda r: (r, 0)) for b in bs],
        out_specs=tuple(pl.BlockSpec((m, b.shape[1]), lambda r: (0, 0)) for b in bs),
        out_shape=tuple(_sds((m, b.shape[1]), F32) for b in bs),
        compiler_params=_cparams("arbitrary"))(a, *bs)


def _mm_simple(name, a, b, dn, out_dtype, tm=512, tk=512, res=None, scale=1.0):
    if dn == TN:
        k, m = a.shape
        n = b.shape[1]
        tk = min(tk, k)
        return _mm(name, (1, 1, k // tk),
                   [(a, pl.BlockSpec((tk, m), lambda i, j, r: (r, 0)), b, pl.BlockSpec((tk, n), lambda i, j, r: (r, 0)))],
                   TN, _sds((m, n), out_dtype), pl.BlockSpec((m, n), lambda i, j, r: (0, 0)), (m, n), scale=scale)
    m, k = a.shape
    n = b.shape[1] if dn == NN else b.shape[0]
    tm = min(tm, m)
    row = pl.BlockSpec((tm, n), lambda i, j, r: (i, 0))
    return _mm(name, (m // tm, 1, 1),
               [(a, pl.BlockSpec((tm, k), lambda i, j, r: (i, 0)), b, pl.BlockSpec(b.shape, lambda i, j, r: (0, 0)))],
               dn, _sds((m, n), out_dtype), row, (tm, n), res=None if res is None else (res, row), scale=scale)


def _t5_bucket(dist):
    max_exact = REL_BUCKETS // 2
    d = np.maximum(dist, 1).astype(np.float32)
    large = max_exact + (np.log(d / max_exact) / np.log(REL_MAX_DIST / max_exact)
                         * (REL_BUCKETS - max_exact)).astype(np.int32)
    large = np.minimum(large, REL_BUCKETS - 1)
    return np.where(dist < max_exact, dist, large).astype(np.int32)


def _bucket_tiles():
    i = np.arange(QB)[:, None]
    j = np.arange(QB + DIL_W)[None, :]
    delta = np.clip(i + DIL_W - j, 0, None)
    return np.stack([_t5_bucket(delta * dil) for dil in DIL_DILATIONS]).astype(np.int32)


def _bias_tiles(rel_bias):
    buckets = jnp.asarray(_bucket_tiles())

    def body(rb_ref, bk_ref, o_ref):
        bk = bk_ref[...]
        for h in range(DIL_HEADS):
            def pick(b, tile):
                return jnp.where(bk == b, rb_ref[h, b], tile)

            o_ref[h] = lax.fori_loop(0, REL_BUCKETS, pick, jnp.zeros((QB, QB + DIL_W), F32))

    return _pcall(
        body, name="dil_bias_tiles", grid=(3,),
        in_specs=[pl.BlockSpec(memory_space=pltpu.SMEM),
                  pl.BlockSpec((None, QB, QB + DIL_W), lambda b: (b, 0, 0))],
        out_specs=pl.BlockSpec((None, DIL_HEADS, QB, QB + DIL_W), lambda b: (b, 0, 0, 0)),
        out_shape=_sds((3, DIL_HEADS, QB, QB + DIL_W), F32),
        compiler_params=_cparams("parallel"))(rel_bias, buckets)


def _bias_grad(dtiles):
    buckets = jnp.asarray(_bucket_tiles())

    def body(dt_ref, bk_ref, o_ref):
        def one(b, carry):
            hit = [bk_ref[br] == b for br in range(3)]
            for h in range(DIL_HEADS):
                tot = jnp.zeros((), F32)
                for br in range(3):
                    tot = tot + jnp.sum(jnp.where(hit[br], dt_ref[br, h], 0.0))
                o_ref[h, b] = tot
            return carry

        lax.fori_loop(0, REL_BUCKETS, one, 0)

    return _pcall(
        body, name="dil_bias_grad",
        in_specs=[pl.BlockSpec(memory_space=pltpu.VMEM), pl.BlockSpec(memory_space=pltpu.VMEM)],
        out_specs=pl.BlockSpec(memory_space=pltpu.SMEM),
        out_shape=_sds((DIL_HEADS, REL_BUCKETS), F32))(dtiles, buckets)


def _split_heads(a, lo):
    zero = jnp.zeros_like(a)
    return jnp.concatenate([jnp.where(lo, a, zero), jnp.where(lo, zero, a)], axis=0)


def _side_by_side(a):
    n = a.shape[0] // 2
    return jnp.concatenate([a[:n], a[n:]], axis=1)


def _band_masks(prev_ok):
    ii = lax.broadcasted_iota(jnp.int32, (2 * QB, QB), 0) & (QB - 1)
    jj = lax.broadcasted_iota(jnp.int32, (2 * QB, QB), 1)
    return jj <= ii, jj >= ii + jnp.where(prev_ok, 0, QB)


def _dil_fwd(q, k, v, bias, dil, name):
    w = DIL_WIDTH
    t = q.shape[0] * dil
    npair = w // LANES
    nl = t // dil // QB
    scale = DIL_HD ** -0.5

    def body(q_ref, kc_ref, kp_ref, vc_ref, vp_ref, b_ref, o_ref, lse_ref):
        nn = pl.program_id(1)
        lo = lax.broadcasted_iota(jnp.int32, (QB, LANES), 1) < DIL_HD
        lo2 = lax.broadcasted_iota(jnp.int32, (2 * QB, LANES), 1) < DIL_HD
        ii = lax.broadcasted_iota(jnp.int32, (2 * QB, 2 * QB), 0) & (QB - 1)
        jj = lax.broadcasted_iota(jnp.int32, (2 * QB, 2 * QB), 1)
        first_key = jnp.maximum(ii, jnp.where(nn != 0, 0, QB))
        valid = (jj >= first_key) & (jj <= ii + QB)
        for p in range(npair):
            cols = slice(p * LANES, (p + 1) * LANES)
            qq = _split_heads(q_ref[:, cols], lo)
            kk = jnp.concatenate([kp_ref[:, cols], kc_ref[:, cols]], axis=0)
            vv = jnp.concatenate([vp_ref[:, cols], vc_ref[:, cols]], axis=0)
            s = jnp.where(valid, _dot(qq, kk, NT) * scale + b_ref[p], NEG)
            m = jnp.max(s, axis=-1, keepdims=True)
            e = jnp.exp(s - m)
            den = jnp.sum(e, axis=-1, keepdims=True)
            pn = (e * (1.0 / den)).astype(BF16)
            o_ref[:, cols] = _dot(_side_by_side(pn), _split_heads(vv, lo2), NN)
            lse = m + jnp.log(den)
            lse_ref[:, cols] = jnp.where(lo, lse[:QB], lse[QB:])

    cur = pl.BlockSpec((QB, w), lambda r, n: (n, r))
    prev = pl.BlockSpec((QB, w), lambda r, n: (jnp.maximum(n - 1, 0), r))
    sd = _sds((t // dil, dil * w), F32)
    return _pcall(
        body, name=name, grid=(dil, nl),
        in_specs=[cur, cur, prev, cur, prev, pl.BlockSpec((npair, 2 * QB, 2 * QB), lambda r, n: (0, 0, 0))],
        out_specs=(cur, cur), out_shape=(sd, sd),
        compiler_params=_cparams("parallel", "parallel"))(q, k, k, v, v, bias)


def _dil_bwd(q, k, v, do, stats, bias, dil, name):
    w = DIL_WIDTH
    t = q.shape[0] * dil
    npair = w // LANES
    nl = t // dil // QB
    scale = DIL_HD ** -0.5

    def body(qc_ref, qn_ref, doc_ref, don_ref, sc_ref, sn_ref, k_ref, v_ref, b_ref,
             dq_ref, dk_ref, dv_ref, db_ref, carry):
        r, nn = pl.program_id(0), pl.program_id(1)
        lo = lax.broadcasted_iota(jnp.int32, (QB, LANES), 1) < DIL_HD
        cur_ok, prev_ok = _band_masks(nn + 1 < nl)

        @pl.when((r == 0) & (nn == 0))
        def _():
            db_ref[...] = jnp.zeros_like(db_ref)
            carry[...] = jnp.zeros_like(carry)

        for p in range(npair):
            cols = slice(p * LANES, (p + 1) * LANES)
            kp, vp = k_ref[:, cols], v_ref[:, cols]
            k2 = _split_heads(kp, lo)

            def column(ref, lane):
                first = p * LANES + lane
                return jnp.concatenate([ref[:, first:first + 1], ref[:, first + DIL_HD:first + DIL_HD + 1]], axis=0)

            def side(q_ref, do_ref, s_ref, bias, ok):
                qq = _split_heads(q_ref[:, cols], lo)
                dd = _split_heads(do_ref[:, cols], lo)
                s = jnp.where(ok, _dot(qq, kp, NT) * scale + bias, NEG)
                prob = jnp.exp(s - column(s_ref, 0))
                ds = prob * (_dot(dd, vp, NT) - column(s_ref, DIL_HD // 2))
                return qq, dd, prob.astype(BF16), ds

            q1, d1, p1, ds1 = side(qc_ref, doc_ref, sc_ref, b_ref[p, :, QB:], cur_ok)
            q2, d2, p2, ds2 = side(qn_ref, don_ref, sn_ref, b_ref[p, :, :QB], prev_ok)
            ds1b, ds2b = ds1.astype(BF16), ds2.astype(BF16)
            dq_ref[:, cols] = carry[:, cols] + _dot(_side_by_side(ds1b), k2, NN) * scale
            carry[:, cols] = _dot(_side_by_side(ds2b), k2, NN) * scale
            dk_ref[:, cols] = _dot(jnp.concatenate([ds1b, ds2b], axis=0), jnp.concatenate([q1, q2], axis=0), TN) * scale
            dv_ref[:, cols] = _dot(jnp.concatenate([p1, p2], axis=0), jnp.concatenate([d1, d2], axis=0), TN)
            db_ref[p, :, QB:] += ds1
            db_ref[p, :, :QB] += ds2

    cur = pl.BlockSpec((QB, w), lambda r, n: (n, r))
    nxt = pl.BlockSpec((QB, w), lambda r, n: (jnp.minimum(n + 1, nl - 1), r))
    tile = pl.BlockSpec((npair, 2 * QB, 2 * QB), lambda r, n: (0, 0, 0))
    sd = _sds((t // dil, dil * w), F32)
    return _pcall(
        body, name=name, grid=(dil, nl),
        in_specs=[cur, nxt, cur, nxt, cur, nxt, cur, cur, tile],
        out_specs=(cur, cur, cur, tile),
        out_shape=(sd, sd, sd, _sds((npair, 2 * QB, 2 * QB), F32)),
        scratch_shapes=[pltpu.VMEM((QB, w), F32)],
        compiler_params=_cparams("arbitrary", "arbitrary"))(q, q, do, do, stats, stats, k, v, bias)


def _head_sum_matrix(scale):
    idx = np.arange(DIL_WIDTH) // DIL_HD
    return jnp.asarray((idx[:, None] == idx[None, :]).astype(np.float32) * scale, BF16)


def _head_sum(x, mat):
    hi = x.astype(BF16)
    lo = (x - hi.astype(F32)).astype(BF16)
    return _dot(hi, mat, NN) + _dot(lo, mat, NN)


def _to_views(src, tmp, out_refs):
    tm, w = src.shape
    for j in range(w // LANES):
        tmp[j] = src[:, j * LANES:(j + 1) * LANES]
    for d, o_ref in zip(DIL_DILATIONS, out_refs):
        if d == 1:
            o_ref[...] = src.astype(o_ref.dtype)
            continue
        for r in range(d):
            for j in range(w // LANES):
                lo = r * w + j * LANES
                o_ref[:, lo:lo + LANES] = tmp[j, pl.ds(r, tm // d, stride=d), :].astype(o_ref.dtype)


def _from_view(v_ref, tmp, d):
    tm = tmp.shape[1]
    w = v_ref.shape[1] // d
    for r in range(d):
        for j in range(w // LANES):
            lo = r * w + j * LANES
            tmp[j, pl.ds(r, tm // d, stride=d), :] = v_ref[:, lo:lo + LANES]
    return jnp.concatenate([tmp[j] for j in range(w // LANES)], axis=1)


def _view_specs(tm, t, dtype):
    specs = tuple(pl.BlockSpec((tm // d, d * DIL_WIDTH), lambda i: (i, 0)) for d in DIL_DILATIONS)
    shapes = tuple(_sds((t // d, d * DIL_WIDTH), dtype) for d in DIL_DILATIONS)
    return specs, shapes


def _view_scratch(tm):
    return pltpu.VMEM((DIL_WIDTH // LANES, tm, LANES), F32)


def _dil_merge(outs, lses, g, tm):
    w = DIL_WIDTH
    t = outs[0].shape[0]
    tm = min(tm, t)

    def body(o0, o1, o2, l0, l1, l2, g_ref, o_ref, l_ref, n_ref, so1, so2, sl1, sl2):
        d1, d2 = DIL_DILATIONS[1], DIL_DILATIONS[2]
        a0, a1, a2 = l0[...], _from_view(l1, sl1, d1), _from_view(l2, sl2, d2)
        m = jnp.maximum(jnp.maximum(a0, a1), a2)
        e0, e1, e2 = jnp.exp(a0 - m), jnp.exp(a1 - m), jnp.exp(a2 - m)
        den = e0 + e1 + e2
        o = (e0 * o0[...] + e1 * _from_view(o1, so1, d1) + e2 * _from_view(o2, so2, d2)) / den
        o_ref[...] = o
        l_ref[...] = m + jnp.log(den)
        r = lax.rsqrt(jnp.mean(o * o, axis=-1, keepdims=True) + EPS)
        n_ref[...] = (o * r * g_ref[...]).astype(n_ref.dtype)

    specs, _ = _view_specs(tm, t, F32)
    spec = pl.BlockSpec((tm, w), lambda i: (i, 0))
    return _pcall(
        body, name="dil_merge", grid=(t // tm,),
        in_specs=list(specs) * 2 + [pl.BlockSpec((1, w), lambda i: (0, 0))], out_specs=(spec, spec, spec),
        out_shape=(_sds((t, w), F32), _sds((t, w), F32), _sds((t, w), BF16)),
        scratch_shapes=[_view_scratch(tm)] * 4,
        compiler_params=_cparams("parallel"))(*outs, *lses, g)


def _dil_stats(do, o, lse, tm):
    t, w = do.shape
    tm = min(tm, t)

    def body(a_ref, b_ref, l_ref, m_ref, s1, s4, s16, d1, d4, d16, tmp):
        first = (lax.broadcasted_iota(jnp.int32, (tm, w), 1) & (DIL_HD - 1)) < DIL_HD // 2
        do_ = a_ref[...]
        _to_views(jnp.where(first, l_ref[...], _head_sum(do_ * b_ref[...], m_ref[...])), tmp, (s1, s4, s16))
        _to_views(do_, tmp, (d1, d4, d16))

    spec = pl.BlockSpec((tm, w), lambda i: (i, 0))
    f_specs, f_shapes = _view_specs(tm, t, F32)
    b_specs, b_shapes = _view_specs(tm, t, BF16)
    res = _pcall(body, name="dil_stats", grid=(t // tm,),
                 in_specs=[spec, spec, spec, pl.BlockSpec((w, w), lambda i: (0, 0))],
                 out_specs=f_specs + b_specs, out_shape=f_shapes + b_shapes,
                 scratch_shapes=[_view_scratch(tm)],
                 compiler_params=_cparams("parallel"))(do, o, lse, _head_sum_matrix(1.0))
    return res[:3], res[3:]


def _head_norm_fwd(x, col, g, name, tm):
    t = x.shape[0]
    w = DIL_WIDTH
    tm = min(tm, t)
    normed = g is not None

    def body(*refs):
        outs, tmp = refs[-4:-1], refs[-1]
        xf = refs[0][...]
        if normed:
            g_ref, m_ref = refs[1], refs[2]
            xf = xf * lax.rsqrt(_head_sum(xf * xf, m_ref[...]) + EPS) * g_ref[...]
        _to_views(xf, tmp, outs)

    specs, shapes = _view_specs(tm, t, BF16)
    extra = [g, _head_sum_matrix(1.0 / DIL_HD)] if normed else []
    extra_specs = [pl.BlockSpec((1, w), lambda i: (0, 0)), pl.BlockSpec((w, w), lambda i: (0, 0))] if normed else []
    return _pcall(
        body, name=name, grid=(t // tm,),
        in_specs=[pl.BlockSpec((tm, w), lambda i: (i, col))] + extra_specs,
        out_specs=specs, out_shape=shapes, scratch_shapes=[_view_scratch(tm)],
        compiler_params=_cparams("parallel"))(x, *extra)


def _head_norm_bwd(dys, x, col, g, name, tm):
    t = x.shape[0]
    w = DIL_WIDTH
    tm = min(tm, t)
    nd = len(dys)
    nt = t // tm
    lane = np.arange(w) % DIL_HD
    fold = jnp.asarray((lane[:, None] == lane[None, :]).astype(np.float32))

    def body(*refs):
        x_ref, g_ref, m_ref, f_ref = refs[nd:nd + 4]
        dx_ref, dg_ref, s1, s2 = refs[-4:]
        dy = refs[0][...] + _from_view(refs[1], s1, DIL_DILATIONS[1]) + _from_view(refs[2], s2, DIL_DILATIONS[2])
        xf = x_ref[...]
        mat = m_ref[...]
        r = lax.rsqrt(_head_sum(xf * xf, mat) + EPS)
        xh = xf * r
        dxh = dy * g_ref[...]
        dx_ref[...] = r * (dxh - xh * _head_sum(dxh * xh, mat))

        @pl.when(pl.program_id(0) == 0)
        def _():
            dg_ref[...] = jnp.zeros_like(dg_ref)

        dg_ref[...] += jnp.sum(dy * xh, axis=0, keepdims=True)

        @pl.when(pl.program_id(0) == nt - 1)
        def _():
            per_lane = jnp.broadcast_to(dg_ref[...], (8, w))
            dg_ref[...] = lax.dot_general(per_lane, f_ref[...], NN, precision=lax.Precision.HIGHEST,
                                          preferred_element_type=F32)[0:1]

    row = pl.BlockSpec((tm, w), lambda i: (i, 0))
    vec = pl.BlockSpec((1, w), lambda i: (0, 0))
    sq = pl.BlockSpec((w, w), lambda i: (0, 0))
    views, _ = _view_specs(tm, t, F32)
    return _pcall(
        body, name=name, grid=(nt,),
        in_specs=list(views) + [pl.BlockSpec((tm, w), lambda i: (i, col)), vec, sq, sq],
        out_specs=(row, vec), out_shape=(_sds((t, w), F32), _sds((1, w), F32)),
        scratch_shapes=[_view_scratch(tm)] * 2,
        compiler_params=_cparams("arbitrary"))(*dys, x, g, _head_sum_matrix(1.0 / DIL_HD), fold)


def _rowdot(a, b, name, tm):
    n, d = a.shape
    tm = min(tm, n)

    def body(a_ref, b_ref, o_ref):
        o_ref[...] = jnp.sum(a_ref[...].astype(F32) * b_ref[...].astype(F32), axis=-1, keepdims=True)

    spec = pl.BlockSpec((tm, d), lambda i: (i, 0))
    return _pcall(body, name=name, grid=(n // tm,), in_specs=[spec, spec],
                  out_specs=pl.BlockSpec((tm, 1), lambda i: (i, 0)), out_shape=_sds((n, 1), F32),
                  compiler_params=_cparams("parallel"))(a, b)


def _sum_branches(parts, name, tm):
    t = parts[0].shape[0]
    w = DIL_WIDTH
    tm = min(tm, t)

    def body(a_ref, b_ref, c_ref, o_ref, s1, s2):
        o_ref[...] = a_ref[...] + _from_view(b_ref, s1, DIL_DILATIONS[1]) + _from_view(c_ref, s2, DIL_DILATIONS[2])

    views, _ = _view_specs(tm, t, F32)
    return _pcall(body, name=name, grid=(t // tm,), in_specs=list(views),
                  out_specs=pl.BlockSpec((tm, w), lambda i: (i, 0)), out_shape=_sds((t, w), F32),
                  scratch_shapes=[_view_scratch(tm)] * 2,
                  compiler_params=_cparams("parallel"))(*parts)


def _rope_tables(t):
    inv = ROPE_BASE ** (-np.arange(0, MLA_ROPE, 2, dtype=np.float64) / MLA_ROPE)
    ang = np.arange(t, dtype=np.float64)[:, None] * inv[None, :]
    cos, sin = np.cos(ang), np.sin(ang)
    return (jnp.asarray(np.concatenate([cos, cos], 1), F32), jnp.asarray(np.concatenate([-sin, sin], 1), F32))


def _swap_halves(a):
    half = MLA_ROPE // 2
    return jnp.concatenate([a[:, half:], a[:, :half]], axis=1)


def _qk_parts(x, pe, tm, nt):
    if pe is None:
        return None
    return (pl.BlockSpec((tm, MLA_NOPE), lambda i: (i, 0)), pl.BlockSpec((tm, MLA_ROPE), lambda i: (i % nt, 0)))


def _mla_qk_fwd(x, g, cos_t, sin_t, scale, name, tm, pe=None):
    n = x.shape[0]
    d = MLA_QK
    t = cos_t.shape[0]
    tm = min(tm, t)
    nt = t // tm
    split = _qk_parts(x, pe, tm, nt)

    def body(*refs):
        if split:
            xn_ref, xr_ref, xv_ref, g_ref, c_ref, s_ref, o_ref, v_ref = refs
            xn, xr = xn_ref[...], xr_ref[...]
            v_ref[...] = xv_ref[...].astype(v_ref.dtype)
        else:
            x_ref, g_ref, c_ref, s_ref, o_ref = refs
            xf = x_ref[...]
            xn, xr = xf[:, :MLA_NOPE], xf[:, MLA_NOPE:]
        ms = (jnp.sum(xn * xn, axis=-1, keepdims=True) + jnp.sum(xr * xr, axis=-1, keepdims=True)) * (1.0 / d)
        r = lax.rsqrt(ms + EPS)
        gg = g_ref[...]
        yn = xn * r * gg[:, :MLA_NOPE]
        yr = xr * r * gg[:, MLA_NOPE:]
        o_ref[:, :MLA_NOPE] = (yn * scale).astype(o_ref.dtype)
        o_ref[:, MLA_NOPE:] = ((yr * c_ref[...] + _swap_halves(yr) * s_ref[...]) * scale).astype(o_ref.dtype)

    row = pl.BlockSpec((tm, d), lambda i: (i, 0))
    vec = pl.BlockSpec((1, d), lambda i: (0, 0))
    tab = pl.BlockSpec((tm, MLA_ROPE), lambda i: (i % nt, 0))
    if not split:
        return _pcall(body, name=name, grid=(n // tm,), in_specs=[row, vec, tab, tab],
                      out_specs=row, out_shape=_sds((n, d), BF16),
                      compiler_params=_cparams("parallel"))(x, g, cos_t, sin_t)
    vals = pl.BlockSpec((tm, MLA_V), lambda i: (i, 1))
    return _pcall(body, name=name, grid=(n // tm,), in_specs=[split[0], split[1], vals, vec, tab, tab],
                  out_specs=(row, pl.BlockSpec((tm, MLA_V), lambda i: (i, 0))),
                  out_shape=(_sds((n, d), BF16), _sds((n, MLA_V), BF16)),
                  compiler_params=_cparams("parallel"))(x, pe, x, g, cos_t, sin_t)


def _mla_qk_bwd(dy, x, g, cos_t, sin_t, scale, name, tm, pe=None):
    n = x.shape[0]
    d = MLA_QK
    t = cos_t.shape[0]
    tm = min(tm, t)
    nt = t // tm
    split = _qk_parts(x, pe, tm, nt)

    def body(*refs):
        if split:
            dy_ref, xn_ref, xr_ref, g_ref, c_ref, s_ref, dxn_ref, dxr_ref, dg_ref = refs
            xn, xr = xn_ref[...], xr_ref[...]
        else:
            dy_ref, x_ref, g_ref, c_ref, s_ref, dx_ref, dg_ref = refs
            xf = x_ref[...]
            xn, xr = xf[:, :MLA_NOPE], xf[:, MLA_NOPE:]
        gg = g_ref[...]
        ms = (jnp.sum(xn * xn, axis=-1, keepdims=True) + jnp.sum(xr * xr, axis=-1, keepdims=True)) * (1.0 / d)
        r = lax.rsqrt(ms + EPS)
        xh_n, xh_r = xn * r, xr * r
        dyf = dy_ref[...] * scale
        dyr = dyf[:, MLA_NOPE:]
        dn_n = dyf[:, :MLA_NOPE]
        dn_r = dyr * c_ref[...] + _swap_halves(dyr * s_ref[...])
        dxh_n = dn_n * gg[:, :MLA_NOPE]
        dxh_r = dn_r * gg[:, MLA_NOPE:]
        mean = (jnp.sum(dxh_n * xh_n, axis=-1, keepdims=True)
                + jnp.sum(dxh_r * xh_r, axis=-1, keepdims=True)) * (1.0 / d)
        dx_n = r * (dxh_n - xh_n * mean)
        dx_r = r * (dxh_r - xh_r * mean)
        if split:
            dxn_ref[...] = dx_n
            dxr_ref[...] = dx_r
        else:
            dx_ref[:, :MLA_NOPE] = dx_n
            dx_ref[:, MLA_NOPE:] = dx_r

        @pl.when(pl.program_id(0) == 0)
        def _():
            dg_ref[...] = jnp.zeros_like(dg_ref)

        dg_ref[:, :MLA_NOPE] += jnp.sum(dn_n * xh_n, axis=0, keepdims=True)
        dg_ref[:, MLA_NOPE:] += jnp.sum(dn_r * xh_r, axis=0, keepdims=True)

    row = pl.BlockSpec((tm, d), lambda i: (i, 0))
    vec = pl.BlockSpec((1, d), lambda i: (0, 0))
    tab = pl.BlockSpec((tm, MLA_ROPE), lambda i: (i % nt, 0))
    if not split:
        return _pcall(body, name=name, grid=(n // tm,), in_specs=[row, row, vec, tab, tab],
                      out_specs=(row, vec), out_shape=(_sds((n, d), F32), _sds((1, d), F32)),
                      compiler_params=_cparams("arbitrary"))(dy, x, g, cos_t, sin_t)
    outs = (pl.BlockSpec((tm, MLA_NOPE), lambda i: (i, 0)), pl.BlockSpec((tm, MLA_ROPE), lambda i: (i, 0)), vec)
    return _pcall(body, name=name, grid=(n // tm,), in_specs=[row, split[0], split[1], vec, tab, tab],
                  out_specs=outs, out_shape=(_sds((n, MLA_NOPE), F32), _sds((n, MLA_ROPE), F32), _sds((1, d), F32)),
                  compiler_params=_cparams("arbitrary"))(dy, x, pe, g, cos_t, sin_t)


def _causal_mask(i, j, tq, tk, width):
    row = i * tq + lax.broadcasted_iota(jnp.int32, (tq, width), 0)
    col = j * tk + lax.broadcasted_iota(jnp.int32, (tq, width), 1)
    return col <= row


def _causal_steps(nq, nk, tq, tk, q_major):
    if q_major:
        groups = [[(i, j) for j in range((i * tq + tq - 1) // tk + 1)] for i in range(nq)]
        nunit = tk // tq if tk % tq == 0 else 1
    else:
        groups = [[(i, j) for i in range((j * tk) // tq, nq)] for j in range(nk)]
        nunit = tq // tk if tq % tk == 0 else 1
    it, jt, fl = [], [], []
    for g in groups:
        for n, (i, j) in enumerate(g):
            crossing = j * tk + tk - 1 > i * tq
            if q_major:
                unit = tk // nunit
                u = min(nunit, -(-(i * tq + tq - j * tk) // unit)) - 1
            else:
                unit = tq // nunit
                u = max(0, j * tk - i * tq) // unit
            it.append(i)
            jt.append(j)
            fl.append((n == 0) + 2 * (n == len(g) - 1) + 4 * crossing + 8 * (u if crossing else 0))
    return tuple(jnp.asarray(np.array(a, np.int32)) for a in (it, jt, fl)), nunit


def _by_crossing(flags, nunit, update):
    pl.when((flags & 4) == 0)(functools.partial(update, None))
    for u in range(nunit):
        pl.when(((flags & 4) != 0) & ((flags >> 3) == u))(functools.partial(update, u))


def _causal_specs(tq, tk):
    def qs(w):
        return pl.BlockSpec((None, tq, w), lambda h, s, it, jt, fl: (h, it[s], 0))

    def kv(w):
        return pl.BlockSpec((None, tk, w), lambda h, s, it, jt, fl: (h, jt[s], 0))

    return qs, kv


def _mla_fwd(q, k, v, tq, tk):
    nh, t, dq = q.shape
    dv = v.shape[2]
    tq, tk = min(tq, t), min(tk, t)
    tables, nunit = _causal_steps(t // tq, t // tk, tq, tk, True)

    def body(it, jt, fl, q_ref, k_ref, v_ref, o_ref, lse_ref, m_sc, l_sc, acc_sc):
        step = pl.program_id(1)
        i, j, flags = it[step], jt[step], fl[step]

        @pl.when((flags & 1) != 0)
        def _():
            m_sc[...] = jnp.full_like(m_sc, NEG)
            l_sc[...] = jnp.zeros_like(l_sc)
            acc_sc[...] = jnp.zeros_like(acc_sc)

        def update(units):
            wk = tk if units is None else (units + 1) * (tk // nunit)
            s = _dot(q_ref[...], k_ref[:wk, :], NT)
            if units is not None:
                s = jnp.where(_causal_mask(i, j, tq, tk, wk), s, NEG)
            m_prev = m_sc[...]
            m_new = jnp.maximum(m_prev, jnp.max(s, axis=-1, keepdims=True))
            alpha = jnp.exp(m_prev - m_new)
            p = jnp.exp(s - m_new)
            l_sc[...] = alpha * l_sc[...] + jnp.sum(p, axis=-1, keepdims=True)
            acc_sc[...] = alpha * acc_sc[...] + _dot(p.astype(BF16), v_ref[:wk, :], NN)
            m_sc[...] = m_new

        _by_crossing(flags, nunit, update)

        @pl.when((flags & 2) != 0)
        def _():
            o_ref[...] = acc_sc[...] / l_sc[...]
            lse_ref[...] = m_sc[...] + jnp.log(l_sc[...])

    qs, kv = _causal_specs(tq, tk)
    return _pcall(
        body, name="mla_attn_fwd",
        grid_spec=pltpu.PrefetchScalarGridSpec(
            num_scalar_prefetch=3, grid=(nh, tables[0].shape[0]),
            in_specs=[qs(dq), kv(dq), kv(dv)], out_specs=(qs(dv), qs(1)),
            scratch_shapes=[pltpu.VMEM((tq, 1), F32), pltpu.VMEM((tq, 1), F32), pltpu.VMEM((tq, dv), F32)]),
        out_shape=(_sds((nh, t, dv), F32), _sds((nh, t, 1), F32)),
        compiler_params=_cparams("parallel", "arbitrary"))(*tables, q, k, v)


def _mla_bwd(q, k, k_t, v, do, lse_row, dl_row, tq, tk):
    nh, t, dq = q.shape
    dv = v.shape[2]
    tq, tk = min(tq, t), min(tk, t)
    nq = t // tq
    tables, nunit = _causal_steps(nq, t // tk, tq, tk, False)

    def body(it, jt, fl, q_ref, k_ref, kt_ref, v_ref, do_ref, lse_ref, dl_ref, dk_ref, dv_ref, dq_ref, dk_sc, dv_sc):
        step = pl.program_id(1)
        i, j, flags = it[step], jt[step], fl[step]

        def update(units):
            off = 0 if units is None else units * (tq // nunit)
            qq = q_ref[off:, :]
            st = _dot(k_ref[...], qq, NT)
            if units is not None:
                key = j * tk + lax.broadcasted_iota(jnp.int32, (tk, tq - off), 0)
                qry = i * tq + off + lax.broadcasted_iota(jnp.int32, (tk, tq - off), 1)
                st = jnp.where(key <= qry, st, NEG)
            pt = jnp.exp(st - lse_ref[:, off:])
            dob = do_ref[off:, :].astype(BF16)
            dpt = _dot(v_ref[...], dob, NT)
            dst = pt * (dpt - dl_ref[:, off:])
            dsb = dst.astype(BF16)
            dv_part = _dot(pt.astype(BF16), dob, NN)
            dk_part = _dot(dsb, qq, NN)
            dq_part = _dot(kt_ref[...], dsb, NN)

            @pl.when((flags & 1) != 0)
            def _():
                dv_sc[...] = dv_part
                dk_sc[...] = dk_part

            @pl.when((flags & 1) == 0)
            def _():
                dv_sc[...] += dv_part
                dk_sc[...] += dk_part

            if off == 0:
                @pl.when(j == 0)
                def _():
                    dq_ref[i] = dq_part

                @pl.when(j != 0)
                def _():
                    dq_ref[i] += dq_part
            else:
                dq_ref[i, :, off:] += dq_part

        _by_crossing(flags, nunit, update)

        @pl.when((flags & 2) != 0)
        def _():
            dk_ref[...] = dk_sc[...]
            dv_ref[...] = dv_sc[...]

    qs, kv = _causal_specs(tq, tk)
    rowv = pl.BlockSpec((None, 1, tq), lambda h, s, it, jt, fl: (h, 0, it[s]))
    ktv = pl.BlockSpec((None, dq, tk), lambda h, s, it, jt, fl: (h, 0, jt[s]))
    whole = pl.BlockSpec((None, nq, dq, tq), lambda h, s, it, jt, fl: (h, 0, 0, 0))
    return _pcall(
        body, name="mla_attn_bwd",
        grid_spec=pltpu.PrefetchScalarGridSpec(
            num_scalar_prefetch=3, grid=(nh, tables[0].shape[0]),
            in_specs=[qs(dq), kv(dq), ktv, kv(dv), qs(dv), rowv, rowv], out_specs=(kv(dq), kv(dv), whole),
            scratch_shapes=[pltpu.VMEM((tk, dq), F32), pltpu.VMEM((tk, dv), F32)]),
        out_shape=(_sds((nh, t, dq), F32), _sds((nh, t, dv), F32), _sds((nh, nq, dq, tq), F32)),
        compiler_params=_cparams("parallel", "arbitrary"))(*tables, q, k, k_t, v, do, lse_row, dl_row)


def _ffn_down_loss(act, wd, x, target, name, tm):
    nc, t, fs = act.shape
    d = x.shape[1]
    tm = min(tm, t)
    nt = t // tm

    def body(*refs):
        a_refs, w_refs = refs[:nc], refs[nc:2 * nc]
        x_ref, t_ref, dy_ref, loss_ref, acc = refs[2 * nc:]
        i = pl.program_id(0)
        tot = _dot(a_refs[0][...], w_refs[0][...], NN)
        for c in range(1, nc):
            tot = tot + _dot(a_refs[c][...], w_refs[c][...], NN)
        err = x_ref[...] + tot * FFN_RESID - t_ref[...]
        dy_ref[...] = err * (1.0 / d)

        @pl.when(i == 0)
        def _():
            acc[...] = jnp.zeros_like(acc)

        acc[...] += jnp.sum(err * err, axis=0, keepdims=True)

        @pl.when(i == nt - 1)
        def _():
            loss_ref[0, 0] = jnp.sum(acc[...]) * (0.5 / d)

    row = pl.BlockSpec((tm, d), lambda i: (i, 0))
    a_specs = [pl.BlockSpec((None, tm, fs), lambda i, c=c: (c, i, 0)) for c in range(nc)]
    w_specs = [pl.BlockSpec((None, fs, d), lambda i, c=c: (c, 0, 0)) for c in range(nc)]
    return _pcall(
        body, name=name, grid=(nt,), in_specs=a_specs + w_specs + [row, row],
        out_specs=(row, pl.BlockSpec(memory_space=pltpu.SMEM)),
        out_shape=(_sds((t, d), F32), _sds((1, 1), F32)),
        scratch_shapes=[pltpu.VMEM((1, d), F32)],
        compiler_params=_cparams("arbitrary"))(*[act] * nc, *[wd] * nc, x, target)


def _adamw(w, g, m, v, name):
    r, c = w.shape
    tr = r
    for cand in (256, 128, 64, 32, 16, 8):
        if r % cand == 0:
            tr = cand
            break

    def body(w_ref, g_ref, m_ref, v_ref, d_ref, nm_ref, nv_ref):
        gg = g_ref[...]
        nm = ADAM_B1 * m_ref[...] + (1.0 - ADAM_B1) * gg
        nv = ADAM_B2 * v_ref[...] + (1.0 - ADAM_B2) * (gg * gg)
        m_hat = nm / (1.0 - ADAM_B1 ** ADAM_STEP)
        v_hat = nv / (1.0 - ADAM_B2 ** ADAM_STEP)
        d_ref[...] = -ADAM_LR * (m_hat / (jnp.sqrt(v_hat) + ADAM_EPS) + ADAM_WD * w_ref[...])
        nm_ref[...] = nm
        nv_ref[...] = nv

    spec = pl.BlockSpec((tr, c), lambda i: (i, 0))
    sd = _sds((r, c), F32)
    return _pcall(body, name=name, grid=(r // tr,), in_specs=[spec] * 4, out_specs=(spec,) * 3,
                  out_shape=(sd, sd, sd), compiler_params=_cparams("parallel"))(w, g, m, v)


MESH_ID = pl.DeviceIdType.MESH
HBM_SPEC = pl.BlockSpec(memory_space=pltpu.HBM)


def _place():
    return lax.axis_index("x"), lax.axis_index("y"), lax.axis_index("c")


def _other_chips(x, y):
    return [(1 - x, y), (x, 1 - y), (1 - x, 1 - y)]


def _remote(src, dst, send_sems, recv_sems, k, to):
    return pltpu.make_async_remote_copy(src_ref=src, dst_ref=dst, send_sem=send_sems.at[k], recv_sem=recv_sems.at[k],
                                        device_id=to, device_id_type=MESH_ID)


def _halves(arrays):
    for a in arrays:
        assert a.shape[-2] % 32 == 0
    return [a.shape[-2] // 2 for a in arrays]


def _gather_start(srcs, outs, halves, send_sems, recv_sems):
    x, y, c = _place()
    for a, half in enumerate(halves):
        rows = pl.ds(c * half, half)
        for k, (cx, cy) in enumerate(_other_chips(x, y)):
            _remote(srcs[a].at[rows, :], outs[a].at[2 * x + y, rows, :], send_sems, recv_sems, 3 * a + k,
                    (cx, cy, c)).start()


def _gather_wait(outs, halves, send_sems, recv_sems):
    x, y, c = _place()
    for a, half in enumerate(halves):
        for k, (cx, cy) in enumerate(_other_chips(x, y)):
            got = outs[a].at[2 * cx + cy, pl.ds(c * half, half), :]
            _remote(got, got, send_sems, recv_sems, 3 * a + k, (x, y, c)).wait()


def _forward_cores(partly):
    n = len(partly)
    halves = _halves(partly)

    def body(*refs):
        srcs, outs, send_sems, recv_sems = refs[:n], refs[n:2 * n], refs[2 * n], refs[2 * n + 1]
        x, y, c = _place()
        for a, half in enumerate(halves):
            for k, (cx, cy) in enumerate(_other_chips(x, y)):
                rows = pl.ds(c * half, half)
                _remote(srcs[a].at[2 * cx + cy, rows, :], outs[a].at[2 * cx + cy, rows, :], send_sems, recv_sems,
                        3 * a + k, (x, y, 1 - c)).start()
        for a, half in enumerate(halves):
            for k, (cx, cy) in enumerate(_other_chips(x, y)):
                mine = outs[a].at[2 * cx + cy, pl.ds(c * half, half), :]
                theirs = outs[a].at[2 * cx + cy, pl.ds((1 - c) * half, half), :]
                _remote(mine, theirs, send_sems, recv_sems, 3 * a + k, (x, y, c)).wait()

    return _pcall(
        body, name="forward_cores", in_specs=[HBM_SPEC] * n, out_specs=tuple([HBM_SPEC] * n),
        out_shape=tuple(_sds(p.shape, p.dtype) for p in partly), input_output_aliases={a: a for a in range(n)},
        scratch_shapes=[pltpu.SemaphoreType.DMA((3 * n,)), pltpu.SemaphoreType.DMA((3 * n,))],
    )(*partly)


def _gather_weights(blocks):
    n = len(blocks)
    halves = _halves(blocks)

    def body(*refs):
        srcs, outs, send_sems, recv_sems = refs[:n], refs[n:2 * n], refs[2 * n], refs[2 * n + 1]
        x, y, c = _place()
        me = 2 * x + y
        sibling = (x, y, 1 - c)
        chips = _other_chips(x, y)

        def part(a, chip, core):
            return outs[a].at[chip, pl.ds(core * halves[a], halves[a]), :]

        for a in range(n):
            mine = srcs[a].at[pl.ds(c * halves[a], halves[a]), :]
            for k, (cx, cy) in enumerate(chips):
                _remote(mine, part(a, me, c), send_sems, recv_sems, 6 * a + k, (cx, cy, c)).start()
        for k, (cx, cy) in enumerate(chips):
            for a in range(n):
                got = part(a, 2 * cx + cy, c)
                _remote(got, got, send_sems, recv_sems, 6 * a + k, (x, y, c)).wait_recv()
                _remote(got, got, send_sems, recv_sems, 6 * a + 3 + k, sibling).start()
        for k, (cx, cy) in enumerate(chips):
            for a in range(n):
                got = part(a, 2 * cx + cy, 1 - c)
                _remote(got, got, send_sems, recv_sems, 6 * a + 3 + k, (x, y, c)).wait_recv()
        for a in range(n):
            sent = part(a, me, c)
            for k in range(6):
                _remote(sent, sent, send_sems, recv_sems, 6 * a + k, (x, y, c)).wait_send()

    return _pcall(
        body, name="gather_weights", in_specs=[HBM_SPEC] * n, out_specs=tuple([HBM_SPEC] * n),
        out_shape=tuple(_sds((N_CHIPS,) + b.shape, b.dtype) for b in blocks),
        scratch_shapes=[pltpu.SemaphoreType.DMA((6 * n,)), pltpu.SemaphoreType.DMA((6 * n,))],
    )(*blocks)


def _reduce_cores(grads, tag):
    n = len(grads)

    def body(*refs):
        gs, outs, send_sems, recv_sems = refs[:n], refs[n:2 * n], refs[2 * n], refs[2 * n + 1]
        _cores_start(gs, outs, send_sems, recv_sems)
        _cores_wait(gs, outs, send_sems, recv_sems)

    return _pcall(
        body, name=f"reduce_cores_{tag}", in_specs=[HBM_SPEC] * n, out_specs=tuple([HBM_SPEC] * n),
        out_shape=_cores_shapes(grads), scratch_shapes=_cores_sems(n),
    )(*grads)


def _cores_shapes(grads):
    return tuple(_sds((N_CHIPS, h, g.shape[2]), g.dtype) for g, h in zip(grads, _halves(grads)))


def _cores_sems(n):
    return [pltpu.SemaphoreType.DMA((n,)), pltpu.SemaphoreType.DMA((n,))]


def _cores_start(gs, outs, send_sems, recv_sems):
    x, y, c = _place()
    for a, g in enumerate(gs):
        half = g.shape[1] // 2
        for j in range(N_CHIPS):
            _remote(g.at[j, pl.ds((1 - c) * half, half), :], outs[a].at[j], send_sems, recv_sems, a,
                    (x, y, 1 - c)).start()


def _cores_wait(gs, outs, send_sems, recv_sems):
    x, y, c = _place()
    for a, g in enumerate(gs):
        half = g.shape[1] // 2
        _remote(g.at[:, pl.ds((1 - c) * half, half), :], outs[a], send_sems, recv_sems, a, (x, y, c)).wait()


def _scatter_shapes(parts):
    return tuple(_sds((3,) + p.shape[1:], p.dtype) for p in parts)


def _scatter_sems(n):
    return [pltpu.SemaphoreType.DMA((3 * n,)), pltpu.SemaphoreType.DMA((3 * n,))]


def _scatter_start(ps, outs, send_sems, recv_sems):
    x, y, c = _place()
    for a in range(len(ps)):
        for k, (cx, cy) in enumerate(_other_chips(x, y)):
            _remote(ps[a].at[2 * cx + cy], outs[a].at[k], send_sems, recv_sems, 3 * a + k, (cx, cy, c)).start()


def _scatter_wait(ps, outs, send_sems, recv_sems):
    x, y, c = _place()
    for a in range(len(ps)):
        for k in range(3):
            _remote(ps[a].at[k], outs[a].at[k], send_sems, recv_sems, 3 * a + k, (x, y, c)).wait()


EXCHANGES = {"chips": (_scatter_start, _scatter_wait, _scatter_shapes, _scatter_sems),
             "cores": (_cores_start, _cores_wait, _cores_shapes, _cores_sems)}


def _sum_partials(received, parts, place):
    n = len(parts)
    steps = 2
    tiles = [p.shape[1] // steps for p in parts]

    def body(place_ref, *refs):
        rs, ps, outs = refs[:n], refs[n:2 * n], refs[2 * n:]
        for a in range(n):
            tot = ps[a][...].astype(F32)
            for k in range(3):
                tot = tot + rs[a][k].astype(F32)
            outs[a][...] = tot

    cols = [p.shape[2] for p in parts]
    return _pcall(
        body, name="sum_chip_partials",
        grid_spec=pltpu.PrefetchScalarGridSpec(
            num_scalar_prefetch=1, grid=(steps,),
            in_specs=[pl.BlockSpec((3, tm, w), lambda i, pc: (0, i, 0)) for tm, w in zip(tiles, cols)]
            + [pl.BlockSpec((None, tm, w), lambda i, pc: (pc[0], i, 0)) for tm, w in zip(tiles, cols)],
            out_specs=tuple(pl.BlockSpec((tm, w), lambda i, pc: (pc[1] * steps + i, 0)) for tm, w in zip(tiles, cols))),
        out_shape=tuple(_sds((2 * p.shape[1], p.shape[2]), F32) for p in parts),
        compiler_params=_cparams("parallel"))(place, *received, *parts)


def _share_cores(blocks):
    n = len(blocks)
    halves = _halves(blocks)

    def body(*refs):
        srcs, outs, send_sems, recv_sems = refs[:n], refs[n:2 * n], refs[2 * n], refs[2 * n + 1]
        x, y, c = _place()
        for a in range(n):
            piece = pl.ds(c * halves[a], halves[a])
            _remote(srcs[a].at[piece, :], outs[a].at[piece, :], send_sems, recv_sems, a, (x, y, 1 - c)).start()
        for a in range(n):
            mine = outs[a].at[pl.ds(c * halves[a], halves[a]), :]
            theirs = outs[a].at[pl.ds((1 - c) * halves[a], halves[a]), :]
            _remote(mine, theirs, send_sems, recv_sems, a, (x, y, c)).wait()

    return _pcall(
        body, name="share_cores", in_specs=[HBM_SPEC] * n, out_specs=tuple([HBM_SPEC] * n),
        out_shape=tuple(_sds(b.shape, b.dtype) for b in blocks), input_output_aliases={a: a for a in range(n)},
        scratch_shapes=[pltpu.SemaphoreType.DMA((n,)), pltpu.SemaphoreType.DMA((n,))],
    )(*blocks)


def _sum_blocks(stacked, name, tm):
    n, rows, lanes = stacked.shape
    tm = min(tm, rows)

    def body(s_ref, o_ref):
        tot = s_ref[n - 1].astype(F32)
        for k in range(n - 1):
            tot = tot + s_ref[k].astype(F32)
        o_ref[...] = tot

    return _pcall(body, name=name, grid=(rows // tm,),
                  in_specs=[pl.BlockSpec((n, tm, lanes), lambda i: (0, i, 0))],
                  out_specs=pl.BlockSpec((tm, lanes), lambda i: (i, 0)), out_shape=_sds((rows, lanes), F32),
                  compiler_params=_cparams("parallel"))(stacked)


def _add_halves(grads, theirs, core, tag):
    n = len(grads)
    steps = 2
    tiles = [t.shape[1] // steps for t in theirs]
    cols = [t.shape[2] for t in theirs]

    def body(c_ref, *refs):
        gs, ts, outs = refs[:n], refs[n:2 * n], refs[2 * n:]
        for a in range(n):
            outs[a][...] = (gs[a][...] + ts[a][...]).astype(BF16)

    own = [pl.BlockSpec((None, tm, w), lambda k, i, c: (k, c[0] * steps + i, 0)) for tm, w in zip(tiles, cols)]
    same = [pl.BlockSpec((None, tm, w), lambda k, i, c: (k, i, 0)) for tm, w in zip(tiles, cols)]
    return _pcall(
        body, name=f"add_core_halves_{tag}",
        grid_spec=pltpu.PrefetchScalarGridSpec(
            num_scalar_prefetch=1, grid=(N_CHIPS, steps), in_specs=own + same, out_specs=tuple(same)),
        out_shape=tuple(_sds(t.shape, BF16) for t in theirs),
        compiler_params=_cparams("parallel", "parallel"))(core, *grads, *theirs)


def _allreduce_small(part):
    rows, lanes = part.shape
    ndev = 8

    def body(src, tot, buf, send_sems, recv_sems):
        x, y, c = _place()
        me = 4 * x + 2 * y + c
        buf[me] = src[...]
        sends = []
        for k in range(1, ndev):
            peer = (x ^ (k >> 2), y ^ ((k >> 1) & 1), c ^ (k & 1))
            cp = _remote(src, buf.at[me], send_sems, recv_sems, k - 1, peer)
            cp.start()
            sends.append(cp)
        for k in range(1, ndev):
            theirs = buf.at[me ^ k]
            _remote(theirs, theirs, send_sems, recv_sems, k - 1, (x, y, c)).wait_recv()
        for cp in sends:
            cp.wait_send()
        acc = buf[0]
        for d in range(1, ndev):
            acc = acc + buf[d]
        tot[...] = acc

    vm = pl.BlockSpec(memory_space=pltpu.VMEM)
    return _pcall(
        body, name="allreduce_small", in_specs=[vm], out_specs=vm, out_shape=_sds((rows, lanes), F32),
        scratch_shapes=[pltpu.VMEM((ndev, rows, lanes), F32), pltpu.SemaphoreType.DMA((ndev - 1,)),
                        pltpu.SemaphoreType.DMA((ndev - 1,))],
    )(part)


def _pack_small(vals):
    parts = []
    for name, shape, r in SMALL:
        flat = vals[name].reshape(-1).astype(F32)
        parts.append(jnp.pad(flat, (0, r * LANES - flat.shape[0])).reshape(r, LANES))
    used = sum(r for _, _, r in SMALL)
    parts.append(jnp.zeros((SMALL_ROWS - used, LANES), F32))
    return jnp.concatenate(parts, axis=0)


def _unpack_small(packed):
    out, off = {}, 0
    for name, shape, r in SMALL:
        n = int(np.prod(shape))
        out[name] = packed[off:off + r].reshape(-1)[:n].reshape(shape)
        off += r
    return out


def _heads_major(a, nh):
    t = a.shape[0]
    return a.reshape(t, nh, a.shape[1] // nh).transpose(1, 0, 2)


def _tokens_major(a):
    nh, t, w = a.shape
    return a.transpose(1, 0, 2).reshape(t, nh * w)


LATE = ("ffn1_w_gate", "ffn1_w_up", "ffn1_w_down")
EARLY = tuple(name for name, _ in BIG if name not in LATE)


def _local_step(x, target, small, wfull, exchanges=None, later_weights=None):
    t = x.shape[0]
    nh, hd = DIL_HEADS, DIL_HD
    grads_s, grads_b = {}, {}

    x1, ffn1_saved, partly = _ffn_fwd(x, small["ffn1_norm"], wfull["ffn1_w_gate"], wfull["ffn1_w_up"],
                                      wfull["ffn1_w_down"], "ffn1", later_weights[0] if later_weights else ())
    if later_weights:
        wfull = {**wfull, **later_weights[1](partly)}
    w_in = wfull["w_in"].transpose(1, 0, 2).reshape(D_MODEL, -1)
    w_out = wfull["w_out"].reshape(D_MODEL, D_MODEL)
    w_qb, w_kvb = wfull["mla_w_q_b"], wfull["mla_w_kv_b"]
    hm = _rms_fwd(x1, small["mix_norm"], BF16, "mix_norm", 512)
    proj = _mm_simple("in_proj", hm, w_in, NN, F32, tm=1024)
    cq, ckv, k_pe = proj[:, 1536:1792], proj[:, 1792:1920], proj[:, 1920:1984]

    gq, gk = jnp.tile(small["dil_q_norm"], (1, nh)), jnp.tile(small["dil_k_norm"], (1, nh))
    qn = _head_norm_fwd(proj, 0, gq, "dil_q_norm", 512)
    kn = _head_norm_fwd(proj, 1, gk, "dil_k_norm", 512)
    v_d = _head_norm_fwd(proj, 2, None, "dil_v_views", 512)
    bias = _bias_tiles(small["rel_bias"]).reshape(3, nh // 2, 2 * QB, QB + DIL_W)
    outs, lses = [], []
    for b, dil in enumerate(DIL_DILATIONS):
        o_b, lse_b = _dil_fwd(qn[b], kn[b], v_d[b], bias[b], dil, f"dil_fwd_{dil}")
        outs.append(o_b)
        lses.append(lse_b)
    o_dil, lse_tot, od = _dil_merge(outs, lses, small["out_norm_dil"], 512)

    mh = MLA_HEADS
    cos_t, sin_t = _rope_tables(t)
    cqn = _rms_fwd(cq, small["mla_q_a_norm"], BF16, "mla_q_a_norm", 512)
    ckvn = _rms_fwd(ckv, small["mla_kv_a_norm"], BF16, "mla_kv_a_norm", 512)
    tm = min(512, t)

    th = min(2048, t)

    def head_proj(name, a, w, width):
        k = a.shape[1]
        return _mm(name, (mh, t // th, 1),
                   [(a, pl.BlockSpec((th, k), lambda h, i, r: (i, 0)), w, pl.BlockSpec((None, k, width), lambda h, i, r: (h, 0, 0)))],
                   NN, _sds((mh, t, width), F32), pl.BlockSpec((None, th, width), lambda h, i, r: (h, i, 0)), (th, width))

    q_raw = head_proj("mla_q_proj", cqn, w_qb, MLA_QK)
    kv_raw = head_proj("mla_kv_proj", ckvn, w_kvb, MLA_NOPE + MLA_V)
    q_raw2, kv_raw2 = q_raw.reshape(mh * t, MLA_QK), kv_raw.reshape(mh * t, MLA_NOPE + MLA_V)
    q_scale = MLA_QK ** -0.5
    q_m = _mla_qk_fwd(q_raw2, small["mla_q_norm"], cos_t, sin_t, q_scale, "mla_q_rope", 2048).reshape(mh, t, MLA_QK)
    k_m, v_m = _mla_qk_fwd(kv_raw2, small["mla_k_norm"], cos_t, sin_t, 1.0, "mla_k_rope", 2048, pe=k_pe)
    k_m, v_m = k_m.reshape(mh, t, MLA_QK), v_m.reshape(mh, t, MLA_V)
    o_mla_h, lse_m = _mla_fwd(q_m, k_m, v_m, 512, 4096)
    o_mla = _tokens_major(o_mla_h)

    om = _rms_fwd(o_mla, small["out_norm_mla"], BF16, "out_norm_mla", 512)
    half_w = DIL_WIDTH
    row = pl.BlockSpec((tm, D_MODEL), lambda i, j, r: (i, 0))
    act_spec = pl.BlockSpec((tm, half_w), lambda i, j, r: (i, 0))
    x2 = _mm("out_proj", (t // tm, 1, 1),
             [(od, act_spec, w_out, pl.BlockSpec((half_w, D_MODEL), lambda i, j, r: (0, 0))),
              (om, act_spec, w_out, pl.BlockSpec((half_w, D_MODEL), lambda i, j, r: (1, 0)))],
             NN, _sds((t, D_MODEL), F32), row, (tm, D_MODEL), res=(x1, row))
    (dy, loss), ffn2_saved, _ = _ffn_fwd(x2, small["ffn2_norm"], wfull["ffn2_w_gate"], wfull["ffn2_w_up"],
                                         wfull["ffn2_w_down"], "ffn2", target=target)

    dx2, grads_s["ffn2_norm"], grads_b["ffn2_w_gate"], grads_b["ffn2_w_up"], grads_b["ffn2_w_down"], _, _ = _ffn_bwd(
        dy, x2, small["ffn2_norm"], wfull["ffn2_w_gate"], wfull["ffn2_w_up"], wfull["ffn2_w_down"], ffn2_saved, "ffn2")

    d_ocat = _mm_simple("out_proj_dx", dx2, w_out, NT, F32, tm=1024)
    dw_out_t = _mm_tn_multi("out_proj_dw", dx2, [od, om], 2048)
    grads_b["w_out"] = jnp.concatenate([w.T for w in dw_out_t], axis=0).reshape(N_CHIPS, D_MODEL // N_CHIPS, D_MODEL)
    do_dil, grads_s["out_norm_dil"] = _rms_bwd([d_ocat[:, :half_w]], o_dil, small["out_norm_dil"], None, "out_norm_dil_bwd", 512)
    do_mla, grads_s["out_norm_mla"] = _rms_bwd([d_ocat[:, half_w:]], o_mla, small["out_norm_mla"], None, "out_norm_mla_bwd", 512)

    do_m = _heads_major(do_mla, mh)
    dl_m = _rowdot(do_m.reshape(mh * t, MLA_V), o_mla_h.reshape(mh * t, MLA_V), "mla_delta", 2048).reshape(mh, t, 1)
    dk_m, dv_m, dq_t = _mla_bwd(q_m, k_m, k_m.transpose(0, 2, 1), v_m, do_m, lse_m.reshape(mh, 1, t),
                                dl_m.reshape(mh, 1, t), 2048, 512)
    dq_m = dq_t.transpose(0, 1, 3, 2).reshape(mh, t, MLA_QK)
    dq_raw, grads_s["mla_q_norm"] = _mla_qk_bwd(dq_m.reshape(mh * t, MLA_QK), q_raw2, small["mla_q_norm"],
                                                 cos_t, sin_t, q_scale, "mla_q_rope_bwd", 2048)
    dk_nope, dk_pe_h, grads_s["mla_k_norm"] = _mla_qk_bwd(dk_m.reshape(mh * t, MLA_QK), kv_raw2, small["mla_k_norm"],
                                                          cos_t, sin_t, 1.0, "mla_k_rope_bwd", 2048, pe=k_pe)
    dq_raw = dq_raw.reshape(mh, t, MLA_QK)
    dk_nope = dk_nope.reshape(mh, t, MLA_NOPE)

    def head_proj_dx(name, d, w):
        width, k = d.shape[2], w.shape[1]
        pairs = [(d, pl.BlockSpec((None, th, width), lambda i, j, r, h=h: (h, i, 0)),
                  w, pl.BlockSpec((None, k, width), lambda i, j, r, h=h: (h, 0, 0))) for h in range(mh)]
        return _mm(name, (t // th, 1, 1), pairs, NT, _sds((t, k), F32),
                   pl.BlockSpec((th, k), lambda i, j, r: (i, 0)), (th, k))

    def head_proj_dw(name, a, d):
        width, k = d.shape[2], a.shape[1]
        return _mm(name, (mh, 1, t // th),
                   [(a, pl.BlockSpec((th, k), lambda h, j, r: (r, 0)), d, pl.BlockSpec((None, th, width), lambda h, j, r: (h, r, 0)))],
                   TN, _sds((mh, k, width), F32), pl.BlockSpec((None, k, width), lambda h, j, r: (h, 0, 0)), (k, width))

    d_cqn = head_proj_dx("mla_q_proj_dx", dq_raw, w_qb)
    kv_pairs = []
    for h in range(mh):
        for part, d_part in enumerate((dk_nope, dv_m)):
            kv_pairs.append((d_part, pl.BlockSpec((None, th, MLA_NOPE), lambda i, j, r, h=h: (h, i, 0)),
                             w_kvb, pl.BlockSpec((None, MLA_KV_RANK, MLA_NOPE), lambda i, j, r, h=h, part=part: (h, 0, part))))
    d_ckvn = _mm("mla_kv_proj_dx", (t // th, 1, 1), kv_pairs, NT, _sds((t, MLA_KV_RANK), F32),
                 pl.BlockSpec((th, MLA_KV_RANK), lambda i, j, r: (i, 0)), (th, MLA_KV_RANK))
    grads_b["mla_w_q_b"] = head_proj_dw("mla_q_proj_dw", cqn, dq_raw)
    grads_b["mla_w_kv_b"] = jnp.concatenate([head_proj_dw("mla_k_proj_dw", ckvn, dk_nope),
                                             head_proj_dw("mla_v_proj_dw", ckvn, dv_m)], axis=2)
    d_cq, grads_s["mla_q_a_norm"] = _rms_bwd([d_cqn], cq, small["mla_q_a_norm"], None, "mla_q_a_norm_bwd", 512)
    d_ckv, grads_s["mla_kv_a_norm"] = _rms_bwd([d_ckvn], ckv, small["mla_kv_a_norm"], None, "mla_kv_a_norm_bwd", 512)
    d_kpe = _sum_blocks(dk_pe_h.reshape(mh, t * MLA_ROPE // LANES, LANES), "mla_kpe_sum", 1024).reshape(t, MLA_ROPE)

    stats, do_db = _dil_stats(do_dil, o_dil, lse_tot, 512)
    dqs, dks, dvs, dtiles = [], [], [], []
    for b, dil in enumerate(DIL_DILATIONS):
        dq_b, dk_b, dv_b, db_b = _dil_bwd(qn[b], kn[b], v_d[b], do_db[b], stats[b], bias[b], dil, f"dil_bwd_{dil}")
        dqs.append(dq_b)
        dks.append(dk_b)
        dvs.append(dv_b)
        dtiles.append(db_b)
    grads_s["rel_bias"] = _bias_grad(jnp.stack(dtiles).reshape(3, nh, QB, QB + DIL_W))
    dq_a, dgq = _head_norm_bwd(dqs, proj, 0, gq, "dil_q_norm_bwd", 512)
    dk_a, dgk = _head_norm_bwd(dks, proj, 1, gk, "dil_k_norm_bwd", 512)
    grads_s["dil_q_norm"], grads_s["dil_k_norm"] = dgq[:, :hd], dgk[:, :hd]
    dv_a = _sum_branches(dvs, "dil_dv_sum", 512)

    dparts = [dq_a, dk_a, dv_a, d_cq, d_ckv, d_kpe]
    t2 = min(512, t)
    pairs, lo = [], 0
    for dpart in dparts:
        width = dpart.shape[1]
        w_part = w_in[:, lo:lo + width]
        pairs.append((dpart, pl.BlockSpec((t2, width), lambda i, j, r: (i, 0)),
                      w_part, pl.BlockSpec((D_MODEL, width), lambda i, j, r: (0, 0))))
        lo += width
    dw_in = jnp.concatenate(_mm_tn_multi("in_proj_dw", hm, dparts, 1024), axis=1)
    grads_b["w_in"] = dw_in.reshape(D_MODEL, N_CHIPS, -1).transpose(1, 0, 2)
    early = tuple(grads_b[n] for n in EARLY) if exchanges else ()
    row2 = pl.BlockSpec((t2, D_MODEL), lambda i, j, r: (i, 0))
    res = _mm("in_proj_dx", (t // t2, 1, 1), pairs, NT, _sds((t, D_MODEL), F32), row2, (t2, D_MODEL),
              res=(dx2, row2), norm=(x1, small["mix_norm"]), outgoing=early, exchange="cores")
    dx1, grads_s["mix_norm"] = res[0], res[1]
    outgoing = exchanges[0](early, res[2]) if exchanges else ()
    dx, grads_s["ffn1_norm"], grads_b["ffn1_w_gate"], grads_b["ffn1_w_up"], grads_b["ffn1_w_down"], arrived, late = _ffn_bwd(
        dx1, x, small["ffn1_norm"], wfull["ffn1_w_gate"], wfull["ffn1_w_up"], wfull["ffn1_w_down"], ffn1_saved, "ffn1",
        outgoing, exchanges[1] if exchanges else None)
    return loss, dx, grads_s, grads_b, (tuple(outgoing), arrived), late


def kernel(x, ffn1_norm, ffn1_w_gate, ffn1_w_up, ffn1_w_down, mix_norm, w_in, dil_q_norm, dil_k_norm, rel_bias, mla_q_a_norm, mla_w_q_b, mla_kv_a_norm, mla_w_kv_b, mla_q_norm, mla_k_norm, out_norm_dil, out_norm_mla, w_out, ffn2_norm, ffn2_w_gate, ffn2_w_up, ffn2_w_down, loss_target, m_ffn1_norm, m_ffn1_w_gate, m_ffn1_w_up, m_ffn1_w_down, m_mix_norm, m_w_in, m_dil_q_norm, m_dil_k_norm, m_rel_bias, m_mla_q_a_norm, m_mla_w_q_b, m_mla_kv_a_norm, m_mla_w_kv_b, m_mla_q_norm, m_mla_k_norm, m_out_norm_dil, m_out_norm_mla, m_w_out, m_ffn2_norm, m_ffn2_w_gate, m_ffn2_w_up, m_ffn2_w_down, v_ffn1_norm, v_ffn1_w_gate, v_ffn1_w_up, v_ffn1_w_down, v_mix_norm, v_w_in, v_dil_q_norm, v_dil_k_norm, v_rel_bias, v_mla_q_a_norm, v_mla_w_q_b, v_mla_kv_a_norm, v_mla_w_kv_b, v_mla_q_norm, v_mla_k_norm, v_out_norm_dil, v_out_norm_mla, v_w_out, v_ffn2_norm, v_ffn2_w_gate, v_ffn2_w_up, v_ffn2_w_down):
    given = dict(locals())
    big_names = [name for name, _ in BIG]
    small_names = [name for name, _, _ in SMALL]

    chip = (2 * lax.axis_index("x") + lax.axis_index("y")).astype(jnp.int32)
    core = lax.axis_index("c").astype(jnp.int32)
    mine = {n: given[n].astype(BF16) for n in big_names}

    def with_own(names, arrays):
        return {n: lax.dynamic_update_slice(a, mine[n], (chip, 0, 0)) for n, a in zip(names, arrays)}

    wfirst = with_own(LATE, _gather_weights([mine[n][0] for n in LATE]))
    later_weights = ([mine[n][0] for n in EARLY], lambda partly: with_own(EARLY, _forward_cores(partly)))
    small = {n: given[n] for n in small_names}

    def early_partials(partial, theirs):
        return _add_halves(partial, theirs, core.reshape(1), "early")

    def late_partials(partial):
        return _add_halves(partial, _reduce_cores(partial, "late"), core.reshape(1), "late")

    exchanges = (early_partials, late_partials)
    loss, dx, grads_s, grads_b, (early_part, early_got), (late_part, late_got) = _local_step(
        x[0], loss_target[0], small, wfirst, exchanges, later_weights)
    loss = lax.psum(loss[0, 0], ("x", "y", "c"))
    reduced = _sum_partials(tuple(late_got) + tuple(early_got), tuple(late_part) + tuple(early_part),
                            jnp.stack([chip, core]))
    g_big = dict(zip(LATE + EARLY, _share_cores(reduced)))
    g_small = _unpack_small(_allreduce_small(_pack_small(grads_s)))

    grad, delta, new_m, new_v = {}, {}, {}, {}
    for name, shape in BIG:
        g2 = g_big[name]
        d_, m_, v_ = _adamw(given[name].reshape(shape), g2, given["m_" + name].reshape(shape),
                            given["v_" + name].reshape(shape), f"adamw_{name}")
        full = given[name].shape
        grad[name], delta[name], new_m[name], new_v[name] = (a.reshape(full) for a in (g2, d_, m_, v_))
    for name in small_names:
        grad[name] = g_small[name]
        delta[name], new_m[name], new_v[name] = _adamw(given[name], g_small[name], given["m_" + name],
                                                       given["v_" + name], f"adamw_{name}")

    return (loss, dx[None], *[grad[n] for n in WEIGHTS], *[delta[n] for n in WEIGHTS],
            *[new_m[n] for n in WEIGHTS], *[new_v[n] for n in WEIGHTS])
```

```python
import functools

import numpy as np
import jax
import jax.numpy as jnp
from jax import lax
from jax.experimental import pallas as pl
from jax.experimental.pallas import tpu as pltpu

F32 = jnp.float32
BF16 = jnp.bfloat16

D_MODEL = 1024
D_FF = 2816
N_CHIPS = 4
DIL_HEADS = 8
DIL_HD = 64
DIL_WIDTH = 512
DIL_DILATIONS = (1, 4, 16)
DIL_W = 128
QB = 128
MLA_HEADS = 4
MLA_NOPE = 128
MLA_ROPE = 64
MLA_QK = 192
MLA_V = 128
MLA_Q_RANK = 256
MLA_KV_RANK = 128
ROPE_BASE = 10000.0
REL_BUCKETS = 32
REL_MAX_DIST = 2048
FFN_RESID = 0.5
EPS = 1e-6
NEG = -1e30
LANES = 128

ADAM_LR = 0.001
ADAM_B1 = 0.9
ADAM_B2 = 0.999
ADAM_EPS = 1e-08
ADAM_WD = 0.01
ADAM_STEP = 10

NT = (((1,), (1,)), ((), ()))
NN = (((1,), (0,)), ((), ()))
TN = (((0,), (0,)), ((), ()))

BIG = (
    ("ffn1_w_gate", (D_MODEL, D_FF // N_CHIPS)),
    ("ffn1_w_up", (D_MODEL, D_FF // N_CHIPS)),
    ("ffn1_w_down", (D_FF // N_CHIPS, D_MODEL)),
    ("w_in", (D_MODEL, 1984 // N_CHIPS)),
    ("mla_w_q_b", (MLA_Q_RANK, MLA_QK)),
    ("mla_w_kv_b", (MLA_KV_RANK, MLA_NOPE + MLA_V)),
    ("w_out", (D_MODEL // N_CHIPS, D_MODEL)),
    ("ffn2_w_gate", (D_MODEL, D_FF // N_CHIPS)),
    ("ffn2_w_up", (D_MODEL, D_FF // N_CHIPS)),
    ("ffn2_w_down", (D_FF // N_CHIPS, D_MODEL)),
)
SMALL = (
    ("ffn1_norm", (1, 1024), 8), ("mix_norm", (1, 1024), 8), ("dil_q_norm", (1, 64), 1),
    ("dil_k_norm", (1, 64), 1), ("rel_bias", (8, 32), 2), ("mla_q_a_norm", (1, 256), 2),
    ("mla_kv_a_norm", (1, 128), 1), ("mla_q_norm", (1, 192), 2), ("mla_k_norm", (1, 192), 2),
    ("out_norm_dil", (1, 512), 4), ("out_norm_mla", (1, 512), 4), ("ffn2_norm", (1, 1024), 8),
)
SMALL_ROWS = 48
WEIGHTS = ("ffn1_norm", "ffn1_w_gate", "ffn1_w_up", "ffn1_w_down", "mix_norm", "w_in", "dil_q_norm",
           "dil_k_norm", "rel_bias", "mla_q_a_norm", "mla_w_q_b", "mla_kv_a_norm", "mla_w_kv_b",
           "mla_q_norm", "mla_k_norm", "out_norm_dil", "out_norm_mla", "w_out", "ffn2_norm",
           "ffn2_w_gate", "ffn2_w_up", "ffn2_w_down")


def _pcall(body, **kw):
    return pl.pallas_call(body, **kw)


def _cparams(*sem):
    return pltpu.CompilerParams(dimension_semantics=sem)


def _sds(shape, dtype):
    return jax.ShapeDtypeStruct(shape, dtype)


def _dot(a, b, dn):
    return lax.dot_general(a, b, dn, preferred_element_type=F32)


def _rms_fwd(x, g, out_dtype, name, tm):
    n, d = x.shape
    tm = min(tm, n)

    def body(x_ref, g_ref, o_ref):
        xf = x_ref[...].astype(F32)
        r = lax.rsqrt(jnp.mean(xf * xf, axis=-1, keepdims=True) + EPS)
        o_ref[...] = (xf * r * g_ref[...]).astype(o_ref.dtype)

    return _pcall(
        body, name=name, grid=(n // tm,),
        in_specs=[pl.BlockSpec((tm, d), lambda i: (i, 0)), pl.BlockSpec((1, d), lambda i: (0, 0))],
        out_specs=pl.BlockSpec((tm, d), lambda i: (i, 0)),
        out_shape=_sds((n, d), out_dtype), compiler_params=_cparams("parallel"))(x, g)


def _rms_bwd(dys, x, g, res, name, tm):
    n, d = x.shape
    tm = min(tm, n)
    nd = len(dys)
    has_res = res is not None

    def body(*refs):
        dy_refs = refs[:nd]
        x_ref, g_ref = refs[nd], refs[nd + 1]
        res_ref = refs[nd + 2] if has_res else None
        dx_ref, dg_ref = refs[-2], refs[-1]
        dy = dy_refs[0][...].astype(F32)
        for r_ in dy_refs[1:]:
            dy = dy + r_[...].astype(F32)
        xf = x_ref[...].astype(F32)
        r = lax.rsqrt(jnp.mean(xf * xf, axis=-1, keepdims=True) + EPS)
        xh = xf * r
        dxh = dy * g_ref[...]
        dx = r * (dxh - xh * jnp.mean(dxh * xh, axis=-1, keepdims=True))
        if has_res:
            dx = dx + res_ref[...]
        dx_ref[...] = dx

        @pl.when(pl.program_id(0) == 0)
        def _():
            dg_ref[...] = jnp.zeros_like(dg_ref)

        dg_ref[...] += jnp.sum(dy * xh, axis=0, keepdims=True)

    row = pl.BlockSpec((tm, d), lambda i: (i, 0))
    vec = pl.BlockSpec((1, d), lambda i: (0, 0))
    ins = list(dys) + [x, g] + ([res] if has_res else [])
    return _pcall(
        body, name=name, grid=(n // tm,),
        in_specs=[row] * nd + [row, vec] + ([row] if has_res else []),
        out_specs=(row, vec),
        out_shape=(_sds((n, d), F32), _sds((1, d), F32)),
        compiler_params=_cparams("arbitrary"))(*ins)


def _mm(name, grid, pairs, dn, out_shape, out_spec, acc_shape, res=None, scale=1.0, outgoing=(), norm=None,
        exchange="chips"):
    npairs = len(pairs)
    nred = grid[2]
    has_res = res is not None
    has_norm = norm is not None
    no = len(outgoing)
    ex_start, ex_wait, ex_shapes, ex_sems = EXCHANGES[exchange]

    def body(*refs):
        ab = refs[:2 * npairs]
        res_ref = refs[2 * npairs] if has_res else None
        nin = 2 * npairs + int(has_res) + 2 * int(has_norm)
        if has_norm:
            x_ref, g_ref = refs[nin - 2:nin]
        first_out = nin + no
        sent = refs[nin:first_out]
        o_ref = refs[first_out]
        nout = 1 + int(has_norm)
        dg_ref = refs[first_out + 1] if has_norm else None
        arrived = refs[first_out + nout:first_out + nout + no]
        acc_ref = refs[first_out + nout + no] if nred > 1 else None
        if no:
            send_sems, recv_sems = refs[-2:]
            ids = [pl.program_id(n) for n in range(3)]

            @pl.when((ids[0] == 0) & (ids[1] == 0) & (ids[2] == 0))
            def _():
                ex_start(sent, arrived, send_sems, recv_sems)

        tot = None
        for p in range(npairs):
            d = _dot(ab[2 * p][...].astype(BF16), ab[2 * p + 1][...].astype(BF16), dn)
            tot = d if tot is None else tot + d

        def finish(v):
            if scale != 1.0:
                v = v * scale
            if has_norm:
                xf = x_ref[...]
                r = lax.rsqrt(jnp.mean(xf * xf, axis=-1, keepdims=True) + EPS)
                xh = xf * r
                dxh = v * g_ref[...]

                @pl.when(pl.program_id(0) == 0)
                def _():
                    dg_ref[...] = jnp.zeros_like(dg_ref)

                dg_ref[...] += jnp.sum(v * xh, axis=0, keepdims=True)
                v = r * (dxh - xh * jnp.mean(dxh * xh, axis=-1, keepdims=True))
            if has_res:
                v = res_ref[...] + v
            o_ref[...] = v.astype(o_ref.dtype)

        if nred == 1:
            finish(tot)
        else:
            r = pl.program_id(2)

            @pl.when(r == 0)
            def _():
                acc_ref[...] = tot

            @pl.when(r > 0)
            def _():
                acc_ref[...] += tot

            @pl.when(r == nred - 1)
            def _():
                finish(acc_ref[...])

        if no:
            @pl.when((ids[0] == grid[0] - 1) & (ids[1] == grid[1] - 1) & (ids[2] == nred - 1))
            def _():
                ex_wait(sent, arrived, send_sems, recv_sems)

    ins, specs = [], []
    for a, a_spec, b, b_spec in pairs:
        ins += [a, b]
        specs += [a_spec, b_spec]
    if has_res:
        ins.append(res[0])
        specs.append(res[1])
    scratch = [pltpu.VMEM(acc_shape, F32)] if nred > 1 else []
    if not no and not has_norm:
        return _pcall(
            body, name=name, grid=grid, in_specs=specs, out_specs=out_spec, out_shape=out_shape,
            scratch_shapes=scratch, compiler_params=_cparams("parallel", "parallel", "arbitrary"))(*ins)
    out_specs, out_shapes = (out_spec,), (out_shape,)
    if has_norm:
        assert grid[1] == 1
        d = norm[1].shape[1]
        ins += [norm[0], norm[1]]
        specs += [out_spec, pl.BlockSpec((1, d), lambda i, j, r: (0, 0))]
        out_specs += (pl.BlockSpec((1, d), lambda i, j, r: (0, 0)),)
        out_shapes += (_sds((1, d), F32),)
    hbm = pl.BlockSpec(memory_space=pltpu.HBM)
    res_ = tuple(_pcall(
        body, name=name, grid=grid, in_specs=specs + [hbm] * no, out_specs=out_specs + (hbm,) * no,
        out_shape=out_shapes + ex_shapes(outgoing),
        scratch_shapes=scratch + (ex_sems(no) if no else []),
        compiler_params=_cparams("arbitrary", "arbitrary", "arbitrary"))(*ins, *outgoing))
    nout = len(out_shapes)
    return res_[:nout] + ((res_[nout:],) if no else ())


def _ffn_up(h, wg, wu, name, tm, incoming=()):
    t, d = h.shape
    nc, _, fs = wg.shape
    tm = min(tm, t)
    nt = t // tm
    ni = len(incoming)
    halves = _halves(incoming)

    def body(*refs):
        h_ref, wg_ref, wu_ref = refs[:3]
        srcs = refs[3:3 + ni]
        g_ref, u_ref, a_ref = refs[3 + ni:6 + ni]
        outs = refs[6 + ni:6 + 2 * ni]
        if ni:
            send_sems, recv_sems = refs[6 + 2 * ni:]
            c, i = pl.program_id(0), pl.program_id(1)

            @pl.when((c == 0) & (i == 0))
            def _():
                _gather_start(srcs, outs, halves, send_sems, recv_sems)

        hh = h_ref[...]
        gate = _dot(hh, wg_ref[...], NN)
        up = _dot(hh, wu_ref[...], NN)
        sig = jax.nn.sigmoid(gate)
        silu = gate * sig
        g_ref[...] = (up * (sig + silu * (1.0 - sig))).astype(BF16)
        u_ref[...] = silu.astype(BF16)
        a_ref[...] = (silu * up).astype(BF16)

        if ni:
            @pl.when((c == nc - 1) & (i == nt - 1))
            def _():
                _gather_wait(outs, halves, send_sems, recv_sems)

    wspec = pl.BlockSpec((None, d, fs), lambda c, i: (c, 0, 0))
    ospec = pl.BlockSpec((None, tm, fs), lambda c, i: (c, i, 0))
    hbm = pl.BlockSpec(memory_space=pltpu.HBM)
    osd = _sds((nc, t, fs), BF16)
    res = tuple(_pcall(
        body, name=name, grid=(nc, nt),
        in_specs=[pl.BlockSpec((tm, d), lambda c, i: (i, 0)), wspec, wspec] + [hbm] * ni,
        out_specs=(ospec, ospec, ospec) + (hbm,) * ni,
        out_shape=(osd, osd, osd) + tuple(_sds((N_CHIPS,) + b.shape, b.dtype) for b in incoming),
        scratch_shapes=[pltpu.SemaphoreType.DMA((3 * ni,)), pltpu.SemaphoreType.DMA((3 * ni,))] if ni else [],
        compiler_params=_cparams("arbitrary", "arbitrary"))(h, wg, wu, *incoming))
    return res[:3] + (res[3:],)


def _ffn_hidden_bwd(dy, h, wd, dact_dgate, dact_dup, act, name, tm, outgoing=()):
    t, d = dy.shape
    nc, fs, _ = wd.shape
    tm = min(tm, t)
    nt = t // tm
    no = len(outgoing)

    def body(*refs):
        dy_ref, h_ref, wd_ref, g_ref, u_ref, a_ref = refs[:6]
        sent = refs[6:6 + no]
        dg_ref, du_ref, dwg_hbm, dwu_hbm, dwd_hbm = refs[6 + no:11 + no]
        arrived = refs[11 + no:11 + 2 * no]
        wg_acc, wu_acc, wd_acc, sem = refs[11 + 2 * no:15 + 2 * no]
        c, i = pl.program_id(0), pl.program_id(1)
        if no:
            send_sems, recv_sems = refs[15 + 2 * no:]

            @pl.when((c == 0) & (i == 0))
            def _():
                _scatter_start(sent, arrived, send_sems, recv_sems)

        dyb = dy_ref[...].astype(BF16)
        da = _dot(dyb, wd_ref[...], NT) * FFN_RESID
        dgate = (da * g_ref[...].astype(F32)).astype(BF16)
        dup = (da * u_ref[...].astype(F32)).astype(BF16)
        dg_ref[...] = dgate
        du_ref[...] = dup
        hh = h_ref[...]
        parts = (_dot(hh, dgate, TN), _dot(hh, dup, TN), _dot(a_ref[...], dyb, TN) * FFN_RESID)
        accs = (wg_acc, wu_acc, wd_acc)

        @pl.when(i == 0)
        def _():
            for acc, part in zip(accs, parts):
                acc[...] = part

        @pl.when(i > 0)
        def _():
            for acc, part in zip(accs, parts):
                acc[...] += part

        @pl.when(i == nt - 1)
        def _():
            copies = [pltpu.make_async_copy(acc, out.at[c], sem.at[n])
                      for n, (acc, out) in enumerate(zip(accs, (dwg_hbm, dwu_hbm, dwd_hbm)))]
            for cp in copies:
                cp.start()
            for cp in copies:
                cp.wait()

        if no:
            @pl.when((c == nc - 1) & (i == nt - 1))
            def _():
                _scatter_wait(sent, arrived, send_sems, recv_sems)

    tok = pl.BlockSpec((tm, d), lambda c, i: (i, 0))
    cspec = pl.BlockSpec((None, tm, fs), lambda c, i: (c, i, 0))
    hbm = pl.BlockSpec(memory_space=pltpu.HBM)
    osd = _sds((nc, t, fs), BF16)
    res = _pcall(
        body, name=name, grid=(nc, nt),
        in_specs=[tok, tok, pl.BlockSpec((None, fs, d), lambda c, i: (c, 0, 0)), cspec, cspec, cspec] + [hbm] * no,
        out_specs=(cspec, cspec, hbm, hbm, hbm) + (hbm,) * no,
        out_shape=(osd, osd, _sds((nc, d, fs), F32), _sds((nc, d, fs), F32), _sds((nc, fs, d), F32))
        + _scatter_shapes(outgoing),
        scratch_shapes=[pltpu.VMEM((d, fs), F32), pltpu.VMEM((d, fs), F32), pltpu.VMEM((fs, d), F32),
                        pltpu.SemaphoreType.DMA((3,))] + (_scatter_sems(no) if no else []),
        compiler_params=_cparams("arbitrary", "arbitrary"))(dy, h, wd, dact_dgate, dact_dup, act, *outgoing)
    res = tuple(res)
    return res[:5] + (res[5:],)


def _ffn_fwd(x, g, wg, wu, wd, tag, incoming=(), target=None):
    t = x.shape[0]
    nc, _, fs = wg.shape
    tm = min(512, t)
    h = _rms_fwd(x, g, BF16, f"{tag}_norm", 512)
    dact_dgate, dact_dup, act, partly = _ffn_up(h, wg, wu, f"{tag}_up", 1024, incoming)
    if target is not None:
        return _ffn_down_loss(act, wd, x, target, f"{tag}_down_loss", 512), (h, dact_dgate, dact_dup, act), partly
    pairs = [(act, pl.BlockSpec((None, tm, fs), lambda i, j, r, c=c: (c, i, 0)),
              wd, pl.BlockSpec((None, fs, D_MODEL), lambda i, j, r, c=c: (c, 0, 0))) for c in range(nc)]
    row = pl.BlockSpec((tm, D_MODEL), lambda i, j, r: (i, 0))
    y = _mm(f"{tag}_down", (t // tm, 1, 1), pairs, NN, _sds((t, D_MODEL), F32), row, (tm, D_MODEL),
            res=(x, row), scale=FFN_RESID)
    return y, (h, dact_dgate, dact_dup, act), partly


def _ffn_bwd(dy, x, g, wg, wu, wd, saved, tag, outgoing=(), own_exchange=None):
    h, dact_dgate, dact_dup, act = saved
    t = x.shape[0]
    nc, _, fs = wg.shape
    tm = min(512, t)
    dgate, dup, dwg, dwu, dwd, arrived = _ffn_hidden_bwd(dy, h, wd, dact_dgate, dact_dup, act,
                                                         f"{tag}_hidden_bwd", 1024, outgoing)
    pairs = []
    for c in range(nc):
        a_spec = pl.BlockSpec((None, tm, fs), lambda i, j, r, c=c: (c, i, 0))
        w_spec = pl.BlockSpec((None, D_MODEL, fs), lambda i, j, r, c=c: (c, 0, 0))
        pairs += [(dgate, a_spec, wg, w_spec), (dup, a_spec, wu, w_spec)]
    own_part = tuple(own_exchange([dwg, dwu, dwd])) if own_exchange else ()
    row = pl.BlockSpec((tm, D_MODEL), lambda i, j, r: (i, 0))
    res = _mm(f"{tag}_dh", (t // tm, 1, 1), pairs, NT, _sds((t, D_MODEL), F32), row, (tm, D_MODEL),
              res=(dy, row), norm=(x, g), outgoing=own_part)
    dx, dg = res[0], res[1]
    own_got = res[2] if own_part else ()
    return dx, dg, dwg, dwu, dwd, arrived, (own_part, own_got)


def _mm_tn_multi(name, a, bs, tk):
    k, m = a.shape
    tk = min(tk, k)
    nb = len(bs)

    def body(*refs):
        a_ref, b_refs, o_refs = refs[0], refs[1:1 + nb], refs[1 + nb:]
        aa = a_ref[...].astype(BF16)
        parts = [_dot(aa, b_ref[...].astype(BF16), TN) for b_ref in b_refs]

        @pl.when(pl.program_id(0) == 0)
        def _():
            for o_ref, part in zip(o_refs, parts):
                o_ref[...] = part

        @pl.when(pl.program_id(0) > 0)
        def _():
            for o_ref, part in zip(o_refs, parts):
                o_ref[...] += part

    return _pcall(
        body, name=name, grid=(k // tk,),
        in_specs=[pl.BlockSpec((tk, m), lambda r: (r, 0))] + [pl.BlockSpec((tk, b.shape[1]), lambda r: (r, 0)) for b in bs],
        out_specs=tuple(pl.BlockSpec((m, b.shape[1]), lambda r: (0, 0)) for b in bs),
        out_shape=tuple(_sds((m, b.shape[1]), F32) for b in bs),
        compiler_params=_cparams("arbitrary"))(a, *bs)


def _mm_simple(name, a, b, dn, out_dtype, tm=512, tk=512, res=None, scale=1.0):
    if dn == TN:
        k, m = a.shape
        n = b.shape[1]
        tk = min(tk, k)
        return _mm(name, (1, 1, k // tk),
                   [(a, pl.BlockSpec((tk, m), lambda i, j, r: (r, 0)), b, pl.BlockSpec((tk, n), lambda i, j, r: (r, 0)))],
                   TN, _sds((m, n), out_dtype), pl.BlockSpec((m, n), lambda i, j, r: (0, 0)), (m, n), scale=scale)
    m, k = a.shape
    n = b.shape[1] if dn == NN else b.shape[0]
    tm = min(tm, m)
    row = pl.BlockSpec((tm, n), lambda i, j, r: (i, 0))
    return _mm(name, (m // tm, 1, 1),
               [(a, pl.BlockSpec((tm, k), lambda i, j, r: (i, 0)), b, pl.BlockSpec(b.shape, lambda i, j, r: (0, 0)))],
               dn, _sds((m, n), out_dtype), row, (tm, n), res=None if res is None else (res, row), scale=scale)


def _t5_bucket(dist):
    max_exact = REL_BUCKETS // 2
    d = np.maximum(dist, 1).astype(np.float32)
    large = max_exact + (np.log(d / max_exact) / np.log(REL_MAX_DIST / max_exact)
                         * (REL_BUCKETS - max_exact)).astype(np.int32)
    large = np.minimum(large, REL_BUCKETS - 1)
    return np.where(dist < max_exact, dist, large).astype(np.int32)


def _bucket_tiles():
    i = np.arange(QB)[:, None]
    j = np.arange(QB + DIL_W)[None, :]
    delta = np.clip(i + DIL_W - j, 0, None)
    return np.stack([_t5_bucket(delta * dil) for dil in DIL_DILATIONS]).astype(np.int32)


def _bias_tiles(rel_bias):
    buckets = jnp.asarray(_bucket_tiles())

    def body(rb_ref, bk_ref, o_ref):
        bk = bk_ref[...]
        for h in range(DIL_HEADS):
            def pick(b, tile):
                return jnp.where(bk == b, rb_ref[h, b], tile)

            o_ref[h] = lax.fori_loop(0, REL_BUCKETS, pick, jnp.zeros((QB, QB + DIL_W), F32))

    return _pcall(
        body, name="dil_bias_tiles", grid=(3,),
        in_specs=[pl.BlockSpec(memory_space=pltpu.SMEM),
                  pl.BlockSpec((None, QB, QB + DIL_W), lambda b: (b, 0, 0))],
        out_specs=pl.BlockSpec((None, DIL_HEADS, QB, QB + DIL_W), lambda b: (b, 0, 0, 0)),
        out_shape=_sds((3, DIL_HEADS, QB, QB + DIL_W), F32),
        compiler_params=_cparams("parallel"))(rel_bias, buckets)


def _bias_grad(dtiles):
    buckets = jnp.asarray(_bucket_tiles())

    def body(dt_ref, bk_ref, o_ref):
        def one(b, carry):
            hit = [bk_ref[br] == b for br in range(3)]
            for h in range(DIL_HEADS):
                tot = jnp.zeros((), F32)
                for br in range(3):
                    tot = tot + jnp.sum(jnp.where(hit[br], dt_ref[br, h], 0.0))
                o_ref[h, b] = tot
            return carry

        lax.fori_loop(0, REL_BUCKETS, one, 0)

    return _pcall(
        body, name="dil_bias_grad",
        in_specs=[pl.BlockSpec(memory_space=pltpu.VMEM), pl.BlockSpec(memory_space=pltpu.VMEM)],
        out_specs=pl.BlockSpec(memory_space=pltpu.SMEM),
        out_shape=_sds((DIL_HEADS, REL_BUCKETS), F32))(dtiles, buckets)


def _split_heads(a, lo):
    zero = jnp.zeros_like(a)
    return jnp.concatenate([jnp.where(lo, a, zero), jnp.where(lo, zero, a)], axis=0)


def _side_by_side(a):
    n = a.shape[0] // 2
    return jnp.concatenate([a[:n], a[n:]], axis=1)


def _band_masks(prev_ok):
    ii = lax.broadcasted_iota(jnp.int32, (2 * QB, QB), 0) & (QB - 1)
    jj = lax.broadcasted_iota(jnp.int32, (2 * QB, QB), 1)
    return jj <= ii, jj >= ii + jnp.where(prev_ok, 0, QB)


def _dil_fwd(q, k, v, bias, dil, name):
    w = DIL_WIDTH
    t = q.shape[0] * dil
    npair = w // LANES
    nl = t // dil // QB
    scale = DIL_HD ** -0.5

    def body(q_ref, kc_ref, kp_ref, vc_ref, vp_ref, b_ref, o_ref, lse_ref):
        nn = pl.program_id(1)
        lo = lax.broadcasted_iota(jnp.int32, (QB, LANES), 1) < DIL_HD
        lo2 = lax.broadcasted_iota(jnp.int32, (2 * QB, LANES), 1) < DIL_HD
        ii = lax.broadcasted_iota(jnp.int32, (2 * QB, 2 * QB), 0) & (QB - 1)
        jj = lax.broadcasted_iota(jnp.int32, (2 * QB, 2 * QB), 1)
        first_key = jnp.maximum(ii, jnp.where(nn != 0, 0, QB))
        valid = (jj >= first_key) & (jj <= ii + QB)
        for p in range(npair):
            cols = slice(p * LANES, (p + 1) * LANES)
            qq = _split_heads(q_ref[:, cols], lo)
            kk = jnp.concatenate([kp_ref[:, cols], kc_ref[:, cols]], axis=0)
            vv = jnp.concatenate([vp_ref[:, cols], vc_ref[:, cols]], axis=0)
            s = jnp.where(valid, _dot(qq, kk, NT) * scale + b_ref[p], NEG)
            m = jnp.max(s, axis=-1, keepdims=True)
            e = jnp.exp(s - m)
            den = jnp.sum(e, axis=-1, keepdims=True)
            pn = (e * (1.0 / den)).astype(BF16)
            o_ref[:, cols] = _dot(_side_by_side(pn), _split_heads(vv, lo2), NN)
            lse = m + jnp.log(den)
            lse_ref[:, cols] = jnp.where(lo, lse[:QB], lse[QB:])

    cur = pl.BlockSpec((QB, w), lambda r, n: (n, r))
    prev = pl.BlockSpec((QB, w), lambda r, n: (jnp.maximum(n - 1, 0), r))
    sd = _sds((t // dil, dil * w), F32)
    return _pcall(
        body, name=name, grid=(dil, nl),
        in_specs=[cur, cur, prev, cur, prev, pl.BlockSpec((npair, 2 * QB, 2 * QB), lambda r, n: (0, 0, 0))],
        out_specs=(cur, cur), out_shape=(sd, sd),
        compiler_params=_cparams("parallel", "parallel"))(q, k, k, v, v, bias)


def _dil_bwd(q, k, v, do, stats, bias, dil, name):
    w = DIL_WIDTH
    t = q.shape[0] * dil
    npair = w // LANES
    nl = t // dil // QB
    scale = DIL_HD ** -0.5

    def body(qc_ref, qn_ref, doc_ref, don_ref, sc_ref, sn_ref, k_ref, v_ref, b_ref,
             dq_ref, dk_ref, dv_ref, db_ref, carry):
        r, nn = pl.program_id(0), pl.program_id(1)
        lo = lax.broadcasted_iota(jnp.int32, (QB, LANES), 1) < DIL_HD
        cur_ok, prev_ok = _band_masks(nn + 1 < nl)

        @pl.when((r == 0) & (nn == 0))
        def _():
            db_ref[...] = jnp.zeros_like(db_ref)
            carry[...] = jnp.zeros_like(carry)

        for p in range(npair):
            cols = slice(p * LANES, (p + 1) * LANES)
            kp, vp = k_ref[:, cols], v_ref[:, cols]
            k2 = _split_heads(kp, lo)

            def column(ref, lane):
                first = p * LANES + lane
                return jnp.concatenate([ref[:, first:first + 1], ref[:, first + DIL_HD:first + DIL_HD + 1]], axis=0)

            def side(q_ref, do_ref, s_ref, bias, ok):
                qq = _split_heads(q_ref[:, cols], lo)
                dd = _split_heads(do_ref[:, cols], lo)
                s = jnp.where(ok, _dot(qq, kp, NT) * scale + bias, NEG)
                prob = jnp.exp(s - column(s_ref, 0))
                ds = prob * (_dot(dd, vp, NT) - column(s_ref, DIL_HD // 2))
                return qq, dd, prob.astype(BF16), ds

            q1, d1, p1, ds1 = side(qc_ref, doc_ref, sc_ref, b_ref[p, :, QB:], cur_ok)
            q2, d2, p2, ds2 = side(qn_ref, don_ref, sn_ref, b_ref[p, :, :QB], prev_ok)
            ds1b, ds2b = ds1.astype(BF16), ds2.astype(BF16)
            dq_ref[:, cols] = carry[:, cols] + _dot(_side_by_side(ds1b), k2, NN) * scale
            carry[:, cols] = _dot(_side_by_side(ds2b), k2, NN) * scale
            dk_ref[:, cols] = _dot(jnp.concatenate([ds1b, ds2b], axis=0), jnp.concatenate([q1, q2], axis=0), TN) * scale
            dv_ref[:, cols] = _dot(jnp.concatenate([p1, p2], axis=0), jnp.concatenate([d1, d2], axis=0), TN)
            db_ref[p, :, QB:] += ds1
            db_ref[p, :, :QB] += ds2

    cur = pl.BlockSpec((QB, w), lambda r, n: (n, r))
    nxt = pl.BlockSpec((QB, w), lambda r, n: (jnp.minimum(n + 1, nl - 1), r))
    tile = pl.BlockSpec((npair, 2 * QB, 2 * QB), lambda r, n: (0, 0, 0))
    sd = _sds((t // dil, dil * w), F32)
    return _pcall(
        body, name=name, grid=(dil, nl),
        in_specs=[cur, nxt, cur, nxt, cur, nxt, cur, cur, tile],
        out_specs=(cur, cur, cur, tile),
        out_shape=(sd, sd, sd, _sds((npair, 2 * QB, 2 * QB), F32)),
        scratch_shapes=[pltpu.VMEM((QB, w), F32)],
        compiler_params=_cparams("arbitrary", "arbitrary"))(q, q, do, do, stats, stats, k, v, bias)


def _head_sum_matrix(scale):
    idx = np.arange(DIL_WIDTH) // DIL_HD
    return jnp.asarray((idx[:, None] == idx[None, :]).astype(np.float32) * scale, BF16)


def _head_sum(x, mat):
    hi = x.astype(BF16)
    lo = (x - hi.astype(F32)).astype(BF16)
    return _dot(hi, mat, NN) + _dot(lo, mat, NN)


def _to_views(src, tmp, out_refs):
    tm, w = src.shape
    for j in range(w // LANES):
        tmp[j] = src[:, j * LANES:(j + 1) * LANES]
    for d, o_ref in zip(DIL_DILATIONS, out_refs):
        if d == 1:
            o_ref[...] = src.astype(o_ref.dtype)
            continue
        for r in range(d):
            for j in range(w // LANES):
                lo = r * w + j * LANES
                o_ref[:, lo:lo + LANES] = tmp[j, pl.ds(r, tm // d, stride=d), :].astype(o_ref.dtype)


def _from_view(v_ref, tmp, d):
    tm = tmp.shape[1]
    w = v_ref.shape[1] // d
    for r in range(d):
        for j in range(w // LANES):
            lo = r * w + j * LANES
            tmp[j, pl.ds(r, tm // d, stride=d), :] = v_ref[:, lo:lo + LANES]
    return jnp.concatenate([tmp[j] for j in range(w // LANES)], axis=1)


def _view_specs(tm, t, dtype):
    specs = tuple(pl.BlockSpec((tm // d, d * DIL_WIDTH), lambda i: (i, 0)) for d in DIL_DILATIONS)
    shapes = tuple(_sds((t // d, d * DIL_WIDTH), dtype) for d in DIL_DILATIONS)
    return specs, shapes


def _view_scratch(tm):
    return pltpu.VMEM((DIL_WIDTH // LANES, tm, LANES), F32)


def _dil_merge(outs, lses, g, tm):
    w = DIL_WIDTH
    t = outs[0].shape[0]
    tm = min(tm, t)

    def body(o0, o1, o2, l0, l1, l2, g_ref, o_ref, l_ref, n_ref, so1, so2, sl1, sl2):
        d1, d2 = DIL_DILATIONS[1], DIL_DILATIONS[2]
        a0, a1, a2 = l0[...], _from_view(l1, sl1, d1), _from_view(l2, sl2, d2)
        m = jnp.maximum(jnp.maximum(a0, a1), a2)
        e0, e1, e2 = jnp.exp(a0 - m), jnp.exp(a1 - m), jnp.exp(a2 - m)
        den = e0 + e1 + e2
        o = (e0 * o0[...] + e1 * _from_view(o1, so1, d1) + e2 * _from_view(o2, so2, d2)) / den
        o_ref[...] = o
        l_ref[...] = m + jnp.log(den)
        r = lax.rsqrt(jnp.mean(o * o, axis=-1, keepdims=True) + EPS)
        n_ref[...] = (o * r * g_ref[...]).astype(n_ref.dtype)

    specs, _ = _view_specs(tm, t, F32)
    spec = pl.BlockSpec((tm, w), lambda i: (i, 0))
    return _pcall(
        body, name="dil_merge", grid=(t // tm,),
        in_specs=list(specs) * 2 + [pl.BlockSpec((1, w), lambda i: (0, 0))], out_specs=(spec, spec, spec),
        out_shape=(_sds((t, w), F32), _sds((t, w), F32), _sds((t, w), BF16)),
        scratch_shapes=[_view_scratch(tm)] * 4,
        compiler_params=_cparams("parallel"))(*outs, *lses, g)


def _dil_stats(do, o, lse, tm):
    t, w = do.shape
    tm = min(tm, t)

    def body(a_ref, b_ref, l_ref, m_ref, s1, s4, s16, d1, d4, d16, tmp):
        first = (lax.broadcasted_iota(jnp.int32, (tm, w), 1) & (DIL_HD - 1)) < DIL_HD // 2
        do_ = a_ref[...]
        _to_views(jnp.where(first, l_ref[...], _head_sum(do_ * b_ref[...], m_ref[...])), tmp, (s1, s4, s16))
        _to_views(do_, tmp, (d1, d4, d16))

    spec = pl.BlockSpec((tm, w), lambda i: (i, 0))
    f_specs, f_shapes = _view_specs(tm, t, F32)
    b_specs, b_shapes = _view_specs(tm, t, BF16)
    res = _pcall(body, name="dil_stats", grid=(t // tm,),
                 in_specs=[spec, spec, spec, pl.BlockSpec((w, w), lambda i: (0, 0))],
                 out_specs=f_specs + b_specs, out_shape=f_shapes + b_shapes,
                 scratch_shapes=[_view_scratch(tm)],
                 compiler_params=_cparams("parallel"))(do, o, lse, _head_sum_matrix(1.0))
    return res[:3], res[3:]


def _head_norm_fwd(x, col, g, name, tm):
    t = x.shape[0]
    w = DIL_WIDTH
    tm = min(tm, t)
    normed = g is not None

    def body(*refs):
        outs, tmp = refs[-4:-1], refs[-1]
        xf = refs[0][...]
        if normed:
            g_ref, m_ref = refs[1], refs[2]
            xf = xf * lax.rsqrt(_head_sum(xf * xf, m_ref[...]) + EPS) * g_ref[...]
        _to_views(xf, tmp, outs)

    specs, shapes = _view_specs(tm, t, BF16)
    extra = [g, _head_sum_matrix(1.0 / DIL_HD)] if normed else []
    extra_specs = [pl.BlockSpec((1, w), lambda i: (0, 0)), pl.BlockSpec((w, w), lambda i: (0, 0))] if normed else []
    return _pcall(
        body, name=name, grid=(t // tm,),
        in_specs=[pl.BlockSpec((tm, w), lambda i: (i, col))] + extra_specs,
        out_specs=specs, out_shape=shapes, scratch_shapes=[_view_scratch(tm)],
        compiler_params=_cparams("parallel"))(x, *extra)


def _head_norm_bwd(dys, x, col, g, name, tm):
    t = x.shape[0]
    w = DIL_WIDTH
    tm = min(tm, t)
    nd = len(dys)
    nt = t // tm
    lane = np.arange(w) % DIL_HD
    fold = jnp.asarray((lane[:, None] == lane[None, :]).astype(np.float32))

    def body(*refs):
        x_ref, g_ref, m_ref, f_ref = refs[nd:nd + 4]
        dx_ref, dg_ref, s1, s2 = refs[-4:]
        dy = refs[0][...] + _from_view(refs[1], s1, DIL_DILATIONS[1]) + _from_view(refs[2], s2, DIL_DILATIONS[2])
        xf = x_ref[...]
        mat = m_ref[...]
        r = lax.rsqrt(_head_sum(xf * xf, mat) + EPS)
        xh = xf * r
        dxh = dy * g_ref[...]
        dx_ref[...] = r * (dxh - xh * _head_sum(dxh * xh, mat))

        @pl.when(pl.program_id(0) == 0)
        def _():
            dg_ref[...] = jnp.zeros_like(dg_ref)

        dg_ref[...] += jnp.sum(dy * xh, axis=0, keepdims=True)

        @pl.when(pl.program_id(0) == nt - 1)
        def _():
            per_lane = jnp.broadcast_to(dg_ref[...], (8, w))
            dg_ref[...] = lax.dot_general(per_lane, f_ref[...], NN, precision=lax.Precision.HIGHEST,
                                          preferred_element_type=F32)[0:1]

    row = pl.BlockSpec((tm, w), lambda i: (i, 0))
    vec = pl.BlockSpec((1, w), lambda i: (0, 0))
    sq = pl.BlockSpec((w, w), lambda i: (0, 0))
    views, _ = _view_specs(tm, t, F32)
    return _pcall(
        body, name=name, grid=(nt,),
        in_specs=list(views) + [pl.BlockSpec((tm, w), lambda i: (i, col)), vec, sq, sq],
        out_specs=(row, vec), out_shape=(_sds((t, w), F32), _sds((1, w), F32)),
        scratch_shapes=[_view_scratch(tm)] * 2,
        compiler_params=_cparams("arbitrary"))(*dys, x, g, _head_sum_matrix(1.0 / DIL_HD), fold)


def _rowdot(a, b, name, tm):
    n, d = a.shape
    tm = min(tm, n)

    def body(a_ref, b_ref, o_ref):
        o_ref[...] = jnp.sum(a_ref[...].astype(F32) * b_ref[...].astype(F32), axis=-1, keepdims=True)

    spec = pl.BlockSpec((tm, d), lambda i: (i, 0))
    return _pcall(body, name=name, grid=(n // tm,), in_specs=[spec, spec],
                  out_specs=pl.BlockSpec((tm, 1), lambda i: (i, 0)), out_shape=_sds((n, 1), F32),
                  compiler_params=_cparams("parallel"))(a, b)


def _sum_branches(parts, name, tm):
    t = parts[0].shape[0]
    w = DIL_WIDTH
    tm = min(tm, t)

    def body(a_ref, b_ref, c_ref, o_ref, s1, s2):
        o_ref[...] = a_ref[...] + _from_view(b_ref, s1, DIL_DILATIONS[1]) + _from_view(c_ref, s2, DIL_DILATIONS[2])

    views, _ = _view_specs(tm, t, F32)
    return _pcall(body, name=name, grid=(t // tm,), in_specs=list(views),
                  out_specs=pl.BlockSpec((tm, w), lambda i: (i, 0)), out_shape=_sds((t, w), F32),
                  scratch_shapes=[_view_scratch(tm)] * 2,
                  compiler_params=_cparams("parallel"))(*parts)


def _rope_tables(t):
    inv = ROPE_BASE ** (-np.arange(0, MLA_ROPE, 2, dtype=np.float64) / MLA_ROPE)
    ang = np.arange(t, dtype=np.float64)[:, None] * inv[None, :]
    cos, sin = np.cos(ang), np.sin(ang)
    return (jnp.asarray(np.concatenate([cos, cos], 1), F32), jnp.asarray(np.concatenate([-sin, sin], 1), F32))


def _swap_halves(a):
    half = MLA_ROPE // 2
    return jnp.concatenate([a[:, half:], a[:, :half]], axis=1)


def _qk_parts(x, pe, tm, nt):
    if pe is None:
        return None
    return (pl.BlockSpec((tm, MLA_NOPE), lambda i: (i, 0)), pl.BlockSpec((tm, MLA_ROPE), lambda i: (i % nt, 0)))


def _mla_qk_fwd(x, g, cos_t, sin_t, scale, name, tm, pe=None):
    n = x.shape[0]
    d = MLA_QK
    t = cos_t.shape[0]
    tm = min(tm, t)
    nt = t // tm
    split = _qk_parts(x, pe, tm, nt)

    def transposed(a):
        w = a.shape[1]
        eye = (lax.broadcasted_iota(jnp.int32, (w, w), 0) == lax.broadcasted_iota(jnp.int32, (w, w), 1)).astype(BF16)
        return _dot(eye, a, NT).astype(BF16)

    def body(*refs):
        if split:
            xn_ref, xr_ref, xv_ref, g_ref, c_ref, s_ref, o_ref, ot_ref, v_ref = refs
            xn, xr = xn_ref[...], xr_ref[...]
            v_ref[...] = xv_ref[...].astype(v_ref.dtype)
        else:
            x_ref, g_ref, c_ref, s_ref, o_ref = refs
            xf = x_ref[...]
            xn, xr = xf[:, :MLA_NOPE], xf[:, MLA_NOPE:]
        ms = (jnp.sum(xn * xn, axis=-1, keepdims=True) + jnp.sum(xr * xr, axis=-1, keepdims=True)) * (1.0 / d)
        r = lax.rsqrt(ms + EPS)
        gg = g_ref[...]
        yn = xn * r * gg[:, :MLA_NOPE]
        yr = xr * r * gg[:, MLA_NOPE:]
        on = (yn * scale).astype(o_ref.dtype)
        orot = ((yr * c_ref[...] + _swap_halves(yr) * s_ref[...]) * scale).astype(o_ref.dtype)
        o_ref[:, :MLA_NOPE] = on
        o_ref[:, MLA_NOPE:] = orot
        if split:
            ot_ref[:MLA_NOPE, :] = transposed(on)
            ot_ref[MLA_NOPE:, :] = transposed(orot)

    row = pl.BlockSpec((tm, d), lambda i: (i, 0))
    vec = pl.BlockSpec((1, d), lambda i: (0, 0))
    tab = pl.BlockSpec((tm, MLA_ROPE), lambda i: (i % nt, 0))
    if not split:
        return _pcall(body, name=name, grid=(n // tm,), in_specs=[row, vec, tab, tab],
                      out_specs=row, out_shape=_sds((n, d), BF16),
                      compiler_params=_cparams("parallel"))(x, g, cos_t, sin_t)
    vals = pl.BlockSpec((tm, MLA_V), lambda i: (i, 1))
    return _pcall(body, name=name, grid=(n // tm,), in_specs=[split[0], split[1], vals, vec, tab, tab],
                  out_specs=(row, pl.BlockSpec((None, d, tm), lambda i: (i // nt, 0, i % nt)),
                             pl.BlockSpec((tm, MLA_V), lambda i: (i, 0))),
                  out_shape=(_sds((n, d), BF16), _sds((n // t, d, t), BF16), _sds((n, MLA_V), BF16)),
                  compiler_params=_cparams("parallel"))(x, pe, x, g, cos_t, sin_t)


def _mla_qk_bwd(dy, x, g, cos_t, sin_t, scale, name, tm, pe=None):
    n = x.shape[0]
    d = MLA_QK
    t = cos_t.shape[0]
    tm = min(tm, t)
    nt = t // tm
    split = _qk_parts(x, pe, tm, nt)

    def body(*refs):
        if split:
            dy_ref, xn_ref, xr_ref, g_ref, c_ref, s_ref, dxn_ref, dxr_ref, dg_ref = refs
            xn, xr = xn_ref[...], xr_ref[...]
        else:
            dy_ref, x_ref, g_ref, c_ref, s_ref, dx_ref, dg_ref = refs
            xf = x_ref[...]
            xn, xr = xf[:, :MLA_NOPE], xf[:, MLA_NOPE:]
        gg = g_ref[...]
        ms = (jnp.sum(xn * xn, axis=-1, keepdims=True) + jnp.sum(xr * xr, axis=-1, keepdims=True)) * (1.0 / d)
        r = lax.rsqrt(ms + EPS)
        xh_n, xh_r = xn * r, xr * r
        dyf = dy_ref[...] * scale
        dyr = dyf[:, MLA_NOPE:]
        dn_n = dyf[:, :MLA_NOPE]
        dn_r = dyr * c_ref[...] + _swap_halves(dyr * s_ref[...])
        dxh_n = dn_n * gg[:, :MLA_NOPE]
        dxh_r = dn_r * gg[:, MLA_NOPE:]
        mean = (jnp.sum(dxh_n * xh_n, axis=-1, keepdims=True)
                + jnp.sum(dxh_r * xh_r, axis=-1, keepdims=True)) * (1.0 / d)
        dx_n = r * (dxh_n - xh_n * mean)
        dx_r = r * (dxh_r - xh_r * mean)
        if split:
            dxn_ref[...] = dx_n
            dxr_ref[...] = dx_r
        else:
            dx_ref[:, :MLA_NOPE] = dx_n
            dx_ref[:, MLA_NOPE:] = dx_r

        @pl.when(pl.program_id(0) == 0)
        def _():
            dg_ref[...] = jnp.zeros_like(dg_ref)

        dg_ref[:, :MLA_NOPE] += jnp.sum(dn_n * xh_n, axis=0, keepdims=True)
        dg_ref[:, MLA_NOPE:] += jnp.sum(dn_r * xh_r, axis=0, keepdims=True)

    row = pl.BlockSpec((tm, d), lambda i: (i, 0))
    vec = pl.BlockSpec((1, d), lambda i: (0, 0))
    tab = pl.BlockSpec((tm, MLA_ROPE), lambda i: (i % nt, 0))
    if not split:
        return _pcall(body, name=name, grid=(n // tm,), in_specs=[row, row, vec, tab, tab],
                      out_specs=(row, vec), out_shape=(_sds((n, d), F32), _sds((1, d), F32)),
                      compiler_params=_cparams("arbitrary"))(dy, x, g, cos_t, sin_t)
    outs = (pl.BlockSpec((tm, MLA_NOPE), lambda i: (i, 0)), pl.BlockSpec((tm, MLA_ROPE), lambda i: (i, 0)), vec)
    return _pcall(body, name=name, grid=(n // tm,), in_specs=[row, split[0], split[1], vec, tab, tab],
                  out_specs=outs, out_shape=(_sds((n, MLA_NOPE), F32), _sds((n, MLA_ROPE), F32), _sds((1, d), F32)),
                  compiler_params=_cparams("arbitrary"))(dy, x, pe, g, cos_t, sin_t)


def _causal_mask(i, j, tq, tk, width):
    row = i * tq + lax.broadcasted_iota(jnp.int32, (tq, width), 0)
    col = j * tk + lax.broadcasted_iota(jnp.int32, (tq, width), 1)
    return col <= row


def _causal_steps(nq, nk, tq, tk, q_major):
    if q_major:
        groups = [[(i, j) for j in range((i * tq + tq - 1) // tk + 1)] for i in range(nq)]
        nunit = tk // tq if tk % tq == 0 else 1
    else:
        groups = [[(i, j) for i in range((j * tk) // tq, nq)] for j in range(nk)]
        nunit = tq // tk if tq % tk == 0 else 1
    it, jt, fl = [], [], []
    for g in groups:
        for n, (i, j) in enumerate(g):
            crossing = j * tk + tk - 1 > i * tq
            if q_major:
                unit = tk // nunit
                u = min(nunit, -(-(i * tq + tq - j * tk) // unit)) - 1
            else:
                unit = tq // nunit
                u = max(0, j * tk - i * tq) // unit
            it.append(i)
            jt.append(j)
            fl.append((n == 0) + 2 * (n == len(g) - 1) + 4 * crossing + 8 * (u if crossing else 0))
    return tuple(jnp.asarray(np.array(a, np.int32)) for a in (it, jt, fl)), nunit


def _by_crossing(flags, nunit, update):
    pl.when((flags & 4) == 0)(functools.partial(update, None))
    for u in range(nunit):
        pl.when(((flags & 4) != 0) & ((flags >> 3) == u))(functools.partial(update, u))


def _causal_specs(tq, tk):
    def qs(w):
        return pl.BlockSpec((None, tq, w), lambda h, s, it, jt, fl: (h, it[s], 0))

    def kv(w):
        return pl.BlockSpec((None, tk, w), lambda h, s, it, jt, fl: (h, jt[s], 0))

    return qs, kv


def _mla_fwd(q, k, v, tq, tk):
    nh, t, dq = q.shape
    dv = v.shape[2]
    tq, tk = min(tq, t), min(tk, t)
    tables, nunit = _causal_steps(t // tq, t // tk, tq, tk, True)

    def body(it, jt, fl, q_ref, k_ref, v_ref, o_ref, lse_ref, m_sc, l_sc, acc_sc):
        step = pl.program_id(1)
        i, j, flags = it[step], jt[step], fl[step]

        @pl.when((flags & 1) != 0)
        def _():
            m_sc[...] = jnp.full_like(m_sc, NEG)
            l_sc[...] = jnp.zeros_like(l_sc)
            acc_sc[...] = jnp.zeros_like(acc_sc)

        def update(units):
            wk = tk if units is None else (units + 1) * (tk // nunit)
            s = _dot(q_ref[...], k_ref[:wk, :], NT)
            if units is not None:
                s = jnp.where(_causal_mask(i, j, tq, tk, wk), s, NEG)
            m_prev = m_sc[...]
            m_new = jnp.maximum(m_prev, jnp.max(s, axis=-1, keepdims=True))
            alpha = jnp.exp(m_prev - m_new)
            p = jnp.exp(s - m_new)
            l_sc[...] = alpha * l_sc[...] + jnp.sum(p, axis=-1, keepdims=True)
            acc_sc[...] = alpha * acc_sc[...] + _dot(p.astype(BF16), v_ref[:wk, :], NN)
            m_sc[...] = m_new

        _by_crossing(flags, nunit, update)

        @pl.when((flags & 2) != 0)
        def _():
            o_ref[...] = acc_sc[...] / l_sc[...]
            lse_ref[...] = m_sc[...] + jnp.log(l_sc[...])

    qs, kv = _causal_specs(tq, tk)
    return _pcall(
        body, name="mla_attn_fwd",
        grid_spec=pltpu.PrefetchScalarGridSpec(
            num_scalar_prefetch=3, grid=(nh, tables[0].shape[0]),
            in_specs=[qs(dq), kv(dq), kv(dv)], out_specs=(qs(dv), qs(1)),
            scratch_shapes=[pltpu.VMEM((tq, 1), F32), pltpu.VMEM((tq, 1), F32), pltpu.VMEM((tq, dv), F32)]),
        out_shape=(_sds((nh, t, dv), F32), _sds((nh, t, 1), F32)),
        compiler_params=_cparams("parallel", "arbitrary"))(*tables, q, k, v)


def _mla_bwd(q, k, k_t, v, do, lse_row, dl_row, tq, tk):
    nh, t, dq = q.shape
    dv = v.shape[2]
    tq, tk = min(tq, t), min(tk, t)
    nq = t // tq
    tables, nunit = _causal_steps(nq, t // tk, tq, tk, False)

    def body(it, jt, fl, q_ref, k_ref, kt_ref, v_ref, do_ref, lse_ref, dl_ref, dk_ref, dv_ref, dq_ref, dk_sc, dv_sc):
        step = pl.program_id(1)
        i, j, flags = it[step], jt[step], fl[step]

        def update(units):
            off = 0 if units is None else units * (tq // nunit)
            qq = q_ref[off:, :]
            st = _dot(k_ref[...], qq, NT)
            if units is not None:
                key = j * tk + lax.broadcasted_iota(jnp.int32, (tk, tq - off), 0)
                qry = i * tq + off + lax.broadcasted_iota(jnp.int32, (tk, tq - off), 1)
                st = jnp.where(key <= qry, st, NEG)
            pt = jnp.exp(st - lse_ref[:, off:])
            dob = do_ref[off:, :].astype(BF16)
            dpt = _dot(v_ref[...], dob, NT)
            dst = pt * (dpt - dl_ref[:, off:])
            dsb = dst.astype(BF16)
            dv_part = _dot(pt.astype(BF16), dob, NN)
            dk_part = _dot(dsb, qq, NN)
            dq_part = _dot(kt_ref[...], dsb, NN)

            @pl.when((flags & 1) != 0)
            def _():
                dv_sc[...] = dv_part
                dk_sc[...] = dk_part

            @pl.when((flags & 1) == 0)
            def _():
                dv_sc[...] += dv_part
                dk_sc[...] += dk_part

            if off == 0:
                @pl.when(j == 0)
                def _():
                    dq_ref[i] = dq_part

                @pl.when(j != 0)
                def _():
                    dq_ref[i] += dq_part
            else:
                dq_ref[i, :, off:] += dq_part

        _by_crossing(flags, nunit, update)

        @pl.when((flags & 2) != 0)
        def _():
            dk_ref[...] = dk_sc[...]
            dv_ref[...] = dv_sc[...]

    qs, kv = _causal_specs(tq, tk)
    rowv = pl.BlockSpec((None, 1, tq), lambda h, s, it, jt, fl: (h, 0, it[s]))
    ktv = pl.BlockSpec((None, dq, tk), lambda h, s, it, jt, fl: (h, 0, jt[s]))
    whole = pl.BlockSpec((None, nq, dq, tq), lambda h, s, it, jt, fl: (h, 0, 0, 0))
    return _pcall(
        body, name="mla_attn_bwd",
        grid_spec=pltpu.PrefetchScalarGridSpec(
            num_scalar_prefetch=3, grid=(nh, tables[0].shape[0]),
            in_specs=[qs(dq), kv(dq), ktv, kv(dv), qs(dv), rowv, rowv], out_specs=(kv(dq), kv(dv), whole),
            scratch_shapes=[pltpu.VMEM((tk, dq), F32), pltpu.VMEM((tk, dv), F32)]),
        out_shape=(_sds((nh, t, dq), F32), _sds((nh, t, dv), F32), _sds((nh, nq, dq, tq), F32)),
        compiler_params=_cparams("parallel", "arbitrary"))(*tables, q, k, k_t, v, do, lse_row, dl_row)


def _ffn_down_loss(act, wd, x, target, name, tm):
    nc, t, fs = act.shape
    d = x.shape[1]
    tm = min(tm, t)
    nt = t // tm

    def body(*refs):
        a_refs, w_refs = refs[:nc], refs[nc:2 * nc]
        x_ref, t_ref, dy_ref, loss_ref, acc = refs[2 * nc:]
        i = pl.program_id(0)
        tot = _dot(a_refs[0][...], w_refs[0][...], NN)
        for c in range(1, nc):
            tot = tot + _dot(a_refs[c][...], w_refs[c][...], NN)
        err = x_ref[...] + tot * FFN_RESID - t_ref[...]
        dy_ref[...] = err * (1.0 / d)

        @pl.when(i == 0)
        def _():
            acc[...] = jnp.zeros_like(acc)

        acc[...] += jnp.sum(err * err, axis=0, keepdims=True)

        @pl.when(i == nt - 1)
        def _():
            loss_ref[0, 0] = jnp.sum(acc[...]) * (0.5 / d)

    row = pl.BlockSpec((tm, d), lambda i: (i, 0))
    a_specs = [pl.BlockSpec((None, tm, fs), lambda i, c=c: (c, i, 0)) for c in range(nc)]
    w_specs = [pl.BlockSpec((None, fs, d), lambda i, c=c: (c, 0, 0)) for c in range(nc)]
    return _pcall(
        body, name=name, grid=(nt,), in_specs=a_specs + w_specs + [row, row],
        out_specs=(row, pl.BlockSpec(memory_space=pltpu.SMEM)),
        out_shape=(_sds((t, d), F32), _sds((1, 1), F32)),
        scratch_shapes=[pltpu.VMEM((1, d), F32)],
        compiler_params=_cparams("arbitrary"))(*[act] * nc, *[wd] * nc, x, target)


def _adamw(w, g, m, v, name):
    r, c = w.shape
    tr = r
    for cand in (256, 128, 64, 32, 16, 8):
        if r % cand == 0:
            tr = cand
            break

    def body(w_ref, g_ref, m_ref, v_ref, d_ref, nm_ref, nv_ref):
        gg = g_ref[...]
        nm = ADAM_B1 * m_ref[...] + (1.0 - ADAM_B1) * gg
        nv = ADAM_B2 * v_ref[...] + (1.0 - ADAM_B2) * (gg * gg)
        m_hat = nm / (1.0 - ADAM_B1 ** ADAM_STEP)
        v_hat = nv / (1.0 - ADAM_B2 ** ADAM_STEP)
        d_ref[...] = -ADAM_LR * (m_hat / (jnp.sqrt(v_hat) + ADAM_EPS) + ADAM_WD * w_ref[...])
        nm_ref[...] = nm
        nv_ref[...] = nv

    spec = pl.BlockSpec((tr, c), lambda i: (i, 0))
    sd = _sds((r, c), F32)
    return _pcall(body, name=name, grid=(r // tr,), in_specs=[spec] * 4, out_specs=(spec,) * 3,
                  out_shape=(sd, sd, sd), compiler_params=_cparams("parallel"))(w, g, m, v)


MESH_ID = pl.DeviceIdType.MESH
HBM_SPEC = pl.BlockSpec(memory_space=pltpu.HBM)


def _place():
    return lax.axis_index("x"), lax.axis_index("y"), lax.axis_index("c")


def _other_chips(x, y):
    return [(1 - x, y), (x, 1 - y), (1 - x, 1 - y)]


def _remote(src, dst, send_sems, recv_sems, k, to):
    return pltpu.make_async_remote_copy(src_ref=src, dst_ref=dst, send_sem=send_sems.at[k], recv_sem=recv_sems.at[k],
                                        device_id=to, device_id_type=MESH_ID)


def _halves(arrays):
    for a in arrays:
        assert a.shape[-2] % 32 == 0
    return [a.shape[-2] // 2 for a in arrays]


def _gather_start(srcs, outs, halves, send_sems, recv_sems):
    x, y, c = _place()
    for a, half in enumerate(halves):
        rows = pl.ds(c * half, half)
        for k, (cx, cy) in enumerate(_other_chips(x, y)):
            _remote(srcs[a].at[rows, :], outs[a].at[2 * x + y, rows, :], send_sems, recv_sems, 3 * a + k,
                    (cx, cy, c)).start()


def _gather_wait(outs, halves, send_sems, recv_sems):
    x, y, c = _place()
    for a, half in enumerate(halves):
        for k, (cx, cy) in enumerate(_other_chips(x, y)):
            got = outs[a].at[2 * cx + cy, pl.ds(c * half, half), :]
            _remote(got, got, send_sems, recv_sems, 3 * a + k, (x, y, c)).wait()


def _forward_cores(partly):
    n = len(partly)
    halves = _halves(partly)

    def body(*refs):
        srcs, outs, send_sems, recv_sems = refs[:n], refs[n:2 * n], refs[2 * n], refs[2 * n + 1]
        x, y, c = _place()
        for a, half in enumerate(halves):
            for k, (cx, cy) in enumerate(_other_chips(x, y)):
                rows = pl.ds(c * half, half)
                _remote(srcs[a].at[2 * cx + cy, rows, :], outs[a].at[2 * cx + cy, rows, :], send_sems, recv_sems,
                        3 * a + k, (x, y, 1 - c)).start()
        for a, half in enumerate(halves):
            for k, (cx, cy) in enumerate(_other_chips(x, y)):
                mine = outs[a].at[2 * cx + cy, pl.ds(c * half, half), :]
                theirs = outs[a].at[2 * cx + cy, pl.ds((1 - c) * half, half), :]
                _remote(mine, theirs, send_sems, recv_sems, 3 * a + k, (x, y, c)).wait()

    return _pcall(
        body, name="forward_cores", in_specs=[HBM_SPEC] * n, out_specs=tuple([HBM_SPEC] * n),
        out_shape=tuple(_sds(p.shape, p.dtype) for p in partly), input_output_aliases={a: a for a in range(n)},
        scratch_shapes=[pltpu.SemaphoreType.DMA((3 * n,)), pltpu.SemaphoreType.DMA((3 * n,))],
    )(*partly)


def _gather_weights(blocks):
    n = len(blocks)
    halves = _halves(blocks)

    def body(*refs):
        srcs, outs, send_sems, recv_sems = refs[:n], refs[n:2 * n], refs[2 * n], refs[2 * n + 1]
        x, y, c = _place()
        me = 2 * x + y
        sibling = (x, y, 1 - c)
        chips = _other_chips(x, y)

        def part(a, chip, core):
            return outs[a].at[chip, pl.ds(core * halves[a], halves[a]), :]

        for a in range(n):
            mine = srcs[a].at[pl.ds(c * halves[a], halves[a]), :]
            for k, (cx, cy) in enumerate(chips):
                _remote(mine, part(a, me, c), send_sems, recv_sems, 6 * a + k, (cx, cy, c)).start()
        for k, (cx, cy) in enumerate(chips):
            for a in range(n):
                got = part(a, 2 * cx + cy, c)
                _remote(got, got, send_sems, recv_sems, 6 * a + k, (x, y, c)).wait_recv()
                _remote(got, got, send_sems, recv_sems, 6 * a + 3 + k, sibling).start()
        for k, (cx, cy) in enumerate(chips):
            for a in range(n):
                got = part(a, 2 * cx + cy, 1 - c)
                _remote(got, got, send_sems, recv_sems, 6 * a + 3 + k, (x, y, c)).wait_recv()
        for a in range(n):
            sent = part(a, me, c)
            for k in range(6):
                _remote(sent, sent, send_sems, recv_sems, 6 * a + k, (x, y, c)).wait_send()

    return _pcall(
        body, name="gather_weights", in_specs=[HBM_SPEC] * n, out_specs=tuple([HBM_SPEC] * n),
        out_shape=tuple(_sds((N_CHIPS,) + b.shape, b.dtype) for b in blocks),
        scratch_shapes=[pltpu.SemaphoreType.DMA((6 * n,)), pltpu.SemaphoreType.DMA((6 * n,))],
    )(*blocks)


def _reduce_cores(grads, tag):
    n = len(grads)

    def body(*refs):
        gs, outs, send_sems, recv_sems = refs[:n], refs[n:2 * n], refs[2 * n], refs[2 * n + 1]
        _cores_start(gs, outs, send_sems, recv_sems)
        _cores_wait(gs, outs, send_sems, recv_sems)

    return _pcall(
        body, name=f"reduce_cores_{tag}", in_specs=[HBM_SPEC] * n, out_specs=tuple([HBM_SPEC] * n),
        out_shape=_cores_shapes(grads), scratch_shapes=_cores_sems(n),
    )(*grads)


def _cores_shapes(grads):
    return tuple(_sds((N_CHIPS, h, g.shape[2]), g.dtype) for g, h in zip(grads, _halves(grads)))


def _cores_sems(n):
    return [pltpu.SemaphoreType.DMA((n,)), pltpu.SemaphoreType.DMA((n,))]


def _cores_start(gs, outs, send_sems, recv_sems):
    x, y, c = _place()
    for a, g in enumerate(gs):
        half = g.shape[1] // 2
        for j in range(N_CHIPS):
            _remote(g.at[j, pl.ds((1 - c) * half, half), :], outs[a].at[j], send_sems, recv_sems, a,
                    (x, y, 1 - c)).start()


def _cores_wait(gs, outs, send_sems, recv_sems):
    x, y, c = _place()
    for a, g in enumerate(gs):
        half = g.shape[1] // 2
        _remote(g.at[:, pl.ds((1 - c) * half, half), :], outs[a], send_sems, recv_sems, a, (x, y, c)).wait()


def _scatter_shapes(parts):
    return tuple(_sds((3,) + p.shape[1:], p.dtype) for p in parts)


def _scatter_sems(n):
    return [pltpu.SemaphoreType.DMA((3 * n,)), pltpu.SemaphoreType.DMA((3 * n,))]


def _scatter_start(ps, outs, send_sems, recv_sems):
    x, y, c = _place()
    for a in range(len(ps)):
        for k, (cx, cy) in enumerate(_other_chips(x, y)):
            _remote(ps[a].at[2 * cx + cy], outs[a].at[k], send_sems, recv_sems, 3 * a + k, (cx, cy, c)).start()


def _scatter_wait(ps, outs, send_sems, recv_sems):
    x, y, c = _place()
    for a in range(len(ps)):
        for k in range(3):
            _remote(ps[a].at[k], outs[a].at[k], send_sems, recv_sems, 3 * a + k, (x, y, c)).wait()


EXCHANGES = {"chips": (_scatter_start, _scatter_wait, _scatter_shapes, _scatter_sems),
             "cores": (_cores_start, _cores_wait, _cores_shapes, _cores_sems)}


def _sum_partials(received, parts, place):
    n = len(parts)
    steps = 2
    tiles = [p.shape[1] // steps for p in parts]

    def body(place_ref, *refs):
        rs, ps, outs = refs[:n], refs[n:2 * n], refs[2 * n:]
        for a in range(n):
            tot = ps[a][...].astype(F32)
            for k in range(3):
                tot = tot + rs[a][k].astype(F32)
            outs[a][...] = tot

    cols = [p.shape[2] for p in parts]
    return _pcall(
        body, name="sum_chip_partials",
        grid_spec=pltpu.PrefetchScalarGridSpec(
            num_scalar_prefetch=1, grid=(steps,),
            in_specs=[pl.BlockSpec((3, tm, w), lambda i, pc: (0, i, 0)) for tm, w in zip(tiles, cols)]
            + [pl.BlockSpec((None, tm, w), lambda i, pc: (pc[0], i, 0)) for tm, w in zip(tiles, cols)],
            out_specs=tuple(pl.BlockSpec((tm, w), lambda i, pc: (pc[1] * steps + i, 0)) for tm, w in zip(tiles, cols))),
        out_shape=tuple(_sds((2 * p.shape[1], p.shape[2]), F32) for p in parts),
        compiler_params=_cparams("parallel"))(place, *received, *parts)


def _share_cores(blocks):
    n = len(blocks)
    halves = _halves(blocks)

    def body(*refs):
        srcs, outs, send_sems, recv_sems = refs[:n], refs[n:2 * n], refs[2 * n], refs[2 * n + 1]
        x, y, c = _place()
        for a in range(n):
            piece = pl.ds(c * halves[a], halves[a])
            _remote(srcs[a].at[piece, :], outs[a].at[piece, :], send_sems, recv_sems, a, (x, y, 1 - c)).start()
        for a in range(n):
            mine = outs[a].at[pl.ds(c * halves[a], halves[a]), :]
            theirs = outs[a].at[pl.ds((1 - c) * halves[a], halves[a]), :]
            _remote(mine, theirs, send_sems, recv_sems, a, (x, y, c)).wait()

    return _pcall(
        body, name="share_cores", in_specs=[HBM_SPEC] * n, out_specs=tuple([HBM_SPEC] * n),
        out_shape=tuple(_sds(b.shape, b.dtype) for b in blocks), input_output_aliases={a: a for a in range(n)},
        scratch_shapes=[pltpu.SemaphoreType.DMA((n,)), pltpu.SemaphoreType.DMA((n,))],
    )(*blocks)


def _sum_blocks(stacked, name, tm):
    n, rows, lanes = stacked.shape
    tm = min(tm, rows)

    def body(s_ref, o_ref):
        tot = s_ref[n - 1].astype(F32)
        for k in range(n - 1):
            tot = tot + s_ref[k].astype(F32)
        o_ref[...] = tot

    return _pcall(body, name=name, grid=(rows // tm,),
                  in_specs=[pl.BlockSpec((n, tm, lanes), lambda i: (0, i, 0))],
                  out_specs=pl.BlockSpec((tm, lanes), lambda i: (i, 0)), out_shape=_sds((rows, lanes), F32),
                  compiler_params=_cparams("parallel"))(stacked)


def _add_halves(grads, theirs, core, tag):
    n = len(grads)
    steps = 2
    tiles = [t.shape[1] // steps for t in theirs]
    cols = [t.shape[2] for t in theirs]

    def body(c_ref, *refs):
        gs, ts, outs = refs[:n], refs[n:2 * n], refs[2 * n:]
        for a in range(n):
            outs[a][...] = (gs[a][...] + ts[a][...]).astype(BF16)

    own = [pl.BlockSpec((None, tm, w), lambda k, i, c: (k, c[0] * steps + i, 0)) for tm, w in zip(tiles, cols)]
    same = [pl.BlockSpec((None, tm, w), lambda k, i, c: (k, i, 0)) for tm, w in zip(tiles, cols)]
    return _pcall(
        body, name=f"add_core_halves_{tag}",
        grid_spec=pltpu.PrefetchScalarGridSpec(
            num_scalar_prefetch=1, grid=(N_CHIPS, steps), in_specs=own + same, out_specs=tuple(same)),
        out_shape=tuple(_sds(t.shape, BF16) for t in theirs),
        compiler_params=_cparams("parallel", "parallel"))(core, *grads, *theirs)


def _allreduce_small(part):
    rows, lanes = part.shape
    ndev = 8

    def body(src, tot, buf, send_sems, recv_sems):
        x, y, c = _place()
        me = 4 * x + 2 * y + c
        buf[me] = src[...]
        sends = []
        for k in range(1, ndev):
            peer = (x ^ (k >> 2), y ^ ((k >> 1) & 1), c ^ (k & 1))
            cp = _remote(src, buf.at[me], send_sems, recv_sems, k - 1, peer)
            cp.start()
            sends.append(cp)
        for k in range(1, ndev):
            theirs = buf.at[me ^ k]
            _remote(theirs, theirs, send_sems, recv_sems, k - 1, (x, y, c)).wait_recv()
        for cp in sends:
            cp.wait_send()
        acc = buf[0]
        for d in range(1, ndev):
            acc = acc + buf[d]
        tot[...] = acc

    vm = pl.BlockSpec(memory_space=pltpu.VMEM)
    return _pcall(
        body, name="allreduce_small", in_specs=[vm], out_specs=vm, out_shape=_sds((rows, lanes), F32),
        scratch_shapes=[pltpu.VMEM((ndev, rows, lanes), F32), pltpu.SemaphoreType.DMA((ndev - 1,)),
                        pltpu.SemaphoreType.DMA((ndev - 1,))],
    )(part)


SMALL_USED = sum(r for _, _, r in SMALL)


def _pack_small(vals, loss):
    parts = []
    for name, shape, r in SMALL:
        flat = vals[name].reshape(-1).astype(F32)
        parts.append(jnp.pad(flat, (0, r * LANES - flat.shape[0])).reshape(r, LANES))
    parts.append(jnp.pad(loss.astype(F32), ((0, SMALL_ROWS - SMALL_USED - 1), (0, LANES - 1))))
    return jnp.concatenate(parts, axis=0)


def _unpack_small(packed):
    out, off = {}, 0
    for name, shape, r in SMALL:
        n = int(np.prod(shape))
        out[name] = packed[off:off + r].reshape(-1)[:n].reshape(shape)
        off += r
    return out


def _heads_major(a, nh):
    t = a.shape[0]
    return a.reshape(t, nh, a.shape[1] // nh).transpose(1, 0, 2)


def _tokens_major(a):
    nh, t, w = a.shape
    return a.transpose(1, 0, 2).reshape(t, nh * w)


LATE = ("ffn1_w_gate", "ffn1_w_up", "ffn1_w_down")
EARLY = tuple(name for name, _ in BIG if name not in LATE)


def _local_step(x, target, small, wfull, exchanges=None, later_weights=None):
    t = x.shape[0]
    nh, hd = DIL_HEADS, DIL_HD
    grads_s, grads_b = {}, {}

    x1, ffn1_saved, partly = _ffn_fwd(x, small["ffn1_norm"], wfull["ffn1_w_gate"], wfull["ffn1_w_up"],
                                      wfull["ffn1_w_down"], "ffn1", later_weights[0] if later_weights else ())
    if later_weights:
        wfull = {**wfull, **later_weights[1](partly)}
    w_in = wfull["w_in"].transpose(1, 0, 2).reshape(D_MODEL, -1)
    w_out = wfull["w_out"].reshape(D_MODEL, D_MODEL)
    w_qb, w_kvb = wfull["mla_w_q_b"], wfull["mla_w_kv_b"]
    hm = _rms_fwd(x1, small["mix_norm"], BF16, "mix_norm", 512)
    proj = _mm_simple("in_proj", hm, w_in, NN, F32, tm=1024)
    cq, ckv, k_pe = proj[:, 1536:1792], proj[:, 1792:1920], proj[:, 1920:1984]

    gq, gk = jnp.tile(small["dil_q_norm"], (1, nh)), jnp.tile(small["dil_k_norm"], (1, nh))
    qn = _head_norm_fwd(proj, 0, gq, "dil_q_norm", 512)
    kn = _head_norm_fwd(proj, 1, gk, "dil_k_norm", 512)
    v_d = _head_norm_fwd(proj, 2, None, "dil_v_views", 512)
    bias = _bias_tiles(small["rel_bias"]).reshape(3, nh // 2, 2 * QB, QB + DIL_W)
    outs, lses = [], []
    for b, dil in enumerate(DIL_DILATIONS):
        o_b, lse_b = _dil_fwd(qn[b], kn[b], v_d[b], bias[b], dil, f"dil_fwd_{dil}")
        outs.append(o_b)
        lses.append(lse_b)
    o_dil, lse_tot, od = _dil_merge(outs, lses, small["out_norm_dil"], 512)

    mh = MLA_HEADS
    cos_t, sin_t = _rope_tables(t)
    cqn = _rms_fwd(cq, small["mla_q_a_norm"], BF16, "mla_q_a_norm", 512)
    ckvn = _rms_fwd(ckv, small["mla_kv_a_norm"], BF16, "mla_kv_a_norm", 512)
    tm = min(512, t)

    th = min(2048, t)

    def head_proj(name, a, w, width):
        k = a.shape[1]
        return _mm(name, (mh, t // th, 1),
                   [(a, pl.BlockSpec((th, k), lambda h, i, r: (i, 0)), w, pl.BlockSpec((None, k, width), lambda h, i, r: (h, 0, 0)))],
                   NN, _sds((mh, t, width), F32), pl.BlockSpec((None, th, width), lambda h, i, r: (h, i, 0)), (th, width))

    q_raw = head_proj("mla_q_proj", cqn, w_qb, MLA_QK)
    kv_raw = head_proj("mla_kv_proj", ckvn, w_kvb, MLA_NOPE + MLA_V)
    q_raw2, kv_raw2 = q_raw.reshape(mh * t, MLA_QK), kv_raw.reshape(mh * t, MLA_NOPE + MLA_V)
    q_scale = MLA_QK ** -0.5
    q_m = _mla_qk_fwd(q_raw2, small["mla_q_norm"], cos_t, sin_t, q_scale, "mla_q_rope", 2048).reshape(mh, t, MLA_QK)
    k_m, k_t, v_m = _mla_qk_fwd(kv_raw2, small["mla_k_norm"], cos_t, sin_t, 1.0, "mla_k_rope", 2048, pe=k_pe)
    k_m, v_m = k_m.reshape(mh, t, MLA_QK), v_m.reshape(mh, t, MLA_V)
    o_mla_h, lse_m = _mla_fwd(q_m, k_m, v_m, 512, 4096)
    o_mla = _tokens_major(o_mla_h)

    om = _rms_fwd(o_mla, small["out_norm_mla"], BF16, "out_norm_mla", 512)
    half_w = DIL_WIDTH
    row = pl.BlockSpec((tm, D_MODEL), lambda i, j, r: (i, 0))
    act_spec = pl.BlockSpec((tm, half_w), lambda i, j, r: (i, 0))
    x2 = _mm("out_proj", (t // tm, 1, 1),
             [(od, act_spec, w_out, pl.BlockSpec((half_w, D_MODEL), lambda i, j, r: (0, 0))),
              (om, act_spec, w_out, pl.BlockSpec((half_w, D_MODEL), lambda i, j, r: (1, 0)))],
             NN, _sds((t, D_MODEL), F32), row, (tm, D_MODEL), res=(x1, row))
    (dy, loss), ffn2_saved, _ = _ffn_fwd(x2, small["ffn2_norm"], wfull["ffn2_w_gate"], wfull["ffn2_w_up"],
                                         wfull["ffn2_w_down"], "ffn2", target=target)

    dx2, grads_s["ffn2_norm"], grads_b["ffn2_w_gate"], grads_b["ffn2_w_up"], grads_b["ffn2_w_down"], _, _ = _ffn_bwd(
        dy, x2, small["ffn2_norm"], wfull["ffn2_w_gate"], wfull["ffn2_w_up"], wfull["ffn2_w_down"], ffn2_saved, "ffn2")

    d_ocat = _mm_simple("out_proj_dx", dx2, w_out, NT, F32, tm=1024)
    dw_out_t = _mm_tn_multi("out_proj_dw", dx2, [od, om], 2048)
    grads_b["w_out"] = jnp.concatenate([w.T for w in dw_out_t], axis=0).reshape(N_CHIPS, D_MODEL // N_CHIPS, D_MODEL)
    do_dil, grads_s["out_norm_dil"] = _rms_bwd([d_ocat[:, :half_w]], o_dil, small["out_norm_dil"], None, "out_norm_dil_bwd", 512)
    do_mla, grads_s["out_norm_mla"] = _rms_bwd([d_ocat[:, half_w:]], o_mla, small["out_norm_mla"], None, "out_norm_mla_bwd", 512)

    do_m = _heads_major(do_mla, mh)
    dl_m = _rowdot(do_m.reshape(mh * t, MLA_V), o_mla_h.reshape(mh * t, MLA_V), "mla_delta", 2048).reshape(mh, t, 1)
    dk_m, dv_m, dq_t = _mla_bwd(q_m, k_m, k_t, v_m, do_m, lse_m.reshape(mh, 1, t),
                                dl_m.reshape(mh, 1, t), 2048, 512)
    dq_m = dq_t.transpose(0, 1, 3, 2).reshape(mh, t, MLA_QK)
    dq_raw, grads_s["mla_q_norm"] = _mla_qk_bwd(dq_m.reshape(mh * t, MLA_QK), q_raw2, small["mla_q_norm"],
                                                 cos_t, sin_t, q_scale, "mla_q_rope_bwd", 2048)
    dk_nope, dk_pe_h, grads_s["mla_k_norm"] = _mla_qk_bwd(dk_m.reshape(mh * t, MLA_QK), kv_raw2, small["mla_k_norm"],
                                                          cos_t, sin_t, 1.0, "mla_k_rope_bwd", 2048, pe=k_pe)
    dq_raw = dq_raw.reshape(mh, t, MLA_QK)
    dk_nope = dk_nope.reshape(mh, t, MLA_NOPE)

    def head_proj_dx(name, d, w):
        width, k = d.shape[2], w.shape[1]
        pairs = [(d, pl.BlockSpec((None, th, width), lambda i, j, r, h=h: (h, i, 0)),
                  w, pl.BlockSpec((None, k, width), lambda i, j, r, h=h: (h, 0, 0))) for h in range(mh)]
        return _mm(name, (t // th, 1, 1), pairs, NT, _sds((t, k), F32),
                   pl.BlockSpec((th, k), lambda i, j, r: (i, 0)), (th, k))

    def head_proj_dw(name, a, d):
        width, k = d.shape[2], a.shape[1]
        return _mm(name, (mh, 1, t // th),
                   [(a, pl.BlockSpec((th, k), lambda h, j, r: (r, 0)), d, pl.BlockSpec((None, th, width), lambda h, j, r: (h, r, 0)))],
                   TN, _sds((mh, k, width), F32), pl.BlockSpec((None, k, width), lambda h, j, r: (h, 0, 0)), (k, width))

    d_cqn = head_proj_dx("mla_q_proj_dx", dq_raw, w_qb)
    kv_pairs = []
    for h in range(mh):
        for part, d_part in enumerate((dk_nope, dv_m)):
            kv_pairs.append((d_part, pl.BlockSpec((None, th, MLA_NOPE), lambda i, j, r, h=h: (h, i, 0)),
                             w_kvb, pl.BlockSpec((None, MLA_KV_RANK, MLA_NOPE), lambda i, j, r, h=h, part=part: (h, 0, part))))
    d_ckvn = _mm("mla_kv_proj_dx", (t // th, 1, 1), kv_pairs, NT, _sds((t, MLA_KV_RANK), F32),
                 pl.BlockSpec((th, MLA_KV_RANK), lambda i, j, r: (i, 0)), (th, MLA_KV_RANK))
    grads_b["mla_w_q_b"] = head_proj_dw("mla_q_proj_dw", cqn, dq_raw)
    grads_b["mla_w_kv_b"] = jnp.concatenate([head_proj_dw("mla_k_proj_dw", ckvn, dk_nope),
                                             head_proj_dw("mla_v_proj_dw", ckvn, dv_m)], axis=2)
    d_cq, grads_s["mla_q_a_norm"] = _rms_bwd([d_cqn], cq, small["mla_q_a_norm"], None, "mla_q_a_norm_bwd", 512)
    d_ckv, grads_s["mla_kv_a_norm"] = _rms_bwd([d_ckvn], ckv, small["mla_kv_a_norm"], None, "mla_kv_a_norm_bwd", 512)
    d_kpe = _sum_blocks(dk_pe_h.reshape(mh, t * MLA_ROPE // LANES, LANES), "mla_kpe_sum", 1024).reshape(t, MLA_ROPE)

    stats, do_db = _dil_stats(do_dil, o_dil, lse_tot, 512)
    dqs, dks, dvs, dtiles = [], [], [], []
    for b, dil in enumerate(DIL_DILATIONS):
        dq_b, dk_b, dv_b, db_b = _dil_bwd(qn[b], kn[b], v_d[b], do_db[b], stats[b], bias[b], dil, f"dil_bwd_{dil}")
        dqs.append(dq_b)
        dks.append(dk_b)
        dvs.append(dv_b)
        dtiles.append(db_b)
    grads_s["rel_bias"] = _bias_grad(jnp.stack(dtiles).reshape(3, nh, QB, QB + DIL_W))
    dq_a, dgq = _head_norm_bwd(dqs, proj, 0, gq, "dil_q_norm_bwd", 512)
    dk_a, dgk = _head_norm_bwd(dks, proj, 1, gk, "dil_k_norm_bwd", 512)
    grads_s["dil_q_norm"], grads_s["dil_k_norm"] = dgq[:, :hd], dgk[:, :hd]
    dv_a = _sum_branches(dvs, "dil_dv_sum", 512)

    dparts = [dq_a, dk_a, dv_a, d_cq, d_ckv, d_kpe]
    t2 = min(512, t)
    pairs, lo = [], 0
    for dpart in dparts:
        width = dpart.shape[1]
        w_part = w_in[:, lo:lo + width]
        pairs.append((dpart, pl.BlockSpec((t2, width), lambda i, j, r: (i, 0)),
                      w_part, pl.BlockSpec((D_MODEL, width), lambda i, j, r: (0, 0))))
        lo += width
    dw_in = jnp.concatenate(_mm_tn_multi("in_proj_dw", hm, dparts, 1024), axis=1)
    grads_b["w_in"] = dw_in.reshape(D_MODEL, N_CHIPS, -1).transpose(1, 0, 2)
    early = tuple(grads_b[n] for n in EARLY) if exchanges else ()
    row2 = pl.BlockSpec((t2, D_MODEL), lambda i, j, r: (i, 0))
    res = _mm("in_proj_dx", (t // t2, 1, 1), pairs, NT, _sds((t, D_MODEL), F32), row2, (t2, D_MODEL),
              res=(dx2, row2), norm=(x1, small["mix_norm"]), outgoing=early, exchange="cores")
    dx1, grads_s["mix_norm"] = res[0], res[1]
    outgoing = exchanges[0](early, res[2]) if exchanges else ()
    dx, grads_s["ffn1_norm"], grads_b["ffn1_w_gate"], grads_b["ffn1_w_up"], grads_b["ffn1_w_down"], arrived, late = _ffn_bwd(
        dx1, x, small["ffn1_norm"], wfull["ffn1_w_gate"], wfull["ffn1_w_up"], wfull["ffn1_w_down"], ffn1_saved, "ffn1",
        outgoing, exchanges[1] if exchanges else None)
    return loss, dx, grads_s, grads_b, (tuple(outgoing), arrived), late


def kernel(x, ffn1_norm, ffn1_w_gate, ffn1_w_up, ffn1_w_down, mix_norm, w_in, dil_q_norm, dil_k_norm, rel_bias, mla_q_a_norm, mla_w_q_b, mla_kv_a_norm, mla_w_kv_b, mla_q_norm, mla_k_norm, out_norm_dil, out_norm_mla, w_out, ffn2_norm, ffn2_w_gate, ffn2_w_up, ffn2_w_down, loss_target, m_ffn1_norm, m_ffn1_w_gate, m_ffn1_w_up, m_ffn1_w_down, m_mix_norm, m_w_in, m_dil_q_norm, m_dil_k_norm, m_rel_bias, m_mla_q_a_norm, m_mla_w_q_b, m_mla_kv_a_norm, m_mla_w_kv_b, m_mla_q_norm, m_mla_k_norm, m_out_norm_dil, m_out_norm_mla, m_w_out, m_ffn2_norm, m_ffn2_w_gate, m_ffn2_w_up, m_ffn2_w_down, v_ffn1_norm, v_ffn1_w_gate, v_ffn1_w_up, v_ffn1_w_down, v_mix_norm, v_w_in, v_dil_q_norm, v_dil_k_norm, v_rel_bias, v_mla_q_a_norm, v_mla_w_q_b, v_mla_kv_a_norm, v_mla_w_kv_b, v_mla_q_norm, v_mla_k_norm, v_out_norm_dil, v_out_norm_mla, v_w_out, v_ffn2_norm, v_ffn2_w_gate, v_ffn2_w_up, v_ffn2_w_down):
    given = dict(locals())
    big_names = [name for name, _ in BIG]
    small_names = [name for name, _, _ in SMALL]

    chip = (2 * lax.axis_index("x") + lax.axis_index("y")).astype(jnp.int32)
    core = lax.axis_index("c").astype(jnp.int32)
    mine = {n: given[n].astype(BF16) for n in big_names}

    def with_own(names, arrays):
        return {n: lax.dynamic_update_slice(a, mine[n], (chip, 0, 0)) for n, a in zip(names, arrays)}

    wfirst = with_own(LATE, _gather_weights([mine[n][0] for n in LATE]))
    later_weights = ([mine[n][0] for n in EARLY], lambda partly: with_own(EARLY, _forward_cores(partly)))
    small = {n: given[n] for n in small_names}

    def early_partials(partial, theirs):
        return _add_halves(partial, theirs, core.reshape(1), "early")

    def late_partials(partial):
        return _add_halves(partial, _reduce_cores(partial, "late"), core.reshape(1), "late")

    exchanges = (early_partials, late_partials)
    loss, dx, grads_s, grads_b, (early_part, early_got), (late_part, late_got) = _local_step(
        x[0], loss_target[0], small, wfirst, exchanges, later_weights)
    reduced = _sum_partials(tuple(late_got) + tuple(early_got), tuple(late_part) + tuple(early_part),
                            jnp.stack([chip, core]))
    g_big = dict(zip(LATE + EARLY, _share_cores(reduced)))
    summed = _allreduce_small(_pack_small(grads_s, loss))
    g_small = _unpack_small(summed)
    loss = summed[SMALL_USED, 0]

    grad, delta, new_m, new_v = {}, {}, {}, {}
    for name, shape in BIG:
        g2 = g_big[name]
        d_, m_, v_ = _adamw(given[name].reshape(shape), g2, given["m_" + name].reshape(shape),
                            given["v_" + name].reshape(shape), f"adamw_{name}")
        full = given[name].shape
        grad[name], delta[name], new_m[name], new_v[name] = (a.reshape(full) for a in (g2, d_, m_, v_))
    for name in small_names:
        grad[name] = g_small[name]
        delta[name], new_m[name], new_v[name] = _adamw(given[name], g_small[name], given["m_" + name],
                                                       given["v_" + name], f"adamw_{name}")

    return (loss, dx[None], *[grad[n] for n in WEIGHTS], *[delta[n] for n in WEIGHTS],
            *[new_m[n] for n in WEIGHTS], *[new_v[n] for n in WEIGHTS])
```

```python
import functools

import numpy as np
import jax
import jax.numpy as jnp
from jax import lax
from jax.experimental import pallas as pl
from jax.experimental.pallas import tpu as pltpu

F32 = jnp.float32
BF16 = jnp.bfloat16

D_MODEL = 1024
D_FF = 2816
N_CHIPS = 4
DIL_HEADS = 8
DIL_HD = 64
DIL_WIDTH = 512
DIL_DILATIONS = (1, 4, 16)
DIL_W = 128
QB = 128
MLA_HEADS = 4
MLA_NOPE = 128
MLA_ROPE = 64
MLA_QK = 192
MLA_V = 128
MLA_Q_RANK = 256
MLA_KV_RANK = 128
ROPE_BASE = 10000.0
REL_BUCKETS = 32
REL_MAX_DIST = 2048
FFN_RESID = 0.5
EPS = 1e-6
NEG = -1e30
LANES = 128

ADAM_LR = 0.001
ADAM_B1 = 0.9
ADAM_B2 = 0.999
ADAM_EPS = 1e-08
ADAM_WD = 0.01
ADAM_STEP = 10

NT = (((1,), (1,)), ((), ()))
NN = (((1,), (0,)), ((), ()))
TN = (((0,), (0,)), ((), ()))

BIG = (
    ("ffn1_w_gate", (D_MODEL, D_FF // N_CHIPS)),
    ("ffn1_w_up", (D_MODEL, D_FF // N_CHIPS)),
    ("ffn1_w_down", (D_FF // N_CHIPS, D_MODEL)),
    ("w_in", (D_MODEL, 1984 // N_CHIPS)),
    ("mla_w_q_b", (MLA_Q_RANK, MLA_QK)),
    ("mla_w_kv_b", (MLA_KV_RANK, MLA_NOPE + MLA_V)),
    ("w_out", (D_MODEL // N_CHIPS, D_MODEL)),
    ("ffn2_w_gate", (D_MODEL, D_FF // N_CHIPS)),
    ("ffn2_w_up", (D_MODEL, D_FF // N_CHIPS)),
    ("ffn2_w_down", (D_FF // N_CHIPS, D_MODEL)),
)
SMALL = (
    ("ffn1_norm", (1, 1024), 8), ("mix_norm", (1, 1024), 8), ("dil_q_norm", (1, 64), 1),
    ("dil_k_norm", (1, 64), 1), ("rel_bias", (8, 32), 2), ("mla_q_a_norm", (1, 256), 2),
    ("mla_kv_a_norm", (1, 128), 1), ("mla_q_norm", (1, 192), 2), ("mla_k_norm", (1, 192), 2),
    ("out_norm_dil", (1, 512), 4), ("out_norm_mla", (1, 512), 4), ("ffn2_norm", (1, 1024), 8),
)
SMALL_ROWS = 48
WEIGHTS = ("ffn1_norm", "ffn1_w_gate", "ffn1_w_up", "ffn1_w_down", "mix_norm", "w_in", "dil_q_norm",
           "dil_k_norm", "rel_bias", "mla_q_a_norm", "mla_w_q_b", "mla_kv_a_norm", "mla_w_kv_b",
           "mla_q_norm", "mla_k_norm", "out_norm_dil", "out_norm_mla", "w_out", "ffn2_norm",
           "ffn2_w_gate", "ffn2_w_up", "ffn2_w_down")


def _pcall(body, **kw):
    return pl.pallas_call(body, **kw)


def _cparams(*sem):
    return pltpu.CompilerParams(dimension_semantics=sem)


def _sds(shape, dtype):
    return jax.ShapeDtypeStruct(shape, dtype)


def _dot(a, b, dn):
    return lax.dot_general(a, b, dn, preferred_element_type=F32)


def _rms_fwd(x, g, out_dtype, name, tm):
    n, d = x.shape
    tm = min(tm, n)

    def body(x_ref, g_ref, o_ref):
        xf = x_ref[...].astype(F32)
        r = lax.rsqrt(jnp.mean(xf * xf, axis=-1, keepdims=True) + EPS)
        o_ref[...] = (xf * r * g_ref[...]).astype(o_ref.dtype)

    return _pcall(
        body, name=name, grid=(n // tm,),
        in_specs=[pl.BlockSpec((tm, d), lambda i: (i, 0)), pl.BlockSpec((1, d), lambda i: (0, 0))],
        out_specs=pl.BlockSpec((tm, d), lambda i: (i, 0)),
        out_shape=_sds((n, d), out_dtype), compiler_params=_cparams("parallel"))(x, g)


def _rms_bwd(dys, x, g, res, name, tm):
    n, d = x.shape
    tm = min(tm, n)
    nd = len(dys)
    has_res = res is not None

    def body(*refs):
        dy_refs = refs[:nd]
        x_ref, g_ref = refs[nd], refs[nd + 1]
        res_ref = refs[nd + 2] if has_res else None
        dx_ref, dg_ref = refs[-2], refs[-1]
        dy = dy_refs[0][...].astype(F32)
        for r_ in dy_refs[1:]:
            dy = dy + r_[...].astype(F32)
        xf = x_ref[...].astype(F32)
        r = lax.rsqrt(jnp.mean(xf * xf, axis=-1, keepdims=True) + EPS)
        xh = xf * r
        dxh = dy * g_ref[...]
        dx = r * (dxh - xh * jnp.mean(dxh * xh, axis=-1, keepdims=True))
        if has_res:
            dx = dx + res_ref[...]
        dx_ref[...] = dx

        @pl.when(pl.program_id(0) == 0)
        def _():
            dg_ref[...] = jnp.zeros_like(dg_ref)

        dg_ref[...] += jnp.sum(dy * xh, axis=0, keepdims=True)

    row = pl.BlockSpec((tm, d), lambda i: (i, 0))
    vec = pl.BlockSpec((1, d), lambda i: (0, 0))
    ins = list(dys) + [x, g] + ([res] if has_res else [])
    return _pcall(
        body, name=name, grid=(n // tm,),
        in_specs=[row] * nd + [row, vec] + ([row] if has_res else []),
        out_specs=(row, vec),
        out_shape=(_sds((n, d), F32), _sds((1, d), F32)),
        compiler_params=_cparams("arbitrary"))(*ins)


def _mm(name, grid, pairs, dn, out_shape, out_spec, acc_shape, res=None, scale=1.0, outgoing=(), norm=None,
        exchange="chips"):
    npairs = len(pairs)
    nred = grid[2]
    has_res = res is not None
    has_norm = norm is not None
    no = len(outgoing)
    ex_start, ex_wait, ex_shapes, ex_sems = EXCHANGES[exchange]

    def body(*refs):
        ab = refs[:2 * npairs]
        res_ref = refs[2 * npairs] if has_res else None
        nin = 2 * npairs + int(has_res) + 2 * int(has_norm)
        if has_norm:
            x_ref, g_ref = refs[nin - 2:nin]
        first_out = nin + no
        sent = refs[nin:first_out]
        o_ref = refs[first_out]
        nout = 1 + int(has_norm)
        dg_ref = refs[first_out + 1] if has_norm else None
        arrived = refs[first_out + nout:first_out + nout + no]
        acc_ref = refs[first_out + nout + no] if nred > 1 else None
        if no:
            send_sems, recv_sems = refs[-2:]
            ids = [pl.program_id(n) for n in range(3)]

            @pl.when((ids[0] == 0) & (ids[1] == 0) & (ids[2] == 0))
            def _():
                ex_start(sent, arrived, send_sems, recv_sems)

        tot = None
        for p in range(npairs):
            d = _dot(ab[2 * p][...].astype(BF16), ab[2 * p + 1][...].astype(BF16), dn)
            tot = d if tot is None else tot + d

        def finish(v):
            if scale != 1.0:
                v = v * scale
            if has_norm:
                xf = x_ref[...]
                r = lax.rsqrt(jnp.mean(xf * xf, axis=-1, keepdims=True) + EPS)
                xh = xf * r
                dxh = v * g_ref[...]

                @pl.when(pl.program_id(0) == 0)
                def _():
                    dg_ref[...] = jnp.zeros_like(dg_ref)

                dg_ref[...] += jnp.sum(v * xh, axis=0, keepdims=True)
                v = r * (dxh - xh * jnp.mean(dxh * xh, axis=-1, keepdims=True))
            if has_res:
                v = res_ref[...] + v
            o_ref[...] = v.astype(o_ref.dtype)

        if nred == 1:
            finish(tot)
        else:
            r = pl.program_id(2)

            @pl.when(r == 0)
            def _():
                acc_ref[...] = tot

            @pl.when(r > 0)
            def _():
                acc_ref[...] += tot

            @pl.when(r == nred - 1)
            def _():
                finish(acc_ref[...])

        if no:
            @pl.when((ids[0] == grid[0] - 1) & (ids[1] == grid[1] - 1) & (ids[2] == nred - 1))
            def _():
                ex_wait(sent, arrived, send_sems, recv_sems)

    ins, specs = [], []
    for a, a_spec, b, b_spec in pairs:
        ins += [a, b]
        specs += [a_spec, b_spec]
    if has_res:
        ins.append(res[0])
        specs.append(res[1])
    scratch = [pltpu.VMEM(acc_shape, F32)] if nred > 1 else []
    if not no and not has_norm:
        return _pcall(
            body, name=name, grid=grid, in_specs=specs, out_specs=out_spec, out_shape=out_shape,
            scratch_shapes=scratch, compiler_params=_cparams("parallel", "parallel", "arbitrary"))(*ins)
    out_specs, out_shapes = (out_spec,), (out_shape,)
    if has_norm:
        assert grid[1] == 1
        d = norm[1].shape[1]
        ins += [norm[0], norm[1]]
        specs += [out_spec, pl.BlockSpec((1, d), lambda i, j, r: (0, 0))]
        out_specs += (pl.BlockSpec((1, d), lambda i, j, r: (0, 0)),)
        out_shapes += (_sds((1, d), F32),)
    hbm = pl.BlockSpec(memory_space=pltpu.HBM)
    res_ = tuple(_pcall(
        body, name=name, grid=grid, in_specs=specs + [hbm] * no, out_specs=out_specs + (hbm,) * no,
        out_shape=out_shapes + ex_shapes(outgoing),
        scratch_shapes=scratch + (ex_sems(no) if no else []),
        compiler_params=_cparams("arbitrary", "arbitrary", "arbitrary"))(*ins, *outgoing))
    nout = len(out_shapes)
    return res_[:nout] + ((res_[nout:],) if no else ())


def _ffn_up(h, wg, wu, name, tm, incoming=()):
    t, d = h.shape
    nc, _, fs = wg.shape
    tm = min(tm, t)
    nt = t // tm
    ni = len(incoming)
    halves = _halves(incoming)

    def body(*refs):
        h_ref, wg_ref, wu_ref = refs[:3]
        srcs = refs[3:3 + ni]
        g_ref, u_ref, a_ref = refs[3 + ni:6 + ni]
        outs = refs[6 + ni:6 + 2 * ni]
        if ni:
            send_sems, recv_sems = refs[6 + 2 * ni:]
            c, i = pl.program_id(0), pl.program_id(1)

            @pl.when((c == 0) & (i == 0))
            def _():
                _gather_start(srcs, outs, halves, send_sems, recv_sems)

        hh = h_ref[...]
        gate = _dot(hh, wg_ref[...], NN)
        up = _dot(hh, wu_ref[...], NN)
        sig = jax.nn.sigmoid(gate)
        silu = gate * sig
        g_ref[...] = (up * (sig + silu * (1.0 - sig))).astype(BF16)
        u_ref[...] = silu.astype(BF16)
        a_ref[...] = (silu * up).astype(BF16)

        if ni:
            @pl.when((c == nc - 1) & (i == nt - 1))
            def _():
                _gather_wait(outs, halves, send_sems, recv_sems)

    wspec = pl.BlockSpec((None, d, fs), lambda c, i: (c, 0, 0))
    ospec = pl.BlockSpec((None, tm, fs), lambda c, i: (c, i, 0))
    hbm = pl.BlockSpec(memory_space=pltpu.HBM)
    osd = _sds((nc, t, fs), BF16)
    res = tuple(_pcall(
        body, name=name, grid=(nc, nt),
        in_specs=[pl.BlockSpec((tm, d), lambda c, i: (i, 0)), wspec, wspec] + [hbm] * ni,
        out_specs=(ospec, ospec, ospec) + (hbm,) * ni,
        out_shape=(osd, osd, osd) + tuple(_sds((N_CHIPS,) + b.shape, b.dtype) for b in incoming),
        scratch_shapes=[pltpu.SemaphoreType.DMA((3 * ni,)), pltpu.SemaphoreType.DMA((3 * ni,))] if ni else [],
        compiler_params=_cparams("arbitrary", "arbitrary"))(h, wg, wu, *incoming))
    return res[:3] + (res[3:],)


def _ffn_hidden_bwd(dy, h, wd, dact_dgate, dact_dup, act, name, tm, outgoing=()):
    t, d = dy.shape
    nc, fs, _ = wd.shape
    tm = min(tm, t)
    nt = t // tm
    no = len(outgoing)

    def body(*refs):
        dy_ref, h_ref, wd_ref, g_ref, u_ref, a_ref = refs[:6]
        sent = refs[6:6 + no]
        dg_ref, du_ref, dwg_hbm, dwu_hbm, dwd_hbm = refs[6 + no:11 + no]
        arrived = refs[11 + no:11 + 2 * no]
        wg_acc, wu_acc, wd_acc, sem = refs[11 + 2 * no:15 + 2 * no]
        c, i = pl.program_id(0), pl.program_id(1)
        if no:
            send_sems, recv_sems = refs[15 + 2 * no:]

            @pl.when((c == 0) & (i == 0))
            def _():
                _scatter_start(sent, arrived, send_sems, recv_sems)

        dyb = dy_ref[...].astype(BF16)
        da = _dot(dyb, wd_ref[...], NT) * FFN_RESID
        dgate = (da * g_ref[...].astype(F32)).astype(BF16)
        dup = (da * u_ref[...].astype(F32)).astype(BF16)
        dg_ref[...] = dgate
        du_ref[...] = dup
        hh = h_ref[...]
        parts = (_dot(hh, dgate, TN), _dot(hh, dup, TN), _dot(a_ref[...], dyb, TN) * FFN_RESID)
        accs = (wg_acc, wu_acc, wd_acc)

        @pl.when(i == 0)
        def _():
            for acc, part in zip(accs, parts):
                acc[...] = part

        @pl.when(i > 0)
        def _():
            for acc, part in zip(accs, parts):
                acc[...] += part

        @pl.when(i == nt - 1)
        def _():
            copies = [pltpu.make_async_copy(acc, out.at[c], sem.at[n])
                      for n, (acc, out) in enumerate(zip(accs, (dwg_hbm, dwu_hbm, dwd_hbm)))]
            for cp in copies:
                cp.start()
            for cp in copies:
                cp.wait()

        if no:
            @pl.when((c == nc - 1) & (i == nt - 1))
            def _():
                _scatter_wait(sent, arrived, send_sems, recv_sems)

    tok = pl.BlockSpec((tm, d), lambda c, i: (i, 0))
    cspec = pl.BlockSpec((None, tm, fs), lambda c, i: (c, i, 0))
    hbm = pl.BlockSpec(memory_space=pltpu.HBM)
    osd = _sds((nc, t, fs), BF16)
    res = _pcall(
        body, name=name, grid=(nc, nt),
        in_specs=[tok, tok, pl.BlockSpec((None, fs, d), lambda c, i: (c, 0, 0)), cspec, cspec, cspec] + [hbm] * no,
        out_specs=(cspec, cspec, hbm, hbm, hbm) + (hbm,) * no,
        out_shape=(osd, osd, _sds((nc, d, fs), F32), _sds((nc, d, fs), F32), _sds((nc, fs, d), F32))
        + _scatter_shapes(outgoing),
        scratch_shapes=[pltpu.VMEM((d, fs), F32), pltpu.VMEM((d, fs), F32), pltpu.VMEM((fs, d), F32),
                        pltpu.SemaphoreType.DMA((3,))] + (_scatter_sems(no) if no else []),
        compiler_params=_cparams("arbitrary", "arbitrary"))(dy, h, wd, dact_dgate, dact_dup, act, *outgoing)
    res = tuple(res)
    return res[:5] + (res[5:],)


def _ffn_fwd(x, g, wg, wu, wd, tag, incoming=(), target=None):
    t = x.shape[0]
    nc, _, fs = wg.shape
    tm = min(512, t)
    h = _rms_fwd(x, g, BF16, f"{tag}_norm", 512)
    behind_down = tuple(incoming[-1:])
    dact_dgate, dact_dup, act, partly = _ffn_up(h, wg, wu, f"{tag}_up", 1024, tuple(incoming[:-1]))
    if target is not None:
        return _ffn_down_loss(act, wd, x, target, f"{tag}_down_loss", 512), (h, dact_dgate, dact_dup, act), partly
    pairs = [(act, pl.BlockSpec((None, tm, fs), lambda i, j, r, c=c: (c, i, 0)),
              wd, pl.BlockSpec((None, fs, D_MODEL), lambda i, j, r, c=c: (c, 0, 0))) for c in range(nc)]
    row = pl.BlockSpec((tm, D_MODEL), lambda i, j, r: (i, 0))
    y = _mm(f"{tag}_down", (t // tm, 1, 1), pairs, NN, _sds((t, D_MODEL), F32), row, (tm, D_MODEL),
            res=(x, row), scale=FFN_RESID, outgoing=behind_down, exchange="gather")
    if behind_down:
        y, more = y
        partly = tuple(partly) + tuple(more)
    return y, (h, dact_dgate, dact_dup, act), partly


def _ffn_bwd(dy, x, g, wg, wu, wd, saved, tag, outgoing=(), own_exchange=None):
    h, dact_dgate, dact_dup, act = saved
    t = x.shape[0]
    nc, _, fs = wg.shape
    tm = min(512, t)
    dgate, dup, dwg, dwu, dwd, arrived = _ffn_hidden_bwd(dy, h, wd, dact_dgate, dact_dup, act,
                                                         f"{tag}_hidden_bwd", 1024, outgoing)
    pairs = []
    for c in range(nc):
        a_spec = pl.BlockSpec((None, tm, fs), lambda i, j, r, c=c: (c, i, 0))
        w_spec = pl.BlockSpec((None, D_MODEL, fs), lambda i, j, r, c=c: (c, 0, 0))
        pairs += [(dgate, a_spec, wg, w_spec), (dup, a_spec, wu, w_spec)]
    own_part = tuple(own_exchange([dwg, dwu, dwd])) if own_exchange else ()
    row = pl.BlockSpec((tm, D_MODEL), lambda i, j, r: (i, 0))
    res = _mm(f"{tag}_dh", (t // tm, 1, 1), pairs, NT, _sds((t, D_MODEL), F32), row, (tm, D_MODEL),
              res=(dy, row), norm=(x, g), outgoing=own_part)
    dx, dg = res[0], res[1]
    own_got = res[2] if own_part else ()
    return dx, dg, dwg, dwu, dwd, arrived, (own_part, own_got)


def _mm_tn_multi(name, a, bs, tk):
    k, m = a.shape
    tk = min(tk, k)
    nb = len(bs)

    def body(*refs):
        a_ref, b_refs, o_refs = refs[0], refs[1:1 + nb], refs[1 + nb:]
        aa = a_ref[...].astype(BF16)
        parts = [_dot(aa, b_ref[...].astype(BF16), TN) for b_ref in b_refs]

        @pl.when(pl.program_id(0) == 0)
        def _():
            for o_ref, part in zip(o_refs, parts):
                o_ref[...] = part

        @pl.when(pl.program_id(0) > 0)
        def _():
            for o_ref, part in zip(o_refs, parts):
                o_ref[...] += part

    return _pcall(
        body, name=name, grid=(k // tk,),
        in_specs=[pl.BlockSpec((tk, m), lambda r: (r, 0))] + [pl.BlockSpec((tk, b.shape[1]), lambda r: (r, 0)) for b in bs],
        out_specs=tuple(pl.BlockSpec((m, b.shape[1]), lambda r: (0, 0)) for b in bs),
        out_shape=tuple(_sds((m, b.shape[1]), F32) for b in bs),
        compiler_params=_cparams("arbitrary"))(a, *bs)


def _mm_simple(name, a, b, dn, out_dtype, tm=512, tk=512, res=None, scale=1.0):
    if dn == TN:
        k, m = a.shape
        n = b.shape[1]
        tk = min(tk, k)
        return _mm(name, (1, 1, k // tk),
                   [(a, pl.BlockSpec((tk, m), lambda i, j, r: (r, 0)), b, pl.BlockSpec((tk, n), lambda i, j, r: (r, 0)))],
                   TN, _sds((m, n), out_dtype), pl.BlockSpec((m, n), lambda i, j, r: (0, 0)), (m, n), scale=scale)
    m, k = a.shape
    n = b.shape[1] if dn == NN else b.shape[0]
    tm = min(tm, m)
    row = pl.BlockSpec((tm, n), lambda i, j, r: (i, 0))
    return _mm(name, (m // tm, 1, 1),
               [(a, pl.BlockSpec((tm, k), lambda i, j, r: (i, 0)), b, pl.BlockSpec(b.shape, lambda i, j, r: (0, 0)))],
               dn, _sds((m, n), out_dtype), row, (tm, n), res=None if res is None else (res, row), scale=scale)


def _t5_bucket(dist):
    max_exact = REL_BUCKETS // 2
    d = np.maximum(dist, 1).astype(np.float32)
    large = max_exact + (np.log(d / max_exact) / np.log(REL_MAX_DIST / max_exact)
                         * (REL_BUCKETS - max_exact)).astype(np.int32)
    large = np.minimum(large, REL_BUCKETS - 1)
    return np.where(dist < max_exact, dist, large).astype(np.int32)


def _bucket_tiles():
    i = np.arange(QB)[:, None]
    j = np.arange(QB + DIL_W)[None, :]
    delta = np.clip(i + DIL_W - j, 0, None)
    return np.stack([_t5_bucket(delta * dil) for dil in DIL_DILATIONS]).astype(np.int32)


def _bias_tiles(rel_bias):
    buckets = jnp.asarray(_bucket_tiles())

    def body(rb_ref, bk_ref, o_ref):
        bk = bk_ref[...]
        for h in range(DIL_HEADS):
            def pick(b, tile):
                return jnp.where(bk == b, rb_ref[h, b], tile)

            o_ref[h] = lax.fori_loop(0, REL_BUCKETS, pick, jnp.zeros((QB, QB + DIL_W), F32))

    return _pcall(
        body, name="dil_bias_tiles", grid=(3,),
        in_specs=[pl.BlockSpec(memory_space=pltpu.SMEM),
                  pl.BlockSpec((None, QB, QB + DIL_W), lambda b: (b, 0, 0))],
        out_specs=pl.BlockSpec((None, DIL_HEADS, QB, QB + DIL_W), lambda b: (b, 0, 0, 0)),
        out_shape=_sds((3, DIL_HEADS, QB, QB + DIL_W), F32),
        compiler_params=_cparams("parallel"))(rel_bias, buckets)


def _bias_grad(dtiles):
    buckets = jnp.asarray(_bucket_tiles())

    def body(dt_ref, bk_ref, o_ref):
        def one(b, carry):
            hit = [bk_ref[br] == b for br in range(3)]
            for h in range(DIL_HEADS):
                tot = jnp.zeros((), F32)
                for br in range(3):
                    tot = tot + jnp.sum(jnp.where(hit[br], dt_ref[br, h], 0.0))
                o_ref[h, b] = tot
            return carry

        lax.fori_loop(0, REL_BUCKETS, one, 0)

    return _pcall(
        body, name="dil_bias_grad",
        in_specs=[pl.BlockSpec(memory_space=pltpu.VMEM), pl.BlockSpec(memory_space=pltpu.VMEM)],
        out_specs=pl.BlockSpec(memory_space=pltpu.SMEM),
        out_shape=_sds((DIL_HEADS, REL_BUCKETS), F32))(dtiles, buckets)


def _split_heads(a, lo):
    zero = jnp.zeros_like(a)
    return jnp.concatenate([jnp.where(lo, a, zero), jnp.where(lo, zero, a)], axis=0)


def _side_by_side(a):
    n = a.shape[0] // 2
    return jnp.concatenate([a[:n], a[n:]], axis=1)


def _band_masks(prev_ok):
    ii = lax.broadcasted_iota(jnp.int32, (2 * QB, QB), 0) & (QB - 1)
    jj = lax.broadcasted_iota(jnp.int32, (2 * QB, QB), 1)
    return jj <= ii, jj >= ii + jnp.where(prev_ok, 0, QB)


def _dil_fwd(q, k, v, bias, dil, name):
    w = DIL_WIDTH
    t = q.shape[0] * dil
    npair = w // LANES
    nl = t // dil // QB
    scale = DIL_HD ** -0.5

    def body(q_ref, kc_ref, kp_ref, vc_ref, vp_ref, b_ref, o_ref, lse_ref):
        nn = pl.program_id(1)
        lo = lax.broadcasted_iota(jnp.int32, (QB, LANES), 1) < DIL_HD
        lo2 = lax.broadcasted_iota(jnp.int32, (2 * QB, LANES), 1) < DIL_HD
        ii = lax.broadcasted_iota(jnp.int32, (2 * QB, 2 * QB), 0) & (QB - 1)
        jj = lax.broadcasted_iota(jnp.int32, (2 * QB, 2 * QB), 1)
        first_key = jnp.maximum(ii, jnp.where(nn != 0, 0, QB))
        valid = (jj >= first_key) & (jj <= ii + QB)
        for p in range(npair):
            cols = slice(p * LANES, (p + 1) * LANES)
            qq = _split_heads(q_ref[:, cols], lo)
            kk = jnp.concatenate([kp_ref[:, cols], kc_ref[:, cols]], axis=0)
            vv = jnp.concatenate([vp_ref[:, cols], vc_ref[:, cols]], axis=0)
            s = jnp.where(valid, _dot(qq, kk, NT) * scale + b_ref[p], NEG)
            m = jnp.max(s, axis=-1, keepdims=True)
            e = jnp.exp(s - m)
            den = jnp.sum(e, axis=-1, keepdims=True)
            pn = (e * (1.0 / den)).astype(BF16)
            o_ref[:, cols] = _dot(_side_by_side(pn), _split_heads(vv, lo2), NN)
            lse = m + jnp.log(den)
            lse_ref[:, cols] = jnp.where(lo, lse[:QB], lse[QB:])

    cur = pl.BlockSpec((QB, w), lambda r, n: (n, r))
    prev = pl.BlockSpec((QB, w), lambda r, n: (jnp.maximum(n - 1, 0), r))
    sd = _sds((t // dil, dil * w), F32)
    return _pcall(
        body, name=name, grid=(dil, nl),
        in_specs=[cur, cur, prev, cur, prev, pl.BlockSpec((npair, 2 * QB, 2 * QB), lambda r, n: (0, 0, 0))],
        out_specs=(cur, cur), out_shape=(sd, sd),
        compiler_params=_cparams("parallel", "parallel"))(q, k, k, v, v, bias)


def _dil_bwd(q, k, v, do, stats, bias, dil, name):
    w = DIL_WIDTH
    t = q.shape[0] * dil
    npair = w // LANES
    nl = t // dil // QB
    scale = DIL_HD ** -0.5

    def body(qc_ref, qn_ref, doc_ref, don_ref, sc_ref, sn_ref, k_ref, v_ref, b_ref,
             dq_ref, dk_ref, dv_ref, db_ref, carry):
        r, nn = pl.program_id(0), pl.program_id(1)
        lo = lax.broadcasted_iota(jnp.int32, (QB, LANES), 1) < DIL_HD
        cur_ok, prev_ok = _band_masks(nn + 1 < nl)

        @pl.when((r == 0) & (nn == 0))
        def _():
            db_ref[...] = jnp.zeros_like(db_ref)
            carry[...] = jnp.zeros_like(carry)

        for p in range(npair):
            cols = slice(p * LANES, (p + 1) * LANES)
            kp, vp = k_ref[:, cols], v_ref[:, cols]
            k2 = _split_heads(kp, lo)

            def column(ref, lane):
                first = p * LANES + lane
                return jnp.concatenate([ref[:, first:first + 1], ref[:, first + DIL_HD:first + DIL_HD + 1]], axis=0)

            def side(q_ref, do_ref, s_ref, bias, ok):
                qq = _split_heads(q_ref[:, cols], lo)
                dd = _split_heads(do_ref[:, cols], lo)
                s = jnp.where(ok, _dot(qq, kp, NT) * scale + bias, NEG)
                prob = jnp.exp(s - column(s_ref, 0))
                ds = prob * (_dot(dd, vp, NT) - column(s_ref, DIL_HD // 2))
                return qq, dd, prob.astype(BF16), ds

            q1, d1, p1, ds1 = side(qc_ref, doc_ref, sc_ref, b_ref[p, :, QB:], cur_ok)
            q2, d2, p2, ds2 = side(qn_ref, don_ref, sn_ref, b_ref[p, :, :QB], prev_ok)
            ds1b, ds2b = ds1.astype(BF16), ds2.astype(BF16)
            dq_ref[:, cols] = carry[:, cols] + _dot(_side_by_side(ds1b), k2, NN) * scale
            carry[:, cols] = _dot(_side_by_side(ds2b), k2, NN) * scale
            dk_ref[:, cols] = _dot(jnp.concatenate([ds1b, ds2b], axis=0), jnp.concatenate([q1, q2], axis=0), TN) * scale
            dv_ref[:, cols] = _dot(jnp.concatenate([p1, p2], axis=0), jnp.concatenate([d1, d2], axis=0), TN)
            db_ref[p, :, QB:] += ds1
            db_ref[p, :, :QB] += ds2

    cur = pl.BlockSpec((QB, w), lambda r, n: (n, r))
    nxt = pl.BlockSpec((QB, w), lambda r, n: (jnp.minimum(n + 1, nl - 1), r))
    tile = pl.BlockSpec((npair, 2 * QB, 2 * QB), lambda r, n: (0, 0, 0))
    sd = _sds((t // dil, dil * w), F32)
    return _pcall(
        body, name=name, grid=(dil, nl),
        in_specs=[cur, nxt, cur, nxt, cur, nxt, cur, cur, tile],
        out_specs=(cur, cur, cur, tile),
        out_shape=(sd, sd, sd, _sds((npair, 2 * QB, 2 * QB), F32)),
        scratch_shapes=[pltpu.VMEM((QB, w), F32)],
        compiler_params=_cparams("arbitrary", "arbitrary"))(q, q, do, do, stats, stats, k, v, bias)


def _head_sum_matrix(scale):
    idx = np.arange(DIL_WIDTH) // DIL_HD
    return jnp.asarray((idx[:, None] == idx[None, :]).astype(np.float32) * scale, BF16)


def _head_sum(x, mat):
    hi = x.astype(BF16)
    lo = (x - hi.astype(F32)).astype(BF16)
    return _dot(hi, mat, NN) + _dot(lo, mat, NN)


def _to_views(src, tmp, out_refs):
    tm, w = src.shape
    for j in range(w // LANES):
        tmp[j] = src[:, j * LANES:(j + 1) * LANES]
    for d, o_ref in zip(DIL_DILATIONS, out_refs):
        if d == 1:
            o_ref[...] = src.astype(o_ref.dtype)
            continue
        for r in range(d):
            for j in range(w // LANES):
                lo = r * w + j * LANES
                o_ref[:, lo:lo + LANES] = tmp[j, pl.ds(r, tm // d, stride=d), :].astype(o_ref.dtype)


def _from_view(v_ref, tmp, d):
    tm = tmp.shape[1]
    w = v_ref.shape[1] // d
    for r in range(d):
        for j in range(w // LANES):
            lo = r * w + j * LANES
            tmp[j, pl.ds(r, tm // d, stride=d), :] = v_ref[:, lo:lo + LANES]
    return jnp.concatenate([tmp[j] for j in range(w // LANES)], axis=1)


def _view_specs(tm, t, dtype):
    specs = tuple(pl.BlockSpec((tm // d, d * DIL_WIDTH), lambda i: (i, 0)) for d in DIL_DILATIONS)
    shapes = tuple(_sds((t // d, d * DIL_WIDTH), dtype) for d in DIL_DILATIONS)
    return specs, shapes


def _view_scratch(tm):
    return pltpu.VMEM((DIL_WIDTH // LANES, tm, LANES), F32)


def _dil_merge(outs, lses, g, tm):
    w = DIL_WIDTH
    t = outs[0].shape[0]
    tm = min(tm, t)

    def body(o0, o1, o2, l0, l1, l2, g_ref, o_ref, l_ref, n_ref, so1, so2, sl1, sl2):
        d1, d2 = DIL_DILATIONS[1], DIL_DILATIONS[2]
        a0, a1, a2 = l0[...], _from_view(l1, sl1, d1), _from_view(l2, sl2, d2)
        m = jnp.maximum(jnp.maximum(a0, a1), a2)
        e0, e1, e2 = jnp.exp(a0 - m), jnp.exp(a1 - m), jnp.exp(a2 - m)
        den = e0 + e1 + e2
        o = (e0 * o0[...] + e1 * _from_view(o1, so1, d1) + e2 * _from_view(o2, so2, d2)) / den
        o_ref[...] = o
        l_ref[...] = m + jnp.log(den)
        r = lax.rsqrt(jnp.mean(o * o, axis=-1, keepdims=True) + EPS)
        n_ref[...] = (o * r * g_ref[...]).astype(n_ref.dtype)

    specs, _ = _view_specs(tm, t, F32)
    spec = pl.BlockSpec((tm, w), lambda i: (i, 0))
    return _pcall(
        body, name="dil_merge", grid=(t // tm,),
        in_specs=list(specs) * 2 + [pl.BlockSpec((1, w), lambda i: (0, 0))], out_specs=(spec, spec, spec),
        out_shape=(_sds((t, w), F32), _sds((t, w), F32), _sds((t, w), BF16)),
        scratch_shapes=[_view_scratch(tm)] * 4,
        compiler_params=_cparams("parallel"))(*outs, *lses, g)


def _dil_stats(do, o, lse, tm):
    t, w = do.shape
    tm = min(tm, t)

    def body(a_ref, b_ref, l_ref, m_ref, s1, s4, s16, d1, d4, d16, tmp):
        first = (lax.broadcasted_iota(jnp.int32, (tm, w), 1) & (DIL_HD - 1)) < DIL_HD // 2
        do_ = a_ref[...]
        _to_views(jnp.where(first, l_ref[...], _head_sum(do_ * b_ref[...], m_ref[...])), tmp, (s1, s4, s16))
        _to_views(do_, tmp, (d1, d4, d16))

    spec = pl.BlockSpec((tm, w), lambda i: (i, 0))
    f_specs, f_shapes = _view_specs(tm, t, F32)
    b_specs, b_shapes = _view_specs(tm, t, BF16)
    res = _pcall(body, name="dil_stats", grid=(t // tm,),
                 in_specs=[spec, spec, spec, pl.BlockSpec((w, w), lambda i: (0, 0))],
                 out_specs=f_specs + b_specs, out_shape=f_shapes + b_shapes,
                 scratch_shapes=[_view_scratch(tm)],
                 compiler_params=_cparams("parallel"))(do, o, lse, _head_sum_matrix(1.0))
    return res[:3], res[3:]


def _head_norm_fwd(x, col, g, name, tm):
    t = x.shape[0]
    w = DIL_WIDTH
    tm = min(tm, t)
    normed = g is not None

    def body(*refs):
        outs, tmp = refs[-4:-1], refs[-1]
        xf = refs[0][...]
        if normed:
            g_ref, m_ref = refs[1], refs[2]
            xf = xf * lax.rsqrt(_head_sum(xf * xf, m_ref[...]) + EPS) * g_ref[...]
        _to_views(xf, tmp, outs)

    specs, shapes = _view_specs(tm, t, BF16)
    extra = [g, _head_sum_matrix(1.0 / DIL_HD)] if normed else []
    extra_specs = [pl.BlockSpec((1, w), lambda i: (0, 0)), pl.BlockSpec((w, w), lambda i: (0, 0))] if normed else []
    return _pcall(
        body, name=name, grid=(t // tm,),
        in_specs=[pl.BlockSpec((tm, w), lambda i: (i, col))] + extra_specs,
        out_specs=specs, out_shape=shapes, scratch_shapes=[_view_scratch(tm)],
        compiler_params=_cparams("parallel"))(x, *extra)


def _head_norm_bwd(dys, x, col, g, name, tm):
    t = x.shape[0]
    w = DIL_WIDTH
    tm = min(tm, t)
    nd = len(dys)
    nt = t // tm
    lane = np.arange(w) % DIL_HD
    fold = jnp.asarray((lane[:, None] == lane[None, :]).astype(np.float32))

    def body(*refs):
        x_ref, g_ref, m_ref, f_ref = refs[nd:nd + 4]
        dx_ref, dg_ref, s1, s2 = refs[-4:]
        dy = refs[0][...] + _from_view(refs[1], s1, DIL_DILATIONS[1]) + _from_view(refs[2], s2, DIL_DILATIONS[2])
        xf = x_ref[...]
        mat = m_ref[...]
        r = lax.rsqrt(_head_sum(xf * xf, mat) + EPS)
        xh = xf * r
        dxh = dy * g_ref[...]
        dx_ref[...] = r * (dxh - xh * _head_sum(dxh * xh, mat))

        @pl.when(pl.program_id(0) == 0)
        def _():
            dg_ref[...] = jnp.zeros_like(dg_ref)

        dg_ref[...] += jnp.sum(dy * xh, axis=0, keepdims=True)

        @pl.when(pl.program_id(0) == nt - 1)
        def _():
            per_lane = jnp.broadcast_to(dg_ref[...], (8, w))
            dg_ref[...] = lax.dot_general(per_lane, f_ref[...], NN, precision=lax.Precision.HIGHEST,
                                          preferred_element_type=F32)[0:1]

    row = pl.BlockSpec((tm, w), lambda i: (i, 0))
    vec = pl.BlockSpec((1, w), lambda i: (0, 0))
    sq = pl.BlockSpec((w, w), lambda i: (0, 0))
    views, _ = _view_specs(tm, t, F32)
    return _pcall(
        body, name=name, grid=(nt,),
        in_specs=list(views) + [pl.BlockSpec((tm, w), lambda i: (i, col)), vec, sq, sq],
        out_specs=(row, vec), out_shape=(_sds((t, w), F32), _sds((1, w), F32)),
        scratch_shapes=[_view_scratch(tm)] * 2,
        compiler_params=_cparams("arbitrary"))(*dys, x, g, _head_sum_matrix(1.0 / DIL_HD), fold)


def _rowdot(a, b, name, tm):
    n, d = a.shape
    tm = min(tm, n)

    def body(a_ref, b_ref, o_ref):
        o_ref[...] = jnp.sum(a_ref[...].astype(F32) * b_ref[...].astype(F32), axis=-1, keepdims=True)

    spec = pl.BlockSpec((tm, d), lambda i: (i, 0))
    return _pcall(body, name=name, grid=(n // tm,), in_specs=[spec, spec],
                  out_specs=pl.BlockSpec((tm, 1), lambda i: (i, 0)), out_shape=_sds((n, 1), F32),
                  compiler_params=_cparams("parallel"))(a, b)


def _sum_branches(parts, name, tm):
    t = parts[0].shape[0]
    w = DIL_WIDTH
    tm = min(tm, t)

    def body(a_ref, b_ref, c_ref, o_ref, s1, s2):
        o_ref[...] = a_ref[...] + _from_view(b_ref, s1, DIL_DILATIONS[1]) + _from_view(c_ref, s2, DIL_DILATIONS[2])

    views, _ = _view_specs(tm, t, F32)
    return _pcall(body, name=name, grid=(t // tm,), in_specs=list(views),
                  out_specs=pl.BlockSpec((tm, w), lambda i: (i, 0)), out_shape=_sds((t, w), F32),
                  scratch_shapes=[_view_scratch(tm)] * 2,
                  compiler_params=_cparams("parallel"))(*parts)


def _rope_tables(t):
    inv = ROPE_BASE ** (-np.arange(0, MLA_ROPE, 2, dtype=np.float64) / MLA_ROPE)
    ang = np.arange(t, dtype=np.float64)[:, None] * inv[None, :]
    cos, sin = np.cos(ang), np.sin(ang)
    return (jnp.asarray(np.concatenate([cos, cos], 1), F32), jnp.asarray(np.concatenate([-sin, sin], 1), F32))


def _swap_halves(a):
    half = MLA_ROPE // 2
    return jnp.concatenate([a[:, half:], a[:, :half]], axis=1)


def _qk_parts(x, pe, tm, nt):
    if pe is None:
        return None
    return (pl.BlockSpec((tm, MLA_NOPE), lambda i: (i, 0)), pl.BlockSpec((tm, MLA_ROPE), lambda i: (i % nt, 0)))


def _mla_qk_fwd(x, g, cos_t, sin_t, scale, name, tm, pe=None):
    n = x.shape[0]
    d = MLA_QK
    t = cos_t.shape[0]
    tm = min(tm, t)
    nt = t // tm
    split = _qk_parts(x, pe, tm, nt)

    def transposed(a):
        w = a.shape[1]
        eye = (lax.broadcasted_iota(jnp.int32, (w, w), 0) == lax.broadcasted_iota(jnp.int32, (w, w), 1)).astype(BF16)
        return _dot(eye, a, NT).astype(BF16)

    def body(*refs):
        if split:
            xn_ref, xr_ref, xv_ref, g_ref, c_ref, s_ref, o_ref, ot_ref, v_ref = refs
            xn, xr = xn_ref[...], xr_ref[...]
            v_ref[...] = xv_ref[...].astype(v_ref.dtype)
        else:
            x_ref, g_ref, c_ref, s_ref, o_ref = refs
            xf = x_ref[...]
            xn, xr = xf[:, :MLA_NOPE], xf[:, MLA_NOPE:]
        ms = (jnp.sum(xn * xn, axis=-1, keepdims=True) + jnp.sum(xr * xr, axis=-1, keepdims=True)) * (1.0 / d)
        r = lax.rsqrt(ms + EPS)
        gg = g_ref[...]
        yn = xn * r * gg[:, :MLA_NOPE]
        yr = xr * r * gg[:, MLA_NOPE:]
        on = (yn * scale).astype(o_ref.dtype)
        orot = ((yr * c_ref[...] + _swap_halves(yr) * s_ref[...]) * scale).astype(o_ref.dtype)
        o_ref[:, :MLA_NOPE] = on
        o_ref[:, MLA_NOPE:] = orot
        if split:
            ot_ref[:MLA_NOPE, :] = transposed(on)
            ot_ref[MLA_NOPE:, :] = transposed(orot)

    row = pl.BlockSpec((tm, d), lambda i: (i, 0))
    vec = pl.BlockSpec((1, d), lambda i: (0, 0))
    tab = pl.BlockSpec((tm, MLA_ROPE), lambda i: (i % nt, 0))
    if not split:
        return _pcall(body, name=name, grid=(n // tm,), in_specs=[row, vec, tab, tab],
                      out_specs=row, out_shape=_sds((n, d), BF16),
                      compiler_params=_cparams("parallel"))(x, g, cos_t, sin_t)
    vals = pl.BlockSpec((tm, MLA_V), lambda i: (i, 1))
    return _pcall(body, name=name, grid=(n // tm,), in_specs=[split[0], split[1], vals, vec, tab, tab],
                  out_specs=(row, pl.BlockSpec((None, d, tm), lambda i: (i // nt, 0, i % nt)),
                             pl.BlockSpec((tm, MLA_V), lambda i: (i, 0))),
                  out_shape=(_sds((n, d), BF16), _sds((n // t, d, t), BF16), _sds((n, MLA_V), BF16)),
                  compiler_params=_cparams("parallel"))(x, pe, x, g, cos_t, sin_t)


def _mla_qk_bwd(dy, x, g, cos_t, sin_t, scale, name, tm, pe=None):
    n = x.shape[0]
    d = MLA_QK
    t = cos_t.shape[0]
    tm = min(tm, t)
    nt = t // tm
    split = _qk_parts(x, pe, tm, nt)

    def body(*refs):
        if split:
            dy_ref, xn_ref, xr_ref, g_ref, c_ref, s_ref, dxn_ref, dxr_ref, dg_ref = refs
            xn, xr = xn_ref[...], xr_ref[...]
        else:
            dy_ref, x_ref, g_ref, c_ref, s_ref, dx_ref, dg_ref = refs
            xf = x_ref[...]
            xn, xr = xf[:, :MLA_NOPE], xf[:, MLA_NOPE:]
        gg = g_ref[...]
        ms = (jnp.sum(xn * xn, axis=-1, keepdims=True) + jnp.sum(xr * xr, axis=-1, keepdims=True)) * (1.0 / d)
        r = lax.rsqrt(ms + EPS)
        xh_n, xh_r = xn * r, xr * r
        dyf = dy_ref[...] * scale
        dyr = dyf[:, MLA_NOPE:]
        dn_n = dyf[:, :MLA_NOPE]
        dn_r = dyr * c_ref[...] + _swap_halves(dyr * s_ref[...])
        dxh_n = dn_n * gg[:, :MLA_NOPE]
        dxh_r = dn_r * gg[:, MLA_NOPE:]
        mean = (jnp.sum(dxh_n * xh_n, axis=-1, keepdims=True)
                + jnp.sum(dxh_r * xh_r, axis=-1, keepdims=True)) * (1.0 / d)
        dx_n = r * (dxh_n - xh_n * mean)
        dx_r = r * (dxh_r - xh_r * mean)
        if split:
            dxn_ref[...] = dx_n
            dxr_ref[...] = dx_r
        else:
            dx_ref[:, :MLA_NOPE] = dx_n
            dx_ref[:, MLA_NOPE:] = dx_r

        @pl.when(pl.program_id(0) == 0)
        def _():
            dg_ref[...] = jnp.zeros_like(dg_ref)

        dg_ref[:, :MLA_NOPE] += jnp.sum(dn_n * xh_n, axis=0, keepdims=True)
        dg_ref[:, MLA_NOPE:] += jnp.sum(dn_r * xh_r, axis=0, keepdims=True)

    row = pl.BlockSpec((tm, d), lambda i: (i, 0))
    vec = pl.BlockSpec((1, d), lambda i: (0, 0))
    tab = pl.BlockSpec((tm, MLA_ROPE), lambda i: (i % nt, 0))
    if not split:
        return _pcall(body, name=name, grid=(n // tm,), in_specs=[row, row, vec, tab, tab],
                      out_specs=(row, vec), out_shape=(_sds((n, d), F32), _sds((1, d), F32)),
                      compiler_params=_cparams("arbitrary"))(dy, x, g, cos_t, sin_t)
    outs = (pl.BlockSpec((tm, MLA_NOPE), lambda i: (i, 0)), pl.BlockSpec((tm, MLA_ROPE), lambda i: (i, 0)), vec)
    return _pcall(body, name=name, grid=(n // tm,), in_specs=[row, split[0], split[1], vec, tab, tab],
                  out_specs=outs, out_shape=(_sds((n, MLA_NOPE), F32), _sds((n, MLA_ROPE), F32), _sds((1, d), F32)),
                  compiler_params=_cparams("arbitrary"))(dy, x, pe, g, cos_t, sin_t)


def _causal_mask(i, j, tq, tk, width):
    row = i * tq + lax.broadcasted_iota(jnp.int32, (tq, width), 0)
    col = j * tk + lax.broadcasted_iota(jnp.int32, (tq, width), 1)
    return col <= row


def _causal_steps(nq, nk, tq, tk, q_major):
    if q_major:
        groups = [[(i, j) for j in range((i * tq + tq - 1) // tk + 1)] for i in range(nq)]
        nunit = tk // tq if tk % tq == 0 else 1
    else:
        groups = [[(i, j) for i in range((j * tk) // tq, nq)] for j in range(nk)]
        nunit = tq // tk if tq % tk == 0 else 1
    it, jt, fl = [], [], []
    for g in groups:
        for n, (i, j) in enumerate(g):
            crossing = j * tk + tk - 1 > i * tq
            if q_major:
                unit = tk // nunit
                u = min(nunit, -(-(i * tq + tq - j * tk) // unit)) - 1
            else:
                unit = tq // nunit
                u = max(0, j * tk - i * tq) // unit
            it.append(i)
            jt.append(j)
            fl.append((n == 0) + 2 * (n == len(g) - 1) + 4 * crossing + 8 * (u if crossing else 0))
    return tuple(jnp.asarray(np.array(a, np.int32)) for a in (it, jt, fl)), nunit


def _by_crossing(flags, nunit, update):
    pl.when((flags & 4) == 0)(functools.partial(update, None))
    for u in range(nunit):
        pl.when(((flags & 4) != 0) & ((flags >> 3) == u))(functools.partial(update, u))


def _causal_specs(tq, tk):
    def qs(w):
        return pl.BlockSpec((None, tq, w), lambda h, s, it, jt, fl: (h, it[s], 0))

    def kv(w):
        return pl.BlockSpec((None, tk, w), lambda h, s, it, jt, fl: (h, jt[s], 0))

    return qs, kv


def _mla_fwd(q, k, v, tq, tk):
    nh, t, dq = q.shape
    dv = v.shape[2]
    tq, tk = min(tq, t), min(tk, t)
    tables, nunit = _causal_steps(t // tq, t // tk, tq, tk, True)

    def body(it, jt, fl, q_ref, k_ref, v_ref, o_ref, lse_ref, m_sc, l_sc, acc_sc):
        step = pl.program_id(1)
        i, j, flags = it[step], jt[step], fl[step]

        @pl.when((flags & 1) != 0)
        def _():
            m_sc[...] = jnp.full_like(m_sc, NEG)
            l_sc[...] = jnp.zeros_like(l_sc)
            acc_sc[...] = jnp.zeros_like(acc_sc)

        def update(units):
            wk = tk if units is None else (units + 1) * (tk // nunit)
            s = _dot(q_ref[...], k_ref[:wk, :], NT)
            if units is not None:
                s = jnp.where(_causal_mask(i, j, tq, tk, wk), s, NEG)
            m_prev = m_sc[...]
            m_new = jnp.maximum(m_prev, jnp.max(s, axis=-1, keepdims=True))
            alpha = jnp.exp(m_prev - m_new)
            p = jnp.exp(s - m_new)
            l_sc[...] = alpha * l_sc[...] + jnp.sum(p, axis=-1, keepdims=True)
            acc_sc[...] = alpha * acc_sc[...] + _dot(p.astype(BF16), v_ref[:wk, :], NN)
            m_sc[...] = m_new

        _by_crossing(flags, nunit, update)

        @pl.when((flags & 2) != 0)
        def _():
            o_ref[...] = acc_sc[...] / l_sc[...]
            lse_ref[...] = m_sc[...] + jnp.log(l_sc[...])

    qs, kv = _causal_specs(tq, tk)
    return _pcall(
        body, name="mla_attn_fwd",
        grid_spec=pltpu.PrefetchScalarGridSpec(
            num_scalar_prefetch=3, grid=(nh, tables[0].shape[0]),
            in_specs=[qs(dq), kv(dq), kv(dv)], out_specs=(qs(dv), qs(1)),
            scratch_shapes=[pltpu.VMEM((tq, 1), F32), pltpu.VMEM((tq, 1), F32), pltpu.VMEM((tq, dv), F32)]),
        out_shape=(_sds((nh, t, dv), F32), _sds((nh, t, 1), F32)),
        compiler_params=_cparams("parallel", "arbitrary"))(*tables, q, k, v)


def _mla_bwd(q, k, k_t, v, do, lse_row, dl_row, tq, tk):
    nh, t, dq = q.shape
    dv = v.shape[2]
    tq, tk = min(tq, t), min(tk, t)
    nq = t // tq
    tables, nunit = _causal_steps(nq, t // tk, tq, tk, False)

    def body(it, jt, fl, q_ref, k_ref, kt_ref, v_ref, do_ref, lse_ref, dl_ref, dk_ref, dv_ref, dq_ref, dk_sc, dv_sc):
        step = pl.program_id(1)
        i, j, flags = it[step], jt[step], fl[step]

        def update(units):
            off = 0 if units is None else units * (tq // nunit)
            qq = q_ref[off:, :]
            st = _dot(k_ref[...], qq, NT)
            if units is not None:
                key = j * tk + lax.broadcasted_iota(jnp.int32, (tk, tq - off), 0)
                qry = i * tq + off + lax.broadcasted_iota(jnp.int32, (tk, tq - off), 1)
                st = jnp.where(key <= qry, st, NEG)
            pt = jnp.exp(st - lse_ref[:, off:])
            dob = do_ref[off:, :].astype(BF16)
            dpt = _dot(v_ref[...], dob, NT)
            dst = pt * (dpt - dl_ref[:, off:])
            dsb = dst.astype(BF16)
            dv_part = _dot(pt.astype(BF16), dob, NN)
            dk_part = _dot(dsb, qq, NN)
            dq_part = _dot(kt_ref[...], dsb, NN)

            @pl.when((flags & 1) != 0)
            def _():
                dv_sc[...] = dv_part
                dk_sc[...] = dk_part

            @pl.when((flags & 1) == 0)
            def _():
                dv_sc[...] += dv_part
                dk_sc[...] += dk_part

            if off == 0:
                @pl.when(j == 0)
                def _():
                    dq_ref[i] = dq_part

                @pl.when(j != 0)
                def _():
                    dq_ref[i] += dq_part
            else:
                dq_ref[i, :, off:] += dq_part

        _by_crossing(flags, nunit, update)

        @pl.when((flags & 2) != 0)
        def _():
            dk_ref[...] = dk_sc[...]
            dv_ref[...] = dv_sc[...]

    qs, kv = _causal_specs(tq, tk)
    rowv = pl.BlockSpec((None, 1, tq), lambda h, s, it, jt, fl: (h, 0, it[s]))
    ktv = pl.BlockSpec((None, dq, tk), lambda h, s, it, jt, fl: (h, 0, jt[s]))
    whole = pl.BlockSpec((None, nq, dq, tq), lambda h, s, it, jt, fl: (h, 0, 0, 0))
    return _pcall(
        body, name="mla_attn_bwd",
        grid_spec=pltpu.PrefetchScalarGridSpec(
            num_scalar_prefetch=3, grid=(nh, tables[0].shape[0]),
            in_specs=[qs(dq), kv(dq), ktv, kv(dv), qs(dv), rowv, rowv], out_specs=(kv(dq), kv(dv), whole),
            scratch_shapes=[pltpu.VMEM((tk, dq), F32), pltpu.VMEM((tk, dv), F32)]),
        out_shape=(_sds((nh, t, dq), F32), _sds((nh, t, dv), F32), _sds((nh, nq, dq, tq), F32)),
        compiler_params=_cparams("parallel", "arbitrary"))(*tables, q, k, k_t, v, do, lse_row, dl_row)


def _ffn_down_loss(act, wd, x, target, name, tm):
    nc, t, fs = act.shape
    d = x.shape[1]
    tm = min(tm, t)
    nt = t // tm

    def body(*refs):
        a_refs, w_refs = refs[:nc], refs[nc:2 * nc]
        x_ref, t_ref, dy_ref, loss_ref, acc = refs[2 * nc:]
        i = pl.program_id(0)
        tot = _dot(a_refs[0][...], w_refs[0][...], NN)
        for c in range(1, nc):
            tot = tot + _dot(a_refs[c][...], w_refs[c][...], NN)
        err = x_ref[...] + tot * FFN_RESID - t_ref[...]
        dy_ref[...] = err * (1.0 / d)

        @pl.when(i == 0)
        def _():
            acc[...] = jnp.zeros_like(acc)

        acc[...] += jnp.sum(err * err, axis=0, keepdims=True)

        @pl.when(i == nt - 1)
        def _():
            loss_ref[0, 0] = jnp.sum(acc[...]) * (0.5 / d)

    row = pl.BlockSpec((tm, d), lambda i: (i, 0))
    a_specs = [pl.BlockSpec((None, tm, fs), lambda i, c=c: (c, i, 0)) for c in range(nc)]
    w_specs = [pl.BlockSpec((None, fs, d), lambda i, c=c: (c, 0, 0)) for c in range(nc)]
    return _pcall(
        body, name=name, grid=(nt,), in_specs=a_specs + w_specs + [row, row],
        out_specs=(row, pl.BlockSpec(memory_space=pltpu.SMEM)),
        out_shape=(_sds((t, d), F32), _sds((1, 1), F32)),
        scratch_shapes=[pltpu.VMEM((1, d), F32)],
        compiler_params=_cparams("arbitrary"))(*[act] * nc, *[wd] * nc, x, target)


def _adamw(w, g, m, v, name):
    r, c = w.shape
    tr = r
    for cand in (256, 128, 64, 32, 16, 8):
        if r % cand == 0:
            tr = cand
            break

    def body(w_ref, g_ref, m_ref, v_ref, d_ref, nm_ref, nv_ref):
        gg = g_ref[...]
        nm = ADAM_B1 * m_ref[...] + (1.0 - ADAM_B1) * gg
        nv = ADAM_B2 * v_ref[...] + (1.0 - ADAM_B2) * (gg * gg)
        m_hat = nm / (1.0 - ADAM_B1 ** ADAM_STEP)
        v_hat = nv / (1.0 - ADAM_B2 ** ADAM_STEP)
        d_ref[...] = -ADAM_LR * (m_hat / (jnp.sqrt(v_hat) + ADAM_EPS) + ADAM_WD * w_ref[...])
        nm_ref[...] = nm
        nv_ref[...] = nv

    spec = pl.BlockSpec((tr, c), lambda i: (i, 0))
    sd = _sds((r, c), F32)
    return _pcall(body, name=name, grid=(r // tr,), in_specs=[spec] * 4, out_specs=(spec,) * 3,
                  out_shape=(sd, sd, sd), compiler_params=_cparams("parallel"))(w, g, m, v)


MESH_ID = pl.DeviceIdType.MESH
HBM_SPEC = pl.BlockSpec(memory_space=pltpu.HBM)


def _place():
    return lax.axis_index("x"), lax.axis_index("y"), lax.axis_index("c")


def _other_chips(x, y):
    return [(1 - x, y), (x, 1 - y), (1 - x, 1 - y)]


def _remote(src, dst, send_sems, recv_sems, k, to):
    return pltpu.make_async_remote_copy(src_ref=src, dst_ref=dst, send_sem=send_sems.at[k], recv_sem=recv_sems.at[k],
                                        device_id=to, device_id_type=MESH_ID)


def _halves(arrays):
    for a in arrays:
        assert a.shape[-2] % 32 == 0
    return [a.shape[-2] // 2 for a in arrays]


def _gather_start(srcs, outs, halves, send_sems, recv_sems):
    x, y, c = _place()
    for a, half in enumerate(halves):
        rows = pl.ds(c * half, half)
        for k, (cx, cy) in enumerate(_other_chips(x, y)):
            _remote(srcs[a].at[rows, :], outs[a].at[2 * x + y, rows, :], send_sems, recv_sems, 3 * a + k,
                    (cx, cy, c)).start()


def _gather_wait(outs, halves, send_sems, recv_sems):
    x, y, c = _place()
    for a, half in enumerate(halves):
        for k, (cx, cy) in enumerate(_other_chips(x, y)):
            got = outs[a].at[2 * cx + cy, pl.ds(c * half, half), :]
            _remote(got, got, send_sems, recv_sems, 3 * a + k, (x, y, c)).wait()


def _forward_cores(partly):
    n = len(partly)
    halves = _halves(partly)

    def body(*refs):
        srcs, outs, send_sems, recv_sems = refs[:n], refs[n:2 * n], refs[2 * n], refs[2 * n + 1]
        x, y, c = _place()
        for a, half in enumerate(halves):
            for k, (cx, cy) in enumerate(_other_chips(x, y)):
                rows = pl.ds(c * half, half)
                _remote(srcs[a].at[2 * cx + cy, rows, :], outs[a].at[2 * cx + cy, rows, :], send_sems, recv_sems,
                        3 * a + k, (x, y, 1 - c)).start()
        for a, half in enumerate(halves):
            for k, (cx, cy) in enumerate(_other_chips(x, y)):
                mine = outs[a].at[2 * cx + cy, pl.ds(c * half, half), :]
                theirs = outs[a].at[2 * cx + cy, pl.ds((1 - c) * half, half), :]
                _remote(mine, theirs, send_sems, recv_sems, 3 * a + k, (x, y, c)).wait()

    return _pcall(
        body, name="forward_cores", in_specs=[HBM_SPEC] * n, out_specs=tuple([HBM_SPEC] * n),
        out_shape=tuple(_sds(p.shape, p.dtype) for p in partly), input_output_aliases={a: a for a in range(n)},
        scratch_shapes=[pltpu.SemaphoreType.DMA((3 * n,)), pltpu.SemaphoreType.DMA((3 * n,))],
    )(*partly)


def _gather_weights(blocks):
    n = len(blocks)
    halves = _halves(blocks)

    def body(*refs):
        srcs, outs, send_sems, recv_sems = refs[:n], refs[n:2 * n], refs[2 * n], refs[2 * n + 1]
        x, y, c = _place()
        me = 2 * x + y
        sibling = (x, y, 1 - c)
        chips = _other_chips(x, y)

        def part(a, chip, core):
            return outs[a].at[chip, pl.ds(core * halves[a], halves[a]), :]

        for a in range(n):
            mine = srcs[a].at[pl.ds(c * halves[a], halves[a]), :]
            for k, (cx, cy) in enumerate(chips):
                _remote(mine, part(a, me, c), send_sems, recv_sems, 6 * a + k, (cx, cy, c)).start()
        for k, (cx, cy) in enumerate(chips):
            for a in range(n):
                got = part(a, 2 * cx + cy, c)
                _remote(got, got, send_sems, recv_sems, 6 * a + k, (x, y, c)).wait_recv()
                _remote(got, got, send_sems, recv_sems, 6 * a + 3 + k, sibling).start()
        for k, (cx, cy) in enumerate(chips):
            for a in range(n):
                got = part(a, 2 * cx + cy, 1 - c)
                _remote(got, got, send_sems, recv_sems, 6 * a + 3 + k, (x, y, c)).wait_recv()
        for a in range(n):
            sent = part(a, me, c)
            for k in range(6):
                _remote(sent, sent, send_sems, recv_sems, 6 * a + k, (x, y, c)).wait_send()

    return _pcall(
        body, name="gather_weights", in_specs=[HBM_SPEC] * n, out_specs=tuple([HBM_SPEC] * n),
        out_shape=tuple(_sds((N_CHIPS,) + b.shape, b.dtype) for b in blocks),
        scratch_shapes=[pltpu.SemaphoreType.DMA((6 * n,)), pltpu.SemaphoreType.DMA((6 * n,))],
    )(*blocks)


def _reduce_cores(grads, tag):
    n = len(grads)

    def body(*refs):
        gs, outs, send_sems, recv_sems = refs[:n], refs[n:2 * n], refs[2 * n], refs[2 * n + 1]
        _cores_start(gs, outs, send_sems, recv_sems)
        _cores_wait(gs, outs, send_sems, recv_sems)

    return _pcall(
        body, name=f"reduce_cores_{tag}", in_specs=[HBM_SPEC] * n, out_specs=tuple([HBM_SPEC] * n),
        out_shape=_cores_shapes(grads), scratch_shapes=_cores_sems(n),
    )(*grads)


def _cores_shapes(grads):
    return tuple(_sds((N_CHIPS, h, g.shape[2]), g.dtype) for g, h in zip(grads, _halves(grads)))


def _cores_sems(n):
    return [pltpu.SemaphoreType.DMA((n,)), pltpu.SemaphoreType.DMA((n,))]


def _cores_start(gs, outs, send_sems, recv_sems):
    x, y, c = _place()
    for a, g in enumerate(gs):
        half = g.shape[1] // 2
        for j in range(N_CHIPS):
            _remote(g.at[j, pl.ds((1 - c) * half, half), :], outs[a].at[j], send_sems, recv_sems, a,
                    (x, y, 1 - c)).start()


def _cores_wait(gs, outs, send_sems, recv_sems):
    x, y, c = _place()
    for a, g in enumerate(gs):
        half = g.shape[1] // 2
        _remote(g.at[:, pl.ds((1 - c) * half, half), :], outs[a], send_sems, recv_sems, a, (x, y, c)).wait()


def _scatter_shapes(parts):
    return tuple(_sds((3,) + p.shape[1:], p.dtype) for p in parts)


def _scatter_sems(n):
    return [pltpu.SemaphoreType.DMA((3 * n,)), pltpu.SemaphoreType.DMA((3 * n,))]


def _scatter_start(ps, outs, send_sems, recv_sems):
    x, y, c = _place()
    for a in range(len(ps)):
        for k, (cx, cy) in enumerate(_other_chips(x, y)):
            _remote(ps[a].at[2 * cx + cy], outs[a].at[k], send_sems, recv_sems, 3 * a + k, (cx, cy, c)).start()


def _scatter_wait(ps, outs, send_sems, recv_sems):
    x, y, c = _place()
    for a in range(len(ps)):
        for k in range(3):
            _remote(ps[a].at[k], outs[a].at[k], send_sems, recv_sems, 3 * a + k, (x, y, c)).wait()


def _gather_shapes(blocks):
    return tuple(_sds((N_CHIPS,) + b.shape, b.dtype) for b in blocks)


def _gather_sems(n):
    return [pltpu.SemaphoreType.DMA((3 * n,)), pltpu.SemaphoreType.DMA((3 * n,))]


EXCHANGES = {"chips": (_scatter_start, _scatter_wait, _scatter_shapes, _scatter_sems),
             "cores": (_cores_start, _cores_wait, _cores_shapes, _cores_sems),
             "gather": (lambda srcs, outs, s, r: _gather_start(srcs, outs, _halves(srcs), s, r),
                        lambda srcs, outs, s, r: _gather_wait(outs, _halves(srcs), s, r),
                        _gather_shapes, _gather_sems)}


def _sum_partials(received, parts, place):
    n = len(parts)
    steps = 2
    tiles = [p.shape[1] // steps for p in parts]

    def body(place_ref, *refs):
        rs, ps, outs = refs[:n], refs[n:2 * n], refs[2 * n:]
        for a in range(n):
            tot = ps[a][...].astype(F32)
            for k in range(3):
                tot = tot + rs[a][k].astype(F32)
            outs[a][...] = tot

    cols = [p.shape[2] for p in parts]
    return _pcall(
        body, name="sum_chip_partials",
        grid_spec=pltpu.PrefetchScalarGridSpec(
            num_scalar_prefetch=1, grid=(steps,),
            in_specs=[pl.BlockSpec((3, tm, w), lambda i, pc: (0, i, 0)) for tm, w in zip(tiles, cols)]
            + [pl.BlockSpec((None, tm, w), lambda i, pc: (pc[0], i, 0)) for tm, w in zip(tiles, cols)],
            out_specs=tuple(pl.BlockSpec((tm, w), lambda i, pc: (pc[1] * steps + i, 0)) for tm, w in zip(tiles, cols))),
        out_shape=tuple(_sds((2 * p.shape[1], p.shape[2]), F32) for p in parts),
        compiler_params=_cparams("parallel"))(place, *received, *parts)


def _share_cores(blocks):
    n = len(blocks)
    halves = _halves(blocks)

    def body(*refs):
        srcs, outs, send_sems, recv_sems = refs[:n], refs[n:2 * n], refs[2 * n], refs[2 * n + 1]
        x, y, c = _place()
        for a in range(n):
            piece = pl.ds(c * halves[a], halves[a])
            _remote(srcs[a].at[piece, :], outs[a].at[piece, :], send_sems, recv_sems, a, (x, y, 1 - c)).start()
        for a in range(n):
            mine = outs[a].at[pl.ds(c * halves[a], halves[a]), :]
            theirs = outs[a].at[pl.ds((1 - c) * halves[a], halves[a]), :]
            _remote(mine, theirs, send_sems, recv_sems, a, (x, y, c)).wait()

    return _pcall(
        body, name="share_cores", in_specs=[HBM_SPEC] * n, out_specs=tuple([HBM_SPEC] * n),
        out_shape=tuple(_sds(b.shape, b.dtype) for b in blocks), input_output_aliases={a: a for a in range(n)},
        scratch_shapes=[pltpu.SemaphoreType.DMA((n,)), pltpu.SemaphoreType.DMA((n,))],
    )(*blocks)


def _sum_blocks(stacked, name, tm):
    n, rows, lanes = stacked.shape
    tm = min(tm, rows)

    def body(s_ref, o_ref):
        tot = s_ref[n - 1].astype(F32)
        for k in range(n - 1):
            tot = tot + s_ref[k].astype(F32)
        o_ref[...] = tot

    return _pcall(body, name=name, grid=(rows // tm,),
                  in_specs=[pl.BlockSpec((n, tm, lanes), lambda i: (0, i, 0))],
                  out_specs=pl.BlockSpec((tm, lanes), lambda i: (i, 0)), out_shape=_sds((rows, lanes), F32),
                  compiler_params=_cparams("parallel"))(stacked)


def _add_halves(grads, theirs, core, tag):
    n = len(grads)
    steps = 2
    tiles = [t.shape[1] // steps for t in theirs]
    cols = [t.shape[2] for t in theirs]

    def body(c_ref, *refs):
        gs, ts, outs = refs[:n], refs[n:2 * n], refs[2 * n:]
        for a in range(n):
            outs[a][...] = (gs[a][...] + ts[a][...]).astype(BF16)

    own = [pl.BlockSpec((None, tm, w), lambda k, i, c: (k, c[0] * steps + i, 0)) for tm, w in zip(tiles, cols)]
    same = [pl.BlockSpec((None, tm, w), lambda k, i, c: (k, i, 0)) for tm, w in zip(tiles, cols)]
    return _pcall(
        body, name=f"add_core_halves_{tag}",
        grid_spec=pltpu.PrefetchScalarGridSpec(
            num_scalar_prefetch=1, grid=(N_CHIPS, steps), in_specs=own + same, out_specs=tuple(same)),
        out_shape=tuple(_sds(t.shape, BF16) for t in theirs),
        compiler_params=_cparams("parallel", "parallel"))(core, *grads, *theirs)


def _allreduce_small(part):
    rows, lanes = part.shape
    ndev = 8

    def body(src, tot, buf, send_sems, recv_sems):
        x, y, c = _place()
        me = 4 * x + 2 * y + c
        buf[me] = src[...]
        sends = []
        for k in range(1, ndev):
            peer = (x ^ (k >> 2), y ^ ((k >> 1) & 1), c ^ (k & 1))
            cp = _remote(src, buf.at[me], send_sems, recv_sems, k - 1, peer)
            cp.start()
            sends.append(cp)
        for k in range(1, ndev):
            theirs = buf.at[me ^ k]
            _remote(theirs, theirs, send_sems, recv_sems, k - 1, (x, y, c)).wait_recv()
        for cp in sends:
            cp.wait_send()
        acc = buf[0]
        for d in range(1, ndev):
            acc = acc + buf[d]
        tot[...] = acc

    vm = pl.BlockSpec(memory_space=pltpu.VMEM)
    return _pcall(
        body, name="allreduce_small", in_specs=[vm], out_specs=vm, out_shape=_sds((rows, lanes), F32),
        scratch_shapes=[pltpu.VMEM((ndev, rows, lanes), F32), pltpu.SemaphoreType.DMA((ndev - 1,)),
                        pltpu.SemaphoreType.DMA((ndev - 1,))],
    )(part)


SMALL_USED = sum(r for _, _, r in SMALL)


def _pack_small(vals, loss):
    parts = []
    for name, shape, r in SMALL:
        flat = vals[name].reshape(-1).astype(F32)
        parts.append(jnp.pad(flat, (0, r * LANES - flat.shape[0])).reshape(r, LANES))
    parts.append(jnp.pad(loss.astype(F32), ((0, SMALL_ROWS - SMALL_USED - 1), (0, LANES - 1))))
    return jnp.concatenate(parts, axis=0)


def _unpack_small(packed):
    out, off = {}, 0
    for name, shape, r in SMALL:
        n = int(np.prod(shape))
        out[name] = packed[off:off + r].reshape(-1)[:n].reshape(shape)
        off += r
    return out


def _heads_major(a, nh):
    t = a.shape[0]
    return a.reshape(t, nh, a.shape[1] // nh).transpose(1, 0, 2)


def _tokens_major(a):
    nh, t, w = a.shape
    return a.transpose(1, 0, 2).reshape(t, nh * w)


LATE = ("ffn1_w_gate", "ffn1_w_up", "ffn1_w_down")
EARLY = tuple(name for name, _ in BIG if name not in LATE)


def _local_step(x, target, small, wfull, exchanges=None, later_weights=None):
    t = x.shape[0]
    nh, hd = DIL_HEADS, DIL_HD
    grads_s, grads_b = {}, {}

    x1, ffn1_saved, partly = _ffn_fwd(x, small["ffn1_norm"], wfull["ffn1_w_gate"], wfull["ffn1_w_up"],
                                      wfull["ffn1_w_down"], "ffn1", later_weights[0] if later_weights else ())
    if later_weights:
        wfull = {**wfull, **later_weights[1](partly)}
    w_in = wfull["w_in"].transpose(1, 0, 2).reshape(D_MODEL, -1)
    w_out = wfull["w_out"].reshape(D_MODEL, D_MODEL)
    w_qb, w_kvb = wfull["mla_w_q_b"], wfull["mla_w_kv_b"]
    hm = _rms_fwd(x1, small["mix_norm"], BF16, "mix_norm", 512)
    proj = _mm_simple("in_proj", hm, w_in, NN, F32, tm=1024)
    cq, ckv, k_pe = proj[:, 1536:1792], proj[:, 1792:1920], proj[:, 1920:1984]

    gq, gk = jnp.tile(small["dil_q_norm"], (1, nh)), jnp.tile(small["dil_k_norm"], (1, nh))
    qn = _head_norm_fwd(proj, 0, gq, "dil_q_norm", 512)
    kn = _head_norm_fwd(proj, 1, gk, "dil_k_norm", 512)
    v_d = _head_norm_fwd(proj, 2, None, "dil_v_views", 512)
    bias = _bias_tiles(small["rel_bias"]).reshape(3, nh // 2, 2 * QB, QB + DIL_W)
    outs, lses = [], []
    for b, dil in enumerate(DIL_DILATIONS):
        o_b, lse_b = _dil_fwd(qn[b], kn[b], v_d[b], bias[b], dil, f"dil_fwd_{dil}")
        outs.append(o_b)
        lses.append(lse_b)
    o_dil, lse_tot, od = _dil_merge(outs, lses, small["out_norm_dil"], 512)

    mh = MLA_HEADS
    cos_t, sin_t = _rope_tables(t)
    cqn = _rms_fwd(cq, small["mla_q_a_norm"], BF16, "mla_q_a_norm", 512)
    ckvn = _rms_fwd(ckv, small["mla_kv_a_norm"], BF16, "mla_kv_a_norm", 512)
    tm = min(512, t)

    th = min(2048, t)

    def head_proj(name, a, w, width):
        k = a.shape[1]
        return _mm(name, (mh, t // th, 1),
                   [(a, pl.BlockSpec((th, k), lambda h, i, r: (i, 0)), w, pl.BlockSpec((None, k, width), lambda h, i, r: (h, 0, 0)))],
                   NN, _sds((mh, t, width), F32), pl.BlockSpec((None, th, width), lambda h, i, r: (h, i, 0)), (th, width))

    q_raw = head_proj("mla_q_proj", cqn, w_qb, MLA_QK)
    kv_raw = head_proj("mla_kv_proj", ckvn, w_kvb, MLA_NOPE + MLA_V)
    q_raw2, kv_raw2 = q_raw.reshape(mh * t, MLA_QK), kv_raw.reshape(mh * t, MLA_NOPE + MLA_V)
    q_scale = MLA_QK ** -0.5
    q_m = _mla_qk_fwd(q_raw2, small["mla_q_norm"], cos_t, sin_t, q_scale, "mla_q_rope", 2048).reshape(mh, t, MLA_QK)
    k_m, k_t, v_m = _mla_qk_fwd(kv_raw2, small["mla_k_norm"], cos_t, sin_t, 1.0, "mla_k_rope", 2048, pe=k_pe)
    k_m, v_m = k_m.reshape(mh, t, MLA_QK), v_m.reshape(mh, t, MLA_V)
    o_mla_h, lse_m = _mla_fwd(q_m, k_m, v_m, 512, 4096)
    o_mla = _tokens_major(o_mla_h)

    om = _rms_fwd(o_mla, small["out_norm_mla"], BF16, "out_norm_mla", 512)
    half_w = DIL_WIDTH
    row = pl.BlockSpec((tm, D_MODEL), lambda i, j, r: (i, 0))
    act_spec = pl.BlockSpec((tm, half_w), lambda i, j, r: (i, 0))
    x2 = _mm("out_proj", (t // tm, 1, 1),
             [(od, act_spec, w_out, pl.BlockSpec((half_w, D_MODEL), lambda i, j, r: (0, 0))),
              (om, act_spec, w_out, pl.BlockSpec((half_w, D_MODEL), lambda i, j, r: (1, 0)))],
             NN, _sds((t, D_MODEL), F32), row, (tm, D_MODEL), res=(x1, row))
    (dy, loss), ffn2_saved, _ = _ffn_fwd(x2, small["ffn2_norm"], wfull["ffn2_w_gate"], wfull["ffn2_w_up"],
                                         wfull["ffn2_w_down"], "ffn2", target=target)

    dx2, grads_s["ffn2_norm"], grads_b["ffn2_w_gate"], grads_b["ffn2_w_up"], grads_b["ffn2_w_down"], _, _ = _ffn_bwd(
        dy, x2, small["ffn2_norm"], wfull["ffn2_w_gate"], wfull["ffn2_w_up"], wfull["ffn2_w_down"], ffn2_saved, "ffn2")

    d_ocat = _mm_simple("out_proj_dx", dx2, w_out, NT, F32, tm=1024)
    dw_out_t = _mm_tn_multi("out_proj_dw", dx2, [od, om], 2048)
    grads_b["w_out"] = jnp.concatenate([w.T for w in dw_out_t], axis=0).reshape(N_CHIPS, D_MODEL // N_CHIPS, D_MODEL)
    do_dil, grads_s["out_norm_dil"] = _rms_bwd([d_ocat[:, :half_w]], o_dil, small["out_norm_dil"], None, "out_norm_dil_bwd", 512)
    do_mla, grads_s["out_norm_mla"] = _rms_bwd([d_ocat[:, half_w:]], o_mla, small["out_norm_mla"], None, "out_norm_mla_bwd", 512)

    do_m = _heads_major(do_mla, mh)
    dl_m = _rowdot(do_m.reshape(mh * t, MLA_V), o_mla_h.reshape(mh * t, MLA_V), "mla_delta", 2048).reshape(mh, t, 1)
    dk_m, dv_m, dq_t = _mla_bwd(q_m, k_m, k_t, v_m, do_m, lse_m.reshape(mh, 1, t),
                                dl_m.reshape(mh, 1, t), 2048, 512)
    dq_m = dq_t.transpose(0, 1, 3, 2).reshape(mh, t, MLA_QK)
    dq_raw, grads_s["mla_q_norm"] = _mla_qk_bwd(dq_m.reshape(mh * t, MLA_QK), q_raw2, small["mla_q_norm"],
                                                 cos_t, sin_t, q_scale, "mla_q_rope_bwd", 2048)
    dk_nope, dk_pe_h, grads_s["mla_k_norm"] = _mla_qk_bwd(dk_m.reshape(mh * t, MLA_QK), kv_raw2, small["mla_k_norm"],
                                                          cos_t, sin_t, 1.0, "mla_k_rope_bwd", 2048, pe=k_pe)
    dq_raw = dq_raw.reshape(mh, t, MLA_QK)
    dk_nope = dk_nope.reshape(mh, t, MLA_NOPE)

    def head_proj_dx(name, d, w):
        width, k = d.shape[2], w.shape[1]
        pairs = [(d, pl.BlockSpec((None, th, width), lambda i, j, r, h=h: (h, i, 0)),
                  w, pl.BlockSpec((None, k, width), lambda i, j, r, h=h: (h, 0, 0))) for h in range(mh)]
        return _mm(name, (t // th, 1, 1), pairs, NT, _sds((t, k), F32),
                   pl.BlockSpec((th, k), lambda i, j, r: (i, 0)), (th, k))

    def head_proj_dw(name, a, d):
        width, k = d.shape[2], a.shape[1]
        return _mm(name, (mh, 1, t // th),
                   [(a, pl.BlockSpec((th, k), lambda h, j, r: (r, 0)), d, pl.BlockSpec((None, th, width), lambda h, j, r: (h, r, 0)))],
                   TN, _sds((mh, k, width), F32), pl.BlockSpec((None, k, width), lambda h, j, r: (h, 0, 0)), (k, width))

    d_cqn = head_proj_dx("mla_q_proj_dx", dq_raw, w_qb)
    kv_pairs = []
    for h in range(mh):
        for part, d_part in enumerate((dk_nope, dv_m)):
            kv_pairs.append((d_part, pl.BlockSpec((None, th, MLA_NOPE), lambda i, j, r, h=h: (h, i, 0)),
                             w_kvb, pl.BlockSpec((None, MLA_KV_RANK, MLA_NOPE), lambda i, j, r, h=h, part=part: (h, 0, part))))
    d_ckvn = _mm("mla_kv_proj_dx", (t // th, 1, 1), kv_pairs, NT, _sds((t, MLA_KV_RANK), F32),
                 pl.BlockSpec((th, MLA_KV_RANK), lambda i, j, r: (i, 0)), (th, MLA_KV_RANK))
    grads_b["mla_w_q_b"] = head_proj_dw("mla_q_proj_dw", cqn, dq_raw)
    grads_b["mla_w_kv_b"] = jnp.concatenate([head_proj_dw("mla_k_proj_dw", ckvn, dk_nope),
                                             head_proj_dw("mla_v_proj_dw", ckvn, dv_m)], axis=2)
    d_cq, grads_s["mla_q_a_norm"] = _rms_bwd([d_cqn], cq, small["mla_q_a_norm"], None, "mla_q_a_norm_bwd", 512)
    d_ckv, grads_s["mla_kv_a_norm"] = _rms_bwd([d_ckvn], ckv, small["mla_kv_a_norm"], None, "mla_kv_a_norm_bwd", 512)
    d_kpe = _sum_blocks(dk_pe_h.reshape(mh, t * MLA_ROPE // LANES, LANES), "mla_kpe_sum", 1024).reshape(t, MLA_ROPE)

    stats, do_db = _dil_stats(do_dil, o_dil, lse_tot, 512)
    dqs, dks, dvs, dtiles = [], [], [], []
    for b, dil in enumerate(DIL_DILATIONS):
        dq_b, dk_b, dv_b, db_b = _dil_bwd(qn[b], kn[b], v_d[b], do_db[b], stats[b], bias[b], dil, f"dil_bwd_{dil}")
        dqs.append(dq_b)
        dks.append(dk_b)
        dvs.append(dv_b)
        dtiles.append(db_b)
    grads_s["rel_bias"] = _bias_grad(jnp.stack(dtiles).reshape(3, nh, QB, QB + DIL_W))
    dq_a, dgq = _head_norm_bwd(dqs, proj, 0, gq, "dil_q_norm_bwd", 512)
    dk_a, dgk = _head_norm_bwd(dks, proj, 1, gk, "dil_k_norm_bwd", 512)
    grads_s["dil_q_norm"], grads_s["dil_k_norm"] = dgq[:, :hd], dgk[:, :hd]
    dv_a = _sum_branches(dvs, "dil_dv_sum", 512)

    dparts = [dq_a, dk_a, dv_a, d_cq, d_ckv, d_kpe]
    t2 = min(512, t)
    pairs, lo = [], 0
    for dpart in dparts:
        width = dpart.shape[1]
        w_part = w_in[:, lo:lo + width]
        pairs.append((dpart, pl.BlockSpec((t2, width), lambda i, j, r: (i, 0)),
                      w_part, pl.BlockSpec((D_MODEL, width), lambda i, j, r: (0, 0))))
        lo += width
    dw_in = jnp.concatenate(_mm_tn_multi("in_proj_dw", hm, dparts, 1024), axis=1)
    grads_b["w_in"] = dw_in.reshape(D_MODEL, N_CHIPS, -1).transpose(1, 0, 2)
    early = tuple(grads_b[n] for n in EARLY) if exchanges else ()
    row2 = pl.BlockSpec((t2, D_MODEL), lambda i, j, r: (i, 0))
    res = _mm("in_proj_dx", (t // t2, 1, 1), pairs, NT, _sds((t, D_MODEL), F32), row2, (t2, D_MODEL),
              res=(dx2, row2), norm=(x1, small["mix_norm"]), outgoing=early, exchange="cores")
    dx1, grads_s["mix_norm"] = res[0], res[1]
    outgoing = exchanges[0](early, res[2]) if exchanges else ()
    dx, grads_s["ffn1_norm"], grads_b["ffn1_w_gate"], grads_b["ffn1_w_up"], grads_b["ffn1_w_down"], arrived, late = _ffn_bwd(
        dx1, x, small["ffn1_norm"], wfull["ffn1_w_gate"], wfull["ffn1_w_up"], wfull["ffn1_w_down"], ffn1_saved, "ffn1",
        outgoing, exchanges[1] if exchanges else None)
    return loss, dx, grads_s, grads_b, (tuple(outgoing), arrived), late


def kernel(x, ffn1_norm, ffn1_w_gate, ffn1_w_up, ffn1_w_down, mix_norm, w_in, dil_q_norm, dil_k_norm, rel_bias, mla_q_a_norm, mla_w_q_b, mla_kv_a_norm, mla_w_kv_b, mla_q_norm, mla_k_norm, out_norm_dil, out_norm_mla, w_out, ffn2_norm, ffn2_w_gate, ffn2_w_up, ffn2_w_down, loss_target, m_ffn1_norm, m_ffn1_w_gate, m_ffn1_w_up, m_ffn1_w_down, m_mix_norm, m_w_in, m_dil_q_norm, m_dil_k_norm, m_rel_bias, m_mla_q_a_norm, m_mla_w_q_b, m_mla_kv_a_norm, m_mla_w_kv_b, m_mla_q_norm, m_mla_k_norm, m_out_norm_dil, m_out_norm_mla, m_w_out, m_ffn2_norm, m_ffn2_w_gate, m_ffn2_w_up, m_ffn2_w_down, v_ffn1_norm, v_ffn1_w_gate, v_ffn1_w_up, v_ffn1_w_down, v_mix_norm, v_w_in, v_dil_q_norm, v_dil_k_norm, v_rel_bias, v_mla_q_a_norm, v_mla_w_q_b, v_mla_kv_a_norm, v_mla_w_kv_b, v_mla_q_norm, v_mla_k_norm, v_out_norm_dil, v_out_norm_mla, v_w_out, v_ffn2_norm, v_ffn2_w_gate, v_ffn2_w_up, v_ffn2_w_down):
    given = dict(locals())
    big_names = [name for name, _ in BIG]
    small_names = [name for name, _, _ in SMALL]

    chip = (2 * lax.axis_index("x") + lax.axis_index("y")).astype(jnp.int32)
    core = lax.axis_index("c").astype(jnp.int32)
    mine = {n: given[n].astype(BF16) for n in big_names}

    def with_own(names, arrays):
        return {n: lax.dynamic_update_slice(a, mine[n], (chip, 0, 0)) for n, a in zip(names, arrays)}

    wfirst = with_own(LATE, _gather_weights([mine[n][0] for n in LATE]))
    later_weights = ([mine[n][0] for n in EARLY], lambda partly: with_own(EARLY, _forward_cores(partly)))
    small = {n: given[n] for n in small_names}

    def early_partials(partial, theirs):
        return _add_halves(partial, theirs, core.reshape(1), "early")

    def late_partials(partial):
        return _add_halves(partial, _reduce_cores(partial, "late"), core.reshape(1), "late")

    exchanges = (early_partials, late_partials)
    loss, dx, grads_s, grads_b, (early_part, early_got), (late_part, late_got) = _local_step(
        x[0], loss_target[0], small, wfirst, exchanges, later_weights)
    reduced = _sum_partials(tuple(late_got) + tuple(early_got), tuple(late_part) + tuple(early_part),
                            jnp.stack([chip, core]))
    g_big = dict(zip(LATE + EARLY, _share_cores(reduced)))
    summed = _allreduce_small(_pack_small(grads_s, loss))
    g_small = _unpack_small(summed)
    loss = summed[SMALL_USED, 0]

    grad, delta, new_m, new_v = {}, {}, {}, {}
    for name, shape in BIG:
        g2 = g_big[name]
        d_, m_, v_ = _adamw(given[name].reshape(shape), g2, given["m_" + name].reshape(shape),
                            given["v_" + name].reshape(shape), f"adamw_{name}")
        full = given[name].shape
        grad[name], delta[name], new_m[name], new_v[name] = (a.reshape(full) for a in (g2, d_, m_, v_))
    for name in small_names:
        grad[name] = g_small[name]
        delta[name], new_m[name], new_v[name] = _adamw(given[name], g_small[name], given["m_" + name],
                                                       given["v_" + name], f"adamw_{name}")

    return (loss, dx[None], *[grad[n] for n in WEIGHTS], *[delta[n] for n in WEIGHTS],
            *[new_m[n] for n in WEIGHTS], *[new_v[n] for n in WEIGHTS])
```

```python
import functools

import numpy as np
import jax
import jax.numpy as jnp
from jax import lax
from jax.experimental import pallas as pl
from jax.experimental.pallas import tpu as pltpu

F32 = jnp.float32
BF16 = jnp.bfloat16

D_MODEL = 1024
D_FF = 2816
N_CHIPS = 4
DIL_HEADS = 8
DIL_HD = 64
DIL_WIDTH = 512
DIL_DILATIONS = (1, 4, 16)
DIL_W = 128
QB = 128
MLA_HEADS = 4
MLA_NOPE = 128
MLA_ROPE = 64
MLA_QK = 192
MLA_V = 128
MLA_Q_RANK = 256
MLA_KV_RANK = 128
ROPE_BASE = 10000.0
REL_BUCKETS = 32
REL_MAX_DIST = 2048
FFN_RESID = 0.5
EPS = 1e-6
NEG = -1e30
LANES = 128

ADAM_LR = 0.001
ADAM_B1 = 0.9
ADAM_B2 = 0.999
ADAM_EPS = 1e-08
ADAM_WD = 0.01
ADAM_STEP = 10

NT = (((1,), (1,)), ((), ()))
NN = (((1,), (0,)), ((), ()))
TN = (((0,), (0,)), ((), ()))

BIG = (
    ("ffn1_w_gate", (D_MODEL, D_FF // N_CHIPS)),
    ("ffn1_w_up", (D_MODEL, D_FF // N_CHIPS)),
    ("ffn1_w_down", (D_FF // N_CHIPS, D_MODEL)),
    ("w_in", (D_MODEL, 1984 // N_CHIPS)),
    ("mla_w_q_b", (MLA_Q_RANK, MLA_QK)),
    ("mla_w_kv_b", (MLA_KV_RANK, MLA_NOPE + MLA_V)),
    ("w_out", (D_MODEL // N_CHIPS, D_MODEL)),
    ("ffn2_w_gate", (D_MODEL, D_FF // N_CHIPS)),
    ("ffn2_w_up", (D_MODEL, D_FF // N_CHIPS)),
    ("ffn2_w_down", (D_FF // N_CHIPS, D_MODEL)),
)
SMALL = (
    ("ffn1_norm", (1, 1024), 8), ("mix_norm", (1, 1024), 8), ("dil_q_norm", (1, 64), 1),
    ("dil_k_norm", (1, 64), 1), ("rel_bias", (8, 32), 2), ("mla_q_a_norm", (1, 256), 2),
    ("mla_kv_a_norm", (1, 128), 1), ("mla_q_norm", (1, 192), 2), ("mla_k_norm", (1, 192), 2),
    ("out_norm_dil", (1, 512), 4), ("out_norm_mla", (1, 512), 4), ("ffn2_norm", (1, 1024), 8),
)
SMALL_ROWS = 48
WEIGHTS = ("ffn1_norm", "ffn1_w_gate", "ffn1_w_up", "ffn1_w_down", "mix_norm", "w_in", "dil_q_norm",
           "dil_k_norm", "rel_bias", "mla_q_a_norm", "mla_w_q_b", "mla_kv_a_norm", "mla_w_kv_b",
           "mla_q_norm", "mla_k_norm", "out_norm_dil", "out_norm_mla", "w_out", "ffn2_norm",
           "ffn2_w_gate", "ffn2_w_up", "ffn2_w_down")


def _pcall(body, **kw):
    return pl.pallas_call(body, **kw)


def _cparams(*sem):
    return pltpu.CompilerParams(dimension_semantics=sem)


def _sds(shape, dtype):
    return jax.ShapeDtypeStruct(shape, dtype)


def _dot(a, b, dn):
    return lax.dot_general(a, b, dn, preferred_element_type=F32)


def _rms_fwd(x, g, out_dtype, name, tm):
    n, d = x.shape
    tm = min(tm, n)

    def body(x_ref, g_ref, o_ref):
        xf = x_ref[...].astype(F32)
        r = lax.rsqrt(jnp.mean(xf * xf, axis=-1, keepdims=True) + EPS)
        o_ref[...] = (xf * r * g_ref[...]).astype(o_ref.dtype)

    return _pcall(
        body, name=name, grid=(n // tm,),
        in_specs=[pl.BlockSpec((tm, d), lambda i: (i, 0)), pl.BlockSpec((1, d), lambda i: (0, 0))],
        out_specs=pl.BlockSpec((tm, d), lambda i: (i, 0)),
        out_shape=_sds((n, d), out_dtype), compiler_params=_cparams("parallel"))(x, g)


def _rms_bwd(dys, x, g, res, name, tm):
    n, d = x.shape
    tm = min(tm, n)
    nd = len(dys)
    has_res = res is not None

    def body(*refs):
        dy_refs = refs[:nd]
        x_ref, g_ref = refs[nd], refs[nd + 1]
        res_ref = refs[nd + 2] if has_res else None
        dx_ref, dg_ref = refs[-2], refs[-1]
        dy = dy_refs[0][...].astype(F32)
        for r_ in dy_refs[1:]:
            dy = dy + r_[...].astype(F32)
        xf = x_ref[...].astype(F32)
        r = lax.rsqrt(jnp.mean(xf * xf, axis=-1, keepdims=True) + EPS)
        xh = xf * r
        dxh = dy * g_ref[...]
        dx = r * (dxh - xh * jnp.mean(dxh * xh, axis=-1, keepdims=True))
        if has_res:
            dx = dx + res_ref[...]
        dx_ref[...] = dx

        @pl.when(pl.program_id(0) == 0)
        def _():
            dg_ref[...] = jnp.zeros_like(dg_ref)

        dg_ref[...] += jnp.sum(dy * xh, axis=0, keepdims=True)

    row = pl.BlockSpec((tm, d), lambda i: (i, 0))
    vec = pl.BlockSpec((1, d), lambda i: (0, 0))
    ins = list(dys) + [x, g] + ([res] if has_res else [])
    return _pcall(
        body, name=name, grid=(n // tm,),
        in_specs=[row] * nd + [row, vec] + ([row] if has_res else []),
        out_specs=(row, vec),
        out_shape=(_sds((n, d), F32), _sds((1, d), F32)),
        compiler_params=_cparams("arbitrary"))(*ins)


def _mm(name, grid, pairs, dn, out_shape, out_spec, acc_shape, res=None, scale=1.0, outgoing=(), norm=None,
        exchange="chips"):
    npairs = len(pairs)
    nred = grid[2]
    has_res = res is not None
    has_norm = norm is not None
    no = len(outgoing)
    ex_start, ex_wait, ex_shapes, ex_sems = EXCHANGES[exchange]

    def body(*refs):
        ab = refs[:2 * npairs]
        res_ref = refs[2 * npairs] if has_res else None
        nin = 2 * npairs + int(has_res) + 2 * int(has_norm)
        if has_norm:
            x_ref, g_ref = refs[nin - 2:nin]
        first_out = nin + no
        sent = refs[nin:first_out]
        o_ref = refs[first_out]
        nout = 1 + int(has_norm)
        dg_ref = refs[first_out + 1] if has_norm else None
        arrived = refs[first_out + nout:first_out + nout + no]
        acc_ref = refs[first_out + nout + no] if nred > 1 else None
        if no:
            send_sems, recv_sems = refs[-2:]
            ids = [pl.program_id(n) for n in range(3)]

            @pl.when((ids[0] == 0) & (ids[1] == 0) & (ids[2] == 0))
            def _():
                ex_start(sent, arrived, send_sems, recv_sems)

        tot = None
        for p in range(npairs):
            d = _dot(ab[2 * p][...].astype(BF16), ab[2 * p + 1][...].astype(BF16), dn)
            tot = d if tot is None else tot + d

        def finish(v):
            if scale != 1.0:
                v = v * scale
            if has_norm:
                xf = x_ref[...]
                r = lax.rsqrt(jnp.mean(xf * xf, axis=-1, keepdims=True) + EPS)
                xh = xf * r
                dxh = v * g_ref[...]

                @pl.when(pl.program_id(0) == 0)
                def _():
                    dg_ref[...] = jnp.zeros_like(dg_ref)

                dg_ref[...] += jnp.sum(v * xh, axis=0, keepdims=True)
                v = r * (dxh - xh * jnp.mean(dxh * xh, axis=-1, keepdims=True))
            if has_res:
                v = res_ref[...] + v
            o_ref[...] = v.astype(o_ref.dtype)

        if nred == 1:
            finish(tot)
        else:
            r = pl.program_id(2)

            @pl.when(r == 0)
            def _():
                acc_ref[...] = tot

            @pl.when(r > 0)
            def _():
                acc_ref[...] += tot

            @pl.when(r == nred - 1)
            def _():
                finish(acc_ref[...])

        if no:
            @pl.when((ids[0] == grid[0] - 1) & (ids[1] == grid[1] - 1) & (ids[2] == nred - 1))
            def _():
                ex_wait(sent, arrived, send_sems, recv_sems)

    ins, specs = [], []
    for a, a_spec, b, b_spec in pairs:
        ins += [a, b]
        specs += [a_spec, b_spec]
    if has_res:
        ins.append(res[0])
        specs.append(res[1])
    scratch = [pltpu.VMEM(acc_shape, F32)] if nred > 1 else []
    if not no and not has_norm:
        return _pcall(
            body, name=name, grid=grid, in_specs=specs, out_specs=out_spec, out_shape=out_shape,
            scratch_shapes=scratch, compiler_params=_cparams("parallel", "parallel", "arbitrary"))(*ins)
    out_specs, out_shapes = (out_spec,), (out_shape,)
    if has_norm:
        assert grid[1] == 1
        d = norm[1].shape[1]
        ins += [norm[0], norm[1]]
        specs += [out_spec, pl.BlockSpec((1, d), lambda i, j, r: (0, 0))]
        out_specs += (pl.BlockSpec((1, d), lambda i, j, r: (0, 0)),)
        out_shapes += (_sds((1, d), F32),)
    hbm = pl.BlockSpec(memory_space=pltpu.HBM)
    res_ = tuple(_pcall(
        body, name=name, grid=grid, in_specs=specs + [hbm] * no, out_specs=out_specs + (hbm,) * no,
        out_shape=out_shapes + ex_shapes(outgoing),
        scratch_shapes=scratch + (ex_sems(no) if no else []),
        compiler_params=_cparams("arbitrary", "arbitrary", "arbitrary"))(*ins, *outgoing))
    nout = len(out_shapes)
    return res_[:nout] + ((res_[nout:],) if no else ())


def _ffn_up(h, wg, wu, name, tm, incoming=()):
    t, d = h.shape
    nc, _, fs = wg.shape
    tm = min(tm, t)
    nt = t // tm
    ni = len(incoming)
    halves = _halves(incoming)

    def body(*refs):
        h_ref, wg_ref, wu_ref = refs[:3]
        srcs = refs[3:3 + ni]
        g_ref, u_ref, a_ref = refs[3 + ni:6 + ni]
        outs = refs[6 + ni:6 + 2 * ni]
        if ni:
            send_sems, recv_sems = refs[6 + 2 * ni:]
            c, i = pl.program_id(0), pl.program_id(1)

            @pl.when((c == 0) & (i == 0))
            def _():
                _gather_start(srcs, outs, halves, send_sems, recv_sems)

        hh = h_ref[...]
        gate = _dot(hh, wg_ref[...], NN)
        up = _dot(hh, wu_ref[...], NN)
        sig = jax.nn.sigmoid(gate)
        silu = gate * sig
        g_ref[...] = (up * (sig + silu * (1.0 - sig))).astype(BF16)
        u_ref[...] = silu.astype(BF16)
        a_ref[...] = (silu * up).astype(BF16)

        if ni:
            @pl.when((c == nc - 1) & (i == nt - 1))
            def _():
                _gather_wait(outs, halves, send_sems, recv_sems)

    wspec = pl.BlockSpec((None, d, fs), lambda c, i: (c, 0, 0))
    ospec = pl.BlockSpec((None, tm, fs), lambda c, i: (c, i, 0))
    hbm = pl.BlockSpec(memory_space=pltpu.HBM)
    osd = _sds((nc, t, fs), BF16)
    res = tuple(_pcall(
        body, name=name, grid=(nc, nt),
        in_specs=[pl.BlockSpec((tm, d), lambda c, i: (i, 0)), wspec, wspec] + [hbm] * ni,
        out_specs=(ospec, ospec, ospec) + (hbm,) * ni,
        out_shape=(osd, osd, osd) + tuple(_sds((N_CHIPS,) + b.shape, b.dtype) for b in incoming),
        scratch_shapes=[pltpu.SemaphoreType.DMA((3 * ni,)), pltpu.SemaphoreType.DMA((3 * ni,))] if ni else [],
        compiler_params=_cparams("arbitrary", "arbitrary"))(h, wg, wu, *incoming))
    return res[:3] + (res[3:],)


def _ffn_hidden_bwd(dy, h, wd, dact_dgate, dact_dup, act, name, tm, outgoing=()):
    t, d = dy.shape
    nc, fs, _ = wd.shape
    tm = min(tm, t)
    nt = t // tm
    no = len(outgoing)

    def body(*refs):
        dy_ref, h_ref, wd_ref, g_ref, u_ref, a_ref = refs[:6]
        sent = refs[6:6 + no]
        dg_ref, du_ref, dwg_hbm, dwu_hbm, dwd_hbm = refs[6 + no:11 + no]
        arrived = refs[11 + no:11 + 2 * no]
        wg_acc, wu_acc, wd_acc, sem = refs[11 + 2 * no:15 + 2 * no]
        c, i = pl.program_id(0), pl.program_id(1)
        if no:
            send_sems, recv_sems = refs[15 + 2 * no:]

            @pl.when((c == 0) & (i == 0))
            def _():
                _scatter_start(sent, arrived, send_sems, recv_sems)

        dyb = dy_ref[...].astype(BF16)
        da = _dot(dyb, wd_ref[...], NT) * FFN_RESID
        dgate = (da * g_ref[...].astype(F32)).astype(BF16)
        dup = (da * u_ref[...].astype(F32)).astype(BF16)
        dg_ref[...] = dgate
        du_ref[...] = dup
        hh = h_ref[...]
        parts = (_dot(hh, dgate, TN), _dot(hh, dup, TN), _dot(a_ref[...], dyb, TN) * FFN_RESID)
        accs = (wg_acc, wu_acc, wd_acc)

        @pl.when(i == 0)
        def _():
            for acc, part in zip(accs, parts):
                acc[...] = part

        @pl.when(i > 0)
        def _():
            for acc, part in zip(accs, parts):
                acc[...] += part

        @pl.when(i == nt - 1)
        def _():
            copies = [pltpu.make_async_copy(acc, out.at[c], sem.at[n])
                      for n, (acc, out) in enumerate(zip(accs, (dwg_hbm, dwu_hbm, dwd_hbm)))]
            for cp in copies:
                cp.start()
            for cp in copies:
                cp.wait()

        if no:
            @pl.when((c == nc - 1) & (i == nt - 1))
            def _():
                _scatter_wait(sent, arrived, send_sems, recv_sems)

    tok = pl.BlockSpec((tm, d), lambda c, i: (i, 0))
    cspec = pl.BlockSpec((None, tm, fs), lambda c, i: (c, i, 0))
    hbm = pl.BlockSpec(memory_space=pltpu.HBM)
    osd = _sds((nc, t, fs), BF16)
    res = _pcall(
        body, name=name, grid=(nc, nt),
        in_specs=[tok, tok, pl.BlockSpec((None, fs, d), lambda c, i: (c, 0, 0)), cspec, cspec, cspec] + [hbm] * no,
        out_specs=(cspec, cspec, hbm, hbm, hbm) + (hbm,) * no,
        out_shape=(osd, osd, _sds((nc, d, fs), F32), _sds((nc, d, fs), F32), _sds((nc, fs, d), F32))
        + _scatter_shapes(outgoing),
        scratch_shapes=[pltpu.VMEM((d, fs), F32), pltpu.VMEM((d, fs), F32), pltpu.VMEM((fs, d), F32),
                        pltpu.SemaphoreType.DMA((3,))] + (_scatter_sems(no) if no else []),
        compiler_params=_cparams("arbitrary", "arbitrary"))(dy, h, wd, dact_dgate, dact_dup, act, *outgoing)
    res = tuple(res)
    return res[:5] + (res[5:],)


def _ffn_fwd(x, g, wg, wu, wd, tag, incoming=(), target=None):
    t = x.shape[0]
    nc, _, fs = wg.shape
    tm = min(512, t)
    h = _rms_fwd(x, g, BF16, f"{tag}_norm", 512)
    behind_down = tuple(incoming[-1:])
    dact_dgate, dact_dup, act, partly = _ffn_up(h, wg, wu, f"{tag}_up", 1024, tuple(incoming[:-1]))
    if target is not None:
        return _ffn_down_loss(act, wd, x, target, f"{tag}_down_loss", 512), (h, dact_dgate, dact_dup, act), partly
    pairs = [(act, pl.BlockSpec((None, tm, fs), lambda i, j, r, c=c: (c, i, 0)),
              wd, pl.BlockSpec((None, fs, D_MODEL), lambda i, j, r, c=c: (c, 0, 0))) for c in range(nc)]
    row = pl.BlockSpec((tm, D_MODEL), lambda i, j, r: (i, 0))
    y = _mm(f"{tag}_down", (t // tm, 1, 1), pairs, NN, _sds((t, D_MODEL), F32), row, (tm, D_MODEL),
            res=(x, row), scale=FFN_RESID, outgoing=behind_down, exchange="gather")
    if behind_down:
        y, more = y
        partly = tuple(partly) + tuple(more)
    return y, (h, dact_dgate, dact_dup, act), partly


def _ffn_bwd(dy, x, g, wg, wu, wd, saved, tag, outgoing=(), own_exchange=None):
    h, dact_dgate, dact_dup, act = saved
    t = x.shape[0]
    nc, _, fs = wg.shape
    tm = min(512, t)
    dgate, dup, dwg, dwu, dwd, arrived = _ffn_hidden_bwd(dy, h, wd, dact_dgate, dact_dup, act,
                                                         f"{tag}_hidden_bwd", 1024, outgoing)
    pairs = []
    for c in range(nc):
        a_spec = pl.BlockSpec((None, tm, fs), lambda i, j, r, c=c: (c, i, 0))
        w_spec = pl.BlockSpec((None, D_MODEL, fs), lambda i, j, r, c=c: (c, 0, 0))
        pairs += [(dgate, a_spec, wg, w_spec), (dup, a_spec, wu, w_spec)]
    own_part = tuple(own_exchange([dwg, dwu, dwd])) if own_exchange else ()
    row = pl.BlockSpec((tm, D_MODEL), lambda i, j, r: (i, 0))
    res = _mm(f"{tag}_dh", (t // tm, 1, 1), pairs, NT, _sds((t, D_MODEL), F32), row, (tm, D_MODEL),
              res=(dy, row), norm=(x, g), outgoing=own_part)
    dx, dg = res[0], res[1]
    own_got = res[2] if own_part else ()
    return dx, dg, dwg, dwu, dwd, arrived, (own_part, own_got)


def _mm_tn_multi(name, a, bs, tk):
    k, m = a.shape
    tk = min(tk, k)
    nb = len(bs)

    def body(*refs):
        a_ref, b_refs, o_refs = refs[0], refs[1:1 + nb], refs[1 + nb:]
        aa = a_ref[...].astype(BF16)
        parts = [_dot(aa, b_ref[...].astype(BF16), TN) for b_ref in b_refs]

        @pl.when(pl.program_id(0) == 0)
        def _():
            for o_ref, part in zip(o_refs, parts):
                o_ref[...] = part

        @pl.when(pl.program_id(0) > 0)
        def _():
            for o_ref, part in zip(o_refs, parts):
                o_ref[...] += part

    return _pcall(
        body, name=name, grid=(k // tk,),
        in_specs=[pl.BlockSpec((tk, m), lambda r: (r, 0))] + [pl.BlockSpec((tk, b.shape[1]), lambda r: (r, 0)) for b in bs],
        out_specs=tuple(pl.BlockSpec((m, b.shape[1]), lambda r: (0, 0)) for b in bs),
        out_shape=tuple(_sds((m, b.shape[1]), F32) for b in bs),
        compiler_params=_cparams("arbitrary"))(a, *bs)


def _mm_simple(name, a, b, dn, out_dtype, tm=512, tk=512, res=None, scale=1.0):
    if dn == TN:
        k, m = a.shape
        n = b.shape[1]
        tk = min(tk, k)
        return _mm(name, (1, 1, k // tk),
                   [(a, pl.BlockSpec((tk, m), lambda i, j, r: (r, 0)), b, pl.BlockSpec((tk, n), lambda i, j, r: (r, 0)))],
                   TN, _sds((m, n), out_dtype), pl.BlockSpec((m, n), lambda i, j, r: (0, 0)), (m, n), scale=scale)
    m, k = a.shape
    n = b.shape[1] if dn == NN else b.shape[0]
    tm = min(tm, m)
    row = pl.BlockSpec((tm, n), lambda i, j, r: (i, 0))
    return _mm(name, (m // tm, 1, 1),
               [(a, pl.BlockSpec((tm, k), lambda i, j, r: (i, 0)), b, pl.BlockSpec(b.shape, lambda i, j, r: (0, 0)))],
               dn, _sds((m, n), out_dtype), row, (tm, n), res=None if res is None else (res, row), scale=scale)


def _t5_bucket(dist):
    max_exact = REL_BUCKETS // 2
    d = np.maximum(dist, 1).astype(np.float32)
    large = max_exact + (np.log(d / max_exact) / np.log(REL_MAX_DIST / max_exact)
                         * (REL_BUCKETS - max_exact)).astype(np.int32)
    large = np.minimum(large, REL_BUCKETS - 1)
    return np.where(dist < max_exact, dist, large).astype(np.int32)


def _bucket_tiles():
    i = np.arange(QB)[:, None]
    j = np.arange(QB + DIL_W)[None, :]
    delta = np.clip(i + DIL_W - j, 0, None)
    return np.stack([_t5_bucket(delta * dil) for dil in DIL_DILATIONS]).astype(np.int32)


def _bias_tiles(rel_bias):
    buckets = jnp.asarray(_bucket_tiles())

    def body(rb_ref, bk_ref, o_ref):
        bk = bk_ref[...]
        for h in range(DIL_HEADS):
            def pick(b, tile):
                return jnp.where(bk == b, rb_ref[h, b], tile)

            o_ref[h] = lax.fori_loop(0, REL_BUCKETS, pick, jnp.zeros((QB, QB + DIL_W), F32))

    return _pcall(
        body, name="dil_bias_tiles", grid=(3,),
        in_specs=[pl.BlockSpec(memory_space=pltpu.SMEM),
                  pl.BlockSpec((None, QB, QB + DIL_W), lambda b: (b, 0, 0))],
        out_specs=pl.BlockSpec((None, DIL_HEADS, QB, QB + DIL_W), lambda b: (b, 0, 0, 0)),
        out_shape=_sds((3, DIL_HEADS, QB, QB + DIL_W), F32),
        compiler_params=_cparams("parallel"))(rel_bias, buckets)


def _bias_grad(dtiles):
    buckets = jnp.asarray(_bucket_tiles())

    def body(dt_ref, bk_ref, o_ref):
        def one(b, carry):
            hit = [bk_ref[br] == b for br in range(3)]
            for h in range(DIL_HEADS):
                tot = jnp.zeros((), F32)
                for br in range(3):
                    tot = tot + jnp.sum(jnp.where(hit[br], dt_ref[br, h], 0.0))
                o_ref[h, b] = tot
            return carry

        lax.fori_loop(0, REL_BUCKETS, one, 0)

    return _pcall(
        body, name="dil_bias_grad",
        in_specs=[pl.BlockSpec(memory_space=pltpu.VMEM), pl.BlockSpec(memory_space=pltpu.VMEM)],
        out_specs=pl.BlockSpec(memory_space=pltpu.SMEM),
        out_shape=_sds((DIL_HEADS, REL_BUCKETS), F32))(dtiles, buckets)


def _split_heads(a, lo):
    zero = jnp.zeros_like(a)
    return jnp.concatenate([jnp.where(lo, a, zero), jnp.where(lo, zero, a)], axis=0)


def _side_by_side(a):
    n = a.shape[0] // 2
    return jnp.concatenate([a[:n], a[n:]], axis=1)


def _band_masks(prev_ok):
    ii = lax.broadcasted_iota(jnp.int32, (2 * QB, QB), 0) & (QB - 1)
    jj = lax.broadcasted_iota(jnp.int32, (2 * QB, QB), 1)
    return jj <= ii, jj >= ii + jnp.where(prev_ok, 0, QB)


def _dil_fwd(q, k, v, bias, dil, name):
    w = DIL_WIDTH
    t = q.shape[0] * dil
    npair = w // LANES
    nl = t // dil // QB
    scale = DIL_HD ** -0.5

    def body(q_ref, kc_ref, kp_ref, vc_ref, vp_ref, b_ref, o_ref, lse_ref):
        nn = pl.program_id(1)
        lo = lax.broadcasted_iota(jnp.int32, (QB, LANES), 1) < DIL_HD
        lo2 = lax.broadcasted_iota(jnp.int32, (2 * QB, LANES), 1) < DIL_HD
        ii = lax.broadcasted_iota(jnp.int32, (2 * QB, 2 * QB), 0) & (QB - 1)
        jj = lax.broadcasted_iota(jnp.int32, (2 * QB, 2 * QB), 1)
        first_key = jnp.maximum(ii, jnp.where(nn != 0, 0, QB))
        valid = (jj >= first_key) & (jj <= ii + QB)
        for p in range(npair):
            cols = slice(p * LANES, (p + 1) * LANES)
            qq = _split_heads(q_ref[:, cols], lo)
            kk = jnp.concatenate([kp_ref[:, cols], kc_ref[:, cols]], axis=0)
            vv = jnp.concatenate([vp_ref[:, cols], vc_ref[:, cols]], axis=0)
            s = jnp.where(valid, _dot(qq, kk, NT) * scale + b_ref[p], NEG)
            m = jnp.max(s, axis=-1, keepdims=True)
            e = jnp.exp(s - m)
            den = jnp.sum(e, axis=-1, keepdims=True)
            pn = (e * (1.0 / den)).astype(BF16)
            o_ref[:, cols] = _dot(_side_by_side(pn), _split_heads(vv, lo2), NN)
            lse = m + jnp.log(den)
            lse_ref[:, cols] = jnp.where(lo, lse[:QB], lse[QB:])

    cur = pl.BlockSpec((QB, w), lambda r, n: (n, r))
    prev = pl.BlockSpec((QB, w), lambda r, n: (jnp.maximum(n - 1, 0), r))
    sd = _sds((t // dil, dil * w), F32)
    return _pcall(
        body, name=name, grid=(dil, nl),
        in_specs=[cur, cur, prev, cur, prev, pl.BlockSpec((npair, 2 * QB, 2 * QB), lambda r, n: (0, 0, 0))],
        out_specs=(cur, cur), out_shape=(sd, sd),
        compiler_params=_cparams("parallel", "parallel"))(q, k, k, v, v, bias)


def _dil_bwd(q, k, v, do, stats, bias, dil, name):
    w = DIL_WIDTH
    t = q.shape[0] * dil
    npair = w // LANES
    nl = t // dil // QB
    scale = DIL_HD ** -0.5

    def body(qc_ref, qn_ref, doc_ref, don_ref, sc_ref, sn_ref, k_ref, v_ref, b_ref,
             dq_ref, dk_ref, dv_ref, db_ref, carry):
        r, nn = pl.program_id(0), pl.program_id(1)
        lo = lax.broadcasted_iota(jnp.int32, (QB, LANES), 1) < DIL_HD
        cur_ok, prev_ok = _band_masks(nn + 1 < nl)

        @pl.when((r == 0) & (nn == 0))
        def _():
            db_ref[...] = jnp.zeros_like(db_ref)
            carry[...] = jnp.zeros_like(carry)

        for p in range(npair):
            cols = slice(p * LANES, (p + 1) * LANES)
            kp, vp = k_ref[:, cols], v_ref[:, cols]
            k2 = _split_heads(kp, lo)

            def column(ref, lane):
                first = p * LANES + lane
                return jnp.concatenate([ref[:, first:first + 1], ref[:, first + DIL_HD:first + DIL_HD + 1]], axis=0)

            def side(q_ref, do_ref, s_ref, bias, ok):
                qq = _split_heads(q_ref[:, cols], lo)
                dd = _split_heads(do_ref[:, cols], lo)
                s = jnp.where(ok, _dot(qq, kp, NT) * scale + bias, NEG)
                prob = jnp.exp(s - column(s_ref, 0))
                ds = prob * (_dot(dd, vp, NT) - column(s_ref, DIL_HD // 2))
                return qq, dd, prob.astype(BF16), ds

            q1, d1, p1, ds1 = side(qc_ref, doc_ref, sc_ref, b_ref[p, :, QB:], cur_ok)
            q2, d2, p2, ds2 = side(qn_ref, don_ref, sn_ref, b_ref[p, :, :QB], prev_ok)
            ds1b, ds2b = ds1.astype(BF16), ds2.astype(BF16)
            dq_ref[:, cols] = carry[:, cols] + _dot(_side_by_side(ds1b), k2, NN) * scale
            carry[:, cols] = _dot(_side_by_side(ds2b), k2, NN) * scale
            dk_ref[:, cols] = _dot(jnp.concatenate([ds1b, ds2b], axis=0), jnp.concatenate([q1, q2], axis=0), TN) * scale
            dv_ref[:, cols] = _dot(jnp.concatenate([p1, p2], axis=0), jnp.concatenate([d1, d2], axis=0), TN)
            db_ref[p, :, QB:] += ds1
            db_ref[p, :, :QB] += ds2

    cur = pl.BlockSpec((QB, w), lambda r, n: (n, r))
    nxt = pl.BlockSpec((QB, w), lambda r, n: (jnp.minimum(n + 1, nl - 1), r))
    tile = pl.BlockSpec((npair, 2 * QB, 2 * QB), lambda r, n: (0, 0, 0))
    sd = _sds((t // dil, dil * w), F32)
    return _pcall(
        body, name=name, grid=(dil, nl),
        in_specs=[cur, nxt, cur, nxt, cur, nxt, cur, cur, tile],
        out_specs=(cur, cur, cur, tile),
        out_shape=(sd, sd, sd, _sds((npair, 2 * QB, 2 * QB), F32)),
        scratch_shapes=[pltpu.VMEM((QB, w), F32)],
        compiler_params=_cparams("arbitrary", "arbitrary"))(q, q, do, do, stats, stats, k, v, bias)


def _head_sum_matrix(scale):
    idx = np.arange(DIL_WIDTH) // DIL_HD
    return jnp.asarray((idx[:, None] == idx[None, :]).astype(np.float32) * scale, BF16)


def _head_sum(x, mat):
    hi = x.astype(BF16)
    lo = (x - hi.astype(F32)).astype(BF16)
    return _dot(hi, mat, NN) + _dot(lo, mat, NN)


def _to_views(src, tmp, out_refs):
    tm, w = src.shape
    for j in range(w // LANES):
        tmp[j] = src[:, j * LANES:(j + 1) * LANES]
    for d, o_ref in zip(DIL_DILATIONS, out_refs):
        if d == 1:
            o_ref[...] = src.astype(o_ref.dtype)
            continue
        for r in range(d):
            for j in range(w // LANES):
                lo = r * w + j * LANES
                o_ref[:, lo:lo + LANES] = tmp[j, pl.ds(r, tm // d, stride=d), :].astype(o_ref.dtype)


def _from_view(v_ref, tmp, d):
    tm = tmp.shape[1]
    w = v_ref.shape[1] // d
    for r in range(d):
        for j in range(w // LANES):
            lo = r * w + j * LANES
            tmp[j, pl.ds(r, tm // d, stride=d), :] = v_ref[:, lo:lo + LANES]
    return jnp.concatenate([tmp[j] for j in range(w // LANES)], axis=1)


def _view_specs(tm, t, dtype):
    specs = tuple(pl.BlockSpec((tm // d, d * DIL_WIDTH), lambda i: (i, 0)) for d in DIL_DILATIONS)
    shapes = tuple(_sds((t // d, d * DIL_WIDTH), dtype) for d in DIL_DILATIONS)
    return specs, shapes


def _view_scratch(tm):
    return pltpu.VMEM((DIL_WIDTH // LANES, tm, LANES), F32)


def _dil_merge(outs, lses, g, tm):
    w = DIL_WIDTH
    t = outs[0].shape[0]
    tm = min(tm, t)

    def body(o0, o1, o2, l0, l1, l2, g_ref, o_ref, l_ref, n_ref, so1, so2, sl1, sl2):
        d1, d2 = DIL_DILATIONS[1], DIL_DILATIONS[2]
        a0, a1, a2 = l0[...], _from_view(l1, sl1, d1), _from_view(l2, sl2, d2)
        m = jnp.maximum(jnp.maximum(a0, a1), a2)
        e0, e1, e2 = jnp.exp(a0 - m), jnp.exp(a1 - m), jnp.exp(a2 - m)
        den = e0 + e1 + e2
        o = (e0 * o0[...] + e1 * _from_view(o1, so1, d1) + e2 * _from_view(o2, so2, d2)) / den
        o_ref[...] = o
        l_ref[...] = m + jnp.log(den)
        r = lax.rsqrt(jnp.mean(o * o, axis=-1, keepdims=True) + EPS)
        n_ref[...] = (o * r * g_ref[...]).astype(n_ref.dtype)

    specs, _ = _view_specs(tm, t, F32)
    spec = pl.BlockSpec((tm, w), lambda i: (i, 0))
    return _pcall(
        body, name="dil_merge", grid=(t // tm,),
        in_specs=list(specs) * 2 + [pl.BlockSpec((1, w), lambda i: (0, 0))], out_specs=(spec, spec, spec),
        out_shape=(_sds((t, w), F32), _sds((t, w), F32), _sds((t, w), BF16)),
        scratch_shapes=[_view_scratch(tm)] * 4,
        compiler_params=_cparams("parallel"))(*outs, *lses, g)


def _dil_stats(do, o, lse, tm):
    t, w = do.shape
    tm = min(tm, t)

    def body(a_ref, b_ref, l_ref, m_ref, s1, s4, s16, d1, d4, d16, tmp):
        first = (lax.broadcasted_iota(jnp.int32, (tm, w), 1) & (DIL_HD - 1)) < DIL_HD // 2
        do_ = a_ref[...]
        _to_views(jnp.where(first, l_ref[...], _head_sum(do_ * b_ref[...], m_ref[...])), tmp, (s1, s4, s16))
        _to_views(do_, tmp, (d1, d4, d16))

    spec = pl.BlockSpec((tm, w), lambda i: (i, 0))
    f_specs, f_shapes = _view_specs(tm, t, F32)
    b_specs, b_shapes = _view_specs(tm, t, BF16)
    res = _pcall(body, name="dil_stats", grid=(t // tm,),
                 in_specs=[spec, spec, spec, pl.BlockSpec((w, w), lambda i: (0, 0))],
                 out_specs=f_specs + b_specs, out_shape=f_shapes + b_shapes,
                 scratch_shapes=[_view_scratch(tm)],
                 compiler_params=_cparams("parallel"))(do, o, lse, _head_sum_matrix(1.0))
    return res[:3], res[3:]


def _head_norm_fwd(x, col, g, name, tm):
    t = x.shape[0]
    w = DIL_WIDTH
    tm = min(tm, t)
    normed = g is not None

    def body(*refs):
        outs, tmp = refs[-4:-1], refs[-1]
        xf = refs[0][...]
        if normed:
            g_ref, m_ref = refs[1], refs[2]
            xf = xf * lax.rsqrt(_head_sum(xf * xf, m_ref[...]) + EPS) * g_ref[...]
        _to_views(xf, tmp, outs)

    specs, shapes = _view_specs(tm, t, BF16)
    extra = [g, _head_sum_matrix(1.0 / DIL_HD)] if normed else []
    extra_specs = [pl.BlockSpec((1, w), lambda i: (0, 0)), pl.BlockSpec((w, w), lambda i: (0, 0))] if normed else []
    return _pcall(
        body, name=name, grid=(t // tm,),
        in_specs=[pl.BlockSpec((tm, w), lambda i: (i, col))] + extra_specs,
        out_specs=specs, out_shape=shapes, scratch_shapes=[_view_scratch(tm)],
        compiler_params=_cparams("parallel"))(x, *extra)


def _head_norm_bwd(dys, x, col, g, name, tm):
    t = x.shape[0]
    w = DIL_WIDTH
    tm = min(tm, t)
    nd = len(dys)
    nt = t // tm
    lane = np.arange(w) % DIL_HD
    fold = jnp.asarray((lane[:, None] == lane[None, :]).astype(np.float32))

    def body(*refs):
        x_ref, g_ref, m_ref, f_ref = refs[nd:nd + 4]
        dx_ref, dg_ref, s1, s2 = refs[-4:]
        dy = refs[0][...] + _from_view(refs[1], s1, DIL_DILATIONS[1]) + _from_view(refs[2], s2, DIL_DILATIONS[2])
        xf = x_ref[...]
        mat = m_ref[...]
        r = lax.rsqrt(_head_sum(xf * xf, mat) + EPS)
        xh = xf * r
        dxh = dy * g_ref[...]
        dx_ref[...] = r * (dxh - xh * _head_sum(dxh * xh, mat))

        @pl.when(pl.program_id(0) == 0)
        def _():
            dg_ref[...] = jnp.zeros_like(dg_ref)

        dg_ref[...] += jnp.sum(dy * xh, axis=0, keepdims=True)

        @pl.when(pl.program_id(0) == nt - 1)
        def _():
            per_lane = jnp.broadcast_to(dg_ref[...], (8, w))
            dg_ref[...] = lax.dot_general(per_lane, f_ref[...], NN, precision=lax.Precision.HIGHEST,
                                          preferred_element_type=F32)[0:1]

    row = pl.BlockSpec((tm, w), lambda i: (i, 0))
    vec = pl.BlockSpec((1, w), lambda i: (0, 0))
    sq = pl.BlockSpec((w, w), lambda i: (0, 0))
    views, _ = _view_specs(tm, t, F32)
    return _pcall(
        body, name=name, grid=(nt,),
        in_specs=list(views) + [pl.BlockSpec((tm, w), lambda i: (i, col)), vec, sq, sq],
        out_specs=(row, vec), out_shape=(_sds((t, w), F32), _sds((1, w), F32)),
        scratch_shapes=[_view_scratch(tm)] * 2,
        compiler_params=_cparams("arbitrary"))(*dys, x, g, _head_sum_matrix(1.0 / DIL_HD), fold)


def _rowdot(a, b, name, tm):
    n, d = a.shape
    tm = min(tm, n)

    def body(a_ref, b_ref, o_ref):
        o_ref[...] = jnp.sum(a_ref[...].astype(F32) * b_ref[...].astype(F32), axis=-1, keepdims=True)

    spec = pl.BlockSpec((tm, d), lambda i: (i, 0))
    return _pcall(body, name=name, grid=(n // tm,), in_specs=[spec, spec],
                  out_specs=pl.BlockSpec((tm, 1), lambda i: (i, 0)), out_shape=_sds((n, 1), F32),
                  compiler_params=_cparams("parallel"))(a, b)


def _sum_branches(parts, name, tm):
    t = parts[0].shape[0]
    w = DIL_WIDTH
    tm = min(tm, t)

    def body(a_ref, b_ref, c_ref, o_ref, s1, s2):
        o_ref[...] = a_ref[...] + _from_view(b_ref, s1, DIL_DILATIONS[1]) + _from_view(c_ref, s2, DIL_DILATIONS[2])

    views, _ = _view_specs(tm, t, F32)
    return _pcall(body, name=name, grid=(t // tm,), in_specs=list(views),
                  out_specs=pl.BlockSpec((tm, w), lambda i: (i, 0)), out_shape=_sds((t, w), F32),
                  scratch_shapes=[_view_scratch(tm)] * 2,
                  compiler_params=_cparams("parallel"))(*parts)


def _rope_tables(t):
    inv = ROPE_BASE ** (-np.arange(0, MLA_ROPE, 2, dtype=np.float64) / MLA_ROPE)
    ang = np.arange(t, dtype=np.float64)[:, None] * inv[None, :]
    cos, sin = np.cos(ang), np.sin(ang)
    return (jnp.asarray(np.concatenate([cos, cos], 1), F32), jnp.asarray(np.concatenate([-sin, sin], 1), F32))


def _swap_halves(a):
    half = MLA_ROPE // 2
    return jnp.concatenate([a[:, half:], a[:, :half]], axis=1)


def _qk_parts(x, pe, tm, nt):
    if pe is None:
        return None
    return (pl.BlockSpec((tm, MLA_NOPE), lambda i: (i, 0)), pl.BlockSpec((tm, MLA_ROPE), lambda i: (i % nt, 0)))


def _mla_qk_fwd(x, g, cos_t, sin_t, scale, name, tm, pe=None):
    n = x.shape[0]
    d = MLA_QK
    t = cos_t.shape[0]
    tm = min(tm, t)
    nt = t // tm
    split = _qk_parts(x, pe, tm, nt)

    def transposed(a):
        w = a.shape[1]
        eye = (lax.broadcasted_iota(jnp.int32, (w, w), 0) == lax.broadcasted_iota(jnp.int32, (w, w), 1)).astype(BF16)
        return _dot(eye, a, NT).astype(BF16)

    def body(*refs):
        if split:
            xn_ref, xr_ref, xv_ref, g_ref, c_ref, s_ref, o_ref, ot_ref, v_ref = refs
            xn, xr = xn_ref[...], xr_ref[...]
            v_ref[...] = xv_ref[...].astype(v_ref.dtype)
        else:
            x_ref, g_ref, c_ref, s_ref, o_ref = refs
            xf = x_ref[...]
            xn, xr = xf[:, :MLA_NOPE], xf[:, MLA_NOPE:]
        ms = (jnp.sum(xn * xn, axis=-1, keepdims=True) + jnp.sum(xr * xr, axis=-1, keepdims=True)) * (1.0 / d)
        r = lax.rsqrt(ms + EPS)
        gg = g_ref[...]
        yn = xn * r * gg[:, :MLA_NOPE]
        yr = xr * r * gg[:, MLA_NOPE:]
        on = (yn * scale).astype(o_ref.dtype)
        orot = ((yr * c_ref[...] + _swap_halves(yr) * s_ref[...]) * scale).astype(o_ref.dtype)
        o_ref[:, :MLA_NOPE] = on
        o_ref[:, MLA_NOPE:] = orot
        if split:
            ot_ref[:MLA_NOPE, :] = transposed(on)
            ot_ref[MLA_NOPE:, :] = transposed(orot)

    row = pl.BlockSpec((tm, d), lambda i: (i, 0))
    vec = pl.BlockSpec((1, d), lambda i: (0, 0))
    tab = pl.BlockSpec((tm, MLA_ROPE), lambda i: (i % nt, 0))
    if not split:
        return _pcall(body, name=name, grid=(n // tm,), in_specs=[row, vec, tab, tab],
                      out_specs=row, out_shape=_sds((n, d), BF16),
                      compiler_params=_cparams("parallel"))(x, g, cos_t, sin_t)
    vals = pl.BlockSpec((tm, MLA_V), lambda i: (i, 1))
    return _pcall(body, name=name, grid=(n // tm,), in_specs=[split[0], split[1], vals, vec, tab, tab],
                  out_specs=(row, pl.BlockSpec((None, d, tm), lambda i: (i // nt, 0, i % nt)),
                             pl.BlockSpec((tm, MLA_V), lambda i: (i, 0))),
                  out_shape=(_sds((n, d), BF16), _sds((n // t, d, t), BF16), _sds((n, MLA_V), BF16)),
                  compiler_params=_cparams("parallel"))(x, pe, x, g, cos_t, sin_t)


def _mla_qk_bwd(dy, x, g, cos_t, sin_t, scale, name, tm, pe=None):
    n = x.shape[0]
    d = MLA_QK
    t = cos_t.shape[0]
    tm = min(tm, t)
    nt = t // tm
    split = _qk_parts(x, pe, tm, nt)

    def body(*refs):
        if split:
            dy_ref, xn_ref, xr_ref, g_ref, c_ref, s_ref, dxn_ref, dxr_ref, dg_ref = refs
            xn, xr = xn_ref[...], xr_ref[...]
        else:
            dy_ref, x_ref, g_ref, c_ref, s_ref, dx_ref, dg_ref = refs
            xf = x_ref[...]
            xn, xr = xf[:, :MLA_NOPE], xf[:, MLA_NOPE:]
        gg = g_ref[...]
        ms = (jnp.sum(xn * xn, axis=-1, keepdims=True) + jnp.sum(xr * xr, axis=-1, keepdims=True)) * (1.0 / d)
        r = lax.rsqrt(ms + EPS)
        xh_n, xh_r = xn * r, xr * r
        dyf = dy_ref[...] * scale
        dyr = dyf[:, MLA_NOPE:]
        dn_n = dyf[:, :MLA_NOPE]
        dn_r = dyr * c_ref[...] + _swap_halves(dyr * s_ref[...])
        dxh_n = dn_n * gg[:, :MLA_NOPE]
        dxh_r = dn_r * gg[:, MLA_NOPE:]
        mean = (jnp.sum(dxh_n * xh_n, axis=-1, keepdims=True)
                + jnp.sum(dxh_r * xh_r, axis=-1, keepdims=True)) * (1.0 / d)
        dx_n = r * (dxh_n - xh_n * mean)
        dx_r = r * (dxh_r - xh_r * mean)
        if split:
            dxn_ref[...] = dx_n
            dxr_ref[...] = dx_r
        else:
            dx_ref[:, :MLA_NOPE] = dx_n
            dx_ref[:, MLA_NOPE:] = dx_r

        @pl.when(pl.program_id(0) == 0)
        def _():
            dg_ref[...] = jnp.zeros_like(dg_ref)

        dg_ref[:, :MLA_NOPE] += jnp.sum(dn_n * xh_n, axis=0, keepdims=True)
        dg_ref[:, MLA_NOPE:] += jnp.sum(dn_r * xh_r, axis=0, keepdims=True)

    row = pl.BlockSpec((tm, d), lambda i: (i, 0))
    vec = pl.BlockSpec((1, d), lambda i: (0, 0))
    tab = pl.BlockSpec((tm, MLA_ROPE), lambda i: (i % nt, 0))
    if not split:
        return _pcall(body, name=name, grid=(n // tm,), in_specs=[row, row, vec, tab, tab],
                      out_specs=(row, vec), out_shape=(_sds((n, d), F32), _sds((1, d), F32)),
                      compiler_params=_cparams("arbitrary"))(dy, x, g, cos_t, sin_t)
    outs = (pl.BlockSpec((tm, MLA_NOPE), lambda i: (i, 0)), pl.BlockSpec((tm, MLA_ROPE), lambda i: (i, 0)), vec)
    return _pcall(body, name=name, grid=(n // tm,), in_specs=[row, split[0], split[1], vec, tab, tab],
                  out_specs=outs, out_shape=(_sds((n, MLA_NOPE), F32), _sds((n, MLA_ROPE), F32), _sds((1, d), F32)),
                  compiler_params=_cparams("arbitrary"))(dy, x, pe, g, cos_t, sin_t)


def _causal_mask(i, j, tq, tk, width):
    row = i * tq + lax.broadcasted_iota(jnp.int32, (tq, width), 0)
    col = j * tk + lax.broadcasted_iota(jnp.int32, (tq, width), 1)
    return col <= row


def _causal_steps(nq, nk, tq, tk, q_major):
    if q_major:
        groups = [[(i, j) for j in range((i * tq + tq - 1) // tk + 1)] for i in range(nq)]
        nunit = tk // tq if tk % tq == 0 else 1
    else:
        groups = [[(i, j) for i in range((j * tk) // tq, nq)] for j in range(nk)]
        nunit = tq // tk if tq % tk == 0 else 1
    it, jt, fl = [], [], []
    for g in groups:
        for n, (i, j) in enumerate(g):
            crossing = j * tk + tk - 1 > i * tq
            if q_major:
                unit = tk // nunit
                u = min(nunit, -(-(i * tq + tq - j * tk) // unit)) - 1
            else:
                unit = tq // nunit
                u = max(0, j * tk - i * tq) // unit
            it.append(i)
            jt.append(j)
            fl.append((n == 0) + 2 * (n == len(g) - 1) + 4 * crossing + 8 * (u if crossing else 0))
    return tuple(jnp.asarray(np.array(a, np.int32)) for a in (it, jt, fl)), nunit


def _by_crossing(flags, nunit, update):
    pl.when((flags & 4) == 0)(functools.partial(update, None))
    for u in range(nunit):
        pl.when(((flags & 4) != 0) & ((flags >> 3) == u))(functools.partial(update, u))


def _causal_specs(tq, tk):
    def qs(w):
        return pl.BlockSpec((None, tq, w), lambda h, s, it, jt, fl: (h, it[s], 0))

    def kv(w):
        return pl.BlockSpec((None, tk, w), lambda h, s, it, jt, fl: (h, jt[s], 0))

    return qs, kv


def _mla_fwd(q, k, v, tq, tk):
    nh, t, dq = q.shape
    dv = v.shape[2]
    tq, tk = min(tq, t), min(tk, t)
    tables, nunit = _causal_steps(t // tq, t // tk, tq, tk, True)

    def body(it, jt, fl, q_ref, k_ref, v_ref, o_ref, lse_ref, m_sc, l_sc, acc_sc):
        step = pl.program_id(1)
        i, j, flags = it[step], jt[step], fl[step]

        @pl.when((flags & 1) != 0)
        def _():
            m_sc[...] = jnp.full_like(m_sc, NEG)
            l_sc[...] = jnp.zeros_like(l_sc)
            acc_sc[...] = jnp.zeros_like(acc_sc)

        def update(units):
            wk = tk if units is None else (units + 1) * (tk // nunit)
            s = _dot(q_ref[...], k_ref[:wk, :], NT)
            if units is not None:
                s = jnp.where(_causal_mask(i, j, tq, tk, wk), s, NEG)
            m_prev = m_sc[...]
            m_new = jnp.maximum(m_prev, jnp.max(s, axis=-1, keepdims=True))
            alpha = jnp.exp(m_prev - m_new)
            p = jnp.exp(s - m_new)
            l_sc[...] = alpha * l_sc[...] + jnp.sum(p, axis=-1, keepdims=True)
            acc_sc[...] = alpha * acc_sc[...] + _dot(p.astype(BF16), v_ref[:wk, :], NN)
            m_sc[...] = m_new

        _by_crossing(flags, nunit, update)

        @pl.when((flags & 2) != 0)
        def _():
            o_ref[...] = acc_sc[...] / l_sc[...]
            lse_ref[...] = m_sc[...] + jnp.log(l_sc[...])

    qs, kv = _causal_specs(tq, tk)
    return _pcall(
        body, name="mla_attn_fwd",
        grid_spec=pltpu.PrefetchScalarGridSpec(
            num_scalar_prefetch=3, grid=(nh, tables[0].shape[0]),
            in_specs=[qs(dq), kv(dq), kv(dv)], out_specs=(qs(dv), qs(1)),
            scratch_shapes=[pltpu.VMEM((tq, 1), F32), pltpu.VMEM((tq, 1), F32), pltpu.VMEM((tq, dv), F32)]),
        out_shape=(_sds((nh, t, dv), F32), _sds((nh, t, 1), F32)),
        compiler_params=_cparams("parallel", "arbitrary"))(*tables, q, k, v)


def _mla_bwd(q, k, k_t, v, do, lse_row, dl_row, tq, tk):
    nh, t, dq = q.shape
    dv = v.shape[2]
    tq, tk = min(tq, t), min(tk, t)
    nq = t // tq
    tables, nunit = _causal_steps(nq, t // tk, tq, tk, False)

    def body(it, jt, fl, q_ref, k_ref, kt_ref, v_ref, do_ref, lse_ref, dl_ref, dk_ref, dv_ref, dq_ref, dk_sc, dv_sc):
        step = pl.program_id(1)
        i, j, flags = it[step], jt[step], fl[step]

        def update(units):
            off = 0 if units is None else units * (tq // nunit)
            qq = q_ref[off:, :]
            st = _dot(k_ref[...], qq, NT)
            if units is not None:
                key = j * tk + lax.broadcasted_iota(jnp.int32, (tk, tq - off), 0)
                qry = i * tq + off + lax.broadcasted_iota(jnp.int32, (tk, tq - off), 1)
                st = jnp.where(key <= qry, st, NEG)
            pt = jnp.exp(st - lse_ref[:, off:])
            dob = do_ref[off:, :].astype(BF16)
            dpt = _dot(v_ref[...], dob, NT)
            dst = pt * (dpt - dl_ref[:, off:])
            dsb = dst.astype(BF16)
            dv_part = _dot(pt.astype(BF16), dob, NN)
            dk_part = _dot(dsb, qq, NN)
            dq_part = _dot(kt_ref[...], dsb, NN)

            @pl.when((flags & 1) != 0)
            def _():
                dv_sc[...] = dv_part
                dk_sc[...] = dk_part

            @pl.when((flags & 1) == 0)
            def _():
                dv_sc[...] += dv_part
                dk_sc[...] += dk_part

            if off == 0:
                @pl.when(j == 0)
                def _():
                    dq_ref[i] = dq_part

                @pl.when(j != 0)
                def _():
                    dq_ref[i] += dq_part
            else:
                dq_ref[i, :, off:] += dq_part

        _by_crossing(flags, nunit, update)

        @pl.when((flags & 2) != 0)
        def _():
            dk_ref[...] = dk_sc[...]
            dv_ref[...] = dv_sc[...]

    qs, kv = _causal_specs(tq, tk)
    rowv = pl.BlockSpec((None, 1, tq), lambda h, s, it, jt, fl: (h, 0, it[s]))
    ktv = pl.BlockSpec((None, dq, tk), lambda h, s, it, jt, fl: (h, 0, jt[s]))
    whole = pl.BlockSpec((None, nq, dq, tq), lambda h, s, it, jt, fl: (h, 0, 0, 0))
    return _pcall(
        body, name="mla_attn_bwd",
        grid_spec=pltpu.PrefetchScalarGridSpec(
            num_scalar_prefetch=3, grid=(nh, tables[0].shape[0]),
            in_specs=[qs(dq), kv(dq), ktv, kv(dv), qs(dv), rowv, rowv], out_specs=(kv(dq), kv(dv), whole),
            scratch_shapes=[pltpu.VMEM((tk, dq), F32), pltpu.VMEM((tk, dv), F32)]),
        out_shape=(_sds((nh, t, dq), F32), _sds((nh, t, dv), F32), _sds((nh, nq, dq, tq), F32)),
        compiler_params=_cparams("parallel", "arbitrary"))(*tables, q, k, k_t, v, do, lse_row, dl_row)


def _ffn_down_loss(act, wd, x, target, name, tm):
    nc, t, fs = act.shape
    d = x.shape[1]
    tm = min(tm, t)
    nt = t // tm

    def body(*refs):
        a_refs, w_refs = refs[:nc], refs[nc:2 * nc]
        x_ref, t_ref, dy_ref, loss_ref, acc = refs[2 * nc:]
        i = pl.program_id(0)
        tot = _dot(a_refs[0][...], w_refs[0][...], NN)
        for c in range(1, nc):
            tot = tot + _dot(a_refs[c][...], w_refs[c][...], NN)
        err = x_ref[...] + tot * FFN_RESID - t_ref[...]
        dy_ref[...] = err * (1.0 / d)

        @pl.when(i == 0)
        def _():
            acc[...] = jnp.zeros_like(acc)

        acc[...] += jnp.sum(err * err, axis=0, keepdims=True)

        @pl.when(i == nt - 1)
        def _():
            loss_ref[0, 0] = jnp.sum(acc[...]) * (0.5 / d)

    row = pl.BlockSpec((tm, d), lambda i: (i, 0))
    a_specs = [pl.BlockSpec((None, tm, fs), lambda i, c=c: (c, i, 0)) for c in range(nc)]
    w_specs = [pl.BlockSpec((None, fs, d), lambda i, c=c: (c, 0, 0)) for c in range(nc)]
    return _pcall(
        body, name=name, grid=(nt,), in_specs=a_specs + w_specs + [row, row],
        out_specs=(row, pl.BlockSpec(memory_space=pltpu.SMEM)),
        out_shape=(_sds((t, d), F32), _sds((1, 1), F32)),
        scratch_shapes=[pltpu.VMEM((1, d), F32)],
        compiler_params=_cparams("arbitrary"))(*[act] * nc, *[wd] * nc, x, target)


def _adamw(w, g, m, v, name):
    r, c = w.shape
    tr = r // 2 if r % 16 == 0 else r

    def body(w_ref, g_ref, m_ref, v_ref, d_ref, nm_ref, nv_ref):
        gg = g_ref[...]
        nm = ADAM_B1 * m_ref[...] + (1.0 - ADAM_B1) * gg
        nv = ADAM_B2 * v_ref[...] + (1.0 - ADAM_B2) * (gg * gg)
        m_hat = nm / (1.0 - ADAM_B1 ** ADAM_STEP)
        v_hat = nv / (1.0 - ADAM_B2 ** ADAM_STEP)
        d_ref[...] = -ADAM_LR * (m_hat / (jnp.sqrt(v_hat) + ADAM_EPS) + ADAM_WD * w_ref[...])
        nm_ref[...] = nm
        nv_ref[...] = nv

    spec = pl.BlockSpec((tr, c), lambda i: (i, 0))
    sd = _sds((r, c), F32)
    return _pcall(body, name=name, grid=(r // tr,), in_specs=[spec] * 4, out_specs=(spec,) * 3,
                  out_shape=(sd, sd, sd), compiler_params=_cparams("parallel"))(w, g, m, v)


MESH_ID = pl.DeviceIdType.MESH
HBM_SPEC = pl.BlockSpec(memory_space=pltpu.HBM)


def _place():
    return lax.axis_index("x"), lax.axis_index("y"), lax.axis_index("c")


def _other_chips(x, y):
    return [(1 - x, y), (x, 1 - y), (1 - x, 1 - y)]


def _remote(src, dst, send_sems, recv_sems, k, to):
    return pltpu.make_async_remote_copy(src_ref=src, dst_ref=dst, send_sem=send_sems.at[k], recv_sem=recv_sems.at[k],
                                        device_id=to, device_id_type=MESH_ID)


def _halves(arrays):
    for a in arrays:
        assert a.shape[-2] % 32 == 0
    return [a.shape[-2] // 2 for a in arrays]


def _gather_start(srcs, outs, halves, send_sems, recv_sems):
    x, y, c = _place()
    for a, half in enumerate(halves):
        rows = pl.ds(c * half, half)
        for k, (cx, cy) in enumerate(_other_chips(x, y)):
            _remote(srcs[a].at[rows, :], outs[a].at[2 * x + y, rows, :], send_sems, recv_sems, 3 * a + k,
                    (cx, cy, c)).start()


def _gather_wait(outs, halves, send_sems, recv_sems):
    x, y, c = _place()
    for a, half in enumerate(halves):
        for k, (cx, cy) in enumerate(_other_chips(x, y)):
            got = outs[a].at[2 * cx + cy, pl.ds(c * half, half), :]
            _remote(got, got, send_sems, recv_sems, 3 * a + k, (x, y, c)).wait()


def _forward_cores(partly):
    n = len(partly)
    halves = _halves(partly)

    def body(*refs):
        srcs, outs, send_sems, recv_sems = refs[:n], refs[n:2 * n], refs[2 * n], refs[2 * n + 1]
        x, y, c = _place()
        for a, half in enumerate(halves):
            for k, (cx, cy) in enumerate(_other_chips(x, y)):
                rows = pl.ds(c * half, half)
                _remote(srcs[a].at[2 * cx + cy, rows, :], outs[a].at[2 * cx + cy, rows, :], send_sems, recv_sems,
                        3 * a + k, (x, y, 1 - c)).start()
        for a, half in enumerate(halves):
            for k, (cx, cy) in enumerate(_other_chips(x, y)):
                mine = outs[a].at[2 * cx + cy, pl.ds(c * half, half), :]
                theirs = outs[a].at[2 * cx + cy, pl.ds((1 - c) * half, half), :]
                _remote(mine, theirs, send_sems, recv_sems, 3 * a + k, (x, y, c)).wait()

    return _pcall(
        body, name="forward_cores", in_specs=[HBM_SPEC] * n, out_specs=tuple([HBM_SPEC] * n),
        out_shape=tuple(_sds(p.shape, p.dtype) for p in partly), input_output_aliases={a: a for a in range(n)},
        scratch_shapes=[pltpu.SemaphoreType.DMA((3 * n,)), pltpu.SemaphoreType.DMA((3 * n,))],
    )(*partly)


def _gather_weights(blocks):
    n = len(blocks)
    halves = _halves(blocks)

    def body(*refs):
        srcs, outs, send_sems, recv_sems = refs[:n], refs[n:2 * n], refs[2 * n], refs[2 * n + 1]
        x, y, c = _place()
        me = 2 * x + y
        sibling = (x, y, 1 - c)
        chips = _other_chips(x, y)

        def part(a, chip, core):
            return outs[a].at[chip, pl.ds(core * halves[a], halves[a]), :]

        for a in range(n):
            mine = srcs[a].at[pl.ds(c * halves[a], halves[a]), :]
            for k, (cx, cy) in enumerate(chips):
                _remote(mine, part(a, me, c), send_sems, recv_sems, 6 * a + k, (cx, cy, c)).start()
        for k, (cx, cy) in enumerate(chips):
            for a in range(n):
                got = part(a, 2 * cx + cy, c)
                _remote(got, got, send_sems, recv_sems, 6 * a + k, (x, y, c)).wait_recv()
                _remote(got, got, send_sems, recv_sems, 6 * a + 3 + k, sibling).start()
        for k, (cx, cy) in enumerate(chips):
            for a in range(n):
                got = part(a, 2 * cx + cy, 1 - c)
                _remote(got, got, send_sems, recv_sems, 6 * a + 3 + k, (x, y, c)).wait_recv()
        for a in range(n):
            sent = part(a, me, c)
            for k in range(6):
                _remote(sent, sent, send_sems, recv_sems, 6 * a + k, (x, y, c)).wait_send()

    return _pcall(
        body, name="gather_weights", in_specs=[HBM_SPEC] * n, out_specs=tuple([HBM_SPEC] * n),
        out_shape=tuple(_sds((N_CHIPS,) + b.shape, b.dtype) for b in blocks),
        scratch_shapes=[pltpu.SemaphoreType.DMA((6 * n,)), pltpu.SemaphoreType.DMA((6 * n,))],
    )(*blocks)


def _reduce_cores(grads, tag):
    n = len(grads)

    def body(*refs):
        gs, outs, send_sems, recv_sems = refs[:n], refs[n:2 * n], refs[2 * n], refs[2 * n + 1]
        _cores_start(gs, outs, send_sems, recv_sems)
        _cores_wait(gs, outs, send_sems, recv_sems)

    return _pcall(
        body, name=f"reduce_cores_{tag}", in_specs=[HBM_SPEC] * n, out_specs=tuple([HBM_SPEC] * n),
        out_shape=_cores_shapes(grads), scratch_shapes=_cores_sems(n),
    )(*grads)


def _cores_shapes(grads):
    return tuple(_sds((N_CHIPS, h, g.shape[2]), g.dtype) for g, h in zip(grads, _halves(grads)))


def _cores_sems(n):
    return [pltpu.SemaphoreType.DMA((n,)), pltpu.SemaphoreType.DMA((n,))]


def _cores_start(gs, outs, send_sems, recv_sems):
    x, y, c = _place()
    for a, g in enumerate(gs):
        half = g.shape[1] // 2
        for j in range(N_CHIPS):
            _remote(g.at[j, pl.ds((1 - c) * half, half), :], outs[a].at[j], send_sems, recv_sems, a,
                    (x, y, 1 - c)).start()


def _cores_wait(gs, outs, send_sems, recv_sems):
    x, y, c = _place()
    for a, g in enumerate(gs):
        half = g.shape[1] // 2
        _remote(g.at[:, pl.ds((1 - c) * half, half), :], outs[a], send_sems, recv_sems, a, (x, y, c)).wait()


def _scatter_shapes(parts):
    return tuple(_sds((3,) + p.shape[1:], p.dtype) for p in parts)


def _scatter_sems(n):
    return [pltpu.SemaphoreType.DMA((3 * n,)), pltpu.SemaphoreType.DMA((3 * n,))]


def _scatter_start(ps, outs, send_sems, recv_sems):
    x, y, c = _place()
    for a in range(len(ps)):
        for k, (cx, cy) in enumerate(_other_chips(x, y)):
            _remote(ps[a].at[2 * cx + cy], outs[a].at[k], send_sems, recv_sems, 3 * a + k, (cx, cy, c)).start()


def _scatter_wait(ps, outs, send_sems, recv_sems):
    x, y, c = _place()
    for a in range(len(ps)):
        for k in range(3):
            _remote(ps[a].at[k], outs[a].at[k], send_sems, recv_sems, 3 * a + k, (x, y, c)).wait()


def _gather_shapes(blocks):
    return tuple(_sds((N_CHIPS,) + b.shape, b.dtype) for b in blocks)


def _gather_sems(n):
    return [pltpu.SemaphoreType.DMA((3 * n,)), pltpu.SemaphoreType.DMA((3 * n,))]


EXCHANGES = {"chips": (_scatter_start, _scatter_wait, _scatter_shapes, _scatter_sems),
             "cores": (_cores_start, _cores_wait, _cores_shapes, _cores_sems),
             "gather": (lambda srcs, outs, s, r: _gather_start(srcs, outs, _halves(srcs), s, r),
                        lambda srcs, outs, s, r: _gather_wait(outs, _halves(srcs), s, r),
                        _gather_shapes, _gather_sems)}


def _sum_partials(received, parts, place):
    n = len(parts)
    steps = 2
    tiles = [p.shape[1] // steps for p in parts]

    def body(place_ref, *refs):
        rs, ps, outs = refs[:n], refs[n:2 * n], refs[2 * n:]
        for a in range(n):
            tot = ps[a][...].astype(F32)
            for k in range(3):
                tot = tot + rs[a][k].astype(F32)
            outs[a][...] = tot

    cols = [p.shape[2] for p in parts]
    return _pcall(
        body, name="sum_chip_partials",
        grid_spec=pltpu.PrefetchScalarGridSpec(
            num_scalar_prefetch=1, grid=(steps,),
            in_specs=[pl.BlockSpec((3, tm, w), lambda i, pc: (0, i, 0)) for tm, w in zip(tiles, cols)]
            + [pl.BlockSpec((None, tm, w), lambda i, pc: (pc[0], i, 0)) for tm, w in zip(tiles, cols)],
            out_specs=tuple(pl.BlockSpec((tm, w), lambda i, pc: (pc[1] * steps + i, 0)) for tm, w in zip(tiles, cols))),
        out_shape=tuple(_sds((2 * p.shape[1], p.shape[2]), F32) for p in parts),
        compiler_params=_cparams("parallel"))(place, *received, *parts)


def _share_cores(blocks):
    n = len(blocks)
    halves = _halves(blocks)

    def body(*refs):
        srcs, outs, send_sems, recv_sems = refs[:n], refs[n:2 * n], refs[2 * n], refs[2 * n + 1]
        x, y, c = _place()
        for a in range(n):
            piece = pl.ds(c * halves[a], halves[a])
            _remote(srcs[a].at[piece, :], outs[a].at[piece, :], send_sems, recv_sems, a, (x, y, 1 - c)).start()
        for a in range(n):
            mine = outs[a].at[pl.ds(c * halves[a], halves[a]), :]
            theirs = outs[a].at[pl.ds((1 - c) * halves[a], halves[a]), :]
            _remote(mine, theirs, send_sems, recv_sems, a, (x, y, c)).wait()

    return _pcall(
        body, name="share_cores", in_specs=[HBM_SPEC] * n, out_specs=tuple([HBM_SPEC] * n),
        out_shape=tuple(_sds(b.shape, b.dtype) for b in blocks), input_output_aliases={a: a for a in range(n)},
        scratch_shapes=[pltpu.SemaphoreType.DMA((n,)), pltpu.SemaphoreType.DMA((n,))],
    )(*blocks)


def _sum_blocks(stacked, name, tm):
    n, rows, lanes = stacked.shape
    tm = min(tm, rows)

    def body(s_ref, o_ref):
        tot = s_ref[n - 1].astype(F32)
        for k in range(n - 1):
            tot = tot + s_ref[k].astype(F32)
        o_ref[...] = tot

    return _pcall(body, name=name, grid=(rows // tm,),
                  in_specs=[pl.BlockSpec((n, tm, lanes), lambda i: (0, i, 0))],
                  out_specs=pl.BlockSpec((tm, lanes), lambda i: (i, 0)), out_shape=_sds((rows, lanes), F32),
                  compiler_params=_cparams("parallel"))(stacked)


def _add_halves(grads, theirs, core, tag):
    n = len(grads)
    steps = 2
    tiles = [t.shape[1] // steps for t in theirs]
    cols = [t.shape[2] for t in theirs]

    def body(c_ref, *refs):
        gs, ts, outs = refs[:n], refs[n:2 * n], refs[2 * n:]
        for a in range(n):
            outs[a][...] = (gs[a][...] + ts[a][...]).astype(BF16)

    own = [pl.BlockSpec((None, tm, w), lambda k, i, c: (k, c[0] * steps + i, 0)) for tm, w in zip(tiles, cols)]
    same = [pl.BlockSpec((None, tm, w), lambda k, i, c: (k, i, 0)) for tm, w in zip(tiles, cols)]
    return _pcall(
        body, name=f"add_core_halves_{tag}",
        grid_spec=pltpu.PrefetchScalarGridSpec(
            num_scalar_prefetch=1, grid=(N_CHIPS, steps), in_specs=own + same, out_specs=tuple(same)),
        out_shape=tuple(_sds(t.shape, BF16) for t in theirs),
        compiler_params=_cparams("parallel", "parallel"))(core, *grads, *theirs)


def _allreduce_small(part):
    rows, lanes = part.shape
    ndev = 8

    def body(src, tot, buf, send_sems, recv_sems):
        x, y, c = _place()
        me = 4 * x + 2 * y + c
        buf[me] = src[...]
        sends = []
        for k in range(1, ndev):
            peer = (x ^ (k >> 2), y ^ ((k >> 1) & 1), c ^ (k & 1))
            cp = _remote(src, buf.at[me], send_sems, recv_sems, k - 1, peer)
            cp.start()
            sends.append(cp)
        for k in range(1, ndev):
            theirs = buf.at[me ^ k]
            _remote(theirs, theirs, send_sems, recv_sems, k - 1, (x, y, c)).wait_recv()
        for cp in sends:
            cp.wait_send()
        acc = buf[0]
        for d in range(1, ndev):
            acc = acc + buf[d]
        tot[...] = acc

    vm = pl.BlockSpec(memory_space=pltpu.VMEM)
    return _pcall(
        body, name="allreduce_small", in_specs=[vm], out_specs=vm, out_shape=_sds((rows, lanes), F32),
        scratch_shapes=[pltpu.VMEM((ndev, rows, lanes), F32), pltpu.SemaphoreType.DMA((ndev - 1,)),
                        pltpu.SemaphoreType.DMA((ndev - 1,))],
    )(part)


SMALL_USED = sum(r for _, _, r in SMALL)


def _pack_small(vals, loss):
    parts = []
    for name, shape, r in SMALL:
        flat = vals[name].reshape(-1).astype(F32)
        parts.append(jnp.pad(flat, (0, r * LANES - flat.shape[0])).reshape(r, LANES))
    parts.append(jnp.pad(loss.astype(F32), ((0, SMALL_ROWS - SMALL_USED - 1), (0, LANES - 1))))
    return jnp.concatenate(parts, axis=0)


def _unpack_small(packed):
    out, off = {}, 0
    for name, shape, r in SMALL:
        n = int(np.prod(shape))
        out[name] = packed[off:off + r].reshape(-1)[:n].reshape(shape)
        off += r
    return out


def _heads_major(a, nh):
    t = a.shape[0]
    return a.reshape(t, nh, a.shape[1] // nh).transpose(1, 0, 2)


def _tokens_major(a):
    nh, t, w = a.shape
    return a.transpose(1, 0, 2).reshape(t, nh * w)


LATE = ("ffn1_w_gate", "ffn1_w_up", "ffn1_w_down")
EARLY = tuple(name for name, _ in BIG if name not in LATE)


def _local_step(x, target, small, wfull, exchanges=None, later_weights=None):
    t = x.shape[0]
    nh, hd = DIL_HEADS, DIL_HD
    grads_s, grads_b = {}, {}

    x1, ffn1_saved, partly = _ffn_fwd(x, small["ffn1_norm"], wfull["ffn1_w_gate"], wfull["ffn1_w_up"],
                                      wfull["ffn1_w_down"], "ffn1", later_weights[0] if later_weights else ())
    if later_weights:
        wfull = {**wfull, **later_weights[1](partly)}
    w_in = wfull["w_in"].transpose(1, 0, 2).reshape(D_MODEL, -1)
    w_out = wfull["w_out"].reshape(D_MODEL, D_MODEL)
    w_qb, w_kvb = wfull["mla_w_q_b"], wfull["mla_w_kv_b"]
    hm = _rms_fwd(x1, small["mix_norm"], BF16, "mix_norm", 512)
    proj = _mm_simple("in_proj", hm, w_in, NN, F32, tm=1024)
    cq, ckv, k_pe = proj[:, 1536:1792], proj[:, 1792:1920], proj[:, 1920:1984]

    gq, gk = jnp.tile(small["dil_q_norm"], (1, nh)), jnp.tile(small["dil_k_norm"], (1, nh))
    qn = _head_norm_fwd(proj, 0, gq, "dil_q_norm", 512)
    kn = _head_norm_fwd(proj, 1, gk, "dil_k_norm", 512)
    v_d = _head_norm_fwd(proj, 2, None, "dil_v_views", 512)
    bias = _bias_tiles(small["rel_bias"]).reshape(3, nh // 2, 2 * QB, QB + DIL_W)
    outs, lses = [], []
    for b, dil in enumerate(DIL_DILATIONS):
        o_b, lse_b = _dil_fwd(qn[b], kn[b], v_d[b], bias[b], dil, f"dil_fwd_{dil}")
        outs.append(o_b)
        lses.append(lse_b)
    o_dil, lse_tot, od = _dil_merge(outs, lses, small["out_norm_dil"], 512)

    mh = MLA_HEADS
    cos_t, sin_t = _rope_tables(t)
    cqn = _rms_fwd(cq, small["mla_q_a_norm"], BF16, "mla_q_a_norm", 512)
    ckvn = _rms_fwd(ckv, small["mla_kv_a_norm"], BF16, "mla_kv_a_norm", 512)
    tm = min(512, t)

    th = min(2048, t)

    def head_proj(name, a, w, width):
        k = a.shape[1]
        return _mm(name, (mh, t // th, 1),
                   [(a, pl.BlockSpec((th, k), lambda h, i, r: (i, 0)), w, pl.BlockSpec((None, k, width), lambda h, i, r: (h, 0, 0)))],
                   NN, _sds((mh, t, width), F32), pl.BlockSpec((None, th, width), lambda h, i, r: (h, i, 0)), (th, width))

    q_raw = head_proj("mla_q_proj", cqn, w_qb, MLA_QK)
    kv_raw = head_proj("mla_kv_proj", ckvn, w_kvb, MLA_NOPE + MLA_V)
    q_raw2, kv_raw2 = q_raw.reshape(mh * t, MLA_QK), kv_raw.reshape(mh * t, MLA_NOPE + MLA_V)
    q_scale = MLA_QK ** -0.5
    q_m = _mla_qk_fwd(q_raw2, small["mla_q_norm"], cos_t, sin_t, q_scale, "mla_q_rope", 2048).reshape(mh, t, MLA_QK)
    k_m, k_t, v_m = _mla_qk_fwd(kv_raw2, small["mla_k_norm"], cos_t, sin_t, 1.0, "mla_k_rope", 2048, pe=k_pe)
    k_m, v_m = k_m.reshape(mh, t, MLA_QK), v_m.reshape(mh, t, MLA_V)
    o_mla_h, lse_m = _mla_fwd(q_m, k_m, v_m, 512, 4096)
    o_mla = _tokens_major(o_mla_h)

    om = _rms_fwd(o_mla, small["out_norm_mla"], BF16, "out_norm_mla", 512)
    half_w = DIL_WIDTH
    row = pl.BlockSpec((tm, D_MODEL), lambda i, j, r: (i, 0))
    act_spec = pl.BlockSpec((tm, half_w), lambda i, j, r: (i, 0))
    x2 = _mm("out_proj", (t // tm, 1, 1),
             [(od, act_spec, w_out, pl.BlockSpec((half_w, D_MODEL), lambda i, j, r: (0, 0))),
              (om, act_spec, w_out, pl.BlockSpec((half_w, D_MODEL), lambda i, j, r: (1, 0)))],
             NN, _sds((t, D_MODEL), F32), row, (tm, D_MODEL), res=(x1, row))
    (dy, loss), ffn2_saved, _ = _ffn_fwd(x2, small["ffn2_norm"], wfull["ffn2_w_gate"], wfull["ffn2_w_up"],
                                         wfull["ffn2_w_down"], "ffn2", target=target)

    dx2, grads_s["ffn2_norm"], grads_b["ffn2_w_gate"], grads_b["ffn2_w_up"], grads_b["ffn2_w_down"], _, _ = _ffn_bwd(
        dy, x2, small["ffn2_norm"], wfull["ffn2_w_gate"], wfull["ffn2_w_up"], wfull["ffn2_w_down"], ffn2_saved, "ffn2")

    d_ocat = _mm_simple("out_proj_dx", dx2, w_out, NT, F32, tm=1024)
    dw_out_t = _mm_tn_multi("out_proj_dw", dx2, [od, om], 2048)
    grads_b["w_out"] = jnp.concatenate([w.T for w in dw_out_t], axis=0).reshape(N_CHIPS, D_MODEL // N_CHIPS, D_MODEL)
    do_dil, grads_s["out_norm_dil"] = _rms_bwd([d_ocat[:, :half_w]], o_dil, small["out_norm_dil"], None, "out_norm_dil_bwd", 512)
    do_mla, grads_s["out_norm_mla"] = _rms_bwd([d_ocat[:, half_w:]], o_mla, small["out_norm_mla"], None, "out_norm_mla_bwd", 512)

    do_m = _heads_major(do_mla, mh)
    dl_m = _rowdot(do_m.reshape(mh * t, MLA_V), o_mla_h.reshape(mh * t, MLA_V), "mla_delta", 2048).reshape(mh, t, 1)
    dk_m, dv_m, dq_t = _mla_bwd(q_m, k_m, k_t, v_m, do_m, lse_m.reshape(mh, 1, t),
                                dl_m.reshape(mh, 1, t), 2048, 512)
    dq_m = dq_t.transpose(0, 1, 3, 2).reshape(mh, t, MLA_QK)
    dq_raw, grads_s["mla_q_norm"] = _mla_qk_bwd(dq_m.reshape(mh * t, MLA_QK), q_raw2, small["mla_q_norm"],
                                                 cos_t, sin_t, q_scale, "mla_q_rope_bwd", 2048)
    dk_nope, dk_pe_h, grads_s["mla_k_norm"] = _mla_qk_bwd(dk_m.reshape(mh * t, MLA_QK), kv_raw2, small["mla_k_norm"],
                                                          cos_t, sin_t, 1.0, "mla_k_rope_bwd", 2048, pe=k_pe)
    dq_raw = dq_raw.reshape(mh, t, MLA_QK)
    dk_nope = dk_nope.reshape(mh, t, MLA_NOPE)

    def head_proj_dx(name, d, w):
        width, k = d.shape[2], w.shape[1]
        pairs = [(d, pl.BlockSpec((None, th, width), lambda i, j, r, h=h: (h, i, 0)),
                  w, pl.BlockSpec((None, k, width), lambda i, j, r, h=h: (h, 0, 0))) for h in range(mh)]
        return _mm(name, (t // th, 1, 1), pairs, NT, _sds((t, k), F32),
                   pl.BlockSpec((th, k), lambda i, j, r: (i, 0)), (th, k))

    def head_proj_dw(name, a, d):
        width, k = d.shape[2], a.shape[1]
        return _mm(name, (mh, 1, t // th),
                   [(a, pl.BlockSpec((th, k), lambda h, j, r: (r, 0)), d, pl.BlockSpec((None, th, width), lambda h, j, r: (h, r, 0)))],
                   TN, _sds((mh, k, width), F32), pl.BlockSpec((None, k, width), lambda h, j, r: (h, 0, 0)), (k, width))

    d_cqn = head_proj_dx("mla_q_proj_dx", dq_raw, w_qb)
    kv_pairs = []
    for h in range(mh):
        for part, d_part in enumerate((dk_nope, dv_m)):
            kv_pairs.append((d_part, pl.BlockSpec((None, th, MLA_NOPE), lambda i, j, r, h=h: (h, i, 0)),
                             w_kvb, pl.BlockSpec((None, MLA_KV_RANK, MLA_NOPE), lambda i, j, r, h=h, part=part: (h, 0, part))))
    d_ckvn = _mm("mla_kv_proj_dx", (t // th, 1, 1), kv_pairs, NT, _sds((t, MLA_KV_RANK), F32),
                 pl.BlockSpec((th, MLA_KV_RANK), lambda i, j, r: (i, 0)), (th, MLA_KV_RANK))
    grads_b["mla_w_q_b"] = head_proj_dw("mla_q_proj_dw", cqn, dq_raw)
    grads_b["mla_w_kv_b"] = jnp.concatenate([head_proj_dw("mla_k_proj_dw", ckvn, dk_nope),
                                             head_proj_dw("mla_v_proj_dw", ckvn, dv_m)], axis=2)
    d_cq, grads_s["mla_q_a_norm"] = _rms_bwd([d_cqn], cq, small["mla_q_a_norm"], None, "mla_q_a_norm_bwd", 512)
    d_ckv, grads_s["mla_kv_a_norm"] = _rms_bwd([d_ckvn], ckv, small["mla_kv_a_norm"], None, "mla_kv_a_norm_bwd", 512)
    d_kpe = _sum_blocks(dk_pe_h.reshape(mh, t * MLA_ROPE // LANES, LANES), "mla_kpe_sum", 1024).reshape(t, MLA_ROPE)

    stats, do_db = _dil_stats(do_dil, o_dil, lse_tot, 512)
    dqs, dks, dvs, dtiles = [], [], [], []
    for b, dil in enumerate(DIL_DILATIONS):
        dq_b, dk_b, dv_b, db_b = _dil_bwd(qn[b], kn[b], v_d[b], do_db[b], stats[b], bias[b], dil, f"dil_bwd_{dil}")
        dqs.append(dq_b)
        dks.append(dk_b)
        dvs.append(dv_b)
        dtiles.append(db_b)
    grads_s["rel_bias"] = _bias_grad(jnp.stack(dtiles).reshape(3, nh, QB, QB + DIL_W))
    dq_a, dgq = _head_norm_bwd(dqs, proj, 0, gq, "dil_q_norm_bwd", 512)
    dk_a, dgk = _head_norm_bwd(dks, proj, 1, gk, "dil_k_norm_bwd", 512)
    grads_s["dil_q_norm"], grads_s["dil_k_norm"] = dgq[:, :hd], dgk[:, :hd]
    dv_a = _sum_branches(dvs, "dil_dv_sum", 512)

    dparts = [dq_a, dk_a, dv_a, d_cq, d_ckv, d_kpe]
    t2 = min(512, t)
    pairs, lo = [], 0
    for dpart in dparts:
        width = dpart.shape[1]
        w_part = w_in[:, lo:lo + width]
        pairs.append((dpart, pl.BlockSpec((t2, width), lambda i, j, r: (i, 0)),
                      w_part, pl.BlockSpec((D_MODEL, width), lambda i, j, r: (0, 0))))
        lo += width
    dw_in = jnp.concatenate(_mm_tn_multi("in_proj_dw", hm, dparts, 1024), axis=1)
    grads_b["w_in"] = dw_in.reshape(D_MODEL, N_CHIPS, -1).transpose(1, 0, 2)
    early = tuple(grads_b[n] for n in EARLY) if exchanges else ()
    row2 = pl.BlockSpec((t2, D_MODEL), lambda i, j, r: (i, 0))
    res = _mm("in_proj_dx", (t // t2, 1, 1), pairs, NT, _sds((t, D_MODEL), F32), row2, (t2, D_MODEL),
              res=(dx2, row2), norm=(x1, small["mix_norm"]), outgoing=early, exchange="cores")
    dx1, grads_s["mix_norm"] = res[0], res[1]
    outgoing = exchanges[0](early, res[2]) if exchanges else ()
    dx, grads_s["ffn1_norm"], grads_b["ffn1_w_gate"], grads_b["ffn1_w_up"], grads_b["ffn1_w_down"], arrived, late = _ffn_bwd(
        dx1, x, small["ffn1_norm"], wfull["ffn1_w_gate"], wfull["ffn1_w_up"], wfull["ffn1_w_down"], ffn1_saved, "ffn1",
        outgoing, exchanges[1] if exchanges else None)
    return loss, dx, grads_s, grads_b, (tuple(outgoing), arrived), late


def kernel(x, ffn1_norm, ffn1_w_gate, ffn1_w_up, ffn1_w_down, mix_norm, w_in, dil_q_norm, dil_k_norm, rel_bias, mla_q_a_norm, mla_w_q_b, mla_kv_a_norm, mla_w_kv_b, mla_q_norm, mla_k_norm, out_norm_dil, out_norm_mla, w_out, ffn2_norm, ffn2_w_gate, ffn2_w_up, ffn2_w_down, loss_target, m_ffn1_norm, m_ffn1_w_gate, m_ffn1_w_up, m_ffn1_w_down, m_mix_norm, m_w_in, m_dil_q_norm, m_dil_k_norm, m_rel_bias, m_mla_q_a_norm, m_mla_w_q_b, m_mla_kv_a_norm, m_mla_w_kv_b, m_mla_q_norm, m_mla_k_norm, m_out_norm_dil, m_out_norm_mla, m_w_out, m_ffn2_norm, m_ffn2_w_gate, m_ffn2_w_up, m_ffn2_w_down, v_ffn1_norm, v_ffn1_w_gate, v_ffn1_w_up, v_ffn1_w_down, v_mix_norm, v_w_in, v_dil_q_norm, v_dil_k_norm, v_rel_bias, v_mla_q_a_norm, v_mla_w_q_b, v_mla_kv_a_norm, v_mla_w_kv_b, v_mla_q_norm, v_mla_k_norm, v_out_norm_dil, v_out_norm_mla, v_w_out, v_ffn2_norm, v_ffn2_w_gate, v_ffn2_w_up, v_ffn2_w_down):
    given = dict(locals())
    big_names = [name for name, _ in BIG]
    small_names = [name for name, _, _ in SMALL]

    chip = (2 * lax.axis_index("x") + lax.axis_index("y")).astype(jnp.int32)
    core = lax.axis_index("c").astype(jnp.int32)
    mine = {n: given[n].astype(BF16) for n in big_names}

    def with_own(names, arrays):
        return {n: lax.dynamic_update_slice(a, mine[n], (chip, 0, 0)) for n, a in zip(names, arrays)}

    wfirst = with_own(LATE, _gather_weights([mine[n][0] for n in LATE]))
    later_weights = ([mine[n][0] for n in EARLY], lambda partly: with_own(EARLY, _forward_cores(partly)))
    small = {n: given[n] for n in small_names}

    def early_partials(partial, theirs):
        return _add_halves(partial, theirs, core.reshape(1), "early")

    def late_partials(partial):
        return _add_halves(partial, _reduce_cores(partial, "late"), core.reshape(1), "late")

    exchanges = (early_partials, late_partials)
    loss, dx, grads_s, grads_b, (early_part, early_got), (late_part, late_got) = _local_step(
        x[0], loss_target[0], small, wfirst, exchanges, later_weights)
    reduced = _sum_partials(tuple(late_got) + tuple(early_got), tuple(late_part) + tuple(early_part),
                            jnp.stack([chip, core]))
    g_big = dict(zip(LATE + EARLY, _share_cores(reduced)))
    summed = _allreduce_small(_pack_small(grads_s, loss))
    g_small = _unpack_small(summed)
    loss = summed[SMALL_USED, 0]

    grad, delta, new_m, new_v = {}, {}, {}, {}
    for name, shape in BIG:
        g2 = g_big[name]
        d_, m_, v_ = _adamw(given[name].reshape(shape), g2, given["m_" + name].reshape(shape),
                            given["v_" + name].reshape(shape), f"adamw_{name}")
        full = given[name].shape
        grad[name], delta[name], new_m[name], new_v[name] = (a.reshape(full) for a in (g2, d_, m_, v_))
    for name in small_names:
        grad[name] = g_small[name]
        delta[name], new_m[name], new_v[name] = _adamw(given[name], g_small[name], given["m_" + name],
                                                       given["v_" + name], f"adamw_{name}")

    return (loss, dx[None], *[grad[n] for n in WEIGHTS], *[delta[n] for n in WEIGHTS],
            *[new_m[n] for n in WEIGHTS], *[new_v[n] for n in WEIGHTS])
```

```python
import functools

import numpy as np
import jax
import jax.numpy as jnp
from jax import lax
from jax.experimental import pallas as pl
from jax.experimental.pallas import tpu as pltpu

F32 = jnp.float32
BF16 = jnp.bfloat16

D_MODEL = 1024
D_FF = 2816
N_CHIPS = 4
DIL_HEADS = 8
DIL_HD = 64
DIL_WIDTH = 512
DIL_DILATIONS = (1, 4, 16)
DIL_W = 128
QB = 128
MLA_HEADS = 4
MLA_NOPE = 128
MLA_ROPE = 64
MLA_QK = 192
MLA_V = 128
MLA_Q_RANK = 256
MLA_KV_RANK = 128
ROPE_BASE = 10000.0
REL_BUCKETS = 32
REL_MAX_DIST = 2048
FFN_RESID = 0.5
EPS = 1e-6
NEG = -1e30
LANES = 128

ADAM_LR = 0.001
ADAM_B1 = 0.9
ADAM_B2 = 0.999
ADAM_EPS = 1e-08
ADAM_WD = 0.01
ADAM_STEP = 10

NT = (((1,), (1,)), ((), ()))
NN = (((1,), (0,)), ((), ()))
TN = (((0,), (0,)), ((), ()))

BIG = (
    ("ffn1_w_gate", (D_MODEL, D_FF // N_CHIPS)),
    ("ffn1_w_up", (D_MODEL, D_FF // N_CHIPS)),
    ("ffn1_w_down", (D_FF // N_CHIPS, D_MODEL)),
    ("w_in", (D_MODEL, 1984 // N_CHIPS)),
    ("mla_w_q_b", (MLA_Q_RANK, MLA_QK)),
    ("mla_w_kv_b", (MLA_KV_RANK, MLA_NOPE + MLA_V)),
    ("w_out", (D_MODEL // N_CHIPS, D_MODEL)),
    ("ffn2_w_gate", (D_MODEL, D_FF // N_CHIPS)),
    ("ffn2_w_up", (D_MODEL, D_FF // N_CHIPS)),
    ("ffn2_w_down", (D_FF // N_CHIPS, D_MODEL)),
)
SMALL = (
    ("ffn1_norm", (1, 1024), 8), ("mix_norm", (1, 1024), 8), ("dil_q_norm", (1, 64), 1),
    ("dil_k_norm", (1, 64), 1), ("rel_bias", (8, 32), 2), ("mla_q_a_norm", (1, 256), 2),
    ("mla_kv_a_norm", (1, 128), 1), ("mla_q_norm", (1, 192), 2), ("mla_k_norm", (1, 192), 2),
    ("out_norm_dil", (1, 512), 4), ("out_norm_mla", (1, 512), 4), ("ffn2_norm", (1, 1024), 8),
)
SMALL_ROWS = 48
WEIGHTS = ("ffn1_norm", "ffn1_w_gate", "ffn1_w_up", "ffn1_w_down", "mix_norm", "w_in", "dil_q_norm",
           "dil_k_norm", "rel_bias", "mla_q_a_norm", "mla_w_q_b", "mla_kv_a_norm", "mla_w_kv_b",
           "mla_q_norm", "mla_k_norm", "out_norm_dil", "out_norm_mla", "w_out", "ffn2_norm",
           "ffn2_w_gate", "ffn2_w_up", "ffn2_w_down")


def _pcall(body, **kw):
    return pl.pallas_call(body, **kw)


def _cparams(*sem):
    return pltpu.CompilerParams(dimension_semantics=sem)


def _sds(shape, dtype):
    return jax.ShapeDtypeStruct(shape, dtype)


def _dot(a, b, dn):
    return lax.dot_general(a, b, dn, preferred_element_type=F32)


def _rms_fwd(x, g, out_dtype, name, tm):
    n, d = x.shape
    tm = min(tm, n)

    def body(x_ref, g_ref, o_ref):
        xf = x_ref[...].astype(F32)
        r = lax.rsqrt(jnp.mean(xf * xf, axis=-1, keepdims=True) + EPS)
        o_ref[...] = (xf * r * g_ref[...]).astype(o_ref.dtype)

    return _pcall(
        body, name=name, grid=(n // tm,),
        in_specs=[pl.BlockSpec((tm, d), lambda i: (i, 0)), pl.BlockSpec((1, d), lambda i: (0, 0))],
        out_specs=pl.BlockSpec((tm, d), lambda i: (i, 0)),
        out_shape=_sds((n, d), out_dtype), compiler_params=_cparams("parallel"))(x, g)


def _rms_bwd(dys, x, g, res, name, tm):
    n, d = x.shape
    tm = min(tm, n)
    nd = len(dys)
    has_res = res is not None

    def body(*refs):
        dy_refs = refs[:nd]
        x_ref, g_ref = refs[nd], refs[nd + 1]
        res_ref = refs[nd + 2] if has_res else None
        dx_ref, dg_ref = refs[-2], refs[-1]
        dy = dy_refs[0][...].astype(F32)
        for r_ in dy_refs[1:]:
            dy = dy + r_[...].astype(F32)
        xf = x_ref[...].astype(F32)
        r = lax.rsqrt(jnp.mean(xf * xf, axis=-1, keepdims=True) + EPS)
        xh = xf * r
        dxh = dy * g_ref[...]
        dx = r * (dxh - xh * jnp.mean(dxh * xh, axis=-1, keepdims=True))
        if has_res:
            dx = dx + res_ref[...]
        dx_ref[...] = dx

        @pl.when(pl.program_id(0) == 0)
        def _():
            dg_ref[...] = jnp.zeros_like(dg_ref)

        dg_ref[...] += jnp.sum(dy * xh, axis=0, keepdims=True)

    row = pl.BlockSpec((tm, d), lambda i: (i, 0))
    vec = pl.BlockSpec((1, d), lambda i: (0, 0))
    ins = list(dys) + [x, g] + ([res] if has_res else [])
    return _pcall(
        body, name=name, grid=(n // tm,),
        in_specs=[row] * nd + [row, vec] + ([row] if has_res else []),
        out_specs=(row, vec),
        out_shape=(_sds((n, d), F32), _sds((1, d), F32)),
        compiler_params=_cparams("arbitrary"))(*ins)


def _mm(name, grid, pairs, dn, out_shape, out_spec, acc_shape, res=None, scale=1.0, outgoing=(), norm=None,
        exchange="chips"):
    npairs = len(pairs)
    nred = grid[2]
    has_res = res is not None
    has_norm = norm is not None
    no = len(outgoing)
    ex_start, ex_wait, ex_shapes, ex_sems = EXCHANGES[exchange]

    def body(*refs):
        ab = refs[:2 * npairs]
        res_ref = refs[2 * npairs] if has_res else None
        nin = 2 * npairs + int(has_res) + 2 * int(has_norm)
        if has_norm:
            x_ref, g_ref = refs[nin - 2:nin]
        first_out = nin + no
        sent = refs[nin:first_out]
        o_ref = refs[first_out]
        nout = 1 + int(has_norm)
        dg_ref = refs[first_out + 1] if has_norm else None
        arrived = refs[first_out + nout:first_out + nout + no]
        acc_ref = refs[first_out + nout + no] if nred > 1 else None
        if no:
            send_sems, recv_sems = refs[-2:]
            ids = [pl.program_id(n) for n in range(3)]

            @pl.when((ids[0] == 0) & (ids[1] == 0) & (ids[2] == 0))
            def _():
                ex_start(sent, arrived, send_sems, recv_sems)

        tot = None
        for p in range(npairs):
            d = _dot(ab[2 * p][...].astype(BF16), ab[2 * p + 1][...].astype(BF16), dn)
            tot = d if tot is None else tot + d

        def finish(v):
            if scale != 1.0:
                v = v * scale
            if has_norm:
                xf = x_ref[...]
                r = lax.rsqrt(jnp.mean(xf * xf, axis=-1, keepdims=True) + EPS)
                xh = xf * r
                dxh = v * g_ref[...]

                @pl.when(pl.program_id(0) == 0)
                def _():
                    dg_ref[...] = jnp.zeros_like(dg_ref)

                dg_ref[...] += jnp.sum(v * xh, axis=0, keepdims=True)
                v = r * (dxh - xh * jnp.mean(dxh * xh, axis=-1, keepdims=True))
            if has_res:
                v = res_ref[...] + v
            o_ref[...] = v.astype(o_ref.dtype)

        if nred == 1:
            finish(tot)
        else:
            r = pl.program_id(2)

            @pl.when(r == 0)
            def _():
                acc_ref[...] = tot

            @pl.when(r > 0)
            def _():
                acc_ref[...] += tot

            @pl.when(r == nred - 1)
            def _():
                finish(acc_ref[...])

        if no:
            @pl.when((ids[0] == grid[0] - 1) & (ids[1] == grid[1] - 1) & (ids[2] == nred - 1))
            def _():
                ex_wait(sent, arrived, send_sems, recv_sems)

    ins, specs = [], []
    for a, a_spec, b, b_spec in pairs:
        ins += [a, b]
        specs += [a_spec, b_spec]
    if has_res:
        ins.append(res[0])
        specs.append(res[1])
    scratch = [pltpu.VMEM(acc_shape, F32)] if nred > 1 else []
    if not no and not has_norm:
        return _pcall(
            body, name=name, grid=grid, in_specs=specs, out_specs=out_spec, out_shape=out_shape,
            scratch_shapes=scratch, compiler_params=_cparams("parallel", "parallel", "arbitrary"))(*ins)
    out_specs, out_shapes = (out_spec,), (out_shape,)
    if has_norm:
        assert grid[1] == 1
        d = norm[1].shape[1]
        ins += [norm[0], norm[1]]
        specs += [out_spec, pl.BlockSpec((1, d), lambda i, j, r: (0, 0))]
        out_specs += (pl.BlockSpec((1, d), lambda i, j, r: (0, 0)),)
        out_shapes += (_sds((1, d), F32),)
    hbm = pl.BlockSpec(memory_space=pltpu.HBM)
    res_ = tuple(_pcall(
        body, name=name, grid=grid, in_specs=specs + [hbm] * no, out_specs=out_specs + (hbm,) * no,
        out_shape=out_shapes + ex_shapes(outgoing),
        scratch_shapes=scratch + (ex_sems(no) if no else []),
        compiler_params=_cparams("arbitrary", "arbitrary", "arbitrary"))(*ins, *outgoing))
    nout = len(out_shapes)
    return res_[:nout] + ((res_[nout:],) if no else ())


def _ffn_up(h, wg, wu, name, tm, incoming=()):
    t, d = h.shape
    nc, _, fs = wg.shape
    tm = min(tm, t)
    nt = t // tm
    ni = len(incoming)
    halves = _halves(incoming)

    def body(*refs):
        h_ref, wg_ref, wu_ref = refs[:3]
        srcs = refs[3:3 + ni]
        g_ref, u_ref, a_ref = refs[3 + ni:6 + ni]
        outs = refs[6 + ni:6 + 2 * ni]
        if ni:
            send_sems, recv_sems = refs[6 + 2 * ni:]
            c, i = pl.program_id(0), pl.program_id(1)

            @pl.when((c == 0) & (i == 0))
            def _():
                _gather_start(srcs, outs, halves, send_sems, recv_sems)

        hh = h_ref[...]
        gate = _dot(hh, wg_ref[...], NN)
        up = _dot(hh, wu_ref[...], NN)
        sig = jax.nn.sigmoid(gate)
        silu = gate * sig
        g_ref[...] = (up * (sig + silu * (1.0 - sig))).astype(BF16)
        u_ref[...] = silu.astype(BF16)
        a_ref[...] = (silu * up).astype(BF16)

        if ni:
            @pl.when((c == nc - 1) & (i == nt - 1))
            def _():
                _gather_wait(outs, halves, send_sems, recv_sems)

    wspec = pl.BlockSpec((None, d, fs), lambda c, i: (c, 0, 0))
    ospec = pl.BlockSpec((None, tm, fs), lambda c, i: (c, i, 0))
    hbm = pl.BlockSpec(memory_space=pltpu.HBM)
    osd = _sds((nc, t, fs), BF16)
    res = tuple(_pcall(
        body, name=name, grid=(nc, nt),
        in_specs=[pl.BlockSpec((tm, d), lambda c, i: (i, 0)), wspec, wspec] + [hbm] * ni,
        out_specs=(ospec, ospec, ospec) + (hbm,) * ni,
        out_shape=(osd, osd, osd) + tuple(_sds((N_CHIPS,) + b.shape, b.dtype) for b in incoming),
        scratch_shapes=[pltpu.SemaphoreType.DMA((3 * ni,)), pltpu.SemaphoreType.DMA((3 * ni,))] if ni else [],
        compiler_params=_cparams("arbitrary", "arbitrary"))(h, wg, wu, *incoming))
    return res[:3] + (res[3:],)


def _ffn_hidden_bwd(dy, h, wd, dact_dgate, dact_dup, act, name, tm, outgoing=()):
    t, d = dy.shape
    nc, fs, _ = wd.shape
    tm = min(tm, t)
    nt = t // tm
    no = len(outgoing)

    def body(*refs):
        dy_ref, h_ref, wd_ref, g_ref, u_ref, a_ref = refs[:6]
        sent = refs[6:6 + no]
        dg_ref, du_ref, dwg_hbm, dwu_hbm, dwd_hbm = refs[6 + no:11 + no]
        arrived = refs[11 + no:11 + 2 * no]
        wg_acc, wu_acc, wd_acc, sem = refs[11 + 2 * no:15 + 2 * no]
        c, i = pl.program_id(0), pl.program_id(1)
        if no:
            send_sems, recv_sems = refs[15 + 2 * no:]

            @pl.when((c == 0) & (i == 0))
            def _():
                _scatter_start(sent, arrived, send_sems, recv_sems)

        dyb = dy_ref[...].astype(BF16)
        da = _dot(dyb, wd_ref[...], NT) * FFN_RESID
        dgate = (da * g_ref[...].astype(F32)).astype(BF16)
        dup = (da * u_ref[...].astype(F32)).astype(BF16)
        dg_ref[...] = dgate
        du_ref[...] = dup
        hh = h_ref[...]
        parts = (_dot(hh, dgate, TN), _dot(hh, dup, TN), _dot(a_ref[...], dyb, TN) * FFN_RESID)
        accs = (wg_acc, wu_acc, wd_acc)

        @pl.when(i == 0)
        def _():
            for acc, part in zip(accs, parts):
                acc[...] = part

        @pl.when(i > 0)
        def _():
            for acc, part in zip(accs, parts):
                acc[...] += part

        @pl.when(i == nt - 1)
        def _():
            copies = [pltpu.make_async_copy(acc, out.at[c], sem.at[n])
                      for n, (acc, out) in enumerate(zip(accs, (dwg_hbm, dwu_hbm, dwd_hbm)))]
            for cp in copies:
                cp.start()
            for cp in copies:
                cp.wait()

        if no:
            @pl.when((c == nc - 1) & (i == nt - 1))
            def _():
                _scatter_wait(sent, arrived, send_sems, recv_sems)

    tok = pl.BlockSpec((tm, d), lambda c, i: (i, 0))
    cspec = pl.BlockSpec((None, tm, fs), lambda c, i: (c, i, 0))
    hbm = pl.BlockSpec(memory_space=pltpu.HBM)
    osd = _sds((nc, t, fs), BF16)
    res = _pcall(
        body, name=name, grid=(nc, nt),
        in_specs=[tok, tok, pl.BlockSpec((None, fs, d), lambda c, i: (c, 0, 0)), cspec, cspec, cspec] + [hbm] * no,
        out_specs=(cspec, cspec, hbm, hbm, hbm) + (hbm,) * no,
        out_shape=(osd, osd, _sds((nc, d, fs), F32), _sds((nc, d, fs), F32), _sds((nc, fs, d), F32))
        + _scatter_shapes(outgoing),
        scratch_shapes=[pltpu.VMEM((d, fs), F32), pltpu.VMEM((d, fs), F32), pltpu.VMEM((fs, d), F32),
                        pltpu.SemaphoreType.DMA((3,))] + (_scatter_sems(no) if no else []),
        compiler_params=_cparams("arbitrary", "arbitrary"))(dy, h, wd, dact_dgate, dact_dup, act, *outgoing)
    res = tuple(res)
    return res[:5] + (res[5:],)


def _ffn_fwd(x, g, wg, wu, wd, tag, incoming=(), target=None):
    t = x.shape[0]
    nc, _, fs = wg.shape
    tm = min(512, t)
    h = _rms_fwd(x, g, BF16, f"{tag}_norm", 1024)
    behind_down = tuple(incoming[-1:])
    dact_dgate, dact_dup, act, partly = _ffn_up(h, wg, wu, f"{tag}_up", 1024, tuple(incoming[:-1]))
    if target is not None:
        return _ffn_down_loss(act, wd, x, target, f"{tag}_down_loss", 512), (h, dact_dgate, dact_dup, act), partly
    pairs = [(act, pl.BlockSpec((None, tm, fs), lambda i, j, r, c=c: (c, i, 0)),
              wd, pl.BlockSpec((None, fs, D_MODEL), lambda i, j, r, c=c: (c, 0, 0))) for c in range(nc)]
    row = pl.BlockSpec((tm, D_MODEL), lambda i, j, r: (i, 0))
    y = _mm(f"{tag}_down", (t // tm, 1, 1), pairs, NN, _sds((t, D_MODEL), F32), row, (tm, D_MODEL),
            res=(x, row), scale=FFN_RESID, outgoing=behind_down, exchange="gather")
    if behind_down:
        y, more = y
        partly = tuple(partly) + tuple(more)
    return y, (h, dact_dgate, dact_dup, act), partly


def _ffn_bwd(dy, x, g, wg, wu, wd, saved, tag, outgoing=(), own_exchange=None):
    h, dact_dgate, dact_dup, act = saved
    t = x.shape[0]
    nc, _, fs = wg.shape
    tm = min(512, t)
    dgate, dup, dwg, dwu, dwd, arrived = _ffn_hidden_bwd(dy, h, wd, dact_dgate, dact_dup, act,
                                                         f"{tag}_hidden_bwd", 1024, outgoing)
    pairs = []
    for c in range(nc):
        a_spec = pl.BlockSpec((None, tm, fs), lambda i, j, r, c=c: (c, i, 0))
        w_spec = pl.BlockSpec((None, D_MODEL, fs), lambda i, j, r, c=c: (c, 0, 0))
        pairs += [(dgate, a_spec, wg, w_spec), (dup, a_spec, wu, w_spec)]
    own_part = tuple(own_exchange([dwg, dwu, dwd])) if own_exchange else ()
    row = pl.BlockSpec((tm, D_MODEL), lambda i, j, r: (i, 0))
    res = _mm(f"{tag}_dh", (t // tm, 1, 1), pairs, NT, _sds((t, D_MODEL), F32), row, (tm, D_MODEL),
              res=(dy, row), norm=(x, g), outgoing=own_part)
    dx, dg = res[0], res[1]
    own_got = res[2] if own_part else ()
    return dx, dg, dwg, dwu, dwd, arrived, (own_part, own_got)


def _mm_tn_multi(name, a, bs, tk):
    k, m = a.shape
    tk = min(tk, k)
    nb = len(bs)

    def body(*refs):
        a_ref, b_refs, o_refs = refs[0], refs[1:1 + nb], refs[1 + nb:]
        aa = a_ref[...].astype(BF16)
        parts = [_dot(aa, b_ref[...].astype(BF16), TN) for b_ref in b_refs]

        @pl.when(pl.program_id(0) == 0)
        def _():
            for o_ref, part in zip(o_refs, parts):
                o_ref[...] = part

        @pl.when(pl.program_id(0) > 0)
        def _():
            for o_ref, part in zip(o_refs, parts):
                o_ref[...] += part

    return _pcall(
        body, name=name, grid=(k // tk,),
        in_specs=[pl.BlockSpec((tk, m), lambda r: (r, 0))] + [pl.BlockSpec((tk, b.shape[1]), lambda r: (r, 0)) for b in bs],
        out_specs=tuple(pl.BlockSpec((m, b.shape[1]), lambda r: (0, 0)) for b in bs),
        out_shape=tuple(_sds((m, b.shape[1]), F32) for b in bs),
        compiler_params=_cparams("arbitrary"))(a, *bs)


def _mm_simple(name, a, b, dn, out_dtype, tm=512, tk=512, res=None, scale=1.0):
    if dn == TN:
        k, m = a.shape
        n = b.shape[1]
        tk = min(tk, k)
        return _mm(name, (1, 1, k // tk),
                   [(a, pl.BlockSpec((tk, m), lambda i, j, r: (r, 0)), b, pl.BlockSpec((tk, n), lambda i, j, r: (r, 0)))],
                   TN, _sds((m, n), out_dtype), pl.BlockSpec((m, n), lambda i, j, r: (0, 0)), (m, n), scale=scale)
    m, k = a.shape
    n = b.shape[1] if dn == NN else b.shape[0]
    tm = min(tm, m)
    row = pl.BlockSpec((tm, n), lambda i, j, r: (i, 0))
    return _mm(name, (m // tm, 1, 1),
               [(a, pl.BlockSpec((tm, k), lambda i, j, r: (i, 0)), b, pl.BlockSpec(b.shape, lambda i, j, r: (0, 0)))],
               dn, _sds((m, n), out_dtype), row, (tm, n), res=None if res is None else (res, row), scale=scale)


def _t5_bucket(dist):
    max_exact = REL_BUCKETS // 2
    d = np.maximum(dist, 1).astype(np.float32)
    large = max_exact + (np.log(d / max_exact) / np.log(REL_MAX_DIST / max_exact)
                         * (REL_BUCKETS - max_exact)).astype(np.int32)
    large = np.minimum(large, REL_BUCKETS - 1)
    return np.where(dist < max_exact, dist, large).astype(np.int32)


def _bucket_tiles():
    i = np.arange(QB)[:, None]
    j = np.arange(QB + DIL_W)[None, :]
    delta = np.clip(i + DIL_W - j, 0, None)
    return np.stack([_t5_bucket(delta * dil) for dil in DIL_DILATIONS]).astype(np.int32)


def _bias_tiles(rel_bias):
    buckets = jnp.asarray(_bucket_tiles())

    def body(rb_ref, bk_ref, o_ref):
        bk = bk_ref[...]
        for h in range(DIL_HEADS):
            def pick(b, tile):
                return jnp.where(bk == b, rb_ref[h, b], tile)

            o_ref[h] = lax.fori_loop(0, REL_BUCKETS, pick, jnp.zeros((QB, QB + DIL_W), F32))

    return _pcall(
        body, name="dil_bias_tiles", grid=(3,),
        in_specs=[pl.BlockSpec(memory_space=pltpu.SMEM),
                  pl.BlockSpec((None, QB, QB + DIL_W), lambda b: (b, 0, 0))],
        out_specs=pl.BlockSpec((None, DIL_HEADS, QB, QB + DIL_W), lambda b: (b, 0, 0, 0)),
        out_shape=_sds((3, DIL_HEADS, QB, QB + DIL_W), F32),
        compiler_params=_cparams("parallel"))(rel_bias, buckets)


def _bias_grad(dtiles):
    buckets = jnp.asarray(_bucket_tiles())

    def body(dt_ref, bk_ref, o_ref):
        def one(b, carry):
            hit = [bk_ref[br] == b for br in range(3)]
            for h in range(DIL_HEADS):
                tot = jnp.zeros((), F32)
                for br in range(3):
                    tot = tot + jnp.sum(jnp.where(hit[br], dt_ref[br, h], 0.0))
                o_ref[h, b] = tot
            return carry

        lax.fori_loop(0, REL_BUCKETS, one, 0)

    return _pcall(
        body, name="dil_bias_grad",
        in_specs=[pl.BlockSpec(memory_space=pltpu.VMEM), pl.BlockSpec(memory_space=pltpu.VMEM)],
        out_specs=pl.BlockSpec(memory_space=pltpu.SMEM),
        out_shape=_sds((DIL_HEADS, REL_BUCKETS), F32))(dtiles, buckets)


def _split_heads(a, lo):
    zero = jnp.zeros_like(a)
    return jnp.concatenate([jnp.where(lo, a, zero), jnp.where(lo, zero, a)], axis=0)


def _side_by_side(a):
    n = a.shape[0] // 2
    return jnp.concatenate([a[:n], a[n:]], axis=1)


def _band_masks(prev_ok):
    ii = lax.broadcasted_iota(jnp.int32, (2 * QB, QB), 0) & (QB - 1)
    jj = lax.broadcasted_iota(jnp.int32, (2 * QB, QB), 1)
    return jj <= ii, jj >= ii + jnp.where(prev_ok, 0, QB)


def _dil_fwd(q, k, v, bias, dil, name):
    w = DIL_WIDTH
    t = q.shape[0] * dil
    npair = w // LANES
    nl = t // dil // QB
    scale = DIL_HD ** -0.5

    def body(q_ref, kc_ref, kp_ref, vc_ref, vp_ref, b_ref, o_ref, lse_ref):
        nn = pl.program_id(1)
        lo = lax.broadcasted_iota(jnp.int32, (QB, LANES), 1) < DIL_HD
        lo2 = lax.broadcasted_iota(jnp.int32, (2 * QB, LANES), 1) < DIL_HD
        ii = lax.broadcasted_iota(jnp.int32, (2 * QB, 2 * QB), 0) & (QB - 1)
        jj = lax.broadcasted_iota(jnp.int32, (2 * QB, 2 * QB), 1)
        first_key = jnp.maximum(ii, jnp.where(nn != 0, 0, QB))
        valid = (jj >= first_key) & (jj <= ii + QB)
        for p in range(npair):
            cols = slice(p * LANES, (p + 1) * LANES)
            qq = _split_heads(q_ref[:, cols], lo)
            kk = jnp.concatenate([kp_ref[:, cols], kc_ref[:, cols]], axis=0)
            vv = jnp.concatenate([vp_ref[:, cols], vc_ref[:, cols]], axis=0)
            s = jnp.where(valid, _dot(qq, kk, NT) * scale + b_ref[p], NEG)
            m = jnp.max(s, axis=-1, keepdims=True)
            e = jnp.exp(s - m)
            den = jnp.sum(e, axis=-1, keepdims=True)
            pn = (e * (1.0 / den)).astype(BF16)
            o_ref[:, cols] = _dot(_side_by_side(pn), _split_heads(vv, lo2), NN)
            lse = m + jnp.log(den)
            lse_ref[:, cols] = jnp.where(lo, lse[:QB], lse[QB:])

    cur = pl.BlockSpec((QB, w), lambda r, n: (n, r))
    prev = pl.BlockSpec((QB, w), lambda r, n: (jnp.maximum(n - 1, 0), r))
    sd = _sds((t // dil, dil * w), F32)
    return _pcall(
        body, name=name, grid=(dil, nl),
        in_specs=[cur, cur, prev, cur, prev, pl.BlockSpec((npair, 2 * QB, 2 * QB), lambda r, n: (0, 0, 0))],
        out_specs=(cur, cur), out_shape=(sd, sd),
        compiler_params=_cparams("parallel", "parallel"))(q, k, k, v, v, bias)


def _dil_bwd(q, k, v, do, stats, bias, dil, name):
    w = DIL_WIDTH
    t = q.shape[0] * dil
    npair = w // LANES
    nl = t // dil // QB
    scale = DIL_HD ** -0.5

    def body(qc_ref, qn_ref, doc_ref, don_ref, sc_ref, sn_ref, k_ref, v_ref, b_ref,
             dq_ref, dk_ref, dv_ref, db_ref, carry):
        r, nn = pl.program_id(0), pl.program_id(1)
        lo = lax.broadcasted_iota(jnp.int32, (QB, LANES), 1) < DIL_HD
        cur_ok, prev_ok = _band_masks(nn + 1 < nl)

        @pl.when((r == 0) & (nn == 0))
        def _():
            db_ref[...] = jnp.zeros_like(db_ref)
            carry[...] = jnp.zeros_like(carry)

        for p in range(npair):
            cols = slice(p * LANES, (p + 1) * LANES)
            kp, vp = k_ref[:, cols], v_ref[:, cols]
            k2 = _split_heads(kp, lo)

            def column(ref, lane):
                first = p * LANES + lane
                return jnp.concatenate([ref[:, first:first + 1], ref[:, first + DIL_HD:first + DIL_HD + 1]], axis=0)

            def side(q_ref, do_ref, s_ref, bias, ok):
                qq = _split_heads(q_ref[:, cols], lo)
                dd = _split_heads(do_ref[:, cols], lo)
                s = jnp.where(ok, _dot(qq, kp, NT) * scale + bias, NEG)
                prob = jnp.exp(s - column(s_ref, 0))
                ds = prob * (_dot(dd, vp, NT) - column(s_ref, DIL_HD // 2))
                return qq, dd, prob.astype(BF16), ds

            q1, d1, p1, ds1 = side(qc_ref, doc_ref, sc_ref, b_ref[p, :, QB:], cur_ok)
            q2, d2, p2, ds2 = side(qn_ref, don_ref, sn_ref, b_ref[p, :, :QB], prev_ok)
            ds1b, ds2b = ds1.astype(BF16), ds2.astype(BF16)
            dq_ref[:, cols] = carry[:, cols] + _dot(_side_by_side(ds1b), k2, NN) * scale
            carry[:, cols] = _dot(_side_by_side(ds2b), k2, NN) * scale
            dk_ref[:, cols] = _dot(jnp.concatenate([ds1b, ds2b], axis=0), jnp.concatenate([q1, q2], axis=0), TN) * scale
            dv_ref[:, cols] = _dot(jnp.concatenate([p1, p2], axis=0), jnp.concatenate([d1, d2], axis=0), TN)
            db_ref[p, :, QB:] += ds1
            db_ref[p, :, :QB] += ds2

    cur = pl.BlockSpec((QB, w), lambda r, n: (n, r))
    nxt = pl.BlockSpec((QB, w), lambda r, n: (jnp.minimum(n + 1, nl - 1), r))
    tile = pl.BlockSpec((npair, 2 * QB, 2 * QB), lambda r, n: (0, 0, 0))
    sd = _sds((t // dil, dil * w), F32)
    return _pcall(
        body, name=name, grid=(dil, nl),
        in_specs=[cur, nxt, cur, nxt, cur, nxt, cur, cur, tile],
        out_specs=(cur, cur, cur, tile),
        out_shape=(sd, sd, sd, _sds((npair, 2 * QB, 2 * QB), F32)),
        scratch_shapes=[pltpu.VMEM((QB, w), F32)],
        compiler_params=_cparams("arbitrary", "arbitrary"))(q, q, do, do, stats, stats, k, v, bias)


def _head_sum_matrix(scale):
    idx = np.arange(DIL_WIDTH) // DIL_HD
    return jnp.asarray((idx[:, None] == idx[None, :]).astype(np.float32) * scale, BF16)


def _head_sum(x, mat):
    hi = x.astype(BF16)
    lo = (x - hi.astype(F32)).astype(BF16)
    return _dot(hi, mat, NN) + _dot(lo, mat, NN)


def _to_views(src, tmp, out_refs):
    tm, w = src.shape
    for j in range(w // LANES):
        tmp[j] = src[:, j * LANES:(j + 1) * LANES]
    for d, o_ref in zip(DIL_DILATIONS, out_refs):
        if d == 1:
            o_ref[...] = src.astype(o_ref.dtype)
            continue
        for r in range(d):
            for j in range(w // LANES):
                lo = r * w + j * LANES
                o_ref[:, lo:lo + LANES] = tmp[j, pl.ds(r, tm // d, stride=d), :].astype(o_ref.dtype)


def _from_view(v_ref, tmp, d):
    tm = tmp.shape[1]
    w = v_ref.shape[1] // d
    for r in range(d):
        for j in range(w // LANES):
            lo = r * w + j * LANES
            tmp[j, pl.ds(r, tm // d, stride=d), :] = v_ref[:, lo:lo + LANES]
    return jnp.concatenate([tmp[j] for j in range(w // LANES)], axis=1)


def _view_specs(tm, t, dtype):
    specs = tuple(pl.BlockSpec((tm // d, d * DIL_WIDTH), lambda i: (i, 0)) for d in DIL_DILATIONS)
    shapes = tuple(_sds((t // d, d * DIL_WIDTH), dtype) for d in DIL_DILATIONS)
    return specs, shapes


def _view_scratch(tm):
    return pltpu.VMEM((DIL_WIDTH // LANES, tm, LANES), F32)


def _dil_merge(outs, lses, g, tm):
    w = DIL_WIDTH
    t = outs[0].shape[0]
    tm = min(tm, t)

    def body(o0, o1, o2, l0, l1, l2, g_ref, o_ref, l_ref, n_ref, so1, so2, sl1, sl2):
        d1, d2 = DIL_DILATIONS[1], DIL_DILATIONS[2]
        a0, a1, a2 = l0[...], _from_view(l1, sl1, d1), _from_view(l2, sl2, d2)
        m = jnp.maximum(jnp.maximum(a0, a1), a2)
        e0, e1, e2 = jnp.exp(a0 - m), jnp.exp(a1 - m), jnp.exp(a2 - m)
        den = e0 + e1 + e2
        o = (e0 * o0[...] + e1 * _from_view(o1, so1, d1) + e2 * _from_view(o2, so2, d2)) / den
        o_ref[...] = o
        l_ref[...] = m + jnp.log(den)
        r = lax.rsqrt(jnp.mean(o * o, axis=-1, keepdims=True) + EPS)
        n_ref[...] = (o * r * g_ref[...]).astype(n_ref.dtype)

    specs, _ = _view_specs(tm, t, F32)
    spec = pl.BlockSpec((tm, w), lambda i: (i, 0))
    return _pcall(
        body, name="dil_merge", grid=(t // tm,),
        in_specs=list(specs) * 2 + [pl.BlockSpec((1, w), lambda i: (0, 0))], out_specs=(spec, spec, spec),
        out_shape=(_sds((t, w), F32), _sds((t, w), F32), _sds((t, w), BF16)),
        scratch_shapes=[_view_scratch(tm)] * 4,
        compiler_params=_cparams("parallel"))(*outs, *lses, g)


def _dil_stats(do, o, lse, tm):
    t, w = do.shape
    tm = min(tm, t)

    def body(a_ref, b_ref, l_ref, m_ref, s1, s4, s16, d1, d4, d16, tmp):
        first = (lax.broadcasted_iota(jnp.int32, (tm, w), 1) & (DIL_HD - 1)) < DIL_HD // 2
        do_ = a_ref[...]
        _to_views(jnp.where(first, l_ref[...], _head_sum(do_ * b_ref[...], m_ref[...])), tmp, (s1, s4, s16))
        _to_views(do_, tmp, (d1, d4, d16))

    spec = pl.BlockSpec((tm, w), lambda i: (i, 0))
    f_specs, f_shapes = _view_specs(tm, t, F32)
    b_specs, b_shapes = _view_specs(tm, t, BF16)
    res = _pcall(body, name="dil_stats", grid=(t // tm,),
                 in_specs=[spec, spec, spec, pl.BlockSpec((w, w), lambda i: (0, 0))],
                 out_specs=f_specs + b_specs, out_shape=f_shapes + b_shapes,
                 scratch_shapes=[_view_scratch(tm)],
                 compiler_params=_cparams("parallel"))(do, o, lse, _head_sum_matrix(1.0))
    return res[:3], res[3:]


def _head_norm_fwd(x, col, g, name, tm):
    t = x.shape[0]
    w = DIL_WIDTH
    tm = min(tm, t)
    normed = g is not None

    def body(*refs):
        outs, tmp = refs[-4:-1], refs[-1]
        xf = refs[0][...]
        if normed:
            g_ref, m_ref = refs[1], refs[2]
            xf = xf * lax.rsqrt(_head_sum(xf * xf, m_ref[...]) + EPS) * g_ref[...]
        _to_views(xf, tmp, outs)

    specs, shapes = _view_specs(tm, t, BF16)
    extra = [g, _head_sum_matrix(1.0 / DIL_HD)] if normed else []
    extra_specs = [pl.BlockSpec((1, w), lambda i: (0, 0)), pl.BlockSpec((w, w), lambda i: (0, 0))] if normed else []
    return _pcall(
        body, name=name, grid=(t // tm,),
        in_specs=[pl.BlockSpec((tm, w), lambda i: (i, col))] + extra_specs,
        out_specs=specs, out_shape=shapes, scratch_shapes=[_view_scratch(tm)],
        compiler_params=_cparams("parallel"))(x, *extra)


def _head_norm_bwd(dys, x, col, g, name, tm):
    t = x.shape[0]
    w = DIL_WIDTH
    tm = min(tm, t)
    nd = len(dys)
    nt = t // tm
    lane = np.arange(w) % DIL_HD
    fold = jnp.asarray((lane[:, None] == lane[None, :]).astype(np.float32))

    def body(*refs):
        x_ref, g_ref, m_ref, f_ref = refs[nd:nd + 4]
        dx_ref, dg_ref, s1, s2 = refs[-4:]
        dy = refs[0][...] + _from_view(refs[1], s1, DIL_DILATIONS[1]) + _from_view(refs[2], s2, DIL_DILATIONS[2])
        xf = x_ref[...]
        mat = m_ref[...]
        r = lax.rsqrt(_head_sum(xf * xf, mat) + EPS)
        xh = xf * r
        dxh = dy * g_ref[...]
        dx_ref[...] = r * (dxh - xh * _head_sum(dxh * xh, mat))

        @pl.when(pl.program_id(0) == 0)
        def _():
            dg_ref[...] = jnp.zeros_like(dg_ref)

        dg_ref[...] += jnp.sum(dy * xh, axis=0, keepdims=True)

        @pl.when(pl.program_id(0) == nt - 1)
        def _():
            per_lane = jnp.broadcast_to(dg_ref[...], (8, w))
            dg_ref[...] = lax.dot_general(per_lane, f_ref[...], NN, precision=lax.Precision.HIGHEST,
                                          preferred_element_type=F32)[0:1]

    row = pl.BlockSpec((tm, w), lambda i: (i, 0))
    vec = pl.BlockSpec((1, w), lambda i: (0, 0))
    sq = pl.BlockSpec((w, w), lambda i: (0, 0))
    views, _ = _view_specs(tm, t, F32)
    return _pcall(
        body, name=name, grid=(nt,),
        in_specs=list(views) + [pl.BlockSpec((tm, w), lambda i: (i, col)), vec, sq, sq],
        out_specs=(row, vec), out_shape=(_sds((t, w), F32), _sds((1, w), F32)),
        scratch_shapes=[_view_scratch(tm)] * 2,
        compiler_params=_cparams("arbitrary"))(*dys, x, g, _head_sum_matrix(1.0 / DIL_HD), fold)


def _rowdot(a, b, name, tm):
    n, d = a.shape
    tm = min(tm, n)

    def body(a_ref, b_ref, o_ref):
        o_ref[...] = jnp.sum(a_ref[...].astype(F32) * b_ref[...].astype(F32), axis=-1, keepdims=True)

    spec = pl.BlockSpec((tm, d), lambda i: (i, 0))
    return _pcall(body, name=name, grid=(n // tm,), in_specs=[spec, spec],
                  out_specs=pl.BlockSpec((tm, 1), lambda i: (i, 0)), out_shape=_sds((n, 1), F32),
                  compiler_params=_cparams("parallel"))(a, b)


def _sum_branches(parts, name, tm):
    t = parts[0].shape[0]
    w = DIL_WIDTH
    tm = min(tm, t)

    def body(a_ref, b_ref, c_ref, o_ref, s1, s2):
        o_ref[...] = a_ref[...] + _from_view(b_ref, s1, DIL_DILATIONS[1]) + _from_view(c_ref, s2, DIL_DILATIONS[2])

    views, _ = _view_specs(tm, t, F32)
    return _pcall(body, name=name, grid=(t // tm,), in_specs=list(views),
                  out_specs=pl.BlockSpec((tm, w), lambda i: (i, 0)), out_shape=_sds((t, w), F32),
                  scratch_shapes=[_view_scratch(tm)] * 2,
                  compiler_params=_cparams("parallel"))(*parts)


def _rope_tables(t):
    inv = ROPE_BASE ** (-np.arange(0, MLA_ROPE, 2, dtype=np.float64) / MLA_ROPE)
    ang = np.arange(t, dtype=np.float64)[:, None] * inv[None, :]
    cos, sin = np.cos(ang), np.sin(ang)
    return (jnp.asarray(np.concatenate([cos, cos], 1), F32), jnp.asarray(np.concatenate([-sin, sin], 1), F32))


def _swap_halves(a):
    half = MLA_ROPE // 2
    return jnp.concatenate([a[:, half:], a[:, :half]], axis=1)


def _qk_parts(x, pe, tm, nt):
    if pe is None:
        return None
    return (pl.BlockSpec((tm, MLA_NOPE), lambda i: (i, 0)), pl.BlockSpec((tm, MLA_ROPE), lambda i: (i % nt, 0)))


def _mla_qk_fwd(x, g, cos_t, sin_t, scale, name, tm, pe=None):
    n = x.shape[0]
    d = MLA_QK
    t = cos_t.shape[0]
    tm = min(tm, t)
    nt = t // tm
    split = _qk_parts(x, pe, tm, nt)

    def transposed(a):
        w = a.shape[1]
        eye = (lax.broadcasted_iota(jnp.int32, (w, w), 0) == lax.broadcasted_iota(jnp.int32, (w, w), 1)).astype(BF16)
        return _dot(eye, a, NT).astype(BF16)

    def body(*refs):
        if split:
            xn_ref, xr_ref, xv_ref, g_ref, c_ref, s_ref, o_ref, ot_ref, v_ref = refs
            xn, xr = xn_ref[...], xr_ref[...]
            v_ref[...] = xv_ref[...].astype(v_ref.dtype)
        else:
            x_ref, g_ref, c_ref, s_ref, o_ref = refs
            xf = x_ref[...]
            xn, xr = xf[:, :MLA_NOPE], xf[:, MLA_NOPE:]
        ms = (jnp.sum(xn * xn, axis=-1, keepdims=True) + jnp.sum(xr * xr, axis=-1, keepdims=True)) * (1.0 / d)
        r = lax.rsqrt(ms + EPS)
        gg = g_ref[...]
        yn = xn * r * gg[:, :MLA_NOPE]
        yr = xr * r * gg[:, MLA_NOPE:]
        on = (yn * scale).astype(o_ref.dtype)
        orot = ((yr * c_ref[...] + _swap_halves(yr) * s_ref[...]) * scale).astype(o_ref.dtype)
        o_ref[:, :MLA_NOPE] = on
        o_ref[:, MLA_NOPE:] = orot
        if split:
            ot_ref[:MLA_NOPE, :] = transposed(on)
            ot_ref[MLA_NOPE:, :] = transposed(orot)

    row = pl.BlockSpec((tm, d), lambda i: (i, 0))
    vec = pl.BlockSpec((1, d), lambda i: (0, 0))
    tab = pl.BlockSpec((tm, MLA_ROPE), lambda i: (i % nt, 0))
    if not split:
        return _pcall(body, name=name, grid=(n // tm,), in_specs=[row, vec, tab, tab],
                      out_specs=row, out_shape=_sds((n, d), BF16),
                      compiler_params=_cparams("parallel"))(x, g, cos_t, sin_t)
    vals = pl.BlockSpec((tm, MLA_V), lambda i: (i, 1))
    return _pcall(body, name=name, grid=(n // tm,), in_specs=[split[0], split[1], vals, vec, tab, tab],
                  out_specs=(row, pl.BlockSpec((None, d, tm), lambda i: (i // nt, 0, i % nt)),
                             pl.BlockSpec((tm, MLA_V), lambda i: (i, 0))),
                  out_shape=(_sds((n, d), BF16), _sds((n // t, d, t), BF16), _sds((n, MLA_V), BF16)),
                  compiler_params=_cparams("parallel"))(x, pe, x, g, cos_t, sin_t)


def _mla_qk_bwd(dy, x, g, cos_t, sin_t, scale, name, tm, pe=None):
    n = x.shape[0]
    d = MLA_QK
    t = cos_t.shape[0]
    tm = min(tm, t)
    nt = t // tm
    split = _qk_parts(x, pe, tm, nt)

    def body(*refs):
        if split:
            dy_ref, xn_ref, xr_ref, g_ref, c_ref, s_ref, dxn_ref, dxr_ref, dg_ref = refs
            xn, xr = xn_ref[...], xr_ref[...]
        else:
            dy_ref, x_ref, g_ref, c_ref, s_ref, dx_ref, dg_ref = refs
            xf = x_ref[...]
            xn, xr = xf[:, :MLA_NOPE], xf[:, MLA_NOPE:]
        gg = g_ref[...]
        ms = (jnp.sum(xn * xn, axis=-1, keepdims=True) + jnp.sum(xr * xr, axis=-1, keepdims=True)) * (1.0 / d)
        r = lax.rsqrt(ms + EPS)
        xh_n, xh_r = xn * r, xr * r
        dyf = dy_ref[...] * scale
        dyr = dyf[:, MLA_NOPE:]
        dn_n = dyf[:, :MLA_NOPE]
        dn_r = dyr * c_ref[...] + _swap_halves(dyr * s_ref[...])
        dxh_n = dn_n * gg[:, :MLA_NOPE]
        dxh_r = dn_r * gg[:, MLA_NOPE:]
        mean = (jnp.sum(dxh_n * xh_n, axis=-1, keepdims=True)
                + jnp.sum(dxh_r * xh_r, axis=-1, keepdims=True)) * (1.0 / d)
        dx_n = r * (dxh_n - xh_n * mean)
        dx_r = r * (dxh_r - xh_r * mean)
        if split:
            dxn_ref[...] = dx_n
            dxr_ref[...] = dx_r
        else:
            dx_ref[:, :MLA_NOPE] = dx_n
            dx_ref[:, MLA_NOPE:] = dx_r

        @pl.when(pl.program_id(0) == 0)
        def _():
            dg_ref[...] = jnp.zeros_like(dg_ref)

        dg_ref[:, :MLA_NOPE] += jnp.sum(dn_n * xh_n, axis=0, keepdims=True)
        dg_ref[:, MLA_NOPE:] += jnp.sum(dn_r * xh_r, axis=0, keepdims=True)

    row = pl.BlockSpec((tm, d), lambda i: (i, 0))
    vec = pl.BlockSpec((1, d), lambda i: (0, 0))
    tab = pl.BlockSpec((tm, MLA_ROPE), lambda i: (i % nt, 0))
    if not split:
        return _pcall(body, name=name, grid=(n // tm,), in_specs=[row, row, vec, tab, tab],
                      out_specs=(row, vec), out_shape=(_sds((n, d), F32), _sds((1, d), F32)),
                      compiler_params=_cparams("arbitrary"))(dy, x, g, cos_t, sin_t)
    outs = (pl.BlockSpec((tm, MLA_NOPE), lambda i: (i, 0)), pl.BlockSpec((tm, MLA_ROPE), lambda i: (i, 0)), vec)
    return _pcall(body, name=name, grid=(n // tm,), in_specs=[row, split[0], split[1], vec, tab, tab],
                  out_specs=outs, out_shape=(_sds((n, MLA_NOPE), F32), _sds((n, MLA_ROPE), F32), _sds((1, d), F32)),
                  compiler_params=_cparams("arbitrary"))(dy, x, pe, g, cos_t, sin_t)


def _causal_mask(i, j, tq, tk, width):
    row = i * tq + lax.broadcasted_iota(jnp.int32, (tq, width), 0)
    col = j * tk + lax.broadcasted_iota(jnp.int32, (tq, width), 1)
    return col <= row


def _causal_steps(nq, nk, tq, tk, q_major):
    if q_major:
        groups = [[(i, j) for j in range((i * tq + tq - 1) // tk + 1)] for i in range(nq)]
        nunit = tk // tq if tk % tq == 0 else 1
    else:
        groups = [[(i, j) for i in range((j * tk) // tq, nq)] for j in range(nk)]
        nunit = tq // tk if tq % tk == 0 else 1
    it, jt, fl = [], [], []
    for g in groups:
        for n, (i, j) in enumerate(g):
            crossing = j * tk + tk - 1 > i * tq
            if q_major:
                unit = tk // nunit
                u = min(nunit, -(-(i * tq + tq - j * tk) // unit)) - 1
            else:
                unit = tq // nunit
                u = max(0, j * tk - i * tq) // unit
            it.append(i)
            jt.append(j)
            fl.append((n == 0) + 2 * (n == len(g) - 1) + 4 * crossing + 8 * (u if crossing else 0))
    return tuple(jnp.asarray(np.array(a, np.int32)) for a in (it, jt, fl)), nunit


def _by_crossing(flags, nunit, update):
    pl.when((flags & 4) == 0)(functools.partial(update, None))
    for u in range(nunit):
        pl.when(((flags & 4) != 0) & ((flags >> 3) == u))(functools.partial(update, u))


def _causal_specs(tq, tk):
    def qs(w):
        return pl.BlockSpec((None, tq, w), lambda h, s, it, jt, fl: (h, it[s], 0))

    def kv(w):
        return pl.BlockSpec((None, tk, w), lambda h, s, it, jt, fl: (h, jt[s], 0))

    return qs, kv


def _mla_fwd(q, k, v, tq, tk):
    nh, t, dq = q.shape
    dv = v.shape[2]
    tq, tk = min(tq, t), min(tk, t)
    tables, nunit = _causal_steps(t // tq, t // tk, tq, tk, True)

    def body(it, jt, fl, q_ref, k_ref, v_ref, o_ref, lse_ref, m_sc, l_sc, acc_sc):
        step = pl.program_id(1)
        i, j, flags = it[step], jt[step], fl[step]

        @pl.when((flags & 1) != 0)
        def _():
            m_sc[...] = jnp.full_like(m_sc, NEG)
            l_sc[...] = jnp.zeros_like(l_sc)
            acc_sc[...] = jnp.zeros_like(acc_sc)

        def update(units):
            wk = tk if units is None else (units + 1) * (tk // nunit)
            s = _dot(q_ref[...], k_ref[:wk, :], NT)
            if units is not None:
                s = jnp.where(_causal_mask(i, j, tq, tk, wk), s, NEG)
            m_prev = m_sc[...]
            m_new = jnp.maximum(m_prev, jnp.max(s, axis=-1, keepdims=True))
            alpha = jnp.exp(m_prev - m_new)
            p = jnp.exp(s - m_new)
            l_sc[...] = alpha * l_sc[...] + jnp.sum(p, axis=-1, keepdims=True)
            acc_sc[...] = alpha * acc_sc[...] + _dot(p.astype(BF16), v_ref[:wk, :], NN)
            m_sc[...] = m_new

        _by_crossing(flags, nunit, update)

        @pl.when((flags & 2) != 0)
        def _():
            o_ref[...] = acc_sc[...] / l_sc[...]
            lse_ref[...] = m_sc[...] + jnp.log(l_sc[...])

    qs, kv = _causal_specs(tq, tk)
    return _pcall(
        body, name="mla_attn_fwd",
        grid_spec=pltpu.PrefetchScalarGridSpec(
            num_scalar_prefetch=3, grid=(nh, tables[0].shape[0]),
            in_specs=[qs(dq), kv(dq), kv(dv)], out_specs=(qs(dv), qs(1)),
            scratch_shapes=[pltpu.VMEM((tq, 1), F32), pltpu.VMEM((tq, 1), F32), pltpu.VMEM((tq, dv), F32)]),
        out_shape=(_sds((nh, t, dv), F32), _sds((nh, t, 1), F32)),
        compiler_params=_cparams("parallel", "arbitrary"))(*tables, q, k, v)


def _mla_bwd(q, k, k_t, v, do, lse_row, dl_row, tq, tk):
    nh, t, dq = q.shape
    dv = v.shape[2]
    tq, tk = min(tq, t), min(tk, t)
    nq = t // tq
    tables, nunit = _causal_steps(nq, t // tk, tq, tk, False)

    def body(it, jt, fl, q_ref, k_ref, kt_ref, v_ref, do_ref, lse_ref, dl_ref, dk_ref, dv_ref, dq_ref, dk_sc, dv_sc):
        step = pl.program_id(1)
        i, j, flags = it[step], jt[step], fl[step]

        def update(units):
            off = 0 if units is None else units * (tq // nunit)
            qq = q_ref[off:, :]
            st = _dot(k_ref[...], qq, NT)
            if units is not None:
                key = j * tk + lax.broadcasted_iota(jnp.int32, (tk, tq - off), 0)
                qry = i * tq + off + lax.broadcasted_iota(jnp.int32, (tk, tq - off), 1)
                st = jnp.where(key <= qry, st, NEG)
            pt = jnp.exp(st - lse_ref[:, off:])
            dob = do_ref[off:, :].astype(BF16)
            dpt = _dot(v_ref[...], dob, NT)
            dst = pt * (dpt - dl_ref[:, off:])
            dsb = dst.astype(BF16)
            dv_part = _dot(pt.astype(BF16), dob, NN)
            dk_part = _dot(dsb, qq, NN)
            dq_part = _dot(kt_ref[...], dsb, NN)

            @pl.when((flags & 1) != 0)
            def _():
                dv_sc[...] = dv_part
                dk_sc[...] = dk_part

            @pl.when((flags & 1) == 0)
            def _():
                dv_sc[...] += dv_part
                dk_sc[...] += dk_part

            if off == 0:
                @pl.when(j == 0)
                def _():
                    dq_ref[i] = dq_part

                @pl.when(j != 0)
                def _():
                    dq_ref[i] += dq_part
            else:
                dq_ref[i, :, off:] += dq_part

        _by_crossing(flags, nunit, update)

        @pl.when((flags & 2) != 0)
        def _():
            dk_ref[...] = dk_sc[...]
            dv_ref[...] = dv_sc[...]

    qs, kv = _causal_specs(tq, tk)
    rowv = pl.BlockSpec((None, 1, tq), lambda h, s, it, jt, fl: (h, 0, it[s]))
    ktv = pl.BlockSpec((None, dq, tk), lambda h, s, it, jt, fl: (h, 0, jt[s]))
    whole = pl.BlockSpec((None, nq, dq, tq), lambda h, s, it, jt, fl: (h, 0, 0, 0))
    return _pcall(
        body, name="mla_attn_bwd",
        grid_spec=pltpu.PrefetchScalarGridSpec(
            num_scalar_prefetch=3, grid=(nh, tables[0].shape[0]),
            in_specs=[qs(dq), kv(dq), ktv, kv(dv), qs(dv), rowv, rowv], out_specs=(kv(dq), kv(dv), whole),
            scratch_shapes=[pltpu.VMEM((tk, dq), F32), pltpu.VMEM((tk, dv), F32)]),
        out_shape=(_sds((nh, t, dq), F32), _sds((nh, t, dv), F32), _sds((nh, nq, dq, tq), F32)),
        compiler_params=_cparams("parallel", "arbitrary"))(*tables, q, k, k_t, v, do, lse_row, dl_row)


def _ffn_down_loss(act, wd, x, target, name, tm):
    nc, t, fs = act.shape
    d = x.shape[1]
    tm = min(tm, t)
    nt = t // tm

    def body(*refs):
        a_refs, w_refs = refs[:nc], refs[nc:2 * nc]
        x_ref, t_ref, dy_ref, loss_ref, acc = refs[2 * nc:]
        i = pl.program_id(0)
        tot = _dot(a_refs[0][...], w_refs[0][...], NN)
        for c in range(1, nc):
            tot = tot + _dot(a_refs[c][...], w_refs[c][...], NN)
        err = x_ref[...] + tot * FFN_RESID - t_ref[...]
        dy_ref[...] = err * (1.0 / d)

        @pl.when(i == 0)
        def _():
            acc[...] = jnp.zeros_like(acc)

        acc[...] += jnp.sum(err * err, axis=0, keepdims=True)

        @pl.when(i == nt - 1)
        def _():
            loss_ref[0, 0] = jnp.sum(acc[...]) * (0.5 / d)

    row = pl.BlockSpec((tm, d), lambda i: (i, 0))
    a_specs = [pl.BlockSpec((None, tm, fs), lambda i, c=c: (c, i, 0)) for c in range(nc)]
    w_specs = [pl.BlockSpec((None, fs, d), lambda i, c=c: (c, 0, 0)) for c in range(nc)]
    return _pcall(
        body, name=name, grid=(nt,), in_specs=a_specs + w_specs + [row, row],
        out_specs=(row, pl.BlockSpec(memory_space=pltpu.SMEM)),
        out_shape=(_sds((t, d), F32), _sds((1, 1), F32)),
        scratch_shapes=[pltpu.VMEM((1, d), F32)],
        compiler_params=_cparams("arbitrary"))(*[act] * nc, *[wd] * nc, x, target)


def _adamw(w, g, m, v, name):
    r, c = w.shape
    tr = r // 2 if r % 16 == 0 else r

    def body(w_ref, g_ref, m_ref, v_ref, d_ref, nm_ref, nv_ref):
        gg = g_ref[...]
        nm = ADAM_B1 * m_ref[...] + (1.0 - ADAM_B1) * gg
        nv = ADAM_B2 * v_ref[...] + (1.0 - ADAM_B2) * (gg * gg)
        m_hat = nm / (1.0 - ADAM_B1 ** ADAM_STEP)
        v_hat = nv / (1.0 - ADAM_B2 ** ADAM_STEP)
        d_ref[...] = -ADAM_LR * (m_hat / (jnp.sqrt(v_hat) + ADAM_EPS) + ADAM_WD * w_ref[...])
        nm_ref[...] = nm
        nv_ref[...] = nv

    spec = pl.BlockSpec((tr, c), lambda i: (i, 0))
    sd = _sds((r, c), F32)
    return _pcall(body, name=name, grid=(r // tr,), in_specs=[spec] * 4, out_specs=(spec,) * 3,
                  out_shape=(sd, sd, sd), compiler_params=_cparams("parallel"))(w, g, m, v)


MESH_ID = pl.DeviceIdType.MESH
HBM_SPEC = pl.BlockSpec(memory_space=pltpu.HBM)


def _place():
    return lax.axis_index("x"), lax.axis_index("y"), lax.axis_index("c")


def _other_chips(x, y):
    return [(1 - x, y), (x, 1 - y), (1 - x, 1 - y)]


def _remote(src, dst, send_sems, recv_sems, k, to):
    return pltpu.make_async_remote_copy(src_ref=src, dst_ref=dst, send_sem=send_sems.at[k], recv_sem=recv_sems.at[k],
                                        device_id=to, device_id_type=MESH_ID)


def _halves(arrays):
    for a in arrays:
        assert a.shape[-2] % 32 == 0
    return [a.shape[-2] // 2 for a in arrays]


def _gather_start(srcs, outs, halves, send_sems, recv_sems):
    x, y, c = _place()
    for a, half in enumerate(halves):
        rows = pl.ds(c * half, half)
        for k, (cx, cy) in enumerate(_other_chips(x, y)):
            _remote(srcs[a].at[rows, :], outs[a].at[2 * x + y, rows, :], send_sems, recv_sems, 3 * a + k,
                    (cx, cy, c)).start()


def _gather_wait(outs, halves, send_sems, recv_sems):
    x, y, c = _place()
    for a, half in enumerate(halves):
        for k, (cx, cy) in enumerate(_other_chips(x, y)):
            got = outs[a].at[2 * cx + cy, pl.ds(c * half, half), :]
            _remote(got, got, send_sems, recv_sems, 3 * a + k, (x, y, c)).wait()


def _forward_cores(partly):
    n = len(partly)
    halves = _halves(partly)

    def body(*refs):
        srcs, outs, send_sems, recv_sems = refs[:n], refs[n:2 * n], refs[2 * n], refs[2 * n + 1]
        x, y, c = _place()
        for a, half in enumerate(halves):
            for k, (cx, cy) in enumerate(_other_chips(x, y)):
                rows = pl.ds(c * half, half)
                _remote(srcs[a].at[2 * cx + cy, rows, :], outs[a].at[2 * cx + cy, rows, :], send_sems, recv_sems,
                        3 * a + k, (x, y, 1 - c)).start()
        for a, half in enumerate(halves):
            for k, (cx, cy) in enumerate(_other_chips(x, y)):
                mine = outs[a].at[2 * cx + cy, pl.ds(c * half, half), :]
                theirs = outs[a].at[2 * cx + cy, pl.ds((1 - c) * half, half), :]
                _remote(mine, theirs, send_sems, recv_sems, 3 * a + k, (x, y, c)).wait()

    return _pcall(
        body, name="forward_cores", in_specs=[HBM_SPEC] * n, out_specs=tuple([HBM_SPEC] * n),
        out_shape=tuple(_sds(p.shape, p.dtype) for p in partly), input_output_aliases={a: a for a in range(n)},
        scratch_shapes=[pltpu.SemaphoreType.DMA((3 * n,)), pltpu.SemaphoreType.DMA((3 * n,))],
    )(*partly)


def _gather_weights(blocks):
    n = len(blocks)
    halves = _halves(blocks)

    def body(*refs):
        srcs, outs, send_sems, recv_sems = refs[:n], refs[n:2 * n], refs[2 * n], refs[2 * n + 1]
        x, y, c = _place()
        me = 2 * x + y
        sibling = (x, y, 1 - c)
        chips = _other_chips(x, y)

        def part(a, chip, core):
            return outs[a].at[chip, pl.ds(core * halves[a], halves[a]), :]

        for a in range(n):
            mine = srcs[a].at[pl.ds(c * halves[a], halves[a]), :]
            for k, (cx, cy) in enumerate(chips):
                _remote(mine, part(a, me, c), send_sems, recv_sems, 6 * a + k, (cx, cy, c)).start()
        for k, (cx, cy) in enumerate(chips):
            for a in range(n):
                got = part(a, 2 * cx + cy, c)
                _remote(got, got, send_sems, recv_sems, 6 * a + k, (x, y, c)).wait_recv()
                _remote(got, got, send_sems, recv_sems, 6 * a + 3 + k, sibling).start()
        for k, (cx, cy) in enumerate(chips):
            for a in range(n):
                got = part(a, 2 * cx + cy, 1 - c)
                _remote(got, got, send_sems, recv_sems, 6 * a + 3 + k, (x, y, c)).wait_recv()
        for a in range(n):
            sent = part(a, me, c)
            for k in range(6):
                _remote(sent, sent, send_sems, recv_sems, 6 * a + k, (x, y, c)).wait_send()

    return _pcall(
        body, name="gather_weights", in_specs=[HBM_SPEC] * n, out_specs=tuple([HBM_SPEC] * n),
        out_shape=tuple(_sds((N_CHIPS,) + b.shape, b.dtype) for b in blocks),
        scratch_shapes=[pltpu.SemaphoreType.DMA((6 * n,)), pltpu.SemaphoreType.DMA((6 * n,))],
    )(*blocks)


def _reduce_cores(grads, tag):
    n = len(grads)

    def body(*refs):
        gs, outs, send_sems, recv_sems = refs[:n], refs[n:2 * n], refs[2 * n], refs[2 * n + 1]
        _cores_start(gs, outs, send_sems, recv_sems)
        _cores_wait(gs, outs, send_sems, recv_sems)

    return _pcall(
        body, name=f"reduce_cores_{tag}", in_specs=[HBM_SPEC] * n, out_specs=tuple([HBM_SPEC] * n),
        out_shape=_cores_shapes(grads), scratch_shapes=_cores_sems(n),
    )(*grads)


def _cores_shapes(grads):
    return tuple(_sds((N_CHIPS, h, g.shape[2]), g.dtype) for g, h in zip(grads, _halves(grads)))


def _cores_sems(n):
    return [pltpu.SemaphoreType.DMA((n,)), pltpu.SemaphoreType.DMA((n,))]


def _cores_start(gs, outs, send_sems, recv_sems):
    x, y, c = _place()
    for a, g in enumerate(gs):
        half = g.shape[1] // 2
        for j in range(N_CHIPS):
            _remote(g.at[j, pl.ds((1 - c) * half, half), :], outs[a].at[j], send_sems, recv_sems, a,
                    (x, y, 1 - c)).start()


def _cores_wait(gs, outs, send_sems, recv_sems):
    x, y, c = _place()
    for a, g in enumerate(gs):
        half = g.shape[1] // 2
        _remote(g.at[:, pl.ds((1 - c) * half, half), :], outs[a], send_sems, recv_sems, a, (x, y, c)).wait()


def _scatter_shapes(parts):
    return tuple(_sds((3,) + p.shape[1:], p.dtype) for p in parts)


def _scatter_sems(n):
    return [pltpu.SemaphoreType.DMA((3 * n,)), pltpu.SemaphoreType.DMA((3 * n,))]


def _scatter_start(ps, outs, send_sems, recv_sems):
    x, y, c = _place()
    for a in range(len(ps)):
        for k, (cx, cy) in enumerate(_other_chips(x, y)):
            _remote(ps[a].at[2 * cx + cy], outs[a].at[k], send_sems, recv_sems, 3 * a + k, (cx, cy, c)).start()


def _scatter_wait(ps, outs, send_sems, recv_sems):
    x, y, c = _place()
    for a in range(len(ps)):
        for k in range(3):
            _remote(ps[a].at[k], outs[a].at[k], send_sems, recv_sems, 3 * a + k, (x, y, c)).wait()


def _gather_shapes(blocks):
    return tuple(_sds((N_CHIPS,) + b.shape, b.dtype) for b in blocks)


def _gather_sems(n):
    return [pltpu.SemaphoreType.DMA((3 * n,)), pltpu.SemaphoreType.DMA((3 * n,))]


EXCHANGES = {"chips": (_scatter_start, _scatter_wait, _scatter_shapes, _scatter_sems),
             "cores": (_cores_start, _cores_wait, _cores_shapes, _cores_sems),
             "gather": (lambda srcs, outs, s, r: _gather_start(srcs, outs, _halves(srcs), s, r),
                        lambda srcs, outs, s, r: _gather_wait(outs, _halves(srcs), s, r),
                        _gather_shapes, _gather_sems)}


def _sum_partials(received, parts, place):
    n = len(parts)
    steps = 2
    tiles = [p.shape[1] // steps for p in parts]

    def body(place_ref, *refs):
        rs, ps, outs = refs[:n], refs[n:2 * n], refs[2 * n:]
        for a in range(n):
            tot = ps[a][...].astype(F32)
            for k in range(3):
                tot = tot + rs[a][k].astype(F32)
            outs[a][...] = tot

    cols = [p.shape[2] for p in parts]
    return _pcall(
        body, name="sum_chip_partials",
        grid_spec=pltpu.PrefetchScalarGridSpec(
            num_scalar_prefetch=1, grid=(steps,),
            in_specs=[pl.BlockSpec((3, tm, w), lambda i, pc: (0, i, 0)) for tm, w in zip(tiles, cols)]
            + [pl.BlockSpec((None, tm, w), lambda i, pc: (pc[0], i, 0)) for tm, w in zip(tiles, cols)],
            out_specs=tuple(pl.BlockSpec((tm, w), lambda i, pc: (pc[1] * steps + i, 0)) for tm, w in zip(tiles, cols))),
        out_shape=tuple(_sds((2 * p.shape[1], p.shape[2]), F32) for p in parts),
        compiler_params=_cparams("parallel"))(place, *received, *parts)


def _share_cores(blocks):
    n = len(blocks)
    halves = _halves(blocks)

    def body(*refs):
        srcs, outs, send_sems, recv_sems = refs[:n], refs[n:2 * n], refs[2 * n], refs[2 * n + 1]
        x, y, c = _place()
        for a in range(n):
            piece = pl.ds(c * halves[a], halves[a])
            _remote(srcs[a].at[piece, :], outs[a].at[piece, :], send_sems, recv_sems, a, (x, y, 1 - c)).start()
        for a in range(n):
            mine = outs[a].at[pl.ds(c * halves[a], halves[a]), :]
            theirs = outs[a].at[pl.ds((1 - c) * halves[a], halves[a]), :]
            _remote(mine, theirs, send_sems, recv_sems, a, (x, y, c)).wait()

    return _pcall(
        body, name="share_cores", in_specs=[HBM_SPEC] * n, out_specs=tuple([HBM_SPEC] * n),
        out_shape=tuple(_sds(b.shape, b.dtype) for b in blocks), input_output_aliases={a: a for a in range(n)},
        scratch_shapes=[pltpu.SemaphoreType.DMA((n,)), pltpu.SemaphoreType.DMA((n,))],
    )(*blocks)


def _sum_blocks(stacked, name, tm):
    n, rows, lanes = stacked.shape
    tm = min(tm, rows)

    def body(s_ref, o_ref):
        tot = s_ref[n - 1].astype(F32)
        for k in range(n - 1):
            tot = tot + s_ref[k].astype(F32)
        o_ref[...] = tot

    return _pcall(body, name=name, grid=(rows // tm,),
                  in_specs=[pl.BlockSpec((n, tm, lanes), lambda i: (0, i, 0))],
                  out_specs=pl.BlockSpec((tm, lanes), lambda i: (i, 0)), out_shape=_sds((rows, lanes), F32),
                  compiler_params=_cparams("parallel"))(stacked)


def _add_halves(grads, theirs, core, tag):
    n = len(grads)
    steps = 2
    tiles = [t.shape[1] // steps for t in theirs]
    cols = [t.shape[2] for t in theirs]

    def body(c_ref, *refs):
        gs, ts, outs = refs[:n], refs[n:2 * n], refs[2 * n:]
        for a in range(n):
            outs[a][...] = (gs[a][...] + ts[a][...]).astype(BF16)

    own = [pl.BlockSpec((None, tm, w), lambda k, i, c: (k, c[0] * steps + i, 0)) for tm, w in zip(tiles, cols)]
    same = [pl.BlockSpec((None, tm, w), lambda k, i, c: (k, i, 0)) for tm, w in zip(tiles, cols)]
    return _pcall(
        body, name=f"add_core_halves_{tag}",
        grid_spec=pltpu.PrefetchScalarGridSpec(
            num_scalar_prefetch=1, grid=(N_CHIPS, steps), in_specs=own + same, out_specs=tuple(same)),
        out_shape=tuple(_sds(t.shape, BF16) for t in theirs),
        compiler_params=_cparams("parallel", "parallel"))(core, *grads, *theirs)


def _allreduce_small(part):
    rows, lanes = part.shape
    ndev = 8

    def body(src, tot, buf, send_sems, recv_sems):
        x, y, c = _place()
        me = 4 * x + 2 * y + c
        buf[me] = src[...]
        sends = []
        for k in range(1, ndev):
            peer = (x ^ (k >> 2), y ^ ((k >> 1) & 1), c ^ (k & 1))
            cp = _remote(src, buf.at[me], send_sems, recv_sems, k - 1, peer)
            cp.start()
            sends.append(cp)
        for k in range(1, ndev):
            theirs = buf.at[me ^ k]
            _remote(theirs, theirs, send_sems, recv_sems, k - 1, (x, y, c)).wait_recv()
        for cp in sends:
            cp.wait_send()
        acc = buf[0]
        for d in range(1, ndev):
            acc = acc + buf[d]
        tot[...] = acc

    vm = pl.BlockSpec(memory_space=pltpu.VMEM)
    return _pcall(
        body, name="allreduce_small", in_specs=[vm], out_specs=vm, out_shape=_sds((rows, lanes), F32),
        scratch_shapes=[pltpu.VMEM((ndev, rows, lanes), F32), pltpu.SemaphoreType.DMA((ndev - 1,)),
                        pltpu.SemaphoreType.DMA((ndev - 1,))],
    )(part)


SMALL_USED = sum(r for _, _, r in SMALL)


def _pack_small(vals, loss):
    parts = []
    for name, shape, r in SMALL:
        flat = vals[name].reshape(-1).astype(F32)
        parts.append(jnp.pad(flat, (0, r * LANES - flat.shape[0])).reshape(r, LANES))
    parts.append(jnp.pad(loss.astype(F32), ((0, SMALL_ROWS - SMALL_USED - 1), (0, LANES - 1))))
    return jnp.concatenate(parts, axis=0)


def _unpack_small(packed):
    out, off = {}, 0
    for name, shape, r in SMALL:
        n = int(np.prod(shape))
        out[name] = packed[off:off + r].reshape(-1)[:n].reshape(shape)
        off += r
    return out


def _heads_major(a, nh):
    t = a.shape[0]
    return a.reshape(t, nh, a.shape[1] // nh).transpose(1, 0, 2)


def _tokens_major(a):
    nh, t, w = a.shape
    return a.transpose(1, 0, 2).reshape(t, nh * w)


LATE = ("ffn1_w_gate", "ffn1_w_up", "ffn1_w_down")
EARLY = tuple(name for name, _ in BIG if name not in LATE)


def _local_step(x, target, small, wfull, exchanges=None, later_weights=None):
    t = x.shape[0]
    nh, hd = DIL_HEADS, DIL_HD
    grads_s, grads_b = {}, {}

    x1, ffn1_saved, partly = _ffn_fwd(x, small["ffn1_norm"], wfull["ffn1_w_gate"], wfull["ffn1_w_up"],
                                      wfull["ffn1_w_down"], "ffn1", later_weights[0] if later_weights else ())
    if later_weights:
        wfull = {**wfull, **later_weights[1](partly)}
    w_in = wfull["w_in"].transpose(1, 0, 2).reshape(D_MODEL, -1)
    w_out = wfull["w_out"].reshape(D_MODEL, D_MODEL)
    w_qb, w_kvb = wfull["mla_w_q_b"], wfull["mla_w_kv_b"]
    hm = _rms_fwd(x1, small["mix_norm"], BF16, "mix_norm", 1024)
    proj = _mm_simple("in_proj", hm, w_in, NN, F32, tm=1024)
    cq, ckv, k_pe = proj[:, 1536:1792], proj[:, 1792:1920], proj[:, 1920:1984]

    gq, gk = jnp.tile(small["dil_q_norm"], (1, nh)), jnp.tile(small["dil_k_norm"], (1, nh))
    qn = _head_norm_fwd(proj, 0, gq, "dil_q_norm", 512)
    kn = _head_norm_fwd(proj, 1, gk, "dil_k_norm", 512)
    v_d = _head_norm_fwd(proj, 2, None, "dil_v_views", 512)
    bias = _bias_tiles(small["rel_bias"]).reshape(3, nh // 2, 2 * QB, QB + DIL_W)
    outs, lses = [], []
    for b, dil in enumerate(DIL_DILATIONS):
        o_b, lse_b = _dil_fwd(qn[b], kn[b], v_d[b], bias[b], dil, f"dil_fwd_{dil}")
        outs.append(o_b)
        lses.append(lse_b)
    o_dil, lse_tot, od = _dil_merge(outs, lses, small["out_norm_dil"], 512)

    mh = MLA_HEADS
    cos_t, sin_t = _rope_tables(t)
    cqn = _rms_fwd(cq, small["mla_q_a_norm"], BF16, "mla_q_a_norm", 2048)
    ckvn = _rms_fwd(ckv, small["mla_kv_a_norm"], BF16, "mla_kv_a_norm", 2048)
    tm = min(512, t)

    th = min(2048, t)

    def head_proj(name, a, w, width):
        k = a.shape[1]
        return _mm(name, (mh, t // th, 1),
                   [(a, pl.BlockSpec((th, k), lambda h, i, r: (i, 0)), w, pl.BlockSpec((None, k, width), lambda h, i, r: (h, 0, 0)))],
                   NN, _sds((mh, t, width), F32), pl.BlockSpec((None, th, width), lambda h, i, r: (h, i, 0)), (th, width))

    q_raw = head_proj("mla_q_proj", cqn, w_qb, MLA_QK)
    kv_raw = head_proj("mla_kv_proj", ckvn, w_kvb, MLA_NOPE + MLA_V)
    q_raw2, kv_raw2 = q_raw.reshape(mh * t, MLA_QK), kv_raw.reshape(mh * t, MLA_NOPE + MLA_V)
    q_scale = MLA_QK ** -0.5
    q_m = _mla_qk_fwd(q_raw2, small["mla_q_norm"], cos_t, sin_t, q_scale, "mla_q_rope", 2048).reshape(mh, t, MLA_QK)
    k_m, k_t, v_m = _mla_qk_fwd(kv_raw2, small["mla_k_norm"], cos_t, sin_t, 1.0, "mla_k_rope", 2048, pe=k_pe)
    k_m, v_m = k_m.reshape(mh, t, MLA_QK), v_m.reshape(mh, t, MLA_V)
    o_mla_h, lse_m = _mla_fwd(q_m, k_m, v_m, 512, 4096)
    o_mla = _tokens_major(o_mla_h)

    om = _rms_fwd(o_mla, small["out_norm_mla"], BF16, "out_norm_mla", 2048)
    half_w = DIL_WIDTH
    row = pl.BlockSpec((tm, D_MODEL), lambda i, j, r: (i, 0))
    act_spec = pl.BlockSpec((tm, half_w), lambda i, j, r: (i, 0))
    x2 = _mm("out_proj", (t // tm, 1, 1),
             [(od, act_spec, w_out, pl.BlockSpec((half_w, D_MODEL), lambda i, j, r: (0, 0))),
              (om, act_spec, w_out, pl.BlockSpec((half_w, D_MODEL), lambda i, j, r: (1, 0)))],
             NN, _sds((t, D_MODEL), F32), row, (tm, D_MODEL), res=(x1, row))
    (dy, loss), ffn2_saved, _ = _ffn_fwd(x2, small["ffn2_norm"], wfull["ffn2_w_gate"], wfull["ffn2_w_up"],
                                         wfull["ffn2_w_down"], "ffn2", target=target)

    dx2, grads_s["ffn2_norm"], grads_b["ffn2_w_gate"], grads_b["ffn2_w_up"], grads_b["ffn2_w_down"], _, _ = _ffn_bwd(
        dy, x2, small["ffn2_norm"], wfull["ffn2_w_gate"], wfull["ffn2_w_up"], wfull["ffn2_w_down"], ffn2_saved, "ffn2")

    d_ocat = _mm_simple("out_proj_dx", dx2, w_out, NT, F32, tm=1024)
    dw_out_t = _mm_tn_multi("out_proj_dw", dx2, [od, om], 2048)
    grads_b["w_out"] = jnp.concatenate([w.T for w in dw_out_t], axis=0).reshape(N_CHIPS, D_MODEL // N_CHIPS, D_MODEL)
    do_dil, grads_s["out_norm_dil"] = _rms_bwd([d_ocat[:, :half_w]], o_dil, small["out_norm_dil"], None, "out_norm_dil_bwd", 512)
    do_mla, grads_s["out_norm_mla"] = _rms_bwd([d_ocat[:, half_w:]], o_mla, small["out_norm_mla"], None, "out_norm_mla_bwd", 512)

    do_m = _heads_major(do_mla, mh)
    dl_m = _rowdot(do_m.reshape(mh * t, MLA_V), o_mla_h.reshape(mh * t, MLA_V), "mla_delta", 2048).reshape(mh, t, 1)
    dk_m, dv_m, dq_t = _mla_bwd(q_m, k_m, k_t, v_m, do_m, lse_m.reshape(mh, 1, t),
                                dl_m.reshape(mh, 1, t), 2048, 512)
    dq_m = dq_t.transpose(0, 1, 3, 2).reshape(mh, t, MLA_QK)
    dq_raw, grads_s["mla_q_norm"] = _mla_qk_bwd(dq_m.reshape(mh * t, MLA_QK), q_raw2, small["mla_q_norm"],
                                                 cos_t, sin_t, q_scale, "mla_q_rope_bwd", 2048)
    dk_nope, dk_pe_h, grads_s["mla_k_norm"] = _mla_qk_bwd(dk_m.reshape(mh * t, MLA_QK), kv_raw2, small["mla_k_norm"],
                                                          cos_t, sin_t, 1.0, "mla_k_rope_bwd", 2048, pe=k_pe)
    dq_raw = dq_raw.reshape(mh, t, MLA_QK)
    dk_nope = dk_nope.reshape(mh, t, MLA_NOPE)

    def head_proj_dx(name, d, w):
        width, k = d.shape[2], w.shape[1]
        pairs = [(d, pl.BlockSpec((None, th, width), lambda i, j, r, h=h: (h, i, 0)),
                  w, pl.BlockSpec((None, k, width), lambda i, j, r, h=h: (h, 0, 0))) for h in range(mh)]
        return _mm(name, (t // th, 1, 1), pairs, NT, _sds((t, k), F32),
                   pl.BlockSpec((th, k), lambda i, j, r: (i, 0)), (th, k))

    def head_proj_dw(name, a, d):
        width, k = d.shape[2], a.shape[1]
        return _mm(name, (mh, 1, t // th),
                   [(a, pl.BlockSpec((th, k), lambda h, j, r: (r, 0)), d, pl.BlockSpec((None, th, width), lambda h, j, r: (h, r, 0)))],
                   TN, _sds((mh, k, width), F32), pl.BlockSpec((None, k, width), lambda h, j, r: (h, 0, 0)), (k, width))

    d_cqn = head_proj_dx("mla_q_proj_dx", dq_raw, w_qb)
    kv_pairs = []
    for h in range(mh):
        for part, d_part in enumerate((dk_nope, dv_m)):
            kv_pairs.append((d_part, pl.BlockSpec((None, th, MLA_NOPE), lambda i, j, r, h=h: (h, i, 0)),
                             w_kvb, pl.BlockSpec((None, MLA_KV_RANK, MLA_NOPE), lambda i, j, r, h=h, part=part: (h, 0, part))))
    d_ckvn = _mm("mla_kv_proj_dx", (t // th, 1, 1), kv_pairs, NT, _sds((t, MLA_KV_RANK), F32),
                 pl.BlockSpec((th, MLA_KV_RANK), lambda i, j, r: (i, 0)), (th, MLA_KV_RANK))
    grads_b["mla_w_q_b"] = head_proj_dw("mla_q_proj_dw", cqn, dq_raw)
    grads_b["mla_w_kv_b"] = jnp.concatenate([head_proj_dw("mla_k_proj_dw", ckvn, dk_nope),
                                             head_proj_dw("mla_v_proj_dw", ckvn, dv_m)], axis=2)
    d_cq, grads_s["mla_q_a_norm"] = _rms_bwd([d_cqn], cq, small["mla_q_a_norm"], None, "mla_q_a_norm_bwd", 512)
    d_ckv, grads_s["mla_kv_a_norm"] = _rms_bwd([d_ckvn], ckv, small["mla_kv_a_norm"], None, "mla_kv_a_norm_bwd", 512)
    d_kpe = _sum_blocks(dk_pe_h.reshape(mh, t * MLA_ROPE // LANES, LANES), "mla_kpe_sum", 1024).reshape(t, MLA_ROPE)

    stats, do_db = _dil_stats(do_dil, o_dil, lse_tot, 512)
    dqs, dks, dvs, dtiles = [], [], [], []
    for b, dil in enumerate(DIL_DILATIONS):
        dq_b, dk_b, dv_b, db_b = _dil_bwd(qn[b], kn[b], v_d[b], do_db[b], stats[b], bias[b], dil, f"dil_bwd_{dil}")
        dqs.append(dq_b)
        dks.append(dk_b)
        dvs.append(dv_b)
        dtiles.append(db_b)
    grads_s["rel_bias"] = _bias_grad(jnp.stack(dtiles).reshape(3, nh, QB, QB + DIL_W))
    dq_a, dgq = _head_norm_bwd(dqs, proj, 0, gq, "dil_q_norm_bwd", 512)
    dk_a, dgk = _head_norm_bwd(dks, proj, 1, gk, "dil_k_norm_bwd", 512)
    grads_s["dil_q_norm"], grads_s["dil_k_norm"] = dgq[:, :hd], dgk[:, :hd]
    dv_a = _sum_branches(dvs, "dil_dv_sum", 512)

    dparts = [dq_a, dk_a, dv_a, d_cq, d_ckv, d_kpe]
    t2 = min(512, t)
    pairs, lo = [], 0
    for dpart in dparts:
        width = dpart.shape[1]
        w_part = w_in[:, lo:lo + width]
        pairs.append((dpart, pl.BlockSpec((t2, width), lambda i, j, r: (i, 0)),
                      w_part, pl.BlockSpec((D_MODEL, width), lambda i, j, r: (0, 0))))
        lo += width
    dw_in = jnp.concatenate(_mm_tn_multi("in_proj_dw", hm, dparts, 1024), axis=1)
    grads_b["w_in"] = dw_in.reshape(D_MODEL, N_CHIPS, -1).transpose(1, 0, 2)
    early = tuple(grads_b[n] for n in EARLY) if exchanges else ()
    row2 = pl.BlockSpec((t2, D_MODEL), lambda i, j, r: (i, 0))
    res = _mm("in_proj_dx", (t // t2, 1, 1), pairs, NT, _sds((t, D_MODEL), F32), row2, (t2, D_MODEL),
              res=(dx2, row2), norm=(x1, small["mix_norm"]), outgoing=early, exchange="cores")
    dx1, grads_s["mix_norm"] = res[0], res[1]
    outgoing = exchanges[0](early, res[2]) if exchanges else ()
    dx, grads_s["ffn1_norm"], grads_b["ffn1_w_gate"], grads_b["ffn1_w_up"], grads_b["ffn1_w_down"], arrived, late = _ffn_bwd(
        dx1, x, small["ffn1_norm"], wfull["ffn1_w_gate"], wfull["ffn1_w_up"], wfull["ffn1_w_down"], ffn1_saved, "ffn1",
        outgoing, exchanges[1] if exchanges else None)
    return loss, dx, grads_s, grads_b, (tuple(outgoing), arrived), late


def kernel(x, ffn1_norm, ffn1_w_gate, ffn1_w_up, ffn1_w_down, mix_norm, w_in, dil_q_norm, dil_k_norm, rel_bias, mla_q_a_norm, mla_w_q_b, mla_kv_a_norm, mla_w_kv_b, mla_q_norm, mla_k_norm, out_norm_dil, out_norm_mla, w_out, ffn2_norm, ffn2_w_gate, ffn2_w_up, ffn2_w_down, loss_target, m_ffn1_norm, m_ffn1_w_gate, m_ffn1_w_up, m_ffn1_w_down, m_mix_norm, m_w_in, m_dil_q_norm, m_dil_k_norm, m_rel_bias, m_mla_q_a_norm, m_mla_w_q_b, m_mla_kv_a_norm, m_mla_w_kv_b, m_mla_q_norm, m_mla_k_norm, m_out_norm_dil, m_out_norm_mla, m_w_out, m_ffn2_norm, m_ffn2_w_gate, m_ffn2_w_up, m_ffn2_w_down, v_ffn1_norm, v_ffn1_w_gate, v_ffn1_w_up, v_ffn1_w_down, v_mix_norm, v_w_in, v_dil_q_norm, v_dil_k_norm, v_rel_bias, v_mla_q_a_norm, v_mla_w_q_b, v_mla_kv_a_norm, v_mla_w_kv_b, v_mla_q_norm, v_mla_k_norm, v_out_norm_dil, v_out_norm_mla, v_w_out, v_ffn2_norm, v_ffn2_w_gate, v_ffn2_w_up, v_ffn2_w_down):
    given = dict(locals())
    big_names = [name for name, _ in BIG]
    small_names = [name for name, _, _ in SMALL]

    chip = (2 * lax.axis_index("x") + lax.axis_index("y")).astype(jnp.int32)
    core = lax.axis_index("c").astype(jnp.int32)
    mine = {n: given[n].astype(BF16) for n in big_names}

    def with_own(names, arrays):
        return {n: lax.dynamic_update_slice(a, mine[n], (chip, 0, 0)) for n, a in zip(names, arrays)}

    wfirst = with_own(LATE, _gather_weights([mine[n][0] for n in LATE]))
    later_weights = ([mine[n][0] for n in EARLY], lambda partly: with_own(EARLY, _forward_cores(partly)))
    small = {n: given[n] for n in small_names}

    def early_partials(partial, theirs):
        return _add_halves(partial, theirs, core.reshape(1), "early")

    def late_partials(partial):
        return _add_halves(partial, _reduce_cores(partial, "late"), core.reshape(1), "late")

    exchanges = (early_partials, late_partials)
    loss, dx, grads_s, grads_b, (early_part, early_got), (late_part, late_got) = _local_step(
        x[0], loss_target[0], small, wfirst, exchanges, later_weights)
    reduced = _sum_partials(tuple(late_got) + tuple(early_got), tuple(late_part) + tuple(early_part),
                            jnp.stack([chip, core]))
    g_big = dict(zip(LATE + EARLY, _share_cores(reduced)))
    summed = _allreduce_small(_pack_small(grads_s, loss))
    g_small = _unpack_small(summed)
    loss = summed[SMALL_USED, 0]

    grad, delta, new_m, new_v = {}, {}, {}, {}
    for name, shape in BIG:
        g2 = g_big[name]
        d_, m_, v_ = _adamw(given[name].reshape(shape), g2, given["m_" + name].reshape(shape),
                            given["v_" + name].reshape(shape), f"adamw_{name}")
        full = given[name].shape
        grad[name], delta[name], new_m[name], new_v[name] = (a.reshape(full) for a in (g2, d_, m_, v_))
    for name in small_names:
        grad[name] = g_small[name]
        delta[name], new_m[name], new_v[name] = _adamw(given[name], g_small[name], given["m_" + name],
                                                       given["v_" + name], f"adamw_{name}")

    return (loss, dx[None], *[grad[n] for n in WEIGHTS], *[delta[n] for n in WEIGHTS],
            *[new_m[n] for n in WEIGHTS], *[new_v[n] for n in WEIGHTS])
```

```python
import functools

import numpy as np
import jax
import jax.numpy as jnp
from jax import lax
from jax.experimental import pallas as pl
from jax.experimental.pallas import tpu as pltpu

F32 = jnp.float32
BF16 = jnp.bfloat16

D_MODEL = 1024
D_FF = 2816
N_CHIPS = 4
DIL_HEADS = 8
DIL_HD = 64
DIL_WIDTH = 512
DIL_DILATIONS = (1, 4, 16)
DIL_W = 128
QB = 128
MLA_HEADS = 4
MLA_NOPE = 128
MLA_ROPE = 64
MLA_QK = 192
MLA_V = 128
MLA_Q_RANK = 256
MLA_KV_RANK = 128
ROPE_BASE = 10000.0
REL_BUCKETS = 32
REL_MAX_DIST = 2048
FFN_RESID = 0.5
EPS = 1e-6
NEG = -1e30
LANES = 128

ADAM_LR = 0.001
ADAM_B1 = 0.9
ADAM_B2 = 0.999
ADAM_EPS = 1e-08
ADAM_WD = 0.01
ADAM_STEP = 10

NT = (((1,), (1,)), ((), ()))
NN = (((1,), (0,)), ((), ()))
TN = (((0,), (0,)), ((), ()))

BIG = (
    ("ffn1_w_gate", (D_MODEL, D_FF // N_CHIPS)),
    ("ffn1_w_up", (D_MODEL, D_FF // N_CHIPS)),
    ("ffn1_w_down", (D_FF // N_CHIPS, D_MODEL)),
    ("w_in", (D_MODEL, 1984 // N_CHIPS)),
    ("mla_w_q_b", (MLA_Q_RANK, MLA_QK)),
    ("mla_w_kv_b", (MLA_KV_RANK, MLA_NOPE + MLA_V)),
    ("w_out", (D_MODEL // N_CHIPS, D_MODEL)),
    ("ffn2_w_gate", (D_MODEL, D_FF // N_CHIPS)),
    ("ffn2_w_up", (D_MODEL, D_FF // N_CHIPS)),
    ("ffn2_w_down", (D_FF // N_CHIPS, D_MODEL)),
)
SMALL = (
    ("ffn1_norm", (1, 1024), 8), ("mix_norm", (1, 1024), 8), ("dil_q_norm", (1, 64), 1),
    ("dil_k_norm", (1, 64), 1), ("rel_bias", (8, 32), 2), ("mla_q_a_norm", (1, 256), 2),
    ("mla_kv_a_norm", (1, 128), 1), ("mla_q_norm", (1, 192), 2), ("mla_k_norm", (1, 192), 2),
    ("out_norm_dil", (1, 512), 4), ("out_norm_mla", (1, 512), 4), ("ffn2_norm", (1, 1024), 8),
)
SMALL_ROWS = 48
WEIGHTS = ("ffn1_norm", "ffn1_w_gate", "ffn1_w_up", "ffn1_w_down", "mix_norm", "w_in", "dil_q_norm",
           "dil_k_norm", "rel_bias", "mla_q_a_norm", "mla_w_q_b", "mla_kv_a_norm", "mla_w_kv_b",
           "mla_q_norm", "mla_k_norm", "out_norm_dil", "out_norm_mla", "w_out", "ffn2_norm",
           "ffn2_w_gate", "ffn2_w_up", "ffn2_w_down")


def _pcall(body, **kw):
    return pl.pallas_call(body, **kw)


def _cparams(*sem):
    return pltpu.CompilerParams(dimension_semantics=sem)


def _sds(shape, dtype):
    return jax.ShapeDtypeStruct(shape, dtype)


def _dot(a, b, dn):
    return lax.dot_general(a, b, dn, preferred_element_type=F32)


def _rms_fwd(x, g, out_dtype, name, tm):
    n, d = x.shape
    tm = min(tm, n)

    def body(x_ref, g_ref, o_ref):
        xf = x_ref[...].astype(F32)
        r = lax.rsqrt(jnp.mean(xf * xf, axis=-1, keepdims=True) + EPS)
        o_ref[...] = (xf * r * g_ref[...]).astype(o_ref.dtype)

    return _pcall(
        body, name=name, grid=(n // tm,),
        in_specs=[pl.BlockSpec((tm, d), lambda i: (i, 0)), pl.BlockSpec((1, d), lambda i: (0, 0))],
        out_specs=pl.BlockSpec((tm, d), lambda i: (i, 0)),
        out_shape=_sds((n, d), out_dtype), compiler_params=_cparams("parallel"))(x, g)


def _rms_bwd(dys, x, g, res, name, tm):
    n, d = x.shape
    tm = min(tm, n)
    nd = len(dys)
    has_res = res is not None

    def body(*refs):
        dy_refs = refs[:nd]
        x_ref, g_ref = refs[nd], refs[nd + 1]
        res_ref = refs[nd + 2] if has_res else None
        dx_ref, dg_ref = refs[-2], refs[-1]
        dy = dy_refs[0][...].astype(F32)
        for r_ in dy_refs[1:]:
            dy = dy + r_[...].astype(F32)
        xf = x_ref[...].astype(F32)
        r = lax.rsqrt(jnp.mean(xf * xf, axis=-1, keepdims=True) + EPS)
        xh = xf * r
        dxh = dy * g_ref[...]
        dx = r * (dxh - xh * jnp.mean(dxh * xh, axis=-1, keepdims=True))
        if has_res:
            dx = dx + res_ref[...]
        dx_ref[...] = dx

        @pl.when(pl.program_id(0) == 0)
        def _():
            dg_ref[...] = jnp.zeros_like(dg_ref)

        dg_ref[...] += jnp.sum(dy * xh, axis=0, keepdims=True)

    row = pl.BlockSpec((tm, d), lambda i: (i, 0))
    vec = pl.BlockSpec((1, d), lambda i: (0, 0))
    ins = list(dys) + [x, g] + ([res] if has_res else [])
    return _pcall(
        body, name=name, grid=(n // tm,),
        in_specs=[row] * nd + [row, vec] + ([row] if has_res else []),
        out_specs=(row, vec),
        out_shape=(_sds((n, d), F32), _sds((1, d), F32)),
        compiler_params=_cparams("arbitrary"))(*ins)


def _mm(name, grid, pairs, dn, out_shape, out_spec, acc_shape, res=None, scale=1.0, outgoing=(), norm=None,
        exchange="chips"):
    npairs = len(pairs)
    nred = grid[2]
    has_res = res is not None
    has_norm = norm is not None
    no = len(outgoing)
    ex_start, ex_wait, ex_shapes, ex_sems = EXCHANGES[exchange]

    def body(*refs):
        ab = refs[:2 * npairs]
        res_ref = refs[2 * npairs] if has_res else None
        nin = 2 * npairs + int(has_res) + 2 * int(has_norm)
        if has_norm:
            x_ref, g_ref = refs[nin - 2:nin]
        first_out = nin + no
        sent = refs[nin:first_out]
        o_ref = refs[first_out]
        nout = 1 + int(has_norm)
        dg_ref = refs[first_out + 1] if has_norm else None
        arrived = refs[first_out + nout:first_out + nout + no]
        acc_ref = refs[first_out + nout + no] if nred > 1 else None
        if no:
            send_sems, recv_sems = refs[-2:]
            ids = [pl.program_id(n) for n in range(3)]

            @pl.when((ids[0] == 0) & (ids[1] == 0) & (ids[2] == 0))
            def _():
                ex_start(sent, arrived, send_sems, recv_sems)

        tot = None
        for p in range(npairs):
            d = _dot(ab[2 * p][...].astype(BF16), ab[2 * p + 1][...].astype(BF16), dn)
            tot = d if tot is None else tot + d

        def finish(v):
            if scale != 1.0:
                v = v * scale
            if has_norm:
                xf = x_ref[...]
                r = lax.rsqrt(jnp.mean(xf * xf, axis=-1, keepdims=True) + EPS)
                xh = xf * r
                dxh = v * g_ref[...]

                @pl.when(pl.program_id(0) == 0)
                def _():
                    dg_ref[...] = jnp.zeros_like(dg_ref)

                dg_ref[...] += jnp.sum(v * xh, axis=0, keepdims=True)
                v = r * (dxh - xh * jnp.mean(dxh * xh, axis=-1, keepdims=True))
            if has_res:
                v = res_ref[...] + v
            o_ref[...] = v.astype(o_ref.dtype)

        if nred == 1:
            finish(tot)
        else:
            r = pl.program_id(2)

            @pl.when(r == 0)
            def _():
                acc_ref[...] = tot

            @pl.when(r > 0)
            def _():
                acc_ref[...] += tot

            @pl.when(r == nred - 1)
            def _():
                finish(acc_ref[...])

        if no:
            @pl.when((ids[0] == grid[0] - 1) & (ids[1] == grid[1] - 1) & (ids[2] == nred - 1))
            def _():
                ex_wait(sent, arrived, send_sems, recv_sems)

    ins, specs = [], []
    for a, a_spec, b, b_spec in pairs:
        ins += [a, b]
        specs += [a_spec, b_spec]
    if has_res:
        ins.append(res[0])
        specs.append(res[1])
    scratch = [pltpu.VMEM(acc_shape, F32)] if nred > 1 else []
    if not no and not has_norm:
        return _pcall(
            body, name=name, grid=grid, in_specs=specs, out_specs=out_spec, out_shape=out_shape,
            scratch_shapes=scratch, compiler_params=_cparams("parallel", "parallel", "arbitrary"))(*ins)
    out_specs, out_shapes = (out_spec,), (out_shape,)
    if has_norm:
        assert grid[1] == 1
        d = norm[1].shape[1]
        ins += [norm[0], norm[1]]
        specs += [out_spec, pl.BlockSpec((1, d), lambda i, j, r: (0, 0))]
        out_specs += (pl.BlockSpec((1, d), lambda i, j, r: (0, 0)),)
        out_shapes += (_sds((1, d), F32),)
    hbm = pl.BlockSpec(memory_space=pltpu.HBM)
    res_ = tuple(_pcall(
        body, name=name, grid=grid, in_specs=specs + [hbm] * no, out_specs=out_specs + (hbm,) * no,
        out_shape=out_shapes + ex_shapes(outgoing),
        scratch_shapes=scratch + (ex_sems(no) if no else []),
        compiler_params=_cparams("arbitrary", "arbitrary", "arbitrary"))(*ins, *outgoing))
    nout = len(out_shapes)
    return res_[:nout] + ((res_[nout:],) if no else ())


def _ffn_up(h, wg, wu, name, tm, incoming=()):
    t, d = h.shape
    nc, _, fs = wg.shape
    tm = min(tm, t)
    nt = t // tm
    ni = len(incoming)
    halves = _halves(incoming)

    def body(*refs):
        h_ref, wg_ref, wu_ref = refs[:3]
        srcs = refs[3:3 + ni]
        g_ref, u_ref, a_ref = refs[3 + ni:6 + ni]
        outs = refs[6 + ni:6 + 2 * ni]
        if ni:
            send_sems, recv_sems = refs[6 + 2 * ni:]
            c, i = pl.program_id(0), pl.program_id(1)

            @pl.when((c == 0) & (i == 0))
            def _():
                _gather_start(srcs, outs, halves, send_sems, recv_sems)

        hh = h_ref[...]
        gate = _dot(hh, wg_ref[...], NN)
        up = _dot(hh, wu_ref[...], NN)
        sig = jax.nn.sigmoid(gate)
        silu = gate * sig
        g_ref[...] = (up * (sig + silu * (1.0 - sig))).astype(BF16)
        u_ref[...] = silu.astype(BF16)
        a_ref[...] = (silu * up).astype(BF16)

        if ni:
            @pl.when((c == nc - 1) & (i == nt - 1))
            def _():
                _gather_wait(outs, halves, send_sems, recv_sems)

    wspec = pl.BlockSpec((None, d, fs), lambda c, i: (c, 0, 0))
    ospec = pl.BlockSpec((None, tm, fs), lambda c, i: (c, i, 0))
    hbm = pl.BlockSpec(memory_space=pltpu.HBM)
    osd = _sds((nc, t, fs), BF16)
    res = tuple(_pcall(
        body, name=name, grid=(nc, nt),
        in_specs=[pl.BlockSpec((tm, d), lambda c, i: (i, 0)), wspec, wspec] + [hbm] * ni,
        out_specs=(ospec, ospec, ospec) + (hbm,) * ni,
        out_shape=(osd, osd, osd) + tuple(_sds((N_CHIPS,) + b.shape, b.dtype) for b in incoming),
        scratch_shapes=[pltpu.SemaphoreType.DMA((3 * ni,)), pltpu.SemaphoreType.DMA((3 * ni,))] if ni else [],
        compiler_params=_cparams("arbitrary", "arbitrary"))(h, wg, wu, *incoming))
    return res[:3] + (res[3:],)


def _ffn_hidden_bwd(dy, h, wd, dact_dgate, dact_dup, act, name, tm, outgoing=()):
    t, d = dy.shape
    nc, fs, _ = wd.shape
    tm = min(tm, t)
    nt = t // tm
    no = len(outgoing)

    def body(*refs):
        dy_ref, h_ref, wd_ref, g_ref, u_ref, a_ref = refs[:6]
        sent = refs[6:6 + no]
        dg_ref, du_ref, dwg_hbm, dwu_hbm, dwd_hbm = refs[6 + no:11 + no]
        arrived = refs[11 + no:11 + 2 * no]
        wg_acc, wu_acc, wd_acc, sem = refs[11 + 2 * no:15 + 2 * no]
        c, i = pl.program_id(0), pl.program_id(1)
        if no:
            send_sems, recv_sems = refs[15 + 2 * no:]

            @pl.when((c == 0) & (i == 0))
            def _():
                _scatter_start(sent, arrived, send_sems, recv_sems)

        dyb = dy_ref[...].astype(BF16)
        da = _dot(dyb, wd_ref[...], NT) * FFN_RESID
        dgate = (da * g_ref[...].astype(F32)).astype(BF16)
        dup = (da * u_ref[...].astype(F32)).astype(BF16)
        dg_ref[...] = dgate
        du_ref[...] = dup
        hh = h_ref[...]
        parts = (_dot(hh, dgate, TN), _dot(hh, dup, TN), _dot(a_ref[...], dyb, TN) * FFN_RESID)
        accs = (wg_acc, wu_acc, wd_acc)

        @pl.when(i == 0)
        def _():
            for acc, part in zip(accs, parts):
                acc[...] = part

        @pl.when(i > 0)
        def _():
            for acc, part in zip(accs, parts):
                acc[...] += part

        @pl.when(i == nt - 1)
        def _():
            copies = [pltpu.make_async_copy(acc, out.at[c], sem.at[n])
                      for n, (acc, out) in enumerate(zip(accs, (dwg_hbm, dwu_hbm, dwd_hbm)))]
            for cp in copies:
                cp.start()
            for cp in copies:
                cp.wait()

        if no:
            @pl.when((c == nc - 1) & (i == nt - 1))
            def _():
                _scatter_wait(sent, arrived, send_sems, recv_sems)

    tok = pl.BlockSpec((tm, d), lambda c, i: (i, 0))
    cspec = pl.BlockSpec((None, tm, fs), lambda c, i: (c, i, 0))
    hbm = pl.BlockSpec(memory_space=pltpu.HBM)
    osd = _sds((nc, t, fs), BF16)
    res = _pcall(
        body, name=name, grid=(nc, nt),
        in_specs=[tok, tok, pl.BlockSpec((None, fs, d), lambda c, i: (c, 0, 0)), cspec, cspec, cspec] + [hbm] * no,
        out_specs=(cspec, cspec, hbm, hbm, hbm) + (hbm,) * no,
        out_shape=(osd, osd, _sds((nc, d, fs), F32), _sds((nc, d, fs), F32), _sds((nc, fs, d), F32))
        + _scatter_shapes(outgoing),
        scratch_shapes=[pltpu.VMEM((d, fs), F32), pltpu.VMEM((d, fs), F32), pltpu.VMEM((fs, d), F32),
                        pltpu.SemaphoreType.DMA((3,))] + (_scatter_sems(no) if no else []),
        compiler_params=_cparams("arbitrary", "arbitrary"))(dy, h, wd, dact_dgate, dact_dup, act, *outgoing)
    res = tuple(res)
    return res[:5] + (res[5:],)


def _ffn_fwd(x, g, wg, wu, wd, tag, incoming=(), target=None):
    t = x.shape[0]
    nc, _, fs = wg.shape
    tm = min(512, t)
    h = _rms_fwd(x, g, BF16, f"{tag}_norm", 1024)
    behind_down = tuple(incoming[-1:])
    dact_dgate, dact_dup, act, partly = _ffn_up(h, wg, wu, f"{tag}_up", 1024, tuple(incoming[:-1]))
    if target is not None:
        return _ffn_down_loss(act, wd, x, target, f"{tag}_down_loss", 512), (h, dact_dgate, dact_dup, act), partly
    pairs = [(act, pl.BlockSpec((None, tm, fs), lambda i, j, r, c=c: (c, i, 0)),
              wd, pl.BlockSpec((None, fs, D_MODEL), lambda i, j, r, c=c: (c, 0, 0))) for c in range(nc)]
    row = pl.BlockSpec((tm, D_MODEL), lambda i, j, r: (i, 0))
    y = _mm(f"{tag}_down", (t // tm, 1, 1), pairs, NN, _sds((t, D_MODEL), F32), row, (tm, D_MODEL),
            res=(x, row), scale=FFN_RESID, outgoing=behind_down, exchange="gather")
    if behind_down:
        y, more = y
        partly = tuple(partly) + tuple(more)
    return y, (h, dact_dgate, dact_dup, act), partly


def _ffn_bwd(dy, x, g, wg, wu, wd, saved, tag, outgoing=(), own_exchange=None):
    h, dact_dgate, dact_dup, act = saved
    t = x.shape[0]
    nc, _, fs = wg.shape
    tm = min(512, t)
    dgate, dup, dwg, dwu, dwd, arrived = _ffn_hidden_bwd(dy, h, wd, dact_dgate, dact_dup, act,
                                                         f"{tag}_hidden_bwd", 1024, outgoing)
    pairs = []
    for c in range(nc):
        a_spec = pl.BlockSpec((None, tm, fs), lambda i, j, r, c=c: (c, i, 0))
        w_spec = pl.BlockSpec((None, D_MODEL, fs), lambda i, j, r, c=c: (c, 0, 0))
        pairs += [(dgate, a_spec, wg, w_spec), (dup, a_spec, wu, w_spec)]
    own_part = tuple(own_exchange([dwg, dwu, dwd])) if own_exchange else ()
    row = pl.BlockSpec((tm, D_MODEL), lambda i, j, r: (i, 0))
    res = _mm(f"{tag}_dh", (t // tm, 1, 1), pairs, NT, _sds((t, D_MODEL), F32), row, (tm, D_MODEL),
              res=(dy, row), norm=(x, g), outgoing=own_part)
    dx, dg = res[0], res[1]
    own_got = res[2] if own_part else ()
    return dx, dg, dwg, dwu, dwd, arrived, (own_part, own_got)


def _mm_tn_multi(name, a, bs, tk):
    k, m = a.shape
    tk = min(tk, k)
    nb = len(bs)

    def body(*refs):
        a_ref, b_refs, o_refs = refs[0], refs[1:1 + nb], refs[1 + nb:]
        aa = a_ref[...].astype(BF16)
        parts = [_dot(aa, b_ref[...].astype(BF16), TN) for b_ref in b_refs]

        @pl.when(pl.program_id(0) == 0)
        def _():
            for o_ref, part in zip(o_refs, parts):
                o_ref[...] = part

        @pl.when(pl.program_id(0) > 0)
        def _():
            for o_ref, part in zip(o_refs, parts):
                o_ref[...] += part

    return _pcall(
        body, name=name, grid=(k // tk,),
        in_specs=[pl.BlockSpec((tk, m), lambda r: (r, 0))] + [pl.BlockSpec((tk, b.shape[1]), lambda r: (r, 0)) for b in bs],
        out_specs=tuple(pl.BlockSpec((m, b.shape[1]), lambda r: (0, 0)) for b in bs),
        out_shape=tuple(_sds((m, b.shape[1]), F32) for b in bs),
        compiler_params=_cparams("arbitrary"))(a, *bs)


def _mm_simple(name, a, b, dn, out_dtype, tm=512, tk=512, res=None, scale=1.0):
    if dn == TN:
        k, m = a.shape
        n = b.shape[1]
        tk = min(tk, k)
        return _mm(name, (1, 1, k // tk),
                   [(a, pl.BlockSpec((tk, m), lambda i, j, r: (r, 0)), b, pl.BlockSpec((tk, n), lambda i, j, r: (r, 0)))],
                   TN, _sds((m, n), out_dtype), pl.BlockSpec((m, n), lambda i, j, r: (0, 0)), (m, n), scale=scale)
    m, k = a.shape
    n = b.shape[1] if dn == NN else b.shape[0]
    tm = min(tm, m)
    row = pl.BlockSpec((tm, n), lambda i, j, r: (i, 0))
    return _mm(name, (m // tm, 1, 1),
               [(a, pl.BlockSpec((tm, k), lambda i, j, r: (i, 0)), b, pl.BlockSpec(b.shape, lambda i, j, r: (0, 0)))],
               dn, _sds((m, n), out_dtype), row, (tm, n), res=None if res is None else (res, row), scale=scale)


def _t5_bucket(dist):
    max_exact = REL_BUCKETS // 2
    d = np.maximum(dist, 1).astype(np.float32)
    large = max_exact + (np.log(d / max_exact) / np.log(REL_MAX_DIST / max_exact)
                         * (REL_BUCKETS - max_exact)).astype(np.int32)
    large = np.minimum(large, REL_BUCKETS - 1)
    return np.where(dist < max_exact, dist, large).astype(np.int32)


def _bucket_tiles():
    i = np.arange(QB)[:, None]
    j = np.arange(QB + DIL_W)[None, :]
    delta = np.clip(i + DIL_W - j, 0, None)
    return np.stack([_t5_bucket(delta * dil) for dil in DIL_DILATIONS]).astype(np.int32)


def _bias_tiles(rel_bias):
    buckets = jnp.asarray(_bucket_tiles())

    def body(rb_ref, bk_ref, o_ref):
        bk = bk_ref[...]
        for h in range(DIL_HEADS):
            def pick(b, tile):
                return jnp.where(bk == b, rb_ref[h, b], tile)

            o_ref[h] = lax.fori_loop(0, REL_BUCKETS, pick, jnp.zeros((QB, QB + DIL_W), F32))

    return _pcall(
        body, name="dil_bias_tiles", grid=(3,),
        in_specs=[pl.BlockSpec(memory_space=pltpu.SMEM),
                  pl.BlockSpec((None, QB, QB + DIL_W), lambda b: (b, 0, 0))],
        out_specs=pl.BlockSpec((None, DIL_HEADS, QB, QB + DIL_W), lambda b: (b, 0, 0, 0)),
        out_shape=_sds((3, DIL_HEADS, QB, QB + DIL_W), F32),
        compiler_params=_cparams("parallel"))(rel_bias, buckets)


def _bias_grad(dtiles):
    buckets = jnp.asarray(_bucket_tiles())

    def body(dt_ref, bk_ref, o_ref):
        def one(b, carry):
            hit = [bk_ref[br] == b for br in range(3)]
            for h in range(DIL_HEADS):
                tot = jnp.zeros((), F32)
                for br in range(3):
                    tot = tot + jnp.sum(jnp.where(hit[br], dt_ref[br, h], 0.0))
                o_ref[h, b] = tot
            return carry

        lax.fori_loop(0, REL_BUCKETS, one, 0)

    return _pcall(
        body, name="dil_bias_grad",
        in_specs=[pl.BlockSpec(memory_space=pltpu.VMEM), pl.BlockSpec(memory_space=pltpu.VMEM)],
        out_specs=pl.BlockSpec(memory_space=pltpu.SMEM),
        out_shape=_sds((DIL_HEADS, REL_BUCKETS), F32))(dtiles, buckets)


def _split_heads(a, lo):
    zero = jnp.zeros_like(a)
    return jnp.concatenate([jnp.where(lo, a, zero), jnp.where(lo, zero, a)], axis=0)


def _side_by_side(a):
    n = a.shape[0] // 2
    return jnp.concatenate([a[:n], a[n:]], axis=1)


def _band_masks(prev_ok):
    ii = lax.broadcasted_iota(jnp.int32, (2 * QB, QB), 0) & (QB - 1)
    jj = lax.broadcasted_iota(jnp.int32, (2 * QB, QB), 1)
    return jj <= ii, jj >= ii + jnp.where(prev_ok, 0, QB)


def _dil_fwd(q, k, v, bias, dil, name):
    w = DIL_WIDTH
    t = q.shape[0] * dil
    npair = w // LANES
    nl = t // dil // QB
    scale = DIL_HD ** -0.5

    def body(q_ref, kc_ref, kp_ref, vc_ref, vp_ref, b_ref, o_ref, lse_ref):
        nn = pl.program_id(1)
        lo = lax.broadcasted_iota(jnp.int32, (QB, LANES), 1) < DIL_HD
        lo2 = lax.broadcasted_iota(jnp.int32, (2 * QB, LANES), 1) < DIL_HD
        ii = lax.broadcasted_iota(jnp.int32, (2 * QB, 2 * QB), 0) & (QB - 1)
        jj = lax.broadcasted_iota(jnp.int32, (2 * QB, 2 * QB), 1)
        first_key = jnp.maximum(ii, jnp.where(nn != 0, 0, QB))
        valid = (jj >= first_key) & (jj <= ii + QB)
        for p in range(npair):
            cols = slice(p * LANES, (p + 1) * LANES)
            qq = _split_heads(q_ref[:, cols], lo)
            kk = jnp.concatenate([kp_ref[:, cols], kc_ref[:, cols]], axis=0)
            vv = jnp.concatenate([vp_ref[:, cols], vc_ref[:, cols]], axis=0)
            s = jnp.where(valid, _dot(qq, kk, NT) * scale + b_ref[p], NEG)
            m = jnp.max(s, axis=-1, keepdims=True)
            e = jnp.exp(s - m)
            den = jnp.sum(e, axis=-1, keepdims=True)
            pn = (e * (1.0 / den)).astype(BF16)
            o_ref[:, cols] = _dot(_side_by_side(pn), _split_heads(vv, lo2), NN)
            lse = m + jnp.log(den)
            lse_ref[:, cols] = jnp.where(lo, lse[:QB], lse[QB:])

    cur = pl.BlockSpec((QB, w), lambda r, n: (n, r))
    prev = pl.BlockSpec((QB, w), lambda r, n: (jnp.maximum(n - 1, 0), r))
    sd = _sds((t // dil, dil * w), F32)
    return _pcall(
        body, name=name, grid=(dil, nl),
        in_specs=[cur, cur, prev, cur, prev, pl.BlockSpec((npair, 2 * QB, 2 * QB), lambda r, n: (0, 0, 0))],
        out_specs=(cur, cur), out_shape=(sd, sd),
        compiler_params=_cparams("parallel", "parallel"))(q, k, k, v, v, bias)


def _dil_bwd(q, k, v, do, stats, bias, dil, name):
    w = DIL_WIDTH
    t = q.shape[0] * dil
    npair = w // LANES
    nl = t // dil // QB
    scale = DIL_HD ** -0.5

    def body(qc_ref, qn_ref, doc_ref, don_ref, sc_ref, sn_ref, k_ref, v_ref, b_ref,
             dq_ref, dk_ref, dv_ref, db_ref, carry):
        r, nn = pl.program_id(0), pl.program_id(1)
        lo = lax.broadcasted_iota(jnp.int32, (QB, LANES), 1) < DIL_HD
        cur_ok, prev_ok = _band_masks(nn + 1 < nl)

        @pl.when((r == 0) & (nn == 0))
        def _():
            db_ref[...] = jnp.zeros_like(db_ref)
            carry[...] = jnp.zeros_like(carry)

        for p in range(npair):
            cols = slice(p * LANES, (p + 1) * LANES)
            kp, vp = k_ref[:, cols], v_ref[:, cols]
            k2 = _split_heads(kp, lo)

            def column(ref, lane):
                first = p * LANES + lane
                return jnp.concatenate([ref[:, first:first + 1], ref[:, first + DIL_HD:first + DIL_HD + 1]], axis=0)

            def side(q_ref, do_ref, s_ref, bias, ok):
                qq = _split_heads(q_ref[:, cols], lo)
                dd = _split_heads(do_ref[:, cols], lo)
                s = jnp.where(ok, _dot(qq, kp, NT) * scale + bias, NEG)
                prob = jnp.exp(s - column(s_ref, 0))
                ds = prob * (_dot(dd, vp, NT) - column(s_ref, DIL_HD // 2))
                return qq, dd, prob.astype(BF16), ds

            q1, d1, p1, ds1 = side(qc_ref, doc_ref, sc_ref, b_ref[p, :, QB:], cur_ok)
            q2, d2, p2, ds2 = side(qn_ref, don_ref, sn_ref, b_ref[p, :, :QB], prev_ok)
            ds1b, ds2b = ds1.astype(BF16), ds2.astype(BF16)
            dq_ref[:, cols] = carry[:, cols] + _dot(_side_by_side(ds1b), k2, NN) * scale
            carry[:, cols] = _dot(_side_by_side(ds2b), k2, NN) * scale
            dk_ref[:, cols] = _dot(jnp.concatenate([ds1b, ds2b], axis=0), jnp.concatenate([q1, q2], axis=0), TN) * scale
            dv_ref[:, cols] = _dot(jnp.concatenate([p1, p2], axis=0), jnp.concatenate([d1, d2], axis=0), TN)
            db_ref[p, :, QB:] += ds1
            db_ref[p, :, :QB] += ds2

    cur = pl.BlockSpec((QB, w), lambda r, n: (n, r))
    nxt = pl.BlockSpec((QB, w), lambda r, n: (jnp.minimum(n + 1, nl - 1), r))
    tile = pl.BlockSpec((npair, 2 * QB, 2 * QB), lambda r, n: (0, 0, 0))
    sd = _sds((t // dil, dil * w), F32)
    return _pcall(
        body, name=name, grid=(dil, nl),
        in_specs=[cur, nxt, cur, nxt, cur, nxt, cur, cur, tile],
        out_specs=(cur, cur, cur, tile),
        out_shape=(sd, sd, sd, _sds((npair, 2 * QB, 2 * QB), F32)),
        scratch_shapes=[pltpu.VMEM((QB, w), F32)],
        compiler_params=_cparams("arbitrary", "arbitrary"))(q, q, do, do, stats, stats, k, v, bias)


def _head_sum_matrix(scale):
    idx = np.arange(DIL_WIDTH) // DIL_HD
    return jnp.asarray((idx[:, None] == idx[None, :]).astype(np.float32) * scale, BF16)


def _head_sum(x, mat):
    hi = x.astype(BF16)
    lo = (x - hi.astype(F32)).astype(BF16)
    return _dot(hi, mat, NN) + _dot(lo, mat, NN)


def _to_views(src, tmp, out_refs):
    tm, w = src.shape
    for j in range(w // LANES):
        tmp[j] = src[:, j * LANES:(j + 1) * LANES]
    for d, o_ref in zip(DIL_DILATIONS, out_refs):
        if d == 1:
            o_ref[...] = src.astype(o_ref.dtype)
            continue
        for r in range(d):
            for j in range(w // LANES):
                lo = r * w + j * LANES
                o_ref[:, lo:lo + LANES] = tmp[j, pl.ds(r, tm // d, stride=d), :].astype(o_ref.dtype)


def _from_view(v_ref, tmp, d):
    tm = tmp.shape[1]
    w = v_ref.shape[1] // d
    for r in range(d):
        for j in range(w // LANES):
            lo = r * w + j * LANES
            tmp[j, pl.ds(r, tm // d, stride=d), :] = v_ref[:, lo:lo + LANES]
    return jnp.concatenate([tmp[j] for j in range(w // LANES)], axis=1)


def _view_specs(tm, t, dtype):
    specs = tuple(pl.BlockSpec((tm // d, d * DIL_WIDTH), lambda i: (i, 0)) for d in DIL_DILATIONS)
    shapes = tuple(_sds((t // d, d * DIL_WIDTH), dtype) for d in DIL_DILATIONS)
    return specs, shapes


def _view_scratch(tm):
    return pltpu.VMEM((DIL_WIDTH // LANES, tm, LANES), F32)


def _dil_merge(outs, lses, g, tm):
    w = DIL_WIDTH
    t = outs[0].shape[0]
    tm = min(tm, t)

    def body(o0, o1, o2, l0, l1, l2, g_ref, o_ref, l_ref, n_ref, so1, so2, sl1, sl2):
        d1, d2 = DIL_DILATIONS[1], DIL_DILATIONS[2]
        a0, a1, a2 = l0[...], _from_view(l1, sl1, d1), _from_view(l2, sl2, d2)
        m = jnp.maximum(jnp.maximum(a0, a1), a2)
        e0, e1, e2 = jnp.exp(a0 - m), jnp.exp(a1 - m), jnp.exp(a2 - m)
        den = e0 + e1 + e2
        o = (e0 * o0[...] + e1 * _from_view(o1, so1, d1) + e2 * _from_view(o2, so2, d2)) / den
        o_ref[...] = o
        l_ref[...] = m + jnp.log(den)
        r = lax.rsqrt(jnp.mean(o * o, axis=-1, keepdims=True) + EPS)
        n_ref[...] = (o * r * g_ref[...]).astype(n_ref.dtype)

    specs, _ = _view_specs(tm, t, F32)
    spec = pl.BlockSpec((tm, w), lambda i: (i, 0))
    return _pcall(
        body, name="dil_merge", grid=(t // tm,),
        in_specs=list(specs) * 2 + [pl.BlockSpec((1, w), lambda i: (0, 0))], out_specs=(spec, spec, spec),
        out_shape=(_sds((t, w), F32), _sds((t, w), F32), _sds((t, w), BF16)),
        scratch_shapes=[_view_scratch(tm)] * 4,
        compiler_params=_cparams("parallel"))(*outs, *lses, g)


def _dil_stats(do, o, lse, tm):
    t, w = do.shape
    tm = min(tm, t)

    def body(a_ref, b_ref, l_ref, m_ref, s1, s4, s16, d1, d4, d16, tmp):
        first = (lax.broadcasted_iota(jnp.int32, (tm, w), 1) & (DIL_HD - 1)) < DIL_HD // 2
        do_ = a_ref[...]
        _to_views(jnp.where(first, l_ref[...], _head_sum(do_ * b_ref[...], m_ref[...])), tmp, (s1, s4, s16))
        _to_views(do_, tmp, (d1, d4, d16))

    spec = pl.BlockSpec((tm, w), lambda i: (i, 0))
    f_specs, f_shapes = _view_specs(tm, t, F32)
    b_specs, b_shapes = _view_specs(tm, t, BF16)
    res = _pcall(body, name="dil_stats", grid=(t // tm,),
                 in_specs=[spec, spec, spec, pl.BlockSpec((w, w), lambda i: (0, 0))],
                 out_specs=f_specs + b_specs, out_shape=f_shapes + b_shapes,
                 scratch_shapes=[_view_scratch(tm)],
                 compiler_params=_cparams("parallel"))(do, o, lse, _head_sum_matrix(1.0))
    return res[:3], res[3:]


def _head_norm_fwd(x, col, g, name, tm):
    t = x.shape[0]
    w = DIL_WIDTH
    tm = min(tm, t)
    normed = g is not None

    def body(*refs):
        outs, tmp = refs[-4:-1], refs[-1]
        xf = refs[0][...]
        if normed:
            g_ref, m_ref = refs[1], refs[2]
            xf = xf * lax.rsqrt(_head_sum(xf * xf, m_ref[...]) + EPS) * g_ref[...]
        _to_views(xf, tmp, outs)

    specs, shapes = _view_specs(tm, t, BF16)
    extra = [g, _head_sum_matrix(1.0 / DIL_HD)] if normed else []
    extra_specs = [pl.BlockSpec((1, w), lambda i: (0, 0)), pl.BlockSpec((w, w), lambda i: (0, 0))] if normed else []
    return _pcall(
        body, name=name, grid=(t // tm,),
        in_specs=[pl.BlockSpec((tm, w), lambda i: (i, col))] + extra_specs,
        out_specs=specs, out_shape=shapes, scratch_shapes=[_view_scratch(tm)],
        compiler_params=_cparams("parallel"))(x, *extra)


def _head_norm_bwd(dys, x, col, g, name, tm):
    t = x.shape[0]
    w = DIL_WIDTH
    tm = min(tm, t)
    nd = len(dys)
    nt = t // tm
    lane = np.arange(w) % DIL_HD
    fold = jnp.asarray((lane[:, None] == lane[None, :]).astype(np.float32))

    def body(*refs):
        x_ref, g_ref, m_ref, f_ref = refs[nd:nd + 4]
        dx_ref, dg_ref, s1, s2 = refs[-4:]
        dy = refs[0][...] + _from_view(refs[1], s1, DIL_DILATIONS[1]) + _from_view(refs[2], s2, DIL_DILATIONS[2])
        xf = x_ref[...]
        mat = m_ref[...]
        r = lax.rsqrt(_head_sum(xf * xf, mat) + EPS)
        xh = xf * r
        dxh = dy * g_ref[...]
        dx_ref[...] = r * (dxh - xh * _head_sum(dxh * xh, mat))

        @pl.when(pl.program_id(0) == 0)
        def _():
            dg_ref[...] = jnp.zeros_like(dg_ref)

        dg_ref[...] += jnp.sum(dy * xh, axis=0, keepdims=True)

        @pl.when(pl.program_id(0) == nt - 1)
        def _():
            per_lane = jnp.broadcast_to(dg_ref[...], (8, w))
            dg_ref[...] = lax.dot_general(per_lane, f_ref[...], NN, precision=lax.Precision.HIGHEST,
                                          preferred_element_type=F32)[0:1]

    row = pl.BlockSpec((tm, w), lambda i: (i, 0))
    vec = pl.BlockSpec((1, w), lambda i: (0, 0))
    sq = pl.BlockSpec((w, w), lambda i: (0, 0))
    views, _ = _view_specs(tm, t, F32)
    return _pcall(
        body, name=name, grid=(nt,),
        in_specs=list(views) + [pl.BlockSpec((tm, w), lambda i: (i, col)), vec, sq, sq],
        out_specs=(row, vec), out_shape=(_sds((t, w), F32), _sds((1, w), F32)),
        scratch_shapes=[_view_scratch(tm)] * 2,
        compiler_params=_cparams("arbitrary"))(*dys, x, g, _head_sum_matrix(1.0 / DIL_HD), fold)


def _rowdot(a, b, name, tm):
    n, d = a.shape
    tm = min(tm, n)

    def body(a_ref, b_ref, o_ref):
        o_ref[...] = jnp.sum(a_ref[...].astype(F32) * b_ref[...].astype(F32), axis=-1, keepdims=True)

    spec = pl.BlockSpec((tm, d), lambda i: (i, 0))
    return _pcall(body, name=name, grid=(n // tm,), in_specs=[spec, spec],
                  out_specs=pl.BlockSpec((tm, 1), lambda i: (i, 0)), out_shape=_sds((n, 1), F32),
                  compiler_params=_cparams("parallel"))(a, b)


def _sum_branches(parts, name, tm):
    t = parts[0].shape[0]
    w = DIL_WIDTH
    tm = min(tm, t)

    def body(a_ref, b_ref, c_ref, o_ref, s1, s2):
        o_ref[...] = a_ref[...] + _from_view(b_ref, s1, DIL_DILATIONS[1]) + _from_view(c_ref, s2, DIL_DILATIONS[2])

    views, _ = _view_specs(tm, t, F32)
    return _pcall(body, name=name, grid=(t // tm,), in_specs=list(views),
                  out_specs=pl.BlockSpec((tm, w), lambda i: (i, 0)), out_shape=_sds((t, w), F32),
                  scratch_shapes=[_view_scratch(tm)] * 2,
                  compiler_params=_cparams("parallel"))(*parts)


def _rope_tables(t):
    inv = ROPE_BASE ** (-np.arange(0, MLA_ROPE, 2, dtype=np.float64) / MLA_ROPE)
    ang = np.arange(t, dtype=np.float64)[:, None] * inv[None, :]
    cos, sin = np.cos(ang), np.sin(ang)
    return (jnp.asarray(np.concatenate([cos, cos], 1), F32), jnp.asarray(np.concatenate([-sin, sin], 1), F32))


def _swap_halves(a):
    half = MLA_ROPE // 2
    return jnp.concatenate([a[:, half:], a[:, :half]], axis=1)


def _qk_parts(x, pe, tm, nt):
    if pe is None:
        return None
    return (pl.BlockSpec((tm, MLA_NOPE), lambda i: (i, 0)), pl.BlockSpec((tm, MLA_ROPE), lambda i: (i % nt, 0)))


def _mla_qk_fwd(x, g, cos_t, sin_t, scale, name, tm, pe=None):
    n = x.shape[0]
    d = MLA_QK
    t = cos_t.shape[0]
    tm = min(tm, t)
    nt = t // tm
    split = _qk_parts(x, pe, tm, nt)

    def transposed(a):
        w = a.shape[1]
        eye = (lax.broadcasted_iota(jnp.int32, (w, w), 0) == lax.broadcasted_iota(jnp.int32, (w, w), 1)).astype(BF16)
        return _dot(eye, a, NT).astype(BF16)

    def body(*refs):
        if split:
            xn_ref, xr_ref, xv_ref, g_ref, c_ref, s_ref, o_ref, ot_ref, v_ref = refs
            xn, xr = xn_ref[...], xr_ref[...]
            v_ref[...] = xv_ref[...].astype(v_ref.dtype)
        else:
            x_ref, g_ref, c_ref, s_ref, o_ref = refs
            xf = x_ref[...]
            xn, xr = xf[:, :MLA_NOPE], xf[:, MLA_NOPE:]
        ms = (jnp.sum(xn * xn, axis=-1, keepdims=True) + jnp.sum(xr * xr, axis=-1, keepdims=True)) * (1.0 / d)
        r = lax.rsqrt(ms + EPS)
        gg = g_ref[...]
        yn = xn * r * gg[:, :MLA_NOPE]
        yr = xr * r * gg[:, MLA_NOPE:]
        on = (yn * scale).astype(o_ref.dtype)
        orot = ((yr * c_ref[...] + _swap_halves(yr) * s_ref[...]) * scale).astype(o_ref.dtype)
        o_ref[:, :MLA_NOPE] = on
        o_ref[:, MLA_NOPE:] = orot
        if split:
            ot_ref[:MLA_NOPE, :] = transposed(on)
            ot_ref[MLA_NOPE:, :] = transposed(orot)

    row = pl.BlockSpec((tm, d), lambda i: (i, 0))
    vec = pl.BlockSpec((1, d), lambda i: (0, 0))
    tab = pl.BlockSpec((tm, MLA_ROPE), lambda i: (i % nt, 0))
    if not split:
        return _pcall(body, name=name, grid=(n // tm,), in_specs=[row, vec, tab, tab],
                      out_specs=row, out_shape=_sds((n, d), BF16),
                      compiler_params=_cparams("parallel"))(x, g, cos_t, sin_t)
    vals = pl.BlockSpec((tm, MLA_V), lambda i: (i, 1))
    return _pcall(body, name=name, grid=(n // tm,), in_specs=[split[0], split[1], vals, vec, tab, tab],
                  out_specs=(row, pl.BlockSpec((None, d, tm), lambda i: (i // nt, 0, i % nt)),
                             pl.BlockSpec((tm, MLA_V), lambda i: (i, 0))),
                  out_shape=(_sds((n, d), BF16), _sds((n // t, d, t), BF16), _sds((n, MLA_V), BF16)),
                  compiler_params=_cparams("parallel"))(x, pe, x, g, cos_t, sin_t)


def _mla_qk_bwd(dy, x, g, cos_t, sin_t, scale, name, tm, pe=None):
    n = x.shape[0]
    d = MLA_QK
    t = cos_t.shape[0]
    tm = min(tm, t)
    nt = t // tm
    split = _qk_parts(x, pe, tm, nt)

    def body(*refs):
        if split:
            dy_ref, xn_ref, xr_ref, g_ref, c_ref, s_ref, dxn_ref, dxr_ref, dg_ref = refs
            xn, xr = xn_ref[...], xr_ref[...]
        else:
            dy_ref, x_ref, g_ref, c_ref, s_ref, dx_ref, dg_ref = refs
            xf = x_ref[...]
            xn, xr = xf[:, :MLA_NOPE], xf[:, MLA_NOPE:]
        gg = g_ref[...]
        ms = (jnp.sum(xn * xn, axis=-1, keepdims=True) + jnp.sum(xr * xr, axis=-1, keepdims=True)) * (1.0 / d)
        r = lax.rsqrt(ms + EPS)
        xh_n, xh_r = xn * r, xr * r
        dyf = dy_ref[...] * scale
        dyr = dyf[:, MLA_NOPE:]
        dn_n = dyf[:, :MLA_NOPE]
        dn_r = dyr * c_ref[...] + _swap_halves(dyr * s_ref[...])
        dxh_n = dn_n * gg[:, :MLA_NOPE]
        dxh_r = dn_r * gg[:, MLA_NOPE:]
        mean = (jnp.sum(dxh_n * xh_n, axis=-1, keepdims=True)
                + jnp.sum(dxh_r * xh_r, axis=-1, keepdims=True)) * (1.0 / d)
        dx_n = r * (dxh_n - xh_n * mean)
        dx_r = r * (dxh_r - xh_r * mean)
        if split:
            dxn_ref[...] = dx_n
            dxr_ref[...] = dx_r
        else:
            dx_ref[:, :MLA_NOPE] = dx_n
            dx_ref[:, MLA_NOPE:] = dx_r

        @pl.when(pl.program_id(0) == 0)
        def _():
            dg_ref[...] = jnp.zeros_like(dg_ref)

        dg_ref[:, :MLA_NOPE] += jnp.sum(dn_n * xh_n, axis=0, keepdims=True)
        dg_ref[:, MLA_NOPE:] += jnp.sum(dn_r * xh_r, axis=0, keepdims=True)

    row = pl.BlockSpec((tm, d), lambda i: (i, 0))
    vec = pl.BlockSpec((1, d), lambda i: (0, 0))
    tab = pl.BlockSpec((tm, MLA_ROPE), lambda i: (i % nt, 0))
    if not split:
        return _pcall(body, name=name, grid=(n // tm,), in_specs=[row, row, vec, tab, tab],
                      out_specs=(row, vec), out_shape=(_sds((n, d), F32), _sds((1, d), F32)),
                      compiler_params=_cparams("arbitrary"))(dy, x, g, cos_t, sin_t)
    outs = (pl.BlockSpec((tm, MLA_NOPE), lambda i: (i, 0)), pl.BlockSpec((tm, MLA_ROPE), lambda i: (i, 0)), vec)
    return _pcall(body, name=name, grid=(n // tm,), in_specs=[row, split[0], split[1], vec, tab, tab],
                  out_specs=outs, out_shape=(_sds((n, MLA_NOPE), F32), _sds((n, MLA_ROPE), F32), _sds((1, d), F32)),
                  compiler_params=_cparams("arbitrary"))(dy, x, pe, g, cos_t, sin_t)


def _causal_mask(i, j, tq, tk, width):
    row = i * tq + lax.broadcasted_iota(jnp.int32, (tq, width), 0)
    col = j * tk + lax.broadcasted_iota(jnp.int32, (tq, width), 1)
    return col <= row


def _causal_steps(nq, nk, tq, tk, q_major):
    if q_major:
        groups = [[(i, j) for j in range((i * tq + tq - 1) // tk + 1)] for i in range(nq)]
        nunit = tk // tq if tk % tq == 0 else 1
    else:
        groups = [[(i, j) for i in range((j * tk) // tq, nq)] for j in range(nk)]
        nunit = tq // tk if tq % tk == 0 else 1
    it, jt, fl = [], [], []
    for g in groups:
        for n, (i, j) in enumerate(g):
            crossing = j * tk + tk - 1 > i * tq
            if q_major:
                unit = tk // nunit
                u = min(nunit, -(-(i * tq + tq - j * tk) // unit)) - 1
            else:
                unit = tq // nunit
                u = max(0, j * tk - i * tq) // unit
            it.append(i)
            jt.append(j)
            fl.append((n == 0) + 2 * (n == len(g) - 1) + 4 * crossing + 8 * (u if crossing else 0))
    return tuple(jnp.asarray(np.array(a, np.int32)) for a in (it, jt, fl)), nunit


def _by_crossing(flags, nunit, update):
    pl.when((flags & 4) == 0)(functools.partial(update, None))
    for u in range(nunit):
        pl.when(((flags & 4) != 0) & ((flags >> 3) == u))(functools.partial(update, u))


def _causal_specs(tq, tk):
    def qs(w):
        return pl.BlockSpec((None, tq, w), lambda h, s, it, jt, fl: (h, it[s], 0))

    def kv(w):
        return pl.BlockSpec((None, tk, w), lambda h, s, it, jt, fl: (h, jt[s], 0))

    return qs, kv


def _mla_fwd(q, k, v, tq, tk):
    nh, t, dq = q.shape
    dv = v.shape[2]
    tq, tk = min(tq, t), min(tk, t)
    tables, nunit = _causal_steps(t // tq, t // tk, tq, tk, True)

    def body(it, jt, fl, q_ref, k_ref, v_ref, o_ref, lse_ref, m_sc, l_sc, acc_sc):
        step = pl.program_id(1)
        i, j, flags = it[step], jt[step], fl[step]

        @pl.when((flags & 1) != 0)
        def _():
            m_sc[...] = jnp.full_like(m_sc, NEG)
            l_sc[...] = jnp.zeros_like(l_sc)
            acc_sc[...] = jnp.zeros_like(acc_sc)

        def update(units):
            wk = tk if units is None else (units + 1) * (tk // nunit)
            s = _dot(q_ref[...], k_ref[:wk, :], NT)
            if units is not None:
                s = jnp.where(_causal_mask(i, j, tq, tk, wk), s, NEG)
            m_prev = m_sc[...]
            m_new = jnp.maximum(m_prev, jnp.max(s, axis=-1, keepdims=True))
            alpha = jnp.exp(m_prev - m_new)
            p = jnp.exp(s - m_new)
            l_sc[...] = alpha * l_sc[...] + jnp.sum(p, axis=-1, keepdims=True)
            acc_sc[...] = alpha * acc_sc[...] + _dot(p.astype(BF16), v_ref[:wk, :], NN)
            m_sc[...] = m_new

        _by_crossing(flags, nunit, update)

        @pl.when((flags & 2) != 0)
        def _():
            o_ref[...] = acc_sc[...] / l_sc[...]
            lse_ref[...] = m_sc[...] + jnp.log(l_sc[...])

    qs, kv = _causal_specs(tq, tk)
    return _pcall(
        body, name="mla_attn_fwd",
        grid_spec=pltpu.PrefetchScalarGridSpec(
            num_scalar_prefetch=3, grid=(nh, tables[0].shape[0]),
            in_specs=[qs(dq), kv(dq), kv(dv)], out_specs=(qs(dv), qs(1)),
            scratch_shapes=[pltpu.VMEM((tq, 1), F32), pltpu.VMEM((tq, 1), F32), pltpu.VMEM((tq, dv), F32)]),
        out_shape=(_sds((nh, t, dv), F32), _sds((nh, t, 1), F32)),
        compiler_params=_cparams("parallel", "arbitrary"))(*tables, q, k, v)


def _mla_bwd(q, k, k_t, v, do, lse_row, dl_row, tq, tk):
    nh, t, dq = q.shape
    dv = v.shape[2]
    tq, tk = min(tq, t), min(tk, t)
    nq = t // tq
    tables, nunit = _causal_steps(nq, t // tk, tq, tk, False)

    def body(it, jt, fl, q_ref, k_ref, kt_ref, v_ref, do_ref, lse_ref, dl_ref, dk_ref, dv_ref, dq_ref, dk_sc, dv_sc):
        step = pl.program_id(1)
        i, j, flags = it[step], jt[step], fl[step]

        def update(units):
            off = 0 if units is None else units * (tq // nunit)
            qq = q_ref[off:, :]
            st = _dot(k_ref[...], qq, NT)
            if units is not None:
                key = j * tk + lax.broadcasted_iota(jnp.int32, (tk, tq - off), 0)
                qry = i * tq + off + lax.broadcasted_iota(jnp.int32, (tk, tq - off), 1)
                st = jnp.where(key <= qry, st, NEG)
            pt = jnp.exp(st - lse_ref[:, off:])
            dob = do_ref[off:, :].astype(BF16)
            dpt = _dot(v_ref[...], dob, NT)
            dst = pt * (dpt - dl_ref[:, off:])
            dsb = dst.astype(BF16)
            dv_part = _dot(pt.astype(BF16), dob, NN)
            dk_part = _dot(dsb, qq, NN)
            dq_part = _dot(kt_ref[...], dsb, NN)

            @pl.when((flags & 1) != 0)
            def _():
                dv_sc[...] = dv_part
                dk_sc[...] = dk_part

            @pl.when((flags & 1) == 0)
            def _():
                dv_sc[...] += dv_part
                dk_sc[...] += dk_part

            if off == 0:
                @pl.when(j == 0)
                def _():
                    dq_ref[i] = dq_part

                @pl.when(j != 0)
                def _():
                    dq_ref[i] += dq_part
            else:
                dq_ref[i, :, off:] += dq_part

        _by_crossing(flags, nunit, update)

        @pl.when((flags & 2) != 0)
        def _():
            dk_ref[...] = dk_sc[...]
            dv_ref[...] = dv_sc[...]

    qs, kv = _causal_specs(tq, tk)
    rowv = pl.BlockSpec((None, 1, tq), lambda h, s, it, jt, fl: (h, 0, it[s]))
    ktv = pl.BlockSpec((None, dq, tk), lambda h, s, it, jt, fl: (h, 0, jt[s]))
    whole = pl.BlockSpec((None, nq, dq, tq), lambda h, s, it, jt, fl: (h, 0, 0, 0))
    return _pcall(
        body, name="mla_attn_bwd",
        grid_spec=pltpu.PrefetchScalarGridSpec(
            num_scalar_prefetch=3, grid=(nh, tables[0].shape[0]),
            in_specs=[qs(dq), kv(dq), ktv, kv(dv), qs(dv), rowv, rowv], out_specs=(kv(dq), kv(dv), whole),
            scratch_shapes=[pltpu.VMEM((tk, dq), F32), pltpu.VMEM((tk, dv), F32)]),
        out_shape=(_sds((nh, t, dq), F32), _sds((nh, t, dv), F32), _sds((nh, nq, dq, tq), F32)),
        compiler_params=_cparams("parallel", "arbitrary"))(*tables, q, k, k_t, v, do, lse_row, dl_row)


def _ffn_down_loss(act, wd, x, target, name, tm):
    nc, t, fs = act.shape
    d = x.shape[1]
    tm = min(tm, t)
    nt = t // tm

    def body(*refs):
        a_refs, w_refs = refs[:nc], refs[nc:2 * nc]
        x_ref, t_ref, dy_ref, loss_ref, acc = refs[2 * nc:]
        i = pl.program_id(0)
        tot = _dot(a_refs[0][...], w_refs[0][...], NN)
        for c in range(1, nc):
            tot = tot + _dot(a_refs[c][...], w_refs[c][...], NN)
        err = x_ref[...] + tot * FFN_RESID - t_ref[...]
        dy_ref[...] = err * (1.0 / d)

        @pl.when(i == 0)
        def _():
            acc[...] = jnp.zeros_like(acc)

        acc[...] += jnp.sum(err * err, axis=0, keepdims=True)

        @pl.when(i == nt - 1)
        def _():
            loss_ref[0, 0] = jnp.sum(acc[...]) * (0.5 / d)

    row = pl.BlockSpec((tm, d), lambda i: (i, 0))
    a_specs = [pl.BlockSpec((None, tm, fs), lambda i, c=c: (c, i, 0)) for c in range(nc)]
    w_specs = [pl.BlockSpec((None, fs, d), lambda i, c=c: (c, 0, 0)) for c in range(nc)]
    return _pcall(
        body, name=name, grid=(nt,), in_specs=a_specs + w_specs + [row, row],
        out_specs=(row, pl.BlockSpec(memory_space=pltpu.SMEM)),
        out_shape=(_sds((t, d), F32), _sds((1, 1), F32)),
        scratch_shapes=[pltpu.VMEM((1, d), F32)],
        compiler_params=_cparams("arbitrary"))(*[act] * nc, *[wd] * nc, x, target)


def _adamw(w, g, m, v, name):
    r, c = w.shape
    tr = r // 2 if r % 16 == 0 else r

    def body(w_ref, g_ref, m_ref, v_ref, d_ref, nm_ref, nv_ref):
        gg = g_ref[...]
        nm = ADAM_B1 * m_ref[...] + (1.0 - ADAM_B1) * gg
        nv = ADAM_B2 * v_ref[...] + (1.0 - ADAM_B2) * (gg * gg)
        m_hat = nm / (1.0 - ADAM_B1 ** ADAM_STEP)
        v_hat = nv / (1.0 - ADAM_B2 ** ADAM_STEP)
        d_ref[...] = -ADAM_LR * (m_hat / (jnp.sqrt(v_hat) + ADAM_EPS) + ADAM_WD * w_ref[...])
        nm_ref[...] = nm
        nv_ref[...] = nv

    spec = pl.BlockSpec((tr, c), lambda i: (i, 0))
    sd = _sds((r, c), F32)
    return _pcall(body, name=name, grid=(r // tr,), in_specs=[spec] * 4, out_specs=(spec,) * 3,
                  out_shape=(sd, sd, sd), compiler_params=_cparams("parallel"))(w, g, m, v)


MESH_ID = pl.DeviceIdType.MESH
HBM_SPEC = pl.BlockSpec(memory_space=pltpu.HBM)


def _place():
    return lax.axis_index("x"), lax.axis_index("y"), lax.axis_index("c")


def _other_chips(x, y):
    return [(1 - x, y), (x, 1 - y), (1 - x, 1 - y)]


def _remote(src, dst, send_sems, recv_sems, k, to):
    return pltpu.make_async_remote_copy(src_ref=src, dst_ref=dst, send_sem=send_sems.at[k], recv_sem=recv_sems.at[k],
                                        device_id=to, device_id_type=MESH_ID)


def _halves(arrays):
    for a in arrays:
        assert a.shape[-2] % 32 == 0
    return [a.shape[-2] // 2 for a in arrays]


def _gather_start(srcs, outs, halves, send_sems, recv_sems):
    x, y, c = _place()
    for a, half in enumerate(halves):
        rows = pl.ds(c * half, half)
        for k, (cx, cy) in enumerate(_other_chips(x, y)):
            _remote(srcs[a].at[rows, :], outs[a].at[2 * x + y, rows, :], send_sems, recv_sems, 3 * a + k,
                    (cx, cy, c)).start()


def _gather_wait(outs, halves, send_sems, recv_sems):
    x, y, c = _place()
    for a, half in enumerate(halves):
        for k, (cx, cy) in enumerate(_other_chips(x, y)):
            got = outs[a].at[2 * cx + cy, pl.ds(c * half, half), :]
            _remote(got, got, send_sems, recv_sems, 3 * a + k, (x, y, c)).wait()


def _forward_cores(partly):
    n = len(partly)
    halves = _halves(partly)

    def body(*refs):
        srcs, outs, send_sems, recv_sems = refs[:n], refs[n:2 * n], refs[2 * n], refs[2 * n + 1]
        x, y, c = _place()
        for a, half in enumerate(halves):
            for k, (cx, cy) in enumerate(_other_chips(x, y)):
                rows = pl.ds(c * half, half)
                _remote(srcs[a].at[2 * cx + cy, rows, :], outs[a].at[2 * cx + cy, rows, :], send_sems, recv_sems,
                        3 * a + k, (x, y, 1 - c)).start()
        for a, half in enumerate(halves):
            for k, (cx, cy) in enumerate(_other_chips(x, y)):
                mine = outs[a].at[2 * cx + cy, pl.ds(c * half, half), :]
                theirs = outs[a].at[2 * cx + cy, pl.ds((1 - c) * half, half), :]
                _remote(mine, theirs, send_sems, recv_sems, 3 * a + k, (x, y, c)).wait()

    return _pcall(
        body, name="forward_cores", in_specs=[HBM_SPEC] * n, out_specs=tuple([HBM_SPEC] * n),
        out_shape=tuple(_sds(p.shape, p.dtype) for p in partly), input_output_aliases={a: a for a in range(n)},
        scratch_shapes=[pltpu.SemaphoreType.DMA((3 * n,)), pltpu.SemaphoreType.DMA((3 * n,))],
    )(*partly)


def _gather_weights(blocks):
    n = len(blocks)
    halves = _halves(blocks)

    def body(*refs):
        srcs, outs, send_sems, recv_sems = refs[:n], refs[n:2 * n], refs[2 * n], refs[2 * n + 1]
        x, y, c = _place()
        me = 2 * x + y
        sibling = (x, y, 1 - c)
        chips = _other_chips(x, y)

        def part(a, chip, core):
            return outs[a].at[chip, pl.ds(core * halves[a], halves[a]), :]

        for a in range(n):
            mine = srcs[a].at[pl.ds(c * halves[a], halves[a]), :]
            for k, (cx, cy) in enumerate(chips):
                _remote(mine, part(a, me, c), send_sems, recv_sems, 6 * a + k, (cx, cy, c)).start()
        for k, (cx, cy) in enumerate(chips):
            for a in range(n):
                got = part(a, 2 * cx + cy, c)
                _remote(got, got, send_sems, recv_sems, 6 * a + k, (x, y, c)).wait_recv()
                _remote(got, got, send_sems, recv_sems, 6 * a + 3 + k, sibling).start()
        for k, (cx, cy) in enumerate(chips):
            for a in range(n):
                got = part(a, 2 * cx + cy, 1 - c)
                _remote(got, got, send_sems, recv_sems, 6 * a + 3 + k, (x, y, c)).wait_recv()
        for a in range(n):
            sent = part(a, me, c)
            for k in range(6):
                _remote(sent, sent, send_sems, recv_sems, 6 * a + k, (x, y, c)).wait_send()

    return _pcall(
        body, name="gather_weights", in_specs=[HBM_SPEC] * n, out_specs=tuple([HBM_SPEC] * n),
        out_shape=tuple(_sds((N_CHIPS,) + b.shape, b.dtype) for b in blocks),
        scratch_shapes=[pltpu.SemaphoreType.DMA((6 * n,)), pltpu.SemaphoreType.DMA((6 * n,))],
    )(*blocks)


def _reduce_cores(grads, tag):
    n = len(grads)

    def body(*refs):
        gs, outs, send_sems, recv_sems = refs[:n], refs[n:2 * n], refs[2 * n], refs[2 * n + 1]
        _cores_start(gs, outs, send_sems, recv_sems)
        _cores_wait(gs, outs, send_sems, recv_sems)

    return _pcall(
        body, name=f"reduce_cores_{tag}", in_specs=[HBM_SPEC] * n, out_specs=tuple([HBM_SPEC] * n),
        out_shape=_cores_shapes(grads), scratch_shapes=_cores_sems(n),
    )(*grads)


def _cores_shapes(grads):
    return tuple(_sds((N_CHIPS, h, g.shape[2]), g.dtype) for g, h in zip(grads, _halves(grads)))


def _cores_sems(n):
    return [pltpu.SemaphoreType.DMA((n,)), pltpu.SemaphoreType.DMA((n,))]


def _cores_start(gs, outs, send_sems, recv_sems):
    x, y, c = _place()
    for a, g in enumerate(gs):
        half = g.shape[1] // 2
        for j in range(N_CHIPS):
            _remote(g.at[j, pl.ds((1 - c) * half, half), :], outs[a].at[j], send_sems, recv_sems, a,
                    (x, y, 1 - c)).start()


def _cores_wait(gs, outs, send_sems, recv_sems):
    x, y, c = _place()
    for a, g in enumerate(gs):
        half = g.shape[1] // 2
        _remote(g.at[:, pl.ds((1 - c) * half, half), :], outs[a], send_sems, recv_sems, a, (x, y, c)).wait()


def _scatter_shapes(parts):
    return tuple(_sds((3,) + p.shape[1:], p.dtype) for p in parts)


def _scatter_sems(n):
    return [pltpu.SemaphoreType.DMA((3 * n,)), pltpu.SemaphoreType.DMA((3 * n,))]


def _scatter_start(ps, outs, send_sems, recv_sems):
    x, y, c = _place()
    for a in range(len(ps)):
        for k, (cx, cy) in enumerate(_other_chips(x, y)):
            _remote(ps[a].at[2 * cx + cy], outs[a].at[k], send_sems, recv_sems, 3 * a + k, (cx, cy, c)).start()


def _scatter_wait(ps, outs, send_sems, recv_sems):
    x, y, c = _place()
    for a in range(len(ps)):
        for k in range(3):
            _remote(ps[a].at[k], outs[a].at[k], send_sems, recv_sems, 3 * a + k, (x, y, c)).wait()


def _gather_shapes(blocks):
    return tuple(_sds((N_CHIPS,) + b.shape, b.dtype) for b in blocks)


def _gather_sems(n):
    return [pltpu.SemaphoreType.DMA((3 * n,)), pltpu.SemaphoreType.DMA((3 * n,))]


EXCHANGES = {"chips": (_scatter_start, _scatter_wait, _scatter_shapes, _scatter_sems),
             "cores": (_cores_start, _cores_wait, _cores_shapes, _cores_sems),
             "gather": (lambda srcs, outs, s, r: _gather_start(srcs, outs, _halves(srcs), s, r),
                        lambda srcs, outs, s, r: _gather_wait(outs, _halves(srcs), s, r),
                        _gather_shapes, _gather_sems)}


def _sum_partials(received, parts, place):
    n = len(parts)
    steps = 2
    tiles = [p.shape[1] // steps for p in parts]

    def body(place_ref, *refs):
        rs, ps, outs = refs[:n], refs[n:2 * n], refs[2 * n:]
        for a in range(n):
            tot = ps[a][...].astype(F32)
            for k in range(3):
                tot = tot + rs[a][k].astype(F32)
            outs[a][...] = tot

    cols = [p.shape[2] for p in parts]
    return _pcall(
        body, name="sum_chip_partials",
        grid_spec=pltpu.PrefetchScalarGridSpec(
            num_scalar_prefetch=1, grid=(steps,),
            in_specs=[pl.BlockSpec((3, tm, w), lambda i, pc: (0, i, 0)) for tm, w in zip(tiles, cols)]
            + [pl.BlockSpec((None, tm, w), lambda i, pc: (pc[0], i, 0)) for tm, w in zip(tiles, cols)],
            out_specs=tuple(pl.BlockSpec((tm, w), lambda i, pc: (pc[1] * steps + i, 0)) for tm, w in zip(tiles, cols))),
        out_shape=tuple(_sds((2 * p.shape[1], p.shape[2]), F32) for p in parts),
        compiler_params=_cparams("parallel"))(place, *received, *parts)


def _share_cores(blocks):
    n = len(blocks)
    halves = _halves(blocks)

    def body(*refs):
        srcs, outs, send_sems, recv_sems = refs[:n], refs[n:2 * n], refs[2 * n], refs[2 * n + 1]
        x, y, c = _place()
        for a in range(n):
            piece = pl.ds(c * halves[a], halves[a])
            _remote(srcs[a].at[piece, :], outs[a].at[piece, :], send_sems, recv_sems, a, (x, y, 1 - c)).start()
        for a in range(n):
            mine = outs[a].at[pl.ds(c * halves[a], halves[a]), :]
            theirs = outs[a].at[pl.ds((1 - c) * halves[a], halves[a]), :]
            _remote(mine, theirs, send_sems, recv_sems, a, (x, y, c)).wait()

    return _pcall(
        body, name="share_cores", in_specs=[HBM_SPEC] * n, out_specs=tuple([HBM_SPEC] * n),
        out_shape=tuple(_sds(b.shape, b.dtype) for b in blocks), input_output_aliases={a: a for a in range(n)},
        scratch_shapes=[pltpu.SemaphoreType.DMA((n,)), pltpu.SemaphoreType.DMA((n,))],
    )(*blocks)


def _sum_blocks(stacked, name, tm):
    n, rows, lanes = stacked.shape
    tm = min(tm, rows)

    def body(s_ref, o_ref):
        tot = s_ref[n - 1].astype(F32)
        for k in range(n - 1):
            tot = tot + s_ref[k].astype(F32)
        o_ref[...] = tot

    return _pcall(body, name=name, grid=(rows // tm,),
                  in_specs=[pl.BlockSpec((n, tm, lanes), lambda i: (0, i, 0))],
                  out_specs=pl.BlockSpec((tm, lanes), lambda i: (i, 0)), out_shape=_sds((rows, lanes), F32),
                  compiler_params=_cparams("parallel"))(stacked)


def _add_halves(grads, theirs, core, tag):
    n = len(grads)
    steps = 2
    tiles = [t.shape[1] // steps for t in theirs]
    cols = [t.shape[2] for t in theirs]

    def body(c_ref, *refs):
        gs, ts, outs = refs[:n], refs[n:2 * n], refs[2 * n:]
        for a in range(n):
            outs[a][...] = (gs[a][...] + ts[a][...]).astype(BF16)

    own = [pl.BlockSpec((None, tm, w), lambda k, i, c: (k, c[0] * steps + i, 0)) for tm, w in zip(tiles, cols)]
    same = [pl.BlockSpec((None, tm, w), lambda k, i, c: (k, i, 0)) for tm, w in zip(tiles, cols)]
    return _pcall(
        body, name=f"add_core_halves_{tag}",
        grid_spec=pltpu.PrefetchScalarGridSpec(
            num_scalar_prefetch=1, grid=(N_CHIPS, steps), in_specs=own + same, out_specs=tuple(same)),
        out_shape=tuple(_sds(t.shape, BF16) for t in theirs),
        compiler_params=_cparams("parallel", "parallel"))(core, *grads, *theirs)


def _allreduce_small(part):
    rows, lanes = part.shape
    ndev = 8

    def body(src, tot, buf, send_sems, recv_sems):
        x, y, c = _place()
        me = 4 * x + 2 * y + c
        buf[me] = src[...]
        sends = []
        for k in range(1, ndev):
            peer = (x ^ (k >> 2), y ^ ((k >> 1) & 1), c ^ (k & 1))
            cp = _remote(src, buf.at[me], send_sems, recv_sems, k - 1, peer)
            cp.start()
            sends.append(cp)
        for k in range(1, ndev):
            theirs = buf.at[me ^ k]
            _remote(theirs, theirs, send_sems, recv_sems, k - 1, (x, y, c)).wait_recv()
        for cp in sends:
            cp.wait_send()
        acc = buf[0]
        for d in range(1, ndev):
            acc = acc + buf[d]
        tot[...] = acc

    vm = pl.BlockSpec(memory_space=pltpu.VMEM)
    return _pcall(
        body, name="allreduce_small", in_specs=[vm], out_specs=vm, out_shape=_sds((rows, lanes), F32),
        scratch_shapes=[pltpu.VMEM((ndev, rows, lanes), F32), pltpu.SemaphoreType.DMA((ndev - 1,)),
                        pltpu.SemaphoreType.DMA((ndev - 1,))],
    )(part)


SMALL_USED = sum(r for _, _, r in SMALL)


def _pack_small(vals, loss):
    parts = []
    for name, shape, r in SMALL:
        flat = vals[name].reshape(-1).astype(F32)
        parts.append(jnp.pad(flat, (0, r * LANES - flat.shape[0])).reshape(r, LANES))
    parts.append(jnp.pad(loss.astype(F32), ((0, SMALL_ROWS - SMALL_USED - 1), (0, LANES - 1))))
    return jnp.concatenate(parts, axis=0)


def _unpack_small(packed):
    out, off = {}, 0
    for name, shape, r in SMALL:
        n = int(np.prod(shape))
        out[name] = packed[off:off + r].reshape(-1)[:n].reshape(shape)
        off += r
    return out


def _heads_major(a, nh):
    t = a.shape[0]
    return a.reshape(t, nh, a.shape[1] // nh).transpose(1, 0, 2)


def _tokens_major(a):
    nh, t, w = a.shape
    return a.transpose(1, 0, 2).reshape(t, nh * w)


LATE = ("ffn1_w_gate", "ffn1_w_up", "ffn1_w_down")
EARLY = tuple(name for name, _ in BIG if name not in LATE)


def _local_step(x, target, small, wfull, exchanges=None, later_weights=None):
    t = x.shape[0]
    nh, hd = DIL_HEADS, DIL_HD
    grads_s, grads_b = {}, {}

    x1, ffn1_saved, partly = _ffn_fwd(x, small["ffn1_norm"], wfull["ffn1_w_gate"], wfull["ffn1_w_up"],
                                      wfull["ffn1_w_down"], "ffn1", later_weights[0] if later_weights else ())
    if later_weights:
        wfull = {**wfull, **later_weights[1](partly)}
    w_in = wfull["w_in"].transpose(1, 0, 2).reshape(D_MODEL, -1)
    w_out = wfull["w_out"].reshape(D_MODEL, D_MODEL)
    w_qb, w_kvb = wfull["mla_w_q_b"], wfull["mla_w_kv_b"]
    hm = _rms_fwd(x1, small["mix_norm"], BF16, "mix_norm", 1024)
    proj = _mm_simple("in_proj", hm, w_in, NN, F32, tm=1024)
    cq, ckv, k_pe = proj[:, 1536:1792], proj[:, 1792:1920], proj[:, 1920:1984]

    gq, gk = jnp.tile(small["dil_q_norm"], (1, nh)), jnp.tile(small["dil_k_norm"], (1, nh))
    qn = _head_norm_fwd(proj, 0, gq, "dil_q_norm", 512)
    kn = _head_norm_fwd(proj, 1, gk, "dil_k_norm", 512)
    v_d = _head_norm_fwd(proj, 2, None, "dil_v_views", 512)
    bias = _bias_tiles(small["rel_bias"]).reshape(3, nh // 2, 2 * QB, QB + DIL_W)
    outs, lses = [], []
    for b, dil in enumerate(DIL_DILATIONS):
        o_b, lse_b = _dil_fwd(qn[b], kn[b], v_d[b], bias[b], dil, f"dil_fwd_{dil}")
        outs.append(o_b)
        lses.append(lse_b)
    o_dil, lse_tot, od = _dil_merge(outs, lses, small["out_norm_dil"], 512)

    mh = MLA_HEADS
    cos_t, sin_t = _rope_tables(t)
    cqn = _rms_fwd(cq, small["mla_q_a_norm"], BF16, "mla_q_a_norm", 2048)
    ckvn = _rms_fwd(ckv, small["mla_kv_a_norm"], BF16, "mla_kv_a_norm", 2048)
    tm = min(512, t)

    th = min(2048, t)

    def head_proj(name, a, w, width):
        k = a.shape[1]
        return _mm(name, (mh, t // th, 1),
                   [(a, pl.BlockSpec((th, k), lambda h, i, r: (i, 0)), w, pl.BlockSpec((None, k, width), lambda h, i, r: (h, 0, 0)))],
                   NN, _sds((mh, t, width), F32), pl.BlockSpec((None, th, width), lambda h, i, r: (h, i, 0)), (th, width))

    q_raw = head_proj("mla_q_proj", cqn, w_qb, MLA_QK)
    kv_raw = head_proj("mla_kv_proj", ckvn, w_kvb, MLA_NOPE + MLA_V)
    q_raw2, kv_raw2 = q_raw.reshape(mh * t, MLA_QK), kv_raw.reshape(mh * t, MLA_NOPE + MLA_V)
    q_scale = MLA_QK ** -0.5
    q_m = _mla_qk_fwd(q_raw2, small["mla_q_norm"], cos_t, sin_t, q_scale, "mla_q_rope", 2048).reshape(mh, t, MLA_QK)
    k_m, k_t, v_m = _mla_qk_fwd(kv_raw2, small["mla_k_norm"], cos_t, sin_t, 1.0, "mla_k_rope", 2048, pe=k_pe)
    k_m, v_m = k_m.reshape(mh, t, MLA_QK), v_m.reshape(mh, t, MLA_V)
    o_mla_h, lse_m = _mla_fwd(q_m, k_m, v_m, 512, 4096)
    o_mla = _tokens_major(o_mla_h)

    om = _rms_fwd(o_mla, small["out_norm_mla"], BF16, "out_norm_mla", 2048)
    half_w = DIL_WIDTH
    row = pl.BlockSpec((tm, D_MODEL), lambda i, j, r: (i, 0))
    act_spec = pl.BlockSpec((tm, half_w), lambda i, j, r: (i, 0))
    x2 = _mm("out_proj", (t // tm, 1, 1),
             [(od, act_spec, w_out, pl.BlockSpec((half_w, D_MODEL), lambda i, j, r: (0, 0))),
              (om, act_spec, w_out, pl.BlockSpec((half_w, D_MODEL), lambda i, j, r: (1, 0)))],
             NN, _sds((t, D_MODEL), F32), row, (tm, D_MODEL), res=(x1, row))
    (dy, loss), ffn2_saved, _ = _ffn_fwd(x2, small["ffn2_norm"], wfull["ffn2_w_gate"], wfull["ffn2_w_up"],
                                         wfull["ffn2_w_down"], "ffn2", target=target)

    dx2, grads_s["ffn2_norm"], grads_b["ffn2_w_gate"], grads_b["ffn2_w_up"], grads_b["ffn2_w_down"], _, _ = _ffn_bwd(
        dy, x2, small["ffn2_norm"], wfull["ffn2_w_gate"], wfull["ffn2_w_up"], wfull["ffn2_w_down"], ffn2_saved, "ffn2")

    d_ocat = _mm_simple("out_proj_dx", dx2, w_out, NT, F32, tm=1024)
    dw_out_t = _mm_tn_multi("out_proj_dw", dx2, [od, om], 2048)
    grads_b["w_out"] = jnp.concatenate([w.T for w in dw_out_t], axis=0).reshape(N_CHIPS, D_MODEL // N_CHIPS, D_MODEL)
    do_dil, grads_s["out_norm_dil"] = _rms_bwd([d_ocat[:, :half_w]], o_dil, small["out_norm_dil"], None, "out_norm_dil_bwd", 2048)
    do_mla, grads_s["out_norm_mla"] = _rms_bwd([d_ocat[:, half_w:]], o_mla, small["out_norm_mla"], None, "out_norm_mla_bwd", 2048)

    do_m = _heads_major(do_mla, mh)
    dl_m = _rowdot(do_m.reshape(mh * t, MLA_V), o_mla_h.reshape(mh * t, MLA_V), "mla_delta", 2048).reshape(mh, t, 1)
    dk_m, dv_m, dq_t = _mla_bwd(q_m, k_m, k_t, v_m, do_m, lse_m.reshape(mh, 1, t),
                                dl_m.reshape(mh, 1, t), 2048, 512)
    dq_m = dq_t.transpose(0, 1, 3, 2).reshape(mh, t, MLA_QK)
    dq_raw, grads_s["mla_q_norm"] = _mla_qk_bwd(dq_m.reshape(mh * t, MLA_QK), q_raw2, small["mla_q_norm"],
                                                 cos_t, sin_t, q_scale, "mla_q_rope_bwd", 2048)
    dk_nope, dk_pe_h, grads_s["mla_k_norm"] = _mla_qk_bwd(dk_m.reshape(mh * t, MLA_QK), kv_raw2, small["mla_k_norm"],
                                                          cos_t, sin_t, 1.0, "mla_k_rope_bwd", 2048, pe=k_pe)
    dq_raw = dq_raw.reshape(mh, t, MLA_QK)
    dk_nope = dk_nope.reshape(mh, t, MLA_NOPE)

    def head_proj_dx(name, d, w):
        width, k = d.shape[2], w.shape[1]
        pairs = [(d, pl.BlockSpec((None, th, width), lambda i, j, r, h=h: (h, i, 0)),
                  w, pl.BlockSpec((None, k, width), lambda i, j, r, h=h: (h, 0, 0))) for h in range(mh)]
        return _mm(name, (t // th, 1, 1), pairs, NT, _sds((t, k), F32),
                   pl.BlockSpec((th, k), lambda i, j, r: (i, 0)), (th, k))

    def head_proj_dw(name, a, d):
        width, k = d.shape[2], a.shape[1]
        return _mm(name, (mh, 1, t // th),
                   [(a, pl.BlockSpec((th, k), lambda h, j, r: (r, 0)), d, pl.BlockSpec((None, th, width), lambda h, j, r: (h, r, 0)))],
                   TN, _sds((mh, k, width), F32), pl.BlockSpec((None, k, width), lambda h, j, r: (h, 0, 0)), (k, width))

    d_cqn = head_proj_dx("mla_q_proj_dx", dq_raw, w_qb)
    kv_pairs = []
    for h in range(mh):
        for part, d_part in enumerate((dk_nope, dv_m)):
            kv_pairs.append((d_part, pl.BlockSpec((None, th, MLA_NOPE), lambda i, j, r, h=h: (h, i, 0)),
                             w_kvb, pl.BlockSpec((None, MLA_KV_RANK, MLA_NOPE), lambda i, j, r, h=h, part=part: (h, 0, part))))
    d_ckvn = _mm("mla_kv_proj_dx", (t // th, 1, 1), kv_pairs, NT, _sds((t, MLA_KV_RANK), F32),
                 pl.BlockSpec((th, MLA_KV_RANK), lambda i, j, r: (i, 0)), (th, MLA_KV_RANK))
    grads_b["mla_w_q_b"] = head_proj_dw("mla_q_proj_dw", cqn, dq_raw)
    grads_b["mla_w_kv_b"] = jnp.concatenate([head_proj_dw("mla_k_proj_dw", ckvn, dk_nope),
                                             head_proj_dw("mla_v_proj_dw", ckvn, dv_m)], axis=2)
    d_cq, grads_s["mla_q_a_norm"] = _rms_bwd([d_cqn], cq, small["mla_q_a_norm"], None, "mla_q_a_norm_bwd", 2048)
    d_ckv, grads_s["mla_kv_a_norm"] = _rms_bwd([d_ckvn], ckv, small["mla_kv_a_norm"], None, "mla_kv_a_norm_bwd", 2048)
    d_kpe = _sum_blocks(dk_pe_h.reshape(mh, t * MLA_ROPE // LANES, LANES), "mla_kpe_sum", 1024).reshape(t, MLA_ROPE)

    stats, do_db = _dil_stats(do_dil, o_dil, lse_tot, 512)
    dqs, dks, dvs, dtiles = [], [], [], []
    for b, dil in enumerate(DIL_DILATIONS):
        dq_b, dk_b, dv_b, db_b = _dil_bwd(qn[b], kn[b], v_d[b], do_db[b], stats[b], bias[b], dil, f"dil_bwd_{dil}")
        dqs.append(dq_b)
        dks.append(dk_b)
        dvs.append(dv_b)
        dtiles.append(db_b)
    grads_s["rel_bias"] = _bias_grad(jnp.stack(dtiles).reshape(3, nh, QB, QB + DIL_W))
    dq_a, dgq = _head_norm_bwd(dqs, proj, 0, gq, "dil_q_norm_bwd", 512)
    dk_a, dgk = _head_norm_bwd(dks, proj, 1, gk, "dil_k_norm_bwd", 512)
    grads_s["dil_q_norm"], grads_s["dil_k_norm"] = dgq[:, :hd], dgk[:, :hd]
    dv_a = _sum_branches(dvs, "dil_dv_sum", 512)

    dparts = [dq_a, dk_a, dv_a, d_cq, d_ckv, d_kpe]
    t2 = min(512, t)
    pairs, lo = [], 0
    for dpart in dparts:
        width = dpart.shape[1]
        w_part = w_in[:, lo:lo + width]
        pairs.append((dpart, pl.BlockSpec((t2, width), lambda i, j, r: (i, 0)),
                      w_part, pl.BlockSpec((D_MODEL, width), lambda i, j, r: (0, 0))))
        lo += width
    dw_in = jnp.concatenate(_mm_tn_multi("in_proj_dw", hm, dparts, 1024), axis=1)
    grads_b["w_in"] = dw_in.reshape(D_MODEL, N_CHIPS, -1).transpose(1, 0, 2)
    early = tuple(grads_b[n] for n in EARLY) if exchanges else ()
    row2 = pl.BlockSpec((t2, D_MODEL), lambda i, j, r: (i, 0))
    res = _mm("in_proj_dx", (t // t2, 1, 1), pairs, NT, _sds((t, D_MODEL), F32), row2, (t2, D_MODEL),
              res=(dx2, row2), norm=(x1, small["mix_norm"]), outgoing=early, exchange="cores")
    dx1, grads_s["mix_norm"] = res[0], res[1]
    outgoing = exchanges[0](early, res[2]) if exchanges else ()
    dx, grads_s["ffn1_norm"], grads_b["ffn1_w_gate"], grads_b["ffn1_w_up"], grads_b["ffn1_w_down"], arrived, late = _ffn_bwd(
        dx1, x, small["ffn1_norm"], wfull["ffn1_w_gate"], wfull["ffn1_w_up"], wfull["ffn1_w_down"], ffn1_saved, "ffn1",
        outgoing, exchanges[1] if exchanges else None)
    return loss, dx, grads_s, grads_b, (tuple(outgoing), arrived), late


def kernel(x, ffn1_norm, ffn1_w_gate, ffn1_w_up, ffn1_w_down, mix_norm, w_in, dil_q_norm, dil_k_norm, rel_bias, mla_q_a_norm, mla_w_q_b, mla_kv_a_norm, mla_w_kv_b, mla_q_norm, mla_k_norm, out_norm_dil, out_norm_mla, w_out, ffn2_norm, ffn2_w_gate, ffn2_w_up, ffn2_w_down, loss_target, m_ffn1_norm, m_ffn1_w_gate, m_ffn1_w_up, m_ffn1_w_down, m_mix_norm, m_w_in, m_dil_q_norm, m_dil_k_norm, m_rel_bias, m_mla_q_a_norm, m_mla_w_q_b, m_mla_kv_a_norm, m_mla_w_kv_b, m_mla_q_norm, m_mla_k_norm, m_out_norm_dil, m_out_norm_mla, m_w_out, m_ffn2_norm, m_ffn2_w_gate, m_ffn2_w_up, m_ffn2_w_down, v_ffn1_norm, v_ffn1_w_gate, v_ffn1_w_up, v_ffn1_w_down, v_mix_norm, v_w_in, v_dil_q_norm, v_dil_k_norm, v_rel_bias, v_mla_q_a_norm, v_mla_w_q_b, v_mla_kv_a_norm, v_mla_w_kv_b, v_mla_q_norm, v_mla_k_norm, v_out_norm_dil, v_out_norm_mla, v_w_out, v_ffn2_norm, v_ffn2_w_gate, v_ffn2_w_up, v_ffn2_w_down):
    given = dict(locals())
    big_names = [name for name, _ in BIG]
    small_names = [name for name, _, _ in SMALL]

    chip = (2 * lax.axis_index("x") + lax.axis_index("y")).astype(jnp.int32)
    core = lax.axis_index("c").astype(jnp.int32)
    mine = {n: given[n].astype(BF16) for n in big_names}

    def with_own(names, arrays):
        return {n: lax.dynamic_update_slice(a, mine[n], (chip, 0, 0)) for n, a in zip(names, arrays)}

    wfirst = with_own(LATE, _gather_weights([mine[n][0] for n in LATE]))
    later_weights = ([mine[n][0] for n in EARLY], lambda partly: with_own(EARLY, _forward_cores(partly)))
    small = {n: given[n] for n in small_names}

    def early_partials(partial, theirs):
        return _add_halves(partial, theirs, core.reshape(1), "early")

    def late_partials(partial):
        return _add_halves(partial, _reduce_cores(partial, "late"), core.reshape(1), "late")

    exchanges = (early_partials, late_partials)
    loss, dx, grads_s, grads_b, (early_part, early_got), (late_part, late_got) = _local_step(
        x[0], loss_target[0], small, wfirst, exchanges, later_weights)
    reduced = _sum_partials(tuple(late_got) + tuple(early_got), tuple(late_part) + tuple(early_part),
                            jnp.stack([chip, core]))
    g_big = dict(zip(LATE + EARLY, _share_cores(reduced)))
    summed = _allreduce_small(_pack_small(grads_s, loss))
    g_small = _unpack_small(summed)
    loss = summed[SMALL_USED, 0]

    grad, delta, new_m, new_v = {}, {}, {}, {}
    for name, shape in BIG:
        g2 = g_big[name]
        d_, m_, v_ = _adamw(given[name].reshape(shape), g2, given["m_" + name].reshape(shape),
                            given["v_" + name].reshape(shape), f"adamw_{name}")
        full = given[name].shape
        grad[name], delta[name], new_m[name], new_v[name] = (a.reshape(full) for a in (g2, d_, m_, v_))
    for name in small_names:
        grad[name] = g_small[name]
        delta[name], new_m[name], new_v[name] = _adamw(given[name], g_small[name], given["m_" + name],
                                                       given["v_" + name], f"adamw_{name}")

    return (loss, dx[None], *[grad[n] for n in WEIGHTS], *[delta[n] for n in WEIGHTS],
            *[new_m[n] for n in WEIGHTS], *[new_v[n] for n in WEIGHTS])
```
